```python
import jax, jax.numpy as jnp
from jax import lax
import numpy as np

D_MODEL = 2048
BATCH = 8
SEQ = 2048
DEPTH = 1

N_HEADS = 16
QK_NOPE_DIM = 128
QK_ROPE_DIM = 64
V_HEAD_DIM = 128
Q_LORA_RANK = 512
KV_LORA_RANK = 512
ROPE_THETA = 10000.0
Q_BLOCK = 128

SGU_GROUPS = 8
SGU_GROUP_DIM = 128
SGU_WIDTH = SGU_GROUPS * SGU_GROUP_DIM
CHUNK = 128

D_FF = -(-8 * D_MODEL // (3 * 256)) * 256

N_BRANCH = 2
RMS_EPS = 1e-6

D_IN = Q_LORA_RANK + KV_LORA_RANK + QK_ROPE_DIM + 2 * SGU_WIDTH + N_BRANCH * D_MODEL

kernel_name = "hybrid_mla_sgu_gated_block"


def rms_norm(x, g):
    xf = x.astype(jnp.float32)
    y = xf * lax.rsqrt(jnp.mean(xf * xf, axis=-1, keepdims=True) + RMS_EPS)
    return (y * g.astype(jnp.float32)).astype(x.dtype)


def rope_tables(positions):
    inv_freq = ROPE_THETA ** (-jnp.arange(0, QK_ROPE_DIM, 2, dtype=jnp.float32) / QK_ROPE_DIM)
    ang = positions.astype(jnp.float32)[..., None] * inv_freq
    return jnp.cos(ang), jnp.sin(ang)


def apply_rope(x, cos, sin):
    xf = x.astype(jnp.float32)
    x1, x2 = jnp.split(xf, 2, axis=-1)
    return jnp.concatenate([x1 * cos - x2 * sin, x2 * cos + x1 * sin], axis=-1).astype(x.dtype)


def mla(q_lat, kv_lat, k_pe, cos, sin, q_norm_g, w_uq, kv_norm_g, w_ukv):
    B, S, _ = q_lat.shape
    q = (rms_norm(q_lat, q_norm_g) @ w_uq).reshape(B, S, N_HEADS, QK_NOPE_DIM + QK_ROPE_DIM)
    q_nope = q[..., :QK_NOPE_DIM]
    q_pe = apply_rope(q[..., QK_NOPE_DIM:], cos[:, :, None, :], sin[:, :, None, :])
    kv = (rms_norm(kv_lat, kv_norm_g) @ w_ukv).reshape(B, S, N_HEADS, QK_NOPE_DIM + V_HEAD_DIM)
    k_nope = kv[..., :QK_NOPE_DIM]
    v = kv[..., QK_NOPE_DIM:]
    k_pe = apply_rope(k_pe, cos, sin)
    scale = (QK_NOPE_DIM + QK_ROPE_DIM) ** -0.5
    outs = []
    for i in range(S // Q_BLOCK):
        q0 = i * Q_BLOCK
        k_end = q0 + Q_BLOCK
        s = (jnp.einsum('bqhd,bkhd->bhqk', q_nope[:, q0:k_end], k_nope[:, :k_end])
             + jnp.einsum('bqhd,bkd->bhqk', q_pe[:, q0:k_end], k_pe[:, :k_end]))
        s = s.astype(jnp.float32) * scale
        causal = (q0 + jnp.arange(Q_BLOCK))[:, None] >= jnp.arange(k_end)[None, :]
        s = jnp.where(causal, s, jnp.finfo(jnp.float32).min)
        p = jax.nn.softmax(s, axis=-1).astype(v.dtype)
        outs.append(jnp.einsum('bhqk,bkhd->bqhd', p, v[:, :k_end]))
    return jnp.concatenate(outs, axis=1).reshape(B, S, N_HEADS * V_HEAD_DIM)


def sgu(uv, norm_g, w_s, b_s):
    B, S, _ = uv.shape
    uv = jax.nn.gelu(uv)
    u, v = uv[..., :SGU_WIDTH], uv[..., SGU_WIDTH:]
    v = rms_norm(v, norm_g).reshape(B, S // CHUNK, CHUNK, SGU_GROUPS, SGU_GROUP_DIM)
    tril = jnp.tril(jnp.ones((CHUNK, CHUNK), dtype=bool))
    ws = jnp.where(tril[None], w_s, jnp.zeros_like(w_s))
    mixed = jnp.einsum('gts,bnsgd->bntgd', ws, v) + b_s.T[None, None, :, :, None]
    return u * mixed.reshape(B, S, SGU_WIDTH)


def _fwd_setup_inputs(seed: int = 0) -> dict:
    key = jax.random.key(seed)
    ks = jax.random.split(key, 24)
    f32 = jnp.float32

    def w(k, shape, fan_in):
        return jax.random.normal(k, shape, f32) * fan_in ** -0.5

    def gain(k, shape):
        return 1.0 + 0.01 * jax.random.normal(k, shape, f32)

    L = DEPTH
    x = jax.random.normal(ks[0], (BATCH, SEQ, D_MODEL), f32)
    offsets = jax.random.randint(ks[1], (BATCH, 1), 0, 1024, dtype=jnp.int32)
    positions = (jnp.arange(SEQ, dtype=jnp.int32)[None, :] + offsets).astype(jnp.int32)
    return {
        "x": x,
        "positions": positions,
        "norm_mix_g": gain(ks[2], (L, D_MODEL)),
        "w_in": w(ks[3], (L, D_MODEL, D_IN), D_MODEL),
        "b_gate": 0.01 * jax.random.normal(ks[4], (L, N_BRANCH * D_MODEL), f32),
        "q_norm_g": gain(ks[5], (L, Q_LORA_RANK)),
        "w_uq": w(ks[6], (L, Q_LORA_RANK, N_HEADS * (QK_NOPE_DIM + QK_ROPE_DIM)), Q_LORA_RANK),
        "kv_norm_g": gain(ks[7], (L, KV_LORA_RANK)),
        "w_ukv": w(ks[8], (L, KV_LORA_RANK, N_HEADS * (QK_NOPE_DIM + V_HEAD_DIM)), KV_LORA_RANK),
        "w_o_attn": w(ks[9], (L, N_HEADS * V_HEAD_DIM, D_MODEL), N_HEADS * V_HEAD_DIM),
        "sgu_norm_g": gain(ks[10], (L, SGU_WIDTH)),
        "w_sgu": w(ks[11], (L, SGU_GROUPS, CHUNK, CHUNK), CHUNK),
        "b_sgu": gain(ks[12], (L, SGU_GROUPS, CHUNK)),
        "w_o_sgu": w(ks[13], (L, SGU_WIDTH, D_MODEL), SGU_WIDTH),
        "w_out": w(ks[14], (L, D_MODEL, D_MODEL), D_MODEL),
        "norm_ffn_g": gain(ks[15], (L, D_MODEL)),
        "w_gate_ffn": w(ks[16], (L, D_MODEL, D_FF), D_MODEL),
        "w_up_ffn": w(ks[17], (L, D_MODEL, D_FF), D_MODEL),
        "w_down_ffn": w(ks[18], (L, D_FF, D_MODEL), D_FF),
        "norm_final_g": gain(ks[19], (D_MODEL,)),
    }


def _fwd_reference(x, positions, norm_mix_g, w_in, b_gate, q_norm_g, w_uq, kv_norm_g, w_ukv, w_o_attn,
              sgu_norm_g, w_sgu, b_sgu, w_o_sgu, w_out, norm_ffn_g, w_gate_ffn, w_up_ffn,
              w_down_ffn, norm_final_g):
    B, S, D = x.shape
    cos, sin = rope_tables(positions)
    o1 = Q_LORA_RANK
    o2 = o1 + KV_LORA_RANK
    o3 = o2 + QK_ROPE_DIM
    o4 = o3 + 2 * SGU_WIDTH
    h = x
    for l in range(DEPTH):
        a = rms_norm(h, norm_mix_g[l])
        z = a @ w_in[l]
        q_lat, kv_lat, k_pe = z[..., :o1], z[..., o1:o2], z[..., o2:o3]
        uv, gate_logits = z[..., o3:o4], z[..., o4:]
        y_attn = mla(q_lat, kv_lat, k_pe, cos, sin, q_norm_g[l], w_uq[l], kv_norm_g[l], w_ukv[l]) @ w_o_attn[l]
        y_sgu = sgu(uv, sgu_norm_g[l], w_sgu[l], b_sgu[l]) @ w_o_sgu[l]
        gates = jax.nn.sigmoid(gate_logits + b_gate[l]).reshape(B, S, N_BRANCH, D)
        merged = gates[:, :, 0, :] * y_attn + gates[:, :, 1, :] * y_sgu
        h = h + merged @ w_out[l]
        f = rms_norm(h, norm_ffn_g[l])
        h = h + (jax.nn.silu(f @ w_gate_ffn[l]) * (f @ w_up_ffn[l])) @ w_down_ffn[l]
    return rms_norm(h, norm_final_g)


import jax as _jax
import jax.numpy as _jnp

TWIN_FORMAT = 'train_step'
FWD_PARAMS = ['x', 'positions', 'norm_mix_g', 'w_in', 'b_gate', 'q_norm_g', 'w_uq', 'kv_norm_g', 'w_ukv', 'w_o_attn', 'sgu_norm_g', 'w_sgu', 'b_sgu', 'w_o_sgu', 'w_out', 'norm_ffn_g', 'w_gate_ffn', 'w_up_ffn', 'w_down_ffn', 'norm_final_g']
TWIN_WEIGHTS = ['norm_mix_g', 'w_in', 'b_gate', 'q_norm_g', 'w_uq', 'kv_norm_g', 'w_ukv', 'w_o_attn', 'sgu_norm_g', 'w_sgu', 'b_sgu', 'w_o_sgu', 'w_out', 'norm_ffn_g', 'w_gate_ffn', 'w_up_ffn', 'w_down_ffn', 'norm_final_g']
TWIN_DIFF_INPUT = 'x'
TWIN_INPUTS = ['x', 'positions', 'norm_mix_g', 'w_in', 'b_gate', 'q_norm_g', 'w_uq', 'kv_norm_g', 'w_ukv', 'w_o_attn', 'sgu_norm_g', 'w_sgu', 'b_sgu', 'w_o_sgu', 'w_out', 'norm_ffn_g', 'w_gate_ffn', 'w_up_ffn', 'w_down_ffn', 'norm_final_g', 'loss_target', 'm_norm_mix_g', 'm_w_in', 'm_b_gate', 'm_q_norm_g', 'm_w_uq', 'm_kv_norm_g', 'm_w_ukv', 'm_w_o_attn', 'm_sgu_norm_g', 'm_w_sgu', 'm_b_sgu', 'm_w_o_sgu', 'm_w_out', 'm_norm_ffn_g', 'm_w_gate_ffn', 'm_w_up_ffn', 'm_w_down_ffn', 'm_norm_final_g', 'v_norm_mix_g', 'v_w_in', 'v_b_gate', 'v_q_norm_g', 'v_w_uq', 'v_kv_norm_g', 'v_w_ukv', 'v_w_o_attn', 'v_sgu_norm_g', 'v_w_sgu', 'v_b_sgu', 'v_w_o_sgu', 'v_w_out', 'v_norm_ffn_g', 'v_w_gate_ffn', 'v_w_up_ffn', 'v_w_down_ffn', 'v_norm_final_g']
TWIN_OUTPUTS = ['loss', 'grad_x', 'grad_norm_mix_g', 'grad_w_in', 'grad_b_gate', 'grad_q_norm_g', 'grad_w_uq', 'grad_kv_norm_g', 'grad_w_ukv', 'grad_w_o_attn', 'grad_sgu_norm_g', 'grad_w_sgu', 'grad_b_sgu', 'grad_w_o_sgu', 'grad_w_out', 'grad_norm_ffn_g', 'grad_w_gate_ffn', 'grad_w_up_ffn', 'grad_w_down_ffn', 'grad_norm_final_g', 'delta_norm_mix_g', 'delta_w_in', 'delta_b_gate', 'delta_q_norm_g', 'delta_w_uq', 'delta_kv_norm_g', 'delta_w_ukv', 'delta_w_o_attn', 'delta_sgu_norm_g', 'delta_w_sgu', 'delta_b_sgu', 'delta_w_o_sgu', 'delta_w_out', 'delta_norm_ffn_g', 'delta_w_gate_ffn', 'delta_w_up_ffn', 'delta_w_down_ffn', 'delta_norm_final_g', 'new_m_norm_mix_g', 'new_m_w_in', 'new_m_b_gate', 'new_m_q_norm_g', 'new_m_w_uq', 'new_m_kv_norm_g', 'new_m_w_ukv', 'new_m_w_o_attn', 'new_m_sgu_norm_g', 'new_m_w_sgu', 'new_m_b_sgu', 'new_m_w_o_sgu', 'new_m_w_out', 'new_m_norm_ffn_g', 'new_m_w_gate_ffn', 'new_m_w_up_ffn', 'new_m_w_down_ffn', 'new_m_norm_final_g', 'new_v_norm_mix_g', 'new_v_w_in', 'new_v_b_gate', 'new_v_q_norm_g', 'new_v_w_uq', 'new_v_kv_norm_g', 'new_v_w_ukv', 'new_v_w_o_attn', 'new_v_sgu_norm_g', 'new_v_w_sgu', 'new_v_b_sgu', 'new_v_w_o_sgu', 'new_v_w_out', 'new_v_norm_ffn_g', 'new_v_w_gate_ffn', 'new_v_w_up_ffn', 'new_v_w_down_ffn', 'new_v_norm_final_g']
TWIN_LEAF_KINDS = {'loss': 'loss', 'grad_x': 'grad_x', 'grad_norm_mix_g': 'grad_w', 'grad_w_in': 'grad_w', 'grad_b_gate': 'grad_w', 'grad_q_norm_g': 'grad_w', 'grad_w_uq': 'grad_w', 'grad_kv_norm_g': 'grad_w', 'grad_w_ukv': 'grad_w', 'grad_w_o_attn': 'grad_w', 'grad_sgu_norm_g': 'grad_w', 'grad_w_sgu': 'grad_w', 'grad_b_sgu': 'grad_w', 'grad_w_o_sgu': 'grad_w', 'grad_w_out': 'grad_w', 'grad_norm_ffn_g': 'grad_w', 'grad_w_gate_ffn': 'grad_w', 'grad_w_up_ffn': 'grad_w', 'grad_w_down_ffn': 'grad_w', 'grad_norm_final_g': 'grad_w', 'delta_norm_mix_g': 'delta_w', 'delta_w_in': 'delta_w', 'delta_b_gate': 'delta_w', 'delta_q_norm_g': 'delta_w', 'delta_w_uq': 'delta_w', 'delta_kv_norm_g': 'delta_w', 'delta_w_ukv': 'delta_w', 'delta_w_o_attn': 'delta_w', 'delta_sgu_norm_g': 'delta_w', 'delta_w_sgu': 'delta_w', 'delta_b_sgu': 'delta_w', 'delta_w_o_sgu': 'delta_w', 'delta_w_out': 'delta_w', 'delta_norm_ffn_g': 'delta_w', 'delta_w_gate_ffn': 'delta_w', 'delta_w_up_ffn': 'delta_w', 'delta_w_down_ffn': 'delta_w', 'delta_norm_final_g': 'delta_w', 'new_m_norm_mix_g': 'new_m', 'new_m_w_in': 'new_m', 'new_m_b_gate': 'new_m', 'new_m_q_norm_g': 'new_m', 'new_m_w_uq': 'new_m', 'new_m_kv_norm_g': 'new_m', 'new_m_w_ukv': 'new_m', 'new_m_w_o_attn': 'new_m', 'new_m_sgu_norm_g': 'new_m', 'new_m_w_sgu': 'new_m', 'new_m_b_sgu': 'new_m', 'new_m_w_o_sgu': 'new_m', 'new_m_w_out': 'new_m', 'new_m_norm_ffn_g': 'new_m', 'new_m_w_gate_ffn': 'new_m', 'new_m_w_up_ffn': 'new_m', 'new_m_w_down_ffn': 'new_m', 'new_m_norm_final_g': 'new_m', 'new_v_norm_mix_g': 'new_v', 'new_v_w_in': 'new_v', 'new_v_b_gate': 'new_v', 'new_v_q_norm_g': 'new_v', 'new_v_w_uq': 'new_v', 'new_v_kv_norm_g': 'new_v', 'new_v_w_ukv': 'new_v', 'new_v_w_o_attn': 'new_v', 'new_v_sgu_norm_g': 'new_v', 'new_v_w_sgu': 'new_v', 'new_v_b_sgu': 'new_v', 'new_v_w_o_sgu': 'new_v', 'new_v_w_out': 'new_v', 'new_v_norm_ffn_g': 'new_v', 'new_v_w_gate_ffn': 'new_v', 'new_v_w_up_ffn': 'new_v', 'new_v_w_down_ffn': 'new_v', 'new_v_norm_final_g': 'new_v'}


def _forward(args):
    return _fwd_reference(*[args[k] for k in FWD_PARAMS])


def _output_shape():
    out = _jax.eval_shape(lambda: _forward(_fwd_setup_inputs(0)))
    return out.shape, out.dtype

N_MICROBATCH = 1
ADAM_LR = 0.001
ADAM_B1 = 0.9
ADAM_B2 = 0.999
ADAM_EPS = 1e-08
ADAM_WD = 0.01
ADAM_STEP = 10
PER_EXAMPLE_BATCH_AXIS = {'x': 0, 'positions': 0, 'loss_target': 0}
SHARED_INPUTS = []
_WEIGHT_DTYPES = {'norm_mix_g': _jnp.float32, 'w_in': _jnp.float32, 'b_gate': _jnp.float32, 'q_norm_g': _jnp.float32, 'w_uq': _jnp.float32, 'kv_norm_g': _jnp.float32, 'w_ukv': _jnp.float32, 'w_o_attn': _jnp.float32, 'sgu_norm_g': _jnp.float32, 'w_sgu': _jnp.float32, 'b_sgu': _jnp.float32, 'w_o_sgu': _jnp.float32, 'w_out': _jnp.float32, 'norm_ffn_g': _jnp.float32, 'w_gate_ffn': _jnp.float32, 'w_up_ffn': _jnp.float32, 'w_down_ffn': _jnp.float32, 'norm_final_g': _jnp.float32}
MOMENT_SCALE = {'norm_mix_g': 3.768414e-02, 'w_in': 1.941044e-02, 'b_gate': 7.701019e-03, 'q_norm_g': 1.196807e-02, 'w_uq': 5.031026e-03, 'kv_norm_g': 1.864393e-02, 'w_ukv': 6.351461e-03, 'w_o_attn': 7.381335e-03, 'sgu_norm_g': 2.254732e-02, 'w_sgu': 2.242468e-02, 'b_sgu': 3.178176e-02, 'w_o_sgu': 2.761661e-02, 'w_out': 2.827666e-02, 'norm_ffn_g': 4.348120e-02, 'w_gate_ffn': 1.868782e-02, 'w_up_ffn': 1.811033e-02, 'w_down_ffn': 3.004884e-02, 'norm_final_g': 7.996682e+00}


def _to_microbatches(a, axis):
    t = _jnp.moveaxis(a, axis, 0)
    t = t.reshape((N_MICROBATCH, t.shape[0] // N_MICROBATCH) + t.shape[1:])
    return _jnp.moveaxis(t, 1, axis + 1)


def setup_inputs(seed: int = 0) -> dict:
    inp = _fwd_setup_inputs(seed)
    key = _jax.random.fold_in(_jax.random.key(seed), 7919)
    shape, _ = _output_shape()
    out = dict(inp)
    out["loss_target"] = _jax.random.normal(_jax.random.fold_in(key, 0), shape, _jnp.float32)
    for i, name in enumerate(TWIN_WEIGHTS):
        w = inp[name].astype(_jnp.float32)
        if MOMENT_SCALE is None:
            s = _jnp.sqrt(_jnp.mean(_jnp.square(w)) + 1e-30)
        else:
            s = MOMENT_SCALE[name]
        km, kv = _jax.random.split(_jax.random.fold_in(key, i + 1))
        out[name] = w
        out["m_" + name] = s * _jax.random.normal(km, w.shape, _jnp.float32)
        out["v_" + name] = (s * s) * _jax.random.uniform(kv, w.shape, _jnp.float32, 0.5, 1.5)
    if N_MICROBATCH > 1:
        for name, axis in PER_EXAMPLE_BATCH_AXIS.items():
            out[name] = _to_microbatches(out[name], axis)
    return {'x': out['x'], 'positions': out['positions'], 'norm_mix_g': out['norm_mix_g'], 'w_in': out['w_in'], 'b_gate': out['b_gate'], 'q_norm_g': out['q_norm_g'], 'w_uq': out['w_uq'], 'kv_norm_g': out['kv_norm_g'], 'w_ukv': out['w_ukv'], 'w_o_attn': out['w_o_attn'], 'sgu_norm_g': out['sgu_norm_g'], 'w_sgu': out['w_sgu'], 'b_sgu': out['b_sgu'], 'w_o_sgu': out['w_o_sgu'], 'w_out': out['w_out'], 'norm_ffn_g': out['norm_ffn_g'], 'w_gate_ffn': out['w_gate_ffn'], 'w_up_ffn': out['w_up_ffn'], 'w_down_ffn': out['w_down_ffn'], 'norm_final_g': out['norm_final_g'], 'loss_target': out['loss_target'], 'm_norm_mix_g': out['m_norm_mix_g'], 'm_w_in': out['m_w_in'], 'm_b_gate': out['m_b_gate'], 'm_q_norm_g': out['m_q_norm_g'], 'm_w_uq': out['m_w_uq'], 'm_kv_norm_g': out['m_kv_norm_g'], 'm_w_ukv': out['m_w_ukv'], 'm_w_o_attn': out['m_w_o_attn'], 'm_sgu_norm_g': out['m_sgu_norm_g'], 'm_w_sgu': out['m_w_sgu'], 'm_b_sgu': out['m_b_sgu'], 'm_w_o_sgu': out['m_w_o_sgu'], 'm_w_out': out['m_w_out'], 'm_norm_ffn_g': out['m_norm_ffn_g'], 'm_w_gate_ffn': out['m_w_gate_ffn'], 'm_w_up_ffn': out['m_w_up_ffn'], 'm_w_down_ffn': out['m_w_down_ffn'], 'm_norm_final_g': out['m_norm_final_g'], 'v_norm_mix_g': out['v_norm_mix_g'], 'v_w_in': out['v_w_in'], 'v_b_gate': out['v_b_gate'], 'v_q_norm_g': out['v_q_norm_g'], 'v_w_uq': out['v_w_uq'], 'v_kv_norm_g': out['v_kv_norm_g'], 'v_w_ukv': out['v_w_ukv'], 'v_w_o_attn': out['v_w_o_attn'], 'v_sgu_norm_g': out['v_sgu_norm_g'], 'v_w_sgu': out['v_w_sgu'], 'v_b_sgu': out['v_b_sgu'], 'v_w_o_sgu': out['v_w_o_sgu'], 'v_w_out': out['v_w_out'], 'v_norm_ffn_g': out['v_norm_ffn_g'], 'v_w_gate_ffn': out['v_w_gate_ffn'], 'v_w_up_ffn': out['v_w_up_ffn'], 'v_w_down_ffn': out['v_w_down_ffn'], 'v_norm_final_g': out['v_norm_final_g']}


def _loss(weights, diff, rest, loss_target):
    with _jax.named_scope("forward"):
        args = {**rest, TWIN_DIFF_INPUT: diff, **{k: w.astype(_WEIGHT_DTYPES[k]) for k, w in weights.items()}}
        y = _forward(args)
    with _jax.named_scope("loss_head"):
        err = _jnp.square(y.astype(_jnp.float32) - loss_target)
        return 0.5 * _jnp.sum(_jnp.mean(err, axis=-1)) if err.ndim else 0.5 * err


def _adamw(w, g, m, v):
    m = ADAM_B1 * m + (1.0 - ADAM_B1) * g
    v = ADAM_B2 * v + (1.0 - ADAM_B2) * _jnp.square(g)
    m_hat = m / (1.0 - ADAM_B1 ** ADAM_STEP)
    v_hat = v / (1.0 - ADAM_B2 ** ADAM_STEP)
    delta = -ADAM_LR * (m_hat / (_jnp.sqrt(v_hat) + ADAM_EPS) + ADAM_WD * w)
    return delta, m, v


def reference(x, positions, norm_mix_g, w_in, b_gate, q_norm_g, w_uq, kv_norm_g, w_ukv, w_o_attn, sgu_norm_g, w_sgu, b_sgu, w_o_sgu, w_out, norm_ffn_g, w_gate_ffn, w_up_ffn, w_down_ffn, norm_final_g, loss_target, m_norm_mix_g, m_w_in, m_b_gate, m_q_norm_g, m_w_uq, m_kv_norm_g, m_w_ukv, m_w_o_attn, m_sgu_norm_g, m_w_sgu, m_b_sgu, m_w_o_sgu, m_w_out, m_norm_ffn_g, m_w_gate_ffn, m_w_up_ffn, m_w_down_ffn, m_norm_final_g, v_norm_mix_g, v_w_in, v_b_gate, v_q_norm_g, v_w_uq, v_kv_norm_g, v_w_ukv, v_w_o_attn, v_sgu_norm_g, v_w_sgu, v_b_sgu, v_w_o_sgu, v_w_out, v_norm_ffn_g, v_w_gate_ffn, v_w_up_ffn, v_w_down_ffn, v_norm_final_g):
    given = dict(x=x, positions=positions, norm_mix_g=norm_mix_g, w_in=w_in, b_gate=b_gate, q_norm_g=q_norm_g, w_uq=w_uq, kv_norm_g=kv_norm_g, w_ukv=w_ukv, w_o_attn=w_o_attn, sgu_norm_g=sgu_norm_g, w_sgu=w_sgu, b_sgu=b_sgu, w_o_sgu=w_o_sgu, w_out=w_out, norm_ffn_g=norm_ffn_g, w_gate_ffn=w_gate_ffn, w_up_ffn=w_up_ffn, w_down_ffn=w_down_ffn, norm_final_g=norm_final_g, loss_target=loss_target, m_norm_mix_g=m_norm_mix_g, m_w_in=m_w_in, m_b_gate=m_b_gate, m_q_norm_g=m_q_norm_g, m_w_uq=m_w_uq, m_kv_norm_g=m_kv_norm_g, m_w_ukv=m_w_ukv, m_w_o_attn=m_w_o_attn, m_sgu_norm_g=m_sgu_norm_g, m_w_sgu=m_w_sgu, m_b_sgu=m_b_sgu, m_w_o_sgu=m_w_o_sgu, m_w_out=m_w_out, m_norm_ffn_g=m_norm_ffn_g, m_w_gate_ffn=m_w_gate_ffn, m_w_up_ffn=m_w_up_ffn, m_w_down_ffn=m_w_down_ffn, m_norm_final_g=m_norm_final_g, v_norm_mix_g=v_norm_mix_g, v_w_in=v_w_in, v_b_gate=v_b_gate, v_q_norm_g=v_q_norm_g, v_w_uq=v_w_uq, v_kv_norm_g=v_kv_norm_g, v_w_ukv=v_w_ukv, v_w_o_attn=v_w_o_attn, v_sgu_norm_g=v_sgu_norm_g, v_w_sgu=v_w_sgu, v_b_sgu=v_b_sgu, v_w_o_sgu=v_w_o_sgu, v_w_out=v_w_out, v_norm_ffn_g=v_norm_ffn_g, v_w_gate_ffn=v_w_gate_ffn, v_w_up_ffn=v_w_up_ffn, v_w_down_ffn=v_w_down_ffn, v_norm_final_g=v_norm_final_g)
    weights = {n: given[n] for n in TWIN_WEIGHTS}
    shared = {n: given[n] for n in SHARED_INPUTS}
    per_example = {n: given[n] for n in ['x', 'positions']}
    grad_fn = _jax.value_and_grad(_loss, argnums=(0, 1))

    def one_microbatch(ex, loss_target):
        ex = dict(ex)
        diff = ex.pop(TWIN_DIFF_INPUT)
        return grad_fn(weights, diff, {**shared, **ex}, loss_target)

    if N_MICROBATCH == 1:
        loss, (grad_w, grad_x) = one_microbatch(per_example, given["loss_target"])
    else:
        def body(carry, xs):
            loss_sum, grad_sum = carry
            l_k, (gw_k, gx_k) = one_microbatch(xs[0], xs[1])
            with _jax.named_scope("update"):
                return (loss_sum + l_k, _jax.tree.map(_jnp.add, grad_sum, gw_k)), gx_k

        init = (_jnp.zeros((), _jnp.float32), _jax.tree.map(_jnp.zeros_like, weights))
        (loss, grad_w), grad_x = _jax.lax.scan(body, init, (per_example, given["loss_target"]))
    with _jax.named_scope("update"):
        delta_w, new_m, new_v = {}, {}, {}
        for n in TWIN_WEIGHTS:
            delta_w[n], new_m[n], new_v[n] = _adamw(weights[n], grad_w[n], given["m_" + n], given["v_" + n])
    return (loss, grad_x, *[grad_w[n] for n in TWIN_WEIGHTS], *[delta_w[n] for n in TWIN_WEIGHTS],
            *[new_m[n] for n in TWIN_WEIGHTS], *[new_v[n] for n in TWIN_WEIGHTS])
```

```python
import functools
import math

import jax
import jax.numpy as jnp
from jax import lax
from jax.experimental import pallas as pl
from jax.experimental.pallas import tpu as pltpu

F32 = jnp.float32
BF16 = jnp.bfloat16

N_DEV = 8
N_HEADS = 16
QK_NOPE = 128
QK_ROPE = 64
V_HEAD = 128
HEAD_PAD = 256
ROPE_THETA = 10000.0
CHUNK = 128
SGU_GROUP = 128
RMS_EPS = 1e-6
LANES = 128
SUBLANES = 8

ADAM_LR = 0.001
ADAM_B1 = 0.9
ADAM_B2 = 0.999
ADAM_EPS = 1e-08
ADAM_WD = 0.01
ADAM_STEP = 10

VMEM_LIMIT = 48 * 1024 * 1024
MM_TILE = (1024, 512, 2048)
ATTN_TILE = 512
ROW_KERNEL_BYTES = 24 * 1024 * 1024
NEG_BIG = -1e30
MESH = pl.DeviceIdType.MESH


def _pick(n, target, mult=LANES):
    best = None
    d = mult
    while d <= min(n, target):
        if n % d == 0:
            best = d
        d += mult
    return best or n


def _row_tile(t, width, n_blocks, mult=2 * SUBLANES):
    return _pick(t, max(mult, ROW_KERNEL_BYTES // (3 * n_blocks * width * 4)), mult)


def _params(sem):
    return pltpu.CompilerParams(dimension_semantics=sem, vmem_limit_bytes=VMEM_LIMIT)


def _full(shape):
    nd = len(shape)
    return pl.BlockSpec(shape, lambda *_: (0,) * nd)


def _rows(tr, w, cb=0):
    return pl.BlockSpec((tr, w), lambda i: (i, cb))


def _mm(a, b, *, ta=False, tb=False, add=None, out_dtype=F32, tm=None, tn=None, tk=None, name):
    m, k = (a.shape[1], a.shape[0]) if ta else a.shape
    n = b.shape[0] if tb else b.shape[1]
    assert k == (b.shape[1] if tb else b.shape[0]), (a.shape, b.shape, ta, tb)
    tm, tn, tk = _pick(m, tm or MM_TILE[0]), _pick(n, tn or MM_TILE[1]), _pick(k, tk or MM_TILE[2])
    nk = k // tk
    dims = (((0 if ta else 1,), (1 if tb else 0,)), ((), ()))

    def body(*refs):
        if add is None:
            a_ref, b_ref, o_ref, acc_ref = refs
            add_ref = None
        else:
            a_ref, b_ref, add_ref, o_ref, acc_ref = refs
        kk = pl.program_id(2)

        @pl.when(kk == 0)
        def _():
            acc_ref[...] = jnp.zeros_like(acc_ref)

        acc_ref[...] += lax.dot_general(a_ref[...].astype(BF16), b_ref[...].astype(BF16), dims,
                                        preferred_element_type=F32)

        @pl.when(kk == nk - 1)
        def _():
            r = acc_ref[...]
            if add_ref is not None:
                r = r + add_ref[...].astype(F32)
            o_ref[...] = r.astype(o_ref.dtype)

    a_spec = pl.BlockSpec((tk, tm), lambda i, j, kk: (kk, i)) if ta else pl.BlockSpec((tm, tk), lambda i, j, kk: (i, kk))
    b_spec = pl.BlockSpec((tn, tk), lambda i, j, kk: (j, kk)) if tb else pl.BlockSpec((tk, tn), lambda i, j, kk: (kk, j))
    o_spec = pl.BlockSpec((tm, tn), lambda i, j, kk: (i, j))
    in_specs = [a_spec, b_spec] + ([o_spec] if add is not None else [])
    args = (a, b) + ((add,) if add is not None else ())
    return pl.pallas_call(
        body, grid=(m // tm, n // tn, nk), in_specs=in_specs, out_specs=o_spec,
        out_shape=jax.ShapeDtypeStruct((m, n), out_dtype), scratch_shapes=[pltpu.VMEM((tm, tn), F32)],
        compiler_params=_params(("parallel", "parallel", "arbitrary")), name=name)(*args)


def _rms_scale(x):
    return lax.rsqrt(jnp.mean(x * x, axis=-1, keepdims=True) + RMS_EPS)


def _rms_bwd(xhat, r, g, dy):
    t = dy * g
    dx = r * (t - xhat * jnp.mean(t * xhat, axis=-1, keepdims=True))
    return dx, dy * xhat


_GELU_C = math.sqrt(2.0 / math.pi)


def _gelu(x):
    return x * (0.5 * (1.0 + jnp.tanh(_GELU_C * (x + 0.044715 * (x * x * x)))))


def _gelu_and_grad(x):
    t = jnp.tanh(_GELU_C * (x + 0.044715 * (x * x * x)))
    cdf = 0.5 * (1.0 + t)
    return x * cdf, cdf + x * (0.5 * (1.0 - t * t) * (_GELU_C * (1.0 + 3.0 * 0.044715 * (x * x))))


def _sigmoid(x):
    return 1.0 / (1.0 + jnp.exp(-x))


def _swap_halves(x):
    lane = lax.broadcasted_iota(jnp.int32, x.shape, 1)
    first = (lane % QK_ROPE) < (QK_ROPE // 2)
    return jnp.where(first, pltpu.roll(x, LANES - QK_ROPE // 2, 1), pltpu.roll(x, QK_ROPE // 2, 1))


def _rope(x, cos, sin_signed):
    return x * cos + _swap_halves(x) * sin_signed


def _rope_bwd(d, cos, sin_signed):
    return d * cos + _swap_halves(d * sin_signed)


def _rope_tables(pos_col, inv_freq_row, sign_row):
    t = pos_col.shape[0]
    tr = _pick(t, 512, SUBLANES)

    def body(p_ref, f_ref, s_ref, cos_ref, sin_ref):
        ang = p_ref[...].astype(F32) * f_ref[...]
        cos_ref[...] = jnp.cos(ang)
        sin_ref[...] = jnp.sin(ang) * s_ref[...]

    return pl.pallas_call(
        body, grid=(t // tr,), in_specs=[_rows(tr, 1), _full((1, LANES)), _full((1, LANES))],
        out_specs=[_rows(tr, LANES), _rows(tr, LANES)],
        out_shape=[jax.ShapeDtypeStruct((t, LANES), F32)] * 2,
        compiler_params=_params(("parallel",)), name="rope_tables")(pos_col, inv_freq_row, sign_row)


def _norm_fwd(x, g, name):
    t, d = x.shape
    tr = _row_tile(t, d, 2)

    def body(x_ref, g_ref, y_ref):
        xv = x_ref[...]
        y_ref[...] = (xv * _rms_scale(xv) * g_ref[...]).astype(BF16)

    return pl.pallas_call(
        body, grid=(t // tr,), in_specs=[_rows(tr, d), _full((1, d))], out_specs=_rows(tr, d),
        out_shape=jax.ShapeDtypeStruct((t, d), BF16), compiler_params=_params(("parallel",)), name=name)(x, g)


def _lat_fwd(z_lat, qg, kvg, cos, sin, ql, kvl):
    t = z_lat.shape[0]
    tr = _row_tile(t, z_lat.shape[1], 2)

    def body(z_ref, qg_ref, kvg_ref, cos_ref, sin_ref, qn_ref, kvn_ref, kpe_ref):
        q = z_ref[:, 0:ql]
        qn_ref[...] = (q * _rms_scale(q) * qg_ref[...]).astype(BF16)
        kv = z_ref[:, ql:ql + kvl]
        kvn_ref[...] = (kv * _rms_scale(kv) * kvg_ref[...]).astype(BF16)
        kpe_ref[...] = _rope(z_ref[:, ql + kvl:ql + kvl + LANES], cos_ref[...], sin_ref[...]).astype(BF16)

    w = z_lat.shape[1]
    return pl.pallas_call(
        body, grid=(t // tr,),
        in_specs=[_rows(tr, w), _full((1, ql)), _full((1, kvl)), _rows(tr, LANES), _rows(tr, LANES)],
        out_specs=[_rows(tr, ql), _rows(tr, kvl), _rows(tr, LANES)],
        out_shape=[jax.ShapeDtypeStruct((t, ql), BF16), jax.ShapeDtypeStruct((t, kvl), BF16),
                   jax.ShapeDtypeStruct((t, LANES), BF16)],
        compiler_params=_params(("parallel",)), name="lat_fwd")(z_lat, qg, kvg, cos, sin)


def _q_rope(q_p, cos, sin, bwd, name):
    t, w = q_p.shape
    tr = _row_tile(t, w, 2)
    fn = _rope_bwd if bwd else _rope

    def body(q_ref, cos_ref, sin_ref, o_ref):
        c, s = cos_ref[...], sin_ref[...]
        for h in range(w // HEAD_PAD):
            o_ref[:, h * HEAD_PAD:h * HEAD_PAD + QK_NOPE] = q_ref[:, h * HEAD_PAD:h * HEAD_PAD + QK_NOPE].astype(BF16)
            lo = h * HEAD_PAD + QK_NOPE
            o_ref[:, lo:lo + LANES] = fn(q_ref[:, lo:lo + LANES].astype(F32), c, s).astype(BF16)

    return pl.pallas_call(
        body, grid=(t // tr,), in_specs=[_rows(tr, w), _rows(tr, LANES), _rows(tr, LANES)], out_specs=_rows(tr, w),
        out_shape=jax.ShapeDtypeStruct((t, w), BF16), compiler_params=_params(("parallel",)), name=name)(q_p, cos, sin)


def _tril_mask():
    r = lax.broadcasted_iota(jnp.int32, (CHUNK, CHUNK), 0)
    c = lax.broadcasted_iota(jnp.int32, (CHUNK, CHUNK), 1)
    return r >= c


def _sgu_fwd(z_uv, gs, ws, b_col):
    t = z_uv.shape[0]
    sw = z_uv.shape[1] // 2
    groups = sw // SGU_GROUP
    tr = _pick(t, 256, CHUNK)

    def body(u_ref, v_ref, gs_ref, ws_ref, b_ref, o_ref):
        v = _gelu(v_ref[...])
        vn = (v * _rms_scale(v) * gs_ref[...]).astype(BF16)
        tri = _tril_mask()
        for g in range(groups):
            wg = jnp.where(tri, ws_ref[g], 0.0).astype(BF16)
            cols = slice(g * SGU_GROUP, (g + 1) * SGU_GROUP)
            for c in range(tr // CHUNK):
                rows = slice(c * CHUNK, (c + 1) * CHUNK)
                mixed = jnp.dot(wg, vn[rows, cols], preferred_element_type=F32) + b_ref[g]
                o_ref[rows, cols] = (_gelu(u_ref[rows, cols]) * mixed).astype(BF16)

    return pl.pallas_call(
        body, grid=(t // tr,),
        in_specs=[_rows(tr, sw, 0), _rows(tr, sw, 1), _full((1, sw)), _full(ws.shape), _full(b_col.shape)],
        out_specs=_rows(tr, sw), out_shape=jax.ShapeDtypeStruct((t, sw), BF16),
        compiler_params=_params(("parallel",)), name="sgu_fwd")(z_uv, z_uv, gs, ws, b_col)


def _merge_fwd(y_attn, y_sgu, z_g, b_gate):
    t, d = y_attn.shape
    tr = _row_tile(t, d, 5)

    def body(ya_ref, ys_ref, g0_ref, g1_ref, b0_ref, b1_ref, o_ref):
        g0 = _sigmoid(g0_ref[...] + b0_ref[...])
        g1 = _sigmoid(g1_ref[...] + b1_ref[...])
        o_ref[...] = (g0 * ya_ref[...] + g1 * ys_ref[...]).astype(BF16)

    bspec0 = pl.BlockSpec((1, d), lambda i: (0, 0))
    bspec1 = pl.BlockSpec((1, d), lambda i: (0, 1))
    return pl.pallas_call(
        body, grid=(t // tr,),
        in_specs=[_rows(tr, d), _rows(tr, d), _rows(tr, d, 0), _rows(tr, d, 1), bspec0, bspec1],
        out_specs=_rows(tr, d), out_shape=jax.ShapeDtypeStruct((t, d), BF16),
        compiler_params=_params(("parallel",)), name="merge_fwd")(y_attn, y_sgu, z_g, z_g, b_gate, b_gate)


def _swiglu_fwd(gate, up):
    t, f = gate.shape
    tr = _row_tile(t, f, 3)

    def body(g_ref, u_ref, o_ref):
        g = g_ref[...]
        o_ref[...] = (g * _sigmoid(g) * u_ref[...]).astype(BF16)

    return pl.pallas_call(
        body, grid=(t // tr,), in_specs=[_rows(tr, f), _rows(tr, f)], out_specs=_rows(tr, f),
        out_shape=jax.ShapeDtypeStruct((t, f), BF16), compiler_params=_params(("parallel",)), name="swiglu_fwd")(gate, up)


def _loss_head(h2, g, target):
    t, d = h2.shape
    tr = _row_tile(t, d, 3)

    def body(h_ref, g_ref, t_ref, loss_ref, dh_ref, dg_ref):
        @pl.when(pl.program_id(0) == 0)
        def _():
            loss_ref[...] = jnp.zeros_like(loss_ref)
            dg_ref[...] = jnp.zeros_like(dg_ref)

        h = h_ref[...]
        r = _rms_scale(h)
        hhat = h * r
        gv = g_ref[...]
        err = hhat * gv - t_ref[...]
        loss_ref[...] += jnp.full(loss_ref.shape, 0.5 * jnp.sum(jnp.mean(err * err, axis=-1)), F32)
        dx, dg_rows = _rms_bwd(hhat, r, gv, err * (1.0 / d))
        dh_ref[...] = dx
        dg_ref[...] += jnp.sum(dg_rows, axis=0, keepdims=True)

    return pl.pallas_call(
        body, grid=(t // tr,), in_specs=[_rows(tr, d), _full((1, d)), _rows(tr, d)],
        out_specs=[_full((1, LANES)), _rows(tr, d), _full((1, d))],
        out_shape=[jax.ShapeDtypeStruct((1, LANES), F32), jax.ShapeDtypeStruct((t, d), F32),
                   jax.ShapeDtypeStruct((1, d), F32)],
        compiler_params=_params(("arbitrary",)), name="loss_head")(h2, g, target)


def _swiglu_bwd(gate, up, dact):
    t, f = gate.shape
    tr = _row_tile(t, f, 4)

    def body(g_ref, u_ref, d_ref, dg_ref, du_ref):
        g = g_ref[...]
        s = _sigmoid(g)
        d = d_ref[...]
        dg_ref[...] = (d * u_ref[...] * (s * (1.0 + g * (1.0 - s)))).astype(BF16)
        du_ref[...] = (d * (g * s)).astype(BF16)

    return pl.pallas_call(
        body, grid=(t // tr,), in_specs=[_rows(tr, f)] * 3, out_specs=[_rows(tr, f)] * 2,
        out_shape=[jax.ShapeDtypeStruct((t, f), BF16)] * 2,
        compiler_params=_params(("parallel",)), name="swiglu_bwd")(gate, up, dact)


def _norm_bwd(x, g, dy, resid, name):
    t, d = x.shape
    tr = _row_tile(t, d, 4)

    def body(x_ref, g_ref, dy_ref, r_ref, dx_ref, dg_ref):
        @pl.when(pl.program_id(0) == 0)
        def _():
            dg_ref[...] = jnp.zeros_like(dg_ref)

        xv = x_ref[...]
        r = _rms_scale(xv)
        dx, dg_rows = _rms_bwd(xv * r, r, g_ref[...], dy_ref[...])
        dx_ref[...] = r_ref[...] + dx
        dg_ref[...] += jnp.sum(dg_rows, axis=0, keepdims=True)

    return pl.pallas_call(
        body, grid=(t // tr,), in_specs=[_rows(tr, d), _full((1, d)), _rows(tr, d), _rows(tr, d)],
        out_specs=[_rows(tr, d), _full((1, d))],
        out_shape=[jax.ShapeDtypeStruct((t, d), F32), jax.ShapeDtypeStruct((1, d), F32)],
        compiler_params=_params(("arbitrary",)), name=name)(x, g, dy, resid)


def _merge_bwd(dmerged, y_attn, y_sgu, z_g, b_gate):
    t, d = y_attn.shape
    tr = _row_tile(t, d, 7)

    def body(dm_ref, ya_ref, ys_ref, g0_ref, g1_ref, b0_ref, b1_ref, dya_ref, dys_ref, dz_ref, db_ref):
        @pl.when(pl.program_id(0) == 0)
        def _():
            db_ref[...] = jnp.zeros_like(db_ref)

        dm = dm_ref[...]
        g0 = _sigmoid(g0_ref[...] + b0_ref[...])
        g1 = _sigmoid(g1_ref[...] + b1_ref[...])
        dya_ref[...] = (dm * g0).astype(BF16)
        dys_ref[...] = (dm * g1).astype(BF16)
        dl0 = dm * ya_ref[...] * (g0 * (1.0 - g0))
        dl1 = dm * ys_ref[...] * (g1 * (1.0 - g1))
        dz_ref[:, 0:d] = dl0.astype(BF16)
        dz_ref[:, d:2 * d] = dl1.astype(BF16)
        db_ref[:, 0:d] += jnp.sum(dl0, axis=0, keepdims=True)
        db_ref[:, d:2 * d] += jnp.sum(dl1, axis=0, keepdims=True)

    bspec0 = pl.BlockSpec((1, d), lambda i: (0, 0))
    bspec1 = pl.BlockSpec((1, d), lambda i: (0, 1))
    return pl.pallas_call(
        body, grid=(t // tr,),
        in_specs=[_rows(tr, d), _rows(tr, d), _rows(tr, d), _rows(tr, d, 0), _rows(tr, d, 1), bspec0, bspec1],
        out_specs=[_rows(tr, d), _rows(tr, d), _rows(tr, 2 * d), _full((1, 2 * d))],
        out_shape=[jax.ShapeDtypeStruct((t, d), BF16), jax.ShapeDtypeStruct((t, d), BF16),
                   jax.ShapeDtypeStruct((t, 2 * d), BF16), jax.ShapeDtypeStruct((1, 2 * d), F32)],
        compiler_params=_params(("arbitrary",)), name="merge_bwd")(dmerged, y_attn, y_sgu, z_g, z_g, b_gate, b_gate)


def _sgu_bwd(z_uv, ds_out, gs, ws, b_col):
    t = z_uv.shape[0]
    sw = z_uv.shape[1] // 2
    groups = sw // SGU_GROUP
    tr = _pick(t, 256, CHUNK)

    def body(u_ref, v_ref, d_ref, gs_ref, ws_ref, b_ref, dz_ref, dws_ref, db_ref, dgs_ref, dvn_ref):
        @pl.when(pl.program_id(0) == 0)
        def _():
            dws_ref[...] = jnp.zeros_like(dws_ref)
            db_ref[...] = jnp.zeros_like(db_ref)
            dgs_ref[...] = jnp.zeros_like(dgs_ref)

        v, dgelu_v = _gelu_and_grad(v_ref[...])
        r = _rms_scale(v)
        vhat = v * r
        gsv = gs_ref[...]
        vn = (vhat * gsv).astype(BF16)
        tri = _tril_mask()
        for g in range(groups):
            wg = jnp.where(tri, ws_ref[g], 0.0).astype(BF16)
            cols = slice(g * SGU_GROUP, (g + 1) * SGU_GROUP)
            for c in range(tr // CHUNK):
                rows = slice(c * CHUNK, (c + 1) * CHUNK)
                vn_cg = vn[rows, cols]
                mixed = jnp.dot(wg, vn_cg, preferred_element_type=F32) + b_ref[g]
                u, dgelu_u = _gelu_and_grad(u_ref[rows, cols])
                dso = d_ref[rows, cols]
                dz_ref[rows, cols] = (dso * mixed * dgelu_u).astype(BF16)
                dmixed = dso * u
                db_ref[g] += jnp.sum(dmixed, axis=1, keepdims=True)
                dmixed_b = dmixed.astype(BF16)
                dws_ref[g] += jnp.where(
                    tri, lax.dot_general(dmixed_b, vn_cg, (((1,), (1,)), ((), ())), preferred_element_type=F32), 0.0)
                dvn_ref[rows, cols] = lax.dot_general(wg, dmixed_b, (((0,), (0,)), ((), ())), preferred_element_type=F32)
        dvn = dvn_ref[...]
        dv, dgs_rows = _rms_bwd(vhat, r, gsv, dvn)
        dz_ref[:, sw:2 * sw] = (dv * dgelu_v).astype(BF16)
        dgs_ref[...] += jnp.sum(dgs_rows, axis=0, keepdims=True)

    return pl.pallas_call(
        body, grid=(t // tr,),
        in_specs=[_rows(tr, sw, 0), _rows(tr, sw, 1), _rows(tr, sw), _full((1, sw)), _full(ws.shape), _full(b_col.shape)],
        out_specs=[_rows(tr, 2 * sw), _full(ws.shape), _full(b_col.shape), _full((1, sw))],
        out_shape=[jax.ShapeDtypeStruct((t, 2 * sw), BF16), jax.ShapeDtypeStruct(ws.shape, F32),
                   jax.ShapeDtypeStruct(b_col.shape, F32), jax.ShapeDtypeStruct((1, sw), F32)],
        scratch_shapes=[pltpu.VMEM((tr, sw), F32)],
        compiler_params=_params(("arbitrary",)), name="sgu_bwd")(z_uv, z_uv, ds_out, gs, ws, b_col)


def _lat_bwd(z_lat, qg, kvg, dqn, dkvn, dkpe_heads, cos, sin, ql, kvl):
    t, w = z_lat.shape
    heads = dkpe_heads.shape[0]
    tr = _row_tile(t, w + heads * LANES, 3)

    def body(z_ref, qg_ref, kvg_ref, dq_ref, dkv_ref, dk_ref, cos_ref, sin_ref, dz_ref, dqg_ref, dkvg_ref):
        @pl.when(pl.program_id(0) == 0)
        def _():
            dqg_ref[...] = jnp.zeros_like(dqg_ref)
            dkvg_ref[...] = jnp.zeros_like(dkvg_ref)

        q = z_ref[:, 0:ql]
        r = _rms_scale(q)
        dx, dg_rows = _rms_bwd(q * r, r, qg_ref[...], dq_ref[...])
        dz_ref[:, 0:ql] = dx.astype(BF16)
        dqg_ref[...] += jnp.sum(dg_rows, axis=0, keepdims=True)
        kv = z_ref[:, ql:ql + kvl]
        r = _rms_scale(kv)
        dx, dg_rows = _rms_bwd(kv * r, r, kvg_ref[...], dkv_ref[...])
        dz_ref[:, ql:ql + kvl] = dx.astype(BF16)
        dkvg_ref[...] += jnp.sum(dg_rows, axis=0, keepdims=True)
        dk = dk_ref[0]
        for h in range(1, heads):
            dk = dk + dk_ref[h]
        dz_ref[:, ql + kvl:ql + kvl + LANES] = _rope_bwd(dk, cos_ref[...], sin_ref[...]).astype(BF16)

    return pl.pallas_call(
        body, grid=(t // tr,),
        in_specs=[_rows(tr, w), _full((1, ql)), _full((1, kvl)), _rows(tr, ql), _rows(tr, kvl),
                  pl.BlockSpec((heads, tr, LANES), lambda i: (0, i, 0)), _rows(tr, LANES), _rows(tr, LANES)],
        out_specs=[_rows(tr, w), _full((1, ql)), _full((1, kvl))],
        out_shape=[jax.ShapeDtypeStruct((t, w), BF16), jax.ShapeDtypeStruct((1, ql), F32),
                   jax.ShapeDtypeStruct((1, kvl), F32)],
        compiler_params=_params(("arbitrary",)), name="lat_bwd")(z_lat, qg, kvg, dqn, dkvn, dkpe_heads, cos, sin)


_NT = (((1,), (1,)), ((), ()))


def _attn_scale():
    return (QK_NOPE + QK_ROPE) ** -0.5


def _attn_fwd(q_c, kv, kpe):
    t = q_c.shape[0]
    heads = q_c.shape[1] // HEAD_PAD
    tq = _pick(t, ATTN_TILE)
    nq = t // tq
    scale = _attn_scale()

    def body(q_ref, kn_ref, kpe_ref, v_ref, o_ref, lse_ref, m_sc, l_sc, acc_sc):
        qi, ki = pl.program_id(1), pl.program_id(2)

        @pl.when(ki == 0)
        def _():
            m_sc[...] = jnp.full_like(m_sc, NEG_BIG)
            l_sc[...] = jnp.zeros_like(l_sc)
            acc_sc[...] = jnp.zeros_like(acc_sc)

        @pl.when(ki <= qi)
        def _():
            kc = jnp.concatenate([kn_ref[...], kpe_ref[...]], axis=1)
            s = lax.dot_general(q_ref[...], kc, _NT, preferred_element_type=F32) * scale
            row = lax.broadcasted_iota(jnp.int32, s.shape, 0) + qi * tq
            col = lax.broadcasted_iota(jnp.int32, s.shape, 1) + ki * tq
            s = jnp.where(row >= col, s, NEG_BIG)
            m_prev = m_sc[...]
            m_new = jnp.maximum(m_prev, jnp.max(s, axis=1, keepdims=True))
            alpha = jnp.exp(m_prev - m_new)
            p = jnp.exp(s - m_new)
            l_sc[...] = alpha * l_sc[...] + jnp.sum(p, axis=1, keepdims=True)
            acc_sc[...] = alpha * acc_sc[...] + jnp.dot(p.astype(BF16), v_ref[...], preferred_element_type=F32)
            m_sc[...] = m_new

        @pl.when(ki == qi)
        def _():
            o_ref[...] = acc_sc[...] / l_sc[...]
            lse_ref[0] = m_sc[...] + jnp.log(l_sc[...])

    kmap = lambda blk: (lambda h, qi, ki: (jnp.minimum(ki, qi), 2 * h + blk))
    return pl.pallas_call(
        body, grid=(heads, nq, nq),
        in_specs=[pl.BlockSpec((tq, HEAD_PAD), lambda h, qi, ki: (qi, h)),
                  pl.BlockSpec((tq, QK_NOPE), kmap(0)),
                  pl.BlockSpec((tq, LANES), lambda h, qi, ki: (jnp.minimum(ki, qi), 0)),
                  pl.BlockSpec((tq, V_HEAD), kmap(1))],
        out_specs=[pl.BlockSpec((tq, V_HEAD), lambda h, qi, ki: (qi, h)),
                   pl.BlockSpec((1, tq, 1), lambda h, qi, ki: (h, qi, 0))],
        out_shape=[jax.ShapeDtypeStruct((t, heads * V_HEAD), F32), jax.ShapeDtypeStruct((heads, t, 1), F32)],
        scratch_shapes=[pltpu.VMEM((tq, 1), F32), pltpu.VMEM((tq, 1), F32), pltpu.VMEM((tq, V_HEAD), F32)],
        compiler_params=_params(("parallel", "parallel", "arbitrary")), name="attn_fwd")(q_c, kv, kpe, kv)


def _attn_bwd_q(q_c, kv, kpe, o, do, lse):
    t = q_c.shape[0]
    heads = q_c.shape[1] // HEAD_PAD
    tq = _pick(t, ATTN_TILE)
    nq = t // tq
    scale = _attn_scale()

    def body(q_ref, kn_ref, kpe_ref, v_ref, o_ref, do_ref, lse_ref, dq_ref, delta_ref, acc_sc):
        qi, ki = pl.program_id(1), pl.program_id(2)

        @pl.when(ki == 0)
        def _():
            acc_sc[...] = jnp.zeros_like(acc_sc)
            delta_ref[0] = jnp.sum(do_ref[...] * o_ref[...], axis=1, keepdims=True)

        @pl.when(ki <= qi)
        def _():
            kc = jnp.concatenate([kn_ref[...], kpe_ref[...]], axis=1)
            s = lax.dot_general(q_ref[...], kc, _NT, preferred_element_type=F32) * scale
            row = lax.broadcasted_iota(jnp.int32, s.shape, 0) + qi * tq
            col = lax.broadcasted_iota(jnp.int32, s.shape, 1) + ki * tq
            p = jnp.where(row >= col, jnp.exp(s - lse_ref[0]), 0.0)
            dp = lax.dot_general(do_ref[...].astype(BF16), v_ref[...], _NT, preferred_element_type=F32)
            ds = (p * (dp - delta_ref[0]) * scale).astype(BF16)
            acc_sc[...] += jnp.dot(ds, kc, preferred_element_type=F32)

        @pl.when(ki == qi)
        def _():
            dq_ref[...] = acc_sc[...]

    kmap = lambda blk: (lambda h, qi, ki: (jnp.minimum(ki, qi), 2 * h + blk))
    qmap = lambda h, qi, ki: (qi, h)
    smap = lambda h, qi, ki: (h, qi, 0)
    return pl.pallas_call(
        body, grid=(heads, nq, nq),
        in_specs=[pl.BlockSpec((tq, HEAD_PAD), qmap), pl.BlockSpec((tq, QK_NOPE), kmap(0)),
                  pl.BlockSpec((tq, LANES), lambda h, qi, ki: (jnp.minimum(ki, qi), 0)),
                  pl.BlockSpec((tq, V_HEAD), kmap(1)),
                  pl.BlockSpec((tq, V_HEAD), qmap), pl.BlockSpec((tq, V_HEAD), qmap), pl.BlockSpec((1, tq, 1), smap)],
        out_specs=[pl.BlockSpec((tq, HEAD_PAD), qmap), pl.BlockSpec((1, tq, 1), smap)],
        out_shape=[jax.ShapeDtypeStruct((t, heads * HEAD_PAD), F32), jax.ShapeDtypeStruct((heads, t, 1), F32)],
        scratch_shapes=[pltpu.VMEM((tq, HEAD_PAD), F32)],
        compiler_params=_params(("parallel", "parallel", "arbitrary")), name="attn_bwd_q")(q_c, kv, kpe, kv, o, do, lse)


def _attn_bwd_kv(q_c, kv, kpe, do, lse_row, delta_row):
    t = q_c.shape[0]
    heads = q_c.shape[1] // HEAD_PAD
    tk = _pick(t, ATTN_TILE)
    nk = t // tk
    scale = _attn_scale()

    def body(q_ref, kn_ref, kpe_ref, v_ref, do_ref, lse_ref, delta_ref, dkv_ref, dkpe_ref, dk_sc, dv_sc):
        ki, qi = pl.program_id(1), pl.program_id(2)

        @pl.when(qi == 0)
        def _():
            dk_sc[...] = jnp.zeros_like(dk_sc)
            dv_sc[...] = jnp.zeros_like(dv_sc)

        @pl.when(qi >= ki)
        def _():
            kc = jnp.concatenate([kn_ref[...], kpe_ref[...]], axis=1)
            q = q_ref[...]
            st = lax.dot_general(kc, q, _NT, preferred_element_type=F32) * scale
            krow = lax.broadcasted_iota(jnp.int32, st.shape, 0) + ki * tk
            qcol = lax.broadcasted_iota(jnp.int32, st.shape, 1) + qi * tk
            pt = jnp.where(qcol >= krow, jnp.exp(st - lse_ref[0]), 0.0)
            do_b = do_ref[...].astype(BF16)
            dv_sc[...] += jnp.dot(pt.astype(BF16), do_b, preferred_element_type=F32)
            dpt = lax.dot_general(v_ref[...], do_b, _NT, preferred_element_type=F32)
            dst = (pt * (dpt - delta_ref[0]) * scale).astype(BF16)
            dk_sc[...] += jnp.dot(dst, q, preferred_element_type=F32)

        @pl.when(qi == nk - 1)
        def _():
            dkv_ref[:, 0:QK_NOPE] = dk_sc[:, 0:QK_NOPE].astype(BF16)
            dkv_ref[:, QK_NOPE:QK_NOPE + V_HEAD] = dv_sc[...].astype(BF16)
            dkpe_ref[0] = dk_sc[:, QK_NOPE:QK_NOPE + LANES]

    qclamp = lambda h, ki, qi: (jnp.maximum(qi, ki), h)
    kmap = lambda blk: (lambda h, ki, qi: (ki, 2 * h + blk))
    rmap = lambda h, ki, qi: (h, 0, jnp.maximum(qi, ki))
    return pl.pallas_call(
        body, grid=(heads, nk, nk),
        in_specs=[pl.BlockSpec((tk, HEAD_PAD), qclamp), pl.BlockSpec((tk, QK_NOPE), kmap(0)),
                  pl.BlockSpec((tk, LANES), lambda h, ki, qi: (ki, 0)), pl.BlockSpec((tk, V_HEAD), kmap(1)),
                  pl.BlockSpec((tk, V_HEAD), qclamp), pl.BlockSpec((1, 1, tk), rmap), pl.BlockSpec((1, 1, tk), rmap)],
        out_specs=[pl.BlockSpec((tk, HEAD_PAD), lambda h, ki, qi: (ki, h)),
                   pl.BlockSpec((1, tk, LANES), lambda h, ki, qi: (h, ki, 0))],
        out_shape=[jax.ShapeDtypeStruct((t, heads * HEAD_PAD), BF16), jax.ShapeDtypeStruct((heads, t, LANES), F32)],
        scratch_shapes=[pltpu.VMEM((tk, HEAD_PAD), F32), pltpu.VMEM((tk, V_HEAD), F32)],
        compiler_params=_params(("parallel", "parallel", "arbitrary")), name="attn_bwd_kv")(
            q_c, kv, kpe, kv, do, lse_row, delta_row)


def _local_step(x, pos_col, target, small, big):
    d = x.shape[1]
    ql, kvl = small["q_norm_g"].shape[1], small["kv_norm_g"].shape[1]
    half = QK_ROPE // 2
    lane = jnp.arange(LANES)
    inv_freq = ROPE_THETA ** (-jnp.arange(0, QK_ROPE, 2, dtype=F32) / QK_ROPE)
    inv_row = inv_freq[lane % half][None, :]
    sign_row = jnp.where((lane % QK_ROPE) < half, -1.0, 1.0).astype(F32)[None, :]
    cos, sin = _rope_tables(pos_col, inv_row, sign_row)
    ws = small["w_sgu"]
    b_col = small["b_sgu_col"]

    a = _norm_fwd(x, small["norm_mix_g"], "norm_mix_fwd")
    z_lat = _mm(a, big["w_lat"], name="z_lat")
    z_uv = _mm(a, big["w_uv"], name="z_uv")
    z_g = _mm(a, big["w_g"], name="z_g")
    qn, kvn, kpe = _lat_fwd(z_lat, small["q_norm_g"], small["kv_norm_g"], cos, sin, ql, kvl)
    q_p = _mm(qn, big["w_uq"], name="q_up")
    kv = _mm(kvn, big["w_ukv"], out_dtype=BF16, name="kv_up")
    q_c = _q_rope(q_p, cos, sin, False, "q_rope")
    attn, lse = _attn_fwd(q_c, kv, kpe)
    y_attn = _mm(attn, big["w_o_attn"], name="y_attn")
    s_out = _sgu_fwd(z_uv, small["sgu_norm_g"], ws, b_col)
    y_sgu = _mm(s_out, big["w_o_sgu"], name="y_sgu")
    merged = _merge_fwd(y_attn, y_sgu, z_g, small["b_gate"])
    h1 = _mm(merged, big["w_out"], add=x, name="h1")
    f = _norm_fwd(h1, small["norm_ffn_g"], "norm_ffn_fwd")
    gate = _mm(f, big["w_gate"], name="ffn_gate")
    up = _mm(f, big["w_up"], name="ffn_up")
    act = _swiglu_fwd(gate, up)
    h2 = _mm(act, big["w_down"], add=h1, name="h2")
    loss_row, dh2, d_norm_final = _loss_head(h2, small["norm_final_g"], target)

    gb = {}
    gb["w_down"] = _mm(act, dh2, ta=True, out_dtype=BF16, name="dw_down")
    dact = _mm(dh2, big["w_down"], tb=True, name="dact")
    dgate, dup = _swiglu_bwd(gate, up, dact)
    gb["w_gate"] = _mm(f, dgate, ta=True, out_dtype=BF16, name="dw_gate")
    gb["w_up"] = _mm(f, dup, ta=True, out_dtype=BF16, name="dw_up")
    df = _mm(dgate, big["w_gate"], tb=True, name="df_gate")
    df = _mm(dup, big["w_up"], tb=True, add=df, name="df_up")
    dh1, d_norm_ffn = _norm_bwd(h1, small["norm_ffn_g"], df, dh2, "norm_ffn_bwd")
    gb["w_out"] = _mm(merged, dh1, ta=True, out_dtype=BF16, name="dw_out")
    dmerged = _mm(dh1, big["w_out"], tb=True, name="dmerged")
    dy_attn, dy_sgu, dz_g, d_b_gate = _merge_bwd(dmerged, y_attn, y_sgu, z_g, small["b_gate"])
    gb["w_o_sgu"] = _mm(s_out, dy_sgu, ta=True, out_dtype=BF16, name="dw_o_sgu")
    ds_out = _mm(dy_sgu, big["w_o_sgu"], tb=True, name="ds_out")
    dz_uv, d_ws, d_b_col, d_sgu_norm = _sgu_bwd(z_uv, ds_out, small["sgu_norm_g"], ws, b_col)
    gb["w_o_attn"] = _mm(attn, dy_attn, ta=True, out_dtype=BF16, name="dw_o_attn")
    dattn = _mm(dy_attn, big["w_o_attn"], tb=True, name="dattn")
    dq_c, delta = _attn_bwd_q(q_c, kv, kpe, attn, dattn, lse)
    heads, t = lse.shape[0], lse.shape[1]
    dkv, dkpe_heads = _attn_bwd_kv(q_c, kv, kpe, dattn, lse.reshape(heads, 1, t), delta.reshape(heads, 1, t))
    dq_p = _q_rope(dq_c, cos, sin, True, "q_rope_bwd")
    gb["w_uq"] = _mm(qn, dq_p, ta=True, out_dtype=BF16, name="dw_uq")
    gb["w_ukv"] = _mm(kvn, dkv, ta=True, out_dtype=BF16, name="dw_ukv")
    dqn = _mm(dq_p, big["w_uq"], tb=True, name="dqn")
    dkvn = _mm(dkv, big["w_ukv"], tb=True, name="dkvn")
    dz_lat, d_q_norm, d_kv_norm = _lat_bwd(z_lat, small["q_norm_g"], small["kv_norm_g"], dqn, dkvn, dkpe_heads,
                                           cos, sin, ql, kvl)
    gb["w_lat"] = _mm(a, dz_lat, ta=True, out_dtype=BF16, name="dw_lat")
    gb["w_uv"] = _mm(a, dz_uv, ta=True, out_dtype=BF16, name="dw_uv")
    gb["w_g"] = _mm(a, dz_g, ta=True, out_dtype=BF16, name="dw_g")
    da = _mm(dz_lat, big["w_lat"], tb=True, name="da_lat")
    da = _mm(dz_uv, big["w_uv"], tb=True, add=da, name="da_uv")
    da = _mm(dz_g, big["w_g"], tb=True, add=da, name="da_g")
    grad_x, d_norm_mix = _norm_bwd(x, small["norm_mix_g"], da, dh1, "norm_mix_bwd")

    gs = {"norm_mix_g": d_norm_mix, "b_gate": d_b_gate, "q_norm_g": d_q_norm, "kv_norm_g": d_kv_norm,
          "sgu_norm_g": d_sgu_norm, "w_sgu": d_ws, "b_sgu_col": d_b_col, "norm_ffn_g": d_norm_ffn,
          "norm_final_g": d_norm_final}
    return loss_row, grad_x, gs, gb


def _my_place():
    return lax.axis_index("x"), lax.axis_index("y"), lax.axis_index("c")


def _all_gather(shards):
    n = len(shards)

    def body(*refs):
        ins, outs = refs[:n], refs[n:2 * n]
        send_sems, recv_sems, local_sems = refs[2 * n:]
        x, y, c = _my_place()
        me, sibling = (x, y, c), (x, y, 1 - c)
        chips = [(1 - x, y), (x, 1 - y), (1 - x, 1 - y)]

        def slab(w, place):
            return outs[w].at[4 * place[0] + 2 * place[1] + place[2]]

        def copy(w, k, place, to, src=None):
            return pltpu.make_async_remote_copy(
                src_ref=slab(w, place) if src is None else src, dst_ref=slab(w, place),
                send_sem=send_sems.at[w, k], recv_sem=recv_sems.at[w, k], device_id=to, device_id_type=MESH)

        mine = [pltpu.make_async_copy(ins[w], slab(w, me), local_sems.at[w]) for w in range(n)]
        for cp in mine:
            cp.start()
        started = []
        for w in range(n):
            first = [copy(w, 0, me, sibling, src=ins[w])]
            first += [copy(w, 1 + j, me, (*chip, c), src=ins[w]) for j, chip in enumerate(chips)]
            for cp in first:
                cp.start()
            started += first
        for w in range(n):
            for j, chip in enumerate(chips):
                copy(w, 1 + j, (*chip, c), me).wait_recv()
                fwd = copy(w, 4 + j, (*chip, c), sibling)
                fwd.start()
                started.append(fwd)
        for w in range(n):
            copy(w, 0, sibling, me).wait_recv()
            for j, chip in enumerate(chips):
                copy(w, 4 + j, (*chip, 1 - c), me).wait_recv()
        for cp in started:
            cp.wait_send()
        for cp in mine:
            cp.wait()

    any_spec = pl.BlockSpec(memory_space=pl.ANY)
    return pl.pallas_call(
        body, in_specs=[any_spec] * n, out_specs=[any_spec] * n,
        out_shape=[jax.ShapeDtypeStruct((N_DEV,) + s.shape, s.dtype) for s in shards],
        scratch_shapes=[pltpu.SemaphoreType.DMA((n, 7)), pltpu.SemaphoreType.DMA((n, 7)), pltpu.SemaphoreType.DMA((n,))],
        compiler_params=pltpu.CompilerParams(has_side_effects=True), name="all_gather_weights")(*shards)


def _exchange_slabs(grads):
    n = len(grads)

    def body(*refs):
        ins, outs = refs[:n], refs[n:2 * n]
        send_sems, recv_sems, local_sems = refs[2 * n:]
        x, y, c = _my_place()
        me = 4 * x + 2 * y + c
        mine = [pltpu.make_async_copy(ins[w].at[me], outs[w].at[me], local_sems.at[w]) for w in range(n)]
        for cp in mine:
            cp.start()

        def copy(w, k):
            px, py, pc = x ^ ((k >> 2) & 1), y ^ ((k >> 1) & 1), c ^ (k & 1)
            return pltpu.make_async_remote_copy(
                src_ref=ins[w].at[4 * px + 2 * py + pc], dst_ref=outs[w].at[me],
                send_sem=send_sems.at[w, k - 1], recv_sem=recv_sems.at[w, k - 1],
                device_id=(px, py, pc), device_id_type=MESH)

        copies = [copy(w, k) for w in range(n) for k in range(1, N_DEV)]
        for cp in copies:
            cp.start()
        for cp in copies:
            cp.wait_recv()
        for cp in copies:
            cp.wait_send()
        for cp in mine:
            cp.wait()

    any_spec = pl.BlockSpec(memory_space=pl.ANY)
    return pl.pallas_call(
        body, in_specs=[any_spec] * n, out_specs=[any_spec] * n,
        out_shape=[jax.ShapeDtypeStruct(g.shape, g.dtype) for g in grads],
        scratch_shapes=[pltpu.SemaphoreType.DMA((n, 7)), pltpu.SemaphoreType.DMA((n, 7)), pltpu.SemaphoreType.DMA((n,))],
        compiler_params=pltpu.CompilerParams(has_side_effects=True), name="exchange_grad_slabs")(*grads)


def _all_reduce_pack(pack):
    r = pack.shape[0]

    def body(x_ref, out_ref, gath_ref, send_sems, recv_sems, local_sem):
        x, y, c = _my_place()
        me, sibling = (x, y, c), (x, y, 1 - c)
        chips = [(1 - x, y), (x, 1 - y), (1 - x, 1 - y)]

        def slab(place):
            return gath_ref.at[4 * place[0] + 2 * place[1] + place[2]]

        def copy(k, place, to, src=None):
            return pltpu.make_async_remote_copy(
                src_ref=slab(place) if src is None else src, dst_ref=slab(place),
                send_sem=send_sems.at[k], recv_sem=recv_sems.at[k], device_id=to, device_id_type=MESH)

        mine = pltpu.make_async_copy(x_ref, slab(me), local_sem)
        mine.start()
        first = [copy(0, me, sibling, src=x_ref)]
        first += [copy(1 + j, me, (*chip, c), src=x_ref) for j, chip in enumerate(chips)]
        for cp in first:
            cp.start()
        passed = [copy(4 + j, (*chip, c), sibling) for j, chip in enumerate(chips)]
        for j, chip in enumerate(chips):
            copy(1 + j, (*chip, c), me).wait_recv()
            passed[j].start()
        copy(0, sibling, me).wait_recv()
        for j, chip in enumerate(chips):
            copy(4 + j, (*chip, 1 - c), me).wait_recv()
        for cp in first + passed:
            cp.wait_send()
        mine.wait()
        acc = gath_ref[0]
        for i in range(1, N_DEV):
            acc = acc + gath_ref[i]
        out_ref[...] = acc

    vmem = pl.BlockSpec(memory_space=pltpu.VMEM)
    return pl.pallas_call(
        body, in_specs=[vmem], out_specs=vmem, out_shape=jax.ShapeDtypeStruct(pack.shape, F32),
        scratch_shapes=[pltpu.VMEM((N_DEV, r, LANES), F32), pltpu.SemaphoreType.DMA((7,)),
                        pltpu.SemaphoreType.DMA((7,)), pltpu.SemaphoreType.DMA],
        compiler_params=pltpu.CompilerParams(vmem_limit_bytes=VMEM_LIMIT), name="all_reduce_small")(pack)


def _adamw_math(w, g, m, v):
    m = ADAM_B1 * m + (1.0 - ADAM_B1) * g
    v = ADAM_B2 * v + (1.0 - ADAM_B2) * (g * g)
    m_hat = m / (1.0 - ADAM_B1 ** ADAM_STEP)
    v_hat = v / (1.0 - ADAM_B2 ** ADAM_STEP)
    delta = -ADAM_LR * (m_hat / (jnp.sqrt(v_hat) + ADAM_EPS) + ADAM_WD * w)
    return delta, m, v


def _adamw_shard(parts, w, m, v, name):
    r, c = w.shape
    tr = _pick(r, max(2 * SUBLANES, (256 * 1024) // c), 2 * SUBLANES)

    def body(p_ref, w_ref, m_ref, v_ref, g_ref, d_ref, nm_ref, nv_ref):
        g = p_ref[0].astype(F32)
        for i in range(1, N_DEV):
            g = g + p_ref[i].astype(F32)
        g_ref[...] = g
        d_ref[...], nm_ref[...], nv_ref[...] = _adamw_math(w_ref[...], g, m_ref[...], v_ref[...])

    spec = pl.BlockSpec((tr, c), lambda i: (i, 0))
    return pl.pallas_call(
        body, grid=(r // tr,), in_specs=[pl.BlockSpec((N_DEV, tr, c), lambda i: (0, i, 0)), spec, spec, spec],
        out_specs=[spec] * 4, out_shape=[jax.ShapeDtypeStruct((r, c), F32)] * 4,
        compiler_params=_params(("parallel",)), name=name)(parts, w, m, v)


def _adamw_pack(g, w, m, v):
    r, c = w.shape

    def body(g_ref, w_ref, m_ref, v_ref, d_ref, nm_ref, nv_ref):
        d_ref[...], nm_ref[...], nv_ref[...] = _adamw_math(w_ref[...], g_ref[...], m_ref[...], v_ref[...])

    return pl.pallas_call(
        body, in_specs=[_full((r, c))] * 4, out_specs=[_full((r, c))] * 3, grid=(1,),
        out_shape=[jax.ShapeDtypeStruct((r, c), F32)] * 3,
        compiler_params=_params(("arbitrary",)), name="adamw_small")(g, w, m, v)


def _cols_from_slabs(g):
    return jnp.transpose(g, (1, 0, 2)).reshape(g.shape[1], N_DEV * g.shape[2])


def _slabs_from_cols(w):
    r, c8 = w.shape
    return jnp.transpose(w.reshape(r, N_DEV, c8 // N_DEV), (1, 0, 2))


def _rows_from_slabs(g):
    return g.reshape(N_DEV * g.shape[1], g.shape[2])


def _slabs_from_rows(w):
    return w.reshape(N_DEV, w.shape[0] // N_DEV, w.shape[1])


def _compute_layout(gathered, ql, kvl, heads, sw):
    lat = ql + kvl + QK_ROPE
    w_in_full = _cols_from_slabs(gathered["w_in"])
    w_uq_heads = _cols_from_slabs(gathered["w_uq"]).reshape(ql, heads, QK_NOPE + QK_ROPE)
    return {
        "w_lat": jnp.pad(w_in_full[:, :lat], ((0, 0), (0, LANES - QK_ROPE))),
        "w_uv": w_in_full[:, lat:lat + 2 * sw],
        "w_g": w_in_full[:, lat + 2 * sw:],
        "w_uq": jnp.pad(w_uq_heads, ((0, 0), (0, 0), (0, HEAD_PAD - QK_NOPE - QK_ROPE))).reshape(ql, heads * HEAD_PAD),
        "w_ukv": _cols_from_slabs(gathered["w_ukv"]),
        "w_o_attn": _rows_from_slabs(gathered["w_o_attn"]),
        "w_o_sgu": _cols_from_slabs(gathered["w_o_sgu"]),
        "w_out": _rows_from_slabs(gathered["w_out"]),
        "w_gate": _cols_from_slabs(gathered["w_gate_ffn"]),
        "w_up": _cols_from_slabs(gathered["w_up_ffn"]),
        "w_down": _rows_from_slabs(gathered["w_down_ffn"]),
    }


def _slab_layout(gb, ql, kvl, heads):
    lat = ql + kvl + QK_ROPE
    d_w_in = jnp.concatenate([gb["w_lat"][:, :lat], gb["w_uv"], gb["w_g"]], axis=1)
    d_w_uq = gb["w_uq"].reshape(ql, heads, HEAD_PAD)[:, :, :QK_NOPE + QK_ROPE].reshape(ql, heads * (QK_NOPE + QK_ROPE))
    return {
        "w_in": _slabs_from_cols(d_w_in), "w_uq": _slabs_from_cols(d_w_uq), "w_ukv": _slabs_from_cols(gb["w_ukv"]),
        "w_o_attn": _slabs_from_rows(gb["w_o_attn"]), "w_o_sgu": _slabs_from_cols(gb["w_o_sgu"]),
        "w_out": _slabs_from_rows(gb["w_out"]), "w_gate_ffn": _slabs_from_cols(gb["w_gate"]),
        "w_up_ffn": _slabs_from_cols(gb["w_up"]), "w_down_ffn": _slabs_from_rows(gb["w_down"]),
    }


_SMALL = ["norm_mix_g", "b_gate", "q_norm_g", "kv_norm_g", "sgu_norm_g", "w_sgu", "b_sgu", "norm_ffn_g", "norm_final_g"]
_BIG = ["w_in", "w_uq", "w_ukv", "w_o_attn", "w_o_sgu", "w_out", "w_gate_ffn", "w_up_ffn", "w_down_ffn"]
_ORDER = ["norm_mix_g", "w_in", "b_gate", "q_norm_g", "w_uq", "kv_norm_g", "w_ukv", "w_o_attn", "sgu_norm_g", "w_sgu",
          "b_sgu", "w_o_sgu", "w_out", "norm_ffn_g", "w_gate_ffn", "w_up_ffn", "w_down_ffn", "norm_final_g"]


def _pack_rows(parts):
    rows, sizes = [], []
    for p in parts:
        flat = p.reshape(-1)
        n = flat.shape[0]
        padded = -(-n // (SUBLANES * LANES)) * (SUBLANES * LANES)
        rows.append(jnp.pad(flat, (0, padded - n)).reshape(padded // LANES, LANES))
        sizes.append((n, padded // LANES))
    return jnp.concatenate(rows, axis=0), sizes


def _unpack_rows(pack, sizes, shapes):
    out, r0 = [], 0
    for (n, nr), shp in zip(sizes, shapes):
        out.append(pack[r0:r0 + nr].reshape(-1)[:n].reshape(shp))
        r0 += nr
    return out


def kernel(x, positions, norm_mix_g, w_in, b_gate, q_norm_g, w_uq, kv_norm_g, w_ukv, w_o_attn, sgu_norm_g, w_sgu, b_sgu, w_o_sgu, w_out, norm_ffn_g, w_gate_ffn, w_up_ffn, w_down_ffn, norm_final_g, loss_target, m_norm_mix_g, m_w_in, m_b_gate, m_q_norm_g, m_w_uq, m_kv_norm_g, m_w_ukv, m_w_o_attn, m_sgu_norm_g, m_w_sgu, m_b_sgu, m_w_o_sgu, m_w_out, m_norm_ffn_g, m_w_gate_ffn, m_w_up_ffn, m_w_down_ffn, m_norm_final_g, v_norm_mix_g, v_w_in, v_b_gate, v_q_norm_g, v_w_uq, v_kv_norm_g, v_w_ukv, v_w_o_attn, v_sgu_norm_g, v_w_sgu, v_b_sgu, v_w_o_sgu, v_w_out, v_norm_ffn_g, v_w_gate_ffn, v_w_up_ffn, v_w_down_ffn, v_norm_final_g):
    wts = dict(norm_mix_g=norm_mix_g, w_in=w_in, b_gate=b_gate, q_norm_g=q_norm_g, w_uq=w_uq, kv_norm_g=kv_norm_g,
               w_ukv=w_ukv, w_o_attn=w_o_attn, sgu_norm_g=sgu_norm_g, w_sgu=w_sgu, b_sgu=b_sgu, w_o_sgu=w_o_sgu,
               w_out=w_out, norm_ffn_g=norm_ffn_g, w_gate_ffn=w_gate_ffn, w_up_ffn=w_up_ffn, w_down_ffn=w_down_ffn,
               norm_final_g=norm_final_g)
    mom = dict(norm_mix_g=m_norm_mix_g, w_in=m_w_in, b_gate=m_b_gate, q_norm_g=m_q_norm_g, w_uq=m_w_uq,
               kv_norm_g=m_kv_norm_g, w_ukv=m_w_ukv, w_o_attn=m_w_o_attn, sgu_norm_g=m_sgu_norm_g, w_sgu=m_w_sgu,
               b_sgu=m_b_sgu, w_o_sgu=m_w_o_sgu, w_out=m_w_out, norm_ffn_g=m_norm_ffn_g, w_gate_ffn=m_w_gate_ffn,
               w_up_ffn=m_w_up_ffn, w_down_ffn=m_w_down_ffn, norm_final_g=m_norm_final_g)
    var = dict(norm_mix_g=v_norm_mix_g, w_in=v_w_in, b_gate=v_b_gate, q_norm_g=v_q_norm_g, w_uq=v_w_uq,
               kv_norm_g=v_kv_norm_g, w_ukv=v_w_ukv, w_o_attn=v_w_o_attn, sgu_norm_g=v_sgu_norm_g, w_sgu=v_w_sgu,
               b_sgu=v_b_sgu, w_o_sgu=v_w_o_sgu, w_out=v_w_out, norm_ffn_g=v_norm_ffn_g, w_gate_ffn=v_w_gate_ffn,
               w_up_ffn=v_w_up_ffn, w_down_ffn=v_w_down_ffn, norm_final_g=v_norm_final_g)

    t, d = x.shape[1], x.shape[2]
    ql, kvl = q_norm_g.shape[1], kv_norm_g.shape[1]
    heads = (w_uq.shape[2] * N_DEV) // (QK_NOPE + QK_ROPE)
    sw = sgu_norm_g.shape[1]

    gathered = dict(zip(_BIG, _all_gather([wts[k][0].astype(BF16) for k in _BIG])))
    big = _compute_layout(gathered, ql, kvl, heads, sw)
    small = {
        "norm_mix_g": norm_mix_g, "b_gate": b_gate, "q_norm_g": q_norm_g, "kv_norm_g": kv_norm_g,
        "sgu_norm_g": sgu_norm_g, "w_sgu": w_sgu[0], "b_sgu_col": b_sgu[0][:, :, None], "norm_ffn_g": norm_ffn_g,
        "norm_final_g": norm_final_g[None, :],
    }

    loss_row, grad_x, gs, gb = _local_step(x[0], positions.reshape(t, 1), loss_target[0], small, big)

    slabs = _slab_layout(gb, ql, kvl, heads)
    parts = dict(zip(_BIG, _exchange_slabs([slabs[k] for k in _BIG])))

    grads, deltas, new_m, new_v = {}, {}, {}, {}
    for k in _BIG:
        shp = wts[k].shape
        g, dl, nm, nv = _adamw_shard(parts[k], wts[k][0], mom[k][0], var[k][0], "adamw_" + k)
        grads[k], deltas[k], new_m[k], new_v[k] = (a.reshape(shp) for a in (g, dl, nm, nv))

    small_grads = [gs["norm_mix_g"], gs["b_gate"], gs["q_norm_g"], gs["kv_norm_g"], gs["sgu_norm_g"], gs["w_sgu"],
                   gs["b_sgu_col"], gs["norm_ffn_g"], gs["norm_final_g"]]
    pack, sizes = _pack_rows([loss_row] + small_grads)
    total = _all_reduce_pack(pack)
    shapes = [(1, LANES)] + [wts[k].shape for k in _SMALL]
    unpacked = _unpack_rows(total, sizes, shapes)
    loss = unpacked[0][0, 0]
    for k, g in zip(_SMALL, unpacked[1:]):
        grads[k] = g
    g_pack = total[sizes[0][1]:]
    w_pack, _ = _pack_rows([wts[k] for k in _SMALL])
    m_pack, _ = _pack_rows([mom[k] for k in _SMALL])
    v_pack, _ = _pack_rows([var[k] for k in _SMALL])
    d_pack, nm_pack, nv_pack = _adamw_pack(g_pack, w_pack, m_pack, v_pack)
    small_shapes = [wts[k].shape for k in _SMALL]
    for store, pk in ((deltas, d_pack), (new_m, nm_pack), (new_v, nv_pack)):
        for k, a in zip(_SMALL, _unpack_rows(pk, sizes[1:], small_shapes)):
            store[k] = a

    return (loss, grad_x[None], *[grads[k] for k in _ORDER], *[deltas[k] for k in _ORDER],
            *[new_m[k] for k in _ORDER], *[new_v[k] for k in _ORDER])
```

```python
import functools
import math

import jax
import jax.numpy as jnp
from jax import lax
from jax.experimental import pallas as pl
from jax.experimental.pallas import tpu as pltpu

F32 = jnp.float32
BF16 = jnp.bfloat16

N_DEV = 8
N_HEADS = 16
QK_NOPE = 128
QK_ROPE = 64
V_HEAD = 128
HEAD_PAD = 256
ROPE_THETA = 10000.0
CHUNK = 128
SGU_GROUP = 128
RMS_EPS = 1e-6
LANES = 128
SUBLANES = 8

ADAM_LR = 0.001
ADAM_B1 = 0.9
ADAM_B2 = 0.999
ADAM_EPS = 1e-08
ADAM_WD = 0.01
ADAM_STEP = 10

VMEM_LIMIT = 48 * 1024 * 1024
MM_TILE = (1024, 512, 2048)
ATTN_TILE = 512
ROW_KERNEL_BYTES = 24 * 1024 * 1024
NEG_BIG = -1e30
MESH = pl.DeviceIdType.MESH


def _pick(n, target, mult=LANES):
    best = None
    d = mult
    while d <= min(n, target):
        if n % d == 0:
            best = d
        d += mult
    return best or n


def _row_tile(t, width, n_blocks, mult=2 * SUBLANES):
    return _pick(t, max(mult, ROW_KERNEL_BYTES // (3 * n_blocks * width * 4)), mult)


def _params(sem):
    return pltpu.CompilerParams(dimension_semantics=sem, vmem_limit_bytes=VMEM_LIMIT)


def _full(shape):
    nd = len(shape)
    return pl.BlockSpec(shape, lambda *_: (0,) * nd)


def _rows(tr, w, cb=0):
    return pl.BlockSpec((tr, w), lambda i: (i, cb))


class _Comm:
    def __init__(self, ins, out_shapes, sems, start, finish):
        self.ins, self.out_shapes, self.sems, self.start, self.finish = list(ins), list(out_shapes), list(sems), start, finish


def _call(body, *, grid, in_specs, out_specs, out_shape, scratch_shapes=(), sem, name, args, comm=None):
    if comm is None:
        outs = pl.pallas_call(body, grid=grid, in_specs=list(in_specs), out_specs=list(out_specs),
                              out_shape=list(out_shape), scratch_shapes=list(scratch_shapes),
                              compiler_params=_params(sem), name=name)(*args)
        return list(outs), []
    n_in, n_out, n_sc = len(in_specs), len(out_shape), len(scratch_shapes)
    nci, nco = len(comm.ins), len(comm.out_shapes)

    def hosted(*refs):
        ins, refs = refs[:n_in], refs[n_in:]
        cins, refs = refs[:nci], refs[nci:]
        outs, refs = refs[:n_out], refs[n_out:]
        couts, refs = refs[:nco], refs[nco:]
        scratch, csems = refs[:n_sc], refs[n_sc:]
        ids = [pl.program_id(i) for i in range(len(grid))]
        first = functools.reduce(jnp.logical_and, [i == 0 for i in ids])
        last = functools.reduce(jnp.logical_and, [i == g - 1 for i, g in zip(ids, grid)])

        @pl.when(first)
        def _():
            comm.start(cins, couts, csems)

        body(*ins, *outs, *scratch)

        @pl.when(last)
        def _():
            comm.finish(cins, couts, csems)

    any_spec = pl.BlockSpec(memory_space=pl.ANY)
    res = pl.pallas_call(
        hosted, grid=grid, in_specs=list(in_specs) + [any_spec] * nci, out_specs=list(out_specs) + [any_spec] * nco,
        out_shape=list(out_shape) + comm.out_shapes, scratch_shapes=list(scratch_shapes) + comm.sems,
        compiler_params=pltpu.CompilerParams(dimension_semantics=("arbitrary",) * len(grid),
                                             vmem_limit_bytes=VMEM_LIMIT, has_side_effects=True),
        name=name)(*args, *comm.ins)
    return list(res[:n_out]), list(res[n_out:])


def _mm(a, b, *, ta=False, tb=False, add=None, out_dtype=F32, tm=None, tn=None, tk=None, name, comm=None):
    m, k = (a.shape[1], a.shape[0]) if ta else a.shape
    n = b.shape[0] if tb else b.shape[1]
    assert k == (b.shape[1] if tb else b.shape[0]), (a.shape, b.shape, ta, tb)
    tm, tn, tk = _pick(m, tm or MM_TILE[0]), _pick(n, tn or MM_TILE[1]), _pick(k, tk or MM_TILE[2])
    nk = k // tk
    dims = (((0 if ta else 1,), (1 if tb else 0,)), ((), ()))

    def body(*refs):
        if add is None:
            a_ref, b_ref, o_ref, acc_ref = refs
            add_ref = None
        else:
            a_ref, b_ref, add_ref, o_ref, acc_ref = refs
        kk = pl.program_id(2)

        @pl.when(kk == 0)
        def _():
            acc_ref[...] = jnp.zeros_like(acc_ref)

        acc_ref[...] += lax.dot_general(a_ref[...].astype(BF16), b_ref[...].astype(BF16), dims,
                                        preferred_element_type=F32)

        @pl.when(kk == nk - 1)
        def _():
            r = acc_ref[...]
            if add_ref is not None:
                r = r + add_ref[...].astype(F32)
            o_ref[...] = r.astype(o_ref.dtype)

    a_spec = pl.BlockSpec((tk, tm), lambda i, j, kk: (kk, i)) if ta else pl.BlockSpec((tm, tk), lambda i, j, kk: (i, kk))
    b_spec = pl.BlockSpec((tn, tk), lambda i, j, kk: (j, kk)) if tb else pl.BlockSpec((tk, tn), lambda i, j, kk: (kk, j))
    o_spec = pl.BlockSpec((tm, tn), lambda i, j, kk: (i, j))
    in_specs = [a_spec, b_spec] + ([o_spec] if add is not None else [])
    args = (a, b) + ((add,) if add is not None else ())
    outs, comm_outs = _call(
        body, grid=(m // tm, n // tn, nk), in_specs=in_specs, out_specs=[o_spec],
        out_shape=[jax.ShapeDtypeStruct((m, n), out_dtype)], scratch_shapes=[pltpu.VMEM((tm, tn), F32)],
        sem=("parallel", "parallel", "arbitrary"), name=name, args=args, comm=comm)
    return outs[0] if comm is None else (outs[0], comm_outs)


def _rms_scale(x):
    return lax.rsqrt(jnp.mean(x * x, axis=-1, keepdims=True) + RMS_EPS)


def _rms_bwd(xhat, r, g, dy):
    t = dy * g
    dx = r * (t - xhat * jnp.mean(t * xhat, axis=-1, keepdims=True))
    return dx, dy * xhat


_GELU_C = math.sqrt(2.0 / math.pi)


def _gelu(x):
    return x * (0.5 * (1.0 + jnp.tanh(_GELU_C * (x + 0.044715 * (x * x * x)))))


def _gelu_and_grad(x):
    t = jnp.tanh(_GELU_C * (x + 0.044715 * (x * x * x)))
    cdf = 0.5 * (1.0 + t)
    return x * cdf, cdf + x * (0.5 * (1.0 - t * t) * (_GELU_C * (1.0 + 3.0 * 0.044715 * (x * x))))


def _sigmoid(x):
    return 1.0 / (1.0 + jnp.exp(-x))


def _swap_halves(x):
    lane = lax.broadcasted_iota(jnp.int32, x.shape, 1)
    first = (lane % QK_ROPE) < (QK_ROPE // 2)
    return jnp.where(first, pltpu.roll(x, LANES - QK_ROPE // 2, 1), pltpu.roll(x, QK_ROPE // 2, 1))


def _rope(x, cos, sin_signed):
    return x * cos + _swap_halves(x) * sin_signed


def _rope_bwd(d, cos, sin_signed):
    return d * cos + _swap_halves(d * sin_signed)


def _rope_tables(pos_col, inv_freq_row, sign_row):
    t = pos_col.shape[0]
    tr = _pick(t, 512, SUBLANES)

    def body(p_ref, f_ref, s_ref, cos_ref, sin_ref):
        ang = p_ref[...].astype(F32) * f_ref[...]
        cos_ref[...] = jnp.cos(ang)
        sin_ref[...] = jnp.sin(ang) * s_ref[...]

    return pl.pallas_call(
        body, grid=(t // tr,), in_specs=[_rows(tr, 1), _full((1, LANES)), _full((1, LANES))],
        out_specs=[_rows(tr, LANES), _rows(tr, LANES)],
        out_shape=[jax.ShapeDtypeStruct((t, LANES), F32)] * 2,
        compiler_params=_params(("parallel",)), name="rope_tables")(pos_col, inv_freq_row, sign_row)


def _norm_fwd(x, g, name):
    t, d = x.shape
    tr = _row_tile(t, d, 2)

    def body(x_ref, g_ref, y_ref):
        xv = x_ref[...]
        y_ref[...] = (xv * _rms_scale(xv) * g_ref[...]).astype(BF16)

    return pl.pallas_call(
        body, grid=(t // tr,), in_specs=[_rows(tr, d), _full((1, d))], out_specs=_rows(tr, d),
        out_shape=jax.ShapeDtypeStruct((t, d), BF16), compiler_params=_params(("parallel",)), name=name)(x, g)


def _lat_fwd(z_lat, qg, kvg, cos, sin, ql, kvl):
    t = z_lat.shape[0]
    tr = _row_tile(t, z_lat.shape[1], 2)

    def body(z_ref, qg_ref, kvg_ref, cos_ref, sin_ref, qn_ref, kvn_ref, kpe_ref):
        q = z_ref[:, 0:ql]
        qn_ref[...] = (q * _rms_scale(q) * qg_ref[...]).astype(BF16)
        kv = z_ref[:, ql:ql + kvl]
        kvn_ref[...] = (kv * _rms_scale(kv) * kvg_ref[...]).astype(BF16)
        kpe_ref[...] = _rope(z_ref[:, ql + kvl:ql + kvl + LANES], cos_ref[...], sin_ref[...]).astype(BF16)

    w = z_lat.shape[1]
    return pl.pallas_call(
        body, grid=(t // tr,),
        in_specs=[_rows(tr, w), _full((1, ql)), _full((1, kvl)), _rows(tr, LANES), _rows(tr, LANES)],
        out_specs=[_rows(tr, ql), _rows(tr, kvl), _rows(tr, LANES)],
        out_shape=[jax.ShapeDtypeStruct((t, ql), BF16), jax.ShapeDtypeStruct((t, kvl), BF16),
                   jax.ShapeDtypeStruct((t, LANES), BF16)],
        compiler_params=_params(("parallel",)), name="lat_fwd")(z_lat, qg, kvg, cos, sin)


def _q_rope(q_p, cos, sin, bwd, name):
    t, w = q_p.shape
    tr = _row_tile(t, w, 2)
    fn = _rope_bwd if bwd else _rope

    def body(q_ref, cos_ref, sin_ref, o_ref):
        c, s = cos_ref[...], sin_ref[...]
        for h in range(w // HEAD_PAD):
            o_ref[:, h * HEAD_PAD:h * HEAD_PAD + QK_NOPE] = q_ref[:, h * HEAD_PAD:h * HEAD_PAD + QK_NOPE].astype(BF16)
            lo = h * HEAD_PAD + QK_NOPE
            o_ref[:, lo:lo + LANES] = fn(q_ref[:, lo:lo + LANES].astype(F32), c, s).astype(BF16)

    return pl.pallas_call(
        body, grid=(t // tr,), in_specs=[_rows(tr, w), _rows(tr, LANES), _rows(tr, LANES)], out_specs=_rows(tr, w),
        out_shape=jax.ShapeDtypeStruct((t, w), BF16), compiler_params=_params(("parallel",)), name=name)(q_p, cos, sin)


def _tril_mask():
    r = lax.broadcasted_iota(jnp.int32, (CHUNK, CHUNK), 0)
    c = lax.broadcasted_iota(jnp.int32, (CHUNK, CHUNK), 1)
    return r >= c


def _sgu_fwd(z_uv, gs, ws, b_col):
    t = z_uv.shape[0]
    sw = z_uv.shape[1] // 2
    groups = sw // SGU_GROUP
    tr = _pick(t, 256, CHUNK)

    def body(u_ref, v_ref, gs_ref, ws_ref, b_ref, o_ref):
        v = _gelu(v_ref[...])
        vn = (v * _rms_scale(v) * gs_ref[...]).astype(BF16)
        tri = _tril_mask()
        for g in range(groups):
            wg = jnp.where(tri, ws_ref[g], 0.0).astype(BF16)
            cols = slice(g * SGU_GROUP, (g + 1) * SGU_GROUP)
            for c in range(tr // CHUNK):
                rows = slice(c * CHUNK, (c + 1) * CHUNK)
                mixed = jnp.dot(wg, vn[rows, cols], preferred_element_type=F32) + b_ref[g]
                o_ref[rows, cols] = (_gelu(u_ref[rows, cols]) * mixed).astype(BF16)

    return pl.pallas_call(
        body, grid=(t // tr,),
        in_specs=[_rows(tr, sw, 0), _rows(tr, sw, 1), _full((1, sw)), _full(ws.shape), _full(b_col.shape)],
        out_specs=_rows(tr, sw), out_shape=jax.ShapeDtypeStruct((t, sw), BF16),
        compiler_params=_params(("parallel",)), name="sgu_fwd")(z_uv, z_uv, gs, ws, b_col)


def _merge_fwd(y_attn, y_sgu, z_g, b_gate):
    t, d = y_attn.shape
    tr = _row_tile(t, d, 5)

    def body(ya_ref, ys_ref, g0_ref, g1_ref, b0_ref, b1_ref, o_ref):
        g0 = _sigmoid(g0_ref[...] + b0_ref[...])
        g1 = _sigmoid(g1_ref[...] + b1_ref[...])
        o_ref[...] = (g0 * ya_ref[...] + g1 * ys_ref[...]).astype(BF16)

    bspec0 = pl.BlockSpec((1, d), lambda i: (0, 0))
    bspec1 = pl.BlockSpec((1, d), lambda i: (0, 1))
    return pl.pallas_call(
        body, grid=(t // tr,),
        in_specs=[_rows(tr, d), _rows(tr, d), _rows(tr, d, 0), _rows(tr, d, 1), bspec0, bspec1],
        out_specs=_rows(tr, d), out_shape=jax.ShapeDtypeStruct((t, d), BF16),
        compiler_params=_params(("parallel",)), name="merge_fwd")(y_attn, y_sgu, z_g, z_g, b_gate, b_gate)


def _swiglu_fwd(gate, up):
    t, f = gate.shape
    tr = _row_tile(t, f, 3)

    def body(g_ref, u_ref, o_ref):
        g = g_ref[...]
        o_ref[...] = (g * _sigmoid(g) * u_ref[...]).astype(BF16)

    return pl.pallas_call(
        body, grid=(t // tr,), in_specs=[_rows(tr, f), _rows(tr, f)], out_specs=_rows(tr, f),
        out_shape=jax.ShapeDtypeStruct((t, f), BF16), compiler_params=_params(("parallel",)), name="swiglu_fwd")(gate, up)


def _loss_head(h2, g, target):
    t, d = h2.shape
    tr = _row_tile(t, d, 3)

    def body(h_ref, g_ref, t_ref, loss_ref, dh_ref, dg_ref):
        @pl.when(pl.program_id(0) == 0)
        def _():
            loss_ref[...] = jnp.zeros_like(loss_ref)
            dg_ref[...] = jnp.zeros_like(dg_ref)

        h = h_ref[...]
        r = _rms_scale(h)
        hhat = h * r
        gv = g_ref[...]
        err = hhat * gv - t_ref[...]
        loss_ref[...] += jnp.full(loss_ref.shape, 0.5 * jnp.sum(jnp.mean(err * err, axis=-1)), F32)
        dx, dg_rows = _rms_bwd(hhat, r, gv, err * (1.0 / d))
        dh_ref[...] = dx
        dg_ref[...] += jnp.sum(dg_rows, axis=0, keepdims=True)

    return pl.pallas_call(
        body, grid=(t // tr,), in_specs=[_rows(tr, d), _full((1, d)), _rows(tr, d)],
        out_specs=[_full((1, LANES)), _rows(tr, d), _full((1, d))],
        out_shape=[jax.ShapeDtypeStruct((1, LANES), F32), jax.ShapeDtypeStruct((t, d), F32),
                   jax.ShapeDtypeStruct((1, d), F32)],
        compiler_params=_params(("arbitrary",)), name="loss_head")(h2, g, target)


def _swiglu_bwd(gate, up, dact):
    t, f = gate.shape
    tr = _row_tile(t, f, 4)

    def body(g_ref, u_ref, d_ref, dg_ref, du_ref):
        g = g_ref[...]
        s = _sigmoid(g)
        d = d_ref[...]
        dg_ref[...] = (d * u_ref[...] * (s * (1.0 + g * (1.0 - s)))).astype(BF16)
        du_ref[...] = (d * (g * s)).astype(BF16)

    return pl.pallas_call(
        body, grid=(t // tr,), in_specs=[_rows(tr, f)] * 3, out_specs=[_rows(tr, f)] * 2,
        out_shape=[jax.ShapeDtypeStruct((t, f), BF16)] * 2,
        compiler_params=_params(("parallel",)), name="swiglu_bwd")(gate, up, dact)


def _norm_bwd(x, g, dy, resid, name):
    t, d = x.shape
    tr = _row_tile(t, d, 4)

    def body(x_ref, g_ref, dy_ref, r_ref, dx_ref, dg_ref):
        @pl.when(pl.program_id(0) == 0)
        def _():
            dg_ref[...] = jnp.zeros_like(dg_ref)

        xv = x_ref[...]
        r = _rms_scale(xv)
        dx, dg_rows = _rms_bwd(xv * r, r, g_ref[...], dy_ref[...])
        dx_ref[...] = r_ref[...] + dx
        dg_ref[...] += jnp.sum(dg_rows, axis=0, keepdims=True)

    return pl.pallas_call(
        body, grid=(t // tr,), in_specs=[_rows(tr, d), _full((1, d)), _rows(tr, d), _rows(tr, d)],
        out_specs=[_rows(tr, d), _full((1, d))],
        out_shape=[jax.ShapeDtypeStruct((t, d), F32), jax.ShapeDtypeStruct((1, d), F32)],
        compiler_params=_params(("arbitrary",)), name=name)(x, g, dy, resid)


def _merge_bwd(dmerged, y_attn, y_sgu, z_g, b_gate):
    t, d = y_attn.shape
    tr = _row_tile(t, d, 7)

    def body(dm_ref, ya_ref, ys_ref, g0_ref, g1_ref, b0_ref, b1_ref, dya_ref, dys_ref, dz_ref, db_ref):
        @pl.when(pl.program_id(0) == 0)
        def _():
            db_ref[...] = jnp.zeros_like(db_ref)

        dm = dm_ref[...]
        g0 = _sigmoid(g0_ref[...] + b0_ref[...])
        g1 = _sigmoid(g1_ref[...] + b1_ref[...])
        dya_ref[...] = (dm * g0).astype(BF16)
        dys_ref[...] = (dm * g1).astype(BF16)
        dl0 = dm * ya_ref[...] * (g0 * (1.0 - g0))
        dl1 = dm * ys_ref[...] * (g1 * (1.0 - g1))
        dz_ref[:, 0:d] = dl0.astype(BF16)
        dz_ref[:, d:2 * d] = dl1.astype(BF16)
        db_ref[:, 0:d] += jnp.sum(dl0, axis=0, keepdims=True)
        db_ref[:, d:2 * d] += jnp.sum(dl1, axis=0, keepdims=True)

    bspec0 = pl.BlockSpec((1, d), lambda i: (0, 0))
    bspec1 = pl.BlockSpec((1, d), lambda i: (0, 1))
    return pl.pallas_call(
        body, grid=(t // tr,),
        in_specs=[_rows(tr, d), _rows(tr, d), _rows(tr, d), _rows(tr, d, 0), _rows(tr, d, 1), bspec0, bspec1],
        out_specs=[_rows(tr, d), _rows(tr, d), _rows(tr, 2 * d), _full((1, 2 * d))],
        out_shape=[jax.ShapeDtypeStruct((t, d), BF16), jax.ShapeDtypeStruct((t, d), BF16),
                   jax.ShapeDtypeStruct((t, 2 * d), BF16), jax.ShapeDtypeStruct((1, 2 * d), F32)],
        compiler_params=_params(("arbitrary",)), name="merge_bwd")(dmerged, y_attn, y_sgu, z_g, z_g, b_gate, b_gate)


def _sgu_bwd(z_uv, ds_out, gs, ws, b_col):
    t = z_uv.shape[0]
    sw = z_uv.shape[1] // 2
    groups = sw // SGU_GROUP
    tr = _pick(t, 256, CHUNK)

    def body(u_ref, v_ref, d_ref, gs_ref, ws_ref, b_ref, dz_ref, dws_ref, db_ref, dgs_ref, dvn_ref):
        @pl.when(pl.program_id(0) == 0)
        def _():
            dws_ref[...] = jnp.zeros_like(dws_ref)
            db_ref[...] = jnp.zeros_like(db_ref)
            dgs_ref[...] = jnp.zeros_like(dgs_ref)

        v, dgelu_v = _gelu_and_grad(v_ref[...])
        r = _rms_scale(v)
        vhat = v * r
        gsv = gs_ref[...]
        vn = (vhat * gsv).astype(BF16)
        tri = _tril_mask()
        for g in range(groups):
            wg = jnp.where(tri, ws_ref[g], 0.0).astype(BF16)
            cols = slice(g * SGU_GROUP, (g + 1) * SGU_GROUP)
            for c in range(tr // CHUNK):
                rows = slice(c * CHUNK, (c + 1) * CHUNK)
                vn_cg = vn[rows, cols]
                mixed = jnp.dot(wg, vn_cg, preferred_element_type=F32) + b_ref[g]
                u, dgelu_u = _gelu_and_grad(u_ref[rows, cols])
                dso = d_ref[rows, cols]
                dz_ref[rows, cols] = (dso * mixed * dgelu_u).astype(BF16)
                dmixed = dso * u
                db_ref[g] += jnp.sum(dmixed, axis=1, keepdims=True)
                dmixed_b = dmixed.astype(BF16)
                dws_ref[g] += jnp.where(
                    tri, lax.dot_general(dmixed_b, vn_cg, (((1,), (1,)), ((), ())), preferred_element_type=F32), 0.0)
                dvn_ref[rows, cols] = lax.dot_general(wg, dmixed_b, (((0,), (0,)), ((), ())), preferred_element_type=F32)
        dvn = dvn_ref[...]
        dv, dgs_rows = _rms_bwd(vhat, r, gsv, dvn)
        dz_ref[:, sw:2 * sw] = (dv * dgelu_v).astype(BF16)
        dgs_ref[...] += jnp.sum(dgs_rows, axis=0, keepdims=True)

    return pl.pallas_call(
        body, grid=(t // tr,),
        in_specs=[_rows(tr, sw, 0), _rows(tr, sw, 1), _rows(tr, sw), _full((1, sw)), _full(ws.shape), _full(b_col.shape)],
        out_specs=[_rows(tr, 2 * sw), _full(ws.shape), _full(b_col.shape), _full((1, sw))],
        out_shape=[jax.ShapeDtypeStruct((t, 2 * sw), BF16), jax.ShapeDtypeStruct(ws.shape, F32),
                   jax.ShapeDtypeStruct(b_col.shape, F32), jax.ShapeDtypeStruct((1, sw), F32)],
        scratch_shapes=[pltpu.VMEM((tr, sw), F32)],
        compiler_params=_params(("arbitrary",)), name="sgu_bwd")(z_uv, z_uv, ds_out, gs, ws, b_col)


def _lat_bwd(z_lat, qg, kvg, dqn, dkvn, dkpe_heads, cos, sin, ql, kvl):
    t, w = z_lat.shape
    heads = dkpe_heads.shape[0]
    tr = _row_tile(t, w + heads * LANES, 3)

    def body(z_ref, qg_ref, kvg_ref, dq_ref, dkv_ref, dk_ref, cos_ref, sin_ref, dz_ref, dqg_ref, dkvg_ref):
        @pl.when(pl.program_id(0) == 0)
        def _():
            dqg_ref[...] = jnp.zeros_like(dqg_ref)
            dkvg_ref[...] = jnp.zeros_like(dkvg_ref)

        q = z_ref[:, 0:ql]
        r = _rms_scale(q)
        dx, dg_rows = _rms_bwd(q * r, r, qg_ref[...], dq_ref[...])
        dz_ref[:, 0:ql] = dx.astype(BF16)
        dqg_ref[...] += jnp.sum(dg_rows, axis=0, keepdims=True)
        kv = z_ref[:, ql:ql + kvl]
        r = _rms_scale(kv)
        dx, dg_rows = _rms_bwd(kv * r, r, kvg_ref[...], dkv_ref[...])
        dz_ref[:, ql:ql + kvl] = dx.astype(BF16)
        dkvg_ref[...] += jnp.sum(dg_rows, axis=0, keepdims=True)
        dk = dk_ref[0]
        for h in range(1, heads):
            dk = dk + dk_ref[h]
        dz_ref[:, ql + kvl:ql + kvl + LANES] = _rope_bwd(dk, cos_ref[...], sin_ref[...]).astype(BF16)

    return pl.pallas_call(
        body, grid=(t // tr,),
        in_specs=[_rows(tr, w), _full((1, ql)), _full((1, kvl)), _rows(tr, ql), _rows(tr, kvl),
                  pl.BlockSpec((heads, tr, LANES), lambda i: (0, i, 0)), _rows(tr, LANES), _rows(tr, LANES)],
        out_specs=[_rows(tr, w), _full((1, ql)), _full((1, kvl))],
        out_shape=[jax.ShapeDtypeStruct((t, w), BF16), jax.ShapeDtypeStruct((1, ql), F32),
                   jax.ShapeDtypeStruct((1, kvl), F32)],
        compiler_params=_params(("arbitrary",)), name="lat_bwd")(z_lat, qg, kvg, dqn, dkvn, dkpe_heads, cos, sin)


_NT = (((1,), (1,)), ((), ()))


def _attn_scale():
    return (QK_NOPE + QK_ROPE) ** -0.5


def _attn_fwd(q_c, kv, kpe):
    t = q_c.shape[0]
    heads = q_c.shape[1] // HEAD_PAD
    tq = _pick(t, ATTN_TILE)
    nq = t // tq
    scale = _attn_scale()

    def body(q_ref, kn_ref, kpe_ref, v_ref, o_ref, lse_ref, m_sc, l_sc, acc_sc):
        qi, ki = pl.program_id(1), pl.program_id(2)

        @pl.when(ki == 0)
        def _():
            m_sc[...] = jnp.full_like(m_sc, NEG_BIG)
            l_sc[...] = jnp.zeros_like(l_sc)
            acc_sc[...] = jnp.zeros_like(acc_sc)

        @pl.when(ki <= qi)
        def _():
            kc = jnp.concatenate([kn_ref[...], kpe_ref[...]], axis=1)
            s = lax.dot_general(q_ref[...], kc, _NT, preferred_element_type=F32) * scale
            row = lax.broadcasted_iota(jnp.int32, s.shape, 0) + qi * tq
            col = lax.broadcasted_iota(jnp.int32, s.shape, 1) + ki * tq
            s = jnp.where(row >= col, s, NEG_BIG)
            m_prev = m_sc[...]
            m_new = jnp.maximum(m_prev, jnp.max(s, axis=1, keepdims=True))
            alpha = jnp.exp(m_prev - m_new)
            p = jnp.exp(s - m_new)
            l_sc[...] = alpha * l_sc[...] + jnp.sum(p, axis=1, keepdims=True)
            acc_sc[...] = alpha * acc_sc[...] + jnp.dot(p.astype(BF16), v_ref[...], preferred_element_type=F32)
            m_sc[...] = m_new

        @pl.when(ki == qi)
        def _():
            o_ref[...] = acc_sc[...] / l_sc[...]
            lse_ref[0] = m_sc[...] + jnp.log(l_sc[...])

    kmap = lambda blk: (lambda h, qi, ki: (jnp.minimum(ki, qi), 2 * h + blk))
    return pl.pallas_call(
        body, grid=(heads, nq, nq),
        in_specs=[pl.BlockSpec((tq, HEAD_PAD), lambda h, qi, ki: (qi, h)),
                  pl.BlockSpec((tq, QK_NOPE), kmap(0)),
                  pl.BlockSpec((tq, LANES), lambda h, qi, ki: (jnp.minimum(ki, qi), 0)),
                  pl.BlockSpec((tq, V_HEAD), kmap(1))],
        out_specs=[pl.BlockSpec((tq, V_HEAD), lambda h, qi, ki: (qi, h)),
                   pl.BlockSpec((1, tq, 1), lambda h, qi, ki: (h, qi, 0))],
        out_shape=[jax.ShapeDtypeStruct((t, heads * V_HEAD), F32), jax.ShapeDtypeStruct((heads, t, 1), F32)],
        scratch_shapes=[pltpu.VMEM((tq, 1), F32), pltpu.VMEM((tq, 1), F32), pltpu.VMEM((tq, V_HEAD), F32)],
        compiler_params=_params(("parallel", "parallel", "arbitrary")), name="attn_fwd")(q_c, kv, kpe, kv)


def _attn_bwd_q(q_c, kv, kpe, o, do, lse, comm=None):
    t = q_c.shape[0]
    heads = q_c.shape[1] // HEAD_PAD
    tq = _pick(t, ATTN_TILE)
    nq = t // tq
    scale = _attn_scale()

    def body(q_ref, kn_ref, kpe_ref, v_ref, o_ref, do_ref, lse_ref, dq_ref, delta_ref, acc_sc):
        qi, ki = pl.program_id(1), pl.program_id(2)

        @pl.when(ki == 0)
        def _():
            acc_sc[...] = jnp.zeros_like(acc_sc)
            delta_ref[0] = jnp.sum(do_ref[...] * o_ref[...], axis=1, keepdims=True)

        @pl.when(ki <= qi)
        def _():
            kc = jnp.concatenate([kn_ref[...], kpe_ref[...]], axis=1)
            s = lax.dot_general(q_ref[...], kc, _NT, preferred_element_type=F32) * scale
            row = lax.broadcasted_iota(jnp.int32, s.shape, 0) + qi * tq
            col = lax.broadcasted_iota(jnp.int32, s.shape, 1) + ki * tq
            p = jnp.where(row >= col, jnp.exp(s - lse_ref[0]), 0.0)
            dp = lax.dot_general(do_ref[...].astype(BF16), v_ref[...], _NT, preferred_element_type=F32)
            ds = (p * (dp - delta_ref[0]) * scale).astype(BF16)
            acc_sc[...] += jnp.dot(ds, kc, preferred_element_type=F32)

        @pl.when(ki == qi)
        def _():
            dq_ref[...] = acc_sc[...]

    kmap = lambda blk: (lambda h, qi, ki: (jnp.minimum(ki, qi), 2 * h + blk))
    qmap = lambda h, qi, ki: (qi, h)
    smap = lambda h, qi, ki: (h, qi, 0)
    outs, comm_outs = _call(
        body, grid=(heads, nq, nq),
        in_specs=[pl.BlockSpec((tq, HEAD_PAD), qmap), pl.BlockSpec((tq, QK_NOPE), kmap(0)),
                  pl.BlockSpec((tq, LANES), lambda h, qi, ki: (jnp.minimum(ki, qi), 0)),
                  pl.BlockSpec((tq, V_HEAD), kmap(1)),
                  pl.BlockSpec((tq, V_HEAD), qmap), pl.BlockSpec((tq, V_HEAD), qmap), pl.BlockSpec((1, tq, 1), smap)],
        out_specs=[pl.BlockSpec((tq, HEAD_PAD), qmap), pl.BlockSpec((1, tq, 1), smap)],
        out_shape=[jax.ShapeDtypeStruct((t, heads * HEAD_PAD), F32), jax.ShapeDtypeStruct((heads, t, 1), F32)],
        scratch_shapes=[pltpu.VMEM((tq, HEAD_PAD), F32)],
        sem=("parallel", "parallel", "arbitrary"), name="attn_bwd_q", args=(q_c, kv, kpe, kv, o, do, lse), comm=comm)
    return outs[0], outs[1], comm_outs


def _attn_bwd_kv(q_c, kv, kpe, do, lse_row, delta_row, comm=None):
    t = q_c.shape[0]
    heads = q_c.shape[1] // HEAD_PAD
    tk = _pick(t, ATTN_TILE)
    nk = t // tk
    scale = _attn_scale()

    def body(q_ref, kn_ref, kpe_ref, v_ref, do_ref, lse_ref, delta_ref, dkv_ref, dkpe_ref, dk_sc, dv_sc):
        ki, qi = pl.program_id(1), pl.program_id(2)

        @pl.when(qi == 0)
        def _():
            dk_sc[...] = jnp.zeros_like(dk_sc)
            dv_sc[...] = jnp.zeros_like(dv_sc)

        @pl.when(qi >= ki)
        def _():
            kc = jnp.concatenate([kn_ref[...], kpe_ref[...]], axis=1)
            q = q_ref[...]
            st = lax.dot_general(kc, q, _NT, preferred_element_type=F32) * scale
            krow = lax.broadcasted_iota(jnp.int32, st.shape, 0) + ki * tk
            qcol = lax.broadcasted_iota(jnp.int32, st.shape, 1) + qi * tk
            pt = jnp.where(qcol >= krow, jnp.exp(st - lse_ref[0]), 0.0)
            do_b = do_ref[...].astype(BF16)
            dv_sc[...] += jnp.dot(pt.astype(BF16), do_b, preferred_element_type=F32)
            dpt = lax.dot_general(v_ref[...], do_b, _NT, preferred_element_type=F32)
            dst = (pt * (dpt - delta_ref[0]) * scale).astype(BF16)
            dk_sc[...] += jnp.dot(dst, q, preferred_element_type=F32)

        @pl.when(qi == nk - 1)
        def _():
            dkv_ref[:, 0:QK_NOPE] = dk_sc[:, 0:QK_NOPE].astype(BF16)
            dkv_ref[:, QK_NOPE:QK_NOPE + V_HEAD] = dv_sc[...].astype(BF16)
            dkpe_ref[0] = dk_sc[:, QK_NOPE:QK_NOPE + LANES]

    qclamp = lambda h, ki, qi: (jnp.maximum(qi, ki), h)
    kmap = lambda blk: (lambda h, ki, qi: (ki, 2 * h + blk))
    rmap = lambda h, ki, qi: (h, 0, jnp.maximum(qi, ki))
    outs, comm_outs = _call(
        body, grid=(heads, nk, nk),
        in_specs=[pl.BlockSpec((tk, HEAD_PAD), qclamp), pl.BlockSpec((tk, QK_NOPE), kmap(0)),
                  pl.BlockSpec((tk, LANES), lambda h, ki, qi: (ki, 0)), pl.BlockSpec((tk, V_HEAD), kmap(1)),
                  pl.BlockSpec((tk, V_HEAD), qclamp), pl.BlockSpec((1, 1, tk), rmap), pl.BlockSpec((1, 1, tk), rmap)],
        out_specs=[pl.BlockSpec((tk, HEAD_PAD), lambda h, ki, qi: (ki, h)),
                   pl.BlockSpec((1, tk, LANES), lambda h, ki, qi: (h, ki, 0))],
        out_shape=[jax.ShapeDtypeStruct((t, heads * HEAD_PAD), BF16), jax.ShapeDtypeStruct((heads, t, LANES), F32)],
        scratch_shapes=[pltpu.VMEM((tk, HEAD_PAD), F32), pltpu.VMEM((tk, V_HEAD), F32)],
        sem=("parallel", "parallel", "arbitrary"), name="attn_bwd_kv",
        args=(q_c, kv, kpe, kv, do, lse_row, delta_row), comm=comm)
    return outs[0], outs[1], comm_outs


def _local_step(x, pos_col, target, small, big):
    d = x.shape[1]
    ql, kvl = small["q_norm_g"].shape[1], small["kv_norm_g"].shape[1]
    half = QK_ROPE // 2
    lane = jnp.arange(LANES)
    inv_freq = ROPE_THETA ** (-jnp.arange(0, QK_ROPE, 2, dtype=F32) / QK_ROPE)
    inv_row = inv_freq[lane % half][None, :]
    sign_row = jnp.where((lane % QK_ROPE) < half, -1.0, 1.0).astype(F32)[None, :]
    cos, sin = _rope_tables(pos_col, inv_row, sign_row)
    ws = small["w_sgu"]
    b_col = small["b_sgu_col"]

    a = _norm_fwd(x, small["norm_mix_g"], "norm_mix_fwd")
    z_lat = _mm(a, big["w_lat"], name="z_lat")
    z_uv = _mm(a, big["w_uv"], name="z_uv")
    z_g = _mm(a, big["w_g"], name="z_g")
    qn, kvn, kpe = _lat_fwd(z_lat, small["q_norm_g"], small["kv_norm_g"], cos, sin, ql, kvl)
    q_p = _mm(qn, big["w_uq"], name="q_up")
    kv = _mm(kvn, big["w_ukv"], out_dtype=BF16, name="kv_up")
    q_c = _q_rope(q_p, cos, sin, False, "q_rope")
    attn, lse = _attn_fwd(q_c, kv, kpe)
    y_attn = _mm(attn, big["w_o_attn"], name="y_attn")
    s_out = _sgu_fwd(z_uv, small["sgu_norm_g"], ws, b_col)
    y_sgu = _mm(s_out, big["w_o_sgu"], name="y_sgu")
    merged = _merge_fwd(y_attn, y_sgu, z_g, small["b_gate"])
    h1 = _mm(merged, big["w_out"], add=x, name="h1")
    f = _norm_fwd(h1, small["norm_ffn_g"], "norm_ffn_fwd")
    gate = _mm(f, big["w_gate"], name="ffn_gate")
    up = _mm(f, big["w_up"], name="ffn_up")
    act = _swiglu_fwd(gate, up)
    h2 = _mm(act, big["w_down"], add=h1, name="h2")
    loss_row, dh2, d_norm_final = _loss_head(h2, small["norm_final_g"], target)

    def pair_sums(names, slabs, bufs):
        return [_pair_sum(g, b, "pair_sum_" + k) for k, g, b in zip(names, slabs, bufs)]

    dw_down = _mm(act, dh2, ta=True, out_dtype=BF16, name="dw_down")
    dact = _mm(dh2, big["w_down"], tb=True, name="dact")
    dgate, dup = _swiglu_bwd(gate, up, dact)
    dw_gate = _mm(f, dgate, ta=True, out_dtype=BF16, name="dw_gate")
    dw_up = _mm(f, dup, ta=True, out_dtype=BF16, name="dw_up")
    ffn_names = ["w_down_ffn", "w_gate_ffn", "w_up_ffn"]
    ffn_slabs = [_slabs_from_rows(dw_down), _slabs_from_cols(dw_gate), _slabs_from_cols(dw_up)]
    df, bufs = _mm(dgate, big["w_gate"], tb=True, name="df_gate", comm=_to_sibling(ffn_slabs))
    ffn_pairs = pair_sums(ffn_names, ffn_slabs, bufs)
    df = _mm(dup, big["w_up"], tb=True, add=df, name="df_up")
    dh1, d_norm_ffn = _norm_bwd(h1, small["norm_ffn_g"], df, dh2, "norm_ffn_bwd")
    dw_out = _mm(merged, dh1, ta=True, out_dtype=BF16, name="dw_out")
    dmerged = _mm(dh1, big["w_out"], tb=True, name="dmerged")
    dy_attn, dy_sgu, dz_g, d_b_gate = _merge_bwd(dmerged, y_attn, y_sgu, z_g, small["b_gate"])
    dw_o_sgu = _mm(s_out, dy_sgu, ta=True, out_dtype=BF16, name="dw_o_sgu")
    ds_out = _mm(dy_sgu, big["w_o_sgu"], tb=True, name="ds_out")
    dz_uv, d_ws, d_b_col, d_sgu_norm = _sgu_bwd(z_uv, ds_out, small["sgu_norm_g"], ws, b_col)
    dw_o_attn = _mm(attn, dy_attn, ta=True, out_dtype=BF16, name="dw_o_attn")
    mix_names = ["w_out", "w_o_sgu", "w_o_attn"]
    mix_slabs = [_slabs_from_rows(dw_out), _slabs_from_cols(dw_o_sgu), _slabs_from_rows(dw_o_attn)]
    dattn, bufs = _mm(dy_attn, big["w_o_attn"], tb=True, name="dattn", comm=_to_sibling(mix_slabs))
    mix_pairs = pair_sums(mix_names, mix_slabs, bufs)
    parts = {}
    dq_c, delta, got = _attn_bwd_q(q_c, kv, kpe, attn, dattn, lse, comm=_to_chips(ffn_pairs[:2]))
    parts.update(zip(ffn_names[:2], got))
    heads, t = lse.shape[0], lse.shape[1]
    dkv, dkpe_heads, got = _attn_bwd_kv(q_c, kv, kpe, dattn, lse.reshape(heads, 1, t), delta.reshape(heads, 1, t),
                                        comm=_to_chips(ffn_pairs[2:] + mix_pairs))
    parts.update(zip(ffn_names[2:] + mix_names, got))
    dq_p = _q_rope(dq_c, cos, sin, True, "q_rope_bwd")
    dw_uq = _mm(qn, dq_p, ta=True, out_dtype=BF16, name="dw_uq")
    dw_ukv = _mm(kvn, dkv, ta=True, out_dtype=BF16, name="dw_ukv")
    dqn = _mm(dq_p, big["w_uq"], tb=True, name="dqn")
    dkvn = _mm(dkv, big["w_ukv"], tb=True, name="dkvn")
    dz_lat, d_q_norm, d_kv_norm = _lat_bwd(z_lat, small["q_norm_g"], small["kv_norm_g"], dqn, dkvn, dkpe_heads,
                                           cos, sin, ql, kvl)
    dw_lat = _mm(a, dz_lat, ta=True, out_dtype=BF16, name="dw_lat")
    dw_uv = _mm(a, dz_uv, ta=True, out_dtype=BF16, name="dw_uv")
    dw_g = _mm(a, dz_g, ta=True, out_dtype=BF16, name="dw_g")
    lat = ql + kvl + QK_ROPE
    dw_uq_cols = dw_uq.reshape(ql, heads, HEAD_PAD)[:, :, :QK_NOPE + QK_ROPE].reshape(ql, heads * (QK_NOPE + QK_ROPE))
    in_names = ["w_uq", "w_ukv", "w_in"]
    in_slabs = [_slabs_from_cols(dw_uq_cols), _slabs_from_cols(dw_ukv),
                _slabs_from_cols(jnp.concatenate([dw_lat[:, :lat], dw_uv, dw_g], axis=1))]
    da = _mm(dz_lat, big["w_lat"], tb=True, name="da_lat")
    da, bufs = _mm(dz_uv, big["w_uv"], tb=True, add=da, name="da_uv", comm=_to_sibling(in_slabs))
    in_pairs = pair_sums(in_names, in_slabs, bufs)
    da, got = _mm(dz_g, big["w_g"], tb=True, add=da, name="da_g", comm=_to_chips(in_pairs))
    parts.update(zip(in_names, got))
    grad_x, d_norm_mix = _norm_bwd(x, small["norm_mix_g"], da, dh1, "norm_mix_bwd")

    gs = {"norm_mix_g": d_norm_mix, "b_gate": d_b_gate, "q_norm_g": d_q_norm, "kv_norm_g": d_kv_norm,
          "sgu_norm_g": d_sgu_norm, "w_sgu": d_ws, "b_sgu_col": d_b_col, "norm_ffn_g": d_norm_ffn,
          "norm_final_g": d_norm_final}
    return loss_row, grad_x, gs, parts


def _my_place():
    return lax.axis_index("x"), lax.axis_index("y"), lax.axis_index("c")


def _all_gather(shards):
    n = len(shards)

    def body(*refs):
        ins, outs = refs[:n], refs[n:2 * n]
        send_sems, recv_sems, local_sems = refs[2 * n:]
        x, y, c = _my_place()
        me, sibling = (x, y, c), (x, y, 1 - c)
        chips = [(1 - x, y), (x, 1 - y), (1 - x, 1 - y)]

        def slab(w, place):
            return outs[w].at[4 * place[0] + 2 * place[1] + place[2]]

        def copy(w, k, place, to, src=None):
            return pltpu.make_async_remote_copy(
                src_ref=slab(w, place) if src is None else src, dst_ref=slab(w, place),
                send_sem=send_sems.at[w, k], recv_sem=recv_sems.at[w, k], device_id=to, device_id_type=MESH)

        mine = [pltpu.make_async_copy(ins[w], slab(w, me), local_sems.at[w]) for w in range(n)]
        for cp in mine:
            cp.start()
        started = []
        for w in range(n):
            first = [copy(w, 0, me, sibling, src=ins[w])]
            first += [copy(w, 1 + j, me, (*chip, c), src=ins[w]) for j, chip in enumerate(chips)]
            for cp in first:
                cp.start()
            started += first
        for w in range(n):
            for j, chip in enumerate(chips):
                copy(w, 1 + j, (*chip, c), me).wait_recv()
                fwd = copy(w, 4 + j, (*chip, c), sibling)
                fwd.start()
                started.append(fwd)
        for w in range(n):
            copy(w, 0, sibling, me).wait_recv()
            for j, chip in enumerate(chips):
                copy(w, 4 + j, (*chip, 1 - c), me).wait_recv()
        for cp in started:
            cp.wait_send()
        for cp in mine:
            cp.wait()

    any_spec = pl.BlockSpec(memory_space=pl.ANY)
    return pl.pallas_call(
        body, in_specs=[any_spec] * n, out_specs=[any_spec] * n,
        out_shape=[jax.ShapeDtypeStruct((N_DEV,) + s.shape, s.dtype) for s in shards],
        scratch_shapes=[pltpu.SemaphoreType.DMA((n, 7)), pltpu.SemaphoreType.DMA((n, 7)), pltpu.SemaphoreType.DMA((n,))],
        compiler_params=pltpu.CompilerParams(has_side_effects=True), name="all_gather_weights")(*shards)


N_CHIPS = N_DEV // 2


def _to_sibling(grads):
    n = len(grads)

    def copies(ins, outs, sems):
        x, y, c = _my_place()
        send_sems, recv_sems = sems
        return [pltpu.make_async_remote_copy(
            src_ref=ins[w].at[2 * i + (1 - c)], dst_ref=outs[w].at[i], send_sem=send_sems.at[w, i],
            recv_sem=recv_sems.at[w, i], device_id=(x, y, 1 - c), device_id_type=MESH)
            for w in range(n) for i in range(N_CHIPS)]

    def start(ins, outs, sems):
        for cp in copies(ins, outs, sems):
            cp.start()

    def finish(ins, outs, sems):
        for cp in copies(ins, outs, sems):
            cp.wait()

    return _Comm(grads, [jax.ShapeDtypeStruct((N_CHIPS,) + g.shape[1:], g.dtype) for g in grads],
                 [pltpu.SemaphoreType.DMA((n, N_CHIPS)), pltpu.SemaphoreType.DMA((n, N_CHIPS))], start, finish)


def _to_chips(parts):
    n = len(parts)

    def copies(ins, outs, sems):
        x, y, c = _my_place()
        send_sems, recv_sems, local_sems = sems
        mine = 2 * x + y
        chips = [(1 - x, y), (x, 1 - y), (1 - x, 1 - y)]
        remote = [pltpu.make_async_remote_copy(
            src_ref=ins[w].at[2 * cx + cy], dst_ref=outs[w].at[mine], send_sem=send_sems.at[w, j],
            recv_sem=recv_sems.at[w, j], device_id=(cx, cy, c), device_id_type=MESH)
            for w in range(n) for j, (cx, cy) in enumerate(chips)]
        local = [pltpu.make_async_copy(ins[w].at[mine], outs[w].at[mine], local_sems.at[w]) for w in range(n)]
        return remote + local

    def start(ins, outs, sems):
        for cp in copies(ins, outs, sems):
            cp.start()

    def finish(ins, outs, sems):
        for cp in copies(ins, outs, sems):
            cp.wait()

    return _Comm(parts, [jax.ShapeDtypeStruct(p.shape, p.dtype) for p in parts],
                 [pltpu.SemaphoreType.DMA((n, N_CHIPS - 1)), pltpu.SemaphoreType.DMA((n, N_CHIPS - 1)),
                  pltpu.SemaphoreType.DMA((n,))], start, finish)


def _pair_sum(g, buf, name):
    _, r, c = g.shape
    tr = _row_tile(r, c, 2)
    core = lax.axis_index("c").astype(jnp.int32).reshape(1)

    def body(core_ref, g_ref, b_ref, o_ref):
        o_ref[...] = (g_ref[...].astype(F32) + b_ref[...].astype(F32)).astype(o_ref.dtype)

    blk = (1, tr, c)
    return pl.pallas_call(
        body, grid_spec=pltpu.PrefetchScalarGridSpec(
            num_scalar_prefetch=1, grid=(N_CHIPS, r // tr),
            in_specs=[pl.BlockSpec(blk, lambda i, j, core_ref: (2 * i + core_ref[0], j, 0)),
                      pl.BlockSpec(blk, lambda i, j, core_ref: (i, j, 0))],
            out_specs=pl.BlockSpec(blk, lambda i, j, core_ref: (i, j, 0))),
        out_shape=jax.ShapeDtypeStruct(buf.shape, buf.dtype),
        compiler_params=_params(("parallel", "parallel")), name=name)(core, g, buf)


def _all_reduce_pack(pack):
    r = pack.shape[0]

    def body(x_ref, out_ref, gath_ref, send_sems, recv_sems, local_sem):
        x, y, c = _my_place()
        me, sibling = (x, y, c), (x, y, 1 - c)
        chips = [(1 - x, y), (x, 1 - y), (1 - x, 1 - y)]

        def slab(place):
            return gath_ref.at[4 * place[0] + 2 * place[1] + place[2]]

        def copy(k, place, to, src=None):
            return pltpu.make_async_remote_copy(
                src_ref=slab(place) if src is None else src, dst_ref=slab(place),
                send_sem=send_sems.at[k], recv_sem=recv_sems.at[k], device_id=to, device_id_type=MESH)

        mine = pltpu.make_async_copy(x_ref, slab(me), local_sem)
        mine.start()
        first = [copy(0, me, sibling, src=x_ref)]
        first += [copy(1 + j, me, (*chip, c), src=x_ref) for j, chip in enumerate(chips)]
        for cp in first:
            cp.start()
        passed = [copy(4 + j, (*chip, c), sibling) for j, chip in enumerate(chips)]
        for j, chip in enumerate(chips):
            copy(1 + j, (*chip, c), me).wait_recv()
            passed[j].start()
        copy(0, sibling, me).wait_recv()
        for j, chip in enumerate(chips):
            copy(4 + j, (*chip, 1 - c), me).wait_recv()
        for cp in first + passed:
            cp.wait_send()
        mine.wait()
        acc = gath_ref[0]
        for i in range(1, N_DEV):
            acc = acc + gath_ref[i]
        out_ref[...] = acc

    vmem = pl.BlockSpec(memory_space=pltpu.VMEM)
    return pl.pallas_call(
        body, in_specs=[vmem], out_specs=vmem, out_shape=jax.ShapeDtypeStruct(pack.shape, F32),
        scratch_shapes=[pltpu.VMEM((N_DEV, r, LANES), F32), pltpu.SemaphoreType.DMA((7,)),
                        pltpu.SemaphoreType.DMA((7,)), pltpu.SemaphoreType.DMA],
        compiler_params=pltpu.CompilerParams(vmem_limit_bytes=VMEM_LIMIT), name="all_reduce_small")(pack)


def _adamw_math(w, g, m, v):
    m = ADAM_B1 * m + (1.0 - ADAM_B1) * g
    v = ADAM_B2 * v + (1.0 - ADAM_B2) * (g * g)
    m_hat = m / (1.0 - ADAM_B1 ** ADAM_STEP)
    v_hat = v / (1.0 - ADAM_B2 ** ADAM_STEP)
    delta = -ADAM_LR * (m_hat / (jnp.sqrt(v_hat) + ADAM_EPS) + ADAM_WD * w)
    return delta, m, v


def _adamw_shard(parts, w, m, v, name):
    r, c = w.shape
    n_parts = parts.shape[0]
    tr = _pick(r, max(2 * SUBLANES, (256 * 1024) // c), 2 * SUBLANES)

    def body(p_ref, w_ref, m_ref, v_ref, g_ref, d_ref, nm_ref, nv_ref):
        g = p_ref[0].astype(F32)
        for i in range(1, n_parts):
            g = g + p_ref[i].astype(F32)
        g_ref[...] = g
        d_ref[...], nm_ref[...], nv_ref[...] = _adamw_math(w_ref[...], g, m_ref[...], v_ref[...])

    spec = pl.BlockSpec((tr, c), lambda i: (i, 0))
    return pl.pallas_call(
        body, grid=(r // tr,), in_specs=[pl.BlockSpec((n_parts, tr, c), lambda i: (0, i, 0)), spec, spec, spec],
        out_specs=[spec] * 4, out_shape=[jax.ShapeDtypeStruct((r, c), F32)] * 4,
        compiler_params=_params(("parallel",)), name=name)(parts, w, m, v)


def _adamw_pack(g, w, m, v):
    r, c = w.shape

    def body(g_ref, w_ref, m_ref, v_ref, d_ref, nm_ref, nv_ref):
        d_ref[...], nm_ref[...], nv_ref[...] = _adamw_math(w_ref[...], g_ref[...], m_ref[...], v_ref[...])

    return pl.pallas_call(
        body, in_specs=[_full((r, c))] * 4, out_specs=[_full((r, c))] * 3, grid=(1,),
        out_shape=[jax.ShapeDtypeStruct((r, c), F32)] * 3,
        compiler_params=_params(("arbitrary",)), name="adamw_small")(g, w, m, v)


def _cols_from_slabs(g):
    return jnp.transpose(g, (1, 0, 2)).reshape(g.shape[1], N_DEV * g.shape[2])


def _slabs_from_cols(w):
    r, c8 = w.shape
    return jnp.transpose(w.reshape(r, N_DEV, c8 // N_DEV), (1, 0, 2))


def _rows_from_slabs(g):
    return g.reshape(N_DEV * g.shape[1], g.shape[2])


def _slabs_from_rows(w):
    return w.reshape(N_DEV, w.shape[0] // N_DEV, w.shape[1])


def _compute_layout(gathered, ql, kvl, heads, sw):
    lat = ql + kvl + QK_ROPE
    w_in_full = _cols_from_slabs(gathered["w_in"])
    w_uq_heads = _cols_from_slabs(gathered["w_uq"]).reshape(ql, heads, QK_NOPE + QK_ROPE)
    return {
        "w_lat": jnp.pad(w_in_full[:, :lat], ((0, 0), (0, LANES - QK_ROPE))),
        "w_uv": w_in_full[:, lat:lat + 2 * sw],
        "w_g": w_in_full[:, lat + 2 * sw:],
        "w_uq": jnp.pad(w_uq_heads, ((0, 0), (0, 0), (0, HEAD_PAD - QK_NOPE - QK_ROPE))).reshape(ql, heads * HEAD_PAD),
        "w_ukv": _cols_from_slabs(gathered["w_ukv"]),
        "w_o_attn": _rows_from_slabs(gathered["w_o_attn"]),
        "w_o_sgu": _cols_from_slabs(gathered["w_o_sgu"]),
        "w_out": _rows_from_slabs(gathered["w_out"]),
        "w_gate": _cols_from_slabs(gathered["w_gate_ffn"]),
        "w_up": _cols_from_slabs(gathered["w_up_ffn"]),
        "w_down": _rows_from_slabs(gathered["w_down_ffn"]),
    }


_SMALL =["norm_mix_g", "b_gate", "q_norm_g", "kv_norm_g", "sgu_norm_g", "w_sgu", "b_sgu", "norm_ffn_g", "norm_final_g"]
_BIG = ["w_in", "w_uq", "w_ukv", "w_o_attn", "w_o_sgu", "w_out", "w_gate_ffn", "w_up_ffn", "w_down_ffn"]
_ORDER = ["norm_mix_g", "w_in", "b_gate", "q_norm_g", "w_uq", "kv_norm_g", "w_ukv", "w_o_attn", "sgu_norm_g", "w_sgu",
          "b_sgu", "w_o_sgu", "w_out", "norm_ffn_g", "w_gate_ffn", "w_up_ffn", "w_down_ffn", "norm_final_g"]


def _pack_rows(parts):
    rows, sizes = [], []
    for p in parts:
        flat = p.reshape(-1)
        n = flat.shape[0]
        padded = -(-n // (SUBLANES * LANES)) * (SUBLANES * LANES)
        rows.append(jnp.pad(flat, (0, padded - n)).reshape(padded // LANES, LANES))
        sizes.append((n, padded // LANES))
    return jnp.concatenate(rows, axis=0), sizes


def _unpack_rows(pack, sizes, shapes):
    out, r0 = [], 0
    for (n, nr), shp in zip(sizes, shapes):
        out.append(pack[r0:r0 + nr].reshape(-1)[:n].reshape(shp))
        r0 += nr
    return out


def kernel(x, positions, norm_mix_g, w_in, b_gate, q_norm_g, w_uq, kv_norm_g, w_ukv, w_o_attn, sgu_norm_g, w_sgu, b_sgu, w_o_sgu, w_out, norm_ffn_g, w_gate_ffn, w_up_ffn, w_down_ffn, norm_final_g, loss_target, m_norm_mix_g, m_w_in, m_b_gate, m_q_norm_g, m_w_uq, m_kv_norm_g, m_w_ukv, m_w_o_attn, m_sgu_norm_g, m_w_sgu, m_b_sgu, m_w_o_sgu, m_w_out, m_norm_ffn_g, m_w_gate_ffn, m_w_up_ffn, m_w_down_ffn, m_norm_final_g, v_norm_mix_g, v_w_in, v_b_gate, v_q_norm_g, v_w_uq, v_kv_norm_g, v_w_ukv, v_w_o_attn, v_sgu_norm_g, v_w_sgu, v_b_sgu, v_w_o_sgu, v_w_out, v_norm_ffn_g, v_w_gate_ffn, v_w_up_ffn, v_w_down_ffn, v_norm_final_g):
    wts = dict(norm_mix_g=norm_mix_g, w_in=w_in, b_gate=b_gate, q_norm_g=q_norm_g, w_uq=w_uq, kv_norm_g=kv_norm_g,
               w_ukv=w_ukv, w_o_attn=w_o_attn, sgu_norm_g=sgu_norm_g, w_sgu=w_sgu, b_sgu=b_sgu, w_o_sgu=w_o_sgu,
               w_out=w_out, norm_ffn_g=norm_ffn_g, w_gate_ffn=w_gate_ffn, w_up_ffn=w_up_ffn, w_down_ffn=w_down_ffn,
               norm_final_g=norm_final_g)
    mom = dict(norm_mix_g=m_norm_mix_g, w_in=m_w_in, b_gate=m_b_gate, q_norm_g=m_q_norm_g, w_uq=m_w_uq,
               kv_norm_g=m_kv_norm_g, w_ukv=m_w_ukv, w_o_attn=m_w_o_attn, sgu_norm_g=m_sgu_norm_g, w_sgu=m_w_sgu,
               b_sgu=m_b_sgu, w_o_sgu=m_w_o_sgu, w_out=m_w_out, norm_ffn_g=m_norm_ffn_g, w_gate_ffn=m_w_gate_ffn,
               w_up_ffn=m_w_up_ffn, w_down_ffn=m_w_down_ffn, norm_final_g=m_norm_final_g)
    var = dict(norm_mix_g=v_norm_mix_g, w_in=v_w_in, b_gate=v_b_gate, q_norm_g=v_q_norm_g, w_uq=v_w_uq,
               kv_norm_g=v_kv_norm_g, w_ukv=v_w_ukv, w_o_attn=v_w_o_attn, sgu_norm_g=v_sgu_norm_g, w_sgu=v_w_sgu,
               b_sgu=v_b_sgu, w_o_sgu=v_w_o_sgu, w_out=v_w_out, norm_ffn_g=v_norm_ffn_g, w_gate_ffn=v_w_gate_ffn,
               w_up_ffn=v_w_up_ffn, w_down_ffn=v_w_down_ffn, norm_final_g=v_norm_final_g)

    t, d = x.shape[1], x.shape[2]
    ql, kvl = q_norm_g.shape[1], kv_norm_g.shape[1]
    heads = (w_uq.shape[2] * N_DEV) // (QK_NOPE + QK_ROPE)
    sw = sgu_norm_g.shape[1]

    gathered = dict(zip(_BIG, _all_gather([wts[k][0].astype(BF16) for k in _BIG])))
    big = _compute_layout(gathered, ql, kvl, heads, sw)
    small = {
        "norm_mix_g": norm_mix_g, "b_gate": b_gate, "q_norm_g": q_norm_g, "kv_norm_g": kv_norm_g,
        "sgu_norm_g": sgu_norm_g, "w_sgu": w_sgu[0], "b_sgu_col": b_sgu[0][:, :, None], "norm_ffn_g": norm_ffn_g,
        "norm_final_g": norm_final_g[None, :],
    }

    loss_row, grad_x, gs, parts = _local_step(x[0], positions.reshape(t, 1), loss_target[0], small, big)

    grads, deltas, new_m, new_v = {}, {}, {}, {}
    for k in _BIG:
        shp = wts[k].shape
        g, dl, nm, nv = _adamw_shard(parts[k], wts[k][0], mom[k][0], var[k][0], "adamw_" + k)
        grads[k], deltas[k], new_m[k], new_v[k] = (a.reshape(shp) for a in (g, dl, nm, nv))

    small_grads = [gs["norm_mix_g"], gs["b_gate"], gs["q_norm_g"], gs["kv_norm_g"], gs["sgu_norm_g"], gs["w_sgu"],
                   gs["b_sgu_col"], gs["norm_ffn_g"], gs["norm_final_g"]]
    pack, sizes = _pack_rows([loss_row] + small_grads)
    total = _all_reduce_pack(pack)
    shapes = [(1, LANES)] + [wts[k].shape for k in _SMALL]
    unpacked = _unpack_rows(total, sizes, shapes)
    loss = unpacked[0][0, 0]
    for k, g in zip(_SMALL, unpacked[1:]):
        grads[k] = g
    g_pack = total[sizes[0][1]:]
    w_pack, _ = _pack_rows([wts[k] for k in _SMALL])
    m_pack, _ = _pack_rows([mom[k] for k in _SMALL])
    v_pack, _ = _pack_rows([var[k] for k in _SMALL])
    d_pack, nm_pack, nv_pack = _adamw_pack(g_pack, w_pack, m_pack, v_pack)
    small_shapes = [wts[k].shape for k in _SMALL]
    for store, pk in ((deltas, d_pack), (new_m, nm_pack), (new_v, nv_pack)):
        for k, a in zip(_SMALL, _unpack_rows(pk, sizes[1:], small_shapes)):
            store[k] = a

    return (loss, grad_x[None], *[grads[k] for k in _ORDER], *[deltas[k] for k in _ORDER],
            *[new_m[k] for k in _ORDER], *[new_v[k] for k in _ORDER])
```

```python
import functools
import math

import jax
import jax.numpy as jnp
from jax import lax
from jax.experimental import pallas as pl
from jax.experimental.pallas import tpu as pltpu

F32 = jnp.float32
BF16 = jnp.bfloat16

N_DEV = 8
N_HEADS = 16
QK_NOPE = 128
QK_ROPE = 64
V_HEAD = 128
HEAD_PAD = 256
ROPE_THETA = 10000.0
CHUNK = 128
SGU_GROUP = 128
RMS_EPS = 1e-6
LANES = 128
SUBLANES = 8

ADAM_LR = 0.001
ADAM_B1 = 0.9
ADAM_B2 = 0.999
ADAM_EPS = 1e-08
ADAM_WD = 0.01
ADAM_STEP = 10

VMEM_LIMIT = 48 * 1024 * 1024
MM_TILE = (1024, 512, 2048)
ATTN_TILE = 512
ROW_KERNEL_BYTES = 24 * 1024 * 1024
NEG_BIG = -1e30
MESH = pl.DeviceIdType.MESH


def _pick(n, target, mult=LANES):
    best = None
    d = mult
    while d <= min(n, target):
        if n % d == 0:
            best = d
        d += mult
    return best or n


def _row_tile(t, width, n_blocks, mult=2 * SUBLANES):
    return _pick(t, max(mult, ROW_KERNEL_BYTES // (3 * n_blocks * width * 4)), mult)


def _params(sem):
    return pltpu.CompilerParams(dimension_semantics=sem, vmem_limit_bytes=VMEM_LIMIT)


def _full(shape):
    nd = len(shape)
    return pl.BlockSpec(shape, lambda *_: (0,) * nd)


def _rows(tr, w, cb=0):
    return pl.BlockSpec((tr, w), lambda i: (i, cb))


class _Comm:
    def __init__(self, ins, out_shapes, sems, start, finish, aliases=None):
        self.ins, self.out_shapes, self.sems, self.start, self.finish = list(ins), list(out_shapes), list(sems), start, finish
        self.aliases = dict(aliases or {})


def _call(body, *, grid, in_specs, out_specs, out_shape, scratch_shapes=(), sem, name, args, comm=None):
    if comm is None:
        outs = pl.pallas_call(body, grid=grid, in_specs=list(in_specs), out_specs=list(out_specs),
                              out_shape=list(out_shape), scratch_shapes=list(scratch_shapes),
                              compiler_params=_params(sem), name=name)(*args)
        return list(outs), []
    n_in, n_out, n_sc = len(in_specs), len(out_shape), len(scratch_shapes)
    nci, nco = len(comm.ins), len(comm.out_shapes)

    def hosted(*refs):
        ins, refs = refs[:n_in], refs[n_in:]
        cins, refs = refs[:nci], refs[nci:]
        outs, refs = refs[:n_out], refs[n_out:]
        couts, refs = refs[:nco], refs[nco:]
        scratch, csems = refs[:n_sc], refs[n_sc:]
        ids = [pl.program_id(i) for i in range(len(grid))]
        first = functools.reduce(jnp.logical_and, [i == 0 for i in ids])
        last = functools.reduce(jnp.logical_and, [i == g - 1 for i, g in zip(ids, grid)])

        @pl.when(first)
        def _():
            comm.start(cins, couts, csems)

        body(*ins, *outs, *scratch)

        @pl.when(last)
        def _():
            comm.finish(cins, couts, csems)

    any_spec = pl.BlockSpec(memory_space=pl.ANY)
    res = pl.pallas_call(
        hosted, grid=grid, in_specs=list(in_specs) + [any_spec] * nci, out_specs=list(out_specs) + [any_spec] * nco,
        out_shape=list(out_shape) + comm.out_shapes, scratch_shapes=list(scratch_shapes) + comm.sems,
        input_output_aliases={n_in + i: n_out + o for i, o in comm.aliases.items()},
        compiler_params=pltpu.CompilerParams(dimension_semantics=("arbitrary",) * len(grid),
                                             vmem_limit_bytes=VMEM_LIMIT, has_side_effects=True),
        name=name)(*args, *comm.ins)
    return list(res[:n_out]), list(res[n_out:])


def _mm(a, b, *, ta=False, tb=False, add=None, out_dtype=F32, tm=None, tn=None, tk=None, name, comm=None):
    m, k = (a.shape[1], a.shape[0]) if ta else a.shape
    n = b.shape[0] if tb else b.shape[1]
    assert k == (b.shape[1] if tb else b.shape[0]), (a.shape, b.shape, ta, tb)
    tm, tn, tk = _pick(m, tm or MM_TILE[0]), _pick(n, tn or MM_TILE[1]), _pick(k, tk or MM_TILE[2])
    nk = k // tk
    dims = (((0 if ta else 1,), (1 if tb else 0,)), ((), ()))

    def body(*refs):
        if add is None:
            a_ref, b_ref, o_ref, acc_ref = refs
            add_ref = None
        else:
            a_ref, b_ref, add_ref, o_ref, acc_ref = refs
        kk = pl.program_id(2)

        @pl.when(kk == 0)
        def _():
            acc_ref[...] = jnp.zeros_like(acc_ref)

        acc_ref[...] += lax.dot_general(a_ref[...].astype(BF16), b_ref[...].astype(BF16), dims,
                                        preferred_element_type=F32)

        @pl.when(kk == nk - 1)
        def _():
            r = acc_ref[...]
            if add_ref is not None:
                r = r + add_ref[...].astype(F32)
            o_ref[...] = r.astype(o_ref.dtype)

    a_spec = pl.BlockSpec((tk, tm), lambda i, j, kk: (kk, i)) if ta else pl.BlockSpec((tm, tk), lambda i, j, kk: (i, kk))
    b_spec = pl.BlockSpec((tn, tk), lambda i, j, kk: (j, kk)) if tb else pl.BlockSpec((tk, tn), lambda i, j, kk: (kk, j))
    o_spec = pl.BlockSpec((tm, tn), lambda i, j, kk: (i, j))
    in_specs = [a_spec, b_spec] + ([o_spec] if add is not None else [])
    args = (a, b) + ((add,) if add is not None else ())
    outs, comm_outs = _call(
        body, grid=(m // tm, n // tn, nk), in_specs=in_specs, out_specs=[o_spec],
        out_shape=[jax.ShapeDtypeStruct((m, n), out_dtype)], scratch_shapes=[pltpu.VMEM((tm, tn), F32)],
        sem=("parallel", "parallel", "arbitrary"), name=name, args=args, comm=comm)
    return outs[0] if comm is None else (outs[0], comm_outs)


def _rms_scale(x):
    return lax.rsqrt(jnp.mean(x * x, axis=-1, keepdims=True) + RMS_EPS)


def _rms_bwd(xhat, r, g, dy):
    t = dy * g
    dx = r * (t - xhat * jnp.mean(t * xhat, axis=-1, keepdims=True))
    return dx, dy * xhat


_GELU_C = math.sqrt(2.0 / math.pi)


def _gelu(x):
    return x * (0.5 * (1.0 + jnp.tanh(_GELU_C * (x + 0.044715 * (x * x * x)))))


def _gelu_and_grad(x):
    t = jnp.tanh(_GELU_C * (x + 0.044715 * (x * x * x)))
    cdf = 0.5 * (1.0 + t)
    return x * cdf, cdf + x * (0.5 * (1.0 - t * t) * (_GELU_C * (1.0 + 3.0 * 0.044715 * (x * x))))


def _sigmoid(x):
    return 1.0 / (1.0 + jnp.exp(-x))


def _swap_halves(x):
    lane = lax.broadcasted_iota(jnp.int32, x.shape, 1)
    first = (lane % QK_ROPE) < (QK_ROPE // 2)
    return jnp.where(first, pltpu.roll(x, LANES - QK_ROPE // 2, 1), pltpu.roll(x, QK_ROPE // 2, 1))


def _rope(x, cos, sin_signed):
    return x * cos + _swap_halves(x) * sin_signed


def _rope_bwd(d, cos, sin_signed):
    return d * cos + _swap_halves(d * sin_signed)


def _rope_tables(pos_col, inv_freq_row, sign_row):
    t = pos_col.shape[0]
    tr = _pick(t, 512, SUBLANES)

    def body(p_ref, f_ref, s_ref, cos_ref, sin_ref):
        ang = p_ref[...].astype(F32) * f_ref[...]
        cos_ref[...] = jnp.cos(ang)
        sin_ref[...] = jnp.sin(ang) * s_ref[...]

    return pl.pallas_call(
        body, grid=(t // tr,), in_specs=[_rows(tr, 1), _full((1, LANES)), _full((1, LANES))],
        out_specs=[_rows(tr, LANES), _rows(tr, LANES)],
        out_shape=[jax.ShapeDtypeStruct((t, LANES), F32)] * 2,
        compiler_params=_params(("parallel",)), name="rope_tables")(pos_col, inv_freq_row, sign_row)


def _norm_fwd(x, g, name):
    t, d = x.shape
    tr = _row_tile(t, d, 2)

    def body(x_ref, g_ref, y_ref):
        xv = x_ref[...]
        y_ref[...] = (xv * _rms_scale(xv) * g_ref[...]).astype(BF16)

    return pl.pallas_call(
        body, grid=(t // tr,), in_specs=[_rows(tr, d), _full((1, d))], out_specs=_rows(tr, d),
        out_shape=jax.ShapeDtypeStruct((t, d), BF16), compiler_params=_params(("parallel",)), name=name)(x, g)


def _lat_fwd(z_lat, qg, kvg, cos, sin, ql, kvl):
    t = z_lat.shape[0]
    tr = _row_tile(t, z_lat.shape[1], 2)

    def body(z_ref, qg_ref, kvg_ref, cos_ref, sin_ref, qn_ref, kvn_ref, kpe_ref):
        q = z_ref[:, 0:ql]
        qn_ref[...] = (q * _rms_scale(q) * qg_ref[...]).astype(BF16)
        kv = z_ref[:, ql:ql + kvl]
        kvn_ref[...] = (kv * _rms_scale(kv) * kvg_ref[...]).astype(BF16)
        kpe_ref[...] = _rope(z_ref[:, ql + kvl:ql + kvl + LANES], cos_ref[...], sin_ref[...]).astype(BF16)

    w = z_lat.shape[1]
    return pl.pallas_call(
        body, grid=(t // tr,),
        in_specs=[_rows(tr, w), _full((1, ql)), _full((1, kvl)), _rows(tr, LANES), _rows(tr, LANES)],
        out_specs=[_rows(tr, ql), _rows(tr, kvl), _rows(tr, LANES)],
        out_shape=[jax.ShapeDtypeStruct((t, ql), BF16), jax.ShapeDtypeStruct((t, kvl), BF16),
                   jax.ShapeDtypeStruct((t, LANES), BF16)],
        compiler_params=_params(("parallel",)), name="lat_fwd")(z_lat, qg, kvg, cos, sin)


def _q_rope(q_p, cos, sin, bwd, name):
    t, w = q_p.shape
    tr = _row_tile(t, w, 2)
    fn = _rope_bwd if bwd else _rope

    def body(q_ref, cos_ref, sin_ref, o_ref):
        c, s = cos_ref[...], sin_ref[...]
        for h in range(w // HEAD_PAD):
            o_ref[:, h * HEAD_PAD:h * HEAD_PAD + QK_NOPE] = q_ref[:, h * HEAD_PAD:h * HEAD_PAD + QK_NOPE].astype(BF16)
            lo = h * HEAD_PAD + QK_NOPE
            o_ref[:, lo:lo + LANES] = fn(q_ref[:, lo:lo + LANES].astype(F32), c, s).astype(BF16)

    return pl.pallas_call(
        body, grid=(t // tr,), in_specs=[_rows(tr, w), _rows(tr, LANES), _rows(tr, LANES)], out_specs=_rows(tr, w),
        out_shape=jax.ShapeDtypeStruct((t, w), BF16), compiler_params=_params(("parallel",)), name=name)(q_p, cos, sin)


def _tril_mask():
    r = lax.broadcasted_iota(jnp.int32, (CHUNK, CHUNK), 0)
    c = lax.broadcasted_iota(jnp.int32, (CHUNK, CHUNK), 1)
    return r >= c


def _sgu_fwd(z_uv, gs, ws, b_col):
    t = z_uv.shape[0]
    sw = z_uv.shape[1] // 2
    groups = sw // SGU_GROUP
    tr = _pick(t, 256, CHUNK)

    def body(u_ref, v_ref, gs_ref, ws_ref, b_ref, o_ref):
        v = _gelu(v_ref[...])
        vn = (v * _rms_scale(v) * gs_ref[...]).astype(BF16)
        tri = _tril_mask()
        for g in range(groups):
            wg = jnp.where(tri, ws_ref[g], 0.0).astype(BF16)
            cols = slice(g * SGU_GROUP, (g + 1) * SGU_GROUP)
            for c in range(tr // CHUNK):
                rows = slice(c * CHUNK, (c + 1) * CHUNK)
                mixed = jnp.dot(wg, vn[rows, cols], preferred_element_type=F32) + b_ref[g]
                o_ref[rows, cols] = (_gelu(u_ref[rows, cols]) * mixed).astype(BF16)

    return pl.pallas_call(
        body, grid=(t // tr,),
        in_specs=[_rows(tr, sw, 0), _rows(tr, sw, 1), _full((1, sw)), _full(ws.shape), _full(b_col.shape)],
        out_specs=_rows(tr, sw), out_shape=jax.ShapeDtypeStruct((t, sw), BF16),
        compiler_params=_params(("parallel",)), name="sgu_fwd")(z_uv, z_uv, gs, ws, b_col)


def _merge_fwd(y_attn, y_sgu, z_g, b_gate):
    t, d = y_attn.shape
    tr = _row_tile(t, d, 5)

    def body(ya_ref, ys_ref, g0_ref, g1_ref, b0_ref, b1_ref, o_ref):
        g0 = _sigmoid(g0_ref[...] + b0_ref[...])
        g1 = _sigmoid(g1_ref[...] + b1_ref[...])
        o_ref[...] = (g0 * ya_ref[...] + g1 * ys_ref[...]).astype(BF16)

    bspec0 = pl.BlockSpec((1, d), lambda i: (0, 0))
    bspec1 = pl.BlockSpec((1, d), lambda i: (0, 1))
    return pl.pallas_call(
        body, grid=(t // tr,),
        in_specs=[_rows(tr, d), _rows(tr, d), _rows(tr, d, 0), _rows(tr, d, 1), bspec0, bspec1],
        out_specs=_rows(tr, d), out_shape=jax.ShapeDtypeStruct((t, d), BF16),
        compiler_params=_params(("parallel",)), name="merge_fwd")(y_attn, y_sgu, z_g, z_g, b_gate, b_gate)


def _swiglu_fwd(gate, up, comm=None):
    t, f = gate.shape
    tr = _row_tile(t, f, 3)

    def body(g_ref, u_ref, o_ref):
        g = g_ref[...]
        o_ref[...] = (g * _sigmoid(g) * u_ref[...]).astype(BF16)

    outs, comm_outs = _call(
        body, grid=(t // tr,), in_specs=[_rows(tr, f), _rows(tr, f)], out_specs=[_rows(tr, f)],
        out_shape=[jax.ShapeDtypeStruct((t, f), BF16)], sem=("parallel",), name="swiglu_fwd", args=(gate, up), comm=comm)
    return outs[0], comm_outs


def _loss_head(h2, g, target):
    t, d = h2.shape
    tr = _row_tile(t, d, 3)

    def body(h_ref, g_ref, t_ref, loss_ref, dh_ref, dg_ref):
        @pl.when(pl.program_id(0) == 0)
        def _():
            loss_ref[...] = jnp.zeros_like(loss_ref)
            dg_ref[...] = jnp.zeros_like(dg_ref)

        h = h_ref[...]
        r = _rms_scale(h)
        hhat = h * r
        gv = g_ref[...]
        err = hhat * gv - t_ref[...]
        loss_ref[...] += jnp.full(loss_ref.shape, 0.5 * jnp.sum(jnp.mean(err * err, axis=-1)), F32)
        dx, dg_rows = _rms_bwd(hhat, r, gv, err * (1.0 / d))
        dh_ref[...] = dx
        dg_ref[...] += jnp.sum(dg_rows, axis=0, keepdims=True)

    return pl.pallas_call(
        body, grid=(t // tr,), in_specs=[_rows(tr, d), _full((1, d)), _rows(tr, d)],
        out_specs=[_full((1, LANES)), _rows(tr, d), _full((1, d))],
        out_shape=[jax.ShapeDtypeStruct((1, LANES), F32), jax.ShapeDtypeStruct((t, d), F32),
                   jax.ShapeDtypeStruct((1, d), F32)],
        compiler_params=_params(("arbitrary",)), name="loss_head")(h2, g, target)


def _swiglu_bwd(gate, up, dact):
    t, f = gate.shape
    tr = _row_tile(t, f, 4)

    def body(g_ref, u_ref, d_ref, dg_ref, du_ref):
        g = g_ref[...]
        s = _sigmoid(g)
        d = d_ref[...]
        dg_ref[...] = (d * u_ref[...] * (s * (1.0 + g * (1.0 - s)))).astype(BF16)
        du_ref[...] = (d * (g * s)).astype(BF16)

    return pl.pallas_call(
        body, grid=(t // tr,), in_specs=[_rows(tr, f)] * 3, out_specs=[_rows(tr, f)] * 2,
        out_shape=[jax.ShapeDtypeStruct((t, f), BF16)] * 2,
        compiler_params=_params(("parallel",)), name="swiglu_bwd")(gate, up, dact)


def _norm_bwd(x, g, dy, resid, name):
    t, d = x.shape
    tr = _row_tile(t, d, 4)

    def body(x_ref, g_ref, dy_ref, r_ref, dx_ref, dg_ref):
        @pl.when(pl.program_id(0) == 0)
        def _():
            dg_ref[...] = jnp.zeros_like(dg_ref)

        xv = x_ref[...]
        r = _rms_scale(xv)
        dx, dg_rows = _rms_bwd(xv * r, r, g_ref[...], dy_ref[...])
        dx_ref[...] = r_ref[...] + dx
        dg_ref[...] += jnp.sum(dg_rows, axis=0, keepdims=True)

    return pl.pallas_call(
        body, grid=(t // tr,), in_specs=[_rows(tr, d), _full((1, d)), _rows(tr, d), _rows(tr, d)],
        out_specs=[_rows(tr, d), _full((1, d))],
        out_shape=[jax.ShapeDtypeStruct((t, d), F32), jax.ShapeDtypeStruct((1, d), F32)],
        compiler_params=_params(("arbitrary",)), name=name)(x, g, dy, resid)


def _merge_bwd(dmerged, y_attn, y_sgu, z_g, b_gate):
    t, d = y_attn.shape
    tr = _row_tile(t, d, 7)

    def body(dm_ref, ya_ref, ys_ref, g0_ref, g1_ref, b0_ref, b1_ref, dya_ref, dys_ref, dz_ref, db_ref):
        @pl.when(pl.program_id(0) == 0)
        def _():
            db_ref[...] = jnp.zeros_like(db_ref)

        dm = dm_ref[...]
        g0 = _sigmoid(g0_ref[...] + b0_ref[...])
        g1 = _sigmoid(g1_ref[...] + b1_ref[...])
        dya_ref[...] = (dm * g0).astype(BF16)
        dys_ref[...] = (dm * g1).astype(BF16)
        dl0 = dm * ya_ref[...] * (g0 * (1.0 - g0))
        dl1 = dm * ys_ref[...] * (g1 * (1.0 - g1))
        dz_ref[:, 0:d] = dl0.astype(BF16)
        dz_ref[:, d:2 * d] = dl1.astype(BF16)
        db_ref[:, 0:d] += jnp.sum(dl0, axis=0, keepdims=True)
        db_ref[:, d:2 * d] += jnp.sum(dl1, axis=0, keepdims=True)

    bspec0 = pl.BlockSpec((1, d), lambda i: (0, 0))
    bspec1 = pl.BlockSpec((1, d), lambda i: (0, 1))
    return pl.pallas_call(
        body, grid=(t // tr,),
        in_specs=[_rows(tr, d), _rows(tr, d), _rows(tr, d), _rows(tr, d, 0), _rows(tr, d, 1), bspec0, bspec1],
        out_specs=[_rows(tr, d), _rows(tr, d), _rows(tr, 2 * d), _full((1, 2 * d))],
        out_shape=[jax.ShapeDtypeStruct((t, d), BF16), jax.ShapeDtypeStruct((t, d), BF16),
                   jax.ShapeDtypeStruct((t, 2 * d), BF16), jax.ShapeDtypeStruct((1, 2 * d), F32)],
        compiler_params=_params(("arbitrary",)), name="merge_bwd")(dmerged, y_attn, y_sgu, z_g, z_g, b_gate, b_gate)


def _sgu_bwd(z_uv, ds_out, gs, ws, b_col):
    t = z_uv.shape[0]
    sw = z_uv.shape[1] // 2
    groups = sw // SGU_GROUP
    tr = _pick(t, 256, CHUNK)

    def body(u_ref, v_ref, d_ref, gs_ref, ws_ref, b_ref, dz_ref, dws_ref, db_ref, dgs_ref, dvn_ref):
        @pl.when(pl.program_id(0) == 0)
        def _():
            dws_ref[...] = jnp.zeros_like(dws_ref)
            db_ref[...] = jnp.zeros_like(db_ref)
            dgs_ref[...] = jnp.zeros_like(dgs_ref)

        v, dgelu_v = _gelu_and_grad(v_ref[...])
        r = _rms_scale(v)
        vhat = v * r
        gsv = gs_ref[...]
        vn = (vhat * gsv).astype(BF16)
        tri = _tril_mask()
        for g in range(groups):
            wg = jnp.where(tri, ws_ref[g], 0.0).astype(BF16)
            cols = slice(g * SGU_GROUP, (g + 1) * SGU_GROUP)
            for c in range(tr // CHUNK):
                rows = slice(c * CHUNK, (c + 1) * CHUNK)
                vn_cg = vn[rows, cols]
                mixed = jnp.dot(wg, vn_cg, preferred_element_type=F32) + b_ref[g]
                u, dgelu_u = _gelu_and_grad(u_ref[rows, cols])
                dso = d_ref[rows, cols]
                dz_ref[rows, cols] = (dso * mixed * dgelu_u).astype(BF16)
                dmixed = dso * u
                db_ref[g] += jnp.sum(dmixed, axis=1, keepdims=True)
                dmixed_b = dmixed.astype(BF16)
                dws_ref[g] += jnp.where(
                    tri, lax.dot_general(dmixed_b, vn_cg, (((1,), (1,)), ((), ())), preferred_element_type=F32), 0.0)
                dvn_ref[rows, cols] = lax.dot_general(wg, dmixed_b, (((0,), (0,)), ((), ())), preferred_element_type=F32)
        dvn = dvn_ref[...]
        dv, dgs_rows = _rms_bwd(vhat, r, gsv, dvn)
        dz_ref[:, sw:2 * sw] = (dv * dgelu_v).astype(BF16)
        dgs_ref[...] += jnp.sum(dgs_rows, axis=0, keepdims=True)

    return pl.pallas_call(
        body, grid=(t // tr,),
        in_specs=[_rows(tr, sw, 0), _rows(tr, sw, 1), _rows(tr, sw), _full((1, sw)), _full(ws.shape), _full(b_col.shape)],
        out_specs=[_rows(tr, 2 * sw), _full(ws.shape), _full(b_col.shape), _full((1, sw))],
        out_shape=[jax.ShapeDtypeStruct((t, 2 * sw), BF16), jax.ShapeDtypeStruct(ws.shape, F32),
                   jax.ShapeDtypeStruct(b_col.shape, F32), jax.ShapeDtypeStruct((1, sw), F32)],
        scratch_shapes=[pltpu.VMEM((tr, sw), F32)],
        compiler_params=_params(("arbitrary",)), name="sgu_bwd")(z_uv, z_uv, ds_out, gs, ws, b_col)


def _lat_bwd(z_lat, qg, kvg, dqn, dkvn, dkpe_heads, cos, sin, ql, kvl):
    t, w = z_lat.shape
    heads = dkpe_heads.shape[0]
    tr = _row_tile(t, w + heads * LANES, 3)

    def body(z_ref, qg_ref, kvg_ref, dq_ref, dkv_ref, dk_ref, cos_ref, sin_ref, dz_ref, dqg_ref, dkvg_ref):
        @pl.when(pl.program_id(0) == 0)
        def _():
            dqg_ref[...] = jnp.zeros_like(dqg_ref)
            dkvg_ref[...] = jnp.zeros_like(dkvg_ref)

        q = z_ref[:, 0:ql]
        r = _rms_scale(q)
        dx, dg_rows = _rms_bwd(q * r, r, qg_ref[...], dq_ref[...])
        dz_ref[:, 0:ql] = dx.astype(BF16)
        dqg_ref[...] += jnp.sum(dg_rows, axis=0, keepdims=True)
        kv = z_ref[:, ql:ql + kvl]
        r = _rms_scale(kv)
        dx, dg_rows = _rms_bwd(kv * r, r, kvg_ref[...], dkv_ref[...])
        dz_ref[:, ql:ql + kvl] = dx.astype(BF16)
        dkvg_ref[...] += jnp.sum(dg_rows, axis=0, keepdims=True)
        dk = dk_ref[0]
        for h in range(1, heads):
            dk = dk + dk_ref[h]
        dz_ref[:, ql + kvl:ql + kvl + LANES] = _rope_bwd(dk, cos_ref[...], sin_ref[...]).astype(BF16)

    return pl.pallas_call(
        body, grid=(t // tr,),
        in_specs=[_rows(tr, w), _full((1, ql)), _full((1, kvl)), _rows(tr, ql), _rows(tr, kvl),
                  pl.BlockSpec((heads, tr, LANES), lambda i: (0, i, 0)), _rows(tr, LANES), _rows(tr, LANES)],
        out_specs=[_rows(tr, w), _full((1, ql)), _full((1, kvl))],
        out_shape=[jax.ShapeDtypeStruct((t, w), BF16), jax.ShapeDtypeStruct((1, ql), F32),
                   jax.ShapeDtypeStruct((1, kvl), F32)],
        compiler_params=_params(("arbitrary",)), name="lat_bwd")(z_lat, qg, kvg, dqn, dkvn, dkpe_heads, cos, sin)


_NT = (((1,), (1,)), ((), ()))


def _attn_scale():
    return (QK_NOPE + QK_ROPE) ** -0.5


def _attn_fwd(q_c, kv, kpe, comm=None):
    t = q_c.shape[0]
    heads = q_c.shape[1] // HEAD_PAD
    tq = _pick(t, ATTN_TILE)
    nq = t // tq
    scale = _attn_scale()
    to_log2 = scale * math.log2(math.e)

    def body(q_ref, kn_ref, kpe_ref, v_ref, o_ref, lse_ref, m_sc, l_sc, acc_sc):
        qi, ki = pl.program_id(1), pl.program_id(2)

        @pl.when(ki == 0)
        def _():
            m_sc[...] = jnp.full_like(m_sc, NEG_BIG)
            l_sc[...] = jnp.zeros_like(l_sc)
            acc_sc[...] = jnp.zeros_like(acc_sc)

        def step(diagonal):
            kc = jnp.concatenate([kn_ref[...], kpe_ref[...]], axis=1)
            s = lax.dot_general(q_ref[...], kc, _NT, preferred_element_type=F32)
            if diagonal:
                row = lax.broadcasted_iota(jnp.int32, s.shape, 0)
                col = lax.broadcasted_iota(jnp.int32, s.shape, 1)
                s = jnp.where(row >= col, s, NEG_BIG)
            m_prev = m_sc[...]
            m_new = jnp.maximum(m_prev, jnp.max(s, axis=1, keepdims=True))
            alpha = jnp.exp2((m_prev - m_new) * to_log2)
            p = jnp.exp2((s - m_new) * to_log2)
            l_sc[...] = alpha * l_sc[...] + jnp.sum(p, axis=1, keepdims=True)
            acc_sc[...] = alpha * acc_sc[...] + jnp.dot(p.astype(BF16), v_ref[...], preferred_element_type=F32)
            m_sc[...] = m_new

        @pl.when(ki < qi)
        def _():
            step(False)

        @pl.when(ki == qi)
        def _():
            step(True)
            o_ref[...] = acc_sc[...] / l_sc[...]
            lse_ref[0] = m_sc[...] * scale + jnp.log(l_sc[...])

    kmap = lambda blk: (lambda h, qi, ki: (jnp.minimum(ki, qi), 2 * h + blk))
    outs, comm_outs = _call(
        body, grid=(heads, nq, nq),
        in_specs=[pl.BlockSpec((tq, HEAD_PAD), lambda h, qi, ki: (qi, h)),
                  pl.BlockSpec((tq, QK_NOPE), kmap(0)),
                  pl.BlockSpec((tq, LANES), lambda h, qi, ki: (jnp.minimum(ki, qi), 0)),
                  pl.BlockSpec((tq, V_HEAD), kmap(1))],
        out_specs=[pl.BlockSpec((tq, V_HEAD), lambda h, qi, ki: (qi, h)),
                   pl.BlockSpec((1, tq, 1), lambda h, qi, ki: (h, qi, 0))],
        out_shape=[jax.ShapeDtypeStruct((t, heads * V_HEAD), F32), jax.ShapeDtypeStruct((heads, t, 1), F32)],
        scratch_shapes=[pltpu.VMEM((tq, 1), F32), pltpu.VMEM((tq, 1), F32), pltpu.VMEM((tq, V_HEAD), F32)],
        sem=("parallel", "parallel", "arbitrary"), name="attn_fwd", args=(q_c, kv, kpe, kv), comm=comm)
    return outs[0], outs[1], comm_outs


def _attn_bwd_q(q_c, kv, kpe, o, do, lse, comm=None):
    t = q_c.shape[0]
    heads = q_c.shape[1] // HEAD_PAD
    tq = _pick(t, ATTN_TILE)
    nq = t // tq
    scale = _attn_scale()

    def body(q_ref, kn_ref, kpe_ref, v_ref, o_ref, do_ref, lse_ref, dq_ref, delta_ref, acc_sc):
        qi, ki = pl.program_id(1), pl.program_id(2)

        @pl.when(ki == 0)
        def _():
            acc_sc[...] = jnp.zeros_like(acc_sc)
            delta_ref[0] = jnp.sum(do_ref[...] * o_ref[...], axis=1, keepdims=True)

        def step(diagonal):
            kc = jnp.concatenate([kn_ref[...], kpe_ref[...]], axis=1)
            s = lax.dot_general(q_ref[...], kc, _NT, preferred_element_type=F32) * scale
            p = jnp.exp(s - lse_ref[0])
            if diagonal:
                row = lax.broadcasted_iota(jnp.int32, s.shape, 0)
                col = lax.broadcasted_iota(jnp.int32, s.shape, 1)
                p = jnp.where(row >= col, p, 0.0)
            dp = lax.dot_general(do_ref[...].astype(BF16), v_ref[...], _NT, preferred_element_type=F32)
            ds = (p * (dp - delta_ref[0]) * scale).astype(BF16)
            acc_sc[...] += jnp.dot(ds, kc, preferred_element_type=F32)

        @pl.when(ki < qi)
        def _():
            step(False)

        @pl.when(ki == qi)
        def _():
            step(True)
            dq_ref[...] = acc_sc[...]

    kmap = lambda blk: (lambda h, qi, ki: (jnp.minimum(ki, qi), 2 * h + blk))
    qmap = lambda h, qi, ki: (qi, h)
    smap = lambda h, qi, ki: (h, qi, 0)
    outs, comm_outs = _call(
        body, grid=(heads, nq, nq),
        in_specs=[pl.BlockSpec((tq, HEAD_PAD), qmap), pl.BlockSpec((tq, QK_NOPE), kmap(0)),
                  pl.BlockSpec((tq, LANES), lambda h, qi, ki: (jnp.minimum(ki, qi), 0)),
                  pl.BlockSpec((tq, V_HEAD), kmap(1)),
                  pl.BlockSpec((tq, V_HEAD), qmap), pl.BlockSpec((tq, V_HEAD), qmap), pl.BlockSpec((1, tq, 1), smap)],
        out_specs=[pl.BlockSpec((tq, HEAD_PAD), qmap), pl.BlockSpec((1, tq, 1), smap)],
        out_shape=[jax.ShapeDtypeStruct((t, heads * HEAD_PAD), F32), jax.ShapeDtypeStruct((heads, t, 1), F32)],
        scratch_shapes=[pltpu.VMEM((tq, HEAD_PAD), F32)],
        sem=("parallel", "parallel", "arbitrary"), name="attn_bwd_q", args=(q_c, kv, kpe, kv, o, do, lse), comm=comm)
    return outs[0], outs[1], comm_outs


def _attn_bwd_kv(q_c, kv, kpe, do, lse_row, delta_row, comm=None):
    t = q_c.shape[0]
    heads = q_c.shape[1] // HEAD_PAD
    tk = _pick(t, ATTN_TILE)
    nk = t // tk
    scale = _attn_scale()

    def body(q_ref, kn_ref, kpe_ref, v_ref, do_ref, lse_ref, delta_ref, dkv_ref, dkpe_ref, dk_sc, dv_sc):
        ki, qi = pl.program_id(1), pl.program_id(2)

        @pl.when(qi == 0)
        def _():
            dk_sc[...] = jnp.zeros_like(dk_sc)
            dv_sc[...] = jnp.zeros_like(dv_sc)

        def step(diagonal):
            kc = jnp.concatenate([kn_ref[...], kpe_ref[...]], axis=1)
            q = q_ref[...]
            st = lax.dot_general(kc, q, _NT, preferred_element_type=F32) * scale
            pt = jnp.exp(st - lse_ref[0])
            if diagonal:
                krow = lax.broadcasted_iota(jnp.int32, st.shape, 0)
                qcol = lax.broadcasted_iota(jnp.int32, st.shape, 1)
                pt = jnp.where(qcol >= krow, pt, 0.0)
            do_b = do_ref[...].astype(BF16)
            dv_sc[...] += jnp.dot(pt.astype(BF16), do_b, preferred_element_type=F32)
            dpt = lax.dot_general(v_ref[...], do_b, _NT, preferred_element_type=F32)
            dst = (pt * (dpt - delta_ref[0]) * scale).astype(BF16)
            dk_sc[...] += jnp.dot(dst, q, preferred_element_type=F32)

        @pl.when(qi > ki)
        def _():
            step(False)

        @pl.when(qi == ki)
        def _():
            step(True)

        @pl.when(qi == nk - 1)
        def _():
            dkv_ref[:, 0:QK_NOPE] = dk_sc[:, 0:QK_NOPE].astype(BF16)
            dkv_ref[:, QK_NOPE:QK_NOPE + V_HEAD] = dv_sc[...].astype(BF16)
            dkpe_ref[0] = dk_sc[:, QK_NOPE:QK_NOPE + LANES]

    qclamp = lambda h, ki, qi: (jnp.maximum(qi, ki), h)
    kmap = lambda blk: (lambda h, ki, qi: (ki, 2 * h + blk))
    rmap = lambda h, ki, qi: (h, 0, jnp.maximum(qi, ki))
    outs, comm_outs = _call(
        body, grid=(heads, nk, nk),
        in_specs=[pl.BlockSpec((tk, HEAD_PAD), qclamp), pl.BlockSpec((tk, QK_NOPE), kmap(0)),
                  pl.BlockSpec((tk, LANES), lambda h, ki, qi: (ki, 0)), pl.BlockSpec((tk, V_HEAD), kmap(1)),
                  pl.BlockSpec((tk, V_HEAD), qclamp), pl.BlockSpec((1, 1, tk), rmap), pl.BlockSpec((1, 1, tk), rmap)],
        out_specs=[pl.BlockSpec((tk, HEAD_PAD), lambda h, ki, qi: (ki, h)),
                   pl.BlockSpec((1, tk, LANES), lambda h, ki, qi: (h, ki, 0))],
        out_shape=[jax.ShapeDtypeStruct((t, heads * HEAD_PAD), BF16), jax.ShapeDtypeStruct((heads, t, LANES), F32)],
        scratch_shapes=[pltpu.VMEM((tk, HEAD_PAD), F32), pltpu.VMEM((tk, V_HEAD), F32)],
        sem=("parallel", "parallel", "arbitrary"), name="attn_bwd_kv",
        args=(q_c, kv, kpe, kv, do, lse_row, delta_row), comm=comm)
    return outs[0], outs[1], comm_outs


def _local_step(x, pos_col, target, small, shards):
    d = x.shape[1]
    ql, kvl = small["q_norm_g"].shape[1], small["kv_norm_g"].shape[1]
    sw = small["sgu_norm_g"].shape[1]
    heads = (shards["w_uq"].shape[1] * N_DEV) // (QK_NOPE + QK_ROPE)
    big = {}
    early = ["w_in", "w_uq", "w_ukv"]
    big.update(_compute_layout(dict(zip(early, _all_gather([shards[k] for k in early]))), ql, kvl, heads, sw))
    half = QK_ROPE // 2
    lane = jnp.arange(LANES)
    inv_freq = ROPE_THETA ** (-jnp.arange(0, QK_ROPE, 2, dtype=F32) / QK_ROPE)
    inv_row = inv_freq[lane % half][None, :]
    sign_row = jnp.where((lane % QK_ROPE) < half, -1.0, 1.0).astype(F32)[None, :]
    cos, sin = _rope_tables(pos_col, inv_row, sign_row)
    ws = small["w_sgu"]
    b_col = small["b_sgu_col"]

    def arrived(names, bufs):
        big.update(_compute_layout(dict(zip(names, bufs)), ql, kvl, heads, sw))

    a = _norm_fwd(x, small["norm_mix_g"], "norm_mix_fwd")
    z_lat = _mm(a, big["w_lat"], name="z_lat")
    z_uv = _mm(a, big["w_uv"], name="z_uv")
    mixers = ["w_o_sgu", "w_o_attn"]
    z_g, bufs = _mm(a, big["w_g"], name="z_g", comm=_gather_first([shards[k] for k in mixers]))
    qn, kvn, kpe = _lat_fwd(z_lat, small["q_norm_g"], small["kv_norm_g"], cos, sin, ql, kvl)
    q_p, bufs = _mm(qn, big["w_uq"], name="q_up", comm=_gather_second(bufs))
    arrived(mixers, bufs)
    kv = _mm(kvn, big["w_ukv"], out_dtype=BF16, name="kv_up")
    q_c = _q_rope(q_p, cos, sin, False, "q_rope")
    wide = ["w_out", "w_gate_ffn", "w_up_ffn"]
    attn, lse, bufs = _attn_fwd(q_c, kv, kpe, comm=_gather_first([shards[k] for k in wide]))
    s_out = _sgu_fwd(z_uv, small["sgu_norm_g"], ws, b_col)
    y_sgu = _mm(s_out, big["w_o_sgu"], name="y_sgu")
    y_attn, bufs = _mm(attn, big["w_o_attn"], name="y_attn", comm=_gather_second(bufs))
    arrived(wide, bufs)
    merged = _merge_fwd(y_attn, y_sgu, z_g, small["b_gate"])
    h1 = _mm(merged, big["w_out"], add=x, name="h1")
    f = _norm_fwd(h1, small["norm_ffn_g"], "norm_ffn_fwd")
    down = shards["w_down_ffn"]
    top = _pick(down.shape[0], down.shape[0] // 2, 2 * SUBLANES)
    gate, bufs = _mm(f, big["w_gate"], name="ffn_gate", comm=_gather_first([down], rows=[(0, top)]))
    up, bufs = _mm(f, big["w_up"], name="ffn_up",
                   comm=_gather_first([down], rows=[(top, down.shape[0] - top)], into=bufs))
    act, bufs = _swiglu_fwd(gate, up, comm=_gather_second(bufs))
    arrived(["w_down_ffn"], bufs)
    h2 = _mm(act, big["w_down"], add=h1, name="h2")
    loss_row, dh2, d_norm_final = _loss_head(h2, small["norm_final_g"], target)

    def pair_sums(names, slabs, bufs):
        return [_pair_sum(g, b, "pair_sum_" + k) for k, g, b in zip(names, slabs, bufs)]

    dw_down = _mm(act, dh2, ta=True, out_dtype=BF16, name="dw_down")
    dact = _mm(dh2, big["w_down"], tb=True, name="dact")
    dgate, dup = _swiglu_bwd(gate, up, dact)
    dw_gate = _mm(f, dgate, ta=True, out_dtype=BF16, name="dw_gate")
    dw_up = _mm(f, dup, ta=True, out_dtype=BF16, name="dw_up")
    ffn_names = ["w_down_ffn", "w_gate_ffn", "w_up_ffn"]
    ffn_slabs = [_slabs_from_rows(dw_down), _slabs_from_cols(dw_gate), _slabs_from_cols(dw_up)]
    df, bufs = _mm(dgate, big["w_gate"], tb=True, name="df_gate", comm=_to_sibling(ffn_slabs))
    ffn_pairs = pair_sums(ffn_names, ffn_slabs, bufs)
    df = _mm(dup, big["w_up"], tb=True, add=df, name="df_up")
    dh1, d_norm_ffn = _norm_bwd(h1, small["norm_ffn_g"], df, dh2, "norm_ffn_bwd")
    dw_out = _mm(merged, dh1, ta=True, out_dtype=BF16, name="dw_out")
    dmerged = _mm(dh1, big["w_out"], tb=True, name="dmerged")
    dy_attn, dy_sgu, dz_g, d_b_gate = _merge_bwd(dmerged, y_attn, y_sgu, z_g, small["b_gate"])
    dw_o_sgu = _mm(s_out, dy_sgu, ta=True, out_dtype=BF16, name="dw_o_sgu")
    ds_out = _mm(dy_sgu, big["w_o_sgu"], tb=True, name="ds_out")
    dz_uv, d_ws, d_b_col, d_sgu_norm = _sgu_bwd(z_uv, ds_out, small["sgu_norm_g"], ws, b_col)
    dw_o_attn = _mm(attn, dy_attn, ta=True, out_dtype=BF16, name="dw_o_attn")
    mix_names = ["w_out", "w_o_sgu", "w_o_attn"]
    mix_slabs = [_slabs_from_rows(dw_out), _slabs_from_cols(dw_o_sgu), _slabs_from_rows(dw_o_attn)]
    dattn, bufs = _mm(dy_attn, big["w_o_attn"], tb=True, name="dattn", comm=_to_sibling(mix_slabs))
    mix_pairs = pair_sums(mix_names, mix_slabs, bufs)
    parts = {}
    dq_c, delta, got = _attn_bwd_q(q_c, kv, kpe, attn, dattn, lse, comm=_to_chips(ffn_pairs[:2]))
    parts.update(zip(ffn_names[:2], got))
    heads, t = lse.shape[0], lse.shape[1]
    dkv, dkpe_heads, got = _attn_bwd_kv(q_c, kv, kpe, dattn, lse.reshape(heads, 1, t), delta.reshape(heads, 1, t),
                                        comm=_to_chips(ffn_pairs[2:] + mix_pairs))
    parts.update(zip(ffn_names[2:] + mix_names, got))
    dq_p = _q_rope(dq_c, cos, sin, True, "q_rope_bwd")
    dw_uq = _mm(qn, dq_p, ta=True, out_dtype=BF16, name="dw_uq")
    dw_ukv = _mm(kvn, dkv, ta=True, out_dtype=BF16, name="dw_ukv")
    dqn = _mm(dq_p, big["w_uq"], tb=True, name="dqn")
    dkvn = _mm(dkv, big["w_ukv"], tb=True, name="dkvn")
    dz_lat, d_q_norm, d_kv_norm = _lat_bwd(z_lat, small["q_norm_g"], small["kv_norm_g"], dqn, dkvn, dkpe_heads,
                                           cos, sin, ql, kvl)
    dw_lat = _mm(a, dz_lat, ta=True, out_dtype=BF16, name="dw_lat")
    dw_uv = _mm(a, dz_uv, ta=True, out_dtype=BF16, name="dw_uv")
    dw_g = _mm(a, dz_g, ta=True, out_dtype=BF16, name="dw_g")
    lat = ql + kvl + QK_ROPE
    dw_uq_cols = dw_uq.reshape(ql, heads, HEAD_PAD)[:, :, :QK_NOPE + QK_ROPE].reshape(ql, heads * (QK_NOPE + QK_ROPE))
    in_names = ["w_uq", "w_ukv", "w_in"]
    in_slabs = [_slabs_from_cols(dw_uq_cols), _slabs_from_cols(dw_ukv),
                _slabs_from_cols(jnp.concatenate([dw_lat[:, :lat], dw_uv, dw_g], axis=1))]
    da = _mm(dz_lat, big["w_lat"], tb=True, name="da_lat")
    da, bufs = _mm(dz_uv, big["w_uv"], tb=True, add=da, name="da_uv", comm=_to_sibling(in_slabs))
    in_pairs = pair_sums(in_names, in_slabs, bufs)
    da, got = _mm(dz_g, big["w_g"], tb=True, add=da, name="da_g", comm=_to_chips(in_pairs))
    parts.update(zip(in_names, got))
    grad_x, d_norm_mix = _norm_bwd(x, small["norm_mix_g"], da, dh1, "norm_mix_bwd")

    gs = {"norm_mix_g": d_norm_mix, "b_gate": d_b_gate, "q_norm_g": d_q_norm, "kv_norm_g": d_kv_norm,
          "sgu_norm_g": d_sgu_norm, "w_sgu": d_ws, "b_sgu_col": d_b_col, "norm_ffn_g": d_norm_ffn,
          "norm_final_g": d_norm_final}
    return loss_row, grad_x, gs, parts


def _my_place():
    return lax.axis_index("x"), lax.axis_index("y"), lax.axis_index("c")


def _all_gather(shards):
    n = len(shards)

    def body(*refs):
        ins, outs = refs[:n], refs[n:2 * n]
        send_sems, recv_sems, local_sems = refs[2 * n:]
        x, y, c = _my_place()
        me, sibling = (x, y, c), (x, y, 1 - c)
        chips = [(1 - x, y), (x, 1 - y), (1 - x, 1 - y)]

        def slab(w, place):
            return outs[w].at[4 * place[0] + 2 * place[1] + place[2]]

        def copy(w, k, place, to, src=None):
            return pltpu.make_async_remote_copy(
                src_ref=slab(w, place) if src is None else src, dst_ref=slab(w, place),
                send_sem=send_sems.at[w, k], recv_sem=recv_sems.at[w, k], device_id=to, device_id_type=MESH)

        mine = [pltpu.make_async_copy(ins[w], slab(w, me), local_sems.at[w]) for w in range(n)]
        for cp in mine:
            cp.start()
        started = []
        for w in range(n):
            first = [copy(w, 0, me, sibling, src=ins[w])]
            first += [copy(w, 1 + j, me, (*chip, c), src=ins[w]) for j, chip in enumerate(chips)]
            for cp in first:
                cp.start()
            started += first
        for w in range(n):
            for j, chip in enumerate(chips):
                copy(w, 1 + j, (*chip, c), me).wait_recv()
                fwd = copy(w, 4 + j, (*chip, c), sibling)
                fwd.start()
                started.append(fwd)
        for w in range(n):
            copy(w, 0, sibling, me).wait_recv()
            for j, chip in enumerate(chips):
                copy(w, 4 + j, (*chip, 1 - c), me).wait_recv()
        for cp in started:
            cp.wait_send()
        for cp in mine:
            cp.wait()

    any_spec = pl.BlockSpec(memory_space=pl.ANY)
    return pl.pallas_call(
        body, in_specs=[any_spec] * n, out_specs=[any_spec] * n,
        out_shape=[jax.ShapeDtypeStruct((N_DEV,) + s.shape, s.dtype) for s in shards],
        scratch_shapes=[pltpu.SemaphoreType.DMA((n, 7)), pltpu.SemaphoreType.DMA((n, 7)), pltpu.SemaphoreType.DMA((n,))],
        compiler_params=pltpu.CompilerParams(has_side_effects=True), name="all_gather_weights")(*shards)


N_CHIPS = N_DEV // 2


def _gather_first(shards, rows=None, into=None):
    n = len(shards)
    rows = rows or [(0, s.shape[0]) for s in shards]

    def copies(ins, outs, sems):
        x, y, c = _my_place()
        send_sems, recv_sems, local_sems = sems
        me = 4 * x + 2 * y + c
        targets = [(x, y, 1 - c), (1 - x, y, c), (x, 1 - y, c), (1 - x, 1 - y, c)]
        out = []
        for w in range(n):
            r0, nr = rows[w]
            src, dst = ins[w].at[pl.ds(r0, nr)], outs[w].at[me, pl.ds(r0, nr)]
            out.append(pltpu.make_async_copy(src, dst, local_sems.at[w]))
            out += [pltpu.make_async_remote_copy(src_ref=src, dst_ref=dst, send_sem=send_sems.at[w, k],
                                                 recv_sem=recv_sems.at[w, k], device_id=to, device_id_type=MESH)
                    for k, to in enumerate(targets)]
        return out

    def start(ins, outs, sems):
        for cp in copies(ins, outs, sems):
            cp.start()

    def finish(ins, outs, sems):
        for cp in copies(ins, outs, sems):
            cp.wait()

    return _Comm(list(shards) + list(into or []), [jax.ShapeDtypeStruct((N_DEV,) + s.shape, s.dtype) for s in shards],
                 [pltpu.SemaphoreType.DMA((n, 4)), pltpu.SemaphoreType.DMA((n, 4)), pltpu.SemaphoreType.DMA((n,))],
                 start, finish, aliases={n + w: w for w in range(n)} if into else None)


def _gather_second(bufs):
    n = len(bufs)

    def copies(ins, outs, sems):
        x, y, c = _my_place()
        send_sems, recv_sems = sems
        out = []
        for w in range(n):
            for j, (cx, cy) in enumerate([(1 - x, y), (x, 1 - y), (1 - x, 1 - y)]):
                slab = 4 * cx + 2 * cy + c
                out.append(pltpu.make_async_remote_copy(
                    src_ref=ins[w].at[slab], dst_ref=outs[w].at[slab], send_sem=send_sems.at[w, j],
                    recv_sem=recv_sems.at[w, j], device_id=(x, y, 1 - c), device_id_type=MESH))
        return out

    def start(ins, outs, sems):
        for cp in copies(ins, outs, sems):
            cp.start()

    def finish(ins, outs, sems):
        for cp in copies(ins, outs, sems):
            cp.wait()

    return _Comm(bufs, [jax.ShapeDtypeStruct(b.shape, b.dtype) for b in bufs],
                 [pltpu.SemaphoreType.DMA((n, 3)), pltpu.SemaphoreType.DMA((n, 3))], start, finish,
                 aliases={w: w for w in range(n)})


def _to_sibling(grads):
    n = len(grads)

    def copies(ins, outs, sems):
        x, y, c = _my_place()
        send_sems, recv_sems = sems
        return [pltpu.make_async_remote_copy(
            src_ref=ins[w].at[2 * i + (1 - c)], dst_ref=outs[w].at[i], send_sem=send_sems.at[w, i],
            recv_sem=recv_sems.at[w, i], device_id=(x, y, 1 - c), device_id_type=MESH)
            for w in range(n) for i in range(N_CHIPS)]

    def start(ins, outs, sems):
        for cp in copies(ins, outs, sems):
            cp.start()

    def finish(ins, outs, sems):
        for cp in copies(ins, outs, sems):
            cp.wait()

    return _Comm(grads, [jax.ShapeDtypeStruct((N_CHIPS,) + g.shape[1:], g.dtype) for g in grads],
                 [pltpu.SemaphoreType.DMA((n, N_CHIPS)), pltpu.SemaphoreType.DMA((n, N_CHIPS))], start, finish)


def _to_chips(parts):
    n = len(parts)

    def copies(ins, outs, sems):
        x, y, c = _my_place()
        send_sems, recv_sems, local_sems = sems
        mine = 2 * x + y
        chips = [(1 - x, y), (x, 1 - y), (1 - x, 1 - y)]
        remote = [pltpu.make_async_remote_copy(
            src_ref=ins[w].at[2 * cx + cy], dst_ref=outs[w].at[mine], send_sem=send_sems.at[w, j],
            recv_sem=recv_sems.at[w, j], device_id=(cx, cy, c), device_id_type=MESH)
            for w in range(n) for j, (cx, cy) in enumerate(chips)]
        local = [pltpu.make_async_copy(ins[w].at[mine], outs[w].at[mine], local_sems.at[w]) for w in range(n)]
        return remote + local

    def start(ins, outs, sems):
        for cp in copies(ins, outs, sems):
            cp.start()

    def finish(ins, outs, sems):
        for cp in copies(ins, outs, sems):
            cp.wait()

    return _Comm(parts, [jax.ShapeDtypeStruct(p.shape, p.dtype) for p in parts],
                 [pltpu.SemaphoreType.DMA((n, N_CHIPS - 1)), pltpu.SemaphoreType.DMA((n, N_CHIPS - 1)),
                  pltpu.SemaphoreType.DMA((n,))], start, finish)


def _pair_sum(g, buf, name):
    _, r, c = g.shape
    tr = _row_tile(r, c, 2)
    core = lax.axis_index("c").astype(jnp.int32).reshape(1)

    def body(core_ref, g_ref, b_ref, o_ref):
        o_ref[...] = (g_ref[...].astype(F32) + b_ref[...].astype(F32)).astype(o_ref.dtype)

    blk = (1, tr, c)
    return pl.pallas_call(
        body, grid_spec=pltpu.PrefetchScalarGridSpec(
            num_scalar_prefetch=1, grid=(N_CHIPS, r // tr),
            in_specs=[pl.BlockSpec(blk, lambda i, j, core_ref: (2 * i + core_ref[0], j, 0)),
                      pl.BlockSpec(blk, lambda i, j, core_ref: (i, j, 0))],
            out_specs=pl.BlockSpec(blk, lambda i, j, core_ref: (i, j, 0))),
        out_shape=jax.ShapeDtypeStruct(buf.shape, buf.dtype),
        compiler_params=_params(("parallel", "parallel")), name=name)(core, g, buf)


def _all_reduce_pack(pack):
    r = pack.shape[0]

    def body(x_ref, out_ref, gath_ref, send_sems, recv_sems, local_sem):
        x, y, c = _my_place()
        me, sibling = (x, y, c), (x, y, 1 - c)
        chips = [(1 - x, y), (x, 1 - y), (1 - x, 1 - y)]

        def slab(place):
            return gath_ref.at[4 * place[0] + 2 * place[1] + place[2]]

        def copy(k, place, to, src=None):
            return pltpu.make_async_remote_copy(
                src_ref=slab(place) if src is None else src, dst_ref=slab(place),
                send_sem=send_sems.at[k], recv_sem=recv_sems.at[k], device_id=to, device_id_type=MESH)

        mine = pltpu.make_async_copy(x_ref, slab(me), local_sem)
        mine.start()
        first = [copy(0, me, sibling, src=x_ref)]
        first += [copy(1 + j, me, (*chip, c), src=x_ref) for j, chip in enumerate(chips)]
        for cp in first:
            cp.start()
        passed = [copy(4 + j, (*chip, c), sibling) for j, chip in enumerate(chips)]
        for j, chip in enumerate(chips):
            copy(1 + j, (*chip, c), me).wait_recv()
            passed[j].start()
        copy(0, sibling, me).wait_recv()
        for j, chip in enumerate(chips):
            copy(4 + j, (*chip, 1 - c), me).wait_recv()
        for cp in first + passed:
            cp.wait_send()
        mine.wait()
        acc = gath_ref[0]
        for i in range(1, N_DEV):
            acc = acc + gath_ref[i]
        out_ref[...] = acc

    vmem = pl.BlockSpec(memory_space=pltpu.VMEM)
    return pl.pallas_call(
        body, in_specs=[vmem], out_specs=vmem, out_shape=jax.ShapeDtypeStruct(pack.shape, F32),
        scratch_shapes=[pltpu.VMEM((N_DEV, r, LANES), F32), pltpu.SemaphoreType.DMA((7,)),
                        pltpu.SemaphoreType.DMA((7,)), pltpu.SemaphoreType.DMA],
        compiler_params=pltpu.CompilerParams(vmem_limit_bytes=VMEM_LIMIT), name="all_reduce_small")(pack)


def _adamw_math(w, g, m, v):
    m = ADAM_B1 * m + (1.0 - ADAM_B1) * g
    v = ADAM_B2 * v + (1.0 - ADAM_B2) * (g * g)
    m_hat = m / (1.0 - ADAM_B1 ** ADAM_STEP)
    v_hat = v / (1.0 - ADAM_B2 ** ADAM_STEP)
    delta = -ADAM_LR * (m_hat / (jnp.sqrt(v_hat) + ADAM_EPS) + ADAM_WD * w)
    return delta, m, v


def _adamw_shard(parts, w, m, v, name):
    r, c = w.shape
    n_parts = parts.shape[0]
    tr = _pick(r, max(2 * SUBLANES, (256 * 1024) // c), 2 * SUBLANES)

    def body(p_ref, w_ref, m_ref, v_ref, g_ref, d_ref, nm_ref, nv_ref):
        g = p_ref[0].astype(F32)
        for i in range(1, n_parts):
            g = g + p_ref[i].astype(F32)
        g_ref[...] = g
        d_ref[...], nm_ref[...], nv_ref[...] = _adamw_math(w_ref[...], g, m_ref[...], v_ref[...])

    spec = pl.BlockSpec((tr, c), lambda i: (i, 0))
    return pl.pallas_call(
        body, grid=(r // tr,), in_specs=[pl.BlockSpec((n_parts, tr, c), lambda i: (0, i, 0)), spec, spec, spec],
        out_specs=[spec] * 4, out_shape=[jax.ShapeDtypeStruct((r, c), F32)] * 4,
        compiler_params=_params(("parallel",)), name=name)(parts, w, m, v)


def _adamw_pack(g, w, m, v):
    r, c = w.shape

    def body(g_ref, w_ref, m_ref, v_ref, d_ref, nm_ref, nv_ref):
        d_ref[...], nm_ref[...], nv_ref[...] = _adamw_math(w_ref[...], g_ref[...], m_ref[...], v_ref[...])

    return pl.pallas_call(
        body, in_specs=[_full((r, c))] * 4, out_specs=[_full((r, c))] * 3, grid=(1,),
        out_shape=[jax.ShapeDtypeStruct((r, c), F32)] * 3,
        compiler_params=_params(("arbitrary",)), name="adamw_small")(g, w, m, v)


def _cols_from_slabs(g):
    return jnp.transpose(g, (1, 0, 2)).reshape(g.shape[1], N_DEV * g.shape[2])


def _slabs_from_cols(w):
    r, c8 = w.shape
    return jnp.transpose(w.reshape(r, N_DEV, c8 // N_DEV), (1, 0, 2))


def _rows_from_slabs(g):
    return g.reshape(N_DEV * g.shape[1], g.shape[2])


def _slabs_from_rows(w):
    return w.reshape(N_DEV, w.shape[0] // N_DEV, w.shape[1])


def _compute_layout(gathered, ql, kvl, heads, sw):
    out = {}
    for k, g in gathered.items():
        if k == "w_in":
            lat = ql + kvl + QK_ROPE
            w_in_full = _cols_from_slabs(g)
            out["w_lat"] = jnp.pad(w_in_full[:, :lat], ((0, 0), (0, LANES - QK_ROPE)))
            out["w_uv"] = w_in_full[:, lat:lat + 2 * sw]
            out["w_g"] = w_in_full[:, lat + 2 * sw:]
        elif k == "w_uq":
            per_head = _cols_from_slabs(g).reshape(ql, heads, QK_NOPE + QK_ROPE)
            pad = HEAD_PAD - QK_NOPE - QK_ROPE
            out["w_uq"] = jnp.pad(per_head, ((0, 0), (0, 0), (0, pad))).reshape(ql, heads * HEAD_PAD)
        elif k in ("w_o_attn", "w_out", "w_down_ffn"):
            out[k.removesuffix("_ffn")] = _rows_from_slabs(g)
        else:
            out[k.removesuffix("_ffn")] = _cols_from_slabs(g)
    return out


_SMALL =["norm_mix_g", "b_gate", "q_norm_g", "kv_norm_g", "sgu_norm_g", "w_sgu", "b_sgu", "norm_ffn_g", "norm_final_g"]
_BIG = ["w_in", "w_uq", "w_ukv", "w_o_attn", "w_o_sgu", "w_out", "w_gate_ffn", "w_up_ffn", "w_down_ffn"]
_ORDER = ["norm_mix_g", "w_in", "b_gate", "q_norm_g", "w_uq", "kv_norm_g", "w_ukv", "w_o_attn", "sgu_norm_g", "w_sgu",
          "b_sgu", "w_o_sgu", "w_out", "norm_ffn_g", "w_gate_ffn", "w_up_ffn", "w_down_ffn", "norm_final_g"]


def _pack_rows(parts):
    rows, sizes = [], []
    for p in parts:
        flat = p.reshape(-1)
        n = flat.shape[0]
        padded = -(-n // (SUBLANES * LANES)) * (SUBLANES * LANES)
        rows.append(jnp.pad(flat, (0, padded - n)).reshape(padded // LANES, LANES))
        sizes.append((n, padded // LANES))
    return jnp.concatenate(rows, axis=0), sizes


def _unpack_rows(pack, sizes, shapes):
    out, r0 = [], 0
    for (n, nr), shp in zip(sizes, shapes):
        out.append(pack[r0:r0 + nr].reshape(-1)[:n].reshape(shp))
        r0 += nr
    return out


def kernel(x, positions, norm_mix_g, w_in, b_gate, q_norm_g, w_uq, kv_norm_g, w_ukv, w_o_attn, sgu_norm_g, w_sgu, b_sgu, w_o_sgu, w_out, norm_ffn_g, w_gate_ffn, w_up_ffn, w_down_ffn, norm_final_g, loss_target, m_norm_mix_g, m_w_in, m_b_gate, m_q_norm_g, m_w_uq, m_kv_norm_g, m_w_ukv, m_w_o_attn, m_sgu_norm_g, m_w_sgu, m_b_sgu, m_w_o_sgu, m_w_out, m_norm_ffn_g, m_w_gate_ffn, m_w_up_ffn, m_w_down_ffn, m_norm_final_g, v_norm_mix_g, v_w_in, v_b_gate, v_q_norm_g, v_w_uq, v_kv_norm_g, v_w_ukv, v_w_o_attn, v_sgu_norm_g, v_w_sgu, v_b_sgu, v_w_o_sgu, v_w_out, v_norm_ffn_g, v_w_gate_ffn, v_w_up_ffn, v_w_down_ffn, v_norm_final_g):
    wts = dict(norm_mix_g=norm_mix_g, w_in=w_in, b_gate=b_gate, q_norm_g=q_norm_g, w_uq=w_uq, kv_norm_g=kv_norm_g,
               w_ukv=w_ukv, w_o_attn=w_o_attn, sgu_norm_g=sgu_norm_g, w_sgu=w_sgu, b_sgu=b_sgu, w_o_sgu=w_o_sgu,
               w_out=w_out, norm_ffn_g=norm_ffn_g, w_gate_ffn=w_gate_ffn, w_up_ffn=w_up_ffn, w_down_ffn=w_down_ffn,
               norm_final_g=norm_final_g)
    mom = dict(norm_mix_g=m_norm_mix_g, w_in=m_w_in, b_gate=m_b_gate, q_norm_g=m_q_norm_g, w_uq=m_w_uq,
               kv_norm_g=m_kv_norm_g, w_ukv=m_w_ukv, w_o_attn=m_w_o_attn, sgu_norm_g=m_sgu_norm_g, w_sgu=m_w_sgu,
               b_sgu=m_b_sgu, w_o_sgu=m_w_o_sgu, w_out=m_w_out, norm_ffn_g=m_norm_ffn_g, w_gate_ffn=m_w_gate_ffn,
               w_up_ffn=m_w_up_ffn, w_down_ffn=m_w_down_ffn, norm_final_g=m_norm_final_g)
    var = dict(norm_mix_g=v_norm_mix_g, w_in=v_w_in, b_gate=v_b_gate, q_norm_g=v_q_norm_g, w_uq=v_w_uq,
               kv_norm_g=v_kv_norm_g, w_ukv=v_w_ukv, w_o_attn=v_w_o_attn, sgu_norm_g=v_sgu_norm_g, w_sgu=v_w_sgu,
               b_sgu=v_b_sgu, w_o_sgu=v_w_o_sgu, w_out=v_w_out, norm_ffn_g=v_norm_ffn_g, w_gate_ffn=v_w_gate_ffn,
               w_up_ffn=v_w_up_ffn, w_down_ffn=v_w_down_ffn, norm_final_g=v_norm_final_g)

    t, d = x.shape[1], x.shape[2]
    ql, kvl = q_norm_g.shape[1], kv_norm_g.shape[1]
    heads = (w_uq.shape[2] * N_DEV) // (QK_NOPE + QK_ROPE)
    sw = sgu_norm_g.shape[1]

    shards = {k: wts[k][0].astype(BF16) for k in _BIG}
    small = {
        "norm_mix_g": norm_mix_g, "b_gate": b_gate, "q_norm_g": q_norm_g, "kv_norm_g": kv_norm_g,
        "sgu_norm_g": sgu_norm_g, "w_sgu": w_sgu[0], "b_sgu_col": b_sgu[0][:, :, None], "norm_ffn_g": norm_ffn_g,
        "norm_final_g": norm_final_g[None, :],
    }

    loss_row, grad_x, gs, parts = _local_step(x[0], positions.reshape(t, 1), loss_target[0], small, shards)

    grads, deltas, new_m, new_v = {}, {}, {}, {}
    for k in _BIG:
        shp = wts[k].shape
        g, dl, nm, nv = _adamw_shard(parts[k], wts[k][0], mom[k][0], var[k][0], "adamw_" + k)
        grads[k], deltas[k], new_m[k], new_v[k] = (a.reshape(shp) for a in (g, dl, nm, nv))

    small_grads = [gs["norm_mix_g"], gs["b_gate"], gs["q_norm_g"], gs["kv_norm_g"], gs["sgu_norm_g"], gs["w_sgu"],
                   gs["b_sgu_col"], gs["norm_ffn_g"], gs["norm_final_g"]]
    pack, sizes = _pack_rows([loss_row] + small_grads)
    total = _all_reduce_pack(pack)
    shapes = [(1, LANES)] + [wts[k].shape for k in _SMALL]
    unpacked = _unpack_rows(total, sizes, shapes)
    loss = unpacked[0][0, 0]
    for k, g in zip(_SMALL, unpacked[1:]):
        grads[k] = g
    g_pack = total[sizes[0][1]:]
    w_pack, _ = _pack_rows([wts[k] for k in _SMALL])
    m_pack, _ = _pack_rows([mom[k] for k in _SMALL])
    v_pack, _ = _pack_rows([var[k] for k in _SMALL])
    d_pack, nm_pack, nv_pack = _adamw_pack(g_pack, w_pack, m_pack, v_pack)
    small_shapes = [wts[k].shape for k in _SMALL]
    for store, pk in ((deltas, d_pack), (new_m, nm_pack), (new_v, nv_pack)):
        for k, a in zip(_SMALL, _unpack_rows(pk, sizes[1:], small_shapes)):
            store[k] = a

    return (loss, grad_x[None], *[grads[k] for k in _ORDER], *[deltas[k] for k in _ORDER],
            *[new_m[k] for k in _ORDER], *[new_v[k] for k in _ORDER])
```

```python
import functools
import math

import jax
import jax.numpy as jnp
from jax import lax
from jax.experimental import pallas as pl
from jax.experimental.pallas import tpu as pltpu

F32 = jnp.float32
BF16 = jnp.bfloat16

N_DEV = 8
N_HEADS = 16
QK_NOPE = 128
QK_ROPE = 64
V_HEAD = 128
HEAD_PAD = 256
ROPE_THETA = 10000.0
CHUNK = 128
SGU_GROUP = 128
RMS_EPS = 1e-6
LANES = 128
SUBLANES = 8

ADAM_LR = 0.001
ADAM_B1 = 0.9
ADAM_B2 = 0.999
ADAM_EPS = 1e-08
ADAM_WD = 0.01
ADAM_STEP = 10

VMEM_LIMIT = 48 * 1024 * 1024
MM_TILE = (1024, 512, 2048)
ATTN_TILE = 512
ROW_KERNEL_BYTES = 24 * 1024 * 1024
TAIL_CHUNKS = 4
NEG_BIG = -1e30
MESH = pl.DeviceIdType.MESH


def _pick(n, target, mult=LANES):
    best = None
    d = mult
    while d <= min(n, target):
        if n % d == 0:
            best = d
        d += mult
    return best or n


def _row_tile(t, width, n_blocks, mult=2 * SUBLANES):
    return _pick(t, max(mult, ROW_KERNEL_BYTES // (3 * n_blocks * width * 4)), mult)


def _params(sem):
    return pltpu.CompilerParams(dimension_semantics=sem, vmem_limit_bytes=VMEM_LIMIT)


def _full(shape):
    nd = len(shape)
    return pl.BlockSpec(shape, lambda *_: (0,) * nd)


def _rows(tr, w, cb=0):
    return pl.BlockSpec((tr, w), lambda i: (i, cb))


class _Comm:
    def __init__(self, ins, out_shapes, sems, start, finish, aliases=None):
        self.ins, self.out_shapes, self.sems, self.start, self.finish = list(ins), list(out_shapes), list(sems), start, finish
        self.aliases = dict(aliases or {})


def _call(body, *, grid, in_specs, out_specs, out_shape, scratch_shapes=(), sem, name, args, comm=None):
    if comm is None:
        outs = pl.pallas_call(body, grid=grid, in_specs=list(in_specs), out_specs=list(out_specs),
                              out_shape=list(out_shape), scratch_shapes=list(scratch_shapes),
                              compiler_params=_params(sem), name=name)(*args)
        return list(outs), []
    n_in, n_out, n_sc = len(in_specs), len(out_shape), len(scratch_shapes)
    nci, nco = len(comm.ins), len(comm.out_shapes)

    def hosted(*refs):
        ins, refs = refs[:n_in], refs[n_in:]
        cins, refs = refs[:nci], refs[nci:]
        outs, refs = refs[:n_out], refs[n_out:]
        couts, refs = refs[:nco], refs[nco:]
        scratch, csems = refs[:n_sc], refs[n_sc:]
        ids = [pl.program_id(i) for i in range(len(grid))]
        first = functools.reduce(jnp.logical_and, [i == 0 for i in ids])
        last = functools.reduce(jnp.logical_and, [i == g - 1 for i, g in zip(ids, grid)])

        @pl.when(first)
        def _():
            comm.start(cins, couts, csems)

        body(*ins, *outs, *scratch)

        @pl.when(last)
        def _():
            comm.finish(cins, couts, csems)

    any_spec = pl.BlockSpec(memory_space=pl.ANY)
    res = pl.pallas_call(
        hosted, grid=grid, in_specs=list(in_specs) + [any_spec] * nci, out_specs=list(out_specs) + [any_spec] * nco,
        out_shape=list(out_shape) + comm.out_shapes, scratch_shapes=list(scratch_shapes) + comm.sems,
        input_output_aliases={n_in + i: n_out + o for i, o in comm.aliases.items()},
        compiler_params=pltpu.CompilerParams(dimension_semantics=("arbitrary",) * len(grid),
                                             vmem_limit_bytes=VMEM_LIMIT, has_side_effects=True),
        name=name)(*args, *comm.ins)
    return list(res[:n_out]), list(res[n_out:])


def _mm(a, b, *, ta=False, tb=False, add=None, out_dtype=F32, tm=None, tn=None, tk=None, name, comm=None, a_cols=None):
    m, k = (a.shape[1], a.shape[0]) if ta else a.shape
    if a_cols is not None:
        assert not ta
        k = a_cols[1]
    n = b.shape[0] if tb else b.shape[1]
    assert k == (b.shape[1] if tb else b.shape[0]), (a.shape, b.shape, ta, tb)
    tm, tn, tk = _pick(m, tm or MM_TILE[0]), _pick(n, tn or MM_TILE[1]), _pick(k, tk or MM_TILE[2])
    nk = k // tk
    k0 = 0
    if a_cols is not None:
        assert a_cols[0] % tk == 0, (a_cols, tk)
        k0 = a_cols[0] // tk
    dims = (((0 if ta else 1,), (1 if tb else 0,)), ((), ()))

    def body(*refs):
        if add is None:
            a_ref, b_ref, o_ref, acc_ref = refs
            add_ref = None
        else:
            a_ref, b_ref, add_ref, o_ref, acc_ref = refs
        kk = pl.program_id(2)

        @pl.when(kk == 0)
        def _():
            acc_ref[...] = jnp.zeros_like(acc_ref)

        acc_ref[...] += lax.dot_general(a_ref[...].astype(BF16), b_ref[...].astype(BF16), dims,
                                        preferred_element_type=F32)

        @pl.when(kk == nk - 1)
        def _():
            r = acc_ref[...]
            if add_ref is not None:
                r = r + add_ref[...].astype(F32)
            o_ref[...] = r.astype(o_ref.dtype)

    a_spec = (pl.BlockSpec((tk, tm), lambda i, j, kk: (kk, i)) if ta
              else pl.BlockSpec((tm, tk), lambda i, j, kk: (i, kk + k0)))
    b_spec = pl.BlockSpec((tn, tk), lambda i, j, kk: (j, kk)) if tb else pl.BlockSpec((tk, tn), lambda i, j, kk: (kk, j))
    o_spec = pl.BlockSpec((tm, tn), lambda i, j, kk: (i, j))
    in_specs = [a_spec, b_spec] + ([o_spec] if add is not None else [])
    args = (a, b) + ((add,) if add is not None else ())
    outs, comm_outs = _call(
        body, grid=(m // tm, n // tn, nk), in_specs=in_specs, out_specs=[o_spec],
        out_shape=[jax.ShapeDtypeStruct((m, n), out_dtype)], scratch_shapes=[pltpu.VMEM((tm, tn), F32)],
        sem=("parallel", "parallel", "arbitrary"), name=name, args=args, comm=comm)
    return outs[0] if comm is None else (outs[0], comm_outs)


def _rms_scale(x):
    return lax.rsqrt(jnp.mean(x * x, axis=-1, keepdims=True) + RMS_EPS)


def _rms_bwd(xhat, r, g, dy):
    t = dy * g
    dx = r * (t - xhat * jnp.mean(t * xhat, axis=-1, keepdims=True))
    return dx, dy * xhat


_GELU_C = math.sqrt(2.0 / math.pi)


def _gelu(x):
    return x * (0.5 * (1.0 + jnp.tanh(_GELU_C * (x + 0.044715 * (x * x * x)))))


def _gelu_and_grad(x):
    t = jnp.tanh(_GELU_C * (x + 0.044715 * (x * x * x)))
    cdf = 0.5 * (1.0 + t)
    return x * cdf, cdf + x * (0.5 * (1.0 - t * t) * (_GELU_C * (1.0 + 3.0 * 0.044715 * (x * x))))


def _sigmoid(x):
    return 1.0 / (1.0 + jnp.exp(-x))


def _swap_halves(x):
    lane = lax.broadcasted_iota(jnp.int32, x.shape, 1)
    first = (lane % QK_ROPE) < (QK_ROPE // 2)
    return jnp.where(first, pltpu.roll(x, LANES - QK_ROPE // 2, 1), pltpu.roll(x, QK_ROPE // 2, 1))


def _rope(x, cos, sin_signed):
    return x * cos + _swap_halves(x) * sin_signed


def _rope_bwd(d, cos, sin_signed):
    return d * cos + _swap_halves(d * sin_signed)


def _rope_tables(pos_col, inv_freq_row, sign_row):
    t = pos_col.shape[0]
    tr = _pick(t, 512, SUBLANES)

    def body(p_ref, f_ref, s_ref, cos_ref, sin_ref):
        ang = p_ref[...].astype(F32) * f_ref[...]
        cos_ref[...] = jnp.cos(ang)
        sin_ref[...] = jnp.sin(ang) * s_ref[...]

    return pl.pallas_call(
        body, grid=(t // tr,), in_specs=[_rows(tr, 1), _full((1, LANES)), _full((1, LANES))],
        out_specs=[_rows(tr, LANES), _rows(tr, LANES)],
        out_shape=[jax.ShapeDtypeStruct((t, LANES), F32)] * 2,
        compiler_params=_params(("parallel",)), name="rope_tables")(pos_col, inv_freq_row, sign_row)


def _norm_fwd(x, g, name):
    t, d = x.shape
    tr = _row_tile(t, d, 2)

    def body(x_ref, g_ref, y_ref):
        xv = x_ref[...]
        y_ref[...] = (xv * _rms_scale(xv) * g_ref[...]).astype(BF16)

    return pl.pallas_call(
        body, grid=(t // tr,), in_specs=[_rows(tr, d), _full((1, d))], out_specs=_rows(tr, d),
        out_shape=jax.ShapeDtypeStruct((t, d), BF16), compiler_params=_params(("parallel",)), name=name)(x, g)


def _lat_fwd(z_lat, qg, kvg, cos, sin, ql, kvl):
    t = z_lat.shape[0]
    tr = _row_tile(t, z_lat.shape[1], 2)

    def body(z_ref, qg_ref, kvg_ref, cos_ref, sin_ref, qn_ref, kvn_ref, kpe_ref):
        q = z_ref[:, 0:ql]
        qn_ref[...] = (q * _rms_scale(q) * qg_ref[...]).astype(BF16)
        kv = z_ref[:, ql:ql + kvl]
        kvn_ref[...] = (kv * _rms_scale(kv) * kvg_ref[...]).astype(BF16)
        kpe_ref[...] = _rope(z_ref[:, ql + kvl:ql + kvl + LANES], cos_ref[...], sin_ref[...]).astype(BF16)

    w = z_lat.shape[1]
    return pl.pallas_call(
        body, grid=(t // tr,),
        in_specs=[_rows(tr, w), _full((1, ql)), _full((1, kvl)), _rows(tr, LANES), _rows(tr, LANES)],
        out_specs=[_rows(tr, ql), _rows(tr, kvl), _rows(tr, LANES)],
        out_shape=[jax.ShapeDtypeStruct((t, ql), BF16), jax.ShapeDtypeStruct((t, kvl), BF16),
                   jax.ShapeDtypeStruct((t, LANES), BF16)],
        compiler_params=_params(("parallel",)), name="lat_fwd")(z_lat, qg, kvg, cos, sin)


def _q_rope(q_p, cos, sin, bwd, name):
    t, w = q_p.shape
    tr = _row_tile(t, w, 2)
    fn = _rope_bwd if bwd else _rope

    def body(q_ref, cos_ref, sin_ref, o_ref):
        c, s = cos_ref[...], sin_ref[...]
        for h in range(w // HEAD_PAD):
            o_ref[:, h * HEAD_PAD:h * HEAD_PAD + QK_NOPE] = q_ref[:, h * HEAD_PAD:h * HEAD_PAD + QK_NOPE].astype(BF16)
            lo = h * HEAD_PAD + QK_NOPE
            o_ref[:, lo:lo + LANES] = fn(q_ref[:, lo:lo + LANES].astype(F32), c, s).astype(BF16)

    return pl.pallas_call(
        body, grid=(t // tr,), in_specs=[_rows(tr, w), _rows(tr, LANES), _rows(tr, LANES)], out_specs=_rows(tr, w),
        out_shape=jax.ShapeDtypeStruct((t, w), BF16), compiler_params=_params(("parallel",)), name=name)(q_p, cos, sin)


def _tril_mask():
    r = lax.broadcasted_iota(jnp.int32, (CHUNK, CHUNK), 0)
    c = lax.broadcasted_iota(jnp.int32, (CHUNK, CHUNK), 1)
    return r >= c


def _sgu_fwd(z_uv, gs, ws, b_col):
    t = z_uv.shape[0]
    sw = z_uv.shape[1] // 2
    groups = sw // SGU_GROUP
    tr = _pick(t, 256, CHUNK)

    def body(u_ref, v_ref, gs_ref, ws_ref, b_ref, o_ref):
        v = _gelu(v_ref[...])
        vn = (v * _rms_scale(v) * gs_ref[...]).astype(BF16)
        tri = _tril_mask()
        for g in range(groups):
            wg = jnp.where(tri, ws_ref[g], 0.0).astype(BF16)
            cols = slice(g * SGU_GROUP, (g + 1) * SGU_GROUP)
            for c in range(tr // CHUNK):
                rows = slice(c * CHUNK, (c + 1) * CHUNK)
                mixed = jnp.dot(wg, vn[rows, cols], preferred_element_type=F32) + b_ref[g]
                o_ref[rows, cols] = (_gelu(u_ref[rows, cols]) * mixed).astype(BF16)

    return pl.pallas_call(
        body, grid=(t // tr,),
        in_specs=[_rows(tr, sw, 0), _rows(tr, sw, 1), _full((1, sw)), _full(ws.shape), _full(b_col.shape)],
        out_specs=_rows(tr, sw), out_shape=jax.ShapeDtypeStruct((t, sw), BF16),
        compiler_params=_params(("parallel",)), name="sgu_fwd")(z_uv, z_uv, gs, ws, b_col)


def _merge_fwd(y_attn, y_sgu, z_g, b_gate):
    t, d = y_attn.shape
    tr = _row_tile(t, d, 5)

    def body(ya_ref, ys_ref, g0_ref, g1_ref, b0_ref, b1_ref, o_ref):
        g0 = _sigmoid(g0_ref[...] + b0_ref[...])
        g1 = _sigmoid(g1_ref[...] + b1_ref[...])
        o_ref[...] = (g0 * ya_ref[...] + g1 * ys_ref[...]).astype(BF16)

    bspec0 = pl.BlockSpec((1, d), lambda i: (0, 0))
    bspec1 = pl.BlockSpec((1, d), lambda i: (0, 1))
    return pl.pallas_call(
        body, grid=(t // tr,),
        in_specs=[_rows(tr, d), _rows(tr, d), _rows(tr, d, 0), _rows(tr, d, 1), bspec0, bspec1],
        out_specs=_rows(tr, d), out_shape=jax.ShapeDtypeStruct((t, d), BF16),
        compiler_params=_params(("parallel",)), name="merge_fwd")(y_attn, y_sgu, z_g, z_g, b_gate, b_gate)


def _swiglu_fwd(gate, up, comm=None):
    t, f = gate.shape
    tr = _row_tile(t, f, 3)

    def body(g_ref, u_ref, o_ref):
        g = g_ref[...]
        o_ref[...] = (g * _sigmoid(g) * u_ref[...]).astype(BF16)

    outs, comm_outs = _call(
        body, grid=(t // tr,), in_specs=[_rows(tr, f), _rows(tr, f)], out_specs=[_rows(tr, f)],
        out_shape=[jax.ShapeDtypeStruct((t, f), BF16)], sem=("parallel",), name="swiglu_fwd", args=(gate, up), comm=comm)
    return outs[0], comm_outs


def _loss_head(h2, g, target):
    t, d = h2.shape
    tr = _row_tile(t, d, 3)

    def body(h_ref, g_ref, t_ref, loss_ref, dh_ref, dg_ref):
        @pl.when(pl.program_id(0) == 0)
        def _():
            loss_ref[...] = jnp.zeros_like(loss_ref)
            dg_ref[...] = jnp.zeros_like(dg_ref)

        h = h_ref[...]
        r = _rms_scale(h)
        hhat = h * r
        gv = g_ref[...]
        err = hhat * gv - t_ref[...]
        loss_ref[...] += jnp.full(loss_ref.shape, 0.5 * jnp.sum(jnp.mean(err * err, axis=-1)), F32)
        dx, dg_rows = _rms_bwd(hhat, r, gv, err * (1.0 / d))
        dh_ref[...] = dx
        dg_ref[...] += jnp.sum(dg_rows, axis=0, keepdims=True)

    return pl.pallas_call(
        body, grid=(t // tr,), in_specs=[_rows(tr, d), _full((1, d)), _rows(tr, d)],
        out_specs=[_full((1, LANES)), _rows(tr, d), _full((1, d))],
        out_shape=[jax.ShapeDtypeStruct((1, LANES), F32), jax.ShapeDtypeStruct((t, d), F32),
                   jax.ShapeDtypeStruct((1, d), F32)],
        compiler_params=_params(("arbitrary",)), name="loss_head")(h2, g, target)


def _swiglu_bwd(gate, up, dact):
    t, f = gate.shape
    tr = _row_tile(t, f, 4)

    def body(g_ref, u_ref, d_ref, dgu_ref):
        g = g_ref[...]
        s = _sigmoid(g)
        d = d_ref[...]
        dgu_ref[:, 0:f] = (d * u_ref[...] * (s * (1.0 + g * (1.0 - s)))).astype(BF16)
        dgu_ref[:, f:2 * f] = (d * (g * s)).astype(BF16)

    return pl.pallas_call(
        body, grid=(t // tr,), in_specs=[_rows(tr, f)] * 3, out_specs=_rows(tr, 2 * f),
        out_shape=jax.ShapeDtypeStruct((t, 2 * f), BF16),
        compiler_params=_params(("parallel",)), name="swiglu_bwd")(gate, up, dact)


def _norm_bwd(x, g, dy, resid, name, comm=None):
    t, d = x.shape
    tr = _row_tile(t, d, 4)

    def body(x_ref, g_ref, dy_ref, r_ref, dx_ref, dg_ref):
        @pl.when(pl.program_id(0) == 0)
        def _():
            dg_ref[...] = jnp.zeros_like(dg_ref)

        xv = x_ref[...]
        r = _rms_scale(xv)
        dx, dg_rows = _rms_bwd(xv * r, r, g_ref[...], dy_ref[...])
        dx_ref[...] = r_ref[...] + dx
        dg_ref[...] += jnp.sum(dg_rows, axis=0, keepdims=True)

    outs, comm_outs = _call(
        body, grid=(t // tr,), in_specs=[_rows(tr, d), _full((1, d)), _rows(tr, d), _rows(tr, d)],
        out_specs=[_rows(tr, d), _full((1, d))],
        out_shape=[jax.ShapeDtypeStruct((t, d), F32), jax.ShapeDtypeStruct((1, d), F32)],
        sem=("arbitrary",), name=name, args=(x, g, dy, resid), comm=comm)
    return (outs[0], outs[1]) if comm is None else (outs[0], outs[1], comm_outs)


def _merge_bwd(dmerged, y_attn, y_sgu, z_g, b_gate):
    t, d = y_attn.shape
    tr = _row_tile(t, d, 7)

    def body(dm_ref, ya_ref, ys_ref, g0_ref, g1_ref, b0_ref, b1_ref, dya_ref, dys_ref, dz_ref, db_ref):
        @pl.when(pl.program_id(0) == 0)
        def _():
            db_ref[...] = jnp.zeros_like(db_ref)

        dm = dm_ref[...]
        g0 = _sigmoid(g0_ref[...] + b0_ref[...])
        g1 = _sigmoid(g1_ref[...] + b1_ref[...])
        dya_ref[...] = (dm * g0).astype(BF16)
        dys_ref[...] = (dm * g1).astype(BF16)
        dl0 = dm * ya_ref[...] * (g0 * (1.0 - g0))
        dl1 = dm * ys_ref[...] * (g1 * (1.0 - g1))
        dz_ref[:, 0:d] = dl0.astype(BF16)
        dz_ref[:, d:2 * d] = dl1.astype(BF16)
        db_ref[:, 0:d] += jnp.sum(dl0, axis=0, keepdims=True)
        db_ref[:, d:2 * d] += jnp.sum(dl1, axis=0, keepdims=True)

    bspec0 = pl.BlockSpec((1, d), lambda i: (0, 0))
    bspec1 = pl.BlockSpec((1, d), lambda i: (0, 1))
    return pl.pallas_call(
        body, grid=(t // tr,),
        in_specs=[_rows(tr, d), _rows(tr, d), _rows(tr, d), _rows(tr, d, 0), _rows(tr, d, 1), bspec0, bspec1],
        out_specs=[_rows(tr, d), _rows(tr, d), _rows(tr, 2 * d), _full((1, 2 * d))],
        out_shape=[jax.ShapeDtypeStruct((t, d), BF16), jax.ShapeDtypeStruct((t, d), BF16),
                   jax.ShapeDtypeStruct((t, 2 * d), BF16), jax.ShapeDtypeStruct((1, 2 * d), F32)],
        compiler_params=_params(("arbitrary",)), name="merge_bwd")(dmerged, y_attn, y_sgu, z_g, z_g, b_gate, b_gate)


def _sgu_bwd(z_uv, ds_out, gs, ws, b_col):
    t = z_uv.shape[0]
    sw = z_uv.shape[1] // 2
    groups = sw // SGU_GROUP
    tr = _pick(t, 256, CHUNK)

    def body(u_ref, v_ref, d_ref, gs_ref, ws_ref, b_ref, dz_ref, dws_ref, db_ref, dgs_ref, dvn_ref):
        @pl.when(pl.program_id(0) == 0)
        def _():
            dws_ref[...] = jnp.zeros_like(dws_ref)
            db_ref[...] = jnp.zeros_like(db_ref)
            dgs_ref[...] = jnp.zeros_like(dgs_ref)

        v, dgelu_v = _gelu_and_grad(v_ref[...])
        r = _rms_scale(v)
        vhat = v * r
        gsv = gs_ref[...]
        vn = (vhat * gsv).astype(BF16)
        tri = _tril_mask()
        for g in range(groups):
            wg = jnp.where(tri, ws_ref[g], 0.0).astype(BF16)
            cols = slice(g * SGU_GROUP, (g + 1) * SGU_GROUP)
            for c in range(tr // CHUNK):
                rows = slice(c * CHUNK, (c + 1) * CHUNK)
                vn_cg = vn[rows, cols]
                mixed = jnp.dot(wg, vn_cg, preferred_element_type=F32) + b_ref[g]
                u, dgelu_u = _gelu_and_grad(u_ref[rows, cols])
                dso = d_ref[rows, cols]
                dz_ref[rows, cols] = (dso * mixed * dgelu_u).astype(BF16)
                dmixed = dso * u
                db_ref[g] += jnp.sum(dmixed, axis=1, keepdims=True)
                dmixed_b = dmixed.astype(BF16)
                dws_ref[g] += jnp.where(
                    tri, lax.dot_general(dmixed_b, vn_cg, (((1,), (1,)), ((), ())), preferred_element_type=F32), 0.0)
                dvn_ref[rows, cols] = lax.dot_general(wg, dmixed_b, (((0,), (0,)), ((), ())), preferred_element_type=F32)
        dvn = dvn_ref[...]
        dv, dgs_rows = _rms_bwd(vhat, r, gsv, dvn)
        dz_ref[:, sw:2 * sw] = (dv * dgelu_v).astype(BF16)
        dgs_ref[...] += jnp.sum(dgs_rows, axis=0, keepdims=True)

    return pl.pallas_call(
        body, grid=(t // tr,),
        in_specs=[_rows(tr, sw, 0), _rows(tr, sw, 1), _rows(tr, sw), _full((1, sw)), _full(ws.shape), _full(b_col.shape)],
        out_specs=[_rows(tr, 2 * sw), _full(ws.shape), _full(b_col.shape), _full((1, sw))],
        out_shape=[jax.ShapeDtypeStruct((t, 2 * sw), BF16), jax.ShapeDtypeStruct(ws.shape, F32),
                   jax.ShapeDtypeStruct(b_col.shape, F32), jax.ShapeDtypeStruct((1, sw), F32)],
        scratch_shapes=[pltpu.VMEM((tr, sw), F32)],
        compiler_params=_params(("arbitrary",)), name="sgu_bwd")(z_uv, z_uv, ds_out, gs, ws, b_col)


def _lat_bwd(z_lat, qg, kvg, dqn, dkvn, dkpe_heads, cos, sin, ql, kvl):
    t, w = z_lat.shape
    heads = dkpe_heads.shape[0]
    tr = _row_tile(t, w + heads * LANES, 3)

    def body(z_ref, qg_ref, kvg_ref, dq_ref, dkv_ref, dk_ref, cos_ref, sin_ref, dz_ref, dqg_ref, dkvg_ref):
        @pl.when(pl.program_id(0) == 0)
        def _():
            dqg_ref[...] = jnp.zeros_like(dqg_ref)
            dkvg_ref[...] = jnp.zeros_like(dkvg_ref)

        q = z_ref[:, 0:ql]
        r = _rms_scale(q)
        dx, dg_rows = _rms_bwd(q * r, r, qg_ref[...], dq_ref[...])
        dz_ref[:, 0:ql] = dx.astype(BF16)
        dqg_ref[...] += jnp.sum(dg_rows, axis=0, keepdims=True)
        kv = z_ref[:, ql:ql + kvl]
        r = _rms_scale(kv)
        dx, dg_rows = _rms_bwd(kv * r, r, kvg_ref[...], dkv_ref[...])
        dz_ref[:, ql:ql + kvl] = dx.astype(BF16)
        dkvg_ref[...] += jnp.sum(dg_rows, axis=0, keepdims=True)
        dk = dk_ref[0]
        for h in range(1, heads):
            dk = dk + dk_ref[h]
        dz_ref[:, ql + kvl:ql + kvl + LANES] = _rope_bwd(dk, cos_ref[...], sin_ref[...]).astype(BF16)

    return pl.pallas_call(
        body, grid=(t // tr,),
        in_specs=[_rows(tr, w), _full((1, ql)), _full((1, kvl)), _rows(tr, ql), _rows(tr, kvl),
                  pl.BlockSpec((heads, tr, LANES), lambda i: (0, i, 0)), _rows(tr, LANES), _rows(tr, LANES)],
        out_specs=[_rows(tr, w), _full((1, ql)), _full((1, kvl))],
        out_shape=[jax.ShapeDtypeStruct((t, w), BF16), jax.ShapeDtypeStruct((1, ql), F32),
                   jax.ShapeDtypeStruct((1, kvl), F32)],
        compiler_params=_params(("arbitrary",)), name="lat_bwd")(z_lat, qg, kvg, dqn, dkvn, dkpe_heads, cos, sin)


_NT = (((1,), (1,)), ((), ()))


def _attn_scale():
    return (QK_NOPE + QK_ROPE) ** -0.5


def _attn_fwd(q_c, kv, kpe, comm=None):
    t = q_c.shape[0]
    heads = q_c.shape[1] // HEAD_PAD
    tq = _pick(t, ATTN_TILE)
    nq = t // tq
    scale = _attn_scale()
    to_log2 = scale * math.log2(math.e)

    def body(q_ref, kn_ref, kpe_ref, v_ref, o_ref, lse_ref, m_sc, l_sc, acc_sc):
        qi, ki = pl.program_id(1), pl.program_id(2)

        @pl.when(ki == 0)
        def _():
            m_sc[...] = jnp.full_like(m_sc, NEG_BIG)
            l_sc[...] = jnp.zeros_like(l_sc)
            acc_sc[...] = jnp.zeros_like(acc_sc)

        def step(diagonal):
            kc = jnp.concatenate([kn_ref[...], kpe_ref[...]], axis=1)
            s = lax.dot_general(q_ref[...], kc, _NT, preferred_element_type=F32)
            if diagonal:
                row = lax.broadcasted_iota(jnp.int32, s.shape, 0)
                col = lax.broadcasted_iota(jnp.int32, s.shape, 1)
                s = jnp.where(row >= col, s, NEG_BIG)
            m_prev = m_sc[...]
            m_new = jnp.maximum(m_prev, jnp.max(s, axis=1, keepdims=True))
            alpha = jnp.exp2((m_prev - m_new) * to_log2)
            p = jnp.exp2((s - m_new) * to_log2)
            l_sc[...] = alpha * l_sc[...] + jnp.sum(p, axis=1, keepdims=True)
            acc_sc[...] = alpha * acc_sc[...] + jnp.dot(p.astype(BF16), v_ref[...], preferred_element_type=F32)
            m_sc[...] = m_new

        @pl.when(ki < qi)
        def _():
            step(False)

        @pl.when(ki == qi)
        def _():
            step(True)
            o_ref[...] = acc_sc[...] / l_sc[...]
            lse_ref[0] = m_sc[...] * scale + jnp.log(l_sc[...])

    kmap = lambda blk: (lambda h, qi, ki: (jnp.minimum(ki, qi), 2 * h + blk))
    outs, comm_outs = _call(
        body, grid=(heads, nq, nq),
        in_specs=[pl.BlockSpec((tq, HEAD_PAD), lambda h, qi, ki: (qi, h)),
                  pl.BlockSpec((tq, QK_NOPE), kmap(0)),
                  pl.BlockSpec((tq, LANES), lambda h, qi, ki: (jnp.minimum(ki, qi), 0)),
                  pl.BlockSpec((tq, V_HEAD), kmap(1))],
        out_specs=[pl.BlockSpec((tq, V_HEAD), lambda h, qi, ki: (qi, h)),
                   pl.BlockSpec((1, tq, 1), lambda h, qi, ki: (h, qi, 0))],
        out_shape=[jax.ShapeDtypeStruct((t, heads * V_HEAD), F32), jax.ShapeDtypeStruct((heads, t, 1), F32)],
        scratch_shapes=[pltpu.VMEM((tq, 1), F32), pltpu.VMEM((tq, 1), F32), pltpu.VMEM((tq, V_HEAD), F32)],
        sem=("parallel", "parallel", "arbitrary"), name="attn_fwd", args=(q_c, kv, kpe, kv), comm=comm)
    return outs[0], outs[1], comm_outs


def _attn_delta(o, do, heads):
    t = o.shape[0]
    tq = _pick(t, ATTN_TILE)

    def body(o_ref, do_ref, delta_ref):
        delta_ref[0] = jnp.sum(do_ref[...] * o_ref[...], axis=1, keepdims=True)

    qmap = lambda h, qi: (qi, h)
    return pl.pallas_call(
        body, grid=(heads, t // tq), in_specs=[pl.BlockSpec((tq, V_HEAD), qmap), pl.BlockSpec((tq, V_HEAD), qmap)],
        out_specs=pl.BlockSpec((1, tq, 1), lambda h, qi: (h, qi, 0)),
        out_shape=jax.ShapeDtypeStruct((heads, t, 1), F32),
        compiler_params=_params(("parallel", "parallel")), name="attn_delta")(o, do)


def _attn_bwd(q_c, kv, kpe, do, lse_row, delta_row, comm=None):
    t = q_c.shape[0]
    heads = q_c.shape[1] // HEAD_PAD
    tk = _pick(t, ATTN_TILE)
    nk = t // tk
    scale = _attn_scale()
    tn_dims = (((0,), (0,)), ((), ()))

    def body(q_ref, kn_ref, kpe_ref, v_ref, do_ref, lse_ref, delta_ref, dq_ref, dkv_ref, dkpe_ref, dk_sc, dv_sc):
        ki, qi = pl.program_id(1), pl.program_id(2)

        @pl.when(jnp.logical_and(ki == 0, qi == 0))
        def _():
            dq_ref[...] = jnp.zeros_like(dq_ref)

        @pl.when(qi == 0)
        def _():
            dk_sc[...] = jnp.zeros_like(dk_sc)
            dv_sc[...] = jnp.zeros_like(dv_sc)

        def step(diagonal):
            kc = jnp.concatenate([kn_ref[...], kpe_ref[...]], axis=1)
            q = q_ref[...]
            st = lax.dot_general(kc, q, _NT, preferred_element_type=F32) * scale
            pt = jnp.exp(st - lse_ref[0])
            if diagonal:
                krow = lax.broadcasted_iota(jnp.int32, st.shape, 0)
                qcol = lax.broadcasted_iota(jnp.int32, st.shape, 1)
                pt = jnp.where(qcol >= krow, pt, 0.0)
            do_b = do_ref[...].astype(BF16)
            dv_sc[...] += jnp.dot(pt.astype(BF16), do_b, preferred_element_type=F32)
            dpt = lax.dot_general(v_ref[...], do_b, _NT, preferred_element_type=F32)
            dst = (pt * (dpt - delta_ref[0]) * scale).astype(BF16)
            dk_sc[...] += jnp.dot(dst, q, preferred_element_type=F32)
            rows = pl.ds(pl.multiple_of(qi * tk, tk), tk)
            dq_ref[rows, :] += lax.dot_general(dst, kc, tn_dims, preferred_element_type=F32)

        @pl.when(qi > ki)
        def _():
            step(False)

        @pl.when(qi == ki)
        def _():
            step(True)

        @pl.when(qi == nk - 1)
        def _():
            dkv_ref[:, 0:QK_NOPE] = dk_sc[:, 0:QK_NOPE].astype(BF16)
            dkv_ref[:, QK_NOPE:QK_NOPE + V_HEAD] = dv_sc[...].astype(BF16)
            dkpe_ref[0] = dk_sc[:, QK_NOPE:QK_NOPE + LANES]

    qclamp = lambda h, ki, qi: (jnp.maximum(qi, ki), h)
    kmap = lambda blk: (lambda h, ki, qi: (ki, 2 * h + blk))
    rmap = lambda h, ki, qi: (h, 0, jnp.maximum(qi, ki))
    outs, comm_outs = _call(
        body, grid=(heads, nk, nk),
        in_specs=[pl.BlockSpec((tk, HEAD_PAD), qclamp), pl.BlockSpec((tk, QK_NOPE), kmap(0)),
                  pl.BlockSpec((tk, LANES), lambda h, ki, qi: (ki, 0)), pl.BlockSpec((tk, V_HEAD), kmap(1)),
                  pl.BlockSpec((tk, V_HEAD), qclamp), pl.BlockSpec((1, 1, tk), rmap), pl.BlockSpec((1, 1, tk), rmap)],
        out_specs=[pl.BlockSpec((t, HEAD_PAD), lambda h, ki, qi: (0, h)),
                   pl.BlockSpec((tk, HEAD_PAD), lambda h, ki, qi: (ki, h)),
                   pl.BlockSpec((1, tk, LANES), lambda h, ki, qi: (h, ki, 0))],
        out_shape=[jax.ShapeDtypeStruct((t, heads * HEAD_PAD), F32),
                   jax.ShapeDtypeStruct((t, heads * HEAD_PAD), BF16), jax.ShapeDtypeStruct((heads, t, LANES), F32)],
        scratch_shapes=[pltpu.VMEM((tk, HEAD_PAD), F32), pltpu.VMEM((tk, V_HEAD), F32)],
        sem=("parallel", "arbitrary", "arbitrary"), name="attn_bwd",
        args=(q_c, kv, kpe, kv, do, lse_row, delta_row), comm=comm)
    return outs[0], outs[1], outs[2], comm_outs


def _local_step(x, pos_col, target, small, shards, opt):
    d = x.shape[1]
    ql, kvl = small["q_norm_g"].shape[1], small["kv_norm_g"].shape[1]
    sw = small["sgu_norm_g"].shape[1]
    heads = (shards["w_uq"].shape[1] * N_DEV) // (QK_NOPE + QK_ROPE)
    big = {}
    early = ["w_in", "w_uq", "w_ukv"]
    big.update(_compute_layout(dict(zip(early, _all_gather([shards[k] for k in early]))), ql, kvl, heads, sw))
    half = QK_ROPE // 2
    lane = jnp.arange(LANES)
    inv_freq = ROPE_THETA ** (-jnp.arange(0, QK_ROPE, 2, dtype=F32) / QK_ROPE)
    inv_row = inv_freq[lane % half][None, :]
    sign_row = jnp.where((lane % QK_ROPE) < half, -1.0, 1.0).astype(F32)[None, :]
    cos, sin = _rope_tables(pos_col, inv_row, sign_row)
    ws = small["w_sgu"]
    b_col = small["b_sgu_col"]

    def arrived(names, bufs):
        big.update(_compute_layout(dict(zip(names, bufs)), ql, kvl, heads, sw))

    a = _norm_fwd(x, small["norm_mix_g"], "norm_mix_fwd")
    z_lat = _mm(a, big["w_lat"], name="z_lat")
    z_uv = _mm(a, big["w_uv"], name="z_uv")
    mixers = ["w_o_sgu", "w_o_attn"]
    z_g, bufs = _mm(a, big["w_g"], name="z_g", comm=_gather_first([shards[k] for k in mixers]))
    qn, kvn, kpe = _lat_fwd(z_lat, small["q_norm_g"], small["kv_norm_g"], cos, sin, ql, kvl)
    q_p, bufs = _mm(qn, big["w_uq"], name="q_up", comm=_gather_second(bufs))
    arrived(mixers, bufs)
    kv = _mm(kvn, big["w_ukv"], out_dtype=BF16, name="kv_up")
    q_c = _q_rope(q_p, cos, sin, False, "q_rope")
    wide = ["w_out", "w_gate_ffn", "w_up_ffn"]
    attn, lse, bufs = _attn_fwd(q_c, kv, kpe, comm=_gather_first([shards[k] for k in wide]))
    s_out = _sgu_fwd(z_uv, small["sgu_norm_g"], ws, b_col)
    y_sgu = _mm(s_out, big["w_o_sgu"], name="y_sgu")
    y_attn, bufs = _mm(attn, big["w_o_attn"], name="y_attn", comm=_gather_second(bufs))
    arrived(wide, bufs)
    merged = _merge_fwd(y_attn, y_sgu, z_g, small["b_gate"])
    h1 = _mm(merged, big["w_out"], add=x, name="h1")
    f = _norm_fwd(h1, small["norm_ffn_g"], "norm_ffn_fwd")
    down = shards["w_down_ffn"]
    top = _pick(down.shape[0], down.shape[0] // 2, 2 * SUBLANES)
    gate, bufs = _mm(f, big["w_gate"], name="ffn_gate", comm=_gather_first([down], rows=[(0, top)]))
    up, bufs = _mm(f, big["w_up"], name="ffn_up",
                   comm=_gather_first([down], rows=[(top, down.shape[0] - top)], into=bufs))
    act, bufs = _swiglu_fwd(gate, up, comm=_gather_second(bufs))
    arrived(["w_down_ffn"], bufs)
    h2 = _mm(act, big["w_down"], add=h1, name="h2")
    loss_row, dh2, d_norm_final = _loss_head(h2, small["norm_final_g"], target)

    def pair_sums(names, slabs, bufs):
        return [_pair_sum(g, b, "pair_sum_" + k) for k, g, b in zip(names, slabs, bufs)]

    parts, updates = {}, {}

    def update(k, comm=None):
        w, m, v = opt[k]
        updates[k], got = _adamw_shard(parts[k], w, m, v, "adamw_" + k, comm=comm)
        return got

    ffn = gate.shape[1]
    dw_down = _mm(act, dh2, ta=True, out_dtype=BF16, name="dw_down")
    down_slabs = [_slabs_from_rows(dw_down)]
    dact, bufs = _mm(dh2, big["w_down"], tb=True, name="dact", comm=_to_sibling(down_slabs))
    down_pair = pair_sums(["w_down_ffn"], down_slabs, bufs)
    dgu = _swiglu_bwd(gate, up, dact)
    dw_gu, got = _mm(f, dgu, ta=True, out_dtype=BF16, name="dw_gate_up", comm=_to_chips(down_pair))
    parts["w_down_ffn"] = got[0]
    gu_names = ["w_gate_ffn", "w_up_ffn"]
    gu_slabs = [_slabs_from_cols(dw_gu[:, :ffn]), _slabs_from_cols(dw_gu[:, ffn:])]
    df, bufs = _mm(dgu, big["w_gate"], tb=True, name="df_gate", a_cols=(0, ffn), comm=_to_sibling(gu_slabs))
    gu_pairs = pair_sums(gu_names, gu_slabs, bufs)
    df = _mm(dgu, big["w_up"], tb=True, add=df, name="df_up", a_cols=(ffn, ffn))
    dh1, d_norm_ffn = _norm_bwd(h1, small["norm_ffn_g"], df, dh2, "norm_ffn_bwd")
    dw_out = _mm(merged, dh1, ta=True, out_dtype=BF16, name="dw_out")
    out_slabs = [_slabs_from_rows(dw_out)]
    dmerged, bufs = _mm(dh1, big["w_out"], tb=True, name="dmerged", comm=_to_sibling(out_slabs))
    out_pair = pair_sums(["w_out"], out_slabs, bufs)
    dy_attn, dy_sgu, dz_g, d_b_gate = _merge_bwd(dmerged, y_attn, y_sgu, z_g, small["b_gate"])
    dw_o_sgu = _mm(s_out, dy_sgu, ta=True, out_dtype=BF16, name="dw_o_sgu")
    ds_out = _mm(dy_sgu, big["w_o_sgu"], tb=True, name="ds_out")
    dz_uv, d_ws, d_b_col, d_sgu_norm = _sgu_bwd(z_uv, ds_out, small["sgu_norm_g"], ws, b_col)
    dw_o_attn = _mm(attn, dy_attn, ta=True, out_dtype=BF16, name="dw_o_attn")
    mix_names = ["w_o_sgu", "w_o_attn"]
    mix_slabs = [_slabs_from_cols(dw_o_sgu), _slabs_from_rows(dw_o_attn)]
    dattn, bufs = _mm(dy_attn, big["w_o_attn"], tb=True, name="dattn", comm=_to_sibling(mix_slabs))
    mix_pairs = pair_sums(mix_names, mix_slabs, bufs)
    t = lse.shape[1]
    delta = _attn_delta(attn, dattn, heads)
    dq_c, dkv, dkpe_heads, got = _attn_bwd(q_c, kv, kpe, dattn, lse.reshape(heads, 1, t), delta.reshape(heads, 1, t),
                                           comm=_to_chips(gu_pairs))
    parts.update(zip(gu_names, got))
    dq_p = _q_rope(dq_c, cos, sin, True, "q_rope_bwd")
    dw_uq = _mm(qn, dq_p, ta=True, out_dtype=BF16, name="dw_uq")
    dw_ukv = _mm(kvn, dkv, ta=True, out_dtype=BF16, name="dw_ukv")
    dqn = _mm(dq_p, big["w_uq"], tb=True, name="dqn")
    dkvn = _mm(dkv, big["w_ukv"], tb=True, name="dkvn")
    dz_lat, d_q_norm, d_kv_norm = _lat_bwd(z_lat, small["q_norm_g"], small["kv_norm_g"], dqn, dkvn, dkpe_heads,
                                           cos, sin, ql, kvl)
    dw_g, got = _mm(a, dz_g, ta=True, out_dtype=BF16, name="dw_g", comm=_to_chips(out_pair))
    parts["w_out"] = got[0]
    dw_uv, got = _mm(a, dz_uv, ta=True, out_dtype=BF16, name="dw_uv", comm=_to_chips(mix_pairs[1:]))
    parts["w_o_attn"] = got[0]
    dw_lat, got = _mm(a, dz_lat, ta=True, out_dtype=BF16, name="dw_lat", comm=_to_chips(mix_pairs[:1]))
    parts["w_o_sgu"] = got[0]
    lat = ql + kvl + QK_ROPE
    dw_uq_cols = dw_uq.reshape(ql, heads, HEAD_PAD)[:, :, :QK_NOPE + QK_ROPE].reshape(ql, heads * (QK_NOPE + QK_ROPE))
    in_names = ["w_uq", "w_ukv", "w_in"]
    in_slabs = [_slabs_from_cols(dw_uq_cols), _slabs_from_cols(dw_ukv),
                _slabs_from_cols(jnp.concatenate([dw_lat[:, :lat], dw_uv, dw_g], axis=1))]
    da = _mm(dz_lat, big["w_lat"], tb=True, name="da_lat")
    da, bufs = _mm(dz_uv, big["w_uv"], tb=True, add=da, name="da_uv", comm=_to_sibling(in_slabs))
    uq_pair, ukv_pair, in_pair = pair_sums(in_names, in_slabs, bufs)
    rows = in_pair.shape[1]
    chunk = _pick(rows, rows // TAIL_CHUNKS, 2 * SUBLANES)
    chunks = [(r0, chunk) for r0 in range(0, rows, chunk)]
    da, got = _mm(dz_g, big["w_g"], tb=True, add=da, name="da_g",
                  comm=_to_chips([uq_pair, in_pair], rows=[(0, uq_pair.shape[1]), chunks[0]]))
    parts["w_uq"], in_parts = got
    grad_x, d_norm_mix, got = _norm_bwd(x, small["norm_mix_g"], da, dh1, "norm_mix_bwd", comm=_to_chips([ukv_pair]))
    parts["w_ukv"] = got[0]
    hosts = ["w_gate_ffn", "w_up_ffn", "w_down_ffn", "w_out", "w_o_attn", "w_o_sgu", "w_uq", "w_ukv"]
    assert len(chunks) <= 1 + len(hosts)
    for i, k in enumerate(hosts):
        if 1 + i < len(chunks):
            in_parts = update(k, comm=_to_chips([in_pair], rows=[chunks[1 + i]], into=[in_parts]))[0]
        else:
            update(k)
    parts["w_in"] = in_parts
    update("w_in")

    gs = {"norm_mix_g": d_norm_mix, "b_gate": d_b_gate, "q_norm_g": d_q_norm, "kv_norm_g": d_kv_norm,
          "sgu_norm_g": d_sgu_norm, "w_sgu": d_ws, "b_sgu_col": d_b_col, "norm_ffn_g": d_norm_ffn,
          "norm_final_g": d_norm_final}
    return loss_row, grad_x, gs, updates


def _my_place():
    return lax.axis_index("x"), lax.axis_index("y"), lax.axis_index("c")


def _all_gather(shards):
    n = len(shards)

    def body(*refs):
        ins, outs = refs[:n], refs[n:2 * n]
        send_sems, recv_sems, local_sems = refs[2 * n:]
        x, y, c = _my_place()
        me, sibling = (x, y, c), (x, y, 1 - c)
        chips = [(1 - x, y), (x, 1 - y), (1 - x, 1 - y)]

        def slab(w, place):
            return outs[w].at[4 * place[0] + 2 * place[1] + place[2]]

        def copy(w, k, place, to, src=None):
            return pltpu.make_async_remote_copy(
                src_ref=slab(w, place) if src is None else src, dst_ref=slab(w, place),
                send_sem=send_sems.at[w, k], recv_sem=recv_sems.at[w, k], device_id=to, device_id_type=MESH)

        mine = [pltpu.make_async_copy(ins[w], slab(w, me), local_sems.at[w]) for w in range(n)]
        for cp in mine:
            cp.start()
        started = []
        for w in range(n):
            first = [copy(w, 0, me, sibling, src=ins[w])]
            first += [copy(w, 1 + j, me, (*chip, c), src=ins[w]) for j, chip in enumerate(chips)]
            for cp in first:
                cp.start()
            started += first
        for w in range(n):
            for j, chip in enumerate(chips):
                copy(w, 1 + j, (*chip, c), me).wait_recv()
                fwd = copy(w, 4 + j, (*chip, c), sibling)
                fwd.start()
                started.append(fwd)
        for w in range(n):
            copy(w, 0, sibling, me).wait_recv()
            for j, chip in enumerate(chips):
                copy(w, 4 + j, (*chip, 1 - c), me).wait_recv()
        for cp in started:
            cp.wait_send()
        for cp in mine:
            cp.wait()

    any_spec = pl.BlockSpec(memory_space=pl.ANY)
    return pl.pallas_call(
        body, in_specs=[any_spec] * n, out_specs=[any_spec] * n,
        out_shape=[jax.ShapeDtypeStruct((N_DEV,) + s.shape, s.dtype) for s in shards],
        scratch_shapes=[pltpu.SemaphoreType.DMA((n, 7)), pltpu.SemaphoreType.DMA((n, 7)), pltpu.SemaphoreType.DMA((n,))],
        compiler_params=pltpu.CompilerParams(has_side_effects=True), name="all_gather_weights")(*shards)


N_CHIPS = N_DEV // 2


def _gather_first(shards, rows=None, into=None):
    n = len(shards)
    rows = rows or [(0, s.shape[0]) for s in shards]

    def copies(ins, outs, sems):
        x, y, c = _my_place()
        send_sems, recv_sems, local_sems = sems
        me = 4 * x + 2 * y + c
        targets = [(x, y, 1 - c), (1 - x, y, c), (x, 1 - y, c), (1 - x, 1 - y, c)]
        out = []
        for w in range(n):
            r0, nr = rows[w]
            src, dst = ins[w].at[pl.ds(r0, nr)], outs[w].at[me, pl.ds(r0, nr)]
            out.append(pltpu.make_async_copy(src, dst, local_sems.at[w]))
            out += [pltpu.make_async_remote_copy(src_ref=src, dst_ref=dst, send_sem=send_sems.at[w, k],
                                                 recv_sem=recv_sems.at[w, k], device_id=to, device_id_type=MESH)
                    for k, to in enumerate(targets)]
        return out

    def start(ins, outs, sems):
        for cp in copies(ins, outs, sems):
            cp.start()

    def finish(ins, outs, sems):
        for cp in copies(ins, outs, sems):
            cp.wait()

    return _Comm(list(shards) + list(into or []), [jax.ShapeDtypeStruct((N_DEV,) + s.shape, s.dtype) for s in shards],
                 [pltpu.SemaphoreType.DMA((n, 4)), pltpu.SemaphoreType.DMA((n, 4)), pltpu.SemaphoreType.DMA((n,))],
                 start, finish, aliases={n + w: w for w in range(n)} if into else None)


def _gather_second(bufs):
    n = len(bufs)

    def copies(ins, outs, sems):
        x, y, c = _my_place()
        send_sems, recv_sems = sems
        out = []
        for w in range(n):
            for j, (cx, cy) in enumerate([(1 - x, y), (x, 1 - y), (1 - x, 1 - y)]):
                slab = 4 * cx + 2 * cy + c
                out.append(pltpu.make_async_remote_copy(
                    src_ref=ins[w].at[slab], dst_ref=outs[w].at[slab], send_sem=send_sems.at[w, j],
                    recv_sem=recv_sems.at[w, j], device_id=(x, y, 1 - c), device_id_type=MESH))
        return out

    def start(ins, outs, sems):
        for cp in copies(ins, outs, sems):
            cp.start()

    def finish(ins, outs, sems):
        for cp in copies(ins, outs, sems):
            cp.wait()

    return _Comm(bufs, [jax.ShapeDtypeStruct(b.shape, b.dtype) for b in bufs],
                 [pltpu.SemaphoreType.DMA((n, 3)), pltpu.SemaphoreType.DMA((n, 3))], start, finish,
                 aliases={w: w for w in range(n)})


def _to_sibling(grads):
    n = len(grads)

    def copies(ins, outs, sems):
        x, y, c = _my_place()
        send_sems, recv_sems = sems
        return [pltpu.make_async_remote_copy(
            src_ref=ins[w].at[2 * i + (1 - c)], dst_ref=outs[w].at[i], send_sem=send_sems.at[w, i],
            recv_sem=recv_sems.at[w, i], device_id=(x, y, 1 - c), device_id_type=MESH)
            for w in range(n) for i in range(N_CHIPS)]

    def start(ins, outs, sems):
        for cp in copies(ins, outs, sems):
            cp.start()

    def finish(ins, outs, sems):
        for cp in copies(ins, outs, sems):
            cp.wait()

    return _Comm(grads, [jax.ShapeDtypeStruct((N_CHIPS,) + g.shape[1:], g.dtype) for g in grads],
                 [pltpu.SemaphoreType.DMA((n, N_CHIPS)), pltpu.SemaphoreType.DMA((n, N_CHIPS))], start, finish)


def _to_chips(parts, rows=None, into=None):
    n = len(parts)
    rows = rows or [(0, p.shape[1]) for p in parts]

    def copies(ins, outs, sems):
        x, y, c = _my_place()
        send_sems, recv_sems, local_sems = sems
        mine = 2 * x + y
        chips = [(1 - x, y), (x, 1 - y), (1 - x, 1 - y)]
        remote = [pltpu.make_async_remote_copy(
            src_ref=ins[w].at[2 * cx + cy, pl.ds(*rows[w])], dst_ref=outs[w].at[mine, pl.ds(*rows[w])],
            send_sem=send_sems.at[w, j], recv_sem=recv_sems.at[w, j], device_id=(cx, cy, c), device_id_type=MESH)
            for w in range(n) for j, (cx, cy) in enumerate(chips)]
        local = [pltpu.make_async_copy(ins[w].at[mine, pl.ds(*rows[w])], outs[w].at[mine, pl.ds(*rows[w])],
                                       local_sems.at[w]) for w in range(n)]
        return remote + local

    def start(ins, outs, sems):
        for cp in copies(ins, outs, sems):
            cp.start()

    def finish(ins, outs, sems):
        for cp in copies(ins, outs, sems):
            cp.wait()

    return _Comm(list(parts) + list(into or []), [jax.ShapeDtypeStruct(p.shape, p.dtype) for p in parts],
                 [pltpu.SemaphoreType.DMA((n, N_CHIPS - 1)), pltpu.SemaphoreType.DMA((n, N_CHIPS - 1)),
                  pltpu.SemaphoreType.DMA((n,))], start, finish,
                 aliases={n + w: w for w in range(n)} if into else None)


def _pair_sum(g, buf, name):
    _, r, c = g.shape
    tr = _row_tile(r, c, 2)
    core = lax.axis_index("c").astype(jnp.int32).reshape(1)

    def body(core_ref, g_ref, b_ref, o_ref):
        o_ref[...] = (g_ref[...].astype(F32) + b_ref[...].astype(F32)).astype(o_ref.dtype)

    blk = (1, tr, c)
    return pl.pallas_call(
        body, grid_spec=pltpu.PrefetchScalarGridSpec(
            num_scalar_prefetch=1, grid=(N_CHIPS, r // tr),
            in_specs=[pl.BlockSpec(blk, lambda i, j, core_ref: (2 * i + core_ref[0], j, 0)),
                      pl.BlockSpec(blk, lambda i, j, core_ref: (i, j, 0))],
            out_specs=pl.BlockSpec(blk, lambda i, j, core_ref: (i, j, 0))),
        out_shape=jax.ShapeDtypeStruct(buf.shape, buf.dtype),
        compiler_params=_params(("parallel", "parallel")), name=name)(core, g, buf)


def _all_reduce_pack(pack):
    r = pack.shape[0]

    def body(x_ref, out_ref, gath_ref, send_sems, recv_sems, local_sem):
        x, y, c = _my_place()
        me, sibling = (x, y, c), (x, y, 1 - c)
        chips = [(1 - x, y), (x, 1 - y), (1 - x, 1 - y)]

        def slab(place):
            return gath_ref.at[4 * place[0] + 2 * place[1] + place[2]]

        def copy(k, place, to, src=None):
            return pltpu.make_async_remote_copy(
                src_ref=slab(place) if src is None else src, dst_ref=slab(place),
                send_sem=send_sems.at[k], recv_sem=recv_sems.at[k], device_id=to, device_id_type=MESH)

        mine = pltpu.make_async_copy(x_ref, slab(me), local_sem)
        mine.start()
        first = [copy(0, me, sibling, src=x_ref)]
        first += [copy(1 + j, me, (*chip, c), src=x_ref) for j, chip in enumerate(chips)]
        for cp in first:
            cp.start()
        passed = [copy(4 + j, (*chip, c), sibling) for j, chip in enumerate(chips)]
        for j, chip in enumerate(chips):
            copy(1 + j, (*chip, c), me).wait_recv()
            passed[j].start()
        copy(0, sibling, me).wait_recv()
        for j, chip in enumerate(chips):
            copy(4 + j, (*chip, 1 - c), me).wait_recv()
        for cp in first + passed:
            cp.wait_send()
        mine.wait()
        acc = gath_ref[0]
        for i in range(1, N_DEV):
            acc = acc + gath_ref[i]
        out_ref[...] = acc

    vmem = pl.BlockSpec(memory_space=pltpu.VMEM)
    return pl.pallas_call(
        body, in_specs=[vmem], out_specs=vmem, out_shape=jax.ShapeDtypeStruct(pack.shape, F32),
        scratch_shapes=[pltpu.VMEM((N_DEV, r, LANES), F32), pltpu.SemaphoreType.DMA((7,)),
                        pltpu.SemaphoreType.DMA((7,)), pltpu.SemaphoreType.DMA],
        compiler_params=pltpu.CompilerParams(vmem_limit_bytes=VMEM_LIMIT), name="all_reduce_small")(pack)


def _adamw_math(w, g, m, v):
    m = ADAM_B1 * m + (1.0 - ADAM_B1) * g
    v = ADAM_B2 * v + (1.0 - ADAM_B2) * (g * g)
    m_hat = m / (1.0 - ADAM_B1 ** ADAM_STEP)
    v_hat = v / (1.0 - ADAM_B2 ** ADAM_STEP)
    delta = -ADAM_LR * (m_hat / (jnp.sqrt(v_hat) + ADAM_EPS) + ADAM_WD * w)
    return delta, m, v


def _adamw_shard(parts, w, m, v, name, comm=None):
    r, c = w.shape
    n_parts = parts.shape[0]
    tr = _pick(r, max(2 * SUBLANES, (256 * 1024) // c), 2 * SUBLANES)

    def body(p_ref, w_ref, m_ref, v_ref, g_ref, d_ref, nm_ref, nv_ref):
        g = p_ref[0].astype(F32)
        for i in range(1, n_parts):
            g = g + p_ref[i].astype(F32)
        g_ref[...] = g
        d_ref[...], nm_ref[...], nv_ref[...] = _adamw_math(w_ref[...], g, m_ref[...], v_ref[...])

    spec = pl.BlockSpec((tr, c), lambda i: (i, 0))
    outs, comm_outs = _call(
        body, grid=(r // tr,), in_specs=[pl.BlockSpec((n_parts, tr, c), lambda i: (0, i, 0)), spec, spec, spec],
        out_specs=[spec] * 4, out_shape=[jax.ShapeDtypeStruct((r, c), F32)] * 4,
        sem=("parallel",), name=name, args=(parts, w, m, v), comm=comm)
    return outs, comm_outs


def _adamw_pack(g, w, m, v):
    r, c = w.shape

    def body(g_ref, w_ref, m_ref, v_ref, d_ref, nm_ref, nv_ref):
        d_ref[...], nm_ref[...], nv_ref[...] = _adamw_math(w_ref[...], g_ref[...], m_ref[...], v_ref[...])

    return pl.pallas_call(
        body, in_specs=[_full((r, c))] * 4, out_specs=[_full((r, c))] * 3, grid=(1,),
        out_shape=[jax.ShapeDtypeStruct((r, c), F32)] * 3,
        compiler_params=_params(("arbitrary",)), name="adamw_small")(g, w, m, v)


def _cols_from_slabs(g):
    return jnp.transpose(g, (1, 0, 2)).reshape(g.shape[1], N_DEV * g.shape[2])


def _slabs_from_cols(w):
    r, c8 = w.shape
    return jnp.transpose(w.reshape(r, N_DEV, c8 // N_DEV), (1, 0, 2))


def _rows_from_slabs(g):
    return g.reshape(N_DEV * g.shape[1], g.shape[2])


def _slabs_from_rows(w):
    return w.reshape(N_DEV, w.shape[0] // N_DEV, w.shape[1])


def _compute_layout(gathered, ql, kvl, heads, sw):
    out = {}
    for k, g in gathered.items():
        if k == "w_in":
            lat = ql + kvl + QK_ROPE
            w_in_full = _cols_from_slabs(g)
            out["w_lat"] = jnp.pad(w_in_full[:, :lat], ((0, 0), (0, LANES - QK_ROPE)))
            out["w_uv"] = w_in_full[:, lat:lat + 2 * sw]
            out["w_g"] = w_in_full[:, lat + 2 * sw:]
        elif k == "w_uq":
            per_head = _cols_from_slabs(g).reshape(ql, heads, QK_NOPE + QK_ROPE)
            pad = HEAD_PAD - QK_NOPE - QK_ROPE
            out["w_uq"] = jnp.pad(per_head, ((0, 0), (0, 0), (0, pad))).reshape(ql, heads * HEAD_PAD)
        elif k in ("w_o_attn", "w_out", "w_down_ffn"):
            out[k.removesuffix("_ffn")] = _rows_from_slabs(g)
        else:
            out[k.removesuffix("_ffn")] = _cols_from_slabs(g)
    return out


_SMALL =["norm_mix_g", "b_gate", "q_norm_g", "kv_norm_g", "sgu_norm_g", "w_sgu", "b_sgu", "norm_ffn_g", "norm_final_g"]
_BIG = ["w_in", "w_uq", "w_ukv", "w_o_attn", "w_o_sgu", "w_out", "w_gate_ffn", "w_up_ffn", "w_down_ffn"]
_ORDER = ["norm_mix_g", "w_in", "b_gate", "q_norm_g", "w_uq", "kv_norm_g", "w_ukv", "w_o_attn", "sgu_norm_g", "w_sgu",
          "b_sgu", "w_o_sgu", "w_out", "norm_ffn_g", "w_gate_ffn", "w_up_ffn", "w_down_ffn", "norm_final_g"]


def _pack_rows(parts):
    rows, sizes = [], []
    for p in parts:
        flat = p.reshape(-1)
        n = flat.shape[0]
        padded = -(-n // (SUBLANES * LANES)) * (SUBLANES * LANES)
        rows.append(jnp.pad(flat, (0, padded - n)).reshape(padded // LANES, LANES))
        sizes.append((n, padded // LANES))
    return jnp.concatenate(rows, axis=0), sizes


def _unpack_rows(pack, sizes, shapes):
    out, r0 = [], 0
    for (n, nr), shp in zip(sizes, shapes):
        out.append(pack[r0:r0 + nr].reshape(-1)[:n].reshape(shp))
        r0 += nr
    return out


def kernel(x, positions, norm_mix_g, w_in, b_gate, q_norm_g, w_uq, kv_norm_g, w_ukv, w_o_attn, sgu_norm_g, w_sgu, b_sgu, w_o_sgu, w_out, norm_ffn_g, w_gate_ffn, w_up_ffn, w_down_ffn, norm_final_g, loss_target, m_norm_mix_g, m_w_in, m_b_gate, m_q_norm_g, m_w_uq, m_kv_norm_g, m_w_ukv, m_w_o_attn, m_sgu_norm_g, m_w_sgu, m_b_sgu, m_w_o_sgu, m_w_out, m_norm_ffn_g, m_w_gate_ffn, m_w_up_ffn, m_w_down_ffn, m_norm_final_g, v_norm_mix_g, v_w_in, v_b_gate, v_q_norm_g, v_w_uq, v_kv_norm_g, v_w_ukv, v_w_o_attn, v_sgu_norm_g, v_w_sgu, v_b_sgu, v_w_o_sgu, v_w_out, v_norm_ffn_g, v_w_gate_ffn, v_w_up_ffn, v_w_down_ffn, v_norm_final_g):
    wts = dict(norm_mix_g=norm_mix_g, w_in=w_in, b_gate=b_gate, q_norm_g=q_norm_g, w_uq=w_uq, kv_norm_g=kv_norm_g,
               w_ukv=w_ukv, w_o_attn=w_o_attn, sgu_norm_g=sgu_norm_g, w_sgu=w_sgu, b_sgu=b_sgu, w_o_sgu=w_o_sgu,
               w_out=w_out, norm_ffn_g=norm_ffn_g, w_gate_ffn=w_gate_ffn, w_up_ffn=w_up_ffn, w_down_ffn=w_down_ffn,
               norm_final_g=norm_final_g)
    mom = dict(norm_mix_g=m_norm_mix_g, w_in=m_w_in, b_gate=m_b_gate, q_norm_g=m_q_norm_g, w_uq=m_w_uq,
               kv_norm_g=m_kv_norm_g, w_ukv=m_w_ukv, w_o_attn=m_w_o_attn, sgu_norm_g=m_sgu_norm_g, w_sgu=m_w_sgu,
               b_sgu=m_b_sgu, w_o_sgu=m_w_o_sgu, w_out=m_w_out, norm_ffn_g=m_norm_ffn_g, w_gate_ffn=m_w_gate_ffn,
               w_up_ffn=m_w_up_ffn, w_down_ffn=m_w_down_ffn, norm_final_g=m_norm_final_g)
    var = dict(norm_mix_g=v_norm_mix_g, w_in=v_w_in, b_gate=v_b_gate, q_norm_g=v_q_norm_g, w_uq=v_w_uq,
               kv_norm_g=v_kv_norm_g, w_ukv=v_w_ukv, w_o_attn=v_w_o_attn, sgu_norm_g=v_sgu_norm_g, w_sgu=v_w_sgu,
               b_sgu=v_b_sgu, w_o_sgu=v_w_o_sgu, w_out=v_w_out, norm_ffn_g=v_norm_ffn_g, w_gate_ffn=v_w_gate_ffn,
               w_up_ffn=v_w_up_ffn, w_down_ffn=v_w_down_ffn, norm_final_g=v_norm_final_g)

    t, d = x.shape[1], x.shape[2]
    ql, kvl = q_norm_g.shape[1], kv_norm_g.shape[1]
    heads = (w_uq.shape[2] * N_DEV) // (QK_NOPE + QK_ROPE)
    sw = sgu_norm_g.shape[1]

    shards = {k: wts[k][0].astype(BF16) for k in _BIG}
    small = {
        "norm_mix_g": norm_mix_g, "b_gate": b_gate, "q_norm_g": q_norm_g, "kv_norm_g": kv_norm_g,
        "sgu_norm_g": sgu_norm_g, "w_sgu": w_sgu[0], "b_sgu_col": b_sgu[0][:, :, None], "norm_ffn_g": norm_ffn_g,
        "norm_final_g": norm_final_g[None, :],
    }

    opt = {k: (wts[k][0], mom[k][0], var[k][0]) for k in _BIG}
    loss_row, grad_x, gs, updates = _local_step(x[0], positions.reshape(t, 1), loss_target[0], small, shards, opt)
    grads, deltas, new_m, new_v = {}, {}, {}, {}
    for k in _BIG:
        grads[k], deltas[k], new_m[k], new_v[k] = (a.reshape(wts[k].shape) for a in updates[k])

    small_grads = [gs["norm_mix_g"], gs["b_gate"], gs["q_norm_g"], gs["kv_norm_g"], gs["sgu_norm_g"], gs["w_sgu"],
                   gs["b_sgu_col"], gs["norm_ffn_g"], gs["norm_final_g"]]
    pack, sizes = _pack_rows([loss_row] + small_grads)
    total = _all_reduce_pack(pack)
    shapes = [(1, LANES)] + [wts[k].shape for k in _SMALL]
    unpacked = _unpack_rows(total, sizes, shapes)
    loss = unpacked[0][0, 0]
    for k, g in zip(_SMALL, unpacked[1:]):
        grads[k] = g
    g_pack = total[sizes[0][1]:]
    w_pack, _ = _pack_rows([wts[k] for k in _SMALL])
    m_pack, _ = _pack_rows([mom[k] for k in _SMALL])
    v_pack, _ = _pack_rows([var[k] for k in _SMALL])
    d_pack, nm_pack, nv_pack = _adamw_pack(g_pack, w_pack, m_pack, v_pack)
    small_shapes = [wts[k].shape for k in _SMALL]
    for store, pk in ((deltas, d_pack), (new_m, nm_pack), (new_v, nv_pack)):
        for k, a in zip(_SMALL, _unpack_rows(pk, sizes[1:], small_shapes)):
            store[k] = a

    return (loss, grad_x[None], *[grads[k] for k in _ORDER], *[deltas[k] for k in _ORDER],
            *[new_m[k] for k in _ORDER], *[new_v[k] for k in _ORDER])
```

```python
import functools
import math

import jax
import jax.numpy as jnp
from jax import lax
from jax.experimental import pallas as pl
from jax.experimental.pallas import tpu as pltpu

F32 = jnp.float32
BF16 = jnp.bfloat16

N_DEV = 8
N_HEADS = 16
QK_NOPE = 128
QK_ROPE = 64
V_HEAD = 128
HEAD_PAD = 256
ROPE_THETA = 10000.0
CHUNK = 128
SGU_GROUP = 128
RMS_EPS = 1e-6
LANES = 128
SUBLANES = 8

ADAM_LR = 0.001
ADAM_B1 = 0.9
ADAM_B2 = 0.999
ADAM_EPS = 1e-08
ADAM_WD = 0.01
ADAM_STEP = 10

VMEM_LIMIT = 48 * 1024 * 1024
MM_TILE = (1024, 512, 2048)
ATTN_TILE = 512
ROW_KERNEL_BYTES = 24 * 1024 * 1024
TAIL_CHUNKS = 4
NEG_BIG = -1e30
MESH = pl.DeviceIdType.MESH


def _pick(n, target, mult=LANES):
    best = None
    d = mult
    while d <= min(n, target):
        if n % d == 0:
            best = d
        d += mult
    return best or n


def _row_tile(t, width, n_blocks, mult=2 * SUBLANES):
    return _pick(t, max(mult, ROW_KERNEL_BYTES // (3 * n_blocks * width * 4)), mult)


def _params(sem):
    return pltpu.CompilerParams(dimension_semantics=sem, vmem_limit_bytes=VMEM_LIMIT)


def _full(shape):
    nd = len(shape)
    return pl.BlockSpec(shape, lambda *_: (0,) * nd)


def _rows(tr, w, cb=0):
    return pl.BlockSpec((tr, w), lambda i: (i, cb))


class _Comm:
    def __init__(self, ins, out_shapes, sems, start, finish, aliases=None):
        self.ins, self.out_shapes, self.sems, self.start, self.finish = list(ins), list(out_shapes), list(sems), start, finish
        self.aliases = dict(aliases or {})


def _call(body, *, grid, in_specs, out_specs, out_shape, scratch_shapes=(), sem, name, args, comm=None):
    if comm is None:
        outs = pl.pallas_call(body, grid=grid, in_specs=list(in_specs), out_specs=list(out_specs),
                              out_shape=list(out_shape), scratch_shapes=list(scratch_shapes),
                              compiler_params=_params(sem), name=name)(*args)
        return list(outs), []
    n_in, n_out, n_sc = len(in_specs), len(out_shape), len(scratch_shapes)
    nci, nco = len(comm.ins), len(comm.out_shapes)

    def hosted(*refs):
        ins, refs = refs[:n_in], refs[n_in:]
        cins, refs = refs[:nci], refs[nci:]
        outs, refs = refs[:n_out], refs[n_out:]
        couts, refs = refs[:nco], refs[nco:]
        scratch, csems = refs[:n_sc], refs[n_sc:]
        ids = [pl.program_id(i) for i in range(len(grid))]
        first = functools.reduce(jnp.logical_and, [i == 0 for i in ids])
        last = functools.reduce(jnp.logical_and, [i == g - 1 for i, g in zip(ids, grid)])

        @pl.when(first)
        def _():
            comm.start(cins, couts, csems)

        body(*ins, *outs, *scratch)

        @pl.when(last)
        def _():
            comm.finish(cins, couts, csems)

    any_spec = pl.BlockSpec(memory_space=pl.ANY)
    res = pl.pallas_call(
        hosted, grid=grid, in_specs=list(in_specs) + [any_spec] * nci, out_specs=list(out_specs) + [any_spec] * nco,
        out_shape=list(out_shape) + comm.out_shapes, scratch_shapes=list(scratch_shapes) + comm.sems,
        input_output_aliases={n_in + i: n_out + o for i, o in comm.aliases.items()},
        compiler_params=pltpu.CompilerParams(dimension_semantics=("arbitrary",) * len(grid),
                                             vmem_limit_bytes=VMEM_LIMIT, has_side_effects=True),
        name=name)(*args, *comm.ins)
    return list(res[:n_out]), list(res[n_out:])


def _mm(a, b, *, ta=False, tb=False, add=None, out_dtype=F32, tm=None, tn=None, tk=None, name, comm=None,
        slab=None, a_slab0=0):
    sq = None
    if slab is None:
        m, k = (a.shape[1], a.shape[0]) if ta else a.shape
        n = b.shape[0] if tb else b.shape[1]
        assert k == (b.shape[1] if tb else b.shape[0]), (a.shape, b.shape, ta, tb)
        tm, tn, tk = _pick(m, tm or MM_TILE[0]), _pick(n, tn or MM_TILE[1]), _pick(k, tk or MM_TILE[2])
        grid = (m // tm, n // tn, k // tk)
        a_spec = pl.BlockSpec((tk, tm), lambda i, j, kk: (kk, i)) if ta else pl.BlockSpec((tm, tk), lambda i, j, kk: (i, kk))
        b_spec = pl.BlockSpec((tn, tk), lambda i, j, kk: (j, kk)) if tb else pl.BlockSpec((tk, tn), lambda i, j, kk: (kk, j))
        o_spec, o_shape = pl.BlockSpec((tm, tn), lambda i, j, kk: (i, j)), (m, n)
    elif slab == "n":
        m, k = (a.shape[1], a.shape[0]) if ta else a.shape
        s, c = b.shape[0], (b.shape[1] if tb else b.shape[2])
        assert k == (b.shape[2] if tb else b.shape[1]), (a.shape, b.shape, ta, tb)
        tm, tn, tk = _pick(m, tm or MM_TILE[0]), c, _pick(k, tk or MM_TILE[2])
        grid = (m // tm, s, k // tk)
        a_spec = pl.BlockSpec((tk, tm), lambda i, j, kk: (kk, i)) if ta else pl.BlockSpec((tm, tk), lambda i, j, kk: (i, kk))
        b_spec = (pl.BlockSpec((sq, c, tk), lambda i, j, kk: (j, 0, kk)) if tb
                  else pl.BlockSpec((sq, tk, c), lambda i, j, kk: (j, kk, 0)))
        o_spec, o_shape = pl.BlockSpec((sq, tm, c), lambda i, j, kk: (j, i, 0)), (s, m, c)
    elif slab == "m":
        assert ta and not tb
        s, k, c = a.shape
        n = b.shape[1]
        assert k == b.shape[0], (a.shape, b.shape)
        tm, tn, tk = c, _pick(n, tn or MM_TILE[1]), _pick(k, tk or MM_TILE[2])
        grid = (s, n // tn, k // tk)
        a_spec = pl.BlockSpec((sq, tk, c), lambda i, j, kk: (i, kk, 0))
        b_spec = pl.BlockSpec((tk, tn), lambda i, j, kk: (kk, j))
        o_spec, o_shape = pl.BlockSpec((sq, c, tn), lambda i, j, kk: (i, 0, j)), (s, c, n)
    else:
        assert slab == "k" and not ta
        s, c = b.shape[0], (b.shape[2] if tb else b.shape[1])
        m, n = a.shape[1], (b.shape[1] if tb else b.shape[2])
        assert a.shape[2] == c and a.shape[0] >= a_slab0 + s, (a.shape, b.shape, a_slab0)
        tm, tn, tk = _pick(m, tm or MM_TILE[0]), _pick(n, tn or MM_TILE[1]), c
        grid = (m // tm, n // tn, s)
        a_spec = pl.BlockSpec((sq, tm, c), lambda i, j, kk: (kk + a_slab0, i, 0))
        b_spec = (pl.BlockSpec((sq, tn, c), lambda i, j, kk: (kk, j, 0)) if tb
                  else pl.BlockSpec((sq, c, tn), lambda i, j, kk: (kk, 0, j)))
        o_spec, o_shape = pl.BlockSpec((tm, tn), lambda i, j, kk: (i, j)), (m, n)
    nk = grid[2]
    dims = (((0 if ta else 1,), (1 if tb else 0,)), ((), ()))

    def body(*refs):
        if add is None:
            a_ref, b_ref, o_ref, acc_ref = refs
            add_ref = None
        else:
            a_ref, b_ref, add_ref, o_ref, acc_ref = refs
        kk = pl.program_id(2)

        @pl.when(kk == 0)
        def _():
            acc_ref[...] = jnp.zeros_like(acc_ref)

        acc_ref[...] += lax.dot_general(a_ref[...].astype(BF16), b_ref[...].astype(BF16), dims,
                                        preferred_element_type=F32)

        @pl.when(kk == nk - 1)
        def _():
            r = acc_ref[...]
            if add_ref is not None:
                r = r + add_ref[...].astype(F32)
            o_ref[...] = r.astype(o_ref.dtype)

    in_specs = [a_spec, b_spec] + ([o_spec] if add is not None else [])
    args = (a, b) + ((add,) if add is not None else ())
    outs, comm_outs = _call(
        body, grid=grid, in_specs=in_specs, out_specs=[o_spec],
        out_shape=[jax.ShapeDtypeStruct(o_shape, out_dtype)], scratch_shapes=[pltpu.VMEM((tm, tn), F32)],
        sem=("parallel", "parallel", "arbitrary"), name=name, args=args, comm=comm)
    return outs[0] if comm is None else (outs[0], comm_outs)


def _rms_scale(x):
    return lax.rsqrt(jnp.mean(x * x, axis=-1, keepdims=True) + RMS_EPS)


def _rms_bwd(xhat, r, g, dy):
    t = dy * g
    dx = r * (t - xhat * jnp.mean(t * xhat, axis=-1, keepdims=True))
    return dx, dy * xhat


_GELU_C = math.sqrt(2.0 / math.pi)


def _gelu(x):
    return x * (0.5 * (1.0 + jnp.tanh(_GELU_C * (x + 0.044715 * (x * x * x)))))


def _gelu_and_grad(x):
    t = jnp.tanh(_GELU_C * (x + 0.044715 * (x * x * x)))
    cdf = 0.5 * (1.0 + t)
    return x * cdf, cdf + x * (0.5 * (1.0 - t * t) * (_GELU_C * (1.0 + 3.0 * 0.044715 * (x * x))))


def _sigmoid(x):
    return 1.0 / (1.0 + jnp.exp(-x))


def _swap_halves(x):
    lane = lax.broadcasted_iota(jnp.int32, x.shape, 1)
    first = (lane % QK_ROPE) < (QK_ROPE // 2)
    return jnp.where(first, pltpu.roll(x, LANES - QK_ROPE // 2, 1), pltpu.roll(x, QK_ROPE // 2, 1))


def _rope(x, cos, sin_signed):
    return x * cos + _swap_halves(x) * sin_signed


def _rope_bwd(d, cos, sin_signed):
    return d * cos + _swap_halves(d * sin_signed)


def _rope_tables(pos_col, inv_freq_row, sign_row):
    t = pos_col.shape[0]
    tr = _pick(t, 512, SUBLANES)

    def body(p_ref, f_ref, s_ref, cos_ref, sin_ref):
        ang = p_ref[...].astype(F32) * f_ref[...]
        cos_ref[...] = jnp.cos(ang)
        sin_ref[...] = jnp.sin(ang) * s_ref[...]

    return pl.pallas_call(
        body, grid=(t // tr,), in_specs=[_rows(tr, 1), _full((1, LANES)), _full((1, LANES))],
        out_specs=[_rows(tr, LANES), _rows(tr, LANES)],
        out_shape=[jax.ShapeDtypeStruct((t, LANES), F32)] * 2,
        compiler_params=_params(("parallel",)), name="rope_tables")(pos_col, inv_freq_row, sign_row)


def _norm_fwd(x, g, name):
    t, d = x.shape
    tr = _row_tile(t, d, 2)

    def body(x_ref, g_ref, y_ref):
        xv = x_ref[...]
        y_ref[...] = (xv * _rms_scale(xv) * g_ref[...]).astype(BF16)

    return pl.pallas_call(
        body, grid=(t // tr,), in_specs=[_rows(tr, d), _full((1, d))], out_specs=_rows(tr, d),
        out_shape=jax.ShapeDtypeStruct((t, d), BF16), compiler_params=_params(("parallel",)), name=name)(x, g)


def _lat_fwd(z_lat, qg, kvg, cos, sin, ql, kvl):
    t = z_lat.shape[0]
    tr = _row_tile(t, z_lat.shape[1], 2)

    def body(z_ref, qg_ref, kvg_ref, cos_ref, sin_ref, qn_ref, kvn_ref, kpe_ref):
        q = z_ref[:, 0:ql]
        qn_ref[...] = (q * _rms_scale(q) * qg_ref[...]).astype(BF16)
        kv = z_ref[:, ql:ql + kvl]
        kvn_ref[...] = (kv * _rms_scale(kv) * kvg_ref[...]).astype(BF16)
        kpe_ref[...] = _rope(z_ref[:, ql + kvl:ql + kvl + LANES], cos_ref[...], sin_ref[...]).astype(BF16)

    w = z_lat.shape[1]
    return pl.pallas_call(
        body, grid=(t // tr,),
        in_specs=[_rows(tr, w), _full((1, ql)), _full((1, kvl)), _rows(tr, LANES), _rows(tr, LANES)],
        out_specs=[_rows(tr, ql), _rows(tr, kvl), _rows(tr, LANES)],
        out_shape=[jax.ShapeDtypeStruct((t, ql), BF16), jax.ShapeDtypeStruct((t, kvl), BF16),
                   jax.ShapeDtypeStruct((t, LANES), BF16)],
        compiler_params=_params(("parallel",)), name="lat_fwd")(z_lat, qg, kvg, cos, sin)


def _q_rope(q_p, cos, sin, bwd, name):
    t, w = q_p.shape
    tr = _row_tile(t, w, 2)
    fn = _rope_bwd if bwd else _rope

    def body(q_ref, cos_ref, sin_ref, o_ref):
        c, s = cos_ref[...], sin_ref[...]
        for h in range(w // HEAD_PAD):
            o_ref[:, h * HEAD_PAD:h * HEAD_PAD + QK_NOPE] = q_ref[:, h * HEAD_PAD:h * HEAD_PAD + QK_NOPE].astype(BF16)
            lo = h * HEAD_PAD + QK_NOPE
            o_ref[:, lo:lo + LANES] = fn(q_ref[:, lo:lo + LANES].astype(F32), c, s).astype(BF16)

    return pl.pallas_call(
        body, grid=(t // tr,), in_specs=[_rows(tr, w), _rows(tr, LANES), _rows(tr, LANES)], out_specs=_rows(tr, w),
        out_shape=jax.ShapeDtypeStruct((t, w), BF16), compiler_params=_params(("parallel",)), name=name)(q_p, cos, sin)


def _tril_mask():
    r = lax.broadcasted_iota(jnp.int32, (CHUNK, CHUNK), 0)
    c = lax.broadcasted_iota(jnp.int32, (CHUNK, CHUNK), 1)
    return r >= c


def _sgu_fwd(z_uv, gs, ws, b_col):
    t = z_uv.shape[0]
    sw = z_uv.shape[1] // 2
    groups = sw // SGU_GROUP
    tr = _pick(t, 256, CHUNK)

    def body(u_ref, v_ref, gs_ref, ws_ref, b_ref, o_ref):
        v = _gelu(v_ref[...])
        vn = (v * _rms_scale(v) * gs_ref[...]).astype(BF16)
        tri = _tril_mask()
        for g in range(groups):
            wg = jnp.where(tri, ws_ref[g], 0.0).astype(BF16)
            cols = slice(g * SGU_GROUP, (g + 1) * SGU_GROUP)
            for c in range(tr // CHUNK):
                rows = slice(c * CHUNK, (c + 1) * CHUNK)
                mixed = jnp.dot(wg, vn[rows, cols], preferred_element_type=F32) + b_ref[g]
                o_ref[rows, cols] = (_gelu(u_ref[rows, cols]) * mixed).astype(BF16)

    return pl.pallas_call(
        body, grid=(t // tr,),
        in_specs=[_rows(tr, sw, 0), _rows(tr, sw, 1), _full((1, sw)), _full(ws.shape), _full(b_col.shape)],
        out_specs=_rows(tr, sw), out_shape=jax.ShapeDtypeStruct((t, sw), BF16),
        compiler_params=_params(("parallel",)), name="sgu_fwd")(z_uv, z_uv, gs, ws, b_col)


def _merge_fwd(y_attn, y_sgu, z_g, b_gate):
    t, d = y_attn.shape
    tr = _row_tile(t, d, 5)

    def body(ya_ref, ys_ref, g0_ref, g1_ref, b0_ref, b1_ref, o_ref):
        g0 = _sigmoid(g0_ref[...] + b0_ref[...])
        g1 = _sigmoid(g1_ref[...] + b1_ref[...])
        o_ref[...] = (g0 * ya_ref[...] + g1 * ys_ref[...]).astype(BF16)

    bspec0 = pl.BlockSpec((1, d), lambda i: (0, 0))
    bspec1 = pl.BlockSpec((1, d), lambda i: (0, 1))
    return pl.pallas_call(
        body, grid=(t // tr,),
        in_specs=[_rows(tr, d), _rows(tr, d), _rows(tr, d, 0), _rows(tr, d, 1), bspec0, bspec1],
        out_specs=_rows(tr, d), out_shape=jax.ShapeDtypeStruct((t, d), BF16),
        compiler_params=_params(("parallel",)), name="merge_fwd")(y_attn, y_sgu, z_g, z_g, b_gate, b_gate)


def _swiglu_fwd(gate, up, comm=None):
    t, f = gate.shape
    tr = _row_tile(t, f, 3)

    def body(g_ref, u_ref, o_ref):
        g = g_ref[...]
        o_ref[...] = (g * _sigmoid(g) * u_ref[...]).astype(BF16)

    outs, comm_outs = _call(
        body, grid=(t // tr,), in_specs=[_rows(tr, f), _rows(tr, f)], out_specs=[_rows(tr, f)],
        out_shape=[jax.ShapeDtypeStruct((t, f), BF16)], sem=("parallel",), name="swiglu_fwd", args=(gate, up), comm=comm)
    return outs[0], comm_outs


def _loss_head(h2, g, target):
    t, d = h2.shape
    tr = _row_tile(t, d, 3)

    def body(h_ref, g_ref, t_ref, loss_ref, dh_ref, dg_ref):
        @pl.when(pl.program_id(0) == 0)
        def _():
            loss_ref[...] = jnp.zeros_like(loss_ref)
            dg_ref[...] = jnp.zeros_like(dg_ref)

        h = h_ref[...]
        r = _rms_scale(h)
        hhat = h * r
        gv = g_ref[...]
        err = hhat * gv - t_ref[...]
        loss_ref[...] += jnp.full(loss_ref.shape, 0.5 * jnp.sum(jnp.mean(err * err, axis=-1)), F32)
        dx, dg_rows = _rms_bwd(hhat, r, gv, err * (1.0 / d))
        dh_ref[...] = dx
        dg_ref[...] += jnp.sum(dg_rows, axis=0, keepdims=True)

    return pl.pallas_call(
        body, grid=(t // tr,), in_specs=[_rows(tr, d), _full((1, d)), _rows(tr, d)],
        out_specs=[_full((1, LANES)), _rows(tr, d), _full((1, d))],
        out_shape=[jax.ShapeDtypeStruct((1, LANES), F32), jax.ShapeDtypeStruct((t, d), F32),
                   jax.ShapeDtypeStruct((1, d), F32)],
        compiler_params=_params(("arbitrary",)), name="loss_head")(h2, g, target)


def _swiglu_bwd(gate, up, dact):
    t, f = gate.shape
    tr = _row_tile(t, f, 4)

    def body(g_ref, u_ref, d_ref, dgu_ref):
        g = g_ref[...]
        s = _sigmoid(g)
        d = d_ref[...]
        dgu_ref[0] = (d * u_ref[...] * (s * (1.0 + g * (1.0 - s)))).astype(BF16)
        dgu_ref[1] = (d * (g * s)).astype(BF16)

    return pl.pallas_call(
        body, grid=(t // tr,), in_specs=[_rows(tr, f)] * 3, out_specs=pl.BlockSpec((2, tr, f), lambda i: (0, i, 0)),
        out_shape=jax.ShapeDtypeStruct((2, t, f), BF16),
        compiler_params=_params(("parallel",)), name="swiglu_bwd")(gate, up, dact)


def _norm_bwd(x, g, dy, resid, name, comm=None):
    t, d = x.shape
    tr = _row_tile(t, d, 4)

    def body(x_ref, g_ref, dy_ref, r_ref, dx_ref, dg_ref):
        @pl.when(pl.program_id(0) == 0)
        def _():
            dg_ref[...] = jnp.zeros_like(dg_ref)

        xv = x_ref[...]
        r = _rms_scale(xv)
        dx, dg_rows = _rms_bwd(xv * r, r, g_ref[...], dy_ref[...])
        dx_ref[...] = r_ref[...] + dx
        dg_ref[...] += jnp.sum(dg_rows, axis=0, keepdims=True)

    outs, comm_outs = _call(
        body, grid=(t // tr,), in_specs=[_rows(tr, d), _full((1, d)), _rows(tr, d), _rows(tr, d)],
        out_specs=[_rows(tr, d), _full((1, d))],
        out_shape=[jax.ShapeDtypeStruct((t, d), F32), jax.ShapeDtypeStruct((1, d), F32)],
        sem=("arbitrary",), name=name, args=(x, g, dy, resid), comm=comm)
    return (outs[0], outs[1]) if comm is None else (outs[0], outs[1], comm_outs)


def _merge_bwd(dmerged, y_attn, y_sgu, z_g, b_gate):
    t, d = y_attn.shape
    tr = _row_tile(t, d, 7)

    def body(dm_ref, ya_ref, ys_ref, g0_ref, g1_ref, b0_ref, b1_ref, dya_ref, dys_ref, dz_ref, db_ref):
        @pl.when(pl.program_id(0) == 0)
        def _():
            db_ref[...] = jnp.zeros_like(db_ref)

        dm = dm_ref[...]
        g0 = _sigmoid(g0_ref[...] + b0_ref[...])
        g1 = _sigmoid(g1_ref[...] + b1_ref[...])
        dya_ref[...] = (dm * g0).astype(BF16)
        dys_ref[...] = (dm * g1).astype(BF16)
        dl0 = dm * ya_ref[...] * (g0 * (1.0 - g0))
        dl1 = dm * ys_ref[...] * (g1 * (1.0 - g1))
        dz_ref[:, 0:d] = dl0.astype(BF16)
        dz_ref[:, d:2 * d] = dl1.astype(BF16)
        db_ref[:, 0:d] += jnp.sum(dl0, axis=0, keepdims=True)
        db_ref[:, d:2 * d] += jnp.sum(dl1, axis=0, keepdims=True)

    bspec0 = pl.BlockSpec((1, d), lambda i: (0, 0))
    bspec1 = pl.BlockSpec((1, d), lambda i: (0, 1))
    return pl.pallas_call(
        body, grid=(t // tr,),
        in_specs=[_rows(tr, d), _rows(tr, d), _rows(tr, d), _rows(tr, d, 0), _rows(tr, d, 1), bspec0, bspec1],
        out_specs=[_rows(tr, d), _rows(tr, d), _rows(tr, 2 * d), _full((1, 2 * d))],
        out_shape=[jax.ShapeDtypeStruct((t, d), BF16), jax.ShapeDtypeStruct((t, d), BF16),
                   jax.ShapeDtypeStruct((t, 2 * d), BF16), jax.ShapeDtypeStruct((1, 2 * d), F32)],
        compiler_params=_params(("arbitrary",)), name="merge_bwd")(dmerged, y_attn, y_sgu, z_g, z_g, b_gate, b_gate)


def _sgu_bwd(z_uv, ds_out, gs, ws, b_col):
    t = z_uv.shape[0]
    sw = z_uv.shape[1] // 2
    groups = sw // SGU_GROUP
    tr = _pick(t, 256, CHUNK)

    def body(u_ref, v_ref, d_ref, gs_ref, ws_ref, b_ref, dz_ref, dws_ref, db_ref, dgs_ref, dvn_ref):
        @pl.when(pl.program_id(0) == 0)
        def _():
            dws_ref[...] = jnp.zeros_like(dws_ref)
            db_ref[...] = jnp.zeros_like(db_ref)
            dgs_ref[...] = jnp.zeros_like(dgs_ref)

        v, dgelu_v = _gelu_and_grad(v_ref[...])
        r = _rms_scale(v)
        vhat = v * r
        gsv = gs_ref[...]
        vn = (vhat * gsv).astype(BF16)
        tri = _tril_mask()
        for g in range(groups):
            wg = jnp.where(tri, ws_ref[g], 0.0).astype(BF16)
            cols = slice(g * SGU_GROUP, (g + 1) * SGU_GROUP)
            for c in range(tr // CHUNK):
                rows = slice(c * CHUNK, (c + 1) * CHUNK)
                vn_cg = vn[rows, cols]
                mixed = jnp.dot(wg, vn_cg, preferred_element_type=F32) + b_ref[g]
                u, dgelu_u = _gelu_and_grad(u_ref[rows, cols])
                dso = d_ref[rows, cols]
                dz_ref[rows, cols] = (dso * mixed * dgelu_u).astype(BF16)
                dmixed = dso * u
                db_ref[g] += jnp.sum(dmixed, axis=1, keepdims=True)
                dmixed_b = dmixed.astype(BF16)
                dws_ref[g] += jnp.where(
                    tri, lax.dot_general(dmixed_b, vn_cg, (((1,), (1,)), ((), ())), preferred_element_type=F32), 0.0)
                dvn_ref[rows, cols] = lax.dot_general(wg, dmixed_b, (((0,), (0,)), ((), ())), preferred_element_type=F32)
        dvn = dvn_ref[...]
        dv, dgs_rows = _rms_bwd(vhat, r, gsv, dvn)
        dz_ref[:, sw:2 * sw] = (dv * dgelu_v).astype(BF16)
        dgs_ref[...] += jnp.sum(dgs_rows, axis=0, keepdims=True)

    return pl.pallas_call(
        body, grid=(t // tr,),
        in_specs=[_rows(tr, sw, 0), _rows(tr, sw, 1), _rows(tr, sw), _full((1, sw)), _full(ws.shape), _full(b_col.shape)],
        out_specs=[_rows(tr, 2 * sw), _full(ws.shape), _full(b_col.shape), _full((1, sw))],
        out_shape=[jax.ShapeDtypeStruct((t, 2 * sw), BF16), jax.ShapeDtypeStruct(ws.shape, F32),
                   jax.ShapeDtypeStruct(b_col.shape, F32), jax.ShapeDtypeStruct((1, sw), F32)],
        scratch_shapes=[pltpu.VMEM((tr, sw), F32)],
        compiler_params=_params(("arbitrary",)), name="sgu_bwd")(z_uv, z_uv, ds_out, gs, ws, b_col)


def _lat_bwd(z_lat, qg, kvg, dqn, dkvn, dkpe_heads, cos, sin, ql, kvl):
    t, w = z_lat.shape
    heads = dkpe_heads.shape[0]
    tr = _row_tile(t, w + heads * LANES, 3)

    def body(z_ref, qg_ref, kvg_ref, dq_ref, dkv_ref, dk_ref, cos_ref, sin_ref, dz_ref, dqg_ref, dkvg_ref):
        @pl.when(pl.program_id(0) == 0)
        def _():
            dqg_ref[...] = jnp.zeros_like(dqg_ref)
            dkvg_ref[...] = jnp.zeros_like(dkvg_ref)

        q = z_ref[:, 0:ql]
        r = _rms_scale(q)
        dx, dg_rows = _rms_bwd(q * r, r, qg_ref[...], dq_ref[...])
        dz_ref[:, 0:ql] = dx.astype(BF16)
        dqg_ref[...] += jnp.sum(dg_rows, axis=0, keepdims=True)
        kv = z_ref[:, ql:ql + kvl]
        r = _rms_scale(kv)
        dx, dg_rows = _rms_bwd(kv * r, r, kvg_ref[...], dkv_ref[...])
        dz_ref[:, ql:ql + kvl] = dx.astype(BF16)
        dkvg_ref[...] += jnp.sum(dg_rows, axis=0, keepdims=True)
        dk = dk_ref[0]
        for h in range(1, heads):
            dk = dk + dk_ref[h]
        dz_ref[:, ql + kvl:ql + kvl + LANES] = _rope_bwd(dk, cos_ref[...], sin_ref[...]).astype(BF16)

    return pl.pallas_call(
        body, grid=(t // tr,),
        in_specs=[_rows(tr, w), _full((1, ql)), _full((1, kvl)), _rows(tr, ql), _rows(tr, kvl),
                  pl.BlockSpec((heads, tr, LANES), lambda i: (0, i, 0)), _rows(tr, LANES), _rows(tr, LANES)],
        out_specs=[_rows(tr, w), _full((1, ql)), _full((1, kvl))],
        out_shape=[jax.ShapeDtypeStruct((t, w), BF16), jax.ShapeDtypeStruct((1, ql), F32),
                   jax.ShapeDtypeStruct((1, kvl), F32)],
        compiler_params=_params(("arbitrary",)), name="lat_bwd")(z_lat, qg, kvg, dqn, dkvn, dkpe_heads, cos, sin)


_NT = (((1,), (1,)), ((), ()))


def _attn_scale():
    return (QK_NOPE + QK_ROPE) ** -0.5


def _attn_fwd(q_c, kv, kpe, comm=None):
    t = q_c.shape[0]
    heads = q_c.shape[1] // HEAD_PAD
    tq = _pick(t, ATTN_TILE)
    nq = t // tq
    scale = _attn_scale()
    to_log2 = scale * math.log2(math.e)

    def body(q_ref, kn_ref, kpe_ref, v_ref, o_ref, lse_ref, m_sc, l_sc, acc_sc):
        qi, ki = pl.program_id(1), pl.program_id(2)

        @pl.when(ki == 0)
        def _():
            m_sc[...] = jnp.full_like(m_sc, NEG_BIG)
            l_sc[...] = jnp.zeros_like(l_sc)
            acc_sc[...] = jnp.zeros_like(acc_sc)

        def step(diagonal):
            kc = jnp.concatenate([kn_ref[...], kpe_ref[...]], axis=1)
            s = lax.dot_general(q_ref[...], kc, _NT, preferred_element_type=F32)
            if diagonal:
                row = lax.broadcasted_iota(jnp.int32, s.shape, 0)
                col = lax.broadcasted_iota(jnp.int32, s.shape, 1)
                s = jnp.where(row >= col, s, NEG_BIG)
            m_prev = m_sc[...]
            m_new = jnp.maximum(m_prev, jnp.max(s, axis=1, keepdims=True))
            alpha = jnp.exp2((m_prev - m_new) * to_log2)
            p = jnp.exp2((s - m_new) * to_log2)
            l_sc[...] = alpha * l_sc[...] + jnp.sum(p, axis=1, keepdims=True)
            acc_sc[...] = alpha * acc_sc[...] + jnp.dot(p.astype(BF16), v_ref[...], preferred_element_type=F32)
            m_sc[...] = m_new

        @pl.when(ki < qi)
        def _():
            step(False)

        @pl.when(ki == qi)
        def _():
            step(True)
            o_ref[...] = acc_sc[...] / l_sc[...]
            lse_ref[0] = m_sc[...] * scale + jnp.log(l_sc[...])

    kmap = lambda blk: (lambda h, qi, ki: (jnp.minimum(ki, qi), 2 * h + blk))
    outs, comm_outs = _call(
        body, grid=(heads, nq, nq),
        in_specs=[pl.BlockSpec((tq, HEAD_PAD), lambda h, qi, ki: (qi, h)),
                  pl.BlockSpec((tq, QK_NOPE), kmap(0)),
                  pl.BlockSpec((tq, LANES), lambda h, qi, ki: (jnp.minimum(ki, qi), 0)),
                  pl.BlockSpec((tq, V_HEAD), kmap(1))],
        out_specs=[pl.BlockSpec((tq, V_HEAD), lambda h, qi, ki: (qi, h)),
                   pl.BlockSpec((1, tq, 1), lambda h, qi, ki: (h, qi, 0))],
        out_shape=[jax.ShapeDtypeStruct((t, heads * V_HEAD), F32), jax.ShapeDtypeStruct((heads, t, 1), F32)],
        scratch_shapes=[pltpu.VMEM((tq, 1), F32), pltpu.VMEM((tq, 1), F32), pltpu.VMEM((tq, V_HEAD), F32)],
        sem=("parallel", "parallel", "arbitrary"), name="attn_fwd", args=(q_c, kv, kpe, kv), comm=comm)
    return outs[0], outs[1], comm_outs


def _attn_bwd(q_c, kv, kpe, o, do, lse_row, comm=None):
    t = q_c.shape[0]
    heads = q_c.shape[1] // HEAD_PAD
    tk = _pick(t, ATTN_TILE)
    nk = t // tk
    scale = _attn_scale()
    tn_dims = (((0,), (0,)), ((), ()))

    def body(q_ref, kn_ref, kpe_ref, v_ref, do_ref, lse_ref, o_ref, dq_ref, dkv_ref, dkpe_ref, dk_sc, dv_sc, delta_sc):
        ki, qi = pl.program_id(1), pl.program_id(2)

        @pl.when(jnp.logical_and(ki == 0, qi == 0))
        def _():
            dq_ref[...] = jnp.zeros_like(dq_ref)

        @pl.when(qi == 0)
        def _():
            dk_sc[...] = jnp.zeros_like(dk_sc)
            dv_sc[...] = jnp.zeros_like(dv_sc)

        @pl.when(ki == 0)
        def _():
            delta_sc[qi] = jnp.sum((do_ref[...] * o_ref[...]).T, axis=0, keepdims=True)

        def step(diagonal):
            kc = jnp.concatenate([kn_ref[...], kpe_ref[...]], axis=1)
            q = q_ref[...]
            st = lax.dot_general(kc, q, _NT, preferred_element_type=F32) * scale
            pt = jnp.exp(st - lse_ref[0])
            if diagonal:
                krow = lax.broadcasted_iota(jnp.int32, st.shape, 0)
                qcol = lax.broadcasted_iota(jnp.int32, st.shape, 1)
                pt = jnp.where(qcol >= krow, pt, 0.0)
            do_b = do_ref[...].astype(BF16)
            dv_sc[...] += jnp.dot(pt.astype(BF16), do_b, preferred_element_type=F32)
            dpt = lax.dot_general(v_ref[...], do_b, _NT, preferred_element_type=F32)
            dst = (pt * (dpt - delta_sc[qi]) * scale).astype(BF16)
            dk_sc[...] += jnp.dot(dst, q, preferred_element_type=F32)
            rows = pl.ds(pl.multiple_of(qi * tk, tk), tk)
            dq_ref[rows, :] += lax.dot_general(dst, kc, tn_dims, preferred_element_type=F32)

        @pl.when(qi > ki)
        def _():
            step(False)

        @pl.when(qi == ki)
        def _():
            step(True)

        @pl.when(qi == nk - 1)
        def _():
            dkv_ref[:, 0:QK_NOPE] = dk_sc[:, 0:QK_NOPE].astype(BF16)
            dkv_ref[:, QK_NOPE:QK_NOPE + V_HEAD] = dv_sc[...].astype(BF16)
            dkpe_ref[0] = dk_sc[:, QK_NOPE:QK_NOPE + LANES]

    qclamp = lambda h, ki, qi: (jnp.maximum(qi, ki), h)
    kmap = lambda blk: (lambda h, ki, qi: (ki, 2 * h + blk))
    rmap = lambda h, ki, qi: (h, 0, jnp.maximum(qi, ki))
    outs, comm_outs = _call(
        body, grid=(heads, nk, nk),
        in_specs=[pl.BlockSpec((tk, HEAD_PAD), qclamp), pl.BlockSpec((tk, QK_NOPE), kmap(0)),
                  pl.BlockSpec((tk, LANES), lambda h, ki, qi: (ki, 0)), pl.BlockSpec((tk, V_HEAD), kmap(1)),
                  pl.BlockSpec((tk, V_HEAD), qclamp), pl.BlockSpec((1, 1, tk), rmap),
                  pl.BlockSpec((tk, V_HEAD), lambda h, ki, qi: (jnp.where(ki == 0, qi, 0), h))],
        out_specs=[pl.BlockSpec((t, HEAD_PAD), lambda h, ki, qi: (0, h)),
                   pl.BlockSpec((tk, HEAD_PAD), lambda h, ki, qi: (ki, h)),
                   pl.BlockSpec((1, tk, LANES), lambda h, ki, qi: (h, ki, 0))],
        out_shape=[jax.ShapeDtypeStruct((t, heads * HEAD_PAD), F32),
                   jax.ShapeDtypeStruct((t, heads * HEAD_PAD), BF16), jax.ShapeDtypeStruct((heads, t, LANES), F32)],
        scratch_shapes=[pltpu.VMEM((tk, HEAD_PAD), F32), pltpu.VMEM((tk, V_HEAD), F32), pltpu.VMEM((nk, 1, tk), F32)],
        sem=("parallel", "arbitrary", "arbitrary"), name="attn_bwd",
        args=(q_c, kv, kpe, kv, do, lse_row, o), comm=comm)
    return outs[0], outs[1], outs[2], comm_outs


def _local_step(x, pos_col, target, small, shards, opt):
    t = x.shape[0]
    ql, kvl = small["q_norm_g"].shape[1], small["kv_norm_g"].shape[1]
    sw = small["sgu_norm_g"].shape[1]
    heads = (shards["w_uq"].shape[1] * N_DEV) // (QK_NOPE + QK_ROPE)
    big = {}
    early = ["w_in", "w_uq", "w_ukv"]
    big.update(_compute_layout(dict(zip(early, _all_gather([shards[k] for k in early]))), ql, kvl, heads, sw))
    half = QK_ROPE // 2
    lane = jnp.arange(LANES)
    inv_freq = ROPE_THETA ** (-jnp.arange(0, QK_ROPE, 2, dtype=F32) / QK_ROPE)
    inv_row = inv_freq[lane % half][None, :]
    sign_row = jnp.where((lane % QK_ROPE) < half, -1.0, 1.0).astype(F32)[None, :]
    cos, sin = _rope_tables(pos_col, inv_row, sign_row)
    ws = small["w_sgu"]
    b_col = small["b_sgu_col"]

    def arrived(names, bufs):
        big.update(_compute_layout(dict(zip(names, bufs)), ql, kvl, heads, sw))

    a = _norm_fwd(x, small["norm_mix_g"], "norm_mix_fwd")
    z_lat = _mm(a, big["w_lat"], name="z_lat")
    z_uv = _mm(a, big["w_uv"], name="z_uv")
    mixers = ["w_o_sgu", "w_o_attn"]
    z_g, bufs = _mm(a, big["w_g"], name="z_g", comm=_gather_first([shards[k] for k in mixers]))
    qn, kvn, kpe = _lat_fwd(z_lat, small["q_norm_g"], small["kv_norm_g"], cos, sin, ql, kvl)
    q_p, bufs = _mm(qn, big["w_uq"], name="q_up", comm=_gather_second(bufs))
    arrived(mixers, bufs)
    kv = _mm(kvn, big["w_ukv"], out_dtype=BF16, name="kv_up")
    q_c = _q_rope(q_p, cos, sin, False, "q_rope")
    wide = ["w_out", "w_gate_ffn", "w_up_ffn"]
    attn, lse, bufs = _attn_fwd(q_c, kv, kpe, comm=_gather_first([shards[k] for k in wide]))
    s_out = _sgu_fwd(z_uv, small["sgu_norm_g"], ws, b_col)
    y_sgu = _mm(s_out, big["w_o_sgu"], name="y_sgu")
    y_attn, bufs = _mm(attn, big["w_o_attn"], name="y_attn", comm=_gather_second(bufs))
    arrived(wide[:1], bufs[:1])
    w_gate, w_up = bufs[1:]
    merged = _merge_fwd(y_attn, y_sgu, z_g, small["b_gate"])
    h1 = _mm(merged, big["w_out"], add=x, name="h1")
    f = _norm_fwd(h1, small["norm_ffn_g"], "norm_ffn_fwd")
    down = shards["w_down_ffn"]
    top = _pick(down.shape[0], down.shape[0] // 2, 2 * SUBLANES)
    gate, bufs = _mm(f, w_gate, slab="n", name="ffn_gate", comm=_gather_first([down], rows=[(0, top)]))
    up, bufs = _mm(f, w_up, slab="n", name="ffn_up",
                   comm=_gather_first([down], rows=[(top, down.shape[0] - top)], into=bufs))
    ffn = gate.shape[2]
    gate, up = gate.reshape(N_DEV * t, ffn), up.reshape(N_DEV * t, ffn)
    act, (w_down,) = _swiglu_fwd(gate, up, comm=_gather_second(bufs))
    act = act.reshape(N_DEV, t, ffn)
    h2 = _mm(act, w_down, slab="k", add=h1, name="h2")
    loss_row, dh2, d_norm_final = _loss_head(h2, small["norm_final_g"], target)

    def pair_sums(names, slabs, bufs):
        return [_pair_sum(g, b, "pair_sum_" + k) for k, g, b in zip(names, slabs, bufs)]

    parts, updates = {}, {}

    def update(k, comm=None):
        w, m, v = opt[k]
        updates[k], got = _adamw_shard(parts[k], w, m, v, "adamw_" + k, comm=comm)
        return got

    down_slabs = [_mm(act, dh2, ta=True, slab="m", out_dtype=BF16, name="dw_down")]
    dact, bufs = _mm(dh2, w_down, tb=True, slab="n", name="dact", comm=_to_sibling(down_slabs))
    down_pair = pair_sums(["w_down_ffn"], down_slabs, bufs)
    dgu = _swiglu_bwd(gate, up, dact.reshape(N_DEV * t, ffn)).reshape(2 * N_DEV, t, ffn)
    dw_gu, got = _mm(f, dgu, ta=True, slab="n", out_dtype=BF16, name="dw_gate_up", comm=_to_chips(down_pair))
    parts["w_down_ffn"] = got[0]
    gu_names = ["w_gate_ffn", "w_up_ffn"]
    df, bufs = _mm(dgu, w_gate, tb=True, slab="k", name="df_gate", comm=_to_sibling([dw_gu, dw_gu], first=[0, N_DEV]))
    gu_pairs = [_pair_sum(dw_gu, b, "pair_sum_" + k, first=s0) for k, b, s0 in zip(gu_names, bufs, [0, N_DEV])]
    df = _mm(dgu, w_up, tb=True, slab="k", a_slab0=N_DEV, add=df, name="df_up")
    dh1, d_norm_ffn = _norm_bwd(h1, small["norm_ffn_g"], df, dh2, "norm_ffn_bwd")
    dw_out = _mm(merged, dh1, ta=True, out_dtype=BF16, name="dw_out")
    out_slabs = [_slabs_from_rows(dw_out)]
    dmerged, bufs = _mm(dh1, big["w_out"], tb=True, name="dmerged", comm=_to_sibling(out_slabs))
    out_pair = pair_sums(["w_out"], out_slabs, bufs)
    dy_attn, dy_sgu, dz_g, d_b_gate = _merge_bwd(dmerged, y_attn, y_sgu, z_g, small["b_gate"])
    dw_o_sgu = _mm(s_out, dy_sgu, ta=True, out_dtype=BF16, name="dw_o_sgu")
    ds_out = _mm(dy_sgu, big["w_o_sgu"], tb=True, name="ds_out")
    dz_uv, d_ws, d_b_col, d_sgu_norm = _sgu_bwd(z_uv, ds_out, small["sgu_norm_g"], ws, b_col)
    dw_o_attn = _mm(attn, dy_attn, ta=True, out_dtype=BF16, name="dw_o_attn")
    mix_names = ["w_o_sgu", "w_o_attn"]
    mix_slabs = [_slabs_from_cols(dw_o_sgu), _slabs_from_rows(dw_o_attn)]
    dattn, bufs = _mm(dy_attn, big["w_o_attn"], tb=True, name="dattn", comm=_to_sibling(mix_slabs))
    mix_pairs = pair_sums(mix_names, mix_slabs, bufs)
    dq_c, dkv, dkpe_heads, got = _attn_bwd(q_c, kv, kpe, attn, dattn, lse.reshape(heads, 1, t), comm=_to_chips(gu_pairs))
    parts.update(zip(gu_names, got))
    dq_p = _q_rope(dq_c, cos, sin, True, "q_rope_bwd")
    dw_uq = _mm(qn, dq_p, ta=True, out_dtype=BF16, name="dw_uq")
    dw_ukv = _mm(kvn, dkv, ta=True, out_dtype=BF16, name="dw_ukv")
    dqn = _mm(dq_p, big["w_uq"], tb=True, name="dqn")
    dkvn = _mm(dkv, big["w_ukv"], tb=True, name="dkvn")
    dz_lat, d_q_norm, d_kv_norm = _lat_bwd(z_lat, small["q_norm_g"], small["kv_norm_g"], dqn, dkvn, dkpe_heads,
                                           cos, sin, ql, kvl)
    dw_g, got = _mm(a, dz_g, ta=True, out_dtype=BF16, name="dw_g", comm=_to_chips(out_pair))
    parts["w_out"] = got[0]
    dw_uv, got = _mm(a, dz_uv, ta=True, out_dtype=BF16, name="dw_uv", comm=_to_chips(mix_pairs[1:]))
    parts["w_o_attn"] = got[0]
    dw_lat, got = _mm(a, dz_lat, ta=True, out_dtype=BF16, name="dw_lat", comm=_to_chips(mix_pairs[:1]))
    parts["w_o_sgu"] = got[0]
    lat = ql + kvl + QK_ROPE
    dw_uq_cols = dw_uq.reshape(ql, heads, HEAD_PAD)[:, :, :QK_NOPE + QK_ROPE].reshape(ql, heads * (QK_NOPE + QK_ROPE))
    in_names = ["w_uq", "w_ukv", "w_in"]
    in_slabs = [_slabs_from_cols(dw_uq_cols), _slabs_from_cols(dw_ukv),
                _slabs_from_cols(jnp.concatenate([dw_lat[:, :lat], dw_uv, dw_g], axis=1))]
    da = _mm(dz_lat, big["w_lat"], tb=True, name="da_lat")
    da, bufs = _mm(dz_uv, big["w_uv"], tb=True, add=da, name="da_uv", comm=_to_sibling(in_slabs))
    uq_pair, ukv_pair, in_pair = pair_sums(in_names, in_slabs, bufs)
    rows = in_pair.shape[1]
    chunk = _pick(rows, rows // TAIL_CHUNKS, 2 * SUBLANES)
    chunks = [(r0, chunk) for r0 in range(0, rows, chunk)]
    da, got = _mm(dz_g, big["w_g"], tb=True, add=da, name="da_g",
                  comm=_to_chips([uq_pair, in_pair], rows=[(0, uq_pair.shape[1]), chunks[0]]))
    parts["w_uq"], in_parts = got
    grad_x, d_norm_mix, got = _norm_bwd(x, small["norm_mix_g"], da, dh1, "norm_mix_bwd", comm=_to_chips([ukv_pair]))
    parts["w_ukv"] = got[0]
    hosts = ["w_gate_ffn", "w_up_ffn", "w_down_ffn", "w_out", "w_o_attn", "w_o_sgu", "w_uq", "w_ukv"]
    assert len(chunks) <= 1 + len(hosts)
    for i, k in enumerate(hosts):
        if 1 + i < len(chunks):
            in_parts = update(k, comm=_to_chips([in_pair], rows=[chunks[1 + i]], into=[in_parts]))[0]
        else:
            update(k)
    parts["w_in"] = in_parts
    update("w_in")

    gs = {"norm_mix_g": d_norm_mix, "b_gate": d_b_gate, "q_norm_g": d_q_norm, "kv_norm_g": d_kv_norm,
          "sgu_norm_g": d_sgu_norm, "w_sgu": d_ws, "b_sgu_col": d_b_col, "norm_ffn_g": d_norm_ffn,
          "norm_final_g": d_norm_final}
    return loss_row, grad_x, gs, updates


def _my_place():
    return lax.axis_index("x"), lax.axis_index("y"), lax.axis_index("c")


def _all_gather(shards):
    n = len(shards)

    def body(*refs):
        ins, outs = refs[:n], refs[n:2 * n]
        send_sems, recv_sems, local_sems = refs[2 * n:]
        x, y, c = _my_place()
        me, sibling = (x, y, c), (x, y, 1 - c)
        chips = [(1 - x, y), (x, 1 - y), (1 - x, 1 - y)]

        def slab(w, place):
            return outs[w].at[4 * place[0] + 2 * place[1] + place[2]]

        def copy(w, k, place, to, src=None):
            return pltpu.make_async_remote_copy(
                src_ref=slab(w, place) if src is None else src, dst_ref=slab(w, place),
                send_sem=send_sems.at[w, k], recv_sem=recv_sems.at[w, k], device_id=to, device_id_type=MESH)

        mine = [pltpu.make_async_copy(ins[w], slab(w, me), local_sems.at[w]) for w in range(n)]
        for cp in mine:
            cp.start()
        started = []
        for w in range(n):
            first = [copy(w, 0, me, sibling, src=ins[w])]
            first += [copy(w, 1 + j, me, (*chip, c), src=ins[w]) for j, chip in enumerate(chips)]
            for cp in first:
                cp.start()
            started += first
        for w in range(n):
            for j, chip in enumerate(chips):
                copy(w, 1 + j, (*chip, c), me).wait_recv()
                fwd = copy(w, 4 + j, (*chip, c), sibling)
                fwd.start()
                started.append(fwd)
        for w in range(n):
            copy(w, 0, sibling, me).wait_recv()
            for j, chip in enumerate(chips):
                copy(w, 4 + j, (*chip, 1 - c), me).wait_recv()
        for cp in started:
            cp.wait_send()
        for cp in mine:
            cp.wait()

    any_spec = pl.BlockSpec(memory_space=pl.ANY)
    return pl.pallas_call(
        body, in_specs=[any_spec] * n, out_specs=[any_spec] * n,
        out_shape=[jax.ShapeDtypeStruct((N_DEV,) + s.shape, s.dtype) for s in shards],
        scratch_shapes=[pltpu.SemaphoreType.DMA((n, 7)), pltpu.SemaphoreType.DMA((n, 7)), pltpu.SemaphoreType.DMA((n,))],
        compiler_params=pltpu.CompilerParams(has_side_effects=True), name="all_gather_weights")(*shards)


N_CHIPS = N_DEV // 2


def _gather_first(shards, rows=None, into=None):
    n = len(shards)
    rows = rows or [(0, s.shape[0]) for s in shards]

    def copies(ins, outs, sems):
        x, y, c = _my_place()
        send_sems, recv_sems, local_sems = sems
        me = 4 * x + 2 * y + c
        targets = [(x, y, 1 - c), (1 - x, y, c), (x, 1 - y, c), (1 - x, 1 - y, c)]
        out = []
        for w in range(n):
            r0, nr = rows[w]
            src, dst = ins[w].at[pl.ds(r0, nr)], outs[w].at[me, pl.ds(r0, nr)]
            out.append(pltpu.make_async_copy(src, dst, local_sems.at[w]))
            out += [pltpu.make_async_remote_copy(src_ref=src, dst_ref=dst, send_sem=send_sems.at[w, k],
                                                 recv_sem=recv_sems.at[w, k], device_id=to, device_id_type=MESH)
                    for k, to in enumerate(targets)]
        return out

    def start(ins, outs, sems):
        for cp in copies(ins, outs, sems):
            cp.start()

    def finish(ins, outs, sems):
        for cp in copies(ins, outs, sems):
            cp.wait()

    return _Comm(list(shards) + list(into or []), [jax.ShapeDtypeStruct((N_DEV,) + s.shape, s.dtype) for s in shards],
                 [pltpu.SemaphoreType.DMA((n, 4)), pltpu.SemaphoreType.DMA((n, 4)), pltpu.SemaphoreType.DMA((n,))],
                 start, finish, aliases={n + w: w for w in range(n)} if into else None)


def _gather_second(bufs):
    n = len(bufs)

    def copies(ins, outs, sems):
        x, y, c = _my_place()
        send_sems, recv_sems = sems
        out = []
        for w in range(n):
            for j, (cx, cy) in enumerate([(1 - x, y), (x, 1 - y), (1 - x, 1 - y)]):
                slab = 4 * cx + 2 * cy + c
                out.append(pltpu.make_async_remote_copy(
                    src_ref=ins[w].at[slab], dst_ref=outs[w].at[slab], send_sem=send_sems.at[w, j],
                    recv_sem=recv_sems.at[w, j], device_id=(x, y, 1 - c), device_id_type=MESH))
        return out

    def start(ins, outs, sems):
        for cp in copies(ins, outs, sems):
            cp.start()

    def finish(ins, outs, sems):
        for cp in copies(ins, outs, sems):
            cp.wait()

    return _Comm(bufs, [jax.ShapeDtypeStruct(b.shape, b.dtype) for b in bufs],
                 [pltpu.SemaphoreType.DMA((n, 3)), pltpu.SemaphoreType.DMA((n, 3))], start, finish,
                 aliases={w: w for w in range(n)})


def _to_sibling(grads, first=None):
    n = len(grads)
    first = first or [0] * n

    def copies(ins, outs, sems):
        x, y, c = _my_place()
        send_sems, recv_sems = sems
        return [pltpu.make_async_remote_copy(
            src_ref=ins[w].at[first[w] + 2 * i + (1 - c)], dst_ref=outs[w].at[i], send_sem=send_sems.at[w, i],
            recv_sem=recv_sems.at[w, i], device_id=(x, y, 1 - c), device_id_type=MESH)
            for w in range(n) for i in range(N_CHIPS)]

    def start(ins, outs, sems):
        for cp in copies(ins, outs, sems):
            cp.start()

    def finish(ins, outs, sems):
        for cp in copies(ins, outs, sems):
            cp.wait()

    return _Comm(grads, [jax.ShapeDtypeStruct((N_CHIPS,) + g.shape[1:], g.dtype) for g in grads],
                 [pltpu.SemaphoreType.DMA((n, N_CHIPS)), pltpu.SemaphoreType.DMA((n, N_CHIPS))], start, finish)


def _to_chips(parts, rows=None, into=None):
    n = len(parts)
    rows = rows or [(0, p.shape[1]) for p in parts]

    def copies(ins, outs, sems):
        x, y, c = _my_place()
        send_sems, recv_sems, local_sems = sems
        mine = 2 * x + y
        chips = [(1 - x, y), (x, 1 - y), (1 - x, 1 - y)]
        remote = [pltpu.make_async_remote_copy(
            src_ref=ins[w].at[2 * cx + cy, pl.ds(*rows[w])], dst_ref=outs[w].at[mine, pl.ds(*rows[w])],
            send_sem=send_sems.at[w, j], recv_sem=recv_sems.at[w, j], device_id=(cx, cy, c), device_id_type=MESH)
            for w in range(n) for j, (cx, cy) in enumerate(chips)]
        local = [pltpu.make_async_copy(ins[w].at[mine, pl.ds(*rows[w])], outs[w].at[mine, pl.ds(*rows[w])],
                                       local_sems.at[w]) for w in range(n)]
        return remote + local

    def start(ins, outs, sems):
        for cp in copies(ins, outs, sems):
            cp.start()

    def finish(ins, outs, sems):
        for cp in copies(ins, outs, sems):
            cp.wait()

    return _Comm(list(parts) + list(into or []), [jax.ShapeDtypeStruct(p.shape, p.dtype) for p in parts],
                 [pltpu.SemaphoreType.DMA((n, N_CHIPS - 1)), pltpu.SemaphoreType.DMA((n, N_CHIPS - 1)),
                  pltpu.SemaphoreType.DMA((n,))], start, finish,
                 aliases={n + w: w for w in range(n)} if into else None)


def _pair_sum(g, buf, name, first=0):
    _, r, c = g.shape
    tr = _row_tile(r, c, 2)
    core = (lax.axis_index("c") + first).astype(jnp.int32).reshape(1)

    def body(core_ref, g_ref, b_ref, o_ref):
        o_ref[...] = (g_ref[...].astype(F32) + b_ref[...].astype(F32)).astype(o_ref.dtype)

    blk = (1, tr, c)
    return pl.pallas_call(
        body, grid_spec=pltpu.PrefetchScalarGridSpec(
            num_scalar_prefetch=1, grid=(N_CHIPS, r // tr),
            in_specs=[pl.BlockSpec(blk, lambda i, j, core_ref: (2 * i + core_ref[0], j, 0)),
                      pl.BlockSpec(blk, lambda i, j, core_ref: (i, j, 0))],
            out_specs=pl.BlockSpec(blk, lambda i, j, core_ref: (i, j, 0))),
        out_shape=jax.ShapeDtypeStruct(buf.shape, buf.dtype),
        compiler_params=_params(("parallel", "parallel")), name=name)(core, g, buf)


def _all_reduce_pack(pack):
    r = pack.shape[0]

    def body(x_ref, out_ref, gath_ref, send_sems, recv_sems, local_sem):
        x, y, c = _my_place()
        me, sibling = (x, y, c), (x, y, 1 - c)
        chips = [(1 - x, y), (x, 1 - y), (1 - x, 1 - y)]

        def slab(place):
            return gath_ref.at[4 * place[0] + 2 * place[1] + place[2]]

        def copy(k, place, to, src=None):
            return pltpu.make_async_remote_copy(
                src_ref=slab(place) if src is None else src, dst_ref=slab(place),
                send_sem=send_sems.at[k], recv_sem=recv_sems.at[k], device_id=to, device_id_type=MESH)

        mine = pltpu.make_async_copy(x_ref, slab(me), local_sem)
        mine.start()
        first = [copy(0, me, sibling, src=x_ref)]
        first += [copy(1 + j, me, (*chip, c), src=x_ref) for j, chip in enumerate(chips)]
        for cp in first:
            cp.start()
        passed = [copy(4 + j, (*chip, c), sibling) for j, chip in enumerate(chips)]
        for j, chip in enumerate(chips):
            copy(1 + j, (*chip, c), me).wait_recv()
            passed[j].start()
        copy(0, sibling, me).wait_recv()
        for j, chip in enumerate(chips):
            copy(4 + j, (*chip, 1 - c), me).wait_recv()
        for cp in first + passed:
            cp.wait_send()
        mine.wait()
        acc = gath_ref[0]
        for i in range(1, N_DEV):
            acc = acc + gath_ref[i]
        out_ref[...] = acc

    vmem = pl.BlockSpec(memory_space=pltpu.VMEM)
    return pl.pallas_call(
        body, in_specs=[vmem], out_specs=vmem, out_shape=jax.ShapeDtypeStruct(pack.shape, F32),
        scratch_shapes=[pltpu.VMEM((N_DEV, r, LANES), F32), pltpu.SemaphoreType.DMA((7,)),
                        pltpu.SemaphoreType.DMA((7,)), pltpu.SemaphoreType.DMA],
        compiler_params=pltpu.CompilerParams(vmem_limit_bytes=VMEM_LIMIT), name="all_reduce_small")(pack)


def _adamw_math(w, g, m, v):
    m = ADAM_B1 * m + (1.0 - ADAM_B1) * g
    v = ADAM_B2 * v + (1.0 - ADAM_B2) * (g * g)
    m_hat = m / (1.0 - ADAM_B1 ** ADAM_STEP)
    v_hat = v / (1.0 - ADAM_B2 ** ADAM_STEP)
    delta = -ADAM_LR * (m_hat / (jnp.sqrt(v_hat) + ADAM_EPS) + ADAM_WD * w)
    return delta, m, v


def _adamw_shard(parts, w, m, v, name, comm=None):
    r, c = w.shape
    n_parts = parts.shape[0]
    tr = _pick(r, max(2 * SUBLANES, (256 * 1024) // c), 2 * SUBLANES)

    def body(p_ref, w_ref, m_ref, v_ref, g_ref, d_ref, nm_ref, nv_ref):
        g = p_ref[0].astype(F32)
        for i in range(1, n_parts):
            g = g + p_ref[i].astype(F32)
        g_ref[...] = g
        d_ref[...], nm_ref[...], nv_ref[...] = _adamw_math(w_ref[...], g, m_ref[...], v_ref[...])

    spec = pl.BlockSpec((tr, c), lambda i: (i, 0))
    outs, comm_outs = _call(
        body, grid=(r // tr,), in_specs=[pl.BlockSpec((n_parts, tr, c), lambda i: (0, i, 0)), spec, spec, spec],
        out_specs=[spec] * 4, out_shape=[jax.ShapeDtypeStruct((r, c), F32)] * 4,
        sem=("parallel",), name=name, args=(parts, w, m, v), comm=comm)
    return outs, comm_outs


def _adamw_pack(g, w, m, v):
    r, c = w.shape

    def body(g_ref, w_ref, m_ref, v_ref, d_ref, nm_ref, nv_ref):
        d_ref[...], nm_ref[...], nv_ref[...] = _adamw_math(w_ref[...], g_ref[...], m_ref[...], v_ref[...])

    return pl.pallas_call(
        body, in_specs=[_full((r, c))] * 4, out_specs=[_full((r, c))] * 3, grid=(1,),
        out_shape=[jax.ShapeDtypeStruct((r, c), F32)] * 3,
        compiler_params=_params(("arbitrary",)), name="adamw_small")(g, w, m, v)


def _cols_from_slabs(g):
    return jnp.transpose(g, (1, 0, 2)).reshape(g.shape[1], N_DEV * g.shape[2])


def _slabs_from_cols(w):
    r, c8 = w.shape
    return jnp.transpose(w.reshape(r, N_DEV, c8 // N_DEV), (1, 0, 2))


def _rows_from_slabs(g):
    return g.reshape(N_DEV * g.shape[1], g.shape[2])


def _slabs_from_rows(w):
    return w.reshape(N_DEV, w.shape[0] // N_DEV, w.shape[1])


def _compute_layout(gathered, ql, kvl, heads, sw):
    out = {}
    for k, g in gathered.items():
        if k == "w_in":
            lat = ql + kvl + QK_ROPE
            w_in_full = _cols_from_slabs(g)
            out["w_lat"] = jnp.pad(w_in_full[:, :lat], ((0, 0), (0, LANES - QK_ROPE)))
            out["w_uv"] = w_in_full[:, lat:lat + 2 * sw]
            out["w_g"] = w_in_full[:, lat + 2 * sw:]
        elif k == "w_uq":
            per_head = _cols_from_slabs(g).reshape(ql, heads, QK_NOPE + QK_ROPE)
            pad = HEAD_PAD - QK_NOPE - QK_ROPE
            out["w_uq"] = jnp.pad(per_head, ((0, 0), (0, 0), (0, pad))).reshape(ql, heads * HEAD_PAD)
        elif k in ("w_o_attn", "w_out", "w_down_ffn"):
            out[k.removesuffix("_ffn")] = _rows_from_slabs(g)
        else:
            out[k.removesuffix("_ffn")] = _cols_from_slabs(g)
    return out


_SMALL =["norm_mix_g", "b_gate", "q_norm_g", "kv_norm_g", "sgu_norm_g", "w_sgu", "b_sgu", "norm_ffn_g", "norm_final_g"]
_BIG = ["w_in", "w_uq", "w_ukv", "w_o_attn", "w_o_sgu", "w_out", "w_gate_ffn", "w_up_ffn", "w_down_ffn"]
_ORDER = ["norm_mix_g", "w_in", "b_gate", "q_norm_g", "w_uq", "kv_norm_g", "w_ukv", "w_o_attn", "sgu_norm_g", "w_sgu",
          "b_sgu", "w_o_sgu", "w_out", "norm_ffn_g", "w_gate_ffn", "w_up_ffn", "w_down_ffn", "norm_final_g"]


def _pack_rows(parts):
    rows, sizes = [], []
    for p in parts:
        flat = p.reshape(-1)
        n = flat.shape[0]
        padded = -(-n // (SUBLANES * LANES)) * (SUBLANES * LANES)
        rows.append(jnp.pad(flat, (0, padded - n)).reshape(padded // LANES, LANES))
        sizes.append((n, padded // LANES))
    return jnp.concatenate(rows, axis=0), sizes


def _unpack_rows(pack, sizes, shapes):
    out, r0 = [], 0
    for (n, nr), shp in zip(sizes, shapes):
        out.append(pack[r0:r0 + nr].reshape(-1)[:n].reshape(shp))
        r0 += nr
    return out


def kernel(x, positions, norm_mix_g, w_in, b_gate, q_norm_g, w_uq, kv_norm_g, w_ukv, w_o_attn, sgu_norm_g, w_sgu, b_sgu, w_o_sgu, w_out, norm_ffn_g, w_gate_ffn, w_up_ffn, w_down_ffn, norm_final_g, loss_target, m_norm_mix_g, m_w_in, m_b_gate, m_q_norm_g, m_w_uq, m_kv_norm_g, m_w_ukv, m_w_o_attn, m_sgu_norm_g, m_w_sgu, m_b_sgu, m_w_o_sgu, m_w_out, m_norm_ffn_g, m_w_gate_ffn, m_w_up_ffn, m_w_down_ffn, m_norm_final_g, v_norm_mix_g, v_w_in, v_b_gate, v_q_norm_g, v_w_uq, v_kv_norm_g, v_w_ukv, v_w_o_attn, v_sgu_norm_g, v_w_sgu, v_b_sgu, v_w_o_sgu, v_w_out, v_norm_ffn_g, v_w_gate_ffn, v_w_up_ffn, v_w_down_ffn, v_norm_final_g):
    wts = dict(norm_mix_g=norm_mix_g, w_in=w_in, b_gate=b_gate, q_norm_g=q_norm_g, w_uq=w_uq, kv_norm_g=kv_norm_g,
               w_ukv=w_ukv, w_o_attn=w_o_attn, sgu_norm_g=sgu_norm_g, w_sgu=w_sgu, b_sgu=b_sgu, w_o_sgu=w_o_sgu,
               w_out=w_out, norm_ffn_g=norm_ffn_g, w_gate_ffn=w_gate_ffn, w_up_ffn=w_up_ffn, w_down_ffn=w_down_ffn,
               norm_final_g=norm_final_g)
    mom = dict(norm_mix_g=m_norm_mix_g, w_in=m_w_in, b_gate=m_b_gate, q_norm_g=m_q_norm_g, w_uq=m_w_uq,
               kv_norm_g=m_kv_norm_g, w_ukv=m_w_ukv, w_o_attn=m_w_o_attn, sgu_norm_g=m_sgu_norm_g, w_sgu=m_w_sgu,
               b_sgu=m_b_sgu, w_o_sgu=m_w_o_sgu, w_out=m_w_out, norm_ffn_g=m_norm_ffn_g, w_gate_ffn=m_w_gate_ffn,
               w_up_ffn=m_w_up_ffn, w_down_ffn=m_w_down_ffn, norm_final_g=m_norm_final_g)
    var = dict(norm_mix_g=v_norm_mix_g, w_in=v_w_in, b_gate=v_b_gate, q_norm_g=v_q_norm_g, w_uq=v_w_uq,
               kv_norm_g=v_kv_norm_g, w_ukv=v_w_ukv, w_o_attn=v_w_o_attn, sgu_norm_g=v_sgu_norm_g, w_sgu=v_w_sgu,
               b_sgu=v_b_sgu, w_o_sgu=v_w_o_sgu, w_out=v_w_out, norm_ffn_g=v_norm_ffn_g, w_gate_ffn=v_w_gate_ffn,
               w_up_ffn=v_w_up_ffn, w_down_ffn=v_w_down_ffn, norm_final_g=v_norm_final_g)

    t, d = x.shape[1], x.shape[2]
    ql, kvl = q_norm_g.shape[1], kv_norm_g.shape[1]
    heads = (w_uq.shape[2] * N_DEV) // (QK_NOPE + QK_ROPE)
    sw = sgu_norm_g.shape[1]

    shards = {k: wts[k][0].astype(BF16) for k in _BIG}
    small = {
        "norm_mix_g": norm_mix_g, "b_gate": b_gate, "q_norm_g": q_norm_g, "kv_norm_g": kv_norm_g,
        "sgu_norm_g": sgu_norm_g, "w_sgu": w_sgu[0], "b_sgu_col": b_sgu[0][:, :, None], "norm_ffn_g": norm_ffn_g,
        "norm_final_g": norm_final_g[None, :],
    }

    opt = {k: (wts[k][0], mom[k][0], var[k][0]) for k in _BIG}
    loss_row, grad_x, gs, updates = _local_step(x[0], positions.reshape(t, 1), loss_target[0], small, shards, opt)
    grads, deltas, new_m, new_v = {}, {}, {}, {}
    for k in _BIG:
        grads[k], deltas[k], new_m[k], new_v[k] = (a.reshape(wts[k].shape) for a in updates[k])

    small_grads = [gs["norm_mix_g"], gs["b_gate"], gs["q_norm_g"], gs["kv_norm_g"], gs["sgu_norm_g"], gs["w_sgu"],
                   gs["b_sgu_col"], gs["norm_ffn_g"], gs["norm_final_g"]]
    pack, sizes = _pack_rows([loss_row] + small_grads)
    total = _all_reduce_pack(pack)
    shapes = [(1, LANES)] + [wts[k].shape for k in _SMALL]
    unpacked = _unpack_rows(total, sizes, shapes)
    loss = unpacked[0][0, 0]
    for k, g in zip(_SMALL, unpacked[1:]):
        grads[k] = g
    g_pack = total[sizes[0][1]:]
    w_pack, _ = _pack_rows([wts[k] for k in _SMALL])
    m_pack, _ = _pack_rows([mom[k] for k in _SMALL])
    v_pack, _ = _pack_rows([var[k] for k in _SMALL])
    d_pack, nm_pack, nv_pack = _adamw_pack(g_pack, w_pack, m_pack, v_pack)
    small_shapes = [wts[k].shape for k in _SMALL]
    for store, pk in ((deltas, d_pack), (new_m, nm_pack), (new_v, nv_pack)):
        for k, a in zip(_SMALL, _unpack_rows(pk, sizes[1:], small_shapes)):
            store[k] = a

    return (loss, grad_x[None], *[grads[k] for k in _ORDER], *[deltas[k] for k in _ORDER],
            *[new_m[k] for k in _ORDER], *[new_v[k] for k in _ORDER])
```

```python
import functools
import math

import jax
import jax.numpy as jnp
from jax import lax
from jax.experimental import pallas as pl
from jax.experimental.pallas import tpu as pltpu

F32 = jnp.float32
BF16 = jnp.bfloat16

N_DEV = 8
N_HEADS = 16
QK_NOPE = 128
QK_ROPE = 64
V_HEAD = 128
HEAD_PAD = 256
ROPE_THETA = 10000.0
CHUNK = 128
SGU_GROUP = 128
RMS_EPS = 1e-6
LANES = 128
SUBLANES = 8

ADAM_LR = 0.001
ADAM_B1 = 0.9
ADAM_B2 = 0.999
ADAM_EPS = 1e-08
ADAM_WD = 0.01
ADAM_STEP = 10

VMEM_LIMIT = 48 * 1024 * 1024
MM_TILE = (1024, 512, 2048)
ATTN_TILE = 512
ROW_KERNEL_BYTES = 24 * 1024 * 1024
SLABS_PER_STEP = 4
TAIL_CHUNKS = 4
NEG_BIG = -1e30
MESH = pl.DeviceIdType.MESH


def _pick(n, target, mult=LANES):
    best = None
    d = mult
    while d <= min(n, target):
        if n % d == 0:
            best = d
        d += mult
    return best or n


def _row_tile(t, width, n_blocks, mult=2 * SUBLANES):
    return _pick(t, max(mult, ROW_KERNEL_BYTES // (3 * n_blocks * width * 4)), mult)


def _params(sem):
    return pltpu.CompilerParams(dimension_semantics=sem, vmem_limit_bytes=VMEM_LIMIT)


def _full(shape):
    nd = len(shape)
    return pl.BlockSpec(shape, lambda *_: (0,) * nd)


def _rows(tr, w, cb=0):
    return pl.BlockSpec((tr, w), lambda i: (i, cb))


class _Comm:
    def __init__(self, ins, out_shapes, sems, start, finish, aliases=None):
        self.ins, self.out_shapes, self.sems, self.start, self.finish = list(ins), list(out_shapes), list(sems), start, finish
        self.aliases = dict(aliases or {})


def _call(body, *, grid, in_specs, out_specs, out_shape, scratch_shapes=(), sem, name, args, comm=None):
    if comm is None:
        outs = pl.pallas_call(body, grid=grid, in_specs=list(in_specs), out_specs=list(out_specs),
                              out_shape=list(out_shape), scratch_shapes=list(scratch_shapes),
                              compiler_params=_params(sem), name=name)(*args)
        return list(outs), []
    n_in, n_out, n_sc = len(in_specs), len(out_shape), len(scratch_shapes)
    nci, nco = len(comm.ins), len(comm.out_shapes)

    def hosted(*refs):
        ins, refs = refs[:n_in], refs[n_in:]
        cins, refs = refs[:nci], refs[nci:]
        outs, refs = refs[:n_out], refs[n_out:]
        couts, refs = refs[:nco], refs[nco:]
        scratch, csems = refs[:n_sc], refs[n_sc:]
        ids = [pl.program_id(i) for i in range(len(grid))]
        first = functools.reduce(jnp.logical_and, [i == 0 for i in ids])
        last = functools.reduce(jnp.logical_and, [i == g - 1 for i, g in zip(ids, grid)])

        @pl.when(first)
        def _():
            comm.start(cins, couts, csems)

        body(*ins, *outs, *scratch)

        @pl.when(last)
        def _():
            comm.finish(cins, couts, csems)

    any_spec = pl.BlockSpec(memory_space=pl.ANY)
    res = pl.pallas_call(
        hosted, grid=grid, in_specs=list(in_specs) + [any_spec] * nci, out_specs=list(out_specs) + [any_spec] * nco,
        out_shape=list(out_shape) + comm.out_shapes, scratch_shapes=list(scratch_shapes) + comm.sems,
        input_output_aliases={n_in + i: n_out + o for i, o in comm.aliases.items()},
        compiler_params=pltpu.CompilerParams(dimension_semantics=("arbitrary",) * len(grid),
                                             vmem_limit_bytes=VMEM_LIMIT, has_side_effects=True),
        name=name)(*args, *comm.ins)
    return list(res[:n_out]), list(res[n_out:])


def _mm(a, b, *, ta=False, tb=False, add=None, out_dtype=F32, tm=None, tn=None, tk=None, name, comm=None,
        slab=None, a_slab0=0):
    sq = None
    if slab is None:
        m, k = (a.shape[1], a.shape[0]) if ta else a.shape
        n = b.shape[0] if tb else b.shape[1]
        assert k == (b.shape[1] if tb else b.shape[0]), (a.shape, b.shape, ta, tb)
        tm, tn, tk = _pick(m, tm or MM_TILE[0]), _pick(n, tn or MM_TILE[1]), _pick(k, tk or MM_TILE[2])
        grid = (m // tm, n // tn, k // tk)
        a_spec = pl.BlockSpec((tk, tm), lambda i, j, kk: (kk, i)) if ta else pl.BlockSpec((tm, tk), lambda i, j, kk: (i, kk))
        b_spec = pl.BlockSpec((tn, tk), lambda i, j, kk: (j, kk)) if tb else pl.BlockSpec((tk, tn), lambda i, j, kk: (kk, j))
        o_spec, o_shape = pl.BlockSpec((tm, tn), lambda i, j, kk: (i, j)), (m, n)
    elif slab == "n":
        m, k = (a.shape[1], a.shape[0]) if ta else a.shape
        s, c = b.shape[0], (b.shape[1] if tb else b.shape[2])
        assert k == (b.shape[2] if tb else b.shape[1]), (a.shape, b.shape, ta, tb)
        tm, tn, tk = _pick(m, tm or MM_TILE[0]), c, _pick(k, tk or MM_TILE[2])
        grid = (m // tm, s, k // tk)
        a_spec = pl.BlockSpec((tk, tm), lambda i, j, kk: (kk, i)) if ta else pl.BlockSpec((tm, tk), lambda i, j, kk: (i, kk))
        b_spec = (pl.BlockSpec((sq, c, tk), lambda i, j, kk: (j, 0, kk)) if tb
                  else pl.BlockSpec((sq, tk, c), lambda i, j, kk: (j, kk, 0)))
        o_spec, o_shape = pl.BlockSpec((sq, tm, c), lambda i, j, kk: (j, i, 0)), (s, m, c)
    elif slab == "m":
        assert ta and not tb
        s, k, c = a.shape
        n = b.shape[1]
        assert k == b.shape[0], (a.shape, b.shape)
        tm, tn, tk = c, _pick(n, tn or MM_TILE[1]), _pick(k, tk or MM_TILE[2])
        grid = (s, n // tn, k // tk)
        a_spec = pl.BlockSpec((sq, tk, c), lambda i, j, kk: (i, kk, 0))
        b_spec = pl.BlockSpec((tk, tn), lambda i, j, kk: (kk, j))
        o_spec, o_shape = pl.BlockSpec((sq, c, tn), lambda i, j, kk: (i, 0, j)), (s, c, n)
    else:
        assert slab == "k" and not ta
        s, c = b.shape[0], (b.shape[2] if tb else b.shape[1])
        m, n = a.shape[1], (b.shape[1] if tb else b.shape[2])
        assert a.shape[2] == c and a.shape[0] >= a_slab0 + s, (a.shape, b.shape, a_slab0)
        tm, tn, tk = _pick(m, tm or MM_TILE[0]), _pick(n, tn or MM_TILE[1]), c
        per_step = SLABS_PER_STEP if (s % SLABS_PER_STEP == 0 and a_slab0 % SLABS_PER_STEP == 0) else 1
        first = a_slab0 // per_step
        grid = (m // tm, n // tn, s // per_step)
        a_spec = pl.BlockSpec((per_step, tm, c), lambda i, j, kk: (kk + first, i, 0))
        b_spec = (pl.BlockSpec((per_step, tn, c), lambda i, j, kk: (kk, j, 0)) if tb
                  else pl.BlockSpec((per_step, c, tn), lambda i, j, kk: (kk, 0, j)))
        o_spec, o_shape = pl.BlockSpec((tm, tn), lambda i, j, kk: (i, j)), (m, n)
    nk = grid[2]
    dims = (((0 if ta else 1,), (1 if tb else 0,)), ((), ()))

    def product(a_ref, b_ref):
        if slab != "k":
            return lax.dot_general(a_ref[...].astype(BF16), b_ref[...].astype(BF16), dims, preferred_element_type=F32)
        r = None
        for u in range(a_ref.shape[0]):
            p = lax.dot_general(a_ref[u].astype(BF16), b_ref[u].astype(BF16), dims, preferred_element_type=F32)
            r = p if r is None else r + p
        return r

    def body(*refs):
        if add is None:
            a_ref, b_ref, o_ref, acc_ref = refs
            add_ref = None
        else:
            a_ref, b_ref, add_ref, o_ref, acc_ref = refs
        kk = pl.program_id(2)

        @pl.when(kk == 0)
        def _():
            acc_ref[...] = jnp.zeros_like(acc_ref)

        acc_ref[...] += product(a_ref, b_ref)

        @pl.when(kk == nk - 1)
        def _():
            r = acc_ref[...]
            if add_ref is not None:
                r = r + add_ref[...].astype(F32)
            o_ref[...] = r.astype(o_ref.dtype)

    in_specs = [a_spec, b_spec] + ([o_spec] if add is not None else [])
    args = (a, b) + ((add,) if add is not None else ())
    outs, comm_outs = _call(
        body, grid=grid, in_specs=in_specs, out_specs=[o_spec],
        out_shape=[jax.ShapeDtypeStruct(o_shape, out_dtype)], scratch_shapes=[pltpu.VMEM((tm, tn), F32)],
        sem=("parallel", "parallel", "arbitrary"), name=name, args=args, comm=comm)
    return outs[0] if comm is None else (outs[0], comm_outs)


def _rms_scale(x):
    return lax.rsqrt(jnp.mean(x * x, axis=-1, keepdims=True) + RMS_EPS)


def _rms_bwd(xhat, r, g, dy):
    t = dy * g
    dx = r * (t - xhat * jnp.mean(t * xhat, axis=-1, keepdims=True))
    return dx, dy * xhat


_GELU_C = math.sqrt(2.0 / math.pi)


def _gelu(x):
    return x * (0.5 * (1.0 + jnp.tanh(_GELU_C * (x + 0.044715 * (x * x * x)))))


def _gelu_and_grad(x):
    t = jnp.tanh(_GELU_C * (x + 0.044715 * (x * x * x)))
    cdf = 0.5 * (1.0 + t)
    return x * cdf, cdf + x * (0.5 * (1.0 - t * t) * (_GELU_C * (1.0 + 3.0 * 0.044715 * (x * x))))


def _sigmoid(x):
    return 1.0 / (1.0 + jnp.exp(-x))


def _swap_halves(x):
    lane = lax.broadcasted_iota(jnp.int32, x.shape, 1)
    first = (lane % QK_ROPE) < (QK_ROPE // 2)
    return jnp.where(first, pltpu.roll(x, LANES - QK_ROPE // 2, 1), pltpu.roll(x, QK_ROPE // 2, 1))


def _rope(x, cos, sin_signed):
    return x * cos + _swap_halves(x) * sin_signed


def _rope_bwd(d, cos, sin_signed):
    return d * cos + _swap_halves(d * sin_signed)


def _rope_tables(pos_col, inv_freq_row, sign_row):
    t = pos_col.shape[0]
    tr = _pick(t, 512, SUBLANES)

    def body(p_ref, f_ref, s_ref, cos_ref, sin_ref):
        ang = p_ref[...].astype(F32) * f_ref[...]
        cos_ref[...] = jnp.cos(ang)
        sin_ref[...] = jnp.sin(ang) * s_ref[...]

    return pl.pallas_call(
        body, grid=(t // tr,), in_specs=[_rows(tr, 1), _full((1, LANES)), _full((1, LANES))],
        out_specs=[_rows(tr, LANES), _rows(tr, LANES)],
        out_shape=[jax.ShapeDtypeStruct((t, LANES), F32)] * 2,
        compiler_params=_params(("parallel",)), name="rope_tables")(pos_col, inv_freq_row, sign_row)


def _norm_fwd(x, g, name):
    t, d = x.shape
    tr = _row_tile(t, d, 2)

    def body(x_ref, g_ref, y_ref):
        xv = x_ref[...]
        y_ref[...] = (xv * _rms_scale(xv) * g_ref[...]).astype(BF16)

    return pl.pallas_call(
        body, grid=(t // tr,), in_specs=[_rows(tr, d), _full((1, d))], out_specs=_rows(tr, d),
        out_shape=jax.ShapeDtypeStruct((t, d), BF16), compiler_params=_params(("parallel",)), name=name)(x, g)


def _lat_fwd(z_lat, qg, kvg, cos, sin, ql, kvl):
    t = z_lat.shape[0]
    tr = _row_tile(t, z_lat.shape[1], 2)

    def body(z_ref, qg_ref, kvg_ref, cos_ref, sin_ref, qn_ref, kvn_ref, kpe_ref):
        q = z_ref[:, 0:ql]
        qn_ref[...] = (q * _rms_scale(q) * qg_ref[...]).astype(BF16)
        kv = z_ref[:, ql:ql + kvl]
        kvn_ref[...] = (kv * _rms_scale(kv) * kvg_ref[...]).astype(BF16)
        kpe_ref[...] = _rope(z_ref[:, ql + kvl:ql + kvl + LANES], cos_ref[...], sin_ref[...]).astype(BF16)

    w = z_lat.shape[1]
    return pl.pallas_call(
        body, grid=(t // tr,),
        in_specs=[_rows(tr, w), _full((1, ql)), _full((1, kvl)), _rows(tr, LANES), _rows(tr, LANES)],
        out_specs=[_rows(tr, ql), _rows(tr, kvl), _rows(tr, LANES)],
        out_shape=[jax.ShapeDtypeStruct((t, ql), BF16), jax.ShapeDtypeStruct((t, kvl), BF16),
                   jax.ShapeDtypeStruct((t, LANES), BF16)],
        compiler_params=_params(("parallel",)), name="lat_fwd")(z_lat, qg, kvg, cos, sin)


def _q_rope(q_p, cos, sin, bwd, name):
    t, w = q_p.shape
    tr = _row_tile(t, w, 2)
    fn = _rope_bwd if bwd else _rope

    def body(q_ref, cos_ref, sin_ref, o_ref):
        c, s = cos_ref[...], sin_ref[...]
        for h in range(w // HEAD_PAD):
            o_ref[:, h * HEAD_PAD:h * HEAD_PAD + QK_NOPE] = q_ref[:, h * HEAD_PAD:h * HEAD_PAD + QK_NOPE].astype(BF16)
            lo = h * HEAD_PAD + QK_NOPE
            o_ref[:, lo:lo + LANES] = fn(q_ref[:, lo:lo + LANES].astype(F32), c, s).astype(BF16)

    return pl.pallas_call(
        body, grid=(t // tr,), in_specs=[_rows(tr, w), _rows(tr, LANES), _rows(tr, LANES)], out_specs=_rows(tr, w),
        out_shape=jax.ShapeDtypeStruct((t, w), BF16), compiler_params=_params(("parallel",)), name=name)(q_p, cos, sin)


def _tril_mask():
    r = lax.broadcasted_iota(jnp.int32, (CHUNK, CHUNK), 0)
    c = lax.broadcasted_iota(jnp.int32, (CHUNK, CHUNK), 1)
    return r >= c


def _sgu_fwd(z_uv, gs, ws, b_col):
    t = z_uv.shape[0]
    sw = z_uv.shape[1] // 2
    groups = sw // SGU_GROUP
    tr = _pick(t, 256, CHUNK)

    def body(u_ref, v_ref, gs_ref, ws_ref, b_ref, o_ref):
        v = _gelu(v_ref[...])
        vn = (v * _rms_scale(v) * gs_ref[...]).astype(BF16)
        tri = _tril_mask()
        for g in range(groups):
            wg = jnp.where(tri, ws_ref[g], 0.0).astype(BF16)
            cols = slice(g * SGU_GROUP, (g + 1) * SGU_GROUP)
            for c in range(tr // CHUNK):
                rows = slice(c * CHUNK, (c + 1) * CHUNK)
                mixed = jnp.dot(wg, vn[rows, cols], preferred_element_type=F32) + b_ref[g]
                o_ref[rows, cols] = (_gelu(u_ref[rows, cols]) * mixed).astype(BF16)

    return pl.pallas_call(
        body, grid=(t // tr,),
        in_specs=[_rows(tr, sw, 0), _rows(tr, sw, 1), _full((1, sw)), _full(ws.shape), _full(b_col.shape)],
        out_specs=_rows(tr, sw), out_shape=jax.ShapeDtypeStruct((t, sw), BF16),
        compiler_params=_params(("parallel",)), name="sgu_fwd")(z_uv, z_uv, gs, ws, b_col)


def _merge_fwd(y_attn, y_sgu, z_g, b_gate):
    t, d = y_attn.shape
    tr = _row_tile(t, d, 5)

    def body(ya_ref, ys_ref, g0_ref, g1_ref, b0_ref, b1_ref, o_ref):
        g0 = _sigmoid(g0_ref[...] + b0_ref[...])
        g1 = _sigmoid(g1_ref[...] + b1_ref[...])
        o_ref[...] = (g0 * ya_ref[...] + g1 * ys_ref[...]).astype(BF16)

    bspec0 = pl.BlockSpec((1, d), lambda i: (0, 0))
    bspec1 = pl.BlockSpec((1, d), lambda i: (0, 1))
    return pl.pallas_call(
        body, grid=(t // tr,),
        in_specs=[_rows(tr, d), _rows(tr, d), _rows(tr, d, 0), _rows(tr, d, 1), bspec0, bspec1],
        out_specs=_rows(tr, d), out_shape=jax.ShapeDtypeStruct((t, d), BF16),
        compiler_params=_params(("parallel",)), name="merge_fwd")(y_attn, y_sgu, z_g, z_g, b_gate, b_gate)


def _swiglu_fwd(gate, up, comm=None):
    t, f = gate.shape
    tr = _row_tile(t, f, 3)

    def body(g_ref, u_ref, o_ref):
        g = g_ref[...]
        o_ref[...] = (g * _sigmoid(g) * u_ref[...]).astype(BF16)

    outs, comm_outs = _call(
        body, grid=(t // tr,), in_specs=[_rows(tr, f), _rows(tr, f)], out_specs=[_rows(tr, f)],
        out_shape=[jax.ShapeDtypeStruct((t, f), BF16)], sem=("parallel",), name="swiglu_fwd", args=(gate, up), comm=comm)
    return outs[0], comm_outs


def _loss_head(h2, g, target):
    t, d = h2.shape
    tr = _row_tile(t, d, 3)

    def body(h_ref, g_ref, t_ref, loss_ref, dh_ref, dg_ref):
        @pl.when(pl.program_id(0) == 0)
        def _():
            loss_ref[...] = jnp.zeros_like(loss_ref)
            dg_ref[...] = jnp.zeros_like(dg_ref)

        h = h_ref[...]
        r = _rms_scale(h)
        hhat = h * r
        gv = g_ref[...]
        err = hhat * gv - t_ref[...]
        loss_ref[...] += jnp.full(loss_ref.shape, 0.5 * jnp.sum(jnp.mean(err * err, axis=-1)), F32)
        dx, dg_rows = _rms_bwd(hhat, r, gv, err * (1.0 / d))
        dh_ref[...] = dx
        dg_ref[...] += jnp.sum(dg_rows, axis=0, keepdims=True)

    return pl.pallas_call(
        body, grid=(t // tr,), in_specs=[_rows(tr, d), _full((1, d)), _rows(tr, d)],
        out_specs=[_full((1, LANES)), _rows(tr, d), _full((1, d))],
        out_shape=[jax.ShapeDtypeStruct((1, LANES), F32), jax.ShapeDtypeStruct((t, d), F32),
                   jax.ShapeDtypeStruct((1, d), F32)],
        compiler_params=_params(("arbitrary",)), name="loss_head")(h2, g, target)


def _swiglu_bwd(gate, up, dact):
    t, f = gate.shape
    tr = _row_tile(t, f, 4)

    def body(g_ref, u_ref, d_ref, dgu_ref):
        g = g_ref[...]
        s = _sigmoid(g)
        d = d_ref[...]
        dgu_ref[0] = (d * u_ref[...] * (s * (1.0 + g * (1.0 - s)))).astype(BF16)
        dgu_ref[1] = (d * (g * s)).astype(BF16)

    return pl.pallas_call(
        body, grid=(t // tr,), in_specs=[_rows(tr, f)] * 3, out_specs=pl.BlockSpec((2, tr, f), lambda i: (0, i, 0)),
        out_shape=jax.ShapeDtypeStruct((2, t, f), BF16),
        compiler_params=_params(("parallel",)), name="swiglu_bwd")(gate, up, dact)


def _norm_bwd(x, g, dy, resid, name, comm=None):
    t, d = x.shape
    tr = _row_tile(t, d, 4)

    def body(x_ref, g_ref, dy_ref, r_ref, dx_ref, dg_ref):
        @pl.when(pl.program_id(0) == 0)
        def _():
            dg_ref[...] = jnp.zeros_like(dg_ref)

        xv = x_ref[...]
        r = _rms_scale(xv)
        dx, dg_rows = _rms_bwd(xv * r, r, g_ref[...], dy_ref[...])
        dx_ref[...] = r_ref[...] + dx
        dg_ref[...] += jnp.sum(dg_rows, axis=0, keepdims=True)

    outs, comm_outs = _call(
        body, grid=(t // tr,), in_specs=[_rows(tr, d), _full((1, d)), _rows(tr, d), _rows(tr, d)],
        out_specs=[_rows(tr, d), _full((1, d))],
        out_shape=[jax.ShapeDtypeStruct((t, d), F32), jax.ShapeDtypeStruct((1, d), F32)],
        sem=("arbitrary",), name=name, args=(x, g, dy, resid), comm=comm)
    return (outs[0], outs[1]) if comm is None else (outs[0], outs[1], comm_outs)


def _merge_bwd(dmerged, y_attn, y_sgu, z_g, b_gate):
    t, d = y_attn.shape
    tr = _row_tile(t, d, 7)

    def body(dm_ref, ya_ref, ys_ref, g0_ref, g1_ref, b0_ref, b1_ref, dya_ref, dys_ref, dz_ref, db_ref):
        @pl.when(pl.program_id(0) == 0)
        def _():
            db_ref[...] = jnp.zeros_like(db_ref)

        dm = dm_ref[...]
        g0 = _sigmoid(g0_ref[...] + b0_ref[...])
        g1 = _sigmoid(g1_ref[...] + b1_ref[...])
        dya_ref[...] = (dm * g0).astype(BF16)
        dys_ref[...] = (dm * g1).astype(BF16)
        dl0 = dm * ya_ref[...] * (g0 * (1.0 - g0))
        dl1 = dm * ys_ref[...] * (g1 * (1.0 - g1))
        dz_ref[:, 0:d] = dl0.astype(BF16)
        dz_ref[:, d:2 * d] = dl1.astype(BF16)
        db_ref[:, 0:d] += jnp.sum(dl0, axis=0, keepdims=True)
        db_ref[:, d:2 * d] += jnp.sum(dl1, axis=0, keepdims=True)

    bspec0 = pl.BlockSpec((1, d), lambda i: (0, 0))
    bspec1 = pl.BlockSpec((1, d), lambda i: (0, 1))
    return pl.pallas_call(
        body, grid=(t // tr,),
        in_specs=[_rows(tr, d), _rows(tr, d), _rows(tr, d), _rows(tr, d, 0), _rows(tr, d, 1), bspec0, bspec1],
        out_specs=[_rows(tr, d), _rows(tr, d), _rows(tr, 2 * d), _full((1, 2 * d))],
        out_shape=[jax.ShapeDtypeStruct((t, d), BF16), jax.ShapeDtypeStruct((t, d), BF16),
                   jax.ShapeDtypeStruct((t, 2 * d), BF16), jax.ShapeDtypeStruct((1, 2 * d), F32)],
        compiler_params=_params(("arbitrary",)), name="merge_bwd")(dmerged, y_attn, y_sgu, z_g, z_g, b_gate, b_gate)


def _sgu_bwd(z_uv, ds_out, gs, ws, b_col):
    t = z_uv.shape[0]
    sw = z_uv.shape[1] // 2
    groups = sw // SGU_GROUP
    tr = _pick(t, 256, CHUNK)

    def body(u_ref, v_ref, d_ref, gs_ref, ws_ref, b_ref, dz_ref, dws_ref, db_ref, dgs_ref, dvn_ref):
        @pl.when(pl.program_id(0) == 0)
        def _():
            dws_ref[...] = jnp.zeros_like(dws_ref)
            db_ref[...] = jnp.zeros_like(db_ref)
            dgs_ref[...] = jnp.zeros_like(dgs_ref)

        v, dgelu_v = _gelu_and_grad(v_ref[...])
        r = _rms_scale(v)
        vhat = v * r
        gsv = gs_ref[...]
        vn = (vhat * gsv).astype(BF16)
        tri = _tril_mask()
        for g in range(groups):
            wg = jnp.where(tri, ws_ref[g], 0.0).astype(BF16)
            cols = slice(g * SGU_GROUP, (g + 1) * SGU_GROUP)
            for c in range(tr // CHUNK):
                rows = slice(c * CHUNK, (c + 1) * CHUNK)
                vn_cg = vn[rows, cols]
                mixed = jnp.dot(wg, vn_cg, preferred_element_type=F32) + b_ref[g]
                u, dgelu_u = _gelu_and_grad(u_ref[rows, cols])
                dso = d_ref[rows, cols]
                dz_ref[rows, cols] = (dso * mixed * dgelu_u).astype(BF16)
                dmixed = dso * u
                db_ref[g] += jnp.sum(dmixed, axis=1, keepdims=True)
                dmixed_b = dmixed.astype(BF16)
                dws_ref[g] += jnp.where(
                    tri, lax.dot_general(dmixed_b, vn_cg, (((1,), (1,)), ((), ())), preferred_element_type=F32), 0.0)
                dvn_ref[rows, cols] = lax.dot_general(wg, dmixed_b, (((0,), (0,)), ((), ())), preferred_element_type=F32)
        dvn = dvn_ref[...]
        dv, dgs_rows = _rms_bwd(vhat, r, gsv, dvn)
        dz_ref[:, sw:2 * sw] = (dv * dgelu_v).astype(BF16)
        dgs_ref[...] += jnp.sum(dgs_rows, axis=0, keepdims=True)

    return pl.pallas_call(
        body, grid=(t // tr,),
        in_specs=[_rows(tr, sw, 0), _rows(tr, sw, 1), _rows(tr, sw), _full((1, sw)), _full(ws.shape), _full(b_col.shape)],
        out_specs=[_rows(tr, 2 * sw), _full(ws.shape), _full(b_col.shape), _full((1, sw))],
        out_shape=[jax.ShapeDtypeStruct((t, 2 * sw), BF16), jax.ShapeDtypeStruct(ws.shape, F32),
                   jax.ShapeDtypeStruct(b_col.shape, F32), jax.ShapeDtypeStruct((1, sw), F32)],
        scratch_shapes=[pltpu.VMEM((tr, sw), F32)],
        compiler_params=_params(("arbitrary",)), name="sgu_bwd")(z_uv, z_uv, ds_out, gs, ws, b_col)


def _lat_bwd(z_lat, qg, kvg, dqn, dkvn, dkpe_heads, cos, sin, ql, kvl):
    t, w = z_lat.shape
    heads = dkpe_heads.shape[0]
    tr = _row_tile(t, w + heads * LANES, 3)

    def body(z_ref, qg_ref, kvg_ref, dq_ref, dkv_ref, dk_ref, cos_ref, sin_ref, dz_ref, dqg_ref, dkvg_ref):
        @pl.when(pl.program_id(0) == 0)
        def _():
            dqg_ref[...] = jnp.zeros_like(dqg_ref)
            dkvg_ref[...] = jnp.zeros_like(dkvg_ref)

        q = z_ref[:, 0:ql]
        r = _rms_scale(q)
        dx, dg_rows = _rms_bwd(q * r, r, qg_ref[...], dq_ref[...])
        dz_ref[:, 0:ql] = dx.astype(BF16)
        dqg_ref[...] += jnp.sum(dg_rows, axis=0, keepdims=True)
        kv = z_ref[:, ql:ql + kvl]
        r = _rms_scale(kv)
        dx, dg_rows = _rms_bwd(kv * r, r, kvg_ref[...], dkv_ref[...])
        dz_ref[:, ql:ql + kvl] = dx.astype(BF16)
        dkvg_ref[...] += jnp.sum(dg_rows, axis=0, keepdims=True)
        dk = dk_ref[0]
        for h in range(1, heads):
            dk = dk + dk_ref[h]
        dz_ref[:, ql + kvl:ql + kvl + LANES] = _rope_bwd(dk, cos_ref[...], sin_ref[...]).astype(BF16)

    return pl.pallas_call(
        body, grid=(t // tr,),
        in_specs=[_rows(tr, w), _full((1, ql)), _full((1, kvl)), _rows(tr, ql), _rows(tr, kvl),
                  pl.BlockSpec((heads, tr, LANES), lambda i: (0, i, 0)), _rows(tr, LANES), _rows(tr, LANES)],
        out_specs=[_rows(tr, w), _full((1, ql)), _full((1, kvl))],
        out_shape=[jax.ShapeDtypeStruct((t, w), BF16), jax.ShapeDtypeStruct((1, ql), F32),
                   jax.ShapeDtypeStruct((1, kvl), F32)],
        compiler_params=_params(("arbitrary",)), name="lat_bwd")(z_lat, qg, kvg, dqn, dkvn, dkpe_heads, cos, sin)


_NT = (((1,), (1,)), ((), ()))


def _attn_scale():
    return (QK_NOPE + QK_ROPE) ** -0.5


def _attn_fwd(q_c, kv, kpe, comm=None):
    t = q_c.shape[0]
    heads = q_c.shape[1] // HEAD_PAD
    tq = _pick(t, ATTN_TILE)
    nq = t // tq
    scale = _attn_scale()
    to_log2 = scale * math.log2(math.e)
    tn_dims = (((0,), (0,)), ((), ()))

    def body(q_ref, kn_ref, kpe_ref, v_ref, o_ref, lse_ref, m_sc, l_sc, acc_sc):
        qi, ki = pl.program_id(1), pl.program_id(2)

        @pl.when(ki == 0)
        def _():
            m_sc[...] = jnp.full_like(m_sc, NEG_BIG)
            l_sc[...] = jnp.zeros_like(l_sc)
            acc_sc[...] = jnp.zeros_like(acc_sc)

        def step(diagonal):
            kc = jnp.concatenate([kn_ref[...], kpe_ref[...]], axis=1)
            st = lax.dot_general(kc, q_ref[...], _NT, preferred_element_type=F32)
            if diagonal:
                krow = lax.broadcasted_iota(jnp.int32, st.shape, 0)
                qcol = lax.broadcasted_iota(jnp.int32, st.shape, 1)
                st = jnp.where(qcol >= krow, st, NEG_BIG)
            m_prev = m_sc[...]
            m_new = jnp.maximum(m_prev, jnp.max(st, axis=0, keepdims=True))
            alpha = jnp.exp2((m_prev - m_new) * to_log2)
            pt = jnp.exp2((st - m_new) * to_log2)
            l_sc[...] = alpha * l_sc[...] + jnp.sum(pt, axis=0, keepdims=True)
            acc_sc[...] = alpha * acc_sc[...] + lax.dot_general(v_ref[...], pt.astype(BF16), tn_dims,
                                                                preferred_element_type=F32)
            m_sc[...] = m_new

        @pl.when(ki < qi)
        def _():
            step(False)

        @pl.when(ki == qi)
        def _():
            step(True)
            o_ref[...] = (acc_sc[...] / l_sc[...]).T
            lse_ref[0] = m_sc[...] * scale + jnp.log(l_sc[...])

    kmap = lambda blk: (lambda h, qi, ki: (jnp.minimum(ki, qi), 2 * h + blk))
    outs, comm_outs = _call(
        body, grid=(heads, nq, nq),
        in_specs=[pl.BlockSpec((tq, HEAD_PAD), lambda h, qi, ki: (qi, h)),
                  pl.BlockSpec((tq, QK_NOPE), kmap(0)),
                  pl.BlockSpec((tq, LANES), lambda h, qi, ki: (jnp.minimum(ki, qi), 0)),
                  pl.BlockSpec((tq, V_HEAD), kmap(1))],
        out_specs=[pl.BlockSpec((tq, V_HEAD), lambda h, qi, ki: (qi, h)),
                   pl.BlockSpec((1, 1, tq), lambda h, qi, ki: (h, 0, qi))],
        out_shape=[jax.ShapeDtypeStruct((t, heads * V_HEAD), F32), jax.ShapeDtypeStruct((heads, 1, t), F32)],
        scratch_shapes=[pltpu.VMEM((1, tq), F32), pltpu.VMEM((1, tq), F32), pltpu.VMEM((V_HEAD, tq), F32)],
        sem=("parallel", "parallel", "arbitrary"), name="attn_fwd", args=(q_c, kv, kpe, kv), comm=comm)
    return outs[0], outs[1], comm_outs


def _attn_bwd(q_c, kv, kpe, o, do, lse_row, comm=None):
    t = q_c.shape[0]
    heads = q_c.shape[1] // HEAD_PAD
    tk = _pick(t, ATTN_TILE)
    nk = t // tk
    scale = _attn_scale()
    tn_dims = (((0,), (0,)), ((), ()))

    def body(q_ref, kn_ref, kpe_ref, v_ref, do_ref, lse_ref, o_ref, dq_ref, dkv_ref, dkpe_ref, dk_sc, dv_sc, delta_sc):
        ki, qi = pl.program_id(1), pl.program_id(2)

        @pl.when(jnp.logical_and(ki == 0, qi == 0))
        def _():
            dq_ref[...] = jnp.zeros_like(dq_ref)

        @pl.when(qi == 0)
        def _():
            dk_sc[...] = jnp.zeros_like(dk_sc)
            dv_sc[...] = jnp.zeros_like(dv_sc)

        @pl.when(ki == 0)
        def _():
            delta_sc[qi] = jnp.sum((do_ref[...] * o_ref[...]).T, axis=0, keepdims=True)

        def step(diagonal):
            kc = jnp.concatenate([kn_ref[...], kpe_ref[...]], axis=1)
            q = q_ref[...]
            st = lax.dot_general(kc, q, _NT, preferred_element_type=F32) * scale
            pt = jnp.exp(st - lse_ref[0])
            if diagonal:
                krow = lax.broadcasted_iota(jnp.int32, st.shape, 0)
                qcol = lax.broadcasted_iota(jnp.int32, st.shape, 1)
                pt = jnp.where(qcol >= krow, pt, 0.0)
            do_b = do_ref[...].astype(BF16)
            dv_sc[...] += jnp.dot(pt.astype(BF16), do_b, preferred_element_type=F32)
            dpt = lax.dot_general(v_ref[...], do_b, _NT, preferred_element_type=F32)
            dst = (pt * (dpt - delta_sc[qi]) * scale).astype(BF16)
            dk_sc[...] += jnp.dot(dst, q, preferred_element_type=F32)
            rows = pl.ds(pl.multiple_of(qi * tk, tk), tk)
            dq_ref[rows, :] += lax.dot_general(dst, kc, tn_dims, preferred_element_type=F32)

        @pl.when(qi > ki)
        def _():
            step(False)

        @pl.when(qi == ki)
        def _():
            step(True)

        @pl.when(qi == nk - 1)
        def _():
            dkv_ref[:, 0:QK_NOPE] = dk_sc[:, 0:QK_NOPE].astype(BF16)
            dkv_ref[:, QK_NOPE:QK_NOPE + V_HEAD] = dv_sc[...].astype(BF16)
            dkpe_ref[0] = dk_sc[:, QK_NOPE:QK_NOPE + LANES]

    qclamp = lambda h, ki, qi: (jnp.maximum(qi, ki), h)
    kmap = lambda blk: (lambda h, ki, qi: (ki, 2 * h + blk))
    rmap = lambda h, ki, qi: (h, 0, jnp.maximum(qi, ki))
    outs, comm_outs = _call(
        body, grid=(heads, nk, nk),
        in_specs=[pl.BlockSpec((tk, HEAD_PAD), qclamp), pl.BlockSpec((tk, QK_NOPE), kmap(0)),
                  pl.BlockSpec((tk, LANES), lambda h, ki, qi: (ki, 0)), pl.BlockSpec((tk, V_HEAD), kmap(1)),
                  pl.BlockSpec((tk, V_HEAD), qclamp), pl.BlockSpec((1, 1, tk), rmap),
                  pl.BlockSpec((tk, V_HEAD), lambda h, ki, qi: (jnp.where(ki == 0, qi, 0), h))],
        out_specs=[pl.BlockSpec((t, HEAD_PAD), lambda h, ki, qi: (0, h)),
                   pl.BlockSpec((tk, HEAD_PAD), lambda h, ki, qi: (ki, h)),
                   pl.BlockSpec((1, tk, LANES), lambda h, ki, qi: (h, ki, 0))],
        out_shape=[jax.ShapeDtypeStruct((t, heads * HEAD_PAD), F32),
                   jax.ShapeDtypeStruct((t, heads * HEAD_PAD), BF16), jax.ShapeDtypeStruct((heads, t, LANES), F32)],
        scratch_shapes=[pltpu.VMEM((tk, HEAD_PAD), F32), pltpu.VMEM((tk, V_HEAD), F32), pltpu.VMEM((nk, 1, tk), F32)],
        sem=("parallel", "arbitrary", "arbitrary"), name="attn_bwd",
        args=(q_c, kv, kpe, kv, do, lse_row, o), comm=comm)
    return outs[0], outs[1], outs[2], comm_outs


def _local_step(x, pos_col, target, small, shards, opt):
    t = x.shape[0]
    ql, kvl = small["q_norm_g"].shape[1], small["kv_norm_g"].shape[1]
    sw = small["sgu_norm_g"].shape[1]
    heads = (shards["w_uq"].shape[1] * N_DEV) // (QK_NOPE + QK_ROPE)
    big = {}
    early = ["w_in", "w_uq", "w_ukv"]
    big.update(_compute_layout(dict(zip(early, _all_gather([shards[k] for k in early]))), ql, kvl, heads, sw))
    half = QK_ROPE // 2
    lane = jnp.arange(LANES)
    inv_freq = ROPE_THETA ** (-jnp.arange(0, QK_ROPE, 2, dtype=F32) / QK_ROPE)
    inv_row = inv_freq[lane % half][None, :]
    sign_row = jnp.where((lane % QK_ROPE) < half, -1.0, 1.0).astype(F32)[None, :]
    cos, sin = _rope_tables(pos_col, inv_row, sign_row)
    ws = small["w_sgu"]
    b_col = small["b_sgu_col"]

    def arrived(names, bufs):
        big.update(_compute_layout(dict(zip(names, bufs)), ql, kvl, heads, sw))

    a = _norm_fwd(x, small["norm_mix_g"], "norm_mix_fwd")
    z_lat = _mm(a, big["w_lat"], name="z_lat")
    z_uv = _mm(a, big["w_uv"], name="z_uv")
    mixers = ["w_o_sgu", "w_o_attn"]
    z_g, bufs = _mm(a, big["w_g"], name="z_g", comm=_gather_first([shards[k] for k in mixers]))
    qn, kvn, kpe = _lat_fwd(z_lat, small["q_norm_g"], small["kv_norm_g"], cos, sin, ql, kvl)
    q_p, bufs = _mm(qn, big["w_uq"], name="q_up", comm=_gather_second(bufs))
    arrived(mixers, bufs)
    kv = _mm(kvn, big["w_ukv"], out_dtype=BF16, name="kv_up")
    q_c = _q_rope(q_p, cos, sin, False, "q_rope")
    wide = ["w_out", "w_gate_ffn", "w_up_ffn"]
    attn, lse, bufs = _attn_fwd(q_c, kv, kpe, comm=_gather_first([shards[k] for k in wide]))
    s_out = _sgu_fwd(z_uv, small["sgu_norm_g"], ws, b_col)
    y_sgu = _mm(s_out, big["w_o_sgu"], name="y_sgu")
    y_attn, bufs = _mm(attn, big["w_o_attn"], name="y_attn", comm=_gather_second(bufs))
    arrived(wide[:1], bufs[:1])
    w_gate, w_up = bufs[1:]
    merged = _merge_fwd(y_attn, y_sgu, z_g, small["b_gate"])
    h1 = _mm(merged, big["w_out"], add=x, name="h1")
    f = _norm_fwd(h1, small["norm_ffn_g"], "norm_ffn_fwd")
    down = shards["w_down_ffn"]
    top = _pick(down.shape[0], down.shape[0] // 2, 2 * SUBLANES)
    gate, bufs = _mm(f, w_gate, slab="n", name="ffn_gate", comm=_gather_first([down], rows=[(0, top)]))
    up, bufs = _mm(f, w_up, slab="n", name="ffn_up",
                   comm=_gather_first([down], rows=[(top, down.shape[0] - top)], into=bufs))
    ffn = gate.shape[2]
    gate, up = gate.reshape(N_DEV * t, ffn), up.reshape(N_DEV * t, ffn)
    act, (w_down,) = _swiglu_fwd(gate, up, comm=_gather_second(bufs))
    act = act.reshape(N_DEV, t, ffn)
    h2 = _mm(act, w_down, slab="k", add=h1, name="h2")
    loss_row, dh2, d_norm_final = _loss_head(h2, small["norm_final_g"], target)

    def pair_sums(names, slabs, bufs):
        return [_pair_sum(g, b, "pair_sum_" + k) for k, g, b in zip(names, slabs, bufs)]

    parts, updates = {}, {}

    def update(k, comm=None):
        w, m, v = opt[k]
        updates[k], got = _adamw_shard(parts[k], w, m, v, "adamw_" + k, comm=comm)
        return got

    down_slabs = [_mm(act, dh2, ta=True, slab="m", out_dtype=BF16, name="dw_down")]
    dact, bufs = _mm(dh2, w_down, tb=True, slab="n", name="dact", comm=_to_sibling(down_slabs))
    down_pair = pair_sums(["w_down_ffn"], down_slabs, bufs)
    dgu = _swiglu_bwd(gate, up, dact.reshape(N_DEV * t, ffn)).reshape(2 * N_DEV, t, ffn)
    dw_gu, got = _mm(f, dgu, ta=True, slab="n", out_dtype=BF16, name="dw_gate_up", comm=_to_chips(down_pair))
    parts["w_down_ffn"] = got[0]
    gu_names = ["w_gate_ffn", "w_up_ffn"]
    df, bufs = _mm(dgu, w_gate, tb=True, slab="k", name="df_gate", comm=_to_sibling([dw_gu, dw_gu], first=[0, N_DEV]))
    gu_pairs = [_pair_sum(dw_gu, b, "pair_sum_" + k, first=s0) for k, b, s0 in zip(gu_names, bufs, [0, N_DEV])]
    df = _mm(dgu, w_up, tb=True, slab="k", a_slab0=N_DEV, add=df, name="df_up")
    dh1, d_norm_ffn = _norm_bwd(h1, small["norm_ffn_g"], df, dh2, "norm_ffn_bwd")
    dw_out = _mm(merged, dh1, ta=True, out_dtype=BF16, name="dw_out")
    out_slabs = [_slabs_from_rows(dw_out)]
    dmerged, bufs = _mm(dh1, big["w_out"], tb=True, name="dmerged", comm=_to_sibling(out_slabs))
    out_pair = pair_sums(["w_out"], out_slabs, bufs)
    dy_attn, dy_sgu, dz_g, d_b_gate = _merge_bwd(dmerged, y_attn, y_sgu, z_g, small["b_gate"])
    dw_o_sgu = _mm(s_out, dy_sgu, ta=True, out_dtype=BF16, name="dw_o_sgu")
    ds_out = _mm(dy_sgu, big["w_o_sgu"], tb=True, name="ds_out")
    dz_uv, d_ws, d_b_col, d_sgu_norm = _sgu_bwd(z_uv, ds_out, small["sgu_norm_g"], ws, b_col)
    dw_o_attn = _mm(attn, dy_attn, ta=True, out_dtype=BF16, name="dw_o_attn")
    mix_names = ["w_o_sgu", "w_o_attn"]
    mix_slabs = [_slabs_from_cols(dw_o_sgu), _slabs_from_rows(dw_o_attn)]
    dattn, bufs = _mm(dy_attn, big["w_o_attn"], tb=True, name="dattn", comm=_to_sibling(mix_slabs))
    mix_pairs = pair_sums(mix_names, mix_slabs, bufs)
    dq_c, dkv, dkpe_heads, got = _attn_bwd(q_c, kv, kpe, attn, dattn, lse, comm=_to_chips(gu_pairs))
    parts.update(zip(gu_names, got))
    dq_p = _q_rope(dq_c, cos, sin, True, "q_rope_bwd")
    dw_uq = _mm(qn, dq_p, ta=True, out_dtype=BF16, name="dw_uq")
    dw_ukv = _mm(kvn, dkv, ta=True, out_dtype=BF16, name="dw_ukv")
    dqn = _mm(dq_p, big["w_uq"], tb=True, name="dqn")
    dkvn = _mm(dkv, big["w_ukv"], tb=True, name="dkvn")
    dz_lat, d_q_norm, d_kv_norm = _lat_bwd(z_lat, small["q_norm_g"], small["kv_norm_g"], dqn, dkvn, dkpe_heads,
                                           cos, sin, ql, kvl)
    dw_g, got = _mm(a, dz_g, ta=True, out_dtype=BF16, name="dw_g", comm=_to_chips(out_pair))
    parts["w_out"] = got[0]
    dw_uv, got = _mm(a, dz_uv, ta=True, out_dtype=BF16, name="dw_uv", comm=_to_chips(mix_pairs[1:]))
    parts["w_o_attn"] = got[0]
    dw_lat, got = _mm(a, dz_lat, ta=True, out_dtype=BF16, name="dw_lat", comm=_to_chips(mix_pairs[:1]))
    parts["w_o_sgu"] = got[0]
    lat = ql + kvl + QK_ROPE
    dw_uq_cols = dw_uq.reshape(ql, heads, HEAD_PAD)[:, :, :QK_NOPE + QK_ROPE].reshape(ql, heads * (QK_NOPE + QK_ROPE))
    in_names = ["w_uq", "w_ukv", "w_in"]
    in_slabs = [_slabs_from_cols(dw_uq_cols), _slabs_from_cols(dw_ukv),
                _slabs_from_cols(jnp.concatenate([dw_lat[:, :lat], dw_uv, dw_g], axis=1))]
    da = _mm(dz_lat, big["w_lat"], tb=True, name="da_lat")
    da, bufs = _mm(dz_uv, big["w_uv"], tb=True, add=da, name="da_uv", comm=_to_sibling(in_slabs))
    uq_pair, ukv_pair, in_pair = pair_sums(in_names, in_slabs, bufs)
    rows = in_pair.shape[1]
    chunk = _pick(rows, rows // TAIL_CHUNKS, 2 * SUBLANES)
    chunks = [(r0, chunk) for r0 in range(0, rows, chunk)]
    da, got = _mm(dz_g, big["w_g"], tb=True, add=da, name="da_g",
                  comm=_to_chips([uq_pair, in_pair], rows=[(0, uq_pair.shape[1]), chunks[0]]))
    parts["w_uq"], in_parts = got
    grad_x, d_norm_mix, got = _norm_bwd(x, small["norm_mix_g"], da, dh1, "norm_mix_bwd", comm=_to_chips([ukv_pair]))
    parts["w_ukv"] = got[0]
    hosts = ["w_gate_ffn", "w_up_ffn", "w_down_ffn", "w_out", "w_o_attn", "w_o_sgu", "w_uq", "w_ukv"]
    assert len(chunks) <= 1 + len(hosts)
    for i, k in enumerate(hosts):
        if 1 + i < len(chunks):
            in_parts = update(k, comm=_to_chips([in_pair], rows=[chunks[1 + i]], into=[in_parts]))[0]
        else:
            update(k)
    parts["w_in"] = in_parts
    update("w_in")

    gs = {"norm_mix_g": d_norm_mix, "b_gate": d_b_gate, "q_norm_g": d_q_norm, "kv_norm_g": d_kv_norm,
          "sgu_norm_g": d_sgu_norm, "w_sgu": d_ws, "b_sgu_col": d_b_col, "norm_ffn_g": d_norm_ffn,
          "norm_final_g": d_norm_final}
    return loss_row, grad_x, gs, updates


def _my_place():
    return lax.axis_index("x"), lax.axis_index("y"), lax.axis_index("c")


def _all_gather(shards):
    n = len(shards)

    def body(*refs):
        ins, outs = refs[:n], refs[n:2 * n]
        send_sems, recv_sems, local_sems = refs[2 * n:]
        x, y, c = _my_place()
        me, sibling = (x, y, c), (x, y, 1 - c)
        chips = [(1 - x, y), (x, 1 - y), (1 - x, 1 - y)]

        def slab(w, place):
            return outs[w].at[4 * place[0] + 2 * place[1] + place[2]]

        def copy(w, k, place, to, src=None):
            return pltpu.make_async_remote_copy(
                src_ref=slab(w, place) if src is None else src, dst_ref=slab(w, place),
                send_sem=send_sems.at[w, k], recv_sem=recv_sems.at[w, k], device_id=to, device_id_type=MESH)

        mine = [pltpu.make_async_copy(ins[w], slab(w, me), local_sems.at[w]) for w in range(n)]
        for cp in mine:
            cp.start()
        started = []
        for w in range(n):
            first = [copy(w, 0, me, sibling, src=ins[w])]
            first += [copy(w, 1 + j, me, (*chip, c), src=ins[w]) for j, chip in enumerate(chips)]
            for cp in first:
                cp.start()
            started += first
        for w in range(n):
            for j, chip in enumerate(chips):
                copy(w, 1 + j, (*chip, c), me).wait_recv()
                fwd = copy(w, 4 + j, (*chip, c), sibling)
                fwd.start()
                started.append(fwd)
        for w in range(n):
            copy(w, 0, sibling, me).wait_recv()
            for j, chip in enumerate(chips):
                copy(w, 4 + j, (*chip, 1 - c), me).wait_recv()
        for cp in started:
            cp.wait_send()
        for cp in mine:
            cp.wait()

    any_spec = pl.BlockSpec(memory_space=pl.ANY)
    return pl.pallas_call(
        body, in_specs=[any_spec] * n, out_specs=[any_spec] * n,
        out_shape=[jax.ShapeDtypeStruct((N_DEV,) + s.shape, s.dtype) for s in shards],
        scratch_shapes=[pltpu.SemaphoreType.DMA((n, 7)), pltpu.SemaphoreType.DMA((n, 7)), pltpu.SemaphoreType.DMA((n,))],
        compiler_params=pltpu.CompilerParams(has_side_effects=True), name="all_gather_weights")(*shards)


N_CHIPS = N_DEV // 2


def _gather_first(shards, rows=None, into=None):
    n = len(shards)
    rows = rows or [(0, s.shape[0]) for s in shards]

    def copies(ins, outs, sems):
        x, y, c = _my_place()
        send_sems, recv_sems, local_sems = sems
        me = 4 * x + 2 * y + c
        targets = [(x, y, 1 - c), (1 - x, y, c), (x, 1 - y, c), (1 - x, 1 - y, c)]
        out = []
        for w in range(n):
            r0, nr = rows[w]
            src, dst = ins[w].at[pl.ds(r0, nr)], outs[w].at[me, pl.ds(r0, nr)]
            out.append(pltpu.make_async_copy(src, dst, local_sems.at[w]))
            out += [pltpu.make_async_remote_copy(src_ref=src, dst_ref=dst, send_sem=send_sems.at[w, k],
                                                 recv_sem=recv_sems.at[w, k], device_id=to, device_id_type=MESH)
                    for k, to in enumerate(targets)]
        return out

    def start(ins, outs, sems):
        for cp in copies(ins, outs, sems):
            cp.start()

    def finish(ins, outs, sems):
        for cp in copies(ins, outs, sems):
            cp.wait()

    return _Comm(list(shards) + list(into or []), [jax.ShapeDtypeStruct((N_DEV,) + s.shape, s.dtype) for s in shards],
                 [pltpu.SemaphoreType.DMA((n, 4)), pltpu.SemaphoreType.DMA((n, 4)), pltpu.SemaphoreType.DMA((n,))],
                 start, finish, aliases={n + w: w for w in range(n)} if into else None)


def _gather_second(bufs):
    n = len(bufs)

    def copies(ins, outs, sems):
        x, y, c = _my_place()
        send_sems, recv_sems = sems
        out = []
        for w in range(n):
            for j, (cx, cy) in enumerate([(1 - x, y), (x, 1 - y), (1 - x, 1 - y)]):
                slab = 4 * cx + 2 * cy + c
                out.append(pltpu.make_async_remote_copy(
                    src_ref=ins[w].at[slab], dst_ref=outs[w].at[slab], send_sem=send_sems.at[w, j],
                    recv_sem=recv_sems.at[w, j], device_id=(x, y, 1 - c), device_id_type=MESH))
        return out

    def start(ins, outs, sems):
        for cp in copies(ins, outs, sems):
            cp.start()

    def finish(ins, outs, sems):
        for cp in copies(ins, outs, sems):
            cp.wait()

    return _Comm(bufs, [jax.ShapeDtypeStruct(b.shape, b.dtype) for b in bufs],
                 [pltpu.SemaphoreType.DMA((n, 3)), pltpu.SemaphoreType.DMA((n, 3))], start, finish,
                 aliases={w: w for w in range(n)})


def _to_sibling(grads, first=None):
    n = len(grads)
    first = first or [0] * n

    def copies(ins, outs, sems):
        x, y, c = _my_place()
        send_sems, recv_sems = sems
        return [pltpu.make_async_remote_copy(
            src_ref=ins[w].at[first[w] + 2 * i + (1 - c)], dst_ref=outs[w].at[i], send_sem=send_sems.at[w, i],
            recv_sem=recv_sems.at[w, i], device_id=(x, y, 1 - c), device_id_type=MESH)
            for w in range(n) for i in range(N_CHIPS)]

    def start(ins, outs, sems):
        for cp in copies(ins, outs, sems):
            cp.start()

    def finish(ins, outs, sems):
        for cp in copies(ins, outs, sems):
            cp.wait()

    return _Comm(grads, [jax.ShapeDtypeStruct((N_CHIPS,) + g.shape[1:], g.dtype) for g in grads],
                 [pltpu.SemaphoreType.DMA((n, N_CHIPS)), pltpu.SemaphoreType.DMA((n, N_CHIPS))], start, finish)


def _to_chips(parts, rows=None, into=None):
    n = len(parts)
    rows = rows or [(0, p.shape[1]) for p in parts]

    def copies(ins, outs, sems):
        x, y, c = _my_place()
        send_sems, recv_sems, local_sems = sems
        mine = 2 * x + y
        chips = [(1 - x, y), (x, 1 - y), (1 - x, 1 - y)]
        remote = [pltpu.make_async_remote_copy(
            src_ref=ins[w].at[2 * cx + cy, pl.ds(*rows[w])], dst_ref=outs[w].at[mine, pl.ds(*rows[w])],
            send_sem=send_sems.at[w, j], recv_sem=recv_sems.at[w, j], device_id=(cx, cy, c), device_id_type=MESH)
            for w in range(n) for j, (cx, cy) in enumerate(chips)]
        local = [pltpu.make_async_copy(ins[w].at[mine, pl.ds(*rows[w])], outs[w].at[mine, pl.ds(*rows[w])],
                                       local_sems.at[w]) for w in range(n)]
        return remote + local

    def start(ins, outs, sems):
        for cp in copies(ins, outs, sems):
            cp.start()

    def finish(ins, outs, sems):
        for cp in copies(ins, outs, sems):
            cp.wait()

    return _Comm(list(parts) + list(into or []), [jax.ShapeDtypeStruct(p.shape, p.dtype) for p in parts],
                 [pltpu.SemaphoreType.DMA((n, N_CHIPS - 1)), pltpu.SemaphoreType.DMA((n, N_CHIPS - 1)),
                  pltpu.SemaphoreType.DMA((n,))], start, finish,
                 aliases={n + w: w for w in range(n)} if into else None)


def _pair_sum(g, buf, name, first=0):
    _, r, c = g.shape
    tr = _row_tile(r, c, 2)
    core = (lax.axis_index("c") + first).astype(jnp.int32).reshape(1)

    def body(core_ref, g_ref, b_ref, o_ref):
        o_ref[...] = (g_ref[...].astype(F32) + b_ref[...].astype(F32)).astype(o_ref.dtype)

    blk = (1, tr, c)
    return pl.pallas_call(
        body, grid_spec=pltpu.PrefetchScalarGridSpec(
            num_scalar_prefetch=1, grid=(N_CHIPS, r // tr),
            in_specs=[pl.BlockSpec(blk, lambda i, j, core_ref: (2 * i + core_ref[0], j, 0)),
                      pl.BlockSpec(blk, lambda i, j, core_ref: (i, j, 0))],
            out_specs=pl.BlockSpec(blk, lambda i, j, core_ref: (i, j, 0))),
        out_shape=jax.ShapeDtypeStruct(buf.shape, buf.dtype),
        compiler_params=_params(("parallel", "parallel")), name=name)(core, g, buf)


def _all_reduce_pack(pack):
    r = pack.shape[0]

    def body(x_ref, out_ref, gath_ref, send_sems, recv_sems, local_sem):
        x, y, c = _my_place()
        me, sibling = (x, y, c), (x, y, 1 - c)
        chips = [(1 - x, y), (x, 1 - y), (1 - x, 1 - y)]

        def slab(place):
            return gath_ref.at[4 * place[0] + 2 * place[1] + place[2]]

        def copy(k, place, to, src=None):
            return pltpu.make_async_remote_copy(
                src_ref=slab(place) if src is None else src, dst_ref=slab(place),
                send_sem=send_sems.at[k], recv_sem=recv_sems.at[k], device_id=to, device_id_type=MESH)

        mine = pltpu.make_async_copy(x_ref, slab(me), local_sem)
        mine.start()
        first = [copy(0, me, sibling, src=x_ref)]
        first += [copy(1 + j, me, (*chip, c), src=x_ref) for j, chip in enumerate(chips)]
        for cp in first:
            cp.start()
        passed = [copy(4 + j, (*chip, c), sibling) for j, chip in enumerate(chips)]
        for j, chip in enumerate(chips):
            copy(1 + j, (*chip, c), me).wait_recv()
            passed[j].start()
        copy(0, sibling, me).wait_recv()
        for j, chip in enumerate(chips):
            copy(4 + j, (*chip, 1 - c), me).wait_recv()
        for cp in first + passed:
            cp.wait_send()
        mine.wait()
        acc = gath_ref[0]
        for i in range(1, N_DEV):
            acc = acc + gath_ref[i]
        out_ref[...] = acc

    vmem = pl.BlockSpec(memory_space=pltpu.VMEM)
    return pl.pallas_call(
        body, in_specs=[vmem], out_specs=vmem, out_shape=jax.ShapeDtypeStruct(pack.shape, F32),
        scratch_shapes=[pltpu.VMEM((N_DEV, r, LANES), F32), pltpu.SemaphoreType.DMA((7,)),
                        pltpu.SemaphoreType.DMA((7,)), pltpu.SemaphoreType.DMA],
        compiler_params=pltpu.CompilerParams(vmem_limit_bytes=VMEM_LIMIT), name="all_reduce_small")(pack)


def _adamw_math(w, g, m, v):
    m = ADAM_B1 * m + (1.0 - ADAM_B1) * g
    v = ADAM_B2 * v + (1.0 - ADAM_B2) * (g * g)
    m_hat = m / (1.0 - ADAM_B1 ** ADAM_STEP)
    v_hat = v / (1.0 - ADAM_B2 ** ADAM_STEP)
    delta = -ADAM_LR * (m_hat / (jnp.sqrt(v_hat) + ADAM_EPS) + ADAM_WD * w)
    return delta, m, v


def _adamw_shard(parts, w, m, v, name, comm=None):
    r, c = w.shape
    n_parts = parts.shape[0]
    tr = _pick(r, max(2 * SUBLANES, (256 * 1024) // c), 2 * SUBLANES)

    def body(p_ref, w_ref, m_ref, v_ref, g_ref, d_ref, nm_ref, nv_ref):
        g = p_ref[0].astype(F32)
        for i in range(1, n_parts):
            g = g + p_ref[i].astype(F32)
        g_ref[...] = g
        d_ref[...], nm_ref[...], nv_ref[...] = _adamw_math(w_ref[...], g, m_ref[...], v_ref[...])

    spec = pl.BlockSpec((tr, c), lambda i: (i, 0))
    outs, comm_outs = _call(
        body, grid=(r // tr,), in_specs=[pl.BlockSpec((n_parts, tr, c), lambda i: (0, i, 0)), spec, spec, spec],
        out_specs=[spec] * 4, out_shape=[jax.ShapeDtypeStruct((r, c), F32)] * 4,
        sem=("parallel",), name=name, args=(parts, w, m, v), comm=comm)
    return outs, comm_outs


def _adamw_pack(g, w, m, v):
    r, c = w.shape

    def body(g_ref, w_ref, m_ref, v_ref, d_ref, nm_ref, nv_ref):
        d_ref[...], nm_ref[...], nv_ref[...] = _adamw_math(w_ref[...], g_ref[...], m_ref[...], v_ref[...])

    return pl.pallas_call(
        body, in_specs=[_full((r, c))] * 4, out_specs=[_full((r, c))] * 3, grid=(1,),
        out_shape=[jax.ShapeDtypeStruct((r, c), F32)] * 3,
        compiler_params=_params(("arbitrary",)), name="adamw_small")(g, w, m, v)


def _cols_from_slabs(g):
    return jnp.transpose(g, (1, 0, 2)).reshape(g.shape[1], N_DEV * g.shape[2])


def _slabs_from_cols(w):
    r, c8 = w.shape
    return jnp.transpose(w.reshape(r, N_DEV, c8 // N_DEV), (1, 0, 2))


def _rows_from_slabs(g):
    return g.reshape(N_DEV * g.shape[1], g.shape[2])


def _slabs_from_rows(w):
    return w.reshape(N_DEV, w.shape[0] // N_DEV, w.shape[1])


def _compute_layout(gathered, ql, kvl, heads, sw):
    out = {}
    for k, g in gathered.items():
        if k == "w_in":
            lat = ql + kvl + QK_ROPE
            w_in_full = _cols_from_slabs(g)
            out["w_lat"] = jnp.pad(w_in_full[:, :lat], ((0, 0), (0, LANES - QK_ROPE)))
            out["w_uv"] = w_in_full[:, lat:lat + 2 * sw]
            out["w_g"] = w_in_full[:, lat + 2 * sw:]
        elif k == "w_uq":
            per_head = _cols_from_slabs(g).reshape(ql, heads, QK_NOPE + QK_ROPE)
            pad = HEAD_PAD - QK_NOPE - QK_ROPE
            out["w_uq"] = jnp.pad(per_head, ((0, 0), (0, 0), (0, pad))).reshape(ql, heads * HEAD_PAD)
        elif k in ("w_o_attn", "w_out", "w_down_ffn"):
            out[k.removesuffix("_ffn")] = _rows_from_slabs(g)
        else:
            out[k.removesuffix("_ffn")] = _cols_from_slabs(g)
    return out


_SMALL =["norm_mix_g", "b_gate", "q_norm_g", "kv_norm_g", "sgu_norm_g", "w_sgu", "b_sgu", "norm_ffn_g", "norm_final_g"]
_BIG = ["w_in", "w_uq", "w_ukv", "w_o_attn", "w_o_sgu", "w_out", "w_gate_ffn", "w_up_ffn", "w_down_ffn"]
_ORDER = ["norm_mix_g", "w_in", "b_gate", "q_norm_g", "w_uq", "kv_norm_g", "w_ukv", "w_o_attn", "sgu_norm_g", "w_sgu",
          "b_sgu", "w_o_sgu", "w_out", "norm_ffn_g", "w_gate_ffn", "w_up_ffn", "w_down_ffn", "norm_final_g"]


def _pack_rows(parts):
    rows, sizes = [], []
    for p in parts:
        flat = p.reshape(-1)
        n = flat.shape[0]
        padded = -(-n // (SUBLANES * LANES)) * (SUBLANES * LANES)
        rows.append(jnp.pad(flat, (0, padded - n)).reshape(padded // LANES, LANES))
        sizes.append((n, padded // LANES))
    return jnp.concatenate(rows, axis=0), sizes


def _unpack_rows(pack, sizes, shapes):
    out, r0 = [], 0
    for (n, nr), shp in zip(sizes, shapes):
        out.append(pack[r0:r0 + nr].reshape(-1)[:n].reshape(shp))
        r0 += nr
    return out


def kernel(x, positions, norm_mix_g, w_in, b_gate, q_norm_g, w_uq, kv_norm_g, w_ukv, w_o_attn, sgu_norm_g, w_sgu, b_sgu, w_o_sgu, w_out, norm_ffn_g, w_gate_ffn, w_up_ffn, w_down_ffn, norm_final_g, loss_target, m_norm_mix_g, m_w_in, m_b_gate, m_q_norm_g, m_w_uq, m_kv_norm_g, m_w_ukv, m_w_o_attn, m_sgu_norm_g, m_w_sgu, m_b_sgu, m_w_o_sgu, m_w_out, m_norm_ffn_g, m_w_gate_ffn, m_w_up_ffn, m_w_down_ffn, m_norm_final_g, v_norm_mix_g, v_w_in, v_b_gate, v_q_norm_g, v_w_uq, v_kv_norm_g, v_w_ukv, v_w_o_attn, v_sgu_norm_g, v_w_sgu, v_b_sgu, v_w_o_sgu, v_w_out, v_norm_ffn_g, v_w_gate_ffn, v_w_up_ffn, v_w_down_ffn, v_norm_final_g):
    wts = dict(norm_mix_g=norm_mix_g, w_in=w_in, b_gate=b_gate, q_norm_g=q_norm_g, w_uq=w_uq, kv_norm_g=kv_norm_g,
               w_ukv=w_ukv, w_o_attn=w_o_attn, sgu_norm_g=sgu_norm_g, w_sgu=w_sgu, b_sgu=b_sgu, w_o_sgu=w_o_sgu,
               w_out=w_out, norm_ffn_g=norm_ffn_g, w_gate_ffn=w_gate_ffn, w_up_ffn=w_up_ffn, w_down_ffn=w_down_ffn,
               norm_final_g=norm_final_g)
    mom = dict(norm_mix_g=m_norm_mix_g, w_in=m_w_in, b_gate=m_b_gate, q_norm_g=m_q_norm_g, w_uq=m_w_uq,
               kv_norm_g=m_kv_norm_g, w_ukv=m_w_ukv, w_o_attn=m_w_o_attn, sgu_norm_g=m_sgu_norm_g, w_sgu=m_w_sgu,
               b_sgu=m_b_sgu, w_o_sgu=m_w_o_sgu, w_out=m_w_out, norm_ffn_g=m_norm_ffn_g, w_gate_ffn=m_w_gate_ffn,
               w_up_ffn=m_w_up_ffn, w_down_ffn=m_w_down_ffn, norm_final_g=m_norm_final_g)
    var = dict(norm_mix_g=v_norm_mix_g, w_in=v_w_in, b_gate=v_b_gate, q_norm_g=v_q_norm_g, w_uq=v_w_uq,
               kv_norm_g=v_kv_norm_g, w_ukv=v_w_ukv, w_o_attn=v_w_o_attn, sgu_norm_g=v_sgu_norm_g, w_sgu=v_w_sgu,
               b_sgu=v_b_sgu, w_o_sgu=v_w_o_sgu, w_out=v_w_out, norm_ffn_g=v_norm_ffn_g, w_gate_ffn=v_w_gate_ffn,
               w_up_ffn=v_w_up_ffn, w_down_ffn=v_w_down_ffn, norm_final_g=v_norm_final_g)

    t, d = x.shape[1], x.shape[2]
    ql, kvl = q_norm_g.shape[1], kv_norm_g.shape[1]
    heads = (w_uq.shape[2] * N_DEV) // (QK_NOPE + QK_ROPE)
    sw = sgu_norm_g.shape[1]

    shards = {k: wts[k][0].astype(BF16) for k in _BIG}
    small = {
        "norm_mix_g": norm_mix_g, "b_gate": b_gate, "q_norm_g": q_norm_g, "kv_norm_g": kv_norm_g,
        "sgu_norm_g": sgu_norm_g, "w_sgu": w_sgu[0], "b_sgu_col": b_sgu[0][:, :, None], "norm_ffn_g": norm_ffn_g,
        "norm_final_g": norm_final_g[None, :],
    }

    opt = {k: (wts[k][0], mom[k][0], var[k][0]) for k in _BIG}
    loss_row, grad_x, gs, updates = _local_step(x[0], positions.reshape(t, 1), loss_target[0], small, shards, opt)
    grads, deltas, new_m, new_v = {}, {}, {}, {}
    for k in _BIG:
        grads[k], deltas[k], new_m[k], new_v[k] = (a.reshape(wts[k].shape) for a in updates[k])

    small_grads = [gs["norm_mix_g"], gs["b_gate"], gs["q_norm_g"], gs["kv_norm_g"], gs["sgu_norm_g"], gs["w_sgu"],
                   gs["b_sgu_col"], gs["norm_ffn_g"], gs["norm_final_g"]]
    pack, sizes = _pack_rows([loss_row] + small_grads)
    total = _all_reduce_pack(pack)
    shapes = [(1, LANES)] + [wts[k].shape for k in _SMALL]
    unpacked = _unpack_rows(total, sizes, shapes)
    loss = unpacked[0][0, 0]
    for k, g in zip(_SMALL, unpacked[1:]):
        grads[k] = g
    g_pack = total[sizes[0][1]:]
    w_pack, _ = _pack_rows([wts[k] for k in _SMALL])
    m_pack, _ = _pack_rows([mom[k] for k in _SMALL])
    v_pack, _ = _pack_rows([var[k] for k in _SMALL])
    d_pack, nm_pack, nv_pack = _adamw_pack(g_pack, w_pack, m_pack, v_pack)
    small_shapes = [wts[k].shape for k in _SMALL]
    for store, pk in ((deltas, d_pack), (new_m, nm_pack), (new_v, nv_pack)):
        for k, a in zip(_SMALL, _unpack_rows(pk, sizes[1:], small_shapes)):
            store[k] = a

    return (loss, grad_x[None], *[grads[k] for k in _ORDER], *[deltas[k] for k in _ORDER],
            *[new_m[k] for k in _ORDER], *[new_v[k] for k in _ORDER])
```

```python
import functools
import math

import jax
import jax.numpy as jnp
from jax import lax
from jax.experimental import pallas as pl
from jax.experimental.pallas import tpu as pltpu

F32 = jnp.float32
BF16 = jnp.bfloat16

N_DEV = 8
N_HEADS = 16
QK_NOPE = 128
QK_ROPE = 64
V_HEAD = 128
HEAD_PAD = 256
ROPE_THETA = 10000.0
CHUNK = 128
SGU_GROUP = 128
RMS_EPS = 1e-6
LANES = 128
SUBLANES = 8

ADAM_LR = 0.001
ADAM_B1 = 0.9
ADAM_B2 = 0.999
ADAM_EPS = 1e-08
ADAM_WD = 0.01
ADAM_STEP = 10

VMEM_LIMIT = 48 * 1024 * 1024
MM_TILE = (1024, 512, 2048)
ATTN_TILE = 512
ROW_KERNEL_BYTES = 24 * 1024 * 1024
SHARD_TILE_ELEMS = 256 * 1024
SLABS_PER_STEP = 4
TAIL_CHUNKS = 4
NEG_BIG = -1e30
MESH = pl.DeviceIdType.MESH


def _pick(n, target, mult=LANES):
    best = None
    d = mult
    while d <= min(n, target):
        if n % d == 0:
            best = d
        d += mult
    return best or n


def _row_tile(t, width, n_blocks, mult=2 * SUBLANES):
    return _pick(t, max(mult, ROW_KERNEL_BYTES // (3 * n_blocks * width * 4)), mult)


def _shard_tile(r, c):
    tr = _pick(r, 256, 2 * SUBLANES)
    return tr, _pick(c, max(LANES, SHARD_TILE_ELEMS // tr))


def _params(sem):
    return pltpu.CompilerParams(dimension_semantics=sem, vmem_limit_bytes=VMEM_LIMIT)


def _full(shape):
    nd = len(shape)
    return pl.BlockSpec(shape, lambda *_: (0,) * nd)


def _rows(tr, w, cb=0):
    return pl.BlockSpec((tr, w), lambda i: (i, cb))


class _Comm:
    def __init__(self, ins, out_shapes, sems, start, finish, aliases=None):
        self.ins, self.out_shapes, self.sems, self.start, self.finish = list(ins), list(out_shapes), list(sems), start, finish
        self.aliases = dict(aliases or {})


def _call(body, *, grid, in_specs, out_specs, out_shape, scratch_shapes=(), sem, name, args, comm=None):
    if comm is None:
        outs = pl.pallas_call(body, grid=grid, in_specs=list(in_specs), out_specs=list(out_specs),
                              out_shape=list(out_shape), scratch_shapes=list(scratch_shapes),
                              compiler_params=_params(sem), name=name)(*args)
        return list(outs), []
    n_in, n_out, n_sc = len(in_specs), len(out_shape), len(scratch_shapes)
    nci, nco = len(comm.ins), len(comm.out_shapes)

    def hosted(*refs):
        ins, refs = refs[:n_in], refs[n_in:]
        cins, refs = refs[:nci], refs[nci:]
        outs, refs = refs[:n_out], refs[n_out:]
        couts, refs = refs[:nco], refs[nco:]
        scratch, csems = refs[:n_sc], refs[n_sc:]
        ids = [pl.program_id(i) for i in range(len(grid))]
        first = functools.reduce(jnp.logical_and, [i == 0 for i in ids])
        last = functools.reduce(jnp.logical_and, [i == g - 1 for i, g in zip(ids, grid)])

        @pl.when(first)
        def _():
            comm.start(cins, couts, csems)

        body(*ins, *outs, *scratch)

        @pl.when(last)
        def _():
            comm.finish(cins, couts, csems)

    any_spec = pl.BlockSpec(memory_space=pl.ANY)
    res = pl.pallas_call(
        hosted, grid=grid, in_specs=list(in_specs) + [any_spec] * nci, out_specs=list(out_specs) + [any_spec] * nco,
        out_shape=list(out_shape) + comm.out_shapes, scratch_shapes=list(scratch_shapes) + comm.sems,
        input_output_aliases={n_in + i: n_out + o for i, o in comm.aliases.items()},
        compiler_params=pltpu.CompilerParams(dimension_semantics=("arbitrary",) * len(grid),
                                             vmem_limit_bytes=VMEM_LIMIT, has_side_effects=True),
        name=name)(*args, *comm.ins)
    return list(res[:n_out]), list(res[n_out:])


def _mm(a, b, *, ta=False, tb=False, add=None, out_dtype=F32, tm=None, tn=None, tk=None, name, comm=None,
        slab=None, a_slab0=0):
    sq = None
    if slab is None:
        m, k = (a.shape[1], a.shape[0]) if ta else a.shape
        n = b.shape[0] if tb else b.shape[1]
        assert k == (b.shape[1] if tb else b.shape[0]), (a.shape, b.shape, ta, tb)
        tm, tn, tk = _pick(m, tm or MM_TILE[0]), _pick(n, tn or MM_TILE[1]), _pick(k, tk or MM_TILE[2])
        grid = (m // tm, n // tn, k // tk)
        a_spec = pl.BlockSpec((tk, tm), lambda i, j, kk: (kk, i)) if ta else pl.BlockSpec((tm, tk), lambda i, j, kk: (i, kk))
        b_spec = pl.BlockSpec((tn, tk), lambda i, j, kk: (j, kk)) if tb else pl.BlockSpec((tk, tn), lambda i, j, kk: (kk, j))
        o_spec, o_shape = pl.BlockSpec((tm, tn), lambda i, j, kk: (i, j)), (m, n)
    elif slab == "n":
        m, k = (a.shape[1], a.shape[0]) if ta else a.shape
        s, c = b.shape[0], (b.shape[1] if tb else b.shape[2])
        assert k == (b.shape[2] if tb else b.shape[1]), (a.shape, b.shape, ta, tb)
        tm, tn, tk = _pick(m, tm or MM_TILE[0]), c, _pick(k, tk or MM_TILE[2])
        grid = (m // tm, s, k // tk)
        a_spec = pl.BlockSpec((tk, tm), lambda i, j, kk: (kk, i)) if ta else pl.BlockSpec((tm, tk), lambda i, j, kk: (i, kk))
        b_spec = (pl.BlockSpec((sq, c, tk), lambda i, j, kk: (j, 0, kk)) if tb
                  else pl.BlockSpec((sq, tk, c), lambda i, j, kk: (j, kk, 0)))
        o_spec, o_shape = pl.BlockSpec((sq, tm, c), lambda i, j, kk: (j, i, 0)), (s, m, c)
    elif slab == "m":
        assert ta and not tb
        s, k, c = a.shape
        n = b.shape[1]
        assert k == b.shape[0], (a.shape, b.shape)
        tm, tn, tk = c, _pick(n, tn or MM_TILE[1]), _pick(k, tk or MM_TILE[2])
        grid = (s, n // tn, k // tk)
        a_spec = pl.BlockSpec((sq, tk, c), lambda i, j, kk: (i, kk, 0))
        b_spec = pl.BlockSpec((tk, tn), lambda i, j, kk: (kk, j))
        o_spec, o_shape = pl.BlockSpec((sq, c, tn), lambda i, j, kk: (i, 0, j)), (s, c, n)
    else:
        assert slab == "k" and not ta
        s, c = b.shape[0], (b.shape[2] if tb else b.shape[1])
        m, n = a.shape[1], (b.shape[1] if tb else b.shape[2])
        assert a.shape[2] == c and a.shape[0] >= a_slab0 + s, (a.shape, b.shape, a_slab0)
        tm, tn, tk = _pick(m, tm or MM_TILE[0]), _pick(n, tn or MM_TILE[1]), c
        per_step = SLABS_PER_STEP if (s % SLABS_PER_STEP == 0 and a_slab0 % SLABS_PER_STEP == 0) else 1
        first = a_slab0 // per_step
        grid = (m // tm, n // tn, s // per_step)
        a_spec = pl.BlockSpec((per_step, tm, c), lambda i, j, kk: (kk + first, i, 0))
        b_spec = (pl.BlockSpec((per_step, tn, c), lambda i, j, kk: (kk, j, 0)) if tb
                  else pl.BlockSpec((per_step, c, tn), lambda i, j, kk: (kk, 0, j)))
        o_spec, o_shape = pl.BlockSpec((tm, tn), lambda i, j, kk: (i, j)), (m, n)
    nk = grid[2]
    dims = (((0 if ta else 1,), (1 if tb else 0,)), ((), ()))

    def product(a_ref, b_ref):
        if slab != "k":
            return lax.dot_general(a_ref[...].astype(BF16), b_ref[...].astype(BF16), dims, preferred_element_type=F32)
        r = None
        for u in range(a_ref.shape[0]):
            p = lax.dot_general(a_ref[u].astype(BF16), b_ref[u].astype(BF16), dims, preferred_element_type=F32)
            r = p if r is None else r + p
        return r

    def body(*refs):
        if add is None:
            a_ref, b_ref, o_ref, acc_ref = refs
            add_ref = None
        else:
            a_ref, b_ref, add_ref, o_ref, acc_ref = refs
        kk = pl.program_id(2)

        @pl.when(kk == 0)
        def _():
            acc_ref[...] = jnp.zeros_like(acc_ref)

        acc_ref[...] += product(a_ref, b_ref)

        @pl.when(kk == nk - 1)
        def _():
            r = acc_ref[...]
            if add_ref is not None:
                r = r + add_ref[...].astype(F32)
            o_ref[...] = r.astype(o_ref.dtype)

    in_specs = [a_spec, b_spec] + ([o_spec] if add is not None else [])
    args = (a, b) + ((add,) if add is not None else ())
    outs, comm_outs = _call(
        body, grid=grid, in_specs=in_specs, out_specs=[o_spec],
        out_shape=[jax.ShapeDtypeStruct(o_shape, out_dtype)], scratch_shapes=[pltpu.VMEM((tm, tn), F32)],
        sem=("parallel", "parallel", "arbitrary"), name=name, args=args, comm=comm)
    return outs[0] if comm is None else (outs[0], comm_outs)


def _rms_scale(x):
    return lax.rsqrt(jnp.mean(x * x, axis=-1, keepdims=True) + RMS_EPS)


def _rms_bwd(xhat, r, g, dy):
    t = dy * g
    dx = r * (t - xhat * jnp.mean(t * xhat, axis=-1, keepdims=True))
    return dx, dy * xhat


_GELU_C = math.sqrt(2.0 / math.pi)


def _gelu(x):
    return x * (0.5 * (1.0 + jnp.tanh(_GELU_C * (x + 0.044715 * (x * x * x)))))


def _gelu_and_grad(x):
    t = jnp.tanh(_GELU_C * (x + 0.044715 * (x * x * x)))
    cdf = 0.5 * (1.0 + t)
    return x * cdf, cdf + x * (0.5 * (1.0 - t * t) * (_GELU_C * (1.0 + 3.0 * 0.044715 * (x * x))))


def _sigmoid(x):
    return 1.0 / (1.0 + jnp.exp(-x))


def _swap_halves(x):
    lane = lax.broadcasted_iota(jnp.int32, x.shape, 1)
    first = (lane % QK_ROPE) < (QK_ROPE // 2)
    return jnp.where(first, pltpu.roll(x, LANES - QK_ROPE // 2, 1), pltpu.roll(x, QK_ROPE // 2, 1))


def _rope(x, cos, sin_signed):
    return x * cos + _swap_halves(x) * sin_signed


def _rope_bwd(d, cos, sin_signed):
    return d * cos + _swap_halves(d * sin_signed)


def _rope_tables(pos_col, inv_freq_row, sign_row):
    t = pos_col.shape[0]
    tr = _pick(t, 512, SUBLANES)

    def body(p_ref, f_ref, s_ref, cos_ref, sin_ref):
        ang = p_ref[...].astype(F32) * f_ref[...]
        cos_ref[...] = jnp.cos(ang)
        sin_ref[...] = jnp.sin(ang) * s_ref[...]

    return pl.pallas_call(
        body, grid=(t // tr,), in_specs=[_rows(tr, 1), _full((1, LANES)), _full((1, LANES))],
        out_specs=[_rows(tr, LANES), _rows(tr, LANES)],
        out_shape=[jax.ShapeDtypeStruct((t, LANES), F32)] * 2,
        compiler_params=_params(("parallel",)), name="rope_tables")(pos_col, inv_freq_row, sign_row)


def _norm_fwd(x, g, name):
    t, d = x.shape
    tr = _row_tile(t, d, 2)

    def body(x_ref, g_ref, y_ref):
        xv = x_ref[...]
        y_ref[...] = (xv * _rms_scale(xv) * g_ref[...]).astype(BF16)

    return pl.pallas_call(
        body, grid=(t // tr,), in_specs=[_rows(tr, d), _full((1, d))], out_specs=_rows(tr, d),
        out_shape=jax.ShapeDtypeStruct((t, d), BF16), compiler_params=_params(("parallel",)), name=name)(x, g)


def _lat_fwd(z_lat, qg, kvg, cos, sin, ql, kvl):
    t = z_lat.shape[0]
    tr = _row_tile(t, z_lat.shape[1], 2)

    def body(z_ref, qg_ref, kvg_ref, cos_ref, sin_ref, qn_ref, kvn_ref, kpe_ref):
        q = z_ref[:, 0:ql]
        qn_ref[...] = (q * _rms_scale(q) * qg_ref[...]).astype(BF16)
        kv = z_ref[:, ql:ql + kvl]
        kvn_ref[...] = (kv * _rms_scale(kv) * kvg_ref[...]).astype(BF16)
        kpe_ref[...] = _rope(z_ref[:, ql + kvl:ql + kvl + LANES], cos_ref[...], sin_ref[...]).astype(BF16)

    w = z_lat.shape[1]
    return pl.pallas_call(
        body, grid=(t // tr,),
        in_specs=[_rows(tr, w), _full((1, ql)), _full((1, kvl)), _rows(tr, LANES), _rows(tr, LANES)],
        out_specs=[_rows(tr, ql), _rows(tr, kvl), _rows(tr, LANES)],
        out_shape=[jax.ShapeDtypeStruct((t, ql), BF16), jax.ShapeDtypeStruct((t, kvl), BF16),
                   jax.ShapeDtypeStruct((t, LANES), BF16)],
        compiler_params=_params(("parallel",)), name="lat_fwd")(z_lat, qg, kvg, cos, sin)


def _q_rope(q_p, cos, sin, bwd, name):
    t, w = q_p.shape
    tr = _row_tile(t, w, 2)
    fn = _rope_bwd if bwd else _rope

    def body(q_ref, cos_ref, sin_ref, o_ref):
        c, s = cos_ref[...], sin_ref[...]
        for h in range(w // HEAD_PAD):
            o_ref[:, h * HEAD_PAD:h * HEAD_PAD + QK_NOPE] = q_ref[:, h * HEAD_PAD:h * HEAD_PAD + QK_NOPE].astype(BF16)
            lo = h * HEAD_PAD + QK_NOPE
            o_ref[:, lo:lo + LANES] = fn(q_ref[:, lo:lo + LANES].astype(F32), c, s).astype(BF16)

    return pl.pallas_call(
        body, grid=(t // tr,), in_specs=[_rows(tr, w), _rows(tr, LANES), _rows(tr, LANES)], out_specs=_rows(tr, w),
        out_shape=jax.ShapeDtypeStruct((t, w), BF16), compiler_params=_params(("parallel",)), name=name)(q_p, cos, sin)


def _tril_mask():
    r = lax.broadcasted_iota(jnp.int32, (CHUNK, CHUNK), 0)
    c = lax.broadcasted_iota(jnp.int32, (CHUNK, CHUNK), 1)
    return r >= c


def _sgu_fwd(z_uv, gs, ws, b_col):
    t = z_uv.shape[0]
    sw = z_uv.shape[1] // 2
    groups = sw // SGU_GROUP
    tr = _pick(t, 256, CHUNK)

    def body(u_ref, v_ref, gs_ref, ws_ref, b_ref, o_ref):
        v = _gelu(v_ref[...])
        vn = (v * _rms_scale(v) * gs_ref[...]).astype(BF16)
        tri = _tril_mask()
        for g in range(groups):
            wg = jnp.where(tri, ws_ref[g], 0.0).astype(BF16)
            cols = slice(g * SGU_GROUP, (g + 1) * SGU_GROUP)
            for c in range(tr // CHUNK):
                rows = slice(c * CHUNK, (c + 1) * CHUNK)
                mixed = jnp.dot(wg, vn[rows, cols], preferred_element_type=F32) + b_ref[g]
                o_ref[rows, cols] = (_gelu(u_ref[rows, cols]) * mixed).astype(BF16)

    return pl.pallas_call(
        body, grid=(t // tr,),
        in_specs=[_rows(tr, sw, 0), _rows(tr, sw, 1), _full((1, sw)), _full(ws.shape), _full(b_col.shape)],
        out_specs=_rows(tr, sw), out_shape=jax.ShapeDtypeStruct((t, sw), BF16),
        compiler_params=_params(("parallel",)), name="sgu_fwd")(z_uv, z_uv, gs, ws, b_col)


def _merge_fwd(y_attn, y_sgu, z_g, b_gate):
    t, d = y_attn.shape
    tr = _row_tile(t, d, 5)

    def body(ya_ref, ys_ref, g0_ref, g1_ref, b0_ref, b1_ref, o_ref):
        g0 = _sigmoid(g0_ref[...] + b0_ref[...])
        g1 = _sigmoid(g1_ref[...] + b1_ref[...])
        o_ref[...] = (g0 * ya_ref[...] + g1 * ys_ref[...]).astype(BF16)

    bspec0 = pl.BlockSpec((1, d), lambda i: (0, 0))
    bspec1 = pl.BlockSpec((1, d), lambda i: (0, 1))
    return pl.pallas_call(
        body, grid=(t // tr,),
        in_specs=[_rows(tr, d), _rows(tr, d), _rows(tr, d, 0), _rows(tr, d, 1), bspec0, bspec1],
        out_specs=_rows(tr, d), out_shape=jax.ShapeDtypeStruct((t, d), BF16),
        compiler_params=_params(("parallel",)), name="merge_fwd")(y_attn, y_sgu, z_g, z_g, b_gate, b_gate)


def _swiglu_fwd(gate, up, comm=None):
    t, f = gate.shape
    tr = _row_tile(t, f, 3)

    def body(g_ref, u_ref, o_ref):
        g = g_ref[...]
        o_ref[...] = (g * _sigmoid(g) * u_ref[...]).astype(BF16)

    outs, comm_outs = _call(
        body, grid=(t // tr,), in_specs=[_rows(tr, f), _rows(tr, f)], out_specs=[_rows(tr, f)],
        out_shape=[jax.ShapeDtypeStruct((t, f), BF16)], sem=("parallel",), name="swiglu_fwd", args=(gate, up), comm=comm)
    return outs[0], comm_outs


def _loss_head(h2, g, target):
    t, d = h2.shape
    tr = _row_tile(t, d, 3)

    def body(h_ref, g_ref, t_ref, loss_ref, dh_ref, dg_ref):
        @pl.when(pl.program_id(0) == 0)
        def _():
            loss_ref[...] = jnp.zeros_like(loss_ref)
            dg_ref[...] = jnp.zeros_like(dg_ref)

        h = h_ref[...]
        r = _rms_scale(h)
        hhat = h * r
        gv = g_ref[...]
        err = hhat * gv - t_ref[...]
        loss_ref[...] += jnp.full(loss_ref.shape, 0.5 * jnp.sum(jnp.mean(err * err, axis=-1)), F32)
        dx, dg_rows = _rms_bwd(hhat, r, gv, err * (1.0 / d))
        dh_ref[...] = dx
        dg_ref[...] += jnp.sum(dg_rows, axis=0, keepdims=True)

    return pl.pallas_call(
        body, grid=(t // tr,), in_specs=[_rows(tr, d), _full((1, d)), _rows(tr, d)],
        out_specs=[_full((1, LANES)), _rows(tr, d), _full((1, d))],
        out_shape=[jax.ShapeDtypeStruct((1, LANES), F32), jax.ShapeDtypeStruct((t, d), F32),
                   jax.ShapeDtypeStruct((1, d), F32)],
        compiler_params=_params(("arbitrary",)), name="loss_head")(h2, g, target)


def _swiglu_bwd(gate, up, dact):
    t, f = gate.shape
    tr = _row_tile(t, f, 4)

    def body(g_ref, u_ref, d_ref, dgu_ref):
        g = g_ref[...]
        s = _sigmoid(g)
        d = d_ref[...]
        dgu_ref[0] = (d * u_ref[...] * (s * (1.0 + g * (1.0 - s)))).astype(BF16)
        dgu_ref[1] = (d * (g * s)).astype(BF16)

    return pl.pallas_call(
        body, grid=(t // tr,), in_specs=[_rows(tr, f)] * 3, out_specs=pl.BlockSpec((2, tr, f), lambda i: (0, i, 0)),
        out_shape=jax.ShapeDtypeStruct((2, t, f), BF16),
        compiler_params=_params(("parallel",)), name="swiglu_bwd")(gate, up, dact)


def _norm_bwd(x, g, dy, resid, name, comm=None):
    t, d = x.shape
    tr = _row_tile(t, d, 4)

    def body(x_ref, g_ref, dy_ref, r_ref, dx_ref, dg_ref):
        @pl.when(pl.program_id(0) == 0)
        def _():
            dg_ref[...] = jnp.zeros_like(dg_ref)

        xv = x_ref[...]
        r = _rms_scale(xv)
        dx, dg_rows = _rms_bwd(xv * r, r, g_ref[...], dy_ref[...])
        dx_ref[...] = r_ref[...] + dx
        dg_ref[...] += jnp.sum(dg_rows, axis=0, keepdims=True)

    outs, comm_outs = _call(
        body, grid=(t // tr,), in_specs=[_rows(tr, d), _full((1, d)), _rows(tr, d), _rows(tr, d)],
        out_specs=[_rows(tr, d), _full((1, d))],
        out_shape=[jax.ShapeDtypeStruct((t, d), F32), jax.ShapeDtypeStruct((1, d), F32)],
        sem=("arbitrary",), name=name, args=(x, g, dy, resid), comm=comm)
    return (outs[0], outs[1]) if comm is None else (outs[0], outs[1], comm_outs)


def _merge_bwd(dmerged, y_attn, y_sgu, z_g, b_gate):
    t, d = y_attn.shape
    tr = _row_tile(t, d, 7)

    def body(dm_ref, ya_ref, ys_ref, g0_ref, g1_ref, b0_ref, b1_ref, dya_ref, dys_ref, dz_ref, db_ref):
        @pl.when(pl.program_id(0) == 0)
        def _():
            db_ref[...] = jnp.zeros_like(db_ref)

        dm = dm_ref[...]
        g0 = _sigmoid(g0_ref[...] + b0_ref[...])
        g1 = _sigmoid(g1_ref[...] + b1_ref[...])
        dya_ref[...] = (dm * g0).astype(BF16)
        dys_ref[...] = (dm * g1).astype(BF16)
        dl0 = dm * ya_ref[...] * (g0 * (1.0 - g0))
        dl1 = dm * ys_ref[...] * (g1 * (1.0 - g1))
        dz_ref[:, 0:d] = dl0.astype(BF16)
        dz_ref[:, d:2 * d] = dl1.astype(BF16)
        db_ref[:, 0:d] += jnp.sum(dl0, axis=0, keepdims=True)
        db_ref[:, d:2 * d] += jnp.sum(dl1, axis=0, keepdims=True)

    bspec0 = pl.BlockSpec((1, d), lambda i: (0, 0))
    bspec1 = pl.BlockSpec((1, d), lambda i: (0, 1))
    return pl.pallas_call(
        body, grid=(t // tr,),
        in_specs=[_rows(tr, d), _rows(tr, d), _rows(tr, d), _rows(tr, d, 0), _rows(tr, d, 1), bspec0, bspec1],
        out_specs=[_rows(tr, d), _rows(tr, d), _rows(tr, 2 * d), _full((1, 2 * d))],
        out_shape=[jax.ShapeDtypeStruct((t, d), BF16), jax.ShapeDtypeStruct((t, d), BF16),
                   jax.ShapeDtypeStruct((t, 2 * d), BF16), jax.ShapeDtypeStruct((1, 2 * d), F32)],
        compiler_params=_params(("arbitrary",)), name="merge_bwd")(dmerged, y_attn, y_sgu, z_g, z_g, b_gate, b_gate)


def _sgu_bwd(z_uv, ds_out, gs, ws, b_col):
    t = z_uv.shape[0]
    sw = z_uv.shape[1] // 2
    groups = sw // SGU_GROUP
    tr = _pick(t, 256, CHUNK)

    def body(u_ref, v_ref, d_ref, gs_ref, ws_ref, b_ref, dz_ref, dws_ref, db_ref, dgs_ref, dvn_ref):
        @pl.when(pl.program_id(0) == 0)
        def _():
            dws_ref[...] = jnp.zeros_like(dws_ref)
            db_ref[...] = jnp.zeros_like(db_ref)
            dgs_ref[...] = jnp.zeros_like(dgs_ref)

        v, dgelu_v = _gelu_and_grad(v_ref[...])
        r = _rms_scale(v)
        vhat = v * r
        gsv = gs_ref[...]
        vn = (vhat * gsv).astype(BF16)
        tri = _tril_mask()
        for g in range(groups):
            wg = jnp.where(tri, ws_ref[g], 0.0).astype(BF16)
            cols = slice(g * SGU_GROUP, (g + 1) * SGU_GROUP)
            for c in range(tr // CHUNK):
                rows = slice(c * CHUNK, (c + 1) * CHUNK)
                vn_cg = vn[rows, cols]
                mixed = jnp.dot(wg, vn_cg, preferred_element_type=F32) + b_ref[g]
                u, dgelu_u = _gelu_and_grad(u_ref[rows, cols])
                dso = d_ref[rows, cols]
                dz_ref[rows, cols] = (dso * mixed * dgelu_u).astype(BF16)
                dmixed = dso * u
                db_ref[g] += jnp.sum(dmixed, axis=1, keepdims=True)
                dmixed_b = dmixed.astype(BF16)
                dws_ref[g] += jnp.where(
                    tri, lax.dot_general(dmixed_b, vn_cg, (((1,), (1,)), ((), ())), preferred_element_type=F32), 0.0)
                dvn_ref[rows, cols] = lax.dot_general(wg, dmixed_b, (((0,), (0,)), ((), ())), preferred_element_type=F32)
        dvn = dvn_ref[...]
        dv, dgs_rows = _rms_bwd(vhat, r, gsv, dvn)
        dz_ref[:, sw:2 * sw] = (dv * dgelu_v).astype(BF16)
        dgs_ref[...] += jnp.sum(dgs_rows, axis=0, keepdims=True)

    return pl.pallas_call(
        body, grid=(t // tr,),
        in_specs=[_rows(tr, sw, 0), _rows(tr, sw, 1), _rows(tr, sw), _full((1, sw)), _full(ws.shape), _full(b_col.shape)],
        out_specs=[_rows(tr, 2 * sw), _full(ws.shape), _full(b_col.shape), _full((1, sw))],
        out_shape=[jax.ShapeDtypeStruct((t, 2 * sw), BF16), jax.ShapeDtypeStruct(ws.shape, F32),
                   jax.ShapeDtypeStruct(b_col.shape, F32), jax.ShapeDtypeStruct((1, sw), F32)],
        scratch_shapes=[pltpu.VMEM((tr, sw), F32)],
        compiler_params=_params(("arbitrary",)), name="sgu_bwd")(z_uv, z_uv, ds_out, gs, ws, b_col)


def _lat_bwd(z_lat, qg, kvg, dqn, dkvn, dkpe_heads, cos, sin, ql, kvl):
    t, w = z_lat.shape
    heads = dkpe_heads.shape[0]
    tr = _row_tile(t, w + heads * LANES, 3)

    def body(z_ref, qg_ref, kvg_ref, dq_ref, dkv_ref, dk_ref, cos_ref, sin_ref, dz_ref, dqg_ref, dkvg_ref):
        @pl.when(pl.program_id(0) == 0)
        def _():
            dqg_ref[...] = jnp.zeros_like(dqg_ref)
            dkvg_ref[...] = jnp.zeros_like(dkvg_ref)

        q = z_ref[:, 0:ql]
        r = _rms_scale(q)
        dx, dg_rows = _rms_bwd(q * r, r, qg_ref[...], dq_ref[...])
        dz_ref[:, 0:ql] = dx.astype(BF16)
        dqg_ref[...] += jnp.sum(dg_rows, axis=0, keepdims=True)
        kv = z_ref[:, ql:ql + kvl]
        r = _rms_scale(kv)
        dx, dg_rows = _rms_bwd(kv * r, r, kvg_ref[...], dkv_ref[...])
        dz_ref[:, ql:ql + kvl] = dx.astype(BF16)
        dkvg_ref[...] += jnp.sum(dg_rows, axis=0, keepdims=True)
        dk = dk_ref[0]
        for h in range(1, heads):
            dk = dk + dk_ref[h]
        dz_ref[:, ql + kvl:ql + kvl + LANES] = _rope_bwd(dk, cos_ref[...], sin_ref[...]).astype(BF16)

    return pl.pallas_call(
        body, grid=(t // tr,),
        in_specs=[_rows(tr, w), _full((1, ql)), _full((1, kvl)), _rows(tr, ql), _rows(tr, kvl),
                  pl.BlockSpec((heads, tr, LANES), lambda i: (0, i, 0)), _rows(tr, LANES), _rows(tr, LANES)],
        out_specs=[_rows(tr, w), _full((1, ql)), _full((1, kvl))],
        out_shape=[jax.ShapeDtypeStruct((t, w), BF16), jax.ShapeDtypeStruct((1, ql), F32),
                   jax.ShapeDtypeStruct((1, kvl), F32)],
        compiler_params=_params(("arbitrary",)), name="lat_bwd")(z_lat, qg, kvg, dqn, dkvn, dkpe_heads, cos, sin)


_NT = (((1,), (1,)), ((), ()))


def _attn_scale():
    return (QK_NOPE + QK_ROPE) ** -0.5


def _attn_fwd(q_c, kv, kpe, comm=None):
    t = q_c.shape[0]
    heads = q_c.shape[1] // HEAD_PAD
    tq = _pick(t, ATTN_TILE)
    nq = t // tq
    scale = _attn_scale()
    to_log2 = scale * math.log2(math.e)
    tn_dims = (((0,), (0,)), ((), ()))

    def body(q_ref, kn_ref, kpe_ref, v_ref, o_ref, lse_ref, m_sc, l_sc, acc_sc):
        qi, ki = pl.program_id(1), pl.program_id(2)

        @pl.when(ki == 0)
        def _():
            m_sc[...] = jnp.full_like(m_sc, NEG_BIG)
            l_sc[...] = jnp.zeros_like(l_sc)
            acc_sc[...] = jnp.zeros_like(acc_sc)

        def step(diagonal):
            kc = jnp.concatenate([kn_ref[...], kpe_ref[...]], axis=1)
            st = lax.dot_general(kc, q_ref[...], _NT, preferred_element_type=F32)
            if diagonal:
                krow = lax.broadcasted_iota(jnp.int32, st.shape, 0)
                qcol = lax.broadcasted_iota(jnp.int32, st.shape, 1)
                st = jnp.where(qcol >= krow, st, NEG_BIG)
            m_prev = m_sc[...]
            m_new = jnp.maximum(m_prev, jnp.max(st, axis=0, keepdims=True))
            alpha = jnp.exp2((m_prev - m_new) * to_log2)
            pt = jnp.exp2((st - m_new) * to_log2)
            l_sc[...] = alpha * l_sc[...] + jnp.sum(pt, axis=0, keepdims=True)
            acc_sc[...] = alpha * acc_sc[...] + lax.dot_general(v_ref[...], pt.astype(BF16), tn_dims,
                                                                preferred_element_type=F32)
            m_sc[...] = m_new

        @pl.when(ki < qi)
        def _():
            step(False)

        @pl.when(ki == qi)
        def _():
            step(True)
            o_ref[...] = (acc_sc[...] / l_sc[...]).T
            lse_ref[0] = m_sc[...] * scale + jnp.log(l_sc[...])

    kmap = lambda blk: (lambda h, qi, ki: (jnp.minimum(ki, qi), 2 * h + blk))
    outs, comm_outs = _call(
        body, grid=(heads, nq, nq),
        in_specs=[pl.BlockSpec((tq, HEAD_PAD), lambda h, qi, ki: (qi, h)),
                  pl.BlockSpec((tq, QK_NOPE), kmap(0)),
                  pl.BlockSpec((tq, LANES), lambda h, qi, ki: (jnp.minimum(ki, qi), 0)),
                  pl.BlockSpec((tq, V_HEAD), kmap(1))],
        out_specs=[pl.BlockSpec((tq, V_HEAD), lambda h, qi, ki: (qi, h)),
                   pl.BlockSpec((1, 1, tq), lambda h, qi, ki: (h, 0, qi))],
        out_shape=[jax.ShapeDtypeStruct((t, heads * V_HEAD), F32), jax.ShapeDtypeStruct((heads, 1, t), F32)],
        scratch_shapes=[pltpu.VMEM((1, tq), F32), pltpu.VMEM((1, tq), F32), pltpu.VMEM((V_HEAD, tq), F32)],
        sem=("parallel", "parallel", "arbitrary"), name="attn_fwd", args=(q_c, kv, kpe, kv), comm=comm)
    return outs[0], outs[1], comm_outs


def _attn_bwd(q_c, kv, kpe, o, do, lse_row, comm=None):
    t = q_c.shape[0]
    heads = q_c.shape[1] // HEAD_PAD
    tk = _pick(t, ATTN_TILE)
    nk = t // tk
    scale = _attn_scale()
    tn_dims = (((0,), (0,)), ((), ()))

    def body(q_ref, kn_ref, kpe_ref, v_ref, do_ref, lse_ref, o_ref, dq_ref, dkv_ref, dkpe_ref, dk_sc, dv_sc, delta_sc):
        ki, qi = pl.program_id(1), pl.program_id(2)

        @pl.when(jnp.logical_and(ki == 0, qi == 0))
        def _():
            dq_ref[...] = jnp.zeros_like(dq_ref)

        @pl.when(qi == 0)
        def _():
            dk_sc[...] = jnp.zeros_like(dk_sc)
            dv_sc[...] = jnp.zeros_like(dv_sc)

        @pl.when(ki == 0)
        def _():
            delta_sc[qi] = jnp.sum((do_ref[...] * o_ref[...]).T, axis=0, keepdims=True)

        def step(diagonal):
            kc = jnp.concatenate([kn_ref[...], kpe_ref[...]], axis=1)
            q = q_ref[...]
            st = lax.dot_general(kc, q, _NT, preferred_element_type=F32) * scale
            pt = jnp.exp(st - lse_ref[0])
            if diagonal:
                krow = lax.broadcasted_iota(jnp.int32, st.shape, 0)
                qcol = lax.broadcasted_iota(jnp.int32, st.shape, 1)
                pt = jnp.where(qcol >= krow, pt, 0.0)
            do_b = do_ref[...].astype(BF16)
            dv_sc[...] += jnp.dot(pt.astype(BF16), do_b, preferred_element_type=F32)
            dpt = lax.dot_general(v_ref[...], do_b, _NT, preferred_element_type=F32)
            dst = (pt * (dpt - delta_sc[qi]) * scale).astype(BF16)
            dk_sc[...] += jnp.dot(dst, q, preferred_element_type=F32)
            rows = pl.ds(pl.multiple_of(qi * tk, tk), tk)
            dq_ref[rows, :] += lax.dot_general(dst, kc, tn_dims, preferred_element_type=F32)

        @pl.when(qi > ki)
        def _():
            step(False)

        @pl.when(qi == ki)
        def _():
            step(True)

        @pl.when(qi == nk - 1)
        def _():
            dkv_ref[:, 0:QK_NOPE] = dk_sc[:, 0:QK_NOPE].astype(BF16)
            dkv_ref[:, QK_NOPE:QK_NOPE + V_HEAD] = dv_sc[...].astype(BF16)
            dkpe_ref[0] = dk_sc[:, QK_NOPE:QK_NOPE + LANES]

    qclamp = lambda h, ki, qi: (jnp.maximum(qi, ki), h)
    kmap = lambda blk: (lambda h, ki, qi: (ki, 2 * h + blk))
    rmap = lambda h, ki, qi: (h, 0, jnp.maximum(qi, ki))
    outs, comm_outs = _call(
        body, grid=(heads, nk, nk),
        in_specs=[pl.BlockSpec((tk, HEAD_PAD), qclamp), pl.BlockSpec((tk, QK_NOPE), kmap(0)),
                  pl.BlockSpec((tk, LANES), lambda h, ki, qi: (ki, 0)), pl.BlockSpec((tk, V_HEAD), kmap(1)),
                  pl.BlockSpec((tk, V_HEAD), qclamp), pl.BlockSpec((1, 1, tk), rmap),
                  pl.BlockSpec((tk, V_HEAD), lambda h, ki, qi: (jnp.where(ki == 0, qi, 0), h))],
        out_specs=[pl.BlockSpec((t, HEAD_PAD), lambda h, ki, qi: (0, h)),
                   pl.BlockSpec((tk, HEAD_PAD), lambda h, ki, qi: (ki, h)),
                   pl.BlockSpec((1, tk, LANES), lambda h, ki, qi: (h, ki, 0))],
        out_shape=[jax.ShapeDtypeStruct((t, heads * HEAD_PAD), F32),
                   jax.ShapeDtypeStruct((t, heads * HEAD_PAD), BF16), jax.ShapeDtypeStruct((heads, t, LANES), F32)],
        scratch_shapes=[pltpu.VMEM((tk, HEAD_PAD), F32), pltpu.VMEM((tk, V_HEAD), F32), pltpu.VMEM((nk, 1, tk), F32)],
        sem=("parallel", "arbitrary", "arbitrary"), name="attn_bwd",
        args=(q_c, kv, kpe, kv, do, lse_row, o), comm=comm)
    return outs[0], outs[1], outs[2], comm_outs


def _local_step(x, pos_col, target, small, shards, opt):
    t = x.shape[0]
    ql, kvl = small["q_norm_g"].shape[1], small["kv_norm_g"].shape[1]
    sw = small["sgu_norm_g"].shape[1]
    heads = (shards["w_uq"].shape[1] * N_DEV) // (QK_NOPE + QK_ROPE)
    big = {}
    early = ["w_in", "w_uq", "w_ukv"]
    big.update(_compute_layout(dict(zip(early, _all_gather([shards[k] for k in early]))), ql, kvl, heads, sw))
    half = QK_ROPE // 2
    lane = jnp.arange(LANES)
    inv_freq = ROPE_THETA ** (-jnp.arange(0, QK_ROPE, 2, dtype=F32) / QK_ROPE)
    inv_row = inv_freq[lane % half][None, :]
    sign_row = jnp.where((lane % QK_ROPE) < half, -1.0, 1.0).astype(F32)[None, :]
    cos, sin = _rope_tables(pos_col, inv_row, sign_row)
    ws = small["w_sgu"]
    b_col = small["b_sgu_col"]

    def arrived(names, bufs):
        big.update(_compute_layout(dict(zip(names, bufs)), ql, kvl, heads, sw))

    a = _norm_fwd(x, small["norm_mix_g"], "norm_mix_fwd")
    z_lat = _mm(a, big["w_lat_t"], tb=True, name="z_lat")
    z_uv = _mm(a, big["w_uv_t"], tb=True, name="z_uv")
    mixers = ["w_o_sgu", "w_o_attn"]
    z_g, bufs = _mm(a, big["w_g_t"], tb=True, name="z_g", comm=_gather_first([shards[k] for k in mixers]))
    qn, kvn, kpe = _lat_fwd(z_lat, small["q_norm_g"], small["kv_norm_g"], cos, sin, ql, kvl)
    q_p, bufs = _mm(qn, big["w_uq"], name="q_up", comm=_gather_second(bufs))
    arrived(mixers, bufs)
    kv = _mm(kvn, big["w_ukv"], out_dtype=BF16, name="kv_up")
    q_c = _q_rope(q_p, cos, sin, False, "q_rope")
    wide = ["w_out", "w_gate_ffn", "w_up_ffn"]
    attn, lse, bufs = _attn_fwd(q_c, kv, kpe, comm=_gather_first([shards[k] for k in wide]))
    s_out = _sgu_fwd(z_uv, small["sgu_norm_g"], ws, b_col)
    y_sgu = _mm(s_out, big["w_o_sgu"], name="y_sgu")
    y_attn, bufs = _mm(attn, big["w_o_attn"], name="y_attn", comm=_gather_second(bufs))
    arrived(wide[:1], bufs[:1])
    w_gate, w_up = bufs[1:]
    merged = _merge_fwd(y_attn, y_sgu, z_g, small["b_gate"])
    h1 = _mm(merged, big["w_out"], add=x, name="h1")
    f = _norm_fwd(h1, small["norm_ffn_g"], "norm_ffn_fwd")
    down = shards["w_down_ffn"]
    top = _pick(down.shape[0], down.shape[0] // 2, 2 * SUBLANES)
    gate, bufs = _mm(f, w_gate, tb=True, slab="n", name="ffn_gate", comm=_gather_first([down], rows=[(0, top)]))
    up, bufs = _mm(f, w_up, tb=True, slab="n", name="ffn_up",
                   comm=_gather_first([down], rows=[(top, down.shape[0] - top)], into=bufs))
    ffn = gate.shape[2]
    gate, up = gate.reshape(N_DEV * t, ffn), up.reshape(N_DEV * t, ffn)
    act, (w_down,) = _swiglu_fwd(gate, up, comm=_gather_second(bufs))
    act = act.reshape(N_DEV, t, ffn)
    h2 = _mm(act, w_down, slab="k", add=h1, name="h2")
    loss_row, dh2, d_norm_final = _loss_head(h2, small["norm_final_g"], target)

    def pair_sums(names, slabs, bufs):
        return [_pair_sum(g, b, "pair_sum_" + k) for k, g, b in zip(names, slabs, bufs)]

    parts, updates = {}, {}

    def update(k, comm=None):
        w, m, v = opt[k]
        updates[k], got = _adamw_shard(parts[k], w, m, v, "adamw_" + k, comm=comm)
        return got

    down_slabs = [_mm(act, dh2, ta=True, slab="m", out_dtype=BF16, name="dw_down")]
    dact, bufs = _mm(dh2, w_down, tb=True, slab="n", name="dact", comm=_to_sibling(down_slabs))
    down_pair = pair_sums(["w_down_ffn"], down_slabs, bufs)
    dgu = _swiglu_bwd(gate, up, dact.reshape(N_DEV * t, ffn)).reshape(2 * N_DEV, t, ffn)
    dw_gu, got = _mm(dgu, f, ta=True, slab="m", out_dtype=BF16, name="dw_gate_up", comm=_to_chips(down_pair))
    parts["w_down_ffn"] = got[0]
    gu_names = ["w_gate_ffn", "w_up_ffn"]
    df, bufs = _mm(dgu, w_gate, slab="k", name="df_gate", comm=_to_sibling([dw_gu, dw_gu], first=[0, N_DEV]))
    gu_pairs = [_pair_sum(dw_gu, b, "pair_sum_" + k, first=s0) for k, b, s0 in zip(gu_names, bufs, [0, N_DEV])]
    df = _mm(dgu, w_up, slab="k", a_slab0=N_DEV, add=df, name="df_up")
    dh1, d_norm_ffn = _norm_bwd(h1, small["norm_ffn_g"], df, dh2, "norm_ffn_bwd")
    dw_out = _mm(merged, dh1, ta=True, out_dtype=BF16, name="dw_out")
    out_slabs = [_slabs_from_rows(dw_out)]
    dmerged, bufs = _mm(dh1, big["w_out"], tb=True, name="dmerged", comm=_to_sibling(out_slabs))
    out_pair = pair_sums(["w_out"], out_slabs, bufs)
    dy_attn, dy_sgu, dz_g, d_b_gate = _merge_bwd(dmerged, y_attn, y_sgu, z_g, small["b_gate"])
    dw_o_sgu = _mm(s_out, dy_sgu, ta=True, out_dtype=BF16, name="dw_o_sgu")
    ds_out = _mm(dy_sgu, big["w_o_sgu"], tb=True, name="ds_out")
    dz_uv, d_ws, d_b_col, d_sgu_norm = _sgu_bwd(z_uv, ds_out, small["sgu_norm_g"], ws, b_col)
    dw_o_attn = _mm(attn, dy_attn, ta=True, out_dtype=BF16, name="dw_o_attn")
    mix_names = ["w_o_sgu", "w_o_attn"]
    mix_slabs = [_slabs_from_cols(dw_o_sgu), _slabs_from_rows(dw_o_attn)]
    dattn, bufs = _mm(dy_attn, big["w_o_attn"], tb=True, name="dattn", comm=_to_sibling(mix_slabs))
    mix_pairs = pair_sums(mix_names, mix_slabs, bufs)
    dq_c, dkv, dkpe_heads, got = _attn_bwd(q_c, kv, kpe, attn, dattn, lse, comm=_to_chips(gu_pairs))
    parts.update(zip(gu_names, got))
    dq_p = _q_rope(dq_c, cos, sin, True, "q_rope_bwd")
    dw_uq = _mm(qn, dq_p, ta=True, out_dtype=BF16, name="dw_uq")
    dw_ukv = _mm(kvn, dkv, ta=True, out_dtype=BF16, name="dw_ukv")
    dqn = _mm(dq_p, big["w_uq"], tb=True, name="dqn")
    dkvn = _mm(dkv, big["w_ukv"], tb=True, name="dkvn")
    dz_lat, d_q_norm, d_kv_norm = _lat_bwd(z_lat, small["q_norm_g"], small["kv_norm_g"], dqn, dkvn, dkpe_heads,
                                           cos, sin, ql, kvl)
    dw_g, got = _mm(dz_g, a, ta=True, out_dtype=BF16, name="dw_g", comm=_to_chips(out_pair))
    parts["w_out"] = got[0]
    dw_uv, got = _mm(dz_uv, a, ta=True, out_dtype=BF16, name="dw_uv", comm=_to_chips(mix_pairs[1:]))
    parts["w_o_attn"] = got[0]
    dw_lat, got = _mm(dz_lat, a, ta=True, out_dtype=BF16, name="dw_lat", comm=_to_chips(mix_pairs[:1]))
    parts["w_o_sgu"] = got[0]
    lat = ql + kvl + QK_ROPE
    dw_uq_cols = dw_uq.reshape(ql, heads, HEAD_PAD)[:, :, :QK_NOPE + QK_ROPE].reshape(ql, heads * (QK_NOPE + QK_ROPE))
    in_names = ["w_uq", "w_ukv", "w_in"]
    in_slabs = [_slabs_from_cols(dw_uq_cols), _slabs_from_cols(dw_ukv),
                _slabs_from_rows(jnp.concatenate([dw_lat[:lat], dw_uv, dw_g], axis=0))]
    da = _mm(dz_lat, big["w_lat_t"], name="da_lat")
    da, bufs = _mm(dz_uv, big["w_uv_t"], add=da, name="da_uv", comm=_to_sibling(in_slabs))
    uq_pair, ukv_pair, in_pair = pair_sums(in_names, in_slabs, bufs)
    cols = in_pair.shape[2]
    chunk = _pick(cols, cols // TAIL_CHUNKS)
    chunks = [("c", c0, chunk) for c0 in range(0, cols, chunk)]
    da, got = _mm(dz_g, big["w_g_t"], add=da, name="da_g",
                  comm=_to_chips([uq_pair, in_pair], rows=[None, chunks[0]]))
    parts["w_uq"], in_parts = got
    grad_x, d_norm_mix, got = _norm_bwd(x, small["norm_mix_g"], da, dh1, "norm_mix_bwd", comm=_to_chips([ukv_pair]))
    parts["w_ukv"] = got[0]
    hosts = ["w_gate_ffn", "w_up_ffn", "w_down_ffn", "w_out", "w_o_attn", "w_o_sgu", "w_uq", "w_ukv"]
    assert len(chunks) <= 1 + len(hosts)
    for i, k in enumerate(hosts):
        if 1 + i < len(chunks):
            in_parts = update(k, comm=_to_chips([in_pair], rows=[chunks[1 + i]], into=[in_parts]))[0]
        else:
            update(k)
    parts["w_in"] = in_parts
    update("w_in")

    gs = {"norm_mix_g": d_norm_mix, "b_gate": d_b_gate, "q_norm_g": d_q_norm, "kv_norm_g": d_kv_norm,
          "sgu_norm_g": d_sgu_norm, "w_sgu": d_ws, "b_sgu_col": d_b_col, "norm_ffn_g": d_norm_ffn,
          "norm_final_g": d_norm_final}
    return loss_row, grad_x, gs, updates


def _my_place():
    return lax.axis_index("x"), lax.axis_index("y"), lax.axis_index("c")


def _all_gather(shards):
    n = len(shards)

    def body(*refs):
        ins, outs = refs[:n], refs[n:2 * n]
        send_sems, recv_sems, local_sems = refs[2 * n:]
        x, y, c = _my_place()
        me, sibling = (x, y, c), (x, y, 1 - c)
        chips = [(1 - x, y), (x, 1 - y), (1 - x, 1 - y)]

        def slab(w, place):
            return outs[w].at[4 * place[0] + 2 * place[1] + place[2]]

        def copy(w, k, place, to, src=None):
            return pltpu.make_async_remote_copy(
                src_ref=slab(w, place) if src is None else src, dst_ref=slab(w, place),
                send_sem=send_sems.at[w, k], recv_sem=recv_sems.at[w, k], device_id=to, device_id_type=MESH)

        mine = [pltpu.make_async_copy(ins[w], slab(w, me), local_sems.at[w]) for w in range(n)]
        for cp in mine:
            cp.start()
        started = []
        for w in range(n):
            first = [copy(w, 0, me, sibling, src=ins[w])]
            first += [copy(w, 1 + j, me, (*chip, c), src=ins[w]) for j, chip in enumerate(chips)]
            for cp in first:
                cp.start()
            started += first
        for w in range(n):
            for j, chip in enumerate(chips):
                copy(w, 1 + j, (*chip, c), me).wait_recv()
                fwd = copy(w, 4 + j, (*chip, c), sibling)
                fwd.start()
                started.append(fwd)
        for w in range(n):
            copy(w, 0, sibling, me).wait_recv()
            for j, chip in enumerate(chips):
                copy(w, 4 + j, (*chip, 1 - c), me).wait_recv()
        for cp in started:
            cp.wait_send()
        for cp in mine:
            cp.wait()

    any_spec = pl.BlockSpec(memory_space=pl.ANY)
    return pl.pallas_call(
        body, in_specs=[any_spec] * n, out_specs=[any_spec] * n,
        out_shape=[jax.ShapeDtypeStruct((N_DEV,) + s.shape, s.dtype) for s in shards],
        scratch_shapes=[pltpu.SemaphoreType.DMA((n, 7)), pltpu.SemaphoreType.DMA((n, 7)), pltpu.SemaphoreType.DMA((n,))],
        compiler_params=pltpu.CompilerParams(has_side_effects=True), name="all_gather_weights")(*shards)


N_CHIPS = N_DEV // 2


def _gather_first(shards, rows=None, into=None):
    n = len(shards)
    rows = rows or [(0, s.shape[0]) for s in shards]

    def copies(ins, outs, sems):
        x, y, c = _my_place()
        send_sems, recv_sems, local_sems = sems
        me = 4 * x + 2 * y + c
        targets = [(x, y, 1 - c), (1 - x, y, c), (x, 1 - y, c), (1 - x, 1 - y, c)]
        out = []
        for w in range(n):
            r0, nr = rows[w]
            src, dst = ins[w].at[pl.ds(r0, nr)], outs[w].at[me, pl.ds(r0, nr)]
            out.append(pltpu.make_async_copy(src, dst, local_sems.at[w]))
            out += [pltpu.make_async_remote_copy(src_ref=src, dst_ref=dst, send_sem=send_sems.at[w, k],
                                                 recv_sem=recv_sems.at[w, k], device_id=to, device_id_type=MESH)
                    for k, to in enumerate(targets)]
        return out

    def start(ins, outs, sems):
        for cp in copies(ins, outs, sems):
            cp.start()

    def finish(ins, outs, sems):
        for cp in copies(ins, outs, sems):
            cp.wait()

    return _Comm(list(shards) + list(into or []), [jax.ShapeDtypeStruct((N_DEV,) + s.shape, s.dtype) for s in shards],
                 [pltpu.SemaphoreType.DMA((n, 4)), pltpu.SemaphoreType.DMA((n, 4)), pltpu.SemaphoreType.DMA((n,))],
                 start, finish, aliases={n + w: w for w in range(n)} if into else None)


def _gather_second(bufs):
    n = len(bufs)

    def copies(ins, outs, sems):
        x, y, c = _my_place()
        send_sems, recv_sems = sems
        out = []
        for w in range(n):
            for j, (cx, cy) in enumerate([(1 - x, y), (x, 1 - y), (1 - x, 1 - y)]):
                slab = 4 * cx + 2 * cy + c
                out.append(pltpu.make_async_remote_copy(
                    src_ref=ins[w].at[slab], dst_ref=outs[w].at[slab], send_sem=send_sems.at[w, j],
                    recv_sem=recv_sems.at[w, j], device_id=(x, y, 1 - c), device_id_type=MESH))
        return out

    def start(ins, outs, sems):
        for cp in copies(ins, outs, sems):
            cp.start()

    def finish(ins, outs, sems):
        for cp in copies(ins, outs, sems):
            cp.wait()

    return _Comm(bufs, [jax.ShapeDtypeStruct(b.shape, b.dtype) for b in bufs],
                 [pltpu.SemaphoreType.DMA((n, 3)), pltpu.SemaphoreType.DMA((n, 3))], start, finish,
                 aliases={w: w for w in range(n)})


def _to_sibling(grads, first=None):
    n = len(grads)
    first = first or [0] * n

    def copies(ins, outs, sems):
        x, y, c = _my_place()
        send_sems, recv_sems = sems
        return [pltpu.make_async_remote_copy(
            src_ref=ins[w].at[first[w] + 2 * i + (1 - c)], dst_ref=outs[w].at[i], send_sem=send_sems.at[w, i],
            recv_sem=recv_sems.at[w, i], device_id=(x, y, 1 - c), device_id_type=MESH)
            for w in range(n) for i in range(N_CHIPS)]

    def start(ins, outs, sems):
        for cp in copies(ins, outs, sems):
            cp.start()

    def finish(ins, outs, sems):
        for cp in copies(ins, outs, sems):
            cp.wait()

    return _Comm(grads, [jax.ShapeDtypeStruct((N_CHIPS,) + g.shape[1:], g.dtype) for g in grads],
                 [pltpu.SemaphoreType.DMA((n, N_CHIPS)), pltpu.SemaphoreType.DMA((n, N_CHIPS))], start, finish)


def _window(ref, slab, win):
    if win is None:
        return ref.at[slab]
    if win[0] == "r":
        return ref.at[slab, pl.ds(win[1], win[2])]
    return ref.at[slab, slice(None), pl.ds(win[1], win[2])]


def _to_chips(parts, rows=None, into=None):
    n = len(parts)
    rows = rows or [None] * n

    def copies(ins, outs, sems):
        x, y, c = _my_place()
        send_sems, recv_sems, local_sems = sems
        mine = 2 * x + y
        chips = [(1 - x, y), (x, 1 - y), (1 - x, 1 - y)]
        remote = [pltpu.make_async_remote_copy(
            src_ref=_window(ins[w], 2 * cx + cy, rows[w]), dst_ref=_window(outs[w], mine, rows[w]),
            send_sem=send_sems.at[w, j], recv_sem=recv_sems.at[w, j], device_id=(cx, cy, c), device_id_type=MESH)
            for w in range(n) for j, (cx, cy) in enumerate(chips)]
        local = [pltpu.make_async_copy(_window(ins[w], mine, rows[w]), _window(outs[w], mine, rows[w]),
                                       local_sems.at[w]) for w in range(n)]
        return remote + local

    def start(ins, outs, sems):
        for cp in copies(ins, outs, sems):
            cp.start()

    def finish(ins, outs, sems):
        for cp in copies(ins, outs, sems):
            cp.wait()

    return _Comm(list(parts) + list(into or []), [jax.ShapeDtypeStruct(p.shape, p.dtype) for p in parts],
                 [pltpu.SemaphoreType.DMA((n, N_CHIPS - 1)), pltpu.SemaphoreType.DMA((n, N_CHIPS - 1)),
                  pltpu.SemaphoreType.DMA((n,))], start, finish,
                 aliases={n + w: w for w in range(n)} if into else None)


def _pair_sum(g, buf, name, first=0):
    _, r, c = g.shape
    tr, tc = _shard_tile(r, c)
    core = (lax.axis_index("c") + first).astype(jnp.int32).reshape(1)

    def body(core_ref, g_ref, b_ref, o_ref):
        o_ref[...] = (g_ref[...].astype(F32) + b_ref[...].astype(F32)).astype(o_ref.dtype)

    blk = (1, tr, tc)
    return pl.pallas_call(
        body, grid_spec=pltpu.PrefetchScalarGridSpec(
            num_scalar_prefetch=1, grid=(N_CHIPS, r // tr, c // tc),
            in_specs=[pl.BlockSpec(blk, lambda i, j, l, core_ref: (2 * i + core_ref[0], j, l)),
                      pl.BlockSpec(blk, lambda i, j, l, core_ref: (i, j, l))],
            out_specs=pl.BlockSpec(blk, lambda i, j, l, core_ref: (i, j, l))),
        out_shape=jax.ShapeDtypeStruct(buf.shape, buf.dtype),
        compiler_params=_params(("parallel", "parallel", "parallel")), name=name)(core, g, buf)


def _all_reduce_pack(pack):
    r = pack.shape[0]

    def body(x_ref, out_ref, gath_ref, send_sems, recv_sems, local_sem):
        x, y, c = _my_place()
        me, sibling = (x, y, c), (x, y, 1 - c)
        chips = [(1 - x, y), (x, 1 - y), (1 - x, 1 - y)]

        def slab(place):
            return gath_ref.at[4 * place[0] + 2 * place[1] + place[2]]

        def copy(k, place, to, src=None):
            return pltpu.make_async_remote_copy(
                src_ref=slab(place) if src is None else src, dst_ref=slab(place),
                send_sem=send_sems.at[k], recv_sem=recv_sems.at[k], device_id=to, device_id_type=MESH)

        mine = pltpu.make_async_copy(x_ref, slab(me), local_sem)
        mine.start()
        first = [copy(0, me, sibling, src=x_ref)]
        first += [copy(1 + j, me, (*chip, c), src=x_ref) for j, chip in enumerate(chips)]
        for cp in first:
            cp.start()
        passed = [copy(4 + j, (*chip, c), sibling) for j, chip in enumerate(chips)]
        for j, chip in enumerate(chips):
            copy(1 + j, (*chip, c), me).wait_recv()
            passed[j].start()
        copy(0, sibling, me).wait_recv()
        for j, chip in enumerate(chips):
            copy(4 + j, (*chip, 1 - c), me).wait_recv()
        for cp in first + passed:
            cp.wait_send()
        mine.wait()
        acc = gath_ref[0]
        for i in range(1, N_DEV):
            acc = acc + gath_ref[i]
        out_ref[...] = acc

    vmem = pl.BlockSpec(memory_space=pltpu.VMEM)
    return pl.pallas_call(
        body, in_specs=[vmem], out_specs=vmem, out_shape=jax.ShapeDtypeStruct(pack.shape, F32),
        scratch_shapes=[pltpu.VMEM((N_DEV, r, LANES), F32), pltpu.SemaphoreType.DMA((7,)),
                        pltpu.SemaphoreType.DMA((7,)), pltpu.SemaphoreType.DMA],
        compiler_params=pltpu.CompilerParams(vmem_limit_bytes=VMEM_LIMIT), name="all_reduce_small")(pack)


def _adamw_math(w, g, m, v):
    m = ADAM_B1 * m + (1.0 - ADAM_B1) * g
    v = ADAM_B2 * v + (1.0 - ADAM_B2) * (g * g)
    m_hat = m / (1.0 - ADAM_B1 ** ADAM_STEP)
    v_hat = v / (1.0 - ADAM_B2 ** ADAM_STEP)
    delta = -ADAM_LR * (m_hat / (jnp.sqrt(v_hat) + ADAM_EPS) + ADAM_WD * w)
    return delta, m, v


def _adamw_shard(parts, w, m, v, name, comm=None):
    r, c = w.shape
    n_parts = parts.shape[0]
    tr, tc = _shard_tile(r, c)

    def body(p_ref, w_ref, m_ref, v_ref, g_ref, d_ref, nm_ref, nv_ref):
        g = p_ref[0].astype(F32)
        for i in range(1, n_parts):
            g = g + p_ref[i].astype(F32)
        g_ref[...] = g
        d_ref[...], nm_ref[...], nv_ref[...] = _adamw_math(w_ref[...], g, m_ref[...], v_ref[...])

    spec = pl.BlockSpec((tr, tc), lambda i, j: (i, j))
    outs, comm_outs = _call(
        body, grid=(r // tr, c // tc),
        in_specs=[pl.BlockSpec((n_parts, tr, tc), lambda i, j: (0, i, j)), spec, spec, spec],
        out_specs=[spec] * 4, out_shape=[jax.ShapeDtypeStruct((r, c), F32)] * 4,
        sem=("parallel", "parallel"), name=name, args=(parts, w, m, v), comm=comm)
    return outs, comm_outs


def _adamw_pack(g, w, m, v):
    r, c = w.shape

    def body(g_ref, w_ref, m_ref, v_ref, d_ref, nm_ref, nv_ref):
        d_ref[...], nm_ref[...], nv_ref[...] = _adamw_math(w_ref[...], g_ref[...], m_ref[...], v_ref[...])

    return pl.pallas_call(
        body, in_specs=[_full((r, c))] * 4, out_specs=[_full((r, c))] * 3, grid=(1,),
        out_shape=[jax.ShapeDtypeStruct((r, c), F32)] * 3,
        compiler_params=_params(("arbitrary",)), name="adamw_small")(g, w, m, v)


def _cols_from_slabs(g):
    return jnp.transpose(g, (1, 0, 2)).reshape(g.shape[1], N_DEV * g.shape[2])


def _slabs_from_cols(w):
    r, c8 = w.shape
    return jnp.transpose(w.reshape(r, N_DEV, c8 // N_DEV), (1, 0, 2))


def _rows_from_slabs(g):
    return g.reshape(N_DEV * g.shape[1], g.shape[2])


def _slabs_from_rows(w):
    return w.reshape(N_DEV, w.shape[0] // N_DEV, w.shape[1])


def _compute_layout(gathered, ql, kvl, heads, sw):
    out = {}
    for k, g in gathered.items():
        if k == "w_in":
            lat = ql + kvl + QK_ROPE
            w_in_t = _rows_from_slabs(g)
            out["w_lat_t"] = jnp.pad(w_in_t[:lat], ((0, LANES - QK_ROPE), (0, 0)))
            out["w_uv_t"] = w_in_t[lat:lat + 2 * sw]
            out["w_g_t"] = w_in_t[lat + 2 * sw:]
        elif k == "w_uq":
            per_head = _cols_from_slabs(g).reshape(ql, heads, QK_NOPE + QK_ROPE)
            pad = HEAD_PAD - QK_NOPE - QK_ROPE
            out["w_uq"] = jnp.pad(per_head, ((0, 0), (0, 0), (0, pad))).reshape(ql, heads * HEAD_PAD)
        elif k in ("w_o_attn", "w_out", "w_down_ffn"):
            out[k.removesuffix("_ffn")] = _rows_from_slabs(g)
        else:
            out[k.removesuffix("_ffn")] = _cols_from_slabs(g)
    return out


_SMALL =["norm_mix_g", "b_gate", "q_norm_g", "kv_norm_g", "sgu_norm_g", "w_sgu", "b_sgu", "norm_ffn_g", "norm_final_g"]
_BIG = ["w_in", "w_uq", "w_ukv", "w_o_attn", "w_o_sgu", "w_out", "w_gate_ffn", "w_up_ffn", "w_down_ffn"]
_TRANSPOSED = ("w_in", "w_gate_ffn", "w_up_ffn")
_ORDER = ["norm_mix_g", "w_in", "b_gate", "q_norm_g", "w_uq", "kv_norm_g", "w_ukv", "w_o_attn", "sgu_norm_g", "w_sgu",
          "b_sgu", "w_o_sgu", "w_out", "norm_ffn_g", "w_gate_ffn", "w_up_ffn", "w_down_ffn", "norm_final_g"]


def _pack_rows(parts):
    rows, sizes = [], []
    for p in parts:
        flat = p.reshape(-1)
        n = flat.shape[0]
        padded = -(-n // (SUBLANES * LANES)) * (SUBLANES * LANES)
        rows.append(jnp.pad(flat, (0, padded - n)).reshape(padded // LANES, LANES))
        sizes.append((n, padded // LANES))
    return jnp.concatenate(rows, axis=0), sizes


def _unpack_rows(pack, sizes, shapes):
    out, r0 = [], 0
    for (n, nr), shp in zip(sizes, shapes):
        out.append(pack[r0:r0 + nr].reshape(-1)[:n].reshape(shp))
        r0 += nr
    return out


def kernel(x, positions, norm_mix_g, w_in, b_gate, q_norm_g, w_uq, kv_norm_g, w_ukv, w_o_attn, sgu_norm_g, w_sgu, b_sgu, w_o_sgu, w_out, norm_ffn_g, w_gate_ffn, w_up_ffn, w_down_ffn, norm_final_g, loss_target, m_norm_mix_g, m_w_in, m_b_gate, m_q_norm_g, m_w_uq, m_kv_norm_g, m_w_ukv, m_w_o_attn, m_sgu_norm_g, m_w_sgu, m_b_sgu, m_w_o_sgu, m_w_out, m_norm_ffn_g, m_w_gate_ffn, m_w_up_ffn, m_w_down_ffn, m_norm_final_g, v_norm_mix_g, v_w_in, v_b_gate, v_q_norm_g, v_w_uq, v_kv_norm_g, v_w_ukv, v_w_o_attn, v_sgu_norm_g, v_w_sgu, v_b_sgu, v_w_o_sgu, v_w_out, v_norm_ffn_g, v_w_gate_ffn, v_w_up_ffn, v_w_down_ffn, v_norm_final_g):
    wts = dict(norm_mix_g=norm_mix_g, w_in=w_in, b_gate=b_gate, q_norm_g=q_norm_g, w_uq=w_uq, kv_norm_g=kv_norm_g,
               w_ukv=w_ukv, w_o_attn=w_o_attn, sgu_norm_g=sgu_norm_g, w_sgu=w_sgu, b_sgu=b_sgu, w_o_sgu=w_o_sgu,
               w_out=w_out, norm_ffn_g=norm_ffn_g, w_gate_ffn=w_gate_ffn, w_up_ffn=w_up_ffn, w_down_ffn=w_down_ffn,
               norm_final_g=norm_final_g)
    mom = dict(norm_mix_g=m_norm_mix_g, w_in=m_w_in, b_gate=m_b_gate, q_norm_g=m_q_norm_g, w_uq=m_w_uq,
               kv_norm_g=m_kv_norm_g, w_ukv=m_w_ukv, w_o_attn=m_w_o_attn, sgu_norm_g=m_sgu_norm_g, w_sgu=m_w_sgu,
               b_sgu=m_b_sgu, w_o_sgu=m_w_o_sgu, w_out=m_w_out, norm_ffn_g=m_norm_ffn_g, w_gate_ffn=m_w_gate_ffn,
               w_up_ffn=m_w_up_ffn, w_down_ffn=m_w_down_ffn, norm_final_g=m_norm_final_g)
    var = dict(norm_mix_g=v_norm_mix_g, w_in=v_w_in, b_gate=v_b_gate, q_norm_g=v_q_norm_g, w_uq=v_w_uq,
               kv_norm_g=v_kv_norm_g, w_ukv=v_w_ukv, w_o_attn=v_w_o_attn, sgu_norm_g=v_sgu_norm_g, w_sgu=v_w_sgu,
               b_sgu=v_b_sgu, w_o_sgu=v_w_o_sgu, w_out=v_w_out, norm_ffn_g=v_norm_ffn_g, w_gate_ffn=v_w_gate_ffn,
               w_up_ffn=v_w_up_ffn, w_down_ffn=v_w_down_ffn, norm_final_g=v_norm_final_g)

    t, d = x.shape[1], x.shape[2]
    ql, kvl = q_norm_g.shape[1], kv_norm_g.shape[1]
    heads = (w_uq.shape[2] * N_DEV) // (QK_NOPE + QK_ROPE)
    sw = sgu_norm_g.shape[1]

    def shard(a, k):
        return a[0].T if k in _TRANSPOSED else a[0]

    def unshard(a, k):
        return (a.T if k in _TRANSPOSED else a).reshape(wts[k].shape)

    opt = {k: (shard(wts[k], k), shard(mom[k], k), shard(var[k], k)) for k in _BIG}
    shards = {k: opt[k][0].astype(BF16) for k in _BIG}
    small = {
        "norm_mix_g": norm_mix_g, "b_gate": b_gate, "q_norm_g": q_norm_g, "kv_norm_g": kv_norm_g,
        "sgu_norm_g": sgu_norm_g, "w_sgu": w_sgu[0], "b_sgu_col": b_sgu[0][:, :, None], "norm_ffn_g": norm_ffn_g,
        "norm_final_g": norm_final_g[None, :],
    }

    loss_row, grad_x, gs, updates = _local_step(x[0], positions.reshape(t, 1), loss_target[0], small, shards, opt)
    grads, deltas, new_m, new_v = {}, {}, {}, {}
    for k in _BIG:
        grads[k], deltas[k], new_m[k], new_v[k] = (unshard(a, k) for a in updates[k])

    small_grads = [gs["norm_mix_g"], gs["b_gate"], gs["q_norm_g"], gs["kv_norm_g"], gs["sgu_norm_g"], gs["w_sgu"],
                   gs["b_sgu_col"], gs["norm_ffn_g"], gs["norm_final_g"]]
    pack, sizes = _pack_rows([loss_row] + small_grads)
    total = _all_reduce_pack(pack)
    shapes = [(1, LANES)] + [wts[k].shape for k in _SMALL]
    unpacked = _unpack_rows(total, sizes, shapes)
    loss = unpacked[0][0, 0]
    for k, g in zip(_SMALL, unpacked[1:]):
        grads[k] = g
    g_pack = total[sizes[0][1]:]
    w_pack, _ = _pack_rows([wts[k] for k in _SMALL])
    m_pack, _ = _pack_rows([mom[k] for k in _SMALL])
    v_pack, _ = _pack_rows([var[k] for k in _SMALL])
    d_pack, nm_pack, nv_pack = _adamw_pack(g_pack, w_pack, m_pack, v_pack)
    small_shapes = [wts[k].shape for k in _SMALL]
    for store, pk in ((deltas, d_pack), (new_m, nm_pack), (new_v, nv_pack)):
        for k, a in zip(_SMALL, _unpack_rows(pk, sizes[1:], small_shapes)):
            store[k] = a

    return (loss, grad_x[None], *[grads[k] for k in _ORDER], *[deltas[k] for k in _ORDER],
            *[new_m[k] for k in _ORDER], *[new_v[k] for k in _ORDER])
```

```python
import functools
import math

import jax
import jax.numpy as jnp
from jax import lax
from jax.experimental import pallas as pl
from jax.experimental.pallas import tpu as pltpu

F32 = jnp.float32
BF16 = jnp.bfloat16

N_DEV = 8
N_HEADS = 16
QK_NOPE = 128
QK_ROPE = 64
V_HEAD = 128
HEAD_PAD = 256
ROPE_THETA = 10000.0
CHUNK = 128
SGU_GROUP = 128
RMS_EPS = 1e-6
LANES = 128
SUBLANES = 8

ADAM_LR = 0.001
ADAM_B1 = 0.9
ADAM_B2 = 0.999
ADAM_EPS = 1e-08
ADAM_WD = 0.01
ADAM_STEP = 10

VMEM_LIMIT = 48 * 1024 * 1024
MM_TILE = (1024, 512, 2048)
MM_TILE_TA = (512, 2048)
ATTN_TILE = 512
ROW_KERNEL_BYTES = 24 * 1024 * 1024
SHARD_TILE_ELEMS = 256 * 1024
SLABS_PER_STEP = 4
TAIL_CHUNKS = 4
NEG_BIG = -1e30
MESH = pl.DeviceIdType.MESH


def _pick(n, target, mult=LANES):
    best = None
    d = mult
    while d <= min(n, target):
        if n % d == 0:
            best = d
        d += mult
    return best or n


def _row_tile(t, width, n_blocks, mult=2 * SUBLANES):
    return _pick(t, max(mult, ROW_KERNEL_BYTES // (3 * n_blocks * width * 4)), mult)


def _shard_tile(r, c, elems=SHARD_TILE_ELEMS, max_rows=256):
    tr = _pick(r, max_rows, 2 * SUBLANES)
    return tr, _pick(c, max(LANES, elems // tr))


def _params(sem):
    return pltpu.CompilerParams(dimension_semantics=sem, vmem_limit_bytes=VMEM_LIMIT)


def _full(shape):
    nd = len(shape)
    return pl.BlockSpec(shape, lambda *_: (0,) * nd)


def _rows(tr, w, cb=0):
    return pl.BlockSpec((tr, w), lambda i: (i, cb))


class _Comm:
    def __init__(self, ins, out_shapes, sems, start, finish, aliases=None):
        self.ins, self.out_shapes, self.sems, self.start, self.finish = list(ins), list(out_shapes), list(sems), start, finish
        self.aliases = dict(aliases or {})


def _call(body, *, grid, in_specs, out_specs, out_shape, scratch_shapes=(), sem, name, args, comm=None):
    if comm is None:
        outs = pl.pallas_call(body, grid=grid, in_specs=list(in_specs), out_specs=list(out_specs),
                              out_shape=list(out_shape), scratch_shapes=list(scratch_shapes),
                              compiler_params=_params(sem), name=name)(*args)
        return list(outs), []
    n_in, n_out, n_sc = len(in_specs), len(out_shape), len(scratch_shapes)
    nci, nco = len(comm.ins), len(comm.out_shapes)

    def hosted(*refs):
        ins, refs = refs[:n_in], refs[n_in:]
        cins, refs = refs[:nci], refs[nci:]
        outs, refs = refs[:n_out], refs[n_out:]
        couts, refs = refs[:nco], refs[nco:]
        scratch, csems = refs[:n_sc], refs[n_sc:]
        ids = [pl.program_id(i) for i in range(len(grid))]
        first = functools.reduce(jnp.logical_and, [i == 0 for i in ids])
        last = functools.reduce(jnp.logical_and, [i == g - 1 for i, g in zip(ids, grid)])

        @pl.when(first)
        def _():
            comm.start(cins, couts, csems)

        body(*ins, *outs, *scratch)

        @pl.when(last)
        def _():
            comm.finish(cins, couts, csems)

    any_spec = pl.BlockSpec(memory_space=pl.ANY)
    res = pl.pallas_call(
        hosted, grid=grid, in_specs=list(in_specs) + [any_spec] * nci, out_specs=list(out_specs) + [any_spec] * nco,
        out_shape=list(out_shape) + comm.out_shapes, scratch_shapes=list(scratch_shapes) + comm.sems,
        input_output_aliases={n_in + i: n_out + o for i, o in comm.aliases.items()},
        compiler_params=pltpu.CompilerParams(dimension_semantics=("arbitrary",) * len(grid),
                                             vmem_limit_bytes=VMEM_LIMIT, has_side_effects=True),
        name=name)(*args, *comm.ins)
    return list(res[:n_out]), list(res[n_out:])


def _mm(a, b, *, ta=False, tb=False, add=None, out_dtype=F32, tm=None, tn=None, tk=None, name, comm=None,
        slab=None, a_slab0=0):
    sq = None
    if ta:
        tm, tn = tm or MM_TILE_TA[0], tn or MM_TILE_TA[1]
    if slab is None:
        m, k = (a.shape[1], a.shape[0]) if ta else a.shape
        n = b.shape[0] if tb else b.shape[1]
        assert k == (b.shape[1] if tb else b.shape[0]), (a.shape, b.shape, ta, tb)
        tm, tn, tk = _pick(m, tm or MM_TILE[0]), _pick(n, tn or MM_TILE[1]), _pick(k, tk or MM_TILE[2])
        grid = (m // tm, n // tn, k // tk)
        a_spec = pl.BlockSpec((tk, tm), lambda i, j, kk: (kk, i)) if ta else pl.BlockSpec((tm, tk), lambda i, j, kk: (i, kk))
        b_spec = pl.BlockSpec((tn, tk), lambda i, j, kk: (j, kk)) if tb else pl.BlockSpec((tk, tn), lambda i, j, kk: (kk, j))
        o_spec, o_shape = pl.BlockSpec((tm, tn), lambda i, j, kk: (i, j)), (m, n)
    elif slab == "n":
        m, k = (a.shape[1], a.shape[0]) if ta else a.shape
        s, c = b.shape[0], (b.shape[1] if tb else b.shape[2])
        assert k == (b.shape[2] if tb else b.shape[1]), (a.shape, b.shape, ta, tb)
        tm, tn, tk = _pick(m, tm or MM_TILE[0]), c, _pick(k, tk or MM_TILE[2])
        grid = (m // tm, s, k // tk)
        a_spec = pl.BlockSpec((tk, tm), lambda i, j, kk: (kk, i)) if ta else pl.BlockSpec((tm, tk), lambda i, j, kk: (i, kk))
        b_spec = (pl.BlockSpec((sq, c, tk), lambda i, j, kk: (j, 0, kk)) if tb
                  else pl.BlockSpec((sq, tk, c), lambda i, j, kk: (j, kk, 0)))
        o_spec, o_shape = pl.BlockSpec((sq, tm, c), lambda i, j, kk: (j, i, 0)), (s, m, c)
    elif slab == "m":
        assert ta and not tb
        s, k, c = a.shape
        n = b.shape[1]
        assert k == b.shape[0], (a.shape, b.shape)
        tm, tn, tk = c, _pick(n, tn or MM_TILE[1]), _pick(k, tk or MM_TILE[2])
        grid = (s, n // tn, k // tk)
        a_spec = pl.BlockSpec((sq, tk, c), lambda i, j, kk: (i, kk, 0))
        b_spec = pl.BlockSpec((tk, tn), lambda i, j, kk: (kk, j))
        o_spec, o_shape = pl.BlockSpec((sq, c, tn), lambda i, j, kk: (i, 0, j)), (s, c, n)
    else:
        assert slab == "k" and not ta
        s, c = b.shape[0], (b.shape[2] if tb else b.shape[1])
        m, n = a.shape[1], (b.shape[1] if tb else b.shape[2])
        assert a.shape[2] == c and a.shape[0] >= a_slab0 + s, (a.shape, b.shape, a_slab0)
        tm, tn, tk = _pick(m, tm or MM_TILE[0]), _pick(n, tn or MM_TILE[1]), c
        per_step = SLABS_PER_STEP if (s % SLABS_PER_STEP == 0 and a_slab0 % SLABS_PER_STEP == 0) else 1
        first = a_slab0 // per_step
        grid = (m // tm, n // tn, s // per_step)
        a_spec = pl.BlockSpec((per_step, tm, c), lambda i, j, kk: (kk + first, i, 0))
        b_spec = (pl.BlockSpec((per_step, tn, c), lambda i, j, kk: (kk, j, 0)) if tb
                  else pl.BlockSpec((per_step, c, tn), lambda i, j, kk: (kk, 0, j)))
        o_spec, o_shape = pl.BlockSpec((tm, tn), lambda i, j, kk: (i, j)), (m, n)
    nk = grid[2]
    dims = (((0 if ta else 1,), (1 if tb else 0,)), ((), ()))

    def product(a_ref, b_ref):
        if slab != "k":
            return lax.dot_general(a_ref[...].astype(BF16), b_ref[...].astype(BF16), dims, preferred_element_type=F32)
        r = None
        for u in range(a_ref.shape[0]):
            p = lax.dot_general(a_ref[u].astype(BF16), b_ref[u].astype(BF16), dims, preferred_element_type=F32)
            r = p if r is None else r + p
        return r

    def body(*refs):
        a_ref, b_ref = refs[:2]
        add_ref = refs[2] if add is not None else None
        o_ref = refs[3] if add is not None else refs[2]
        acc_ref = refs[-1] if nk > 1 else None

        def finish(r):
            if add_ref is not None:
                r = r + add_ref[...].astype(F32)
            o_ref[...] = r.astype(o_ref.dtype)

        if nk == 1:
            finish(product(a_ref, b_ref))
            return
        kk = pl.program_id(2)

        @pl.when(kk == 0)
        def _():
            acc_ref[...] = product(a_ref, b_ref)

        if nk > 2:
            @pl.when(jnp.logical_and(kk > 0, kk < nk - 1))
            def _():
                acc_ref[...] += product(a_ref, b_ref)

        @pl.when(kk == nk - 1)
        def _():
            finish(acc_ref[...] + product(a_ref, b_ref))

    in_specs = [a_spec, b_spec] + ([o_spec] if add is not None else [])
    args = (a, b) + ((add,) if add is not None else ())
    outs, comm_outs = _call(
        body, grid=grid, in_specs=in_specs, out_specs=[o_spec],
        out_shape=[jax.ShapeDtypeStruct(o_shape, out_dtype)],
        scratch_shapes=[pltpu.VMEM((tm, tn), F32)] if nk > 1 else [],
        sem=("parallel", "parallel", "arbitrary"), name=name, args=args, comm=comm)
    return outs[0] if comm is None else (outs[0], comm_outs)


def _rms_scale(x):
    return lax.rsqrt(jnp.mean(x * x, axis=-1, keepdims=True) + RMS_EPS)


def _rms_bwd(xhat, r, g, dy):
    t = dy * g
    dx = r * (t - xhat * jnp.mean(t * xhat, axis=-1, keepdims=True))
    return dx, dy * xhat


_GELU_C = math.sqrt(2.0 / math.pi)


def _gelu(x):
    return x * (0.5 * (1.0 + jnp.tanh(_GELU_C * (x + 0.044715 * (x * x * x)))))


def _gelu_and_grad(x):
    t = jnp.tanh(_GELU_C * (x + 0.044715 * (x * x * x)))
    cdf = 0.5 * (1.0 + t)
    return x * cdf, cdf + x * (0.5 * (1.0 - t * t) * (_GELU_C * (1.0 + 3.0 * 0.044715 * (x * x))))


def _sigmoid(x):
    return 1.0 / (1.0 + jnp.exp(-x))


def _swap_halves(x):
    lane = lax.broadcasted_iota(jnp.int32, x.shape, 1)
    first = (lane % QK_ROPE) < (QK_ROPE // 2)
    return jnp.where(first, pltpu.roll(x, LANES - QK_ROPE // 2, 1), pltpu.roll(x, QK_ROPE // 2, 1))


def _rope(x, cos, sin_signed):
    return x * cos + _swap_halves(x) * sin_signed


def _rope_bwd(d, cos, sin_signed):
    return d * cos + _swap_halves(d * sin_signed)


def _rope_tables(pos_col, inv_freq_row, sign_row):
    t = pos_col.shape[0]
    tr = _pick(t, 512, SUBLANES)

    def body(p_ref, f_ref, s_ref, cos_ref, sin_ref):
        ang = p_ref[...].astype(F32) * f_ref[...]
        cos_ref[...] = jnp.cos(ang)
        sin_ref[...] = jnp.sin(ang) * s_ref[...]

    return pl.pallas_call(
        body, grid=(t // tr,), in_specs=[_rows(tr, 1), _full((1, LANES)), _full((1, LANES))],
        out_specs=[_rows(tr, LANES), _rows(tr, LANES)],
        out_shape=[jax.ShapeDtypeStruct((t, LANES), F32)] * 2,
        compiler_params=_params(("parallel",)), name="rope_tables")(pos_col, inv_freq_row, sign_row)


def _norm_fwd(x, g, name):
    t, d = x.shape
    tr = _row_tile(t, d, 2)

    def body(x_ref, g_ref, y_ref):
        xv = x_ref[...]
        y_ref[...] = (xv * _rms_scale(xv) * g_ref[...]).astype(BF16)

    return pl.pallas_call(
        body, grid=(t // tr,), in_specs=[_rows(tr, d), _full((1, d))], out_specs=_rows(tr, d),
        out_shape=jax.ShapeDtypeStruct((t, d), BF16), compiler_params=_params(("parallel",)), name=name)(x, g)


def _lat_fwd(z_lat, qg, kvg, cos, sin, ql, kvl):
    t = z_lat.shape[0]
    tr = _row_tile(t, z_lat.shape[1], 2)

    def body(z_ref, qg_ref, kvg_ref, cos_ref, sin_ref, qn_ref, kvn_ref, kpe_ref):
        q = z_ref[:, 0:ql]
        qn_ref[...] = (q * _rms_scale(q) * qg_ref[...]).astype(BF16)
        kv = z_ref[:, ql:ql + kvl]
        kvn_ref[...] = (kv * _rms_scale(kv) * kvg_ref[...]).astype(BF16)
        kpe_ref[...] = _rope(z_ref[:, ql + kvl:ql + kvl + LANES], cos_ref[...], sin_ref[...]).astype(BF16)

    w = z_lat.shape[1]
    return pl.pallas_call(
        body, grid=(t // tr,),
        in_specs=[_rows(tr, w), _full((1, ql)), _full((1, kvl)), _rows(tr, LANES), _rows(tr, LANES)],
        out_specs=[_rows(tr, ql), _rows(tr, kvl), _rows(tr, LANES)],
        out_shape=[jax.ShapeDtypeStruct((t, ql), BF16), jax.ShapeDtypeStruct((t, kvl), BF16),
                   jax.ShapeDtypeStruct((t, LANES), BF16)],
        compiler_params=_params(("parallel",)), name="lat_fwd")(z_lat, qg, kvg, cos, sin)


def _q_rope(q_p, cos, sin, bwd, name):
    t, w = q_p.shape
    tr = _row_tile(t, w, 2)
    fn = _rope_bwd if bwd else _rope

    def body(q_ref, cos_ref, sin_ref, o_ref):
        c, s = cos_ref[...], sin_ref[...]
        for h in range(w // HEAD_PAD):
            o_ref[:, h * HEAD_PAD:h * HEAD_PAD + QK_NOPE] = q_ref[:, h * HEAD_PAD:h * HEAD_PAD + QK_NOPE].astype(BF16)
            lo = h * HEAD_PAD + QK_NOPE
            o_ref[:, lo:lo + LANES] = fn(q_ref[:, lo:lo + LANES].astype(F32), c, s).astype(BF16)

    return pl.pallas_call(
        body, grid=(t // tr,), in_specs=[_rows(tr, w), _rows(tr, LANES), _rows(tr, LANES)], out_specs=_rows(tr, w),
        out_shape=jax.ShapeDtypeStruct((t, w), BF16), compiler_params=_params(("parallel",)), name=name)(q_p, cos, sin)


def _tril_mask():
    r = lax.broadcasted_iota(jnp.int32, (CHUNK, CHUNK), 0)
    c = lax.broadcasted_iota(jnp.int32, (CHUNK, CHUNK), 1)
    return r >= c


def _sgu_fwd(z_uv, gs, ws, b_col):
    t = z_uv.shape[0]
    sw = z_uv.shape[1] // 2
    groups = sw // SGU_GROUP
    tr = _pick(t, 256, CHUNK)

    def body(u_ref, v_ref, gs_ref, ws_ref, b_ref, o_ref):
        v = _gelu(v_ref[...])
        vn = (v * _rms_scale(v) * gs_ref[...]).astype(BF16)
        tri = _tril_mask()
        for g in range(groups):
            wg = jnp.where(tri, ws_ref[g], 0.0).astype(BF16)
            cols = slice(g * SGU_GROUP, (g + 1) * SGU_GROUP)
            for c in range(tr // CHUNK):
                rows = slice(c * CHUNK, (c + 1) * CHUNK)
                mixed = jnp.dot(wg, vn[rows, cols], preferred_element_type=F32) + b_ref[g]
                o_ref[rows, cols] = (_gelu(u_ref[rows, cols]) * mixed).astype(BF16)

    return pl.pallas_call(
        body, grid=(t // tr,),
        in_specs=[_rows(tr, sw, 0), _rows(tr, sw, 1), _full((1, sw)), _full(ws.shape), _full(b_col.shape)],
        out_specs=_rows(tr, sw), out_shape=jax.ShapeDtypeStruct((t, sw), BF16),
        compiler_params=_params(("parallel",)), name="sgu_fwd")(z_uv, z_uv, gs, ws, b_col)


def _merge_fwd(y_attn, y_sgu, z_g, b_gate):
    t, d = y_attn.shape
    tr = _row_tile(t, d, 5)

    def body(ya_ref, ys_ref, g0_ref, g1_ref, b0_ref, b1_ref, o_ref):
        g0 = _sigmoid(g0_ref[...] + b0_ref[...])
        g1 = _sigmoid(g1_ref[...] + b1_ref[...])
        o_ref[...] = (g0 * ya_ref[...] + g1 * ys_ref[...]).astype(BF16)

    bspec0 = pl.BlockSpec((1, d), lambda i: (0, 0))
    bspec1 = pl.BlockSpec((1, d), lambda i: (0, 1))
    return pl.pallas_call(
        body, grid=(t // tr,),
        in_specs=[_rows(tr, d), _rows(tr, d), _rows(tr, d, 0), _rows(tr, d, 1), bspec0, bspec1],
        out_specs=_rows(tr, d), out_shape=jax.ShapeDtypeStruct((t, d), BF16),
        compiler_params=_params(("parallel",)), name="merge_fwd")(y_attn, y_sgu, z_g, z_g, b_gate, b_gate)


def _swiglu_fwd(gate, up, comm=None):
    t, f = gate.shape
    tr = _row_tile(t, f, 3)

    def body(g_ref, u_ref, o_ref):
        g = g_ref[...]
        o_ref[...] = (g * _sigmoid(g) * u_ref[...]).astype(BF16)

    outs, comm_outs = _call(
        body, grid=(t // tr,), in_specs=[_rows(tr, f), _rows(tr, f)], out_specs=[_rows(tr, f)],
        out_shape=[jax.ShapeDtypeStruct((t, f), BF16)], sem=("parallel",), name="swiglu_fwd", args=(gate, up), comm=comm)
    return outs[0], comm_outs


def _loss_head(h2, g, target):
    t, d = h2.shape
    tr = _row_tile(t, d, 3)

    def body(h_ref, g_ref, t_ref, loss_ref, dh_ref, dhb_ref, dg_ref):
        @pl.when(pl.program_id(0) == 0)
        def _():
            loss_ref[...] = jnp.zeros_like(loss_ref)
            dg_ref[...] = jnp.zeros_like(dg_ref)

        h = h_ref[...]
        r = _rms_scale(h)
        hhat = h * r
        gv = g_ref[...]
        err = hhat * gv - t_ref[...]
        loss_ref[...] += jnp.full(loss_ref.shape, 0.5 * jnp.sum(jnp.mean(err * err, axis=-1)), F32)
        dx, dg_rows = _rms_bwd(hhat, r, gv, err * (1.0 / d))
        dh_ref[...] = dx
        dhb_ref[...] = dx.astype(BF16)
        dg_ref[...] += jnp.sum(dg_rows, axis=0, keepdims=True)

    return pl.pallas_call(
        body, grid=(t // tr,), in_specs=[_rows(tr, d), _full((1, d)), _rows(tr, d)],
        out_specs=[_full((1, LANES)), _rows(tr, d), _rows(tr, d), _full((1, d))],
        out_shape=[jax.ShapeDtypeStruct((1, LANES), F32), jax.ShapeDtypeStruct((t, d), F32),
                   jax.ShapeDtypeStruct((t, d), BF16), jax.ShapeDtypeStruct((1, d), F32)],
        compiler_params=_params(("arbitrary",)), name="loss_head")(h2, g, target)


def _swiglu_bwd(gate, up, dact):
    t, f = gate.shape
    tr = _row_tile(t, f, 4)

    def body(g_ref, u_ref, d_ref, dgu_ref):
        g = g_ref[...]
        s = _sigmoid(g)
        d = d_ref[...]
        dgu_ref[0] = (d * u_ref[...] * (s * (1.0 + g * (1.0 - s)))).astype(BF16)
        dgu_ref[1] = (d * (g * s)).astype(BF16)

    return pl.pallas_call(
        body, grid=(t // tr,), in_specs=[_rows(tr, f)] * 3, out_specs=pl.BlockSpec((2, tr, f), lambda i: (0, i, 0)),
        out_shape=jax.ShapeDtypeStruct((2, t, f), BF16),
        compiler_params=_params(("parallel",)), name="swiglu_bwd")(gate, up, dact)


def _norm_bwd(x, g, dy, resid, name, comm=None):
    t, d = x.shape
    tr = _row_tile(t, d, 5)

    def body(x_ref, g_ref, dy_ref, r_ref, dx_ref, dxb_ref, dg_ref):
        @pl.when(pl.program_id(0) == 0)
        def _():
            dg_ref[...] = jnp.zeros_like(dg_ref)

        xv = x_ref[...]
        r = _rms_scale(xv)
        dx, dg_rows = _rms_bwd(xv * r, r, g_ref[...], dy_ref[...])
        dx = r_ref[...] + dx
        dx_ref[...] = dx
        dxb_ref[...] = dx.astype(BF16)
        dg_ref[...] += jnp.sum(dg_rows, axis=0, keepdims=True)

    outs, comm_outs = _call(
        body, grid=(t // tr,), in_specs=[_rows(tr, d), _full((1, d)), _rows(tr, d), _rows(tr, d)],
        out_specs=[_rows(tr, d), _rows(tr, d), _full((1, d))],
        out_shape=[jax.ShapeDtypeStruct((t, d), F32), jax.ShapeDtypeStruct((t, d), BF16),
                   jax.ShapeDtypeStruct((1, d), F32)],
        sem=("arbitrary",), name=name, args=(x, g, dy, resid), comm=comm)
    return (outs[0], outs[1], outs[2]) if comm is None else (outs[0], outs[1], outs[2], comm_outs)


def _merge_bwd(dmerged, y_attn, y_sgu, z_g, b_gate):
    t, d = y_attn.shape
    tr = _row_tile(t, d, 7)

    def body(dm_ref, ya_ref, ys_ref, g0_ref, g1_ref, b0_ref, b1_ref, dya_ref, dys_ref, dz_ref, db_ref):
        @pl.when(pl.program_id(0) == 0)
        def _():
            db_ref[...] = jnp.zeros_like(db_ref)

        dm = dm_ref[...]
        g0 = _sigmoid(g0_ref[...] + b0_ref[...])
        g1 = _sigmoid(g1_ref[...] + b1_ref[...])
        dya_ref[...] = (dm * g0).astype(BF16)
        dys_ref[...] = (dm * g1).astype(BF16)
        dl0 = dm * ya_ref[...] * (g0 * (1.0 - g0))
        dl1 = dm * ys_ref[...] * (g1 * (1.0 - g1))
        dz_ref[:, 0:d] = dl0.astype(BF16)
        dz_ref[:, d:2 * d] = dl1.astype(BF16)
        db_ref[:, 0:d] += jnp.sum(dl0, axis=0, keepdims=True)
        db_ref[:, d:2 * d] += jnp.sum(dl1, axis=0, keepdims=True)

    bspec0 = pl.BlockSpec((1, d), lambda i: (0, 0))
    bspec1 = pl.BlockSpec((1, d), lambda i: (0, 1))
    return pl.pallas_call(
        body, grid=(t // tr,),
        in_specs=[_rows(tr, d), _rows(tr, d), _rows(tr, d), _rows(tr, d, 0), _rows(tr, d, 1), bspec0, bspec1],
        out_specs=[_rows(tr, d), _rows(tr, d), _rows(tr, 2 * d), _full((1, 2 * d))],
        out_shape=[jax.ShapeDtypeStruct((t, d), BF16), jax.ShapeDtypeStruct((t, d), BF16),
                   jax.ShapeDtypeStruct((t, 2 * d), BF16), jax.ShapeDtypeStruct((1, 2 * d), F32)],
        compiler_params=_params(("arbitrary",)), name="merge_bwd")(dmerged, y_attn, y_sgu, z_g, z_g, b_gate, b_gate)


def _sgu_bwd(z_uv, ds_out, gs, ws, b_col):
    t = z_uv.shape[0]
    sw = z_uv.shape[1] // 2
    groups = sw // SGU_GROUP
    tr = _pick(t, 256, CHUNK)

    def body(u_ref, v_ref, d_ref, gs_ref, ws_ref, b_ref, dz_ref, dws_ref, db_ref, dgs_ref, dvn_ref):
        @pl.when(pl.program_id(0) == 0)
        def _():
            dws_ref[...] = jnp.zeros_like(dws_ref)
            db_ref[...] = jnp.zeros_like(db_ref)
            dgs_ref[...] = jnp.zeros_like(dgs_ref)

        v, dgelu_v = _gelu_and_grad(v_ref[...])
        r = _rms_scale(v)
        vhat = v * r
        gsv = gs_ref[...]
        vn = (vhat * gsv).astype(BF16)
        tri = _tril_mask()
        for g in range(groups):
            wg = jnp.where(tri, ws_ref[g], 0.0).astype(BF16)
            cols = slice(g * SGU_GROUP, (g + 1) * SGU_GROUP)
            for c in range(tr // CHUNK):
                rows = slice(c * CHUNK, (c + 1) * CHUNK)
                vn_cg = vn[rows, cols]
                mixed = jnp.dot(wg, vn_cg, preferred_element_type=F32) + b_ref[g]
                u, dgelu_u = _gelu_and_grad(u_ref[rows, cols])
                dso = d_ref[rows, cols]
                dz_ref[rows, cols] = (dso * mixed * dgelu_u).astype(BF16)
                dmixed = dso * u
                db_ref[g] += jnp.sum(dmixed, axis=1, keepdims=True)
                dmixed_b = dmixed.astype(BF16)
                dws_ref[g] += jnp.where(
                    tri, lax.dot_general(dmixed_b, vn_cg, (((1,), (1,)), ((), ())), preferred_element_type=F32), 0.0)
                dvn_ref[rows, cols] = lax.dot_general(wg, dmixed_b, (((0,), (0,)), ((), ())), preferred_element_type=F32)
        dvn = dvn_ref[...]
        dv, dgs_rows = _rms_bwd(vhat, r, gsv, dvn)
        dz_ref[:, sw:2 * sw] = (dv * dgelu_v).astype(BF16)
        dgs_ref[...] += jnp.sum(dgs_rows, axis=0, keepdims=True)

    return pl.pallas_call(
        body, grid=(t // tr,),
        in_specs=[_rows(tr, sw, 0), _rows(tr, sw, 1), _rows(tr, sw), _full((1, sw)), _full(ws.shape), _full(b_col.shape)],
        out_specs=[_rows(tr, 2 * sw), _full(ws.shape), _full(b_col.shape), _full((1, sw))],
        out_shape=[jax.ShapeDtypeStruct((t, 2 * sw), BF16), jax.ShapeDtypeStruct(ws.shape, F32),
                   jax.ShapeDtypeStruct(b_col.shape, F32), jax.ShapeDtypeStruct((1, sw), F32)],
        scratch_shapes=[pltpu.VMEM((tr, sw), F32)],
        compiler_params=_params(("arbitrary",)), name="sgu_bwd")(z_uv, z_uv, ds_out, gs, ws, b_col)


def _lat_bwd(z_lat, qg, kvg, dqn, dkvn, dkpe_heads, cos, sin, ql, kvl):
    t, w = z_lat.shape
    heads = dkpe_heads.shape[0]
    tr = _row_tile(t, w + heads * LANES, 3)

    def body(z_ref, qg_ref, kvg_ref, dq_ref, dkv_ref, dk_ref, cos_ref, sin_ref, dz_ref, dqg_ref, dkvg_ref):
        @pl.when(pl.program_id(0) == 0)
        def _():
            dqg_ref[...] = jnp.zeros_like(dqg_ref)
            dkvg_ref[...] = jnp.zeros_like(dkvg_ref)

        q = z_ref[:, 0:ql]
        r = _rms_scale(q)
        dx, dg_rows = _rms_bwd(q * r, r, qg_ref[...], dq_ref[...])
        dz_ref[:, 0:ql] = dx.astype(BF16)
        dqg_ref[...] += jnp.sum(dg_rows, axis=0, keepdims=True)
        kv = z_ref[:, ql:ql + kvl]
        r = _rms_scale(kv)
        dx, dg_rows = _rms_bwd(kv * r, r, kvg_ref[...], dkv_ref[...])
        dz_ref[:, ql:ql + kvl] = dx.astype(BF16)
        dkvg_ref[...] += jnp.sum(dg_rows, axis=0, keepdims=True)
        dk = dk_ref[0]
        for h in range(1, heads):
            dk = dk + dk_ref[h]
        dz_ref[:, ql + kvl:ql + kvl + LANES] = _rope_bwd(dk, cos_ref[...], sin_ref[...]).astype(BF16)

    return pl.pallas_call(
        body, grid=(t // tr,),
        in_specs=[_rows(tr, w), _full((1, ql)), _full((1, kvl)), _rows(tr, ql), _rows(tr, kvl),
                  pl.BlockSpec((heads, tr, LANES), lambda i: (0, i, 0)), _rows(tr, LANES), _rows(tr, LANES)],
        out_specs=[_rows(tr, w), _full((1, ql)), _full((1, kvl))],
        out_shape=[jax.ShapeDtypeStruct((t, w), BF16), jax.ShapeDtypeStruct((1, ql), F32),
                   jax.ShapeDtypeStruct((1, kvl), F32)],
        compiler_params=_params(("arbitrary",)), name="lat_bwd")(z_lat, qg, kvg, dqn, dkvn, dkpe_heads, cos, sin)


_NT = (((1,), (1,)), ((), ()))


def _attn_scale():
    return (QK_NOPE + QK_ROPE) ** -0.5


def _attn_fwd(q_c, kv, kpe, comm=None):
    t = q_c.shape[0]
    heads = q_c.shape[1] // HEAD_PAD
    tq = _pick(t, ATTN_TILE)
    nq = t // tq
    scale = _attn_scale()
    to_log2 = scale * math.log2(math.e)
    tn_dims = (((0,), (0,)), ((), ()))

    def body(q_ref, kn_ref, kpe_ref, v_ref, o_ref, lse_ref, m_sc, l_sc, acc_sc):
        qi, ki = pl.program_id(1), pl.program_id(2)

        @pl.when(ki == 0)
        def _():
            m_sc[...] = jnp.full_like(m_sc, NEG_BIG)
            l_sc[...] = jnp.zeros_like(l_sc)
            acc_sc[...] = jnp.zeros_like(acc_sc)

        def step(diagonal):
            kc = jnp.concatenate([kn_ref[...], kpe_ref[...]], axis=1)
            st = lax.dot_general(kc, q_ref[...], _NT, preferred_element_type=F32)
            if diagonal:
                krow = lax.broadcasted_iota(jnp.int32, st.shape, 0)
                qcol = lax.broadcasted_iota(jnp.int32, st.shape, 1)
                st = jnp.where(qcol >= krow, st, NEG_BIG)
            m_prev = m_sc[...]
            m_new = jnp.maximum(m_prev, jnp.max(st, axis=0, keepdims=True))
            alpha = jnp.exp2((m_prev - m_new) * to_log2)
            pt = jnp.exp2((st - m_new) * to_log2)
            l_sc[...] = alpha * l_sc[...] + jnp.sum(pt, axis=0, keepdims=True)
            acc_sc[...] = alpha * acc_sc[...] + lax.dot_general(v_ref[...], pt.astype(BF16), tn_dims,
                                                                preferred_element_type=F32)
            m_sc[...] = m_new

        @pl.when(ki < qi)
        def _():
            step(False)

        @pl.when(ki == qi)
        def _():
            step(True)
            o_ref[...] = (acc_sc[...] / l_sc[...]).T
            lse_ref[0] = m_sc[...] * scale + jnp.log(l_sc[...])

    kmap = lambda blk: (lambda h, qi, ki: (jnp.minimum(ki, qi), 2 * h + blk))
    outs, comm_outs = _call(
        body, grid=(heads, nq, nq),
        in_specs=[pl.BlockSpec((tq, HEAD_PAD), lambda h, qi, ki: (qi, h)),
                  pl.BlockSpec((tq, QK_NOPE), kmap(0)),
                  pl.BlockSpec((tq, LANES), lambda h, qi, ki: (jnp.minimum(ki, qi), 0)),
                  pl.BlockSpec((tq, V_HEAD), kmap(1))],
        out_specs=[pl.BlockSpec((tq, V_HEAD), lambda h, qi, ki: (qi, h)),
                   pl.BlockSpec((1, 1, tq), lambda h, qi, ki: (h, 0, qi))],
        out_shape=[jax.ShapeDtypeStruct((t, heads * V_HEAD), F32), jax.ShapeDtypeStruct((heads, 1, t), F32)],
        scratch_shapes=[pltpu.VMEM((1, tq), F32), pltpu.VMEM((1, tq), F32), pltpu.VMEM((V_HEAD, tq), F32)],
        sem=("parallel", "parallel", "arbitrary"), name="attn_fwd", args=(q_c, kv, kpe, kv), comm=comm)
    return outs[0], outs[1], comm_outs


def _attn_bwd(q_c, kv, kpe, o, do, lse_row, comm=None):
    t = q_c.shape[0]
    heads = q_c.shape[1] // HEAD_PAD
    tk = _pick(t, ATTN_TILE)
    nk = t // tk
    scale = _attn_scale()
    tn_dims = (((0,), (0,)), ((), ()))

    def body(q_ref, kn_ref, kpe_ref, v_ref, do_ref, lse_ref, o_ref, dq_ref, dkv_ref, dkpe_ref, dk_sc, dv_sc, delta_sc):
        ki, qi = pl.program_id(1), pl.program_id(2)

        @pl.when(jnp.logical_and(ki == 0, qi == 0))
        def _():
            dq_ref[...] = jnp.zeros_like(dq_ref)

        @pl.when(qi == 0)
        def _():
            dk_sc[...] = jnp.zeros_like(dk_sc)
            dv_sc[...] = jnp.zeros_like(dv_sc)

        @pl.when(ki == 0)
        def _():
            delta_sc[qi] = jnp.sum((do_ref[...] * o_ref[...]).T, axis=0, keepdims=True)

        def step(diagonal):
            kc = jnp.concatenate([kn_ref[...], kpe_ref[...]], axis=1)
            q = q_ref[...]
            st = lax.dot_general(kc, q, _NT, preferred_element_type=F32) * scale
            pt = jnp.exp(st - lse_ref[0])
            if diagonal:
                krow = lax.broadcasted_iota(jnp.int32, st.shape, 0)
                qcol = lax.broadcasted_iota(jnp.int32, st.shape, 1)
                pt = jnp.where(qcol >= krow, pt, 0.0)
            do_b = do_ref[...].astype(BF16)
            dv_sc[...] += jnp.dot(pt.astype(BF16), do_b, preferred_element_type=F32)
            dpt = lax.dot_general(v_ref[...], do_b, _NT, preferred_element_type=F32)
            dst = (pt * (dpt - delta_sc[qi]) * scale).astype(BF16)
            dk_sc[...] += jnp.dot(dst, q, preferred_element_type=F32)
            rows = pl.ds(pl.multiple_of(qi * tk, tk), tk)
            dq_ref[rows, :] += lax.dot_general(dst, kc, tn_dims, preferred_element_type=F32)

        @pl.when(qi > ki)
        def _():
            step(False)

        @pl.when(qi == ki)
        def _():
            step(True)

        @pl.when(qi == nk - 1)
        def _():
            dkv_ref[:, 0:QK_NOPE] = dk_sc[:, 0:QK_NOPE].astype(BF16)
            dkv_ref[:, QK_NOPE:QK_NOPE + V_HEAD] = dv_sc[...].astype(BF16)
            dkpe_ref[0] = dk_sc[:, QK_NOPE:QK_NOPE + LANES]

    qclamp = lambda h, ki, qi: (jnp.maximum(qi, ki), h)
    kmap = lambda blk: (lambda h, ki, qi: (ki, 2 * h + blk))
    rmap = lambda h, ki, qi: (h, 0, jnp.maximum(qi, ki))
    outs, comm_outs = _call(
        body, grid=(heads, nk, nk),
        in_specs=[pl.BlockSpec((tk, HEAD_PAD), qclamp), pl.BlockSpec((tk, QK_NOPE), kmap(0)),
                  pl.BlockSpec((tk, LANES), lambda h, ki, qi: (ki, 0)), pl.BlockSpec((tk, V_HEAD), kmap(1)),
                  pl.BlockSpec((tk, V_HEAD), qclamp), pl.BlockSpec((1, 1, tk), rmap),
                  pl.BlockSpec((tk, V_HEAD), lambda h, ki, qi: (jnp.where(ki == 0, qi, 0), h))],
        out_specs=[pl.BlockSpec((t, HEAD_PAD), lambda h, ki, qi: (0, h)),
                   pl.BlockSpec((tk, HEAD_PAD), lambda h, ki, qi: (ki, h)),
                   pl.BlockSpec((1, tk, LANES), lambda h, ki, qi: (h, ki, 0))],
        out_shape=[jax.ShapeDtypeStruct((t, heads * HEAD_PAD), F32),
                   jax.ShapeDtypeStruct((t, heads * HEAD_PAD), BF16), jax.ShapeDtypeStruct((heads, t, LANES), F32)],
        scratch_shapes=[pltpu.VMEM((tk, HEAD_PAD), F32), pltpu.VMEM((tk, V_HEAD), F32), pltpu.VMEM((nk, 1, tk), F32)],
        sem=("parallel", "arbitrary", "arbitrary"), name="attn_bwd",
        args=(q_c, kv, kpe, kv, do, lse_row, o), comm=comm)
    return outs[0], outs[1], outs[2], comm_outs


def _local_step(x, pos_col, target, small, shards, opt):
    t = x.shape[0]
    ql, kvl = small["q_norm_g"].shape[1], small["kv_norm_g"].shape[1]
    sw = small["sgu_norm_g"].shape[1]
    heads = (shards["w_uq"].shape[1] * N_DEV) // (QK_NOPE + QK_ROPE)
    big = {}
    early = ["w_in", "w_uq", "w_ukv"]
    big.update(_compute_layout(dict(zip(early, _all_gather([shards[k] for k in early]))), ql, kvl, heads, sw))
    half = QK_ROPE // 2
    lane = jnp.arange(LANES)
    inv_freq = ROPE_THETA ** (-jnp.arange(0, QK_ROPE, 2, dtype=F32) / QK_ROPE)
    inv_row = inv_freq[lane % half][None, :]
    sign_row = jnp.where((lane % QK_ROPE) < half, -1.0, 1.0).astype(F32)[None, :]
    cos, sin = _rope_tables(pos_col, inv_row, sign_row)
    ws = small["w_sgu"]
    b_col = small["b_sgu_col"]

    def arrived(names, bufs):
        big.update(_compute_layout(dict(zip(names, bufs)), ql, kvl, heads, sw))

    a = _norm_fwd(x, small["norm_mix_g"], "norm_mix_fwd")
    z_lat = _mm(a, big["w_lat_t"], tb=True, name="z_lat")
    mixers = ["w_o_sgu", "w_o_attn"]
    z_uv, bufs_sgu = _mm(a, big["w_uv_t"], tb=True, name="z_uv", comm=_gather_first([shards["w_o_sgu"]]))
    z_g, bufs_attn = _mm(a, big["w_g_t"], tb=True, name="z_g", comm=_gather_first([shards["w_o_attn"]]))
    qn, kvn, kpe = _lat_fwd(z_lat, small["q_norm_g"], small["kv_norm_g"], cos, sin, ql, kvl)
    q_p, bufs = _mm(qn, big["w_uq"], name="q_up", comm=_gather_second(bufs_sgu + bufs_attn))
    arrived(mixers, bufs)
    kv = _mm(kvn, big["w_ukv"], out_dtype=BF16, name="kv_up")
    q_c = _q_rope(q_p, cos, sin, False, "q_rope")
    wide = ["w_out", "w_gate_ffn", "w_up_ffn"]
    attn, lse, bufs = _attn_fwd(q_c, kv, kpe, comm=_gather_first([shards[k] for k in wide]))
    s_out = _sgu_fwd(z_uv, small["sgu_norm_g"], ws, b_col)
    y_sgu = _mm(s_out, big["w_o_sgu"], name="y_sgu")
    y_attn, bufs = _mm(attn, big["w_o_attn"], name="y_attn", comm=_gather_second(bufs))
    arrived(wide[:1], bufs[:1])
    w_gate, w_up = bufs[1:]
    merged = _merge_fwd(y_attn, y_sgu, z_g, small["b_gate"])
    h1 = _mm(merged, big["w_out"], add=x, name="h1")
    f = _norm_fwd(h1, small["norm_ffn_g"], "norm_ffn_fwd")
    down = shards["w_down_ffn"]
    top = _pick(down.shape[0], down.shape[0] // 2, 2 * SUBLANES)
    gate, bufs = _mm(f, w_gate, tb=True, slab="n", name="ffn_gate", comm=_gather_first([down], rows=[(0, top)]))
    up, bufs = _mm(f, w_up, tb=True, slab="n", name="ffn_up",
                   comm=_gather_first([down], rows=[(top, down.shape[0] - top)], into=bufs))
    ffn = gate.shape[2]
    gate, up = gate.reshape(N_DEV * t, ffn), up.reshape(N_DEV * t, ffn)
    act, (w_down,) = _swiglu_fwd(gate, up, comm=_gather_second(bufs))
    act = act.reshape(N_DEV, t, ffn)
    h2 = _mm(act, w_down, slab="k", add=h1, name="h2")
    loss_row, dh2, dh2_b, d_norm_final = _loss_head(h2, small["norm_final_g"], target)

    def pair_sums(names, slabs, bufs):
        return [_pair_sum(g, b, "pair_sum_" + k) for k, g, b in zip(names, slabs, bufs)]

    parts, updates = {}, {}

    def update(k, comm=None):
        w, m, v = opt[k]
        updates[k], got = _adamw_shard(parts[k], w, m, v, "adamw_" + k, comm=comm)
        return got

    down_slabs = [_mm(act, dh2_b, ta=True, slab="m", out_dtype=BF16, name="dw_down")]
    dact, bufs = _mm(dh2_b, w_down, tb=True, slab="n", name="dact", comm=_to_sibling(down_slabs))
    down_pair = pair_sums(["w_down_ffn"], down_slabs, bufs)
    dgu = _swiglu_bwd(gate, up, dact.reshape(N_DEV * t, ffn)).reshape(2 * N_DEV, t, ffn)
    dw_gu, got = _mm(dgu, f, ta=True, slab="m", out_dtype=BF16, name="dw_gate_up", comm=_to_chips(down_pair))
    parts["w_down_ffn"] = got[0]
    gu_names = ["w_gate_ffn", "w_up_ffn"]
    df, bufs = _mm(dgu, w_gate, slab="k", name="df_gate", comm=_to_sibling([dw_gu, dw_gu], first=[0, N_DEV]))
    gu_pairs = [_pair_sum(dw_gu, b, "pair_sum_" + k, first=s0) for k, b, s0 in zip(gu_names, bufs, [0, N_DEV])]
    df = _mm(dgu, w_up, slab="k", a_slab0=N_DEV, add=df, name="df_up")
    dh1, dh1_b, d_norm_ffn = _norm_bwd(h1, small["norm_ffn_g"], df, dh2, "norm_ffn_bwd")
    dw_out = _mm(merged, dh1_b, ta=True, out_dtype=BF16, name="dw_out")
    out_slabs = [_slabs_from_rows(dw_out)]
    dmerged, bufs = _mm(dh1_b, big["w_out"], tb=True, name="dmerged", comm=_to_sibling(out_slabs))
    out_pair = pair_sums(["w_out"], out_slabs, bufs)
    dy_attn, dy_sgu, dz_g, d_b_gate = _merge_bwd(dmerged, y_attn, y_sgu, z_g, small["b_gate"])
    dw_o_sgu = _mm(s_out, dy_sgu, ta=True, out_dtype=BF16, name="dw_o_sgu")
    ds_out = _mm(dy_sgu, big["w_o_sgu"], tb=True, name="ds_out")
    dz_uv, d_ws, d_b_col, d_sgu_norm = _sgu_bwd(z_uv, ds_out, small["sgu_norm_g"], ws, b_col)
    dw_o_attn = _mm(attn, dy_attn, ta=True, out_dtype=BF16, name="dw_o_attn")
    mix_names = ["w_o_sgu", "w_o_attn"]
    mix_slabs = [_slabs_from_cols(dw_o_sgu), _slabs_from_rows(dw_o_attn)]
    dattn, bufs = _mm(dy_attn, big["w_o_attn"], tb=True, name="dattn", comm=_to_sibling(mix_slabs))
    mix_pairs = pair_sums(mix_names, mix_slabs, bufs)
    dq_c, dkv, dkpe_heads, got = _attn_bwd(q_c, kv, kpe, attn, dattn, lse, comm=_to_chips(gu_pairs))
    parts.update(zip(gu_names, got))
    dq_p = _q_rope(dq_c, cos, sin, True, "q_rope_bwd")
    dw_uq = _mm(qn, dq_p, ta=True, out_dtype=BF16, name="dw_uq")
    dw_ukv = _mm(kvn, dkv, ta=True, out_dtype=BF16, name="dw_ukv")
    dqn = _mm(dq_p, big["w_uq"], tb=True, name="dqn")
    dkvn = _mm(dkv, big["w_ukv"], tb=True, name="dkvn")
    dz_lat, d_q_norm, d_kv_norm = _lat_bwd(z_lat, small["q_norm_g"], small["kv_norm_g"], dqn, dkvn, dkpe_heads,
                                           cos, sin, ql, kvl)
    dw_g, got = _mm(dz_g, a, ta=True, out_dtype=BF16, name="dw_g", comm=_to_chips(out_pair))
    parts["w_out"] = got[0]
    dw_uv, got = _mm(dz_uv, a, ta=True, out_dtype=BF16, name="dw_uv", comm=_to_chips(mix_pairs[1:]))
    parts["w_o_attn"] = got[0]
    dw_lat, got = _mm(dz_lat, a, ta=True, out_dtype=BF16, name="dw_lat", comm=_to_chips(mix_pairs[:1]))
    parts["w_o_sgu"] = got[0]
    lat = ql + kvl + QK_ROPE
    dw_uq_cols = dw_uq.reshape(ql, heads, HEAD_PAD)[:, :, :QK_NOPE + QK_ROPE].reshape(ql, heads * (QK_NOPE + QK_ROPE))
    in_names = ["w_uq", "w_ukv", "w_in"]
    in_slabs = [_slabs_from_cols(dw_uq_cols), _slabs_from_cols(dw_ukv),
                _slabs_from_rows(jnp.concatenate([dw_lat[:lat], dw_uv, dw_g], axis=0))]
    da = _mm(dz_lat, big["w_lat_t"], name="da_lat")
    da, bufs = _mm(dz_uv, big["w_uv_t"], add=da, name="da_uv", comm=_to_sibling(in_slabs))
    uq_pair, ukv_pair, in_pair = pair_sums(in_names, in_slabs, bufs)
    cols = in_pair.shape[2]
    chunk = _pick(cols, cols // TAIL_CHUNKS)
    chunks = [("c", c0, chunk) for c0 in range(0, cols, chunk)]
    da, got = _mm(dz_g, big["w_g_t"], add=da, name="da_g",
                  comm=_to_chips([uq_pair, in_pair], rows=[None, chunks[0]]))
    parts["w_uq"], in_parts = got
    grad_x, _, d_norm_mix, got = _norm_bwd(x, small["norm_mix_g"], da, dh1, "norm_mix_bwd", comm=_to_chips([ukv_pair]))
    parts["w_ukv"] = got[0]
    hosts = ["w_gate_ffn", "w_up_ffn", "w_down_ffn", "w_out", "w_o_attn", "w_o_sgu", "w_uq", "w_ukv"]
    assert len(chunks) <= 1 + len(hosts)
    for i, k in enumerate(hosts):
        if 1 + i < len(chunks):
            in_parts = update(k, comm=_to_chips([in_pair], rows=[chunks[1 + i]], into=[in_parts]))[0]
        else:
            update(k)
    parts["w_in"] = in_parts
    update("w_in")

    gs = {"norm_mix_g": d_norm_mix, "b_gate": d_b_gate, "q_norm_g": d_q_norm, "kv_norm_g": d_kv_norm,
          "sgu_norm_g": d_sgu_norm, "w_sgu": d_ws, "b_sgu_col": d_b_col, "norm_ffn_g": d_norm_ffn,
          "norm_final_g": d_norm_final}
    return loss_row, grad_x, gs, updates


def _my_place():
    return lax.axis_index("x"), lax.axis_index("y"), lax.axis_index("c")


def _all_gather(shards):
    n = len(shards)

    def body(*refs):
        ins, outs = refs[:n], refs[n:2 * n]
        send_sems, recv_sems, local_sems = refs[2 * n:]
        x, y, c = _my_place()
        me, sibling = (x, y, c), (x, y, 1 - c)
        chips = [(1 - x, y), (x, 1 - y), (1 - x, 1 - y)]

        def slab(w, place):
            return outs[w].at[4 * place[0] + 2 * place[1] + place[2]]

        def copy(w, k, place, to, src=None):
            return pltpu.make_async_remote_copy(
                src_ref=slab(w, place) if src is None else src, dst_ref=slab(w, place),
                send_sem=send_sems.at[w, k], recv_sem=recv_sems.at[w, k], device_id=to, device_id_type=MESH)

        mine = [pltpu.make_async_copy(ins[w], slab(w, me), local_sems.at[w]) for w in range(n)]
        for cp in mine:
            cp.start()
        started = []
        for w in range(n):
            first = [copy(w, 0, me, sibling, src=ins[w])]
            first += [copy(w, 1 + j, me, (*chip, c), src=ins[w]) for j, chip in enumerate(chips)]
            for cp in first:
                cp.start()
            started += first
        for w in range(n):
            for j, chip in enumerate(chips):
                copy(w, 1 + j, (*chip, c), me).wait_recv()
                fwd = copy(w, 4 + j, (*chip, c), sibling)
                fwd.start()
                started.append(fwd)
        for w in range(n):
            copy(w, 0, sibling, me).wait_recv()
            for j, chip in enumerate(chips):
                copy(w, 4 + j, (*chip, 1 - c), me).wait_recv()
        for cp in started:
            cp.wait_send()
        for cp in mine:
            cp.wait()

    any_spec = pl.BlockSpec(memory_space=pl.ANY)
    return pl.pallas_call(
        body, in_specs=[any_spec] * n, out_specs=[any_spec] * n,
        out_shape=[jax.ShapeDtypeStruct((N_DEV,) + s.shape, s.dtype) for s in shards],
        scratch_shapes=[pltpu.SemaphoreType.DMA((n, 7)), pltpu.SemaphoreType.DMA((n, 7)), pltpu.SemaphoreType.DMA((n,))],
        compiler_params=pltpu.CompilerParams(has_side_effects=True), name="all_gather_weights")(*shards)


N_CHIPS = N_DEV // 2


def _gather_first(shards, rows=None, into=None):
    n = len(shards)
    rows = rows or [(0, s.shape[0]) for s in shards]

    def copies(ins, outs, sems):
        x, y, c = _my_place()
        send_sems, recv_sems, local_sems = sems
        me = 4 * x + 2 * y + c
        targets = [(x, y, 1 - c), (1 - x, y, c), (x, 1 - y, c), (1 - x, 1 - y, c)]
        out = []
        for w in range(n):
            r0, nr = rows[w]
            src, dst = ins[w].at[pl.ds(r0, nr)], outs[w].at[me, pl.ds(r0, nr)]
            out.append(pltpu.make_async_copy(src, dst, local_sems.at[w]))
            out += [pltpu.make_async_remote_copy(src_ref=src, dst_ref=dst, send_sem=send_sems.at[w, k],
                                                 recv_sem=recv_sems.at[w, k], device_id=to, device_id_type=MESH)
                    for k, to in enumerate(targets)]
        return out

    def start(ins, outs, sems):
        for cp in copies(ins, outs, sems):
            cp.start()

    def finish(ins, outs, sems):
        for cp in copies(ins, outs, sems):
            cp.wait()

    return _Comm(list(shards) + list(into or []), [jax.ShapeDtypeStruct((N_DEV,) + s.shape, s.dtype) for s in shards],
                 [pltpu.SemaphoreType.DMA((n, 4)), pltpu.SemaphoreType.DMA((n, 4)), pltpu.SemaphoreType.DMA((n,))],
                 start, finish, aliases={n + w: w for w in range(n)} if into else None)


def _gather_second(bufs):
    n = len(bufs)

    def copies(ins, outs, sems):
        x, y, c = _my_place()
        send_sems, recv_sems = sems
        out = []
        for w in range(n):
            for j, (cx, cy) in enumerate([(1 - x, y), (x, 1 - y), (1 - x, 1 - y)]):
                slab = 4 * cx + 2 * cy + c
                out.append(pltpu.make_async_remote_copy(
                    src_ref=ins[w].at[slab], dst_ref=outs[w].at[slab], send_sem=send_sems.at[w, j],
                    recv_sem=recv_sems.at[w, j], device_id=(x, y, 1 - c), device_id_type=MESH))
        return out

    def start(ins, outs, sems):
        for cp in copies(ins, outs, sems):
            cp.start()

    def finish(ins, outs, sems):
        for cp in copies(ins, outs, sems):
            cp.wait()

    return _Comm(bufs, [jax.ShapeDtypeStruct(b.shape, b.dtype) for b in bufs],
                 [pltpu.SemaphoreType.DMA((n, 3)), pltpu.SemaphoreType.DMA((n, 3))], start, finish,
                 aliases={w: w for w in range(n)})


def _to_sibling(grads, first=None):
    n = len(grads)
    first = first or [0] * n

    def copies(ins, outs, sems):
        x, y, c = _my_place()
        send_sems, recv_sems = sems
        return [pltpu.make_async_remote_copy(
            src_ref=ins[w].at[first[w] + 2 * i + (1 - c)], dst_ref=outs[w].at[i], send_sem=send_sems.at[w, i],
            recv_sem=recv_sems.at[w, i], device_id=(x, y, 1 - c), device_id_type=MESH)
            for w in range(n) for i in range(N_CHIPS)]

    def start(ins, outs, sems):
        for cp in copies(ins, outs, sems):
            cp.start()

    def finish(ins, outs, sems):
        for cp in copies(ins, outs, sems):
            cp.wait()

    return _Comm(grads, [jax.ShapeDtypeStruct((N_CHIPS,) + g.shape[1:], g.dtype) for g in grads],
                 [pltpu.SemaphoreType.DMA((n, N_CHIPS)), pltpu.SemaphoreType.DMA((n, N_CHIPS))], start, finish)


def _window(ref, slab, win):
    if win is None:
        return ref.at[slab]
    if win[0] == "r":
        return ref.at[slab, pl.ds(win[1], win[2])]
    return ref.at[slab, slice(None), pl.ds(win[1], win[2])]


def _to_chips(parts, rows=None, into=None):
    n = len(parts)
    rows = rows or [None] * n

    def copies(ins, outs, sems):
        x, y, c = _my_place()
        send_sems, recv_sems, local_sems = sems
        mine = 2 * x + y
        chips = [(1 - x, y), (x, 1 - y), (1 - x, 1 - y)]
        remote = [pltpu.make_async_remote_copy(
            src_ref=_window(ins[w], 2 * cx + cy, rows[w]), dst_ref=_window(outs[w], mine, rows[w]),
            send_sem=send_sems.at[w, j], recv_sem=recv_sems.at[w, j], device_id=(cx, cy, c), device_id_type=MESH)
            for w in range(n) for j, (cx, cy) in enumerate(chips)]
        local = [pltpu.make_async_copy(_window(ins[w], mine, rows[w]), _window(outs[w], mine, rows[w]),
                                       local_sems.at[w]) for w in range(n)]
        return remote + local

    def start(ins, outs, sems):
        for cp in copies(ins, outs, sems):
            cp.start()

    def finish(ins, outs, sems):
        for cp in copies(ins, outs, sems):
            cp.wait()

    return _Comm(list(parts) + list(into or []), [jax.ShapeDtypeStruct(p.shape, p.dtype) for p in parts],
                 [pltpu.SemaphoreType.DMA((n, N_CHIPS - 1)), pltpu.SemaphoreType.DMA((n, N_CHIPS - 1)),
                  pltpu.SemaphoreType.DMA((n,))], start, finish,
                 aliases={n + w: w for w in range(n)} if into else None)


def _pair_sum(g, buf, name, first=0):
    _, r, c = g.shape
    tr, tc = _shard_tile(r, c, 4 * SHARD_TILE_ELEMS, 1024)
    core = (lax.axis_index("c") + first).astype(jnp.int32).reshape(1)

    def body(core_ref, g_ref, b_ref, o_ref):
        o_ref[...] = (g_ref[...].astype(F32) + b_ref[...].astype(F32)).astype(o_ref.dtype)

    blk = (1, tr, tc)
    return pl.pallas_call(
        body, grid_spec=pltpu.PrefetchScalarGridSpec(
            num_scalar_prefetch=1, grid=(N_CHIPS, r // tr, c // tc),
            in_specs=[pl.BlockSpec(blk, lambda i, j, l, core_ref: (2 * i + core_ref[0], j, l)),
                      pl.BlockSpec(blk, lambda i, j, l, core_ref: (i, j, l))],
            out_specs=pl.BlockSpec(blk, lambda i, j, l, core_ref: (i, j, l))),
        out_shape=jax.ShapeDtypeStruct(buf.shape, buf.dtype),
        compiler_params=_params(("parallel", "parallel", "parallel")), name=name)(core, g, buf)


def _all_reduce_pack(pack):
    r = pack.shape[0]

    def body(x_ref, out_ref, gath_ref, send_sems, recv_sems, local_sem):
        x, y, c = _my_place()
        me, sibling = (x, y, c), (x, y, 1 - c)
        chips = [(1 - x, y), (x, 1 - y), (1 - x, 1 - y)]

        def slab(place):
            return gath_ref.at[4 * place[0] + 2 * place[1] + place[2]]

        def copy(k, place, to, src=None):
            return pltpu.make_async_remote_copy(
                src_ref=slab(place) if src is None else src, dst_ref=slab(place),
                send_sem=send_sems.at[k], recv_sem=recv_sems.at[k], device_id=to, device_id_type=MESH)

        mine = pltpu.make_async_copy(x_ref, slab(me), local_sem)
        mine.start()
        first = [copy(0, me, sibling, src=x_ref)]
        first += [copy(1 + j, me, (*chip, c), src=x_ref) for j, chip in enumerate(chips)]
        for cp in first:
            cp.start()
        passed = [copy(4 + j, (*chip, c), sibling) for j, chip in enumerate(chips)]
        for j, chip in enumerate(chips):
            copy(1 + j, (*chip, c), me).wait_recv()
            passed[j].start()
        copy(0, sibling, me).wait_recv()
        for j, chip in enumerate(chips):
            copy(4 + j, (*chip, 1 - c), me).wait_recv()
        for cp in first + passed:
            cp.wait_send()
        mine.wait()
        acc = gath_ref[0]
        for i in range(1, N_DEV):
            acc = acc + gath_ref[i]
        out_ref[...] = acc

    vmem = pl.BlockSpec(memory_space=pltpu.VMEM)
    return pl.pallas_call(
        body, in_specs=[vmem], out_specs=vmem, out_shape=jax.ShapeDtypeStruct(pack.shape, F32),
        scratch_shapes=[pltpu.VMEM((N_DEV, r, LANES), F32), pltpu.SemaphoreType.DMA((7,)),
                        pltpu.SemaphoreType.DMA((7,)), pltpu.SemaphoreType.DMA],
        compiler_params=pltpu.CompilerParams(vmem_limit_bytes=VMEM_LIMIT), name="all_reduce_small")(pack)


def _adamw_math(w, g, m, v):
    m = ADAM_B1 * m + (1.0 - ADAM_B1) * g
    v = ADAM_B2 * v + (1.0 - ADAM_B2) * (g * g)
    m_hat = m / (1.0 - ADAM_B1 ** ADAM_STEP)
    v_hat = v / (1.0 - ADAM_B2 ** ADAM_STEP)
    delta = -ADAM_LR * (m_hat / (jnp.sqrt(v_hat) + ADAM_EPS) + ADAM_WD * w)
    return delta, m, v


def _adamw_shard(parts, w, m, v, name, comm=None):
    r, c = w.shape
    n_parts = parts.shape[0]
    tr, tc = _shard_tile(r, c)

    def body(p_ref, w_ref, m_ref, v_ref, g_ref, d_ref, nm_ref, nv_ref):
        g = p_ref[0].astype(F32)
        for i in range(1, n_parts):
            g = g + p_ref[i].astype(F32)
        g_ref[...] = g
        d_ref[...], nm_ref[...], nv_ref[...] = _adamw_math(w_ref[...], g, m_ref[...], v_ref[...])

    spec = pl.BlockSpec((tr, tc), lambda i, j: (i, j))
    outs, comm_outs = _call(
        body, grid=(r // tr, c // tc),
        in_specs=[pl.BlockSpec((n_parts, tr, tc), lambda i, j: (0, i, j)), spec, spec, spec],
        out_specs=[spec] * 4, out_shape=[jax.ShapeDtypeStruct((r, c), F32)] * 4,
        sem=("parallel", "parallel"), name=name, args=(parts, w, m, v), comm=comm)
    return outs, comm_outs


def _adamw_pack(g, w, m, v):
    r, c = w.shape

    def body(g_ref, w_ref, m_ref, v_ref, d_ref, nm_ref, nv_ref):
        d_ref[...], nm_ref[...], nv_ref[...] = _adamw_math(w_ref[...], g_ref[...], m_ref[...], v_ref[...])

    return pl.pallas_call(
        body, in_specs=[_full((r, c))] * 4, out_specs=[_full((r, c))] * 3, grid=(1,),
        out_shape=[jax.ShapeDtypeStruct((r, c), F32)] * 3,
        compiler_params=_params(("arbitrary",)), name="adamw_small")(g, w, m, v)


def _cols_from_slabs(g):
    return jnp.transpose(g, (1, 0, 2)).reshape(g.shape[1], N_DEV * g.shape[2])


def _slabs_from_cols(w):
    r, c8 = w.shape
    return jnp.transpose(w.reshape(r, N_DEV, c8 // N_DEV), (1, 0, 2))


def _rows_from_slabs(g):
    return g.reshape(N_DEV * g.shape[1], g.shape[2])


def _slabs_from_rows(w):
    return w.reshape(N_DEV, w.shape[0] // N_DEV, w.shape[1])


def _compute_layout(gathered, ql, kvl, heads, sw):
    out = {}
    for k, g in gathered.items():
        if k == "w_in":
            lat = ql + kvl + QK_ROPE
            w_in_t = _rows_from_slabs(g)
            out["w_lat_t"] = jnp.pad(w_in_t[:lat], ((0, LANES - QK_ROPE), (0, 0)))
            out["w_uv_t"] = w_in_t[lat:lat + 2 * sw]
            out["w_g_t"] = w_in_t[lat + 2 * sw:]
        elif k == "w_uq":
            per_head = _cols_from_slabs(g).reshape(ql, heads, QK_NOPE + QK_ROPE)
            pad = HEAD_PAD - QK_NOPE - QK_ROPE
            out["w_uq"] = jnp.pad(per_head, ((0, 0), (0, 0), (0, pad))).reshape(ql, heads * HEAD_PAD)
        elif k in ("w_o_attn", "w_out", "w_down_ffn"):
            out[k.removesuffix("_ffn")] = _rows_from_slabs(g)
        else:
            out[k.removesuffix("_ffn")] = _cols_from_slabs(g)
    return out


_SMALL =["norm_mix_g", "b_gate", "q_norm_g", "kv_norm_g", "sgu_norm_g", "w_sgu", "b_sgu", "norm_ffn_g", "norm_final_g"]
_BIG = ["w_in", "w_uq", "w_ukv", "w_o_attn", "w_o_sgu", "w_out", "w_gate_ffn", "w_up_ffn", "w_down_ffn"]
_TRANSPOSED = ("w_in", "w_gate_ffn", "w_up_ffn")
_ORDER = ["norm_mix_g", "w_in", "b_gate", "q_norm_g", "w_uq", "kv_norm_g", "w_ukv", "w_o_attn", "sgu_norm_g", "w_sgu",
          "b_sgu", "w_o_sgu", "w_out", "norm_ffn_g", "w_gate_ffn", "w_up_ffn", "w_down_ffn", "norm_final_g"]


def _pack_rows(parts):
    rows, sizes = [], []
    for p in parts:
        flat = p.reshape(-1)
        n = flat.shape[0]
        padded = -(-n // (SUBLANES * LANES)) * (SUBLANES * LANES)
        rows.append(jnp.pad(flat, (0, padded - n)).reshape(padded // LANES, LANES))
        sizes.append((n, padded // LANES))
    return jnp.concatenate(rows, axis=0), sizes


def _unpack_rows(pack, sizes, shapes):
    out, r0 = [], 0
    for (n, nr), shp in zip(sizes, shapes):
        out.append(pack[r0:r0 + nr].reshape(-1)[:n].reshape(shp))
        r0 += nr
    return out


def kernel(x, positions, norm_mix_g, w_in, b_gate, q_norm_g, w_uq, kv_norm_g, w_ukv, w_o_attn, sgu_norm_g, w_sgu, b_sgu, w_o_sgu, w_out, norm_ffn_g, w_gate_ffn, w_up_ffn, w_down_ffn, norm_final_g, loss_target, m_norm_mix_g, m_w_in, m_b_gate, m_q_norm_g, m_w_uq, m_kv_norm_g, m_w_ukv, m_w_o_attn, m_sgu_norm_g, m_w_sgu, m_b_sgu, m_w_o_sgu, m_w_out, m_norm_ffn_g, m_w_gate_ffn, m_w_up_ffn, m_w_down_ffn, m_norm_final_g, v_norm_mix_g, v_w_in, v_b_gate, v_q_norm_g, v_w_uq, v_kv_norm_g, v_w_ukv, v_w_o_attn, v_sgu_norm_g, v_w_sgu, v_b_sgu, v_w_o_sgu, v_w_out, v_norm_ffn_g, v_w_gate_ffn, v_w_up_ffn, v_w_down_ffn, v_norm_final_g):
    wts = dict(norm_mix_g=norm_mix_g, w_in=w_in, b_gate=b_gate, q_norm_g=q_norm_g, w_uq=w_uq, kv_norm_g=kv_norm_g,
               w_ukv=w_ukv, w_o_attn=w_o_attn, sgu_norm_g=sgu_norm_g, w_sgu=w_sgu, b_sgu=b_sgu, w_o_sgu=w_o_sgu,
               w_out=w_out, norm_ffn_g=norm_ffn_g, w_gate_ffn=w_gate_ffn, w_up_ffn=w_up_ffn, w_down_ffn=w_down_ffn,
               norm_final_g=norm_final_g)
    mom = dict(norm_mix_g=m_norm_mix_g, w_in=m_w_in, b_gate=m_b_gate, q_norm_g=m_q_norm_g, w_uq=m_w_uq,
               kv_norm_g=m_kv_norm_g, w_ukv=m_w_ukv, w_o_attn=m_w_o_attn, sgu_norm_g=m_sgu_norm_g, w_sgu=m_w_sgu,
               b_sgu=m_b_sgu, w_o_sgu=m_w_o_sgu, w_out=m_w_out, norm_ffn_g=m_norm_ffn_g, w_gate_ffn=m_w_gate_ffn,
               w_up_ffn=m_w_up_ffn, w_down_ffn=m_w_down_ffn, norm_final_g=m_norm_final_g)
    var = dict(norm_mix_g=v_norm_mix_g, w_in=v_w_in, b_gate=v_b_gate, q_norm_g=v_q_norm_g, w_uq=v_w_uq,
               kv_norm_g=v_kv_norm_g, w_ukv=v_w_ukv, w_o_attn=v_w_o_attn, sgu_norm_g=v_sgu_norm_g, w_sgu=v_w_sgu,
               b_sgu=v_b_sgu, w_o_sgu=v_w_o_sgu, w_out=v_w_out, norm_ffn_g=v_norm_ffn_g, w_gate_ffn=v_w_gate_ffn,
               w_up_ffn=v_w_up_ffn, w_down_ffn=v_w_down_ffn, norm_final_g=v_norm_final_g)

    t, d = x.shape[1], x.shape[2]
    ql, kvl = q_norm_g.shape[1], kv_norm_g.shape[1]
    heads = (w_uq.shape[2] * N_DEV) // (QK_NOPE + QK_ROPE)
    sw = sgu_norm_g.shape[1]

    def shard(a, k):
        return a[0].T if k in _TRANSPOSED else a[0]

    def unshard(a, k):
        return (a.T if k in _TRANSPOSED else a).reshape(wts[k].shape)

    opt = {k: (shard(wts[k], k), shard(mom[k], k), shard(var[k], k)) for k in _BIG}
    shards = {k: opt[k][0].astype(BF16) for k in _BIG}
    small = {
        "norm_mix_g": norm_mix_g, "b_gate": b_gate, "q_norm_g": q_norm_g, "kv_norm_g": kv_norm_g,
        "sgu_norm_g": sgu_norm_g, "w_sgu": w_sgu[0], "b_sgu_col": b_sgu[0][:, :, None], "norm_ffn_g": norm_ffn_g,
        "norm_final_g": norm_final_g[None, :],
    }

    loss_row, grad_x, gs, updates = _local_step(x[0], positions.reshape(t, 1), loss_target[0], small, shards, opt)
    grads, deltas, new_m, new_v = {}, {}, {}, {}
    for k in _BIG:
        grads[k], deltas[k], new_m[k], new_v[k] = (unshard(a, k) for a in updates[k])

    small_grads = [gs["norm_mix_g"], gs["b_gate"], gs["q_norm_g"], gs["kv_norm_g"], gs["sgu_norm_g"], gs["w_sgu"],
                   gs["b_sgu_col"], gs["norm_ffn_g"], gs["norm_final_g"]]
    pack, sizes = _pack_rows([loss_row] + small_grads)
    total = _all_reduce_pack(pack)
    shapes = [(1, LANES)] + [wts[k].shape for k in _SMALL]
    unpacked = _unpack_rows(total, sizes, shapes)
    loss = unpacked[0][0, 0]
    for k, g in zip(_SMALL, unpacked[1:]):
        grads[k] = g
    g_pack = total[sizes[0][1]:]
    w_pack, _ = _pack_rows([wts[k] for k in _SMALL])
    m_pack, _ = _pack_rows([mom[k] for k in _SMALL])
    v_pack, _ = _pack_rows([var[k] for k in _SMALL])
    d_pack, nm_pack, nv_pack = _adamw_pack(g_pack, w_pack, m_pack, v_pack)
    small_shapes = [wts[k].shape for k in _SMALL]
    for store, pk in ((deltas, d_pack), (new_m, nm_pack), (new_v, nv_pack)):
        for k, a in zip(_SMALL, _unpack_rows(pk, sizes[1:], small_shapes)):
            store[k] = a

    return (loss, grad_x[None], *[grads[k] for k in _ORDER], *[deltas[k] for k in _ORDER],
            *[new_m[k] for k in _ORDER], *[new_v[k] for k in _ORDER])
```

```python
import functools
import math

import jax
import jax.numpy as jnp
from jax import lax
from jax.experimental import pallas as pl
from jax.experimental.pallas import tpu as pltpu

F32 = jnp.float32
BF16 = jnp.bfloat16

N_DEV = 8
N_HEADS = 16
QK_NOPE = 128
QK_ROPE = 64
V_HEAD = 128
HEAD_PAD = 256
ROPE_THETA = 10000.0
CHUNK = 128
SGU_GROUP = 128
RMS_EPS = 1e-6
LANES = 128
SUBLANES = 8

ADAM_LR = 0.001
ADAM_B1 = 0.9
ADAM_B2 = 0.999
ADAM_EPS = 1e-08
ADAM_WD = 0.01
ADAM_STEP = 10

VMEM_LIMIT = 48 * 1024 * 1024
MM_TILE = (2048, 512, 2048)
MM_TILE_TA = (512, 2048)
ATTN_TILE = 512
ROW_KERNEL_BYTES = 24 * 1024 * 1024
SHARD_TILE_ELEMS = 256 * 1024
SLABS_PER_STEP = 2
TAIL_CHUNKS = 4
NEG_BIG = -1e30
MESH = pl.DeviceIdType.MESH


def _pick(n, target, mult=LANES):
    best = None
    d = mult
    while d <= min(n, target):
        if n % d == 0:
            best = d
        d += mult
    return best or n


def _row_tile(t, width, n_blocks, mult=2 * SUBLANES):
    return _pick(t, max(mult, ROW_KERNEL_BYTES // (3 * n_blocks * width * 4)), mult)


def _shard_tile(r, c, elems=SHARD_TILE_ELEMS, max_rows=256):
    tr = _pick(r, max_rows, 2 * SUBLANES)
    return tr, _pick(c, max(LANES, elems // tr))


def _params(sem):
    return pltpu.CompilerParams(dimension_semantics=sem, vmem_limit_bytes=VMEM_LIMIT)


def _full(shape):
    nd = len(shape)
    return pl.BlockSpec(shape, lambda *_: (0,) * nd)


def _rows(tr, w, cb=0):
    return pl.BlockSpec((tr, w), lambda i: (i, cb))


class _Comm:
    def __init__(self, ins, out_shapes, sems, start, finish, aliases=None):
        self.ins, self.out_shapes, self.sems, self.start, self.finish = list(ins), list(out_shapes), list(sems), start, finish
        self.aliases = dict(aliases or {})


def _call(body, *, grid, in_specs, out_specs, out_shape, scratch_shapes=(), sem, name, args, comm=None):
    if comm is None:
        outs = pl.pallas_call(body, grid=grid, in_specs=list(in_specs), out_specs=list(out_specs),
                              out_shape=list(out_shape), scratch_shapes=list(scratch_shapes),
                              compiler_params=_params(sem), name=name)(*args)
        return list(outs), []
    n_in, n_out, n_sc = len(in_specs), len(out_shape), len(scratch_shapes)
    nci, nco = len(comm.ins), len(comm.out_shapes)

    def hosted(*refs):
        ins, refs = refs[:n_in], refs[n_in:]
        cins, refs = refs[:nci], refs[nci:]
        outs, refs = refs[:n_out], refs[n_out:]
        couts, refs = refs[:nco], refs[nco:]
        scratch, csems = refs[:n_sc], refs[n_sc:]
        ids = [pl.program_id(i) for i in range(len(grid))]
        first = functools.reduce(jnp.logical_and, [i == 0 for i in ids])
        last = functools.reduce(jnp.logical_and, [i == g - 1 for i, g in zip(ids, grid)])

        @pl.when(first)
        def _():
            comm.start(cins, couts, csems)

        body(*ins, *outs, *scratch)

        @pl.when(last)
        def _():
            comm.finish(cins, couts, csems)

    any_spec = pl.BlockSpec(memory_space=pl.ANY)
    res = pl.pallas_call(
        hosted, grid=grid, in_specs=list(in_specs) + [any_spec] * nci, out_specs=list(out_specs) + [any_spec] * nco,
        out_shape=list(out_shape) + comm.out_shapes, scratch_shapes=list(scratch_shapes) + comm.sems,
        input_output_aliases={n_in + i: n_out + o for i, o in comm.aliases.items()},
        compiler_params=pltpu.CompilerParams(dimension_semantics=("arbitrary",) * len(grid),
                                             vmem_limit_bytes=VMEM_LIMIT, has_side_effects=True),
        name=name)(*args, *comm.ins)
    return list(res[:n_out]), list(res[n_out:])


def _mm(a, b, *, ta=False, tb=False, add=None, out_dtype=F32, tm=None, tn=None, tk=None, name, comm=None,
        slab=None, a_slab0=0):
    sq = None
    if ta:
        tm, tn = tm or MM_TILE_TA[0], tn or MM_TILE_TA[1]
    if slab is None:
        m, k = (a.shape[1], a.shape[0]) if ta else a.shape
        n = b.shape[0] if tb else b.shape[1]
        assert k == (b.shape[1] if tb else b.shape[0]), (a.shape, b.shape, ta, tb)
        tm, tn, tk = _pick(m, tm or MM_TILE[0]), _pick(n, tn or MM_TILE[1]), _pick(k, tk or MM_TILE[2])
        grid = (m // tm, n // tn, k // tk)
        a_spec = pl.BlockSpec((tk, tm), lambda i, j, kk: (kk, i)) if ta else pl.BlockSpec((tm, tk), lambda i, j, kk: (i, kk))
        b_spec = pl.BlockSpec((tn, tk), lambda i, j, kk: (j, kk)) if tb else pl.BlockSpec((tk, tn), lambda i, j, kk: (kk, j))
        o_spec, o_shape = pl.BlockSpec((tm, tn), lambda i, j, kk: (i, j)), (m, n)
    elif slab == "n":
        m, k = (a.shape[1], a.shape[0]) if ta else a.shape
        s, c = b.shape[0], (b.shape[1] if tb else b.shape[2])
        assert k == (b.shape[2] if tb else b.shape[1]), (a.shape, b.shape, ta, tb)
        tm, tn, tk = _pick(m, tm or MM_TILE[0]), c, _pick(k, tk or MM_TILE[2])
        grid = (m // tm, s, k // tk)
        a_spec = pl.BlockSpec((tk, tm), lambda i, j, kk: (kk, i)) if ta else pl.BlockSpec((tm, tk), lambda i, j, kk: (i, kk))
        b_spec = (pl.BlockSpec((sq, c, tk), lambda i, j, kk: (j, 0, kk)) if tb
                  else pl.BlockSpec((sq, tk, c), lambda i, j, kk: (j, kk, 0)))
        o_spec, o_shape = pl.BlockSpec((sq, tm, c), lambda i, j, kk: (j, i, 0)), (s, m, c)
    elif slab == "m":
        assert ta and not tb
        s, k, c = a.shape
        n = b.shape[1]
        assert k == b.shape[0], (a.shape, b.shape)
        tm, tn, tk = c, _pick(n, tn or MM_TILE[1]), _pick(k, tk or MM_TILE[2])
        grid = (s, n // tn, k // tk)
        a_spec = pl.BlockSpec((sq, tk, c), lambda i, j, kk: (i, kk, 0))
        b_spec = pl.BlockSpec((tk, tn), lambda i, j, kk: (kk, j))
        o_spec, o_shape = pl.BlockSpec((sq, c, tn), lambda i, j, kk: (i, 0, j)), (s, c, n)
    else:
        assert slab == "k" and not ta
        s, c = b.shape[0], (b.shape[2] if tb else b.shape[1])
        m, n = a.shape[1], (b.shape[1] if tb else b.shape[2])
        assert a.shape[2] == c and a.shape[0] >= a_slab0 + s, (a.shape, b.shape, a_slab0)
        tm, tn, tk = _pick(m, tm or MM_TILE[0]), _pick(n, tn or MM_TILE[1]), c
        per_step = SLABS_PER_STEP if (s % SLABS_PER_STEP == 0 and a_slab0 % SLABS_PER_STEP == 0) else 1
        first = a_slab0 // per_step
        grid = (m // tm, n // tn, s // per_step)
        a_spec = pl.BlockSpec((per_step, tm, c), lambda i, j, kk: (kk + first, i, 0))
        b_spec = (pl.BlockSpec((per_step, tn, c), lambda i, j, kk: (kk, j, 0)) if tb
                  else pl.BlockSpec((per_step, c, tn), lambda i, j, kk: (kk, 0, j)))
        o_spec, o_shape = pl.BlockSpec((tm, tn), lambda i, j, kk: (i, j)), (m, n)
    nk = grid[2]
    dims = (((0 if ta else 1,), (1 if tb else 0,)), ((), ()))

    def product(a_ref, b_ref):
        if slab != "k":
            return lax.dot_general(a_ref[...].astype(BF16), b_ref[...].astype(BF16), dims, preferred_element_type=F32)
        r = None
        for u in range(a_ref.shape[0]):
            p = lax.dot_general(a_ref[u].astype(BF16), b_ref[u].astype(BF16), dims, preferred_element_type=F32)
            r = p if r is None else r + p
        return r

    def body(*refs):
        a_ref, b_ref = refs[:2]
        add_ref = refs[2] if add is not None else None
        o_ref = refs[3] if add is not None else refs[2]
        acc_ref = refs[-1] if nk > 1 else None

        def finish(r):
            if add_ref is not None:
                r = r + add_ref[...].astype(F32)
            o_ref[...] = r.astype(o_ref.dtype)

        if nk == 1:
            finish(product(a_ref, b_ref))
            return
        kk = pl.program_id(2)

        @pl.when(kk == 0)
        def _():
            acc_ref[...] = product(a_ref, b_ref)

        if nk > 2:
            @pl.when(jnp.logical_and(kk > 0, kk < nk - 1))
            def _():
                acc_ref[...] += product(a_ref, b_ref)

        @pl.when(kk == nk - 1)
        def _():
            finish(acc_ref[...] + product(a_ref, b_ref))

    in_specs = [a_spec, b_spec] + ([o_spec] if add is not None else [])
    args = (a, b) + ((add,) if add is not None else ())
    outs, comm_outs = _call(
        body, grid=grid, in_specs=in_specs, out_specs=[o_spec],
        out_shape=[jax.ShapeDtypeStruct(o_shape, out_dtype)],
        scratch_shapes=[pltpu.VMEM((tm, tn), F32)] if nk > 1 else [],
        sem=("parallel", "parallel", "arbitrary"), name=name, args=args, comm=comm)
    return outs[0] if comm is None else (outs[0], comm_outs)


def _rms_scale(x):
    return lax.rsqrt(jnp.mean(x * x, axis=-1, keepdims=True) + RMS_EPS)


def _rms_bwd(xhat, r, g, dy):
    t = dy * g
    dx = r * (t - xhat * jnp.mean(t * xhat, axis=-1, keepdims=True))
    return dx, dy * xhat


_GELU_C = math.sqrt(2.0 / math.pi)


def _gelu(x):
    return x * (0.5 * (1.0 + jnp.tanh(_GELU_C * (x + 0.044715 * (x * x * x)))))


def _gelu_and_grad(x):
    t = jnp.tanh(_GELU_C * (x + 0.044715 * (x * x * x)))
    cdf = 0.5 * (1.0 + t)
    return x * cdf, cdf + x * (0.5 * (1.0 - t * t) * (_GELU_C * (1.0 + 3.0 * 0.044715 * (x * x))))


def _sigmoid(x):
    return 1.0 / (1.0 + jnp.exp(-x))


def _swap_halves(x):
    lane = lax.broadcasted_iota(jnp.int32, x.shape, 1)
    first = (lane % QK_ROPE) < (QK_ROPE // 2)
    return jnp.where(first, pltpu.roll(x, LANES - QK_ROPE // 2, 1), pltpu.roll(x, QK_ROPE // 2, 1))


def _rope(x, cos, sin_signed):
    return x * cos + _swap_halves(x) * sin_signed


def _rope_bwd(d, cos, sin_signed):
    return d * cos + _swap_halves(d * sin_signed)


def _rope_tables(pos_col, inv_freq_row, sign_row):
    t = pos_col.shape[0]
    tr = _pick(t, 512, SUBLANES)

    def body(p_ref, f_ref, s_ref, cos_ref, sin_ref):
        ang = p_ref[...].astype(F32) * f_ref[...]
        cos_ref[...] = jnp.cos(ang)
        sin_ref[...] = jnp.sin(ang) * s_ref[...]

    return pl.pallas_call(
        body, grid=(t // tr,), in_specs=[_rows(tr, 1), _full((1, LANES)), _full((1, LANES))],
        out_specs=[_rows(tr, LANES), _rows(tr, LANES)],
        out_shape=[jax.ShapeDtypeStruct((t, LANES), F32)] * 2,
        compiler_params=_params(("parallel",)), name="rope_tables")(pos_col, inv_freq_row, sign_row)


def _norm_fwd(x, g, name):
    t, d = x.shape
    tr = _row_tile(t, d, 2)

    def body(x_ref, g_ref, y_ref):
        xv = x_ref[...]
        y_ref[...] = (xv * _rms_scale(xv) * g_ref[...]).astype(BF16)

    return pl.pallas_call(
        body, grid=(t // tr,), in_specs=[_rows(tr, d), _full((1, d))], out_specs=_rows(tr, d),
        out_shape=jax.ShapeDtypeStruct((t, d), BF16), compiler_params=_params(("parallel",)), name=name)(x, g)


def _lat_fwd(z_lat, qg, kvg, cos, sin, ql, kvl):
    t = z_lat.shape[0]
    tr = _row_tile(t, z_lat.shape[1], 2)

    def body(z_ref, qg_ref, kvg_ref, cos_ref, sin_ref, qn_ref, kvn_ref, kpe_ref):
        q = z_ref[:, 0:ql]
        qn_ref[...] = (q * _rms_scale(q) * qg_ref[...]).astype(BF16)
        kv = z_ref[:, ql:ql + kvl]
        kvn_ref[...] = (kv * _rms_scale(kv) * kvg_ref[...]).astype(BF16)
        kpe_ref[...] = _rope(z_ref[:, ql + kvl:ql + kvl + LANES], cos_ref[...], sin_ref[...]).astype(BF16)

    w = z_lat.shape[1]
    return pl.pallas_call(
        body, grid=(t // tr,),
        in_specs=[_rows(tr, w), _full((1, ql)), _full((1, kvl)), _rows(tr, LANES), _rows(tr, LANES)],
        out_specs=[_rows(tr, ql), _rows(tr, kvl), _rows(tr, LANES)],
        out_shape=[jax.ShapeDtypeStruct((t, ql), BF16), jax.ShapeDtypeStruct((t, kvl), BF16),
                   jax.ShapeDtypeStruct((t, LANES), BF16)],
        compiler_params=_params(("parallel",)), name="lat_fwd")(z_lat, qg, kvg, cos, sin)


def _q_rope(q_p, cos, sin, bwd, name):
    t, w = q_p.shape
    tr = _row_tile(t, w, 2)
    fn = _rope_bwd if bwd else _rope

    def body(q_ref, cos_ref, sin_ref, o_ref):
        c, s = cos_ref[...], sin_ref[...]
        for h in range(w // HEAD_PAD):
            o_ref[:, h * HEAD_PAD:h * HEAD_PAD + QK_NOPE] = q_ref[:, h * HEAD_PAD:h * HEAD_PAD + QK_NOPE].astype(BF16)
            lo = h * HEAD_PAD + QK_NOPE
            o_ref[:, lo:lo + LANES] = fn(q_ref[:, lo:lo + LANES].astype(F32), c, s).astype(BF16)

    return pl.pallas_call(
        body, grid=(t // tr,), in_specs=[_rows(tr, w), _rows(tr, LANES), _rows(tr, LANES)], out_specs=_rows(tr, w),
        out_shape=jax.ShapeDtypeStruct((t, w), BF16), compiler_params=_params(("parallel",)), name=name)(q_p, cos, sin)


def _tril_mask():
    r = lax.broadcasted_iota(jnp.int32, (CHUNK, CHUNK), 0)
    c = lax.broadcasted_iota(jnp.int32, (CHUNK, CHUNK), 1)
    return r >= c


def _sgu_fwd(z_uv, gs, ws, b_col):
    t = z_uv.shape[0]
    sw = z_uv.shape[1] // 2
    groups = sw // SGU_GROUP
    tr = _pick(t, 256, CHUNK)

    def body(u_ref, v_ref, gs_ref, ws_ref, b_ref, o_ref):
        v = _gelu(v_ref[...])
        vn = (v * _rms_scale(v) * gs_ref[...]).astype(BF16)
        tri = _tril_mask()
        for g in range(groups):
            wg = jnp.where(tri, ws_ref[g], 0.0).astype(BF16)
            cols = slice(g * SGU_GROUP, (g + 1) * SGU_GROUP)
            for c in range(tr // CHUNK):
                rows = slice(c * CHUNK, (c + 1) * CHUNK)
                mixed = jnp.dot(wg, vn[rows, cols], preferred_element_type=F32) + b_ref[g]
                o_ref[rows, cols] = (_gelu(u_ref[rows, cols]) * mixed).astype(BF16)

    return pl.pallas_call(
        body, grid=(t // tr,),
        in_specs=[_rows(tr, sw, 0), _rows(tr, sw, 1), _full((1, sw)), _full(ws.shape), _full(b_col.shape)],
        out_specs=_rows(tr, sw), out_shape=jax.ShapeDtypeStruct((t, sw), BF16),
        compiler_params=_params(("parallel",)), name="sgu_fwd")(z_uv, z_uv, gs, ws, b_col)


def _merge_fwd(y_attn, y_sgu, z_g, b_gate, comm=None):
    t, d = y_attn.shape
    tr = _row_tile(t, d, 5)

    def body(ya_ref, ys_ref, g0_ref, g1_ref, b0_ref, b1_ref, o_ref):
        g0 = _sigmoid(g0_ref[...] + b0_ref[...])
        g1 = _sigmoid(g1_ref[...] + b1_ref[...])
        o_ref[...] = (g0 * ya_ref[...] + g1 * ys_ref[...]).astype(BF16)

    bspec0 = pl.BlockSpec((1, d), lambda i: (0, 0))
    bspec1 = pl.BlockSpec((1, d), lambda i: (0, 1))
    outs, comm_outs = _call(
        body, grid=(t // tr,),
        in_specs=[_rows(tr, d), _rows(tr, d), _rows(tr, d, 0), _rows(tr, d, 1), bspec0, bspec1],
        out_specs=[_rows(tr, d)], out_shape=[jax.ShapeDtypeStruct((t, d), BF16)],
        sem=("parallel",), name="merge_fwd", args=(y_attn, y_sgu, z_g, z_g, b_gate, b_gate), comm=comm)
    return outs[0], comm_outs


def _swiglu_fwd(gate, up, comm=None):
    t, f = gate.shape
    tr = _row_tile(t, f, 3)

    def body(g_ref, u_ref, o_ref):
        g = g_ref[...]
        o_ref[...] = (g * _sigmoid(g) * u_ref[...]).astype(BF16)

    outs, comm_outs = _call(
        body, grid=(t // tr,), in_specs=[_rows(tr, f), _rows(tr, f)], out_specs=[_rows(tr, f)],
        out_shape=[jax.ShapeDtypeStruct((t, f), BF16)], sem=("parallel",), name="swiglu_fwd", args=(gate, up), comm=comm)
    return outs[0], comm_outs


def _loss_head(h2, g, target):
    t, d = h2.shape
    tr = _row_tile(t, d, 3)

    def body(h_ref, g_ref, t_ref, loss_ref, dh_ref, dhb_ref, dg_ref):
        @pl.when(pl.program_id(0) == 0)
        def _():
            loss_ref[...] = jnp.zeros_like(loss_ref)
            dg_ref[...] = jnp.zeros_like(dg_ref)

        h = h_ref[...]
        r = _rms_scale(h)
        hhat = h * r
        gv = g_ref[...]
        err = hhat * gv - t_ref[...]
        loss_ref[...] += jnp.full(loss_ref.shape, 0.5 * jnp.sum(jnp.mean(err * err, axis=-1)), F32)
        dx, dg_rows = _rms_bwd(hhat, r, gv, err * (1.0 / d))
        dh_ref[...] = dx
        dhb_ref[...] = dx.astype(BF16)
        dg_ref[...] += jnp.sum(dg_rows, axis=0, keepdims=True)

    return pl.pallas_call(
        body, grid=(t // tr,), in_specs=[_rows(tr, d), _full((1, d)), _rows(tr, d)],
        out_specs=[_full((1, LANES)), _rows(tr, d), _rows(tr, d), _full((1, d))],
        out_shape=[jax.ShapeDtypeStruct((1, LANES), F32), jax.ShapeDtypeStruct((t, d), F32),
                   jax.ShapeDtypeStruct((t, d), BF16), jax.ShapeDtypeStruct((1, d), F32)],
        compiler_params=_params(("arbitrary",)), name="loss_head")(h2, g, target)


def _swiglu_bwd(gate, up, dact):
    t, f = gate.shape
    tr = _row_tile(t, f, 4)

    def body(g_ref, u_ref, d_ref, dgu_ref):
        g = g_ref[...]
        s = _sigmoid(g)
        d = d_ref[...]
        dgu_ref[0] = (d * u_ref[...] * (s * (1.0 + g * (1.0 - s)))).astype(BF16)
        dgu_ref[1] = (d * (g * s)).astype(BF16)

    return pl.pallas_call(
        body, grid=(t // tr,), in_specs=[_rows(tr, f)] * 3, out_specs=pl.BlockSpec((2, tr, f), lambda i: (0, i, 0)),
        out_shape=jax.ShapeDtypeStruct((2, t, f), BF16),
        compiler_params=_params(("parallel",)), name="swiglu_bwd")(gate, up, dact)


def _norm_bwd(x, g, dy, resid, name, comm=None):
    t, d = x.shape
    tr = _row_tile(t, d, 5)

    def body(x_ref, g_ref, dy_ref, r_ref, dx_ref, dxb_ref, dg_ref):
        @pl.when(pl.program_id(0) == 0)
        def _():
            dg_ref[...] = jnp.zeros_like(dg_ref)

        xv = x_ref[...]
        r = _rms_scale(xv)
        dx, dg_rows = _rms_bwd(xv * r, r, g_ref[...], dy_ref[...])
        dx = r_ref[...] + dx
        dx_ref[...] = dx
        dxb_ref[...] = dx.astype(BF16)
        dg_ref[...] += jnp.sum(dg_rows, axis=0, keepdims=True)

    outs, comm_outs = _call(
        body, grid=(t // tr,), in_specs=[_rows(tr, d), _full((1, d)), _rows(tr, d), _rows(tr, d)],
        out_specs=[_rows(tr, d), _rows(tr, d), _full((1, d))],
        out_shape=[jax.ShapeDtypeStruct((t, d), F32), jax.ShapeDtypeStruct((t, d), BF16),
                   jax.ShapeDtypeStruct((1, d), F32)],
        sem=("arbitrary",), name=name, args=(x, g, dy, resid), comm=comm)
    return (outs[0], outs[1], outs[2]) if comm is None else (outs[0], outs[1], outs[2], comm_outs)


def _merge_bwd(dmerged, y_attn, y_sgu, z_g, b_gate):
    t, d = y_attn.shape
    tr = _row_tile(t, d, 7)

    def body(dm_ref, ya_ref, ys_ref, g0_ref, g1_ref, b0_ref, b1_ref, dya_ref, dys_ref, dz_ref, db_ref):
        @pl.when(pl.program_id(0) == 0)
        def _():
            db_ref[...] = jnp.zeros_like(db_ref)

        dm = dm_ref[...]
        g0 = _sigmoid(g0_ref[...] + b0_ref[...])
        g1 = _sigmoid(g1_ref[...] + b1_ref[...])
        dya_ref[...] = (dm * g0).astype(BF16)
        dys_ref[...] = (dm * g1).astype(BF16)
        dl0 = dm * ya_ref[...] * (g0 * (1.0 - g0))
        dl1 = dm * ys_ref[...] * (g1 * (1.0 - g1))
        dz_ref[:, 0:d] = dl0.astype(BF16)
        dz_ref[:, d:2 * d] = dl1.astype(BF16)
        db_ref[:, 0:d] += jnp.sum(dl0, axis=0, keepdims=True)
        db_ref[:, d:2 * d] += jnp.sum(dl1, axis=0, keepdims=True)

    bspec0 = pl.BlockSpec((1, d), lambda i: (0, 0))
    bspec1 = pl.BlockSpec((1, d), lambda i: (0, 1))
    return pl.pallas_call(
        body, grid=(t // tr,),
        in_specs=[_rows(tr, d), _rows(tr, d), _rows(tr, d), _rows(tr, d, 0), _rows(tr, d, 1), bspec0, bspec1],
        out_specs=[_rows(tr, d), _rows(tr, d), _rows(tr, 2 * d), _full((1, 2 * d))],
        out_shape=[jax.ShapeDtypeStruct((t, d), BF16), jax.ShapeDtypeStruct((t, d), BF16),
                   jax.ShapeDtypeStruct((t, 2 * d), BF16), jax.ShapeDtypeStruct((1, 2 * d), F32)],
        compiler_params=_params(("arbitrary",)), name="merge_bwd")(dmerged, y_attn, y_sgu, z_g, z_g, b_gate, b_gate)


def _sgu_bwd(z_uv, ds_out, gs, ws, b_col):
    t = z_uv.shape[0]
    sw = z_uv.shape[1] // 2
    groups = sw // SGU_GROUP
    tr = _pick(t, 256, CHUNK)

    def body(u_ref, v_ref, d_ref, gs_ref, ws_ref, b_ref, dz_ref, dws_ref, db_ref, dgs_ref, dvn_ref):
        @pl.when(pl.program_id(0) == 0)
        def _():
            dws_ref[...] = jnp.zeros_like(dws_ref)
            db_ref[...] = jnp.zeros_like(db_ref)
            dgs_ref[...] = jnp.zeros_like(dgs_ref)

        v, dgelu_v = _gelu_and_grad(v_ref[...])
        r = _rms_scale(v)
        vhat = v * r
        gsv = gs_ref[...]
        vn = (vhat * gsv).astype(BF16)
        tri = _tril_mask()
        for g in range(groups):
            wg = jnp.where(tri, ws_ref[g], 0.0).astype(BF16)
            cols = slice(g * SGU_GROUP, (g + 1) * SGU_GROUP)
            for c in range(tr // CHUNK):
                rows = slice(c * CHUNK, (c + 1) * CHUNK)
                vn_cg = vn[rows, cols]
                mixed = jnp.dot(wg, vn_cg, preferred_element_type=F32) + b_ref[g]
                u, dgelu_u = _gelu_and_grad(u_ref[rows, cols])
                dso = d_ref[rows, cols]
                dz_ref[rows, cols] = (dso * mixed * dgelu_u).astype(BF16)
                dmixed = dso * u
                db_ref[g] += jnp.sum(dmixed, axis=1, keepdims=True)
                dmixed_b = dmixed.astype(BF16)
                dws_ref[g] += jnp.where(
                    tri, lax.dot_general(dmixed_b, vn_cg, (((1,), (1,)), ((), ())), preferred_element_type=F32), 0.0)
                dvn_ref[rows, cols] = lax.dot_general(wg, dmixed_b, (((0,), (0,)), ((), ())), preferred_element_type=F32)
        dvn = dvn_ref[...]
        dv, dgs_rows = _rms_bwd(vhat, r, gsv, dvn)
        dz_ref[:, sw:2 * sw] = (dv * dgelu_v).astype(BF16)
        dgs_ref[...] += jnp.sum(dgs_rows, axis=0, keepdims=True)

    return pl.pallas_call(
        body, grid=(t // tr,),
        in_specs=[_rows(tr, sw, 0), _rows(tr, sw, 1), _rows(tr, sw), _full((1, sw)), _full(ws.shape), _full(b_col.shape)],
        out_specs=[_rows(tr, 2 * sw), _full(ws.shape), _full(b_col.shape), _full((1, sw))],
        out_shape=[jax.ShapeDtypeStruct((t, 2 * sw), BF16), jax.ShapeDtypeStruct(ws.shape, F32),
                   jax.ShapeDtypeStruct(b_col.shape, F32), jax.ShapeDtypeStruct((1, sw), F32)],
        scratch_shapes=[pltpu.VMEM((tr, sw), F32)],
        compiler_params=_params(("arbitrary",)), name="sgu_bwd")(z_uv, z_uv, ds_out, gs, ws, b_col)


def _lat_bwd(z_lat, qg, kvg, dqn, dkvn, dkpe_heads, cos, sin, ql, kvl):
    t, w = z_lat.shape
    heads = dkpe_heads.shape[0]
    tr = _row_tile(t, w + heads * LANES, 3)

    def body(z_ref, qg_ref, kvg_ref, dq_ref, dkv_ref, dk_ref, cos_ref, sin_ref, dz_ref, dqg_ref, dkvg_ref):
        @pl.when(pl.program_id(0) == 0)
        def _():
            dqg_ref[...] = jnp.zeros_like(dqg_ref)
            dkvg_ref[...] = jnp.zeros_like(dkvg_ref)

        q = z_ref[:, 0:ql]
        r = _rms_scale(q)
        dx, dg_rows = _rms_bwd(q * r, r, qg_ref[...], dq_ref[...])
        dz_ref[:, 0:ql] = dx.astype(BF16)
        dqg_ref[...] += jnp.sum(dg_rows, axis=0, keepdims=True)
        kv = z_ref[:, ql:ql + kvl]
        r = _rms_scale(kv)
        dx, dg_rows = _rms_bwd(kv * r, r, kvg_ref[...], dkv_ref[...])
        dz_ref[:, ql:ql + kvl] = dx.astype(BF16)
        dkvg_ref[...] += jnp.sum(dg_rows, axis=0, keepdims=True)
        dk = dk_ref[0]
        for h in range(1, heads):
            dk = dk + dk_ref[h]
        dz_ref[:, ql + kvl:ql + kvl + LANES] = _rope_bwd(dk, cos_ref[...], sin_ref[...]).astype(BF16)

    return pl.pallas_call(
        body, grid=(t // tr,),
        in_specs=[_rows(tr, w), _full((1, ql)), _full((1, kvl)), _rows(tr, ql), _rows(tr, kvl),
                  pl.BlockSpec((heads, tr, LANES), lambda i: (0, i, 0)), _rows(tr, LANES), _rows(tr, LANES)],
        out_specs=[_rows(tr, w), _full((1, ql)), _full((1, kvl))],
        out_shape=[jax.ShapeDtypeStruct((t, w), BF16), jax.ShapeDtypeStruct((1, ql), F32),
                   jax.ShapeDtypeStruct((1, kvl), F32)],
        compiler_params=_params(("arbitrary",)), name="lat_bwd")(z_lat, qg, kvg, dqn, dkvn, dkpe_heads, cos, sin)


_NT = (((1,), (1,)), ((), ()))


def _attn_scale():
    return (QK_NOPE + QK_ROPE) ** -0.5


def _attn_fwd(q_c, kv, kpe, comm=None):
    t = q_c.shape[0]
    heads = q_c.shape[1] // HEAD_PAD
    tq = _pick(t, ATTN_TILE)
    nq = t // tq
    scale = _attn_scale()
    to_log2 = scale * math.log2(math.e)
    tn_dims = (((0,), (0,)), ((), ()))

    def body(q_ref, kn_ref, kpe_ref, v_ref, o_ref, ob_ref, lse_ref, m_sc, l_sc, acc_sc):
        qi, ki = pl.program_id(1), pl.program_id(2)

        @pl.when(ki == 0)
        def _():
            m_sc[...] = jnp.full_like(m_sc, NEG_BIG)
            l_sc[...] = jnp.zeros_like(l_sc)
            acc_sc[...] = jnp.zeros_like(acc_sc)

        def step(diagonal):
            kc = jnp.concatenate([kn_ref[...], kpe_ref[...]], axis=1)
            st = lax.dot_general(kc, q_ref[...], _NT, preferred_element_type=F32)
            if diagonal:
                krow = lax.broadcasted_iota(jnp.int32, st.shape, 0)
                qcol = lax.broadcasted_iota(jnp.int32, st.shape, 1)
                st = jnp.where(qcol >= krow, st, NEG_BIG)
            m_prev = m_sc[...]
            m_new = jnp.maximum(m_prev, jnp.max(st, axis=0, keepdims=True))
            alpha = jnp.exp2((m_prev - m_new) * to_log2)
            pt = jnp.exp2((st - m_new) * to_log2)
            l_sc[...] = alpha * l_sc[...] + jnp.sum(pt, axis=0, keepdims=True)
            acc_sc[...] = alpha * acc_sc[...] + lax.dot_general(v_ref[...], pt.astype(BF16), tn_dims,
                                                                preferred_element_type=F32)
            m_sc[...] = m_new

        @pl.when(ki < qi)
        def _():
            step(False)

        @pl.when(ki == qi)
        def _():
            step(True)
            o = (acc_sc[...] / l_sc[...]).T
            o_ref[...] = o
            ob_ref[...] = o.astype(BF16)
            lse_ref[0] = m_sc[...] * scale + jnp.log(l_sc[...])

    kmap = lambda blk: (lambda h, qi, ki: (jnp.minimum(ki, qi), 2 * h + blk))
    outs, comm_outs = _call(
        body, grid=(heads, nq, nq),
        in_specs=[pl.BlockSpec((tq, HEAD_PAD), lambda h, qi, ki: (qi, h)),
                  pl.BlockSpec((tq, QK_NOPE), kmap(0)),
                  pl.BlockSpec((tq, LANES), lambda h, qi, ki: (jnp.minimum(ki, qi), 0)),
                  pl.BlockSpec((tq, V_HEAD), kmap(1))],
        out_specs=[pl.BlockSpec((tq, V_HEAD), lambda h, qi, ki: (qi, h)),
                   pl.BlockSpec((tq, V_HEAD), lambda h, qi, ki: (qi, h)),
                   pl.BlockSpec((1, 1, tq), lambda h, qi, ki: (h, 0, qi))],
        out_shape=[jax.ShapeDtypeStruct((t, heads * V_HEAD), F32), jax.ShapeDtypeStruct((t, heads * V_HEAD), BF16),
                   jax.ShapeDtypeStruct((heads, 1, t), F32)],
        scratch_shapes=[pltpu.VMEM((1, tq), F32), pltpu.VMEM((1, tq), F32), pltpu.VMEM((V_HEAD, tq), F32)],
        sem=("parallel", "parallel", "arbitrary"), name="attn_fwd", args=(q_c, kv, kpe, kv), comm=comm)
    return outs[0], outs[1], outs[2], comm_outs


def _attn_bwd(q_c, kv, kpe, o, do, lse_row, comm=None):
    t = q_c.shape[0]
    heads = q_c.shape[1] // HEAD_PAD
    tk = _pick(t, ATTN_TILE)
    nk = t // tk
    scale = _attn_scale()
    tn_dims = (((0,), (0,)), ((), ()))

    def body(q_ref, kn_ref, kpe_ref, v_ref, do_ref, lse_ref, o_ref, dq_ref, dkv_ref, dkpe_ref, dk_sc, dv_sc, delta_sc):
        ki, qi = pl.program_id(1), pl.program_id(2)

        @pl.when(jnp.logical_and(ki == 0, qi == 0))
        def _():
            dq_ref[...] = jnp.zeros_like(dq_ref)

        @pl.when(qi == 0)
        def _():
            dk_sc[...] = jnp.zeros_like(dk_sc)
            dv_sc[...] = jnp.zeros_like(dv_sc)

        @pl.when(ki == 0)
        def _():
            delta_sc[qi] = jnp.sum((do_ref[...] * o_ref[...]).T, axis=0, keepdims=True)

        def step(diagonal):
            kc = jnp.concatenate([kn_ref[...], kpe_ref[...]], axis=1)
            q = q_ref[...]
            st = lax.dot_general(kc, q, _NT, preferred_element_type=F32) * scale
            pt = jnp.exp(st - lse_ref[0])
            if diagonal:
                krow = lax.broadcasted_iota(jnp.int32, st.shape, 0)
                qcol = lax.broadcasted_iota(jnp.int32, st.shape, 1)
                pt = jnp.where(qcol >= krow, pt, 0.0)
            do_b = do_ref[...].astype(BF16)
            dv_sc[...] += jnp.dot(pt.astype(BF16), do_b, preferred_element_type=F32)
            dpt = lax.dot_general(v_ref[...], do_b, _NT, preferred_element_type=F32)
            dst = (pt * (dpt - delta_sc[qi]) * scale).astype(BF16)
            dk_sc[...] += jnp.dot(dst, q, preferred_element_type=F32)
            rows = pl.ds(pl.multiple_of(qi * tk, tk), tk)
            dq_ref[rows, :] += lax.dot_general(dst, kc, tn_dims, preferred_element_type=F32)

        @pl.when(qi > ki)
        def _():
            step(False)

        @pl.when(qi == ki)
        def _():
            step(True)

        @pl.when(qi == nk - 1)
        def _():
            dkv_ref[:, 0:QK_NOPE] = dk_sc[:, 0:QK_NOPE].astype(BF16)
            dkv_ref[:, QK_NOPE:QK_NOPE + V_HEAD] = dv_sc[...].astype(BF16)
            dkpe_ref[0] = dk_sc[:, QK_NOPE:QK_NOPE + LANES]

    qclamp = lambda h, ki, qi: (jnp.maximum(qi, ki), h)
    kmap = lambda blk: (lambda h, ki, qi: (ki, 2 * h + blk))
    rmap = lambda h, ki, qi: (h, 0, jnp.maximum(qi, ki))
    outs, comm_outs = _call(
        body, grid=(heads, nk, nk),
        in_specs=[pl.BlockSpec((tk, HEAD_PAD), qclamp), pl.BlockSpec((tk, QK_NOPE), kmap(0)),
                  pl.BlockSpec((tk, LANES), lambda h, ki, qi: (ki, 0)), pl.BlockSpec((tk, V_HEAD), kmap(1)),
                  pl.BlockSpec((tk, V_HEAD), qclamp), pl.BlockSpec((1, 1, tk), rmap),
                  pl.BlockSpec((tk, V_HEAD), lambda h, ki, qi: (jnp.where(ki == 0, qi, 0), h))],
        out_specs=[pl.BlockSpec((t, HEAD_PAD), lambda h, ki, qi: (0, h)),
                   pl.BlockSpec((tk, HEAD_PAD), lambda h, ki, qi: (ki, h)),
                   pl.BlockSpec((1, tk, LANES), lambda h, ki, qi: (h, ki, 0))],
        out_shape=[jax.ShapeDtypeStruct((t, heads * HEAD_PAD), F32),
                   jax.ShapeDtypeStruct((t, heads * HEAD_PAD), BF16), jax.ShapeDtypeStruct((heads, t, LANES), F32)],
        scratch_shapes=[pltpu.VMEM((tk, HEAD_PAD), F32), pltpu.VMEM((tk, V_HEAD), F32), pltpu.VMEM((nk, 1, tk), F32)],
        sem=("parallel", "arbitrary", "arbitrary"), name="attn_bwd",
        args=(q_c, kv, kpe, kv, do, lse_row, o), comm=comm)
    return outs[0], outs[1], outs[2], comm_outs


def _local_step(x, pos_col, target, small, shards, opt):
    t = x.shape[0]
    ql, kvl = small["q_norm_g"].shape[1], small["kv_norm_g"].shape[1]
    sw = small["sgu_norm_g"].shape[1]
    heads = (shards["w_uq"].shape[1] * N_DEV) // (QK_NOPE + QK_ROPE)
    big = {}
    early = ["w_in", "w_uq", "w_ukv"]
    big.update(_compute_layout(dict(zip(early, _all_gather([shards[k] for k in early]))), ql, kvl, heads, sw))
    half = QK_ROPE // 2
    lane = jnp.arange(LANES)
    inv_freq = ROPE_THETA ** (-jnp.arange(0, QK_ROPE, 2, dtype=F32) / QK_ROPE)
    inv_row = inv_freq[lane % half][None, :]
    sign_row = jnp.where((lane % QK_ROPE) < half, -1.0, 1.0).astype(F32)[None, :]
    cos, sin = _rope_tables(pos_col, inv_row, sign_row)
    ws = small["w_sgu"]
    b_col = small["b_sgu_col"]

    def arrived(names, bufs):
        big.update(_compute_layout(dict(zip(names, bufs)), ql, kvl, heads, sw))

    a = _norm_fwd(x, small["norm_mix_g"], "norm_mix_fwd")
    z_lat = _mm(a, big["w_lat_t"], tb=True, name="z_lat")
    z_uv, g_sgu = _mm(a, big["w_uv_t"], tb=True, name="z_uv", comm=_gather_stage(1, [shards["w_o_sgu"]]))
    z_g, (g_attn, g_sgu) = _mm(a, big["w_g_t"], tb=True, name="z_g",
                               comm=_join(_gather_stage(1, [shards["w_o_attn"]]), _gather_stage(2, g_sgu)))
    qn, kvn, kpe = _lat_fwd(z_lat, small["q_norm_g"], small["kv_norm_g"], cos, sin, ql, kvl)
    q_p, (g_attn, g_sgu) = _mm(qn, big["w_uq"], name="q_up",
                               comm=_join(_gather_stage(2, [g_attn]), _gather_stage(3, [g_sgu])))
    kv, (g_attn,) = _mm(kvn, big["w_ukv"], out_dtype=BF16, name="kv_up", comm=_gather_stage(3, [g_attn]))
    arrived(["w_o_sgu", "w_o_attn"], [g_sgu, g_attn])
    q_c = _q_rope(q_p, cos, sin, False, "q_rope")
    attn, attn_b, lse, (g_out, w_gate, w_up) = _attn_fwd(
        q_c, kv, kpe, comm=_gather_stage(1, [shards[k] for k in ("w_out", "w_gate_ffn", "w_up_ffn")]))
    s_out = _sgu_fwd(z_uv, small["sgu_norm_g"], ws, b_col)
    y_sgu, (g_out,) = _mm(s_out, big["w_o_sgu"], name="y_sgu", comm=_gather_stage(2, [g_out]))
    y_attn, (w_gate, g_out) = _mm(attn_b, big["w_o_attn"], name="y_attn",
                                  comm=_join(_gather_stage(2, [w_gate]), _gather_stage(3, [g_out])))
    arrived(["w_out"], [g_out])
    merged, (w_up, w_gate) = _merge_fwd(y_attn, y_sgu, z_g, small["b_gate"],
                                        comm=_join(_gather_stage(2, [w_up]), _gather_stage(3, [w_gate])))
    h1, (w_up,) = _mm(merged, big["w_out"], add=x, name="h1", comm=_gather_stage(3, [w_up]))
    f = _norm_fwd(h1, small["norm_ffn_g"], "norm_ffn_fwd")
    gate, w_down = _mm(f, w_gate, tb=True, slab="n", name="ffn_gate", comm=_gather_stage(1, [shards["w_down_ffn"]]))
    up, w_down = _mm(f, w_up, tb=True, slab="n", name="ffn_up", comm=_gather_stage(2, w_down))
    ffn = gate.shape[2]
    gate, up = gate.reshape(N_DEV * t, ffn), up.reshape(N_DEV * t, ffn)
    act, (w_down,) = _swiglu_fwd(gate, up, comm=_gather_stage(3, w_down))
    act = act.reshape(N_DEV, t, ffn)
    h2 = _mm(act, w_down, slab="k", add=h1, name="h2")
    loss_row, dh2, dh2_b, d_norm_final = _loss_head(h2, small["norm_final_g"], target)

    def pair_sums(names, slabs, bufs):
        return [_pair_sum(g, b, "pair_sum_" + k) for k, g, b in zip(names, slabs, bufs)]

    parts, updates = {}, {}

    def update(k, comm=None):
        w, m, v = opt[k]
        updates[k], got = _adamw_shard(parts[k], w, m, v, "adamw_" + k, comm=comm)
        return got

    down_slabs = [_mm(act, dh2_b, ta=True, slab="m", out_dtype=BF16, name="dw_down")]
    dact, bufs = _mm(dh2_b, w_down, tb=True, slab="n", name="dact", comm=_to_sibling(down_slabs))
    down_pair = pair_sums(["w_down_ffn"], down_slabs, bufs)
    dgu = _swiglu_bwd(gate, up, dact.reshape(N_DEV * t, ffn)).reshape(2 * N_DEV, t, ffn)
    dw_gu, got = _mm(dgu, f, ta=True, slab="m", out_dtype=BF16, name="dw_gate_up", comm=_to_chips(down_pair))
    parts["w_down_ffn"] = got[0]
    gu_names = ["w_gate_ffn", "w_up_ffn"]
    df, bufs = _mm(dgu, w_gate, slab="k", name="df_gate", comm=_to_sibling([dw_gu, dw_gu], first=[0, N_DEV]))
    gu_pairs = [_pair_sum(dw_gu, b, "pair_sum_" + k, first=s0) for k, b, s0 in zip(gu_names, bufs, [0, N_DEV])]
    df = _mm(dgu, w_up, slab="k", a_slab0=N_DEV, add=df, name="df_up")
    dh1, dh1_b, d_norm_ffn = _norm_bwd(h1, small["norm_ffn_g"], df, dh2, "norm_ffn_bwd")
    dw_out = _mm(merged, dh1_b, ta=True, out_dtype=BF16, name="dw_out")
    out_slabs = [_slabs_from_rows(dw_out)]
    dmerged, bufs = _mm(dh1_b, big["w_out"], tb=True, name="dmerged", comm=_to_sibling(out_slabs))
    out_pair = pair_sums(["w_out"], out_slabs, bufs)
    dy_attn, dy_sgu, dz_g, d_b_gate = _merge_bwd(dmerged, y_attn, y_sgu, z_g, small["b_gate"])
    dw_o_sgu = _mm(s_out, dy_sgu, ta=True, out_dtype=BF16, name="dw_o_sgu")
    ds_out = _mm(dy_sgu, big["w_o_sgu"], tb=True, name="ds_out")
    dz_uv, d_ws, d_b_col, d_sgu_norm = _sgu_bwd(z_uv, ds_out, small["sgu_norm_g"], ws, b_col)
    dw_o_attn = _mm(attn_b, dy_attn, ta=True, out_dtype=BF16, name="dw_o_attn")
    mix_names = ["w_o_sgu", "w_o_attn"]
    mix_slabs = [_slabs_from_cols(dw_o_sgu), _slabs_from_rows(dw_o_attn)]
    dattn, bufs = _mm(dy_attn, big["w_o_attn"], tb=True, name="dattn", comm=_to_sibling(mix_slabs))
    mix_pairs = pair_sums(mix_names, mix_slabs, bufs)
    dq_c, dkv, dkpe_heads, got = _attn_bwd(q_c, kv, kpe, attn, dattn, lse, comm=_to_chips(gu_pairs))
    parts.update(zip(gu_names, got))
    dq_p = _q_rope(dq_c, cos, sin, True, "q_rope_bwd")
    dw_uq = _mm(qn, dq_p, ta=True, out_dtype=BF16, name="dw_uq")
    dw_ukv = _mm(kvn, dkv, ta=True, out_dtype=BF16, name="dw_ukv")
    dqn = _mm(dq_p, big["w_uq"], tb=True, name="dqn")
    dkvn = _mm(dkv, big["w_ukv"], tb=True, name="dkvn")
    dz_lat, d_q_norm, d_kv_norm = _lat_bwd(z_lat, small["q_norm_g"], small["kv_norm_g"], dqn, dkvn, dkpe_heads,
                                           cos, sin, ql, kvl)
    dw_g, got = _mm(dz_g, a, ta=True, out_dtype=BF16, name="dw_g", comm=_to_chips(out_pair))
    parts["w_out"] = got[0]
    dw_uv, got = _mm(dz_uv, a, ta=True, out_dtype=BF16, name="dw_uv", comm=_to_chips(mix_pairs[1:]))
    parts["w_o_attn"] = got[0]
    dw_lat, got = _mm(dz_lat, a, ta=True, out_dtype=BF16, name="dw_lat", comm=_to_chips(mix_pairs[:1]))
    parts["w_o_sgu"] = got[0]
    lat = ql + kvl + QK_ROPE
    dw_uq_cols = dw_uq.reshape(ql, heads, HEAD_PAD)[:, :, :QK_NOPE + QK_ROPE].reshape(ql, heads * (QK_NOPE + QK_ROPE))
    in_names = ["w_uq", "w_ukv", "w_in"]
    in_slabs = [_slabs_from_cols(dw_uq_cols), _slabs_from_cols(dw_ukv),
                _slabs_from_rows(jnp.concatenate([dw_lat[:lat], dw_uv, dw_g], axis=0))]
    da = _mm(dz_lat, big["w_lat_t"], name="da_lat")
    da, bufs = _mm(dz_uv, big["w_uv_t"], add=da, name="da_uv", comm=_to_sibling(in_slabs))
    uq_pair, ukv_pair, in_pair = pair_sums(in_names, in_slabs, bufs)
    cols = in_pair.shape[2]
    chunk = _pick(cols, cols // TAIL_CHUNKS)
    chunks = [("c", c0, chunk) for c0 in range(0, cols, chunk)]
    da, got = _mm(dz_g, big["w_g_t"], add=da, name="da_g",
                  comm=_to_chips([uq_pair, in_pair], rows=[None, chunks[0]]))
    parts["w_uq"], in_parts = got
    grad_x, _, d_norm_mix, got = _norm_bwd(x, small["norm_mix_g"], da, dh1, "norm_mix_bwd", comm=_to_chips([ukv_pair]))
    parts["w_ukv"] = got[0]
    hosts = ["w_gate_ffn", "w_up_ffn", "w_down_ffn", "w_out", "w_o_attn", "w_o_sgu", "w_uq", "w_ukv"]
    assert len(chunks) <= 1 + len(hosts)
    for i, k in enumerate(hosts):
        if 1 + i < len(chunks):
            in_parts = update(k, comm=_to_chips([in_pair], rows=[chunks[1 + i]], into=[in_parts]))[0]
        else:
            update(k)
    parts["w_in"] = in_parts
    update("w_in")

    gs = {"norm_mix_g": d_norm_mix, "b_gate": d_b_gate, "q_norm_g": d_q_norm, "kv_norm_g": d_kv_norm,
          "sgu_norm_g": d_sgu_norm, "w_sgu": d_ws, "b_sgu_col": d_b_col, "norm_ffn_g": d_norm_ffn,
          "norm_final_g": d_norm_final}
    return loss_row, grad_x, gs, updates


def _my_place():
    return lax.axis_index("x"), lax.axis_index("y"), lax.axis_index("c")


N_CHIPS = N_DEV // 2

_GATHER_SEMS = [[(3,), (3,), ()], [(4,), (4,)], [(1,), (1,)]]


def _halves(shape):
    r, c = shape
    if (c // 2) % LANES == 0:
        return ("c", 0, c // 2), ("c", c // 2, c // 2)
    assert (r // 2) % (2 * SUBLANES) == 0, shape
    return ("r", 0, r // 2), ("r", r // 2, r // 2)


def _gather_copies(stage, ins, outs, sems):
    x, y, c = _my_place()
    me, x_nbr, y_nbr, diag = 4 * x + 2 * y + c, 4 * (1 - x) + 2 * y + c, 4 * x + 2 * (1 - y) + c, 4 * (1 - x) + 2 * (1 - y) + c
    sibling = (x, y, 1 - c)

    def remote(w, k, src, dst, to):
        return pltpu.make_async_remote_copy(src_ref=src, dst_ref=dst, send_sem=sems[0].at[w, k], recv_sem=sems[1].at[w, k],
                                            device_id=to, device_id_type=MESH)

    out = []
    for w in range(len(outs)):
        if stage == 1:
            dst = outs[w].at[me]
            out.append(pltpu.make_async_copy(ins[w], dst, sems[2].at[w]))
            out += [remote(w, k, ins[w], dst, to) for k, to in enumerate([sibling, (1 - x, y, c), (x, 1 - y, c)])]
        elif stage == 2:
            first, second = _halves(outs[w].shape[1:])
            out.append(remote(w, 0, _window(ins[w], x_nbr, first), _window(outs[w], x_nbr, first), (x, 1 - y, c)))
            out.append(remote(w, 1, _window(ins[w], y_nbr, second), _window(outs[w], y_nbr, second), (1 - x, y, c)))
            out.append(remote(w, 2, ins[w].at[x_nbr], outs[w].at[x_nbr], sibling))
            out.append(remote(w, 3, ins[w].at[y_nbr], outs[w].at[y_nbr], sibling))
        else:
            out.append(remote(w, 0, ins[w].at[diag], outs[w].at[diag], sibling))
    return out


def _gather_stage(stage, arrays):
    n = len(arrays)

    def start(ins, outs, sems):
        for cp in _gather_copies(stage, ins, outs, sems):
            cp.start()

    def finish(ins, outs, sems):
        for cp in _gather_copies(stage, ins, outs, sems):
            cp.wait()

    shapes = [jax.ShapeDtypeStruct(((N_DEV,) + a.shape) if stage == 1 else a.shape, a.dtype) for a in arrays]
    return _Comm(arrays, shapes, [pltpu.SemaphoreType.DMA((n,) + s) for s in _GATHER_SEMS[stage - 1]], start, finish,
                 aliases=None if stage == 1 else {w: w for w in range(n)})


def _join(*comms):
    ins, shapes, sems, aliases, spans = [], [], [], {}, []
    for cm in comms:
        spans.append((len(ins), len(ins) + len(cm.ins), len(shapes), len(shapes) + len(cm.out_shapes),
                      len(sems), len(sems) + len(cm.sems)))
        aliases.update({len(ins) + i: len(shapes) + o for i, o in cm.aliases.items()})
        ins, shapes, sems = ins + cm.ins, shapes + cm.out_shapes, sems + cm.sems

    def each(half):
        def run(i_refs, o_refs, s_refs):
            for cm, (i0, i1, o0, o1, s0, s1) in zip(comms, spans):
                getattr(cm, half)(i_refs[i0:i1], o_refs[o0:o1], s_refs[s0:s1])
        return run

    return _Comm(ins, shapes, sems, each("start"), each("finish"), aliases)


def _all_gather(shards):
    n = len(shards)
    n_sems = [len(s) for s in _GATHER_SEMS]

    def body(*refs):
        ins, outs, sems = refs[:n], refs[n:2 * n], refs[2 * n:]
        s0 = 0
        for stage in (1, 2, 3):
            mine = sems[s0:s0 + n_sems[stage - 1]]
            s0 += n_sems[stage - 1]
            copies = _gather_copies(stage, ins if stage == 1 else outs, outs, mine)
            for cp in copies:
                cp.start()
            for cp in copies:
                cp.wait()

    any_spec = pl.BlockSpec(memory_space=pl.ANY)
    return pl.pallas_call(
        body, in_specs=[any_spec] * n, out_specs=[any_spec] * n,
        out_shape=[jax.ShapeDtypeStruct((N_DEV,) + s.shape, s.dtype) for s in shards],
        scratch_shapes=[pltpu.SemaphoreType.DMA((n,) + s) for stage in _GATHER_SEMS for s in stage],
        compiler_params=pltpu.CompilerParams(has_side_effects=True), name="all_gather_weights")(*shards)


def _to_sibling(grads, first=None):
    n = len(grads)
    first = first or [0] * n

    def copies(ins, outs, sems):
        x, y, c = _my_place()
        send_sems, recv_sems = sems
        return [pltpu.make_async_remote_copy(
            src_ref=ins[w].at[first[w] + 2 * i + (1 - c)], dst_ref=outs[w].at[i], send_sem=send_sems.at[w, i],
            recv_sem=recv_sems.at[w, i], device_id=(x, y, 1 - c), device_id_type=MESH)
            for w in range(n) for i in range(N_CHIPS)]

    def start(ins, outs, sems):
        for cp in copies(ins, outs, sems):
            cp.start()

    def finish(ins, outs, sems):
        for cp in copies(ins, outs, sems):
            cp.wait()

    return _Comm(grads, [jax.ShapeDtypeStruct((N_CHIPS,) + g.shape[1:], g.dtype) for g in grads],
                 [pltpu.SemaphoreType.DMA((n, N_CHIPS)), pltpu.SemaphoreType.DMA((n, N_CHIPS))], start, finish)


def _window(ref, slab, win):
    if win is None:
        return ref.at[slab]
    if win[0] == "r":
        return ref.at[slab, pl.ds(win[1], win[2])]
    return ref.at[slab, slice(None), pl.ds(win[1], win[2])]


def _to_chips(parts, rows=None, into=None):
    n = len(parts)
    rows = rows or [None] * n

    def copies(ins, outs, sems):
        x, y, c = _my_place()
        send_sems, recv_sems, local_sems = sems
        mine = 2 * x + y
        chips = [(1 - x, y), (x, 1 - y), (1 - x, 1 - y)]
        remote = [pltpu.make_async_remote_copy(
            src_ref=_window(ins[w], 2 * cx + cy, rows[w]), dst_ref=_window(outs[w], mine, rows[w]),
            send_sem=send_sems.at[w, j], recv_sem=recv_sems.at[w, j], device_id=(cx, cy, c), device_id_type=MESH)
            for w in range(n) for j, (cx, cy) in enumerate(chips)]
        local = [pltpu.make_async_copy(_window(ins[w], mine, rows[w]), _window(outs[w], mine, rows[w]),
                                       local_sems.at[w]) for w in range(n)]
        return remote + local

    def start(ins, outs, sems):
        for cp in copies(ins, outs, sems):
            cp.start()

    def finish(ins, outs, sems):
        for cp in copies(ins, outs, sems):
            cp.wait()

    return _Comm(list(parts) + list(into or []), [jax.ShapeDtypeStruct(p.shape, p.dtype) for p in parts],
                 [pltpu.SemaphoreType.DMA((n, N_CHIPS - 1)), pltpu.SemaphoreType.DMA((n, N_CHIPS - 1)),
                  pltpu.SemaphoreType.DMA((n,))], start, finish,
                 aliases={n + w: w for w in range(n)} if into else None)


def _pair_sum(g, buf, name, first=0):
    _, r, c = g.shape
    tr, tc = _shard_tile(r, c, 4 * SHARD_TILE_ELEMS, 1024)
    core = (lax.axis_index("c") + first).astype(jnp.int32).reshape(1)

    def body(core_ref, g_ref, b_ref, o_ref):
        o_ref[...] = (g_ref[...].astype(F32) + b_ref[...].astype(F32)).astype(o_ref.dtype)

    blk = (1, tr, tc)
    return pl.pallas_call(
        body, grid_spec=pltpu.PrefetchScalarGridSpec(
            num_scalar_prefetch=1, grid=(N_CHIPS, r // tr, c // tc),
            in_specs=[pl.BlockSpec(blk, lambda i, j, l, core_ref: (2 * i + core_ref[0], j, l)),
                      pl.BlockSpec(blk, lambda i, j, l, core_ref: (i, j, l))],
            out_specs=pl.BlockSpec(blk, lambda i, j, l, core_ref: (i, j, l))),
        out_shape=jax.ShapeDtypeStruct(buf.shape, buf.dtype),
        compiler_params=_params(("parallel", "parallel", "parallel")), name=name)(core, g, buf)


def _all_reduce_pack(pack):
    r = pack.shape[0]

    def body(x_ref, out_ref, gath_ref, send_sems, recv_sems, local_sem):
        x, y, c = _my_place()
        me, sibling = (x, y, c), (x, y, 1 - c)
        chips = [(1 - x, y), (x, 1 - y), (1 - x, 1 - y)]

        def slab(place):
            return gath_ref.at[4 * place[0] + 2 * place[1] + place[2]]

        def copy(k, place, to, src=None):
            return pltpu.make_async_remote_copy(
                src_ref=slab(place) if src is None else src, dst_ref=slab(place),
                send_sem=send_sems.at[k], recv_sem=recv_sems.at[k], device_id=to, device_id_type=MESH)

        mine = pltpu.make_async_copy(x_ref, slab(me), local_sem)
        mine.start()
        first = [copy(0, me, sibling, src=x_ref)]
        first += [copy(1 + j, me, (*chip, c), src=x_ref) for j, chip in enumerate(chips)]
        for cp in first:
            cp.start()
        passed = [copy(4 + j, (*chip, c), sibling) for j, chip in enumerate(chips)]
        for j, chip in enumerate(chips):
            copy(1 + j, (*chip, c), me).wait_recv()
            passed[j].start()
        copy(0, sibling, me).wait_recv()
        for j, chip in enumerate(chips):
            copy(4 + j, (*chip, 1 - c), me).wait_recv()
        for cp in first + passed:
            cp.wait_send()
        mine.wait()
        acc = gath_ref[0]
        for i in range(1, N_DEV):
            acc = acc + gath_ref[i]
        out_ref[...] = acc

    vmem = pl.BlockSpec(memory_space=pltpu.VMEM)
    return pl.pallas_call(
        body, in_specs=[vmem], out_specs=vmem, out_shape=jax.ShapeDtypeStruct(pack.shape, F32),
        scratch_shapes=[pltpu.VMEM((N_DEV, r, LANES), F32), pltpu.SemaphoreType.DMA((7,)),
                        pltpu.SemaphoreType.DMA((7,)), pltpu.SemaphoreType.DMA],
        compiler_params=pltpu.CompilerParams(vmem_limit_bytes=VMEM_LIMIT), name="all_reduce_small")(pack)


def _adamw_math(w, g, m, v):
    m = ADAM_B1 * m + (1.0 - ADAM_B1) * g
    v = ADAM_B2 * v + (1.0 - ADAM_B2) * (g * g)
    m_hat = m / (1.0 - ADAM_B1 ** ADAM_STEP)
    v_hat = v / (1.0 - ADAM_B2 ** ADAM_STEP)
    delta = -ADAM_LR * (m_hat / (jnp.sqrt(v_hat) + ADAM_EPS) + ADAM_WD * w)
    return delta, m, v


def _adamw_shard(parts, w, m, v, name, comm=None):
    r, c = w.shape
    n_parts = parts.shape[0]
    tr, tc = _shard_tile(r, c)

    def body(p_ref, w_ref, m_ref, v_ref, g_ref, d_ref, nm_ref, nv_ref):
        g = p_ref[0].astype(F32)
        for i in range(1, n_parts):
            g = g + p_ref[i].astype(F32)
        g_ref[...] = g
        d_ref[...], nm_ref[...], nv_ref[...] = _adamw_math(w_ref[...], g, m_ref[...], v_ref[...])

    spec = pl.BlockSpec((tr, tc), lambda i, j: (i, j))
    outs, comm_outs = _call(
        body, grid=(r // tr, c // tc),
        in_specs=[pl.BlockSpec((n_parts, tr, tc), lambda i, j: (0, i, j)), spec, spec, spec],
        out_specs=[spec] * 4, out_shape=[jax.ShapeDtypeStruct((r, c), F32)] * 4,
        sem=("parallel", "parallel"), name=name, args=(parts, w, m, v), comm=comm)
    return outs, comm_outs


def _adamw_pack(g, w, m, v):
    r, c = w.shape

    def body(g_ref, w_ref, m_ref, v_ref, d_ref, nm_ref, nv_ref):
        d_ref[...], nm_ref[...], nv_ref[...] = _adamw_math(w_ref[...], g_ref[...], m_ref[...], v_ref[...])

    return pl.pallas_call(
        body, in_specs=[_full((r, c))] * 4, out_specs=[_full((r, c))] * 3, grid=(1,),
        out_shape=[jax.ShapeDtypeStruct((r, c), F32)] * 3,
        compiler_params=_params(("arbitrary",)), name="adamw_small")(g, w, m, v)


def _cols_from_slabs(g):
    return jnp.transpose(g, (1, 0, 2)).reshape(g.shape[1], N_DEV * g.shape[2])


def _slabs_from_cols(w):
    r, c8 = w.shape
    return jnp.transpose(w.reshape(r, N_DEV, c8 // N_DEV), (1, 0, 2))


def _rows_from_slabs(g):
    return g.reshape(N_DEV * g.shape[1], g.shape[2])


def _slabs_from_rows(w):
    return w.reshape(N_DEV, w.shape[0] // N_DEV, w.shape[1])


def _compute_layout(gathered, ql, kvl, heads, sw):
    out = {}
    for k, g in gathered.items():
        if k == "w_in":
            lat = ql + kvl + QK_ROPE
            w_in_t = _rows_from_slabs(g)
            out["w_lat_t"] = jnp.pad(w_in_t[:lat], ((0, LANES - QK_ROPE), (0, 0)))
            out["w_uv_t"] = w_in_t[lat:lat + 2 * sw]
            out["w_g_t"] = w_in_t[lat + 2 * sw:]
        elif k == "w_uq":
            per_head = _cols_from_slabs(g).reshape(ql, heads, QK_NOPE + QK_ROPE)
            pad = HEAD_PAD - QK_NOPE - QK_ROPE
            out["w_uq"] = jnp.pad(per_head, ((0, 0), (0, 0), (0, pad))).reshape(ql, heads * HEAD_PAD)
        elif k in ("w_o_attn", "w_out", "w_down_ffn"):
            out[k.removesuffix("_ffn")] = _rows_from_slabs(g)
        else:
            out[k.removesuffix("_ffn")] = _cols_from_slabs(g)
    return out


_SMALL =["norm_mix_g", "b_gate", "q_norm_g", "kv_norm_g", "sgu_norm_g", "w_sgu", "b_sgu", "norm_ffn_g", "norm_final_g"]
_BIG = ["w_in", "w_uq", "w_ukv", "w_o_attn", "w_o_sgu", "w_out", "w_gate_ffn", "w_up_ffn", "w_down_ffn"]
_TRANSPOSED = ("w_in", "w_gate_ffn", "w_up_ffn")
_ORDER = ["norm_mix_g", "w_in", "b_gate", "q_norm_g", "w_uq", "kv_norm_g", "w_ukv", "w_o_attn", "sgu_norm_g", "w_sgu",
          "b_sgu", "w_o_sgu", "w_out", "norm_ffn_g", "w_gate_ffn", "w_up_ffn", "w_down_ffn", "norm_final_g"]


def _pack_rows(parts):
    rows, sizes = [], []
    for p in parts:
        flat = p.reshape(-1)
        n = flat.shape[0]
        padded = -(-n // (SUBLANES * LANES)) * (SUBLANES * LANES)
        rows.append(jnp.pad(flat, (0, padded - n)).reshape(padded // LANES, LANES))
        sizes.append((n, padded // LANES))
    return jnp.concatenate(rows, axis=0), sizes


def _unpack_rows(pack, sizes, shapes):
    out, r0 = [], 0
    for (n, nr), shp in zip(sizes, shapes):
        out.append(pack[r0:r0 + nr].reshape(-1)[:n].reshape(shp))
        r0 += nr
    return out


def kernel(x, positions, norm_mix_g, w_in, b_gate, q_norm_g, w_uq, kv_norm_g, w_ukv, w_o_attn, sgu_norm_g, w_sgu, b_sgu, w_o_sgu, w_out, norm_ffn_g, w_gate_ffn, w_up_ffn, w_down_ffn, norm_final_g, loss_target, m_norm_mix_g, m_w_in, m_b_gate, m_q_norm_g, m_w_uq, m_kv_norm_g, m_w_ukv, m_w_o_attn, m_sgu_norm_g, m_w_sgu, m_b_sgu, m_w_o_sgu, m_w_out, m_norm_ffn_g, m_w_gate_ffn, m_w_up_ffn, m_w_down_ffn, m_norm_final_g, v_norm_mix_g, v_w_in, v_b_gate, v_q_norm_g, v_w_uq, v_kv_norm_g, v_w_ukv, v_w_o_attn, v_sgu_norm_g, v_w_sgu, v_b_sgu, v_w_o_sgu, v_w_out, v_norm_ffn_g, v_w_gate_ffn, v_w_up_ffn, v_w_down_ffn, v_norm_final_g):
    wts = dict(norm_mix_g=norm_mix_g, w_in=w_in, b_gate=b_gate, q_norm_g=q_norm_g, w_uq=w_uq, kv_norm_g=kv_norm_g,
               w_ukv=w_ukv, w_o_attn=w_o_attn, sgu_norm_g=sgu_norm_g, w_sgu=w_sgu, b_sgu=b_sgu, w_o_sgu=w_o_sgu,
               w_out=w_out, norm_ffn_g=norm_ffn_g, w_gate_ffn=w_gate_ffn, w_up_ffn=w_up_ffn, w_down_ffn=w_down_ffn,
               norm_final_g=norm_final_g)
    mom = dict(norm_mix_g=m_norm_mix_g, w_in=m_w_in, b_gate=m_b_gate, q_norm_g=m_q_norm_g, w_uq=m_w_uq,
               kv_norm_g=m_kv_norm_g, w_ukv=m_w_ukv, w_o_attn=m_w_o_attn, sgu_norm_g=m_sgu_norm_g, w_sgu=m_w_sgu,
               b_sgu=m_b_sgu, w_o_sgu=m_w_o_sgu, w_out=m_w_out, norm_ffn_g=m_norm_ffn_g, w_gate_ffn=m_w_gate_ffn,
               w_up_ffn=m_w_up_ffn, w_down_ffn=m_w_down_ffn, norm_final_g=m_norm_final_g)
    var = dict(norm_mix_g=v_norm_mix_g, w_in=v_w_in, b_gate=v_b_gate, q_norm_g=v_q_norm_g, w_uq=v_w_uq,
               kv_norm_g=v_kv_norm_g, w_ukv=v_w_ukv, w_o_attn=v_w_o_attn, sgu_norm_g=v_sgu_norm_g, w_sgu=v_w_sgu,
               b_sgu=v_b_sgu, w_o_sgu=v_w_o_sgu, w_out=v_w_out, norm_ffn_g=v_norm_ffn_g, w_gate_ffn=v_w_gate_ffn,
               w_up_ffn=v_w_up_ffn, w_down_ffn=v_w_down_ffn, norm_final_g=v_norm_final_g)

    t, d = x.shape[1], x.shape[2]
    ql, kvl = q_norm_g.shape[1], kv_norm_g.shape[1]
    heads = (w_uq.shape[2] * N_DEV) // (QK_NOPE + QK_ROPE)
    sw = sgu_norm_g.shape[1]

    def shard(a, k):
        return a[0].T if k in _TRANSPOSED else a[0]

    def unshard(a, k):
        return (a.T if k in _TRANSPOSED else a).reshape(wts[k].shape)

    opt = {k: (shard(wts[k], k), shard(mom[k], k), shard(var[k], k)) for k in _BIG}
    shards = {k: opt[k][0].astype(BF16) for k in _BIG}
    small = {
        "norm_mix_g": norm_mix_g, "b_gate": b_gate, "q_norm_g": q_norm_g, "kv_norm_g": kv_norm_g,
        "sgu_norm_g": sgu_norm_g, "w_sgu": w_sgu[0], "b_sgu_col": b_sgu[0][:, :, None], "norm_ffn_g": norm_ffn_g,
        "norm_final_g": norm_final_g[None, :],
    }

    loss_row, grad_x, gs, updates = _local_step(x[0], positions.reshape(t, 1), loss_target[0], small, shards, opt)
    grads, deltas, new_m, new_v = {}, {}, {}, {}
    for k in _BIG:
        grads[k], deltas[k], new_m[k], new_v[k] = (unshard(a, k) for a in updates[k])

    small_grads = [gs["norm_mix_g"], gs["b_gate"], gs["q_norm_g"], gs["kv_norm_g"], gs["sgu_norm_g"], gs["w_sgu"],
                   gs["b_sgu_col"], gs["norm_ffn_g"], gs["norm_final_g"]]
    pack, sizes = _pack_rows([loss_row] + small_grads)
    total = _all_reduce_pack(pack)
    shapes = [(1, LANES)] + [wts[k].shape for k in _SMALL]
    unpacked = _unpack_rows(total, sizes, shapes)
    loss = unpacked[0][0, 0]
    for k, g in zip(_SMALL, unpacked[1:]):
        grads[k] = g
    g_pack = total[sizes[0][1]:]
    w_pack, _ = _pack_rows([wts[k] for k in _SMALL])
    m_pack, _ = _pack_rows([mom[k] for k in _SMALL])
    v_pack, _ = _pack_rows([var[k] for k in _SMALL])
    d_pack, nm_pack, nv_pack = _adamw_pack(g_pack, w_pack, m_pack, v_pack)
    small_shapes = [wts[k].shape for k in _SMALL]
    for store, pk in ((deltas, d_pack), (new_m, nm_pack), (new_v, nv_pack)):
        for k, a in zip(_SMALL, _unpack_rows(pk, sizes[1:], small_shapes)):
            store[k] = a

    return (loss, grad_x[None], *[grads[k] for k in _ORDER], *[deltas[k] for k in _ORDER],
            *[new_m[k] for k in _ORDER], *[new_v[k] for k in _ORDER])
```

```python
import functools
import math

import jax
import jax.numpy as jnp
from jax import lax
from jax.experimental import pallas as pl
from jax.experimental.pallas import tpu as pltpu

F32 = jnp.float32
BF16 = jnp.bfloat16

N_DEV = 8
N_HEADS = 16
QK_NOPE = 128
QK_ROPE = 64
V_HEAD = 128
HEAD_PAD = 256
ROPE_THETA = 10000.0
CHUNK = 128
SGU_GROUP = 128
RMS_EPS = 1e-6
LANES = 128
SUBLANES = 8

ADAM_LR = 0.001
ADAM_B1 = 0.9
ADAM_B2 = 0.999
ADAM_EPS = 1e-08
ADAM_WD = 0.01
ADAM_STEP = 10

VMEM_LIMIT = 48 * 1024 * 1024
MM_TILE = (2048, 512, 2048)
MM_TILE_TA = (512, 2048)
ATTN_TILE = 512
HEADS_PER_STEP = 2
ROW_KERNEL_BYTES = 24 * 1024 * 1024
SHARD_TILE_ELEMS = 256 * 1024
SLABS_PER_STEP = 2
TAIL_CHUNKS = 4
NEG_BIG = -1e30
MESH = pl.DeviceIdType.MESH


def _pick(n, target, mult=LANES):
    best = None
    d = mult
    while d <= min(n, target):
        if n % d == 0:
            best = d
        d += mult
    return best or n


def _row_tile(t, width, n_blocks, mult=2 * SUBLANES):
    return _pick(t, max(mult, ROW_KERNEL_BYTES // (3 * n_blocks * width * 4)), mult)


def _shard_tile(r, c, elems=SHARD_TILE_ELEMS, max_rows=256):
    tr = _pick(r, max_rows, 2 * SUBLANES)
    return tr, _pick(c, max(LANES, elems // tr))


def _params(sem):
    return pltpu.CompilerParams(dimension_semantics=sem, vmem_limit_bytes=VMEM_LIMIT)


def _full(shape):
    nd = len(shape)
    return pl.BlockSpec(shape, lambda *_: (0,) * nd)


def _rows(tr, w, cb=0):
    return pl.BlockSpec((tr, w), lambda i: (i, cb))


class _Comm:
    def __init__(self, ins, out_shapes, sems, start, finish, aliases=None):
        self.ins, self.out_shapes, self.sems, self.start, self.finish = list(ins), list(out_shapes), list(sems), start, finish
        self.aliases = dict(aliases or {})


def _call(body, *, grid, in_specs, out_specs, out_shape, scratch_shapes=(), sem, name, args, comm=None):
    if comm is None:
        outs = pl.pallas_call(body, grid=grid, in_specs=list(in_specs), out_specs=list(out_specs),
                              out_shape=list(out_shape), scratch_shapes=list(scratch_shapes),
                              compiler_params=_params(sem), name=name)(*args)
        return list(outs), []
    n_in, n_out, n_sc = len(in_specs), len(out_shape), len(scratch_shapes)
    nci, nco = len(comm.ins), len(comm.out_shapes)

    def hosted(*refs):
        ins, refs = refs[:n_in], refs[n_in:]
        cins, refs = refs[:nci], refs[nci:]
        outs, refs = refs[:n_out], refs[n_out:]
        couts, refs = refs[:nco], refs[nco:]
        scratch, csems = refs[:n_sc], refs[n_sc:]
        ids = [pl.program_id(i) for i in range(len(grid))]
        first = functools.reduce(jnp.logical_and, [i == 0 for i in ids])
        last = functools.reduce(jnp.logical_and, [i == g - 1 for i, g in zip(ids, grid)])

        @pl.when(first)
        def _():
            comm.start(cins, couts, csems)

        body(*ins, *outs, *scratch)

        @pl.when(last)
        def _():
            comm.finish(cins, couts, csems)

    any_spec = pl.BlockSpec(memory_space=pl.ANY)
    res = pl.pallas_call(
        hosted, grid=grid, in_specs=list(in_specs) + [any_spec] * nci, out_specs=list(out_specs) + [any_spec] * nco,
        out_shape=list(out_shape) + comm.out_shapes, scratch_shapes=list(scratch_shapes) + comm.sems,
        input_output_aliases={n_in + i: n_out + o for i, o in comm.aliases.items()},
        compiler_params=pltpu.CompilerParams(dimension_semantics=("arbitrary",) * len(grid),
                                             vmem_limit_bytes=VMEM_LIMIT, has_side_effects=True),
        name=name)(*args, *comm.ins)
    return list(res[:n_out]), list(res[n_out:])


def _mm(a, b, *, ta=False, tb=False, add=None, out_dtype=F32, tm=None, tn=None, tk=None, name, comm=None,
        slab=None, a_slab0=0):
    sq = None
    if ta:
        tm, tn = tm or MM_TILE_TA[0], tn or MM_TILE_TA[1]
    if slab is None:
        m, k = (a.shape[1], a.shape[0]) if ta else a.shape
        n = b.shape[0] if tb else b.shape[1]
        assert k == (b.shape[1] if tb else b.shape[0]), (a.shape, b.shape, ta, tb)
        tm, tn, tk = _pick(m, tm or MM_TILE[0]), _pick(n, tn or MM_TILE[1]), _pick(k, tk or MM_TILE[2])
        grid = (m // tm, n // tn, k // tk)
        a_spec = pl.BlockSpec((tk, tm), lambda i, j, kk: (kk, i)) if ta else pl.BlockSpec((tm, tk), lambda i, j, kk: (i, kk))
        b_spec = pl.BlockSpec((tn, tk), lambda i, j, kk: (j, kk)) if tb else pl.BlockSpec((tk, tn), lambda i, j, kk: (kk, j))
        o_spec, o_shape = pl.BlockSpec((tm, tn), lambda i, j, kk: (i, j)), (m, n)
    elif slab == "n":
        m, k = (a.shape[1], a.shape[0]) if ta else a.shape
        s, c = b.shape[0], (b.shape[1] if tb else b.shape[2])
        assert k == (b.shape[2] if tb else b.shape[1]), (a.shape, b.shape, ta, tb)
        tm, tn, tk = _pick(m, tm or MM_TILE[0]), c, _pick(k, tk or MM_TILE[2])
        grid = (m // tm, s, k // tk)
        a_spec = pl.BlockSpec((tk, tm), lambda i, j, kk: (kk, i)) if ta else pl.BlockSpec((tm, tk), lambda i, j, kk: (i, kk))
        b_spec = (pl.BlockSpec((sq, c, tk), lambda i, j, kk: (j, 0, kk)) if tb
                  else pl.BlockSpec((sq, tk, c), lambda i, j, kk: (j, kk, 0)))
        o_spec, o_shape = pl.BlockSpec((sq, tm, c), lambda i, j, kk: (j, i, 0)), (s, m, c)
    elif slab == "m":
        assert ta and not tb
        s, k, c = a.shape
        n = b.shape[1]
        assert k == b.shape[0], (a.shape, b.shape)
        tm, tn, tk = c, _pick(n, tn or MM_TILE[1]), _pick(k, tk or MM_TILE[2])
        grid = (s, n // tn, k // tk)
        a_spec = pl.BlockSpec((sq, tk, c), lambda i, j, kk: (i, kk, 0))
        b_spec = pl.BlockSpec((tk, tn), lambda i, j, kk: (kk, j))
        o_spec, o_shape = pl.BlockSpec((sq, c, tn), lambda i, j, kk: (i, 0, j)), (s, c, n)
    else:
        assert slab == "k" and not ta
        s, c = b.shape[0], (b.shape[2] if tb else b.shape[1])
        m, n = a.shape[1], (b.shape[1] if tb else b.shape[2])
        assert a.shape[2] == c and a.shape[0] >= a_slab0 + s, (a.shape, b.shape, a_slab0)
        tm, tn, tk = _pick(m, tm or MM_TILE[0]), _pick(n, tn or MM_TILE[1]), c
        per_step = SLABS_PER_STEP if (s % SLABS_PER_STEP == 0 and a_slab0 % SLABS_PER_STEP == 0) else 1
        first = a_slab0 // per_step
        grid = (m // tm, n // tn, s // per_step)
        a_spec = pl.BlockSpec((per_step, tm, c), lambda i, j, kk: (kk + first, i, 0))
        b_spec = (pl.BlockSpec((per_step, tn, c), lambda i, j, kk: (kk, j, 0)) if tb
                  else pl.BlockSpec((per_step, c, tn), lambda i, j, kk: (kk, 0, j)))
        o_spec, o_shape = pl.BlockSpec((tm, tn), lambda i, j, kk: (i, j)), (m, n)
    nk = grid[2]
    dims = (((0 if ta else 1,), (1 if tb else 0,)), ((), ()))

    def product(a_ref, b_ref):
        if slab != "k":
            return lax.dot_general(a_ref[...].astype(BF16), b_ref[...].astype(BF16), dims, preferred_element_type=F32)
        r = None
        for u in range(a_ref.shape[0]):
            p = lax.dot_general(a_ref[u].astype(BF16), b_ref[u].astype(BF16), dims, preferred_element_type=F32)
            r = p if r is None else r + p
        return r

    def body(*refs):
        a_ref, b_ref = refs[:2]
        add_ref = refs[2] if add is not None else None
        o_ref = refs[3] if add is not None else refs[2]
        acc_ref = refs[-1] if nk > 1 else None

        def finish(r):
            if add_ref is not None:
                r = r + add_ref[...].astype(F32)
            o_ref[...] = r.astype(o_ref.dtype)

        if nk == 1:
            finish(product(a_ref, b_ref))
            return
        kk = pl.program_id(2)

        @pl.when(kk == 0)
        def _():
            acc_ref[...] = product(a_ref, b_ref)

        if nk > 2:
            @pl.when(jnp.logical_and(kk > 0, kk < nk - 1))
            def _():
                acc_ref[...] += product(a_ref, b_ref)

        @pl.when(kk == nk - 1)
        def _():
            finish(acc_ref[...] + product(a_ref, b_ref))

    in_specs = [a_spec, b_spec] + ([o_spec] if add is not None else [])
    args = (a, b) + ((add,) if add is not None else ())
    outs, comm_outs = _call(
        body, grid=grid, in_specs=in_specs, out_specs=[o_spec],
        out_shape=[jax.ShapeDtypeStruct(o_shape, out_dtype)],
        scratch_shapes=[pltpu.VMEM((tm, tn), F32)] if nk > 1 else [],
        sem=("parallel", "parallel", "arbitrary"), name=name, args=args, comm=comm)
    return outs[0] if comm is None else (outs[0], comm_outs)


def _rms_scale(x):
    return lax.rsqrt(jnp.mean(x * x, axis=-1, keepdims=True) + RMS_EPS)


def _rms_bwd(xhat, r, g, dy):
    t = dy * g
    dx = r * (t - xhat * jnp.mean(t * xhat, axis=-1, keepdims=True))
    return dx, dy * xhat


_GELU_C = math.sqrt(2.0 / math.pi)


def _gelu(x):
    return x * (0.5 * (1.0 + jnp.tanh(_GELU_C * (x + 0.044715 * (x * x * x)))))


def _gelu_and_grad(x):
    t = jnp.tanh(_GELU_C * (x + 0.044715 * (x * x * x)))
    cdf = 0.5 * (1.0 + t)
    return x * cdf, cdf + x * (0.5 * (1.0 - t * t) * (_GELU_C * (1.0 + 3.0 * 0.044715 * (x * x))))


def _sigmoid(x):
    return 1.0 / (1.0 + jnp.exp(-x))


def _swap_halves(x):
    lane = lax.broadcasted_iota(jnp.int32, x.shape, 1)
    first = (lane % QK_ROPE) < (QK_ROPE // 2)
    return jnp.where(first, pltpu.roll(x, LANES - QK_ROPE // 2, 1), pltpu.roll(x, QK_ROPE // 2, 1))


def _rope(x, cos, sin_signed):
    return x * cos + _swap_halves(x) * sin_signed


def _rope_bwd(d, cos, sin_signed):
    return d * cos + _swap_halves(d * sin_signed)


def _rope_tables(pos_col, inv_freq_row, sign_row):
    t = pos_col.shape[0]
    tr = _pick(t, 512, SUBLANES)

    def body(p_ref, f_ref, s_ref, cos_ref, sin_ref):
        ang = p_ref[...].astype(F32) * f_ref[...]
        cos_ref[...] = jnp.cos(ang)
        sin_ref[...] = jnp.sin(ang) * s_ref[...]

    return pl.pallas_call(
        body, grid=(t // tr,), in_specs=[_rows(tr, 1), _full((1, LANES)), _full((1, LANES))],
        out_specs=[_rows(tr, LANES), _rows(tr, LANES)],
        out_shape=[jax.ShapeDtypeStruct((t, LANES), F32)] * 2,
        compiler_params=_params(("parallel",)), name="rope_tables")(pos_col, inv_freq_row, sign_row)


def _norm_fwd(x, g, name):
    t, d = x.shape
    tr = _row_tile(t, d, 2)

    def body(x_ref, g_ref, y_ref):
        xv = x_ref[...]
        y_ref[...] = (xv * _rms_scale(xv) * g_ref[...]).astype(BF16)

    return pl.pallas_call(
        body, grid=(t // tr,), in_specs=[_rows(tr, d), _full((1, d))], out_specs=_rows(tr, d),
        out_shape=jax.ShapeDtypeStruct((t, d), BF16), compiler_params=_params(("parallel",)), name=name)(x, g)


def _lat_fwd(z_lat, qg, kvg, cos, sin, ql, kvl):
    t = z_lat.shape[0]
    tr = _row_tile(t, z_lat.shape[1], 2)

    def body(z_ref, qg_ref, kvg_ref, cos_ref, sin_ref, qn_ref, kvn_ref, kpe_ref):
        q = z_ref[:, 0:ql]
        qn_ref[...] = (q * _rms_scale(q) * qg_ref[...]).astype(BF16)
        kv = z_ref[:, ql:ql + kvl]
        kvn_ref[...] = (kv * _rms_scale(kv) * kvg_ref[...]).astype(BF16)
        kpe_ref[...] = _rope(z_ref[:, ql + kvl:ql + kvl + LANES], cos_ref[...], sin_ref[...]).astype(BF16)

    w = z_lat.shape[1]
    return pl.pallas_call(
        body, grid=(t // tr,),
        in_specs=[_rows(tr, w), _full((1, ql)), _full((1, kvl)), _rows(tr, LANES), _rows(tr, LANES)],
        out_specs=[_rows(tr, ql), _rows(tr, kvl), _rows(tr, LANES)],
        out_shape=[jax.ShapeDtypeStruct((t, ql), BF16), jax.ShapeDtypeStruct((t, kvl), BF16),
                   jax.ShapeDtypeStruct((t, LANES), BF16)],
        compiler_params=_params(("parallel",)), name="lat_fwd")(z_lat, qg, kvg, cos, sin)


def _q_rope(q_p, cos, sin, bwd, name):
    t, w = q_p.shape
    tr = _row_tile(t, w, 2)
    fn = _rope_bwd if bwd else _rope

    def body(q_ref, cos_ref, sin_ref, o_ref):
        c, s = cos_ref[...], sin_ref[...]
        for h in range(w // HEAD_PAD):
            o_ref[:, h * HEAD_PAD:h * HEAD_PAD + QK_NOPE] = q_ref[:, h * HEAD_PAD:h * HEAD_PAD + QK_NOPE].astype(BF16)
            lo = h * HEAD_PAD + QK_NOPE
            o_ref[:, lo:lo + LANES] = fn(q_ref[:, lo:lo + LANES].astype(F32), c, s).astype(BF16)

    return pl.pallas_call(
        body, grid=(t // tr,), in_specs=[_rows(tr, w), _rows(tr, LANES), _rows(tr, LANES)], out_specs=_rows(tr, w),
        out_shape=jax.ShapeDtypeStruct((t, w), BF16), compiler_params=_params(("parallel",)), name=name)(q_p, cos, sin)


def _tril_mask():
    r = lax.broadcasted_iota(jnp.int32, (CHUNK, CHUNK), 0)
    c = lax.broadcasted_iota(jnp.int32, (CHUNK, CHUNK), 1)
    return r >= c


def _sgu_fwd(z_uv, gs, ws, b_col):
    t = z_uv.shape[0]
    sw = z_uv.shape[1] // 2
    groups = sw // SGU_GROUP
    tr = _pick(t, 256, CHUNK)

    def body(u_ref, v_ref, gs_ref, ws_ref, b_ref, o_ref):
        v = _gelu(v_ref[...])
        vn = (v * _rms_scale(v) * gs_ref[...]).astype(BF16)
        tri = _tril_mask()
        for g in range(groups):
            wg = jnp.where(tri, ws_ref[g], 0.0).astype(BF16)
            cols = slice(g * SGU_GROUP, (g + 1) * SGU_GROUP)
            for c in range(tr // CHUNK):
                rows = slice(c * CHUNK, (c + 1) * CHUNK)
                mixed = jnp.dot(wg, vn[rows, cols], preferred_element_type=F32) + b_ref[g]
                o_ref[rows, cols] = (_gelu(u_ref[rows, cols]) * mixed).astype(BF16)

    return pl.pallas_call(
        body, grid=(t // tr,),
        in_specs=[_rows(tr, sw, 0), _rows(tr, sw, 1), _full((1, sw)), _full(ws.shape), _full(b_col.shape)],
        out_specs=_rows(tr, sw), out_shape=jax.ShapeDtypeStruct((t, sw), BF16),
        compiler_params=_params(("parallel",)), name="sgu_fwd")(z_uv, z_uv, gs, ws, b_col)


def _merge_fwd(y_attn, y_sgu, z_g, b_gate, comm=None):
    t, d = y_attn.shape
    tr = _row_tile(t, d, 5)

    def body(ya_ref, ys_ref, g0_ref, g1_ref, b0_ref, b1_ref, o_ref):
        g0 = _sigmoid(g0_ref[...] + b0_ref[...])
        g1 = _sigmoid(g1_ref[...] + b1_ref[...])
        o_ref[...] = (g0 * ya_ref[...] + g1 * ys_ref[...]).astype(BF16)

    bspec0 = pl.BlockSpec((1, d), lambda i: (0, 0))
    bspec1 = pl.BlockSpec((1, d), lambda i: (0, 1))
    outs, comm_outs = _call(
        body, grid=(t // tr,),
        in_specs=[_rows(tr, d), _rows(tr, d), _rows(tr, d, 0), _rows(tr, d, 1), bspec0, bspec1],
        out_specs=[_rows(tr, d)], out_shape=[jax.ShapeDtypeStruct((t, d), BF16)],
        sem=("parallel",), name="merge_fwd", args=(y_attn, y_sgu, z_g, z_g, b_gate, b_gate), comm=comm)
    return outs[0], comm_outs


def _swiglu_fwd(gate, up, comm=None):
    t, f = gate.shape
    tr = _row_tile(t, f, 3)

    def body(g_ref, u_ref, o_ref):
        g = g_ref[...]
        o_ref[...] = (g * _sigmoid(g) * u_ref[...]).astype(BF16)

    outs, comm_outs = _call(
        body, grid=(t // tr,), in_specs=[_rows(tr, f), _rows(tr, f)], out_specs=[_rows(tr, f)],
        out_shape=[jax.ShapeDtypeStruct((t, f), BF16)], sem=("parallel",), name="swiglu_fwd", args=(gate, up), comm=comm)
    return outs[0], comm_outs


def _loss_head(h2, g, target):
    t, d = h2.shape
    tr = _row_tile(t, d, 3)

    def body(h_ref, g_ref, t_ref, loss_ref, dh_ref, dhb_ref, dg_ref):
        @pl.when(pl.program_id(0) == 0)
        def _():
            loss_ref[...] = jnp.zeros_like(loss_ref)
            dg_ref[...] = jnp.zeros_like(dg_ref)

        h = h_ref[...]
        r = _rms_scale(h)
        hhat = h * r
        gv = g_ref[...]
        err = hhat * gv - t_ref[...]
        loss_ref[...] += jnp.full(loss_ref.shape, 0.5 * jnp.sum(jnp.mean(err * err, axis=-1)), F32)
        dx, dg_rows = _rms_bwd(hhat, r, gv, err * (1.0 / d))
        dh_ref[...] = dx
        dhb_ref[...] = dx.astype(BF16)
        dg_ref[...] += jnp.sum(dg_rows, axis=0, keepdims=True)

    return pl.pallas_call(
        body, grid=(t // tr,), in_specs=[_rows(tr, d), _full((1, d)), _rows(tr, d)],
        out_specs=[_full((1, LANES)), _rows(tr, d), _rows(tr, d), _full((1, d))],
        out_shape=[jax.ShapeDtypeStruct((1, LANES), F32), jax.ShapeDtypeStruct((t, d), F32),
                   jax.ShapeDtypeStruct((t, d), BF16), jax.ShapeDtypeStruct((1, d), F32)],
        compiler_params=_params(("arbitrary",)), name="loss_head")(h2, g, target)


def _swiglu_bwd(gate, up, dact):
    t, f = gate.shape
    tr = _row_tile(t, f, 4)

    def body(g_ref, u_ref, d_ref, dgu_ref):
        g = g_ref[...]
        s = _sigmoid(g)
        d = d_ref[...]
        dgu_ref[0] = (d * u_ref[...] * (s * (1.0 + g * (1.0 - s)))).astype(BF16)
        dgu_ref[1] = (d * (g * s)).astype(BF16)

    return pl.pallas_call(
        body, grid=(t // tr,), in_specs=[_rows(tr, f)] * 3, out_specs=pl.BlockSpec((2, tr, f), lambda i: (0, i, 0)),
        out_shape=jax.ShapeDtypeStruct((2, t, f), BF16),
        compiler_params=_params(("parallel",)), name="swiglu_bwd")(gate, up, dact)


def _norm_bwd(x, g, dy, resid, name, comm=None):
    t, d = x.shape
    tr = _row_tile(t, d, 5)

    def body(x_ref, g_ref, dy_ref, r_ref, dx_ref, dxb_ref, dg_ref):
        @pl.when(pl.program_id(0) == 0)
        def _():
            dg_ref[...] = jnp.zeros_like(dg_ref)

        xv = x_ref[...]
        r = _rms_scale(xv)
        dx, dg_rows = _rms_bwd(xv * r, r, g_ref[...], dy_ref[...])
        dx = r_ref[...] + dx
        dx_ref[...] = dx
        dxb_ref[...] = dx.astype(BF16)
        dg_ref[...] += jnp.sum(dg_rows, axis=0, keepdims=True)

    outs, comm_outs = _call(
        body, grid=(t // tr,), in_specs=[_rows(tr, d), _full((1, d)), _rows(tr, d), _rows(tr, d)],
        out_specs=[_rows(tr, d), _rows(tr, d), _full((1, d))],
        out_shape=[jax.ShapeDtypeStruct((t, d), F32), jax.ShapeDtypeStruct((t, d), BF16),
                   jax.ShapeDtypeStruct((1, d), F32)],
        sem=("arbitrary",), name=name, args=(x, g, dy, resid), comm=comm)
    return (outs[0], outs[1], outs[2]) if comm is None else (outs[0], outs[1], outs[2], comm_outs)


def _merge_bwd(dmerged, y_attn, y_sgu, z_g, b_gate):
    t, d = y_attn.shape
    tr = _row_tile(t, d, 7)

    def body(dm_ref, ya_ref, ys_ref, g0_ref, g1_ref, b0_ref, b1_ref, dya_ref, dys_ref, dz_ref, db_ref):
        @pl.when(pl.program_id(0) == 0)
        def _():
            db_ref[...] = jnp.zeros_like(db_ref)

        dm = dm_ref[...]
        g0 = _sigmoid(g0_ref[...] + b0_ref[...])
        g1 = _sigmoid(g1_ref[...] + b1_ref[...])
        dya_ref[...] = (dm * g0).astype(BF16)
        dys_ref[...] = (dm * g1).astype(BF16)
        dl0 = dm * ya_ref[...] * (g0 * (1.0 - g0))
        dl1 = dm * ys_ref[...] * (g1 * (1.0 - g1))
        dz_ref[:, 0:d] = dl0.astype(BF16)
        dz_ref[:, d:2 * d] = dl1.astype(BF16)
        db_ref[:, 0:d] += jnp.sum(dl0, axis=0, keepdims=True)
        db_ref[:, d:2 * d] += jnp.sum(dl1, axis=0, keepdims=True)

    bspec0 = pl.BlockSpec((1, d), lambda i: (0, 0))
    bspec1 = pl.BlockSpec((1, d), lambda i: (0, 1))
    return pl.pallas_call(
        body, grid=(t // tr,),
        in_specs=[_rows(tr, d), _rows(tr, d), _rows(tr, d), _rows(tr, d, 0), _rows(tr, d, 1), bspec0, bspec1],
        out_specs=[_rows(tr, d), _rows(tr, d), _rows(tr, 2 * d), _full((1, 2 * d))],
        out_shape=[jax.ShapeDtypeStruct((t, d), BF16), jax.ShapeDtypeStruct((t, d), BF16),
                   jax.ShapeDtypeStruct((t, 2 * d), BF16), jax.ShapeDtypeStruct((1, 2 * d), F32)],
        compiler_params=_params(("arbitrary",)), name="merge_bwd")(dmerged, y_attn, y_sgu, z_g, z_g, b_gate, b_gate)


def _sgu_bwd(z_uv, ds_out, gs, ws, b_col):
    t = z_uv.shape[0]
    sw = z_uv.shape[1] // 2
    groups = sw // SGU_GROUP
    tr = _pick(t, 256, CHUNK)

    def body(u_ref, v_ref, d_ref, gs_ref, ws_ref, b_ref, dz_ref, dws_ref, db_ref, dgs_ref, dvn_ref):
        @pl.when(pl.program_id(0) == 0)
        def _():
            dws_ref[...] = jnp.zeros_like(dws_ref)
            db_ref[...] = jnp.zeros_like(db_ref)
            dgs_ref[...] = jnp.zeros_like(dgs_ref)

        v, dgelu_v = _gelu_and_grad(v_ref[...])
        r = _rms_scale(v)
        vhat = v * r
        gsv = gs_ref[...]
        vn = (vhat * gsv).astype(BF16)
        tri = _tril_mask()
        for g in range(groups):
            wg = jnp.where(tri, ws_ref[g], 0.0).astype(BF16)
            cols = slice(g * SGU_GROUP, (g + 1) * SGU_GROUP)
            for c in range(tr // CHUNK):
                rows = slice(c * CHUNK, (c + 1) * CHUNK)
                vn_cg = vn[rows, cols]
                mixed = jnp.dot(wg, vn_cg, preferred_element_type=F32) + b_ref[g]
                u, dgelu_u = _gelu_and_grad(u_ref[rows, cols])
                dso = d_ref[rows, cols]
                dz_ref[rows, cols] = (dso * mixed * dgelu_u).astype(BF16)
                dmixed = dso * u
                db_ref[g] += jnp.sum(dmixed, axis=1, keepdims=True)
                dmixed_b = dmixed.astype(BF16)
                dws_ref[g] += jnp.where(
                    tri, lax.dot_general(dmixed_b, vn_cg, (((1,), (1,)), ((), ())), preferred_element_type=F32), 0.0)
                dvn_ref[rows, cols] = lax.dot_general(wg, dmixed_b, (((0,), (0,)), ((), ())), preferred_element_type=F32)
        dvn = dvn_ref[...]
        dv, dgs_rows = _rms_bwd(vhat, r, gsv, dvn)
        dz_ref[:, sw:2 * sw] = (dv * dgelu_v).astype(BF16)
        dgs_ref[...] += jnp.sum(dgs_rows, axis=0, keepdims=True)

    return pl.pallas_call(
        body, grid=(t // tr,),
        in_specs=[_rows(tr, sw, 0), _rows(tr, sw, 1), _rows(tr, sw), _full((1, sw)), _full(ws.shape), _full(b_col.shape)],
        out_specs=[_rows(tr, 2 * sw), _full(ws.shape), _full(b_col.shape), _full((1, sw))],
        out_shape=[jax.ShapeDtypeStruct((t, 2 * sw), BF16), jax.ShapeDtypeStruct(ws.shape, F32),
                   jax.ShapeDtypeStruct(b_col.shape, F32), jax.ShapeDtypeStruct((1, sw), F32)],
        scratch_shapes=[pltpu.VMEM((tr, sw), F32)],
        compiler_params=_params(("arbitrary",)), name="sgu_bwd")(z_uv, z_uv, ds_out, gs, ws, b_col)


def _lat_bwd(z_lat, qg, kvg, dqn, dkvn, dkpe_heads, cos, sin, ql, kvl):
    t, w = z_lat.shape
    heads = dkpe_heads.shape[0]
    tr = _row_tile(t, w + heads * LANES, 3)

    def body(z_ref, qg_ref, kvg_ref, dq_ref, dkv_ref, dk_ref, cos_ref, sin_ref, dz_ref, dqg_ref, dkvg_ref):
        @pl.when(pl.program_id(0) == 0)
        def _():
            dqg_ref[...] = jnp.zeros_like(dqg_ref)
            dkvg_ref[...] = jnp.zeros_like(dkvg_ref)

        q = z_ref[:, 0:ql]
        r = _rms_scale(q)
        dx, dg_rows = _rms_bwd(q * r, r, qg_ref[...], dq_ref[...])
        dz_ref[:, 0:ql] = dx.astype(BF16)
        dqg_ref[...] += jnp.sum(dg_rows, axis=0, keepdims=True)
        kv = z_ref[:, ql:ql + kvl]
        r = _rms_scale(kv)
        dx, dg_rows = _rms_bwd(kv * r, r, kvg_ref[...], dkv_ref[...])
        dz_ref[:, ql:ql + kvl] = dx.astype(BF16)
        dkvg_ref[...] += jnp.sum(dg_rows, axis=0, keepdims=True)
        dk = dk_ref[0]
        for h in range(1, heads):
            dk = dk + dk_ref[h]
        dz_ref[:, ql + kvl:ql + kvl + LANES] = _rope_bwd(dk, cos_ref[...], sin_ref[...]).astype(BF16)

    return pl.pallas_call(
        body, grid=(t // tr,),
        in_specs=[_rows(tr, w), _full((1, ql)), _full((1, kvl)), _rows(tr, ql), _rows(tr, kvl),
                  pl.BlockSpec((heads, tr, LANES), lambda i: (0, i, 0)), _rows(tr, LANES), _rows(tr, LANES)],
        out_specs=[_rows(tr, w), _full((1, ql)), _full((1, kvl))],
        out_shape=[jax.ShapeDtypeStruct((t, w), BF16), jax.ShapeDtypeStruct((1, ql), F32),
                   jax.ShapeDtypeStruct((1, kvl), F32)],
        compiler_params=_params(("arbitrary",)), name="lat_bwd")(z_lat, qg, kvg, dqn, dkvn, dkpe_heads, cos, sin)


_NT = (((1,), (1,)), ((), ()))


def _attn_scale():
    return (QK_NOPE + QK_ROPE) ** -0.5


def _heads_per_step(heads):
    return HEADS_PER_STEP if heads % HEADS_PER_STEP == 0 else 1


def _attn_fwd(q_c, kv, kpe, comm=None):
    t = q_c.shape[0]
    heads = q_c.shape[1] // HEAD_PAD
    tq = _pick(t, ATTN_TILE)
    nq = t // tq
    scale = _attn_scale()
    to_log2 = scale * math.log2(math.e)
    tn_dims = (((0,), (0,)), ((), ()))

    hps = _heads_per_step(heads)

    def body(q_ref, kv_ref, kpe_ref, o_ref, ob_ref, lse_ref, m_sc, l_sc, acc_sc):
        qi, ki = pl.program_id(1), pl.program_id(2)

        @pl.when(ki == 0)
        def _():
            m_sc[...] = jnp.full_like(m_sc, NEG_BIG)
            l_sc[...] = jnp.zeros_like(l_sc)
            acc_sc[...] = jnp.zeros_like(acc_sc)

        def step(diagonal):
            for u in range(hps):
                lo = u * HEAD_PAD
                kc = jnp.concatenate([kv_ref[:, lo:lo + QK_NOPE], kpe_ref[...]], axis=1)
                st = lax.dot_general(kc, q_ref[:, lo:lo + HEAD_PAD], _NT, preferred_element_type=F32)
                if diagonal:
                    krow = lax.broadcasted_iota(jnp.int32, st.shape, 0)
                    qcol = lax.broadcasted_iota(jnp.int32, st.shape, 1)
                    st = jnp.where(qcol >= krow, st, NEG_BIG)
                m_prev = m_sc[u]
                m_new = jnp.maximum(m_prev, jnp.max(st, axis=0, keepdims=True))
                alpha = jnp.exp2((m_prev - m_new) * to_log2)
                pt = jnp.exp2((st - m_new) * to_log2)
                l_sc[u] = alpha * l_sc[u] + jnp.sum(pt, axis=0, keepdims=True)
                acc_sc[u] = alpha * acc_sc[u] + lax.dot_general(
                    kv_ref[:, lo + QK_NOPE:lo + HEAD_PAD], pt.astype(BF16), tn_dims, preferred_element_type=F32)
                m_sc[u] = m_new

        @pl.when(ki < qi)
        def _():
            step(False)

        @pl.when(ki == qi)
        def _():
            step(True)
            for u in range(hps):
                o = (acc_sc[u] / l_sc[u]).T
                o_ref[:, u * V_HEAD:(u + 1) * V_HEAD] = o
                ob_ref[:, u * V_HEAD:(u + 1) * V_HEAD] = o.astype(BF16)
                lse_ref[u] = m_sc[u] * scale + jnp.log(l_sc[u])

    omap = lambda g, qi, ki: (qi, g)
    outs, comm_outs = _call(
        body, grid=(heads // hps, nq, nq),
        in_specs=[pl.BlockSpec((tq, hps * HEAD_PAD), omap),
                  pl.BlockSpec((tq, hps * HEAD_PAD), lambda g, qi, ki: (jnp.minimum(ki, qi), g)),
                  pl.BlockSpec((tq, LANES), lambda g, qi, ki: (jnp.minimum(ki, qi), 0))],
        out_specs=[pl.BlockSpec((tq, hps * V_HEAD), omap), pl.BlockSpec((tq, hps * V_HEAD), omap),
                   pl.BlockSpec((hps, 1, tq), lambda g, qi, ki: (g, 0, qi))],
        out_shape=[jax.ShapeDtypeStruct((t, heads * V_HEAD), F32), jax.ShapeDtypeStruct((t, heads * V_HEAD), BF16),
                   jax.ShapeDtypeStruct((heads, 1, t), F32)],
        scratch_shapes=[pltpu.VMEM((hps, 1, tq), F32), pltpu.VMEM((hps, 1, tq), F32),
                        pltpu.VMEM((hps, V_HEAD, tq), F32)],
        sem=("parallel", "parallel", "arbitrary"), name="attn_fwd", args=(q_c, kv, kpe), comm=comm)
    return outs[0], outs[1], outs[2], comm_outs


def _attn_bwd(q_c, kv, kpe, o, do, lse_row, comm=None):
    t = q_c.shape[0]
    heads = q_c.shape[1] // HEAD_PAD
    tk = _pick(t, ATTN_TILE)
    nk = t // tk
    scale = _attn_scale()
    tn_dims = (((0,), (0,)), ((), ()))

    hps = _heads_per_step(heads)

    def body(q_ref, kv_ref, kpe_ref, do_ref, lse_ref, o_ref, dq_ref, dkv_ref, dkpe_ref, dk_sc, dv_sc, delta_sc):
        ki, qi = pl.program_id(1), pl.program_id(2)

        @pl.when(jnp.logical_and(ki == 0, qi == 0))
        def _():
            dq_ref[...] = jnp.zeros_like(dq_ref)

        @pl.when(qi == 0)
        def _():
            dk_sc[...] = jnp.zeros_like(dk_sc)
            dv_sc[...] = jnp.zeros_like(dv_sc)

        @pl.when(ki == 0)
        def _():
            for u in range(hps):
                cols = slice(u * V_HEAD, (u + 1) * V_HEAD)
                delta_sc[qi * hps + u] = jnp.sum((do_ref[:, cols] * o_ref[:, cols]).T, axis=0, keepdims=True)

        def step(diagonal):
            for u in range(hps):
                lo = u * HEAD_PAD
                kc = jnp.concatenate([kv_ref[:, lo:lo + QK_NOPE], kpe_ref[...]], axis=1)
                q = q_ref[:, lo:lo + HEAD_PAD]
                st = lax.dot_general(kc, q, _NT, preferred_element_type=F32) * scale
                pt = jnp.exp(st - lse_ref[u])
                if diagonal:
                    krow = lax.broadcasted_iota(jnp.int32, st.shape, 0)
                    qcol = lax.broadcasted_iota(jnp.int32, st.shape, 1)
                    pt = jnp.where(qcol >= krow, pt, 0.0)
                do_b = do_ref[:, u * V_HEAD:(u + 1) * V_HEAD].astype(BF16)
                dv_sc[u] += jnp.dot(pt.astype(BF16), do_b, preferred_element_type=F32)
                dpt = lax.dot_general(kv_ref[:, lo + QK_NOPE:lo + HEAD_PAD], do_b, _NT, preferred_element_type=F32)
                dst = (pt * (dpt - delta_sc[qi * hps + u]) * scale).astype(BF16)
                dk_sc[u] += jnp.dot(dst, q, preferred_element_type=F32)
                rows = pl.ds(pl.multiple_of(qi * tk, tk), tk)
                dq_ref[rows, lo:lo + HEAD_PAD] += lax.dot_general(dst, kc, tn_dims, preferred_element_type=F32)

        @pl.when(qi > ki)
        def _():
            step(False)

        @pl.when(qi == ki)
        def _():
            step(True)

        @pl.when(qi == nk - 1)
        def _():
            for u in range(hps):
                lo = u * HEAD_PAD
                dkv_ref[:, lo:lo + QK_NOPE] = dk_sc[u, :, 0:QK_NOPE].astype(BF16)
                dkv_ref[:, lo + QK_NOPE:lo + HEAD_PAD] = dv_sc[u].astype(BF16)
                dkpe_ref[u] = dk_sc[u, :, QK_NOPE:QK_NOPE + LANES]

    qclamp = lambda g, ki, qi: (jnp.maximum(qi, ki), g)
    outs, comm_outs = _call(
        body, grid=(heads // hps, nk, nk),
        in_specs=[pl.BlockSpec((tk, hps * HEAD_PAD), qclamp),
                  pl.BlockSpec((tk, hps * HEAD_PAD), lambda g, ki, qi: (ki, g)),
                  pl.BlockSpec((tk, LANES), lambda g, ki, qi: (ki, 0)),
                  pl.BlockSpec((tk, hps * V_HEAD), qclamp),
                  pl.BlockSpec((hps, 1, tk), lambda g, ki, qi: (g, 0, jnp.maximum(qi, ki))),
                  pl.BlockSpec((tk, hps * V_HEAD), lambda g, ki, qi: (jnp.where(ki == 0, qi, 0), g))],
        out_specs=[pl.BlockSpec((t, hps * HEAD_PAD), lambda g, ki, qi: (0, g)),
                   pl.BlockSpec((tk, hps * HEAD_PAD), lambda g, ki, qi: (ki, g)),
                   pl.BlockSpec((hps, tk, LANES), lambda g, ki, qi: (g, ki, 0))],
        out_shape=[jax.ShapeDtypeStruct((t, heads * HEAD_PAD), F32),
                   jax.ShapeDtypeStruct((t, heads * HEAD_PAD), BF16), jax.ShapeDtypeStruct((heads, t, LANES), F32)],
        scratch_shapes=[pltpu.VMEM((hps, tk, HEAD_PAD), F32), pltpu.VMEM((hps, tk, V_HEAD), F32),
                        pltpu.VMEM((nk * hps, 1, tk), F32)],
        sem=("parallel", "arbitrary", "arbitrary"), name="attn_bwd",
        args=(q_c, kv, kpe, do, lse_row, o), comm=comm)
    return outs[0], outs[1], outs[2], comm_outs


def _local_step(x, pos_col, target, small, shards, opt):
    t = x.shape[0]
    ql, kvl = small["q_norm_g"].shape[1], small["kv_norm_g"].shape[1]
    sw = small["sgu_norm_g"].shape[1]
    heads = (shards["w_uq"].shape[1] * N_DEV) // (QK_NOPE + QK_ROPE)
    big = {}
    early = ["w_in", "w_uq", "w_ukv"]
    big.update(_compute_layout(dict(zip(early, _all_gather([shards[k] for k in early]))), ql, kvl, heads, sw))
    half = QK_ROPE // 2
    lane = jnp.arange(LANES)
    inv_freq = ROPE_THETA ** (-jnp.arange(0, QK_ROPE, 2, dtype=F32) / QK_ROPE)
    inv_row = inv_freq[lane % half][None, :]
    sign_row = jnp.where((lane % QK_ROPE) < half, -1.0, 1.0).astype(F32)[None, :]
    cos, sin = _rope_tables(pos_col, inv_row, sign_row)
    ws = small["w_sgu"]
    b_col = small["b_sgu_col"]

    def arrived(names, bufs):
        big.update(_compute_layout(dict(zip(names, bufs)), ql, kvl, heads, sw))

    a = _norm_fwd(x, small["norm_mix_g"], "norm_mix_fwd")
    z_lat = _mm(a, big["w_lat_t"], tb=True, name="z_lat")
    z_uv, g_sgu = _mm(a, big["w_uv_t"], tb=True, name="z_uv", comm=_gather_stage(1, [shards["w_o_sgu"]]))
    z_g, (g_attn, g_sgu) = _mm(a, big["w_g_t"], tb=True, name="z_g",
                               comm=_join(_gather_stage(1, [shards["w_o_attn"]]), _gather_stage(2, g_sgu)))
    qn, kvn, kpe = _lat_fwd(z_lat, small["q_norm_g"], small["kv_norm_g"], cos, sin, ql, kvl)
    q_p, (g_attn, g_sgu) = _mm(qn, big["w_uq"], name="q_up",
                               comm=_join(_gather_stage(2, [g_attn]), _gather_stage(3, [g_sgu])))
    kv, (g_attn,) = _mm(kvn, big["w_ukv"], out_dtype=BF16, name="kv_up", comm=_gather_stage(3, [g_attn]))
    arrived(["w_o_sgu", "w_o_attn"], [g_sgu, g_attn])
    q_c = _q_rope(q_p, cos, sin, False, "q_rope")
    attn, attn_b, lse, (g_out, w_gate, w_up) = _attn_fwd(
        q_c, kv, kpe, comm=_gather_stage(1, [shards[k] for k in ("w_out", "w_gate_ffn", "w_up_ffn")]))
    s_out = _sgu_fwd(z_uv, small["sgu_norm_g"], ws, b_col)
    y_sgu, (g_out,) = _mm(s_out, big["w_o_sgu"], name="y_sgu", comm=_gather_stage(2, [g_out]))
    y_attn, (w_gate, g_out) = _mm(attn_b, big["w_o_attn"], name="y_attn",
                                  comm=_join(_gather_stage(2, [w_gate]), _gather_stage(3, [g_out])))
    arrived(["w_out"], [g_out])
    merged, (w_up, w_gate) = _merge_fwd(y_attn, y_sgu, z_g, small["b_gate"],
                                        comm=_join(_gather_stage(2, [w_up]), _gather_stage(3, [w_gate])))
    h1, (w_up,) = _mm(merged, big["w_out"], add=x, name="h1", comm=_gather_stage(3, [w_up]))
    f = _norm_fwd(h1, small["norm_ffn_g"], "norm_ffn_fwd")
    gate, w_down = _mm(f, w_gate, tb=True, slab="n", name="ffn_gate", comm=_gather_stage(1, [shards["w_down_ffn"]]))
    up, w_down = _mm(f, w_up, tb=True, slab="n", name="ffn_up", comm=_gather_stage(2, w_down))
    ffn = gate.shape[2]
    gate, up = gate.reshape(N_DEV * t, ffn), up.reshape(N_DEV * t, ffn)
    act, (w_down,) = _swiglu_fwd(gate, up, comm=_gather_stage(3, w_down))
    act = act.reshape(N_DEV, t, ffn)
    h2 = _mm(act, w_down, slab="k", add=h1, name="h2")
    loss_row, dh2, dh2_b, d_norm_final = _loss_head(h2, small["norm_final_g"], target)

    def pair_sums(names, slabs, bufs):
        return [_pair_sum(g, b, "pair_sum_" + k) for k, g, b in zip(names, slabs, bufs)]

    parts, updates = {}, {}

    def update(k, comm=None):
        w, m, v = opt[k]
        updates[k], got = _adamw_shard(parts[k], w, m, v, "adamw_" + k, comm=comm)
        return got

    down_slabs = [_mm(act, dh2_b, ta=True, slab="m", out_dtype=BF16, name="dw_down")]
    dact, bufs = _mm(dh2_b, w_down, tb=True, slab="n", name="dact", comm=_to_sibling(down_slabs))
    down_pair = pair_sums(["w_down_ffn"], down_slabs, bufs)
    dgu = _swiglu_bwd(gate, up, dact.reshape(N_DEV * t, ffn)).reshape(2 * N_DEV, t, ffn)
    dw_gu, got = _mm(dgu, f, ta=True, slab="m", out_dtype=BF16, name="dw_gate_up", comm=_to_chips(down_pair))
    parts["w_down_ffn"] = got[0]
    gu_names = ["w_gate_ffn", "w_up_ffn"]
    df, bufs = _mm(dgu, w_gate, slab="k", name="df_gate", comm=_to_sibling([dw_gu, dw_gu], first=[0, N_DEV]))
    gu_pairs = [_pair_sum(dw_gu, b, "pair_sum_" + k, first=s0) for k, b, s0 in zip(gu_names, bufs, [0, N_DEV])]
    df = _mm(dgu, w_up, slab="k", a_slab0=N_DEV, add=df, name="df_up")
    dh1, dh1_b, d_norm_ffn = _norm_bwd(h1, small["norm_ffn_g"], df, dh2, "norm_ffn_bwd")
    dw_out = _mm(merged, dh1_b, ta=True, out_dtype=BF16, name="dw_out")
    out_slabs = [_slabs_from_rows(dw_out)]
    dmerged, bufs = _mm(dh1_b, big["w_out"], tb=True, name="dmerged", comm=_to_sibling(out_slabs))
    out_pair = pair_sums(["w_out"], out_slabs, bufs)
    dy_attn, dy_sgu, dz_g, d_b_gate = _merge_bwd(dmerged, y_attn, y_sgu, z_g, small["b_gate"])
    dw_o_sgu = _mm(s_out, dy_sgu, ta=True, out_dtype=BF16, name="dw_o_sgu")
    ds_out = _mm(dy_sgu, big["w_o_sgu"], tb=True, name="ds_out")
    dz_uv, d_ws, d_b_col, d_sgu_norm = _sgu_bwd(z_uv, ds_out, small["sgu_norm_g"], ws, b_col)
    dw_o_attn = _mm(attn_b, dy_attn, ta=True, out_dtype=BF16, name="dw_o_attn")
    mix_names = ["w_o_sgu", "w_o_attn"]
    mix_slabs = [_slabs_from_cols(dw_o_sgu), _slabs_from_rows(dw_o_attn)]
    dattn, bufs = _mm(dy_attn, big["w_o_attn"], tb=True, name="dattn", comm=_to_sibling(mix_slabs))
    mix_pairs = pair_sums(mix_names, mix_slabs, bufs)
    dq_c, dkv, dkpe_heads, got = _attn_bwd(q_c, kv, kpe, attn, dattn, lse, comm=_to_chips(gu_pairs))
    parts.update(zip(gu_names, got))
    dq_p = _q_rope(dq_c, cos, sin, True, "q_rope_bwd")
    dw_uq = _mm(qn, dq_p, ta=True, out_dtype=BF16, name="dw_uq")
    dw_ukv = _mm(kvn, dkv, ta=True, out_dtype=BF16, name="dw_ukv")
    dqn = _mm(dq_p, big["w_uq"], tb=True, name="dqn")
    dkvn = _mm(dkv, big["w_ukv"], tb=True, name="dkvn")
    dz_lat, d_q_norm, d_kv_norm = _lat_bwd(z_lat, small["q_norm_g"], small["kv_norm_g"], dqn, dkvn, dkpe_heads,
                                           cos, sin, ql, kvl)
    dw_g, got = _mm(dz_g, a, ta=True, out_dtype=BF16, name="dw_g", comm=_to_chips(out_pair))
    parts["w_out"] = got[0]
    dw_uv, got = _mm(dz_uv, a, ta=True, out_dtype=BF16, name="dw_uv", comm=_to_chips(mix_pairs[1:]))
    parts["w_o_attn"] = got[0]
    dw_lat, got = _mm(dz_lat, a, ta=True, out_dtype=BF16, name="dw_lat", comm=_to_chips(mix_pairs[:1]))
    parts["w_o_sgu"] = got[0]
    lat = ql + kvl + QK_ROPE
    dw_uq_cols = dw_uq.reshape(ql, heads, HEAD_PAD)[:, :, :QK_NOPE + QK_ROPE].reshape(ql, heads * (QK_NOPE + QK_ROPE))
    in_names = ["w_uq", "w_ukv", "w_in"]
    in_slabs = [_slabs_from_cols(dw_uq_cols), _slabs_from_cols(dw_ukv),
                _slabs_from_rows(jnp.concatenate([dw_lat[:lat], dw_uv, dw_g], axis=0))]
    da = _mm(dz_lat, big["w_lat_t"], name="da_lat")
    da, bufs = _mm(dz_uv, big["w_uv_t"], add=da, name="da_uv", comm=_to_sibling(in_slabs))
    uq_pair, ukv_pair, in_pair = pair_sums(in_names, in_slabs, bufs)
    cols = in_pair.shape[2]
    chunk = _pick(cols, cols // TAIL_CHUNKS)
    chunks = [("c", c0, chunk) for c0 in range(0, cols, chunk)]
    da, got = _mm(dz_g, big["w_g_t"], add=da, name="da_g",
                  comm=_to_chips([uq_pair, in_pair], rows=[None, chunks[0]]))
    parts["w_uq"], in_parts = got
    grad_x, _, d_norm_mix, got = _norm_bwd(x, small["norm_mix_g"], da, dh1, "norm_mix_bwd", comm=_to_chips([ukv_pair]))
    parts["w_ukv"] = got[0]
    hosts = ["w_gate_ffn", "w_up_ffn", "w_down_ffn", "w_out", "w_o_attn", "w_o_sgu", "w_uq", "w_ukv"]
    assert len(chunks) <= 1 + len(hosts)
    for i, k in enumerate(hosts):
        if 1 + i < len(chunks):
            in_parts = update(k, comm=_to_chips([in_pair], rows=[chunks[1 + i]], into=[in_parts]))[0]
        else:
            update(k)
    parts["w_in"] = in_parts
    update("w_in")

    gs = {"norm_mix_g": d_norm_mix, "b_gate": d_b_gate, "q_norm_g": d_q_norm, "kv_norm_g": d_kv_norm,
          "sgu_norm_g": d_sgu_norm, "w_sgu": d_ws, "b_sgu_col": d_b_col, "norm_ffn_g": d_norm_ffn,
          "norm_final_g": d_norm_final}
    return loss_row, grad_x, gs, updates


def _my_place():
    return lax.axis_index("x"), lax.axis_index("y"), lax.axis_index("c")


N_CHIPS = N_DEV // 2

_GATHER_SEMS = [[(3,), (3,), ()], [(4,), (4,)], [(1,), (1,)]]


def _halves(shape):
    r, c = shape
    if (c // 2) % LANES == 0:
        return ("c", 0, c // 2), ("c", c // 2, c // 2)
    assert (r // 2) % (2 * SUBLANES) == 0, shape
    return ("r", 0, r // 2), ("r", r // 2, r // 2)


def _gather_copies(stage, ins, outs, sems):
    x, y, c = _my_place()
    me, x_nbr, y_nbr, diag = 4 * x + 2 * y + c, 4 * (1 - x) + 2 * y + c, 4 * x + 2 * (1 - y) + c, 4 * (1 - x) + 2 * (1 - y) + c
    sibling = (x, y, 1 - c)

    def remote(w, k, src, dst, to):
        return pltpu.make_async_remote_copy(src_ref=src, dst_ref=dst, send_sem=sems[0].at[w, k], recv_sem=sems[1].at[w, k],
                                            device_id=to, device_id_type=MESH)

    out = []
    for w in range(len(outs)):
        if stage == 1:
            dst = outs[w].at[me]
            out.append(pltpu.make_async_copy(ins[w], dst, sems[2].at[w]))
            out += [remote(w, k, ins[w], dst, to) for k, to in enumerate([sibling, (1 - x, y, c), (x, 1 - y, c)])]
        elif stage == 2:
            first, second = _halves(outs[w].shape[1:])
            out.append(remote(w, 0, _window(ins[w], x_nbr, first), _window(outs[w], x_nbr, first), (x, 1 - y, c)))
            out.append(remote(w, 1, _window(ins[w], y_nbr, second), _window(outs[w], y_nbr, second), (1 - x, y, c)))
            out.append(remote(w, 2, ins[w].at[x_nbr], outs[w].at[x_nbr], sibling))
            out.append(remote(w, 3, ins[w].at[y_nbr], outs[w].at[y_nbr], sibling))
        else:
            out.append(remote(w, 0, ins[w].at[diag], outs[w].at[diag], sibling))
    return out


def _gather_stage(stage, arrays):
    n = len(arrays)

    def start(ins, outs, sems):
        for cp in _gather_copies(stage, ins, outs, sems):
            cp.start()

    def finish(ins, outs, sems):
        for cp in _gather_copies(stage, ins, outs, sems):
            cp.wait()

    shapes = [jax.ShapeDtypeStruct(((N_DEV,) + a.shape) if stage == 1 else a.shape, a.dtype) for a in arrays]
    return _Comm(arrays, shapes, [pltpu.SemaphoreType.DMA((n,) + s) for s in _GATHER_SEMS[stage - 1]], start, finish,
                 aliases=None if stage == 1 else {w: w for w in range(n)})


def _join(*comms):
    ins, shapes, sems, aliases, spans = [], [], [], {}, []
    for cm in comms:
        spans.append((len(ins), len(ins) + len(cm.ins), len(shapes), len(shapes) + len(cm.out_shapes),
                      len(sems), len(sems) + len(cm.sems)))
        aliases.update({len(ins) + i: len(shapes) + o for i, o in cm.aliases.items()})
        ins, shapes, sems = ins + cm.ins, shapes + cm.out_shapes, sems + cm.sems

    def each(half):
        def run(i_refs, o_refs, s_refs):
            for cm, (i0, i1, o0, o1, s0, s1) in zip(comms, spans):
                getattr(cm, half)(i_refs[i0:i1], o_refs[o0:o1], s_refs[s0:s1])
        return run

    return _Comm(ins, shapes, sems, each("start"), each("finish"), aliases)


def _all_gather(shards):
    n = len(shards)
    n_sems = [len(s) for s in _GATHER_SEMS]

    def body(*refs):
        ins, outs, sems = refs[:n], refs[n:2 * n], refs[2 * n:]
        s0 = 0
        for stage in (1, 2, 3):
            mine = sems[s0:s0 + n_sems[stage - 1]]
            s0 += n_sems[stage - 1]
            copies = _gather_copies(stage, ins if stage == 1 else outs, outs, mine)
            for cp in copies:
                cp.start()
            for cp in copies:
                cp.wait()

    any_spec = pl.BlockSpec(memory_space=pl.ANY)
    return pl.pallas_call(
        body, in_specs=[any_spec] * n, out_specs=[any_spec] * n,
        out_shape=[jax.ShapeDtypeStruct((N_DEV,) + s.shape, s.dtype) for s in shards],
        scratch_shapes=[pltpu.SemaphoreType.DMA((n,) + s) for stage in _GATHER_SEMS for s in stage],
        compiler_params=pltpu.CompilerParams(has_side_effects=True), name="all_gather_weights")(*shards)


def _to_sibling(grads, first=None):
    n = len(grads)
    first = first or [0] * n

    def copies(ins, outs, sems):
        x, y, c = _my_place()
        send_sems, recv_sems = sems
        return [pltpu.make_async_remote_copy(
            src_ref=ins[w].at[first[w] + 2 * i + (1 - c)], dst_ref=outs[w].at[i], send_sem=send_sems.at[w, i],
            recv_sem=recv_sems.at[w, i], device_id=(x, y, 1 - c), device_id_type=MESH)
            for w in range(n) for i in range(N_CHIPS)]

    def start(ins, outs, sems):
        for cp in copies(ins, outs, sems):
            cp.start()

    def finish(ins, outs, sems):
        for cp in copies(ins, outs, sems):
            cp.wait()

    return _Comm(grads, [jax.ShapeDtypeStruct((N_CHIPS,) + g.shape[1:], g.dtype) for g in grads],
                 [pltpu.SemaphoreType.DMA((n, N_CHIPS)), pltpu.SemaphoreType.DMA((n, N_CHIPS))], start, finish)


def _window(ref, slab, win):
    if win is None:
        return ref.at[slab]
    if win[0] == "r":
        return ref.at[slab, pl.ds(win[1], win[2])]
    return ref.at[slab, slice(None), pl.ds(win[1], win[2])]


def _to_chips(parts, rows=None, into=None):
    n = len(parts)
    rows = rows or [None] * n

    def copies(ins, outs, sems):
        x, y, c = _my_place()
        send_sems, recv_sems, local_sems = sems
        mine = 2 * x + y
        chips = [(1 - x, y), (x, 1 - y), (1 - x, 1 - y)]
        remote = [pltpu.make_async_remote_copy(
            src_ref=_window(ins[w], 2 * cx + cy, rows[w]), dst_ref=_window(outs[w], mine, rows[w]),
            send_sem=send_sems.at[w, j], recv_sem=recv_sems.at[w, j], device_id=(cx, cy, c), device_id_type=MESH)
            for w in range(n) for j, (cx, cy) in enumerate(chips)]
        local = [pltpu.make_async_copy(_window(ins[w], mine, rows[w]), _window(outs[w], mine, rows[w]),
                                       local_sems.at[w]) for w in range(n)]
        return remote + local

    def start(ins, outs, sems):
        for cp in copies(ins, outs, sems):
            cp.start()

    def finish(ins, outs, sems):
        for cp in copies(ins, outs, sems):
            cp.wait()

    return _Comm(list(parts) + list(into or []), [jax.ShapeDtypeStruct(p.shape, p.dtype) for p in parts],
                 [pltpu.SemaphoreType.DMA((n, N_CHIPS - 1)), pltpu.SemaphoreType.DMA((n, N_CHIPS - 1)),
                  pltpu.SemaphoreType.DMA((n,))], start, finish,
                 aliases={n + w: w for w in range(n)} if into else None)


def _pair_sum(g, buf, name, first=0):
    _, r, c = g.shape
    tr, tc = _shard_tile(r, c, 4 * SHARD_TILE_ELEMS, 1024)
    core = (lax.axis_index("c") + first).astype(jnp.int32).reshape(1)

    def body(core_ref, g_ref, b_ref, o_ref):
        o_ref[...] = (g_ref[...].astype(F32) + b_ref[...].astype(F32)).astype(o_ref.dtype)

    blk = (1, tr, tc)
    return pl.pallas_call(
        body, grid_spec=pltpu.PrefetchScalarGridSpec(
            num_scalar_prefetch=1, grid=(N_CHIPS, r // tr, c // tc),
            in_specs=[pl.BlockSpec(blk, lambda i, j, l, core_ref: (2 * i + core_ref[0], j, l)),
                      pl.BlockSpec(blk, lambda i, j, l, core_ref: (i, j, l))],
            out_specs=pl.BlockSpec(blk, lambda i, j, l, core_ref: (i, j, l))),
        out_shape=jax.ShapeDtypeStruct(buf.shape, buf.dtype),
        compiler_params=_params(("parallel", "parallel", "parallel")), name=name)(core, g, buf)


def _all_reduce_pack(pack):
    r = pack.shape[0]

    def body(x_ref, out_ref, gath_ref, send_sems, recv_sems, local_sem):
        x, y, c = _my_place()
        me, sibling = (x, y, c), (x, y, 1 - c)
        chips = [(1 - x, y), (x, 1 - y), (1 - x, 1 - y)]

        def slab(place):
            return gath_ref.at[4 * place[0] + 2 * place[1] + place[2]]

        def copy(k, place, to, src=None):
            return pltpu.make_async_remote_copy(
                src_ref=slab(place) if src is None else src, dst_ref=slab(place),
                send_sem=send_sems.at[k], recv_sem=recv_sems.at[k], device_id=to, device_id_type=MESH)

        mine = pltpu.make_async_copy(x_ref, slab(me), local_sem)
        mine.start()
        first = [copy(0, me, sibling, src=x_ref)]
        first += [copy(1 + j, me, (*chip, c), src=x_ref) for j, chip in enumerate(chips)]
        for cp in first:
            cp.start()
        passed = [copy(4 + j, (*chip, c), sibling) for j, chip in enumerate(chips)]
        for j, chip in enumerate(chips):
            copy(1 + j, (*chip, c), me).wait_recv()
            passed[j].start()
        copy(0, sibling, me).wait_recv()
        for j, chip in enumerate(chips):
            copy(4 + j, (*chip, 1 - c), me).wait_recv()
        for cp in first + passed:
            cp.wait_send()
        mine.wait()
        acc = gath_ref[0]
        for i in range(1, N_DEV):
            acc = acc + gath_ref[i]
        out_ref[...] = acc

    vmem = pl.BlockSpec(memory_space=pltpu.VMEM)
    return pl.pallas_call(
        body, in_specs=[vmem], out_specs=vmem, out_shape=jax.ShapeDtypeStruct(pack.shape, F32),
        scratch_shapes=[pltpu.VMEM((N_DEV, r, LANES), F32), pltpu.SemaphoreType.DMA((7,)),
                        pltpu.SemaphoreType.DMA((7,)), pltpu.SemaphoreType.DMA],
        compiler_params=pltpu.CompilerParams(vmem_limit_bytes=VMEM_LIMIT), name="all_reduce_small")(pack)


def _adamw_math(w, g, m, v):
    m = ADAM_B1 * m + (1.0 - ADAM_B1) * g
    v = ADAM_B2 * v + (1.0 - ADAM_B2) * (g * g)
    m_hat = m / (1.0 - ADAM_B1 ** ADAM_STEP)
    v_hat = v / (1.0 - ADAM_B2 ** ADAM_STEP)
    delta = -ADAM_LR * (m_hat / (jnp.sqrt(v_hat) + ADAM_EPS) + ADAM_WD * w)
    return delta, m, v


def _adamw_shard(parts, w, m, v, name, comm=None):
    r, c = w.shape
    n_parts = parts.shape[0]
    tr, tc = _shard_tile(r, c)

    def body(p_ref, w_ref, m_ref, v_ref, g_ref, d_ref, nm_ref, nv_ref):
        g = p_ref[0].astype(F32)
        for i in range(1, n_parts):
            g = g + p_ref[i].astype(F32)
        g_ref[...] = g
        d_ref[...], nm_ref[...], nv_ref[...] = _adamw_math(w_ref[...], g, m_ref[...], v_ref[...])

    spec = pl.BlockSpec((tr, tc), lambda i, j: (i, j))
    outs, comm_outs = _call(
        body, grid=(r // tr, c // tc),
        in_specs=[pl.BlockSpec((n_parts, tr, tc), lambda i, j: (0, i, j)), spec, spec, spec],
        out_specs=[spec] * 4, out_shape=[jax.ShapeDtypeStruct((r, c), F32)] * 4,
        sem=("parallel", "parallel"), name=name, args=(parts, w, m, v), comm=comm)
    return outs, comm_outs


def _adamw_pack(g, w, m, v):
    r, c = w.shape

    def body(g_ref, w_ref, m_ref, v_ref, d_ref, nm_ref, nv_ref):
        d_ref[...], nm_ref[...], nv_ref[...] = _adamw_math(w_ref[...], g_ref[...], m_ref[...], v_ref[...])

    return pl.pallas_call(
        body, in_specs=[_full((r, c))] * 4, out_specs=[_full((r, c))] * 3, grid=(1,),
        out_shape=[jax.ShapeDtypeStruct((r, c), F32)] * 3,
        compiler_params=_params(("arbitrary",)), name="adamw_small")(g, w, m, v)


def _cols_from_slabs(g):
    return jnp.transpose(g, (1, 0, 2)).reshape(g.shape[1], N_DEV * g.shape[2])


def _slabs_from_cols(w):
    r, c8 = w.shape
    return jnp.transpose(w.reshape(r, N_DEV, c8 // N_DEV), (1, 0, 2))


def _rows_from_slabs(g):
    return g.reshape(N_DEV * g.shape[1], g.shape[2])


def _slabs_from_rows(w):
    return w.reshape(N_DEV, w.shape[0] // N_DEV, w.shape[1])


def _compute_layout(gathered, ql, kvl, heads, sw):
    out = {}
    for k, g in gathered.items():
        if k == "w_in":
            lat = ql + kvl + QK_ROPE
            w_in_t = _rows_from_slabs(g)
            out["w_lat_t"] = jnp.pad(w_in_t[:lat], ((0, LANES - QK_ROPE), (0, 0)))
            out["w_uv_t"] = w_in_t[lat:lat + 2 * sw]
            out["w_g_t"] = w_in_t[lat + 2 * sw:]
        elif k == "w_uq":
            per_head = _cols_from_slabs(g).reshape(ql, heads, QK_NOPE + QK_ROPE)
            pad = HEAD_PAD - QK_NOPE - QK_ROPE
            out["w_uq"] = jnp.pad(per_head, ((0, 0), (0, 0), (0, pad))).reshape(ql, heads * HEAD_PAD)
        elif k in ("w_o_attn", "w_out", "w_down_ffn"):
            out[k.removesuffix("_ffn")] = _rows_from_slabs(g)
        else:
            out[k.removesuffix("_ffn")] = _cols_from_slabs(g)
    return out


_SMALL =["norm_mix_g", "b_gate", "q_norm_g", "kv_norm_g", "sgu_norm_g", "w_sgu", "b_sgu", "norm_ffn_g", "norm_final_g"]
_BIG = ["w_in", "w_uq", "w_ukv", "w_o_attn", "w_o_sgu", "w_out", "w_gate_ffn", "w_up_ffn", "w_down_ffn"]
_TRANSPOSED = ("w_in", "w_gate_ffn", "w_up_ffn")
_ORDER = ["norm_mix_g", "w_in", "b_gate", "q_norm_g", "w_uq", "kv_norm_g", "w_ukv", "w_o_attn", "sgu_norm_g", "w_sgu",
          "b_sgu", "w_o_sgu", "w_out", "norm_ffn_g", "w_gate_ffn", "w_up_ffn", "w_down_ffn", "norm_final_g"]


def _pack_rows(parts):
    rows, sizes = [], []
    for p in parts:
        flat = p.reshape(-1)
        n = flat.shape[0]
        padded = -(-n // (SUBLANES * LANES)) * (SUBLANES * LANES)
        rows.append(jnp.pad(flat, (0, padded - n)).reshape(padded // LANES, LANES))
        sizes.append((n, padded // LANES))
    return jnp.concatenate(rows, axis=0), sizes


def _unpack_rows(pack, sizes, shapes):
    out, r0 = [], 0
    for (n, nr), shp in zip(sizes, shapes):
        out.append(pack[r0:r0 + nr].reshape(-1)[:n].reshape(shp))
        r0 += nr
    return out


def kernel(x, positions, norm_mix_g, w_in, b_gate, q_norm_g, w_uq, kv_norm_g, w_ukv, w_o_attn, sgu_norm_g, w_sgu, b_sgu, w_o_sgu, w_out, norm_ffn_g, w_gate_ffn, w_up_ffn, w_down_ffn, norm_final_g, loss_target, m_norm_mix_g, m_w_in, m_b_gate, m_q_norm_g, m_w_uq, m_kv_norm_g, m_w_ukv, m_w_o_attn, m_sgu_norm_g, m_w_sgu, m_b_sgu, m_w_o_sgu, m_w_out, m_norm_ffn_g, m_w_gate_ffn, m_w_up_ffn, m_w_down_ffn, m_norm_final_g, v_norm_mix_g, v_w_in, v_b_gate, v_q_norm_g, v_w_uq, v_kv_norm_g, v_w_ukv, v_w_o_attn, v_sgu_norm_g, v_w_sgu, v_b_sgu, v_w_o_sgu, v_w_out, v_norm_ffn_g, v_w_gate_ffn, v_w_up_ffn, v_w_down_ffn, v_norm_final_g):
    wts = dict(norm_mix_g=norm_mix_g, w_in=w_in, b_gate=b_gate, q_norm_g=q_norm_g, w_uq=w_uq, kv_norm_g=kv_norm_g,
               w_ukv=w_ukv, w_o_attn=w_o_attn, sgu_norm_g=sgu_norm_g, w_sgu=w_sgu, b_sgu=b_sgu, w_o_sgu=w_o_sgu,
               w_out=w_out, norm_ffn_g=norm_ffn_g, w_gate_ffn=w_gate_ffn, w_up_ffn=w_up_ffn, w_down_ffn=w_down_ffn,
               norm_final_g=norm_final_g)
    mom = dict(norm_mix_g=m_norm_mix_g, w_in=m_w_in, b_gate=m_b_gate, q_norm_g=m_q_norm_g, w_uq=m_w_uq,
               kv_norm_g=m_kv_norm_g, w_ukv=m_w_ukv, w_o_attn=m_w_o_attn, sgu_norm_g=m_sgu_norm_g, w_sgu=m_w_sgu,
               b_sgu=m_b_sgu, w_o_sgu=m_w_o_sgu, w_out=m_w_out, norm_ffn_g=m_norm_ffn_g, w_gate_ffn=m_w_gate_ffn,
               w_up_ffn=m_w_up_ffn, w_down_ffn=m_w_down_ffn, norm_final_g=m_norm_final_g)
    var = dict(norm_mix_g=v_norm_mix_g, w_in=v_w_in, b_gate=v_b_gate, q_norm_g=v_q_norm_g, w_uq=v_w_uq,
               kv_norm_g=v_kv_norm_g, w_ukv=v_w_ukv, w_o_attn=v_w_o_attn, sgu_norm_g=v_sgu_norm_g, w_sgu=v_w_sgu,
               b_sgu=v_b_sgu, w_o_sgu=v_w_o_sgu, w_out=v_w_out, norm_ffn_g=v_norm_ffn_g, w_gate_ffn=v_w_gate_ffn,
               w_up_ffn=v_w_up_ffn, w_down_ffn=v_w_down_ffn, norm_final_g=v_norm_final_g)

    t, d = x.shape[1], x.shape[2]
    ql, kvl = q_norm_g.shape[1], kv_norm_g.shape[1]
    heads = (w_uq.shape[2] * N_DEV) // (QK_NOPE + QK_ROPE)
    sw = sgu_norm_g.shape[1]

    def shard(a, k):
        return a[0].T if k in _TRANSPOSED else a[0]

    def unshard(a, k):
        return (a.T if k in _TRANSPOSED else a).reshape(wts[k].shape)

    opt = {k: (shard(wts[k], k), shard(mom[k], k), shard(var[k], k)) for k in _BIG}
    shards = {k: opt[k][0].astype(BF16) for k in _BIG}
    small = {
        "norm_mix_g": norm_mix_g, "b_gate": b_gate, "q_norm_g": q_norm_g, "kv_norm_g": kv_norm_g,
        "sgu_norm_g": sgu_norm_g, "w_sgu": w_sgu[0], "b_sgu_col": b_sgu[0][:, :, None], "norm_ffn_g": norm_ffn_g,
        "norm_final_g": norm_final_g[None, :],
    }

    loss_row, grad_x, gs, updates = _local_step(x[0], positions.reshape(t, 1), loss_target[0], small, shards, opt)
    grads, deltas, new_m, new_v = {}, {}, {}, {}
    for k in _BIG:
        grads[k], deltas[k], new_m[k], new_v[k] = (unshard(a, k) for a in updates[k])

    small_grads = [gs["norm_mix_g"], gs["b_gate"], gs["q_norm_g"], gs["kv_norm_g"], gs["sgu_norm_g"], gs["w_sgu"],
                   gs["b_sgu_col"], gs["norm_ffn_g"], gs["norm_final_g"]]
    pack, sizes = _pack_rows([loss_row] + small_grads)
    total = _all_reduce_pack(pack)
    shapes = [(1, LANES)] + [wts[k].shape for k in _SMALL]
    unpacked = _unpack_rows(total, sizes, shapes)
    loss = unpacked[0][0, 0]
    for k, g in zip(_SMALL, unpacked[1:]):
        grads[k] = g
    g_pack = total[sizes[0][1]:]
    w_pack, _ = _pack_rows([wts[k] for k in _SMALL])
    m_pack, _ = _pack_rows([mom[k] for k in _SMALL])
    v_pack, _ = _pack_rows([var[k] for k in _SMALL])
    d_pack, nm_pack, nv_pack = _adamw_pack(g_pack, w_pack, m_pack, v_pack)
    small_shapes = [wts[k].shape for k in _SMALL]
    for store, pk in ((deltas, d_pack), (new_m, nm_pack), (new_v, nv_pack)):
        for k, a in zip(_SMALL, _unpack_rows(pk, sizes[1:], small_shapes)):
            store[k] = a

    return (loss, grad_x[None], *[grads[k] for k in _ORDER], *[deltas[k] for k in _ORDER],
            *[new_m[k] for k in _ORDER], *[new_v[k] for k in _ORDER])
```

```python
import functools
import math

import jax
import jax.numpy as jnp
from jax import lax
from jax.experimental import pallas as pl
from jax.experimental.pallas import tpu as pltpu

F32 = jnp.float32
BF16 = jnp.bfloat16

N_DEV = 8
N_HEADS = 16
QK_NOPE = 128
QK_ROPE = 64
V_HEAD = 128
HEAD_PAD = 256
ROPE_THETA = 10000.0
CHUNK = 128
SGU_GROUP = 128
RMS_EPS = 1e-6
LANES = 128
SUBLANES = 8

ADAM_LR = 0.001
ADAM_B1 = 0.9
ADAM_B2 = 0.999
ADAM_EPS = 1e-08
ADAM_WD = 0.01
ADAM_STEP = 10

VMEM_LIMIT = 48 * 1024 * 1024
MM_TILE = (2048, 512, 2048)
MM_TILE_TA = (512, 2048)
ATTN_TILE = 512
HEADS_PER_STEP = 2
ROW_KERNEL_BYTES = 24 * 1024 * 1024
SHARD_TILE_ELEMS = 256 * 1024
SLABS_PER_STEP = 2
TAIL_CHUNKS = 4
NEG_BIG = -1e30
MESH = pl.DeviceIdType.MESH


def _pick(n, target, mult=LANES):
    best = None
    d = mult
    while d <= min(n, target):
        if n % d == 0:
            best = d
        d += mult
    return best or n


def _row_tile(t, width, n_blocks, mult=2 * SUBLANES):
    return _pick(t, max(mult, ROW_KERNEL_BYTES // (3 * n_blocks * width * 4)), mult)


def _shard_tile(r, c, elems=SHARD_TILE_ELEMS, max_rows=256):
    tr = _pick(r, max_rows, 2 * SUBLANES)
    return tr, _pick(c, max(LANES, elems // tr))


def _params(sem):
    return pltpu.CompilerParams(dimension_semantics=sem, vmem_limit_bytes=VMEM_LIMIT)


def _full(shape):
    nd = len(shape)
    return pl.BlockSpec(shape, lambda *_: (0,) * nd)


def _rows(tr, w, cb=0):
    return pl.BlockSpec((tr, w), lambda i: (i, cb))


class _Comm:
    def __init__(self, ins, out_shapes, sems, start, finish, aliases=None):
        self.ins, self.out_shapes, self.sems, self.start, self.finish = list(ins), list(out_shapes), list(sems), start, finish
        self.aliases = dict(aliases or {})


def _call(body, *, grid, in_specs, out_specs, out_shape, scratch_shapes=(), sem, name, args, comm=None):
    if comm is None:
        outs = pl.pallas_call(body, grid=grid, in_specs=list(in_specs), out_specs=list(out_specs),
                              out_shape=list(out_shape), scratch_shapes=list(scratch_shapes),
                              compiler_params=_params(sem), name=name)(*args)
        return list(outs), []
    n_in, n_out, n_sc = len(in_specs), len(out_shape), len(scratch_shapes)
    nci, nco = len(comm.ins), len(comm.out_shapes)

    def hosted(*refs):
        ins, refs = refs[:n_in], refs[n_in:]
        cins, refs = refs[:nci], refs[nci:]
        outs, refs = refs[:n_out], refs[n_out:]
        couts, refs = refs[:nco], refs[nco:]
        scratch, csems = refs[:n_sc], refs[n_sc:]
        ids = [pl.program_id(i) for i in range(len(grid))]
        first = functools.reduce(jnp.logical_and, [i == 0 for i in ids])
        last = functools.reduce(jnp.logical_and, [i == g - 1 for i, g in zip(ids, grid)])

        @pl.when(first)
        def _():
            comm.start(cins, couts, csems)

        body(*ins, *outs, *scratch)

        @pl.when(last)
        def _():
            comm.finish(cins, couts, csems)

    any_spec = pl.BlockSpec(memory_space=pl.ANY)
    res = pl.pallas_call(
        hosted, grid=grid, in_specs=list(in_specs) + [any_spec] * nci, out_specs=list(out_specs) + [any_spec] * nco,
        out_shape=list(out_shape) + comm.out_shapes, scratch_shapes=list(scratch_shapes) + comm.sems,
        input_output_aliases={n_in + i: n_out + o for i, o in comm.aliases.items()},
        compiler_params=pltpu.CompilerParams(dimension_semantics=("arbitrary",) * len(grid),
                                             vmem_limit_bytes=VMEM_LIMIT, has_side_effects=True),
        name=name)(*args, *comm.ins)
    return list(res[:n_out]), list(res[n_out:])


def _mm(a, b, *, ta=False, tb=False, add=None, out_dtype=F32, tm=None, tn=None, tk=None, name, comm=None,
        slab=None, a_slab0=0):
    sq = None
    if ta:
        tm, tn = tm or MM_TILE_TA[0], tn or MM_TILE_TA[1]
    if slab is None:
        m, k = (a.shape[1], a.shape[0]) if ta else a.shape
        n = b.shape[0] if tb else b.shape[1]
        assert k == (b.shape[1] if tb else b.shape[0]), (a.shape, b.shape, ta, tb)
        tm, tn, tk = _pick(m, tm or MM_TILE[0]), _pick(n, tn or MM_TILE[1]), _pick(k, tk or MM_TILE[2])
        grid = (m // tm, n // tn, k // tk)
        a_spec = pl.BlockSpec((tk, tm), lambda i, j, kk: (kk, i)) if ta else pl.BlockSpec((tm, tk), lambda i, j, kk: (i, kk))
        b_spec = pl.BlockSpec((tn, tk), lambda i, j, kk: (j, kk)) if tb else pl.BlockSpec((tk, tn), lambda i, j, kk: (kk, j))
        o_spec, o_shape = pl.BlockSpec((tm, tn), lambda i, j, kk: (i, j)), (m, n)
    elif slab == "n":
        m, k = (a.shape[1], a.shape[0]) if ta else a.shape
        s, c = b.shape[0], (b.shape[1] if tb else b.shape[2])
        assert k == (b.shape[2] if tb else b.shape[1]), (a.shape, b.shape, ta, tb)
        tm, tn, tk = _pick(m, tm or MM_TILE[0]), c, _pick(k, tk or MM_TILE[2])
        grid = (m // tm, s, k // tk)
        a_spec = pl.BlockSpec((tk, tm), lambda i, j, kk: (kk, i)) if ta else pl.BlockSpec((tm, tk), lambda i, j, kk: (i, kk))
        b_spec = (pl.BlockSpec((sq, c, tk), lambda i, j, kk: (j, 0, kk)) if tb
                  else pl.BlockSpec((sq, tk, c), lambda i, j, kk: (j, kk, 0)))
        o_spec, o_shape = pl.BlockSpec((sq, tm, c), lambda i, j, kk: (j, i, 0)), (s, m, c)
    elif slab == "m":
        assert ta and not tb
        s, k, c = a.shape
        n = b.shape[1]
        assert k == b.shape[0], (a.shape, b.shape)
        tm, tn, tk = c, _pick(n, tn or MM_TILE[1]), _pick(k, tk or MM_TILE[2])
        grid = (s, n // tn, k // tk)
        a_spec = pl.BlockSpec((sq, tk, c), lambda i, j, kk: (i, kk, 0))
        b_spec = pl.BlockSpec((tk, tn), lambda i, j, kk: (kk, j))
        o_spec, o_shape = pl.BlockSpec((sq, c, tn), lambda i, j, kk: (i, 0, j)), (s, c, n)
    else:
        assert slab == "k" and not ta
        s, c = b.shape[0], (b.shape[2] if tb else b.shape[1])
        m, n = a.shape[1], (b.shape[1] if tb else b.shape[2])
        assert a.shape[2] == c and a.shape[0] >= a_slab0 + s, (a.shape, b.shape, a_slab0)
        tm, tn, tk = _pick(m, tm or MM_TILE[0]), _pick(n, tn or MM_TILE[1]), c
        per_step = SLABS_PER_STEP if (s % SLABS_PER_STEP == 0 and a_slab0 % SLABS_PER_STEP == 0) else 1
        first = a_slab0 // per_step
        grid = (m // tm, n // tn, s // per_step)
        a_spec = pl.BlockSpec((per_step, tm, c), lambda i, j, kk: (kk + first, i, 0))
        b_spec = (pl.BlockSpec((per_step, tn, c), lambda i, j, kk: (kk, j, 0)) if tb
                  else pl.BlockSpec((per_step, c, tn), lambda i, j, kk: (kk, 0, j)))
        o_spec, o_shape = pl.BlockSpec((tm, tn), lambda i, j, kk: (i, j)), (m, n)
    nk = grid[2]
    dims = (((0 if ta else 1,), (1 if tb else 0,)), ((), ()))

    def product(a_ref, b_ref):
        if slab != "k":
            return lax.dot_general(a_ref[...].astype(BF16), b_ref[...].astype(BF16), dims, preferred_element_type=F32)
        r = None
        for u in range(a_ref.shape[0]):
            p = lax.dot_general(a_ref[u].astype(BF16), b_ref[u].astype(BF16), dims, preferred_element_type=F32)
            r = p if r is None else r + p
        return r

    def body(*refs):
        a_ref, b_ref = refs[:2]
        add_ref = refs[2] if add is not None else None
        o_ref = refs[3] if add is not None else refs[2]
        acc_ref = refs[-1] if nk > 1 else None

        def finish(r):
            if add_ref is not None:
                r = r + add_ref[...].astype(F32)
            o_ref[...] = r.astype(o_ref.dtype)

        if nk == 1:
            finish(product(a_ref, b_ref))
            return
        kk = pl.program_id(2)

        @pl.when(kk == 0)
        def _():
            acc_ref[...] = product(a_ref, b_ref)

        if nk > 2:
            @pl.when(jnp.logical_and(kk > 0, kk < nk - 1))
            def _():
                acc_ref[...] += product(a_ref, b_ref)

        @pl.when(kk == nk - 1)
        def _():
            finish(acc_ref[...] + product(a_ref, b_ref))

    in_specs = [a_spec, b_spec] + ([o_spec] if add is not None else [])
    args = (a, b) + ((add,) if add is not None else ())
    outs, comm_outs = _call(
        body, grid=grid, in_specs=in_specs, out_specs=[o_spec],
        out_shape=[jax.ShapeDtypeStruct(o_shape, out_dtype)],
        scratch_shapes=[pltpu.VMEM((tm, tn), F32)] if nk > 1 else [],
        sem=("parallel", "parallel", "arbitrary"), name=name, args=args, comm=comm)
    return outs[0] if comm is None else (outs[0], comm_outs)


def _rms_scale(x):
    return lax.rsqrt(jnp.mean(x * x, axis=-1, keepdims=True) + RMS_EPS)


def _rms_bwd(xhat, r, g, dy):
    t = dy * g
    dx = r * (t - xhat * jnp.mean(t * xhat, axis=-1, keepdims=True))
    return dx, dy * xhat


_GELU_C = math.sqrt(2.0 / math.pi)


def _gelu(x):
    return x * (0.5 * (1.0 + jnp.tanh(_GELU_C * (x + 0.044715 * (x * x * x)))))


def _gelu_and_grad(x):
    t = jnp.tanh(_GELU_C * (x + 0.044715 * (x * x * x)))
    cdf = 0.5 * (1.0 + t)
    return x * cdf, cdf + x * (0.5 * (1.0 - t * t) * (_GELU_C * (1.0 + 3.0 * 0.044715 * (x * x))))


def _sigmoid(x):
    return 1.0 / (1.0 + jnp.exp(-x))


def _swap_halves(x):
    lane = lax.broadcasted_iota(jnp.int32, x.shape, 1)
    first = (lane % QK_ROPE) < (QK_ROPE // 2)
    return jnp.where(first, pltpu.roll(x, LANES - QK_ROPE // 2, 1), pltpu.roll(x, QK_ROPE // 2, 1))


def _rope(x, cos, sin_signed):
    return x * cos + _swap_halves(x) * sin_signed


def _rope_bwd(d, cos, sin_signed):
    return d * cos + _swap_halves(d * sin_signed)


def _rope_tables(pos_col, inv_freq_row, sign_row):
    t = pos_col.shape[0]
    tr = _pick(t, 512, SUBLANES)

    def body(p_ref, f_ref, s_ref, cos_ref, sin_ref):
        ang = p_ref[...].astype(F32) * f_ref[...]
        cos_ref[...] = jnp.cos(ang)
        sin_ref[...] = jnp.sin(ang) * s_ref[...]

    return pl.pallas_call(
        body, grid=(t // tr,), in_specs=[_rows(tr, 1), _full((1, LANES)), _full((1, LANES))],
        out_specs=[_rows(tr, LANES), _rows(tr, LANES)],
        out_shape=[jax.ShapeDtypeStruct((t, LANES), F32)] * 2,
        compiler_params=_params(("parallel",)), name="rope_tables")(pos_col, inv_freq_row, sign_row)


def _norm_fwd(x, g, name):
    t, d = x.shape
    tr = _row_tile(t, d, 2)

    def body(x_ref, g_ref, y_ref):
        xv = x_ref[...]
        y_ref[...] = (xv * _rms_scale(xv) * g_ref[...]).astype(BF16)

    return pl.pallas_call(
        body, grid=(t // tr,), in_specs=[_rows(tr, d), _full((1, d))], out_specs=_rows(tr, d),
        out_shape=jax.ShapeDtypeStruct((t, d), BF16), compiler_params=_params(("parallel",)), name=name)(x, g)


def _lat_fwd(z_lat, qg, kvg, cos, sin, ql, kvl):
    t = z_lat.shape[0]
    tr = _row_tile(t, z_lat.shape[1], 2)

    def body(z_ref, qg_ref, kvg_ref, cos_ref, sin_ref, qn_ref, kvn_ref, kpe_ref):
        q = z_ref[:, 0:ql]
        qn_ref[...] = (q * _rms_scale(q) * qg_ref[...]).astype(BF16)
        kv = z_ref[:, ql:ql + kvl]
        kvn_ref[...] = (kv * _rms_scale(kv) * kvg_ref[...]).astype(BF16)
        kpe_ref[...] = _rope(z_ref[:, ql + kvl:ql + kvl + LANES], cos_ref[...], sin_ref[...]).astype(BF16)

    w = z_lat.shape[1]
    return pl.pallas_call(
        body, grid=(t // tr,),
        in_specs=[_rows(tr, w), _full((1, ql)), _full((1, kvl)), _rows(tr, LANES), _rows(tr, LANES)],
        out_specs=[_rows(tr, ql), _rows(tr, kvl), _rows(tr, LANES)],
        out_shape=[jax.ShapeDtypeStruct((t, ql), BF16), jax.ShapeDtypeStruct((t, kvl), BF16),
                   jax.ShapeDtypeStruct((t, LANES), BF16)],
        compiler_params=_params(("parallel",)), name="lat_fwd")(z_lat, qg, kvg, cos, sin)


def _q_rope(q_p, cos, sin, bwd, name):
    t, w = q_p.shape
    tr = _row_tile(t, w, 2)
    fn = _rope_bwd if bwd else _rope

    def body(q_ref, cos_ref, sin_ref, o_ref):
        c, s = cos_ref[...], sin_ref[...]
        for h in range(w // HEAD_PAD):
            o_ref[:, h * HEAD_PAD:h * HEAD_PAD + QK_NOPE] = q_ref[:, h * HEAD_PAD:h * HEAD_PAD + QK_NOPE].astype(BF16)
            lo = h * HEAD_PAD + QK_NOPE
            o_ref[:, lo:lo + LANES] = fn(q_ref[:, lo:lo + LANES].astype(F32), c, s).astype(BF16)

    return pl.pallas_call(
        body, grid=(t // tr,), in_specs=[_rows(tr, w), _rows(tr, LANES), _rows(tr, LANES)], out_specs=_rows(tr, w),
        out_shape=jax.ShapeDtypeStruct((t, w), BF16), compiler_params=_params(("parallel",)), name=name)(q_p, cos, sin)


def _tril_mask():
    r = lax.broadcasted_iota(jnp.int32, (CHUNK, CHUNK), 0)
    c = lax.broadcasted_iota(jnp.int32, (CHUNK, CHUNK), 1)
    return r >= c


def _sgu_fwd(z_uv, gs, ws, b_col):
    t = z_uv.shape[0]
    sw = z_uv.shape[1] // 2
    groups = sw // SGU_GROUP
    tr = _pick(t, 256, CHUNK)

    def body(u_ref, v_ref, gs_ref, ws_ref, b_ref, o_ref):
        v = _gelu(v_ref[...])
        vn = (v * _rms_scale(v) * gs_ref[...]).astype(BF16)
        tri = _tril_mask()
        for g in range(groups):
            wg = jnp.where(tri, ws_ref[g], 0.0).astype(BF16)
            cols = slice(g * SGU_GROUP, (g + 1) * SGU_GROUP)
            for c in range(tr // CHUNK):
                rows = slice(c * CHUNK, (c + 1) * CHUNK)
                mixed = jnp.dot(wg, vn[rows, cols], preferred_element_type=F32) + b_ref[g]
                o_ref[rows, cols] = (_gelu(u_ref[rows, cols]) * mixed).astype(BF16)

    return pl.pallas_call(
        body, grid=(t // tr,),
        in_specs=[_rows(tr, sw, 0), _rows(tr, sw, 1), _full((1, sw)), _full(ws.shape), _full(b_col.shape)],
        out_specs=_rows(tr, sw), out_shape=jax.ShapeDtypeStruct((t, sw), BF16),
        compiler_params=_params(("parallel",)), name="sgu_fwd")(z_uv, z_uv, gs, ws, b_col)


def _merge_fwd(y_attn, y_sgu, z_g, b_gate, comm=None):
    t, d = y_attn.shape
    tr = _row_tile(t, d, 5)

    def body(ya_ref, ys_ref, g0_ref, g1_ref, b0_ref, b1_ref, o_ref):
        g0 = _sigmoid(g0_ref[...] + b0_ref[...])
        g1 = _sigmoid(g1_ref[...] + b1_ref[...])
        o_ref[...] = (g0 * ya_ref[...] + g1 * ys_ref[...]).astype(BF16)

    bspec0 = pl.BlockSpec((1, d), lambda i: (0, 0))
    bspec1 = pl.BlockSpec((1, d), lambda i: (0, 1))
    outs, comm_outs = _call(
        body, grid=(t // tr,),
        in_specs=[_rows(tr, d), _rows(tr, d), _rows(tr, d, 0), _rows(tr, d, 1), bspec0, bspec1],
        out_specs=[_rows(tr, d)], out_shape=[jax.ShapeDtypeStruct((t, d), BF16)],
        sem=("parallel",), name="merge_fwd", args=(y_attn, y_sgu, z_g, z_g, b_gate, b_gate), comm=comm)
    return outs[0], comm_outs


def _swiglu_fwd(gate, up, comm=None):
    t, f = gate.shape
    tr = _row_tile(t, f, 3)

    def body(g_ref, u_ref, o_ref):
        g = g_ref[...]
        o_ref[...] = (g * _sigmoid(g) * u_ref[...]).astype(BF16)

    outs, comm_outs = _call(
        body, grid=(t // tr,), in_specs=[_rows(tr, f), _rows(tr, f)], out_specs=[_rows(tr, f)],
        out_shape=[jax.ShapeDtypeStruct((t, f), BF16)], sem=("parallel",), name="swiglu_fwd", args=(gate, up), comm=comm)
    return outs[0], comm_outs


def _loss_head(h2, g, target):
    t, d = h2.shape
    tr = _row_tile(t, d, 3)

    def body(h_ref, g_ref, t_ref, loss_ref, dh_ref, dhb_ref, dg_ref):
        @pl.when(pl.program_id(0) == 0)
        def _():
            loss_ref[...] = jnp.zeros_like(loss_ref)
            dg_ref[...] = jnp.zeros_like(dg_ref)

        h = h_ref[...]
        r = _rms_scale(h)
        hhat = h * r
        gv = g_ref[...]
        err = hhat * gv - t_ref[...]
        loss_ref[...] += jnp.full(loss_ref.shape, 0.5 * jnp.sum(jnp.mean(err * err, axis=-1)), F32)
        dx, dg_rows = _rms_bwd(hhat, r, gv, err * (1.0 / d))
        dh_ref[...] = dx
        dhb_ref[...] = dx.astype(BF16)
        dg_ref[...] += jnp.sum(dg_rows, axis=0, keepdims=True)

    return pl.pallas_call(
        body, grid=(t // tr,), in_specs=[_rows(tr, d), _full((1, d)), _rows(tr, d)],
        out_specs=[_full((1, LANES)), _rows(tr, d), _rows(tr, d), _full((1, d))],
        out_shape=[jax.ShapeDtypeStruct((1, LANES), F32), jax.ShapeDtypeStruct((t, d), F32),
                   jax.ShapeDtypeStruct((t, d), BF16), jax.ShapeDtypeStruct((1, d), F32)],
        compiler_params=_params(("arbitrary",)), name="loss_head")(h2, g, target)


def _swiglu_bwd(gate, up, dact):
    t, f = gate.shape
    tr = _row_tile(t, f, 4)

    def body(g_ref, u_ref, d_ref, dgu_ref):
        g = g_ref[...]
        s = _sigmoid(g)
        d = d_ref[...]
        dgu_ref[0] = (d * u_ref[...] * (s * (1.0 + g * (1.0 - s)))).astype(BF16)
        dgu_ref[1] = (d * (g * s)).astype(BF16)

    return pl.pallas_call(
        body, grid=(t // tr,), in_specs=[_rows(tr, f)] * 3, out_specs=pl.BlockSpec((2, tr, f), lambda i: (0, i, 0)),
        out_shape=jax.ShapeDtypeStruct((2, t, f), BF16),
        compiler_params=_params(("parallel",)), name="swiglu_bwd")(gate, up, dact)


def _norm_bwd(x, g, dy, resid, name, comm=None):
    t, d = x.shape
    tr = _row_tile(t, d, 5)

    def body(x_ref, g_ref, dy_ref, r_ref, dx_ref, dxb_ref, dg_ref):
        @pl.when(pl.program_id(0) == 0)
        def _():
            dg_ref[...] = jnp.zeros_like(dg_ref)

        xv = x_ref[...]
        r = _rms_scale(xv)
        dx, dg_rows = _rms_bwd(xv * r, r, g_ref[...], dy_ref[...])
        dx = r_ref[...] + dx
        dx_ref[...] = dx
        dxb_ref[...] = dx.astype(BF16)
        dg_ref[...] += jnp.sum(dg_rows, axis=0, keepdims=True)

    outs, comm_outs = _call(
        body, grid=(t // tr,), in_specs=[_rows(tr, d), _full((1, d)), _rows(tr, d), _rows(tr, d)],
        out_specs=[_rows(tr, d), _rows(tr, d), _full((1, d))],
        out_shape=[jax.ShapeDtypeStruct((t, d), F32), jax.ShapeDtypeStruct((t, d), BF16),
                   jax.ShapeDtypeStruct((1, d), F32)],
        sem=("arbitrary",), name=name, args=(x, g, dy, resid), comm=comm)
    return (outs[0], outs[1], outs[2]) if comm is None else (outs[0], outs[1], outs[2], comm_outs)


def _merge_bwd(dmerged, y_attn, y_sgu, z_g, b_gate):
    t, d = y_attn.shape
    tr = _row_tile(t, d, 7)

    def body(dm_ref, ya_ref, ys_ref, g0_ref, g1_ref, b0_ref, b1_ref, dya_ref, dys_ref, dz_ref, db_ref):
        @pl.when(pl.program_id(0) == 0)
        def _():
            db_ref[...] = jnp.zeros_like(db_ref)

        dm = dm_ref[...]
        g0 = _sigmoid(g0_ref[...] + b0_ref[...])
        g1 = _sigmoid(g1_ref[...] + b1_ref[...])
        dya_ref[...] = (dm * g0).astype(BF16)
        dys_ref[...] = (dm * g1).astype(BF16)
        dl0 = dm * ya_ref[...] * (g0 * (1.0 - g0))
        dl1 = dm * ys_ref[...] * (g1 * (1.0 - g1))
        dz_ref[:, 0:d] = dl0.astype(BF16)
        dz_ref[:, d:2 * d] = dl1.astype(BF16)
        db_ref[:, 0:d] += jnp.sum(dl0, axis=0, keepdims=True)
        db_ref[:, d:2 * d] += jnp.sum(dl1, axis=0, keepdims=True)

    bspec0 = pl.BlockSpec((1, d), lambda i: (0, 0))
    bspec1 = pl.BlockSpec((1, d), lambda i: (0, 1))
    return pl.pallas_call(
        body, grid=(t // tr,),
        in_specs=[_rows(tr, d), _rows(tr, d), _rows(tr, d), _rows(tr, d, 0), _rows(tr, d, 1), bspec0, bspec1],
        out_specs=[_rows(tr, d), _rows(tr, d), _rows(tr, 2 * d), _full((1, 2 * d))],
        out_shape=[jax.ShapeDtypeStruct((t, d), BF16), jax.ShapeDtypeStruct((t, d), BF16),
                   jax.ShapeDtypeStruct((t, 2 * d), BF16), jax.ShapeDtypeStruct((1, 2 * d), F32)],
        compiler_params=_params(("arbitrary",)), name="merge_bwd")(dmerged, y_attn, y_sgu, z_g, z_g, b_gate, b_gate)


def _sgu_bwd(z_uv, ds_out, gs, ws, b_col):
    t = z_uv.shape[0]
    sw = z_uv.shape[1] // 2
    groups = sw // SGU_GROUP
    tr = _pick(t, 256, CHUNK)

    def body(u_ref, v_ref, d_ref, gs_ref, ws_ref, b_ref, dz_ref, dws_ref, db_ref, dgs_ref, dvn_ref):
        @pl.when(pl.program_id(0) == 0)
        def _():
            dws_ref[...] = jnp.zeros_like(dws_ref)
            db_ref[...] = jnp.zeros_like(db_ref)
            dgs_ref[...] = jnp.zeros_like(dgs_ref)

        v, dgelu_v = _gelu_and_grad(v_ref[...])
        r = _rms_scale(v)
        vhat = v * r
        gsv = gs_ref[...]
        vn = (vhat * gsv).astype(BF16)
        tri = _tril_mask()
        for g in range(groups):
            wg = jnp.where(tri, ws_ref[g], 0.0).astype(BF16)
            cols = slice(g * SGU_GROUP, (g + 1) * SGU_GROUP)
            for c in range(tr // CHUNK):
                rows = slice(c * CHUNK, (c + 1) * CHUNK)
                vn_cg = vn[rows, cols]
                mixed = jnp.dot(wg, vn_cg, preferred_element_type=F32) + b_ref[g]
                u, dgelu_u = _gelu_and_grad(u_ref[rows, cols])
                dso = d_ref[rows, cols]
                dz_ref[rows, cols] = (dso * mixed * dgelu_u).astype(BF16)
                dmixed = dso * u
                db_ref[g] += jnp.sum(dmixed, axis=1, keepdims=True)
                dmixed_b = dmixed.astype(BF16)
                dws_ref[g] += jnp.where(
                    tri, lax.dot_general(dmixed_b, vn_cg, (((1,), (1,)), ((), ())), preferred_element_type=F32), 0.0)
                dvn_ref[rows, cols] = lax.dot_general(wg, dmixed_b, (((0,), (0,)), ((), ())), preferred_element_type=F32)
        dvn = dvn_ref[...]
        dv, dgs_rows = _rms_bwd(vhat, r, gsv, dvn)
        dz_ref[:, sw:2 * sw] = (dv * dgelu_v).astype(BF16)
        dgs_ref[...] += jnp.sum(dgs_rows, axis=0, keepdims=True)

    return pl.pallas_call(
        body, grid=(t // tr,),
        in_specs=[_rows(tr, sw, 0), _rows(tr, sw, 1), _rows(tr, sw), _full((1, sw)), _full(ws.shape), _full(b_col.shape)],
        out_specs=[_rows(tr, 2 * sw), _full(ws.shape), _full(b_col.shape), _full((1, sw))],
        out_shape=[jax.ShapeDtypeStruct((t, 2 * sw), BF16), jax.ShapeDtypeStruct(ws.shape, F32),
                   jax.ShapeDtypeStruct(b_col.shape, F32), jax.ShapeDtypeStruct((1, sw), F32)],
        scratch_shapes=[pltpu.VMEM((tr, sw), F32)],
        compiler_params=_params(("arbitrary",)), name="sgu_bwd")(z_uv, z_uv, ds_out, gs, ws, b_col)


def _lat_bwd(z_lat, qg, kvg, dqn, dkvn, dkpe_heads, cos, sin, ql, kvl):
    t, w = z_lat.shape
    heads = dkpe_heads.shape[0]
    tr = _row_tile(t, w + heads * LANES, 3)

    def body(z_ref, qg_ref, kvg_ref, dq_ref, dkv_ref, dk_ref, cos_ref, sin_ref, dz_ref, dqg_ref, dkvg_ref):
        @pl.when(pl.program_id(0) == 0)
        def _():
            dqg_ref[...] = jnp.zeros_like(dqg_ref)
            dkvg_ref[...] = jnp.zeros_like(dkvg_ref)

        q = z_ref[:, 0:ql]
        r = _rms_scale(q)
        dx, dg_rows = _rms_bwd(q * r, r, qg_ref[...], dq_ref[...])
        dz_ref[:, 0:ql] = dx.astype(BF16)
        dqg_ref[...] += jnp.sum(dg_rows, axis=0, keepdims=True)
        kv = z_ref[:, ql:ql + kvl]
        r = _rms_scale(kv)
        dx, dg_rows = _rms_bwd(kv * r, r, kvg_ref[...], dkv_ref[...])
        dz_ref[:, ql:ql + kvl] = dx.astype(BF16)
        dkvg_ref[...] += jnp.sum(dg_rows, axis=0, keepdims=True)
        dk = dk_ref[0]
        for h in range(1, heads):
            dk = dk + dk_ref[h]
        dz_ref[:, ql + kvl:ql + kvl + LANES] = _rope_bwd(dk, cos_ref[...], sin_ref[...]).astype(BF16)

    return pl.pallas_call(
        body, grid=(t // tr,),
        in_specs=[_rows(tr, w), _full((1, ql)), _full((1, kvl)), _rows(tr, ql), _rows(tr, kvl),
                  pl.BlockSpec((heads, tr, LANES), lambda i: (0, i, 0)), _rows(tr, LANES), _rows(tr, LANES)],
        out_specs=[_rows(tr, w), _full((1, ql)), _full((1, kvl))],
        out_shape=[jax.ShapeDtypeStruct((t, w), BF16), jax.ShapeDtypeStruct((1, ql), F32),
                   jax.ShapeDtypeStruct((1, kvl), F32)],
        compiler_params=_params(("arbitrary",)), name="lat_bwd")(z_lat, qg, kvg, dqn, dkvn, dkpe_heads, cos, sin)


_NT = (((1,), (1,)), ((), ()))


def _attn_scale():
    return (QK_NOPE + QK_ROPE) ** -0.5


def _heads_per_step(heads):
    return HEADS_PER_STEP if heads % HEADS_PER_STEP == 0 else 1


def _attn_fwd(q_c, kv, kpe, comm=None):
    t = q_c.shape[0]
    heads = q_c.shape[1] // HEAD_PAD
    tq = _pick(t, ATTN_TILE)
    nq = t // tq
    scale = _attn_scale()
    to_log2 = scale * math.log2(math.e)
    tn_dims = (((0,), (0,)), ((), ()))

    hps = _heads_per_step(heads)

    def body(q_ref, kv_ref, kpe_ref, o_ref, ob_ref, lse_ref, m_sc, l_sc, acc_sc):
        qi, ki = pl.program_id(1), pl.program_id(2)

        @pl.when(ki == 0)
        def _():
            m_sc[...] = jnp.full_like(m_sc, NEG_BIG)
            l_sc[...] = jnp.zeros_like(l_sc)
            acc_sc[...] = jnp.zeros_like(acc_sc)

        def step(diagonal):
            for u in range(hps):
                lo = u * HEAD_PAD
                kc = jnp.concatenate([kv_ref[:, lo:lo + QK_NOPE], kpe_ref[...]], axis=1)
                st = lax.dot_general(kc, q_ref[:, lo:lo + HEAD_PAD], _NT, preferred_element_type=F32)
                if diagonal:
                    krow = lax.broadcasted_iota(jnp.int32, st.shape, 0)
                    qcol = lax.broadcasted_iota(jnp.int32, st.shape, 1)
                    st = jnp.where(qcol >= krow, st, NEG_BIG)
                m_prev = m_sc[u]
                m_new = jnp.maximum(m_prev, jnp.max(st, axis=0, keepdims=True))
                alpha = jnp.exp2((m_prev - m_new) * to_log2)
                pt = jnp.exp2((st - m_new) * to_log2)
                l_sc[u] = alpha * l_sc[u] + jnp.sum(pt, axis=0, keepdims=True)
                acc_sc[u] = alpha * acc_sc[u] + lax.dot_general(
                    kv_ref[:, lo + QK_NOPE:lo + HEAD_PAD], pt.astype(BF16), tn_dims, preferred_element_type=F32)
                m_sc[u] = m_new

        @pl.when(ki < qi)
        def _():
            step(False)

        @pl.when(ki == qi)
        def _():
            step(True)
            for u in range(hps):
                o = (acc_sc[u] / l_sc[u]).T
                o_ref[:, u * V_HEAD:(u + 1) * V_HEAD] = o
                ob_ref[:, u * V_HEAD:(u + 1) * V_HEAD] = o.astype(BF16)
                lse_ref[u] = m_sc[u] * scale + jnp.log(l_sc[u])

    omap = lambda g, qi, ki: (qi, g)
    outs, comm_outs = _call(
        body, grid=(heads // hps, nq, nq),
        in_specs=[pl.BlockSpec((tq, hps * HEAD_PAD), omap),
                  pl.BlockSpec((tq, hps * HEAD_PAD), lambda g, qi, ki: (jnp.minimum(ki, qi), g)),
                  pl.BlockSpec((tq, LANES), lambda g, qi, ki: (jnp.minimum(ki, qi), 0))],
        out_specs=[pl.BlockSpec((tq, hps * V_HEAD), omap), pl.BlockSpec((tq, hps * V_HEAD), omap),
                   pl.BlockSpec((hps, 1, tq), lambda g, qi, ki: (g, 0, qi))],
        out_shape=[jax.ShapeDtypeStruct((t, heads * V_HEAD), F32), jax.ShapeDtypeStruct((t, heads * V_HEAD), BF16),
                   jax.ShapeDtypeStruct((heads, 1, t), F32)],
        scratch_shapes=[pltpu.VMEM((hps, 1, tq), F32), pltpu.VMEM((hps, 1, tq), F32),
                        pltpu.VMEM((hps, V_HEAD, tq), F32)],
        sem=("parallel", "parallel", "arbitrary"), name="attn_fwd", args=(q_c, kv, kpe), comm=comm)
    return outs[0], outs[1], outs[2], comm_outs


def _attn_bwd(q_c, kv, kpe, o, do, lse_row, comm=None):
    t = q_c.shape[0]
    heads = q_c.shape[1] // HEAD_PAD
    tk = _pick(t, ATTN_TILE)
    nk = t // tk
    scale = _attn_scale()
    tn_dims = (((0,), (0,)), ((), ()))

    hps = _heads_per_step(heads)

    def body(q_ref, kv_ref, kpe_ref, do_ref, lse_ref, o_ref, dq_ref, dkv_ref, dkpe_ref, dk_sc, dv_sc, delta_sc):
        ki, qi = pl.program_id(1), pl.program_id(2)

        @pl.when(jnp.logical_and(ki == 0, qi == 0))
        def _():
            dq_ref[...] = jnp.zeros_like(dq_ref)

        @pl.when(qi == 0)
        def _():
            dk_sc[...] = jnp.zeros_like(dk_sc)
            dv_sc[...] = jnp.zeros_like(dv_sc)

        @pl.when(ki == 0)
        def _():
            for u in range(hps):
                cols = slice(u * V_HEAD, (u + 1) * V_HEAD)
                delta_sc[qi * hps + u] = jnp.sum((do_ref[:, cols] * o_ref[:, cols]).T, axis=0, keepdims=True)

        def step(diagonal):
            for u in range(hps):
                lo = u * HEAD_PAD
                kc = jnp.concatenate([kv_ref[:, lo:lo + QK_NOPE], kpe_ref[...]], axis=1)
                q = q_ref[:, lo:lo + HEAD_PAD]
                st = lax.dot_general(kc, q, _NT, preferred_element_type=F32) * scale
                pt = jnp.exp(st - lse_ref[u])
                if diagonal:
                    krow = lax.broadcasted_iota(jnp.int32, st.shape, 0)
                    qcol = lax.broadcasted_iota(jnp.int32, st.shape, 1)
                    pt = jnp.where(qcol >= krow, pt, 0.0)
                do_b = do_ref[:, u * V_HEAD:(u + 1) * V_HEAD].astype(BF16)
                dv_sc[u] += jnp.dot(pt.astype(BF16), do_b, preferred_element_type=F32)
                dpt = lax.dot_general(kv_ref[:, lo + QK_NOPE:lo + HEAD_PAD], do_b, _NT, preferred_element_type=F32)
                dst = (pt * (dpt - delta_sc[qi * hps + u]) * scale).astype(BF16)
                dk_sc[u] += jnp.dot(dst, q, preferred_element_type=F32)
                rows = pl.ds(pl.multiple_of(qi * tk, tk), tk)
                dq_ref[rows, lo:lo + HEAD_PAD] += lax.dot_general(dst, kc, tn_dims, preferred_element_type=F32)

        @pl.when(qi > ki)
        def _():
            step(False)

        @pl.when(qi == ki)
        def _():
            step(True)

        @pl.when(qi == nk - 1)
        def _():
            for u in range(hps):
                lo = u * HEAD_PAD
                dkv_ref[:, lo:lo + QK_NOPE] = dk_sc[u, :, 0:QK_NOPE].astype(BF16)
                dkv_ref[:, lo + QK_NOPE:lo + HEAD_PAD] = dv_sc[u].astype(BF16)
                dkpe_ref[u] = dk_sc[u, :, QK_NOPE:QK_NOPE + LANES]

    qclamp = lambda g, ki, qi: (jnp.maximum(qi, ki), g)
    outs, comm_outs = _call(
        body, grid=(heads // hps, nk, nk),
        in_specs=[pl.BlockSpec((tk, hps * HEAD_PAD), qclamp),
                  pl.BlockSpec((tk, hps * HEAD_PAD), lambda g, ki, qi: (ki, g)),
                  pl.BlockSpec((tk, LANES), lambda g, ki, qi: (ki, 0)),
                  pl.BlockSpec((tk, hps * V_HEAD), qclamp),
                  pl.BlockSpec((hps, 1, tk), lambda g, ki, qi: (g, 0, jnp.maximum(qi, ki))),
                  pl.BlockSpec((tk, hps * V_HEAD), lambda g, ki, qi: (jnp.where(ki == 0, qi, 0), g))],
        out_specs=[pl.BlockSpec((t, hps * HEAD_PAD), lambda g, ki, qi: (0, g)),
                   pl.BlockSpec((tk, hps * HEAD_PAD), lambda g, ki, qi: (ki, g)),
                   pl.BlockSpec((hps, tk, LANES), lambda g, ki, qi: (g, ki, 0))],
        out_shape=[jax.ShapeDtypeStruct((t, heads * HEAD_PAD), F32),
                   jax.ShapeDtypeStruct((t, heads * HEAD_PAD), BF16), jax.ShapeDtypeStruct((heads, t, LANES), F32)],
        scratch_shapes=[pltpu.VMEM((hps, tk, HEAD_PAD), F32), pltpu.VMEM((hps, tk, V_HEAD), F32),
                        pltpu.VMEM((nk * hps, 1, tk), F32)],
        sem=("parallel", "arbitrary", "arbitrary"), name="attn_bwd",
        args=(q_c, kv, kpe, do, lse_row, o), comm=comm)
    return outs[0], outs[1], outs[2], comm_outs


def _local_step(x, pos_col, target, small, shards, opt):
    t = x.shape[0]
    ql, kvl = small["q_norm_g"].shape[1], small["kv_norm_g"].shape[1]
    sw = small["sgu_norm_g"].shape[1]
    heads = (shards["w_uq"].shape[1] * N_DEV) // (QK_NOPE + QK_ROPE)
    big = {}
    early = ["w_in", "w_uq", "w_ukv"]
    big.update(_compute_layout(dict(zip(early, _all_gather([shards[k] for k in early]))), ql, kvl, heads, sw))
    half = QK_ROPE // 2
    lane = jnp.arange(LANES)
    inv_freq = ROPE_THETA ** (-jnp.arange(0, QK_ROPE, 2, dtype=F32) / QK_ROPE)
    inv_row = inv_freq[lane % half][None, :]
    sign_row = jnp.where((lane % QK_ROPE) < half, -1.0, 1.0).astype(F32)[None, :]
    cos, sin = _rope_tables(pos_col, inv_row, sign_row)
    ws = small["w_sgu"]
    b_col = small["b_sgu_col"]

    def arrived(names, bufs):
        big.update(_compute_layout(dict(zip(names, bufs)), ql, kvl, heads, sw))

    a = _norm_fwd(x, small["norm_mix_g"], "norm_mix_fwd")
    z_lat = _mm(a, big["w_lat_t"], tb=True, name="z_lat")
    z_uv, g_sgu = _mm(a, big["w_uv_t"], tb=True, name="z_uv", comm=_gather_stage(1, [shards["w_o_sgu"]]))
    z_g, (g_attn, g_sgu) = _mm(a, big["w_g_t"], tb=True, name="z_g",
                               comm=_join(_gather_stage(1, [shards["w_o_attn"]]), _gather_stage(2, g_sgu)))
    qn, kvn, kpe = _lat_fwd(z_lat, small["q_norm_g"], small["kv_norm_g"], cos, sin, ql, kvl)
    q_p, (g_attn, g_sgu) = _mm(qn, big["w_uq"], name="q_up",
                               comm=_join(_gather_stage(2, [g_attn]), _gather_stage(3, [g_sgu])))
    kv, (g_attn, g_out) = _mm(kvn, big["w_ukv"], out_dtype=BF16, name="kv_up",
                              comm=_join(_gather_stage(3, [g_attn]), _gather_stage(1, [shards["w_out"]])))
    arrived(["w_o_sgu", "w_o_attn"], [g_sgu, g_attn])
    q_c = _q_rope(q_p, cos, sin, False, "q_rope")
    attn, attn_b, lse, (w_gate, w_up) = _attn_fwd(
        q_c, kv, kpe, comm=_gather_stage(1, [shards["w_gate_ffn"], shards["w_up_ffn"]]))
    s_out = _sgu_fwd(z_uv, small["sgu_norm_g"], ws, b_col)
    y_sgu, (g_out,) = _mm(s_out, big["w_o_sgu"], name="y_sgu", comm=_gather_stage(2, [g_out]))
    y_attn, (w_gate, g_out) = _mm(attn_b, big["w_o_attn"], name="y_attn",
                                  comm=_join(_gather_stage(2, [w_gate]), _gather_stage(3, [g_out])))
    arrived(["w_out"], [g_out])
    merged, (w_up, w_gate) = _merge_fwd(y_attn, y_sgu, z_g, small["b_gate"],
                                        comm=_join(_gather_stage(2, [w_up]), _gather_stage(3, [w_gate])))
    h1, (w_up,) = _mm(merged, big["w_out"], add=x, name="h1", comm=_gather_stage(3, [w_up]))
    f = _norm_fwd(h1, small["norm_ffn_g"], "norm_ffn_fwd")
    gate, w_down = _mm(f, w_gate, tb=True, slab="n", name="ffn_gate", comm=_gather_stage(1, [shards["w_down_ffn"]]))
    up, w_down = _mm(f, w_up, tb=True, slab="n", name="ffn_up", comm=_gather_stage(2, w_down))
    ffn = gate.shape[2]
    gate, up = gate.reshape(N_DEV * t, ffn), up.reshape(N_DEV * t, ffn)
    act, (w_down,) = _swiglu_fwd(gate, up, comm=_gather_stage(3, w_down))
    act = act.reshape(N_DEV, t, ffn)
    h2 = _mm(act, w_down, slab="k", add=h1, name="h2")
    loss_row, dh2, dh2_b, d_norm_final = _loss_head(h2, small["norm_final_g"], target)

    def pair_sums(names, slabs, bufs):
        return [_pair_sum(g, b, "pair_sum_" + k) for k, g, b in zip(names, slabs, bufs)]

    parts, updates = {}, {}

    def update(k, comm=None):
        w, m, v = opt[k]
        updates[k], got = _adamw_shard(parts[k], w, m, v, "adamw_" + k, comm=comm)
        return got

    down_slabs = [_mm(act, dh2_b, ta=True, slab="m", out_dtype=BF16, name="dw_down")]
    dact, bufs = _mm(dh2_b, w_down, tb=True, slab="n", name="dact", comm=_to_sibling(down_slabs))
    down_pair = pair_sums(["w_down_ffn"], down_slabs, bufs)
    dgu = _swiglu_bwd(gate, up, dact.reshape(N_DEV * t, ffn)).reshape(2 * N_DEV, t, ffn)
    dw_gu, got = _mm(dgu, f, ta=True, slab="m", out_dtype=BF16, name="dw_gate_up", comm=_to_chips(down_pair))
    parts["w_down_ffn"] = got[0]
    gu_names = ["w_gate_ffn", "w_up_ffn"]
    df, bufs = _mm(dgu, w_gate, slab="k", name="df_gate", comm=_to_sibling([dw_gu, dw_gu], first=[0, N_DEV]))
    gu_pairs = [_pair_sum(dw_gu, b, "pair_sum_" + k, first=s0) for k, b, s0 in zip(gu_names, bufs, [0, N_DEV])]
    half = _pick(gu_pairs[1].shape[1], gu_pairs[1].shape[1] // 2, 2 * SUBLANES)
    df, up_parts = _mm(dgu, w_up, slab="k", a_slab0=N_DEV, add=df, name="df_up",
                       comm=_to_chips(gu_pairs[1:], rows=[("r", 0, half)]))
    dh1, dh1_b, d_norm_ffn = _norm_bwd(h1, small["norm_ffn_g"], df, dh2, "norm_ffn_bwd")
    dw_out = _mm(merged, dh1_b, ta=True, out_dtype=BF16, name="dw_out")
    out_slabs = [_slabs_from_rows(dw_out)]
    dmerged, bufs = _mm(dh1_b, big["w_out"], tb=True, name="dmerged", comm=_to_sibling(out_slabs))
    out_pair = pair_sums(["w_out"], out_slabs, bufs)
    dy_attn, dy_sgu, dz_g, d_b_gate = _merge_bwd(dmerged, y_attn, y_sgu, z_g, small["b_gate"])
    dw_o_sgu = _mm(s_out, dy_sgu, ta=True, out_dtype=BF16, name="dw_o_sgu")
    ds_out = _mm(dy_sgu, big["w_o_sgu"], tb=True, name="ds_out")
    dz_uv, d_ws, d_b_col, d_sgu_norm = _sgu_bwd(z_uv, ds_out, small["sgu_norm_g"], ws, b_col)
    dw_o_attn = _mm(attn_b, dy_attn, ta=True, out_dtype=BF16, name="dw_o_attn")
    mix_names = ["w_o_sgu", "w_o_attn"]
    mix_slabs = [_slabs_from_cols(dw_o_sgu), _slabs_from_rows(dw_o_attn)]
    dattn, bufs = _mm(dy_attn, big["w_o_attn"], tb=True, name="dattn", comm=_to_sibling(mix_slabs))
    mix_pairs = pair_sums(mix_names, mix_slabs, bufs)
    rest = ("r", half, gu_pairs[1].shape[1] - half)
    dq_c, dkv, dkpe_heads, got = _attn_bwd(
        q_c, kv, kpe, attn, dattn, lse,
        comm=_join(_to_chips(gu_pairs[:1]), _to_chips(gu_pairs[1:], rows=[rest], into=up_parts)))
    parts.update(zip(gu_names, got))
    dq_p = _q_rope(dq_c, cos, sin, True, "q_rope_bwd")
    dw_uq = _mm(qn, dq_p, ta=True, out_dtype=BF16, name="dw_uq")
    dw_ukv = _mm(kvn, dkv, ta=True, out_dtype=BF16, name="dw_ukv")
    dqn = _mm(dq_p, big["w_uq"], tb=True, name="dqn")
    dkvn = _mm(dkv, big["w_ukv"], tb=True, name="dkvn")
    dz_lat, d_q_norm, d_kv_norm = _lat_bwd(z_lat, small["q_norm_g"], small["kv_norm_g"], dqn, dkvn, dkpe_heads,
                                           cos, sin, ql, kvl)
    dw_g, got = _mm(dz_g, a, ta=True, out_dtype=BF16, name="dw_g", comm=_to_chips(out_pair))
    parts["w_out"] = got[0]
    dw_uv, got = _mm(dz_uv, a, ta=True, out_dtype=BF16, name="dw_uv", comm=_to_chips(mix_pairs[1:]))
    parts["w_o_attn"] = got[0]
    dw_lat, got = _mm(dz_lat, a, ta=True, out_dtype=BF16, name="dw_lat", comm=_to_chips(mix_pairs[:1]))
    parts["w_o_sgu"] = got[0]
    lat = ql + kvl + QK_ROPE
    dw_uq_cols = dw_uq.reshape(ql, heads, HEAD_PAD)[:, :, :QK_NOPE + QK_ROPE].reshape(ql, heads * (QK_NOPE + QK_ROPE))
    in_names = ["w_uq", "w_ukv", "w_in"]
    in_slabs = [_slabs_from_cols(dw_uq_cols), _slabs_from_cols(dw_ukv),
                _slabs_from_rows(jnp.concatenate([dw_lat[:lat], dw_uv, dw_g], axis=0))]
    da = _mm(dz_lat, big["w_lat_t"], name="da_lat")
    da, bufs = _mm(dz_uv, big["w_uv_t"], add=da, name="da_uv", comm=_to_sibling(in_slabs))
    uq_pair, ukv_pair, in_pair = pair_sums(in_names, in_slabs, bufs)
    cols = in_pair.shape[2]
    chunk = _pick(cols, cols // TAIL_CHUNKS)
    chunks = [("c", c0, chunk) for c0 in range(0, cols, chunk)]
    da, got = _mm(dz_g, big["w_g_t"], add=da, name="da_g",
                  comm=_to_chips([uq_pair, in_pair], rows=[None, chunks[0]]))
    parts["w_uq"], in_parts = got
    grad_x, _, d_norm_mix, got = _norm_bwd(x, small["norm_mix_g"], da, dh1, "norm_mix_bwd", comm=_to_chips([ukv_pair]))
    parts["w_ukv"] = got[0]
    hosts = ["w_gate_ffn", "w_up_ffn", "w_down_ffn", "w_out", "w_o_attn", "w_o_sgu", "w_uq", "w_ukv"]
    assert len(chunks) <= 1 + len(hosts)
    for i, k in enumerate(hosts):
        if 1 + i < len(chunks):
            in_parts = update(k, comm=_to_chips([in_pair], rows=[chunks[1 + i]], into=[in_parts]))[0]
        else:
            update(k)
    parts["w_in"] = in_parts
    update("w_in")

    gs = {"norm_mix_g": d_norm_mix, "b_gate": d_b_gate, "q_norm_g": d_q_norm, "kv_norm_g": d_kv_norm,
          "sgu_norm_g": d_sgu_norm, "w_sgu": d_ws, "b_sgu_col": d_b_col, "norm_ffn_g": d_norm_ffn,
          "norm_final_g": d_norm_final}
    return loss_row, grad_x, gs, updates


def _my_place():
    return lax.axis_index("x"), lax.axis_index("y"), lax.axis_index("c")


N_CHIPS = N_DEV // 2

_GATHER_SEMS = [[(3,), (3,), ()], [(4,), (4,)], [(1,), (1,)]]


def _halves(shape):
    r, c = shape
    if (c // 2) % LANES == 0:
        return ("c", 0, c // 2), ("c", c // 2, c // 2)
    assert (r // 2) % (2 * SUBLANES) == 0, shape
    return ("r", 0, r // 2), ("r", r // 2, r // 2)


def _gather_copies(stage, ins, outs, sems):
    x, y, c = _my_place()
    me, x_nbr, y_nbr, diag = 4 * x + 2 * y + c, 4 * (1 - x) + 2 * y + c, 4 * x + 2 * (1 - y) + c, 4 * (1 - x) + 2 * (1 - y) + c
    sibling = (x, y, 1 - c)

    def remote(w, k, src, dst, to):
        return pltpu.make_async_remote_copy(src_ref=src, dst_ref=dst, send_sem=sems[0].at[w, k], recv_sem=sems[1].at[w, k],
                                            device_id=to, device_id_type=MESH)

    out = []
    for w in range(len(outs)):
        if stage == 1:
            dst = outs[w].at[me]
            out.append(pltpu.make_async_copy(ins[w], dst, sems[2].at[w]))
            out += [remote(w, k, ins[w], dst, to) for k, to in enumerate([sibling, (1 - x, y, c), (x, 1 - y, c)])]
        elif stage == 2:
            first, second = _halves(outs[w].shape[1:])
            out.append(remote(w, 0, _window(ins[w], x_nbr, first), _window(outs[w], x_nbr, first), (x, 1 - y, c)))
            out.append(remote(w, 1, _window(ins[w], y_nbr, second), _window(outs[w], y_nbr, second), (1 - x, y, c)))
            out.append(remote(w, 2, ins[w].at[x_nbr], outs[w].at[x_nbr], sibling))
            out.append(remote(w, 3, ins[w].at[y_nbr], outs[w].at[y_nbr], sibling))
        else:
            out.append(remote(w, 0, ins[w].at[diag], outs[w].at[diag], sibling))
    return out


def _gather_stage(stage, arrays):
    n = len(arrays)

    def start(ins, outs, sems):
        for cp in _gather_copies(stage, ins, outs, sems):
            cp.start()

    def finish(ins, outs, sems):
        for cp in _gather_copies(stage, ins, outs, sems):
            cp.wait()

    shapes = [jax.ShapeDtypeStruct(((N_DEV,) + a.shape) if stage == 1 else a.shape, a.dtype) for a in arrays]
    return _Comm(arrays, shapes, [pltpu.SemaphoreType.DMA((n,) + s) for s in _GATHER_SEMS[stage - 1]], start, finish,
                 aliases=None if stage == 1 else {w: w for w in range(n)})


def _join(*comms):
    ins, shapes, sems, aliases, spans = [], [], [], {}, []
    for cm in comms:
        spans.append((len(ins), len(ins) + len(cm.ins), len(shapes), len(shapes) + len(cm.out_shapes),
                      len(sems), len(sems) + len(cm.sems)))
        aliases.update({len(ins) + i: len(shapes) + o for i, o in cm.aliases.items()})
        ins, shapes, sems = ins + cm.ins, shapes + cm.out_shapes, sems + cm.sems

    def each(half):
        def run(i_refs, o_refs, s_refs):
            for cm, (i0, i1, o0, o1, s0, s1) in zip(comms, spans):
                getattr(cm, half)(i_refs[i0:i1], o_refs[o0:o1], s_refs[s0:s1])
        return run

    return _Comm(ins, shapes, sems, each("start"), each("finish"), aliases)


def _all_gather(shards):
    n = len(shards)
    n_sems = [len(s) for s in _GATHER_SEMS]

    def body(*refs):
        ins, outs, sems = refs[:n], refs[n:2 * n], refs[2 * n:]
        s0 = 0
        for stage in (1, 2, 3):
            mine = sems[s0:s0 + n_sems[stage - 1]]
            s0 += n_sems[stage - 1]
            copies = _gather_copies(stage, ins if stage == 1 else outs, outs, mine)
            for cp in copies:
                cp.start()
            for cp in copies:
                cp.wait()

    any_spec = pl.BlockSpec(memory_space=pl.ANY)
    return pl.pallas_call(
        body, in_specs=[any_spec] * n, out_specs=[any_spec] * n,
        out_shape=[jax.ShapeDtypeStruct((N_DEV,) + s.shape, s.dtype) for s in shards],
        scratch_shapes=[pltpu.SemaphoreType.DMA((n,) + s) for stage in _GATHER_SEMS for s in stage],
        compiler_params=pltpu.CompilerParams(has_side_effects=True), name="all_gather_weights")(*shards)


def _to_sibling(grads, first=None):
    n = len(grads)
    first = first or [0] * n

    def copies(ins, outs, sems):
        x, y, c = _my_place()
        send_sems, recv_sems = sems
        return [pltpu.make_async_remote_copy(
            src_ref=ins[w].at[first[w] + 2 * i + (1 - c)], dst_ref=outs[w].at[i], send_sem=send_sems.at[w, i],
            recv_sem=recv_sems.at[w, i], device_id=(x, y, 1 - c), device_id_type=MESH)
            for w in range(n) for i in range(N_CHIPS)]

    def start(ins, outs, sems):
        for cp in copies(ins, outs, sems):
            cp.start()

    def finish(ins, outs, sems):
        for cp in copies(ins, outs, sems):
            cp.wait()

    return _Comm(grads, [jax.ShapeDtypeStruct((N_CHIPS,) + g.shape[1:], g.dtype) for g in grads],
                 [pltpu.SemaphoreType.DMA((n, N_CHIPS)), pltpu.SemaphoreType.DMA((n, N_CHIPS))], start, finish)


def _window(ref, slab, win):
    if win is None:
        return ref.at[slab]
    if win[0] == "r":
        return ref.at[slab, pl.ds(win[1], win[2])]
    return ref.at[slab, slice(None), pl.ds(win[1], win[2])]


def _to_chips(parts, rows=None, into=None):
    n = len(parts)
    rows = rows or [None] * n

    def copies(ins, outs, sems):
        x, y, c = _my_place()
        send_sems, recv_sems, local_sems = sems
        mine = 2 * x + y
        chips = [(1 - x, y), (x, 1 - y), (1 - x, 1 - y)]
        remote = [pltpu.make_async_remote_copy(
            src_ref=_window(ins[w], 2 * cx + cy, rows[w]), dst_ref=_window(outs[w], mine, rows[w]),
            send_sem=send_sems.at[w, j], recv_sem=recv_sems.at[w, j], device_id=(cx, cy, c), device_id_type=MESH)
            for w in range(n) for j, (cx, cy) in enumerate(chips)]
        local = [pltpu.make_async_copy(_window(ins[w], mine, rows[w]), _window(outs[w], mine, rows[w]),
                                       local_sems.at[w]) for w in range(n)]
        return remote + local

    def start(ins, outs, sems):
        for cp in copies(ins, outs, sems):
            cp.start()

    def finish(ins, outs, sems):
        for cp in copies(ins, outs, sems):
            cp.wait()

    return _Comm(list(parts) + list(into or []), [jax.ShapeDtypeStruct(p.shape, p.dtype) for p in parts],
                 [pltpu.SemaphoreType.DMA((n, N_CHIPS - 1)), pltpu.SemaphoreType.DMA((n, N_CHIPS - 1)),
                  pltpu.SemaphoreType.DMA((n,))], start, finish,
                 aliases={n + w: w for w in range(n)} if into else None)


def _pair_sum(g, buf, name, first=0):
    _, r, c = g.shape
    tr, tc = _shard_tile(r, c, 4 * SHARD_TILE_ELEMS, 1024)
    core = (lax.axis_index("c") + first).astype(jnp.int32).reshape(1)

    def body(core_ref, g_ref, b_ref, o_ref):
        o_ref[...] = (g_ref[...].astype(F32) + b_ref[...].astype(F32)).astype(o_ref.dtype)

    blk = (1, tr, tc)
    return pl.pallas_call(
        body, grid_spec=pltpu.PrefetchScalarGridSpec(
            num_scalar_prefetch=1, grid=(N_CHIPS, r // tr, c // tc),
            in_specs=[pl.BlockSpec(blk, lambda i, j, l, core_ref: (2 * i + core_ref[0], j, l)),
                      pl.BlockSpec(blk, lambda i, j, l, core_ref: (i, j, l))],
            out_specs=pl.BlockSpec(blk, lambda i, j, l, core_ref: (i, j, l))),
        out_shape=jax.ShapeDtypeStruct(buf.shape, buf.dtype),
        compiler_params=_params(("parallel", "parallel", "parallel")), name=name)(core, g, buf)


def _all_reduce_pack(pack):
    r = pack.shape[0]

    def body(x_ref, out_ref, gath_ref, send_sems, recv_sems, local_sem):
        x, y, c = _my_place()
        me, sibling = (x, y, c), (x, y, 1 - c)
        chips = [(1 - x, y), (x, 1 - y), (1 - x, 1 - y)]

        def slab(place):
            return gath_ref.at[4 * place[0] + 2 * place[1] + place[2]]

        def copy(k, place, to, src=None):
            return pltpu.make_async_remote_copy(
                src_ref=slab(place) if src is None else src, dst_ref=slab(place),
                send_sem=send_sems.at[k], recv_sem=recv_sems.at[k], device_id=to, device_id_type=MESH)

        mine = pltpu.make_async_copy(x_ref, slab(me), local_sem)
        mine.start()
        first = [copy(0, me, sibling, src=x_ref)]
        first += [copy(1 + j, me, (*chip, c), src=x_ref) for j, chip in enumerate(chips)]
        for cp in first:
            cp.start()
        passed = [copy(4 + j, (*chip, c), sibling) for j, chip in enumerate(chips)]
        for j, chip in enumerate(chips):
            copy(1 + j, (*chip, c), me).wait_recv()
            passed[j].start()
        copy(0, sibling, me).wait_recv()
        for j, chip in enumerate(chips):
            copy(4 + j, (*chip, 1 - c), me).wait_recv()
        for cp in first + passed:
            cp.wait_send()
        mine.wait()
        acc = gath_ref[0]
        for i in range(1, N_DEV):
            acc = acc + gath_ref[i]
        out_ref[...] = acc

    vmem = pl.BlockSpec(memory_space=pltpu.VMEM)
    return pl.pallas_call(
        body, in_specs=[vmem], out_specs=vmem, out_shape=jax.ShapeDtypeStruct(pack.shape, F32),
        scratch_shapes=[pltpu.VMEM((N_DEV, r, LANES), F32), pltpu.SemaphoreType.DMA((7,)),
                        pltpu.SemaphoreType.DMA((7,)), pltpu.SemaphoreType.DMA],
        compiler_params=pltpu.CompilerParams(vmem_limit_bytes=VMEM_LIMIT), name="all_reduce_small")(pack)


def _adamw_math(w, g, m, v):
    m = ADAM_B1 * m + (1.0 - ADAM_B1) * g
    v = ADAM_B2 * v + (1.0 - ADAM_B2) * (g * g)
    m_hat = m / (1.0 - ADAM_B1 ** ADAM_STEP)
    v_hat = v / (1.0 - ADAM_B2 ** ADAM_STEP)
    delta = -ADAM_LR * (m_hat / (jnp.sqrt(v_hat) + ADAM_EPS) + ADAM_WD * w)
    return delta, m, v


def _adamw_shard(parts, w, m, v, name, comm=None):
    r, c = w.shape
    n_parts = parts.shape[0]
    tr, tc = _shard_tile(r, c)

    def body(p_ref, w_ref, m_ref, v_ref, g_ref, d_ref, nm_ref, nv_ref):
        g = p_ref[0].astype(F32)
        for i in range(1, n_parts):
            g = g + p_ref[i].astype(F32)
        g_ref[...] = g
        d_ref[...], nm_ref[...], nv_ref[...] = _adamw_math(w_ref[...], g, m_ref[...], v_ref[...])

    spec = pl.BlockSpec((tr, tc), lambda i, j: (i, j))
    outs, comm_outs = _call(
        body, grid=(r // tr, c // tc),
        in_specs=[pl.BlockSpec((n_parts, tr, tc), lambda i, j: (0, i, j)), spec, spec, spec],
        out_specs=[spec] * 4, out_shape=[jax.ShapeDtypeStruct((r, c), F32)] * 4,
        sem=("parallel", "parallel"), name=name, args=(parts, w, m, v), comm=comm)
    return outs, comm_outs


def _adamw_pack(g, w, m, v):
    r, c = w.shape

    def body(g_ref, w_ref, m_ref, v_ref, d_ref, nm_ref, nv_ref):
        d_ref[...], nm_ref[...], nv_ref[...] = _adamw_math(w_ref[...], g_ref[...], m_ref[...], v_ref[...])

    return pl.pallas_call(
        body, in_specs=[_full((r, c))] * 4, out_specs=[_full((r, c))] * 3, grid=(1,),
        out_shape=[jax.ShapeDtypeStruct((r, c), F32)] * 3,
        compiler_params=_params(("arbitrary",)), name="adamw_small")(g, w, m, v)


def _cols_from_slabs(g):
    return jnp.transpose(g, (1, 0, 2)).reshape(g.shape[1], N_DEV * g.shape[2])


def _slabs_from_cols(w):
    r, c8 = w.shape
    return jnp.transpose(w.reshape(r, N_DEV, c8 // N_DEV), (1, 0, 2))


def _rows_from_slabs(g):
    return g.reshape(N_DEV * g.shape[1], g.shape[2])


def _slabs_from_rows(w):
    return w.reshape(N_DEV, w.shape[0] // N_DEV, w.shape[1])


def _compute_layout(gathered, ql, kvl, heads, sw):
    out = {}
    for k, g in gathered.items():
        if k == "w_in":
            lat = ql + kvl + QK_ROPE
            w_in_t = _rows_from_slabs(g)
            out["w_lat_t"] = jnp.pad(w_in_t[:lat], ((0, LANES - QK_ROPE), (0, 0)))
            out["w_uv_t"] = w_in_t[lat:lat + 2 * sw]
            out["w_g_t"] = w_in_t[lat + 2 * sw:]
        elif k == "w_uq":
            per_head = _cols_from_slabs(g).reshape(ql, heads, QK_NOPE + QK_ROPE)
            pad = HEAD_PAD - QK_NOPE - QK_ROPE
            out["w_uq"] = jnp.pad(per_head, ((0, 0), (0, 0), (0, pad))).reshape(ql, heads * HEAD_PAD)
        elif k in ("w_o_attn", "w_out", "w_down_ffn"):
            out[k.removesuffix("_ffn")] = _rows_from_slabs(g)
        else:
            out[k.removesuffix("_ffn")] = _cols_from_slabs(g)
    return out


_SMALL =["norm_mix_g", "b_gate", "q_norm_g", "kv_norm_g", "sgu_norm_g", "w_sgu", "b_sgu", "norm_ffn_g", "norm_final_g"]
_BIG = ["w_in", "w_uq", "w_ukv", "w_o_attn", "w_o_sgu", "w_out", "w_gate_ffn", "w_up_ffn", "w_down_ffn"]
_TRANSPOSED = ("w_in", "w_gate_ffn", "w_up_ffn")
_ORDER = ["norm_mix_g", "w_in", "b_gate", "q_norm_g", "w_uq", "kv_norm_g", "w_ukv", "w_o_attn", "sgu_norm_g", "w_sgu",
          "b_sgu", "w_o_sgu", "w_out", "norm_ffn_g", "w_gate_ffn", "w_up_ffn", "w_down_ffn", "norm_final_g"]


def _pack_rows(parts):
    rows, sizes = [], []
    for p in parts:
        flat = p.reshape(-1)
        n = flat.shape[0]
        padded = -(-n // (SUBLANES * LANES)) * (SUBLANES * LANES)
        rows.append(jnp.pad(flat, (0, padded - n)).reshape(padded // LANES, LANES))
        sizes.append((n, padded // LANES))
    return jnp.concatenate(rows, axis=0), sizes


def _unpack_rows(pack, sizes, shapes):
    out, r0 = [], 0
    for (n, nr), shp in zip(sizes, shapes):
        out.append(pack[r0:r0 + nr].reshape(-1)[:n].reshape(shp))
        r0 += nr
    return out


def kernel(x, positions, norm_mix_g, w_in, b_gate, q_norm_g, w_uq, kv_norm_g, w_ukv, w_o_attn, sgu_norm_g, w_sgu, b_sgu, w_o_sgu, w_out, norm_ffn_g, w_gate_ffn, w_up_ffn, w_down_ffn, norm_final_g, loss_target, m_norm_mix_g, m_w_in, m_b_gate, m_q_norm_g, m_w_uq, m_kv_norm_g, m_w_ukv, m_w_o_attn, m_sgu_norm_g, m_w_sgu, m_b_sgu, m_w_o_sgu, m_w_out, m_norm_ffn_g, m_w_gate_ffn, m_w_up_ffn, m_w_down_ffn, m_norm_final_g, v_norm_mix_g, v_w_in, v_b_gate, v_q_norm_g, v_w_uq, v_kv_norm_g, v_w_ukv, v_w_o_attn, v_sgu_norm_g, v_w_sgu, v_b_sgu, v_w_o_sgu, v_w_out, v_norm_ffn_g, v_w_gate_ffn, v_w_up_ffn, v_w_down_ffn, v_norm_final_g):
    wts = dict(norm_mix_g=norm_mix_g, w_in=w_in, b_gate=b_gate, q_norm_g=q_norm_g, w_uq=w_uq, kv_norm_g=kv_norm_g,
               w_ukv=w_ukv, w_o_attn=w_o_attn, sgu_norm_g=sgu_norm_g, w_sgu=w_sgu, b_sgu=b_sgu, w_o_sgu=w_o_sgu,
               w_out=w_out, norm_ffn_g=norm_ffn_g, w_gate_ffn=w_gate_ffn, w_up_ffn=w_up_ffn, w_down_ffn=w_down_ffn,
               norm_final_g=norm_final_g)
    mom = dict(norm_mix_g=m_norm_mix_g, w_in=m_w_in, b_gate=m_b_gate, q_norm_g=m_q_norm_g, w_uq=m_w_uq,
               kv_norm_g=m_kv_norm_g, w_ukv=m_w_ukv, w_o_attn=m_w_o_attn, sgu_norm_g=m_sgu_norm_g, w_sgu=m_w_sgu,
               b_sgu=m_b_sgu, w_o_sgu=m_w_o_sgu, w_out=m_w_out, norm_ffn_g=m_norm_ffn_g, w_gate_ffn=m_w_gate_ffn,
               w_up_ffn=m_w_up_ffn, w_down_ffn=m_w_down_ffn, norm_final_g=m_norm_final_g)
    var = dict(norm_mix_g=v_norm_mix_g, w_in=v_w_in, b_gate=v_b_gate, q_norm_g=v_q_norm_g, w_uq=v_w_uq,
               kv_norm_g=v_kv_norm_g, w_ukv=v_w_ukv, w_o_attn=v_w_o_attn, sgu_norm_g=v_sgu_norm_g, w_sgu=v_w_sgu,
               b_sgu=v_b_sgu, w_o_sgu=v_w_o_sgu, w_out=v_w_out, norm_ffn_g=v_norm_ffn_g, w_gate_ffn=v_w_gate_ffn,
               w_up_ffn=v_w_up_ffn, w_down_ffn=v_w_down_ffn, norm_final_g=v_norm_final_g)

    t, d = x.shape[1], x.shape[2]
    ql, kvl = q_norm_g.shape[1], kv_norm_g.shape[1]
    heads = (w_uq.shape[2] * N_DEV) // (QK_NOPE + QK_ROPE)
    sw = sgu_norm_g.shape[1]

    def shard(a, k):
        return a[0].T if k in _TRANSPOSED else a[0]

    def unshard(a, k):
        return (a.T if k in _TRANSPOSED else a).reshape(wts[k].shape)

    opt = {k: (shard(wts[k], k), shard(mom[k], k), shard(var[k], k)) for k in _BIG}
    shards = {k: opt[k][0].astype(BF16) for k in _BIG}
    small = {
        "norm_mix_g": norm_mix_g, "b_gate": b_gate, "q_norm_g": q_norm_g, "kv_norm_g": kv_norm_g,
        "sgu_norm_g": sgu_norm_g, "w_sgu": w_sgu[0], "b_sgu_col": b_sgu[0][:, :, None], "norm_ffn_g": norm_ffn_g,
        "norm_final_g": norm_final_g[None, :],
    }

    loss_row, grad_x, gs, updates = _local_step(x[0], positions.reshape(t, 1), loss_target[0], small, shards, opt)
    grads, deltas, new_m, new_v = {}, {}, {}, {}
    for k in _BIG:
        grads[k], deltas[k], new_m[k], new_v[k] = (unshard(a, k) for a in updates[k])

    small_grads = [gs["norm_mix_g"], gs["b_gate"], gs["q_norm_g"], gs["kv_norm_g"], gs["sgu_norm_g"], gs["w_sgu"],
                   gs["b_sgu_col"], gs["norm_ffn_g"], gs["norm_final_g"]]
    pack, sizes = _pack_rows([loss_row] + small_grads)
    total = _all_reduce_pack(pack)
    shapes = [(1, LANES)] + [wts[k].shape for k in _SMALL]
    unpacked = _unpack_rows(total, sizes, shapes)
    loss = unpacked[0][0, 0]
    for k, g in zip(_SMALL, unpacked[1:]):
        grads[k] = g
    g_pack = total[sizes[0][1]:]
    w_pack, _ = _pack_rows([wts[k] for k in _SMALL])
    m_pack, _ = _pack_rows([mom[k] for k in _SMALL])
    v_pack, _ = _pack_rows([var[k] for k in _SMALL])
    d_pack, nm_pack, nv_pack = _adamw_pack(g_pack, w_pack, m_pack, v_pack)
    small_shapes = [wts[k].shape for k in _SMALL]
    for store, pk in ((deltas, d_pack), (new_m, nm_pack), (new_v, nv_pack)):
        for k, a in zip(_SMALL, _unpack_rows(pk, sizes[1:], small_shapes)):
            store[k] = a

    return (loss, grad_x[None], *[grads[k] for k in _ORDER], *[deltas[k] for k in _ORDER],
            *[new_m[k] for k in _ORDER], *[new_v[k] for k in _ORDER])
```

```python
import functools
import math

import jax
import jax.numpy as jnp
from jax import lax
from jax.experimental import pallas as pl
from jax.experimental.pallas import tpu as pltpu

F32 = jnp.float32
BF16 = jnp.bfloat16

N_DEV = 8
N_HEADS = 16
QK_NOPE = 128
QK_ROPE = 64
V_HEAD = 128
HEAD_PAD = 256
ROPE_THETA = 10000.0
CHUNK = 128
SGU_GROUP = 128
RMS_EPS = 1e-6
LANES = 128
SUBLANES = 8

ADAM_LR = 0.001
ADAM_B1 = 0.9
ADAM_B2 = 0.999
ADAM_EPS = 1e-08
ADAM_WD = 0.01
ADAM_STEP = 10

VMEM_LIMIT = 48 * 1024 * 1024
MM_TILE = (2048, 512, 2048)
MM_TILE_TA = (512, 2048)
ATTN_TILE = 512
HEADS_PER_STEP = (4, 2)
ROW_KERNEL_BYTES = 24 * 1024 * 1024
SHARD_TILE_ELEMS = 256 * 1024
SLABS_PER_STEP = 2
TAIL_CHUNKS = 4
NEG_BIG = -1e30
MESH = pl.DeviceIdType.MESH


def _pick(n, target, mult=LANES):
    best = None
    d = mult
    while d <= min(n, target):
        if n % d == 0:
            best = d
        d += mult
    return best or n


def _row_tile(t, width, n_blocks, mult=2 * SUBLANES):
    return _pick(t, max(mult, ROW_KERNEL_BYTES // (3 * n_blocks * width * 4)), mult)


def _shard_tile(r, c, elems=SHARD_TILE_ELEMS, max_rows=256):
    tr = _pick(r, max_rows, 2 * SUBLANES)
    return tr, _pick(c, max(LANES, elems // tr))


def _params(sem):
    return pltpu.CompilerParams(dimension_semantics=sem, vmem_limit_bytes=VMEM_LIMIT)


def _full(shape):
    nd = len(shape)
    return pl.BlockSpec(shape, lambda *_: (0,) * nd)


def _rows(tr, w, cb=0):
    return pl.BlockSpec((tr, w), lambda i: (i, cb))


class _Comm:
    def __init__(self, ins, out_shapes, sems, start, finish, aliases=None):
        self.ins, self.out_shapes, self.sems, self.start, self.finish = list(ins), list(out_shapes), list(sems), start, finish
        self.aliases = dict(aliases or {})


def _call(body, *, grid, in_specs, out_specs, out_shape, scratch_shapes=(), sem, name, args, comm=None):
    if comm is None:
        outs = pl.pallas_call(body, grid=grid, in_specs=list(in_specs), out_specs=list(out_specs),
                              out_shape=list(out_shape), scratch_shapes=list(scratch_shapes),
                              compiler_params=_params(sem), name=name)(*args)
        return list(outs), []
    n_in, n_out, n_sc = len(in_specs), len(out_shape), len(scratch_shapes)
    nci, nco = len(comm.ins), len(comm.out_shapes)

    def hosted(*refs):
        ins, refs = refs[:n_in], refs[n_in:]
        cins, refs = refs[:nci], refs[nci:]
        outs, refs = refs[:n_out], refs[n_out:]
        couts, refs = refs[:nco], refs[nco:]
        scratch, csems = refs[:n_sc], refs[n_sc:]
        ids = [pl.program_id(i) for i in range(len(grid))]
        first = functools.reduce(jnp.logical_and, [i == 0 for i in ids])
        last = functools.reduce(jnp.logical_and, [i == g - 1 for i, g in zip(ids, grid)])

        @pl.when(first)
        def _():
            comm.start(cins, couts, csems)

        body(*ins, *outs, *scratch)

        @pl.when(last)
        def _():
            comm.finish(cins, couts, csems)

    any_spec = pl.BlockSpec(memory_space=pl.ANY)
    res = pl.pallas_call(
        hosted, grid=grid, in_specs=list(in_specs) + [any_spec] * nci, out_specs=list(out_specs) + [any_spec] * nco,
        out_shape=list(out_shape) + comm.out_shapes, scratch_shapes=list(scratch_shapes) + comm.sems,
        input_output_aliases={n_in + i: n_out + o for i, o in comm.aliases.items()},
        compiler_params=pltpu.CompilerParams(dimension_semantics=("arbitrary",) * len(grid),
                                             vmem_limit_bytes=VMEM_LIMIT, has_side_effects=True),
        name=name)(*args, *comm.ins)
    return list(res[:n_out]), list(res[n_out:])


def _swiglu_grads(g, u, d):
    s = 1.0 / (1.0 + jnp.exp(-g))
    return (d * u * (s * (1.0 + g * (1.0 - s)))).astype(BF16), (d * (g * s)).astype(BF16)


def _mm(a, b, *, ta=False, tb=False, add=None, out_dtype=F32, tm=None, tn=None, tk=None, name, comm=None,
        slab=None, a_slab0=0, swiglu=None):
    sq = None
    if ta:
        tm, tn = tm or MM_TILE_TA[0], tn or MM_TILE_TA[1]
    if slab is None:
        m, k = (a.shape[1], a.shape[0]) if ta else a.shape
        n = b.shape[0] if tb else b.shape[1]
        assert k == (b.shape[1] if tb else b.shape[0]), (a.shape, b.shape, ta, tb)
        tm, tn, tk = _pick(m, tm or MM_TILE[0]), _pick(n, tn or MM_TILE[1]), _pick(k, tk or MM_TILE[2])
        grid = (m // tm, n // tn, k // tk)
        a_spec = pl.BlockSpec((tk, tm), lambda i, j, kk: (kk, i)) if ta else pl.BlockSpec((tm, tk), lambda i, j, kk: (i, kk))
        b_spec = pl.BlockSpec((tn, tk), lambda i, j, kk: (j, kk)) if tb else pl.BlockSpec((tk, tn), lambda i, j, kk: (kk, j))
        o_spec, o_shape = pl.BlockSpec((tm, tn), lambda i, j, kk: (i, j)), (m, n)
    elif slab == "n":
        m, k = (a.shape[1], a.shape[0]) if ta else a.shape
        s, c = b.shape[0], (b.shape[1] if tb else b.shape[2])
        assert k == (b.shape[2] if tb else b.shape[1]), (a.shape, b.shape, ta, tb)
        tm, tn, tk = _pick(m, tm or MM_TILE[0]), c, _pick(k, tk or MM_TILE[2])
        grid = (m // tm, s, k // tk)
        a_spec = pl.BlockSpec((tk, tm), lambda i, j, kk: (kk, i)) if ta else pl.BlockSpec((tm, tk), lambda i, j, kk: (i, kk))
        b_spec = (pl.BlockSpec((sq, c, tk), lambda i, j, kk: (j, 0, kk)) if tb
                  else pl.BlockSpec((sq, tk, c), lambda i, j, kk: (j, kk, 0)))
        o_spec, o_shape = pl.BlockSpec((sq, tm, c), lambda i, j, kk: (j, i, 0)), (s, m, c)
    elif slab == "m":
        assert ta and not tb
        s, k, c = a.shape
        n = b.shape[1]
        assert k == b.shape[0], (a.shape, b.shape)
        tm, tn, tk = c, _pick(n, tn or MM_TILE[1]), _pick(k, tk or MM_TILE[2])
        grid = (s, n // tn, k // tk)
        a_spec = pl.BlockSpec((sq, tk, c), lambda i, j, kk: (i, kk, 0))
        b_spec = pl.BlockSpec((tk, tn), lambda i, j, kk: (kk, j))
        o_spec, o_shape = pl.BlockSpec((sq, c, tn), lambda i, j, kk: (i, 0, j)), (s, c, n)
    else:
        assert slab == "k" and not ta
        s, c = b.shape[0], (b.shape[2] if tb else b.shape[1])
        m, n = a.shape[1], (b.shape[1] if tb else b.shape[2])
        assert a.shape[2] == c and a.shape[0] >= a_slab0 + s, (a.shape, b.shape, a_slab0)
        tm, tn, tk = _pick(m, tm or MM_TILE[0]), _pick(n, tn or MM_TILE[1]), c
        per_step = SLABS_PER_STEP if (s % SLABS_PER_STEP == 0 and a_slab0 % SLABS_PER_STEP == 0) else 1
        first = a_slab0 // per_step
        grid = (m // tm, n // tn, s // per_step)
        a_spec = pl.BlockSpec((per_step, tm, c), lambda i, j, kk: (kk + first, i, 0))
        b_spec = (pl.BlockSpec((per_step, tn, c), lambda i, j, kk: (kk, j, 0)) if tb
                  else pl.BlockSpec((per_step, c, tn), lambda i, j, kk: (kk, 0, j)))
        o_spec, o_shape = pl.BlockSpec((tm, tn), lambda i, j, kk: (i, j)), (m, n)
    nk = grid[2]
    dims = (((0 if ta else 1,), (1 if tb else 0,)), ((), ()))

    def product(a_ref, b_ref):
        if slab != "k":
            return lax.dot_general(a_ref[...].astype(BF16), b_ref[...].astype(BF16), dims, preferred_element_type=F32)
        r = None
        for u in range(a_ref.shape[0]):
            p = lax.dot_general(a_ref[u].astype(BF16), b_ref[u].astype(BF16), dims, preferred_element_type=F32)
            r = p if r is None else r + p
        return r

    if swiglu is not None:
        assert slab == "n" and add is None
        o_block = pl.BlockSpec((2, sq, tm, c), lambda i, j, kk: (0, j, i, 0))
        o_shape, out_dtype = (2,) + o_shape, BF16

    def body(*refs):
        a_ref, b_ref = refs[:2]
        add_ref = refs[2] if add is not None else None
        gate_ref, up_ref = refs[2:4] if swiglu is not None else (None, None)
        o_ref = refs[2 + (add is not None) + 2 * (swiglu is not None)]
        acc_ref = refs[-1] if nk > 1 else None

        def finish(r):
            if swiglu is not None:
                o_ref[0], o_ref[1] = _swiglu_grads(gate_ref[...], up_ref[...], r)
                return
            if add_ref is not None:
                r = r + add_ref[...].astype(F32)
            o_ref[...] = r.astype(o_ref.dtype)

        if nk == 1:
            finish(product(a_ref, b_ref))
            return
        kk = pl.program_id(2)

        @pl.when(kk == 0)
        def _():
            acc_ref[...] = product(a_ref, b_ref)

        if nk > 2:
            @pl.when(jnp.logical_and(kk > 0, kk < nk - 1))
            def _():
                acc_ref[...] += product(a_ref, b_ref)

        @pl.when(kk == nk - 1)
        def _():
            finish(acc_ref[...] + product(a_ref, b_ref))

    in_specs = [a_spec, b_spec] + ([o_spec] if add is not None else []) + ([o_spec] * 2 if swiglu is not None else [])
    args = (a, b) + ((add,) if add is not None else ()) + (tuple(swiglu) if swiglu is not None else ())
    if swiglu is not None:
        o_spec = o_block
    outs, comm_outs = _call(
        body, grid=grid, in_specs=in_specs, out_specs=[o_spec],
        out_shape=[jax.ShapeDtypeStruct(o_shape, out_dtype)],
        scratch_shapes=[pltpu.VMEM((tm, tn), F32)] if nk > 1 else [],
        sem=("parallel", "parallel", "arbitrary"), name=name, args=args, comm=comm)
    return outs[0] if comm is None else (outs[0], comm_outs)


def _rms_scale(x):
    return lax.rsqrt(jnp.mean(x * x, axis=-1, keepdims=True) + RMS_EPS)


def _rms_bwd(xhat, r, g, dy):
    t = dy * g
    dx = r * (t - xhat * jnp.mean(t * xhat, axis=-1, keepdims=True))
    return dx, dy * xhat


_GELU_C = math.sqrt(2.0 / math.pi)


def _gelu(x):
    return x * (0.5 * (1.0 + jnp.tanh(_GELU_C * (x + 0.044715 * (x * x * x)))))


def _gelu_and_grad(x):
    t = jnp.tanh(_GELU_C * (x + 0.044715 * (x * x * x)))
    cdf = 0.5 * (1.0 + t)
    return x * cdf, cdf + x * (0.5 * (1.0 - t * t) * (_GELU_C * (1.0 + 3.0 * 0.044715 * (x * x))))


def _sigmoid(x):
    return 1.0 / (1.0 + jnp.exp(-x))


def _swap_halves(x):
    lane = lax.broadcasted_iota(jnp.int32, x.shape, 1)
    first = (lane % QK_ROPE) < (QK_ROPE // 2)
    return jnp.where(first, pltpu.roll(x, LANES - QK_ROPE // 2, 1), pltpu.roll(x, QK_ROPE // 2, 1))


def _rope(x, cos, sin_signed):
    return x * cos + _swap_halves(x) * sin_signed


def _rope_bwd(d, cos, sin_signed):
    return d * cos + _swap_halves(d * sin_signed)


def _rope_tables(pos_col, inv_freq_row, sign_row):
    t = pos_col.shape[0]
    tr = _pick(t, 512, SUBLANES)

    def body(p_ref, f_ref, s_ref, cos_ref, sin_ref):
        ang = p_ref[...].astype(F32) * f_ref[...]
        cos_ref[...] = jnp.cos(ang)
        sin_ref[...] = jnp.sin(ang) * s_ref[...]

    return pl.pallas_call(
        body, grid=(t // tr,), in_specs=[_rows(tr, 1), _full((1, LANES)), _full((1, LANES))],
        out_specs=[_rows(tr, LANES), _rows(tr, LANES)],
        out_shape=[jax.ShapeDtypeStruct((t, LANES), F32)] * 2,
        compiler_params=_params(("parallel",)), name="rope_tables")(pos_col, inv_freq_row, sign_row)


def _norm_fwd(x, g, name):
    t, d = x.shape
    tr = _row_tile(t, d, 2)

    def body(x_ref, g_ref, y_ref):
        xv = x_ref[...]
        y_ref[...] = (xv * _rms_scale(xv) * g_ref[...]).astype(BF16)

    return pl.pallas_call(
        body, grid=(t // tr,), in_specs=[_rows(tr, d), _full((1, d))], out_specs=_rows(tr, d),
        out_shape=jax.ShapeDtypeStruct((t, d), BF16), compiler_params=_params(("parallel",)), name=name)(x, g)


def _lat_fwd(z_lat, qg, kvg, cos, sin, ql, kvl):
    t = z_lat.shape[0]
    tr = _row_tile(t, z_lat.shape[1], 2)

    def body(z_ref, qg_ref, kvg_ref, cos_ref, sin_ref, qn_ref, kvn_ref, kpe_ref):
        q = z_ref[:, 0:ql]
        qn_ref[...] = (q * _rms_scale(q) * qg_ref[...]).astype(BF16)
        kv = z_ref[:, ql:ql + kvl]
        kvn_ref[...] = (kv * _rms_scale(kv) * kvg_ref[...]).astype(BF16)
        kpe_ref[...] = _rope(z_ref[:, ql + kvl:ql + kvl + LANES], cos_ref[...], sin_ref[...]).astype(BF16)

    w = z_lat.shape[1]
    return pl.pallas_call(
        body, grid=(t // tr,),
        in_specs=[_rows(tr, w), _full((1, ql)), _full((1, kvl)), _rows(tr, LANES), _rows(tr, LANES)],
        out_specs=[_rows(tr, ql), _rows(tr, kvl), _rows(tr, LANES)],
        out_shape=[jax.ShapeDtypeStruct((t, ql), BF16), jax.ShapeDtypeStruct((t, kvl), BF16),
                   jax.ShapeDtypeStruct((t, LANES), BF16)],
        compiler_params=_params(("parallel",)), name="lat_fwd")(z_lat, qg, kvg, cos, sin)


def _q_rope(q_p, cos, sin, bwd, name):
    t, w = q_p.shape
    tr = _row_tile(t, w, 2)
    fn = _rope_bwd if bwd else _rope

    def body(q_ref, cos_ref, sin_ref, o_ref):
        c, s = cos_ref[...], sin_ref[...]
        for h in range(w // HEAD_PAD):
            o_ref[:, h * HEAD_PAD:h * HEAD_PAD + QK_NOPE] = q_ref[:, h * HEAD_PAD:h * HEAD_PAD + QK_NOPE].astype(BF16)
            lo = h * HEAD_PAD + QK_NOPE
            o_ref[:, lo:lo + LANES] = fn(q_ref[:, lo:lo + LANES].astype(F32), c, s).astype(BF16)

    return pl.pallas_call(
        body, grid=(t // tr,), in_specs=[_rows(tr, w), _rows(tr, LANES), _rows(tr, LANES)], out_specs=_rows(tr, w),
        out_shape=jax.ShapeDtypeStruct((t, w), BF16), compiler_params=_params(("parallel",)), name=name)(q_p, cos, sin)


def _tril_mask():
    r = lax.broadcasted_iota(jnp.int32, (CHUNK, CHUNK), 0)
    c = lax.broadcasted_iota(jnp.int32, (CHUNK, CHUNK), 1)
    return r >= c


def _sgu_fwd(z_uv, gs, ws, b_col):
    t = z_uv.shape[0]
    sw = z_uv.shape[1] // 2
    groups = sw // SGU_GROUP
    tr = _pick(t, 256, CHUNK)

    def body(u_ref, v_ref, gs_ref, ws_ref, b_ref, o_ref):
        v = _gelu(v_ref[...])
        vn = (v * _rms_scale(v) * gs_ref[...]).astype(BF16)
        tri = _tril_mask()
        for g in range(groups):
            wg = jnp.where(tri, ws_ref[g], 0.0).astype(BF16)
            cols = slice(g * SGU_GROUP, (g + 1) * SGU_GROUP)
            for c in range(tr // CHUNK):
                rows = slice(c * CHUNK, (c + 1) * CHUNK)
                mixed = jnp.dot(wg, vn[rows, cols], preferred_element_type=F32) + b_ref[g]
                o_ref[rows, cols] = (_gelu(u_ref[rows, cols]) * mixed).astype(BF16)

    return pl.pallas_call(
        body, grid=(t // tr,),
        in_specs=[_rows(tr, sw, 0), _rows(tr, sw, 1), _full((1, sw)), _full(ws.shape), _full(b_col.shape)],
        out_specs=_rows(tr, sw), out_shape=jax.ShapeDtypeStruct((t, sw), BF16),
        compiler_params=_params(("parallel",)), name="sgu_fwd")(z_uv, z_uv, gs, ws, b_col)


def _merge_fwd(y_attn, y_sgu, z_g, b_gate, comm=None):
    t, d = y_attn.shape
    tr = _row_tile(t, d, 5)

    def body(ya_ref, ys_ref, g0_ref, g1_ref, b0_ref, b1_ref, o_ref):
        g0 = _sigmoid(g0_ref[...] + b0_ref[...])
        g1 = _sigmoid(g1_ref[...] + b1_ref[...])
        o_ref[...] = (g0 * ya_ref[...] + g1 * ys_ref[...]).astype(BF16)

    bspec0 = pl.BlockSpec((1, d), lambda i: (0, 0))
    bspec1 = pl.BlockSpec((1, d), lambda i: (0, 1))
    outs, comm_outs = _call(
        body, grid=(t // tr,),
        in_specs=[_rows(tr, d), _rows(tr, d), _rows(tr, d, 0), _rows(tr, d, 1), bspec0, bspec1],
        out_specs=[_rows(tr, d)], out_shape=[jax.ShapeDtypeStruct((t, d), BF16)],
        sem=("parallel",), name="merge_fwd", args=(y_attn, y_sgu, z_g, z_g, b_gate, b_gate), comm=comm)
    return outs[0], comm_outs


def _swiglu_fwd(gate, up, comm=None):
    t, f = gate.shape
    tr = _row_tile(t, f, 3)

    def body(g_ref, u_ref, o_ref):
        g = g_ref[...]
        o_ref[...] = (g * _sigmoid(g) * u_ref[...]).astype(BF16)

    outs, comm_outs = _call(
        body, grid=(t // tr,), in_specs=[_rows(tr, f), _rows(tr, f)], out_specs=[_rows(tr, f)],
        out_shape=[jax.ShapeDtypeStruct((t, f), BF16)], sem=("parallel",), name="swiglu_fwd", args=(gate, up), comm=comm)
    return outs[0], comm_outs


def _loss_head(h2, g, target):
    t, d = h2.shape
    tr = _row_tile(t, d, 3)

    def body(h_ref, g_ref, t_ref, loss_ref, dh_ref, dhb_ref, dg_ref):
        @pl.when(pl.program_id(0) == 0)
        def _():
            loss_ref[...] = jnp.zeros_like(loss_ref)
            dg_ref[...] = jnp.zeros_like(dg_ref)

        h = h_ref[...]
        r = _rms_scale(h)
        hhat = h * r
        gv = g_ref[...]
        err = hhat * gv - t_ref[...]
        loss_ref[...] += jnp.full(loss_ref.shape, 0.5 * jnp.sum(jnp.mean(err * err, axis=-1)), F32)
        dx, dg_rows = _rms_bwd(hhat, r, gv, err * (1.0 / d))
        dh_ref[...] = dx
        dhb_ref[...] = dx.astype(BF16)
        dg_ref[...] += jnp.sum(dg_rows, axis=0, keepdims=True)

    return pl.pallas_call(
        body, grid=(t // tr,), in_specs=[_rows(tr, d), _full((1, d)), _rows(tr, d)],
        out_specs=[_full((1, LANES)), _rows(tr, d), _rows(tr, d), _full((1, d))],
        out_shape=[jax.ShapeDtypeStruct((1, LANES), F32), jax.ShapeDtypeStruct((t, d), F32),
                   jax.ShapeDtypeStruct((t, d), BF16), jax.ShapeDtypeStruct((1, d), F32)],
        compiler_params=_params(("arbitrary",)), name="loss_head")(h2, g, target)


def _norm_bwd(x, g, dy, resid, name, comm=None):
    t, d = x.shape
    tr = _row_tile(t, d, 5)

    def body(x_ref, g_ref, dy_ref, r_ref, dx_ref, dxb_ref, dg_ref):
        @pl.when(pl.program_id(0) == 0)
        def _():
            dg_ref[...] = jnp.zeros_like(dg_ref)

        xv = x_ref[...]
        r = _rms_scale(xv)
        dx, dg_rows = _rms_bwd(xv * r, r, g_ref[...], dy_ref[...])
        dx = r_ref[...] + dx
        dx_ref[...] = dx
        dxb_ref[...] = dx.astype(BF16)
        dg_ref[...] += jnp.sum(dg_rows, axis=0, keepdims=True)

    outs, comm_outs = _call(
        body, grid=(t // tr,), in_specs=[_rows(tr, d), _full((1, d)), _rows(tr, d), _rows(tr, d)],
        out_specs=[_rows(tr, d), _rows(tr, d), _full((1, d))],
        out_shape=[jax.ShapeDtypeStruct((t, d), F32), jax.ShapeDtypeStruct((t, d), BF16),
                   jax.ShapeDtypeStruct((1, d), F32)],
        sem=("arbitrary",), name=name, args=(x, g, dy, resid), comm=comm)
    return (outs[0], outs[1], outs[2]) if comm is None else (outs[0], outs[1], outs[2], comm_outs)


def _merge_bwd(dmerged, y_attn, y_sgu, z_g, b_gate):
    t, d = y_attn.shape
    tr = _row_tile(t, d, 7)

    def body(dm_ref, ya_ref, ys_ref, g0_ref, g1_ref, b0_ref, b1_ref, dya_ref, dys_ref, dz_ref, db_ref):
        @pl.when(pl.program_id(0) == 0)
        def _():
            db_ref[...] = jnp.zeros_like(db_ref)

        dm = dm_ref[...]
        g0 = _sigmoid(g0_ref[...] + b0_ref[...])
        g1 = _sigmoid(g1_ref[...] + b1_ref[...])
        dya_ref[...] = (dm * g0).astype(BF16)
        dys_ref[...] = (dm * g1).astype(BF16)
        dl0 = dm * ya_ref[...] * (g0 * (1.0 - g0))
        dl1 = dm * ys_ref[...] * (g1 * (1.0 - g1))
        dz_ref[:, 0:d] = dl0.astype(BF16)
        dz_ref[:, d:2 * d] = dl1.astype(BF16)
        db_ref[:, 0:d] += jnp.sum(dl0, axis=0, keepdims=True)
        db_ref[:, d:2 * d] += jnp.sum(dl1, axis=0, keepdims=True)

    bspec0 = pl.BlockSpec((1, d), lambda i: (0, 0))
    bspec1 = pl.BlockSpec((1, d), lambda i: (0, 1))
    return pl.pallas_call(
        body, grid=(t // tr,),
        in_specs=[_rows(tr, d), _rows(tr, d), _rows(tr, d), _rows(tr, d, 0), _rows(tr, d, 1), bspec0, bspec1],
        out_specs=[_rows(tr, d), _rows(tr, d), _rows(tr, 2 * d), _full((1, 2 * d))],
        out_shape=[jax.ShapeDtypeStruct((t, d), BF16), jax.ShapeDtypeStruct((t, d), BF16),
                   jax.ShapeDtypeStruct((t, 2 * d), BF16), jax.ShapeDtypeStruct((1, 2 * d), F32)],
        compiler_params=_params(("arbitrary",)), name="merge_bwd")(dmerged, y_attn, y_sgu, z_g, z_g, b_gate, b_gate)


def _sgu_bwd(z_uv, ds_out, gs, ws, b_col):
    t = z_uv.shape[0]
    sw = z_uv.shape[1] // 2
    groups = sw // SGU_GROUP
    tr = _pick(t, 256, CHUNK)

    def body(u_ref, v_ref, d_ref, gs_ref, ws_ref, b_ref, dz_ref, dws_ref, db_ref, dgs_ref, dvn_ref):
        @pl.when(pl.program_id(0) == 0)
        def _():
            dws_ref[...] = jnp.zeros_like(dws_ref)
            db_ref[...] = jnp.zeros_like(db_ref)
            dgs_ref[...] = jnp.zeros_like(dgs_ref)

        v, dgelu_v = _gelu_and_grad(v_ref[...])
        r = _rms_scale(v)
        vhat = v * r
        gsv = gs_ref[...]
        vn = (vhat * gsv).astype(BF16)
        tri = _tril_mask()
        for g in range(groups):
            wg = jnp.where(tri, ws_ref[g], 0.0).astype(BF16)
            cols = slice(g * SGU_GROUP, (g + 1) * SGU_GROUP)
            for c in range(tr // CHUNK):
                rows = slice(c * CHUNK, (c + 1) * CHUNK)
                vn_cg = vn[rows, cols]
                mixed = jnp.dot(wg, vn_cg, preferred_element_type=F32) + b_ref[g]
                u, dgelu_u = _gelu_and_grad(u_ref[rows, cols])
                dso = d_ref[rows, cols]
                dz_ref[rows, cols] = (dso * mixed * dgelu_u).astype(BF16)
                dmixed = dso * u
                db_ref[g] += jnp.sum(dmixed, axis=1, keepdims=True)
                dmixed_b = dmixed.astype(BF16)
                dws_ref[g] += jnp.where(
                    tri, lax.dot_general(dmixed_b, vn_cg, (((1,), (1,)), ((), ())), preferred_element_type=F32), 0.0)
                dvn_ref[rows, cols] = lax.dot_general(wg, dmixed_b, (((0,), (0,)), ((), ())), preferred_element_type=F32)
        dvn = dvn_ref[...]
        dv, dgs_rows = _rms_bwd(vhat, r, gsv, dvn)
        dz_ref[:, sw:2 * sw] = (dv * dgelu_v).astype(BF16)
        dgs_ref[...] += jnp.sum(dgs_rows, axis=0, keepdims=True)

    return pl.pallas_call(
        body, grid=(t // tr,),
        in_specs=[_rows(tr, sw, 0), _rows(tr, sw, 1), _rows(tr, sw), _full((1, sw)), _full(ws.shape), _full(b_col.shape)],
        out_specs=[_rows(tr, 2 * sw), _full(ws.shape), _full(b_col.shape), _full((1, sw))],
        out_shape=[jax.ShapeDtypeStruct((t, 2 * sw), BF16), jax.ShapeDtypeStruct(ws.shape, F32),
                   jax.ShapeDtypeStruct(b_col.shape, F32), jax.ShapeDtypeStruct((1, sw), F32)],
        scratch_shapes=[pltpu.VMEM((tr, sw), F32)],
        compiler_params=_params(("arbitrary",)), name="sgu_bwd")(z_uv, z_uv, ds_out, gs, ws, b_col)


def _lat_bwd(z_lat, qg, kvg, dqn, dkvn, dkpe_heads, cos, sin, ql, kvl):
    t, w = z_lat.shape
    heads = dkpe_heads.shape[0]
    tr = _row_tile(t, w + heads * LANES, 3)

    def body(z_ref, qg_ref, kvg_ref, dq_ref, dkv_ref, dk_ref, cos_ref, sin_ref, dz_ref, dqg_ref, dkvg_ref):
        @pl.when(pl.program_id(0) == 0)
        def _():
            dqg_ref[...] = jnp.zeros_like(dqg_ref)
            dkvg_ref[...] = jnp.zeros_like(dkvg_ref)

        q = z_ref[:, 0:ql]
        r = _rms_scale(q)
        dx, dg_rows = _rms_bwd(q * r, r, qg_ref[...], dq_ref[...])
        dz_ref[:, 0:ql] = dx.astype(BF16)
        dqg_ref[...] += jnp.sum(dg_rows, axis=0, keepdims=True)
        kv = z_ref[:, ql:ql + kvl]
        r = _rms_scale(kv)
        dx, dg_rows = _rms_bwd(kv * r, r, kvg_ref[...], dkv_ref[...])
        dz_ref[:, ql:ql + kvl] = dx.astype(BF16)
        dkvg_ref[...] += jnp.sum(dg_rows, axis=0, keepdims=True)
        dk = dk_ref[0]
        for h in range(1, heads):
            dk = dk + dk_ref[h]
        dz_ref[:, ql + kvl:ql + kvl + LANES] = _rope_bwd(dk, cos_ref[...], sin_ref[...]).astype(BF16)

    return pl.pallas_call(
        body, grid=(t // tr,),
        in_specs=[_rows(tr, w), _full((1, ql)), _full((1, kvl)), _rows(tr, ql), _rows(tr, kvl),
                  pl.BlockSpec((heads, tr, LANES), lambda i: (0, i, 0)), _rows(tr, LANES), _rows(tr, LANES)],
        out_specs=[_rows(tr, w), _full((1, ql)), _full((1, kvl))],
        out_shape=[jax.ShapeDtypeStruct((t, w), BF16), jax.ShapeDtypeStruct((1, ql), F32),
                   jax.ShapeDtypeStruct((1, kvl), F32)],
        compiler_params=_params(("arbitrary",)), name="lat_bwd")(z_lat, qg, kvg, dqn, dkvn, dkpe_heads, cos, sin)


_NT = (((1,), (1,)), ((), ()))


def _attn_scale():
    return (QK_NOPE + QK_ROPE) ** -0.5


def _heads_per_step(heads, wanted):
    return wanted if heads % wanted == 0 else 1


def _attn_fwd(q_c, kv, kpe, comm=None):
    t = q_c.shape[0]
    heads = q_c.shape[1] // HEAD_PAD
    tq = _pick(t, ATTN_TILE)
    nq = t // tq
    scale = _attn_scale()
    to_log2 = scale * math.log2(math.e)
    tn_dims = (((0,), (0,)), ((), ()))

    hps = _heads_per_step(heads, HEADS_PER_STEP[0])

    def body(q_ref, kv_ref, kpe_ref, o_ref, ob_ref, lse_ref, m_sc, l_sc, acc_sc):
        qi, ki = pl.program_id(1), pl.program_id(2)

        @pl.when(ki == 0)
        def _():
            m_sc[...] = jnp.full_like(m_sc, NEG_BIG)
            l_sc[...] = jnp.zeros_like(l_sc)
            acc_sc[...] = jnp.zeros_like(acc_sc)

        def step(diagonal):
            for u in range(hps):
                lo = u * HEAD_PAD
                kc = jnp.concatenate([kv_ref[:, lo:lo + QK_NOPE], kpe_ref[...]], axis=1)
                st = lax.dot_general(kc, q_ref[:, lo:lo + HEAD_PAD], _NT, preferred_element_type=F32)
                if diagonal:
                    krow = lax.broadcasted_iota(jnp.int32, st.shape, 0)
                    qcol = lax.broadcasted_iota(jnp.int32, st.shape, 1)
                    st = jnp.where(qcol >= krow, st, NEG_BIG)
                m_prev = m_sc[u]
                m_new = jnp.maximum(m_prev, jnp.max(st, axis=0, keepdims=True))
                alpha = jnp.exp2((m_prev - m_new) * to_log2)
                pt = jnp.exp2((st - m_new) * to_log2)
                l_sc[u] = alpha * l_sc[u] + jnp.sum(pt, axis=0, keepdims=True)
                acc_sc[u] = alpha * acc_sc[u] + lax.dot_general(
                    kv_ref[:, lo + QK_NOPE:lo + HEAD_PAD], pt.astype(BF16), tn_dims, preferred_element_type=F32)
                m_sc[u] = m_new

        @pl.when(ki < qi)
        def _():
            step(False)

        @pl.when(ki == qi)
        def _():
            step(True)
            for u in range(hps):
                o = (acc_sc[u] / l_sc[u]).T
                o_ref[:, u * V_HEAD:(u + 1) * V_HEAD] = o
                ob_ref[:, u * V_HEAD:(u + 1) * V_HEAD] = o.astype(BF16)
                lse_ref[u] = m_sc[u] * scale + jnp.log(l_sc[u])

    omap = lambda g, qi, ki: (qi, g)
    outs, comm_outs = _call(
        body, grid=(heads // hps, nq, nq),
        in_specs=[pl.BlockSpec((tq, hps * HEAD_PAD), omap),
                  pl.BlockSpec((tq, hps * HEAD_PAD), lambda g, qi, ki: (jnp.minimum(ki, qi), g)),
                  pl.BlockSpec((tq, LANES), lambda g, qi, ki: (jnp.minimum(ki, qi), 0))],
        out_specs=[pl.BlockSpec((tq, hps * V_HEAD), omap), pl.BlockSpec((tq, hps * V_HEAD), omap),
                   pl.BlockSpec((hps, 1, tq), lambda g, qi, ki: (g, 0, qi))],
        out_shape=[jax.ShapeDtypeStruct((t, heads * V_HEAD), F32), jax.ShapeDtypeStruct((t, heads * V_HEAD), BF16),
                   jax.ShapeDtypeStruct((heads, 1, t), F32)],
        scratch_shapes=[pltpu.VMEM((hps, 1, tq), F32), pltpu.VMEM((hps, 1, tq), F32),
                        pltpu.VMEM((hps, V_HEAD, tq), F32)],
        sem=("parallel", "parallel", "arbitrary"), name="attn_fwd", args=(q_c, kv, kpe), comm=comm)
    return outs[0], outs[1], outs[2], comm_outs


def _attn_bwd(q_c, kv, kpe, o, do, lse_row, comm=None):
    t = q_c.shape[0]
    heads = q_c.shape[1] // HEAD_PAD
    tk = _pick(t, ATTN_TILE)
    nk = t // tk
    scale = _attn_scale()
    tn_dims = (((0,), (0,)), ((), ()))

    hps = _heads_per_step(heads, HEADS_PER_STEP[1])

    def body(q_ref, kv_ref, kpe_ref, do_ref, lse_ref, o_ref, dq_ref, dkv_ref, dkpe_ref, dk_sc, dv_sc, delta_sc):
        ki, qi = pl.program_id(1), pl.program_id(2)

        @pl.when(jnp.logical_and(ki == 0, qi == 0))
        def _():
            dq_ref[...] = jnp.zeros_like(dq_ref)

        @pl.when(qi == 0)
        def _():
            dk_sc[...] = jnp.zeros_like(dk_sc)
            dv_sc[...] = jnp.zeros_like(dv_sc)

        @pl.when(ki == 0)
        def _():
            for u in range(hps):
                cols = slice(u * V_HEAD, (u + 1) * V_HEAD)
                delta_sc[qi * hps + u] = jnp.sum((do_ref[:, cols] * o_ref[:, cols]).T, axis=0, keepdims=True)

        def step(diagonal):
            for u in range(hps):
                lo = u * HEAD_PAD
                kc = jnp.concatenate([kv_ref[:, lo:lo + QK_NOPE], kpe_ref[...]], axis=1)
                q = q_ref[:, lo:lo + HEAD_PAD]
                st = lax.dot_general(kc, q, _NT, preferred_element_type=F32) * scale
                pt = jnp.exp(st - lse_ref[u])
                if diagonal:
                    krow = lax.broadcasted_iota(jnp.int32, st.shape, 0)
                    qcol = lax.broadcasted_iota(jnp.int32, st.shape, 1)
                    pt = jnp.where(qcol >= krow, pt, 0.0)
                do_b = do_ref[:, u * V_HEAD:(u + 1) * V_HEAD].astype(BF16)
                dv_sc[u] += jnp.dot(pt.astype(BF16), do_b, preferred_element_type=F32)
                dpt = lax.dot_general(kv_ref[:, lo + QK_NOPE:lo + HEAD_PAD], do_b, _NT, preferred_element_type=F32)
                dst = (pt * (dpt - delta_sc[qi * hps + u]) * scale).astype(BF16)
                dk_sc[u] += jnp.dot(dst, q, preferred_element_type=F32)
                rows = pl.ds(pl.multiple_of(qi * tk, tk), tk)
                dq_ref[rows, lo:lo + HEAD_PAD] += lax.dot_general(dst, kc, tn_dims, preferred_element_type=F32)

        @pl.when(qi > ki)
        def _():
            step(False)

        @pl.when(qi == ki)
        def _():
            step(True)

        @pl.when(qi == nk - 1)
        def _():
            for u in range(hps):
                lo = u * HEAD_PAD
                dkv_ref[:, lo:lo + QK_NOPE] = dk_sc[u, :, 0:QK_NOPE].astype(BF16)
                dkv_ref[:, lo + QK_NOPE:lo + HEAD_PAD] = dv_sc[u].astype(BF16)
                dkpe_ref[u] = dk_sc[u, :, QK_NOPE:QK_NOPE + LANES]

    qclamp = lambda g, ki, qi: (jnp.maximum(qi, ki), g)
    outs, comm_outs = _call(
        body, grid=(heads // hps, nk, nk),
        in_specs=[pl.BlockSpec((tk, hps * HEAD_PAD), qclamp),
                  pl.BlockSpec((tk, hps * HEAD_PAD), lambda g, ki, qi: (ki, g)),
                  pl.BlockSpec((tk, LANES), lambda g, ki, qi: (ki, 0)),
                  pl.BlockSpec((tk, hps * V_HEAD), qclamp),
                  pl.BlockSpec((hps, 1, tk), lambda g, ki, qi: (g, 0, jnp.maximum(qi, ki))),
                  pl.BlockSpec((tk, hps * V_HEAD), lambda g, ki, qi: (jnp.where(ki == 0, qi, 0), g))],
        out_specs=[pl.BlockSpec((t, hps * HEAD_PAD), lambda g, ki, qi: (0, g)),
                   pl.BlockSpec((tk, hps * HEAD_PAD), lambda g, ki, qi: (ki, g)),
                   pl.BlockSpec((hps, tk, LANES), lambda g, ki, qi: (g, ki, 0))],
        out_shape=[jax.ShapeDtypeStruct((t, heads * HEAD_PAD), F32),
                   jax.ShapeDtypeStruct((t, heads * HEAD_PAD), BF16), jax.ShapeDtypeStruct((heads, t, LANES), F32)],
        scratch_shapes=[pltpu.VMEM((hps, tk, HEAD_PAD), F32), pltpu.VMEM((hps, tk, V_HEAD), F32),
                        pltpu.VMEM((nk * hps, 1, tk), F32)],
        sem=("parallel", "arbitrary", "arbitrary"), name="attn_bwd",
        args=(q_c, kv, kpe, do, lse_row, o), comm=comm)
    return outs[0], outs[1], outs[2], comm_outs


def _local_step(x, pos_col, target, small, shards, opt):
    t = x.shape[0]
    ql, kvl = small["q_norm_g"].shape[1], small["kv_norm_g"].shape[1]
    sw = small["sgu_norm_g"].shape[1]
    heads = (shards["w_uq"].shape[1] * N_DEV) // (QK_NOPE + QK_ROPE)
    big = {}
    early = ["w_in", "w_uq", "w_ukv"]
    big.update(_compute_layout(dict(zip(early, _all_gather([shards[k] for k in early]))), ql, kvl, heads, sw))
    half = QK_ROPE // 2
    lane = jnp.arange(LANES)
    inv_freq = ROPE_THETA ** (-jnp.arange(0, QK_ROPE, 2, dtype=F32) / QK_ROPE)
    inv_row = inv_freq[lane % half][None, :]
    sign_row = jnp.where((lane % QK_ROPE) < half, -1.0, 1.0).astype(F32)[None, :]
    cos, sin = _rope_tables(pos_col, inv_row, sign_row)
    ws = small["w_sgu"]
    b_col = small["b_sgu_col"]

    def arrived(names, bufs):
        big.update(_compute_layout(dict(zip(names, bufs)), ql, kvl, heads, sw))

    a = _norm_fwd(x, small["norm_mix_g"], "norm_mix_fwd")
    z_lat = _mm(a, big["w_lat_t"], tb=True, name="z_lat")
    z_uv, g_sgu = _mm(a, big["w_uv_t"], tb=True, name="z_uv", comm=_gather_stage(1, [shards["w_o_sgu"]]))
    z_g, (g_attn, g_sgu) = _mm(a, big["w_g_t"], tb=True, name="z_g",
                               comm=_join(_gather_stage(1, [shards["w_o_attn"]]), _gather_stage(2, g_sgu)))
    qn, kvn, kpe = _lat_fwd(z_lat, small["q_norm_g"], small["kv_norm_g"], cos, sin, ql, kvl)
    q_p, (g_attn, g_sgu) = _mm(qn, big["w_uq"], name="q_up",
                               comm=_join(_gather_stage(2, [g_attn]), _gather_stage(3, [g_sgu])))
    kv, (g_attn, g_out) = _mm(kvn, big["w_ukv"], out_dtype=BF16, name="kv_up",
                              comm=_join(_gather_stage(3, [g_attn]), _gather_stage(1, [shards["w_out"]])))
    arrived(["w_o_sgu", "w_o_attn"], [g_sgu, g_attn])
    q_c = _q_rope(q_p, cos, sin, False, "q_rope")
    attn, attn_b, lse, (w_gate, w_up) = _attn_fwd(
        q_c, kv, kpe, comm=_gather_stage(1, [shards["w_gate_ffn"], shards["w_up_ffn"]]))
    s_out = _sgu_fwd(z_uv, small["sgu_norm_g"], ws, b_col)
    y_sgu, (g_out,) = _mm(s_out, big["w_o_sgu"], name="y_sgu", comm=_gather_stage(2, [g_out]))
    y_attn, (w_gate, g_out) = _mm(attn_b, big["w_o_attn"], name="y_attn",
                                  comm=_join(_gather_stage(2, [w_gate]), _gather_stage(3, [g_out])))
    arrived(["w_out"], [g_out])
    merged, (w_up, w_gate) = _merge_fwd(y_attn, y_sgu, z_g, small["b_gate"],
                                        comm=_join(_gather_stage(2, [w_up]), _gather_stage(3, [w_gate])))
    h1, (w_up,) = _mm(merged, big["w_out"], add=x, name="h1", comm=_gather_stage(3, [w_up]))
    f = _norm_fwd(h1, small["norm_ffn_g"], "norm_ffn_fwd")
    gate, w_down = _mm(f, w_gate, tb=True, slab="n", name="ffn_gate", comm=_gather_stage(1, [shards["w_down_ffn"]]))
    up, w_down = _mm(f, w_up, tb=True, slab="n", name="ffn_up", comm=_gather_stage(2, w_down))
    ffn = gate.shape[2]
    gate, up = gate.reshape(N_DEV * t, ffn), up.reshape(N_DEV * t, ffn)
    act, (w_down,) = _swiglu_fwd(gate, up, comm=_gather_stage(3, w_down))
    act = act.reshape(N_DEV, t, ffn)
    h2 = _mm(act, w_down, slab="k", add=h1, name="h2")
    loss_row, dh2, dh2_b, d_norm_final = _loss_head(h2, small["norm_final_g"], target)

    def pair_sums(names, slabs, bufs):
        return [_pair_sum(g, b, "pair_sum_" + k) for k, g, b in zip(names, slabs, bufs)]

    parts, updates = {}, {}

    def update(k, comm=None):
        w, m, v = opt[k]
        updates[k], got = _adamw_shard(parts[k], w, m, v, "adamw_" + k, comm=comm)
        return got

    down_slabs = [_mm(act, dh2_b, ta=True, slab="m", out_dtype=BF16, name="dw_down")]
    dgu, bufs = _mm(dh2_b, w_down, tb=True, slab="n", tm=MM_TILE[0] // 2, name="dact_swiglu_bwd",
                    comm=_to_sibling(down_slabs), swiglu=(gate.reshape(N_DEV, t, ffn), up.reshape(N_DEV, t, ffn)))
    dgu = dgu.reshape(2 * N_DEV, t, ffn)
    down_pair = pair_sums(["w_down_ffn"], down_slabs, bufs)
    dw_gu, got = _mm(dgu, f, ta=True, slab="m", out_dtype=BF16, name="dw_gate_up", comm=_to_chips(down_pair))
    parts["w_down_ffn"] = got[0]
    gu_names = ["w_gate_ffn", "w_up_ffn"]
    df, bufs = _mm(dgu, w_gate, slab="k", name="df_gate", comm=_to_sibling([dw_gu, dw_gu], first=[0, N_DEV]))
    gu_pairs = [_pair_sum(dw_gu, b, "pair_sum_" + k, first=s0) for k, b, s0 in zip(gu_names, bufs, [0, N_DEV])]
    half = _pick(gu_pairs[1].shape[1], gu_pairs[1].shape[1] // 2, 2 * SUBLANES)
    df, up_parts = _mm(dgu, w_up, slab="k", a_slab0=N_DEV, add=df, name="df_up",
                       comm=_to_chips(gu_pairs[1:], rows=[("r", 0, half)]))
    dh1, dh1_b, d_norm_ffn = _norm_bwd(h1, small["norm_ffn_g"], df, dh2, "norm_ffn_bwd")
    dw_out = _mm(merged, dh1_b, ta=True, out_dtype=BF16, name="dw_out")
    out_slabs = [_slabs_from_rows(dw_out)]
    dmerged, bufs = _mm(dh1_b, big["w_out"], tb=True, name="dmerged", comm=_to_sibling(out_slabs))
    out_pair = pair_sums(["w_out"], out_slabs, bufs)
    dy_attn, dy_sgu, dz_g, d_b_gate = _merge_bwd(dmerged, y_attn, y_sgu, z_g, small["b_gate"])
    dw_o_sgu = _mm(s_out, dy_sgu, ta=True, out_dtype=BF16, name="dw_o_sgu")
    ds_out = _mm(dy_sgu, big["w_o_sgu"], tb=True, name="ds_out")
    dz_uv, d_ws, d_b_col, d_sgu_norm = _sgu_bwd(z_uv, ds_out, small["sgu_norm_g"], ws, b_col)
    dw_o_attn = _mm(attn_b, dy_attn, ta=True, out_dtype=BF16, name="dw_o_attn")
    mix_names = ["w_o_sgu", "w_o_attn"]
    mix_slabs = [_slabs_from_cols(dw_o_sgu), _slabs_from_rows(dw_o_attn)]
    dattn, bufs = _mm(dy_attn, big["w_o_attn"], tb=True, name="dattn", comm=_to_sibling(mix_slabs))
    mix_pairs = pair_sums(mix_names, mix_slabs, bufs)
    rest = ("r", half, gu_pairs[1].shape[1] - half)
    dq_c, dkv, dkpe_heads, got = _attn_bwd(
        q_c, kv, kpe, attn, dattn, lse,
        comm=_join(_to_chips(gu_pairs[:1]), _to_chips(gu_pairs[1:], rows=[rest], into=up_parts)))
    parts.update(zip(gu_names, got))
    dq_p = _q_rope(dq_c, cos, sin, True, "q_rope_bwd")
    dw_uq = _mm(qn, dq_p, ta=True, out_dtype=BF16, name="dw_uq")
    dw_ukv = _mm(kvn, dkv, ta=True, out_dtype=BF16, name="dw_ukv")
    dqn = _mm(dq_p, big["w_uq"], tb=True, name="dqn")
    dkvn = _mm(dkv, big["w_ukv"], tb=True, name="dkvn")
    dz_lat, d_q_norm, d_kv_norm = _lat_bwd(z_lat, small["q_norm_g"], small["kv_norm_g"], dqn, dkvn, dkpe_heads,
                                           cos, sin, ql, kvl)
    dw_g, got = _mm(dz_g, a, ta=True, out_dtype=BF16, name="dw_g", comm=_to_chips(out_pair))
    parts["w_out"] = got[0]
    dw_uv, got = _mm(dz_uv, a, ta=True, out_dtype=BF16, name="dw_uv", comm=_to_chips(mix_pairs[1:]))
    parts["w_o_attn"] = got[0]
    dw_lat, got = _mm(dz_lat, a, ta=True, out_dtype=BF16, name="dw_lat", comm=_to_chips(mix_pairs[:1]))
    parts["w_o_sgu"] = got[0]
    lat = ql + kvl + QK_ROPE
    dw_uq_cols = dw_uq.reshape(ql, heads, HEAD_PAD)[:, :, :QK_NOPE + QK_ROPE].reshape(ql, heads * (QK_NOPE + QK_ROPE))
    in_names = ["w_uq", "w_ukv", "w_in"]
    in_slabs = [_slabs_from_cols(dw_uq_cols), _slabs_from_cols(dw_ukv),
                _slabs_from_rows(jnp.concatenate([dw_lat[:lat], dw_uv, dw_g], axis=0))]
    da = _mm(dz_lat, big["w_lat_t"], name="da_lat")
    da, bufs = _mm(dz_uv, big["w_uv_t"], add=da, name="da_uv", comm=_to_sibling(in_slabs))
    uq_pair, ukv_pair, in_pair = pair_sums(in_names, in_slabs, bufs)
    cols = in_pair.shape[2]
    chunk = _pick(cols, cols // TAIL_CHUNKS)
    chunks = [("c", c0, chunk) for c0 in range(0, cols, chunk)]
    da, got = _mm(dz_g, big["w_g_t"], add=da, name="da_g",
                  comm=_to_chips([uq_pair, in_pair], rows=[None, chunks[0]]))
    parts["w_uq"], in_parts = got
    grad_x, _, d_norm_mix, got = _norm_bwd(x, small["norm_mix_g"], da, dh1, "norm_mix_bwd", comm=_to_chips([ukv_pair]))
    parts["w_ukv"] = got[0]
    hosts = ["w_gate_ffn", "w_up_ffn", "w_down_ffn", "w_out", "w_o_attn", "w_o_sgu", "w_uq", "w_ukv"]
    assert len(chunks) <= 1 + len(hosts)
    for i, k in enumerate(hosts):
        if 1 + i < len(chunks):
            in_parts = update(k, comm=_to_chips([in_pair], rows=[chunks[1 + i]], into=[in_parts]))[0]
        else:
            update(k)
    parts["w_in"] = in_parts
    update("w_in")

    gs = {"norm_mix_g": d_norm_mix, "b_gate": d_b_gate, "q_norm_g": d_q_norm, "kv_norm_g": d_kv_norm,
          "sgu_norm_g": d_sgu_norm, "w_sgu": d_ws, "b_sgu_col": d_b_col, "norm_ffn_g": d_norm_ffn,
          "norm_final_g": d_norm_final}
    return loss_row, grad_x, gs, updates


def _my_place():
    return lax.axis_index("x"), lax.axis_index("y"), lax.axis_index("c")


N_CHIPS = N_DEV // 2

_GATHER_SEMS = [[(3,), (3,), ()], [(4,), (4,)], [(1,), (1,)]]


def _halves(shape):
    r, c = shape
    if (c // 2) % LANES == 0:
        return ("c", 0, c // 2), ("c", c // 2, c // 2)
    assert (r // 2) % (2 * SUBLANES) == 0, shape
    return ("r", 0, r // 2), ("r", r // 2, r // 2)


def _gather_copies(stage, ins, outs, sems):
    x, y, c = _my_place()
    me, x_nbr, y_nbr, diag = 4 * x + 2 * y + c, 4 * (1 - x) + 2 * y + c, 4 * x + 2 * (1 - y) + c, 4 * (1 - x) + 2 * (1 - y) + c
    sibling = (x, y, 1 - c)

    def remote(w, k, src, dst, to):
        return pltpu.make_async_remote_copy(src_ref=src, dst_ref=dst, send_sem=sems[0].at[w, k], recv_sem=sems[1].at[w, k],
                                            device_id=to, device_id_type=MESH)

    out = []
    for w in range(len(outs)):
        if stage == 1:
            dst = outs[w].at[me]
            out.append(pltpu.make_async_copy(ins[w], dst, sems[2].at[w]))
            out += [remote(w, k, ins[w], dst, to) for k, to in enumerate([sibling, (1 - x, y, c), (x, 1 - y, c)])]
        elif stage == 2:
            first, second = _halves(outs[w].shape[1:])
            out.append(remote(w, 0, _window(ins[w], x_nbr, first), _window(outs[w], x_nbr, first), (x, 1 - y, c)))
            out.append(remote(w, 1, _window(ins[w], y_nbr, second), _window(outs[w], y_nbr, second), (1 - x, y, c)))
            out.append(remote(w, 2, ins[w].at[x_nbr], outs[w].at[x_nbr], sibling))
            out.append(remote(w, 3, ins[w].at[y_nbr], outs[w].at[y_nbr], sibling))
        else:
            out.append(remote(w, 0, ins[w].at[diag], outs[w].at[diag], sibling))
    return out


def _gather_stage(stage, arrays):
    n = len(arrays)

    def start(ins, outs, sems):
        for cp in _gather_copies(stage, ins, outs, sems):
            cp.start()

    def finish(ins, outs, sems):
        for cp in _gather_copies(stage, ins, outs, sems):
            cp.wait()

    shapes = [jax.ShapeDtypeStruct(((N_DEV,) + a.shape) if stage == 1 else a.shape, a.dtype) for a in arrays]
    return _Comm(arrays, shapes, [pltpu.SemaphoreType.DMA((n,) + s) for s in _GATHER_SEMS[stage - 1]], start, finish,
                 aliases=None if stage == 1 else {w: w for w in range(n)})


def _join(*comms):
    ins, shapes, sems, aliases, spans = [], [], [], {}, []
    for cm in comms:
        spans.append((len(ins), len(ins) + len(cm.ins), len(shapes), len(shapes) + len(cm.out_shapes),
                      len(sems), len(sems) + len(cm.sems)))
        aliases.update({len(ins) + i: len(shapes) + o for i, o in cm.aliases.items()})
        ins, shapes, sems = ins + cm.ins, shapes + cm.out_shapes, sems + cm.sems

    def each(half):
        def run(i_refs, o_refs, s_refs):
            for cm, (i0, i1, o0, o1, s0, s1) in zip(comms, spans):
                getattr(cm, half)(i_refs[i0:i1], o_refs[o0:o1], s_refs[s0:s1])
        return run

    return _Comm(ins, shapes, sems, each("start"), each("finish"), aliases)


def _all_gather(shards):
    n = len(shards)
    n_sems = [len(s) for s in _GATHER_SEMS]

    def body(*refs):
        ins, outs, sems = refs[:n], refs[n:2 * n], refs[2 * n:]
        s0 = 0
        for stage in (1, 2, 3):
            mine = sems[s0:s0 + n_sems[stage - 1]]
            s0 += n_sems[stage - 1]
            copies = _gather_copies(stage, ins if stage == 1 else outs, outs, mine)
            for cp in copies:
                cp.start()
            for cp in copies:
                cp.wait()

    any_spec = pl.BlockSpec(memory_space=pl.ANY)
    return pl.pallas_call(
        body, in_specs=[any_spec] * n, out_specs=[any_spec] * n,
        out_shape=[jax.ShapeDtypeStruct((N_DEV,) + s.shape, s.dtype) for s in shards],
        scratch_shapes=[pltpu.SemaphoreType.DMA((n,) + s) for stage in _GATHER_SEMS for s in stage],
        compiler_params=pltpu.CompilerParams(has_side_effects=True), name="all_gather_weights")(*shards)


def _to_sibling(grads, first=None):
    n = len(grads)
    first = first or [0] * n

    def copies(ins, outs, sems):
        x, y, c = _my_place()
        send_sems, recv_sems = sems
        return [pltpu.make_async_remote_copy(
            src_ref=ins[w].at[first[w] + 2 * i + (1 - c)], dst_ref=outs[w].at[i], send_sem=send_sems.at[w, i],
            recv_sem=recv_sems.at[w, i], device_id=(x, y, 1 - c), device_id_type=MESH)
            for w in range(n) for i in range(N_CHIPS)]

    def start(ins, outs, sems):
        for cp in copies(ins, outs, sems):
            cp.start()

    def finish(ins, outs, sems):
        for cp in copies(ins, outs, sems):
            cp.wait()

    return _Comm(grads, [jax.ShapeDtypeStruct((N_CHIPS,) + g.shape[1:], g.dtype) for g in grads],
                 [pltpu.SemaphoreType.DMA((n, N_CHIPS)), pltpu.SemaphoreType.DMA((n, N_CHIPS))], start, finish)


def _window(ref, slab, win):
    if win is None:
        return ref.at[slab]
    if win[0] == "r":
        return ref.at[slab, pl.ds(win[1], win[2])]
    return ref.at[slab, slice(None), pl.ds(win[1], win[2])]


def _to_chips(parts, rows=None, into=None):
    n = len(parts)
    rows = rows or [None] * n

    def copies(ins, outs, sems):
        x, y, c = _my_place()
        send_sems, recv_sems, local_sems = sems
        mine = 2 * x + y
        chips = [(1 - x, y), (x, 1 - y), (1 - x, 1 - y)]
        remote = [pltpu.make_async_remote_copy(
            src_ref=_window(ins[w], 2 * cx + cy, rows[w]), dst_ref=_window(outs[w], mine, rows[w]),
            send_sem=send_sems.at[w, j], recv_sem=recv_sems.at[w, j], device_id=(cx, cy, c), device_id_type=MESH)
            for w in range(n) for j, (cx, cy) in enumerate(chips)]
        local = [pltpu.make_async_copy(_window(ins[w], mine, rows[w]), _window(outs[w], mine, rows[w]),
                                       local_sems.at[w]) for w in range(n)]
        return remote + local

    def start(ins, outs, sems):
        for cp in copies(ins, outs, sems):
            cp.start()

    def finish(ins, outs, sems):
        for cp in copies(ins, outs, sems):
            cp.wait()

    return _Comm(list(parts) + list(into or []), [jax.ShapeDtypeStruct(p.shape, p.dtype) for p in parts],
                 [pltpu.SemaphoreType.DMA((n, N_CHIPS - 1)), pltpu.SemaphoreType.DMA((n, N_CHIPS - 1)),
                  pltpu.SemaphoreType.DMA((n,))], start, finish,
                 aliases={n + w: w for w in range(n)} if into else None)


def _pair_sum(g, buf, name, first=0):
    _, r, c = g.shape
    tr, tc = _shard_tile(r, c, 4 * SHARD_TILE_ELEMS, 1024)
    core = (lax.axis_index("c") + first).astype(jnp.int32).reshape(1)

    def body(core_ref, g_ref, b_ref, o_ref):
        o_ref[...] = (g_ref[...].astype(F32) + b_ref[...].astype(F32)).astype(o_ref.dtype)

    blk = (1, tr, tc)
    return pl.pallas_call(
        body, grid_spec=pltpu.PrefetchScalarGridSpec(
            num_scalar_prefetch=1, grid=(N_CHIPS, r // tr, c // tc),
            in_specs=[pl.BlockSpec(blk, lambda i, j, l, core_ref: (2 * i + core_ref[0], j, l)),
                      pl.BlockSpec(blk, lambda i, j, l, core_ref: (i, j, l))],
            out_specs=pl.BlockSpec(blk, lambda i, j, l, core_ref: (i, j, l))),
        out_shape=jax.ShapeDtypeStruct(buf.shape, buf.dtype),
        compiler_params=_params(("parallel", "parallel", "parallel")), name=name)(core, g, buf)


def _all_reduce_pack(pack):
    r = pack.shape[0]

    def body(x_ref, out_ref, gath_ref, send_sems, recv_sems, local_sem):
        x, y, c = _my_place()
        me, sibling = (x, y, c), (x, y, 1 - c)
        chips = [(1 - x, y), (x, 1 - y), (1 - x, 1 - y)]

        def slab(place):
            return gath_ref.at[4 * place[0] + 2 * place[1] + place[2]]

        def copy(k, place, to, src=None):
            return pltpu.make_async_remote_copy(
                src_ref=slab(place) if src is None else src, dst_ref=slab(place),
                send_sem=send_sems.at[k], recv_sem=recv_sems.at[k], device_id=to, device_id_type=MESH)

        mine = pltpu.make_async_copy(x_ref, slab(me), local_sem)
        mine.start()
        first = [copy(0, me, sibling, src=x_ref)]
        first += [copy(1 + j, me, (*chip, c), src=x_ref) for j, chip in enumerate(chips)]
        for cp in first:
            cp.start()
        passed = [copy(4 + j, (*chip, c), sibling) for j, chip in enumerate(chips)]
        for j, chip in enumerate(chips):
            copy(1 + j, (*chip, c), me).wait_recv()
            passed[j].start()
        copy(0, sibling, me).wait_recv()
        for j, chip in enumerate(chips):
            copy(4 + j, (*chip, 1 - c), me).wait_recv()
        for cp in first + passed:
            cp.wait_send()
        mine.wait()
        acc = gath_ref[0]
        for i in range(1, N_DEV):
            acc = acc + gath_ref[i]
        out_ref[...] = acc

    vmem = pl.BlockSpec(memory_space=pltpu.VMEM)
    return pl.pallas_call(
        body, in_specs=[vmem], out_specs=vmem, out_shape=jax.ShapeDtypeStruct(pack.shape, F32),
        scratch_shapes=[pltpu.VMEM((N_DEV, r, LANES), F32), pltpu.SemaphoreType.DMA((7,)),
                        pltpu.SemaphoreType.DMA((7,)), pltpu.SemaphoreType.DMA],
        compiler_params=pltpu.CompilerParams(vmem_limit_bytes=VMEM_LIMIT), name="all_reduce_small")(pack)


def _adamw_math(w, g, m, v):
    m = ADAM_B1 * m + (1.0 - ADAM_B1) * g
    v = ADAM_B2 * v + (1.0 - ADAM_B2) * (g * g)
    m_hat = m / (1.0 - ADAM_B1 ** ADAM_STEP)
    v_hat = v / (1.0 - ADAM_B2 ** ADAM_STEP)
    delta = -ADAM_LR * (m_hat / (jnp.sqrt(v_hat) + ADAM_EPS) + ADAM_WD * w)
    return delta, m, v


def _adamw_shard(parts, w, m, v, name, comm=None):
    r, c = w.shape
    n_parts = parts.shape[0]
    tr, tc = _shard_tile(r, c)

    def body(p_ref, w_ref, m_ref, v_ref, g_ref, d_ref, nm_ref, nv_ref):
        g = p_ref[0].astype(F32)
        for i in range(1, n_parts):
            g = g + p_ref[i].astype(F32)
        g_ref[...] = g
        d_ref[...], nm_ref[...], nv_ref[...] = _adamw_math(w_ref[...], g, m_ref[...], v_ref[...])

    spec = pl.BlockSpec((tr, tc), lambda i, j: (i, j))
    outs, comm_outs = _call(
        body, grid=(r // tr, c // tc),
        in_specs=[pl.BlockSpec((n_parts, tr, tc), lambda i, j: (0, i, j)), spec, spec, spec],
        out_specs=[spec] * 4, out_shape=[jax.ShapeDtypeStruct((r, c), F32)] * 4,
        sem=("parallel", "parallel"), name=name, args=(parts, w, m, v), comm=comm)
    return outs, comm_outs


def _adamw_pack(g, w, m, v):
    r, c = w.shape

    def body(g_ref, w_ref, m_ref, v_ref, d_ref, nm_ref, nv_ref):
        d_ref[...], nm_ref[...], nv_ref[...] = _adamw_math(w_ref[...], g_ref[...], m_ref[...], v_ref[...])

    return pl.pallas_call(
        body, in_specs=[_full((r, c))] * 4, out_specs=[_full((r, c))] * 3, grid=(1,),
        out_shape=[jax.ShapeDtypeStruct((r, c), F32)] * 3,
        compiler_params=_params(("arbitrary",)), name="adamw_small")(g, w, m, v)


def _cols_from_slabs(g):
    return jnp.transpose(g, (1, 0, 2)).reshape(g.shape[1], N_DEV * g.shape[2])


def _slabs_from_cols(w):
    r, c8 = w.shape
    return jnp.transpose(w.reshape(r, N_DEV, c8 // N_DEV), (1, 0, 2))


def _rows_from_slabs(g):
    return g.reshape(N_DEV * g.shape[1], g.shape[2])


def _slabs_from_rows(w):
    return w.reshape(N_DEV, w.shape[0] // N_DEV, w.shape[1])


def _compute_layout(gathered, ql, kvl, heads, sw):
    out = {}
    for k, g in gathered.items():
        if k == "w_in":
            lat = ql + kvl + QK_ROPE
            w_in_t = _rows_from_slabs(g)
            out["w_lat_t"] = jnp.pad(w_in_t[:lat], ((0, LANES - QK_ROPE), (0, 0)))
            out["w_uv_t"] = w_in_t[lat:lat + 2 * sw]
            out["w_g_t"] = w_in_t[lat + 2 * sw:]
        elif k == "w_uq":
            per_head = _cols_from_slabs(g).reshape(ql, heads, QK_NOPE + QK_ROPE)
            pad = HEAD_PAD - QK_NOPE - QK_ROPE
            out["w_uq"] = jnp.pad(per_head, ((0, 0), (0, 0), (0, pad))).reshape(ql, heads * HEAD_PAD)
        elif k in ("w_o_attn", "w_out", "w_down_ffn"):
            out[k.removesuffix("_ffn")] = _rows_from_slabs(g)
        else:
            out[k.removesuffix("_ffn")] = _cols_from_slabs(g)
    return out


_SMALL =["norm_mix_g", "b_gate", "q_norm_g", "kv_norm_g", "sgu_norm_g", "w_sgu", "b_sgu", "norm_ffn_g", "norm_final_g"]
_BIG = ["w_in", "w_uq", "w_ukv", "w_o_attn", "w_o_sgu", "w_out", "w_gate_ffn", "w_up_ffn", "w_down_ffn"]
_TRANSPOSED = ("w_in", "w_gate_ffn", "w_up_ffn")
_ORDER = ["norm_mix_g", "w_in", "b_gate", "q_norm_g", "w_uq", "kv_norm_g", "w_ukv", "w_o_attn", "sgu_norm_g", "w_sgu",
          "b_sgu", "w_o_sgu", "w_out", "norm_ffn_g", "w_gate_ffn", "w_up_ffn", "w_down_ffn", "norm_final_g"]


def _pack_rows(parts):
    rows, sizes = [], []
    for p in parts:
        flat = p.reshape(-1)
        n = flat.shape[0]
        padded = -(-n // (SUBLANES * LANES)) * (SUBLANES * LANES)
        rows.append(jnp.pad(flat, (0, padded - n)).reshape(padded // LANES, LANES))
        sizes.append((n, padded // LANES))
    return jnp.concatenate(rows, axis=0), sizes


def _unpack_rows(pack, sizes, shapes):
    out, r0 = [], 0
    for (n, nr), shp in zip(sizes, shapes):
        out.append(pack[r0:r0 + nr].reshape(-1)[:n].reshape(shp))
        r0 += nr
    return out


def kernel(x, positions, norm_mix_g, w_in, b_gate, q_norm_g, w_uq, kv_norm_g, w_ukv, w_o_attn, sgu_norm_g, w_sgu, b_sgu, w_o_sgu, w_out, norm_ffn_g, w_gate_ffn, w_up_ffn, w_down_ffn, norm_final_g, loss_target, m_norm_mix_g, m_w_in, m_b_gate, m_q_norm_g, m_w_uq, m_kv_norm_g, m_w_ukv, m_w_o_attn, m_sgu_norm_g, m_w_sgu, m_b_sgu, m_w_o_sgu, m_w_out, m_norm_ffn_g, m_w_gate_ffn, m_w_up_ffn, m_w_down_ffn, m_norm_final_g, v_norm_mix_g, v_w_in, v_b_gate, v_q_norm_g, v_w_uq, v_kv_norm_g, v_w_ukv, v_w_o_attn, v_sgu_norm_g, v_w_sgu, v_b_sgu, v_w_o_sgu, v_w_out, v_norm_ffn_g, v_w_gate_ffn, v_w_up_ffn, v_w_down_ffn, v_norm_final_g):
    wts = dict(norm_mix_g=norm_mix_g, w_in=w_in, b_gate=b_gate, q_norm_g=q_norm_g, w_uq=w_uq, kv_norm_g=kv_norm_g,
               w_ukv=w_ukv, w_o_attn=w_o_attn, sgu_norm_g=sgu_norm_g, w_sgu=w_sgu, b_sgu=b_sgu, w_o_sgu=w_o_sgu,
               w_out=w_out, norm_ffn_g=norm_ffn_g, w_gate_ffn=w_gate_ffn, w_up_ffn=w_up_ffn, w_down_ffn=w_down_ffn,
               norm_final_g=norm_final_g)
    mom = dict(norm_mix_g=m_norm_mix_g, w_in=m_w_in, b_gate=m_b_gate, q_norm_g=m_q_norm_g, w_uq=m_w_uq,
               kv_norm_g=m_kv_norm_g, w_ukv=m_w_ukv, w_o_attn=m_w_o_attn, sgu_norm_g=m_sgu_norm_g, w_sgu=m_w_sgu,
               b_sgu=m_b_sgu, w_o_sgu=m_w_o_sgu, w_out=m_w_out, norm_ffn_g=m_norm_ffn_g, w_gate_ffn=m_w_gate_ffn,
               w_up_ffn=m_w_up_ffn, w_down_ffn=m_w_down_ffn, norm_final_g=m_norm_final_g)
    var = dict(norm_mix_g=v_norm_mix_g, w_in=v_w_in, b_gate=v_b_gate, q_norm_g=v_q_norm_g, w_uq=v_w_uq,
               kv_norm_g=v_kv_norm_g, w_ukv=v_w_ukv, w_o_attn=v_w_o_attn, sgu_norm_g=v_sgu_norm_g, w_sgu=v_w_sgu,
               b_sgu=v_b_sgu, w_o_sgu=v_w_o_sgu, w_out=v_w_out, norm_ffn_g=v_norm_ffn_g, w_gate_ffn=v_w_gate_ffn,
               w_up_ffn=v_w_up_ffn, w_down_ffn=v_w_down_ffn, norm_final_g=v_norm_final_g)

    t, d = x.shape[1], x.shape[2]
    ql, kvl = q_norm_g.shape[1], kv_norm_g.shape[1]
    heads = (w_uq.shape[2] * N_DEV) // (QK_NOPE + QK_ROPE)
    sw = sgu_norm_g.shape[1]

    def shard(a, k):
        return a[0].T if k in _TRANSPOSED else a[0]

    def unshard(a, k):
        return (a.T if k in _TRANSPOSED else a).reshape(wts[k].shape)

    opt = {k: (shard(wts[k], k), shard(mom[k], k), shard(var[k], k)) for k in _BIG}
    shards = {k: opt[k][0].astype(BF16) for k in _BIG}
    small = {
        "norm_mix_g": norm_mix_g, "b_gate": b_gate, "q_norm_g": q_norm_g, "kv_norm_g": kv_norm_g,
        "sgu_norm_g": sgu_norm_g, "w_sgu": w_sgu[0], "b_sgu_col": b_sgu[0][:, :, None], "norm_ffn_g": norm_ffn_g,
        "norm_final_g": norm_final_g[None, :],
    }

    loss_row, grad_x, gs, updates = _local_step(x[0], positions.reshape(t, 1), loss_target[0], small, shards, opt)
    grads, deltas, new_m, new_v = {}, {}, {}, {}
    for k in _BIG:
        grads[k], deltas[k], new_m[k], new_v[k] = (unshard(a, k) for a in updates[k])

    small_grads = [gs["norm_mix_g"], gs["b_gate"], gs["q_norm_g"], gs["kv_norm_g"], gs["sgu_norm_g"], gs["w_sgu"],
                   gs["b_sgu_col"], gs["norm_ffn_g"], gs["norm_final_g"]]
    pack, sizes = _pack_rows([loss_row] + small_grads)
    total = _all_reduce_pack(pack)
    shapes = [(1, LANES)] + [wts[k].shape for k in _SMALL]
    unpacked = _unpack_rows(total, sizes, shapes)
    loss = unpacked[0][0, 0]
    for k, g in zip(_SMALL, unpacked[1:]):
        grads[k] = g
    g_pack = total[sizes[0][1]:]
    w_pack, _ = _pack_rows([wts[k] for k in _SMALL])
    m_pack, _ = _pack_rows([mom[k] for k in _SMALL])
    v_pack, _ = _pack_rows([var[k] for k in _SMALL])
    d_pack, nm_pack, nv_pack = _adamw_pack(g_pack, w_pack, m_pack, v_pack)
    small_shapes = [wts[k].shape for k in _SMALL]
    for store, pk in ((deltas, d_pack), (new_m, nm_pack), (new_v, nv_pack)):
        for k, a in zip(_SMALL, _unpack_rows(pk, sizes[1:], small_shapes)):
            store[k] = a

    return (loss, grad_x[None], *[grads[k] for k in _ORDER], *[deltas[k] for k in _ORDER],
            *[new_m[k] for k in _ORDER], *[new_v[k] for k in _ORDER])
```

```python
import functools
import math

import jax
import jax.numpy as jnp
from jax import lax
from jax.experimental import pallas as pl
from jax.experimental.pallas import tpu as pltpu

F32 = jnp.float32
BF16 = jnp.bfloat16

N_DEV = 8
N_HEADS = 16
QK_NOPE = 128
QK_ROPE = 64
V_HEAD = 128
HEAD_PAD = 256
ROPE_THETA = 10000.0
CHUNK = 128
SGU_GROUP = 128
RMS_EPS = 1e-6
LANES = 128
SUBLANES = 8

ADAM_LR = 0.001
ADAM_B1 = 0.9
ADAM_B2 = 0.999
ADAM_EPS = 1e-08
ADAM_WD = 0.01
ADAM_STEP = 10

VMEM_LIMIT = 48 * 1024 * 1024
MM_TILE = (2048, 512, 2048)
MM_TILE_TA = (512, 2048)
ATTN_TILE = 512
HEADS_PER_STEP = (4, 2)
ROW_KERNEL_BYTES = 24 * 1024 * 1024
SHARD_TILE_ELEMS = 256 * 1024
SLABS_PER_STEP = 2
TAIL_CHUNKS = 8
NEG_BIG = -1e30
MESH = pl.DeviceIdType.MESH


def _pick(n, target, mult=LANES):
    best = None
    d = mult
    while d <= min(n, target):
        if n % d == 0:
            best = d
        d += mult
    return best or n


def _row_tile(t, width, n_blocks, mult=2 * SUBLANES):
    return _pick(t, max(mult, ROW_KERNEL_BYTES // (3 * n_blocks * width * 4)), mult)


def _shard_tile(r, c, elems=SHARD_TILE_ELEMS, max_rows=256):
    tr = _pick(r, max_rows, 2 * SUBLANES)
    return tr, _pick(c, max(LANES, elems // tr))


def _params(sem):
    return pltpu.CompilerParams(dimension_semantics=sem, vmem_limit_bytes=VMEM_LIMIT)


def _full(shape):
    nd = len(shape)
    return pl.BlockSpec(shape, lambda *_: (0,) * nd)


def _rows(tr, w, cb=0):
    return pl.BlockSpec((tr, w), lambda i: (i, cb))


class _Comm:
    def __init__(self, ins, out_shapes, sems, start, finish, aliases=None):
        self.ins, self.out_shapes, self.sems, self.start, self.finish = list(ins), list(out_shapes), list(sems), start, finish
        self.aliases = dict(aliases or {})


def _call(body, *, grid, in_specs, out_specs, out_shape, scratch_shapes=(), sem, name, args, comm=None):
    if comm is None:
        outs = pl.pallas_call(body, grid=grid, in_specs=list(in_specs), out_specs=list(out_specs),
                              out_shape=list(out_shape), scratch_shapes=list(scratch_shapes),
                              compiler_params=_params(sem), name=name)(*args)
        return list(outs), []
    n_in, n_out, n_sc = len(in_specs), len(out_shape), len(scratch_shapes)
    nci, nco = len(comm.ins), len(comm.out_shapes)

    def hosted(*refs):
        ins, refs = refs[:n_in], refs[n_in:]
        cins, refs = refs[:nci], refs[nci:]
        outs, refs = refs[:n_out], refs[n_out:]
        couts, refs = refs[:nco], refs[nco:]
        scratch, csems = refs[:n_sc], refs[n_sc:]
        ids = [pl.program_id(i) for i in range(len(grid))]
        first = functools.reduce(jnp.logical_and, [i == 0 for i in ids])
        last = functools.reduce(jnp.logical_and, [i == g - 1 for i, g in zip(ids, grid)])

        @pl.when(first)
        def _():
            comm.start(cins, couts, csems)

        body(*ins, *outs, *scratch)

        @pl.when(last)
        def _():
            comm.finish(cins, couts, csems)

    any_spec = pl.BlockSpec(memory_space=pl.ANY)
    res = pl.pallas_call(
        hosted, grid=grid, in_specs=list(in_specs) + [any_spec] * nci, out_specs=list(out_specs) + [any_spec] * nco,
        out_shape=list(out_shape) + comm.out_shapes, scratch_shapes=list(scratch_shapes) + comm.sems,
        input_output_aliases={n_in + i: n_out + o for i, o in comm.aliases.items()},
        compiler_params=pltpu.CompilerParams(dimension_semantics=("arbitrary",) * len(grid),
                                             vmem_limit_bytes=VMEM_LIMIT, has_side_effects=True),
        name=name)(*args, *comm.ins)
    return list(res[:n_out]), list(res[n_out:])


def _swiglu_grads(g, u, d):
    s = 1.0 / (1.0 + jnp.exp(-g))
    return (d * u * (s * (1.0 + g * (1.0 - s)))).astype(BF16), (d * (g * s)).astype(BF16)


def _mm(a, b, *, ta=False, tb=False, add=None, out_dtype=F32, tm=None, tn=None, tk=None, name, comm=None,
        slab=None, a_slab0=0, swiglu=None, rope=None):
    sq = None
    if ta:
        tm, tn = tm or MM_TILE_TA[0], tn or MM_TILE_TA[1]
    if slab is None:
        m, k = (a.shape[1], a.shape[0]) if ta else a.shape
        n = b.shape[0] if tb else b.shape[1]
        assert k == (b.shape[1] if tb else b.shape[0]), (a.shape, b.shape, ta, tb)
        tm, tn, tk = _pick(m, tm or MM_TILE[0]), _pick(n, tn or MM_TILE[1]), _pick(k, tk or MM_TILE[2])
        if rope is not None:
            tn = _pick(n, max(tn, HEAD_PAD), HEAD_PAD)
        grid = (m // tm, n // tn, k // tk)
        a_spec = pl.BlockSpec((tk, tm), lambda i, j, kk: (kk, i)) if ta else pl.BlockSpec((tm, tk), lambda i, j, kk: (i, kk))
        b_spec = pl.BlockSpec((tn, tk), lambda i, j, kk: (j, kk)) if tb else pl.BlockSpec((tk, tn), lambda i, j, kk: (kk, j))
        o_spec, o_shape = pl.BlockSpec((tm, tn), lambda i, j, kk: (i, j)), (m, n)
    elif slab == "n":
        m, k = (a.shape[1], a.shape[0]) if ta else a.shape
        s, c = b.shape[0], (b.shape[1] if tb else b.shape[2])
        assert k == (b.shape[2] if tb else b.shape[1]), (a.shape, b.shape, ta, tb)
        tm, tn, tk = _pick(m, tm or MM_TILE[0]), c, _pick(k, tk or MM_TILE[2])
        grid = (m // tm, s, k // tk)
        a_spec = pl.BlockSpec((tk, tm), lambda i, j, kk: (kk, i)) if ta else pl.BlockSpec((tm, tk), lambda i, j, kk: (i, kk))
        b_spec = (pl.BlockSpec((sq, c, tk), lambda i, j, kk: (j, 0, kk)) if tb
                  else pl.BlockSpec((sq, tk, c), lambda i, j, kk: (j, kk, 0)))
        o_spec, o_shape = pl.BlockSpec((sq, tm, c), lambda i, j, kk: (j, i, 0)), (s, m, c)
    elif slab == "m":
        assert ta and not tb
        s, k, c = a.shape
        n = b.shape[1]
        assert k == b.shape[0], (a.shape, b.shape)
        tm, tn, tk = c, _pick(n, tn or MM_TILE[1]), _pick(k, tk or MM_TILE[2])
        grid = (s, n // tn, k // tk)
        a_spec = pl.BlockSpec((sq, tk, c), lambda i, j, kk: (i, kk, 0))
        b_spec = pl.BlockSpec((tk, tn), lambda i, j, kk: (kk, j))
        o_spec, o_shape = pl.BlockSpec((sq, c, tn), lambda i, j, kk: (i, 0, j)), (s, c, n)
    else:
        assert slab == "k" and not ta
        s, c = b.shape[0], (b.shape[2] if tb else b.shape[1])
        m, n = a.shape[1], (b.shape[1] if tb else b.shape[2])
        assert a.shape[2] == c and a.shape[0] >= a_slab0 + s, (a.shape, b.shape, a_slab0)
        tm, tn, tk = _pick(m, tm or MM_TILE[0]), _pick(n, tn or MM_TILE[1]), c
        per_step = SLABS_PER_STEP if (s % SLABS_PER_STEP == 0 and a_slab0 % SLABS_PER_STEP == 0) else 1
        first = a_slab0 // per_step
        grid = (m // tm, n // tn, s // per_step)
        a_spec = pl.BlockSpec((per_step, tm, c), lambda i, j, kk: (kk + first, i, 0))
        b_spec = (pl.BlockSpec((per_step, tn, c), lambda i, j, kk: (kk, j, 0)) if tb
                  else pl.BlockSpec((per_step, c, tn), lambda i, j, kk: (kk, 0, j)))
        o_spec, o_shape = pl.BlockSpec((tm, tn), lambda i, j, kk: (i, j)), (m, n)
    nk = grid[2]
    dims = (((0 if ta else 1,), (1 if tb else 0,)), ((), ()))

    def product(a_ref, b_ref):
        if slab != "k":
            return lax.dot_general(a_ref[...].astype(BF16), b_ref[...].astype(BF16), dims, preferred_element_type=F32)
        r = None
        for u in range(a_ref.shape[0]):
            p = lax.dot_general(a_ref[u].astype(BF16), b_ref[u].astype(BF16), dims, preferred_element_type=F32)
            r = p if r is None else r + p
        return r

    if swiglu is not None:
        assert slab == "n" and add is None
        o_block = pl.BlockSpec((2, sq, tm, c), lambda i, j, kk: (0, j, i, 0))
        o_shape, out_dtype = (2,) + o_shape, BF16

    if rope is not None:
        assert slab is None and add is None and swiglu is None and tn % HEAD_PAD == 0
        out_dtype = BF16
    extras = tuple(swiglu or ()) + tuple(rope or ())

    def body(*refs):
        a_ref, b_ref = refs[:2]
        add_ref = refs[2] if add is not None else None
        x0_ref, x1_ref = refs[2:4] if extras else (None, None)
        o_ref = refs[2 + (add is not None) + len(extras)]
        acc_ref = refs[-1] if nk > 1 else None

        def finish(r):
            if swiglu is not None:
                o_ref[0], o_ref[1] = _swiglu_grads(x0_ref[...], x1_ref[...], r)
                return
            if rope is not None:
                cos, sin = x0_ref[...], x1_ref[...]
                for h in range(tn // HEAD_PAD):
                    lo = h * HEAD_PAD
                    o_ref[:, lo:lo + QK_NOPE] = r[:, lo:lo + QK_NOPE].astype(BF16)
                    o_ref[:, lo + QK_NOPE:lo + HEAD_PAD] = _rope(r[:, lo + QK_NOPE:lo + HEAD_PAD], cos, sin).astype(BF16)
                return
            if add_ref is not None:
                r = r + add_ref[...].astype(F32)
            o_ref[...] = r.astype(o_ref.dtype)

        if nk == 1:
            finish(product(a_ref, b_ref))
            return
        kk = pl.program_id(2)

        @pl.when(kk == 0)
        def _():
            acc_ref[...] = product(a_ref, b_ref)

        if nk > 2:
            @pl.when(jnp.logical_and(kk > 0, kk < nk - 1))
            def _():
                acc_ref[...] += product(a_ref, b_ref)

        @pl.when(kk == nk - 1)
        def _():
            finish(acc_ref[...] + product(a_ref, b_ref))

    in_specs = [a_spec, b_spec] + ([o_spec] if add is not None else []) + ([o_spec] * 2 if swiglu is not None else [])
    if rope is not None:
        in_specs += [pl.BlockSpec((tm, LANES), lambda i, j, kk: (i, 0))] * 2
    args = (a, b) + ((add,) if add is not None else ()) + extras
    if swiglu is not None:
        o_spec = o_block
    outs, comm_outs = _call(
        body, grid=grid, in_specs=in_specs, out_specs=[o_spec],
        out_shape=[jax.ShapeDtypeStruct(o_shape, out_dtype)],
        scratch_shapes=[pltpu.VMEM((tm, tn), F32)] if nk > 1 else [],
        sem=("parallel", "parallel", "arbitrary"), name=name, args=args, comm=comm)
    return outs[0] if comm is None else (outs[0], comm_outs)


def _rms_scale(x):
    return lax.rsqrt(jnp.mean(x * x, axis=-1, keepdims=True) + RMS_EPS)


def _rms_bwd(xhat, r, g, dy):
    t = dy * g
    dx = r * (t - xhat * jnp.mean(t * xhat, axis=-1, keepdims=True))
    return dx, dy * xhat


_GELU_C = math.sqrt(2.0 / math.pi)


def _gelu(x):
    return x * (0.5 * (1.0 + jnp.tanh(_GELU_C * (x + 0.044715 * (x * x * x)))))


def _gelu_and_grad(x):
    t = jnp.tanh(_GELU_C * (x + 0.044715 * (x * x * x)))
    cdf = 0.5 * (1.0 + t)
    return x * cdf, cdf + x * (0.5 * (1.0 - t * t) * (_GELU_C * (1.0 + 3.0 * 0.044715 * (x * x))))


def _sigmoid(x):
    return 1.0 / (1.0 + jnp.exp(-x))


def _swap_halves(x):
    lane = lax.broadcasted_iota(jnp.int32, x.shape, 1)
    first = (lane % QK_ROPE) < (QK_ROPE // 2)
    return jnp.where(first, pltpu.roll(x, LANES - QK_ROPE // 2, 1), pltpu.roll(x, QK_ROPE // 2, 1))


def _rope(x, cos, sin_signed):
    return x * cos + _swap_halves(x) * sin_signed


def _rope_bwd(d, cos, sin_signed):
    return d * cos + _swap_halves(d * sin_signed)


def _rope_tables(pos_col, inv_freq_row, sign_row):
    t = pos_col.shape[0]
    tr = _pick(t, 512, SUBLANES)

    def body(p_ref, f_ref, s_ref, cos_ref, sin_ref):
        ang = p_ref[...].astype(F32) * f_ref[...]
        cos_ref[...] = jnp.cos(ang)
        sin_ref[...] = jnp.sin(ang) * s_ref[...]

    return pl.pallas_call(
        body, grid=(t // tr,), in_specs=[_rows(tr, 1), _full((1, LANES)), _full((1, LANES))],
        out_specs=[_rows(tr, LANES), _rows(tr, LANES)],
        out_shape=[jax.ShapeDtypeStruct((t, LANES), F32)] * 2,
        compiler_params=_params(("parallel",)), name="rope_tables")(pos_col, inv_freq_row, sign_row)


def _norm_fwd(x, g, name):
    t, d = x.shape
    tr = _row_tile(t, d, 2)

    def body(x_ref, g_ref, y_ref):
        xv = x_ref[...]
        y_ref[...] = (xv * _rms_scale(xv) * g_ref[...]).astype(BF16)

    return pl.pallas_call(
        body, grid=(t // tr,), in_specs=[_rows(tr, d), _full((1, d))], out_specs=_rows(tr, d),
        out_shape=jax.ShapeDtypeStruct((t, d), BF16), compiler_params=_params(("parallel",)), name=name)(x, g)


def _lat_fwd(z_lat, qg, kvg, cos, sin, ql, kvl):
    t = z_lat.shape[0]
    tr = _row_tile(t, z_lat.shape[1], 2)

    def body(z_ref, qg_ref, kvg_ref, cos_ref, sin_ref, qn_ref, kvn_ref, kpe_ref):
        q = z_ref[:, 0:ql]
        qn_ref[...] = (q * _rms_scale(q) * qg_ref[...]).astype(BF16)
        kv = z_ref[:, ql:ql + kvl]
        kvn_ref[...] = (kv * _rms_scale(kv) * kvg_ref[...]).astype(BF16)
        kpe_ref[...] = _rope(z_ref[:, ql + kvl:ql + kvl + LANES], cos_ref[...], sin_ref[...]).astype(BF16)

    w = z_lat.shape[1]
    return pl.pallas_call(
        body, grid=(t // tr,),
        in_specs=[_rows(tr, w), _full((1, ql)), _full((1, kvl)), _rows(tr, LANES), _rows(tr, LANES)],
        out_specs=[_rows(tr, ql), _rows(tr, kvl), _rows(tr, LANES)],
        out_shape=[jax.ShapeDtypeStruct((t, ql), BF16), jax.ShapeDtypeStruct((t, kvl), BF16),
                   jax.ShapeDtypeStruct((t, LANES), BF16)],
        compiler_params=_params(("parallel",)), name="lat_fwd")(z_lat, qg, kvg, cos, sin)


def _tril_mask():
    r = lax.broadcasted_iota(jnp.int32, (CHUNK, CHUNK), 0)
    c = lax.broadcasted_iota(jnp.int32, (CHUNK, CHUNK), 1)
    return r >= c


def _sgu_fwd(z_uv, gs, ws, b_col):
    t = z_uv.shape[0]
    sw = z_uv.shape[1] // 2
    groups = sw // SGU_GROUP
    tr = _pick(t, 256, CHUNK)

    def body(u_ref, v_ref, gs_ref, ws_ref, b_ref, o_ref):
        v = _gelu(v_ref[...])
        vn = (v * _rms_scale(v) * gs_ref[...]).astype(BF16)
        tri = _tril_mask()
        for g in range(groups):
            wg = jnp.where(tri, ws_ref[g], 0.0).astype(BF16)
            cols = slice(g * SGU_GROUP, (g + 1) * SGU_GROUP)
            for c in range(tr // CHUNK):
                rows = slice(c * CHUNK, (c + 1) * CHUNK)
                mixed = jnp.dot(wg, vn[rows, cols], preferred_element_type=F32) + b_ref[g]
                o_ref[rows, cols] = (_gelu(u_ref[rows, cols]) * mixed).astype(BF16)

    return pl.pallas_call(
        body, grid=(t // tr,),
        in_specs=[_rows(tr, sw, 0), _rows(tr, sw, 1), _full((1, sw)), _full(ws.shape), _full(b_col.shape)],
        out_specs=_rows(tr, sw), out_shape=jax.ShapeDtypeStruct((t, sw), BF16),
        compiler_params=_params(("parallel",)), name="sgu_fwd")(z_uv, z_uv, gs, ws, b_col)


def _merge_fwd(y_attn, y_sgu, z_g, b_gate, comm=None):
    t, d = y_attn.shape
    tr = _row_tile(t, d, 5)

    def body(ya_ref, ys_ref, g0_ref, g1_ref, b0_ref, b1_ref, o_ref):
        g0 = _sigmoid(g0_ref[...] + b0_ref[...])
        g1 = _sigmoid(g1_ref[...] + b1_ref[...])
        o_ref[...] = (g0 * ya_ref[...] + g1 * ys_ref[...]).astype(BF16)

    bspec0 = pl.BlockSpec((1, d), lambda i: (0, 0))
    bspec1 = pl.BlockSpec((1, d), lambda i: (0, 1))
    outs, comm_outs = _call(
        body, grid=(t // tr,),
        in_specs=[_rows(tr, d), _rows(tr, d), _rows(tr, d, 0), _rows(tr, d, 1), bspec0, bspec1],
        out_specs=[_rows(tr, d)], out_shape=[jax.ShapeDtypeStruct((t, d), BF16)],
        sem=("parallel",), name="merge_fwd", args=(y_attn, y_sgu, z_g, z_g, b_gate, b_gate), comm=comm)
    return outs[0], comm_outs


def _swiglu_fwd(gate, up, comm=None):
    t, f = gate.shape
    tr = _row_tile(t, f, 3)

    def body(g_ref, u_ref, o_ref):
        g = g_ref[...]
        o_ref[...] = (g * _sigmoid(g) * u_ref[...]).astype(BF16)

    outs, comm_outs = _call(
        body, grid=(t // tr,), in_specs=[_rows(tr, f), _rows(tr, f)], out_specs=[_rows(tr, f)],
        out_shape=[jax.ShapeDtypeStruct((t, f), BF16)], sem=("parallel",), name="swiglu_fwd", args=(gate, up), comm=comm)
    return outs[0], comm_outs


def _loss_head(h2, g, target):
    t, d = h2.shape
    tr = _row_tile(t, d, 3)

    def body(h_ref, g_ref, t_ref, loss_ref, dh_ref, dhb_ref, dg_ref):
        @pl.when(pl.program_id(0) == 0)
        def _():
            loss_ref[...] = jnp.zeros_like(loss_ref)
            dg_ref[...] = jnp.zeros_like(dg_ref)

        h = h_ref[...]
        r = _rms_scale(h)
        hhat = h * r
        gv = g_ref[...]
        err = hhat * gv - t_ref[...]
        loss_ref[...] += jnp.full(loss_ref.shape, 0.5 * jnp.sum(jnp.mean(err * err, axis=-1)), F32)
        dx, dg_rows = _rms_bwd(hhat, r, gv, err * (1.0 / d))
        dh_ref[...] = dx
        dhb_ref[...] = dx.astype(BF16)
        dg_ref[...] += jnp.sum(dg_rows, axis=0, keepdims=True)

    return pl.pallas_call(
        body, grid=(t // tr,), in_specs=[_rows(tr, d), _full((1, d)), _rows(tr, d)],
        out_specs=[_full((1, LANES)), _rows(tr, d), _rows(tr, d), _full((1, d))],
        out_shape=[jax.ShapeDtypeStruct((1, LANES), F32), jax.ShapeDtypeStruct((t, d), F32),
                   jax.ShapeDtypeStruct((t, d), BF16), jax.ShapeDtypeStruct((1, d), F32)],
        compiler_params=_params(("arbitrary",)), name="loss_head")(h2, g, target)


def _norm_bwd(x, g, dy, resid, name, comm=None):
    t, d = x.shape
    tr = _row_tile(t, d, 5)

    def body(x_ref, g_ref, dy_ref, r_ref, dx_ref, dxb_ref, dg_ref):
        @pl.when(pl.program_id(0) == 0)
        def _():
            dg_ref[...] = jnp.zeros_like(dg_ref)

        xv = x_ref[...]
        r = _rms_scale(xv)
        dx, dg_rows = _rms_bwd(xv * r, r, g_ref[...], dy_ref[...])
        dx = r_ref[...] + dx
        dx_ref[...] = dx
        dxb_ref[...] = dx.astype(BF16)
        dg_ref[...] += jnp.sum(dg_rows, axis=0, keepdims=True)

    outs, comm_outs = _call(
        body, grid=(t // tr,), in_specs=[_rows(tr, d), _full((1, d)), _rows(tr, d), _rows(tr, d)],
        out_specs=[_rows(tr, d), _rows(tr, d), _full((1, d))],
        out_shape=[jax.ShapeDtypeStruct((t, d), F32), jax.ShapeDtypeStruct((t, d), BF16),
                   jax.ShapeDtypeStruct((1, d), F32)],
        sem=("arbitrary",), name=name, args=(x, g, dy, resid), comm=comm)
    return (outs[0], outs[1], outs[2]) if comm is None else (outs[0], outs[1], outs[2], comm_outs)


def _merge_bwd(dmerged, y_attn, y_sgu, z_g, b_gate):
    t, d = y_attn.shape
    tr = _row_tile(t, d, 7)

    def body(dm_ref, ya_ref, ys_ref, g0_ref, g1_ref, b0_ref, b1_ref, dya_ref, dys_ref, dz_ref, db_ref):
        @pl.when(pl.program_id(0) == 0)
        def _():
            db_ref[...] = jnp.zeros_like(db_ref)

        dm = dm_ref[...]
        g0 = _sigmoid(g0_ref[...] + b0_ref[...])
        g1 = _sigmoid(g1_ref[...] + b1_ref[...])
        dya_ref[...] = (dm * g0).astype(BF16)
        dys_ref[...] = (dm * g1).astype(BF16)
        dl0 = dm * ya_ref[...] * (g0 * (1.0 - g0))
        dl1 = dm * ys_ref[...] * (g1 * (1.0 - g1))
        dz_ref[:, 0:d] = dl0.astype(BF16)
        dz_ref[:, d:2 * d] = dl1.astype(BF16)
        db_ref[:, 0:d] += jnp.sum(dl0, axis=0, keepdims=True)
        db_ref[:, d:2 * d] += jnp.sum(dl1, axis=0, keepdims=True)

    bspec0 = pl.BlockSpec((1, d), lambda i: (0, 0))
    bspec1 = pl.BlockSpec((1, d), lambda i: (0, 1))
    return pl.pallas_call(
        body, grid=(t // tr,),
        in_specs=[_rows(tr, d), _rows(tr, d), _rows(tr, d), _rows(tr, d, 0), _rows(tr, d, 1), bspec0, bspec1],
        out_specs=[_rows(tr, d), _rows(tr, d), _rows(tr, 2 * d), _full((1, 2 * d))],
        out_shape=[jax.ShapeDtypeStruct((t, d), BF16), jax.ShapeDtypeStruct((t, d), BF16),
                   jax.ShapeDtypeStruct((t, 2 * d), BF16), jax.ShapeDtypeStruct((1, 2 * d), F32)],
        compiler_params=_params(("arbitrary",)), name="merge_bwd")(dmerged, y_attn, y_sgu, z_g, z_g, b_gate, b_gate)


def _sgu_bwd(z_uv, ds_out, gs, ws, b_col):
    t = z_uv.shape[0]
    sw = z_uv.shape[1] // 2
    groups = sw // SGU_GROUP
    tr = _pick(t, 256, CHUNK)

    def body(u_ref, v_ref, d_ref, gs_ref, ws_ref, b_ref, dz_ref, dws_ref, db_ref, dgs_ref, dvn_ref):
        @pl.when(pl.program_id(0) == 0)
        def _():
            dws_ref[...] = jnp.zeros_like(dws_ref)
            db_ref[...] = jnp.zeros_like(db_ref)
            dgs_ref[...] = jnp.zeros_like(dgs_ref)

        v, dgelu_v = _gelu_and_grad(v_ref[...])
        r = _rms_scale(v)
        vhat = v * r
        gsv = gs_ref[...]
        vn = (vhat * gsv).astype(BF16)
        tri = _tril_mask()
        for g in range(groups):
            wg = jnp.where(tri, ws_ref[g], 0.0).astype(BF16)
            cols = slice(g * SGU_GROUP, (g + 1) * SGU_GROUP)
            for c in range(tr // CHUNK):
                rows = slice(c * CHUNK, (c + 1) * CHUNK)
                vn_cg = vn[rows, cols]
                mixed = jnp.dot(wg, vn_cg, preferred_element_type=F32) + b_ref[g]
                u, dgelu_u = _gelu_and_grad(u_ref[rows, cols])
                dso = d_ref[rows, cols]
                dz_ref[rows, cols] = (dso * mixed * dgelu_u).astype(BF16)
                dmixed = dso * u
                db_ref[g] += jnp.sum(dmixed, axis=1, keepdims=True)
                dmixed_b = dmixed.astype(BF16)
                dws_ref[g] += jnp.where(
                    tri, lax.dot_general(dmixed_b, vn_cg, (((1,), (1,)), ((), ())), preferred_element_type=F32), 0.0)
                dvn_ref[rows, cols] = lax.dot_general(wg, dmixed_b, (((0,), (0,)), ((), ())), preferred_element_type=F32)
        dvn = dvn_ref[...]
        dv, dgs_rows = _rms_bwd(vhat, r, gsv, dvn)
        dz_ref[:, sw:2 * sw] = (dv * dgelu_v).astype(BF16)
        dgs_ref[...] += jnp.sum(dgs_rows, axis=0, keepdims=True)

    return pl.pallas_call(
        body, grid=(t // tr,),
        in_specs=[_rows(tr, sw, 0), _rows(tr, sw, 1), _rows(tr, sw), _full((1, sw)), _full(ws.shape), _full(b_col.shape)],
        out_specs=[_rows(tr, 2 * sw), _full(ws.shape), _full(b_col.shape), _full((1, sw))],
        out_shape=[jax.ShapeDtypeStruct((t, 2 * sw), BF16), jax.ShapeDtypeStruct(ws.shape, F32),
                   jax.ShapeDtypeStruct(b_col.shape, F32), jax.ShapeDtypeStruct((1, sw), F32)],
        scratch_shapes=[pltpu.VMEM((tr, sw), F32)],
        compiler_params=_params(("arbitrary",)), name="sgu_bwd")(z_uv, z_uv, ds_out, gs, ws, b_col)


def _lat_bwd(z_lat, qg, kvg, dqn, dkvn, dkpe_heads, cos, sin, ql, kvl):
    t, w = z_lat.shape
    heads = dkpe_heads.shape[0]
    tr = _row_tile(t, w + heads * LANES, 3)

    def body(z_ref, qg_ref, kvg_ref, dq_ref, dkv_ref, dk_ref, cos_ref, sin_ref, dz_ref, dqg_ref, dkvg_ref):
        @pl.when(pl.program_id(0) == 0)
        def _():
            dqg_ref[...] = jnp.zeros_like(dqg_ref)
            dkvg_ref[...] = jnp.zeros_like(dkvg_ref)

        q = z_ref[:, 0:ql]
        r = _rms_scale(q)
        dx, dg_rows = _rms_bwd(q * r, r, qg_ref[...], dq_ref[...])
        dz_ref[:, 0:ql] = dx.astype(BF16)
        dqg_ref[...] += jnp.sum(dg_rows, axis=0, keepdims=True)
        kv = z_ref[:, ql:ql + kvl]
        r = _rms_scale(kv)
        dx, dg_rows = _rms_bwd(kv * r, r, kvg_ref[...], dkv_ref[...])
        dz_ref[:, ql:ql + kvl] = dx.astype(BF16)
        dkvg_ref[...] += jnp.sum(dg_rows, axis=0, keepdims=True)
        dk = dk_ref[0]
        for h in range(1, heads):
            dk = dk + dk_ref[h]
        dz_ref[:, ql + kvl:ql + kvl + LANES] = _rope_bwd(dk, cos_ref[...], sin_ref[...]).astype(BF16)

    return pl.pallas_call(
        body, grid=(t // tr,),
        in_specs=[_rows(tr, w), _full((1, ql)), _full((1, kvl)), _rows(tr, ql), _rows(tr, kvl),
                  pl.BlockSpec((heads, tr, LANES), lambda i: (0, i, 0)), _rows(tr, LANES), _rows(tr, LANES)],
        out_specs=[_rows(tr, w), _full((1, ql)), _full((1, kvl))],
        out_shape=[jax.ShapeDtypeStruct((t, w), BF16), jax.ShapeDtypeStruct((1, ql), F32),
                   jax.ShapeDtypeStruct((1, kvl), F32)],
        compiler_params=_params(("arbitrary",)), name="lat_bwd")(z_lat, qg, kvg, dqn, dkvn, dkpe_heads, cos, sin)


_NT = (((1,), (1,)), ((), ()))


def _attn_scale():
    return (QK_NOPE + QK_ROPE) ** -0.5


def _heads_per_step(heads, wanted):
    return wanted if heads % wanted == 0 else 1


def _attn_fwd(q_c, kv, kpe, comm=None):
    t = q_c.shape[0]
    heads = q_c.shape[1] // HEAD_PAD
    tq = _pick(t, ATTN_TILE)
    nq = t // tq
    scale = _attn_scale()
    to_log2 = scale * math.log2(math.e)
    tn_dims = (((0,), (0,)), ((), ()))

    hps = _heads_per_step(heads, HEADS_PER_STEP[0])

    def body(q_ref, kv_ref, kpe_ref, o_ref, ob_ref, lse_ref, m_sc, l_sc, acc_sc):
        qi, ki = pl.program_id(1), pl.program_id(2)

        @pl.when(ki == 0)
        def _():
            m_sc[...] = jnp.full_like(m_sc, NEG_BIG)
            l_sc[...] = jnp.zeros_like(l_sc)
            acc_sc[...] = jnp.zeros_like(acc_sc)

        def step(diagonal):
            for u in range(hps):
                lo = u * HEAD_PAD
                kc = jnp.concatenate([kv_ref[:, lo:lo + QK_NOPE], kpe_ref[...]], axis=1)
                st = lax.dot_general(kc, q_ref[:, lo:lo + HEAD_PAD], _NT, preferred_element_type=F32)
                if diagonal:
                    krow = lax.broadcasted_iota(jnp.int32, st.shape, 0)
                    qcol = lax.broadcasted_iota(jnp.int32, st.shape, 1)
                    st = jnp.where(qcol >= krow, st, NEG_BIG)
                m_prev = m_sc[u]
                m_new = jnp.maximum(m_prev, jnp.max(st, axis=0, keepdims=True))
                alpha = jnp.exp2((m_prev - m_new) * to_log2)
                pt = jnp.exp2((st - m_new) * to_log2)
                l_sc[u] = alpha * l_sc[u] + jnp.sum(pt, axis=0, keepdims=True)
                acc_sc[u] = alpha * acc_sc[u] + lax.dot_general(
                    kv_ref[:, lo + QK_NOPE:lo + HEAD_PAD], pt.astype(BF16), tn_dims, preferred_element_type=F32)
                m_sc[u] = m_new

        @pl.when(ki < qi)
        def _():
            step(False)

        @pl.when(ki == qi)
        def _():
            step(True)
            for u in range(hps):
                o = (acc_sc[u] / l_sc[u]).T
                o_ref[:, u * V_HEAD:(u + 1) * V_HEAD] = o
                ob_ref[:, u * V_HEAD:(u + 1) * V_HEAD] = o.astype(BF16)
                lse_ref[u] = m_sc[u] * scale + jnp.log(l_sc[u])

    omap = lambda g, qi, ki: (qi, g)
    outs, comm_outs = _call(
        body, grid=(heads // hps, nq, nq),
        in_specs=[pl.BlockSpec((tq, hps * HEAD_PAD), omap),
                  pl.BlockSpec((tq, hps * HEAD_PAD), lambda g, qi, ki: (jnp.minimum(ki, qi), g)),
                  pl.BlockSpec((tq, LANES), lambda g, qi, ki: (jnp.minimum(ki, qi), 0))],
        out_specs=[pl.BlockSpec((tq, hps * V_HEAD), omap), pl.BlockSpec((tq, hps * V_HEAD), omap),
                   pl.BlockSpec((hps, 1, tq), lambda g, qi, ki: (g, 0, qi))],
        out_shape=[jax.ShapeDtypeStruct((t, heads * V_HEAD), F32), jax.ShapeDtypeStruct((t, heads * V_HEAD), BF16),
                   jax.ShapeDtypeStruct((heads, 1, t), F32)],
        scratch_shapes=[pltpu.VMEM((hps, 1, tq), F32), pltpu.VMEM((hps, 1, tq), F32),
                        pltpu.VMEM((hps, V_HEAD, tq), F32)],
        sem=("parallel", "parallel", "arbitrary"), name="attn_fwd", args=(q_c, kv, kpe), comm=comm)
    return outs[0], outs[1], outs[2], comm_outs


def _attn_bwd(q_c, kv, kpe, o, do, lse_row, cos, sin, comm=None):
    t = q_c.shape[0]
    heads = q_c.shape[1] // HEAD_PAD
    tk = _pick(t, ATTN_TILE)
    nk = t // tk
    scale = _attn_scale()
    tn_dims = (((0,), (0,)), ((), ()))

    hps = _heads_per_step(heads, HEADS_PER_STEP[1])

    def body(q_ref, kv_ref, kpe_ref, do_ref, lse_ref, o_ref, cos_ref, sin_ref, dq_ref, dkv_ref, dkpe_ref,
             dk_sc, dv_sc, delta_sc, dq_sc):
        ki, qi = pl.program_id(1), pl.program_id(2)

        @pl.when(jnp.logical_and(ki == 0, qi == 0))
        def _():
            dq_sc[...] = jnp.zeros_like(dq_sc)

        @pl.when(qi == 0)
        def _():
            dk_sc[...] = jnp.zeros_like(dk_sc)
            dv_sc[...] = jnp.zeros_like(dv_sc)

        @pl.when(ki == 0)
        def _():
            for u in range(hps):
                cols = slice(u * V_HEAD, (u + 1) * V_HEAD)
                delta_sc[qi * hps + u] = jnp.sum((do_ref[:, cols] * o_ref[:, cols]).T, axis=0, keepdims=True)

        def step(diagonal):
            for u in range(hps):
                lo = u * HEAD_PAD
                kc = jnp.concatenate([kv_ref[:, lo:lo + QK_NOPE], kpe_ref[...]], axis=1)
                q = q_ref[:, lo:lo + HEAD_PAD]
                st = lax.dot_general(kc, q, _NT, preferred_element_type=F32) * scale
                pt = jnp.exp(st - lse_ref[u])
                if diagonal:
                    krow = lax.broadcasted_iota(jnp.int32, st.shape, 0)
                    qcol = lax.broadcasted_iota(jnp.int32, st.shape, 1)
                    pt = jnp.where(qcol >= krow, pt, 0.0)
                do_b = do_ref[:, u * V_HEAD:(u + 1) * V_HEAD].astype(BF16)
                dv_sc[u] += jnp.dot(pt.astype(BF16), do_b, preferred_element_type=F32)
                dpt = lax.dot_general(kv_ref[:, lo + QK_NOPE:lo + HEAD_PAD], do_b, _NT, preferred_element_type=F32)
                dst = (pt * (dpt - delta_sc[qi * hps + u]) * scale).astype(BF16)
                dk_sc[u] += jnp.dot(dst, q, preferred_element_type=F32)
                rows = pl.ds(pl.multiple_of(qi * tk, tk), tk)
                dq_sc[rows, lo:lo + HEAD_PAD] += lax.dot_general(dst, kc, tn_dims, preferred_element_type=F32)

        @pl.when(qi > ki)
        def _():
            step(False)

        @pl.when(qi == ki)
        def _():
            step(True)

        @pl.when(qi == nk - 1)
        def _():
            for u in range(hps):
                lo = u * HEAD_PAD
                dkv_ref[:, lo:lo + QK_NOPE] = dk_sc[u, :, 0:QK_NOPE].astype(BF16)
                dkv_ref[:, lo + QK_NOPE:lo + HEAD_PAD] = dv_sc[u].astype(BF16)
                dkpe_ref[u] = dk_sc[u, :, QK_NOPE:QK_NOPE + LANES]

        @pl.when(jnp.logical_and(ki == nk - 1, qi == nk - 1))
        def _():
            cos, sin = cos_ref[...], sin_ref[...]
            for u in range(hps):
                lo = u * HEAD_PAD
                dq_ref[:, lo:lo + QK_NOPE] = dq_sc[:, lo:lo + QK_NOPE].astype(BF16)
                dq_ref[:, lo + QK_NOPE:lo + HEAD_PAD] = _rope_bwd(dq_sc[:, lo + QK_NOPE:lo + HEAD_PAD], cos, sin).astype(BF16)

    qclamp = lambda g, ki, qi: (jnp.maximum(qi, ki), g)
    outs, comm_outs = _call(
        body, grid=(heads // hps, nk, nk),
        in_specs=[pl.BlockSpec((tk, hps * HEAD_PAD), qclamp),
                  pl.BlockSpec((tk, hps * HEAD_PAD), lambda g, ki, qi: (ki, g)),
                  pl.BlockSpec((tk, LANES), lambda g, ki, qi: (ki, 0)),
                  pl.BlockSpec((tk, hps * V_HEAD), qclamp),
                  pl.BlockSpec((hps, 1, tk), lambda g, ki, qi: (g, 0, jnp.maximum(qi, ki))),
                  pl.BlockSpec((tk, hps * V_HEAD), lambda g, ki, qi: (jnp.where(ki == 0, qi, 0), g)),
                  _full((t, LANES)), _full((t, LANES))],
        out_specs=[pl.BlockSpec((t, hps * HEAD_PAD), lambda g, ki, qi: (0, g)),
                   pl.BlockSpec((tk, hps * HEAD_PAD), lambda g, ki, qi: (ki, g)),
                   pl.BlockSpec((hps, tk, LANES), lambda g, ki, qi: (g, ki, 0))],
        out_shape=[jax.ShapeDtypeStruct((t, heads * HEAD_PAD), BF16),
                   jax.ShapeDtypeStruct((t, heads * HEAD_PAD), BF16), jax.ShapeDtypeStruct((heads, t, LANES), F32)],
        scratch_shapes=[pltpu.VMEM((hps, tk, HEAD_PAD), F32), pltpu.VMEM((hps, tk, V_HEAD), F32),
                        pltpu.VMEM((nk * hps, 1, tk), F32), pltpu.VMEM((t, hps * HEAD_PAD), F32)],
        sem=("parallel", "arbitrary", "arbitrary"), name="attn_bwd",
        args=(q_c, kv, kpe, do, lse_row, o, cos, sin), comm=comm)
    return outs[0], outs[1], outs[2], comm_outs


def _local_step(x, pos_col, target, small, shards, opt):
    t = x.shape[0]
    ql, kvl = small["q_norm_g"].shape[1], small["kv_norm_g"].shape[1]
    sw = small["sgu_norm_g"].shape[1]
    heads = (shards["w_uq"].shape[1] * N_DEV) // (QK_NOPE + QK_ROPE)
    big = {}
    early = ["w_in", "w_uq", "w_ukv"]
    big.update(_compute_layout(dict(zip(early, _all_gather([shards[k] for k in early]))), ql, kvl, heads, sw))
    half = QK_ROPE // 2
    lane = jnp.arange(LANES)
    inv_freq = ROPE_THETA ** (-jnp.arange(0, QK_ROPE, 2, dtype=F32) / QK_ROPE)
    inv_row = inv_freq[lane % half][None, :]
    sign_row = jnp.where((lane % QK_ROPE) < half, -1.0, 1.0).astype(F32)[None, :]
    cos, sin = _rope_tables(pos_col, inv_row, sign_row)
    ws = small["w_sgu"]
    b_col = small["b_sgu_col"]

    def arrived(names, bufs):
        big.update(_compute_layout(dict(zip(names, bufs)), ql, kvl, heads, sw))

    a = _norm_fwd(x, small["norm_mix_g"], "norm_mix_fwd")
    z_lat = _mm(a, big["w_lat_t"], tb=True, name="z_lat")
    z_uv, g_sgu = _mm(a, big["w_uv_t"], tb=True, name="z_uv", comm=_gather_stage(1, [shards["w_o_sgu"]]))
    z_g, (g_attn, g_sgu) = _mm(a, big["w_g_t"], tb=True, name="z_g",
                               comm=_join(_gather_stage(1, [shards["w_o_attn"]]), _gather_stage(2, g_sgu)))
    qn, kvn, kpe = _lat_fwd(z_lat, small["q_norm_g"], small["kv_norm_g"], cos, sin, ql, kvl)
    q_c, (g_attn, g_sgu) = _mm(qn, big["w_uq"], name="q_up_rope", rope=(cos, sin),
                               comm=_join(_gather_stage(2, [g_attn]), _gather_stage(3, [g_sgu])))
    kv, (g_attn, g_out) = _mm(kvn, big["w_ukv"], out_dtype=BF16, name="kv_up",
                              comm=_join(_gather_stage(3, [g_attn]), _gather_stage(1, [shards["w_out"]])))
    arrived(["w_o_sgu", "w_o_attn"], [g_sgu, g_attn])
    attn, attn_b, lse, (w_gate, w_up) = _attn_fwd(
        q_c, kv, kpe, comm=_gather_stage(1, [shards["w_gate_ffn"], shards["w_up_ffn"]]))
    s_out = _sgu_fwd(z_uv, small["sgu_norm_g"], ws, b_col)
    y_sgu, (g_out,) = _mm(s_out, big["w_o_sgu"], name="y_sgu", comm=_gather_stage(2, [g_out]))
    y_attn, (w_gate, g_out) = _mm(attn_b, big["w_o_attn"], name="y_attn",
                                  comm=_join(_gather_stage(2, [w_gate]), _gather_stage(3, [g_out])))
    arrived(["w_out"], [g_out])
    merged, (w_up, w_gate) = _merge_fwd(y_attn, y_sgu, z_g, small["b_gate"],
                                        comm=_join(_gather_stage(2, [w_up]), _gather_stage(3, [w_gate])))
    h1, (w_up,) = _mm(merged, big["w_out"], add=x, name="h1", comm=_gather_stage(3, [w_up]))
    f = _norm_fwd(h1, small["norm_ffn_g"], "norm_ffn_fwd")
    gate, w_down = _mm(f, w_gate, tb=True, slab="n", name="ffn_gate", comm=_gather_stage(1, [shards["w_down_ffn"]]))
    up, w_down = _mm(f, w_up, tb=True, slab="n", name="ffn_up", comm=_gather_stage(2, w_down))
    ffn = gate.shape[2]
    gate, up = gate.reshape(N_DEV * t, ffn), up.reshape(N_DEV * t, ffn)
    act, (w_down,) = _swiglu_fwd(gate, up, comm=_gather_stage(3, w_down))
    act = act.reshape(N_DEV, t, ffn)
    h2 = _mm(act, w_down, slab="k", add=h1, name="h2")
    loss_row, dh2, dh2_b, d_norm_final = _loss_head(h2, small["norm_final_g"], target)

    def pair_sums(names, slabs, bufs):
        return [_pair_sum(g, b, "pair_sum_" + k) for k, g, b in zip(names, slabs, bufs)]

    parts, updates = {}, {}

    def update(k, comm=None):
        w, m, v = opt[k]
        updates[k], got = _adamw_shard(parts[k], w, m, v, "adamw_" + k, comm=comm)
        return got

    down_slabs = [_mm(act, dh2_b, ta=True, slab="m", out_dtype=BF16, name="dw_down")]
    dgu, bufs = _mm(dh2_b, w_down, tb=True, slab="n", tm=MM_TILE[0] // 2, name="dact_swiglu_bwd",
                    comm=_to_sibling(down_slabs), swiglu=(gate.reshape(N_DEV, t, ffn), up.reshape(N_DEV, t, ffn)))
    dgu = dgu.reshape(2 * N_DEV, t, ffn)
    down_pair = pair_sums(["w_down_ffn"], down_slabs, bufs)
    dw_gu, got = _mm(dgu, f, ta=True, slab="m", out_dtype=BF16, name="dw_gate_up", comm=_to_chips(down_pair))
    parts["w_down_ffn"] = got[0]
    gu_names = ["w_gate_ffn", "w_up_ffn"]
    df, bufs = _mm(dgu, w_gate, slab="k", name="df_gate", comm=_to_sibling([dw_gu, dw_gu], first=[0, N_DEV]))
    gu_pairs = [_pair_sum(dw_gu, b, "pair_sum_" + k, first=s0) for k, b, s0 in zip(gu_names, bufs, [0, N_DEV])]
    half = _pick(gu_pairs[1].shape[1], gu_pairs[1].shape[1] // 2, 2 * SUBLANES)
    df, up_parts = _mm(dgu, w_up, slab="k", a_slab0=N_DEV, add=df, name="df_up",
                       comm=_to_chips(gu_pairs[1:], rows=[("r", 0, half)]))
    dh1, dh1_b, d_norm_ffn = _norm_bwd(h1, small["norm_ffn_g"], df, dh2, "norm_ffn_bwd")
    dw_out = _mm(merged, dh1_b, ta=True, out_dtype=BF16, name="dw_out")
    out_slabs = [_slabs_from_rows(dw_out)]
    dmerged, bufs = _mm(dh1_b, big["w_out"], tb=True, name="dmerged", comm=_to_sibling(out_slabs))
    out_pair = pair_sums(["w_out"], out_slabs, bufs)
    dy_attn, dy_sgu, dz_g, d_b_gate = _merge_bwd(dmerged, y_attn, y_sgu, z_g, small["b_gate"])
    dw_o_sgu = _mm(s_out, dy_sgu, ta=True, out_dtype=BF16, name="dw_o_sgu")
    ds_out = _mm(dy_sgu, big["w_o_sgu"], tb=True, name="ds_out")
    dz_uv, d_ws, d_b_col, d_sgu_norm = _sgu_bwd(z_uv, ds_out, small["sgu_norm_g"], ws, b_col)
    dw_o_attn = _mm(attn_b, dy_attn, ta=True, out_dtype=BF16, name="dw_o_attn")
    mix_names = ["w_o_sgu", "w_o_attn"]
    mix_slabs = [_slabs_from_cols(dw_o_sgu), _slabs_from_rows(dw_o_attn)]
    dattn, bufs = _mm(dy_attn, big["w_o_attn"], tb=True, name="dattn", comm=_to_sibling(mix_slabs))
    mix_pairs = pair_sums(mix_names, mix_slabs, bufs)
    rest = ("r", half, gu_pairs[1].shape[1] - half)
    dq_p, dkv, dkpe_heads, got = _attn_bwd(
        q_c, kv, kpe, attn, dattn, lse, cos, sin,
        comm=_join(_to_chips(gu_pairs[:1]), _to_chips(gu_pairs[1:], rows=[rest], into=up_parts)))
    parts.update(zip(gu_names, got))
    dw_uq = _mm(qn, dq_p, ta=True, out_dtype=BF16, name="dw_uq")
    dw_ukv = _mm(kvn, dkv, ta=True, out_dtype=BF16, name="dw_ukv")
    dqn = _mm(dq_p, big["w_uq"], tb=True, name="dqn")
    dkvn = _mm(dkv, big["w_ukv"], tb=True, name="dkvn")
    dz_lat, d_q_norm, d_kv_norm = _lat_bwd(z_lat, small["q_norm_g"], small["kv_norm_g"], dqn, dkvn, dkpe_heads,
                                           cos, sin, ql, kvl)
    dw_g, got = _mm(dz_g, a, ta=True, out_dtype=BF16, name="dw_g", comm=_to_chips(out_pair))
    parts["w_out"] = got[0]
    dw_uv, got = _mm(dz_uv, a, ta=True, out_dtype=BF16, name="dw_uv", comm=_to_chips(mix_pairs[1:]))
    parts["w_o_attn"] = got[0]
    dw_lat, got = _mm(dz_lat, a, ta=True, out_dtype=BF16, name="dw_lat", comm=_to_chips(mix_pairs[:1]))
    parts["w_o_sgu"] = got[0]
    lat = ql + kvl + QK_ROPE
    dw_uq_cols = dw_uq.reshape(ql, heads, HEAD_PAD)[:, :, :QK_NOPE + QK_ROPE].reshape(ql, heads * (QK_NOPE + QK_ROPE))
    in_names = ["w_uq", "w_ukv", "w_in"]
    in_slabs = [_slabs_from_cols(dw_uq_cols), _slabs_from_cols(dw_ukv),
                _slabs_from_rows(jnp.concatenate([dw_lat[:lat], dw_uv, dw_g], axis=0))]
    da = _mm(dz_lat, big["w_lat_t"], name="da_lat")
    da, bufs = _mm(dz_uv, big["w_uv_t"], add=da, name="da_uv", comm=_to_sibling(in_slabs))
    uq_pair, ukv_pair, in_pair = pair_sums(in_names, in_slabs, bufs)
    cols = in_pair.shape[2]
    chunk = _pick(cols, cols // TAIL_CHUNKS)
    first = min(cols, 2 * chunk)
    chunks = [("c", 0, first)] + [("c", c0, chunk) for c0 in range(first, cols, chunk)]
    da, got = _mm(dz_g, big["w_g_t"], add=da, name="da_g",
                  comm=_to_chips([uq_pair, in_pair], rows=[None, chunks[0]]))
    parts["w_uq"], in_parts = got
    grad_x, _, d_norm_mix, got = _norm_bwd(x, small["norm_mix_g"], da, dh1, "norm_mix_bwd", comm=_to_chips([ukv_pair]))
    parts["w_ukv"] = got[0]
    hosts = ["w_gate_ffn", "w_up_ffn", "w_down_ffn", "w_out", "w_o_attn", "w_o_sgu", "w_uq", "w_ukv"]
    assert len(chunks) <= 1 + len(hosts)
    for i, k in enumerate(hosts):
        if 1 + i < len(chunks):
            in_parts = update(k, comm=_to_chips([in_pair], rows=[chunks[1 + i]], into=[in_parts]))[0]
        else:
            update(k)
    parts["w_in"] = in_parts
    update("w_in")

    gs = {"norm_mix_g": d_norm_mix, "b_gate": d_b_gate, "q_norm_g": d_q_norm, "kv_norm_g": d_kv_norm,
          "sgu_norm_g": d_sgu_norm, "w_sgu": d_ws, "b_sgu_col": d_b_col, "norm_ffn_g": d_norm_ffn,
          "norm_final_g": d_norm_final}
    return loss_row, grad_x, gs, updates


def _my_place():
    return lax.axis_index("x"), lax.axis_index("y"), lax.axis_index("c")


N_CHIPS = N_DEV // 2

_GATHER_SEMS = [[(3,), (3,), ()], [(4,), (4,)], [(1,), (1,)]]


def _halves(shape):
    r, c = shape
    if (c // 2) % LANES == 0:
        return ("c", 0, c // 2), ("c", c // 2, c // 2)
    assert (r // 2) % (2 * SUBLANES) == 0, shape
    return ("r", 0, r // 2), ("r", r // 2, r // 2)


def _gather_copies(stage, ins, outs, sems):
    x, y, c = _my_place()
    me, x_nbr, y_nbr, diag = 4 * x + 2 * y + c, 4 * (1 - x) + 2 * y + c, 4 * x + 2 * (1 - y) + c, 4 * (1 - x) + 2 * (1 - y) + c
    sibling = (x, y, 1 - c)

    def remote(w, k, src, dst, to):
        return pltpu.make_async_remote_copy(src_ref=src, dst_ref=dst, send_sem=sems[0].at[w, k], recv_sem=sems[1].at[w, k],
                                            device_id=to, device_id_type=MESH)

    out = []
    for w in range(len(outs)):
        if stage == 1:
            dst = outs[w].at[me]
            out.append(pltpu.make_async_copy(ins[w], dst, sems[2].at[w]))
            out += [remote(w, k, ins[w], dst, to) for k, to in enumerate([sibling, (1 - x, y, c), (x, 1 - y, c)])]
        elif stage == 2:
            first, second = _halves(outs[w].shape[1:])
            out.append(remote(w, 0, _window(ins[w], x_nbr, first), _window(outs[w], x_nbr, first), (x, 1 - y, c)))
            out.append(remote(w, 1, _window(ins[w], y_nbr, second), _window(outs[w], y_nbr, second), (1 - x, y, c)))
            out.append(remote(w, 2, ins[w].at[x_nbr], outs[w].at[x_nbr], sibling))
            out.append(remote(w, 3, ins[w].at[y_nbr], outs[w].at[y_nbr], sibling))
        else:
            out.append(remote(w, 0, ins[w].at[diag], outs[w].at[diag], sibling))
    return out


def _gather_stage(stage, arrays):
    n = len(arrays)

    def start(ins, outs, sems):
        for cp in _gather_copies(stage, ins, outs, sems):
            cp.start()

    def finish(ins, outs, sems):
        for cp in _gather_copies(stage, ins, outs, sems):
            cp.wait()

    shapes = [jax.ShapeDtypeStruct(((N_DEV,) + a.shape) if stage == 1 else a.shape, a.dtype) for a in arrays]
    return _Comm(arrays, shapes, [pltpu.SemaphoreType.DMA((n,) + s) for s in _GATHER_SEMS[stage - 1]], start, finish,
                 aliases=None if stage == 1 else {w: w for w in range(n)})


def _join(*comms):
    ins, shapes, sems, aliases, spans = [], [], [], {}, []
    for cm in comms:
        spans.append((len(ins), len(ins) + len(cm.ins), len(shapes), len(shapes) + len(cm.out_shapes),
                      len(sems), len(sems) + len(cm.sems)))
        aliases.update({len(ins) + i: len(shapes) + o for i, o in cm.aliases.items()})
        ins, shapes, sems = ins + cm.ins, shapes + cm.out_shapes, sems + cm.sems

    def each(half):
        def run(i_refs, o_refs, s_refs):
            for cm, (i0, i1, o0, o1, s0, s1) in zip(comms, spans):
                getattr(cm, half)(i_refs[i0:i1], o_refs[o0:o1], s_refs[s0:s1])
        return run

    return _Comm(ins, shapes, sems, each("start"), each("finish"), aliases)


def _all_gather(shards):
    n = len(shards)
    n_sems = [len(s) for s in _GATHER_SEMS]

    def body(*refs):
        ins, outs, sems = refs[:n], refs[n:2 * n], refs[2 * n:]
        s0 = 0
        for stage in (1, 2, 3):
            mine = sems[s0:s0 + n_sems[stage - 1]]
            s0 += n_sems[stage - 1]
            copies = _gather_copies(stage, ins if stage == 1 else outs, outs, mine)
            for cp in copies:
                cp.start()
            for cp in copies:
                cp.wait()

    any_spec = pl.BlockSpec(memory_space=pl.ANY)
    return pl.pallas_call(
        body, in_specs=[any_spec] * n, out_specs=[any_spec] * n,
        out_shape=[jax.ShapeDtypeStruct((N_DEV,) + s.shape, s.dtype) for s in shards],
        scratch_shapes=[pltpu.SemaphoreType.DMA((n,) + s) for stage in _GATHER_SEMS for s in stage],
        compiler_params=pltpu.CompilerParams(has_side_effects=True), name="all_gather_weights")(*shards)


def _to_sibling(grads, first=None):
    n = len(grads)
    first = first or [0] * n

    def copies(ins, outs, sems):
        x, y, c = _my_place()
        send_sems, recv_sems = sems
        return [pltpu.make_async_remote_copy(
            src_ref=ins[w].at[first[w] + 2 * i + (1 - c)], dst_ref=outs[w].at[i], send_sem=send_sems.at[w, i],
            recv_sem=recv_sems.at[w, i], device_id=(x, y, 1 - c), device_id_type=MESH)
            for w in range(n) for i in range(N_CHIPS)]

    def start(ins, outs, sems):
        for cp in copies(ins, outs, sems):
            cp.start()

    def finish(ins, outs, sems):
        for cp in copies(ins, outs, sems):
            cp.wait()

    return _Comm(grads, [jax.ShapeDtypeStruct((N_CHIPS,) + g.shape[1:], g.dtype) for g in grads],
                 [pltpu.SemaphoreType.DMA((n, N_CHIPS)), pltpu.SemaphoreType.DMA((n, N_CHIPS))], start, finish)


def _window(ref, slab, win):
    if win is None:
        return ref.at[slab]
    if win[0] == "r":
        return ref.at[slab, pl.ds(win[1], win[2])]
    return ref.at[slab, slice(None), pl.ds(win[1], win[2])]


def _to_chips(parts, rows=None, into=None):
    n = len(parts)
    rows = rows or [None] * n

    def copies(ins, outs, sems):
        x, y, c = _my_place()
        send_sems, recv_sems, local_sems = sems
        mine = 2 * x + y
        chips = [(1 - x, y), (x, 1 - y), (1 - x, 1 - y)]
        remote = [pltpu.make_async_remote_copy(
            src_ref=_window(ins[w], 2 * cx + cy, rows[w]), dst_ref=_window(outs[w], mine, rows[w]),
            send_sem=send_sems.at[w, j], recv_sem=recv_sems.at[w, j], device_id=(cx, cy, c), device_id_type=MESH)
            for w in range(n) for j, (cx, cy) in enumerate(chips)]
        local = [pltpu.make_async_copy(_window(ins[w], mine, rows[w]), _window(outs[w], mine, rows[w]),
                                       local_sems.at[w]) for w in range(n)]
        return remote + local

    def start(ins, outs, sems):
        for cp in copies(ins, outs, sems):
            cp.start()

    def finish(ins, outs, sems):
        for cp in copies(ins, outs, sems):
            cp.wait()

    return _Comm(list(parts) + list(into or []), [jax.ShapeDtypeStruct(p.shape, p.dtype) for p in parts],
                 [pltpu.SemaphoreType.DMA((n, N_CHIPS - 1)), pltpu.SemaphoreType.DMA((n, N_CHIPS - 1)),
                  pltpu.SemaphoreType.DMA((n,))], start, finish,
                 aliases={n + w: w for w in range(n)} if into else None)


def _pair_sum(g, buf, name, first=0):
    _, r, c = g.shape
    tr, tc = _shard_tile(r, c, 4 * SHARD_TILE_ELEMS, 1024)
    core = (lax.axis_index("c") + first).astype(jnp.int32).reshape(1)

    def body(core_ref, g_ref, b_ref, o_ref):
        o_ref[...] = (g_ref[...].astype(F32) + b_ref[...].astype(F32)).astype(o_ref.dtype)

    blk = (1, tr, tc)
    return pl.pallas_call(
        body, grid_spec=pltpu.PrefetchScalarGridSpec(
            num_scalar_prefetch=1, grid=(N_CHIPS, r // tr, c // tc),
            in_specs=[pl.BlockSpec(blk, lambda i, j, l, core_ref: (2 * i + core_ref[0], j, l)),
                      pl.BlockSpec(blk, lambda i, j, l, core_ref: (i, j, l))],
            out_specs=pl.BlockSpec(blk, lambda i, j, l, core_ref: (i, j, l))),
        out_shape=jax.ShapeDtypeStruct(buf.shape, buf.dtype),
        compiler_params=_params(("parallel", "parallel", "parallel")), name=name)(core, g, buf)


def _all_reduce_pack(pack):
    r = pack.shape[0]

    def body(x_ref, out_ref, gath_ref, send_sems, recv_sems, local_sem):
        x, y, c = _my_place()
        me, sibling = (x, y, c), (x, y, 1 - c)
        chips = [(1 - x, y), (x, 1 - y), (1 - x, 1 - y)]

        def slab(place):
            return gath_ref.at[4 * place[0] + 2 * place[1] + place[2]]

        def copy(k, place, to, src=None):
            return pltpu.make_async_remote_copy(
                src_ref=slab(place) if src is None else src, dst_ref=slab(place),
                send_sem=send_sems.at[k], recv_sem=recv_sems.at[k], device_id=to, device_id_type=MESH)

        mine = pltpu.make_async_copy(x_ref, slab(me), local_sem)
        mine.start()
        first = [copy(0, me, sibling, src=x_ref)]
        first += [copy(1 + j, me, (*chip, c), src=x_ref) for j, chip in enumerate(chips)]
        for cp in first:
            cp.start()
        passed = [copy(4 + j, (*chip, c), sibling) for j, chip in enumerate(chips)]
        for j, chip in enumerate(chips):
            copy(1 + j, (*chip, c), me).wait_recv()
            passed[j].start()
        copy(0, sibling, me).wait_recv()
        for j, chip in enumerate(chips):
            copy(4 + j, (*chip, 1 - c), me).wait_recv()
        for cp in first + passed:
            cp.wait_send()
        mine.wait()
        acc = gath_ref[0]
        for i in range(1, N_DEV):
            acc = acc + gath_ref[i]
        out_ref[...] = acc

    vmem = pl.BlockSpec(memory_space=pltpu.VMEM)
    return pl.pallas_call(
        body, in_specs=[vmem], out_specs=vmem, out_shape=jax.ShapeDtypeStruct(pack.shape, F32),
        scratch_shapes=[pltpu.VMEM((N_DEV, r, LANES), F32), pltpu.SemaphoreType.DMA((7,)),
                        pltpu.SemaphoreType.DMA((7,)), pltpu.SemaphoreType.DMA],
        compiler_params=pltpu.CompilerParams(vmem_limit_bytes=VMEM_LIMIT), name="all_reduce_small")(pack)


def _adamw_math(w, g, m, v):
    m = ADAM_B1 * m + (1.0 - ADAM_B1) * g
    v = ADAM_B2 * v + (1.0 - ADAM_B2) * (g * g)
    m_hat = m / (1.0 - ADAM_B1 ** ADAM_STEP)
    v_hat = v / (1.0 - ADAM_B2 ** ADAM_STEP)
    delta = -ADAM_LR * (m_hat / (jnp.sqrt(v_hat) + ADAM_EPS) + ADAM_WD * w)
    return delta, m, v


def _adamw_shard(parts, w, m, v, name, comm=None):
    r, c = w.shape
    n_parts = parts.shape[0]
    tr, tc = _shard_tile(r, c)

    def body(p_ref, w_ref, m_ref, v_ref, g_ref, d_ref, nm_ref, nv_ref):
        g = p_ref[0].astype(F32)
        for i in range(1, n_parts):
            g = g + p_ref[i].astype(F32)
        g_ref[...] = g
        d_ref[...], nm_ref[...], nv_ref[...] = _adamw_math(w_ref[...], g, m_ref[...], v_ref[...])

    spec = pl.BlockSpec((tr, tc), lambda i, j: (i, j))
    outs, comm_outs = _call(
        body, grid=(r // tr, c // tc),
        in_specs=[pl.BlockSpec((n_parts, tr, tc), lambda i, j: (0, i, j)), spec, spec, spec],
        out_specs=[spec] * 4, out_shape=[jax.ShapeDtypeStruct((r, c), F32)] * 4,
        sem=("parallel", "parallel"), name=name, args=(parts, w, m, v), comm=comm)
    return outs, comm_outs


def _adamw_pack(g, w, m, v):
    r, c = w.shape

    def body(g_ref, w_ref, m_ref, v_ref, d_ref, nm_ref, nv_ref):
        d_ref[...], nm_ref[...], nv_ref[...] = _adamw_math(w_ref[...], g_ref[...], m_ref[...], v_ref[...])

    return pl.pallas_call(
        body, in_specs=[_full((r, c))] * 4, out_specs=[_full((r, c))] * 3, grid=(1,),
        out_shape=[jax.ShapeDtypeStruct((r, c), F32)] * 3,
        compiler_params=_params(("arbitrary",)), name="adamw_small")(g, w, m, v)


def _cols_from_slabs(g):
    return jnp.transpose(g, (1, 0, 2)).reshape(g.shape[1], N_DEV * g.shape[2])


def _slabs_from_cols(w):
    r, c8 = w.shape
    return jnp.transpose(w.reshape(r, N_DEV, c8 // N_DEV), (1, 0, 2))


def _rows_from_slabs(g):
    return g.reshape(N_DEV * g.shape[1], g.shape[2])


def _slabs_from_rows(w):
    return w.reshape(N_DEV, w.shape[0] // N_DEV, w.shape[1])


def _compute_layout(gathered, ql, kvl, heads, sw):
    out = {}
    for k, g in gathered.items():
        if k == "w_in":
            lat = ql + kvl + QK_ROPE
            w_in_t = _rows_from_slabs(g)
            out["w_lat_t"] = jnp.pad(w_in_t[:lat], ((0, LANES - QK_ROPE), (0, 0)))
            out["w_uv_t"] = w_in_t[lat:lat + 2 * sw]
            out["w_g_t"] = w_in_t[lat + 2 * sw:]
        elif k == "w_uq":
            per_head = _cols_from_slabs(g).reshape(ql, heads, QK_NOPE + QK_ROPE)
            pad = HEAD_PAD - QK_NOPE - QK_ROPE
            out["w_uq"] = jnp.pad(per_head, ((0, 0), (0, 0), (0, pad))).reshape(ql, heads * HEAD_PAD)
        elif k in ("w_o_attn", "w_out", "w_down_ffn"):
            out[k.removesuffix("_ffn")] = _rows_from_slabs(g)
        else:
            out[k.removesuffix("_ffn")] = _cols_from_slabs(g)
    return out


_SMALL =["norm_mix_g", "b_gate", "q_norm_g", "kv_norm_g", "sgu_norm_g", "w_sgu", "b_sgu", "norm_ffn_g", "norm_final_g"]
_BIG = ["w_in", "w_uq", "w_ukv", "w_o_attn", "w_o_sgu", "w_out", "w_gate_ffn", "w_up_ffn", "w_down_ffn"]
_TRANSPOSED = ("w_in", "w_gate_ffn", "w_up_ffn")
_ORDER = ["norm_mix_g", "w_in", "b_gate", "q_norm_g", "w_uq", "kv_norm_g", "w_ukv", "w_o_attn", "sgu_norm_g", "w_sgu",
          "b_sgu", "w_o_sgu", "w_out", "norm_ffn_g", "w_gate_ffn", "w_up_ffn", "w_down_ffn", "norm_final_g"]


def _pack_rows(parts):
    rows, sizes = [], []
    for p in parts:
        flat = p.reshape(-1)
        n = flat.shape[0]
        padded = -(-n // (SUBLANES * LANES)) * (SUBLANES * LANES)
        rows.append(jnp.pad(flat, (0, padded - n)).reshape(padded // LANES, LANES))
        sizes.append((n, padded // LANES))
    return jnp.concatenate(rows, axis=0), sizes


def _unpack_rows(pack, sizes, shapes):
    out, r0 = [], 0
    for (n, nr), shp in zip(sizes, shapes):
        out.append(pack[r0:r0 + nr].reshape(-1)[:n].reshape(shp))
        r0 += nr
    return out


def kernel(x, positions, norm_mix_g, w_in, b_gate, q_norm_g, w_uq, kv_norm_g, w_ukv, w_o_attn, sgu_norm_g, w_sgu, b_sgu, w_o_sgu, w_out, norm_ffn_g, w_gate_ffn, w_up_ffn, w_down_ffn, norm_final_g, loss_target, m_norm_mix_g, m_w_in, m_b_gate, m_q_norm_g, m_w_uq, m_kv_norm_g, m_w_ukv, m_w_o_attn, m_sgu_norm_g, m_w_sgu, m_b_sgu, m_w_o_sgu, m_w_out, m_norm_ffn_g, m_w_gate_ffn, m_w_up_ffn, m_w_down_ffn, m_norm_final_g, v_norm_mix_g, v_w_in, v_b_gate, v_q_norm_g, v_w_uq, v_kv_norm_g, v_w_ukv, v_w_o_attn, v_sgu_norm_g, v_w_sgu, v_b_sgu, v_w_o_sgu, v_w_out, v_norm_ffn_g, v_w_gate_ffn, v_w_up_ffn, v_w_down_ffn, v_norm_final_g):
    wts = dict(norm_mix_g=norm_mix_g, w_in=w_in, b_gate=b_gate, q_norm_g=q_norm_g, w_uq=w_uq, kv_norm_g=kv_norm_g,
               w_ukv=w_ukv, w_o_attn=w_o_attn, sgu_norm_g=sgu_norm_g, w_sgu=w_sgu, b_sgu=b_sgu, w_o_sgu=w_o_sgu,
               w_out=w_out, norm_ffn_g=norm_ffn_g, w_gate_ffn=w_gate_ffn, w_up_ffn=w_up_ffn, w_down_ffn=w_down_ffn,
               norm_final_g=norm_final_g)
    mom = dict(norm_mix_g=m_norm_mix_g, w_in=m_w_in, b_gate=m_b_gate, q_norm_g=m_q_norm_g, w_uq=m_w_uq,
               kv_norm_g=m_kv_norm_g, w_ukv=m_w_ukv, w_o_attn=m_w_o_attn, sgu_norm_g=m_sgu_norm_g, w_sgu=m_w_sgu,
               b_sgu=m_b_sgu, w_o_sgu=m_w_o_sgu, w_out=m_w_out, norm_ffn_g=m_norm_ffn_g, w_gate_ffn=m_w_gate_ffn,
               w_up_ffn=m_w_up_ffn, w_down_ffn=m_w_down_ffn, norm_final_g=m_norm_final_g)
    var = dict(norm_mix_g=v_norm_mix_g, w_in=v_w_in, b_gate=v_b_gate, q_norm_g=v_q_norm_g, w_uq=v_w_uq,
               kv_norm_g=v_kv_norm_g, w_ukv=v_w_ukv, w_o_attn=v_w_o_attn, sgu_norm_g=v_sgu_norm_g, w_sgu=v_w_sgu,
               b_sgu=v_b_sgu, w_o_sgu=v_w_o_sgu, w_out=v_w_out, norm_ffn_g=v_norm_ffn_g, w_gate_ffn=v_w_gate_ffn,
               w_up_ffn=v_w_up_ffn, w_down_ffn=v_w_down_ffn, norm_final_g=v_norm_final_g)

    t, d = x.shape[1], x.shape[2]
    ql, kvl = q_norm_g.shape[1], kv_norm_g.shape[1]
    heads = (w_uq.shape[2] * N_DEV) // (QK_NOPE + QK_ROPE)
    sw = sgu_norm_g.shape[1]

    def shard(a, k):
        return a[0].T if k in _TRANSPOSED else a[0]

    def unshard(a, k):
        return (a.T if k in _TRANSPOSED else a).reshape(wts[k].shape)

    opt = {k: (shard(wts[k], k), shard(mom[k], k), shard(var[k], k)) for k in _BIG}
    shards = {k: opt[k][0].astype(BF16) for k in _BIG}
    small = {
        "norm_mix_g": norm_mix_g, "b_gate": b_gate, "q_norm_g": q_norm_g, "kv_norm_g": kv_norm_g,
        "sgu_norm_g": sgu_norm_g, "w_sgu": w_sgu[0], "b_sgu_col": b_sgu[0][:, :, None], "norm_ffn_g": norm_ffn_g,
        "norm_final_g": norm_final_g[None, :],
    }

    loss_row, grad_x, gs, updates = _local_step(x[0], positions.reshape(t, 1), loss_target[0], small, shards, opt)
    grads, deltas, new_m, new_v = {}, {}, {}, {}
    for k in _BIG:
        grads[k], deltas[k], new_m[k], new_v[k] = (unshard(a, k) for a in updates[k])

    small_grads = [gs["norm_mix_g"], gs["b_gate"], gs["q_norm_g"], gs["kv_norm_g"], gs["sgu_norm_g"], gs["w_sgu"],
                   gs["b_sgu_col"], gs["norm_ffn_g"], gs["norm_final_g"]]
    pack, sizes = _pack_rows([loss_row] + small_grads)
    total = _all_reduce_pack(pack)
    shapes = [(1, LANES)] + [wts[k].shape for k in _SMALL]
    unpacked = _unpack_rows(total, sizes, shapes)
    loss = unpacked[0][0, 0]
    for k, g in zip(_SMALL, unpacked[1:]):
        grads[k] = g
    g_pack = total[sizes[0][1]:]
    w_pack, _ = _pack_rows([wts[k] for k in _SMALL])
    m_pack, _ = _pack_rows([mom[k] for k in _SMALL])
    v_pack, _ = _pack_rows([var[k] for k in _SMALL])
    d_pack, nm_pack, nv_pack = _adamw_pack(g_pack, w_pack, m_pack, v_pack)
    small_shapes = [wts[k].shape for k in _SMALL]
    for store, pk in ((deltas, d_pack), (new_m, nm_pack), (new_v, nv_pack)):
        for k, a in zip(_SMALL, _unpack_rows(pk, sizes[1:], small_shapes)):
            store[k] = a

    return (loss, grad_x[None], *[grads[k] for k in _ORDER], *[deltas[k] for k in _ORDER],
            *[new_m[k] for k in _ORDER], *[new_v[k] for k in _ORDER])
```

```python
import functools
import math

import jax
import jax.numpy as jnp
from jax import lax
from jax.experimental import pallas as pl
from jax.experimental.pallas import tpu as pltpu

F32 = jnp.float32
BF16 = jnp.bfloat16

N_DEV = 8
N_HEADS = 16
QK_NOPE = 128
QK_ROPE = 64
V_HEAD = 128
HEAD_PAD = 256
ROPE_THETA = 10000.0
CHUNK = 128
SGU_GROUP = 128
RMS_EPS = 1e-6
LANES = 128
SUBLANES = 8

ADAM_LR = 0.001
ADAM_B1 = 0.9
ADAM_B2 = 0.999
ADAM_EPS = 1e-08
ADAM_WD = 0.01
ADAM_STEP = 10

VMEM_LIMIT = 48 * 1024 * 1024
MM_TILE = (2048, 512, 2048)
MM_TILE_TA = (512, 2048)
ATTN_TILE = 512
HEADS_PER_STEP = (4, 2)
ROW_KERNEL_BYTES = 24 * 1024 * 1024
SHARD_TILE_ELEMS = 256 * 1024
SLABS_PER_STEP = 2
TAIL_SPLIT = 4
NEG_BIG = -1e30
MESH = pl.DeviceIdType.MESH


def _pick(n, target, mult=LANES):
    best = None
    d = mult
    while d <= min(n, target):
        if n % d == 0:
            best = d
        d += mult
    return best or n


def _row_tile(t, width, n_blocks, mult=2 * SUBLANES):
    return _pick(t, max(mult, ROW_KERNEL_BYTES // (3 * n_blocks * width * 4)), mult)


def _shard_tile(r, c, elems=SHARD_TILE_ELEMS, max_rows=256):
    tr = _pick(r, max_rows, 2 * SUBLANES)
    return tr, _pick(c, max(LANES, elems // tr))


def _params(sem):
    return pltpu.CompilerParams(dimension_semantics=sem, vmem_limit_bytes=VMEM_LIMIT)


def _full(shape):
    nd = len(shape)
    return pl.BlockSpec(shape, lambda *_: (0,) * nd)


def _rows(tr, w, cb=0):
    return pl.BlockSpec((tr, w), lambda i: (i, cb))


class _Comm:
    def __init__(self, ins, out_shapes, sems, start, finish, aliases=None):
        self.ins, self.out_shapes, self.sems, self.start, self.finish = list(ins), list(out_shapes), list(sems), start, finish
        self.aliases = dict(aliases or {})


def _call(body, *, grid, in_specs, out_specs, out_shape, scratch_shapes=(), sem, name, args, comm=None):
    if comm is None:
        outs = pl.pallas_call(body, grid=grid, in_specs=list(in_specs), out_specs=list(out_specs),
                              out_shape=list(out_shape), scratch_shapes=list(scratch_shapes),
                              compiler_params=_params(sem), name=name)(*args)
        return list(outs), []
    n_in, n_out, n_sc = len(in_specs), len(out_shape), len(scratch_shapes)
    nci, nco = len(comm.ins), len(comm.out_shapes)

    def hosted(*refs):
        ins, refs = refs[:n_in], refs[n_in:]
        cins, refs = refs[:nci], refs[nci:]
        outs, refs = refs[:n_out], refs[n_out:]
        couts, refs = refs[:nco], refs[nco:]
        scratch, csems = refs[:n_sc], refs[n_sc:]
        ids = [pl.program_id(i) for i in range(len(grid))]
        first = functools.reduce(jnp.logical_and, [i == 0 for i in ids])
        last = functools.reduce(jnp.logical_and, [i == g - 1 for i, g in zip(ids, grid)])

        @pl.when(first)
        def _():
            comm.start(cins, couts, csems)

        body(*ins, *outs, *scratch)

        @pl.when(last)
        def _():
            comm.finish(cins, couts, csems)

    any_spec = pl.BlockSpec(memory_space=pl.ANY)
    res = pl.pallas_call(
        hosted, grid=grid, in_specs=list(in_specs) + [any_spec] * nci, out_specs=list(out_specs) + [any_spec] * nco,
        out_shape=list(out_shape) + comm.out_shapes, scratch_shapes=list(scratch_shapes) + comm.sems,
        input_output_aliases={n_in + i: n_out + o for i, o in comm.aliases.items()},
        compiler_params=pltpu.CompilerParams(dimension_semantics=("arbitrary",) * len(grid),
                                             vmem_limit_bytes=VMEM_LIMIT, has_side_effects=True),
        name=name)(*args, *comm.ins)
    return list(res[:n_out]), list(res[n_out:])


def _swiglu_grads(g, u, d):
    s = 1.0 / (1.0 + jnp.exp(-g))
    return (d * u * (s * (1.0 + g * (1.0 - s)))).astype(BF16), (d * (g * s)).astype(BF16)


def _mm(a, b, *, ta=False, tb=False, add=None, out_dtype=F32, tm=None, tn=None, tk=None, name, comm=None,
        slab=None, a_slab0=0, swiglu=None, rope=None):
    sq = None
    if ta:
        tm, tn = tm or MM_TILE_TA[0], tn or MM_TILE_TA[1]
    if slab is None:
        m, k = (a.shape[1], a.shape[0]) if ta else a.shape
        n = b.shape[0] if tb else b.shape[1]
        assert k == (b.shape[1] if tb else b.shape[0]), (a.shape, b.shape, ta, tb)
        tm, tn, tk = _pick(m, tm or MM_TILE[0]), _pick(n, tn or MM_TILE[1]), _pick(k, tk or MM_TILE[2])
        if rope is not None:
            tn = _pick(n, max(tn, HEAD_PAD), HEAD_PAD)
        grid = (m // tm, n // tn, k // tk)
        a_spec = pl.BlockSpec((tk, tm), lambda i, j, kk: (kk, i)) if ta else pl.BlockSpec((tm, tk), lambda i, j, kk: (i, kk))
        b_spec = pl.BlockSpec((tn, tk), lambda i, j, kk: (j, kk)) if tb else pl.BlockSpec((tk, tn), lambda i, j, kk: (kk, j))
        o_spec, o_shape = pl.BlockSpec((tm, tn), lambda i, j, kk: (i, j)), (m, n)
    elif slab == "n":
        m, k = (a.shape[1], a.shape[0]) if ta else a.shape
        s, c = b.shape[0], (b.shape[1] if tb else b.shape[2])
        assert k == (b.shape[2] if tb else b.shape[1]), (a.shape, b.shape, ta, tb)
        tm, tn, tk = _pick(m, tm or MM_TILE[0]), c, _pick(k, tk or MM_TILE[2])
        grid = (m // tm, s, k // tk)
        a_spec = pl.BlockSpec((tk, tm), lambda i, j, kk: (kk, i)) if ta else pl.BlockSpec((tm, tk), lambda i, j, kk: (i, kk))
        b_spec = (pl.BlockSpec((sq, c, tk), lambda i, j, kk: (j, 0, kk)) if tb
                  else pl.BlockSpec((sq, tk, c), lambda i, j, kk: (j, kk, 0)))
        o_spec, o_shape = pl.BlockSpec((sq, tm, c), lambda i, j, kk: (j, i, 0)), (s, m, c)
    elif slab == "m":
        assert ta and not tb
        s, k, c = a.shape
        n = b.shape[1]
        assert k == b.shape[0], (a.shape, b.shape)
        tm, tn, tk = c, _pick(n, tn or MM_TILE[1]), _pick(k, tk or MM_TILE[2])
        grid = (s, n // tn, k // tk)
        a_spec = pl.BlockSpec((sq, tk, c), lambda i, j, kk: (i, kk, 0))
        b_spec = pl.BlockSpec((tk, tn), lambda i, j, kk: (kk, j))
        o_spec, o_shape = pl.BlockSpec((sq, c, tn), lambda i, j, kk: (i, 0, j)), (s, c, n)
    else:
        assert slab == "k" and not ta
        s, c = b.shape[0], (b.shape[2] if tb else b.shape[1])
        m, n = a.shape[1], (b.shape[1] if tb else b.shape[2])
        assert a.shape[2] == c and a.shape[0] >= a_slab0 + s, (a.shape, b.shape, a_slab0)
        tm, tn, tk = _pick(m, tm or MM_TILE[0]), _pick(n, tn or MM_TILE[1]), c
        per_step = SLABS_PER_STEP if (s % SLABS_PER_STEP == 0 and a_slab0 % SLABS_PER_STEP == 0) else 1
        first = a_slab0 // per_step
        grid = (m // tm, n // tn, s // per_step)
        a_spec = pl.BlockSpec((per_step, tm, c), lambda i, j, kk: (kk + first, i, 0))
        b_spec = (pl.BlockSpec((per_step, tn, c), lambda i, j, kk: (kk, j, 0)) if tb
                  else pl.BlockSpec((per_step, c, tn), lambda i, j, kk: (kk, 0, j)))
        o_spec, o_shape = pl.BlockSpec((tm, tn), lambda i, j, kk: (i, j)), (m, n)
    nk = grid[2]
    dims = (((0 if ta else 1,), (1 if tb else 0,)), ((), ()))

    def product(a_ref, b_ref):
        if slab != "k":
            return lax.dot_general(a_ref[...].astype(BF16), b_ref[...].astype(BF16), dims, preferred_element_type=F32)
        r = None
        for u in range(a_ref.shape[0]):
            p = lax.dot_general(a_ref[u].astype(BF16), b_ref[u].astype(BF16), dims, preferred_element_type=F32)
            r = p if r is None else r + p
        return r

    if swiglu is not None:
        assert slab == "n" and add is None
        o_block = pl.BlockSpec((2, sq, tm, c), lambda i, j, kk: (0, j, i, 0))
        o_shape, out_dtype = (2,) + o_shape, BF16

    if rope is not None:
        assert slab is None and add is None and swiglu is None and tn % HEAD_PAD == 0
        out_dtype = BF16
    extras = tuple(swiglu or ()) + tuple(rope or ())

    def body(*refs):
        a_ref, b_ref = refs[:2]
        add_ref = refs[2] if add is not None else None
        x0_ref, x1_ref = refs[2:4] if extras else (None, None)
        o_ref = refs[2 + (add is not None) + len(extras)]
        acc_ref = refs[-1] if nk > 1 else None

        def finish(r):
            if swiglu is not None:
                o_ref[0], o_ref[1] = _swiglu_grads(x0_ref[...], x1_ref[...], r)
                return
            if rope is not None:
                cos, sin = x0_ref[...], x1_ref[...]
                for h in range(tn // HEAD_PAD):
                    lo = h * HEAD_PAD
                    o_ref[:, lo:lo + QK_NOPE] = r[:, lo:lo + QK_NOPE].astype(BF16)
                    o_ref[:, lo + QK_NOPE:lo + HEAD_PAD] = _rope(r[:, lo + QK_NOPE:lo + HEAD_PAD], cos, sin).astype(BF16)
                return
            if add_ref is not None:
                r = r + add_ref[...].astype(F32)
            o_ref[...] = r.astype(o_ref.dtype)

        if nk == 1:
            finish(product(a_ref, b_ref))
            return
        kk = pl.program_id(2)

        @pl.when(kk == 0)
        def _():
            acc_ref[...] = product(a_ref, b_ref)

        if nk > 2:
            @pl.when(jnp.logical_and(kk > 0, kk < nk - 1))
            def _():
                acc_ref[...] += product(a_ref, b_ref)

        @pl.when(kk == nk - 1)
        def _():
            finish(acc_ref[...] + product(a_ref, b_ref))

    in_specs = [a_spec, b_spec] + ([o_spec] if add is not None else []) + ([o_spec] * 2 if swiglu is not None else [])
    if rope is not None:
        in_specs += [pl.BlockSpec((tm, LANES), lambda i, j, kk: (i, 0))] * 2
    args = (a, b) + ((add,) if add is not None else ()) + extras
    if swiglu is not None:
        o_spec = o_block
    outs, comm_outs = _call(
        body, grid=grid, in_specs=in_specs, out_specs=[o_spec],
        out_shape=[jax.ShapeDtypeStruct(o_shape, out_dtype)],
        scratch_shapes=[pltpu.VMEM((tm, tn), F32)] if nk > 1 else [],
        sem=("parallel", "parallel", "arbitrary"), name=name, args=args, comm=comm)
    return outs[0] if comm is None else (outs[0], comm_outs)


def _rms_scale(x):
    return lax.rsqrt(jnp.mean(x * x, axis=-1, keepdims=True) + RMS_EPS)


def _rms_bwd(xhat, r, g, dy):
    t = dy * g
    dx = r * (t - xhat * jnp.mean(t * xhat, axis=-1, keepdims=True))
    return dx, dy * xhat


_GELU_C = math.sqrt(2.0 / math.pi)


def _gelu(x):
    return x * (0.5 * (1.0 + jnp.tanh(_GELU_C * (x + 0.044715 * (x * x * x)))))


def _gelu_and_grad(x):
    t = jnp.tanh(_GELU_C * (x + 0.044715 * (x * x * x)))
    cdf = 0.5 * (1.0 + t)
    return x * cdf, cdf + x * (0.5 * (1.0 - t * t) * (_GELU_C * (1.0 + 3.0 * 0.044715 * (x * x))))


def _sigmoid(x):
    return 1.0 / (1.0 + jnp.exp(-x))


def _swap_halves(x):
    lane = lax.broadcasted_iota(jnp.int32, x.shape, 1)
    first = (lane % QK_ROPE) < (QK_ROPE // 2)
    return jnp.where(first, pltpu.roll(x, LANES - QK_ROPE // 2, 1), pltpu.roll(x, QK_ROPE // 2, 1))


def _rope(x, cos, sin_signed):
    return x * cos + _swap_halves(x) * sin_signed


def _rope_bwd(d, cos, sin_signed):
    return d * cos + _swap_halves(d * sin_signed)


def _rope_tables(pos_col, inv_freq_row, sign_row):
    t = pos_col.shape[0]
    tr = _pick(t, 512, SUBLANES)

    def body(p_ref, f_ref, s_ref, cos_ref, sin_ref):
        ang = p_ref[...].astype(F32) * f_ref[...]
        cos_ref[...] = jnp.cos(ang)
        sin_ref[...] = jnp.sin(ang) * s_ref[...]

    return pl.pallas_call(
        body, grid=(t // tr,), in_specs=[_rows(tr, 1), _full((1, LANES)), _full((1, LANES))],
        out_specs=[_rows(tr, LANES), _rows(tr, LANES)],
        out_shape=[jax.ShapeDtypeStruct((t, LANES), F32)] * 2,
        compiler_params=_params(("parallel",)), name="rope_tables")(pos_col, inv_freq_row, sign_row)


def _norm_fwd(x, g, name):
    t, d = x.shape
    tr = _row_tile(t, d, 2)

    def body(x_ref, g_ref, y_ref):
        xv = x_ref[...]
        y_ref[...] = (xv * _rms_scale(xv) * g_ref[...]).astype(BF16)

    return pl.pallas_call(
        body, grid=(t // tr,), in_specs=[_rows(tr, d), _full((1, d))], out_specs=_rows(tr, d),
        out_shape=jax.ShapeDtypeStruct((t, d), BF16), compiler_params=_params(("parallel",)), name=name)(x, g)


def _lat_fwd(z_lat, qg, kvg, cos, sin, ql, kvl):
    t = z_lat.shape[0]
    tr = _row_tile(t, z_lat.shape[1], 2)

    def body(z_ref, qg_ref, kvg_ref, cos_ref, sin_ref, qn_ref, kvn_ref, kpe_ref):
        q = z_ref[:, 0:ql]
        qn_ref[...] = (q * _rms_scale(q) * qg_ref[...]).astype(BF16)
        kv = z_ref[:, ql:ql + kvl]
        kvn_ref[...] = (kv * _rms_scale(kv) * kvg_ref[...]).astype(BF16)
        kpe_ref[...] = _rope(z_ref[:, ql + kvl:ql + kvl + LANES], cos_ref[...], sin_ref[...]).astype(BF16)

    w = z_lat.shape[1]
    return pl.pallas_call(
        body, grid=(t // tr,),
        in_specs=[_rows(tr, w), _full((1, ql)), _full((1, kvl)), _rows(tr, LANES), _rows(tr, LANES)],
        out_specs=[_rows(tr, ql), _rows(tr, kvl), _rows(tr, LANES)],
        out_shape=[jax.ShapeDtypeStruct((t, ql), BF16), jax.ShapeDtypeStruct((t, kvl), BF16),
                   jax.ShapeDtypeStruct((t, LANES), BF16)],
        compiler_params=_params(("parallel",)), name="lat_fwd")(z_lat, qg, kvg, cos, sin)


def _tril_mask():
    r = lax.broadcasted_iota(jnp.int32, (CHUNK, CHUNK), 0)
    c = lax.broadcasted_iota(jnp.int32, (CHUNK, CHUNK), 1)
    return r >= c


def _sgu_fwd(z_uv, gs, ws, b_col):
    t = z_uv.shape[0]
    sw = z_uv.shape[1] // 2
    groups = sw // SGU_GROUP
    tr = _pick(t, 256, CHUNK)

    def body(u_ref, v_ref, gs_ref, ws_ref, b_ref, o_ref):
        v = _gelu(v_ref[...])
        vn = (v * _rms_scale(v) * gs_ref[...]).astype(BF16)
        tri = _tril_mask()
        for g in range(groups):
            wg = jnp.where(tri, ws_ref[g], 0.0).astype(BF16)
            cols = slice(g * SGU_GROUP, (g + 1) * SGU_GROUP)
            for c in range(tr // CHUNK):
                rows = slice(c * CHUNK, (c + 1) * CHUNK)
                mixed = jnp.dot(wg, vn[rows, cols], preferred_element_type=F32) + b_ref[g]
                o_ref[rows, cols] = (_gelu(u_ref[rows, cols]) * mixed).astype(BF16)

    return pl.pallas_call(
        body, grid=(t // tr,),
        in_specs=[_rows(tr, sw, 0), _rows(tr, sw, 1), _full((1, sw)), _full(ws.shape), _full(b_col.shape)],
        out_specs=_rows(tr, sw), out_shape=jax.ShapeDtypeStruct((t, sw), BF16),
        compiler_params=_params(("parallel",)), name="sgu_fwd")(z_uv, z_uv, gs, ws, b_col)


def _merge_fwd(y_attn, y_sgu, z_g, b_gate, comm=None):
    t, d = y_attn.shape
    tr = _row_tile(t, d, 5)

    def body(ya_ref, ys_ref, g0_ref, g1_ref, b0_ref, b1_ref, o_ref):
        g0 = _sigmoid(g0_ref[...] + b0_ref[...])
        g1 = _sigmoid(g1_ref[...] + b1_ref[...])
        o_ref[...] = (g0 * ya_ref[...] + g1 * ys_ref[...]).astype(BF16)

    bspec0 = pl.BlockSpec((1, d), lambda i: (0, 0))
    bspec1 = pl.BlockSpec((1, d), lambda i: (0, 1))
    outs, comm_outs = _call(
        body, grid=(t // tr,),
        in_specs=[_rows(tr, d), _rows(tr, d), _rows(tr, d, 0), _rows(tr, d, 1), bspec0, bspec1],
        out_specs=[_rows(tr, d)], out_shape=[jax.ShapeDtypeStruct((t, d), BF16)],
        sem=("parallel",), name="merge_fwd", args=(y_attn, y_sgu, z_g, z_g, b_gate, b_gate), comm=comm)
    return outs[0], comm_outs


def _swiglu_fwd(gate, up, comm=None):
    t, f = gate.shape
    tr = _row_tile(t, f, 3)

    def body(g_ref, u_ref, o_ref):
        g = g_ref[...]
        o_ref[...] = (g * _sigmoid(g) * u_ref[...]).astype(BF16)

    outs, comm_outs = _call(
        body, grid=(t // tr,), in_specs=[_rows(tr, f), _rows(tr, f)], out_specs=[_rows(tr, f)],
        out_shape=[jax.ShapeDtypeStruct((t, f), BF16)], sem=("parallel",), name="swiglu_fwd", args=(gate, up), comm=comm)
    return outs[0], comm_outs


def _loss_head(h2, g, target):
    t, d = h2.shape
    tr = _row_tile(t, d, 3)

    def body(h_ref, g_ref, t_ref, loss_ref, dh_ref, dhb_ref, dg_ref):
        @pl.when(pl.program_id(0) == 0)
        def _():
            loss_ref[...] = jnp.zeros_like(loss_ref)
            dg_ref[...] = jnp.zeros_like(dg_ref)

        h = h_ref[...]
        r = _rms_scale(h)
        hhat = h * r
        gv = g_ref[...]
        err = hhat * gv - t_ref[...]
        loss_ref[...] += jnp.full(loss_ref.shape, 0.5 * jnp.sum(jnp.mean(err * err, axis=-1)), F32)
        dx, dg_rows = _rms_bwd(hhat, r, gv, err * (1.0 / d))
        dh_ref[...] = dx
        dhb_ref[...] = dx.astype(BF16)
        dg_ref[...] += jnp.sum(dg_rows, axis=0, keepdims=True)

    return pl.pallas_call(
        body, grid=(t // tr,), in_specs=[_rows(tr, d), _full((1, d)), _rows(tr, d)],
        out_specs=[_full((1, LANES)), _rows(tr, d), _rows(tr, d), _full((1, d))],
        out_shape=[jax.ShapeDtypeStruct((1, LANES), F32), jax.ShapeDtypeStruct((t, d), F32),
                   jax.ShapeDtypeStruct((t, d), BF16), jax.ShapeDtypeStruct((1, d), F32)],
        compiler_params=_params(("arbitrary",)), name="loss_head")(h2, g, target)


def _norm_bwd(x, g, dy, resid, name, comm=None):
    t, d = x.shape
    tr = _row_tile(t, d, 5)

    def body(x_ref, g_ref, dy_ref, r_ref, dx_ref, dxb_ref, dg_ref):
        @pl.when(pl.program_id(0) == 0)
        def _():
            dg_ref[...] = jnp.zeros_like(dg_ref)

        xv = x_ref[...]
        r = _rms_scale(xv)
        dx, dg_rows = _rms_bwd(xv * r, r, g_ref[...], dy_ref[...])
        dx = r_ref[...] + dx
        dx_ref[...] = dx
        dxb_ref[...] = dx.astype(BF16)
        dg_ref[...] += jnp.sum(dg_rows, axis=0, keepdims=True)

    outs, comm_outs = _call(
        body, grid=(t // tr,), in_specs=[_rows(tr, d), _full((1, d)), _rows(tr, d), _rows(tr, d)],
        out_specs=[_rows(tr, d), _rows(tr, d), _full((1, d))],
        out_shape=[jax.ShapeDtypeStruct((t, d), F32), jax.ShapeDtypeStruct((t, d), BF16),
                   jax.ShapeDtypeStruct((1, d), F32)],
        sem=("arbitrary",), name=name, args=(x, g, dy, resid), comm=comm)
    return (outs[0], outs[1], outs[2]) if comm is None else (outs[0], outs[1], outs[2], comm_outs)


def _merge_bwd(dmerged, y_attn, y_sgu, z_g, b_gate):
    t, d = y_attn.shape
    tr = _row_tile(t, d, 7)

    def body(dm_ref, ya_ref, ys_ref, g0_ref, g1_ref, b0_ref, b1_ref, dya_ref, dys_ref, dz_ref, db_ref):
        @pl.when(pl.program_id(0) == 0)
        def _():
            db_ref[...] = jnp.zeros_like(db_ref)

        dm = dm_ref[...]
        g0 = _sigmoid(g0_ref[...] + b0_ref[...])
        g1 = _sigmoid(g1_ref[...] + b1_ref[...])
        dya_ref[...] = (dm * g0).astype(BF16)
        dys_ref[...] = (dm * g1).astype(BF16)
        dl0 = dm * ya_ref[...] * (g0 * (1.0 - g0))
        dl1 = dm * ys_ref[...] * (g1 * (1.0 - g1))
        dz_ref[:, 0:d] = dl0.astype(BF16)
        dz_ref[:, d:2 * d] = dl1.astype(BF16)
        db_ref[:, 0:d] += jnp.sum(dl0, axis=0, keepdims=True)
        db_ref[:, d:2 * d] += jnp.sum(dl1, axis=0, keepdims=True)

    bspec0 = pl.BlockSpec((1, d), lambda i: (0, 0))
    bspec1 = pl.BlockSpec((1, d), lambda i: (0, 1))
    return pl.pallas_call(
        body, grid=(t // tr,),
        in_specs=[_rows(tr, d), _rows(tr, d), _rows(tr, d), _rows(tr, d, 0), _rows(tr, d, 1), bspec0, bspec1],
        out_specs=[_rows(tr, d), _rows(tr, d), _rows(tr, 2 * d), _full((1, 2 * d))],
        out_shape=[jax.ShapeDtypeStruct((t, d), BF16), jax.ShapeDtypeStruct((t, d), BF16),
                   jax.ShapeDtypeStruct((t, 2 * d), BF16), jax.ShapeDtypeStruct((1, 2 * d), F32)],
        compiler_params=_params(("arbitrary",)), name="merge_bwd")(dmerged, y_attn, y_sgu, z_g, z_g, b_gate, b_gate)


def _sgu_bwd(z_uv, ds_out, gs, ws, b_col):
    t = z_uv.shape[0]
    sw = z_uv.shape[1] // 2
    groups = sw // SGU_GROUP
    tr = _pick(t, 256, CHUNK)

    def body(u_ref, v_ref, d_ref, gs_ref, ws_ref, b_ref, dz_ref, dws_ref, db_ref, dgs_ref, dvn_ref):
        @pl.when(pl.program_id(0) == 0)
        def _():
            dws_ref[...] = jnp.zeros_like(dws_ref)
            db_ref[...] = jnp.zeros_like(db_ref)
            dgs_ref[...] = jnp.zeros_like(dgs_ref)

        v, dgelu_v = _gelu_and_grad(v_ref[...])
        r = _rms_scale(v)
        vhat = v * r
        gsv = gs_ref[...]
        vn = (vhat * gsv).astype(BF16)
        tri = _tril_mask()
        for g in range(groups):
            wg = jnp.where(tri, ws_ref[g], 0.0).astype(BF16)
            cols = slice(g * SGU_GROUP, (g + 1) * SGU_GROUP)
            for c in range(tr // CHUNK):
                rows = slice(c * CHUNK, (c + 1) * CHUNK)
                vn_cg = vn[rows, cols]
                mixed = jnp.dot(wg, vn_cg, preferred_element_type=F32) + b_ref[g]
                u, dgelu_u = _gelu_and_grad(u_ref[rows, cols])
                dso = d_ref[rows, cols]
                dz_ref[rows, cols] = (dso * mixed * dgelu_u).astype(BF16)
                dmixed = dso * u
                db_ref[g] += jnp.sum(dmixed, axis=1, keepdims=True)
                dmixed_b = dmixed.astype(BF16)
                dws_ref[g] += jnp.where(
                    tri, lax.dot_general(dmixed_b, vn_cg, (((1,), (1,)), ((), ())), preferred_element_type=F32), 0.0)
                dvn_ref[rows, cols] = lax.dot_general(wg, dmixed_b, (((0,), (0,)), ((), ())), preferred_element_type=F32)
        dvn = dvn_ref[...]
        dv, dgs_rows = _rms_bwd(vhat, r, gsv, dvn)
        dz_ref[:, sw:2 * sw] = (dv * dgelu_v).astype(BF16)
        dgs_ref[...] += jnp.sum(dgs_rows, axis=0, keepdims=True)

    return pl.pallas_call(
        body, grid=(t // tr,),
        in_specs=[_rows(tr, sw, 0), _rows(tr, sw, 1), _rows(tr, sw), _full((1, sw)), _full(ws.shape), _full(b_col.shape)],
        out_specs=[_rows(tr, 2 * sw), _full(ws.shape), _full(b_col.shape), _full((1, sw))],
        out_shape=[jax.ShapeDtypeStruct((t, 2 * sw), BF16), jax.ShapeDtypeStruct(ws.shape, F32),
                   jax.ShapeDtypeStruct(b_col.shape, F32), jax.ShapeDtypeStruct((1, sw), F32)],
        scratch_shapes=[pltpu.VMEM((tr, sw), F32)],
        compiler_params=_params(("arbitrary",)), name="sgu_bwd")(z_uv, z_uv, ds_out, gs, ws, b_col)


def _lat_bwd(z_lat, qg, kvg, dqn, dkvn, dkpe_heads, cos, sin, ql, kvl):
    t, w = z_lat.shape
    heads = dkpe_heads.shape[0]
    tr = _row_tile(t, w + heads * LANES, 3)

    def body(z_ref, qg_ref, kvg_ref, dq_ref, dkv_ref, dk_ref, cos_ref, sin_ref, dz_ref, dqg_ref, dkvg_ref):
        @pl.when(pl.program_id(0) == 0)
        def _():
            dqg_ref[...] = jnp.zeros_like(dqg_ref)
            dkvg_ref[...] = jnp.zeros_like(dkvg_ref)

        q = z_ref[:, 0:ql]
        r = _rms_scale(q)
        dx, dg_rows = _rms_bwd(q * r, r, qg_ref[...], dq_ref[...])
        dz_ref[:, 0:ql] = dx.astype(BF16)
        dqg_ref[...] += jnp.sum(dg_rows, axis=0, keepdims=True)
        kv = z_ref[:, ql:ql + kvl]
        r = _rms_scale(kv)
        dx, dg_rows = _rms_bwd(kv * r, r, kvg_ref[...], dkv_ref[...])
        dz_ref[:, ql:ql + kvl] = dx.astype(BF16)
        dkvg_ref[...] += jnp.sum(dg_rows, axis=0, keepdims=True)
        dk = dk_ref[0]
        for h in range(1, heads):
            dk = dk + dk_ref[h]
        dz_ref[:, ql + kvl:ql + kvl + LANES] = _rope_bwd(dk, cos_ref[...], sin_ref[...]).astype(BF16)

    return pl.pallas_call(
        body, grid=(t // tr,),
        in_specs=[_rows(tr, w), _full((1, ql)), _full((1, kvl)), _rows(tr, ql), _rows(tr, kvl),
                  pl.BlockSpec((heads, tr, LANES), lambda i: (0, i, 0)), _rows(tr, LANES), _rows(tr, LANES)],
        out_specs=[_rows(tr, w), _full((1, ql)), _full((1, kvl))],
        out_shape=[jax.ShapeDtypeStruct((t, w), BF16), jax.ShapeDtypeStruct((1, ql), F32),
                   jax.ShapeDtypeStruct((1, kvl), F32)],
        compiler_params=_params(("arbitrary",)), name="lat_bwd")(z_lat, qg, kvg, dqn, dkvn, dkpe_heads, cos, sin)


_NT = (((1,), (1,)), ((), ()))


def _attn_scale():
    return (QK_NOPE + QK_ROPE) ** -0.5


def _heads_per_step(heads, wanted):
    return wanted if heads % wanted == 0 else 1


def _attn_fwd(q_c, kv, kpe, comm=None):
    t = q_c.shape[0]
    heads = q_c.shape[1] // HEAD_PAD
    tq = _pick(t, ATTN_TILE)
    nq = t // tq
    scale = _attn_scale()
    to_log2 = scale * math.log2(math.e)
    tn_dims = (((0,), (0,)), ((), ()))

    hps = _heads_per_step(heads, HEADS_PER_STEP[0])

    def body(q_ref, kv_ref, kpe_ref, o_ref, ob_ref, lse_ref, m_sc, l_sc, acc_sc):
        qi, ki = pl.program_id(1), pl.program_id(2)

        @pl.when(ki == 0)
        def _():
            m_sc[...] = jnp.full_like(m_sc, NEG_BIG)
            l_sc[...] = jnp.zeros_like(l_sc)
            acc_sc[...] = jnp.zeros_like(acc_sc)

        def step(diagonal):
            for u in range(hps):
                lo = u * HEAD_PAD
                kc = jnp.concatenate([kv_ref[:, lo:lo + QK_NOPE], kpe_ref[...]], axis=1)
                st = lax.dot_general(kc, q_ref[:, lo:lo + HEAD_PAD], _NT, preferred_element_type=F32)
                if diagonal:
                    krow = lax.broadcasted_iota(jnp.int32, st.shape, 0)
                    qcol = lax.broadcasted_iota(jnp.int32, st.shape, 1)
                    st = jnp.where(qcol >= krow, st, NEG_BIG)
                m_prev = m_sc[u]
                m_new = jnp.maximum(m_prev, jnp.max(st, axis=0, keepdims=True))
                alpha = jnp.exp2((m_prev - m_new) * to_log2)
                pt = jnp.exp2((st - m_new) * to_log2)
                l_sc[u] = alpha * l_sc[u] + jnp.sum(pt, axis=0, keepdims=True)
                acc_sc[u] = alpha * acc_sc[u] + lax.dot_general(
                    kv_ref[:, lo + QK_NOPE:lo + HEAD_PAD], pt.astype(BF16), tn_dims, preferred_element_type=F32)
                m_sc[u] = m_new

        @pl.when(ki < qi)
        def _():
            step(False)

        @pl.when(ki == qi)
        def _():
            step(True)
            for u in range(hps):
                o = (acc_sc[u] / l_sc[u]).T
                o_ref[:, u * V_HEAD:(u + 1) * V_HEAD] = o
                ob_ref[:, u * V_HEAD:(u + 1) * V_HEAD] = o.astype(BF16)
                lse_ref[u] = m_sc[u] * scale + jnp.log(l_sc[u])

    omap = lambda g, qi, ki: (qi, g)
    outs, comm_outs = _call(
        body, grid=(heads // hps, nq, nq),
        in_specs=[pl.BlockSpec((tq, hps * HEAD_PAD), omap),
                  pl.BlockSpec((tq, hps * HEAD_PAD), lambda g, qi, ki: (jnp.minimum(ki, qi), g)),
                  pl.BlockSpec((tq, LANES), lambda g, qi, ki: (jnp.minimum(ki, qi), 0))],
        out_specs=[pl.BlockSpec((tq, hps * V_HEAD), omap), pl.BlockSpec((tq, hps * V_HEAD), omap),
                   pl.BlockSpec((hps, 1, tq), lambda g, qi, ki: (g, 0, qi))],
        out_shape=[jax.ShapeDtypeStruct((t, heads * V_HEAD), F32), jax.ShapeDtypeStruct((t, heads * V_HEAD), BF16),
                   jax.ShapeDtypeStruct((heads, 1, t), F32)],
        scratch_shapes=[pltpu.VMEM((hps, 1, tq), F32), pltpu.VMEM((hps, 1, tq), F32),
                        pltpu.VMEM((hps, V_HEAD, tq), F32)],
        sem=("parallel", "parallel", "arbitrary"), name="attn_fwd", args=(q_c, kv, kpe), comm=comm)
    return outs[0], outs[1], outs[2], comm_outs


def _attn_bwd(q_c, kv, kpe, o, do, lse_row, cos, sin, comm=None):
    t = q_c.shape[0]
    heads = q_c.shape[1] // HEAD_PAD
    tk = _pick(t, ATTN_TILE)
    nk = t // tk
    scale = _attn_scale()
    tn_dims = (((0,), (0,)), ((), ()))

    hps = _heads_per_step(heads, HEADS_PER_STEP[1])

    def body(q_ref, kv_ref, kpe_ref, do_ref, lse_ref, o_ref, cos_ref, sin_ref, dq_ref, dkv_ref, dkpe_ref,
             dk_sc, dv_sc, delta_sc, dq_sc):
        ki, qi = pl.program_id(1), pl.program_id(2)

        @pl.when(jnp.logical_and(ki == 0, qi == 0))
        def _():
            dq_sc[...] = jnp.zeros_like(dq_sc)

        @pl.when(qi == 0)
        def _():
            dk_sc[...] = jnp.zeros_like(dk_sc)
            dv_sc[...] = jnp.zeros_like(dv_sc)

        @pl.when(ki == 0)
        def _():
            for u in range(hps):
                cols = slice(u * V_HEAD, (u + 1) * V_HEAD)
                delta_sc[qi * hps + u] = jnp.sum((do_ref[:, cols] * o_ref[:, cols]).T, axis=0, keepdims=True)

        def step(diagonal):
            for u in range(hps):
                lo = u * HEAD_PAD
                kc = jnp.concatenate([kv_ref[:, lo:lo + QK_NOPE], kpe_ref[...]], axis=1)
                q = q_ref[:, lo:lo + HEAD_PAD]
                st = lax.dot_general(kc, q, _NT, preferred_element_type=F32) * scale
                pt = jnp.exp(st - lse_ref[u])
                if diagonal:
                    krow = lax.broadcasted_iota(jnp.int32, st.shape, 0)
                    qcol = lax.broadcasted_iota(jnp.int32, st.shape, 1)
                    pt = jnp.where(qcol >= krow, pt, 0.0)
                do_b = do_ref[:, u * V_HEAD:(u + 1) * V_HEAD].astype(BF16)
                dv_sc[u] += jnp.dot(pt.astype(BF16), do_b, preferred_element_type=F32)
                dpt = lax.dot_general(kv_ref[:, lo + QK_NOPE:lo + HEAD_PAD], do_b, _NT, preferred_element_type=F32)
                dst = (pt * (dpt - delta_sc[qi * hps + u]) * scale).astype(BF16)
                dk_sc[u] += jnp.dot(dst, q, preferred_element_type=F32)
                rows = pl.ds(pl.multiple_of(qi * tk, tk), tk)
                dq_sc[rows, lo:lo + HEAD_PAD] += lax.dot_general(dst, kc, tn_dims, preferred_element_type=F32)

        @pl.when(qi > ki)
        def _():
            step(False)

        @pl.when(qi == ki)
        def _():
            step(True)

        @pl.when(qi == nk - 1)
        def _():
            for u in range(hps):
                lo = u * HEAD_PAD
                dkv_ref[:, lo:lo + QK_NOPE] = dk_sc[u, :, 0:QK_NOPE].astype(BF16)
                dkv_ref[:, lo + QK_NOPE:lo + HEAD_PAD] = dv_sc[u].astype(BF16)
                dkpe_ref[u] = dk_sc[u, :, QK_NOPE:QK_NOPE + LANES]

        @pl.when(jnp.logical_and(ki == nk - 1, qi == nk - 1))
        def _():
            cos, sin = cos_ref[...], sin_ref[...]
            for u in range(hps):
                lo = u * HEAD_PAD
                dq_ref[:, lo:lo + QK_NOPE] = dq_sc[:, lo:lo + QK_NOPE].astype(BF16)
                dq_ref[:, lo + QK_NOPE:lo + HEAD_PAD] = _rope_bwd(dq_sc[:, lo + QK_NOPE:lo + HEAD_PAD], cos, sin).astype(BF16)

    qclamp = lambda g, ki, qi: (jnp.maximum(qi, ki), g)
    outs, comm_outs = _call(
        body, grid=(heads // hps, nk, nk),
        in_specs=[pl.BlockSpec((tk, hps * HEAD_PAD), qclamp),
                  pl.BlockSpec((tk, hps * HEAD_PAD), lambda g, ki, qi: (ki, g)),
                  pl.BlockSpec((tk, LANES), lambda g, ki, qi: (ki, 0)),
                  pl.BlockSpec((tk, hps * V_HEAD), qclamp),
                  pl.BlockSpec((hps, 1, tk), lambda g, ki, qi: (g, 0, jnp.maximum(qi, ki))),
                  pl.BlockSpec((tk, hps * V_HEAD), lambda g, ki, qi: (jnp.where(ki == 0, qi, 0), g)),
                  _full((t, LANES)), _full((t, LANES))],
        out_specs=[pl.BlockSpec((t, hps * HEAD_PAD), lambda g, ki, qi: (0, g)),
                   pl.BlockSpec((tk, hps * HEAD_PAD), lambda g, ki, qi: (ki, g)),
                   pl.BlockSpec((hps, tk, LANES), lambda g, ki, qi: (g, ki, 0))],
        out_shape=[jax.ShapeDtypeStruct((t, heads * HEAD_PAD), BF16),
                   jax.ShapeDtypeStruct((t, heads * HEAD_PAD), BF16), jax.ShapeDtypeStruct((heads, t, LANES), F32)],
        scratch_shapes=[pltpu.VMEM((hps, tk, HEAD_PAD), F32), pltpu.VMEM((hps, tk, V_HEAD), F32),
                        pltpu.VMEM((nk * hps, 1, tk), F32), pltpu.VMEM((t, hps * HEAD_PAD), F32)],
        sem=("parallel", "arbitrary", "arbitrary"), name="attn_bwd",
        args=(q_c, kv, kpe, do, lse_row, o, cos, sin), comm=comm)
    return outs[0], outs[1], outs[2], comm_outs


def _local_step(x, pos_col, target, small, shards, opt):
    t = x.shape[0]
    ql, kvl = small["q_norm_g"].shape[1], small["kv_norm_g"].shape[1]
    sw = small["sgu_norm_g"].shape[1]
    heads = (shards["w_uq"].shape[1] * N_DEV) // (QK_NOPE + QK_ROPE)
    big = {}
    early = ["w_in", "w_uq", "w_ukv"]
    big.update(_compute_layout(dict(zip(early, _all_gather([shards[k] for k in early]))), ql, kvl, heads, sw))
    half = QK_ROPE // 2
    lane = jnp.arange(LANES)
    inv_freq = ROPE_THETA ** (-jnp.arange(0, QK_ROPE, 2, dtype=F32) / QK_ROPE)
    inv_row = inv_freq[lane % half][None, :]
    sign_row = jnp.where((lane % QK_ROPE) < half, -1.0, 1.0).astype(F32)[None, :]
    cos, sin = _rope_tables(pos_col, inv_row, sign_row)
    ws = small["w_sgu"]
    b_col = small["b_sgu_col"]

    def arrived(names, bufs):
        big.update(_compute_layout(dict(zip(names, bufs)), ql, kvl, heads, sw))

    a = _norm_fwd(x, small["norm_mix_g"], "norm_mix_fwd")
    z_lat = _mm(a, big["w_lat_t"], tb=True, name="z_lat")
    z_uv, g_sgu = _mm(a, big["w_uv_t"], tb=True, name="z_uv", comm=_gather_stage(1, [shards["w_o_sgu"]]))
    z_g, (g_attn, g_sgu) = _mm(a, big["w_g_t"], tb=True, name="z_g",
                               comm=_join(_gather_stage(1, [shards["w_o_attn"]]), _gather_stage(2, g_sgu)))
    qn, kvn, kpe = _lat_fwd(z_lat, small["q_norm_g"], small["kv_norm_g"], cos, sin, ql, kvl)
    q_c, (g_attn, g_sgu) = _mm(qn, big["w_uq"], name="q_up_rope", rope=(cos, sin),
                               comm=_join(_gather_stage(2, [g_attn]), _gather_stage(3, [g_sgu])))
    kv, (g_attn, g_out) = _mm(kvn, big["w_ukv"], out_dtype=BF16, name="kv_up",
                              comm=_join(_gather_stage(3, [g_attn]), _gather_stage(1, [shards["w_out"]])))
    arrived(["w_o_sgu", "w_o_attn"], [g_sgu, g_attn])
    attn, attn_b, lse, (w_gate, w_up) = _attn_fwd(
        q_c, kv, kpe, comm=_gather_stage(1, [shards["w_gate_ffn"], shards["w_up_ffn"]]))
    s_out = _sgu_fwd(z_uv, small["sgu_norm_g"], ws, b_col)
    y_sgu, (g_out,) = _mm(s_out, big["w_o_sgu"], name="y_sgu", comm=_gather_stage(2, [g_out]))
    y_attn, (w_gate, g_out) = _mm(attn_b, big["w_o_attn"], name="y_attn",
                                  comm=_join(_gather_stage(2, [w_gate]), _gather_stage(3, [g_out])))
    arrived(["w_out"], [g_out])
    merged, (w_up, w_gate) = _merge_fwd(y_attn, y_sgu, z_g, small["b_gate"],
                                        comm=_join(_gather_stage(2, [w_up]), _gather_stage(3, [w_gate])))
    h1, (w_up,) = _mm(merged, big["w_out"], add=x, name="h1", comm=_gather_stage(3, [w_up]))
    f = _norm_fwd(h1, small["norm_ffn_g"], "norm_ffn_fwd")
    gate, w_down = _mm(f, w_gate, tb=True, slab="n", name="ffn_gate", comm=_gather_stage(1, [shards["w_down_ffn"]]))
    up, w_down = _mm(f, w_up, tb=True, slab="n", name="ffn_up", comm=_gather_stage(2, w_down))
    ffn = gate.shape[2]
    gate, up = gate.reshape(N_DEV * t, ffn), up.reshape(N_DEV * t, ffn)
    act, (w_down,) = _swiglu_fwd(gate, up, comm=_gather_stage(3, w_down))
    act = act.reshape(N_DEV, t, ffn)
    h2 = _mm(act, w_down, slab="k", add=h1, name="h2")
    loss_row, dh2, dh2_b, d_norm_final = _loss_head(h2, small["norm_final_g"], target)

    def pair_sums(names, slabs, bufs):
        return [_pair_sum(g, b, "pair_sum_" + k) for k, g, b in zip(names, slabs, bufs)]

    parts, updates = {}, {}

    def update(names, label, comm=None):
        res, got = _adamw_shards([parts[k] for k in names], [opt[k] for k in names], "adamw_" + label, comm=comm)
        updates.update(zip(names, res))
        return got

    down_slabs = [_mm(act, dh2_b, ta=True, slab="m", out_dtype=BF16, name="dw_down")]
    dgu, bufs = _mm(dh2_b, w_down, tb=True, slab="n", tm=MM_TILE[0] // 2, name="dact_swiglu_bwd",
                    comm=_to_sibling(down_slabs), swiglu=(gate.reshape(N_DEV, t, ffn), up.reshape(N_DEV, t, ffn)))
    dgu = dgu.reshape(2 * N_DEV, t, ffn)
    down_pair = pair_sums(["w_down_ffn"], down_slabs, bufs)
    dw_gu, got = _mm(dgu, f, ta=True, slab="m", out_dtype=BF16, name="dw_gate_up", comm=_to_chips(down_pair))
    parts["w_down_ffn"] = got[0]
    gu_names = ["w_gate_ffn", "w_up_ffn"]
    df, bufs = _mm(dgu, w_gate, slab="k", name="df_gate", comm=_to_sibling([dw_gu, dw_gu], first=[0, N_DEV]))
    gu_pairs = [_pair_sum(dw_gu, b, "pair_sum_" + k, first=s0) for k, b, s0 in zip(gu_names, bufs, [0, N_DEV])]
    half = _pick(gu_pairs[1].shape[1], gu_pairs[1].shape[1] // 2, 2 * SUBLANES)
    df, up_parts = _mm(dgu, w_up, slab="k", a_slab0=N_DEV, add=df, name="df_up",
                       comm=_to_chips(gu_pairs[1:], rows=[("r", 0, half)]))
    dh1, dh1_b, d_norm_ffn = _norm_bwd(h1, small["norm_ffn_g"], df, dh2, "norm_ffn_bwd")
    dw_out = _mm(merged, dh1_b, ta=True, out_dtype=BF16, name="dw_out")
    out_slabs = [_slabs_from_rows(dw_out)]
    dmerged, bufs = _mm(dh1_b, big["w_out"], tb=True, name="dmerged", comm=_to_sibling(out_slabs))
    out_pair = pair_sums(["w_out"], out_slabs, bufs)
    dy_attn, dy_sgu, dz_g, d_b_gate = _merge_bwd(dmerged, y_attn, y_sgu, z_g, small["b_gate"])
    dw_o_sgu = _mm(s_out, dy_sgu, ta=True, out_dtype=BF16, name="dw_o_sgu")
    ds_out = _mm(dy_sgu, big["w_o_sgu"], tb=True, name="ds_out")
    dz_uv, d_ws, d_b_col, d_sgu_norm = _sgu_bwd(z_uv, ds_out, small["sgu_norm_g"], ws, b_col)
    dw_o_attn = _mm(attn_b, dy_attn, ta=True, out_dtype=BF16, name="dw_o_attn")
    mix_names = ["w_o_sgu", "w_o_attn"]
    mix_slabs = [_slabs_from_cols(dw_o_sgu), _slabs_from_rows(dw_o_attn)]
    dattn, bufs = _mm(dy_attn, big["w_o_attn"], tb=True, name="dattn", comm=_to_sibling(mix_slabs))
    mix_pairs = pair_sums(mix_names, mix_slabs, bufs)
    rest = ("r", half, gu_pairs[1].shape[1] - half)
    dq_p, dkv, dkpe_heads, got = _attn_bwd(
        q_c, kv, kpe, attn, dattn, lse, cos, sin,
        comm=_join(_to_chips(gu_pairs[:1]), _to_chips(gu_pairs[1:], rows=[rest], into=up_parts)))
    parts.update(zip(gu_names, got))
    dw_uq = _mm(qn, dq_p, ta=True, out_dtype=BF16, name="dw_uq")
    dw_ukv = _mm(kvn, dkv, ta=True, out_dtype=BF16, name="dw_ukv")
    dqn = _mm(dq_p, big["w_uq"], tb=True, name="dqn")
    dkvn = _mm(dkv, big["w_ukv"], tb=True, name="dkvn")
    dz_lat, d_q_norm, d_kv_norm = _lat_bwd(z_lat, small["q_norm_g"], small["kv_norm_g"], dqn, dkvn, dkpe_heads,
                                           cos, sin, ql, kvl)
    dw_g, got = _mm(dz_g, a, ta=True, out_dtype=BF16, name="dw_g", comm=_to_chips(out_pair))
    parts["w_out"] = got[0]
    dw_uv, got = _mm(dz_uv, a, ta=True, out_dtype=BF16, name="dw_uv", comm=_to_chips(mix_pairs[1:]))
    parts["w_o_attn"] = got[0]
    dw_lat, got = _mm(dz_lat, a, ta=True, out_dtype=BF16, name="dw_lat", comm=_to_chips(mix_pairs[:1]))
    parts["w_o_sgu"] = got[0]
    lat = ql + kvl + QK_ROPE
    dw_uq_cols = dw_uq.reshape(ql, heads, HEAD_PAD)[:, :, :QK_NOPE + QK_ROPE].reshape(ql, heads * (QK_NOPE + QK_ROPE))
    in_names = ["w_uq", "w_ukv", "w_in"]
    in_slabs = [_slabs_from_cols(dw_uq_cols), _slabs_from_cols(dw_ukv),
                _slabs_from_rows(jnp.concatenate([dw_lat[:lat], dw_uv, dw_g], axis=0))]
    da = _mm(dz_lat, big["w_lat_t"], name="da_lat")
    da, bufs = _mm(dz_uv, big["w_uv_t"], add=da, name="da_uv", comm=_to_sibling(in_slabs))
    uq_pair, ukv_pair, in_pair = pair_sums(in_names, in_slabs, bufs)
    cols = in_pair.shape[2]
    first = _pick(cols, cols // TAIL_SPLIT)
    da, got = _mm(dz_g, big["w_g_t"], add=da, name="da_g",
                  comm=_to_chips([uq_pair, in_pair], rows=[None, ("c", 0, first)]))
    parts["w_uq"], in_parts = got
    grad_x, _, d_norm_mix, got = _norm_bwd(x, small["norm_mix_g"], da, dh1, "norm_mix_bwd", comm=_to_chips([ukv_pair]))
    parts["w_ukv"] = got[0]
    rest = _to_chips([in_pair], rows=[("c", first, cols - first)], into=[in_parts]) if first < cols else None
    got = update(["w_gate_ffn", "w_up_ffn", "w_down_ffn"], "ffn", comm=rest)
    parts["w_in"] = got[0] if rest is not None else in_parts
    update(["w_out", "w_o_attn"], "mixer_out")
    for k in ("w_o_sgu", "w_uq", "w_ukv", "w_in"):
        update([k], k)

    gs = {"norm_mix_g": d_norm_mix, "b_gate": d_b_gate, "q_norm_g": d_q_norm, "kv_norm_g": d_kv_norm,
          "sgu_norm_g": d_sgu_norm, "w_sgu": d_ws, "b_sgu_col": d_b_col, "norm_ffn_g": d_norm_ffn,
          "norm_final_g": d_norm_final}
    return loss_row, grad_x, gs, updates


def _my_place():
    return lax.axis_index("x"), lax.axis_index("y"), lax.axis_index("c")


N_CHIPS = N_DEV // 2

_GATHER_SEMS = [[(3,), (3,), ()], [(4,), (4,)], [(1,), (1,)]]


def _halves(shape):
    r, c = shape
    if (c // 2) % LANES == 0:
        return ("c", 0, c // 2), ("c", c // 2, c // 2)
    assert (r // 2) % (2 * SUBLANES) == 0, shape
    return ("r", 0, r // 2), ("r", r // 2, r // 2)


def _gather_copies(stage, ins, outs, sems):
    x, y, c = _my_place()
    me, x_nbr, y_nbr, diag = 4 * x + 2 * y + c, 4 * (1 - x) + 2 * y + c, 4 * x + 2 * (1 - y) + c, 4 * (1 - x) + 2 * (1 - y) + c
    sibling = (x, y, 1 - c)

    def remote(w, k, src, dst, to):
        return pltpu.make_async_remote_copy(src_ref=src, dst_ref=dst, send_sem=sems[0].at[w, k], recv_sem=sems[1].at[w, k],
                                            device_id=to, device_id_type=MESH)

    out = []
    for w in range(len(outs)):
        if stage == 1:
            dst = outs[w].at[me]
            out.append(pltpu.make_async_copy(ins[w], dst, sems[2].at[w]))
            out += [remote(w, k, ins[w], dst, to) for k, to in enumerate([sibling, (1 - x, y, c), (x, 1 - y, c)])]
        elif stage == 2:
            first, second = _halves(outs[w].shape[1:])
            out.append(remote(w, 0, _window(ins[w], x_nbr, first), _window(outs[w], x_nbr, first), (x, 1 - y, c)))
            out.append(remote(w, 1, _window(ins[w], y_nbr, second), _window(outs[w], y_nbr, second), (1 - x, y, c)))
            out.append(remote(w, 2, ins[w].at[x_nbr], outs[w].at[x_nbr], sibling))
            out.append(remote(w, 3, ins[w].at[y_nbr], outs[w].at[y_nbr], sibling))
        else:
            out.append(remote(w, 0, ins[w].at[diag], outs[w].at[diag], sibling))
    return out


def _gather_stage(stage, arrays):
    n = len(arrays)

    def start(ins, outs, sems):
        for cp in _gather_copies(stage, ins, outs, sems):
            cp.start()

    def finish(ins, outs, sems):
        for cp in _gather_copies(stage, ins, outs, sems):
            cp.wait()

    shapes = [jax.ShapeDtypeStruct(((N_DEV,) + a.shape) if stage == 1 else a.shape, a.dtype) for a in arrays]
    return _Comm(arrays, shapes, [pltpu.SemaphoreType.DMA((n,) + s) for s in _GATHER_SEMS[stage - 1]], start, finish,
                 aliases=None if stage == 1 else {w: w for w in range(n)})


def _join(*comms):
    ins, shapes, sems, aliases, spans = [], [], [], {}, []
    for cm in comms:
        spans.append((len(ins), len(ins) + len(cm.ins), len(shapes), len(shapes) + len(cm.out_shapes),
                      len(sems), len(sems) + len(cm.sems)))
        aliases.update({len(ins) + i: len(shapes) + o for i, o in cm.aliases.items()})
        ins, shapes, sems = ins + cm.ins, shapes + cm.out_shapes, sems + cm.sems

    def each(half):
        def run(i_refs, o_refs, s_refs):
            for cm, (i0, i1, o0, o1, s0, s1) in zip(comms, spans):
                getattr(cm, half)(i_refs[i0:i1], o_refs[o0:o1], s_refs[s0:s1])
        return run

    return _Comm(ins, shapes, sems, each("start"), each("finish"), aliases)


def _all_gather(shards):
    n = len(shards)
    n_sems = [len(s) for s in _GATHER_SEMS]

    def body(*refs):
        ins, outs, sems = refs[:n], refs[n:2 * n], refs[2 * n:]
        s0 = 0
        for stage in (1, 2, 3):
            mine = sems[s0:s0 + n_sems[stage - 1]]
            s0 += n_sems[stage - 1]
            copies = _gather_copies(stage, ins if stage == 1 else outs, outs, mine)
            for cp in copies:
                cp.start()
            for cp in copies:
                cp.wait()

    any_spec = pl.BlockSpec(memory_space=pl.ANY)
    return pl.pallas_call(
        body, in_specs=[any_spec] * n, out_specs=[any_spec] * n,
        out_shape=[jax.ShapeDtypeStruct((N_DEV,) + s.shape, s.dtype) for s in shards],
        scratch_shapes=[pltpu.SemaphoreType.DMA((n,) + s) for stage in _GATHER_SEMS for s in stage],
        compiler_params=pltpu.CompilerParams(has_side_effects=True), name="all_gather_weights")(*shards)


def _to_sibling(grads, first=None):
    n = len(grads)
    first = first or [0] * n

    def copies(ins, outs, sems):
        x, y, c = _my_place()
        send_sems, recv_sems = sems
        return [pltpu.make_async_remote_copy(
            src_ref=ins[w].at[first[w] + 2 * i + (1 - c)], dst_ref=outs[w].at[i], send_sem=send_sems.at[w, i],
            recv_sem=recv_sems.at[w, i], device_id=(x, y, 1 - c), device_id_type=MESH)
            for w in range(n) for i in range(N_CHIPS)]

    def start(ins, outs, sems):
        for cp in copies(ins, outs, sems):
            cp.start()

    def finish(ins, outs, sems):
        for cp in copies(ins, outs, sems):
            cp.wait()

    return _Comm(grads, [jax.ShapeDtypeStruct((N_CHIPS,) + g.shape[1:], g.dtype) for g in grads],
                 [pltpu.SemaphoreType.DMA((n, N_CHIPS)), pltpu.SemaphoreType.DMA((n, N_CHIPS))], start, finish)


def _window(ref, slab, win):
    if win is None:
        return ref.at[slab]
    if win[0] == "r":
        return ref.at[slab, pl.ds(win[1], win[2])]
    return ref.at[slab, slice(None), pl.ds(win[1], win[2])]


def _to_chips(parts, rows=None, into=None):
    n = len(parts)
    rows = rows or [None] * n

    def copies(ins, outs, sems):
        x, y, c = _my_place()
        send_sems, recv_sems, local_sems = sems
        mine = 2 * x + y
        chips = [(1 - x, y), (x, 1 - y), (1 - x, 1 - y)]
        remote = [pltpu.make_async_remote_copy(
            src_ref=_window(ins[w], 2 * cx + cy, rows[w]), dst_ref=_window(outs[w], mine, rows[w]),
            send_sem=send_sems.at[w, j], recv_sem=recv_sems.at[w, j], device_id=(cx, cy, c), device_id_type=MESH)
            for w in range(n) for j, (cx, cy) in enumerate(chips)]
        local = [pltpu.make_async_copy(_window(ins[w], mine, rows[w]), _window(outs[w], mine, rows[w]),
                                       local_sems.at[w]) for w in range(n)]
        return remote + local

    def start(ins, outs, sems):
        for cp in copies(ins, outs, sems):
            cp.start()

    def finish(ins, outs, sems):
        for cp in copies(ins, outs, sems):
            cp.wait()

    return _Comm(list(parts) + list(into or []), [jax.ShapeDtypeStruct(p.shape, p.dtype) for p in parts],
                 [pltpu.SemaphoreType.DMA((n, N_CHIPS - 1)), pltpu.SemaphoreType.DMA((n, N_CHIPS - 1)),
                  pltpu.SemaphoreType.DMA((n,))], start, finish,
                 aliases={n + w: w for w in range(n)} if into else None)


def _pair_sum(g, buf, name, first=0):
    _, r, c = g.shape
    tr, tc = _shard_tile(r, c, 4 * SHARD_TILE_ELEMS, 1024)
    core = (lax.axis_index("c") + first).astype(jnp.int32).reshape(1)

    def body(core_ref, g_ref, b_ref, o_ref):
        o_ref[...] = (g_ref[...].astype(F32) + b_ref[...].astype(F32)).astype(o_ref.dtype)

    blk = (1, tr, tc)
    return pl.pallas_call(
        body, grid_spec=pltpu.PrefetchScalarGridSpec(
            num_scalar_prefetch=1, grid=(N_CHIPS, r // tr, c // tc),
            in_specs=[pl.BlockSpec(blk, lambda i, j, l, core_ref: (2 * i + core_ref[0], j, l)),
                      pl.BlockSpec(blk, lambda i, j, l, core_ref: (i, j, l))],
            out_specs=pl.BlockSpec(blk, lambda i, j, l, core_ref: (i, j, l))),
        out_shape=jax.ShapeDtypeStruct(buf.shape, buf.dtype),
        compiler_params=_params(("parallel", "parallel", "parallel")), name=name)(core, g, buf)


def _all_reduce_pack(pack):
    r = pack.shape[0]

    def body(x_ref, out_ref, gath_ref, send_sems, recv_sems, local_sem):
        x, y, c = _my_place()
        me, sibling = (x, y, c), (x, y, 1 - c)
        chips = [(1 - x, y), (x, 1 - y), (1 - x, 1 - y)]

        def slab(place):
            return gath_ref.at[4 * place[0] + 2 * place[1] + place[2]]

        def copy(k, place, to, src=None):
            return pltpu.make_async_remote_copy(
                src_ref=slab(place) if src is None else src, dst_ref=slab(place),
                send_sem=send_sems.at[k], recv_sem=recv_sems.at[k], device_id=to, device_id_type=MESH)

        mine = pltpu.make_async_copy(x_ref, slab(me), local_sem)
        mine.start()
        first = [copy(0, me, sibling, src=x_ref)]
        first += [copy(1 + j, me, (*chip, c), src=x_ref) for j, chip in enumerate(chips)]
        for cp in first:
            cp.start()
        passed = [copy(4 + j, (*chip, c), sibling) for j, chip in enumerate(chips)]
        for j, chip in enumerate(chips):
            copy(1 + j, (*chip, c), me).wait_recv()
            passed[j].start()
        copy(0, sibling, me).wait_recv()
        for j, chip in enumerate(chips):
            copy(4 + j, (*chip, 1 - c), me).wait_recv()
        for cp in first + passed:
            cp.wait_send()
        mine.wait()
        acc = gath_ref[0]
        for i in range(1, N_DEV):
            acc = acc + gath_ref[i]
        out_ref[...] = acc

    vmem = pl.BlockSpec(memory_space=pltpu.VMEM)
    return pl.pallas_call(
        body, in_specs=[vmem], out_specs=vmem, out_shape=jax.ShapeDtypeStruct(pack.shape, F32),
        scratch_shapes=[pltpu.VMEM((N_DEV, r, LANES), F32), pltpu.SemaphoreType.DMA((7,)),
                        pltpu.SemaphoreType.DMA((7,)), pltpu.SemaphoreType.DMA],
        compiler_params=pltpu.CompilerParams(vmem_limit_bytes=VMEM_LIMIT), name="all_reduce_small")(pack)


def _adamw_math(w, g, m, v):
    m = ADAM_B1 * m + (1.0 - ADAM_B1) * g
    v = ADAM_B2 * v + (1.0 - ADAM_B2) * (g * g)
    m_hat = m / (1.0 - ADAM_B1 ** ADAM_STEP)
    v_hat = v / (1.0 - ADAM_B2 ** ADAM_STEP)
    delta = -ADAM_LR * (m_hat / (jnp.sqrt(v_hat) + ADAM_EPS) + ADAM_WD * w)
    return delta, m, v


def _adamw_shards(parts, opts, name, comm=None):
    r, c = opts[0][0].shape
    n_parts, k = parts[0].shape[0], len(parts)
    tr, tc = _shard_tile(r, c, SHARD_TILE_ELEMS // k)

    def body(*refs):
        ins, outs = refs[:4 * k], refs[4 * k:]
        for s in range(k):
            p_ref, w_ref, m_ref, v_ref = ins[4 * s:4 * s + 4]
            g_ref, d_ref, nm_ref, nv_ref = outs[4 * s:4 * s + 4]
            g = p_ref[0].astype(F32)
            for i in range(1, n_parts):
                g = g + p_ref[i].astype(F32)
            g_ref[...] = g
            d_ref[...], nm_ref[...], nv_ref[...] = _adamw_math(w_ref[...], g, m_ref[...], v_ref[...])

    spec = pl.BlockSpec((tr, tc), lambda i, j: (i, j))
    args = [a for p, o in zip(parts, opts) for a in (p,) + tuple(o)]
    outs, comm_outs = _call(
        body, grid=(r // tr, c // tc),
        in_specs=[pl.BlockSpec((n_parts, tr, tc), lambda i, j: (0, i, j)), spec, spec, spec] * k,
        out_specs=[spec] * (4 * k), out_shape=[jax.ShapeDtypeStruct((r, c), F32)] * (4 * k),
        sem=("parallel", "parallel"), name=name, args=args, comm=comm)
    return [outs[4 * s:4 * s + 4] for s in range(k)], comm_outs


def _adamw_pack(g, w, m, v):
    r, c = w.shape

    def body(g_ref, w_ref, m_ref, v_ref, d_ref, nm_ref, nv_ref):
        d_ref[...], nm_ref[...], nv_ref[...] = _adamw_math(w_ref[...], g_ref[...], m_ref[...], v_ref[...])

    return pl.pallas_call(
        body, in_specs=[_full((r, c))] * 4, out_specs=[_full((r, c))] * 3, grid=(1,),
        out_shape=[jax.ShapeDtypeStruct((r, c), F32)] * 3,
        compiler_params=_params(("arbitrary",)), name="adamw_small")(g, w, m, v)


def _cols_from_slabs(g):
    return jnp.transpose(g, (1, 0, 2)).reshape(g.shape[1], N_DEV * g.shape[2])


def _slabs_from_cols(w):
    r, c8 = w.shape
    return jnp.transpose(w.reshape(r, N_DEV, c8 // N_DEV), (1, 0, 2))


def _rows_from_slabs(g):
    return g.reshape(N_DEV * g.shape[1], g.shape[2])


def _slabs_from_rows(w):
    return w.reshape(N_DEV, w.shape[0] // N_DEV, w.shape[1])


def _compute_layout(gathered, ql, kvl, heads, sw):
    out = {}
    for k, g in gathered.items():
        if k == "w_in":
            lat = ql + kvl + QK_ROPE
            w_in_t = _rows_from_slabs(g)
            out["w_lat_t"] = jnp.pad(w_in_t[:lat], ((0, LANES - QK_ROPE), (0, 0)))
            out["w_uv_t"] = w_in_t[lat:lat + 2 * sw]
            out["w_g_t"] = w_in_t[lat + 2 * sw:]
        elif k == "w_uq":
            per_head = _cols_from_slabs(g).reshape(ql, heads, QK_NOPE + QK_ROPE)
            pad = HEAD_PAD - QK_NOPE - QK_ROPE
            out["w_uq"] = jnp.pad(per_head, ((0, 0), (0, 0), (0, pad))).reshape(ql, heads * HEAD_PAD)
        elif k in ("w_o_attn", "w_out", "w_down_ffn"):
            out[k.removesuffix("_ffn")] = _rows_from_slabs(g)
        else:
            out[k.removesuffix("_ffn")] = _cols_from_slabs(g)
    return out


_SMALL =["norm_mix_g", "b_gate", "q_norm_g", "kv_norm_g", "sgu_norm_g", "w_sgu", "b_sgu", "norm_ffn_g", "norm_final_g"]
_BIG = ["w_in", "w_uq", "w_ukv", "w_o_attn", "w_o_sgu", "w_out", "w_gate_ffn", "w_up_ffn", "w_down_ffn"]
_TRANSPOSED = ("w_in", "w_gate_ffn", "w_up_ffn")
_ORDER = ["norm_mix_g", "w_in", "b_gate", "q_norm_g", "w_uq", "kv_norm_g", "w_ukv", "w_o_attn", "sgu_norm_g", "w_sgu",
          "b_sgu", "w_o_sgu", "w_out", "norm_ffn_g", "w_gate_ffn", "w_up_ffn", "w_down_ffn", "norm_final_g"]


def _pack_rows(parts):
    rows, sizes = [], []
    for p in parts:
        flat = p.reshape(-1)
        n = flat.shape[0]
        padded = -(-n // (SUBLANES * LANES)) * (SUBLANES * LANES)
        rows.append(jnp.pad(flat, (0, padded - n)).reshape(padded // LANES, LANES))
        sizes.append((n, padded // LANES))
    return jnp.concatenate(rows, axis=0), sizes


def _unpack_rows(pack, sizes, shapes):
    out, r0 = [], 0
    for (n, nr), shp in zip(sizes, shapes):
        out.append(pack[r0:r0 + nr].reshape(-1)[:n].reshape(shp))
        r0 += nr
    return out


def kernel(x, positions, norm_mix_g, w_in, b_gate, q_norm_g, w_uq, kv_norm_g, w_ukv, w_o_attn, sgu_norm_g, w_sgu, b_sgu, w_o_sgu, w_out, norm_ffn_g, w_gate_ffn, w_up_ffn, w_down_ffn, norm_final_g, loss_target, m_norm_mix_g, m_w_in, m_b_gate, m_q_norm_g, m_w_uq, m_kv_norm_g, m_w_ukv, m_w_o_attn, m_sgu_norm_g, m_w_sgu, m_b_sgu, m_w_o_sgu, m_w_out, m_norm_ffn_g, m_w_gate_ffn, m_w_up_ffn, m_w_down_ffn, m_norm_final_g, v_norm_mix_g, v_w_in, v_b_gate, v_q_norm_g, v_w_uq, v_kv_norm_g, v_w_ukv, v_w_o_attn, v_sgu_norm_g, v_w_sgu, v_b_sgu, v_w_o_sgu, v_w_out, v_norm_ffn_g, v_w_gate_ffn, v_w_up_ffn, v_w_down_ffn, v_norm_final_g):
    wts = dict(norm_mix_g=norm_mix_g, w_in=w_in, b_gate=b_gate, q_norm_g=q_norm_g, w_uq=w_uq, kv_norm_g=kv_norm_g,
               w_ukv=w_ukv, w_o_attn=w_o_attn, sgu_norm_g=sgu_norm_g, w_sgu=w_sgu, b_sgu=b_sgu, w_o_sgu=w_o_sgu,
               w_out=w_out, norm_ffn_g=norm_ffn_g, w_gate_ffn=w_gate_ffn, w_up_ffn=w_up_ffn, w_down_ffn=w_down_ffn,
               norm_final_g=norm_final_g)
    mom = dict(norm_mix_g=m_norm_mix_g, w_in=m_w_in, b_gate=m_b_gate, q_norm_g=m_q_norm_g, w_uq=m_w_uq,
               kv_norm_g=m_kv_norm_g, w_ukv=m_w_ukv, w_o_attn=m_w_o_attn, sgu_norm_g=m_sgu_norm_g, w_sgu=m_w_sgu,
               b_sgu=m_b_sgu, w_o_sgu=m_w_o_sgu, w_out=m_w_out, norm_ffn_g=m_norm_ffn_g, w_gate_ffn=m_w_gate_ffn,
               w_up_ffn=m_w_up_ffn, w_down_ffn=m_w_down_ffn, norm_final_g=m_norm_final_g)
    var = dict(norm_mix_g=v_norm_mix_g, w_in=v_w_in, b_gate=v_b_gate, q_norm_g=v_q_norm_g, w_uq=v_w_uq,
               kv_norm_g=v_kv_norm_g, w_ukv=v_w_ukv, w_o_attn=v_w_o_attn, sgu_norm_g=v_sgu_norm_g, w_sgu=v_w_sgu,
               b_sgu=v_b_sgu, w_o_sgu=v_w_o_sgu, w_out=v_w_out, norm_ffn_g=v_norm_ffn_g, w_gate_ffn=v_w_gate_ffn,
               w_up_ffn=v_w_up_ffn, w_down_ffn=v_w_down_ffn, norm_final_g=v_norm_final_g)

    t, d = x.shape[1], x.shape[2]
    ql, kvl = q_norm_g.shape[1], kv_norm_g.shape[1]
    heads = (w_uq.shape[2] * N_DEV) // (QK_NOPE + QK_ROPE)
    sw = sgu_norm_g.shape[1]

    def shard(a, k):
        return a[0].T if k in _TRANSPOSED else a[0]

    def unshard(a, k):
        return (a.T if k in _TRANSPOSED else a).reshape(wts[k].shape)

    opt = {k: (shard(wts[k], k), shard(mom[k], k), shard(var[k], k)) for k in _BIG}
    shards = {k: opt[k][0].astype(BF16) for k in _BIG}
    small = {
        "norm_mix_g": norm_mix_g, "b_gate": b_gate, "q_norm_g": q_norm_g, "kv_norm_g": kv_norm_g,
        "sgu_norm_g": sgu_norm_g, "w_sgu": w_sgu[0], "b_sgu_col": b_sgu[0][:, :, None], "norm_ffn_g": norm_ffn_g,
        "norm_final_g": norm_final_g[None, :],
    }

    loss_row, grad_x, gs, updates = _local_step(x[0], positions.reshape(t, 1), loss_target[0], small, shards, opt)
    grads, deltas, new_m, new_v = {}, {}, {}, {}
    for k in _BIG:
        grads[k], deltas[k], new_m[k], new_v[k] = (unshard(a, k) for a in updates[k])

    small_grads = [gs["norm_mix_g"], gs["b_gate"], gs["q_norm_g"], gs["kv_norm_g"], gs["sgu_norm_g"], gs["w_sgu"],
                   gs["b_sgu_col"], gs["norm_ffn_g"], gs["norm_final_g"]]
    pack, sizes = _pack_rows([loss_row] + small_grads)
    total = _all_reduce_pack(pack)
    shapes = [(1, LANES)] + [wts[k].shape for k in _SMALL]
    unpacked = _unpack_rows(total, sizes, shapes)
    loss = unpacked[0][0, 0]
    for k, g in zip(_SMALL, unpacked[1:]):
        grads[k] = g
    g_pack = total[sizes[0][1]:]
    w_pack, _ = _pack_rows([wts[k] for k in _SMALL])
    m_pack, _ = _pack_rows([mom[k] for k in _SMALL])
    v_pack, _ = _pack_rows([var[k] for k in _SMALL])
    d_pack, nm_pack, nv_pack = _adamw_pack(g_pack, w_pack, m_pack, v_pack)
    small_shapes = [wts[k].shape for k in _SMALL]
    for store, pk in ((deltas, d_pack), (new_m, nm_pack), (new_v, nv_pack)):
        for k, a in zip(_SMALL, _unpack_rows(pk, sizes[1:], small_shapes)):
            store[k] = a

    return (loss, grad_x[None], *[grads[k] for k in _ORDER], *[deltas[k] for k in _ORDER],
            *[new_m[k] for k in _ORDER], *[new_v[k] for k in _ORDER])
```

```python
import functools
import math

import jax
import jax.numpy as jnp
from jax import lax
from jax.experimental import pallas as pl
from jax.experimental.pallas import tpu as pltpu

F32 = jnp.float32
BF16 = jnp.bfloat16

N_DEV = 8
N_HEADS = 16
QK_NOPE = 128
QK_ROPE = 64
V_HEAD = 128
HEAD_PAD = 256
ROPE_THETA = 10000.0
CHUNK = 128
SGU_GROUP = 128
RMS_EPS = 1e-6
LANES = 128
SUBLANES = 8

ADAM_LR = 0.001
ADAM_B1 = 0.9
ADAM_B2 = 0.999
ADAM_EPS = 1e-08
ADAM_WD = 0.01
ADAM_STEP = 10

VMEM_LIMIT = 48 * 1024 * 1024
MM_TILE = (2048, 512, 2048)
MM_TILE_TA = (512, 2048)
ATTN_TILE = 512
HEADS_PER_STEP = (4, 2)
ROW_KERNEL_BYTES = 24 * 1024 * 1024
SHARD_TILE_ELEMS = 256 * 1024
SLABS_PER_STEP = 2
TAIL_SPLIT = (3, 8)
NEG_BIG = -1e30
MESH = pl.DeviceIdType.MESH


def _pick(n, target, mult=LANES):
    best = None
    d = mult
    while d <= min(n, target):
        if n % d == 0:
            best = d
        d += mult
    return best or n


def _row_tile(t, width, n_blocks, mult=2 * SUBLANES):
    return _pick(t, max(mult, ROW_KERNEL_BYTES // (3 * n_blocks * width * 4)), mult)


def _shard_tile(r, c, elems=SHARD_TILE_ELEMS, max_rows=256):
    tr = _pick(r, max_rows, 2 * SUBLANES)
    return tr, _pick(c, max(LANES, elems // tr))


def _params(sem):
    return pltpu.CompilerParams(dimension_semantics=sem, vmem_limit_bytes=VMEM_LIMIT)


def _full(shape):
    nd = len(shape)
    return pl.BlockSpec(shape, lambda *_: (0,) * nd)


def _rows(tr, w, cb=0):
    return pl.BlockSpec((tr, w), lambda i: (i, cb))


class _Comm:
    def __init__(self, ins, out_shapes, sems, start, finish, aliases=None):
        self.ins, self.out_shapes, self.sems, self.start, self.finish = list(ins), list(out_shapes), list(sems), start, finish
        self.aliases = dict(aliases or {})


def _call(body, *, grid, in_specs, out_specs, out_shape, scratch_shapes=(), sem, name, args, comm=None):
    if comm is None:
        outs = pl.pallas_call(body, grid=grid, in_specs=list(in_specs), out_specs=list(out_specs),
                              out_shape=list(out_shape), scratch_shapes=list(scratch_shapes),
                              compiler_params=_params(sem), name=name)(*args)
        return list(outs), []
    n_in, n_out, n_sc = len(in_specs), len(out_shape), len(scratch_shapes)
    nci, nco = len(comm.ins), len(comm.out_shapes)

    def hosted(*refs):
        ins, refs = refs[:n_in], refs[n_in:]
        cins, refs = refs[:nci], refs[nci:]
        outs, refs = refs[:n_out], refs[n_out:]
        couts, refs = refs[:nco], refs[nco:]
        scratch, csems = refs[:n_sc], refs[n_sc:]
        ids = [pl.program_id(i) for i in range(len(grid))]
        first = functools.reduce(jnp.logical_and, [i == 0 for i in ids])
        last = functools.reduce(jnp.logical_and, [i == g - 1 for i, g in zip(ids, grid)])

        @pl.when(first)
        def _():
            comm.start(cins, couts, csems)

        body(*ins, *outs, *scratch)

        @pl.when(last)
        def _():
            comm.finish(cins, couts, csems)

    any_spec = pl.BlockSpec(memory_space=pl.ANY)
    res = pl.pallas_call(
        hosted, grid=grid, in_specs=list(in_specs) + [any_spec] * nci, out_specs=list(out_specs) + [any_spec] * nco,
        out_shape=list(out_shape) + comm.out_shapes, scratch_shapes=list(scratch_shapes) + comm.sems,
        input_output_aliases={n_in + i: n_out + o for i, o in comm.aliases.items()},
        compiler_params=pltpu.CompilerParams(dimension_semantics=("arbitrary",) * len(grid),
                                             vmem_limit_bytes=VMEM_LIMIT, has_side_effects=True),
        name=name)(*args, *comm.ins)
    return list(res[:n_out]), list(res[n_out:])


def _swiglu_grads(g, u, d):
    s = 1.0 / (1.0 + jnp.exp(-g))
    return (d * u * (s * (1.0 + g * (1.0 - s)))).astype(BF16), (d * (g * s)).astype(BF16)


def _mm(a, b, *, ta=False, tb=False, add=None, out_dtype=F32, tm=None, tn=None, tk=None, name, comm=None,
        slab=None, a_slab0=0, swiglu=None, rope=None):
    sq = None
    if ta:
        tm, tn = tm or MM_TILE_TA[0], tn or MM_TILE_TA[1]
    if slab is None:
        m, k = (a.shape[1], a.shape[0]) if ta else a.shape
        n = b.shape[0] if tb else b.shape[1]
        assert k == (b.shape[1] if tb else b.shape[0]), (a.shape, b.shape, ta, tb)
        tm, tn, tk = _pick(m, tm or MM_TILE[0]), _pick(n, tn or MM_TILE[1]), _pick(k, tk or MM_TILE[2])
        if rope is not None:
            tn = _pick(n, max(tn, HEAD_PAD), HEAD_PAD)
        grid = (m // tm, n // tn, k // tk)
        a_spec = pl.BlockSpec((tk, tm), lambda i, j, kk: (kk, i)) if ta else pl.BlockSpec((tm, tk), lambda i, j, kk: (i, kk))
        b_spec = pl.BlockSpec((tn, tk), lambda i, j, kk: (j, kk)) if tb else pl.BlockSpec((tk, tn), lambda i, j, kk: (kk, j))
        o_spec, o_shape = pl.BlockSpec((tm, tn), lambda i, j, kk: (i, j)), (m, n)
    elif slab == "n":
        m, k = (a.shape[1], a.shape[0]) if ta else a.shape
        s, c = b.shape[0], (b.shape[1] if tb else b.shape[2])
        assert k == (b.shape[2] if tb else b.shape[1]), (a.shape, b.shape, ta, tb)
        tm, tn, tk = _pick(m, tm or MM_TILE[0]), c, _pick(k, tk or MM_TILE[2])
        grid = (m // tm, s, k // tk)
        a_spec = pl.BlockSpec((tk, tm), lambda i, j, kk: (kk, i)) if ta else pl.BlockSpec((tm, tk), lambda i, j, kk: (i, kk))
        b_spec = (pl.BlockSpec((sq, c, tk), lambda i, j, kk: (j, 0, kk)) if tb
                  else pl.BlockSpec((sq, tk, c), lambda i, j, kk: (j, kk, 0)))
        o_spec, o_shape = pl.BlockSpec((sq, tm, c), lambda i, j, kk: (j, i, 0)), (s, m, c)
    elif slab == "m":
        assert ta and not tb
        s, k, c = a.shape
        n = b.shape[1]
        assert k == b.shape[0], (a.shape, b.shape)
        tm, tn, tk = c, _pick(n, tn or MM_TILE[1]), _pick(k, tk or MM_TILE[2])
        grid = (s, n // tn, k // tk)
        a_spec = pl.BlockSpec((sq, tk, c), lambda i, j, kk: (i, kk, 0))
        b_spec = pl.BlockSpec((tk, tn), lambda i, j, kk: (kk, j))
        o_spec, o_shape = pl.BlockSpec((sq, c, tn), lambda i, j, kk: (i, 0, j)), (s, c, n)
    else:
        assert slab == "k" and not ta
        s, c = b.shape[0], (b.shape[2] if tb else b.shape[1])
        m, n = a.shape[1], (b.shape[1] if tb else b.shape[2])
        assert a.shape[2] == c and a.shape[0] >= a_slab0 + s, (a.shape, b.shape, a_slab0)
        tm, tn, tk = _pick(m, tm or MM_TILE[0]), _pick(n, tn or MM_TILE[1]), c
        per_step = SLABS_PER_STEP if (s % SLABS_PER_STEP == 0 and a_slab0 % SLABS_PER_STEP == 0) else 1
        first = a_slab0 // per_step
        grid = (m // tm, n // tn, s // per_step)
        a_spec = pl.BlockSpec((per_step, tm, c), lambda i, j, kk: (kk + first, i, 0))
        b_spec = (pl.BlockSpec((per_step, tn, c), lambda i, j, kk: (kk, j, 0)) if tb
                  else pl.BlockSpec((per_step, c, tn), lambda i, j, kk: (kk, 0, j)))
        o_spec, o_shape = pl.BlockSpec((tm, tn), lambda i, j, kk: (i, j)), (m, n)
    nk = grid[2]
    dims = (((0 if ta else 1,), (1 if tb else 0,)), ((), ()))

    def product(a_ref, b_ref):
        if slab != "k":
            return lax.dot_general(a_ref[...].astype(BF16), b_ref[...].astype(BF16), dims, preferred_element_type=F32)
        r = None
        for u in range(a_ref.shape[0]):
            p = lax.dot_general(a_ref[u].astype(BF16), b_ref[u].astype(BF16), dims, preferred_element_type=F32)
            r = p if r is None else r + p
        return r

    if swiglu is not None:
        assert slab == "n" and add is None
        o_block = pl.BlockSpec((2, sq, tm, c), lambda i, j, kk: (0, j, i, 0))
        o_shape, out_dtype = (2,) + o_shape, BF16

    if rope is not None:
        assert slab is None and add is None and swiglu is None and tn % HEAD_PAD == 0
        out_dtype = BF16
    extras = tuple(swiglu or ()) + tuple(rope or ())

    def body(*refs):
        a_ref, b_ref = refs[:2]
        add_ref = refs[2] if add is not None else None
        x0_ref, x1_ref = refs[2:4] if extras else (None, None)
        o_ref = refs[2 + (add is not None) + len(extras)]
        acc_ref = refs[-1] if nk > 1 else None

        def finish(r):
            if swiglu is not None:
                o_ref[0], o_ref[1] = _swiglu_grads(x0_ref[...], x1_ref[...], r)
                return
            if rope is not None:
                cos, sin = x0_ref[...], x1_ref[...]
                for h in range(tn // HEAD_PAD):
                    lo = h * HEAD_PAD
                    o_ref[:, lo:lo + QK_NOPE] = r[:, lo:lo + QK_NOPE].astype(BF16)
                    o_ref[:, lo + QK_NOPE:lo + HEAD_PAD] = _rope(r[:, lo + QK_NOPE:lo + HEAD_PAD], cos, sin).astype(BF16)
                return
            if add_ref is not None:
                r = r + add_ref[...].astype(F32)
            o_ref[...] = r.astype(o_ref.dtype)

        if nk == 1:
            finish(product(a_ref, b_ref))
            return
        kk = pl.program_id(2)

        @pl.when(kk == 0)
        def _():
            acc_ref[...] = product(a_ref, b_ref)

        if nk > 2:
            @pl.when(jnp.logical_and(kk > 0, kk < nk - 1))
            def _():
                acc_ref[...] += product(a_ref, b_ref)

        @pl.when(kk == nk - 1)
        def _():
            finish(acc_ref[...] + product(a_ref, b_ref))

    in_specs = [a_spec, b_spec] + ([o_spec] if add is not None else []) + ([o_spec] * 2 if swiglu is not None else [])
    if rope is not None:
        in_specs += [pl.BlockSpec((tm, LANES), lambda i, j, kk: (i, 0))] * 2
    args = (a, b) + ((add,) if add is not None else ()) + extras
    if swiglu is not None:
        o_spec = o_block
    outs, comm_outs = _call(
        body, grid=grid, in_specs=in_specs, out_specs=[o_spec],
        out_shape=[jax.ShapeDtypeStruct(o_shape, out_dtype)],
        scratch_shapes=[pltpu.VMEM((tm, tn), F32)] if nk > 1 else [],
        sem=("parallel", "parallel", "arbitrary"), name=name, args=args, comm=comm)
    return outs[0] if comm is None else (outs[0], comm_outs)


def _rms_scale(x):
    return lax.rsqrt(jnp.mean(x * x, axis=-1, keepdims=True) + RMS_EPS)


def _rms_bwd(xhat, r, g, dy):
    t = dy * g
    dx = r * (t - xhat * jnp.mean(t * xhat, axis=-1, keepdims=True))
    return dx, dy * xhat


_GELU_C = math.sqrt(2.0 / math.pi)


def _gelu(x):
    return x * (0.5 * (1.0 + jnp.tanh(_GELU_C * (x + 0.044715 * (x * x * x)))))


def _gelu_and_grad(x):
    t = jnp.tanh(_GELU_C * (x + 0.044715 * (x * x * x)))
    cdf = 0.5 * (1.0 + t)
    return x * cdf, cdf + x * (0.5 * (1.0 - t * t) * (_GELU_C * (1.0 + 3.0 * 0.044715 * (x * x))))


def _sigmoid(x):
    return 1.0 / (1.0 + jnp.exp(-x))


def _swap_halves(x):
    lane = lax.broadcasted_iota(jnp.int32, x.shape, 1)
    first = (lane % QK_ROPE) < (QK_ROPE // 2)
    return jnp.where(first, pltpu.roll(x, LANES - QK_ROPE // 2, 1), pltpu.roll(x, QK_ROPE // 2, 1))


def _rope(x, cos, sin_signed):
    return x * cos + _swap_halves(x) * sin_signed


def _rope_bwd(d, cos, sin_signed):
    return d * cos + _swap_halves(d * sin_signed)


def _rope_tables(pos_col, inv_freq_row, sign_row):
    t = pos_col.shape[0]
    tr = _pick(t, 512, SUBLANES)

    def body(p_ref, f_ref, s_ref, cos_ref, sin_ref):
        ang = p_ref[...].astype(F32) * f_ref[...]
        cos_ref[...] = jnp.cos(ang)
        sin_ref[...] = jnp.sin(ang) * s_ref[...]

    return pl.pallas_call(
        body, grid=(t // tr,), in_specs=[_rows(tr, 1), _full((1, LANES)), _full((1, LANES))],
        out_specs=[_rows(tr, LANES), _rows(tr, LANES)],
        out_shape=[jax.ShapeDtypeStruct((t, LANES), F32)] * 2,
        compiler_params=_params(("parallel",)), name="rope_tables")(pos_col, inv_freq_row, sign_row)


def _norm_fwd(x, g, name):
    t, d = x.shape
    tr = _row_tile(t, d, 2)

    def body(x_ref, g_ref, y_ref):
        xv = x_ref[...]
        y_ref[...] = (xv * _rms_scale(xv) * g_ref[...]).astype(BF16)

    return pl.pallas_call(
        body, grid=(t // tr,), in_specs=[_rows(tr, d), _full((1, d))], out_specs=_rows(tr, d),
        out_shape=jax.ShapeDtypeStruct((t, d), BF16), compiler_params=_params(("parallel",)), name=name)(x, g)


def _lat_fwd(z_lat, qg, kvg, cos, sin, ql, kvl):
    t = z_lat.shape[0]
    tr = _row_tile(t, z_lat.shape[1], 2)

    def body(z_ref, qg_ref, kvg_ref, cos_ref, sin_ref, qn_ref, kvn_ref, kpe_ref):
        q = z_ref[:, 0:ql]
        qn_ref[...] = (q * _rms_scale(q) * qg_ref[...]).astype(BF16)
        kv = z_ref[:, ql:ql + kvl]
        kvn_ref[...] = (kv * _rms_scale(kv) * kvg_ref[...]).astype(BF16)
        kpe_ref[...] = _rope(z_ref[:, ql + kvl:ql + kvl + LANES], cos_ref[...], sin_ref[...]).astype(BF16)

    w = z_lat.shape[1]
    return pl.pallas_call(
        body, grid=(t // tr,),
        in_specs=[_rows(tr, w), _full((1, ql)), _full((1, kvl)), _rows(tr, LANES), _rows(tr, LANES)],
        out_specs=[_rows(tr, ql), _rows(tr, kvl), _rows(tr, LANES)],
        out_shape=[jax.ShapeDtypeStruct((t, ql), BF16), jax.ShapeDtypeStruct((t, kvl), BF16),
                   jax.ShapeDtypeStruct((t, LANES), BF16)],
        compiler_params=_params(("parallel",)), name="lat_fwd")(z_lat, qg, kvg, cos, sin)


def _tril_mask():
    r = lax.broadcasted_iota(jnp.int32, (CHUNK, CHUNK), 0)
    c = lax.broadcasted_iota(jnp.int32, (CHUNK, CHUNK), 1)
    return r >= c


def _sgu_fwd(z_uv, gs, ws, b_col):
    t = z_uv.shape[0]
    sw = z_uv.shape[1] // 2
    groups = sw // SGU_GROUP
    tr = _pick(t, 256, CHUNK)

    def body(u_ref, v_ref, gs_ref, ws_ref, b_ref, o_ref):
        v = _gelu(v_ref[...])
        vn = (v * _rms_scale(v) * gs_ref[...]).astype(BF16)
        tri = _tril_mask()
        for g in range(groups):
            wg = jnp.where(tri, ws_ref[g], 0.0).astype(BF16)
            cols = slice(g * SGU_GROUP, (g + 1) * SGU_GROUP)
            for c in range(tr // CHUNK):
                rows = slice(c * CHUNK, (c + 1) * CHUNK)
                mixed = jnp.dot(wg, vn[rows, cols], preferred_element_type=F32) + b_ref[g]
                o_ref[rows, cols] = (_gelu(u_ref[rows, cols]) * mixed).astype(BF16)

    return pl.pallas_call(
        body, grid=(t // tr,),
        in_specs=[_rows(tr, sw, 0), _rows(tr, sw, 1), _full((1, sw)), _full(ws.shape), _full(b_col.shape)],
        out_specs=_rows(tr, sw), out_shape=jax.ShapeDtypeStruct((t, sw), BF16),
        compiler_params=_params(("parallel",)), name="sgu_fwd")(z_uv, z_uv, gs, ws, b_col)


def _merge_fwd(y_attn, y_sgu, z_g, b_gate, comm=None):
    t, d = y_attn.shape
    tr = _row_tile(t, d, 5)

    def body(ya_ref, ys_ref, g0_ref, g1_ref, b0_ref, b1_ref, o_ref):
        g0 = _sigmoid(g0_ref[...] + b0_ref[...])
        g1 = _sigmoid(g1_ref[...] + b1_ref[...])
        o_ref[...] = (g0 * ya_ref[...] + g1 * ys_ref[...]).astype(BF16)

    bspec0 = pl.BlockSpec((1, d), lambda i: (0, 0))
    bspec1 = pl.BlockSpec((1, d), lambda i: (0, 1))
    outs, comm_outs = _call(
        body, grid=(t // tr,),
        in_specs=[_rows(tr, d), _rows(tr, d), _rows(tr, d, 0), _rows(tr, d, 1), bspec0, bspec1],
        out_specs=[_rows(tr, d)], out_shape=[jax.ShapeDtypeStruct((t, d), BF16)],
        sem=("parallel",), name="merge_fwd", args=(y_attn, y_sgu, z_g, z_g, b_gate, b_gate), comm=comm)
    return outs[0], comm_outs


def _swiglu_fwd(gate, up, comm=None):
    t, f = gate.shape
    tr = _row_tile(t, f, 3)

    def body(g_ref, u_ref, o_ref):
        g = g_ref[...]
        o_ref[...] = (g * _sigmoid(g) * u_ref[...]).astype(BF16)

    outs, comm_outs = _call(
        body, grid=(t // tr,), in_specs=[_rows(tr, f), _rows(tr, f)], out_specs=[_rows(tr, f)],
        out_shape=[jax.ShapeDtypeStruct((t, f), BF16)], sem=("parallel",), name="swiglu_fwd", args=(gate, up), comm=comm)
    return outs[0], comm_outs


def _loss_head(h2, g, target):
    t, d = h2.shape
    tr = _row_tile(t, d, 3)

    def body(h_ref, g_ref, t_ref, loss_ref, dh_ref, dhb_ref, dg_ref):
        @pl.when(pl.program_id(0) == 0)
        def _():
            loss_ref[...] = jnp.zeros_like(loss_ref)
            dg_ref[...] = jnp.zeros_like(dg_ref)

        h = h_ref[...]
        r = _rms_scale(h)
        hhat = h * r
        gv = g_ref[...]
        err = hhat * gv - t_ref[...]
        loss_ref[...] += jnp.full(loss_ref.shape, 0.5 * jnp.sum(jnp.mean(err * err, axis=-1)), F32)
        dx, dg_rows = _rms_bwd(hhat, r, gv, err * (1.0 / d))
        dh_ref[...] = dx
        dhb_ref[...] = dx.astype(BF16)
        dg_ref[...] += jnp.sum(dg_rows, axis=0, keepdims=True)

    return pl.pallas_call(
        body, grid=(t // tr,), in_specs=[_rows(tr, d), _full((1, d)), _rows(tr, d)],
        out_specs=[_full((1, LANES)), _rows(tr, d), _rows(tr, d), _full((1, d))],
        out_shape=[jax.ShapeDtypeStruct((1, LANES), F32), jax.ShapeDtypeStruct((t, d), F32),
                   jax.ShapeDtypeStruct((t, d), BF16), jax.ShapeDtypeStruct((1, d), F32)],
        compiler_params=_params(("arbitrary",)), name="loss_head")(h2, g, target)


def _norm_bwd(x, g, dy, resid, name, comm=None):
    t, d = x.shape
    tr = _row_tile(t, d, 5)

    def body(x_ref, g_ref, dy_ref, r_ref, dx_ref, dxb_ref, dg_ref):
        @pl.when(pl.program_id(0) == 0)
        def _():
            dg_ref[...] = jnp.zeros_like(dg_ref)

        xv = x_ref[...]
        r = _rms_scale(xv)
        dx, dg_rows = _rms_bwd(xv * r, r, g_ref[...], dy_ref[...])
        dx = r_ref[...] + dx
        dx_ref[...] = dx
        dxb_ref[...] = dx.astype(BF16)
        dg_ref[...] += jnp.sum(dg_rows, axis=0, keepdims=True)

    outs, comm_outs = _call(
        body, grid=(t // tr,), in_specs=[_rows(tr, d), _full((1, d)), _rows(tr, d), _rows(tr, d)],
        out_specs=[_rows(tr, d), _rows(tr, d), _full((1, d))],
        out_shape=[jax.ShapeDtypeStruct((t, d), F32), jax.ShapeDtypeStruct((t, d), BF16),
                   jax.ShapeDtypeStruct((1, d), F32)],
        sem=("arbitrary",), name=name, args=(x, g, dy, resid), comm=comm)
    return (outs[0], outs[1], outs[2]) if comm is None else (outs[0], outs[1], outs[2], comm_outs)


def _merge_bwd(dmerged, y_attn, y_sgu, z_g, b_gate):
    t, d = y_attn.shape
    tr = _row_tile(t, d, 7)

    def body(dm_ref, ya_ref, ys_ref, g0_ref, g1_ref, b0_ref, b1_ref, dya_ref, dys_ref, dz_ref, db_ref):
        @pl.when(pl.program_id(0) == 0)
        def _():
            db_ref[...] = jnp.zeros_like(db_ref)

        dm = dm_ref[...]
        g0 = _sigmoid(g0_ref[...] + b0_ref[...])
        g1 = _sigmoid(g1_ref[...] + b1_ref[...])
        dya_ref[...] = (dm * g0).astype(BF16)
        dys_ref[...] = (dm * g1).astype(BF16)
        dl0 = dm * ya_ref[...] * (g0 * (1.0 - g0))
        dl1 = dm * ys_ref[...] * (g1 * (1.0 - g1))
        dz_ref[:, 0:d] = dl0.astype(BF16)
        dz_ref[:, d:2 * d] = dl1.astype(BF16)
        db_ref[:, 0:d] += jnp.sum(dl0, axis=0, keepdims=True)
        db_ref[:, d:2 * d] += jnp.sum(dl1, axis=0, keepdims=True)

    bspec0 = pl.BlockSpec((1, d), lambda i: (0, 0))
    bspec1 = pl.BlockSpec((1, d), lambda i: (0, 1))
    return pl.pallas_call(
        body, grid=(t // tr,),
        in_specs=[_rows(tr, d), _rows(tr, d), _rows(tr, d), _rows(tr, d, 0), _rows(tr, d, 1), bspec0, bspec1],
        out_specs=[_rows(tr, d), _rows(tr, d), _rows(tr, 2 * d), _full((1, 2 * d))],
        out_shape=[jax.ShapeDtypeStruct((t, d), BF16), jax.ShapeDtypeStruct((t, d), BF16),
                   jax.ShapeDtypeStruct((t, 2 * d), BF16), jax.ShapeDtypeStruct((1, 2 * d), F32)],
        compiler_params=_params(("arbitrary",)), name="merge_bwd")(dmerged, y_attn, y_sgu, z_g, z_g, b_gate, b_gate)


def _sgu_bwd(z_uv, ds_out, gs, ws, b_col):
    t = z_uv.shape[0]
    sw = z_uv.shape[1] // 2
    groups = sw // SGU_GROUP
    tr = _pick(t, 256, CHUNK)

    def body(u_ref, v_ref, d_ref, gs_ref, ws_ref, b_ref, dz_ref, dws_ref, db_ref, dgs_ref, dvn_ref):
        @pl.when(pl.program_id(0) == 0)
        def _():
            dws_ref[...] = jnp.zeros_like(dws_ref)
            db_ref[...] = jnp.zeros_like(db_ref)
            dgs_ref[...] = jnp.zeros_like(dgs_ref)

        v, dgelu_v = _gelu_and_grad(v_ref[...])
        r = _rms_scale(v)
        vhat = v * r
        gsv = gs_ref[...]
        vn = (vhat * gsv).astype(BF16)
        tri = _tril_mask()
        for g in range(groups):
            wg = jnp.where(tri, ws_ref[g], 0.0).astype(BF16)
            cols = slice(g * SGU_GROUP, (g + 1) * SGU_GROUP)
            for c in range(tr // CHUNK):
                rows = slice(c * CHUNK, (c + 1) * CHUNK)
                vn_cg = vn[rows, cols]
                mixed = jnp.dot(wg, vn_cg, preferred_element_type=F32) + b_ref[g]
                u, dgelu_u = _gelu_and_grad(u_ref[rows, cols])
                dso = d_ref[rows, cols]
                dz_ref[rows, cols] = (dso * mixed * dgelu_u).astype(BF16)
                dmixed = dso * u
                db_ref[g] += jnp.sum(dmixed, axis=1, keepdims=True)
                dmixed_b = dmixed.astype(BF16)
                dws_ref[g] += jnp.where(
                    tri, lax.dot_general(dmixed_b, vn_cg, (((1,), (1,)), ((), ())), preferred_element_type=F32), 0.0)
                dvn_ref[rows, cols] = lax.dot_general(wg, dmixed_b, (((0,), (0,)), ((), ())), preferred_element_type=F32)
        dvn = dvn_ref[...]
        dv, dgs_rows = _rms_bwd(vhat, r, gsv, dvn)
        dz_ref[:, sw:2 * sw] = (dv * dgelu_v).astype(BF16)
        dgs_ref[...] += jnp.sum(dgs_rows, axis=0, keepdims=True)

    return pl.pallas_call(
        body, grid=(t // tr,),
        in_specs=[_rows(tr, sw, 0), _rows(tr, sw, 1), _rows(tr, sw), _full((1, sw)), _full(ws.shape), _full(b_col.shape)],
        out_specs=[_rows(tr, 2 * sw), _full(ws.shape), _full(b_col.shape), _full((1, sw))],
        out_shape=[jax.ShapeDtypeStruct((t, 2 * sw), BF16), jax.ShapeDtypeStruct(ws.shape, F32),
                   jax.ShapeDtypeStruct(b_col.shape, F32), jax.ShapeDtypeStruct((1, sw), F32)],
        scratch_shapes=[pltpu.VMEM((tr, sw), F32)],
        compiler_params=_params(("arbitrary",)), name="sgu_bwd")(z_uv, z_uv, ds_out, gs, ws, b_col)


def _lat_bwd(z_lat, qg, kvg, dqn, dkvn, dkpe_heads, cos, sin, ql, kvl):
    t, w = z_lat.shape
    heads = dkpe_heads.shape[0]
    tr = _row_tile(t, w + heads * LANES, 3)

    def body(z_ref, qg_ref, kvg_ref, dq_ref, dkv_ref, dk_ref, cos_ref, sin_ref, dz_ref, dqg_ref, dkvg_ref):
        @pl.when(pl.program_id(0) == 0)
        def _():
            dqg_ref[...] = jnp.zeros_like(dqg_ref)
            dkvg_ref[...] = jnp.zeros_like(dkvg_ref)

        q = z_ref[:, 0:ql]
        r = _rms_scale(q)
        dx, dg_rows = _rms_bwd(q * r, r, qg_ref[...], dq_ref[...])
        dz_ref[:, 0:ql] = dx.astype(BF16)
        dqg_ref[...] += jnp.sum(dg_rows, axis=0, keepdims=True)
        kv = z_ref[:, ql:ql + kvl]
        r = _rms_scale(kv)
        dx, dg_rows = _rms_bwd(kv * r, r, kvg_ref[...], dkv_ref[...])
        dz_ref[:, ql:ql + kvl] = dx.astype(BF16)
        dkvg_ref[...] += jnp.sum(dg_rows, axis=0, keepdims=True)
        dk = dk_ref[0]
        for h in range(1, heads):
            dk = dk + dk_ref[h]
        dz_ref[:, ql + kvl:ql + kvl + LANES] = _rope_bwd(dk, cos_ref[...], sin_ref[...]).astype(BF16)

    return pl.pallas_call(
        body, grid=(t // tr,),
        in_specs=[_rows(tr, w), _full((1, ql)), _full((1, kvl)), _rows(tr, ql), _rows(tr, kvl),
                  pl.BlockSpec((heads, tr, LANES), lambda i: (0, i, 0)), _rows(tr, LANES), _rows(tr, LANES)],
        out_specs=[_rows(tr, w), _full((1, ql)), _full((1, kvl))],
        out_shape=[jax.ShapeDtypeStruct((t, w), BF16), jax.ShapeDtypeStruct((1, ql), F32),
                   jax.ShapeDtypeStruct((1, kvl), F32)],
        compiler_params=_params(("arbitrary",)), name="lat_bwd")(z_lat, qg, kvg, dqn, dkvn, dkpe_heads, cos, sin)


_NT = (((1,), (1,)), ((), ()))


def _attn_scale():
    return (QK_NOPE + QK_ROPE) ** -0.5


def _heads_per_step(heads, wanted):
    return wanted if heads % wanted == 0 else 1


def _attn_fwd(q_c, kv, kpe, comm=None):
    t = q_c.shape[0]
    heads = q_c.shape[1] // HEAD_PAD
    tq = _pick(t, ATTN_TILE)
    nq = t // tq
    scale = _attn_scale()
    to_log2 = scale * math.log2(math.e)
    tn_dims = (((0,), (0,)), ((), ()))

    hps = _heads_per_step(heads, HEADS_PER_STEP[0])

    def body(q_ref, kv_ref, kpe_ref, o_ref, ob_ref, lse_ref, m_sc, l_sc, acc_sc):
        qi, ki = pl.program_id(1), pl.program_id(2)

        @pl.when(ki == 0)
        def _():
            m_sc[...] = jnp.full_like(m_sc, NEG_BIG)
            l_sc[...] = jnp.zeros_like(l_sc)
            acc_sc[...] = jnp.zeros_like(acc_sc)

        def step(diagonal):
            for u in range(hps):
                lo = u * HEAD_PAD
                kc = jnp.concatenate([kv_ref[:, lo:lo + QK_NOPE], kpe_ref[...]], axis=1)
                st = lax.dot_general(kc, q_ref[:, lo:lo + HEAD_PAD], _NT, preferred_element_type=F32)
                if diagonal:
                    krow = lax.broadcasted_iota(jnp.int32, st.shape, 0)
                    qcol = lax.broadcasted_iota(jnp.int32, st.shape, 1)
                    st = jnp.where(qcol >= krow, st, NEG_BIG)
                m_prev = m_sc[u]
                m_new = jnp.maximum(m_prev, jnp.max(st, axis=0, keepdims=True))
                alpha = jnp.exp2((m_prev - m_new) * to_log2)
                pt = jnp.exp2((st - m_new) * to_log2)
                l_sc[u] = alpha * l_sc[u] + jnp.sum(pt, axis=0, keepdims=True)
                acc_sc[u] = alpha * acc_sc[u] + lax.dot_general(
                    kv_ref[:, lo + QK_NOPE:lo + HEAD_PAD], pt.astype(BF16), tn_dims, preferred_element_type=F32)
                m_sc[u] = m_new

        @pl.when(ki < qi)
        def _():
            step(False)

        @pl.when(ki == qi)
        def _():
            step(True)
            for u in range(hps):
                o = (acc_sc[u] / l_sc[u]).T
                o_ref[:, u * V_HEAD:(u + 1) * V_HEAD] = o
                ob_ref[:, u * V_HEAD:(u + 1) * V_HEAD] = o.astype(BF16)
                lse_ref[u] = m_sc[u] * scale + jnp.log(l_sc[u])

    omap = lambda g, qi, ki: (qi, g)
    outs, comm_outs = _call(
        body, grid=(heads // hps, nq, nq),
        in_specs=[pl.BlockSpec((tq, hps * HEAD_PAD), omap),
                  pl.BlockSpec((tq, hps * HEAD_PAD), lambda g, qi, ki: (jnp.minimum(ki, qi), g)),
                  pl.BlockSpec((tq, LANES), lambda g, qi, ki: (jnp.minimum(ki, qi), 0))],
        out_specs=[pl.BlockSpec((tq, hps * V_HEAD), omap), pl.BlockSpec((tq, hps * V_HEAD), omap),
                   pl.BlockSpec((hps, 1, tq), lambda g, qi, ki: (g, 0, qi))],
        out_shape=[jax.ShapeDtypeStruct((t, heads * V_HEAD), F32), jax.ShapeDtypeStruct((t, heads * V_HEAD), BF16),
                   jax.ShapeDtypeStruct((heads, 1, t), F32)],
        scratch_shapes=[pltpu.VMEM((hps, 1, tq), F32), pltpu.VMEM((hps, 1, tq), F32),
                        pltpu.VMEM((hps, V_HEAD, tq), F32)],
        sem=("parallel", "parallel", "arbitrary"), name="attn_fwd", args=(q_c, kv, kpe), comm=comm)
    return outs[0], outs[1], outs[2], comm_outs


def _attn_bwd(q_c, kv, kpe, o, do, lse_row, cos, sin, comm=None):
    t = q_c.shape[0]
    heads = q_c.shape[1] // HEAD_PAD
    tk = _pick(t, ATTN_TILE)
    nk = t // tk
    scale = _attn_scale()
    tn_dims = (((0,), (0,)), ((), ()))

    hps = _heads_per_step(heads, HEADS_PER_STEP[1])

    def body(q_ref, kv_ref, kpe_ref, do_ref, lse_ref, o_ref, cos_ref, sin_ref, dq_ref, dkv_ref, dkpe_ref,
             dk_sc, dv_sc, delta_sc, dq_sc):
        ki, qi = pl.program_id(1), pl.program_id(2)

        @pl.when(jnp.logical_and(ki == 0, qi == 0))
        def _():
            dq_sc[...] = jnp.zeros_like(dq_sc)

        @pl.when(qi == 0)
        def _():
            dk_sc[...] = jnp.zeros_like(dk_sc)
            dv_sc[...] = jnp.zeros_like(dv_sc)

        @pl.when(ki == 0)
        def _():
            for u in range(hps):
                cols = slice(u * V_HEAD, (u + 1) * V_HEAD)
                delta_sc[qi * hps + u] = jnp.sum((do_ref[:, cols] * o_ref[:, cols]).T, axis=0, keepdims=True)

        def step(diagonal):
            for u in range(hps):
                lo = u * HEAD_PAD
                kc = jnp.concatenate([kv_ref[:, lo:lo + QK_NOPE], kpe_ref[...]], axis=1)
                q = q_ref[:, lo:lo + HEAD_PAD]
                st = lax.dot_general(kc, q, _NT, preferred_element_type=F32) * scale
                pt = jnp.exp(st - lse_ref[u])
                if diagonal:
                    krow = lax.broadcasted_iota(jnp.int32, st.shape, 0)
                    qcol = lax.broadcasted_iota(jnp.int32, st.shape, 1)
                    pt = jnp.where(qcol >= krow, pt, 0.0)
                do_b = do_ref[:, u * V_HEAD:(u + 1) * V_HEAD].astype(BF16)
                dv_sc[u] += jnp.dot(pt.astype(BF16), do_b, preferred_element_type=F32)
                dpt = lax.dot_general(kv_ref[:, lo + QK_NOPE:lo + HEAD_PAD], do_b, _NT, preferred_element_type=F32)
                dst = (pt * (dpt - delta_sc[qi * hps + u]) * scale).astype(BF16)
                dk_sc[u] += jnp.dot(dst, q, preferred_element_type=F32)
                rows = pl.ds(pl.multiple_of(qi * tk, tk), tk)
                dq_sc[rows, lo:lo + HEAD_PAD] += lax.dot_general(dst, kc, tn_dims, preferred_element_type=F32)

        @pl.when(qi > ki)
        def _():
            step(False)

        @pl.when(qi == ki)
        def _():
            step(True)

        @pl.when(qi == nk - 1)
        def _():
            for u in range(hps):
                lo = u * HEAD_PAD
                dkv_ref[:, lo:lo + QK_NOPE] = dk_sc[u, :, 0:QK_NOPE].astype(BF16)
                dkv_ref[:, lo + QK_NOPE:lo + HEAD_PAD] = dv_sc[u].astype(BF16)
                dkpe_ref[u] = dk_sc[u, :, QK_NOPE:QK_NOPE + LANES]

        @pl.when(jnp.logical_and(ki == nk - 1, qi == nk - 1))
        def _():
            cos, sin = cos_ref[...], sin_ref[...]
            for u in range(hps):
                lo = u * HEAD_PAD
                dq_ref[:, lo:lo + QK_NOPE] = dq_sc[:, lo:lo + QK_NOPE].astype(BF16)
                dq_ref[:, lo + QK_NOPE:lo + HEAD_PAD] = _rope_bwd(dq_sc[:, lo + QK_NOPE:lo + HEAD_PAD], cos, sin).astype(BF16)

    qclamp = lambda g, ki, qi: (jnp.maximum(qi, ki), g)
    outs, comm_outs = _call(
        body, grid=(heads // hps, nk, nk),
        in_specs=[pl.BlockSpec((tk, hps * HEAD_PAD), qclamp),
                  pl.BlockSpec((tk, hps * HEAD_PAD), lambda g, ki, qi: (ki, g)),
                  pl.BlockSpec((tk, LANES), lambda g, ki, qi: (ki, 0)),
                  pl.BlockSpec((tk, hps * V_HEAD), qclamp),
                  pl.BlockSpec((hps, 1, tk), lambda g, ki, qi: (g, 0, jnp.maximum(qi, ki))),
                  pl.BlockSpec((tk, hps * V_HEAD), lambda g, ki, qi: (jnp.where(ki == 0, qi, 0), g)),
                  _full((t, LANES)), _full((t, LANES))],
        out_specs=[pl.BlockSpec((t, hps * HEAD_PAD), lambda g, ki, qi: (0, g)),
                   pl.BlockSpec((tk, hps * HEAD_PAD), lambda g, ki, qi: (ki, g)),
                   pl.BlockSpec((hps, tk, LANES), lambda g, ki, qi: (g, ki, 0))],
        out_shape=[jax.ShapeDtypeStruct((t, heads * HEAD_PAD), BF16),
                   jax.ShapeDtypeStruct((t, heads * HEAD_PAD), BF16), jax.ShapeDtypeStruct((heads, t, LANES), F32)],
        scratch_shapes=[pltpu.VMEM((hps, tk, HEAD_PAD), F32), pltpu.VMEM((hps, tk, V_HEAD), F32),
                        pltpu.VMEM((nk * hps, 1, tk), F32), pltpu.VMEM((t, hps * HEAD_PAD), F32)],
        sem=("parallel", "arbitrary", "arbitrary"), name="attn_bwd",
        args=(q_c, kv, kpe, do, lse_row, o, cos, sin), comm=comm)
    return outs[0], outs[1], outs[2], comm_outs


def _local_step(x, pos_col, target, small, shards, opt):
    t = x.shape[0]
    ql, kvl = small["q_norm_g"].shape[1], small["kv_norm_g"].shape[1]
    sw = small["sgu_norm_g"].shape[1]
    heads = (shards["w_uq"].shape[1] * N_DEV) // (QK_NOPE + QK_ROPE)
    big = {}
    early = ["w_in", "w_uq", "w_ukv"]
    big.update(_compute_layout(dict(zip(early, _all_gather([shards[k] for k in early]))), ql, kvl, heads, sw))
    half = QK_ROPE // 2
    lane = jnp.arange(LANES)
    inv_freq = ROPE_THETA ** (-jnp.arange(0, QK_ROPE, 2, dtype=F32) / QK_ROPE)
    inv_row = inv_freq[lane % half][None, :]
    sign_row = jnp.where((lane % QK_ROPE) < half, -1.0, 1.0).astype(F32)[None, :]
    cos, sin = _rope_tables(pos_col, inv_row, sign_row)
    ws = small["w_sgu"]
    b_col = small["b_sgu_col"]

    def arrived(names, bufs):
        big.update(_compute_layout(dict(zip(names, bufs)), ql, kvl, heads, sw))

    a = _norm_fwd(x, small["norm_mix_g"], "norm_mix_fwd")
    z_lat = _mm(a, big["w_lat_t"], tb=True, name="z_lat")
    z_uv, g_sgu = _mm(a, big["w_uv_t"], tb=True, name="z_uv", comm=_gather_stage(1, [shards["w_o_sgu"]]))
    z_g, (g_attn, g_sgu) = _mm(a, big["w_g_t"], tb=True, name="z_g",
                               comm=_join(_gather_stage(1, [shards["w_o_attn"]]), _gather_stage(2, g_sgu)))
    qn, kvn, kpe = _lat_fwd(z_lat, small["q_norm_g"], small["kv_norm_g"], cos, sin, ql, kvl)
    q_c, (g_attn, g_sgu) = _mm(qn, big["w_uq"], name="q_up_rope", rope=(cos, sin),
                               comm=_join(_gather_stage(2, [g_attn]), _gather_stage(3, [g_sgu])))
    kv, (g_attn, g_out) = _mm(kvn, big["w_ukv"], out_dtype=BF16, name="kv_up",
                              comm=_join(_gather_stage(3, [g_attn]), _gather_stage(1, [shards["w_out"]])))
    arrived(["w_o_sgu", "w_o_attn"], [g_sgu, g_attn])
    attn, attn_b, lse, (w_gate, w_up) = _attn_fwd(
        q_c, kv, kpe, comm=_gather_stage(1, [shards["w_gate_ffn"], shards["w_up_ffn"]]))
    s_out = _sgu_fwd(z_uv, small["sgu_norm_g"], ws, b_col)
    y_sgu, (g_out,) = _mm(s_out, big["w_o_sgu"], name="y_sgu", comm=_gather_stage(2, [g_out]))
    y_attn, (w_gate, g_out) = _mm(attn_b, big["w_o_attn"], name="y_attn",
                                  comm=_join(_gather_stage(2, [w_gate]), _gather_stage(3, [g_out])))
    arrived(["w_out"], [g_out])
    merged, (w_up, w_gate) = _merge_fwd(y_attn, y_sgu, z_g, small["b_gate"],
                                        comm=_join(_gather_stage(2, [w_up]), _gather_stage(3, [w_gate])))
    h1, (w_up,) = _mm(merged, big["w_out"], add=x, name="h1", comm=_gather_stage(3, [w_up]))
    f = _norm_fwd(h1, small["norm_ffn_g"], "norm_ffn_fwd")
    gate, w_down = _mm(f, w_gate, tb=True, slab="n", name="ffn_gate", comm=_gather_stage(1, [shards["w_down_ffn"]]))
    up, w_down = _mm(f, w_up, tb=True, slab="n", name="ffn_up", comm=_gather_stage(2, w_down))
    ffn = gate.shape[2]
    gate, up = gate.reshape(N_DEV * t, ffn), up.reshape(N_DEV * t, ffn)
    act, (w_down,) = _swiglu_fwd(gate, up, comm=_gather_stage(3, w_down))
    act = act.reshape(N_DEV, t, ffn)
    h2 = _mm(act, w_down, slab="k", add=h1, name="h2")
    loss_row, dh2, dh2_b, d_norm_final = _loss_head(h2, small["norm_final_g"], target)

    def pair_sums(names, slabs, bufs):
        return [_pair_sum(g, b, "pair_sum_" + k) for k, g, b in zip(names, slabs, bufs)]

    parts, updates = {}, {}

    def update(names, label, comm=None):
        res, got = _adamw_shards([parts[k] for k in names], [opt[k] for k in names], "adamw_" + label, comm=comm)
        updates.update(zip(names, res))
        return got

    down_slabs = [_mm(act, dh2_b, ta=True, slab="m", out_dtype=BF16, name="dw_down")]
    dgu, bufs = _mm(dh2_b, w_down, tb=True, slab="n", tm=MM_TILE[0] // 2, name="dact_swiglu_bwd",
                    comm=_to_sibling(down_slabs), swiglu=(gate.reshape(N_DEV, t, ffn), up.reshape(N_DEV, t, ffn)))
    dgu = dgu.reshape(2 * N_DEV, t, ffn)
    down_pair = pair_sums(["w_down_ffn"], down_slabs, bufs)
    dw_gu, got = _mm(dgu, f, ta=True, slab="m", out_dtype=BF16, name="dw_gate_up", comm=_to_chips(down_pair))
    parts["w_down_ffn"] = got[0]
    gu_names = ["w_gate_ffn", "w_up_ffn"]
    df, bufs = _mm(dgu, w_gate, slab="k", name="df_gate", comm=_to_sibling([dw_gu, dw_gu], first=[0, N_DEV]))
    gu_pairs = [_pair_sum(dw_gu, b, "pair_sum_" + k, first=s0) for k, b, s0 in zip(gu_names, bufs, [0, N_DEV])]
    half = _pick(gu_pairs[1].shape[1], gu_pairs[1].shape[1] // 2, 2 * SUBLANES)
    df, up_parts = _mm(dgu, w_up, slab="k", a_slab0=N_DEV, add=df, name="df_up",
                       comm=_to_chips(gu_pairs[1:], rows=[("r", 0, half)]))
    quarter = _pick(half, half // 2, 2 * SUBLANES)
    dh1, dh1_b, d_norm_ffn, gate_parts = _norm_bwd(h1, small["norm_ffn_g"], df, dh2, "norm_ffn_bwd",
                                                  comm=_to_chips(gu_pairs[:1], rows=[("r", 0, quarter)]))
    dw_out = _mm(merged, dh1_b, ta=True, out_dtype=BF16, name="dw_out")
    out_slabs = [_slabs_from_rows(dw_out)]
    dmerged, bufs = _mm(dh1_b, big["w_out"], tb=True, name="dmerged", comm=_to_sibling(out_slabs))
    out_pair = pair_sums(["w_out"], out_slabs, bufs)
    dy_attn, dy_sgu, dz_g, d_b_gate = _merge_bwd(dmerged, y_attn, y_sgu, z_g, small["b_gate"])
    dw_o_sgu = _mm(s_out, dy_sgu, ta=True, out_dtype=BF16, name="dw_o_sgu")
    ds_out = _mm(dy_sgu, big["w_o_sgu"], tb=True, name="ds_out")
    dz_uv, d_ws, d_b_col, d_sgu_norm = _sgu_bwd(z_uv, ds_out, small["sgu_norm_g"], ws, b_col)
    dw_o_attn = _mm(attn_b, dy_attn, ta=True, out_dtype=BF16, name="dw_o_attn")
    mix_names = ["w_o_sgu", "w_o_attn"]
    mix_slabs = [_slabs_from_cols(dw_o_sgu), _slabs_from_rows(dw_o_attn)]
    dattn, bufs = _mm(dy_attn, big["w_o_attn"], tb=True, name="dattn", comm=_to_sibling(mix_slabs))
    mix_pairs = pair_sums(mix_names, mix_slabs, bufs)
    rows = gu_pairs[1].shape[1]
    dq_p, dkv, dkpe_heads, got = _attn_bwd(
        q_c, kv, kpe, attn, dattn, lse, cos, sin,
        comm=_join(_to_chips(gu_pairs[:1], rows=[("r", quarter, rows - quarter)], into=gate_parts),
                   _to_chips(gu_pairs[1:], rows=[("r", half, rows - half)], into=up_parts)))
    parts.update(zip(gu_names, got))
    dw_uq = _mm(qn, dq_p, ta=True, out_dtype=BF16, name="dw_uq")
    dw_ukv = _mm(kvn, dkv, ta=True, out_dtype=BF16, name="dw_ukv")
    dqn = _mm(dq_p, big["w_uq"], tb=True, name="dqn")
    dkvn = _mm(dkv, big["w_ukv"], tb=True, name="dkvn")
    dz_lat, d_q_norm, d_kv_norm = _lat_bwd(z_lat, small["q_norm_g"], small["kv_norm_g"], dqn, dkvn, dkpe_heads,
                                           cos, sin, ql, kvl)
    dw_g, got = _mm(dz_g, a, ta=True, out_dtype=BF16, name="dw_g", comm=_to_chips(out_pair))
    parts["w_out"] = got[0]
    dw_uv, got = _mm(dz_uv, a, ta=True, out_dtype=BF16, name="dw_uv", comm=_to_chips(mix_pairs[1:]))
    parts["w_o_attn"] = got[0]
    dw_lat, got = _mm(dz_lat, a, ta=True, out_dtype=BF16, name="dw_lat", comm=_to_chips(mix_pairs[:1]))
    parts["w_o_sgu"] = got[0]
    lat = ql + kvl + QK_ROPE
    dw_uq_cols = dw_uq.reshape(ql, heads, HEAD_PAD)[:, :, :QK_NOPE + QK_ROPE].reshape(ql, heads * (QK_NOPE + QK_ROPE))
    in_names = ["w_uq", "w_ukv", "w_in"]
    in_slabs = [_slabs_from_cols(dw_uq_cols), _slabs_from_cols(dw_ukv),
                _slabs_from_rows(jnp.concatenate([dw_lat[:lat], dw_uv, dw_g], axis=0))]
    da = _mm(dz_lat, big["w_lat_t"], name="da_lat")
    da, bufs = _mm(dz_uv, big["w_uv_t"], add=da, name="da_uv", comm=_to_sibling(in_slabs))
    uq_pair, ukv_pair, in_pair = pair_sums(in_names, in_slabs, bufs)
    cols = in_pair.shape[2]
    first = ((cols * TAIL_SPLIT[0]) // TAIL_SPLIT[1]) // LANES * LANES or cols
    da, in_parts = _mm(dz_g, big["w_g_t"], add=da, name="da_g", comm=_to_chips([in_pair], rows=[("c", 0, first)]))
    grad_x, _, d_norm_mix, got = _norm_bwd(x, small["norm_mix_g"], da, dh1, "norm_mix_bwd",
                                          comm=_to_chips([uq_pair, ukv_pair]))
    parts["w_uq"], parts["w_ukv"] = got
    rest = _to_chips([in_pair], rows=[("c", first, cols - first)], into=in_parts) if first < cols else None
    got = update(["w_gate_ffn", "w_up_ffn", "w_down_ffn"], "ffn", comm=rest)
    parts["w_in"] = got[0] if rest is not None else in_parts[0]
    update(["w_out", "w_o_attn"], "mixer_out")
    for k in ("w_o_sgu", "w_uq", "w_ukv", "w_in"):
        update([k], k)

    gs = {"norm_mix_g": d_norm_mix, "b_gate": d_b_gate, "q_norm_g": d_q_norm, "kv_norm_g": d_kv_norm,
          "sgu_norm_g": d_sgu_norm, "w_sgu": d_ws, "b_sgu_col": d_b_col, "norm_ffn_g": d_norm_ffn,
          "norm_final_g": d_norm_final}
    return loss_row, grad_x, gs, updates


def _my_place():
    return lax.axis_index("x"), lax.axis_index("y"), lax.axis_index("c")


N_CHIPS = N_DEV // 2

_GATHER_SEMS = [[(3,), (3,), ()], [(4,), (4,)], [(1,), (1,)]]


def _halves(shape):
    r, c = shape
    if (c // 2) % LANES == 0:
        return ("c", 0, c // 2), ("c", c // 2, c // 2)
    assert (r // 2) % (2 * SUBLANES) == 0, shape
    return ("r", 0, r // 2), ("r", r // 2, r // 2)


def _gather_copies(stage, ins, outs, sems):
    x, y, c = _my_place()
    me, x_nbr, y_nbr, diag = 4 * x + 2 * y + c, 4 * (1 - x) + 2 * y + c, 4 * x + 2 * (1 - y) + c, 4 * (1 - x) + 2 * (1 - y) + c
    sibling = (x, y, 1 - c)

    def remote(w, k, src, dst, to):
        return pltpu.make_async_remote_copy(src_ref=src, dst_ref=dst, send_sem=sems[0].at[w, k], recv_sem=sems[1].at[w, k],
                                            device_id=to, device_id_type=MESH)

    out = []
    for w in range(len(outs)):
        if stage == 1:
            dst = outs[w].at[me]
            out.append(pltpu.make_async_copy(ins[w], dst, sems[2].at[w]))
            out += [remote(w, k, ins[w], dst, to) for k, to in enumerate([sibling, (1 - x, y, c), (x, 1 - y, c)])]
        elif stage == 2:
            first, second = _halves(outs[w].shape[1:])
            out.append(remote(w, 0, _window(ins[w], x_nbr, first), _window(outs[w], x_nbr, first), (x, 1 - y, c)))
            out.append(remote(w, 1, _window(ins[w], y_nbr, second), _window(outs[w], y_nbr, second), (1 - x, y, c)))
            out.append(remote(w, 2, ins[w].at[x_nbr], outs[w].at[x_nbr], sibling))
            out.append(remote(w, 3, ins[w].at[y_nbr], outs[w].at[y_nbr], sibling))
        else:
            out.append(remote(w, 0, ins[w].at[diag], outs[w].at[diag], sibling))
    return out


def _gather_stage(stage, arrays):
    n = len(arrays)

    def start(ins, outs, sems):
        for cp in _gather_copies(stage, ins, outs, sems):
            cp.start()

    def finish(ins, outs, sems):
        for cp in _gather_copies(stage, ins, outs, sems):
            cp.wait()

    shapes = [jax.ShapeDtypeStruct(((N_DEV,) + a.shape) if stage == 1 else a.shape, a.dtype) for a in arrays]
    return _Comm(arrays, shapes, [pltpu.SemaphoreType.DMA((n,) + s) for s in _GATHER_SEMS[stage - 1]], start, finish,
                 aliases=None if stage == 1 else {w: w for w in range(n)})


def _join(*comms):
    ins, shapes, sems, aliases, spans = [], [], [], {}, []
    for cm in comms:
        spans.append((len(ins), len(ins) + len(cm.ins), len(shapes), len(shapes) + len(cm.out_shapes),
                      len(sems), len(sems) + len(cm.sems)))
        aliases.update({len(ins) + i: len(shapes) + o for i, o in cm.aliases.items()})
        ins, shapes, sems = ins + cm.ins, shapes + cm.out_shapes, sems + cm.sems

    def each(half):
        def run(i_refs, o_refs, s_refs):
            for cm, (i0, i1, o0, o1, s0, s1) in zip(comms, spans):
                getattr(cm, half)(i_refs[i0:i1], o_refs[o0:o1], s_refs[s0:s1])
        return run

    return _Comm(ins, shapes, sems, each("start"), each("finish"), aliases)


def _all_gather(shards):
    n = len(shards)
    n_sems = [len(s) for s in _GATHER_SEMS]

    def body(*refs):
        ins, outs, sems = refs[:n], refs[n:2 * n], refs[2 * n:]
        s0 = 0
        for stage in (1, 2, 3):
            mine = sems[s0:s0 + n_sems[stage - 1]]
            s0 += n_sems[stage - 1]
            copies = _gather_copies(stage, ins if stage == 1 else outs, outs, mine)
            for cp in copies:
                cp.start()
            for cp in copies:
                cp.wait()

    any_spec = pl.BlockSpec(memory_space=pl.ANY)
    return pl.pallas_call(
        body, in_specs=[any_spec] * n, out_specs=[any_spec] * n,
        out_shape=[jax.ShapeDtypeStruct((N_DEV,) + s.shape, s.dtype) for s in shards],
        scratch_shapes=[pltpu.SemaphoreType.DMA((n,) + s) for stage in _GATHER_SEMS for s in stage],
        compiler_params=pltpu.CompilerParams(has_side_effects=True), name="all_gather_weights")(*shards)


def _to_sibling(grads, first=None):
    n = len(grads)
    first = first or [0] * n

    def copies(ins, outs, sems):
        x, y, c = _my_place()
        send_sems, recv_sems = sems
        return [pltpu.make_async_remote_copy(
            src_ref=ins[w].at[first[w] + 2 * i + (1 - c)], dst_ref=outs[w].at[i], send_sem=send_sems.at[w, i],
            recv_sem=recv_sems.at[w, i], device_id=(x, y, 1 - c), device_id_type=MESH)
            for w in range(n) for i in range(N_CHIPS)]

    def start(ins, outs, sems):
        for cp in copies(ins, outs, sems):
            cp.start()

    def finish(ins, outs, sems):
        for cp in copies(ins, outs, sems):
            cp.wait()

    return _Comm(grads, [jax.ShapeDtypeStruct((N_CHIPS,) + g.shape[1:], g.dtype) for g in grads],
                 [pltpu.SemaphoreType.DMA((n, N_CHIPS)), pltpu.SemaphoreType.DMA((n, N_CHIPS))], start, finish)


def _window(ref, slab, win):
    if win is None:
        return ref.at[slab]
    if win[0] == "r":
        return ref.at[slab, pl.ds(win[1], win[2])]
    return ref.at[slab, slice(None), pl.ds(win[1], win[2])]


def _to_chips(parts, rows=None, into=None):
    n = len(parts)
    rows = rows or [None] * n

    def copies(ins, outs, sems):
        x, y, c = _my_place()
        send_sems, recv_sems, local_sems = sems
        mine = 2 * x + y
        chips = [(1 - x, y), (x, 1 - y), (1 - x, 1 - y)]
        remote = [pltpu.make_async_remote_copy(
            src_ref=_window(ins[w], 2 * cx + cy, rows[w]), dst_ref=_window(outs[w], mine, rows[w]),
            send_sem=send_sems.at[w, j], recv_sem=recv_sems.at[w, j], device_id=(cx, cy, c), device_id_type=MESH)
            for w in range(n) for j, (cx, cy) in enumerate(chips)]
        local = [pltpu.make_async_copy(_window(ins[w], mine, rows[w]), _window(outs[w], mine, rows[w]),
                                       local_sems.at[w]) for w in range(n)]
        return remote + local

    def start(ins, outs, sems):
        for cp in copies(ins, outs, sems):
            cp.start()

    def finish(ins, outs, sems):
        for cp in copies(ins, outs, sems):
            cp.wait()

    return _Comm(list(parts) + list(into or []), [jax.ShapeDtypeStruct(p.shape, p.dtype) for p in parts],
                 [pltpu.SemaphoreType.DMA((n, N_CHIPS - 1)), pltpu.SemaphoreType.DMA((n, N_CHIPS - 1)),
                  pltpu.SemaphoreType.DMA((n,))], start, finish,
                 aliases={n + w: w for w in range(n)} if into else None)


def _pair_sum(g, buf, name, first=0):
    _, r, c = g.shape
    tr, tc = _shard_tile(r, c, 4 * SHARD_TILE_ELEMS, 1024)
    core = (lax.axis_index("c") + first).astype(jnp.int32).reshape(1)

    def body(core_ref, g_ref, b_ref, o_ref):
        o_ref[...] = (g_ref[...].astype(F32) + b_ref[...].astype(F32)).astype(o_ref.dtype)

    blk = (1, tr, tc)
    return pl.pallas_call(
        body, grid_spec=pltpu.PrefetchScalarGridSpec(
            num_scalar_prefetch=1, grid=(N_CHIPS, r // tr, c // tc),
            in_specs=[pl.BlockSpec(blk, lambda i, j, l, core_ref: (2 * i + core_ref[0], j, l)),
                      pl.BlockSpec(blk, lambda i, j, l, core_ref: (i, j, l))],
            out_specs=pl.BlockSpec(blk, lambda i, j, l, core_ref: (i, j, l))),
        out_shape=jax.ShapeDtypeStruct(buf.shape, buf.dtype),
        compiler_params=_params(("parallel", "parallel", "parallel")), name=name)(core, g, buf)


def _all_reduce_pack(pack):
    r = pack.shape[0]

    def body(x_ref, out_ref, gath_ref, send_sems, recv_sems, local_sem):
        x, y, c = _my_place()
        me, sibling = (x, y, c), (x, y, 1 - c)
        chips = [(1 - x, y), (x, 1 - y), (1 - x, 1 - y)]

        def slab(place):
            return gath_ref.at[4 * place[0] + 2 * place[1] + place[2]]

        def copy(k, place, to, src=None):
            return pltpu.make_async_remote_copy(
                src_ref=slab(place) if src is None else src, dst_ref=slab(place),
                send_sem=send_sems.at[k], recv_sem=recv_sems.at[k], device_id=to, device_id_type=MESH)

        mine = pltpu.make_async_copy(x_ref, slab(me), local_sem)
        mine.start()
        first = [copy(0, me, sibling, src=x_ref)]
        first += [copy(1 + j, me, (*chip, c), src=x_ref) for j, chip in enumerate(chips)]
        for cp in first:
            cp.start()
        passed = [copy(4 + j, (*chip, c), sibling) for j, chip in enumerate(chips)]
        for j, chip in enumerate(chips):
            copy(1 + j, (*chip, c), me).wait_recv()
            passed[j].start()
        copy(0, sibling, me).wait_recv()
        for j, chip in enumerate(chips):
            copy(4 + j, (*chip, 1 - c), me).wait_recv()
        for cp in first + passed:
            cp.wait_send()
        mine.wait()
        acc = gath_ref[0]
        for i in range(1, N_DEV):
            acc = acc + gath_ref[i]
        out_ref[...] = acc

    vmem = pl.BlockSpec(memory_space=pltpu.VMEM)
    return pl.pallas_call(
        body, in_specs=[vmem], out_specs=vmem, out_shape=jax.ShapeDtypeStruct(pack.shape, F32),
        scratch_shapes=[pltpu.VMEM((N_DEV, r, LANES), F32), pltpu.SemaphoreType.DMA((7,)),
                        pltpu.SemaphoreType.DMA((7,)), pltpu.SemaphoreType.DMA],
        compiler_params=pltpu.CompilerParams(vmem_limit_bytes=VMEM_LIMIT), name="all_reduce_small")(pack)


def _adamw_math(w, g, m, v):
    m = ADAM_B1 * m + (1.0 - ADAM_B1) * g
    v = ADAM_B2 * v + (1.0 - ADAM_B2) * (g * g)
    m_hat = m / (1.0 - ADAM_B1 ** ADAM_STEP)
    v_hat = v / (1.0 - ADAM_B2 ** ADAM_STEP)
    delta = -ADAM_LR * (m_hat / (jnp.sqrt(v_hat) + ADAM_EPS) + ADAM_WD * w)
    return delta, m, v


def _adamw_shards(parts, opts, name, comm=None):
    r, c = opts[0][0].shape
    n_parts, k = parts[0].shape[0], len(parts)
    tr, tc = _shard_tile(r, c, SHARD_TILE_ELEMS // k)

    def body(*refs):
        ins, outs = refs[:4 * k], refs[4 * k:]
        for s in range(k):
            p_ref, w_ref, m_ref, v_ref = ins[4 * s:4 * s + 4]
            g_ref, d_ref, nm_ref, nv_ref = outs[4 * s:4 * s + 4]
            g = p_ref[0].astype(F32)
            for i in range(1, n_parts):
                g = g + p_ref[i].astype(F32)
            g_ref[...] = g
            d_ref[...], nm_ref[...], nv_ref[...] = _adamw_math(w_ref[...], g, m_ref[...], v_ref[...])

    spec = pl.BlockSpec((tr, tc), lambda i, j: (i, j))
    args = [a for p, o in zip(parts, opts) for a in (p,) + tuple(o)]
    outs, comm_outs = _call(
        body, grid=(r // tr, c // tc),
        in_specs=[pl.BlockSpec((n_parts, tr, tc), lambda i, j: (0, i, j)), spec, spec, spec] * k,
        out_specs=[spec] * (4 * k), out_shape=[jax.ShapeDtypeStruct((r, c), F32)] * (4 * k),
        sem=("parallel", "parallel"), name=name, args=args, comm=comm)
    return [outs[4 * s:4 * s + 4] for s in range(k)], comm_outs


def _adamw_pack(g, w, m, v):
    r, c = w.shape

    def body(g_ref, w_ref, m_ref, v_ref, d_ref, nm_ref, nv_ref):
        d_ref[...], nm_ref[...], nv_ref[...] = _adamw_math(w_ref[...], g_ref[...], m_ref[...], v_ref[...])

    return pl.pallas_call(
        body, in_specs=[_full((r, c))] * 4, out_specs=[_full((r, c))] * 3, grid=(1,),
        out_shape=[jax.ShapeDtypeStruct((r, c), F32)] * 3,
        compiler_params=_params(("arbitrary",)), name="adamw_small")(g, w, m, v)


def _cols_from_slabs(g):
    return jnp.transpose(g, (1, 0, 2)).reshape(g.shape[1], N_DEV * g.shape[2])


def _slabs_from_cols(w):
    r, c8 = w.shape
    return jnp.transpose(w.reshape(r, N_DEV, c8 // N_DEV), (1, 0, 2))


def _rows_from_slabs(g):
    return g.reshape(N_DEV * g.shape[1], g.shape[2])


def _slabs_from_rows(w):
    return w.reshape(N_DEV, w.shape[0] // N_DEV, w.shape[1])


def _compute_layout(gathered, ql, kvl, heads, sw):
    out = {}
    for k, g in gathered.items():
        if k == "w_in":
            lat = ql + kvl + QK_ROPE
            w_in_t = _rows_from_slabs(g)
            out["w_lat_t"] = jnp.pad(w_in_t[:lat], ((0, LANES - QK_ROPE), (0, 0)))
            out["w_uv_t"] = w_in_t[lat:lat + 2 * sw]
            out["w_g_t"] = w_in_t[lat + 2 * sw:]
        elif k == "w_uq":
            per_head = _cols_from_slabs(g).reshape(ql, heads, QK_NOPE + QK_ROPE)
            pad = HEAD_PAD - QK_NOPE - QK_ROPE
            out["w_uq"] = jnp.pad(per_head, ((0, 0), (0, 0), (0, pad))).reshape(ql, heads * HEAD_PAD)
        elif k in ("w_o_attn", "w_out", "w_down_ffn"):
            out[k.removesuffix("_ffn")] = _rows_from_slabs(g)
        else:
            out[k.removesuffix("_ffn")] = _cols_from_slabs(g)
    return out


_SMALL =["norm_mix_g", "b_gate", "q_norm_g", "kv_norm_g", "sgu_norm_g", "w_sgu", "b_sgu", "norm_ffn_g", "norm_final_g"]
_BIG = ["w_in", "w_uq", "w_ukv", "w_o_attn", "w_o_sgu", "w_out", "w_gate_ffn", "w_up_ffn", "w_down_ffn"]
_TRANSPOSED = ("w_in", "w_gate_ffn", "w_up_ffn")
_ORDER = ["norm_mix_g", "w_in", "b_gate", "q_norm_g", "w_uq", "kv_norm_g", "w_ukv", "w_o_attn", "sgu_norm_g", "w_sgu",
          "b_sgu", "w_o_sgu", "w_out", "norm_ffn_g", "w_gate_ffn", "w_up_ffn", "w_down_ffn", "norm_final_g"]


def _pack_rows(parts):
    rows, sizes = [], []
    for p in parts:
        flat = p.reshape(-1)
        n = flat.shape[0]
        padded = -(-n // (SUBLANES * LANES)) * (SUBLANES * LANES)
        rows.append(jnp.pad(flat, (0, padded - n)).reshape(padded // LANES, LANES))
        sizes.append((n, padded // LANES))
    return jnp.concatenate(rows, axis=0), sizes


def _unpack_rows(pack, sizes, shapes):
    out, r0 = [], 0
    for (n, nr), shp in zip(sizes, shapes):
        out.append(pack[r0:r0 + nr].reshape(-1)[:n].reshape(shp))
        r0 += nr
    return out


def kernel(x, positions, norm_mix_g, w_in, b_gate, q_norm_g, w_uq, kv_norm_g, w_ukv, w_o_attn, sgu_norm_g, w_sgu, b_sgu, w_o_sgu, w_out, norm_ffn_g, w_gate_ffn, w_up_ffn, w_down_ffn, norm_final_g, loss_target, m_norm_mix_g, m_w_in, m_b_gate, m_q_norm_g, m_w_uq, m_kv_norm_g, m_w_ukv, m_w_o_attn, m_sgu_norm_g, m_w_sgu, m_b_sgu, m_w_o_sgu, m_w_out, m_norm_ffn_g, m_w_gate_ffn, m_w_up_ffn, m_w_down_ffn, m_norm_final_g, v_norm_mix_g, v_w_in, v_b_gate, v_q_norm_g, v_w_uq, v_kv_norm_g, v_w_ukv, v_w_o_attn, v_sgu_norm_g, v_w_sgu, v_b_sgu, v_w_o_sgu, v_w_out, v_norm_ffn_g, v_w_gate_ffn, v_w_up_ffn, v_w_down_ffn, v_norm_final_g):
    wts = dict(norm_mix_g=norm_mix_g, w_in=w_in, b_gate=b_gate, q_norm_g=q_norm_g, w_uq=w_uq, kv_norm_g=kv_norm_g,
               w_ukv=w_ukv, w_o_attn=w_o_attn, sgu_norm_g=sgu_norm_g, w_sgu=w_sgu, b_sgu=b_sgu, w_o_sgu=w_o_sgu,
               w_out=w_out, norm_ffn_g=norm_ffn_g, w_gate_ffn=w_gate_ffn, w_up_ffn=w_up_ffn, w_down_ffn=w_down_ffn,
               norm_final_g=norm_final_g)
    mom = dict(norm_mix_g=m_norm_mix_g, w_in=m_w_in, b_gate=m_b_gate, q_norm_g=m_q_norm_g, w_uq=m_w_uq,
               kv_norm_g=m_kv_norm_g, w_ukv=m_w_ukv, w_o_attn=m_w_o_attn, sgu_norm_g=m_sgu_norm_g, w_sgu=m_w_sgu,
               b_sgu=m_b_sgu, w_o_sgu=m_w_o_sgu, w_out=m_w_out, norm_ffn_g=m_norm_ffn_g, w_gate_ffn=m_w_gate_ffn,
               w_up_ffn=m_w_up_ffn, w_down_ffn=m_w_down_ffn, norm_final_g=m_norm_final_g)
    var = dict(norm_mix_g=v_norm_mix_g, w_in=v_w_in, b_gate=v_b_gate, q_norm_g=v_q_norm_g, w_uq=v_w_uq,
               kv_norm_g=v_kv_norm_g, w_ukv=v_w_ukv, w_o_attn=v_w_o_attn, sgu_norm_g=v_sgu_norm_g, w_sgu=v_w_sgu,
               b_sgu=v_b_sgu, w_o_sgu=v_w_o_sgu, w_out=v_w_out, norm_ffn_g=v_norm_ffn_g, w_gate_ffn=v_w_gate_ffn,
               w_up_ffn=v_w_up_ffn, w_down_ffn=v_w_down_ffn, norm_final_g=v_norm_final_g)

    t, d = x.shape[1], x.shape[2]
    ql, kvl = q_norm_g.shape[1], kv_norm_g.shape[1]
    heads = (w_uq.shape[2] * N_DEV) // (QK_NOPE + QK_ROPE)
    sw = sgu_norm_g.shape[1]

    def shard(a, k):
        return a[0].T if k in _TRANSPOSED else a[0]

    def unshard(a, k):
        return (a.T if k in _TRANSPOSED else a).reshape(wts[k].shape)

    opt = {k: (shard(wts[k], k), shard(mom[k], k), shard(var[k], k)) for k in _BIG}
    shards = {k: opt[k][0].astype(BF16) for k in _BIG}
    small = {
        "norm_mix_g": norm_mix_g, "b_gate": b_gate, "q_norm_g": q_norm_g, "kv_norm_g": kv_norm_g,
        "sgu_norm_g": sgu_norm_g, "w_sgu": w_sgu[0], "b_sgu_col": b_sgu[0][:, :, None], "norm_ffn_g": norm_ffn_g,
        "norm_final_g": norm_final_g[None, :],
    }

    loss_row, grad_x, gs, updates = _local_step(x[0], positions.reshape(t, 1), loss_target[0], small, shards, opt)
    grads, deltas, new_m, new_v = {}, {}, {}, {}
    for k in _BIG:
        grads[k], deltas[k], new_m[k], new_v[k] = (unshard(a, k) for a in updates[k])

    small_grads = [gs["norm_mix_g"], gs["b_gate"], gs["q_norm_g"], gs["kv_norm_g"], gs["sgu_norm_g"], gs["w_sgu"],
                   gs["b_sgu_col"], gs["norm_ffn_g"], gs["norm_final_g"]]
    pack, sizes = _pack_rows([loss_row] + small_grads)
    total = _all_reduce_pack(pack)
    shapes = [(1, LANES)] + [wts[k].shape for k in _SMALL]
    unpacked = _unpack_rows(total, sizes, shapes)
    loss = unpacked[0][0, 0]
    for k, g in zip(_SMALL, unpacked[1:]):
        grads[k] = g
    g_pack = total[sizes[0][1]:]
    w_pack, _ = _pack_rows([wts[k] for k in _SMALL])
    m_pack, _ = _pack_rows([mom[k] for k in _SMALL])
    v_pack, _ = _pack_rows([var[k] for k in _SMALL])
    d_pack, nm_pack, nv_pack = _adamw_pack(g_pack, w_pack, m_pack, v_pack)
    small_shapes = [wts[k].shape for k in _SMALL]
    for store, pk in ((deltas, d_pack), (new_m, nm_pack), (new_v, nv_pack)):
        for k, a in zip(_SMALL, _unpack_rows(pk, sizes[1:], small_shapes)):
            store[k] = a

    return (loss, grad_x[None], *[grads[k] for k in _ORDER], *[deltas[k] for k in _ORDER],
            *[new_m[k] for k in _ORDER], *[new_v[k] for k in _ORDER])
```

```python
import functools
import math

import jax
import jax.numpy as jnp
from jax import lax
from jax.experimental import pallas as pl
from jax.experimental.pallas import tpu as pltpu

F32 = jnp.float32
BF16 = jnp.bfloat16

N_DEV = 8
N_HEADS = 16
QK_NOPE = 128
QK_ROPE = 64
V_HEAD = 128
HEAD_PAD = 256
ROPE_THETA = 10000.0
CHUNK = 128
SGU_GROUP = 128
RMS_EPS = 1e-6
LANES = 128
SUBLANES = 8

ADAM_LR = 0.001
ADAM_B1 = 0.9
ADAM_B2 = 0.999
ADAM_EPS = 1e-08
ADAM_WD = 0.01
ADAM_STEP = 10

VMEM_LIMIT = 48 * 1024 * 1024
MM_TILE = (2048, 512, 2048)
MM_TILE_TA = (512, 2048)
ATTN_TILE = 512
HEADS_PER_STEP = (4, 4)
ROW_KERNEL_BYTES = 24 * 1024 * 1024
SHARD_TILE_ELEMS = 256 * 1024
SLABS_PER_STEP = 2
TAIL_SPLIT = (3, 8)
NEG_BIG = -1e30
MESH = pl.DeviceIdType.MESH


def _pick(n, target, mult=LANES):
    best = None
    d = mult
    while d <= min(n, target):
        if n % d == 0:
            best = d
        d += mult
    return best or n


def _row_tile(t, width, n_blocks, mult=2 * SUBLANES):
    return _pick(t, max(mult, ROW_KERNEL_BYTES // (3 * n_blocks * width * 4)), mult)


def _shard_tile(r, c, elems=SHARD_TILE_ELEMS, max_rows=256):
    tr = _pick(r, max_rows, 2 * SUBLANES)
    return tr, _pick(c, max(LANES, elems // tr))


def _params(sem):
    return pltpu.CompilerParams(dimension_semantics=sem, vmem_limit_bytes=VMEM_LIMIT)


def _full(shape):
    nd = len(shape)
    return pl.BlockSpec(shape, lambda *_: (0,) * nd)


def _rows(tr, w, cb=0):
    return pl.BlockSpec((tr, w), lambda i: (i, cb))


class _Comm:
    def __init__(self, ins, out_shapes, sems, start, finish, aliases=None):
        self.ins, self.out_shapes, self.sems, self.start, self.finish = list(ins), list(out_shapes), list(sems), start, finish
        self.aliases = dict(aliases or {})


def _call(body, *, grid, in_specs, out_specs, out_shape, scratch_shapes=(), sem, name, args, comm=None):
    if comm is None:
        outs = pl.pallas_call(body, grid=grid, in_specs=list(in_specs), out_specs=list(out_specs),
                              out_shape=list(out_shape), scratch_shapes=list(scratch_shapes),
                              compiler_params=_params(sem), name=name)(*args)
        return list(outs), []
    n_in, n_out, n_sc = len(in_specs), len(out_shape), len(scratch_shapes)
    nci, nco = len(comm.ins), len(comm.out_shapes)

    def hosted(*refs):
        ins, refs = refs[:n_in], refs[n_in:]
        cins, refs = refs[:nci], refs[nci:]
        outs, refs = refs[:n_out], refs[n_out:]
        couts, refs = refs[:nco], refs[nco:]
        scratch, csems = refs[:n_sc], refs[n_sc:]
        ids = [pl.program_id(i) for i in range(len(grid))]
        first = functools.reduce(jnp.logical_and, [i == 0 for i in ids])
        last = functools.reduce(jnp.logical_and, [i == g - 1 for i, g in zip(ids, grid)])

        @pl.when(first)
        def _():
            comm.start(cins, couts, csems)

        body(*ins, *outs, *scratch)

        @pl.when(last)
        def _():
            comm.finish(cins, couts, csems)

    any_spec = pl.BlockSpec(memory_space=pl.ANY)
    res = pl.pallas_call(
        hosted, grid=grid, in_specs=list(in_specs) + [any_spec] * nci, out_specs=list(out_specs) + [any_spec] * nco,
        out_shape=list(out_shape) + comm.out_shapes, scratch_shapes=list(scratch_shapes) + comm.sems,
        input_output_aliases={n_in + i: n_out + o for i, o in comm.aliases.items()},
        compiler_params=pltpu.CompilerParams(dimension_semantics=("arbitrary",) * len(grid),
                                             vmem_limit_bytes=VMEM_LIMIT, has_side_effects=True),
        name=name)(*args, *comm.ins)
    return list(res[:n_out]), list(res[n_out:])


def _swiglu_grads(g, u, d):
    s = 1.0 / (1.0 + jnp.exp(-g))
    return (d * u * (s * (1.0 + g * (1.0 - s)))).astype(BF16), (d * (g * s)).astype(BF16)


def _mm(a, b, *, ta=False, tb=False, add=None, out_dtype=F32, tm=None, tn=None, tk=None, name, comm=None,
        slab=None, a_slab0=0, swiglu=None, rope=None):
    sq = None
    if ta:
        tm, tn = tm or MM_TILE_TA[0], tn or MM_TILE_TA[1]
    if slab is None:
        m, k = (a.shape[1], a.shape[0]) if ta else a.shape
        n = b.shape[0] if tb else b.shape[1]
        assert k == (b.shape[1] if tb else b.shape[0]), (a.shape, b.shape, ta, tb)
        tm, tn, tk = _pick(m, tm or MM_TILE[0]), _pick(n, tn or MM_TILE[1]), _pick(k, tk or MM_TILE[2])
        if rope is not None:
            tn = _pick(n, max(tn, HEAD_PAD), HEAD_PAD)
        grid = (m // tm, n // tn, k // tk)
        a_spec = pl.BlockSpec((tk, tm), lambda i, j, kk: (kk, i)) if ta else pl.BlockSpec((tm, tk), lambda i, j, kk: (i, kk))
        b_spec = pl.BlockSpec((tn, tk), lambda i, j, kk: (j, kk)) if tb else pl.BlockSpec((tk, tn), lambda i, j, kk: (kk, j))
        o_spec, o_shape = pl.BlockSpec((tm, tn), lambda i, j, kk: (i, j)), (m, n)
    elif slab == "n":
        m, k = (a.shape[1], a.shape[0]) if ta else a.shape
        s, c = b.shape[0], (b.shape[1] if tb else b.shape[2])
        assert k == (b.shape[2] if tb else b.shape[1]), (a.shape, b.shape, ta, tb)
        tm, tn, tk = _pick(m, tm or MM_TILE[0]), c, _pick(k, tk or MM_TILE[2])
        grid = (m // tm, s, k // tk)
        a_spec = pl.BlockSpec((tk, tm), lambda i, j, kk: (kk, i)) if ta else pl.BlockSpec((tm, tk), lambda i, j, kk: (i, kk))
        b_spec = (pl.BlockSpec((sq, c, tk), lambda i, j, kk: (j, 0, kk)) if tb
                  else pl.BlockSpec((sq, tk, c), lambda i, j, kk: (j, kk, 0)))
        o_spec, o_shape = pl.BlockSpec((sq, tm, c), lambda i, j, kk: (j, i, 0)), (s, m, c)
    elif slab == "m":
        assert ta and not tb
        s, k, c = a.shape
        n = b.shape[1]
        assert k == b.shape[0], (a.shape, b.shape)
        tm, tn, tk = c, _pick(n, tn or MM_TILE[1]), _pick(k, tk or MM_TILE[2])
        grid = (s, n // tn, k // tk)
        a_spec = pl.BlockSpec((sq, tk, c), lambda i, j, kk: (i, kk, 0))
        b_spec = pl.BlockSpec((tk, tn), lambda i, j, kk: (kk, j))
        o_spec, o_shape = pl.BlockSpec((sq, c, tn), lambda i, j, kk: (i, 0, j)), (s, c, n)
    else:
        assert slab == "k" and not ta
        s, c = b.shape[0], (b.shape[2] if tb else b.shape[1])
        m, n = a.shape[1], (b.shape[1] if tb else b.shape[2])
        assert a.shape[2] == c and a.shape[0] >= a_slab0 + s, (a.shape, b.shape, a_slab0)
        tm, tn, tk = _pick(m, tm or MM_TILE[0]), _pick(n, tn or MM_TILE[1]), c
        per_step = SLABS_PER_STEP if (s % SLABS_PER_STEP == 0 and a_slab0 % SLABS_PER_STEP == 0) else 1
        first = a_slab0 // per_step
        grid = (m // tm, n // tn, s // per_step)
        a_spec = pl.BlockSpec((per_step, tm, c), lambda i, j, kk: (kk + first, i, 0))
        b_spec = (pl.BlockSpec((per_step, tn, c), lambda i, j, kk: (kk, j, 0)) if tb
                  else pl.BlockSpec((per_step, c, tn), lambda i, j, kk: (kk, 0, j)))
        o_spec, o_shape = pl.BlockSpec((tm, tn), lambda i, j, kk: (i, j)), (m, n)
    nk = grid[2]
    dims = (((0 if ta else 1,), (1 if tb else 0,)), ((), ()))

    def product(a_ref, b_ref):
        if slab != "k":
            return lax.dot_general(a_ref[...].astype(BF16), b_ref[...].astype(BF16), dims, preferred_element_type=F32)
        r = None
        for u in range(a_ref.shape[0]):
            p = lax.dot_general(a_ref[u].astype(BF16), b_ref[u].astype(BF16), dims, preferred_element_type=F32)
            r = p if r is None else r + p
        return r

    if swiglu is not None:
        assert slab == "n" and add is None
        o_block = pl.BlockSpec((2, sq, tm, c), lambda i, j, kk: (0, j, i, 0))
        o_shape, out_dtype = (2,) + o_shape, BF16

    if rope is not None:
        assert slab is None and add is None and swiglu is None and tn % HEAD_PAD == 0
        out_dtype = BF16
    extras = tuple(swiglu or ()) + tuple(rope or ())

    def body(*refs):
        a_ref, b_ref = refs[:2]
        add_ref = refs[2] if add is not None else None
        x0_ref, x1_ref = refs[2:4] if extras else (None, None)
        o_ref = refs[2 + (add is not None) + len(extras)]
        acc_ref = refs[-1] if nk > 1 else None

        def finish(r):
            if swiglu is not None:
                o_ref[0], o_ref[1] = _swiglu_grads(x0_ref[...], x1_ref[...], r)
                return
            if rope is not None:
                cos, sin = x0_ref[...], x1_ref[...]
                for h in range(tn // HEAD_PAD):
                    lo = h * HEAD_PAD
                    o_ref[:, lo:lo + QK_NOPE] = r[:, lo:lo + QK_NOPE].astype(BF16)
                    o_ref[:, lo + QK_NOPE:lo + HEAD_PAD] = _rope(r[:, lo + QK_NOPE:lo + HEAD_PAD], cos, sin).astype(BF16)
                return
            if add_ref is not None:
                r = r + add_ref[...].astype(F32)
            o_ref[...] = r.astype(o_ref.dtype)

        if nk == 1:
            finish(product(a_ref, b_ref))
            return
        kk = pl.program_id(2)

        @pl.when(kk == 0)
        def _():
            acc_ref[...] = product(a_ref, b_ref)

        if nk > 2:
            @pl.when(jnp.logical_and(kk > 0, kk < nk - 1))
            def _():
                acc_ref[...] += product(a_ref, b_ref)

        @pl.when(kk == nk - 1)
        def _():
            finish(acc_ref[...] + product(a_ref, b_ref))

    in_specs = [a_spec, b_spec] + ([o_spec] if add is not None else []) + ([o_spec] * 2 if swiglu is not None else [])
    if rope is not None:
        in_specs += [pl.BlockSpec((tm, LANES), lambda i, j, kk: (i, 0))] * 2
    args = (a, b) + ((add,) if add is not None else ()) + extras
    if swiglu is not None:
        o_spec = o_block
    outs, comm_outs = _call(
        body, grid=grid, in_specs=in_specs, out_specs=[o_spec],
        out_shape=[jax.ShapeDtypeStruct(o_shape, out_dtype)],
        scratch_shapes=[pltpu.VMEM((tm, tn), F32)] if nk > 1 else [],
        sem=("parallel", "parallel", "arbitrary"), name=name, args=args, comm=comm)
    return outs[0] if comm is None else (outs[0], comm_outs)


def _rms_scale(x):
    return lax.rsqrt(jnp.mean(x * x, axis=-1, keepdims=True) + RMS_EPS)


def _rms_bwd(xhat, r, g, dy):
    t = dy * g
    dx = r * (t - xhat * jnp.mean(t * xhat, axis=-1, keepdims=True))
    return dx, dy * xhat


_GELU_C = math.sqrt(2.0 / math.pi)


def _gelu(x):
    return x * (0.5 * (1.0 + jnp.tanh(_GELU_C * (x + 0.044715 * (x * x * x)))))


def _gelu_and_grad(x):
    t = jnp.tanh(_GELU_C * (x + 0.044715 * (x * x * x)))
    cdf = 0.5 * (1.0 + t)
    return x * cdf, cdf + x * (0.5 * (1.0 - t * t) * (_GELU_C * (1.0 + 3.0 * 0.044715 * (x * x))))


def _sigmoid(x):
    return 1.0 / (1.0 + jnp.exp(-x))


def _swap_halves(x):
    lane = lax.broadcasted_iota(jnp.int32, x.shape, 1)
    first = (lane % QK_ROPE) < (QK_ROPE // 2)
    return jnp.where(first, pltpu.roll(x, LANES - QK_ROPE // 2, 1), pltpu.roll(x, QK_ROPE // 2, 1))


def _rope(x, cos, sin_signed):
    return x * cos + _swap_halves(x) * sin_signed


def _rope_bwd(d, cos, sin_signed):
    return d * cos + _swap_halves(d * sin_signed)


def _rope_tables(pos_col, inv_freq_row, sign_row):
    t = pos_col.shape[0]
    tr = _pick(t, 512, SUBLANES)

    def body(p_ref, f_ref, s_ref, cos_ref, sin_ref):
        ang = p_ref[...].astype(F32) * f_ref[...]
        cos_ref[...] = jnp.cos(ang)
        sin_ref[...] = jnp.sin(ang) * s_ref[...]

    return pl.pallas_call(
        body, grid=(t // tr,), in_specs=[_rows(tr, 1), _full((1, LANES)), _full((1, LANES))],
        out_specs=[_rows(tr, LANES), _rows(tr, LANES)],
        out_shape=[jax.ShapeDtypeStruct((t, LANES), F32)] * 2,
        compiler_params=_params(("parallel",)), name="rope_tables")(pos_col, inv_freq_row, sign_row)


def _norm_fwd(x, g, name):
    t, d = x.shape
    tr = _row_tile(t, d, 2)

    def body(x_ref, g_ref, y_ref):
        xv = x_ref[...]
        y_ref[...] = (xv * _rms_scale(xv) * g_ref[...]).astype(BF16)

    return pl.pallas_call(
        body, grid=(t // tr,), in_specs=[_rows(tr, d), _full((1, d))], out_specs=_rows(tr, d),
        out_shape=jax.ShapeDtypeStruct((t, d), BF16), compiler_params=_params(("parallel",)), name=name)(x, g)


def _lat_fwd(z_lat, qg, kvg, cos, sin, ql, kvl):
    t = z_lat.shape[0]
    tr = _row_tile(t, z_lat.shape[1], 2)

    def body(z_ref, qg_ref, kvg_ref, cos_ref, sin_ref, qn_ref, kvn_ref, kpe_ref):
        q = z_ref[:, 0:ql]
        qn_ref[...] = (q * _rms_scale(q) * qg_ref[...]).astype(BF16)
        kv = z_ref[:, ql:ql + kvl]
        kvn_ref[...] = (kv * _rms_scale(kv) * kvg_ref[...]).astype(BF16)
        kpe_ref[...] = _rope(z_ref[:, ql + kvl:ql + kvl + LANES], cos_ref[...], sin_ref[...]).astype(BF16)

    w = z_lat.shape[1]
    return pl.pallas_call(
        body, grid=(t // tr,),
        in_specs=[_rows(tr, w), _full((1, ql)), _full((1, kvl)), _rows(tr, LANES), _rows(tr, LANES)],
        out_specs=[_rows(tr, ql), _rows(tr, kvl), _rows(tr, LANES)],
        out_shape=[jax.ShapeDtypeStruct((t, ql), BF16), jax.ShapeDtypeStruct((t, kvl), BF16),
                   jax.ShapeDtypeStruct((t, LANES), BF16)],
        compiler_params=_params(("parallel",)), name="lat_fwd")(z_lat, qg, kvg, cos, sin)


def _tril_mask():
    r = lax.broadcasted_iota(jnp.int32, (CHUNK, CHUNK), 0)
    c = lax.broadcasted_iota(jnp.int32, (CHUNK, CHUNK), 1)
    return r >= c


def _sgu_fwd(z_uv, gs, ws, b_col):
    t = z_uv.shape[0]
    sw = z_uv.shape[1] // 2
    groups = sw // SGU_GROUP
    tr = _pick(t, 256, CHUNK)

    def body(u_ref, v_ref, gs_ref, ws_ref, b_ref, o_ref):
        v = _gelu(v_ref[...])
        vn = (v * _rms_scale(v) * gs_ref[...]).astype(BF16)
        tri = _tril_mask()
        for g in range(groups):
            wg = jnp.where(tri, ws_ref[g], 0.0).astype(BF16)
            cols = slice(g * SGU_GROUP, (g + 1) * SGU_GROUP)
            for c in range(tr // CHUNK):
                rows = slice(c * CHUNK, (c + 1) * CHUNK)
                mixed = jnp.dot(wg, vn[rows, cols], preferred_element_type=F32) + b_ref[g]
                o_ref[rows, cols] = (_gelu(u_ref[rows, cols]) * mixed).astype(BF16)

    return pl.pallas_call(
        body, grid=(t // tr,),
        in_specs=[_rows(tr, sw, 0), _rows(tr, sw, 1), _full((1, sw)), _full(ws.shape), _full(b_col.shape)],
        out_specs=_rows(tr, sw), out_shape=jax.ShapeDtypeStruct((t, sw), BF16),
        compiler_params=_params(("parallel",)), name="sgu_fwd")(z_uv, z_uv, gs, ws, b_col)


def _merge_fwd(y_attn, y_sgu, z_g, b_gate, comm=None):
    t, d = y_attn.shape
    tr = _row_tile(t, d, 5)

    def body(ya_ref, ys_ref, g0_ref, g1_ref, b0_ref, b1_ref, o_ref):
        g0 = _sigmoid(g0_ref[...] + b0_ref[...])
        g1 = _sigmoid(g1_ref[...] + b1_ref[...])
        o_ref[...] = (g0 * ya_ref[...] + g1 * ys_ref[...]).astype(BF16)

    bspec0 = pl.BlockSpec((1, d), lambda i: (0, 0))
    bspec1 = pl.BlockSpec((1, d), lambda i: (0, 1))
    outs, comm_outs = _call(
        body, grid=(t // tr,),
        in_specs=[_rows(tr, d), _rows(tr, d), _rows(tr, d, 0), _rows(tr, d, 1), bspec0, bspec1],
        out_specs=[_rows(tr, d)], out_shape=[jax.ShapeDtypeStruct((t, d), BF16)],
        sem=("parallel",), name="merge_fwd", args=(y_attn, y_sgu, z_g, z_g, b_gate, b_gate), comm=comm)
    return outs[0], comm_outs


def _swiglu_fwd(gate, up, comm=None):
    t, f = gate.shape
    tr = _row_tile(t, f, 3)

    def body(g_ref, u_ref, o_ref):
        g = g_ref[...]
        o_ref[...] = (g * _sigmoid(g) * u_ref[...]).astype(BF16)

    outs, comm_outs = _call(
        body, grid=(t // tr,), in_specs=[_rows(tr, f), _rows(tr, f)], out_specs=[_rows(tr, f)],
        out_shape=[jax.ShapeDtypeStruct((t, f), BF16)], sem=("parallel",), name="swiglu_fwd", args=(gate, up), comm=comm)
    return outs[0], comm_outs


def _loss_head(h2, g, target):
    t, d = h2.shape
    tr = _row_tile(t, d, 3)

    def body(h_ref, g_ref, t_ref, loss_ref, dh_ref, dhb_ref, dg_ref):
        @pl.when(pl.program_id(0) == 0)
        def _():
            loss_ref[...] = jnp.zeros_like(loss_ref)
            dg_ref[...] = jnp.zeros_like(dg_ref)

        h = h_ref[...]
        r = _rms_scale(h)
        hhat = h * r
        gv = g_ref[...]
        err = hhat * gv - t_ref[...]
        loss_ref[...] += jnp.full(loss_ref.shape, 0.5 * jnp.sum(jnp.mean(err * err, axis=-1)), F32)
        dx, dg_rows = _rms_bwd(hhat, r, gv, err * (1.0 / d))
        dh_ref[...] = dx
        dhb_ref[...] = dx.astype(BF16)
        dg_ref[...] += jnp.sum(dg_rows, axis=0, keepdims=True)

    return pl.pallas_call(
        body, grid=(t // tr,), in_specs=[_rows(tr, d), _full((1, d)), _rows(tr, d)],
        out_specs=[_full((1, LANES)), _rows(tr, d), _rows(tr, d), _full((1, d))],
        out_shape=[jax.ShapeDtypeStruct((1, LANES), F32), jax.ShapeDtypeStruct((t, d), F32),
                   jax.ShapeDtypeStruct((t, d), BF16), jax.ShapeDtypeStruct((1, d), F32)],
        compiler_params=_params(("arbitrary",)), name="loss_head")(h2, g, target)


def _norm_bwd(x, g, dy, resid, name, comm=None):
    t, d = x.shape
    tr = _row_tile(t, d, 5)

    def body(x_ref, g_ref, dy_ref, r_ref, dx_ref, dxb_ref, dg_ref):
        @pl.when(pl.program_id(0) == 0)
        def _():
            dg_ref[...] = jnp.zeros_like(dg_ref)

        xv = x_ref[...]
        r = _rms_scale(xv)
        dx, dg_rows = _rms_bwd(xv * r, r, g_ref[...], dy_ref[...])
        dx = r_ref[...] + dx
        dx_ref[...] = dx
        dxb_ref[...] = dx.astype(BF16)
        dg_ref[...] += jnp.sum(dg_rows, axis=0, keepdims=True)

    outs, comm_outs = _call(
        body, grid=(t // tr,), in_specs=[_rows(tr, d), _full((1, d)), _rows(tr, d), _rows(tr, d)],
        out_specs=[_rows(tr, d), _rows(tr, d), _full((1, d))],
        out_shape=[jax.ShapeDtypeStruct((t, d), F32), jax.ShapeDtypeStruct((t, d), BF16),
                   jax.ShapeDtypeStruct((1, d), F32)],
        sem=("arbitrary",), name=name, args=(x, g, dy, resid), comm=comm)
    return (outs[0], outs[1], outs[2]) if comm is None else (outs[0], outs[1], outs[2], comm_outs)


def _merge_bwd(dmerged, y_attn, y_sgu, z_g, b_gate):
    t, d = y_attn.shape
    tr = _row_tile(t, d, 7)

    def body(dm_ref, ya_ref, ys_ref, g0_ref, g1_ref, b0_ref, b1_ref, dya_ref, dys_ref, dz_ref, db_ref):
        @pl.when(pl.program_id(0) == 0)
        def _():
            db_ref[...] = jnp.zeros_like(db_ref)

        dm = dm_ref[...]
        g0 = _sigmoid(g0_ref[...] + b0_ref[...])
        g1 = _sigmoid(g1_ref[...] + b1_ref[...])
        dya_ref[...] = (dm * g0).astype(BF16)
        dys_ref[...] = (dm * g1).astype(BF16)
        dl0 = dm * ya_ref[...] * (g0 * (1.0 - g0))
        dl1 = dm * ys_ref[...] * (g1 * (1.0 - g1))
        dz_ref[:, 0:d] = dl0.astype(BF16)
        dz_ref[:, d:2 * d] = dl1.astype(BF16)
        db_ref[:, 0:d] += jnp.sum(dl0, axis=0, keepdims=True)
        db_ref[:, d:2 * d] += jnp.sum(dl1, axis=0, keepdims=True)

    bspec0 = pl.BlockSpec((1, d), lambda i: (0, 0))
    bspec1 = pl.BlockSpec((1, d), lambda i: (0, 1))
    return pl.pallas_call(
        body, grid=(t // tr,),
        in_specs=[_rows(tr, d), _rows(tr, d), _rows(tr, d), _rows(tr, d, 0), _rows(tr, d, 1), bspec0, bspec1],
        out_specs=[_rows(tr, d), _rows(tr, d), _rows(tr, 2 * d), _full((1, 2 * d))],
        out_shape=[jax.ShapeDtypeStruct((t, d), BF16), jax.ShapeDtypeStruct((t, d), BF16),
                   jax.ShapeDtypeStruct((t, 2 * d), BF16), jax.ShapeDtypeStruct((1, 2 * d), F32)],
        compiler_params=_params(("arbitrary",)), name="merge_bwd")(dmerged, y_attn, y_sgu, z_g, z_g, b_gate, b_gate)


def _sgu_bwd(z_uv, ds_out, gs, ws, b_col):
    t = z_uv.shape[0]
    sw = z_uv.shape[1] // 2
    groups = sw // SGU_GROUP
    tr = _pick(t, 256, CHUNK)

    def body(u_ref, v_ref, d_ref, gs_ref, ws_ref, b_ref, dz_ref, dws_ref, db_ref, dgs_ref, dvn_ref):
        @pl.when(pl.program_id(0) == 0)
        def _():
            dws_ref[...] = jnp.zeros_like(dws_ref)
            db_ref[...] = jnp.zeros_like(db_ref)
            dgs_ref[...] = jnp.zeros_like(dgs_ref)

        v, dgelu_v = _gelu_and_grad(v_ref[...])
        r = _rms_scale(v)
        vhat = v * r
        gsv = gs_ref[...]
        vn = (vhat * gsv).astype(BF16)
        tri = _tril_mask()
        for g in range(groups):
            wg = jnp.where(tri, ws_ref[g], 0.0).astype(BF16)
            cols = slice(g * SGU_GROUP, (g + 1) * SGU_GROUP)
            for c in range(tr // CHUNK):
                rows = slice(c * CHUNK, (c + 1) * CHUNK)
                vn_cg = vn[rows, cols]
                mixed = jnp.dot(wg, vn_cg, preferred_element_type=F32) + b_ref[g]
                u, dgelu_u = _gelu_and_grad(u_ref[rows, cols])
                dso = d_ref[rows, cols]
                dz_ref[rows, cols] = (dso * mixed * dgelu_u).astype(BF16)
                dmixed = dso * u
                db_ref[g] += jnp.sum(dmixed, axis=1, keepdims=True)
                dmixed_b = dmixed.astype(BF16)
                dws_ref[g] += jnp.where(
                    tri, lax.dot_general(dmixed_b, vn_cg, (((1,), (1,)), ((), ())), preferred_element_type=F32), 0.0)
                dvn_ref[rows, cols] = lax.dot_general(wg, dmixed_b, (((0,), (0,)), ((), ())), preferred_element_type=F32)
        dvn = dvn_ref[...]
        dv, dgs_rows = _rms_bwd(vhat, r, gsv, dvn)
        dz_ref[:, sw:2 * sw] = (dv * dgelu_v).astype(BF16)
        dgs_ref[...] += jnp.sum(dgs_rows, axis=0, keepdims=True)

    return pl.pallas_call(
        body, grid=(t // tr,),
        in_specs=[_rows(tr, sw, 0), _rows(tr, sw, 1), _rows(tr, sw), _full((1, sw)), _full(ws.shape), _full(b_col.shape)],
        out_specs=[_rows(tr, 2 * sw), _full(ws.shape), _full(b_col.shape), _full((1, sw))],
        out_shape=[jax.ShapeDtypeStruct((t, 2 * sw), BF16), jax.ShapeDtypeStruct(ws.shape, F32),
                   jax.ShapeDtypeStruct(b_col.shape, F32), jax.ShapeDtypeStruct((1, sw), F32)],
        scratch_shapes=[pltpu.VMEM((tr, sw), F32)],
        compiler_params=_params(("arbitrary",)), name="sgu_bwd")(z_uv, z_uv, ds_out, gs, ws, b_col)


def _lat_bwd(z_lat, qg, kvg, dqn, dkvn, dkpe_heads, cos, sin, ql, kvl):
    t, w = z_lat.shape
    heads = dkpe_heads.shape[0]
    tr = _row_tile(t, w + heads * LANES, 3)

    def body(z_ref, qg_ref, kvg_ref, dq_ref, dkv_ref, dk_ref, cos_ref, sin_ref, dz_ref, dqg_ref, dkvg_ref):
        @pl.when(pl.program_id(0) == 0)
        def _():
            dqg_ref[...] = jnp.zeros_like(dqg_ref)
            dkvg_ref[...] = jnp.zeros_like(dkvg_ref)

        q = z_ref[:, 0:ql]
        r = _rms_scale(q)
        dx, dg_rows = _rms_bwd(q * r, r, qg_ref[...], dq_ref[...])
        dz_ref[:, 0:ql] = dx.astype(BF16)
        dqg_ref[...] += jnp.sum(dg_rows, axis=0, keepdims=True)
        kv = z_ref[:, ql:ql + kvl]
        r = _rms_scale(kv)
        dx, dg_rows = _rms_bwd(kv * r, r, kvg_ref[...], dkv_ref[...])
        dz_ref[:, ql:ql + kvl] = dx.astype(BF16)
        dkvg_ref[...] += jnp.sum(dg_rows, axis=0, keepdims=True)
        dk = dk_ref[0]
        for h in range(1, heads):
            dk = dk + dk_ref[h]
        dz_ref[:, ql + kvl:ql + kvl + LANES] = _rope_bwd(dk, cos_ref[...], sin_ref[...]).astype(BF16)

    return pl.pallas_call(
        body, grid=(t // tr,),
        in_specs=[_rows(tr, w), _full((1, ql)), _full((1, kvl)), _rows(tr, ql), _rows(tr, kvl),
                  pl.BlockSpec((heads, tr, LANES), lambda i: (0, i, 0)), _rows(tr, LANES), _rows(tr, LANES)],
        out_specs=[_rows(tr, w), _full((1, ql)), _full((1, kvl))],
        out_shape=[jax.ShapeDtypeStruct((t, w), BF16), jax.ShapeDtypeStruct((1, ql), F32),
                   jax.ShapeDtypeStruct((1, kvl), F32)],
        compiler_params=_params(("arbitrary",)), name="lat_bwd")(z_lat, qg, kvg, dqn, dkvn, dkpe_heads, cos, sin)


_NT = (((1,), (1,)), ((), ()))


def _attn_scale():
    return (QK_NOPE + QK_ROPE) ** -0.5


def _heads_per_step(heads, wanted):
    return wanted if heads % wanted == 0 else 1


def _attn_fwd(q_c, kv, kpe, comm=None):
    t = q_c.shape[0]
    heads = q_c.shape[1] // HEAD_PAD
    tq = _pick(t, ATTN_TILE)
    nq = t // tq
    scale = _attn_scale()
    to_log2 = scale * math.log2(math.e)
    tn_dims = (((0,), (0,)), ((), ()))

    hps = _heads_per_step(heads, HEADS_PER_STEP[0])

    def body(q_ref, kv_ref, kpe_ref, o_ref, ob_ref, lse_ref, m_sc, l_sc, acc_sc):
        qi, ki = pl.program_id(1), pl.program_id(2)

        @pl.when(ki == 0)
        def _():
            m_sc[...] = jnp.full_like(m_sc, NEG_BIG)
            l_sc[...] = jnp.zeros_like(l_sc)
            acc_sc[...] = jnp.zeros_like(acc_sc)

        def step(diagonal):
            for u in range(hps):
                lo = u * HEAD_PAD
                kc = jnp.concatenate([kv_ref[:, lo:lo + QK_NOPE], kpe_ref[...]], axis=1)
                st = lax.dot_general(kc, q_ref[:, lo:lo + HEAD_PAD], _NT, preferred_element_type=F32)
                if diagonal:
                    krow = lax.broadcasted_iota(jnp.int32, st.shape, 0)
                    qcol = lax.broadcasted_iota(jnp.int32, st.shape, 1)
                    st = jnp.where(qcol >= krow, st, NEG_BIG)
                m_prev = m_sc[u]
                m_new = jnp.maximum(m_prev, jnp.max(st, axis=0, keepdims=True))
                alpha = jnp.exp2((m_prev - m_new) * to_log2)
                pt = jnp.exp2((st - m_new) * to_log2)
                l_sc[u] = alpha * l_sc[u] + jnp.sum(pt, axis=0, keepdims=True)
                acc_sc[u] = alpha * acc_sc[u] + lax.dot_general(
                    kv_ref[:, lo + QK_NOPE:lo + HEAD_PAD], pt.astype(BF16), tn_dims, preferred_element_type=F32)
                m_sc[u] = m_new

        @pl.when(ki < qi)
        def _():
            step(False)

        @pl.when(ki == qi)
        def _():
            step(True)
            for u in range(hps):
                o = (acc_sc[u] / l_sc[u]).T
                o_ref[:, u * V_HEAD:(u + 1) * V_HEAD] = o
                ob_ref[:, u * V_HEAD:(u + 1) * V_HEAD] = o.astype(BF16)
                lse_ref[u] = m_sc[u] * scale + jnp.log(l_sc[u])

    omap = lambda g, qi, ki: (qi, g)
    outs, comm_outs = _call(
        body, grid=(heads // hps, nq, nq),
        in_specs=[pl.BlockSpec((tq, hps * HEAD_PAD), omap),
                  pl.BlockSpec((tq, hps * HEAD_PAD), lambda g, qi, ki: (jnp.minimum(ki, qi), g)),
                  pl.BlockSpec((tq, LANES), lambda g, qi, ki: (jnp.minimum(ki, qi), 0))],
        out_specs=[pl.BlockSpec((tq, hps * V_HEAD), omap), pl.BlockSpec((tq, hps * V_HEAD), omap),
                   pl.BlockSpec((hps, 1, tq), lambda g, qi, ki: (g, 0, qi))],
        out_shape=[jax.ShapeDtypeStruct((t, heads * V_HEAD), F32), jax.ShapeDtypeStruct((t, heads * V_HEAD), BF16),
                   jax.ShapeDtypeStruct((heads, 1, t), F32)],
        scratch_shapes=[pltpu.VMEM((hps, 1, tq), F32), pltpu.VMEM((hps, 1, tq), F32),
                        pltpu.VMEM((hps, V_HEAD, tq), F32)],
        sem=("parallel", "parallel", "arbitrary"), name="attn_fwd", args=(q_c, kv, kpe), comm=comm)
    return outs[0], outs[1], outs[2], comm_outs


def _attn_bwd(q_c, kv, kpe, o, do, lse_row, cos, sin, comm=None):
    t = q_c.shape[0]
    heads = q_c.shape[1] // HEAD_PAD
    tk = _pick(t, ATTN_TILE)
    nk = t // tk
    scale = _attn_scale()
    tn_dims = (((0,), (0,)), ((), ()))

    hps = _heads_per_step(heads, HEADS_PER_STEP[1])

    def body(q_ref, kv_ref, kpe_ref, do_ref, lse_ref, o_ref, cos_ref, sin_ref, dq_ref, dkv_ref, dkpe_ref,
             dk_sc, dv_sc, delta_sc, dq_sc):
        ki, qi = pl.program_id(1), pl.program_id(2)

        @pl.when(jnp.logical_and(ki == 0, qi == 0))
        def _():
            dq_sc[...] = jnp.zeros_like(dq_sc)

        @pl.when(qi == 0)
        def _():
            dk_sc[...] = jnp.zeros_like(dk_sc)
            dv_sc[...] = jnp.zeros_like(dv_sc)

        @pl.when(ki == 0)
        def _():
            for u in range(hps):
                cols = slice(u * V_HEAD, (u + 1) * V_HEAD)
                delta_sc[qi * hps + u] = jnp.sum((do_ref[:, cols] * o_ref[:, cols]).T, axis=0, keepdims=True)

        def step(diagonal):
            for u in range(hps):
                lo = u * HEAD_PAD
                kc = jnp.concatenate([kv_ref[:, lo:lo + QK_NOPE], kpe_ref[...]], axis=1)
                q = q_ref[:, lo:lo + HEAD_PAD]
                st = lax.dot_general(kc, q, _NT, preferred_element_type=F32) * scale
                pt = jnp.exp(st - lse_ref[u])
                if diagonal:
                    krow = lax.broadcasted_iota(jnp.int32, st.shape, 0)
                    qcol = lax.broadcasted_iota(jnp.int32, st.shape, 1)
                    pt = jnp.where(qcol >= krow, pt, 0.0)
                do_b = do_ref[:, u * V_HEAD:(u + 1) * V_HEAD].astype(BF16)
                dv_sc[u] += jnp.dot(pt.astype(BF16), do_b, preferred_element_type=F32)
                dpt = lax.dot_general(kv_ref[:, lo + QK_NOPE:lo + HEAD_PAD], do_b, _NT, preferred_element_type=F32)
                dst = (pt * (dpt - delta_sc[qi * hps + u]) * scale).astype(BF16)
                dk_sc[u] += jnp.dot(dst, q, preferred_element_type=F32)
                rows = pl.ds(pl.multiple_of(qi * tk, tk), tk)
                dq_sc[rows, lo:lo + HEAD_PAD] += lax.dot_general(dst, kc, tn_dims, preferred_element_type=F32)

        @pl.when(qi > ki)
        def _():
            step(False)

        @pl.when(qi == ki)
        def _():
            step(True)

        @pl.when(qi == nk - 1)
        def _():
            for u in range(hps):
                lo = u * HEAD_PAD
                dkv_ref[:, lo:lo + QK_NOPE] = dk_sc[u, :, 0:QK_NOPE].astype(BF16)
                dkv_ref[:, lo + QK_NOPE:lo + HEAD_PAD] = dv_sc[u].astype(BF16)
                dkpe_ref[u] = dk_sc[u, :, QK_NOPE:QK_NOPE + LANES]

        @pl.when(jnp.logical_and(ki == nk - 1, qi == nk - 1))
        def _():
            cos, sin = cos_ref[...], sin_ref[...]
            for u in range(hps):
                lo = u * HEAD_PAD
                dq_ref[:, lo:lo + QK_NOPE] = dq_sc[:, lo:lo + QK_NOPE].astype(BF16)
                dq_ref[:, lo + QK_NOPE:lo + HEAD_PAD] = _rope_bwd(dq_sc[:, lo + QK_NOPE:lo + HEAD_PAD], cos, sin).astype(BF16)

    qclamp = lambda g, ki, qi: (jnp.maximum(qi, ki), g)
    outs, comm_outs = _call(
        body, grid=(heads // hps, nk, nk),
        in_specs=[pl.BlockSpec((tk, hps * HEAD_PAD), qclamp),
                  pl.BlockSpec((tk, hps * HEAD_PAD), lambda g, ki, qi: (ki, g)),
                  pl.BlockSpec((tk, LANES), lambda g, ki, qi: (ki, 0)),
                  pl.BlockSpec((tk, hps * V_HEAD), qclamp),
                  pl.BlockSpec((hps, 1, tk), lambda g, ki, qi: (g, 0, jnp.maximum(qi, ki))),
                  pl.BlockSpec((tk, hps * V_HEAD), lambda g, ki, qi: (jnp.where(ki == 0, qi, 0), g)),
                  _full((t, LANES)), _full((t, LANES))],
        out_specs=[pl.BlockSpec((t, hps * HEAD_PAD), lambda g, ki, qi: (0, g)),
                   pl.BlockSpec((tk, hps * HEAD_PAD), lambda g, ki, qi: (ki, g)),
                   pl.BlockSpec((hps, tk, LANES), lambda g, ki, qi: (g, ki, 0))],
        out_shape=[jax.ShapeDtypeStruct((t, heads * HEAD_PAD), BF16),
                   jax.ShapeDtypeStruct((t, heads * HEAD_PAD), BF16), jax.ShapeDtypeStruct((heads, t, LANES), F32)],
        scratch_shapes=[pltpu.VMEM((hps, tk, HEAD_PAD), F32), pltpu.VMEM((hps, tk, V_HEAD), F32),
                        pltpu.VMEM((nk * hps, 1, tk), F32), pltpu.VMEM((t, hps * HEAD_PAD), F32)],
        sem=("parallel", "arbitrary", "arbitrary"), name="attn_bwd",
        args=(q_c, kv, kpe, do, lse_row, o, cos, sin), comm=comm)
    return outs[0], outs[1], outs[2], comm_outs


def _local_step(x, pos_col, target, small, shards, opt):
    t = x.shape[0]
    ql, kvl = small["q_norm_g"].shape[1], small["kv_norm_g"].shape[1]
    sw = small["sgu_norm_g"].shape[1]
    heads = (shards["w_uq"].shape[1] * N_DEV) // (QK_NOPE + QK_ROPE)
    big = {}
    early = ["w_in", "w_uq", "w_ukv"]
    big.update(_compute_layout(dict(zip(early, _all_gather([shards[k] for k in early]))), ql, kvl, heads, sw))
    half = QK_ROPE // 2
    lane = jnp.arange(LANES)
    inv_freq = ROPE_THETA ** (-jnp.arange(0, QK_ROPE, 2, dtype=F32) / QK_ROPE)
    inv_row = inv_freq[lane % half][None, :]
    sign_row = jnp.where((lane % QK_ROPE) < half, -1.0, 1.0).astype(F32)[None, :]
    cos, sin = _rope_tables(pos_col, inv_row, sign_row)
    ws = small["w_sgu"]
    b_col = small["b_sgu_col"]

    def arrived(names, bufs):
        big.update(_compute_layout(dict(zip(names, bufs)), ql, kvl, heads, sw))

    a = _norm_fwd(x, small["norm_mix_g"], "norm_mix_fwd")
    z_lat = _mm(a, big["w_lat_t"], tb=True, name="z_lat")
    z_uv, g_sgu = _mm(a, big["w_uv_t"], tb=True, name="z_uv", comm=_gather_stage(1, [shards["w_o_sgu"]]))
    z_g, (g_attn, g_sgu) = _mm(a, big["w_g_t"], tb=True, name="z_g",
                               comm=_join(_gather_stage(1, [shards["w_o_attn"]]), _gather_stage(2, g_sgu)))
    qn, kvn, kpe = _lat_fwd(z_lat, small["q_norm_g"], small["kv_norm_g"], cos, sin, ql, kvl)
    q_c, (g_attn, g_sgu) = _mm(qn, big["w_uq"], name="q_up_rope", rope=(cos, sin),
                               comm=_join(_gather_stage(2, [g_attn]), _gather_stage(3, [g_sgu])))
    kv, (g_attn, g_out) = _mm(kvn, big["w_ukv"], out_dtype=BF16, name="kv_up",
                              comm=_join(_gather_stage(3, [g_attn]), _gather_stage(1, [shards["w_out"]])))
    arrived(["w_o_sgu", "w_o_attn"], [g_sgu, g_attn])
    attn, attn_b, lse, (w_gate, w_up) = _attn_fwd(
        q_c, kv, kpe, comm=_gather_stage(1, [shards["w_gate_ffn"], shards["w_up_ffn"]]))
    s_out = _sgu_fwd(z_uv, small["sgu_norm_g"], ws, b_col)
    y_sgu, (g_out,) = _mm(s_out, big["w_o_sgu"], name="y_sgu", comm=_gather_stage(2, [g_out]))
    y_attn, (w_gate, g_out) = _mm(attn_b, big["w_o_attn"], name="y_attn",
                                  comm=_join(_gather_stage(2, [w_gate]), _gather_stage(3, [g_out])))
    arrived(["w_out"], [g_out])
    merged, (w_up, w_gate) = _merge_fwd(y_attn, y_sgu, z_g, small["b_gate"],
                                        comm=_join(_gather_stage(2, [w_up]), _gather_stage(3, [w_gate])))
    h1, (w_up,) = _mm(merged, big["w_out"], add=x, name="h1", comm=_gather_stage(3, [w_up]))
    f = _norm_fwd(h1, small["norm_ffn_g"], "norm_ffn_fwd")
    gate, w_down = _mm(f, w_gate, tb=True, slab="n", name="ffn_gate", comm=_gather_stage(1, [shards["w_down_ffn"]]))
    up, w_down = _mm(f, w_up, tb=True, slab="n", name="ffn_up", comm=_gather_stage(2, w_down))
    ffn = gate.shape[2]
    gate, up = gate.reshape(N_DEV * t, ffn), up.reshape(N_DEV * t, ffn)
    act, (w_down,) = _swiglu_fwd(gate, up, comm=_gather_stage(3, w_down))
    act = act.reshape(N_DEV, t, ffn)
    h2 = _mm(act, w_down, slab="k", add=h1, name="h2")
    loss_row, dh2, dh2_b, d_norm_final = _loss_head(h2, small["norm_final_g"], target)

    def pair_sums(names, slabs, bufs):
        return [_pair_sum(g, b, "pair_sum_" + k) for k, g, b in zip(names, slabs, bufs)]

    parts, updates = {}, {}

    def update(names, label, comm=None):
        res, got = _adamw_shards([parts[k] for k in names], [opt[k] for k in names], "adamw_" + label, comm=comm)
        updates.update(zip(names, res))
        return got

    down_slabs = [_mm(act, dh2_b, ta=True, slab="m", out_dtype=BF16, name="dw_down")]
    dgu, bufs = _mm(dh2_b, w_down, tb=True, slab="n", tm=MM_TILE[0] // 2, name="dact_swiglu_bwd",
                    comm=_to_sibling(down_slabs), swiglu=(gate.reshape(N_DEV, t, ffn), up.reshape(N_DEV, t, ffn)))
    dgu = dgu.reshape(2 * N_DEV, t, ffn)
    down_pair = pair_sums(["w_down_ffn"], down_slabs, bufs)
    dw_gu, got = _mm(dgu, f, ta=True, slab="m", out_dtype=BF16, name="dw_gate_up", comm=_to_chips(down_pair))
    parts["w_down_ffn"] = got[0]
    gu_names = ["w_gate_ffn", "w_up_ffn"]
    df, bufs = _mm(dgu, w_gate, slab="k", name="df_gate", comm=_to_sibling([dw_gu, dw_gu], first=[0, N_DEV]))
    gu_pairs = [_pair_sum(dw_gu, b, "pair_sum_" + k, first=s0) for k, b, s0 in zip(gu_names, bufs, [0, N_DEV])]
    half = _pick(gu_pairs[1].shape[1], gu_pairs[1].shape[1] // 2, 2 * SUBLANES)
    df, up_parts = _mm(dgu, w_up, slab="k", a_slab0=N_DEV, add=df, name="df_up",
                       comm=_to_chips(gu_pairs[1:], rows=[("r", 0, half)]))
    dh1, dh1_b, d_norm_ffn = _norm_bwd(h1, small["norm_ffn_g"], df, dh2, "norm_ffn_bwd")
    dw_out = _mm(merged, dh1_b, ta=True, out_dtype=BF16, name="dw_out")
    out_slabs = [_slabs_from_rows(dw_out)]
    dmerged, bufs = _mm(dh1_b, big["w_out"], tb=True, name="dmerged", comm=_to_sibling(out_slabs))
    out_pair = pair_sums(["w_out"], out_slabs, bufs)
    dy_attn, dy_sgu, dz_g, d_b_gate = _merge_bwd(dmerged, y_attn, y_sgu, z_g, small["b_gate"])
    dw_o_sgu = _mm(s_out, dy_sgu, ta=True, out_dtype=BF16, name="dw_o_sgu")
    ds_out = _mm(dy_sgu, big["w_o_sgu"], tb=True, name="ds_out")
    dz_uv, d_ws, d_b_col, d_sgu_norm = _sgu_bwd(z_uv, ds_out, small["sgu_norm_g"], ws, b_col)
    dw_o_attn = _mm(attn_b, dy_attn, ta=True, out_dtype=BF16, name="dw_o_attn")
    mix_names = ["w_o_sgu", "w_o_attn"]
    mix_slabs = [_slabs_from_cols(dw_o_sgu), _slabs_from_rows(dw_o_attn)]
    dattn, bufs = _mm(dy_attn, big["w_o_attn"], tb=True, name="dattn", comm=_to_sibling(mix_slabs))
    mix_pairs = pair_sums(mix_names, mix_slabs, bufs)
    rows = gu_pairs[1].shape[1]
    dq_p, dkv, dkpe_heads, got = _attn_bwd(
        q_c, kv, kpe, attn, dattn, lse, cos, sin,
        comm=_join(_to_chips(gu_pairs[:1]), _to_chips(gu_pairs[1:], rows=[("r", half, rows - half)], into=up_parts)))
    parts.update(zip(gu_names, got))
    dw_uq = _mm(qn, dq_p, ta=True, out_dtype=BF16, name="dw_uq")
    dw_ukv = _mm(kvn, dkv, ta=True, out_dtype=BF16, name="dw_ukv")
    dqn = _mm(dq_p, big["w_uq"], tb=True, name="dqn")
    dkvn = _mm(dkv, big["w_ukv"], tb=True, name="dkvn")
    dz_lat, d_q_norm, d_kv_norm = _lat_bwd(z_lat, small["q_norm_g"], small["kv_norm_g"], dqn, dkvn, dkpe_heads,
                                           cos, sin, ql, kvl)
    dw_g, got = _mm(dz_g, a, ta=True, out_dtype=BF16, name="dw_g", comm=_to_chips(out_pair))
    parts["w_out"] = got[0]
    dw_uv, got = _mm(dz_uv, a, ta=True, out_dtype=BF16, name="dw_uv", comm=_to_chips(mix_pairs[1:]))
    parts["w_o_attn"] = got[0]
    dw_lat, got = _mm(dz_lat, a, ta=True, out_dtype=BF16, name="dw_lat", comm=_to_chips(mix_pairs[:1]))
    parts["w_o_sgu"] = got[0]
    lat = ql + kvl + QK_ROPE
    dw_uq_cols = dw_uq.reshape(ql, heads, HEAD_PAD)[:, :, :QK_NOPE + QK_ROPE].reshape(ql, heads * (QK_NOPE + QK_ROPE))
    in_names = ["w_uq", "w_ukv", "w_in"]
    in_slabs = [_slabs_from_cols(dw_uq_cols), _slabs_from_cols(dw_ukv),
                _slabs_from_rows(jnp.concatenate([dw_lat[:lat], dw_uv, dw_g], axis=0))]
    da = _mm(dz_lat, big["w_lat_t"], name="da_lat")
    da, bufs = _mm(dz_uv, big["w_uv_t"], add=da, name="da_uv", comm=_to_sibling(in_slabs))
    uq_pair, ukv_pair, in_pair = pair_sums(in_names, in_slabs, bufs)
    cols = in_pair.shape[2]
    first = ((cols * TAIL_SPLIT[0]) // TAIL_SPLIT[1]) // LANES * LANES or cols
    da, in_parts = _mm(dz_g, big["w_g_t"], add=da, name="da_g", comm=_to_chips([in_pair], rows=[("c", 0, first)]))
    grad_x, _, d_norm_mix, got = _norm_bwd(x, small["norm_mix_g"], da, dh1, "norm_mix_bwd",
                                          comm=_to_chips([uq_pair, ukv_pair]))
    parts["w_uq"], parts["w_ukv"] = got
    rest = _to_chips([in_pair], rows=[("c", first, cols - first)], into=in_parts) if first < cols else None
    got = update(["w_gate_ffn", "w_up_ffn", "w_down_ffn"], "ffn", comm=rest)
    parts["w_in"] = got[0] if rest is not None else in_parts[0]
    update(["w_out", "w_o_attn"], "mixer_out")
    for k in ("w_o_sgu", "w_uq", "w_ukv", "w_in"):
        update([k], k)

    gs = {"norm_mix_g": d_norm_mix, "b_gate": d_b_gate, "q_norm_g": d_q_norm, "kv_norm_g": d_kv_norm,
          "sgu_norm_g": d_sgu_norm, "w_sgu": d_ws, "b_sgu_col": d_b_col, "norm_ffn_g": d_norm_ffn,
          "norm_final_g": d_norm_final}
    return loss_row, grad_x, gs, updates


def _my_place():
    return lax.axis_index("x"), lax.axis_index("y"), lax.axis_index("c")


N_CHIPS = N_DEV // 2

_GATHER_SEMS = [[(3,), (3,), ()], [(4,), (4,)], [(1,), (1,)]]


def _halves(shape):
    r, c = shape
    if (c // 2) % LANES == 0:
        return ("c", 0, c // 2), ("c", c // 2, c // 2)
    assert (r // 2) % (2 * SUBLANES) == 0, shape
    return ("r", 0, r // 2), ("r", r // 2, r // 2)


def _gather_copies(stage, ins, outs, sems):
    x, y, c = _my_place()
    me, x_nbr, y_nbr, diag = 4 * x + 2 * y + c, 4 * (1 - x) + 2 * y + c, 4 * x + 2 * (1 - y) + c, 4 * (1 - x) + 2 * (1 - y) + c
    sibling = (x, y, 1 - c)

    def remote(w, k, src, dst, to):
        return pltpu.make_async_remote_copy(src_ref=src, dst_ref=dst, send_sem=sems[0].at[w, k], recv_sem=sems[1].at[w, k],
                                            device_id=to, device_id_type=MESH)

    out = []
    for w in range(len(outs)):
        if stage == 1:
            dst = outs[w].at[me]
            out.append(pltpu.make_async_copy(ins[w], dst, sems[2].at[w]))
            out += [remote(w, k, ins[w], dst, to) for k, to in enumerate([sibling, (1 - x, y, c), (x, 1 - y, c)])]
        elif stage == 2:
            first, second = _halves(outs[w].shape[1:])
            out.append(remote(w, 0, _window(ins[w], x_nbr, first), _window(outs[w], x_nbr, first), (x, 1 - y, c)))
            out.append(remote(w, 1, _window(ins[w], y_nbr, second), _window(outs[w], y_nbr, second), (1 - x, y, c)))
            out.append(remote(w, 2, ins[w].at[x_nbr], outs[w].at[x_nbr], sibling))
            out.append(remote(w, 3, ins[w].at[y_nbr], outs[w].at[y_nbr], sibling))
        else:
            out.append(remote(w, 0, ins[w].at[diag], outs[w].at[diag], sibling))
    return out


def _gather_stage(stage, arrays):
    n = len(arrays)

    def start(ins, outs, sems):
        for cp in _gather_copies(stage, ins, outs, sems):
            cp.start()

    def finish(ins, outs, sems):
        for cp in _gather_copies(stage, ins, outs, sems):
            cp.wait()

    shapes = [jax.ShapeDtypeStruct(((N_DEV,) + a.shape) if stage == 1 else a.shape, a.dtype) for a in arrays]
    return _Comm(arrays, shapes, [pltpu.SemaphoreType.DMA((n,) + s) for s in _GATHER_SEMS[stage - 1]], start, finish,
                 aliases=None if stage == 1 else {w: w for w in range(n)})


def _join(*comms):
    ins, shapes, sems, aliases, spans = [], [], [], {}, []
    for cm in comms:
        spans.append((len(ins), len(ins) + len(cm.ins), len(shapes), len(shapes) + len(cm.out_shapes),
                      len(sems), len(sems) + len(cm.sems)))
        aliases.update({len(ins) + i: len(shapes) + o for i, o in cm.aliases.items()})
        ins, shapes, sems = ins + cm.ins, shapes + cm.out_shapes, sems + cm.sems

    def each(half):
        def run(i_refs, o_refs, s_refs):
            for cm, (i0, i1, o0, o1, s0, s1) in zip(comms, spans):
                getattr(cm, half)(i_refs[i0:i1], o_refs[o0:o1], s_refs[s0:s1])
        return run

    return _Comm(ins, shapes, sems, each("start"), each("finish"), aliases)


def _all_gather(shards):
    n = len(shards)
    n_sems = [len(s) for s in _GATHER_SEMS]

    def body(*refs):
        ins, outs, sems = refs[:n], refs[n:2 * n], refs[2 * n:]
        s0 = 0
        for stage in (1, 2, 3):
            mine = sems[s0:s0 + n_sems[stage - 1]]
            s0 += n_sems[stage - 1]
            copies = _gather_copies(stage, ins if stage == 1 else outs, outs, mine)
            for cp in copies:
                cp.start()
            for cp in copies:
                cp.wait()

    any_spec = pl.BlockSpec(memory_space=pl.ANY)
    return pl.pallas_call(
        body, in_specs=[any_spec] * n, out_specs=[any_spec] * n,
        out_shape=[jax.ShapeDtypeStruct((N_DEV,) + s.shape, s.dtype) for s in shards],
        scratch_shapes=[pltpu.SemaphoreType.DMA((n,) + s) for stage in _GATHER_SEMS for s in stage],
        compiler_params=pltpu.CompilerParams(has_side_effects=True), name="all_gather_weights")(*shards)


def _to_sibling(grads, first=None):
    n = len(grads)
    first = first or [0] * n

    def copies(ins, outs, sems):
        x, y, c = _my_place()
        send_sems, recv_sems = sems
        return [pltpu.make_async_remote_copy(
            src_ref=ins[w].at[first[w] + 2 * i + (1 - c)], dst_ref=outs[w].at[i], send_sem=send_sems.at[w, i],
            recv_sem=recv_sems.at[w, i], device_id=(x, y, 1 - c), device_id_type=MESH)
            for w in range(n) for i in range(N_CHIPS)]

    def start(ins, outs, sems):
        for cp in copies(ins, outs, sems):
            cp.start()

    def finish(ins, outs, sems):
        for cp in copies(ins, outs, sems):
            cp.wait()

    return _Comm(grads, [jax.ShapeDtypeStruct((N_CHIPS,) + g.shape[1:], g.dtype) for g in grads],
                 [pltpu.SemaphoreType.DMA((n, N_CHIPS)), pltpu.SemaphoreType.DMA((n, N_CHIPS))], start, finish)


def _window(ref, slab, win):
    if win is None:
        return ref.at[slab]
    if win[0] == "r":
        return ref.at[slab, pl.ds(win[1], win[2])]
    return ref.at[slab, slice(None), pl.ds(win[1], win[2])]


def _to_chips(parts, rows=None, into=None):
    n = len(parts)
    rows = rows or [None] * n

    def copies(ins, outs, sems):
        x, y, c = _my_place()
        send_sems, recv_sems, local_sems = sems
        mine = 2 * x + y
        chips = [(1 - x, y), (x, 1 - y), (1 - x, 1 - y)]
        remote = [pltpu.make_async_remote_copy(
            src_ref=_window(ins[w], 2 * cx + cy, rows[w]), dst_ref=_window(outs[w], mine, rows[w]),
            send_sem=send_sems.at[w, j], recv_sem=recv_sems.at[w, j], device_id=(cx, cy, c), device_id_type=MESH)
            for w in range(n) for j, (cx, cy) in enumerate(chips)]
        local = [pltpu.make_async_copy(_window(ins[w], mine, rows[w]), _window(outs[w], mine, rows[w]),
                                       local_sems.at[w]) for w in range(n)]
        return remote + local

    def start(ins, outs, sems):
        for cp in copies(ins, outs, sems):
            cp.start()

    def finish(ins, outs, sems):
        for cp in copies(ins, outs, sems):
            cp.wait()

    return _Comm(list(parts) + list(into or []), [jax.ShapeDtypeStruct(p.shape, p.dtype) for p in parts],
                 [pltpu.SemaphoreType.DMA((n, N_CHIPS - 1)), pltpu.SemaphoreType.DMA((n, N_CHIPS - 1)),
                  pltpu.SemaphoreType.DMA((n,))], start, finish,
                 aliases={n + w: w for w in range(n)} if into else None)


def _pair_sum(g, buf, name, first=0):
    _, r, c = g.shape
    tr, tc = _shard_tile(r, c, 4 * SHARD_TILE_ELEMS, 1024)
    core = (lax.axis_index("c") + first).astype(jnp.int32).reshape(1)

    def body(core_ref, g_ref, b_ref, o_ref):
        o_ref[...] = (g_ref[...].astype(F32) + b_ref[...].astype(F32)).astype(o_ref.dtype)

    blk = (1, tr, tc)
    return pl.pallas_call(
        body, grid_spec=pltpu.PrefetchScalarGridSpec(
            num_scalar_prefetch=1, grid=(N_CHIPS, r // tr, c // tc),
            in_specs=[pl.BlockSpec(blk, lambda i, j, l, core_ref: (2 * i + core_ref[0], j, l)),
                      pl.BlockSpec(blk, lambda i, j, l, core_ref: (i, j, l))],
            out_specs=pl.BlockSpec(blk, lambda i, j, l, core_ref: (i, j, l))),
        out_shape=jax.ShapeDtypeStruct(buf.shape, buf.dtype),
        compiler_params=_params(("parallel", "parallel", "parallel")), name=name)(core, g, buf)


def _all_reduce_pack(pack):
    r = pack.shape[0]

    def body(x_ref, out_ref, gath_ref, send_sems, recv_sems, local_sem):
        x, y, c = _my_place()
        me, sibling = (x, y, c), (x, y, 1 - c)
        chips = [(1 - x, y), (x, 1 - y), (1 - x, 1 - y)]

        def slab(place):
            return gath_ref.at[4 * place[0] + 2 * place[1] + place[2]]

        def copy(k, place, to, src=None):
            return pltpu.make_async_remote_copy(
                src_ref=slab(place) if src is None else src, dst_ref=slab(place),
                send_sem=send_sems.at[k], recv_sem=recv_sems.at[k], device_id=to, device_id_type=MESH)

        mine = pltpu.make_async_copy(x_ref, slab(me), local_sem)
        mine.start()
        first = [copy(0, me, sibling, src=x_ref)]
        first += [copy(1 + j, me, (*chip, c), src=x_ref) for j, chip in enumerate(chips)]
        for cp in first:
            cp.start()
        passed = [copy(4 + j, (*chip, c), sibling) for j, chip in enumerate(chips)]
        for j, chip in enumerate(chips):
            copy(1 + j, (*chip, c), me).wait_recv()
            passed[j].start()
        copy(0, sibling, me).wait_recv()
        for j, chip in enumerate(chips):
            copy(4 + j, (*chip, 1 - c), me).wait_recv()
        for cp in first + passed:
            cp.wait_send()
        mine.wait()
        acc = gath_ref[0]
        for i in range(1, N_DEV):
            acc = acc + gath_ref[i]
        out_ref[...] = acc

    vmem = pl.BlockSpec(memory_space=pltpu.VMEM)
    return pl.pallas_call(
        body, in_specs=[vmem], out_specs=vmem, out_shape=jax.ShapeDtypeStruct(pack.shape, F32),
        scratch_shapes=[pltpu.VMEM((N_DEV, r, LANES), F32), pltpu.SemaphoreType.DMA((7,)),
                        pltpu.SemaphoreType.DMA((7,)), pltpu.SemaphoreType.DMA],
        compiler_params=pltpu.CompilerParams(vmem_limit_bytes=VMEM_LIMIT), name="all_reduce_small")(pack)


def _adamw_math(w, g, m, v):
    m = ADAM_B1 * m + (1.0 - ADAM_B1) * g
    v = ADAM_B2 * v + (1.0 - ADAM_B2) * (g * g)
    m_hat = m / (1.0 - ADAM_B1 ** ADAM_STEP)
    v_hat = v / (1.0 - ADAM_B2 ** ADAM_STEP)
    delta = -ADAM_LR * (m_hat / (jnp.sqrt(v_hat) + ADAM_EPS) + ADAM_WD * w)
    return delta, m, v


def _adamw_shards(parts, opts, name, comm=None):
    r, c = opts[0][0].shape
    n_parts, k = parts[0].shape[0], len(parts)
    tr, tc = _shard_tile(r, c, SHARD_TILE_ELEMS // k)

    def body(*refs):
        ins, outs = refs[:4 * k], refs[4 * k:]
        for s in range(k):
            p_ref, w_ref, m_ref, v_ref = ins[4 * s:4 * s + 4]
            g_ref, d_ref, nm_ref, nv_ref = outs[4 * s:4 * s + 4]
            g = p_ref[0].astype(F32)
            for i in range(1, n_parts):
                g = g + p_ref[i].astype(F32)
            g_ref[...] = g
            d_ref[...], nm_ref[...], nv_ref[...] = _adamw_math(w_ref[...], g, m_ref[...], v_ref[...])

    spec = pl.BlockSpec((tr, tc), lambda i, j: (i, j))
    args = [a for p, o in zip(parts, opts) for a in (p,) + tuple(o)]
    outs, comm_outs = _call(
        body, grid=(r // tr, c // tc),
        in_specs=[pl.BlockSpec((n_parts, tr, tc), lambda i, j: (0, i, j)), spec, spec, spec] * k,
        out_specs=[spec] * (4 * k), out_shape=[jax.ShapeDtypeStruct((r, c), F32)] * (4 * k),
        sem=("parallel", "parallel"), name=name, args=args, comm=comm)
    return [outs[4 * s:4 * s + 4] for s in range(k)], comm_outs


def _adamw_pack(g, w, m, v):
    r, c = w.shape

    def body(g_ref, w_ref, m_ref, v_ref, d_ref, nm_ref, nv_ref):
        d_ref[...], nm_ref[...], nv_ref[...] = _adamw_math(w_ref[...], g_ref[...], m_ref[...], v_ref[...])

    return pl.pallas_call(
        body, in_specs=[_full((r, c))] * 4, out_specs=[_full((r, c))] * 3, grid=(1,),
        out_shape=[jax.ShapeDtypeStruct((r, c), F32)] * 3,
        compiler_params=_params(("arbitrary",)), name="adamw_small")(g, w, m, v)


def _cols_from_slabs(g):
    return jnp.transpose(g, (1, 0, 2)).reshape(g.shape[1], N_DEV * g.shape[2])


def _slabs_from_cols(w):
    r, c8 = w.shape
    return jnp.transpose(w.reshape(r, N_DEV, c8 // N_DEV), (1, 0, 2))


def _rows_from_slabs(g):
    return g.reshape(N_DEV * g.shape[1], g.shape[2])


def _slabs_from_rows(w):
    return w.reshape(N_DEV, w.shape[0] // N_DEV, w.shape[1])


def _compute_layout(gathered, ql, kvl, heads, sw):
    out = {}
    for k, g in gathered.items():
        if k == "w_in":
            lat = ql + kvl + QK_ROPE
            w_in_t = _rows_from_slabs(g)
            out["w_lat_t"] = jnp.pad(w_in_t[:lat], ((0, LANES - QK_ROPE), (0, 0)))
            out["w_uv_t"] = w_in_t[lat:lat + 2 * sw]
            out["w_g_t"] = w_in_t[lat + 2 * sw:]
        elif k == "w_uq":
            per_head = _cols_from_slabs(g).reshape(ql, heads, QK_NOPE + QK_ROPE)
            pad = HEAD_PAD - QK_NOPE - QK_ROPE
            out["w_uq"] = jnp.pad(per_head, ((0, 0), (0, 0), (0, pad))).reshape(ql, heads * HEAD_PAD)
        elif k in ("w_o_attn", "w_out", "w_down_ffn"):
            out[k.removesuffix("_ffn")] = _rows_from_slabs(g)
        else:
            out[k.removesuffix("_ffn")] = _cols_from_slabs(g)
    return out


_SMALL =["norm_mix_g", "b_gate", "q_norm_g", "kv_norm_g", "sgu_norm_g", "w_sgu", "b_sgu", "norm_ffn_g", "norm_final_g"]
_BIG = ["w_in", "w_uq", "w_ukv", "w_o_attn", "w_o_sgu", "w_out", "w_gate_ffn", "w_up_ffn", "w_down_ffn"]
_TRANSPOSED = ("w_in", "w_gate_ffn", "w_up_ffn")
_ORDER = ["norm_mix_g", "w_in", "b_gate", "q_norm_g", "w_uq", "kv_norm_g", "w_ukv", "w_o_attn", "sgu_norm_g", "w_sgu",
          "b_sgu", "w_o_sgu", "w_out", "norm_ffn_g", "w_gate_ffn", "w_up_ffn", "w_down_ffn", "norm_final_g"]


def _pack_rows(parts):
    rows, sizes = [], []
    for p in parts:
        flat = p.reshape(-1)
        n = flat.shape[0]
        padded = -(-n // (SUBLANES * LANES)) * (SUBLANES * LANES)
        rows.append(jnp.pad(flat, (0, padded - n)).reshape(padded // LANES, LANES))
        sizes.append((n, padded // LANES))
    return jnp.concatenate(rows, axis=0), sizes


def _unpack_rows(pack, sizes, shapes):
    out, r0 = [], 0
    for (n, nr), shp in zip(sizes, shapes):
        out.append(pack[r0:r0 + nr].reshape(-1)[:n].reshape(shp))
        r0 += nr
    return out


def kernel(x, positions, norm_mix_g, w_in, b_gate, q_norm_g, w_uq, kv_norm_g, w_ukv, w_o_attn, sgu_norm_g, w_sgu, b_sgu, w_o_sgu, w_out, norm_ffn_g, w_gate_ffn, w_up_ffn, w_down_ffn, norm_final_g, loss_target, m_norm_mix_g, m_w_in, m_b_gate, m_q_norm_g, m_w_uq, m_kv_norm_g, m_w_ukv, m_w_o_attn, m_sgu_norm_g, m_w_sgu, m_b_sgu, m_w_o_sgu, m_w_out, m_norm_ffn_g, m_w_gate_ffn, m_w_up_ffn, m_w_down_ffn, m_norm_final_g, v_norm_mix_g, v_w_in, v_b_gate, v_q_norm_g, v_w_uq, v_kv_norm_g, v_w_ukv, v_w_o_attn, v_sgu_norm_g, v_w_sgu, v_b_sgu, v_w_o_sgu, v_w_out, v_norm_ffn_g, v_w_gate_ffn, v_w_up_ffn, v_w_down_ffn, v_norm_final_g):
    wts = dict(norm_mix_g=norm_mix_g, w_in=w_in, b_gate=b_gate, q_norm_g=q_norm_g, w_uq=w_uq, kv_norm_g=kv_norm_g,
               w_ukv=w_ukv, w_o_attn=w_o_attn, sgu_norm_g=sgu_norm_g, w_sgu=w_sgu, b_sgu=b_sgu, w_o_sgu=w_o_sgu,
               w_out=w_out, norm_ffn_g=norm_ffn_g, w_gate_ffn=w_gate_ffn, w_up_ffn=w_up_ffn, w_down_ffn=w_down_ffn,
               norm_final_g=norm_final_g)
    mom = dict(norm_mix_g=m_norm_mix_g, w_in=m_w_in, b_gate=m_b_gate, q_norm_g=m_q_norm_g, w_uq=m_w_uq,
               kv_norm_g=m_kv_norm_g, w_ukv=m_w_ukv, w_o_attn=m_w_o_attn, sgu_norm_g=m_sgu_norm_g, w_sgu=m_w_sgu,
               b_sgu=m_b_sgu, w_o_sgu=m_w_o_sgu, w_out=m_w_out, norm_ffn_g=m_norm_ffn_g, w_gate_ffn=m_w_gate_ffn,
               w_up_ffn=m_w_up_ffn, w_down_ffn=m_w_down_ffn, norm_final_g=m_norm_final_g)
    var = dict(norm_mix_g=v_norm_mix_g, w_in=v_w_in, b_gate=v_b_gate, q_norm_g=v_q_norm_g, w_uq=v_w_uq,
               kv_norm_g=v_kv_norm_g, w_ukv=v_w_ukv, w_o_attn=v_w_o_attn, sgu_norm_g=v_sgu_norm_g, w_sgu=v_w_sgu,
               b_sgu=v_b_sgu, w_o_sgu=v_w_o_sgu, w_out=v_w_out, norm_ffn_g=v_norm_ffn_g, w_gate_ffn=v_w_gate_ffn,
               w_up_ffn=v_w_up_ffn, w_down_ffn=v_w_down_ffn, norm_final_g=v_norm_final_g)

    t, d = x.shape[1], x.shape[2]
    ql, kvl = q_norm_g.shape[1], kv_norm_g.shape[1]
    heads = (w_uq.shape[2] * N_DEV) // (QK_NOPE + QK_ROPE)
    sw = sgu_norm_g.shape[1]

    def shard(a, k):
        return a[0].T if k in _TRANSPOSED else a[0]

    def unshard(a, k):
        return (a.T if k in _TRANSPOSED else a).reshape(wts[k].shape)

    opt = {k: (shard(wts[k], k), shard(mom[k], k), shard(var[k], k)) for k in _BIG}
    shards = {k: opt[k][0].astype(BF16) for k in _BIG}
    small = {
        "norm_mix_g": norm_mix_g, "b_gate": b_gate, "q_norm_g": q_norm_g, "kv_norm_g": kv_norm_g,
        "sgu_norm_g": sgu_norm_g, "w_sgu": w_sgu[0], "b_sgu_col": b_sgu[0][:, :, None], "norm_ffn_g": norm_ffn_g,
        "norm_final_g": norm_final_g[None, :],
    }

    loss_row, grad_x, gs, updates = _local_step(x[0], positions.reshape(t, 1), loss_target[0], small, shards, opt)
    grads, deltas, new_m, new_v = {}, {}, {}, {}
    for k in _BIG:
        grads[k], deltas[k], new_m[k], new_v[k] = (unshard(a, k) for a in updates[k])

    small_grads = [gs["norm_mix_g"], gs["b_gate"], gs["q_norm_g"], gs["kv_norm_g"], gs["sgu_norm_g"], gs["w_sgu"],
                   gs["b_sgu_col"], gs["norm_ffn_g"], gs["norm_final_g"]]
    pack, sizes = _pack_rows([loss_row] + small_grads)
    total = _all_reduce_pack(pack)
    shapes = [(1, LANES)] + [wts[k].shape for k in _SMALL]
    unpacked = _unpack_rows(total, sizes, shapes)
    loss = unpacked[0][0, 0]
    for k, g in zip(_SMALL, unpacked[1:]):
        grads[k] = g
    g_pack = total[sizes[0][1]:]
    w_pack, _ = _pack_rows([wts[k] for k in _SMALL])
    m_pack, _ = _pack_rows([mom[k] for k in _SMALL])
    v_pack, _ = _pack_rows([var[k] for k in _SMALL])
    d_pack, nm_pack, nv_pack = _adamw_pack(g_pack, w_pack, m_pack, v_pack)
    small_shapes = [wts[k].shape for k in _SMALL]
    for store, pk in ((deltas, d_pack), (new_m, nm_pack), (new_v, nv_pack)):
        for k, a in zip(_SMALL, _unpack_rows(pk, sizes[1:], small_shapes)):
            store[k] = a

    return (loss, grad_x[None], *[grads[k] for k in _ORDER], *[deltas[k] for k in _ORDER],
            *[new_m[k] for k in _ORDER], *[new_v[k] for k in _ORDER])
```

```python
import functools
import math

import jax
import jax.numpy as jnp
from jax import lax
from jax.experimental import pallas as pl
from jax.experimental.pallas import tpu as pltpu

F32 = jnp.float32
BF16 = jnp.bfloat16

N_DEV = 8
N_HEADS = 16
QK_NOPE = 128
QK_ROPE = 64
V_HEAD = 128
HEAD_PAD = 256
ROPE_THETA = 10000.0
CHUNK = 128
SGU_GROUP = 128
RMS_EPS = 1e-6
LANES = 128
SUBLANES = 8

ADAM_LR = 0.001
ADAM_B1 = 0.9
ADAM_B2 = 0.999
ADAM_EPS = 1e-08
ADAM_WD = 0.01
ADAM_STEP = 10

VMEM_LIMIT = 48 * 1024 * 1024
MM_TILE = (2048, 512, 2048)
MM_TILE_TA = (512, 2048)
ATTN_TILE = 512
HEADS_PER_STEP = (4, 4)
ROW_KERNEL_BYTES = 24 * 1024 * 1024
SHARD_TILE_ELEMS = 256 * 1024
SLABS_PER_STEP = 2
TAIL_SPLIT = (3, 8)
NEG_BIG = -1e30
MESH = pl.DeviceIdType.MESH


def _pick(n, target, mult=LANES):
    best = None
    d = mult
    while d <= min(n, target):
        if n % d == 0:
            best = d
        d += mult
    return best or n


def _row_tile(t, width, n_blocks, mult=2 * SUBLANES):
    return _pick(t, max(mult, ROW_KERNEL_BYTES // (3 * n_blocks * width * 4)), mult)


def _shard_tile(r, c, elems=SHARD_TILE_ELEMS, max_rows=256):
    tr = _pick(r, max_rows, 2 * SUBLANES)
    return tr, _pick(c, max(LANES, elems // tr))


def _params(sem):
    return pltpu.CompilerParams(dimension_semantics=sem, vmem_limit_bytes=VMEM_LIMIT)


def _full(shape):
    nd = len(shape)
    return pl.BlockSpec(shape, lambda *_: (0,) * nd)


def _rows(tr, w, cb=0):
    return pl.BlockSpec((tr, w), lambda i: (i, cb))


class _Comm:
    def __init__(self, ins, out_shapes, sems, start, finish, aliases=None):
        self.ins, self.out_shapes, self.sems, self.start, self.finish = list(ins), list(out_shapes), list(sems), start, finish
        self.aliases = dict(aliases or {})


def _call(body, *, grid, in_specs, out_specs, out_shape, scratch_shapes=(), sem, name, args, comm=None):
    if comm is None:
        outs = pl.pallas_call(body, grid=grid, in_specs=list(in_specs), out_specs=list(out_specs),
                              out_shape=list(out_shape), scratch_shapes=list(scratch_shapes),
                              compiler_params=_params(sem), name=name)(*args)
        return list(outs), []
    n_in, n_out, n_sc = len(in_specs), len(out_shape), len(scratch_shapes)
    nci, nco = len(comm.ins), len(comm.out_shapes)

    def hosted(*refs):
        ins, refs = refs[:n_in], refs[n_in:]
        cins, refs = refs[:nci], refs[nci:]
        outs, refs = refs[:n_out], refs[n_out:]
        couts, refs = refs[:nco], refs[nco:]
        scratch, csems = refs[:n_sc], refs[n_sc:]
        ids = [pl.program_id(i) for i in range(len(grid))]
        first = functools.reduce(jnp.logical_and, [i == 0 for i in ids])
        last = functools.reduce(jnp.logical_and, [i == g - 1 for i, g in zip(ids, grid)])

        @pl.when(first)
        def _():
            comm.start(cins, couts, csems)

        body(*ins, *outs, *scratch)

        @pl.when(last)
        def _():
            comm.finish(cins, couts, csems)

    any_spec = pl.BlockSpec(memory_space=pl.ANY)
    res = pl.pallas_call(
        hosted, grid=grid, in_specs=list(in_specs) + [any_spec] * nci, out_specs=list(out_specs) + [any_spec] * nco,
        out_shape=list(out_shape) + comm.out_shapes, scratch_shapes=list(scratch_shapes) + comm.sems,
        input_output_aliases={n_in + i: n_out + o for i, o in comm.aliases.items()},
        compiler_params=pltpu.CompilerParams(dimension_semantics=("arbitrary",) * len(grid),
                                             vmem_limit_bytes=VMEM_LIMIT, has_side_effects=True),
        name=name)(*args, *comm.ins)
    return list(res[:n_out]), list(res[n_out:])


def _swiglu_grads(g, u, d):
    s = 1.0 / (1.0 + jnp.exp(-g))
    return (d * u * (s * (1.0 + g * (1.0 - s)))).astype(BF16), (d * (g * s)).astype(BF16)


def _mm(a, b, *, ta=False, tb=False, add=None, out_dtype=F32, tm=None, tn=None, tk=None, name, comm=None,
        slab=None, a_slab0=0, swiglu=None, rope=None):
    sq = None
    if ta:
        tm, tn = tm or MM_TILE_TA[0], tn or MM_TILE_TA[1]
    if slab is None:
        m, k = (a.shape[1], a.shape[0]) if ta else a.shape
        n = b.shape[0] if tb else b.shape[1]
        assert k == (b.shape[1] if tb else b.shape[0]), (a.shape, b.shape, ta, tb)
        tm, tn, tk = _pick(m, tm or MM_TILE[0]), _pick(n, tn or MM_TILE[1]), _pick(k, tk or MM_TILE[2])
        if rope is not None:
            tn = _pick(n, max(tn, HEAD_PAD), HEAD_PAD)
        grid = (m // tm, n // tn, k // tk)
        a_spec = pl.BlockSpec((tk, tm), lambda i, j, kk: (kk, i)) if ta else pl.BlockSpec((tm, tk), lambda i, j, kk: (i, kk))
        b_spec = pl.BlockSpec((tn, tk), lambda i, j, kk: (j, kk)) if tb else pl.BlockSpec((tk, tn), lambda i, j, kk: (kk, j))
        o_spec, o_shape = pl.BlockSpec((tm, tn), lambda i, j, kk: (i, j)), (m, n)
    elif slab == "n":
        m, k = (a.shape[1], a.shape[0]) if ta else a.shape
        s, c = b.shape[0], (b.shape[1] if tb else b.shape[2])
        assert k == (b.shape[2] if tb else b.shape[1]), (a.shape, b.shape, ta, tb)
        tm, tn, tk = _pick(m, tm or MM_TILE[0]), c, _pick(k, tk or MM_TILE[2])
        grid = (m // tm, s, k // tk)
        a_spec = pl.BlockSpec((tk, tm), lambda i, j, kk: (kk, i)) if ta else pl.BlockSpec((tm, tk), lambda i, j, kk: (i, kk))
        b_spec = (pl.BlockSpec((sq, c, tk), lambda i, j, kk: (j, 0, kk)) if tb
                  else pl.BlockSpec((sq, tk, c), lambda i, j, kk: (j, kk, 0)))
        o_spec, o_shape = pl.BlockSpec((sq, tm, c), lambda i, j, kk: (j, i, 0)), (s, m, c)
    elif slab == "m":
        assert ta and not tb
        s, k, c = a.shape
        n = b.shape[1]
        assert k == b.shape[0], (a.shape, b.shape)
        tm, tn, tk = c, _pick(n, tn or MM_TILE[1]), _pick(k, tk or MM_TILE[2])
        grid = (s, n // tn, k // tk)
        a_spec = pl.BlockSpec((sq, tk, c), lambda i, j, kk: (i, kk, 0))
        b_spec = pl.BlockSpec((tk, tn), lambda i, j, kk: (kk, j))
        o_spec, o_shape = pl.BlockSpec((sq, c, tn), lambda i, j, kk: (i, 0, j)), (s, c, n)
    else:
        assert slab == "k" and not ta
        s, c = b.shape[0], (b.shape[2] if tb else b.shape[1])
        m, n = a.shape[1], (b.shape[1] if tb else b.shape[2])
        assert a.shape[2] == c and a.shape[0] >= a_slab0 + s, (a.shape, b.shape, a_slab0)
        tm, tn, tk = _pick(m, tm or MM_TILE[0]), _pick(n, tn or MM_TILE[1]), c
        per_step = SLABS_PER_STEP if (s % SLABS_PER_STEP == 0 and a_slab0 % SLABS_PER_STEP == 0) else 1
        first = a_slab0 // per_step
        grid = (m // tm, n // tn, s // per_step)
        a_spec = pl.BlockSpec((per_step, tm, c), lambda i, j, kk: (kk + first, i, 0))
        b_spec = (pl.BlockSpec((per_step, tn, c), lambda i, j, kk: (kk, j, 0)) if tb
                  else pl.BlockSpec((per_step, c, tn), lambda i, j, kk: (kk, 0, j)))
        o_spec, o_shape = pl.BlockSpec((tm, tn), lambda i, j, kk: (i, j)), (m, n)
    nk = grid[2]
    dims = (((0 if ta else 1,), (1 if tb else 0,)), ((), ()))

    def product(a_ref, b_ref):
        if slab != "k":
            return lax.dot_general(a_ref[...].astype(BF16), b_ref[...].astype(BF16), dims, preferred_element_type=F32)
        r = None
        for u in range(a_ref.shape[0]):
            p = lax.dot_general(a_ref[u].astype(BF16), b_ref[u].astype(BF16), dims, preferred_element_type=F32)
            r = p if r is None else r + p
        return r

    if swiglu is not None:
        assert slab == "n" and add is None
        o_block = pl.BlockSpec((2, sq, tm, c), lambda i, j, kk: (0, j, i, 0))
        o_shape, out_dtype = (2,) + o_shape, BF16

    if rope is not None:
        assert slab is None and add is None and swiglu is None and tn % HEAD_PAD == 0
        out_dtype = BF16
    extras = tuple(swiglu or ()) + tuple(rope or ())

    def body(*refs):
        a_ref, b_ref = refs[:2]
        add_ref = refs[2] if add is not None else None
        x0_ref, x1_ref = refs[2:4] if extras else (None, None)
        o_ref = refs[2 + (add is not None) + len(extras)]
        acc_ref = refs[-1] if nk > 1 else None

        def finish(r):
            if swiglu is not None:
                o_ref[0], o_ref[1] = _swiglu_grads(x0_ref[...], x1_ref[...], r)
                return
            if rope is not None:
                cos, sin = x0_ref[...], x1_ref[...]
                for h in range(tn // HEAD_PAD):
                    lo = h * HEAD_PAD
                    o_ref[:, lo:lo + QK_NOPE] = r[:, lo:lo + QK_NOPE].astype(BF16)
                    o_ref[:, lo + QK_NOPE:lo + HEAD_PAD] = _rope(r[:, lo + QK_NOPE:lo + HEAD_PAD], cos, sin).astype(BF16)
                return
            if add_ref is not None:
                r = r + add_ref[...].astype(F32)
            o_ref[...] = r.astype(o_ref.dtype)

        if nk == 1:
            finish(product(a_ref, b_ref))
            return
        kk = pl.program_id(2)

        @pl.when(kk == 0)
        def _():
            acc_ref[...] = product(a_ref, b_ref)

        if nk > 2:
            @pl.when(jnp.logical_and(kk > 0, kk < nk - 1))
            def _():
                acc_ref[...] += product(a_ref, b_ref)

        @pl.when(kk == nk - 1)
        def _():
            finish(acc_ref[...] + product(a_ref, b_ref))

    in_specs = [a_spec, b_spec] + ([o_spec] if add is not None else []) + ([o_spec] * 2 if swiglu is not None else [])
    if rope is not None:
        in_specs += [pl.BlockSpec((tm, LANES), lambda i, j, kk: (i, 0))] * 2
    args = (a, b) + ((add,) if add is not None else ()) + extras
    if swiglu is not None:
        o_spec = o_block
    outs, comm_outs = _call(
        body, grid=grid, in_specs=in_specs, out_specs=[o_spec],
        out_shape=[jax.ShapeDtypeStruct(o_shape, out_dtype)],
        scratch_shapes=[pltpu.VMEM((tm, tn), F32)] if nk > 1 else [],
        sem=("parallel", "parallel", "arbitrary"), name=name, args=args, comm=comm)
    return outs[0] if comm is None else (outs[0], comm_outs)


def _rms_scale(x):
    return lax.rsqrt(jnp.mean(x * x, axis=-1, keepdims=True) + RMS_EPS)


def _rms_bwd(xhat, r, g, dy):
    t = dy * g
    dx = r * (t - xhat * jnp.mean(t * xhat, axis=-1, keepdims=True))
    return dx, dy * xhat


_GELU_C = math.sqrt(2.0 / math.pi)


def _gelu(x):
    return x * (0.5 * (1.0 + jnp.tanh(_GELU_C * (x + 0.044715 * (x * x * x)))))


def _gelu_and_grad(x):
    t = jnp.tanh(_GELU_C * (x + 0.044715 * (x * x * x)))
    cdf = 0.5 * (1.0 + t)
    return x * cdf, cdf + x * (0.5 * (1.0 - t * t) * (_GELU_C * (1.0 + 3.0 * 0.044715 * (x * x))))


def _sigmoid(x):
    return 1.0 / (1.0 + jnp.exp(-x))


def _swap_halves(x):
    lane = lax.broadcasted_iota(jnp.int32, x.shape, 1)
    first = (lane % QK_ROPE) < (QK_ROPE // 2)
    return jnp.where(first, pltpu.roll(x, LANES - QK_ROPE // 2, 1), pltpu.roll(x, QK_ROPE // 2, 1))


def _rope(x, cos, sin_signed):
    return x * cos + _swap_halves(x) * sin_signed


def _rope_bwd(d, cos, sin_signed):
    return d * cos + _swap_halves(d * sin_signed)


def _rope_tables(pos_col, inv_freq_row, sign_row):
    t = pos_col.shape[0]
    tr = _pick(t, 512, SUBLANES)

    def body(p_ref, f_ref, s_ref, cos_ref, sin_ref):
        ang = p_ref[...].astype(F32) * f_ref[...]
        cos_ref[...] = jnp.cos(ang)
        sin_ref[...] = jnp.sin(ang) * s_ref[...]

    return pl.pallas_call(
        body, grid=(t // tr,), in_specs=[_rows(tr, 1), _full((1, LANES)), _full((1, LANES))],
        out_specs=[_rows(tr, LANES), _rows(tr, LANES)],
        out_shape=[jax.ShapeDtypeStruct((t, LANES), F32)] * 2,
        compiler_params=_params(("parallel",)), name="rope_tables")(pos_col, inv_freq_row, sign_row)


def _norm_fwd(x, g, name):
    t, d = x.shape
    tr = _row_tile(t, d, 2)

    def body(x_ref, g_ref, y_ref):
        xv = x_ref[...]
        y_ref[...] = (xv * _rms_scale(xv) * g_ref[...]).astype(BF16)

    return pl.pallas_call(
        body, grid=(t // tr,), in_specs=[_rows(tr, d), _full((1, d))], out_specs=_rows(tr, d),
        out_shape=jax.ShapeDtypeStruct((t, d), BF16), compiler_params=_params(("parallel",)), name=name)(x, g)


def _lat_fwd(z_lat, qg, kvg, cos, sin, ql, kvl):
    t = z_lat.shape[0]
    tr = _row_tile(t, z_lat.shape[1], 2)

    def body(z_ref, qg_ref, kvg_ref, cos_ref, sin_ref, qn_ref, kvn_ref, kpe_ref):
        q = z_ref[:, 0:ql]
        qn_ref[...] = (q * _rms_scale(q) * qg_ref[...]).astype(BF16)
        kv = z_ref[:, ql:ql + kvl]
        kvn_ref[...] = (kv * _rms_scale(kv) * kvg_ref[...]).astype(BF16)
        kpe_ref[...] = _rope(z_ref[:, ql + kvl:ql + kvl + LANES], cos_ref[...], sin_ref[...]).astype(BF16)

    w = z_lat.shape[1]
    return pl.pallas_call(
        body, grid=(t // tr,),
        in_specs=[_rows(tr, w), _full((1, ql)), _full((1, kvl)), _rows(tr, LANES), _rows(tr, LANES)],
        out_specs=[_rows(tr, ql), _rows(tr, kvl), _rows(tr, LANES)],
        out_shape=[jax.ShapeDtypeStruct((t, ql), BF16), jax.ShapeDtypeStruct((t, kvl), BF16),
                   jax.ShapeDtypeStruct((t, LANES), BF16)],
        compiler_params=_params(("parallel",)), name="lat_fwd")(z_lat, qg, kvg, cos, sin)


def _tril_mask():
    r = lax.broadcasted_iota(jnp.int32, (CHUNK, CHUNK), 0)
    c = lax.broadcasted_iota(jnp.int32, (CHUNK, CHUNK), 1)
    return r >= c


def _sgu_fwd(z_uv, gs, ws, b_col):
    t = z_uv.shape[0]
    sw = z_uv.shape[1] // 2
    groups = sw // SGU_GROUP
    tr = _pick(t, 256, CHUNK)

    def body(u_ref, v_ref, gs_ref, ws_ref, b_ref, o_ref):
        v = _gelu(v_ref[...])
        vn = (v * _rms_scale(v) * gs_ref[...]).astype(BF16)
        tri = _tril_mask()
        for g in range(groups):
            wg = jnp.where(tri, ws_ref[g], 0.0).astype(BF16)
            cols = slice(g * SGU_GROUP, (g + 1) * SGU_GROUP)
            for c in range(tr // CHUNK):
                rows = slice(c * CHUNK, (c + 1) * CHUNK)
                mixed = jnp.dot(wg, vn[rows, cols], preferred_element_type=F32) + b_ref[g]
                o_ref[rows, cols] = (_gelu(u_ref[rows, cols]) * mixed).astype(BF16)

    return pl.pallas_call(
        body, grid=(t // tr,),
        in_specs=[_rows(tr, sw, 0), _rows(tr, sw, 1), _full((1, sw)), _full(ws.shape), _full(b_col.shape)],
        out_specs=_rows(tr, sw), out_shape=jax.ShapeDtypeStruct((t, sw), BF16),
        compiler_params=_params(("parallel",)), name="sgu_fwd")(z_uv, z_uv, gs, ws, b_col)


def _merge_fwd(y_attn, y_sgu, z_g, b_gate, comm=None):
    t, d = y_attn.shape
    tr = _row_tile(t, d, 5)

    def body(ya_ref, ys_ref, g0_ref, g1_ref, b0_ref, b1_ref, o_ref):
        g0 = _sigmoid(g0_ref[...] + b0_ref[...])
        g1 = _sigmoid(g1_ref[...] + b1_ref[...])
        o_ref[...] = (g0 * ya_ref[...] + g1 * ys_ref[...]).astype(BF16)

    bspec0 = pl.BlockSpec((1, d), lambda i: (0, 0))
    bspec1 = pl.BlockSpec((1, d), lambda i: (0, 1))
    outs, comm_outs = _call(
        body, grid=(t // tr,),
        in_specs=[_rows(tr, d), _rows(tr, d), _rows(tr, d, 0), _rows(tr, d, 1), bspec0, bspec1],
        out_specs=[_rows(tr, d)], out_shape=[jax.ShapeDtypeStruct((t, d), BF16)],
        sem=("parallel",), name="merge_fwd", args=(y_attn, y_sgu, z_g, z_g, b_gate, b_gate), comm=comm)
    return outs[0], comm_outs


def _swiglu_fwd(gate, up, comm=None):
    t, f = gate.shape
    tr = _row_tile(t, f, 3)

    def body(g_ref, u_ref, o_ref):
        g = g_ref[...]
        o_ref[...] = (g * _sigmoid(g) * u_ref[...]).astype(BF16)

    outs, comm_outs = _call(
        body, grid=(t // tr,), in_specs=[_rows(tr, f), _rows(tr, f)], out_specs=[_rows(tr, f)],
        out_shape=[jax.ShapeDtypeStruct((t, f), BF16)], sem=("parallel",), name="swiglu_fwd", args=(gate, up), comm=comm)
    return outs[0], comm_outs


def _loss_head(h2, g, target):
    t, d = h2.shape
    tr = _row_tile(t, d, 3)

    def body(h_ref, g_ref, t_ref, loss_ref, dh_ref, dhb_ref, dg_ref):
        @pl.when(pl.program_id(0) == 0)
        def _():
            loss_ref[...] = jnp.zeros_like(loss_ref)
            dg_ref[...] = jnp.zeros_like(dg_ref)

        h = h_ref[...]
        r = _rms_scale(h)
        hhat = h * r
        gv = g_ref[...]
        err = hhat * gv - t_ref[...]
        loss_ref[...] += jnp.full(loss_ref.shape, 0.5 * jnp.sum(jnp.mean(err * err, axis=-1)), F32)
        dx, dg_rows = _rms_bwd(hhat, r, gv, err * (1.0 / d))
        dh_ref[...] = dx
        dhb_ref[...] = dx.astype(BF16)
        dg_ref[...] += jnp.sum(dg_rows, axis=0, keepdims=True)

    return pl.pallas_call(
        body, grid=(t // tr,), in_specs=[_rows(tr, d), _full((1, d)), _rows(tr, d)],
        out_specs=[_full((1, LANES)), _rows(tr, d), _rows(tr, d), _full((1, d))],
        out_shape=[jax.ShapeDtypeStruct((1, LANES), F32), jax.ShapeDtypeStruct((t, d), F32),
                   jax.ShapeDtypeStruct((t, d), BF16), jax.ShapeDtypeStruct((1, d), F32)],
        compiler_params=_params(("arbitrary",)), name="loss_head")(h2, g, target)


def _norm_bwd(x, g, dy, resid, name, comm=None):
    t, d = x.shape
    tr = _row_tile(t, d, 5)

    def body(x_ref, g_ref, dy_ref, r_ref, dx_ref, dxb_ref, dg_ref):
        @pl.when(pl.program_id(0) == 0)
        def _():
            dg_ref[...] = jnp.zeros_like(dg_ref)

        xv = x_ref[...]
        r = _rms_scale(xv)
        dx, dg_rows = _rms_bwd(xv * r, r, g_ref[...], dy_ref[...])
        dx = r_ref[...] + dx
        dx_ref[...] = dx
        dxb_ref[...] = dx.astype(BF16)
        dg_ref[...] += jnp.sum(dg_rows, axis=0, keepdims=True)

    outs, comm_outs = _call(
        body, grid=(t // tr,), in_specs=[_rows(tr, d), _full((1, d)), _rows(tr, d), _rows(tr, d)],
        out_specs=[_rows(tr, d), _rows(tr, d), _full((1, d))],
        out_shape=[jax.ShapeDtypeStruct((t, d), F32), jax.ShapeDtypeStruct((t, d), BF16),
                   jax.ShapeDtypeStruct((1, d), F32)],
        sem=("arbitrary",), name=name, args=(x, g, dy, resid), comm=comm)
    return (outs[0], outs[1], outs[2]) if comm is None else (outs[0], outs[1], outs[2], comm_outs)


def _merge_bwd(dmerged, y_attn, y_sgu, z_g, b_gate):
    t, d = y_attn.shape
    tr = _row_tile(t, d, 7)

    def body(dm_ref, ya_ref, ys_ref, g0_ref, g1_ref, b0_ref, b1_ref, dya_ref, dys_ref, dz_ref, db_ref):
        @pl.when(pl.program_id(0) == 0)
        def _():
            db_ref[...] = jnp.zeros_like(db_ref)

        dm = dm_ref[...]
        g0 = _sigmoid(g0_ref[...] + b0_ref[...])
        g1 = _sigmoid(g1_ref[...] + b1_ref[...])
        dya_ref[...] = (dm * g0).astype(BF16)
        dys_ref[...] = (dm * g1).astype(BF16)
        dl0 = dm * ya_ref[...] * (g0 * (1.0 - g0))
        dl1 = dm * ys_ref[...] * (g1 * (1.0 - g1))
        dz_ref[:, 0:d] = dl0.astype(BF16)
        dz_ref[:, d:2 * d] = dl1.astype(BF16)
        db_ref[:, 0:d] += jnp.sum(dl0, axis=0, keepdims=True)
        db_ref[:, d:2 * d] += jnp.sum(dl1, axis=0, keepdims=True)

    bspec0 = pl.BlockSpec((1, d), lambda i: (0, 0))
    bspec1 = pl.BlockSpec((1, d), lambda i: (0, 1))
    return pl.pallas_call(
        body, grid=(t // tr,),
        in_specs=[_rows(tr, d), _rows(tr, d), _rows(tr, d), _rows(tr, d, 0), _rows(tr, d, 1), bspec0, bspec1],
        out_specs=[_rows(tr, d), _rows(tr, d), _rows(tr, 2 * d), _full((1, 2 * d))],
        out_shape=[jax.ShapeDtypeStruct((t, d), BF16), jax.ShapeDtypeStruct((t, d), BF16),
                   jax.ShapeDtypeStruct((t, 2 * d), BF16), jax.ShapeDtypeStruct((1, 2 * d), F32)],
        compiler_params=_params(("arbitrary",)), name="merge_bwd")(dmerged, y_attn, y_sgu, z_g, z_g, b_gate, b_gate)


def _sgu_bwd(z_uv, ds_out, gs, ws, b_col):
    t = z_uv.shape[0]
    sw = z_uv.shape[1] // 2
    groups = sw // SGU_GROUP
    tr = _pick(t, 256, CHUNK)

    def body(u_ref, v_ref, d_ref, gs_ref, ws_ref, b_ref, dz_ref, dws_ref, db_ref, dgs_ref, dvn_ref):
        @pl.when(pl.program_id(0) == 0)
        def _():
            dws_ref[...] = jnp.zeros_like(dws_ref)
            db_ref[...] = jnp.zeros_like(db_ref)
            dgs_ref[...] = jnp.zeros_like(dgs_ref)

        v, dgelu_v = _gelu_and_grad(v_ref[...])
        r = _rms_scale(v)
        vhat = v * r
        gsv = gs_ref[...]
        vn = (vhat * gsv).astype(BF16)
        tri = _tril_mask()
        for g in range(groups):
            wg = jnp.where(tri, ws_ref[g], 0.0).astype(BF16)
            cols = slice(g * SGU_GROUP, (g + 1) * SGU_GROUP)
            for c in range(tr // CHUNK):
                rows = slice(c * CHUNK, (c + 1) * CHUNK)
                vn_cg = vn[rows, cols]
                mixed = jnp.dot(wg, vn_cg, preferred_element_type=F32) + b_ref[g]
                u, dgelu_u = _gelu_and_grad(u_ref[rows, cols])
                dso = d_ref[rows, cols]
                dz_ref[rows, cols] = (dso * mixed * dgelu_u).astype(BF16)
                dmixed = dso * u
                db_ref[g] += jnp.sum(dmixed, axis=1, keepdims=True)
                dmixed_b = dmixed.astype(BF16)
                dws_ref[g] += jnp.where(
                    tri, lax.dot_general(dmixed_b, vn_cg, (((1,), (1,)), ((), ())), preferred_element_type=F32), 0.0)
                dvn_ref[rows, cols] = lax.dot_general(wg, dmixed_b, (((0,), (0,)), ((), ())), preferred_element_type=F32)
        dvn = dvn_ref[...]
        dv, dgs_rows = _rms_bwd(vhat, r, gsv, dvn)
        dz_ref[:, sw:2 * sw] = (dv * dgelu_v).astype(BF16)
        dgs_ref[...] += jnp.sum(dgs_rows, axis=0, keepdims=True)

    return pl.pallas_call(
        body, grid=(t // tr,),
        in_specs=[_rows(tr, sw, 0), _rows(tr, sw, 1), _rows(tr, sw), _full((1, sw)), _full(ws.shape), _full(b_col.shape)],
        out_specs=[_rows(tr, 2 * sw), _full(ws.shape), _full(b_col.shape), _full((1, sw))],
        out_shape=[jax.ShapeDtypeStruct((t, 2 * sw), BF16), jax.ShapeDtypeStruct(ws.shape, F32),
                   jax.ShapeDtypeStruct(b_col.shape, F32), jax.ShapeDtypeStruct((1, sw), F32)],
        scratch_shapes=[pltpu.VMEM((tr, sw), F32)],
        compiler_params=_params(("arbitrary",)), name="sgu_bwd")(z_uv, z_uv, ds_out, gs, ws, b_col)


def _lat_bwd(z_lat, qg, kvg, dqn, dkvn, dkpe_heads, cos, sin, ql, kvl):
    t, w = z_lat.shape
    heads = dkpe_heads.shape[0]
    tr = _row_tile(t, w + heads * LANES, 3)

    def body(z_ref, qg_ref, kvg_ref, dq_ref, dkv_ref, dk_ref, cos_ref, sin_ref, dz_ref, dqg_ref, dkvg_ref):
        @pl.when(pl.program_id(0) == 0)
        def _():
            dqg_ref[...] = jnp.zeros_like(dqg_ref)
            dkvg_ref[...] = jnp.zeros_like(dkvg_ref)

        q = z_ref[:, 0:ql]
        r = _rms_scale(q)
        dx, dg_rows = _rms_bwd(q * r, r, qg_ref[...], dq_ref[...])
        dz_ref[:, 0:ql] = dx.astype(BF16)
        dqg_ref[...] += jnp.sum(dg_rows, axis=0, keepdims=True)
        kv = z_ref[:, ql:ql + kvl]
        r = _rms_scale(kv)
        dx, dg_rows = _rms_bwd(kv * r, r, kvg_ref[...], dkv_ref[...])
        dz_ref[:, ql:ql + kvl] = dx.astype(BF16)
        dkvg_ref[...] += jnp.sum(dg_rows, axis=0, keepdims=True)
        dk = dk_ref[0]
        for h in range(1, heads):
            dk = dk + dk_ref[h]
        dz_ref[:, ql + kvl:ql + kvl + LANES] = _rope_bwd(dk, cos_ref[...], sin_ref[...]).astype(BF16)

    return pl.pallas_call(
        body, grid=(t // tr,),
        in_specs=[_rows(tr, w), _full((1, ql)), _full((1, kvl)), _rows(tr, ql), _rows(tr, kvl),
                  pl.BlockSpec((heads, tr, LANES), lambda i: (0, i, 0)), _rows(tr, LANES), _rows(tr, LANES)],
        out_specs=[_rows(tr, w), _full((1, ql)), _full((1, kvl))],
        out_shape=[jax.ShapeDtypeStruct((t, w), BF16), jax.ShapeDtypeStruct((1, ql), F32),
                   jax.ShapeDtypeStruct((1, kvl), F32)],
        compiler_params=_params(("arbitrary",)), name="lat_bwd")(z_lat, qg, kvg, dqn, dkvn, dkpe_heads, cos, sin)


_NT = (((1,), (1,)), ((), ()))


def _attn_scale():
    return (QK_NOPE + QK_ROPE) ** -0.5


def _heads_per_step(heads, wanted):
    return wanted if heads % wanted == 0 else 1


def _attn_fwd(q_c, kv, kpe, comm=None):
    t = q_c.shape[0]
    heads = q_c.shape[1] // HEAD_PAD
    tq = _pick(t, ATTN_TILE)
    nq = t // tq
    scale = _attn_scale()
    to_log2 = scale * math.log2(math.e)
    tn_dims = (((0,), (0,)), ((), ()))

    hps = _heads_per_step(heads, HEADS_PER_STEP[0])

    def body(q_ref, kv_ref, kpe_ref, o_ref, ob_ref, lse_ref, m_sc, l_sc, acc_sc):
        qi, ki = pl.program_id(1), pl.program_id(2)

        @pl.when(ki == 0)
        def _():
            m_sc[...] = jnp.full_like(m_sc, NEG_BIG)
            l_sc[...] = jnp.zeros_like(l_sc)
            acc_sc[...] = jnp.zeros_like(acc_sc)

        def step(diagonal):
            for u in range(hps):
                lo = u * HEAD_PAD
                kc = jnp.concatenate([kv_ref[:, lo:lo + QK_NOPE], kpe_ref[...]], axis=1)
                st = lax.dot_general(kc, q_ref[:, lo:lo + HEAD_PAD], _NT, preferred_element_type=F32)
                if diagonal:
                    krow = lax.broadcasted_iota(jnp.int32, st.shape, 0)
                    qcol = lax.broadcasted_iota(jnp.int32, st.shape, 1)
                    st = jnp.where(qcol >= krow, st, NEG_BIG)
                m_prev = m_sc[u]
                m_new = jnp.maximum(m_prev, jnp.max(st, axis=0, keepdims=True))
                alpha = jnp.exp2((m_prev - m_new) * to_log2)
                pt = jnp.exp2((st - m_new) * to_log2)
                l_sc[u] = alpha * l_sc[u] + jnp.sum(pt, axis=0, keepdims=True)
                acc_sc[u] = alpha * acc_sc[u] + lax.dot_general(
                    kv_ref[:, lo + QK_NOPE:lo + HEAD_PAD], pt.astype(BF16), tn_dims, preferred_element_type=F32)
                m_sc[u] = m_new

        @pl.when(ki < qi)
        def _():
            step(False)

        @pl.when(ki == qi)
        def _():
            step(True)
            for u in range(hps):
                o = (acc_sc[u] / l_sc[u]).T
                o_ref[:, u * V_HEAD:(u + 1) * V_HEAD] = o
                ob_ref[:, u * V_HEAD:(u + 1) * V_HEAD] = o.astype(BF16)
                lse_ref[u] = m_sc[u] * scale + jnp.log(l_sc[u])

    omap = lambda g, qi, ki: (qi, g)
    outs, comm_outs = _call(
        body, grid=(heads // hps, nq, nq),
        in_specs=[pl.BlockSpec((tq, hps * HEAD_PAD), omap),
                  pl.BlockSpec((tq, hps * HEAD_PAD), lambda g, qi, ki: (jnp.minimum(ki, qi), g)),
                  pl.BlockSpec((tq, LANES), lambda g, qi, ki: (jnp.minimum(ki, qi), 0))],
        out_specs=[pl.BlockSpec((tq, hps * V_HEAD), omap), pl.BlockSpec((tq, hps * V_HEAD), omap),
                   pl.BlockSpec((hps, 1, tq), lambda g, qi, ki: (g, 0, qi))],
        out_shape=[jax.ShapeDtypeStruct((t, heads * V_HEAD), F32), jax.ShapeDtypeStruct((t, heads * V_HEAD), BF16),
                   jax.ShapeDtypeStruct((heads, 1, t), F32)],
        scratch_shapes=[pltpu.VMEM((hps, 1, tq), F32), pltpu.VMEM((hps, 1, tq), F32),
                        pltpu.VMEM((hps, V_HEAD, tq), F32)],
        sem=("parallel", "parallel", "arbitrary"), name="attn_fwd", args=(q_c, kv, kpe), comm=comm)
    return outs[0], outs[1], outs[2], comm_outs


def _attn_bwd(q_c, kv, kpe, o, do, lse_row, cos, sin, comm=None):
    t = q_c.shape[0]
    heads = q_c.shape[1] // HEAD_PAD
    tk = _pick(t, ATTN_TILE)
    nk = t // tk
    scale = _attn_scale()
    tn_dims = (((0,), (0,)), ((), ()))

    hps = _heads_per_step(heads, HEADS_PER_STEP[1])

    def body(q_ref, kv_ref, kpe_ref, do_ref, lse_ref, o_ref, cos_ref, sin_ref, dq_ref, dkv_ref, dkpe_ref,
             dk_sc, dv_sc, delta_sc, dq_sc):
        ki, qi = pl.program_id(1), pl.program_id(2)

        @pl.when(jnp.logical_and(ki == 0, qi == 0))
        def _():
            dq_sc[...] = jnp.zeros_like(dq_sc)

        @pl.when(qi == 0)
        def _():
            dk_sc[...] = jnp.zeros_like(dk_sc)
            dv_sc[...] = jnp.zeros_like(dv_sc)

        @pl.when(ki == 0)
        def _():
            for u in range(hps):
                cols = slice(u * V_HEAD, (u + 1) * V_HEAD)
                delta_sc[qi * hps + u] = jnp.sum((do_ref[:, cols] * o_ref[:, cols]).T, axis=0, keepdims=True)

        def step(diagonal):
            for u in range(hps):
                lo = u * HEAD_PAD
                kc = jnp.concatenate([kv_ref[:, lo:lo + QK_NOPE], kpe_ref[...]], axis=1)
                q = q_ref[:, lo:lo + HEAD_PAD]
                st = lax.dot_general(kc, q, _NT, preferred_element_type=F32) * scale
                pt = jnp.exp(st - lse_ref[u])
                if diagonal:
                    krow = lax.broadcasted_iota(jnp.int32, st.shape, 0)
                    qcol = lax.broadcasted_iota(jnp.int32, st.shape, 1)
                    pt = jnp.where(qcol >= krow, pt, 0.0)
                do_b = do_ref[:, u * V_HEAD:(u + 1) * V_HEAD].astype(BF16)
                dv_sc[u] += jnp.dot(pt.astype(BF16), do_b, preferred_element_type=F32)
                dpt = lax.dot_general(kv_ref[:, lo + QK_NOPE:lo + HEAD_PAD], do_b, _NT, preferred_element_type=F32)
                dst = (pt * (dpt - delta_sc[qi * hps + u]) * scale).astype(BF16)
                dk_sc[u] += jnp.dot(dst, q, preferred_element_type=F32)
                rows = pl.ds(pl.multiple_of(qi * tk, tk), tk)
                dq_sc[rows, lo:lo + HEAD_PAD] += lax.dot_general(dst, kc, tn_dims, preferred_element_type=F32)

        @pl.when(qi > ki)
        def _():
            step(False)

        @pl.when(qi == ki)
        def _():
            step(True)

        @pl.when(qi == nk - 1)
        def _():
            for u in range(hps):
                lo = u * HEAD_PAD
                dkv_ref[:, lo:lo + QK_NOPE] = dk_sc[u, :, 0:QK_NOPE].astype(BF16)
                dkv_ref[:, lo + QK_NOPE:lo + HEAD_PAD] = dv_sc[u].astype(BF16)
                dkpe_ref[u] = dk_sc[u, :, QK_NOPE:QK_NOPE + LANES]

        @pl.when(jnp.logical_and(ki == nk - 1, qi == nk - 1))
        def _():
            cos, sin = cos_ref[...], sin_ref[...]
            for u in range(hps):
                lo = u * HEAD_PAD
                dq_ref[:, lo:lo + QK_NOPE] = dq_sc[:, lo:lo + QK_NOPE].astype(BF16)
                dq_ref[:, lo + QK_NOPE:lo + HEAD_PAD] = _rope_bwd(dq_sc[:, lo + QK_NOPE:lo + HEAD_PAD], cos, sin).astype(BF16)

    qclamp = lambda g, ki, qi: (jnp.maximum(qi, ki), g)
    outs, comm_outs = _call(
        body, grid=(heads // hps, nk, nk),
        in_specs=[pl.BlockSpec((tk, hps * HEAD_PAD), qclamp),
                  pl.BlockSpec((tk, hps * HEAD_PAD), lambda g, ki, qi: (ki, g)),
                  pl.BlockSpec((tk, LANES), lambda g, ki, qi: (ki, 0)),
                  pl.BlockSpec((tk, hps * V_HEAD), qclamp),
                  pl.BlockSpec((hps, 1, tk), lambda g, ki, qi: (g, 0, jnp.maximum(qi, ki))),
                  pl.BlockSpec((tk, hps * V_HEAD), lambda g, ki, qi: (jnp.where(ki == 0, qi, 0), g)),
                  _full((t, LANES)), _full((t, LANES))],
        out_specs=[pl.BlockSpec((t, hps * HEAD_PAD), lambda g, ki, qi: (0, g)),
                   pl.BlockSpec((tk, hps * HEAD_PAD), lambda g, ki, qi: (ki, g)),
                   pl.BlockSpec((hps, tk, LANES), lambda g, ki, qi: (g, ki, 0))],
        out_shape=[jax.ShapeDtypeStruct((t, heads * HEAD_PAD), BF16),
                   jax.ShapeDtypeStruct((t, heads * HEAD_PAD), BF16), jax.ShapeDtypeStruct((heads, t, LANES), F32)],
        scratch_shapes=[pltpu.VMEM((hps, tk, HEAD_PAD), F32), pltpu.VMEM((hps, tk, V_HEAD), F32),
                        pltpu.VMEM((nk * hps, 1, tk), F32), pltpu.VMEM((t, hps * HEAD_PAD), F32)],
        sem=("parallel", "arbitrary", "arbitrary"), name="attn_bwd",
        args=(q_c, kv, kpe, do, lse_row, o, cos, sin), comm=comm)
    return outs[0], outs[1], outs[2], comm_outs


def _local_step(x, pos_col, target, small, shards, opt):
    t = x.shape[0]
    ql, kvl = small["q_norm_g"].shape[1], small["kv_norm_g"].shape[1]
    sw = small["sgu_norm_g"].shape[1]
    heads = (shards["w_uq"].shape[1] * N_DEV) // (QK_NOPE + QK_ROPE)
    big = {}
    big.update(_compute_layout({"w_in": _all_gather([shards["w_in"]])[0]}, ql, kvl, heads, sw))
    half = QK_ROPE // 2
    lane = jnp.arange(LANES)
    inv_freq = ROPE_THETA ** (-jnp.arange(0, QK_ROPE, 2, dtype=F32) / QK_ROPE)
    inv_row = inv_freq[lane % half][None, :]
    sign_row = jnp.where((lane % QK_ROPE) < half, -1.0, 1.0).astype(F32)[None, :]
    cos, sin = _rope_tables(pos_col, inv_row, sign_row)
    ws = small["w_sgu"]
    b_col = small["b_sgu_col"]

    def arrived(names, bufs):
        big.update(_compute_layout(dict(zip(names, bufs)), ql, kvl, heads, sw))

    a = _norm_fwd(x, small["norm_mix_g"], "norm_mix_fwd")
    z_lat, g_qk = _mm(a, big["w_lat_t"], tb=True, name="z_lat",
                      comm=_gather_stage(1, [shards["w_uq"], shards["w_ukv"]]))
    z_uv, (g_sgu, *g_qk) = _mm(a, big["w_uv_t"], tb=True, name="z_uv",
                               comm=_join(_gather_stage(1, [shards["w_o_sgu"]]), _gather_stage(2, g_qk)))
    z_g, (g_attn, g_sgu, *g_qk) = _mm(
        a, big["w_g_t"], tb=True, name="z_g",
        comm=_join(_gather_stage(1, [shards["w_o_attn"]]), _gather_stage(2, [g_sgu]), _gather_stage(3, g_qk)))
    arrived(["w_uq", "w_ukv"], g_qk)
    qn, kvn, kpe = _lat_fwd(z_lat, small["q_norm_g"], small["kv_norm_g"], cos, sin, ql, kvl)
    q_c, (g_attn, g_sgu) = _mm(qn, big["w_uq"], name="q_up_rope", rope=(cos, sin),
                               comm=_join(_gather_stage(2, [g_attn]), _gather_stage(3, [g_sgu])))
    kv, (g_attn, g_out) = _mm(kvn, big["w_ukv"], out_dtype=BF16, name="kv_up",
                              comm=_join(_gather_stage(3, [g_attn]), _gather_stage(1, [shards["w_out"]])))
    arrived(["w_o_sgu", "w_o_attn"], [g_sgu, g_attn])
    attn, attn_b, lse, (w_gate, w_up) = _attn_fwd(
        q_c, kv, kpe, comm=_gather_stage(1, [shards["w_gate_ffn"], shards["w_up_ffn"]]))
    s_out = _sgu_fwd(z_uv, small["sgu_norm_g"], ws, b_col)
    y_sgu, (g_out,) = _mm(s_out, big["w_o_sgu"], name="y_sgu", comm=_gather_stage(2, [g_out]))
    y_attn, (w_gate, g_out) = _mm(attn_b, big["w_o_attn"], name="y_attn",
                                  comm=_join(_gather_stage(2, [w_gate]), _gather_stage(3, [g_out])))
    arrived(["w_out"], [g_out])
    merged, (w_up, w_gate) = _merge_fwd(y_attn, y_sgu, z_g, small["b_gate"],
                                        comm=_join(_gather_stage(2, [w_up]), _gather_stage(3, [w_gate])))
    h1, (w_up,) = _mm(merged, big["w_out"], add=x, name="h1", comm=_gather_stage(3, [w_up]))
    f = _norm_fwd(h1, small["norm_ffn_g"], "norm_ffn_fwd")
    gate, w_down = _mm(f, w_gate, tb=True, slab="n", name="ffn_gate", comm=_gather_stage(1, [shards["w_down_ffn"]]))
    up, w_down = _mm(f, w_up, tb=True, slab="n", name="ffn_up", comm=_gather_stage(2, w_down))
    ffn = gate.shape[2]
    gate, up = gate.reshape(N_DEV * t, ffn), up.reshape(N_DEV * t, ffn)
    act, (w_down,) = _swiglu_fwd(gate, up, comm=_gather_stage(3, w_down))
    act = act.reshape(N_DEV, t, ffn)
    h2 = _mm(act, w_down, slab="k", add=h1, name="h2")
    loss_row, dh2, dh2_b, d_norm_final = _loss_head(h2, small["norm_final_g"], target)

    def pair_sums(names, slabs, bufs):
        return [_pair_sum(g, b, "pair_sum_" + k) for k, g, b in zip(names, slabs, bufs)]

    parts, updates = {}, {}

    def update(names, label, comm=None):
        res, got = _adamw_shards([parts[k] for k in names], [opt[k] for k in names], "adamw_" + label, comm=comm)
        updates.update(zip(names, res))
        return got

    down_slabs = [_mm(act, dh2_b, ta=True, slab="m", out_dtype=BF16, name="dw_down")]
    dgu, bufs = _mm(dh2_b, w_down, tb=True, slab="n", tm=MM_TILE[0] // 2, name="dact_swiglu_bwd",
                    comm=_to_sibling(down_slabs), swiglu=(gate.reshape(N_DEV, t, ffn), up.reshape(N_DEV, t, ffn)))
    dgu = dgu.reshape(2 * N_DEV, t, ffn)
    down_pair = pair_sums(["w_down_ffn"], down_slabs, bufs)
    dw_gu, got = _mm(dgu, f, ta=True, slab="m", out_dtype=BF16, name="dw_gate_up", comm=_to_chips(down_pair))
    parts["w_down_ffn"] = got[0]
    gu_names = ["w_gate_ffn", "w_up_ffn"]
    df, bufs = _mm(dgu, w_gate, slab="k", name="df_gate", comm=_to_sibling([dw_gu, dw_gu], first=[0, N_DEV]))
    gu_pairs = [_pair_sum(dw_gu, b, "pair_sum_" + k, first=s0) for k, b, s0 in zip(gu_names, bufs, [0, N_DEV])]
    half = _pick(gu_pairs[1].shape[1], gu_pairs[1].shape[1] // 2, 2 * SUBLANES)
    df, up_parts = _mm(dgu, w_up, slab="k", a_slab0=N_DEV, add=df, name="df_up",
                       comm=_to_chips(gu_pairs[1:], rows=[("r", 0, half)]))
    dh1, dh1_b, d_norm_ffn = _norm_bwd(h1, small["norm_ffn_g"], df, dh2, "norm_ffn_bwd")
    dw_out = _mm(merged, dh1_b, ta=True, out_dtype=BF16, name="dw_out")
    out_slabs = [_slabs_from_rows(dw_out)]
    dmerged, bufs = _mm(dh1_b, big["w_out"], tb=True, name="dmerged", comm=_to_sibling(out_slabs))
    out_pair = pair_sums(["w_out"], out_slabs, bufs)
    dy_attn, dy_sgu, dz_g, d_b_gate = _merge_bwd(dmerged, y_attn, y_sgu, z_g, small["b_gate"])
    dw_o_sgu = _mm(s_out, dy_sgu, ta=True, out_dtype=BF16, name="dw_o_sgu")
    ds_out = _mm(dy_sgu, big["w_o_sgu"], tb=True, name="ds_out")
    dz_uv, d_ws, d_b_col, d_sgu_norm = _sgu_bwd(z_uv, ds_out, small["sgu_norm_g"], ws, b_col)
    dw_o_attn = _mm(attn_b, dy_attn, ta=True, out_dtype=BF16, name="dw_o_attn")
    mix_names = ["w_o_sgu", "w_o_attn"]
    mix_slabs = [_slabs_from_cols(dw_o_sgu), _slabs_from_rows(dw_o_attn)]
    dattn, bufs = _mm(dy_attn, big["w_o_attn"], tb=True, name="dattn", comm=_to_sibling(mix_slabs))
    mix_pairs = pair_sums(mix_names, mix_slabs, bufs)
    rows = gu_pairs[1].shape[1]
    dq_p, dkv, dkpe_heads, got = _attn_bwd(
        q_c, kv, kpe, attn, dattn, lse, cos, sin,
        comm=_join(_to_chips(gu_pairs[:1]), _to_chips(gu_pairs[1:], rows=[("r", half, rows - half)], into=up_parts)))
    parts.update(zip(gu_names, got))
    dw_uq = _mm(qn, dq_p, ta=True, out_dtype=BF16, name="dw_uq")
    dw_ukv = _mm(kvn, dkv, ta=True, out_dtype=BF16, name="dw_ukv")
    dqn = _mm(dq_p, big["w_uq"], tb=True, name="dqn")
    dkvn = _mm(dkv, big["w_ukv"], tb=True, name="dkvn")
    dz_lat, d_q_norm, d_kv_norm = _lat_bwd(z_lat, small["q_norm_g"], small["kv_norm_g"], dqn, dkvn, dkpe_heads,
                                           cos, sin, ql, kvl)
    dw_g, got = _mm(dz_g, a, ta=True, out_dtype=BF16, name="dw_g", comm=_to_chips(out_pair))
    parts["w_out"] = got[0]
    dw_uv, got = _mm(dz_uv, a, ta=True, out_dtype=BF16, name="dw_uv", comm=_to_chips(mix_pairs[1:]))
    parts["w_o_attn"] = got[0]
    dw_lat, got = _mm(dz_lat, a, ta=True, out_dtype=BF16, name="dw_lat", comm=_to_chips(mix_pairs[:1]))
    parts["w_o_sgu"] = got[0]
    lat = ql + kvl + QK_ROPE
    dw_uq_cols = dw_uq.reshape(ql, heads, HEAD_PAD)[:, :, :QK_NOPE + QK_ROPE].reshape(ql, heads * (QK_NOPE + QK_ROPE))
    in_names = ["w_uq", "w_ukv", "w_in"]
    in_slabs = [_slabs_from_cols(dw_uq_cols), _slabs_from_cols(dw_ukv),
                _slabs_from_rows(jnp.concatenate([dw_lat[:lat], dw_uv, dw_g], axis=0))]
    da = _mm(dz_lat, big["w_lat_t"], name="da_lat")
    da, bufs = _mm(dz_uv, big["w_uv_t"], add=da, name="da_uv", comm=_to_sibling(in_slabs))
    uq_pair, ukv_pair, in_pair = pair_sums(in_names, in_slabs, bufs)
    cols = in_pair.shape[2]
    first = ((cols * TAIL_SPLIT[0]) // TAIL_SPLIT[1]) // LANES * LANES or cols
    da, in_parts = _mm(dz_g, big["w_g_t"], add=da, name="da_g", comm=_to_chips([in_pair], rows=[("c", 0, first)]))
    grad_x, _, d_norm_mix, got = _norm_bwd(x, small["norm_mix_g"], da, dh1, "norm_mix_bwd",
                                          comm=_to_chips([uq_pair, ukv_pair]))
    parts["w_uq"], parts["w_ukv"] = got
    rest = _to_chips([in_pair], rows=[("c", first, cols - first)], into=in_parts) if first < cols else None
    got = update(["w_gate_ffn", "w_up_ffn", "w_down_ffn"], "ffn", comm=rest)
    parts["w_in"] = got[0] if rest is not None else in_parts[0]
    update(["w_out", "w_o_attn"], "mixer_out")
    for k in ("w_o_sgu", "w_uq", "w_ukv", "w_in"):
        update([k], k)

    gs = {"norm_mix_g": d_norm_mix, "b_gate": d_b_gate, "q_norm_g": d_q_norm, "kv_norm_g": d_kv_norm,
          "sgu_norm_g": d_sgu_norm, "w_sgu": d_ws, "b_sgu_col": d_b_col, "norm_ffn_g": d_norm_ffn,
          "norm_final_g": d_norm_final}
    return loss_row, grad_x, gs, updates


def _my_place():
    return lax.axis_index("x"), lax.axis_index("y"), lax.axis_index("c")


N_CHIPS = N_DEV // 2

_GATHER_SEMS = [[(3,), (3,), ()], [(4,), (4,)], [(1,), (1,)]]


def _halves(shape):
    r, c = shape
    if (c // 2) % LANES == 0:
        return ("c", 0, c // 2), ("c", c // 2, c // 2)
    assert (r // 2) % (2 * SUBLANES) == 0, shape
    return ("r", 0, r // 2), ("r", r // 2, r // 2)


def _gather_copies(stage, ins, outs, sems):
    x, y, c = _my_place()
    me, x_nbr, y_nbr, diag = 4 * x + 2 * y + c, 4 * (1 - x) + 2 * y + c, 4 * x + 2 * (1 - y) + c, 4 * (1 - x) + 2 * (1 - y) + c
    sibling = (x, y, 1 - c)

    def remote(w, k, src, dst, to):
        return pltpu.make_async_remote_copy(src_ref=src, dst_ref=dst, send_sem=sems[0].at[w, k], recv_sem=sems[1].at[w, k],
                                            device_id=to, device_id_type=MESH)

    out = []
    for w in range(len(outs)):
        if stage == 1:
            dst = outs[w].at[me]
            out.append(pltpu.make_async_copy(ins[w], dst, sems[2].at[w]))
            out += [remote(w, k, ins[w], dst, to) for k, to in enumerate([sibling, (1 - x, y, c), (x, 1 - y, c)])]
        elif stage == 2:
            first, second = _halves(outs[w].shape[1:])
            out.append(remote(w, 0, _window(ins[w], x_nbr, first), _window(outs[w], x_nbr, first), (x, 1 - y, c)))
            out.append(remote(w, 1, _window(ins[w], y_nbr, second), _window(outs[w], y_nbr, second), (1 - x, y, c)))
            out.append(remote(w, 2, ins[w].at[x_nbr], outs[w].at[x_nbr], sibling))
            out.append(remote(w, 3, ins[w].at[y_nbr], outs[w].at[y_nbr], sibling))
        else:
            out.append(remote(w, 0, ins[w].at[diag], outs[w].at[diag], sibling))
    return out


def _gather_stage(stage, arrays):
    n = len(arrays)

    def start(ins, outs, sems):
        for cp in _gather_copies(stage, ins, outs, sems):
            cp.start()

    def finish(ins, outs, sems):
        for cp in _gather_copies(stage, ins, outs, sems):
            cp.wait()

    shapes = [jax.ShapeDtypeStruct(((N_DEV,) + a.shape) if stage == 1 else a.shape, a.dtype) for a in arrays]
    return _Comm(arrays, shapes, [pltpu.SemaphoreType.DMA((n,) + s) for s in _GATHER_SEMS[stage - 1]], start, finish,
                 aliases=None if stage == 1 else {w: w for w in range(n)})


def _join(*comms):
    ins, shapes, sems, aliases, spans = [], [], [], {}, []
    for cm in comms:
        spans.append((len(ins), len(ins) + len(cm.ins), len(shapes), len(shapes) + len(cm.out_shapes),
                      len(sems), len(sems) + len(cm.sems)))
        aliases.update({len(ins) + i: len(shapes) + o for i, o in cm.aliases.items()})
        ins, shapes, sems = ins + cm.ins, shapes + cm.out_shapes, sems + cm.sems

    def each(half):
        def run(i_refs, o_refs, s_refs):
            for cm, (i0, i1, o0, o1, s0, s1) in zip(comms, spans):
                getattr(cm, half)(i_refs[i0:i1], o_refs[o0:o1], s_refs[s0:s1])
        return run

    return _Comm(ins, shapes, sems, each("start"), each("finish"), aliases)


def _all_gather(shards):
    n = len(shards)
    n_sems = [len(s) for s in _GATHER_SEMS]

    def body(*refs):
        ins, outs, sems = refs[:n], refs[n:2 * n], refs[2 * n:]
        s0 = 0
        for stage in (1, 2, 3):
            mine = sems[s0:s0 + n_sems[stage - 1]]
            s0 += n_sems[stage - 1]
            copies = _gather_copies(stage, ins if stage == 1 else outs, outs, mine)
            for cp in copies:
                cp.start()
            for cp in copies:
                cp.wait()

    any_spec = pl.BlockSpec(memory_space=pl.ANY)
    return pl.pallas_call(
        body, in_specs=[any_spec] * n, out_specs=[any_spec] * n,
        out_shape=[jax.ShapeDtypeStruct((N_DEV,) + s.shape, s.dtype) for s in shards],
        scratch_shapes=[pltpu.SemaphoreType.DMA((n,) + s) for stage in _GATHER_SEMS for s in stage],
        compiler_params=pltpu.CompilerParams(has_side_effects=True), name="all_gather_weights")(*shards)


def _to_sibling(grads, first=None):
    n = len(grads)
    first = first or [0] * n

    def copies(ins, outs, sems):
        x, y, c = _my_place()
        send_sems, recv_sems = sems
        return [pltpu.make_async_remote_copy(
            src_ref=ins[w].at[first[w] + 2 * i + (1 - c)], dst_ref=outs[w].at[i], send_sem=send_sems.at[w, i],
            recv_sem=recv_sems.at[w, i], device_id=(x, y, 1 - c), device_id_type=MESH)
            for w in range(n) for i in range(N_CHIPS)]

    def start(ins, outs, sems):
        for cp in copies(ins, outs, sems):
            cp.start()

    def finish(ins, outs, sems):
        for cp in copies(ins, outs, sems):
            cp.wait()

    return _Comm(grads, [jax.ShapeDtypeStruct((N_CHIPS,) + g.shape[1:], g.dtype) for g in grads],
                 [pltpu.SemaphoreType.DMA((n, N_CHIPS)), pltpu.SemaphoreType.DMA((n, N_CHIPS))], start, finish)


def _window(ref, slab, win):
    if win is None:
        return ref.at[slab]
    if win[0] == "r":
        return ref.at[slab, pl.ds(win[1], win[2])]
    return ref.at[slab, slice(None), pl.ds(win[1], win[2])]


def _to_chips(parts, rows=None, into=None):
    n = len(parts)
    rows = rows or [None] * n

    def copies(ins, outs, sems):
        x, y, c = _my_place()
        send_sems, recv_sems, local_sems = sems
        mine = 2 * x + y
        chips = [(1 - x, y), (x, 1 - y), (1 - x, 1 - y)]
        remote = [pltpu.make_async_remote_copy(
            src_ref=_window(ins[w], 2 * cx + cy, rows[w]), dst_ref=_window(outs[w], mine, rows[w]),
            send_sem=send_sems.at[w, j], recv_sem=recv_sems.at[w, j], device_id=(cx, cy, c), device_id_type=MESH)
            for w in range(n) for j, (cx, cy) in enumerate(chips)]
        local = [pltpu.make_async_copy(_window(ins[w], mine, rows[w]), _window(outs[w], mine, rows[w]),
                                       local_sems.at[w]) for w in range(n)]
        return remote + local

    def start(ins, outs, sems):
        for cp in copies(ins, outs, sems):
            cp.start()

    def finish(ins, outs, sems):
        for cp in copies(ins, outs, sems):
            cp.wait()

    return _Comm(list(parts) + list(into or []), [jax.ShapeDtypeStruct(p.shape, p.dtype) for p in parts],
                 [pltpu.SemaphoreType.DMA((n, N_CHIPS - 1)), pltpu.SemaphoreType.DMA((n, N_CHIPS - 1)),
                  pltpu.SemaphoreType.DMA((n,))], start, finish,
                 aliases={n + w: w for w in range(n)} if into else None)


def _pair_sum(g, buf, name, first=0):
    _, r, c = g.shape
    tr, tc = _shard_tile(r, c, 4 * SHARD_TILE_ELEMS, 1024)
    core = (lax.axis_index("c") + first).astype(jnp.int32).reshape(1)

    def body(core_ref, g_ref, b_ref, o_ref):
        o_ref[...] = (g_ref[...].astype(F32) + b_ref[...].astype(F32)).astype(o_ref.dtype)

    blk = (1, tr, tc)
    return pl.pallas_call(
        body, grid_spec=pltpu.PrefetchScalarGridSpec(
            num_scalar_prefetch=1, grid=(N_CHIPS, r // tr, c // tc),
            in_specs=[pl.BlockSpec(blk, lambda i, j, l, core_ref: (2 * i + core_ref[0], j, l)),
                      pl.BlockSpec(blk, lambda i, j, l, core_ref: (i, j, l))],
            out_specs=pl.BlockSpec(blk, lambda i, j, l, core_ref: (i, j, l))),
        out_shape=jax.ShapeDtypeStruct(buf.shape, buf.dtype),
        compiler_params=_params(("parallel", "parallel", "parallel")), name=name)(core, g, buf)


def _all_reduce_pack(pack):
    r = pack.shape[0]

    def body(x_ref, out_ref, gath_ref, send_sems, recv_sems, local_sem):
        x, y, c = _my_place()
        me, sibling = (x, y, c), (x, y, 1 - c)
        chips = [(1 - x, y), (x, 1 - y), (1 - x, 1 - y)]

        def slab(place):
            return gath_ref.at[4 * place[0] + 2 * place[1] + place[2]]

        def copy(k, place, to, src=None):
            return pltpu.make_async_remote_copy(
                src_ref=slab(place) if src is None else src, dst_ref=slab(place),
                send_sem=send_sems.at[k], recv_sem=recv_sems.at[k], device_id=to, device_id_type=MESH)

        mine = pltpu.make_async_copy(x_ref, slab(me), local_sem)
        mine.start()
        first = [copy(0, me, sibling, src=x_ref)]
        first += [copy(1 + j, me, (*chip, c), src=x_ref) for j, chip in enumerate(chips)]
        for cp in first:
            cp.start()
        passed = [copy(4 + j, (*chip, c), sibling) for j, chip in enumerate(chips)]
        for j, chip in enumerate(chips):
            copy(1 + j, (*chip, c), me).wait_recv()
            passed[j].start()
        copy(0, sibling, me).wait_recv()
        for j, chip in enumerate(chips):
            copy(4 + j, (*chip, 1 - c), me).wait_recv()
        for cp in first + passed:
            cp.wait_send()
        mine.wait()
        acc = gath_ref[0]
        for i in range(1, N_DEV):
            acc = acc + gath_ref[i]
        out_ref[...] = acc

    vmem = pl.BlockSpec(memory_space=pltpu.VMEM)
    return pl.pallas_call(
        body, in_specs=[vmem], out_specs=vmem, out_shape=jax.ShapeDtypeStruct(pack.shape, F32),
        scratch_shapes=[pltpu.VMEM((N_DEV, r, LANES), F32), pltpu.SemaphoreType.DMA((7,)),
                        pltpu.SemaphoreType.DMA((7,)), pltpu.SemaphoreType.DMA],
        compiler_params=pltpu.CompilerParams(vmem_limit_bytes=VMEM_LIMIT), name="all_reduce_small")(pack)


def _adamw_math(w, g, m, v):
    m = ADAM_B1 * m + (1.0 - ADAM_B1) * g
    v = ADAM_B2 * v + (1.0 - ADAM_B2) * (g * g)
    m_hat = m / (1.0 - ADAM_B1 ** ADAM_STEP)
    v_hat = v / (1.0 - ADAM_B2 ** ADAM_STEP)
    delta = -ADAM_LR * (m_hat / (jnp.sqrt(v_hat) + ADAM_EPS) + ADAM_WD * w)
    return delta, m, v


def _adamw_shards(parts, opts, name, comm=None):
    r, c = opts[0][0].shape
    n_parts, k = parts[0].shape[0], len(parts)
    tr, tc = _shard_tile(r, c, SHARD_TILE_ELEMS // k)

    def body(*refs):
        ins, outs = refs[:4 * k], refs[4 * k:]
        for s in range(k):
            p_ref, w_ref, m_ref, v_ref = ins[4 * s:4 * s + 4]
            g_ref, d_ref, nm_ref, nv_ref = outs[4 * s:4 * s + 4]
            g = p_ref[0].astype(F32)
            for i in range(1, n_parts):
                g = g + p_ref[i].astype(F32)
            g_ref[...] = g
            d_ref[...], nm_ref[...], nv_ref[...] = _adamw_math(w_ref[...], g, m_ref[...], v_ref[...])

    spec = pl.BlockSpec((tr, tc), lambda i, j: (i, j))
    args = [a for p, o in zip(parts, opts) for a in (p,) + tuple(o)]
    outs, comm_outs = _call(
        body, grid=(r // tr, c // tc),
        in_specs=[pl.BlockSpec((n_parts, tr, tc), lambda i, j: (0, i, j)), spec, spec, spec] * k,
        out_specs=[spec] * (4 * k), out_shape=[jax.ShapeDtypeStruct((r, c), F32)] * (4 * k),
        sem=("parallel", "parallel"), name=name, args=args, comm=comm)
    return [outs[4 * s:4 * s + 4] for s in range(k)], comm_outs


def _adamw_pack(g, w, m, v):
    r, c = w.shape

    def body(g_ref, w_ref, m_ref, v_ref, d_ref, nm_ref, nv_ref):
        d_ref[...], nm_ref[...], nv_ref[...] = _adamw_math(w_ref[...], g_ref[...], m_ref[...], v_ref[...])

    return pl.pallas_call(
        body, in_specs=[_full((r, c))] * 4, out_specs=[_full((r, c))] * 3, grid=(1,),
        out_shape=[jax.ShapeDtypeStruct((r, c), F32)] * 3,
        compiler_params=_params(("arbitrary",)), name="adamw_small")(g, w, m, v)


def _cols_from_slabs(g):
    return jnp.transpose(g, (1, 0, 2)).reshape(g.shape[1], N_DEV * g.shape[2])


def _slabs_from_cols(w):
    r, c8 = w.shape
    return jnp.transpose(w.reshape(r, N_DEV, c8 // N_DEV), (1, 0, 2))


def _rows_from_slabs(g):
    return g.reshape(N_DEV * g.shape[1], g.shape[2])


def _slabs_from_rows(w):
    return w.reshape(N_DEV, w.shape[0] // N_DEV, w.shape[1])


def _compute_layout(gathered, ql, kvl, heads, sw):
    out = {}
    for k, g in gathered.items():
        if k == "w_in":
            lat = ql + kvl + QK_ROPE
            w_in_t = _rows_from_slabs(g)
            out["w_lat_t"] = jnp.pad(w_in_t[:lat], ((0, LANES - QK_ROPE), (0, 0)))
            out["w_uv_t"] = w_in_t[lat:lat + 2 * sw]
            out["w_g_t"] = w_in_t[lat + 2 * sw:]
        elif k == "w_uq":
            per_head = _cols_from_slabs(g).reshape(ql, heads, QK_NOPE + QK_ROPE)
            pad = HEAD_PAD - QK_NOPE - QK_ROPE
            out["w_uq"] = jnp.pad(per_head, ((0, 0), (0, 0), (0, pad))).reshape(ql, heads * HEAD_PAD)
        elif k in ("w_o_attn", "w_out", "w_down_ffn"):
            out[k.removesuffix("_ffn")] = _rows_from_slabs(g)
        else:
            out[k.removesuffix("_ffn")] = _cols_from_slabs(g)
    return out


_SMALL =["norm_mix_g", "b_gate", "q_norm_g", "kv_norm_g", "sgu_norm_g", "w_sgu", "b_sgu", "norm_ffn_g", "norm_final_g"]
_BIG = ["w_in", "w_uq", "w_ukv", "w_o_attn", "w_o_sgu", "w_out", "w_gate_ffn", "w_up_ffn", "w_down_ffn"]
_TRANSPOSED = ("w_in", "w_gate_ffn", "w_up_ffn")
_ORDER = ["norm_mix_g", "w_in", "b_gate", "q_norm_g", "w_uq", "kv_norm_g", "w_ukv", "w_o_attn", "sgu_norm_g", "w_sgu",
          "b_sgu", "w_o_sgu", "w_out", "norm_ffn_g", "w_gate_ffn", "w_up_ffn", "w_down_ffn", "norm_final_g"]


def _pack_rows(parts):
    rows, sizes = [], []
    for p in parts:
        flat = p.reshape(-1)
        n = flat.shape[0]
        padded = -(-n // (SUBLANES * LANES)) * (SUBLANES * LANES)
        rows.append(jnp.pad(flat, (0, padded - n)).reshape(padded // LANES, LANES))
        sizes.append((n, padded // LANES))
    return jnp.concatenate(rows, axis=0), sizes


def _unpack_rows(pack, sizes, shapes):
    out, r0 = [], 0
    for (n, nr), shp in zip(sizes, shapes):
        out.append(pack[r0:r0 + nr].reshape(-1)[:n].reshape(shp))
        r0 += nr
    return out


def kernel(x, positions, norm_mix_g, w_in, b_gate, q_norm_g, w_uq, kv_norm_g, w_ukv, w_o_attn, sgu_norm_g, w_sgu, b_sgu, w_o_sgu, w_out, norm_ffn_g, w_gate_ffn, w_up_ffn, w_down_ffn, norm_final_g, loss_target, m_norm_mix_g, m_w_in, m_b_gate, m_q_norm_g, m_w_uq, m_kv_norm_g, m_w_ukv, m_w_o_attn, m_sgu_norm_g, m_w_sgu, m_b_sgu, m_w_o_sgu, m_w_out, m_norm_ffn_g, m_w_gate_ffn, m_w_up_ffn, m_w_down_ffn, m_norm_final_g, v_norm_mix_g, v_w_in, v_b_gate, v_q_norm_g, v_w_uq, v_kv_norm_g, v_w_ukv, v_w_o_attn, v_sgu_norm_g, v_w_sgu, v_b_sgu, v_w_o_sgu, v_w_out, v_norm_ffn_g, v_w_gate_ffn, v_w_up_ffn, v_w_down_ffn, v_norm_final_g):
    wts = dict(norm_mix_g=norm_mix_g, w_in=w_in, b_gate=b_gate, q_norm_g=q_norm_g, w_uq=w_uq, kv_norm_g=kv_norm_g,
               w_ukv=w_ukv, w_o_attn=w_o_attn, sgu_norm_g=sgu_norm_g, w_sgu=w_sgu, b_sgu=b_sgu, w_o_sgu=w_o_sgu,
               w_out=w_out, norm_ffn_g=norm_ffn_g, w_gate_ffn=w_gate_ffn, w_up_ffn=w_up_ffn, w_down_ffn=w_down_ffn,
               norm_final_g=norm_final_g)
    mom = dict(norm_mix_g=m_norm_mix_g, w_in=m_w_in, b_gate=m_b_gate, q_norm_g=m_q_norm_g, w_uq=m_w_uq,
               kv_norm_g=m_kv_norm_g, w_ukv=m_w_ukv, w_o_attn=m_w_o_attn, sgu_norm_g=m_sgu_norm_g, w_sgu=m_w_sgu,
               b_sgu=m_b_sgu, w_o_sgu=m_w_o_sgu, w_out=m_w_out, norm_ffn_g=m_norm_ffn_g, w_gate_ffn=m_w_gate_ffn,
               w_up_ffn=m_w_up_ffn, w_down_ffn=m_w_down_ffn, norm_final_g=m_norm_final_g)
    var = dict(norm_mix_g=v_norm_mix_g, w_in=v_w_in, b_gate=v_b_gate, q_norm_g=v_q_norm_g, w_uq=v_w_uq,
               kv_norm_g=v_kv_norm_g, w_ukv=v_w_ukv, w_o_attn=v_w_o_attn, sgu_norm_g=v_sgu_norm_g, w_sgu=v_w_sgu,
               b_sgu=v_b_sgu, w_o_sgu=v_w_o_sgu, w_out=v_w_out, norm_ffn_g=v_norm_ffn_g, w_gate_ffn=v_w_gate_ffn,
               w_up_ffn=v_w_up_ffn, w_down_ffn=v_w_down_ffn, norm_final_g=v_norm_final_g)

    t, d = x.shape[1], x.shape[2]
    ql, kvl = q_norm_g.shape[1], kv_norm_g.shape[1]
    heads = (w_uq.shape[2] * N_DEV) // (QK_NOPE + QK_ROPE)
    sw = sgu_norm_g.shape[1]

    def shard(a, k):
        return a[0].T if k in _TRANSPOSED else a[0]

    def unshard(a, k):
        return (a.T if k in _TRANSPOSED else a).reshape(wts[k].shape)

    opt = {k: (shard(wts[k], k), shard(mom[k], k), shard(var[k], k)) for k in _BIG}
    shards = {k: opt[k][0].astype(BF16) for k in _BIG}
    small = {
        "norm_mix_g": norm_mix_g, "b_gate": b_gate, "q_norm_g": q_norm_g, "kv_norm_g": kv_norm_g,
        "sgu_norm_g": sgu_norm_g, "w_sgu": w_sgu[0], "b_sgu_col": b_sgu[0][:, :, None], "norm_ffn_g": norm_ffn_g,
        "norm_final_g": norm_final_g[None, :],
    }

    loss_row, grad_x, gs, updates = _local_step(x[0], positions.reshape(t, 1), loss_target[0], small, shards, opt)
    grads, deltas, new_m, new_v = {}, {}, {}, {}
    for k in _BIG:
        grads[k], deltas[k], new_m[k], new_v[k] = (unshard(a, k) for a in updates[k])

    small_grads = [gs["norm_mix_g"], gs["b_gate"], gs["q_norm_g"], gs["kv_norm_g"], gs["sgu_norm_g"], gs["w_sgu"],
                   gs["b_sgu_col"], gs["norm_ffn_g"], gs["norm_final_g"]]
    pack, sizes = _pack_rows([loss_row] + small_grads)
    total = _all_reduce_pack(pack)
    shapes = [(1, LANES)] + [wts[k].shape for k in _SMALL]
    unpacked = _unpack_rows(total, sizes, shapes)
    loss = unpacked[0][0, 0]
    for k, g in zip(_SMALL, unpacked[1:]):
        grads[k] = g
    g_pack = total[sizes[0][1]:]
    w_pack, _ = _pack_rows([wts[k] for k in _SMALL])
    m_pack, _ = _pack_rows([mom[k] for k in _SMALL])
    v_pack, _ = _pack_rows([var[k] for k in _SMALL])
    d_pack, nm_pack, nv_pack = _adamw_pack(g_pack, w_pack, m_pack, v_pack)
    small_shapes = [wts[k].shape for k in _SMALL]
    for store, pk in ((deltas, d_pack), (new_m, nm_pack), (new_v, nv_pack)):
        for k, a in zip(_SMALL, _unpack_rows(pk, sizes[1:], small_shapes)):
            store[k] = a

    return (loss, grad_x[None], *[grads[k] for k in _ORDER], *[deltas[k] for k in _ORDER],
            *[new_m[k] for k in _ORDER], *[new_v[k] for k in _ORDER])
```

```python
import functools
import math

import jax
import jax.numpy as jnp
from jax import lax
from jax.experimental import pallas as pl
from jax.experimental.pallas import tpu as pltpu

F32 = jnp.float32
BF16 = jnp.bfloat16

N_DEV = 8
QK_NOPE = 128
QK_ROPE = 64
V_HEAD = 128
HEAD_PAD = 256
ROPE_THETA = 10000.0
CHUNK = 128
SGU_GROUP = 128
RMS_EPS = 1e-6
LANES = 128
SUBLANES = 8

ADAM_LR = 0.001
ADAM_B1 = 0.9
ADAM_B2 = 0.999
ADAM_EPS = 1e-08
ADAM_WD = 0.01
ADAM_STEP = 10

VMEM_LIMIT = 48 * 1024 * 1024
MM_TILE = (2048, 512, 2048)
MM_TILE_TA = (512, 2048)
ATTN_TILE = 512
HEADS_PER_STEP = (4, 4)
ROW_KERNEL_BYTES = 24 * 1024 * 1024
SHARD_TILE_ELEMS = 256 * 1024
SLABS_PER_STEP = 2
TAIL_SPLIT = (3, 8)
NEG_BIG = -1e30
MESH = pl.DeviceIdType.MESH


def _pick(n, target, mult=LANES):
    best = None
    d = mult
    while d <= min(n, target):
        if n % d == 0:
            best = d
        d += mult
    return best or n


def _row_tile(t, width, n_blocks, mult=2 * SUBLANES):
    return _pick(t, max(mult, ROW_KERNEL_BYTES // (3 * n_blocks * width * 4)), mult)


def _shard_tile(r, c, elems=SHARD_TILE_ELEMS, max_rows=256):
    tr = _pick(r, max_rows, 2 * SUBLANES)
    return tr, _pick(c, max(LANES, elems // tr))


def _params(sem):
    return pltpu.CompilerParams(dimension_semantics=sem, vmem_limit_bytes=VMEM_LIMIT)


def _full(shape):
    nd = len(shape)
    return pl.BlockSpec(shape, lambda *_: (0,) * nd)


def _rows(tr, w, cb=0):
    return pl.BlockSpec((tr, w), lambda i: (i, cb))


class _Comm:
    def __init__(self, ins, out_shapes, sems, start, finish, aliases=None):
        self.ins, self.out_shapes, self.sems, self.start, self.finish = list(ins), list(out_shapes), list(sems), start, finish
        self.aliases = dict(aliases or {})


def _call(body, *, grid, in_specs, out_specs, out_shape, scratch_shapes=(), sem, name, args, comm=None):
    if comm is None:
        outs = pl.pallas_call(body, grid=grid, in_specs=list(in_specs), out_specs=list(out_specs),
                              out_shape=list(out_shape), scratch_shapes=list(scratch_shapes),
                              compiler_params=_params(sem), name=name)(*args)
        return list(outs), []
    n_in, n_out, n_sc = len(in_specs), len(out_shape), len(scratch_shapes)
    nci, nco = len(comm.ins), len(comm.out_shapes)

    def hosted(*refs):
        ins, refs = refs[:n_in], refs[n_in:]
        cins, refs = refs[:nci], refs[nci:]
        outs, refs = refs[:n_out], refs[n_out:]
        couts, refs = refs[:nco], refs[nco:]
        scratch, csems = refs[:n_sc], refs[n_sc:]
        ids = [pl.program_id(i) for i in range(len(grid))]
        first = functools.reduce(jnp.logical_and, [i == 0 for i in ids])
        last = functools.reduce(jnp.logical_and, [i == g - 1 for i, g in zip(ids, grid)])

        @pl.when(first)
        def _():
            comm.start(cins, couts, csems)

        body(*ins, *outs, *scratch)

        @pl.when(last)
        def _():
            comm.finish(cins, couts, csems)

    any_spec = pl.BlockSpec(memory_space=pl.ANY)
    res = pl.pallas_call(
        hosted, grid=grid, in_specs=list(in_specs) + [any_spec] * nci, out_specs=list(out_specs) + [any_spec] * nco,
        out_shape=list(out_shape) + comm.out_shapes, scratch_shapes=list(scratch_shapes) + comm.sems,
        input_output_aliases={n_in + i: n_out + o for i, o in comm.aliases.items()},
        compiler_params=pltpu.CompilerParams(dimension_semantics=("arbitrary",) * len(grid),
                                             vmem_limit_bytes=VMEM_LIMIT, has_side_effects=True),
        name=name)(*args, *comm.ins)
    return list(res[:n_out]), list(res[n_out:])


def _swiglu_grads(g, u, d):
    s = 1.0 / (1.0 + jnp.exp(-g))
    return (d * u * (s * (1.0 + g * (1.0 - s)))).astype(BF16), (d * (g * s)).astype(BF16)


def _mm(a, b, *, ta=False, tb=False, add=None, out_dtype=F32, tm=None, tn=None, tk=None, name, comm=None,
        slab=None, a_slab0=0, swiglu=None, rope=None):
    sq = None
    if ta:
        tm, tn = tm or MM_TILE_TA[0], tn or MM_TILE_TA[1]
    if slab is None:
        m, k = (a.shape[1], a.shape[0]) if ta else a.shape
        n = b.shape[0] if tb else b.shape[1]
        assert k == (b.shape[1] if tb else b.shape[0]), (a.shape, b.shape, ta, tb)
        tm, tn, tk = _pick(m, tm or MM_TILE[0]), _pick(n, tn or MM_TILE[1]), _pick(k, tk or MM_TILE[2])
        if rope is not None:
            tn = _pick(n, max(tn, HEAD_PAD), HEAD_PAD)
        grid = (m // tm, n // tn, k // tk)
        a_spec = pl.BlockSpec((tk, tm), lambda i, j, kk: (kk, i)) if ta else pl.BlockSpec((tm, tk), lambda i, j, kk: (i, kk))
        b_spec = pl.BlockSpec((tn, tk), lambda i, j, kk: (j, kk)) if tb else pl.BlockSpec((tk, tn), lambda i, j, kk: (kk, j))
        o_spec, o_shape = pl.BlockSpec((tm, tn), lambda i, j, kk: (i, j)), (m, n)
    elif slab == "n":
        m, k = (a.shape[1], a.shape[0]) if ta else a.shape
        s, c = b.shape[0], (b.shape[1] if tb else b.shape[2])
        assert k == (b.shape[2] if tb else b.shape[1]), (a.shape, b.shape, ta, tb)
        tm, tn, tk = _pick(m, tm or MM_TILE[0]), c, _pick(k, tk or MM_TILE[2])
        grid = (m // tm, s, k // tk)
        a_spec = pl.BlockSpec((tk, tm), lambda i, j, kk: (kk, i)) if ta else pl.BlockSpec((tm, tk), lambda i, j, kk: (i, kk))
        b_spec = (pl.BlockSpec((sq, c, tk), lambda i, j, kk: (j, 0, kk)) if tb
                  else pl.BlockSpec((sq, tk, c), lambda i, j, kk: (j, kk, 0)))
        o_spec, o_shape = pl.BlockSpec((sq, tm, c), lambda i, j, kk: (j, i, 0)), (s, m, c)
    elif slab == "m":
        assert ta and not tb
        s, k, c = a.shape
        n = b.shape[1]
        assert k == b.shape[0], (a.shape, b.shape)
        tm, tn, tk = c, _pick(n, tn or MM_TILE[1]), _pick(k, tk or MM_TILE[2])
        grid = (s, n // tn, k // tk)
        a_spec = pl.BlockSpec((sq, tk, c), lambda i, j, kk: (i, kk, 0))
        b_spec = pl.BlockSpec((tk, tn), lambda i, j, kk: (kk, j))
        o_spec, o_shape = pl.BlockSpec((sq, c, tn), lambda i, j, kk: (i, 0, j)), (s, c, n)
    else:
        assert slab == "k" and not ta
        s, c = b.shape[0], (b.shape[2] if tb else b.shape[1])
        m, n = a.shape[1], (b.shape[1] if tb else b.shape[2])
        assert a.shape[2] == c and a.shape[0] >= a_slab0 + s, (a.shape, b.shape, a_slab0)
        tm, tn, tk = _pick(m, tm or MM_TILE[0]), _pick(n, tn or MM_TILE[1]), c
        per_step = SLABS_PER_STEP if (s % SLABS_PER_STEP == 0 and a_slab0 % SLABS_PER_STEP == 0) else 1
        first = a_slab0 // per_step
        grid = (m // tm, n // tn, s // per_step)
        a_spec = pl.BlockSpec((per_step, tm, c), lambda i, j, kk: (kk + first, i, 0))
        b_spec = (pl.BlockSpec((per_step, tn, c), lambda i, j, kk: (kk, j, 0)) if tb
                  else pl.BlockSpec((per_step, c, tn), lambda i, j, kk: (kk, 0, j)))
        o_spec, o_shape = pl.BlockSpec((tm, tn), lambda i, j, kk: (i, j)), (m, n)
    nk = grid[2]
    dims = (((0 if ta else 1,), (1 if tb else 0,)), ((), ()))

    def product(a_ref, b_ref):
        if slab != "k":
            return lax.dot_general(a_ref[...].astype(BF16), b_ref[...].astype(BF16), dims, preferred_element_type=F32)
        r = None
        for u in range(a_ref.shape[0]):
            p = lax.dot_general(a_ref[u].astype(BF16), b_ref[u].astype(BF16), dims, preferred_element_type=F32)
            r = p if r is None else r + p
        return r

    if swiglu is not None:
        assert slab == "n" and add is None
        o_block = pl.BlockSpec((2, sq, tm, c), lambda i, j, kk: (0, j, i, 0))
        o_shape, out_dtype = (2,) + o_shape, BF16

    if rope is not None:
        assert slab is None and add is None and swiglu is None and tn % HEAD_PAD == 0
        out_dtype = BF16
    extras = tuple(swiglu or ()) + tuple(rope or ())

    def body(*refs):
        a_ref, b_ref = refs[:2]
        add_ref = refs[2] if add is not None else None
        x0_ref, x1_ref = refs[2:4] if extras else (None, None)
        o_ref = refs[2 + (add is not None) + len(extras)]
        acc_ref = refs[-1] if nk > 1 else None

        def finish(r):
            if swiglu is not None:
                o_ref[0], o_ref[1] = _swiglu_grads(x0_ref[...], x1_ref[...], r)
                return
            if rope is not None:
                cos, sin = x0_ref[...], x1_ref[...]
                for h in range(tn // HEAD_PAD):
                    lo = h * HEAD_PAD
                    o_ref[:, lo:lo + QK_NOPE] = r[:, lo:lo + QK_NOPE].astype(BF16)
                    o_ref[:, lo + QK_NOPE:lo + HEAD_PAD] = _rope(r[:, lo + QK_NOPE:lo + HEAD_PAD], cos, sin).astype(BF16)
                return
            if add_ref is not None:
                r = r + add_ref[...].astype(F32)
            o_ref[...] = r.astype(o_ref.dtype)

        if nk == 1:
            finish(product(a_ref, b_ref))
            return
        kk = pl.program_id(2)

        @pl.when(kk == 0)
        def _():
            acc_ref[...] = product(a_ref, b_ref)

        if nk > 2:
            @pl.when(jnp.logical_and(kk > 0, kk < nk - 1))
            def _():
                acc_ref[...] += product(a_ref, b_ref)

        @pl.when(kk == nk - 1)
        def _():
            finish(acc_ref[...] + product(a_ref, b_ref))

    in_specs = [a_spec, b_spec] + ([o_spec] if add is not None else []) + ([o_spec] * 2 if swiglu is not None else [])
    if rope is not None:
        in_specs += [pl.BlockSpec((tm, LANES), lambda i, j, kk: (i, 0))] * 2
    args = (a, b) + ((add,) if add is not None else ()) + extras
    if swiglu is not None:
        o_spec = o_block
    outs, comm_outs = _call(
        body, grid=grid, in_specs=in_specs, out_specs=[o_spec],
        out_shape=[jax.ShapeDtypeStruct(o_shape, out_dtype)],
        scratch_shapes=[pltpu.VMEM((tm, tn), F32)] if nk > 1 else [],
        sem=("parallel", "parallel", "arbitrary"), name=name, args=args, comm=comm)
    return outs[0] if comm is None else (outs[0], comm_outs)


def _rms_scale(x):
    return lax.rsqrt(jnp.mean(x * x, axis=-1, keepdims=True) + RMS_EPS)


def _rms_bwd(xhat, r, g, dy):
    t = dy * g
    dx = r * (t - xhat * jnp.mean(t * xhat, axis=-1, keepdims=True))
    return dx, dy * xhat


_GELU_C = math.sqrt(2.0 / math.pi)


def _gelu(x):
    return x * (0.5 * (1.0 + jnp.tanh(_GELU_C * (x + 0.044715 * (x * x * x)))))


def _gelu_and_grad(x):
    t = jnp.tanh(_GELU_C * (x + 0.044715 * (x * x * x)))
    cdf = 0.5 * (1.0 + t)
    return x * cdf, cdf + x * (0.5 * (1.0 - t * t) * (_GELU_C * (1.0 + 3.0 * 0.044715 * (x * x))))


def _sigmoid(x):
    return 1.0 / (1.0 + jnp.exp(-x))


def _swap_halves(x):
    lane = lax.broadcasted_iota(jnp.int32, x.shape, 1)
    first = (lane % QK_ROPE) < (QK_ROPE // 2)
    return jnp.where(first, pltpu.roll(x, LANES - QK_ROPE // 2, 1), pltpu.roll(x, QK_ROPE // 2, 1))


def _rope(x, cos, sin_signed):
    return x * cos + _swap_halves(x) * sin_signed


def _rope_bwd(d, cos, sin_signed):
    return d * cos + _swap_halves(d * sin_signed)


def _rope_tables(pos_col, inv_freq_row, sign_row):
    t = pos_col.shape[0]
    tr = _pick(t, 512, SUBLANES)

    def body(p_ref, f_ref, s_ref, cos_ref, sin_ref):
        ang = p_ref[...].astype(F32) * f_ref[...]
        cos_ref[...] = jnp.cos(ang)
        sin_ref[...] = jnp.sin(ang) * s_ref[...]

    return pl.pallas_call(
        body, grid=(t // tr,), in_specs=[_rows(tr, 1), _full((1, LANES)), _full((1, LANES))],
        out_specs=[_rows(tr, LANES), _rows(tr, LANES)],
        out_shape=[jax.ShapeDtypeStruct((t, LANES), F32)] * 2,
        compiler_params=_params(("parallel",)), name="rope_tables")(pos_col, inv_freq_row, sign_row)


def _norm_fwd(x, g, name):
    t, d = x.shape
    tr = _row_tile(t, d, 2)

    def body(x_ref, g_ref, y_ref):
        xv = x_ref[...]
        y_ref[...] = (xv * _rms_scale(xv) * g_ref[...]).astype(BF16)

    return pl.pallas_call(
        body, grid=(t // tr,), in_specs=[_rows(tr, d), _full((1, d))], out_specs=_rows(tr, d),
        out_shape=jax.ShapeDtypeStruct((t, d), BF16), compiler_params=_params(("parallel",)), name=name)(x, g)


def _lat_fwd(z_lat, qg, kvg, cos, sin, ql, kvl):
    t = z_lat.shape[0]
    tr = _row_tile(t, z_lat.shape[1], 2)

    def body(z_ref, qg_ref, kvg_ref, cos_ref, sin_ref, qn_ref, kvn_ref, kpe_ref):
        q = z_ref[:, 0:ql]
        qn_ref[...] = (q * _rms_scale(q) * qg_ref[...]).astype(BF16)
        kv = z_ref[:, ql:ql + kvl]
        kvn_ref[...] = (kv * _rms_scale(kv) * kvg_ref[...]).astype(BF16)
        kpe_ref[...] = _rope(z_ref[:, ql + kvl:ql + kvl + LANES], cos_ref[...], sin_ref[...]).astype(BF16)

    w = z_lat.shape[1]
    return pl.pallas_call(
        body, grid=(t // tr,),
        in_specs=[_rows(tr, w), _full((1, ql)), _full((1, kvl)), _rows(tr, LANES), _rows(tr, LANES)],
        out_specs=[_rows(tr, ql), _rows(tr, kvl), _rows(tr, LANES)],
        out_shape=[jax.ShapeDtypeStruct((t, ql), BF16), jax.ShapeDtypeStruct((t, kvl), BF16),
                   jax.ShapeDtypeStruct((t, LANES), BF16)],
        compiler_params=_params(("parallel",)), name="lat_fwd")(z_lat, qg, kvg, cos, sin)


def _tril_mask():
    r = lax.broadcasted_iota(jnp.int32, (CHUNK, CHUNK), 0)
    c = lax.broadcasted_iota(jnp.int32, (CHUNK, CHUNK), 1)
    return r >= c


def _sgu_fwd(z_uv, gs, ws, b_col):
    t = z_uv.shape[0]
    sw = z_uv.shape[1] // 2
    groups = sw // SGU_GROUP
    tr = _pick(t, 256, CHUNK)

    def body(u_ref, v_ref, gs_ref, ws_ref, b_ref, o_ref):
        v = _gelu(v_ref[...])
        vn = (v * _rms_scale(v) * gs_ref[...]).astype(BF16)
        tri = _tril_mask()
        for g in range(groups):
            wg = jnp.where(tri, ws_ref[g], 0.0).astype(BF16)
            cols = slice(g * SGU_GROUP, (g + 1) * SGU_GROUP)
            for c in range(tr // CHUNK):
                rows = slice(c * CHUNK, (c + 1) * CHUNK)
                mixed = jnp.dot(wg, vn[rows, cols], preferred_element_type=F32) + b_ref[g]
                o_ref[rows, cols] = (_gelu(u_ref[rows, cols]) * mixed).astype(BF16)

    return pl.pallas_call(
        body, grid=(t // tr,),
        in_specs=[_rows(tr, sw, 0), _rows(tr, sw, 1), _full((1, sw)), _full(ws.shape), _full(b_col.shape)],
        out_specs=_rows(tr, sw), out_shape=jax.ShapeDtypeStruct((t, sw), BF16),
        compiler_params=_params(("parallel",)), name="sgu_fwd")(z_uv, z_uv, gs, ws, b_col)


def _merge_fwd(y_attn, y_sgu, z_g, b_gate, comm=None):
    t, d = y_attn.shape
    tr = _row_tile(t, d, 5)

    def body(ya_ref, ys_ref, g0_ref, g1_ref, b0_ref, b1_ref, o_ref):
        g0 = _sigmoid(g0_ref[...] + b0_ref[...])
        g1 = _sigmoid(g1_ref[...] + b1_ref[...])
        o_ref[...] = (g0 * ya_ref[...] + g1 * ys_ref[...]).astype(BF16)

    bspec0 = pl.BlockSpec((1, d), lambda i: (0, 0))
    bspec1 = pl.BlockSpec((1, d), lambda i: (0, 1))
    outs, comm_outs = _call(
        body, grid=(t // tr,),
        in_specs=[_rows(tr, d), _rows(tr, d), _rows(tr, d, 0), _rows(tr, d, 1), bspec0, bspec1],
        out_specs=[_rows(tr, d)], out_shape=[jax.ShapeDtypeStruct((t, d), BF16)],
        sem=("parallel",), name="merge_fwd", args=(y_attn, y_sgu, z_g, z_g, b_gate, b_gate), comm=comm)
    return outs[0], comm_outs


def _swiglu_fwd(gate, up, comm=None):
    t, f = gate.shape
    tr = _row_tile(t, f, 3)

    def body(g_ref, u_ref, o_ref):
        g = g_ref[...]
        o_ref[...] = (g * _sigmoid(g) * u_ref[...]).astype(BF16)

    outs, comm_outs = _call(
        body, grid=(t // tr,), in_specs=[_rows(tr, f), _rows(tr, f)], out_specs=[_rows(tr, f)],
        out_shape=[jax.ShapeDtypeStruct((t, f), BF16)], sem=("parallel",), name="swiglu_fwd", args=(gate, up), comm=comm)
    return outs[0], comm_outs


def _loss_head(h2, g, target):
    t, d = h2.shape
    tr = _row_tile(t, d, 3)

    def body(h_ref, g_ref, t_ref, loss_ref, dh_ref, dhb_ref, dg_ref):
        @pl.when(pl.program_id(0) == 0)
        def _():
            loss_ref[...] = jnp.zeros_like(loss_ref)
            dg_ref[...] = jnp.zeros_like(dg_ref)

        h = h_ref[...]
        r = _rms_scale(h)
        hhat = h * r
        gv = g_ref[...]
        err = hhat * gv - t_ref[...]
        loss_ref[...] += jnp.full(loss_ref.shape, 0.5 * jnp.sum(jnp.mean(err * err, axis=-1)), F32)
        dx, dg_rows = _rms_bwd(hhat, r, gv, err * (1.0 / d))
        dh_ref[...] = dx
        dhb_ref[...] = dx.astype(BF16)
        dg_ref[...] += jnp.sum(dg_rows, axis=0, keepdims=True)

    return pl.pallas_call(
        body, grid=(t // tr,), in_specs=[_rows(tr, d), _full((1, d)), _rows(tr, d)],
        out_specs=[_full((1, LANES)), _rows(tr, d), _rows(tr, d), _full((1, d))],
        out_shape=[jax.ShapeDtypeStruct((1, LANES), F32), jax.ShapeDtypeStruct((t, d), F32),
                   jax.ShapeDtypeStruct((t, d), BF16), jax.ShapeDtypeStruct((1, d), F32)],
        compiler_params=_params(("arbitrary",)), name="loss_head")(h2, g, target)


def _norm_bwd(x, g, dy, resid, name, comm=None):
    t, d = x.shape
    tr = _row_tile(t, d, 5)

    def body(x_ref, g_ref, dy_ref, r_ref, dx_ref, dxb_ref, dg_ref):
        @pl.when(pl.program_id(0) == 0)
        def _():
            dg_ref[...] = jnp.zeros_like(dg_ref)

        xv = x_ref[...]
        r = _rms_scale(xv)
        dx, dg_rows = _rms_bwd(xv * r, r, g_ref[...], dy_ref[...])
        dx = r_ref[...] + dx
        dx_ref[...] = dx
        dxb_ref[...] = dx.astype(BF16)
        dg_ref[...] += jnp.sum(dg_rows, axis=0, keepdims=True)

    outs, comm_outs = _call(
        body, grid=(t // tr,), in_specs=[_rows(tr, d), _full((1, d)), _rows(tr, d), _rows(tr, d)],
        out_specs=[_rows(tr, d), _rows(tr, d), _full((1, d))],
        out_shape=[jax.ShapeDtypeStruct((t, d), F32), jax.ShapeDtypeStruct((t, d), BF16),
                   jax.ShapeDtypeStruct((1, d), F32)],
        sem=("arbitrary",), name=name, args=(x, g, dy, resid), comm=comm)
    return (outs[0], outs[1], outs[2]) if comm is None else (outs[0], outs[1], outs[2], comm_outs)


def _merge_bwd(dmerged, y_attn, y_sgu, z_g, b_gate):
    t, d = y_attn.shape
    tr = _row_tile(t, d, 7)

    def body(dm_ref, ya_ref, ys_ref, g0_ref, g1_ref, b0_ref, b1_ref, dya_ref, dys_ref, dz_ref, db_ref):
        @pl.when(pl.program_id(0) == 0)
        def _():
            db_ref[...] = jnp.zeros_like(db_ref)

        dm = dm_ref[...]
        g0 = _sigmoid(g0_ref[...] + b0_ref[...])
        g1 = _sigmoid(g1_ref[...] + b1_ref[...])
        dya_ref[...] = (dm * g0).astype(BF16)
        dys_ref[...] = (dm * g1).astype(BF16)
        dl0 = dm * ya_ref[...] * (g0 * (1.0 - g0))
        dl1 = dm * ys_ref[...] * (g1 * (1.0 - g1))
        dz_ref[:, 0:d] = dl0.astype(BF16)
        dz_ref[:, d:2 * d] = dl1.astype(BF16)
        db_ref[:, 0:d] += jnp.sum(dl0, axis=0, keepdims=True)
        db_ref[:, d:2 * d] += jnp.sum(dl1, axis=0, keepdims=True)

    bspec0 = pl.BlockSpec((1, d), lambda i: (0, 0))
    bspec1 = pl.BlockSpec((1, d), lambda i: (0, 1))
    return pl.pallas_call(
        body, grid=(t // tr,),
        in_specs=[_rows(tr, d), _rows(tr, d), _rows(tr, d), _rows(tr, d, 0), _rows(tr, d, 1), bspec0, bspec1],
        out_specs=[_rows(tr, d), _rows(tr, d), _rows(tr, 2 * d), _full((1, 2 * d))],
        out_shape=[jax.ShapeDtypeStruct((t, d), BF16), jax.ShapeDtypeStruct((t, d), BF16),
                   jax.ShapeDtypeStruct((t, 2 * d), BF16), jax.ShapeDtypeStruct((1, 2 * d), F32)],
        compiler_params=_params(("arbitrary",)), name="merge_bwd")(dmerged, y_attn, y_sgu, z_g, z_g, b_gate, b_gate)


def _sgu_bwd(z_uv, ds_out, gs, ws, b_col):
    t = z_uv.shape[0]
    sw = z_uv.shape[1] // 2
    groups = sw // SGU_GROUP
    tr = _pick(t, 256, CHUNK)

    def body(u_ref, v_ref, d_ref, gs_ref, ws_ref, b_ref, dz_ref, dws_ref, db_ref, dgs_ref, dvn_ref):
        @pl.when(pl.program_id(0) == 0)
        def _():
            dws_ref[...] = jnp.zeros_like(dws_ref)
            db_ref[...] = jnp.zeros_like(db_ref)
            dgs_ref[...] = jnp.zeros_like(dgs_ref)

        v, dgelu_v = _gelu_and_grad(v_ref[...])
        r = _rms_scale(v)
        vhat = v * r
        gsv = gs_ref[...]
        vn = (vhat * gsv).astype(BF16)
        tri = _tril_mask()
        for g in range(groups):
            wg = jnp.where(tri, ws_ref[g], 0.0).astype(BF16)
            cols = slice(g * SGU_GROUP, (g + 1) * SGU_GROUP)
            for c in range(tr // CHUNK):
                rows = slice(c * CHUNK, (c + 1) * CHUNK)
                vn_cg = vn[rows, cols]
                mixed = jnp.dot(wg, vn_cg, preferred_element_type=F32) + b_ref[g]
                u, dgelu_u = _gelu_and_grad(u_ref[rows, cols])
                dso = d_ref[rows, cols]
                dz_ref[rows, cols] = (dso * mixed * dgelu_u).astype(BF16)
                dmixed = dso * u
                db_ref[g] += jnp.sum(dmixed, axis=1, keepdims=True)
                dmixed_b = dmixed.astype(BF16)
                dws_ref[g] += jnp.where(
                    tri, lax.dot_general(dmixed_b, vn_cg, (((1,), (1,)), ((), ())), preferred_element_type=F32), 0.0)
                dvn_ref[rows, cols] = lax.dot_general(wg, dmixed_b, (((0,), (0,)), ((), ())), preferred_element_type=F32)
        dvn = dvn_ref[...]
        dv, dgs_rows = _rms_bwd(vhat, r, gsv, dvn)
        dz_ref[:, sw:2 * sw] = (dv * dgelu_v).astype(BF16)
        dgs_ref[...] += jnp.sum(dgs_rows, axis=0, keepdims=True)

    return pl.pallas_call(
        body, grid=(t // tr,),
        in_specs=[_rows(tr, sw, 0), _rows(tr, sw, 1), _rows(tr, sw), _full((1, sw)), _full(ws.shape), _full(b_col.shape)],
        out_specs=[_rows(tr, 2 * sw), _full(ws.shape), _full(b_col.shape), _full((1, sw))],
        out_shape=[jax.ShapeDtypeStruct((t, 2 * sw), BF16), jax.ShapeDtypeStruct(ws.shape, F32),
                   jax.ShapeDtypeStruct(b_col.shape, F32), jax.ShapeDtypeStruct((1, sw), F32)],
        scratch_shapes=[pltpu.VMEM((tr, sw), F32)],
        compiler_params=_params(("arbitrary",)), name="sgu_bwd")(z_uv, z_uv, ds_out, gs, ws, b_col)


def _lat_bwd(z_lat, qg, kvg, dqn, dkvn, dkpe_heads, cos, sin, ql, kvl):
    t, w = z_lat.shape
    heads = dkpe_heads.shape[0]
    tr = _row_tile(t, w + heads * LANES, 3)

    def body(z_ref, qg_ref, kvg_ref, dq_ref, dkv_ref, dk_ref, cos_ref, sin_ref, dz_ref, dqg_ref, dkvg_ref):
        @pl.when(pl.program_id(0) == 0)
        def _():
            dqg_ref[...] = jnp.zeros_like(dqg_ref)
            dkvg_ref[...] = jnp.zeros_like(dkvg_ref)

        q = z_ref[:, 0:ql]
        r = _rms_scale(q)
        dx, dg_rows = _rms_bwd(q * r, r, qg_ref[...], dq_ref[...])
        dz_ref[:, 0:ql] = dx.astype(BF16)
        dqg_ref[...] += jnp.sum(dg_rows, axis=0, keepdims=True)
        kv = z_ref[:, ql:ql + kvl]
        r = _rms_scale(kv)
        dx, dg_rows = _rms_bwd(kv * r, r, kvg_ref[...], dkv_ref[...])
        dz_ref[:, ql:ql + kvl] = dx.astype(BF16)
        dkvg_ref[...] += jnp.sum(dg_rows, axis=0, keepdims=True)
        dk = dk_ref[0]
        for h in range(1, heads):
            dk = dk + dk_ref[h]
        dz_ref[:, ql + kvl:ql + kvl + LANES] = _rope_bwd(dk, cos_ref[...], sin_ref[...]).astype(BF16)

    return pl.pallas_call(
        body, grid=(t // tr,),
        in_specs=[_rows(tr, w), _full((1, ql)), _full((1, kvl)), _rows(tr, ql), _rows(tr, kvl),
                  pl.BlockSpec((heads, tr, LANES), lambda i: (0, i, 0)), _rows(tr, LANES), _rows(tr, LANES)],
        out_specs=[_rows(tr, w), _full((1, ql)), _full((1, kvl))],
        out_shape=[jax.ShapeDtypeStruct((t, w), BF16), jax.ShapeDtypeStruct((1, ql), F32),
                   jax.ShapeDtypeStruct((1, kvl), F32)],
        compiler_params=_params(("arbitrary",)), name="lat_bwd")(z_lat, qg, kvg, dqn, dkvn, dkpe_heads, cos, sin)


_NT = (((1,), (1,)), ((), ()))


def _attn_scale():
    return (QK_NOPE + QK_ROPE) ** -0.5


def _heads_per_step(heads, wanted):
    return wanted if heads % wanted == 0 else 1


def _attn_fwd(q_c, kv, kpe, comm=None):
    t = q_c.shape[0]
    heads = q_c.shape[1] // HEAD_PAD
    tq = _pick(t, ATTN_TILE)
    nq = t // tq
    scale = _attn_scale()
    to_log2 = scale * math.log2(math.e)
    tn_dims = (((0,), (0,)), ((), ()))

    hps = _heads_per_step(heads, HEADS_PER_STEP[0])

    def body(q_ref, kv_ref, kpe_ref, o_ref, ob_ref, lse_ref, m_sc, l_sc, acc_sc):
        qi, ki = pl.program_id(1), pl.program_id(2)

        @pl.when(ki == 0)
        def _():
            m_sc[...] = jnp.full_like(m_sc, NEG_BIG)
            l_sc[...] = jnp.zeros_like(l_sc)
            acc_sc[...] = jnp.zeros_like(acc_sc)

        def step(diagonal):
            for u in range(hps):
                lo = u * HEAD_PAD
                kc = jnp.concatenate([kv_ref[:, lo:lo + QK_NOPE], kpe_ref[...]], axis=1)
                st = lax.dot_general(kc, q_ref[:, lo:lo + HEAD_PAD], _NT, preferred_element_type=F32)
                if diagonal:
                    krow = lax.broadcasted_iota(jnp.int32, st.shape, 0)
                    qcol = lax.broadcasted_iota(jnp.int32, st.shape, 1)
                    st = jnp.where(qcol >= krow, st, NEG_BIG)
                m_prev = m_sc[u]
                m_new = jnp.maximum(m_prev, jnp.max(st, axis=0, keepdims=True))
                alpha = jnp.exp2((m_prev - m_new) * to_log2)
                pt = jnp.exp2((st - m_new) * to_log2)
                l_sc[u] = alpha * l_sc[u] + jnp.sum(pt, axis=0, keepdims=True)
                acc_sc[u] = alpha * acc_sc[u] + lax.dot_general(
                    kv_ref[:, lo + QK_NOPE:lo + HEAD_PAD], pt.astype(BF16), tn_dims, preferred_element_type=F32)
                m_sc[u] = m_new

        @pl.when(ki < qi)
        def _():
            step(False)

        @pl.when(ki == qi)
        def _():
            step(True)
            for u in range(hps):
                o = (acc_sc[u] / l_sc[u]).T
                o_ref[:, u * V_HEAD:(u + 1) * V_HEAD] = o
                ob_ref[:, u * V_HEAD:(u + 1) * V_HEAD] = o.astype(BF16)
                lse_ref[u] = m_sc[u] * scale + jnp.log(l_sc[u])

    omap = lambda g, qi, ki: (qi, g)
    outs, comm_outs = _call(
        body, grid=(heads // hps, nq, nq),
        in_specs=[pl.BlockSpec((tq, hps * HEAD_PAD), omap),
                  pl.BlockSpec((tq, hps * HEAD_PAD), lambda g, qi, ki: (jnp.minimum(ki, qi), g)),
                  pl.BlockSpec((tq, LANES), lambda g, qi, ki: (jnp.minimum(ki, qi), 0))],
        out_specs=[pl.BlockSpec((tq, hps * V_HEAD), omap), pl.BlockSpec((tq, hps * V_HEAD), omap),
                   pl.BlockSpec((hps, 1, tq), lambda g, qi, ki: (g, 0, qi))],
        out_shape=[jax.ShapeDtypeStruct((t, heads * V_HEAD), F32), jax.ShapeDtypeStruct((t, heads * V_HEAD), BF16),
                   jax.ShapeDtypeStruct((heads, 1, t), F32)],
        scratch_shapes=[pltpu.VMEM((hps, 1, tq), F32), pltpu.VMEM((hps, 1, tq), F32),
                        pltpu.VMEM((hps, V_HEAD, tq), F32)],
        sem=("parallel", "parallel", "arbitrary"), name="attn_fwd", args=(q_c, kv, kpe), comm=comm)
    return outs[0], outs[1], outs[2], comm_outs


def _attn_bwd(q_c, kv, kpe, o, do, lse_row, cos, sin, comm=None):
    t = q_c.shape[0]
    heads = q_c.shape[1] // HEAD_PAD
    tk = _pick(t, ATTN_TILE)
    nk = t // tk
    scale = _attn_scale()
    tn_dims = (((0,), (0,)), ((), ()))

    hps = _heads_per_step(heads, HEADS_PER_STEP[1])

    def body(q_ref, kv_ref, kpe_ref, do_ref, lse_ref, o_ref, cos_ref, sin_ref, dq_ref, dkv_ref, dkpe_ref,
             dk_sc, dv_sc, delta_sc, dq_sc):
        ki, qi = pl.program_id(1), pl.program_id(2)

        @pl.when(jnp.logical_and(ki == 0, qi == 0))
        def _():
            dq_sc[...] = jnp.zeros_like(dq_sc)

        @pl.when(qi == 0)
        def _():
            dk_sc[...] = jnp.zeros_like(dk_sc)
            dv_sc[...] = jnp.zeros_like(dv_sc)

        @pl.when(ki == 0)
        def _():
            for u in range(hps):
                cols = slice(u * V_HEAD, (u + 1) * V_HEAD)
                delta_sc[qi * hps + u] = jnp.sum((do_ref[:, cols] * o_ref[:, cols]).T, axis=0, keepdims=True)

        def step(diagonal):
            for u in range(hps):
                lo = u * HEAD_PAD
                kc = jnp.concatenate([kv_ref[:, lo:lo + QK_NOPE], kpe_ref[...]], axis=1)
                q = q_ref[:, lo:lo + HEAD_PAD]
                st = lax.dot_general(kc, q, _NT, preferred_element_type=F32) * scale
                pt = jnp.exp(st - lse_ref[u])
                if diagonal:
                    krow = lax.broadcasted_iota(jnp.int32, st.shape, 0)
                    qcol = lax.broadcasted_iota(jnp.int32, st.shape, 1)
                    pt = jnp.where(qcol >= krow, pt, 0.0)
                do_b = do_ref[:, u * V_HEAD:(u + 1) * V_HEAD].astype(BF16)
                dv_sc[u] += jnp.dot(pt.astype(BF16), do_b, preferred_element_type=F32)
                dpt = lax.dot_general(kv_ref[:, lo + QK_NOPE:lo + HEAD_PAD], do_b, _NT, preferred_element_type=F32)
                dst = (pt * (dpt - delta_sc[qi * hps + u]) * scale).astype(BF16)
                dk_sc[u] += jnp.dot(dst, q, preferred_element_type=F32)
                rows = pl.ds(pl.multiple_of(qi * tk, tk), tk)
                dq_sc[rows, lo:lo + HEAD_PAD] += lax.dot_general(dst, kc, tn_dims, preferred_element_type=F32)

        @pl.when(qi > ki)
        def _():
            step(False)

        @pl.when(qi == ki)
        def _():
            step(True)

        @pl.when(qi == nk - 1)
        def _():
            for u in range(hps):
                lo = u * HEAD_PAD
                dkv_ref[:, lo:lo + QK_NOPE] = dk_sc[u, :, 0:QK_NOPE].astype(BF16)
                dkv_ref[:, lo + QK_NOPE:lo + HEAD_PAD] = dv_sc[u].astype(BF16)
                dkpe_ref[u] = dk_sc[u, :, QK_NOPE:QK_NOPE + LANES]

        @pl.when(jnp.logical_and(ki == nk - 1, qi == nk - 1))
        def _():
            cos, sin = cos_ref[...], sin_ref[...]
            for u in range(hps):
                lo = u * HEAD_PAD
                dq_ref[:, lo:lo + QK_NOPE] = dq_sc[:, lo:lo + QK_NOPE].astype(BF16)
                dq_ref[:, lo + QK_NOPE:lo + HEAD_PAD] = _rope_bwd(dq_sc[:, lo + QK_NOPE:lo + HEAD_PAD], cos, sin).astype(BF16)

    qclamp = lambda g, ki, qi: (jnp.maximum(qi, ki), g)
    outs, comm_outs = _call(
        body, grid=(heads // hps, nk, nk),
        in_specs=[pl.BlockSpec((tk, hps * HEAD_PAD), qclamp),
                  pl.BlockSpec((tk, hps * HEAD_PAD), lambda g, ki, qi: (ki, g)),
                  pl.BlockSpec((tk, LANES), lambda g, ki, qi: (ki, 0)),
                  pl.BlockSpec((tk, hps * V_HEAD), qclamp),
                  pl.BlockSpec((hps, 1, tk), lambda g, ki, qi: (g, 0, jnp.maximum(qi, ki))),
                  pl.BlockSpec((tk, hps * V_HEAD), lambda g, ki, qi: (jnp.where(ki == 0, qi, 0), g)),
                  _full((t, LANES)), _full((t, LANES))],
        out_specs=[pl.BlockSpec((t, hps * HEAD_PAD), lambda g, ki, qi: (0, g)),
                   pl.BlockSpec((tk, hps * HEAD_PAD), lambda g, ki, qi: (ki, g)),
                   pl.BlockSpec((hps, tk, LANES), lambda g, ki, qi: (g, ki, 0))],
        out_shape=[jax.ShapeDtypeStruct((t, heads * HEAD_PAD), BF16),
                   jax.ShapeDtypeStruct((t, heads * HEAD_PAD), BF16), jax.ShapeDtypeStruct((heads, t, LANES), F32)],
        scratch_shapes=[pltpu.VMEM((hps, tk, HEAD_PAD), F32), pltpu.VMEM((hps, tk, V_HEAD), F32),
                        pltpu.VMEM((nk * hps, 1, tk), F32), pltpu.VMEM((t, hps * HEAD_PAD), F32)],
        sem=("parallel", "arbitrary", "arbitrary"), name="attn_bwd",
        args=(q_c, kv, kpe, do, lse_row, o, cos, sin), comm=comm)
    return outs[0], outs[1], outs[2], comm_outs


def _local_step(x, pos_col, target, small, shards, opt):
    t = x.shape[0]
    ql, kvl = small["q_norm_g"].shape[1], small["kv_norm_g"].shape[1]
    sw = small["sgu_norm_g"].shape[1]
    heads = (shards["w_uq"].shape[1] * N_DEV) // (QK_NOPE + QK_ROPE)
    big = {}
    big.update(_compute_layout({"w_in": _all_gather([shards["w_in"]])[0]}, ql, kvl, heads, sw))
    half = QK_ROPE // 2
    lane = jnp.arange(LANES)
    inv_freq = ROPE_THETA ** (-jnp.arange(0, QK_ROPE, 2, dtype=F32) / QK_ROPE)
    inv_row = inv_freq[lane % half][None, :]
    sign_row = jnp.where((lane % QK_ROPE) < half, -1.0, 1.0).astype(F32)[None, :]
    cos, sin = _rope_tables(pos_col, inv_row, sign_row)
    ws = small["w_sgu"]
    b_col = small["b_sgu_col"]

    def arrived(names, bufs):
        big.update(_compute_layout(dict(zip(names, bufs)), ql, kvl, heads, sw))

    a = _norm_fwd(x, small["norm_mix_g"], "norm_mix_fwd")
    z_lat, g_qk = _mm(a, big["w_lat_t"], tb=True, name="z_lat",
                      comm=_gather_stage(1, [shards["w_uq"], shards["w_ukv"]]))
    z_uv, (g_sgu, *g_qk) = _mm(a, big["w_uv_t"], tb=True, name="z_uv",
                               comm=_join(_gather_stage(1, [shards["w_o_sgu"]]), _gather_stage(2, g_qk)))
    z_g, (g_attn, g_sgu, *g_qk) = _mm(
        a, big["w_g_t"], tb=True, name="z_g",
        comm=_join(_gather_stage(1, [shards["w_o_attn"]]), _gather_stage(2, [g_sgu]), _gather_stage(3, g_qk)))
    arrived(["w_uq", "w_ukv"], g_qk)
    qn, kvn, kpe = _lat_fwd(z_lat, small["q_norm_g"], small["kv_norm_g"], cos, sin, ql, kvl)
    q_c, (g_attn, g_sgu) = _mm(qn, big["w_uq"], name="q_up_rope", rope=(cos, sin),
                               comm=_join(_gather_stage(2, [g_attn]), _gather_stage(3, [g_sgu])))
    kv, (g_attn, g_out) = _mm(kvn, big["w_ukv"], out_dtype=BF16, name="kv_up",
                              comm=_join(_gather_stage(3, [g_attn]), _gather_stage(1, [shards["w_out"]])))
    arrived(["w_o_sgu", "w_o_attn"], [g_sgu, g_attn])
    attn, attn_b, lse, (w_gate, w_up) = _attn_fwd(
        q_c, kv, kpe, comm=_gather_stage(1, [shards["w_gate_ffn"], shards["w_up_ffn"]]))
    s_out = _sgu_fwd(z_uv, small["sgu_norm_g"], ws, b_col)
    y_sgu, (g_out,) = _mm(s_out, big["w_o_sgu"], name="y_sgu", comm=_gather_stage(2, [g_out]))
    y_attn, (w_gate, g_out) = _mm(attn_b, big["w_o_attn"], name="y_attn",
                                  comm=_join(_gather_stage(2, [w_gate]), _gather_stage(3, [g_out])))
    arrived(["w_out"], [g_out])
    merged, (w_up, w_gate) = _merge_fwd(y_attn, y_sgu, z_g, small["b_gate"],
                                        comm=_join(_gather_stage(2, [w_up]), _gather_stage(3, [w_gate])))
    h1, (w_up,) = _mm(merged, big["w_out"], add=x, name="h1", comm=_gather_stage(3, [w_up]))
    f = _norm_fwd(h1, small["norm_ffn_g"], "norm_ffn_fwd")
    gate, w_down = _mm(f, w_gate, tb=True, slab="n", name="ffn_gate", comm=_gather_stage(1, [shards["w_down_ffn"]]))
    up, w_down = _mm(f, w_up, tb=True, slab="n", name="ffn_up", comm=_gather_stage(2, w_down))
    ffn = gate.shape[2]
    gate, up = gate.reshape(N_DEV * t, ffn), up.reshape(N_DEV * t, ffn)
    act, (w_down,) = _swiglu_fwd(gate, up, comm=_gather_stage(3, w_down))
    act = act.reshape(N_DEV, t, ffn)
    h2 = _mm(act, w_down, slab="k", add=h1, name="h2")
    loss_row, dh2, dh2_b, d_norm_final = _loss_head(h2, small["norm_final_g"], target)

    def pair_sums(names, slabs, bufs):
        return [_pair_sum(g, b, "pair_sum_" + k) for k, g, b in zip(names, slabs, bufs)]

    parts, updates = {}, {}

    def update(names, label, comm=None):
        res, got = _adamw_shards([parts[k] for k in names], [opt[k] for k in names], "adamw_" + label, comm=comm)
        updates.update(zip(names, res))
        return got

    down_slabs = [_mm(act, dh2_b, ta=True, slab="m", out_dtype=BF16, name="dw_down")]
    dgu, bufs = _mm(dh2_b, w_down, tb=True, slab="n", tm=MM_TILE[0] // 2, name="dact_swiglu_bwd",
                    comm=_to_sibling(down_slabs), swiglu=(gate.reshape(N_DEV, t, ffn), up.reshape(N_DEV, t, ffn)))
    dgu = dgu.reshape(2 * N_DEV, t, ffn)
    down_pair = pair_sums(["w_down_ffn"], down_slabs, bufs)
    dw_gu, got = _mm(dgu, f, ta=True, slab="m", out_dtype=BF16, name="dw_gate_up", comm=_to_chips(down_pair))
    parts["w_down_ffn"] = got[0]
    gu_names = ["w_gate_ffn", "w_up_ffn"]
    df, bufs = _mm(dgu, w_gate, slab="k", name="df_gate", comm=_to_sibling([dw_gu, dw_gu], first=[0, N_DEV]))
    gu_pairs = [_pair_sum(dw_gu, b, "pair_sum_" + k, first=s0) for k, b, s0 in zip(gu_names, bufs, [0, N_DEV])]
    half = _pick(gu_pairs[1].shape[1], gu_pairs[1].shape[1] // 2, 2 * SUBLANES)
    df, up_parts = _mm(dgu, w_up, slab="k", a_slab0=N_DEV, add=df, name="df_up",
                       comm=_to_chips(gu_pairs[1:], rows=[("r", 0, half)]))
    quarter = _pick(half, half // 2, 2 * SUBLANES)
    dh1, dh1_b, d_norm_ffn, gate_parts = _norm_bwd(h1, small["norm_ffn_g"], df, dh2, "norm_ffn_bwd",
                                                  comm=_to_chips(gu_pairs[:1], rows=[("r", 0, quarter)]))
    dw_out = _mm(merged, dh1_b, ta=True, out_dtype=BF16, name="dw_out")
    out_slabs = [_slabs_from_rows(dw_out)]
    dmerged, bufs = _mm(dh1_b, big["w_out"], tb=True, name="dmerged", comm=_to_sibling(out_slabs))
    out_pair = pair_sums(["w_out"], out_slabs, bufs)
    dy_attn, dy_sgu, dz_g, d_b_gate = _merge_bwd(dmerged, y_attn, y_sgu, z_g, small["b_gate"])
    dw_o_sgu = _mm(s_out, dy_sgu, ta=True, out_dtype=BF16, name="dw_o_sgu")
    ds_out = _mm(dy_sgu, big["w_o_sgu"], tb=True, name="ds_out")
    dz_uv, d_ws, d_b_col, d_sgu_norm = _sgu_bwd(z_uv, ds_out, small["sgu_norm_g"], ws, b_col)
    dw_o_attn = _mm(attn_b, dy_attn, ta=True, out_dtype=BF16, name="dw_o_attn")
    mix_names = ["w_o_sgu", "w_o_attn"]
    mix_slabs = [_slabs_from_cols(dw_o_sgu), _slabs_from_rows(dw_o_attn)]
    dattn, bufs = _mm(dy_attn, big["w_o_attn"], tb=True, name="dattn", comm=_to_sibling(mix_slabs))
    mix_pairs = pair_sums(mix_names, mix_slabs, bufs)
    rows = gu_pairs[1].shape[1]
    dq_p, dkv, dkpe_heads, got = _attn_bwd(
        q_c, kv, kpe, attn, dattn, lse, cos, sin,
        comm=_join(_to_chips(gu_pairs[:1], rows=[("r", quarter, rows - quarter)], into=gate_parts),
                   _to_chips(gu_pairs[1:], rows=[("r", half, rows - half)], into=up_parts)))
    parts.update(zip(gu_names, got))
    dw_uq = _mm(qn, dq_p, ta=True, out_dtype=BF16, name="dw_uq")
    dw_ukv = _mm(kvn, dkv, ta=True, out_dtype=BF16, name="dw_ukv")
    dqn = _mm(dq_p, big["w_uq"], tb=True, name="dqn")
    dkvn = _mm(dkv, big["w_ukv"], tb=True, name="dkvn")
    dz_lat, d_q_norm, d_kv_norm = _lat_bwd(z_lat, small["q_norm_g"], small["kv_norm_g"], dqn, dkvn, dkpe_heads,
                                           cos, sin, ql, kvl)
    dw_g, got = _mm(dz_g, a, ta=True, out_dtype=BF16, name="dw_g", comm=_to_chips(out_pair))
    parts["w_out"] = got[0]
    dw_uv, got = _mm(dz_uv, a, ta=True, out_dtype=BF16, name="dw_uv", comm=_to_chips(mix_pairs[1:]))
    parts["w_o_attn"] = got[0]
    dw_lat, got = _mm(dz_lat, a, ta=True, out_dtype=BF16, name="dw_lat", comm=_to_chips(mix_pairs[:1]))
    parts["w_o_sgu"] = got[0]
    lat = ql + kvl + QK_ROPE
    dw_uq_cols = dw_uq.reshape(ql, heads, HEAD_PAD)[:, :, :QK_NOPE + QK_ROPE].reshape(ql, heads * (QK_NOPE + QK_ROPE))
    in_names = ["w_uq", "w_ukv", "w_in"]
    in_slabs = [_slabs_from_cols(dw_uq_cols), _slabs_from_cols(dw_ukv),
                _slabs_from_rows(jnp.concatenate([dw_lat[:lat], dw_uv, dw_g], axis=0))]
    da = _mm(dz_lat, big["w_lat_t"], name="da_lat")
    da, bufs = _mm(dz_uv, big["w_uv_t"], add=da, name="da_uv", comm=_to_sibling(in_slabs))
    uq_pair, ukv_pair, in_pair = pair_sums(in_names, in_slabs, bufs)
    cols = in_pair.shape[2]
    first = ((cols * TAIL_SPLIT[0]) // TAIL_SPLIT[1]) // LANES * LANES or cols
    da, in_parts = _mm(dz_g, big["w_g_t"], add=da, name="da_g", comm=_to_chips([in_pair], rows=[("c", 0, first)]))
    grad_x, _, d_norm_mix, got = _norm_bwd(x, small["norm_mix_g"], da, dh1, "norm_mix_bwd",
                                          comm=_to_chips([uq_pair, ukv_pair]))
    parts["w_uq"], parts["w_ukv"] = got
    rest = _to_chips([in_pair], rows=[("c", first, cols - first)], into=in_parts) if first < cols else None
    got = update(["w_gate_ffn", "w_up_ffn", "w_down_ffn"], "ffn", comm=rest)
    parts["w_in"] = got[0] if rest is not None else in_parts[0]
    update(["w_out", "w_o_attn"], "mixer_out")
    for k in ("w_o_sgu", "w_uq", "w_ukv", "w_in"):
        update([k], k)

    gs = {"norm_mix_g": d_norm_mix, "b_gate": d_b_gate, "q_norm_g": d_q_norm, "kv_norm_g": d_kv_norm,
          "sgu_norm_g": d_sgu_norm, "w_sgu": d_ws, "b_sgu_col": d_b_col, "norm_ffn_g": d_norm_ffn,
          "norm_final_g": d_norm_final}
    return loss_row, grad_x, gs, updates


def _my_place():
    return lax.axis_index("x"), lax.axis_index("y"), lax.axis_index("c")


N_CHIPS = N_DEV // 2

_GATHER_SEMS = [[(3,), (3,), ()], [(4,), (4,)], [(1,), (1,)]]


def _halves(shape):
    r, c = shape
    if (c // 2) % LANES == 0:
        return ("c", 0, c // 2), ("c", c // 2, c // 2)
    assert (r // 2) % (2 * SUBLANES) == 0, shape
    return ("r", 0, r // 2), ("r", r // 2, r // 2)


def _gather_copies(stage, ins, outs, sems):
    x, y, c = _my_place()
    me, x_nbr, y_nbr, diag = 4 * x + 2 * y + c, 4 * (1 - x) + 2 * y + c, 4 * x + 2 * (1 - y) + c, 4 * (1 - x) + 2 * (1 - y) + c
    sibling = (x, y, 1 - c)

    def remote(w, k, src, dst, to):
        return pltpu.make_async_remote_copy(src_ref=src, dst_ref=dst, send_sem=sems[0].at[w, k], recv_sem=sems[1].at[w, k],
                                            device_id=to, device_id_type=MESH)

    out = []
    for w in range(len(outs)):
        if stage == 1:
            dst = outs[w].at[me]
            out.append(pltpu.make_async_copy(ins[w], dst, sems[2].at[w]))
            out += [remote(w, k, ins[w], dst, to) for k, to in enumerate([sibling, (1 - x, y, c), (x, 1 - y, c)])]
        elif stage == 2:
            first, second = _halves(outs[w].shape[1:])
            out.append(remote(w, 0, _window(ins[w], x_nbr, first), _window(outs[w], x_nbr, first), (x, 1 - y, c)))
            out.append(remote(w, 1, _window(ins[w], y_nbr, second), _window(outs[w], y_nbr, second), (1 - x, y, c)))
            out.append(remote(w, 2, ins[w].at[x_nbr], outs[w].at[x_nbr], sibling))
            out.append(remote(w, 3, ins[w].at[y_nbr], outs[w].at[y_nbr], sibling))
        else:
            out.append(remote(w, 0, ins[w].at[diag], outs[w].at[diag], sibling))
    return out


def _gather_stage(stage, arrays):
    n = len(arrays)

    def start(ins, outs, sems):
        for cp in _gather_copies(stage, ins, outs, sems):
            cp.start()

    def finish(ins, outs, sems):
        for cp in _gather_copies(stage, ins, outs, sems):
            cp.wait()

    shapes = [jax.ShapeDtypeStruct(((N_DEV,) + a.shape) if stage == 1 else a.shape, a.dtype) for a in arrays]
    return _Comm(arrays, shapes, [pltpu.SemaphoreType.DMA((n,) + s) for s in _GATHER_SEMS[stage - 1]], start, finish,
                 aliases=None if stage == 1 else {w: w for w in range(n)})


def _join(*comms):
    ins, shapes, sems, aliases, spans = [], [], [], {}, []
    for cm in comms:
        spans.append((len(ins), len(ins) + len(cm.ins), len(shapes), len(shapes) + len(cm.out_shapes),
                      len(sems), len(sems) + len(cm.sems)))
        aliases.update({len(ins) + i: len(shapes) + o for i, o in cm.aliases.items()})
        ins, shapes, sems = ins + cm.ins, shapes + cm.out_shapes, sems + cm.sems

    def each(half):
        def run(i_refs, o_refs, s_refs):
            for cm, (i0, i1, o0, o1, s0, s1) in zip(comms, spans):
                getattr(cm, half)(i_refs[i0:i1], o_refs[o0:o1], s_refs[s0:s1])
        return run

    return _Comm(ins, shapes, sems, each("start"), each("finish"), aliases)


def _all_gather(shards):
    n = len(shards)
    n_sems = [len(s) for s in _GATHER_SEMS]

    def body(*refs):
        ins, outs, sems = refs[:n], refs[n:2 * n], refs[2 * n:]
        s0 = 0
        for stage in (1, 2, 3):
            mine = sems[s0:s0 + n_sems[stage - 1]]
            s0 += n_sems[stage - 1]
            copies = _gather_copies(stage, ins if stage == 1 else outs, outs, mine)
            for cp in copies:
                cp.start()
            for cp in copies:
                cp.wait()

    any_spec = pl.BlockSpec(memory_space=pl.ANY)
    return pl.pallas_call(
        body, in_specs=[any_spec] * n, out_specs=[any_spec] * n,
        out_shape=[jax.ShapeDtypeStruct((N_DEV,) + s.shape, s.dtype) for s in shards],
        scratch_shapes=[pltpu.SemaphoreType.DMA((n,) + s) for stage in _GATHER_SEMS for s in stage],
        compiler_params=pltpu.CompilerParams(has_side_effects=True), name="all_gather_weights")(*shards)


def _to_sibling(grads, first=None):
    n = len(grads)
    first = first or [0] * n

    def copies(ins, outs, sems):
        x, y, c = _my_place()
        send_sems, recv_sems = sems
        return [pltpu.make_async_remote_copy(
            src_ref=ins[w].at[first[w] + 2 * i + (1 - c)], dst_ref=outs[w].at[i], send_sem=send_sems.at[w, i],
            recv_sem=recv_sems.at[w, i], device_id=(x, y, 1 - c), device_id_type=MESH)
            for w in range(n) for i in range(N_CHIPS)]

    def start(ins, outs, sems):
        for cp in copies(ins, outs, sems):
            cp.start()

    def finish(ins, outs, sems):
        for cp in copies(ins, outs, sems):
            cp.wait()

    return _Comm(grads, [jax.ShapeDtypeStruct((N_CHIPS,) + g.shape[1:], g.dtype) for g in grads],
                 [pltpu.SemaphoreType.DMA((n, N_CHIPS)), pltpu.SemaphoreType.DMA((n, N_CHIPS))], start, finish)


def _window(ref, slab, win):
    if win is None:
        return ref.at[slab]
    if win[0] == "r":
        return ref.at[slab, pl.ds(win[1], win[2])]
    return ref.at[slab, slice(None), pl.ds(win[1], win[2])]


def _to_chips(parts, rows=None, into=None):
    n = len(parts)
    rows = rows or [None] * n

    def copies(ins, outs, sems):
        x, y, c = _my_place()
        send_sems, recv_sems, local_sems = sems
        mine = 2 * x + y
        chips = [(1 - x, y), (x, 1 - y), (1 - x, 1 - y)]
        remote = [pltpu.make_async_remote_copy(
            src_ref=_window(ins[w], 2 * cx + cy, rows[w]), dst_ref=_window(outs[w], mine, rows[w]),
            send_sem=send_sems.at[w, j], recv_sem=recv_sems.at[w, j], device_id=(cx, cy, c), device_id_type=MESH)
            for w in range(n) for j, (cx, cy) in enumerate(chips)]
        local = [pltpu.make_async_copy(_window(ins[w], mine, rows[w]), _window(outs[w], mine, rows[w]),
                                       local_sems.at[w]) for w in range(n)]
        return remote + local

    def start(ins, outs, sems):
        for cp in copies(ins, outs, sems):
            cp.start()

    def finish(ins, outs, sems):
        for cp in copies(ins, outs, sems):
            cp.wait()

    return _Comm(list(parts) + list(into or []), [jax.ShapeDtypeStruct(p.shape, p.dtype) for p in parts],
                 [pltpu.SemaphoreType.DMA((n, N_CHIPS - 1)), pltpu.SemaphoreType.DMA((n, N_CHIPS - 1)),
                  pltpu.SemaphoreType.DMA((n,))], start, finish,
                 aliases={n + w: w for w in range(n)} if into else None)


def _pair_sum(g, buf, name, first=0):
    _, r, c = g.shape
    tr, tc = _shard_tile(r, c, 4 * SHARD_TILE_ELEMS, 1024)
    core = (lax.axis_index("c") + first).astype(jnp.int32).reshape(1)

    def body(core_ref, g_ref, b_ref, o_ref):
        o_ref[...] = (g_ref[...].astype(F32) + b_ref[...].astype(F32)).astype(o_ref.dtype)

    blk = (1, tr, tc)
    return pl.pallas_call(
        body, grid_spec=pltpu.PrefetchScalarGridSpec(
            num_scalar_prefetch=1, grid=(N_CHIPS, r // tr, c // tc),
            in_specs=[pl.BlockSpec(blk, lambda i, j, l, core_ref: (2 * i + core_ref[0], j, l)),
                      pl.BlockSpec(blk, lambda i, j, l, core_ref: (i, j, l))],
            out_specs=pl.BlockSpec(blk, lambda i, j, l, core_ref: (i, j, l))),
        out_shape=jax.ShapeDtypeStruct(buf.shape, buf.dtype),
        compiler_params=_params(("parallel", "parallel", "parallel")), name=name)(core, g, buf)


def _all_reduce_pack(pack):
    r = pack.shape[0]

    def body(x_ref, out_ref, gath_ref, send_sems, recv_sems, local_sem):
        x, y, c = _my_place()
        me, sibling = (x, y, c), (x, y, 1 - c)
        chips = [(1 - x, y), (x, 1 - y), (1 - x, 1 - y)]

        def slab(place):
            return gath_ref.at[4 * place[0] + 2 * place[1] + place[2]]

        def copy(k, place, to, src=None):
            return pltpu.make_async_remote_copy(
                src_ref=slab(place) if src is None else src, dst_ref=slab(place),
                send_sem=send_sems.at[k], recv_sem=recv_sems.at[k], device_id=to, device_id_type=MESH)

        mine = pltpu.make_async_copy(x_ref, slab(me), local_sem)
        mine.start()
        first = [copy(0, me, sibling, src=x_ref)]
        first += [copy(1 + j, me, (*chip, c), src=x_ref) for j, chip in enumerate(chips)]
        for cp in first:
            cp.start()
        passed = [copy(4 + j, (*chip, c), sibling) for j, chip in enumerate(chips)]
        for j, chip in enumerate(chips):
            copy(1 + j, (*chip, c), me).wait_recv()
            passed[j].start()
        copy(0, sibling, me).wait_recv()
        for j, chip in enumerate(chips):
            copy(4 + j, (*chip, 1 - c), me).wait_recv()
        for cp in first + passed:
            cp.wait_send()
        mine.wait()
        acc = gath_ref[0]
        for i in range(1, N_DEV):
            acc = acc + gath_ref[i]
        out_ref[...] = acc

    vmem = pl.BlockSpec(memory_space=pltpu.VMEM)
    return pl.pallas_call(
        body, in_specs=[vmem], out_specs=vmem, out_shape=jax.ShapeDtypeStruct(pack.shape, F32),
        scratch_shapes=[pltpu.VMEM((N_DEV, r, LANES), F32), pltpu.SemaphoreType.DMA((7,)),
                        pltpu.SemaphoreType.DMA((7,)), pltpu.SemaphoreType.DMA],
        compiler_params=pltpu.CompilerParams(vmem_limit_bytes=VMEM_LIMIT), name="all_reduce_small")(pack)


def _adamw_math(w, g, m, v):
    m = ADAM_B1 * m + (1.0 - ADAM_B1) * g
    v = ADAM_B2 * v + (1.0 - ADAM_B2) * (g * g)
    m_hat = m / (1.0 - ADAM_B1 ** ADAM_STEP)
    v_hat = v / (1.0 - ADAM_B2 ** ADAM_STEP)
    delta = -ADAM_LR * (m_hat / (jnp.sqrt(v_hat) + ADAM_EPS) + ADAM_WD * w)
    return delta, m, v


def _adamw_shards(parts, opts, name, comm=None):
    r, c = opts[0][0].shape
    n_parts, k = parts[0].shape[0], len(parts)
    tr, tc = _shard_tile(r, c, SHARD_TILE_ELEMS // k)

    def body(*refs):
        ins, outs = refs[:4 * k], refs[4 * k:]
        for s in range(k):
            p_ref, w_ref, m_ref, v_ref = ins[4 * s:4 * s + 4]
            g_ref, d_ref, nm_ref, nv_ref = outs[4 * s:4 * s + 4]
            g = p_ref[0].astype(F32)
            for i in range(1, n_parts):
                g = g + p_ref[i].astype(F32)
            g_ref[...] = g
            d_ref[...], nm_ref[...], nv_ref[...] = _adamw_math(w_ref[...], g, m_ref[...], v_ref[...])

    spec = pl.BlockSpec((tr, tc), lambda i, j: (i, j))
    args = [a for p, o in zip(parts, opts) for a in (p,) + tuple(o)]
    outs, comm_outs = _call(
        body, grid=(r // tr, c // tc),
        in_specs=[pl.BlockSpec((n_parts, tr, tc), lambda i, j: (0, i, j)), spec, spec, spec] * k,
        out_specs=[spec] * (4 * k), out_shape=[jax.ShapeDtypeStruct((r, c), F32)] * (4 * k),
        sem=("parallel", "parallel"), name=name, args=args, comm=comm)
    return [outs[4 * s:4 * s + 4] for s in range(k)], comm_outs


def _adamw_pack(g, w, m, v):
    r, c = w.shape

    def body(g_ref, w_ref, m_ref, v_ref, d_ref, nm_ref, nv_ref):
        d_ref[...], nm_ref[...], nv_ref[...] = _adamw_math(w_ref[...], g_ref[...], m_ref[...], v_ref[...])

    return pl.pallas_call(
        body, in_specs=[_full((r, c))] * 4, out_specs=[_full((r, c))] * 3, grid=(1,),
        out_shape=[jax.ShapeDtypeStruct((r, c), F32)] * 3,
        compiler_params=_params(("arbitrary",)), name="adamw_small")(g, w, m, v)


def _cols_from_slabs(g):
    return jnp.transpose(g, (1, 0, 2)).reshape(g.shape[1], N_DEV * g.shape[2])


def _slabs_from_cols(w):
    r, c8 = w.shape
    return jnp.transpose(w.reshape(r, N_DEV, c8 // N_DEV), (1, 0, 2))


def _rows_from_slabs(g):
    return g.reshape(N_DEV * g.shape[1], g.shape[2])


def _slabs_from_rows(w):
    return w.reshape(N_DEV, w.shape[0] // N_DEV, w.shape[1])


def _compute_layout(gathered, ql, kvl, heads, sw):
    out = {}
    for k, g in gathered.items():
        if k == "w_in":
            lat = ql + kvl + QK_ROPE
            w_in_t = _rows_from_slabs(g)
            out["w_lat_t"] = jnp.pad(w_in_t[:lat], ((0, LANES - QK_ROPE), (0, 0)))
            out["w_uv_t"] = w_in_t[lat:lat + 2 * sw]
            out["w_g_t"] = w_in_t[lat + 2 * sw:]
        elif k == "w_uq":
            per_head = _cols_from_slabs(g).reshape(ql, heads, QK_NOPE + QK_ROPE)
            pad = HEAD_PAD - QK_NOPE - QK_ROPE
            out["w_uq"] = jnp.pad(per_head, ((0, 0), (0, 0), (0, pad))).reshape(ql, heads * HEAD_PAD)
        elif k in ("w_o_attn", "w_out", "w_down_ffn"):
            out[k.removesuffix("_ffn")] = _rows_from_slabs(g)
        else:
            out[k.removesuffix("_ffn")] = _cols_from_slabs(g)
    return out


_SMALL =["norm_mix_g", "b_gate", "q_norm_g", "kv_norm_g", "sgu_norm_g", "w_sgu", "b_sgu", "norm_ffn_g", "norm_final_g"]
_BIG = ["w_in", "w_uq", "w_ukv", "w_o_attn", "w_o_sgu", "w_out", "w_gate_ffn", "w_up_ffn", "w_down_ffn"]
_TRANSPOSED = ("w_in", "w_gate_ffn", "w_up_ffn")
_ORDER = ["norm_mix_g", "w_in", "b_gate", "q_norm_g", "w_uq", "kv_norm_g", "w_ukv", "w_o_attn", "sgu_norm_g", "w_sgu",
          "b_sgu", "w_o_sgu", "w_out", "norm_ffn_g", "w_gate_ffn", "w_up_ffn", "w_down_ffn", "norm_final_g"]


def _pack_rows(parts):
    rows, sizes = [], []
    for p in parts:
        flat = p.reshape(-1)
        n = flat.shape[0]
        padded = -(-n // (SUBLANES * LANES)) * (SUBLANES * LANES)
        rows.append(jnp.pad(flat, (0, padded - n)).reshape(padded // LANES, LANES))
        sizes.append((n, padded // LANES))
    return jnp.concatenate(rows, axis=0), sizes


def _unpack_rows(pack, sizes, shapes):
    out, r0 = [], 0
    for (n, nr), shp in zip(sizes, shapes):
        out.append(pack[r0:r0 + nr].reshape(-1)[:n].reshape(shp))
        r0 += nr
    return out


def kernel(x, positions, norm_mix_g, w_in, b_gate, q_norm_g, w_uq, kv_norm_g, w_ukv, w_o_attn, sgu_norm_g, w_sgu, b_sgu, w_o_sgu, w_out, norm_ffn_g, w_gate_ffn, w_up_ffn, w_down_ffn, norm_final_g, loss_target, m_norm_mix_g, m_w_in, m_b_gate, m_q_norm_g, m_w_uq, m_kv_norm_g, m_w_ukv, m_w_o_attn, m_sgu_norm_g, m_w_sgu, m_b_sgu, m_w_o_sgu, m_w_out, m_norm_ffn_g, m_w_gate_ffn, m_w_up_ffn, m_w_down_ffn, m_norm_final_g, v_norm_mix_g, v_w_in, v_b_gate, v_q_norm_g, v_w_uq, v_kv_norm_g, v_w_ukv, v_w_o_attn, v_sgu_norm_g, v_w_sgu, v_b_sgu, v_w_o_sgu, v_w_out, v_norm_ffn_g, v_w_gate_ffn, v_w_up_ffn, v_w_down_ffn, v_norm_final_g):
    wts = dict(norm_mix_g=norm_mix_g, w_in=w_in, b_gate=b_gate, q_norm_g=q_norm_g, w_uq=w_uq, kv_norm_g=kv_norm_g,
               w_ukv=w_ukv, w_o_attn=w_o_attn, sgu_norm_g=sgu_norm_g, w_sgu=w_sgu, b_sgu=b_sgu, w_o_sgu=w_o_sgu,
               w_out=w_out, norm_ffn_g=norm_ffn_g, w_gate_ffn=w_gate_ffn, w_up_ffn=w_up_ffn, w_down_ffn=w_down_ffn,
               norm_final_g=norm_final_g)
    mom = dict(norm_mix_g=m_norm_mix_g, w_in=m_w_in, b_gate=m_b_gate, q_norm_g=m_q_norm_g, w_uq=m_w_uq,
               kv_norm_g=m_kv_norm_g, w_ukv=m_w_ukv, w_o_attn=m_w_o_attn, sgu_norm_g=m_sgu_norm_g, w_sgu=m_w_sgu,
               b_sgu=m_b_sgu, w_o_sgu=m_w_o_sgu, w_out=m_w_out, norm_ffn_g=m_norm_ffn_g, w_gate_ffn=m_w_gate_ffn,
               w_up_ffn=m_w_up_ffn, w_down_ffn=m_w_down_ffn, norm_final_g=m_norm_final_g)
    var = dict(norm_mix_g=v_norm_mix_g, w_in=v_w_in, b_gate=v_b_gate, q_norm_g=v_q_norm_g, w_uq=v_w_uq,
               kv_norm_g=v_kv_norm_g, w_ukv=v_w_ukv, w_o_attn=v_w_o_attn, sgu_norm_g=v_sgu_norm_g, w_sgu=v_w_sgu,
               b_sgu=v_b_sgu, w_o_sgu=v_w_o_sgu, w_out=v_w_out, norm_ffn_g=v_norm_ffn_g, w_gate_ffn=v_w_gate_ffn,
               w_up_ffn=v_w_up_ffn, w_down_ffn=v_w_down_ffn, norm_final_g=v_norm_final_g)

    t, d = x.shape[1], x.shape[2]
    ql, kvl = q_norm_g.shape[1], kv_norm_g.shape[1]
    heads = (w_uq.shape[2] * N_DEV) // (QK_NOPE + QK_ROPE)
    sw = sgu_norm_g.shape[1]

    def shard(a, k):
        return a[0].T if k in _TRANSPOSED else a[0]

    def unshard(a, k):
        return (a.T if k in _TRANSPOSED else a).reshape(wts[k].shape)

    opt = {k: (shard(wts[k], k), shard(mom[k], k), shard(var[k], k)) for k in _BIG}
    shards = {k: opt[k][0].astype(BF16) for k in _BIG}
    small = {
        "norm_mix_g": norm_mix_g, "b_gate": b_gate, "q_norm_g": q_norm_g, "kv_norm_g": kv_norm_g,
        "sgu_norm_g": sgu_norm_g, "w_sgu": w_sgu[0], "b_sgu_col": b_sgu[0][:, :, None], "norm_ffn_g": norm_ffn_g,
        "norm_final_g": norm_final_g[None, :],
    }

    loss_row, grad_x, gs, updates = _local_step(x[0], positions.reshape(t, 1), loss_target[0], small, shards, opt)
    grads, deltas, new_m, new_v = {}, {}, {}, {}
    for k in _BIG:
        grads[k], deltas[k], new_m[k], new_v[k] = (unshard(a, k) for a in updates[k])

    small_grads = [gs["norm_mix_g"], gs["b_gate"], gs["q_norm_g"], gs["kv_norm_g"], gs["sgu_norm_g"], gs["w_sgu"],
                   gs["b_sgu_col"], gs["norm_ffn_g"], gs["norm_final_g"]]
    pack, sizes = _pack_rows([loss_row] + small_grads)
    total = _all_reduce_pack(pack)
    shapes = [(1, LANES)] + [wts[k].shape for k in _SMALL]
    unpacked = _unpack_rows(total, sizes, shapes)
    loss = unpacked[0][0, 0]
    for k, g in zip(_SMALL, unpacked[1:]):
        grads[k] = g
    g_pack = total[sizes[0][1]:]
    w_pack, _ = _pack_rows([wts[k] for k in _SMALL])
    m_pack, _ = _pack_rows([mom[k] for k in _SMALL])
    v_pack, _ = _pack_rows([var[k] for k in _SMALL])
    d_pack, nm_pack, nv_pack = _adamw_pack(g_pack, w_pack, m_pack, v_pack)
    small_shapes = [wts[k].shape for k in _SMALL]
    for store, pk in ((deltas, d_pack), (new_m, nm_pack), (new_v, nv_pack)):
        for k, a in zip(_SMALL, _unpack_rows(pk, sizes[1:], small_shapes)):
            store[k] = a

    return (loss, grad_x[None], *[grads[k] for k in _ORDER], *[deltas[k] for k in _ORDER],
            *[new_m[k] for k in _ORDER], *[new_v[k] for k in _ORDER])
```

```python
import functools
import math

import jax
import jax.numpy as jnp
from jax import lax
from jax.experimental import pallas as pl
from jax.experimental.pallas import tpu as pltpu

F32 = jnp.float32
BF16 = jnp.bfloat16

N_DEV = 8
QK_NOPE = 128
QK_ROPE = 64
V_HEAD = 128
HEAD_PAD = 256
ROPE_THETA = 10000.0
CHUNK = 128
SGU_GROUP = 128
RMS_EPS = 1e-6
LANES = 128
SUBLANES = 8

ADAM_LR = 0.001
ADAM_B1 = 0.9
ADAM_B2 = 0.999
ADAM_EPS = 1e-08
ADAM_WD = 0.01
ADAM_STEP = 10

VMEM_LIMIT = 48 * 1024 * 1024
MM_TILE = (2048, 512, 2048)
MM_TILE_TA = (512, 2048)
ATTN_TILE = 512
HEADS_PER_STEP = (8, 4)
ROW_KERNEL_BYTES = 24 * 1024 * 1024
SHARD_TILE_ELEMS = 256 * 1024
SLABS_PER_STEP = 2
TAIL_SPLIT = (3, 8)
NEG_BIG = -1e30
MESH = pl.DeviceIdType.MESH


def _pick(n, target, mult=LANES):
    best = None
    d = mult
    while d <= min(n, target):
        if n % d == 0:
            best = d
        d += mult
    return best or n


def _row_tile(t, width, n_blocks, mult=2 * SUBLANES):
    return _pick(t, max(mult, ROW_KERNEL_BYTES // (3 * n_blocks * width * 4)), mult)


def _shard_tile(r, c, elems=SHARD_TILE_ELEMS, max_rows=256):
    tr = _pick(r, max_rows, 2 * SUBLANES)
    return tr, _pick(c, max(LANES, elems // tr))


def _params(sem):
    return pltpu.CompilerParams(dimension_semantics=sem, vmem_limit_bytes=VMEM_LIMIT)


def _full(shape):
    nd = len(shape)
    return pl.BlockSpec(shape, lambda *_: (0,) * nd)


def _rows(tr, w, cb=0):
    return pl.BlockSpec((tr, w), lambda i: (i, cb))


class _Comm:
    def __init__(self, ins, out_shapes, sems, start, finish, aliases=None):
        self.ins, self.out_shapes, self.sems, self.start, self.finish = list(ins), list(out_shapes), list(sems), start, finish
        self.aliases = dict(aliases or {})


def _call(body, *, grid, in_specs, out_specs, out_shape, scratch_shapes=(), sem, name, args, comm=None):
    if comm is None:
        outs = pl.pallas_call(body, grid=grid, in_specs=list(in_specs), out_specs=list(out_specs),
                              out_shape=list(out_shape), scratch_shapes=list(scratch_shapes),
                              compiler_params=_params(sem), name=name)(*args)
        return list(outs), []
    n_in, n_out, n_sc = len(in_specs), len(out_shape), len(scratch_shapes)
    nci, nco = len(comm.ins), len(comm.out_shapes)

    def hosted(*refs):
        ins, refs = refs[:n_in], refs[n_in:]
        cins, refs = refs[:nci], refs[nci:]
        outs, refs = refs[:n_out], refs[n_out:]
        couts, refs = refs[:nco], refs[nco:]
        scratch, csems = refs[:n_sc], refs[n_sc:]
        ids = [pl.program_id(i) for i in range(len(grid))]
        first = functools.reduce(jnp.logical_and, [i == 0 for i in ids])
        last = functools.reduce(jnp.logical_and, [i == g - 1 for i, g in zip(ids, grid)])

        @pl.when(first)
        def _():
            comm.start(cins, couts, csems)

        body(*ins, *outs, *scratch)

        @pl.when(last)
        def _():
            comm.finish(cins, couts, csems)

    any_spec = pl.BlockSpec(memory_space=pl.ANY)
    res = pl.pallas_call(
        hosted, grid=grid, in_specs=list(in_specs) + [any_spec] * nci, out_specs=list(out_specs) + [any_spec] * nco,
        out_shape=list(out_shape) + comm.out_shapes, scratch_shapes=list(scratch_shapes) + comm.sems,
        input_output_aliases={n_in + i: n_out + o for i, o in comm.aliases.items()},
        compiler_params=pltpu.CompilerParams(dimension_semantics=("arbitrary",) * len(grid),
                                             vmem_limit_bytes=VMEM_LIMIT, has_side_effects=True),
        name=name)(*args, *comm.ins)
    return list(res[:n_out]), list(res[n_out:])


def _swiglu_grads(g, u, d):
    s = 1.0 / (1.0 + jnp.exp(-g))
    return (d * u * (s * (1.0 + g * (1.0 - s)))).astype(BF16), (d * (g * s)).astype(BF16)


def _mm(a, b, *, ta=False, tb=False, add=None, out_dtype=F32, tm=None, tn=None, tk=None, name, comm=None,
        slab=None, a_slab0=0, swiglu=None, rope=None):
    sq = None
    if ta:
        tm, tn = tm or MM_TILE_TA[0], tn or MM_TILE_TA[1]
    if slab is None:
        m, k = (a.shape[1], a.shape[0]) if ta else a.shape
        n = b.shape[0] if tb else b.shape[1]
        assert k == (b.shape[1] if tb else b.shape[0]), (a.shape, b.shape, ta, tb)
        tm, tn, tk = _pick(m, tm or MM_TILE[0]), _pick(n, tn or MM_TILE[1]), _pick(k, tk or MM_TILE[2])
        if rope is not None:
            tn = _pick(n, max(tn, HEAD_PAD), HEAD_PAD)
        grid = (m // tm, n // tn, k // tk)
        a_spec = pl.BlockSpec((tk, tm), lambda i, j, kk: (kk, i)) if ta else pl.BlockSpec((tm, tk), lambda i, j, kk: (i, kk))
        b_spec = pl.BlockSpec((tn, tk), lambda i, j, kk: (j, kk)) if tb else pl.BlockSpec((tk, tn), lambda i, j, kk: (kk, j))
        o_spec, o_shape = pl.BlockSpec((tm, tn), lambda i, j, kk: (i, j)), (m, n)
    elif slab == "n":
        m, k = (a.shape[1], a.shape[0]) if ta else a.shape
        s, c = b.shape[0], (b.shape[1] if tb else b.shape[2])
        assert k == (b.shape[2] if tb else b.shape[1]), (a.shape, b.shape, ta, tb)
        tm, tn, tk = _pick(m, tm or MM_TILE[0]), c, _pick(k, tk or MM_TILE[2])
        grid = (m // tm, s, k // tk)
        a_spec = pl.BlockSpec((tk, tm), lambda i, j, kk: (kk, i)) if ta else pl.BlockSpec((tm, tk), lambda i, j, kk: (i, kk))
        b_spec = (pl.BlockSpec((sq, c, tk), lambda i, j, kk: (j, 0, kk)) if tb
                  else pl.BlockSpec((sq, tk, c), lambda i, j, kk: (j, kk, 0)))
        o_spec, o_shape = pl.BlockSpec((sq, tm, c), lambda i, j, kk: (j, i, 0)), (s, m, c)
    elif slab == "m":
        assert ta and not tb
        s, k, c = a.shape
        n = b.shape[1]
        assert k == b.shape[0], (a.shape, b.shape)
        tm, tn, tk = c, _pick(n, tn or MM_TILE[1]), _pick(k, tk or MM_TILE[2])
        grid = (s, n // tn, k // tk)
        a_spec = pl.BlockSpec((sq, tk, c), lambda i, j, kk: (i, kk, 0))
        b_spec = pl.BlockSpec((tk, tn), lambda i, j, kk: (kk, j))
        o_spec, o_shape = pl.BlockSpec((sq, c, tn), lambda i, j, kk: (i, 0, j)), (s, c, n)
    else:
        assert slab == "k" and not ta
        s, c = b.shape[0], (b.shape[2] if tb else b.shape[1])
        m, n = a.shape[1], (b.shape[1] if tb else b.shape[2])
        assert a.shape[2] == c and a.shape[0] >= a_slab0 + s, (a.shape, b.shape, a_slab0)
        tm, tn, tk = _pick(m, tm or MM_TILE[0]), _pick(n, tn or MM_TILE[1]), c
        per_step = SLABS_PER_STEP if (s % SLABS_PER_STEP == 0 and a_slab0 % SLABS_PER_STEP == 0) else 1
        first = a_slab0 // per_step
        grid = (m // tm, n // tn, s // per_step)
        a_spec = pl.BlockSpec((per_step, tm, c), lambda i, j, kk: (kk + first, i, 0))
        b_spec = (pl.BlockSpec((per_step, tn, c), lambda i, j, kk: (kk, j, 0)) if tb
                  else pl.BlockSpec((per_step, c, tn), lambda i, j, kk: (kk, 0, j)))
        o_spec, o_shape = pl.BlockSpec((tm, tn), lambda i, j, kk: (i, j)), (m, n)
    nk = grid[2]
    dims = (((0 if ta else 1,), (1 if tb else 0,)), ((), ()))

    def product(a_ref, b_ref):
        if slab != "k":
            return lax.dot_general(a_ref[...].astype(BF16), b_ref[...].astype(BF16), dims, preferred_element_type=F32)
        r = None
        for u in range(a_ref.shape[0]):
            p = lax.dot_general(a_ref[u].astype(BF16), b_ref[u].astype(BF16), dims, preferred_element_type=F32)
            r = p if r is None else r + p
        return r

    if swiglu is not None:
        assert slab == "n" and add is None
        o_block = pl.BlockSpec((2, sq, tm, c), lambda i, j, kk: (0, j, i, 0))
        o_shape, out_dtype = (2,) + o_shape, BF16

    if rope is not None:
        assert slab is None and add is None and swiglu is None and tn % HEAD_PAD == 0
        out_dtype = BF16
    extras = tuple(swiglu or ()) + tuple(rope or ())

    def body(*refs):
        a_ref, b_ref = refs[:2]
        add_ref = refs[2] if add is not None else None
        x0_ref, x1_ref = refs[2:4] if extras else (None, None)
        o_ref = refs[2 + (add is not None) + len(extras)]
        acc_ref = refs[-1] if nk > 1 else None

        def finish(r):
            if swiglu is not None:
                o_ref[0], o_ref[1] = _swiglu_grads(x0_ref[...], x1_ref[...], r)
                return
            if rope is not None:
                cos, sin = x0_ref[...], x1_ref[...]
                for h in range(tn // HEAD_PAD):
                    lo = h * HEAD_PAD
                    o_ref[:, lo:lo + QK_NOPE] = r[:, lo:lo + QK_NOPE].astype(BF16)
                    o_ref[:, lo + QK_NOPE:lo + HEAD_PAD] = _rope(r[:, lo + QK_NOPE:lo + HEAD_PAD], cos, sin).astype(BF16)
                return
            if add_ref is not None:
                r = r + add_ref[...].astype(F32)
            o_ref[...] = r.astype(o_ref.dtype)

        if nk == 1:
            finish(product(a_ref, b_ref))
            return
        kk = pl.program_id(2)

        @pl.when(kk == 0)
        def _():
            acc_ref[...] = product(a_ref, b_ref)

        if nk > 2:
            @pl.when(jnp.logical_and(kk > 0, kk < nk - 1))
            def _():
                acc_ref[...] += product(a_ref, b_ref)

        @pl.when(kk == nk - 1)
        def _():
            finish(acc_ref[...] + product(a_ref, b_ref))

    in_specs = [a_spec, b_spec] + ([o_spec] if add is not None else []) + ([o_spec] * 2 if swiglu is not None else [])
    if rope is not None:
        in_specs += [pl.BlockSpec((tm, LANES), lambda i, j, kk: (i, 0))] * 2
    args = (a, b) + ((add,) if add is not None else ()) + extras
    if swiglu is not None:
        o_spec = o_block
    outs, comm_outs = _call(
        body, grid=grid, in_specs=in_specs, out_specs=[o_spec],
        out_shape=[jax.ShapeDtypeStruct(o_shape, out_dtype)],
        scratch_shapes=[pltpu.VMEM((tm, tn), F32)] if nk > 1 else [],
        sem=("parallel", "parallel", "arbitrary"), name=name, args=args, comm=comm)
    return outs[0] if comm is None else (outs[0], comm_outs)


def _rms_scale(x):
    return lax.rsqrt(jnp.mean(x * x, axis=-1, keepdims=True) + RMS_EPS)


def _rms_bwd(xhat, r, g, dy):
    t = dy * g
    dx = r * (t - xhat * jnp.mean(t * xhat, axis=-1, keepdims=True))
    return dx, dy * xhat


_GELU_C = math.sqrt(2.0 / math.pi)


def _gelu(x):
    return x * (0.5 * (1.0 + jnp.tanh(_GELU_C * (x + 0.044715 * (x * x * x)))))


def _gelu_and_grad(x):
    t = jnp.tanh(_GELU_C * (x + 0.044715 * (x * x * x)))
    cdf = 0.5 * (1.0 + t)
    return x * cdf, cdf + x * (0.5 * (1.0 - t * t) * (_GELU_C * (1.0 + 3.0 * 0.044715 * (x * x))))


def _sigmoid(x):
    return 1.0 / (1.0 + jnp.exp(-x))


def _swap_halves(x):
    lane = lax.broadcasted_iota(jnp.int32, x.shape, 1)
    first = (lane % QK_ROPE) < (QK_ROPE // 2)
    return jnp.where(first, pltpu.roll(x, LANES - QK_ROPE // 2, 1), pltpu.roll(x, QK_ROPE // 2, 1))


def _rope(x, cos, sin_signed):
    return x * cos + _swap_halves(x) * sin_signed


def _rope_bwd(d, cos, sin_signed):
    return d * cos + _swap_halves(d * sin_signed)


def _rope_tables(pos_col, inv_freq_row, sign_row):
    t = pos_col.shape[0]
    tr = _pick(t, 512, SUBLANES)

    def body(p_ref, f_ref, s_ref, cos_ref, sin_ref):
        ang = p_ref[...].astype(F32) * f_ref[...]
        cos_ref[...] = jnp.cos(ang)
        sin_ref[...] = jnp.sin(ang) * s_ref[...]

    return pl.pallas_call(
        body, grid=(t // tr,), in_specs=[_rows(tr, 1), _full((1, LANES)), _full((1, LANES))],
        out_specs=[_rows(tr, LANES), _rows(tr, LANES)],
        out_shape=[jax.ShapeDtypeStruct((t, LANES), F32)] * 2,
        compiler_params=_params(("parallel",)), name="rope_tables")(pos_col, inv_freq_row, sign_row)


def _norm_fwd(x, g, name):
    t, d = x.shape
    tr = _row_tile(t, d, 2)

    def body(x_ref, g_ref, y_ref):
        xv = x_ref[...]
        y_ref[...] = (xv * _rms_scale(xv) * g_ref[...]).astype(BF16)

    return pl.pallas_call(
        body, grid=(t // tr,), in_specs=[_rows(tr, d), _full((1, d))], out_specs=_rows(tr, d),
        out_shape=jax.ShapeDtypeStruct((t, d), BF16), compiler_params=_params(("parallel",)), name=name)(x, g)


def _lat_fwd(z_lat, qg, kvg, cos, sin, ql, kvl):
    t = z_lat.shape[0]
    tr = _row_tile(t, z_lat.shape[1], 2)

    def body(z_ref, qg_ref, kvg_ref, cos_ref, sin_ref, qn_ref, kvn_ref, kpe_ref):
        q = z_ref[:, 0:ql]
        qn_ref[...] = (q * _rms_scale(q) * qg_ref[...]).astype(BF16)
        kv = z_ref[:, ql:ql + kvl]
        kvn_ref[...] = (kv * _rms_scale(kv) * kvg_ref[...]).astype(BF16)
        kpe_ref[...] = _rope(z_ref[:, ql + kvl:ql + kvl + LANES], cos_ref[...], sin_ref[...]).astype(BF16)

    w = z_lat.shape[1]
    return pl.pallas_call(
        body, grid=(t // tr,),
        in_specs=[_rows(tr, w), _full((1, ql)), _full((1, kvl)), _rows(tr, LANES), _rows(tr, LANES)],
        out_specs=[_rows(tr, ql), _rows(tr, kvl), _rows(tr, LANES)],
        out_shape=[jax.ShapeDtypeStruct((t, ql), BF16), jax.ShapeDtypeStruct((t, kvl), BF16),
                   jax.ShapeDtypeStruct((t, LANES), BF16)],
        compiler_params=_params(("parallel",)), name="lat_fwd")(z_lat, qg, kvg, cos, sin)


def _tril_mask():
    r = lax.broadcasted_iota(jnp.int32, (CHUNK, CHUNK), 0)
    c = lax.broadcasted_iota(jnp.int32, (CHUNK, CHUNK), 1)
    return r >= c


def _sgu_fwd(z_uv, gs, ws, b_col):
    t = z_uv.shape[0]
    sw = z_uv.shape[1] // 2
    groups = sw // SGU_GROUP
    tr = _pick(t, 256, CHUNK)

    def body(u_ref, v_ref, gs_ref, ws_ref, b_ref, o_ref):
        v = _gelu(v_ref[...])
        vn = (v * _rms_scale(v) * gs_ref[...]).astype(BF16)
        tri = _tril_mask()
        for g in range(groups):
            wg = jnp.where(tri, ws_ref[g], 0.0).astype(BF16)
            cols = slice(g * SGU_GROUP, (g + 1) * SGU_GROUP)
            for c in range(tr // CHUNK):
                rows = slice(c * CHUNK, (c + 1) * CHUNK)
                mixed = jnp.dot(wg, vn[rows, cols], preferred_element_type=F32) + b_ref[g]
                o_ref[rows, cols] = (_gelu(u_ref[rows, cols]) * mixed).astype(BF16)

    return pl.pallas_call(
        body, grid=(t // tr,),
        in_specs=[_rows(tr, sw, 0), _rows(tr, sw, 1), _full((1, sw)), _full(ws.shape), _full(b_col.shape)],
        out_specs=_rows(tr, sw), out_shape=jax.ShapeDtypeStruct((t, sw), BF16),
        compiler_params=_params(("parallel",)), name="sgu_fwd")(z_uv, z_uv, gs, ws, b_col)


def _merge_fwd(y_attn, y_sgu, z_g, b_gate, comm=None):
    t, d = y_attn.shape
    tr = _row_tile(t, d, 5)

    def body(ya_ref, ys_ref, g0_ref, g1_ref, b0_ref, b1_ref, o_ref):
        g0 = _sigmoid(g0_ref[...] + b0_ref[...])
        g1 = _sigmoid(g1_ref[...] + b1_ref[...])
        o_ref[...] = (g0 * ya_ref[...] + g1 * ys_ref[...]).astype(BF16)

    bspec0 = pl.BlockSpec((1, d), lambda i: (0, 0))
    bspec1 = pl.BlockSpec((1, d), lambda i: (0, 1))
    outs, comm_outs = _call(
        body, grid=(t // tr,),
        in_specs=[_rows(tr, d), _rows(tr, d), _rows(tr, d, 0), _rows(tr, d, 1), bspec0, bspec1],
        out_specs=[_rows(tr, d)], out_shape=[jax.ShapeDtypeStruct((t, d), BF16)],
        sem=("parallel",), name="merge_fwd", args=(y_attn, y_sgu, z_g, z_g, b_gate, b_gate), comm=comm)
    return outs[0], comm_outs


def _swiglu_fwd(gate, up, comm=None):
    t, f = gate.shape
    tr = _row_tile(t, f, 3)

    def body(g_ref, u_ref, o_ref):
        g = g_ref[...]
        o_ref[...] = (g * _sigmoid(g) * u_ref[...]).astype(BF16)

    outs, comm_outs = _call(
        body, grid=(t // tr,), in_specs=[_rows(tr, f), _rows(tr, f)], out_specs=[_rows(tr, f)],
        out_shape=[jax.ShapeDtypeStruct((t, f), BF16)], sem=("parallel",), name="swiglu_fwd", args=(gate, up), comm=comm)
    return outs[0], comm_outs


def _loss_head(h2, g, target):
    t, d = h2.shape
    tr = _row_tile(t, d, 3)

    def body(h_ref, g_ref, t_ref, loss_ref, dh_ref, dhb_ref, dg_ref):
        @pl.when(pl.program_id(0) == 0)
        def _():
            loss_ref[...] = jnp.zeros_like(loss_ref)
            dg_ref[...] = jnp.zeros_like(dg_ref)

        h = h_ref[...]
        r = _rms_scale(h)
        hhat = h * r
        gv = g_ref[...]
        err = hhat * gv - t_ref[...]
        loss_ref[...] += jnp.full(loss_ref.shape, 0.5 * jnp.sum(jnp.mean(err * err, axis=-1)), F32)
        dx, dg_rows = _rms_bwd(hhat, r, gv, err * (1.0 / d))
        dh_ref[...] = dx
        dhb_ref[...] = dx.astype(BF16)
        dg_ref[...] += jnp.sum(dg_rows, axis=0, keepdims=True)

    return pl.pallas_call(
        body, grid=(t // tr,), in_specs=[_rows(tr, d), _full((1, d)), _rows(tr, d)],
        out_specs=[_full((1, LANES)), _rows(tr, d), _rows(tr, d), _full((1, d))],
        out_shape=[jax.ShapeDtypeStruct((1, LANES), F32), jax.ShapeDtypeStruct((t, d), F32),
                   jax.ShapeDtypeStruct((t, d), BF16), jax.ShapeDtypeStruct((1, d), F32)],
        compiler_params=_params(("arbitrary",)), name="loss_head")(h2, g, target)


def _norm_bwd(x, g, dy, resid, name, comm=None):
    t, d = x.shape
    tr = _row_tile(t, d, 5)

    def body(x_ref, g_ref, dy_ref, r_ref, dx_ref, dxb_ref, dg_ref):
        @pl.when(pl.program_id(0) == 0)
        def _():
            dg_ref[...] = jnp.zeros_like(dg_ref)

        xv = x_ref[...]
        r = _rms_scale(xv)
        dx, dg_rows = _rms_bwd(xv * r, r, g_ref[...], dy_ref[...])
        dx = r_ref[...] + dx
        dx_ref[...] = dx
        dxb_ref[...] = dx.astype(BF16)
        dg_ref[...] += jnp.sum(dg_rows, axis=0, keepdims=True)

    outs, comm_outs = _call(
        body, grid=(t // tr,), in_specs=[_rows(tr, d), _full((1, d)), _rows(tr, d), _rows(tr, d)],
        out_specs=[_rows(tr, d), _rows(tr, d), _full((1, d))],
        out_shape=[jax.ShapeDtypeStruct((t, d), F32), jax.ShapeDtypeStruct((t, d), BF16),
                   jax.ShapeDtypeStruct((1, d), F32)],
        sem=("arbitrary",), name=name, args=(x, g, dy, resid), comm=comm)
    return (outs[0], outs[1], outs[2]) if comm is None else (outs[0], outs[1], outs[2], comm_outs)


def _merge_bwd(dmerged, y_attn, y_sgu, z_g, b_gate):
    t, d = y_attn.shape
    tr = _row_tile(t, d, 7)

    def body(dm_ref, ya_ref, ys_ref, g0_ref, g1_ref, b0_ref, b1_ref, dya_ref, dys_ref, dz_ref, db_ref):
        @pl.when(pl.program_id(0) == 0)
        def _():
            db_ref[...] = jnp.zeros_like(db_ref)

        dm = dm_ref[...]
        g0 = _sigmoid(g0_ref[...] + b0_ref[...])
        g1 = _sigmoid(g1_ref[...] + b1_ref[...])
        dya_ref[...] = (dm * g0).astype(BF16)
        dys_ref[...] = (dm * g1).astype(BF16)
        dl0 = dm * ya_ref[...] * (g0 * (1.0 - g0))
        dl1 = dm * ys_ref[...] * (g1 * (1.0 - g1))
        dz_ref[:, 0:d] = dl0.astype(BF16)
        dz_ref[:, d:2 * d] = dl1.astype(BF16)
        db_ref[:, 0:d] += jnp.sum(dl0, axis=0, keepdims=True)
        db_ref[:, d:2 * d] += jnp.sum(dl1, axis=0, keepdims=True)

    bspec0 = pl.BlockSpec((1, d), lambda i: (0, 0))
    bspec1 = pl.BlockSpec((1, d), lambda i: (0, 1))
    return pl.pallas_call(
        body, grid=(t // tr,),
        in_specs=[_rows(tr, d), _rows(tr, d), _rows(tr, d), _rows(tr, d, 0), _rows(tr, d, 1), bspec0, bspec1],
        out_specs=[_rows(tr, d), _rows(tr, d), _rows(tr, 2 * d), _full((1, 2 * d))],
        out_shape=[jax.ShapeDtypeStruct((t, d), BF16), jax.ShapeDtypeStruct((t, d), BF16),
                   jax.ShapeDtypeStruct((t, 2 * d), BF16), jax.ShapeDtypeStruct((1, 2 * d), F32)],
        compiler_params=_params(("arbitrary",)), name="merge_bwd")(dmerged, y_attn, y_sgu, z_g, z_g, b_gate, b_gate)


def _sgu_bwd(z_uv, ds_out, gs, ws, b_col):
    t = z_uv.shape[0]
    sw = z_uv.shape[1] // 2
    groups = sw // SGU_GROUP
    tr = _pick(t, 256, CHUNK)

    def body(u_ref, v_ref, d_ref, gs_ref, ws_ref, b_ref, dz_ref, dws_ref, db_ref, dgs_ref, dvn_ref):
        @pl.when(pl.program_id(0) == 0)
        def _():
            dws_ref[...] = jnp.zeros_like(dws_ref)
            db_ref[...] = jnp.zeros_like(db_ref)
            dgs_ref[...] = jnp.zeros_like(dgs_ref)

        v, dgelu_v = _gelu_and_grad(v_ref[...])
        r = _rms_scale(v)
        vhat = v * r
        gsv = gs_ref[...]
        vn = (vhat * gsv).astype(BF16)
        tri = _tril_mask()
        for g in range(groups):
            wg = jnp.where(tri, ws_ref[g], 0.0).astype(BF16)
            cols = slice(g * SGU_GROUP, (g + 1) * SGU_GROUP)
            for c in range(tr // CHUNK):
                rows = slice(c * CHUNK, (c + 1) * CHUNK)
                vn_cg = vn[rows, cols]
                mixed = jnp.dot(wg, vn_cg, preferred_element_type=F32) + b_ref[g]
                u, dgelu_u = _gelu_and_grad(u_ref[rows, cols])
                dso = d_ref[rows, cols]
                dz_ref[rows, cols] = (dso * mixed * dgelu_u).astype(BF16)
                dmixed = dso * u
                db_ref[g] += jnp.sum(dmixed, axis=1, keepdims=True)
                dmixed_b = dmixed.astype(BF16)
                dws_ref[g] += jnp.where(
                    tri, lax.dot_general(dmixed_b, vn_cg, (((1,), (1,)), ((), ())), preferred_element_type=F32), 0.0)
                dvn_ref[rows, cols] = lax.dot_general(wg, dmixed_b, (((0,), (0,)), ((), ())), preferred_element_type=F32)
        dvn = dvn_ref[...]
        dv, dgs_rows = _rms_bwd(vhat, r, gsv, dvn)
        dz_ref[:, sw:2 * sw] = (dv * dgelu_v).astype(BF16)
        dgs_ref[...] += jnp.sum(dgs_rows, axis=0, keepdims=True)

    return pl.pallas_call(
        body, grid=(t // tr,),
        in_specs=[_rows(tr, sw, 0), _rows(tr, sw, 1), _rows(tr, sw), _full((1, sw)), _full(ws.shape), _full(b_col.shape)],
        out_specs=[_rows(tr, 2 * sw), _full(ws.shape), _full(b_col.shape), _full((1, sw))],
        out_shape=[jax.ShapeDtypeStruct((t, 2 * sw), BF16), jax.ShapeDtypeStruct(ws.shape, F32),
                   jax.ShapeDtypeStruct(b_col.shape, F32), jax.ShapeDtypeStruct((1, sw), F32)],
        scratch_shapes=[pltpu.VMEM((tr, sw), F32)],
        compiler_params=_params(("arbitrary",)), name="sgu_bwd")(z_uv, z_uv, ds_out, gs, ws, b_col)


def _lat_bwd(z_lat, qg, kvg, dqn, dkvn, dkpe_heads, cos, sin, ql, kvl):
    t, w = z_lat.shape
    heads = dkpe_heads.shape[0]
    tr = _row_tile(t, w + heads * LANES, 3)

    def body(z_ref, qg_ref, kvg_ref, dq_ref, dkv_ref, dk_ref, cos_ref, sin_ref, dz_ref, dqg_ref, dkvg_ref):
        @pl.when(pl.program_id(0) == 0)
        def _():
            dqg_ref[...] = jnp.zeros_like(dqg_ref)
            dkvg_ref[...] = jnp.zeros_like(dkvg_ref)

        q = z_ref[:, 0:ql]
        r = _rms_scale(q)
        dx, dg_rows = _rms_bwd(q * r, r, qg_ref[...], dq_ref[...])
        dz_ref[:, 0:ql] = dx.astype(BF16)
        dqg_ref[...] += jnp.sum(dg_rows, axis=0, keepdims=True)
        kv = z_ref[:, ql:ql + kvl]
        r = _rms_scale(kv)
        dx, dg_rows = _rms_bwd(kv * r, r, kvg_ref[...], dkv_ref[...])
        dz_ref[:, ql:ql + kvl] = dx.astype(BF16)
        dkvg_ref[...] += jnp.sum(dg_rows, axis=0, keepdims=True)
        dk = dk_ref[0]
        for h in range(1, heads):
            dk = dk + dk_ref[h]
        dz_ref[:, ql + kvl:ql + kvl + LANES] = _rope_bwd(dk, cos_ref[...], sin_ref[...]).astype(BF16)

    return pl.pallas_call(
        body, grid=(t // tr,),
        in_specs=[_rows(tr, w), _full((1, ql)), _full((1, kvl)), _rows(tr, ql), _rows(tr, kvl),
                  pl.BlockSpec((heads, tr, LANES), lambda i: (0, i, 0)), _rows(tr, LANES), _rows(tr, LANES)],
        out_specs=[_rows(tr, w), _full((1, ql)), _full((1, kvl))],
        out_shape=[jax.ShapeDtypeStruct((t, w), BF16), jax.ShapeDtypeStruct((1, ql), F32),
                   jax.ShapeDtypeStruct((1, kvl), F32)],
        compiler_params=_params(("arbitrary",)), name="lat_bwd")(z_lat, qg, kvg, dqn, dkvn, dkpe_heads, cos, sin)


_NT = (((1,), (1,)), ((), ()))


def _attn_scale():
    return (QK_NOPE + QK_ROPE) ** -0.5


def _heads_per_step(heads, wanted):
    return wanted if heads % wanted == 0 else 1


def _attn_fwd(q_c, kv, kpe, comm=None):
    t = q_c.shape[0]
    heads = q_c.shape[1] // HEAD_PAD
    tq = _pick(t, ATTN_TILE)
    nq = t // tq
    scale = _attn_scale()
    to_log2 = scale * math.log2(math.e)
    tn_dims = (((0,), (0,)), ((), ()))

    hps = _heads_per_step(heads, HEADS_PER_STEP[0])

    def body(q_ref, kv_ref, kpe_ref, o_ref, ob_ref, lse_ref, m_sc, l_sc, acc_sc):
        qi, ki = pl.program_id(1), pl.program_id(2)

        @pl.when(ki == 0)
        def _():
            m_sc[...] = jnp.full_like(m_sc, NEG_BIG)
            l_sc[...] = jnp.zeros_like(l_sc)
            acc_sc[...] = jnp.zeros_like(acc_sc)

        def step(diagonal):
            for u in range(hps):
                lo = u * HEAD_PAD
                kc = jnp.concatenate([kv_ref[:, lo:lo + QK_NOPE], kpe_ref[...]], axis=1)
                st = lax.dot_general(kc, q_ref[:, lo:lo + HEAD_PAD], _NT, preferred_element_type=F32)
                if diagonal:
                    krow = lax.broadcasted_iota(jnp.int32, st.shape, 0)
                    qcol = lax.broadcasted_iota(jnp.int32, st.shape, 1)
                    st = jnp.where(qcol >= krow, st, NEG_BIG)
                m_prev = m_sc[u]
                m_new = jnp.maximum(m_prev, jnp.max(st, axis=0, keepdims=True))
                alpha = jnp.exp2((m_prev - m_new) * to_log2)
                pt = jnp.exp2((st - m_new) * to_log2)
                l_sc[u] = alpha * l_sc[u] + jnp.sum(pt, axis=0, keepdims=True)
                acc_sc[u] = alpha * acc_sc[u] + lax.dot_general(
                    kv_ref[:, lo + QK_NOPE:lo + HEAD_PAD], pt.astype(BF16), tn_dims, preferred_element_type=F32)
                m_sc[u] = m_new

        @pl.when(ki < qi)
        def _():
            step(False)

        @pl.when(ki == qi)
        def _():
            step(True)
            for u in range(hps):
                o = (acc_sc[u] / l_sc[u]).T
                o_ref[:, u * V_HEAD:(u + 1) * V_HEAD] = o
                ob_ref[:, u * V_HEAD:(u + 1) * V_HEAD] = o.astype(BF16)
                lse_ref[u] = m_sc[u] * scale + jnp.log(l_sc[u])

    omap = lambda g, qi, ki: (qi, g)
    outs, comm_outs = _call(
        body, grid=(heads // hps, nq, nq),
        in_specs=[pl.BlockSpec((tq, hps * HEAD_PAD), omap),
                  pl.BlockSpec((tq, hps * HEAD_PAD), lambda g, qi, ki: (jnp.minimum(ki, qi), g)),
                  pl.BlockSpec((tq, LANES), lambda g, qi, ki: (jnp.minimum(ki, qi), 0))],
        out_specs=[pl.BlockSpec((tq, hps * V_HEAD), omap), pl.BlockSpec((tq, hps * V_HEAD), omap),
                   pl.BlockSpec((hps, 1, tq), lambda g, qi, ki: (g, 0, qi))],
        out_shape=[jax.ShapeDtypeStruct((t, heads * V_HEAD), F32), jax.ShapeDtypeStruct((t, heads * V_HEAD), BF16),
                   jax.ShapeDtypeStruct((heads, 1, t), F32)],
        scratch_shapes=[pltpu.VMEM((hps, 1, tq), F32), pltpu.VMEM((hps, 1, tq), F32),
                        pltpu.VMEM((hps, V_HEAD, tq), F32)],
        sem=("parallel", "parallel", "arbitrary"), name="attn_fwd", args=(q_c, kv, kpe), comm=comm)
    return outs[0], outs[1], outs[2], comm_outs


def _attn_bwd(q_c, kv, kpe, o, do, lse_row, cos, sin, comm=None):
    t = q_c.shape[0]
    heads = q_c.shape[1] // HEAD_PAD
    tk = _pick(t, ATTN_TILE)
    nk = t // tk
    scale = _attn_scale()
    tn_dims = (((0,), (0,)), ((), ()))

    hps = _heads_per_step(heads, HEADS_PER_STEP[1])

    def body(q_ref, kv_ref, kpe_ref, do_ref, lse_ref, o_ref, cos_ref, sin_ref, dq_ref, dkv_ref, dkpe_ref,
             dk_sc, dv_sc, delta_sc, dq_sc):
        ki, qi = pl.program_id(1), pl.program_id(2)

        @pl.when(jnp.logical_and(ki == 0, qi == 0))
        def _():
            dq_sc[...] = jnp.zeros_like(dq_sc)

        @pl.when(qi == 0)
        def _():
            dk_sc[...] = jnp.zeros_like(dk_sc)
            dv_sc[...] = jnp.zeros_like(dv_sc)

        @pl.when(ki == 0)
        def _():
            for u in range(hps):
                cols = slice(u * V_HEAD, (u + 1) * V_HEAD)
                delta_sc[qi * hps + u] = jnp.sum((do_ref[:, cols] * o_ref[:, cols]).T, axis=0, keepdims=True)

        def step(diagonal):
            for u in range(hps):
                lo = u * HEAD_PAD
                kc = jnp.concatenate([kv_ref[:, lo:lo + QK_NOPE], kpe_ref[...]], axis=1)
                q = q_ref[:, lo:lo + HEAD_PAD]
                st = lax.dot_general(kc, q, _NT, preferred_element_type=F32) * scale
                pt = jnp.exp(st - lse_ref[u])
                if diagonal:
                    krow = lax.broadcasted_iota(jnp.int32, st.shape, 0)
                    qcol = lax.broadcasted_iota(jnp.int32, st.shape, 1)
                    pt = jnp.where(qcol >= krow, pt, 0.0)
                do_b = do_ref[:, u * V_HEAD:(u + 1) * V_HEAD].astype(BF16)
                dv_sc[u] += jnp.dot(pt.astype(BF16), do_b, preferred_element_type=F32)
                dpt = lax.dot_general(kv_ref[:, lo + QK_NOPE:lo + HEAD_PAD], do_b, _NT, preferred_element_type=F32)
                dst = (pt * (dpt - delta_sc[qi * hps + u]) * scale).astype(BF16)
                dk_sc[u] += jnp.dot(dst, q, preferred_element_type=F32)
                rows = pl.ds(pl.multiple_of(qi * tk, tk), tk)
                dq_sc[rows, lo:lo + HEAD_PAD] += lax.dot_general(dst, kc, tn_dims, preferred_element_type=F32)

        @pl.when(qi > ki)
        def _():
            step(False)

        @pl.when(qi == ki)
        def _():
            step(True)

        @pl.when(qi == nk - 1)
        def _():
            for u in range(hps):
                lo = u * HEAD_PAD
                dkv_ref[:, lo:lo + QK_NOPE] = dk_sc[u, :, 0:QK_NOPE].astype(BF16)
                dkv_ref[:, lo + QK_NOPE:lo + HEAD_PAD] = dv_sc[u].astype(BF16)
                dkpe_ref[u] = dk_sc[u, :, QK_NOPE:QK_NOPE + LANES]

        @pl.when(jnp.logical_and(ki == nk - 1, qi == nk - 1))
        def _():
            cos, sin = cos_ref[...], sin_ref[...]
            for u in range(hps):
                lo = u * HEAD_PAD
                dq_ref[:, lo:lo + QK_NOPE] = dq_sc[:, lo:lo + QK_NOPE].astype(BF16)
                dq_ref[:, lo + QK_NOPE:lo + HEAD_PAD] = _rope_bwd(dq_sc[:, lo + QK_NOPE:lo + HEAD_PAD], cos, sin).astype(BF16)

    qclamp = lambda g, ki, qi: (jnp.maximum(qi, ki), g)
    outs, comm_outs = _call(
        body, grid=(heads // hps, nk, nk),
        in_specs=[pl.BlockSpec((tk, hps * HEAD_PAD), qclamp),
                  pl.BlockSpec((tk, hps * HEAD_PAD), lambda g, ki, qi: (ki, g)),
                  pl.BlockSpec((tk, LANES), lambda g, ki, qi: (ki, 0)),
                  pl.BlockSpec((tk, hps * V_HEAD), qclamp),
                  pl.BlockSpec((hps, 1, tk), lambda g, ki, qi: (g, 0, jnp.maximum(qi, ki))),
                  pl.BlockSpec((tk, hps * V_HEAD), lambda g, ki, qi: (jnp.where(ki == 0, qi, 0), g)),
                  _full((t, LANES)), _full((t, LANES))],
        out_specs=[pl.BlockSpec((t, hps * HEAD_PAD), lambda g, ki, qi: (0, g)),
                   pl.BlockSpec((tk, hps * HEAD_PAD), lambda g, ki, qi: (ki, g)),
                   pl.BlockSpec((hps, tk, LANES), lambda g, ki, qi: (g, ki, 0))],
        out_shape=[jax.ShapeDtypeStruct((t, heads * HEAD_PAD), BF16),
                   jax.ShapeDtypeStruct((t, heads * HEAD_PAD), BF16), jax.ShapeDtypeStruct((heads, t, LANES), F32)],
        scratch_shapes=[pltpu.VMEM((hps, tk, HEAD_PAD), F32), pltpu.VMEM((hps, tk, V_HEAD), F32),
                        pltpu.VMEM((nk * hps, 1, tk), F32), pltpu.VMEM((t, hps * HEAD_PAD), F32)],
        sem=("parallel", "arbitrary", "arbitrary"), name="attn_bwd",
        args=(q_c, kv, kpe, do, lse_row, o, cos, sin), comm=comm)
    return outs[0], outs[1], outs[2], comm_outs


def _local_step(x, pos_col, target, small, shards, opt):
    t = x.shape[0]
    ql, kvl = small["q_norm_g"].shape[1], small["kv_norm_g"].shape[1]
    sw = small["sgu_norm_g"].shape[1]
    heads = (shards["w_uq"].shape[1] * N_DEV) // (QK_NOPE + QK_ROPE)
    big = {}
    big.update(_compute_layout({"w_in": _all_gather([shards["w_in"]])[0]}, ql, kvl, heads, sw))
    half = QK_ROPE // 2
    lane = jnp.arange(LANES)
    inv_freq = ROPE_THETA ** (-jnp.arange(0, QK_ROPE, 2, dtype=F32) / QK_ROPE)
    inv_row = inv_freq[lane % half][None, :]
    sign_row = jnp.where((lane % QK_ROPE) < half, -1.0, 1.0).astype(F32)[None, :]
    cos, sin = _rope_tables(pos_col, inv_row, sign_row)
    ws = small["w_sgu"]
    b_col = small["b_sgu_col"]

    def arrived(names, bufs):
        big.update(_compute_layout(dict(zip(names, bufs)), ql, kvl, heads, sw))

    a = _norm_fwd(x, small["norm_mix_g"], "norm_mix_fwd")
    z_lat, g_qk = _mm(a, big["w_lat_t"], tb=True, name="z_lat",
                      comm=_gather_stage(1, [shards["w_uq"], shards["w_ukv"]]))
    z_uv, (g_sgu, *g_qk) = _mm(a, big["w_uv_t"], tb=True, name="z_uv",
                               comm=_join(_gather_stage(1, [shards["w_o_sgu"]]), _gather_stage(2, g_qk)))
    z_g, (g_attn, g_sgu, *g_qk) = _mm(
        a, big["w_g_t"], tb=True, name="z_g",
        comm=_join(_gather_stage(1, [shards["w_o_attn"]]), _gather_stage(2, [g_sgu]), _gather_stage(3, g_qk)))
    arrived(["w_uq", "w_ukv"], g_qk)
    qn, kvn, kpe = _lat_fwd(z_lat, small["q_norm_g"], small["kv_norm_g"], cos, sin, ql, kvl)
    q_c, (g_attn, g_sgu) = _mm(qn, big["w_uq"], name="q_up_rope", rope=(cos, sin),
                               comm=_join(_gather_stage(2, [g_attn]), _gather_stage(3, [g_sgu])))
    kv, (g_attn, g_out) = _mm(kvn, big["w_ukv"], out_dtype=BF16, name="kv_up",
                              comm=_join(_gather_stage(3, [g_attn]), _gather_stage(1, [shards["w_out"]])))
    arrived(["w_o_sgu", "w_o_attn"], [g_sgu, g_attn])
    attn, attn_b, lse, (w_gate, w_up) = _attn_fwd(
        q_c, kv, kpe, comm=_gather_stage(1, [shards["w_gate_ffn"], shards["w_up_ffn"]]))
    s_out = _sgu_fwd(z_uv, small["sgu_norm_g"], ws, b_col)
    y_sgu, (g_out,) = _mm(s_out, big["w_o_sgu"], name="y_sgu", comm=_gather_stage(2, [g_out]))
    y_attn, (w_gate, g_out) = _mm(attn_b, big["w_o_attn"], name="y_attn",
                                  comm=_join(_gather_stage(2, [w_gate]), _gather_stage(3, [g_out])))
    arrived(["w_out"], [g_out])
    merged, (w_up, w_gate) = _merge_fwd(y_attn, y_sgu, z_g, small["b_gate"],
                                        comm=_join(_gather_stage(2, [w_up]), _gather_stage(3, [w_gate])))
    h1, (w_up,) = _mm(merged, big["w_out"], add=x, name="h1", comm=_gather_stage(3, [w_up]))
    f = _norm_fwd(h1, small["norm_ffn_g"], "norm_ffn_fwd")
    gate, w_down = _mm(f, w_gate, tb=True, slab="n", name="ffn_gate", comm=_gather_stage(1, [shards["w_down_ffn"]]))
    up, w_down = _mm(f, w_up, tb=True, slab="n", name="ffn_up", comm=_gather_stage(2, w_down))
    ffn = gate.shape[2]
    gate, up = gate.reshape(N_DEV * t, ffn), up.reshape(N_DEV * t, ffn)
    act, (w_down,) = _swiglu_fwd(gate, up, comm=_gather_stage(3, w_down))
    act = act.reshape(N_DEV, t, ffn)
    h2 = _mm(act, w_down, slab="k", add=h1, name="h2")
    loss_row, dh2, dh2_b, d_norm_final = _loss_head(h2, small["norm_final_g"], target)

    def pair_sums(names, slabs, bufs):
        return [_pair_sum(g, b, "pair_sum_" + k) for k, g, b in zip(names, slabs, bufs)]

    parts, updates = {}, {}

    def update(names, label, comm=None):
        res, got = _adamw_shards([parts[k] for k in names], [opt[k] for k in names], "adamw_" + label, comm=comm)
        updates.update(zip(names, res))
        return got

    down_slabs = [_mm(act, dh2_b, ta=True, slab="m", out_dtype=BF16, name="dw_down")]
    dgu, bufs = _mm(dh2_b, w_down, tb=True, slab="n", tm=MM_TILE[0] // 2, name="dact_swiglu_bwd",
                    comm=_to_sibling(down_slabs), swiglu=(gate.reshape(N_DEV, t, ffn), up.reshape(N_DEV, t, ffn)))
    dgu = dgu.reshape(2 * N_DEV, t, ffn)
    down_pair = pair_sums(["w_down_ffn"], down_slabs, bufs)
    dw_gu, got = _mm(dgu, f, ta=True, slab="m", out_dtype=BF16, name="dw_gate_up", comm=_to_chips(down_pair))
    parts["w_down_ffn"] = got[0]
    gu_names = ["w_gate_ffn", "w_up_ffn"]
    df, bufs = _mm(dgu, w_gate, slab="k", name="df_gate", comm=_to_sibling([dw_gu, dw_gu], first=[0, N_DEV]))
    gu_pairs = [_pair_sum(dw_gu, b, "pair_sum_" + k, first=s0) for k, b, s0 in zip(gu_names, bufs, [0, N_DEV])]
    half = _pick(gu_pairs[1].shape[1], gu_pairs[1].shape[1] // 2, 2 * SUBLANES)
    df, up_parts = _mm(dgu, w_up, slab="k", a_slab0=N_DEV, add=df, name="df_up",
                       comm=_to_chips(gu_pairs[1:], rows=[("r", 0, half)]))
    quarter = _pick(half, half // 2, 2 * SUBLANES)
    dh1, dh1_b, d_norm_ffn, gate_parts = _norm_bwd(h1, small["norm_ffn_g"], df, dh2, "norm_ffn_bwd",
                                                  comm=_to_chips(gu_pairs[:1], rows=[("r", 0, quarter)]))
    dw_out = _mm(merged, dh1_b, ta=True, out_dtype=BF16, name="dw_out")
    out_slabs = [_slabs_from_rows(dw_out)]
    dmerged, bufs = _mm(dh1_b, big["w_out"], tb=True, name="dmerged", comm=_to_sibling(out_slabs))
    out_pair = pair_sums(["w_out"], out_slabs, bufs)
    dy_attn, dy_sgu, dz_g, d_b_gate = _merge_bwd(dmerged, y_attn, y_sgu, z_g, small["b_gate"])
    dw_o_sgu = _mm(s_out, dy_sgu, ta=True, out_dtype=BF16, name="dw_o_sgu")
    ds_out = _mm(dy_sgu, big["w_o_sgu"], tb=True, name="ds_out")
    dz_uv, d_ws, d_b_col, d_sgu_norm = _sgu_bwd(z_uv, ds_out, small["sgu_norm_g"], ws, b_col)
    dw_o_attn = _mm(attn_b, dy_attn, ta=True, out_dtype=BF16, name="dw_o_attn")
    mix_names = ["w_o_sgu", "w_o_attn"]
    mix_slabs = [_slabs_from_cols(dw_o_sgu), _slabs_from_rows(dw_o_attn)]
    dattn, bufs = _mm(dy_attn, big["w_o_attn"], tb=True, name="dattn", comm=_to_sibling(mix_slabs))
    mix_pairs = pair_sums(mix_names, mix_slabs, bufs)
    rows = gu_pairs[1].shape[1]
    dq_p, dkv, dkpe_heads, got = _attn_bwd(
        q_c, kv, kpe, attn, dattn, lse, cos, sin,
        comm=_join(_to_chips(gu_pairs[:1], rows=[("r", quarter, rows - quarter)], into=gate_parts),
                   _to_chips(gu_pairs[1:], rows=[("r", half, rows - half)], into=up_parts)))
    parts.update(zip(gu_names, got))
    dw_uq = _mm(qn, dq_p, ta=True, out_dtype=BF16, name="dw_uq")
    dw_ukv = _mm(kvn, dkv, ta=True, out_dtype=BF16, name="dw_ukv")
    dqn = _mm(dq_p, big["w_uq"], tb=True, name="dqn")
    dkvn = _mm(dkv, big["w_ukv"], tb=True, name="dkvn")
    dz_lat, d_q_norm, d_kv_norm = _lat_bwd(z_lat, small["q_norm_g"], small["kv_norm_g"], dqn, dkvn, dkpe_heads,
                                           cos, sin, ql, kvl)
    dw_g, got = _mm(dz_g, a, ta=True, out_dtype=BF16, name="dw_g", comm=_to_chips(out_pair))
    parts["w_out"] = got[0]
    dw_uv, got = _mm(dz_uv, a, ta=True, out_dtype=BF16, name="dw_uv", comm=_to_chips(mix_pairs[1:]))
    parts["w_o_attn"] = got[0]
    dw_lat, got = _mm(dz_lat, a, ta=True, out_dtype=BF16, name="dw_lat", comm=_to_chips(mix_pairs[:1]))
    parts["w_o_sgu"] = got[0]
    lat = ql + kvl + QK_ROPE
    dw_uq_cols = dw_uq.reshape(ql, heads, HEAD_PAD)[:, :, :QK_NOPE + QK_ROPE].reshape(ql, heads * (QK_NOPE + QK_ROPE))
    in_names = ["w_uq", "w_ukv", "w_in"]
    in_slabs = [_slabs_from_cols(dw_uq_cols), _slabs_from_cols(dw_ukv),
                _slabs_from_rows(jnp.concatenate([dw_lat[:lat], dw_uv, dw_g], axis=0))]
    da = _mm(dz_lat, big["w_lat_t"], name="da_lat")
    da, bufs = _mm(dz_uv, big["w_uv_t"], add=da, name="da_uv", comm=_to_sibling(in_slabs))
    uq_pair, ukv_pair, in_pair = pair_sums(in_names, in_slabs, bufs)
    cols = in_pair.shape[2]
    first = ((cols * TAIL_SPLIT[0]) // TAIL_SPLIT[1]) // LANES * LANES or cols
    da, in_parts = _mm(dz_g, big["w_g_t"], add=da, name="da_g", comm=_to_chips([in_pair], rows=[("c", 0, first)]))
    grad_x, _, d_norm_mix, got = _norm_bwd(x, small["norm_mix_g"], da, dh1, "norm_mix_bwd",
                                          comm=_to_chips([uq_pair, ukv_pair]))
    parts["w_uq"], parts["w_ukv"] = got
    rest = _to_chips([in_pair], rows=[("c", first, cols - first)], into=in_parts) if first < cols else None
    got = update(["w_gate_ffn", "w_up_ffn", "w_down_ffn"], "ffn", comm=rest)
    parts["w_in"] = got[0] if rest is not None else in_parts[0]
    update(["w_out", "w_o_attn"], "mixer_out")
    for k in ("w_o_sgu", "w_uq", "w_ukv", "w_in"):
        update([k], k)

    gs = {"norm_mix_g": d_norm_mix, "b_gate": d_b_gate, "q_norm_g": d_q_norm, "kv_norm_g": d_kv_norm,
          "sgu_norm_g": d_sgu_norm, "w_sgu": d_ws, "b_sgu_col": d_b_col, "norm_ffn_g": d_norm_ffn,
          "norm_final_g": d_norm_final}
    return loss_row, grad_x, gs, updates


def _my_place():
    return lax.axis_index("x"), lax.axis_index("y"), lax.axis_index("c")


N_CHIPS = N_DEV // 2

_GATHER_SEMS = [[(3,), (3,), ()], [(4,), (4,)], [(1,), (1,)]]


def _halves(shape):
    r, c = shape
    if (c // 2) % LANES == 0:
        return ("c", 0, c // 2), ("c", c // 2, c // 2)
    assert (r // 2) % (2 * SUBLANES) == 0, shape
    return ("r", 0, r // 2), ("r", r // 2, r // 2)


def _gather_copies(stage, ins, outs, sems):
    x, y, c = _my_place()
    me, x_nbr, y_nbr, diag = 4 * x + 2 * y + c, 4 * (1 - x) + 2 * y + c, 4 * x + 2 * (1 - y) + c, 4 * (1 - x) + 2 * (1 - y) + c
    sibling = (x, y, 1 - c)

    def remote(w, k, src, dst, to):
        return pltpu.make_async_remote_copy(src_ref=src, dst_ref=dst, send_sem=sems[0].at[w, k], recv_sem=sems[1].at[w, k],
                                            device_id=to, device_id_type=MESH)

    out = []
    for w in range(len(outs)):
        if stage == 1:
            dst = outs[w].at[me]
            out.append(pltpu.make_async_copy(ins[w], dst, sems[2].at[w]))
            out += [remote(w, k, ins[w], dst, to) for k, to in enumerate([sibling, (1 - x, y, c), (x, 1 - y, c)])]
        elif stage == 2:
            first, second = _halves(outs[w].shape[1:])
            out.append(remote(w, 0, _window(ins[w], x_nbr, first), _window(outs[w], x_nbr, first), (x, 1 - y, c)))
            out.append(remote(w, 1, _window(ins[w], y_nbr, second), _window(outs[w], y_nbr, second), (1 - x, y, c)))
            out.append(remote(w, 2, ins[w].at[x_nbr], outs[w].at[x_nbr], sibling))
            out.append(remote(w, 3, ins[w].at[y_nbr], outs[w].at[y_nbr], sibling))
        else:
            out.append(remote(w, 0, ins[w].at[diag], outs[w].at[diag], sibling))
    return out


def _gather_stage(stage, arrays):
    n = len(arrays)

    def start(ins, outs, sems):
        for cp in _gather_copies(stage, ins, outs, sems):
            cp.start()

    def finish(ins, outs, sems):
        for cp in _gather_copies(stage, ins, outs, sems):
            cp.wait()

    shapes = [jax.ShapeDtypeStruct(((N_DEV,) + a.shape) if stage == 1 else a.shape, a.dtype) for a in arrays]
    return _Comm(arrays, shapes, [pltpu.SemaphoreType.DMA((n,) + s) for s in _GATHER_SEMS[stage - 1]], start, finish,
                 aliases=None if stage == 1 else {w: w for w in range(n)})


def _join(*comms):
    ins, shapes, sems, aliases, spans = [], [], [], {}, []
    for cm in comms:
        spans.append((len(ins), len(ins) + len(cm.ins), len(shapes), len(shapes) + len(cm.out_shapes),
                      len(sems), len(sems) + len(cm.sems)))
        aliases.update({len(ins) + i: len(shapes) + o for i, o in cm.aliases.items()})
        ins, shapes, sems = ins + cm.ins, shapes + cm.out_shapes, sems + cm.sems

    def each(half):
        def run(i_refs, o_refs, s_refs):
            for cm, (i0, i1, o0, o1, s0, s1) in zip(comms, spans):
                getattr(cm, half)(i_refs[i0:i1], o_refs[o0:o1], s_refs[s0:s1])
        return run

    return _Comm(ins, shapes, sems, each("start"), each("finish"), aliases)


def _all_gather(shards):
    n = len(shards)
    n_sems = [len(s) for s in _GATHER_SEMS]

    def body(*refs):
        ins, outs, sems = refs[:n], refs[n:2 * n], refs[2 * n:]
        s0 = 0
        for stage in (1, 2, 3):
            mine = sems[s0:s0 + n_sems[stage - 1]]
            s0 += n_sems[stage - 1]
            copies = _gather_copies(stage, ins if stage == 1 else outs, outs, mine)
            for cp in copies:
                cp.start()
            for cp in copies:
                cp.wait()

    any_spec = pl.BlockSpec(memory_space=pl.ANY)
    return pl.pallas_call(
        body, in_specs=[any_spec] * n, out_specs=[any_spec] * n,
        out_shape=[jax.ShapeDtypeStruct((N_DEV,) + s.shape, s.dtype) for s in shards],
        scratch_shapes=[pltpu.SemaphoreType.DMA((n,) + s) for stage in _GATHER_SEMS for s in stage],
        compiler_params=pltpu.CompilerParams(has_side_effects=True), name="all_gather_weights")(*shards)


def _to_sibling(grads, first=None):
    n = len(grads)
    first = first or [0] * n

    def copies(ins, outs, sems):
        x, y, c = _my_place()
        send_sems, recv_sems = sems
        return [pltpu.make_async_remote_copy(
            src_ref=ins[w].at[first[w] + 2 * i + (1 - c)], dst_ref=outs[w].at[i], send_sem=send_sems.at[w, i],
            recv_sem=recv_sems.at[w, i], device_id=(x, y, 1 - c), device_id_type=MESH)
            for w in range(n) for i in range(N_CHIPS)]

    def start(ins, outs, sems):
        for cp in copies(ins, outs, sems):
            cp.start()

    def finish(ins, outs, sems):
        for cp in copies(ins, outs, sems):
            cp.wait()

    return _Comm(grads, [jax.ShapeDtypeStruct((N_CHIPS,) + g.shape[1:], g.dtype) for g in grads],
                 [pltpu.SemaphoreType.DMA((n, N_CHIPS)), pltpu.SemaphoreType.DMA((n, N_CHIPS))], start, finish)


def _window(ref, slab, win):
    if win is None:
        return ref.at[slab]
    if win[0] == "r":
        return ref.at[slab, pl.ds(win[1], win[2])]
    return ref.at[slab, slice(None), pl.ds(win[1], win[2])]


def _to_chips(parts, rows=None, into=None):
    n = len(parts)
    rows = rows or [None] * n

    def copies(ins, outs, sems):
        x, y, c = _my_place()
        send_sems, recv_sems, local_sems = sems
        mine = 2 * x + y
        chips = [(1 - x, y), (x, 1 - y), (1 - x, 1 - y)]
        remote = [pltpu.make_async_remote_copy(
            src_ref=_window(ins[w], 2 * cx + cy, rows[w]), dst_ref=_window(outs[w], mine, rows[w]),
            send_sem=send_sems.at[w, j], recv_sem=recv_sems.at[w, j], device_id=(cx, cy, c), device_id_type=MESH)
            for w in range(n) for j, (cx, cy) in enumerate(chips)]
        local = [pltpu.make_async_copy(_window(ins[w], mine, rows[w]), _window(outs[w], mine, rows[w]),
                                       local_sems.at[w]) for w in range(n)]
        return remote + local

    def start(ins, outs, sems):
        for cp in copies(ins, outs, sems):
            cp.start()

    def finish(ins, outs, sems):
        for cp in copies(ins, outs, sems):
            cp.wait()

    return _Comm(list(parts) + list(into or []), [jax.ShapeDtypeStruct(p.shape, p.dtype) for p in parts],
                 [pltpu.SemaphoreType.DMA((n, N_CHIPS - 1)), pltpu.SemaphoreType.DMA((n, N_CHIPS - 1)),
                  pltpu.SemaphoreType.DMA((n,))], start, finish,
                 aliases={n + w: w for w in range(n)} if into else None)


def _pair_sum(g, buf, name, first=0):
    _, r, c = g.shape
    tr, tc = _shard_tile(r, c, 4 * SHARD_TILE_ELEMS, 1024)
    core = (lax.axis_index("c") + first).astype(jnp.int32).reshape(1)

    def body(core_ref, g_ref, b_ref, o_ref):
        o_ref[...] = (g_ref[...].astype(F32) + b_ref[...].astype(F32)).astype(o_ref.dtype)

    blk = (1, tr, tc)
    return pl.pallas_call(
        body, grid_spec=pltpu.PrefetchScalarGridSpec(
            num_scalar_prefetch=1, grid=(N_CHIPS, r // tr, c // tc),
            in_specs=[pl.BlockSpec(blk, lambda i, j, l, core_ref: (2 * i + core_ref[0], j, l)),
                      pl.BlockSpec(blk, lambda i, j, l, core_ref: (i, j, l))],
            out_specs=pl.BlockSpec(blk, lambda i, j, l, core_ref: (i, j, l))),
        out_shape=jax.ShapeDtypeStruct(buf.shape, buf.dtype),
        compiler_params=_params(("parallel", "parallel", "parallel")), name=name)(core, g, buf)


def _all_reduce_pack(pack):
    r = pack.shape[0]

    def body(x_ref, out_ref, gath_ref, send_sems, recv_sems, local_sem):
        x, y, c = _my_place()
        me, sibling = (x, y, c), (x, y, 1 - c)
        chips = [(1 - x, y), (x, 1 - y), (1 - x, 1 - y)]

        def slab(place):
            return gath_ref.at[4 * place[0] + 2 * place[1] + place[2]]

        def copy(k, place, to, src=None):
            return pltpu.make_async_remote_copy(
                src_ref=slab(place) if src is None else src, dst_ref=slab(place),
                send_sem=send_sems.at[k], recv_sem=recv_sems.at[k], device_id=to, device_id_type=MESH)

        mine = pltpu.make_async_copy(x_ref, slab(me), local_sem)
        mine.start()
        first = [copy(0, me, sibling, src=x_ref)]
        first += [copy(1 + j, me, (*chip, c), src=x_ref) for j, chip in enumerate(chips)]
        for cp in first:
            cp.start()
        passed = [copy(4 + j, (*chip, c), sibling) for j, chip in enumerate(chips)]
        for j, chip in enumerate(chips):
            copy(1 + j, (*chip, c), me).wait_recv()
            passed[j].start()
        copy(0, sibling, me).wait_recv()
        for j, chip in enumerate(chips):
            copy(4 + j, (*chip, 1 - c), me).wait_recv()
        for cp in first + passed:
            cp.wait_send()
        mine.wait()
        acc = gath_ref[0]
        for i in range(1, N_DEV):
            acc = acc + gath_ref[i]
        out_ref[...] = acc

    vmem = pl.BlockSpec(memory_space=pltpu.VMEM)
    return pl.pallas_call(
        body, in_specs=[vmem], out_specs=vmem, out_shape=jax.ShapeDtypeStruct(pack.shape, F32),
        scratch_shapes=[pltpu.VMEM((N_DEV, r, LANES), F32), pltpu.SemaphoreType.DMA((7,)),
                        pltpu.SemaphoreType.DMA((7,)), pltpu.SemaphoreType.DMA],
        compiler_params=pltpu.CompilerParams(vmem_limit_bytes=VMEM_LIMIT), name="all_reduce_small")(pack)


def _adamw_math(w, g, m, v):
    m = ADAM_B1 * m + (1.0 - ADAM_B1) * g
    v = ADAM_B2 * v + (1.0 - ADAM_B2) * (g * g)
    m_hat = m / (1.0 - ADAM_B1 ** ADAM_STEP)
    v_hat = v / (1.0 - ADAM_B2 ** ADAM_STEP)
    delta = -ADAM_LR * (m_hat / (jnp.sqrt(v_hat) + ADAM_EPS) + ADAM_WD * w)
    return delta, m, v


def _adamw_shards(parts, opts, name, comm=None):
    r, c = opts[0][0].shape
    n_parts, k = parts[0].shape[0], len(parts)
    tr, tc = _shard_tile(r, c, SHARD_TILE_ELEMS // k)

    def body(*refs):
        ins, outs = refs[:4 * k], refs[4 * k:]
        for s in range(k):
            p_ref, w_ref, m_ref, v_ref = ins[4 * s:4 * s + 4]
            g_ref, d_ref, nm_ref, nv_ref = outs[4 * s:4 * s + 4]
            g = p_ref[0].astype(F32)
            for i in range(1, n_parts):
                g = g + p_ref[i].astype(F32)
            g_ref[...] = g
            d_ref[...], nm_ref[...], nv_ref[...] = _adamw_math(w_ref[...], g, m_ref[...], v_ref[...])

    spec = pl.BlockSpec((tr, tc), lambda i, j: (i, j))
    args = [a for p, o in zip(parts, opts) for a in (p,) + tuple(o)]
    outs, comm_outs = _call(
        body, grid=(r // tr, c // tc),
        in_specs=[pl.BlockSpec((n_parts, tr, tc), lambda i, j: (0, i, j)), spec, spec, spec] * k,
        out_specs=[spec] * (4 * k), out_shape=[jax.ShapeDtypeStruct((r, c), F32)] * (4 * k),
        sem=("parallel", "parallel"), name=name, args=args, comm=comm)
    return [outs[4 * s:4 * s + 4] for s in range(k)], comm_outs


def _adamw_pack(g, w, m, v):
    r, c = w.shape

    def body(g_ref, w_ref, m_ref, v_ref, d_ref, nm_ref, nv_ref):
        d_ref[...], nm_ref[...], nv_ref[...] = _adamw_math(w_ref[...], g_ref[...], m_ref[...], v_ref[...])

    return pl.pallas_call(
        body, in_specs=[_full((r, c))] * 4, out_specs=[_full((r, c))] * 3, grid=(1,),
        out_shape=[jax.ShapeDtypeStruct((r, c), F32)] * 3,
        compiler_params=_params(("arbitrary",)), name="adamw_small")(g, w, m, v)


def _cols_from_slabs(g):
    return jnp.transpose(g, (1, 0, 2)).reshape(g.shape[1], N_DEV * g.shape[2])


def _slabs_from_cols(w):
    r, c8 = w.shape
    return jnp.transpose(w.reshape(r, N_DEV, c8 // N_DEV), (1, 0, 2))


def _rows_from_slabs(g):
    return g.reshape(N_DEV * g.shape[1], g.shape[2])


def _slabs_from_rows(w):
    return w.reshape(N_DEV, w.shape[0] // N_DEV, w.shape[1])


def _compute_layout(gathered, ql, kvl, heads, sw):
    out = {}
    for k, g in gathered.items():
        if k == "w_in":
            lat = ql + kvl + QK_ROPE
            w_in_t = _rows_from_slabs(g)
            out["w_lat_t"] = jnp.pad(w_in_t[:lat], ((0, LANES - QK_ROPE), (0, 0)))
            out["w_uv_t"] = w_in_t[lat:lat + 2 * sw]
            out["w_g_t"] = w_in_t[lat + 2 * sw:]
        elif k == "w_uq":
            per_head = _cols_from_slabs(g).reshape(ql, heads, QK_NOPE + QK_ROPE)
            pad = HEAD_PAD - QK_NOPE - QK_ROPE
            out["w_uq"] = jnp.pad(per_head, ((0, 0), (0, 0), (0, pad))).reshape(ql, heads * HEAD_PAD)
        elif k in ("w_o_attn", "w_out", "w_down_ffn"):
            out[k.removesuffix("_ffn")] = _rows_from_slabs(g)
        else:
            out[k.removesuffix("_ffn")] = _cols_from_slabs(g)
    return out


_SMALL =["norm_mix_g", "b_gate", "q_norm_g", "kv_norm_g", "sgu_norm_g", "w_sgu", "b_sgu", "norm_ffn_g", "norm_final_g"]
_BIG = ["w_in", "w_uq", "w_ukv", "w_o_attn", "w_o_sgu", "w_out", "w_gate_ffn", "w_up_ffn", "w_down_ffn"]
_TRANSPOSED = ("w_in", "w_gate_ffn", "w_up_ffn")
_ORDER = ["norm_mix_g", "w_in", "b_gate", "q_norm_g", "w_uq", "kv_norm_g", "w_ukv", "w_o_attn", "sgu_norm_g", "w_sgu",
          "b_sgu", "w_o_sgu", "w_out", "norm_ffn_g", "w_gate_ffn", "w_up_ffn", "w_down_ffn", "norm_final_g"]


def _pack_rows(parts):
    rows, sizes = [], []
    for p in parts:
        flat = p.reshape(-1)
        n = flat.shape[0]
        padded = -(-n // (SUBLANES * LANES)) * (SUBLANES * LANES)
        rows.append(jnp.pad(flat, (0, padded - n)).reshape(padded // LANES, LANES))
        sizes.append((n, padded // LANES))
    return jnp.concatenate(rows, axis=0), sizes


def _unpack_rows(pack, sizes, shapes):
    out, r0 = [], 0
    for (n, nr), shp in zip(sizes, shapes):
        out.append(pack[r0:r0 + nr].reshape(-1)[:n].reshape(shp))
        r0 += nr
    return out


def kernel(x, positions, norm_mix_g, w_in, b_gate, q_norm_g, w_uq, kv_norm_g, w_ukv, w_o_attn, sgu_norm_g, w_sgu, b_sgu, w_o_sgu, w_out, norm_ffn_g, w_gate_ffn, w_up_ffn, w_down_ffn, norm_final_g, loss_target, m_norm_mix_g, m_w_in, m_b_gate, m_q_norm_g, m_w_uq, m_kv_norm_g, m_w_ukv, m_w_o_attn, m_sgu_norm_g, m_w_sgu, m_b_sgu, m_w_o_sgu, m_w_out, m_norm_ffn_g, m_w_gate_ffn, m_w_up_ffn, m_w_down_ffn, m_norm_final_g, v_norm_mix_g, v_w_in, v_b_gate, v_q_norm_g, v_w_uq, v_kv_norm_g, v_w_ukv, v_w_o_attn, v_sgu_norm_g, v_w_sgu, v_b_sgu, v_w_o_sgu, v_w_out, v_norm_ffn_g, v_w_gate_ffn, v_w_up_ffn, v_w_down_ffn, v_norm_final_g):
    wts = dict(norm_mix_g=norm_mix_g, w_in=w_in, b_gate=b_gate, q_norm_g=q_norm_g, w_uq=w_uq, kv_norm_g=kv_norm_g,
               w_ukv=w_ukv, w_o_attn=w_o_attn, sgu_norm_g=sgu_norm_g, w_sgu=w_sgu, b_sgu=b_sgu, w_o_sgu=w_o_sgu,
               w_out=w_out, norm_ffn_g=norm_ffn_g, w_gate_ffn=w_gate_ffn, w_up_ffn=w_up_ffn, w_down_ffn=w_down_ffn,
               norm_final_g=norm_final_g)
    mom = dict(norm_mix_g=m_norm_mix_g, w_in=m_w_in, b_gate=m_b_gate, q_norm_g=m_q_norm_g, w_uq=m_w_uq,
               kv_norm_g=m_kv_norm_g, w_ukv=m_w_ukv, w_o_attn=m_w_o_attn, sgu_norm_g=m_sgu_norm_g, w_sgu=m_w_sgu,
               b_sgu=m_b_sgu, w_o_sgu=m_w_o_sgu, w_out=m_w_out, norm_ffn_g=m_norm_ffn_g, w_gate_ffn=m_w_gate_ffn,
               w_up_ffn=m_w_up_ffn, w_down_ffn=m_w_down_ffn, norm_final_g=m_norm_final_g)
    var = dict(norm_mix_g=v_norm_mix_g, w_in=v_w_in, b_gate=v_b_gate, q_norm_g=v_q_norm_g, w_uq=v_w_uq,
               kv_norm_g=v_kv_norm_g, w_ukv=v_w_ukv, w_o_attn=v_w_o_attn, sgu_norm_g=v_sgu_norm_g, w_sgu=v_w_sgu,
               b_sgu=v_b_sgu, w_o_sgu=v_w_o_sgu, w_out=v_w_out, norm_ffn_g=v_norm_ffn_g, w_gate_ffn=v_w_gate_ffn,
               w_up_ffn=v_w_up_ffn, w_down_ffn=v_w_down_ffn, norm_final_g=v_norm_final_g)

    t, d = x.shape[1], x.shape[2]
    ql, kvl = q_norm_g.shape[1], kv_norm_g.shape[1]
    heads = (w_uq.shape[2] * N_DEV) // (QK_NOPE + QK_ROPE)
    sw = sgu_norm_g.shape[1]

    def shard(a, k):
        return a[0].T if k in _TRANSPOSED else a[0]

    def unshard(a, k):
        return (a.T if k in _TRANSPOSED else a).reshape(wts[k].shape)

    opt = {k: (shard(wts[k], k), shard(mom[k], k), shard(var[k], k)) for k in _BIG}
    shards = {k: opt[k][0].astype(BF16) for k in _BIG}
    small = {
        "norm_mix_g": norm_mix_g, "b_gate": b_gate, "q_norm_g": q_norm_g, "kv_norm_g": kv_norm_g,
        "sgu_norm_g": sgu_norm_g, "w_sgu": w_sgu[0], "b_sgu_col": b_sgu[0][:, :, None], "norm_ffn_g": norm_ffn_g,
        "norm_final_g": norm_final_g[None, :],
    }

    loss_row, grad_x, gs, updates = _local_step(x[0], positions.reshape(t, 1), loss_target[0], small, shards, opt)
    grads, deltas, new_m, new_v = {}, {}, {}, {}
    for k in _BIG:
        grads[k], deltas[k], new_m[k], new_v[k] = (unshard(a, k) for a in updates[k])

    small_grads = [gs["norm_mix_g"], gs["b_gate"], gs["q_norm_g"], gs["kv_norm_g"], gs["sgu_norm_g"], gs["w_sgu"],
                   gs["b_sgu_col"], gs["norm_ffn_g"], gs["norm_final_g"]]
    pack, sizes = _pack_rows([loss_row] + small_grads)
    total = _all_reduce_pack(pack)
    shapes = [(1, LANES)] + [wts[k].shape for k in _SMALL]
    unpacked = _unpack_rows(total, sizes, shapes)
    loss = unpacked[0][0, 0]
    for k, g in zip(_SMALL, unpacked[1:]):
        grads[k] = g
    g_pack = total[sizes[0][1]:]
    w_pack, _ = _pack_rows([wts[k] for k in _SMALL])
    m_pack, _ = _pack_rows([mom[k] for k in _SMALL])
    v_pack, _ = _pack_rows([var[k] for k in _SMALL])
    d_pack, nm_pack, nv_pack = _adamw_pack(g_pack, w_pack, m_pack, v_pack)
    small_shapes = [wts[k].shape for k in _SMALL]
    for store, pk in ((deltas, d_pack), (new_m, nm_pack), (new_v, nv_pack)):
        for k, a in zip(_SMALL, _unpack_rows(pk, sizes[1:], small_shapes)):
            store[k] = a

    return (loss, grad_x[None], *[grads[k] for k in _ORDER], *[deltas[k] for k in _ORDER],
            *[new_m[k] for k in _ORDER], *[new_v[k] for k in _ORDER])
```

```python
import functools
import math

import jax
import jax.numpy as jnp
from jax import lax
from jax.experimental import pallas as pl
from jax.experimental.pallas import tpu as pltpu

F32 = jnp.float32
BF16 = jnp.bfloat16

N_DEV = 8
QK_NOPE = 128
QK_ROPE = 64
V_HEAD = 128
HEAD_PAD = 256
ROPE_THETA = 10000.0
CHUNK = 128
SGU_GROUP = 128
RMS_EPS = 1e-6
LANES = 128
SUBLANES = 8

ADAM_LR = 0.001
ADAM_B1 = 0.9
ADAM_B2 = 0.999
ADAM_EPS = 1e-08
ADAM_WD = 0.01
ADAM_STEP = 10

VMEM_LIMIT = 48 * 1024 * 1024
MM_TILE = (2048, 512, 2048)
MM_TILE_TA = (512, 2048)
ATTN_TILE = 512
GATHER_STRIPS = 4
HEADS_PER_STEP = (4, 4)
ROW_KERNEL_BYTES = 24 * 1024 * 1024
SHARD_TILE_ELEMS = 256 * 1024
SLABS_PER_STEP = 2
TAIL_SPLIT = (3, 8)
NEG_BIG = -1e30
MESH = pl.DeviceIdType.MESH


def _pick(n, target, mult=LANES):
    best = None
    d = mult
    while d <= min(n, target):
        if n % d == 0:
            best = d
        d += mult
    return best or n


def _row_tile(t, width, n_blocks, mult=2 * SUBLANES):
    return _pick(t, max(mult, ROW_KERNEL_BYTES // (3 * n_blocks * width * 4)), mult)


def _shard_tile(r, c, elems=SHARD_TILE_ELEMS, max_rows=256):
    tr = _pick(r, max_rows, 2 * SUBLANES)
    return tr, _pick(c, max(LANES, elems // tr))


def _params(sem):
    return pltpu.CompilerParams(dimension_semantics=sem, vmem_limit_bytes=VMEM_LIMIT)


def _full(shape):
    nd = len(shape)
    return pl.BlockSpec(shape, lambda *_: (0,) * nd)


def _rows(tr, w, cb=0):
    return pl.BlockSpec((tr, w), lambda i: (i, cb))


class _Comm:
    def __init__(self, ins, out_shapes, sems, start, finish, aliases=None):
        self.ins, self.out_shapes, self.sems, self.start, self.finish = list(ins), list(out_shapes), list(sems), start, finish
        self.aliases = dict(aliases or {})


def _call(body, *, grid, in_specs, out_specs, out_shape, scratch_shapes=(), sem, name, args, comm=None):
    if comm is None:
        outs = pl.pallas_call(body, grid=grid, in_specs=list(in_specs), out_specs=list(out_specs),
                              out_shape=list(out_shape), scratch_shapes=list(scratch_shapes),
                              compiler_params=_params(sem), name=name)(*args)
        return list(outs), []
    n_in, n_out, n_sc = len(in_specs), len(out_shape), len(scratch_shapes)
    nci, nco = len(comm.ins), len(comm.out_shapes)

    def hosted(*refs):
        ins, refs = refs[:n_in], refs[n_in:]
        cins, refs = refs[:nci], refs[nci:]
        outs, refs = refs[:n_out], refs[n_out:]
        couts, refs = refs[:nco], refs[nco:]
        scratch, csems = refs[:n_sc], refs[n_sc:]
        ids = [pl.program_id(i) for i in range(len(grid))]
        first = functools.reduce(jnp.logical_and, [i == 0 for i in ids])
        last = functools.reduce(jnp.logical_and, [i == g - 1 for i, g in zip(ids, grid)])

        @pl.when(first)
        def _():
            comm.start(cins, couts, csems)

        body(*ins, *outs, *scratch)

        @pl.when(last)
        def _():
            comm.finish(cins, couts, csems)

    any_spec = pl.BlockSpec(memory_space=pl.ANY)
    res = pl.pallas_call(
        hosted, grid=grid, in_specs=list(in_specs) + [any_spec] * nci, out_specs=list(out_specs) + [any_spec] * nco,
        out_shape=list(out_shape) + comm.out_shapes, scratch_shapes=list(scratch_shapes) + comm.sems,
        input_output_aliases={n_in + i: n_out + o for i, o in comm.aliases.items()},
        compiler_params=pltpu.CompilerParams(dimension_semantics=("arbitrary",) * len(grid),
                                             vmem_limit_bytes=VMEM_LIMIT, has_side_effects=True),
        name=name)(*args, *comm.ins)
    return list(res[:n_out]), list(res[n_out:])


def _swiglu_grads(g, u, d):
    s = 1.0 / (1.0 + jnp.exp(-g))
    return (d * u * (s * (1.0 + g * (1.0 - s)))).astype(BF16), (d * (g * s)).astype(BF16)


def _mm(a, b, *, ta=False, tb=False, add=None, out_dtype=F32, tm=None, tn=None, tk=None, name, comm=None,
        slab=None, a_slab0=0, swiglu=None, rope=None):
    sq = None
    if ta:
        tm, tn = tm or MM_TILE_TA[0], tn or MM_TILE_TA[1]
    if slab is None:
        m, k = (a.shape[1], a.shape[0]) if ta else a.shape
        n = b.shape[0] if tb else b.shape[1]
        assert k == (b.shape[1] if tb else b.shape[0]), (a.shape, b.shape, ta, tb)
        tm, tn, tk = _pick(m, tm or MM_TILE[0]), _pick(n, tn or MM_TILE[1]), _pick(k, tk or MM_TILE[2])
        if rope is not None:
            tn = _pick(n, max(tn, HEAD_PAD), HEAD_PAD)
        grid = (m // tm, n // tn, k // tk)
        a_spec = pl.BlockSpec((tk, tm), lambda i, j, kk: (kk, i)) if ta else pl.BlockSpec((tm, tk), lambda i, j, kk: (i, kk))
        b_spec = pl.BlockSpec((tn, tk), lambda i, j, kk: (j, kk)) if tb else pl.BlockSpec((tk, tn), lambda i, j, kk: (kk, j))
        o_spec, o_shape = pl.BlockSpec((tm, tn), lambda i, j, kk: (i, j)), (m, n)
    elif slab == "n":
        m, k = (a.shape[1], a.shape[0]) if ta else a.shape
        s, c = b.shape[0], (b.shape[1] if tb else b.shape[2])
        assert k == (b.shape[2] if tb else b.shape[1]), (a.shape, b.shape, ta, tb)
        tm, tn, tk = _pick(m, tm or MM_TILE[0]), c, _pick(k, tk or MM_TILE[2])
        grid = (m // tm, s, k // tk)
        a_spec = pl.BlockSpec((tk, tm), lambda i, j, kk: (kk, i)) if ta else pl.BlockSpec((tm, tk), lambda i, j, kk: (i, kk))
        b_spec = (pl.BlockSpec((sq, c, tk), lambda i, j, kk: (j, 0, kk)) if tb
                  else pl.BlockSpec((sq, tk, c), lambda i, j, kk: (j, kk, 0)))
        o_spec, o_shape = pl.BlockSpec((sq, tm, c), lambda i, j, kk: (j, i, 0)), (s, m, c)
    elif slab == "m":
        assert ta and not tb
        s, k, c = a.shape
        n = b.shape[1]
        assert k == b.shape[0], (a.shape, b.shape)
        tm, tn, tk = c, _pick(n, tn or MM_TILE[1]), _pick(k, tk or MM_TILE[2])
        grid = (s, n // tn, k // tk)
        a_spec = pl.BlockSpec((sq, tk, c), lambda i, j, kk: (i, kk, 0))
        b_spec = pl.BlockSpec((tk, tn), lambda i, j, kk: (kk, j))
        o_spec, o_shape = pl.BlockSpec((sq, c, tn), lambda i, j, kk: (i, 0, j)), (s, c, n)
    else:
        assert slab == "k" and not ta
        s, c = b.shape[0], (b.shape[2] if tb else b.shape[1])
        m, n = a.shape[1], (b.shape[1] if tb else b.shape[2])
        assert a.shape[2] == c and a.shape[0] >= a_slab0 + s, (a.shape, b.shape, a_slab0)
        tm, tn, tk = _pick(m, tm or MM_TILE[0]), _pick(n, tn or MM_TILE[1]), c
        per_step = SLABS_PER_STEP if (s % SLABS_PER_STEP == 0 and a_slab0 % SLABS_PER_STEP == 0) else 1
        first = a_slab0 // per_step
        grid = (m // tm, n // tn, s // per_step)
        a_spec = pl.BlockSpec((per_step, tm, c), lambda i, j, kk: (kk + first, i, 0))
        b_spec = (pl.BlockSpec((per_step, tn, c), lambda i, j, kk: (kk, j, 0)) if tb
                  else pl.BlockSpec((per_step, c, tn), lambda i, j, kk: (kk, 0, j)))
        o_spec, o_shape = pl.BlockSpec((tm, tn), lambda i, j, kk: (i, j)), (m, n)
    nk = grid[2]
    dims = (((0 if ta else 1,), (1 if tb else 0,)), ((), ()))

    def product(a_ref, b_ref):
        if slab != "k":
            return lax.dot_general(a_ref[...].astype(BF16), b_ref[...].astype(BF16), dims, preferred_element_type=F32)
        r = None
        for u in range(a_ref.shape[0]):
            p = lax.dot_general(a_ref[u].astype(BF16), b_ref[u].astype(BF16), dims, preferred_element_type=F32)
            r = p if r is None else r + p
        return r

    if swiglu is not None:
        assert slab == "n" and add is None
        o_block = pl.BlockSpec((2, sq, tm, c), lambda i, j, kk: (0, j, i, 0))
        o_shape, out_dtype = (2,) + o_shape, BF16

    if rope is not None:
        assert slab is None and add is None and swiglu is None and tn % HEAD_PAD == 0
        out_dtype = BF16
    extras = tuple(swiglu or ()) + tuple(rope or ())

    def body(*refs):
        a_ref, b_ref = refs[:2]
        add_ref = refs[2] if add is not None else None
        x0_ref, x1_ref = refs[2:4] if extras else (None, None)
        o_ref = refs[2 + (add is not None) + len(extras)]
        acc_ref = refs[-1] if nk > 1 else None

        def finish(r):
            if swiglu is not None:
                o_ref[0], o_ref[1] = _swiglu_grads(x0_ref[...], x1_ref[...], r)
                return
            if rope is not None:
                cos, sin = x0_ref[...], x1_ref[...]
                for h in range(tn // HEAD_PAD):
                    lo = h * HEAD_PAD
                    o_ref[:, lo:lo + QK_NOPE] = r[:, lo:lo + QK_NOPE].astype(BF16)
                    o_ref[:, lo + QK_NOPE:lo + HEAD_PAD] = _rope(r[:, lo + QK_NOPE:lo + HEAD_PAD], cos, sin).astype(BF16)
                return
            if add_ref is not None:
                r = r + add_ref[...].astype(F32)
            o_ref[...] = r.astype(o_ref.dtype)

        if nk == 1:
            finish(product(a_ref, b_ref))
            return
        kk = pl.program_id(2)

        @pl.when(kk == 0)
        def _():
            acc_ref[...] = product(a_ref, b_ref)

        if nk > 2:
            @pl.when(jnp.logical_and(kk > 0, kk < nk - 1))
            def _():
                acc_ref[...] += product(a_ref, b_ref)

        @pl.when(kk == nk - 1)
        def _():
            finish(acc_ref[...] + product(a_ref, b_ref))

    in_specs = [a_spec, b_spec] + ([o_spec] if add is not None else []) + ([o_spec] * 2 if swiglu is not None else [])
    if rope is not None:
        in_specs += [pl.BlockSpec((tm, LANES), lambda i, j, kk: (i, 0))] * 2
    args = (a, b) + ((add,) if add is not None else ()) + extras
    if swiglu is not None:
        o_spec = o_block
    outs, comm_outs = _call(
        body, grid=grid, in_specs=in_specs, out_specs=[o_spec],
        out_shape=[jax.ShapeDtypeStruct(o_shape, out_dtype)],
        scratch_shapes=[pltpu.VMEM((tm, tn), F32)] if nk > 1 else [],
        sem=("parallel", "parallel", "arbitrary"), name=name, args=args, comm=comm)
    return outs[0] if comm is None else (outs[0], comm_outs)


def _rms_scale(x):
    return lax.rsqrt(jnp.mean(x * x, axis=-1, keepdims=True) + RMS_EPS)


def _rms_bwd(xhat, r, g, dy):
    t = dy * g
    dx = r * (t - xhat * jnp.mean(t * xhat, axis=-1, keepdims=True))
    return dx, dy * xhat


_GELU_C = math.sqrt(2.0 / math.pi)


def _gelu(x):
    return x * (0.5 * (1.0 + jnp.tanh(_GELU_C * (x + 0.044715 * (x * x * x)))))


def _gelu_and_grad(x):
    t = jnp.tanh(_GELU_C * (x + 0.044715 * (x * x * x)))
    cdf = 0.5 * (1.0 + t)
    return x * cdf, cdf + x * (0.5 * (1.0 - t * t) * (_GELU_C * (1.0 + 3.0 * 0.044715 * (x * x))))


def _sigmoid(x):
    return 1.0 / (1.0 + jnp.exp(-x))


def _swap_halves(x):
    lane = lax.broadcasted_iota(jnp.int32, x.shape, 1)
    first = (lane % QK_ROPE) < (QK_ROPE // 2)
    return jnp.where(first, pltpu.roll(x, LANES - QK_ROPE // 2, 1), pltpu.roll(x, QK_ROPE // 2, 1))


def _rope(x, cos, sin_signed):
    return x * cos + _swap_halves(x) * sin_signed


def _rope_bwd(d, cos, sin_signed):
    return d * cos + _swap_halves(d * sin_signed)


def _rope_tables(pos_col, inv_freq_row, sign_row):
    t = pos_col.shape[0]
    tr = _pick(t, 512, SUBLANES)

    def body(p_ref, f_ref, s_ref, cos_ref, sin_ref):
        ang = p_ref[...].astype(F32) * f_ref[...]
        cos_ref[...] = jnp.cos(ang)
        sin_ref[...] = jnp.sin(ang) * s_ref[...]

    return pl.pallas_call(
        body, grid=(t // tr,), in_specs=[_rows(tr, 1), _full((1, LANES)), _full((1, LANES))],
        out_specs=[_rows(tr, LANES), _rows(tr, LANES)],
        out_shape=[jax.ShapeDtypeStruct((t, LANES), F32)] * 2,
        compiler_params=_params(("parallel",)), name="rope_tables")(pos_col, inv_freq_row, sign_row)


def _norm_fwd(x, g, name):
    t, d = x.shape
    tr = _row_tile(t, d, 2)

    def body(x_ref, g_ref, y_ref):
        xv = x_ref[...]
        y_ref[...] = (xv * _rms_scale(xv) * g_ref[...]).astype(BF16)

    return pl.pallas_call(
        body, grid=(t // tr,), in_specs=[_rows(tr, d), _full((1, d))], out_specs=_rows(tr, d),
        out_shape=jax.ShapeDtypeStruct((t, d), BF16), compiler_params=_params(("parallel",)), name=name)(x, g)


def _lat_fwd(z_lat, qg, kvg, cos, sin, ql, kvl):
    t = z_lat.shape[0]
    tr = _row_tile(t, z_lat.shape[1], 2)

    def body(z_ref, qg_ref, kvg_ref, cos_ref, sin_ref, qn_ref, kvn_ref, kpe_ref):
        q = z_ref[:, 0:ql]
        qn_ref[...] = (q * _rms_scale(q) * qg_ref[...]).astype(BF16)
        kv = z_ref[:, ql:ql + kvl]
        kvn_ref[...] = (kv * _rms_scale(kv) * kvg_ref[...]).astype(BF16)
        kpe_ref[...] = _rope(z_ref[:, ql + kvl:ql + kvl + LANES], cos_ref[...], sin_ref[...]).astype(BF16)

    w = z_lat.shape[1]
    return pl.pallas_call(
        body, grid=(t // tr,),
        in_specs=[_rows(tr, w), _full((1, ql)), _full((1, kvl)), _rows(tr, LANES), _rows(tr, LANES)],
        out_specs=[_rows(tr, ql), _rows(tr, kvl), _rows(tr, LANES)],
        out_shape=[jax.ShapeDtypeStruct((t, ql), BF16), jax.ShapeDtypeStruct((t, kvl), BF16),
                   jax.ShapeDtypeStruct((t, LANES), BF16)],
        compiler_params=_params(("parallel",)), name="lat_fwd")(z_lat, qg, kvg, cos, sin)


def _tril_mask():
    r = lax.broadcasted_iota(jnp.int32, (CHUNK, CHUNK), 0)
    c = lax.broadcasted_iota(jnp.int32, (CHUNK, CHUNK), 1)
    return r >= c


def _sgu_fwd(z_uv, gs, ws, b_col):
    t = z_uv.shape[0]
    sw = z_uv.shape[1] // 2
    groups = sw // SGU_GROUP
    tr = _pick(t, 256, CHUNK)

    def body(u_ref, v_ref, gs_ref, ws_ref, b_ref, o_ref):
        v = _gelu(v_ref[...])
        vn = (v * _rms_scale(v) * gs_ref[...]).astype(BF16)
        tri = _tril_mask()
        for g in range(groups):
            wg = jnp.where(tri, ws_ref[g], 0.0).astype(BF16)
            cols = slice(g * SGU_GROUP, (g + 1) * SGU_GROUP)
            for c in range(tr // CHUNK):
                rows = slice(c * CHUNK, (c + 1) * CHUNK)
                mixed = jnp.dot(wg, vn[rows, cols], preferred_element_type=F32) + b_ref[g]
                o_ref[rows, cols] = (_gelu(u_ref[rows, cols]) * mixed).astype(BF16)

    return pl.pallas_call(
        body, grid=(t // tr,),
        in_specs=[_rows(tr, sw, 0), _rows(tr, sw, 1), _full((1, sw)), _full(ws.shape), _full(b_col.shape)],
        out_specs=_rows(tr, sw), out_shape=jax.ShapeDtypeStruct((t, sw), BF16),
        compiler_params=_params(("parallel",)), name="sgu_fwd")(z_uv, z_uv, gs, ws, b_col)


def _merge_fwd(y_attn, y_sgu, z_g, b_gate, comm=None):
    t, d = y_attn.shape
    tr = _row_tile(t, d, 5)

    def body(ya_ref, ys_ref, g0_ref, g1_ref, b0_ref, b1_ref, o_ref):
        g0 = _sigmoid(g0_ref[...] + b0_ref[...])
        g1 = _sigmoid(g1_ref[...] + b1_ref[...])
        o_ref[...] = (g0 * ya_ref[...] + g1 * ys_ref[...]).astype(BF16)

    bspec0 = pl.BlockSpec((1, d), lambda i: (0, 0))
    bspec1 = pl.BlockSpec((1, d), lambda i: (0, 1))
    outs, comm_outs = _call(
        body, grid=(t // tr,),
        in_specs=[_rows(tr, d), _rows(tr, d), _rows(tr, d, 0), _rows(tr, d, 1), bspec0, bspec1],
        out_specs=[_rows(tr, d)], out_shape=[jax.ShapeDtypeStruct((t, d), BF16)],
        sem=("parallel",), name="merge_fwd", args=(y_attn, y_sgu, z_g, z_g, b_gate, b_gate), comm=comm)
    return outs[0], comm_outs


def _swiglu_fwd(gate, up, comm=None):
    t, f = gate.shape
    tr = _row_tile(t, f, 3)

    def body(g_ref, u_ref, o_ref):
        g = g_ref[...]
        o_ref[...] = (g * _sigmoid(g) * u_ref[...]).astype(BF16)

    outs, comm_outs = _call(
        body, grid=(t // tr,), in_specs=[_rows(tr, f), _rows(tr, f)], out_specs=[_rows(tr, f)],
        out_shape=[jax.ShapeDtypeStruct((t, f), BF16)], sem=("parallel",), name="swiglu_fwd", args=(gate, up), comm=comm)
    return outs[0], comm_outs


def _loss_head(h2, g, target):
    t, d = h2.shape
    tr = _row_tile(t, d, 3)

    def body(h_ref, g_ref, t_ref, loss_ref, dh_ref, dhb_ref, dg_ref):
        @pl.when(pl.program_id(0) == 0)
        def _():
            loss_ref[...] = jnp.zeros_like(loss_ref)
            dg_ref[...] = jnp.zeros_like(dg_ref)

        h = h_ref[...]
        r = _rms_scale(h)
        hhat = h * r
        gv = g_ref[...]
        err = hhat * gv - t_ref[...]
        loss_ref[...] += jnp.full(loss_ref.shape, 0.5 * jnp.sum(jnp.mean(err * err, axis=-1)), F32)
        dx, dg_rows = _rms_bwd(hhat, r, gv, err * (1.0 / d))
        dh_ref[...] = dx
        dhb_ref[...] = dx.astype(BF16)
        dg_ref[...] += jnp.sum(dg_rows, axis=0, keepdims=True)

    return pl.pallas_call(
        body, grid=(t // tr,), in_specs=[_rows(tr, d), _full((1, d)), _rows(tr, d)],
        out_specs=[_full((1, LANES)), _rows(tr, d), _rows(tr, d), _full((1, d))],
        out_shape=[jax.ShapeDtypeStruct((1, LANES), F32), jax.ShapeDtypeStruct((t, d), F32),
                   jax.ShapeDtypeStruct((t, d), BF16), jax.ShapeDtypeStruct((1, d), F32)],
        compiler_params=_params(("arbitrary",)), name="loss_head")(h2, g, target)


def _norm_bwd(x, g, dy, resid, name, comm=None):
    t, d = x.shape
    tr = _row_tile(t, d, 5)

    def body(x_ref, g_ref, dy_ref, r_ref, dx_ref, dxb_ref, dg_ref):
        @pl.when(pl.program_id(0) == 0)
        def _():
            dg_ref[...] = jnp.zeros_like(dg_ref)

        xv = x_ref[...]
        r = _rms_scale(xv)
        dx, dg_rows = _rms_bwd(xv * r, r, g_ref[...], dy_ref[...])
        dx = r_ref[...] + dx
        dx_ref[...] = dx
        dxb_ref[...] = dx.astype(BF16)
        dg_ref[...] += jnp.sum(dg_rows, axis=0, keepdims=True)

    outs, comm_outs = _call(
        body, grid=(t // tr,), in_specs=[_rows(tr, d), _full((1, d)), _rows(tr, d), _rows(tr, d)],
        out_specs=[_rows(tr, d), _rows(tr, d), _full((1, d))],
        out_shape=[jax.ShapeDtypeStruct((t, d), F32), jax.ShapeDtypeStruct((t, d), BF16),
                   jax.ShapeDtypeStruct((1, d), F32)],
        sem=("arbitrary",), name=name, args=(x, g, dy, resid), comm=comm)
    return (outs[0], outs[1], outs[2]) if comm is None else (outs[0], outs[1], outs[2], comm_outs)


def _merge_bwd(dmerged, y_attn, y_sgu, z_g, b_gate):
    t, d = y_attn.shape
    tr = _row_tile(t, d, 7)

    def body(dm_ref, ya_ref, ys_ref, g0_ref, g1_ref, b0_ref, b1_ref, dya_ref, dys_ref, dz_ref, db_ref):
        @pl.when(pl.program_id(0) == 0)
        def _():
            db_ref[...] = jnp.zeros_like(db_ref)

        dm = dm_ref[...]
        g0 = _sigmoid(g0_ref[...] + b0_ref[...])
        g1 = _sigmoid(g1_ref[...] + b1_ref[...])
        dya_ref[...] = (dm * g0).astype(BF16)
        dys_ref[...] = (dm * g1).astype(BF16)
        dl0 = dm * ya_ref[...] * (g0 * (1.0 - g0))
        dl1 = dm * ys_ref[...] * (g1 * (1.0 - g1))
        dz_ref[:, 0:d] = dl0.astype(BF16)
        dz_ref[:, d:2 * d] = dl1.astype(BF16)
        db_ref[:, 0:d] += jnp.sum(dl0, axis=0, keepdims=True)
        db_ref[:, d:2 * d] += jnp.sum(dl1, axis=0, keepdims=True)

    bspec0 = pl.BlockSpec((1, d), lambda i: (0, 0))
    bspec1 = pl.BlockSpec((1, d), lambda i: (0, 1))
    return pl.pallas_call(
        body, grid=(t // tr,),
        in_specs=[_rows(tr, d), _rows(tr, d), _rows(tr, d), _rows(tr, d, 0), _rows(tr, d, 1), bspec0, bspec1],
        out_specs=[_rows(tr, d), _rows(tr, d), _rows(tr, 2 * d), _full((1, 2 * d))],
        out_shape=[jax.ShapeDtypeStruct((t, d), BF16), jax.ShapeDtypeStruct((t, d), BF16),
                   jax.ShapeDtypeStruct((t, 2 * d), BF16), jax.ShapeDtypeStruct((1, 2 * d), F32)],
        compiler_params=_params(("arbitrary",)), name="merge_bwd")(dmerged, y_attn, y_sgu, z_g, z_g, b_gate, b_gate)


def _sgu_bwd(z_uv, ds_out, gs, ws, b_col):
    t = z_uv.shape[0]
    sw = z_uv.shape[1] // 2
    groups = sw // SGU_GROUP
    tr = _pick(t, 256, CHUNK)

    def body(u_ref, v_ref, d_ref, gs_ref, ws_ref, b_ref, dz_ref, dws_ref, db_ref, dgs_ref, dvn_ref):
        @pl.when(pl.program_id(0) == 0)
        def _():
            dws_ref[...] = jnp.zeros_like(dws_ref)
            db_ref[...] = jnp.zeros_like(db_ref)
            dgs_ref[...] = jnp.zeros_like(dgs_ref)

        v, dgelu_v = _gelu_and_grad(v_ref[...])
        r = _rms_scale(v)
        vhat = v * r
        gsv = gs_ref[...]
        vn = (vhat * gsv).astype(BF16)
        tri = _tril_mask()
        for g in range(groups):
            wg = jnp.where(tri, ws_ref[g], 0.0).astype(BF16)
            cols = slice(g * SGU_GROUP, (g + 1) * SGU_GROUP)
            for c in range(tr // CHUNK):
                rows = slice(c * CHUNK, (c + 1) * CHUNK)
                vn_cg = vn[rows, cols]
                mixed = jnp.dot(wg, vn_cg, preferred_element_type=F32) + b_ref[g]
                u, dgelu_u = _gelu_and_grad(u_ref[rows, cols])
                dso = d_ref[rows, cols]
                dz_ref[rows, cols] = (dso * mixed * dgelu_u).astype(BF16)
                dmixed = dso * u
                db_ref[g] += jnp.sum(dmixed, axis=1, keepdims=True)
                dmixed_b = dmixed.astype(BF16)
                dws_ref[g] += jnp.where(
                    tri, lax.dot_general(dmixed_b, vn_cg, (((1,), (1,)), ((), ())), preferred_element_type=F32), 0.0)
                dvn_ref[rows, cols] = lax.dot_general(wg, dmixed_b, (((0,), (0,)), ((), ())), preferred_element_type=F32)
        dvn = dvn_ref[...]
        dv, dgs_rows = _rms_bwd(vhat, r, gsv, dvn)
        dz_ref[:, sw:2 * sw] = (dv * dgelu_v).astype(BF16)
        dgs_ref[...] += jnp.sum(dgs_rows, axis=0, keepdims=True)

    return pl.pallas_call(
        body, grid=(t // tr,),
        in_specs=[_rows(tr, sw, 0), _rows(tr, sw, 1), _rows(tr, sw), _full((1, sw)), _full(ws.shape), _full(b_col.shape)],
        out_specs=[_rows(tr, 2 * sw), _full(ws.shape), _full(b_col.shape), _full((1, sw))],
        out_shape=[jax.ShapeDtypeStruct((t, 2 * sw), BF16), jax.ShapeDtypeStruct(ws.shape, F32),
                   jax.ShapeDtypeStruct(b_col.shape, F32), jax.ShapeDtypeStruct((1, sw), F32)],
        scratch_shapes=[pltpu.VMEM((tr, sw), F32)],
        compiler_params=_params(("arbitrary",)), name="sgu_bwd")(z_uv, z_uv, ds_out, gs, ws, b_col)


def _lat_bwd(z_lat, qg, kvg, dqn, dkvn, dkpe_heads, cos, sin, ql, kvl):
    t, w = z_lat.shape
    heads = dkpe_heads.shape[0]
    tr = _row_tile(t, w + heads * LANES, 3)

    def body(z_ref, qg_ref, kvg_ref, dq_ref, dkv_ref, dk_ref, cos_ref, sin_ref, dz_ref, dqg_ref, dkvg_ref):
        @pl.when(pl.program_id(0) == 0)
        def _():
            dqg_ref[...] = jnp.zeros_like(dqg_ref)
            dkvg_ref[...] = jnp.zeros_like(dkvg_ref)

        q = z_ref[:, 0:ql]
        r = _rms_scale(q)
        dx, dg_rows = _rms_bwd(q * r, r, qg_ref[...], dq_ref[...])
        dz_ref[:, 0:ql] = dx.astype(BF16)
        dqg_ref[...] += jnp.sum(dg_rows, axis=0, keepdims=True)
        kv = z_ref[:, ql:ql + kvl]
        r = _rms_scale(kv)
        dx, dg_rows = _rms_bwd(kv * r, r, kvg_ref[...], dkv_ref[...])
        dz_ref[:, ql:ql + kvl] = dx.astype(BF16)
        dkvg_ref[...] += jnp.sum(dg_rows, axis=0, keepdims=True)
        dk = dk_ref[0]
        for h in range(1, heads):
            dk = dk + dk_ref[h]
        dz_ref[:, ql + kvl:ql + kvl + LANES] = _rope_bwd(dk, cos_ref[...], sin_ref[...]).astype(BF16)

    return pl.pallas_call(
        body, grid=(t // tr,),
        in_specs=[_rows(tr, w), _full((1, ql)), _full((1, kvl)), _rows(tr, ql), _rows(tr, kvl),
                  pl.BlockSpec((heads, tr, LANES), lambda i: (0, i, 0)), _rows(tr, LANES), _rows(tr, LANES)],
        out_specs=[_rows(tr, w), _full((1, ql)), _full((1, kvl))],
        out_shape=[jax.ShapeDtypeStruct((t, w), BF16), jax.ShapeDtypeStruct((1, ql), F32),
                   jax.ShapeDtypeStruct((1, kvl), F32)],
        compiler_params=_params(("arbitrary",)), name="lat_bwd")(z_lat, qg, kvg, dqn, dkvn, dkpe_heads, cos, sin)


_NT = (((1,), (1,)), ((), ()))


def _attn_scale():
    return (QK_NOPE + QK_ROPE) ** -0.5


def _heads_per_step(heads, wanted):
    return wanted if heads % wanted == 0 else 1


def _attn_fwd(q_c, kv, kpe, comm=None):
    t = q_c.shape[0]
    heads = q_c.shape[1] // HEAD_PAD
    tq = _pick(t, ATTN_TILE)
    nq = t // tq
    scale = _attn_scale()
    to_log2 = scale * math.log2(math.e)
    tn_dims = (((0,), (0,)), ((), ()))

    hps = _heads_per_step(heads, HEADS_PER_STEP[0])

    def body(q_ref, kv_ref, kpe_ref, o_ref, ob_ref, lse_ref, m_sc, l_sc, acc_sc):
        qi, ki = pl.program_id(1), pl.program_id(2)

        @pl.when(ki == 0)
        def _():
            m_sc[...] = jnp.full_like(m_sc, NEG_BIG)
            l_sc[...] = jnp.zeros_like(l_sc)
            acc_sc[...] = jnp.zeros_like(acc_sc)

        def step(diagonal):
            for u in range(hps):
                lo = u * HEAD_PAD
                kc = jnp.concatenate([kv_ref[:, lo:lo + QK_NOPE], kpe_ref[...]], axis=1)
                st = lax.dot_general(kc, q_ref[:, lo:lo + HEAD_PAD], _NT, preferred_element_type=F32)
                if diagonal:
                    krow = lax.broadcasted_iota(jnp.int32, st.shape, 0)
                    qcol = lax.broadcasted_iota(jnp.int32, st.shape, 1)
                    st = jnp.where(qcol >= krow, st, NEG_BIG)
                m_prev = m_sc[u]
                m_new = jnp.maximum(m_prev, jnp.max(st, axis=0, keepdims=True))
                alpha = jnp.exp2((m_prev - m_new) * to_log2)
                pt = jnp.exp2((st - m_new) * to_log2)
                l_sc[u] = alpha * l_sc[u] + jnp.sum(pt, axis=0, keepdims=True)
                acc_sc[u] = alpha * acc_sc[u] + lax.dot_general(
                    kv_ref[:, lo + QK_NOPE:lo + HEAD_PAD], pt.astype(BF16), tn_dims, preferred_element_type=F32)
                m_sc[u] = m_new

        @pl.when(ki < qi)
        def _():
            step(False)

        @pl.when(ki == qi)
        def _():
            step(True)
            for u in range(hps):
                o = (acc_sc[u] / l_sc[u]).T
                o_ref[:, u * V_HEAD:(u + 1) * V_HEAD] = o
                ob_ref[:, u * V_HEAD:(u + 1) * V_HEAD] = o.astype(BF16)
                lse_ref[u] = m_sc[u] * scale + jnp.log(l_sc[u])

    omap = lambda g, qi, ki: (qi, g)
    outs, comm_outs = _call(
        body, grid=(heads // hps, nq, nq),
        in_specs=[pl.BlockSpec((tq, hps * HEAD_PAD), omap),
                  pl.BlockSpec((tq, hps * HEAD_PAD), lambda g, qi, ki: (jnp.minimum(ki, qi), g)),
                  pl.BlockSpec((tq, LANES), lambda g, qi, ki: (jnp.minimum(ki, qi), 0))],
        out_specs=[pl.BlockSpec((tq, hps * V_HEAD), omap), pl.BlockSpec((tq, hps * V_HEAD), omap),
                   pl.BlockSpec((hps, 1, tq), lambda g, qi, ki: (g, 0, qi))],
        out_shape=[jax.ShapeDtypeStruct((t, heads * V_HEAD), F32), jax.ShapeDtypeStruct((t, heads * V_HEAD), BF16),
                   jax.ShapeDtypeStruct((heads, 1, t), F32)],
        scratch_shapes=[pltpu.VMEM((hps, 1, tq), F32), pltpu.VMEM((hps, 1, tq), F32),
                        pltpu.VMEM((hps, V_HEAD, tq), F32)],
        sem=("parallel", "parallel", "arbitrary"), name="attn_fwd", args=(q_c, kv, kpe), comm=comm)
    return outs[0], outs[1], outs[2], comm_outs


def _attn_bwd(q_c, kv, kpe, o, do, lse_row, cos, sin, comm=None):
    t = q_c.shape[0]
    heads = q_c.shape[1] // HEAD_PAD
    tk = _pick(t, ATTN_TILE)
    nk = t // tk
    scale = _attn_scale()
    tn_dims = (((0,), (0,)), ((), ()))

    hps = _heads_per_step(heads, HEADS_PER_STEP[1])

    def body(q_ref, kv_ref, kpe_ref, do_ref, lse_ref, o_ref, cos_ref, sin_ref, dq_ref, dkv_ref, dkpe_ref,
             dk_sc, dv_sc, delta_sc, dq_sc):
        ki, qi = pl.program_id(1), pl.program_id(2)

        @pl.when(jnp.logical_and(ki == 0, qi == 0))
        def _():
            dq_sc[...] = jnp.zeros_like(dq_sc)

        @pl.when(qi == 0)
        def _():
            dk_sc[...] = jnp.zeros_like(dk_sc)
            dv_sc[...] = jnp.zeros_like(dv_sc)

        @pl.when(ki == 0)
        def _():
            for u in range(hps):
                cols = slice(u * V_HEAD, (u + 1) * V_HEAD)
                delta_sc[qi * hps + u] = jnp.sum((do_ref[:, cols] * o_ref[:, cols]).T, axis=0, keepdims=True)

        def step(diagonal):
            for u in range(hps):
                lo = u * HEAD_PAD
                kc = jnp.concatenate([kv_ref[:, lo:lo + QK_NOPE], kpe_ref[...]], axis=1)
                q = q_ref[:, lo:lo + HEAD_PAD]
                st = lax.dot_general(kc, q, _NT, preferred_element_type=F32) * scale
                pt = jnp.exp(st - lse_ref[u])
                if diagonal:
                    krow = lax.broadcasted_iota(jnp.int32, st.shape, 0)
                    qcol = lax.broadcasted_iota(jnp.int32, st.shape, 1)
                    pt = jnp.where(qcol >= krow, pt, 0.0)
                do_b = do_ref[:, u * V_HEAD:(u + 1) * V_HEAD].astype(BF16)
                dv_sc[u] += jnp.dot(pt.astype(BF16), do_b, preferred_element_type=F32)
                dpt = lax.dot_general(kv_ref[:, lo + QK_NOPE:lo + HEAD_PAD], do_b, _NT, preferred_element_type=F32)
                dst = (pt * (dpt - delta_sc[qi * hps + u]) * scale).astype(BF16)
                dk_sc[u] += jnp.dot(dst, q, preferred_element_type=F32)
                rows = pl.ds(pl.multiple_of(qi * tk, tk), tk)
                dq_sc[rows, lo:lo + HEAD_PAD] += lax.dot_general(dst, kc, tn_dims, preferred_element_type=F32)

        @pl.when(qi > ki)
        def _():
            step(False)

        @pl.when(qi == ki)
        def _():
            step(True)

        @pl.when(qi == nk - 1)
        def _():
            for u in range(hps):
                lo = u * HEAD_PAD
                dkv_ref[:, lo:lo + QK_NOPE] = dk_sc[u, :, 0:QK_NOPE].astype(BF16)
                dkv_ref[:, lo + QK_NOPE:lo + HEAD_PAD] = dv_sc[u].astype(BF16)
                dkpe_ref[u] = dk_sc[u, :, QK_NOPE:QK_NOPE + LANES]

        @pl.when(jnp.logical_and(ki == nk - 1, qi == nk - 1))
        def _():
            cos, sin = cos_ref[...], sin_ref[...]
            for u in range(hps):
                lo = u * HEAD_PAD
                dq_ref[:, lo:lo + QK_NOPE] = dq_sc[:, lo:lo + QK_NOPE].astype(BF16)
                dq_ref[:, lo + QK_NOPE:lo + HEAD_PAD] = _rope_bwd(dq_sc[:, lo + QK_NOPE:lo + HEAD_PAD], cos, sin).astype(BF16)

    qclamp = lambda g, ki, qi: (jnp.maximum(qi, ki), g)
    outs, comm_outs = _call(
        body, grid=(heads // hps, nk, nk),
        in_specs=[pl.BlockSpec((tk, hps * HEAD_PAD), qclamp),
                  pl.BlockSpec((tk, hps * HEAD_PAD), lambda g, ki, qi: (ki, g)),
                  pl.BlockSpec((tk, LANES), lambda g, ki, qi: (ki, 0)),
                  pl.BlockSpec((tk, hps * V_HEAD), qclamp),
                  pl.BlockSpec((hps, 1, tk), lambda g, ki, qi: (g, 0, jnp.maximum(qi, ki))),
                  pl.BlockSpec((tk, hps * V_HEAD), lambda g, ki, qi: (jnp.where(ki == 0, qi, 0), g)),
                  _full((t, LANES)), _full((t, LANES))],
        out_specs=[pl.BlockSpec((t, hps * HEAD_PAD), lambda g, ki, qi: (0, g)),
                   pl.BlockSpec((tk, hps * HEAD_PAD), lambda g, ki, qi: (ki, g)),
                   pl.BlockSpec((hps, tk, LANES), lambda g, ki, qi: (g, ki, 0))],
        out_shape=[jax.ShapeDtypeStruct((t, heads * HEAD_PAD), BF16),
                   jax.ShapeDtypeStruct((t, heads * HEAD_PAD), BF16), jax.ShapeDtypeStruct((heads, t, LANES), F32)],
        scratch_shapes=[pltpu.VMEM((hps, tk, HEAD_PAD), F32), pltpu.VMEM((hps, tk, V_HEAD), F32),
                        pltpu.VMEM((nk * hps, 1, tk), F32), pltpu.VMEM((t, hps * HEAD_PAD), F32)],
        sem=("parallel", "arbitrary", "arbitrary"), name="attn_bwd",
        args=(q_c, kv, kpe, do, lse_row, o, cos, sin), comm=comm)
    return outs[0], outs[1], outs[2], comm_outs


def _local_step(x, pos_col, target, small, shards, opt):
    t = x.shape[0]
    ql, kvl = small["q_norm_g"].shape[1], small["kv_norm_g"].shape[1]
    sw = small["sgu_norm_g"].shape[1]
    heads = (shards["w_uq"].shape[1] * N_DEV) // (QK_NOPE + QK_ROPE)
    big = {}
    big.update(_compute_layout({"w_in": _all_gather([shards["w_in"]])[0]}, ql, kvl, heads, sw))
    half = QK_ROPE // 2
    lane = jnp.arange(LANES)
    inv_freq = ROPE_THETA ** (-jnp.arange(0, QK_ROPE, 2, dtype=F32) / QK_ROPE)
    inv_row = inv_freq[lane % half][None, :]
    sign_row = jnp.where((lane % QK_ROPE) < half, -1.0, 1.0).astype(F32)[None, :]
    cos, sin = _rope_tables(pos_col, inv_row, sign_row)
    ws = small["w_sgu"]
    b_col = small["b_sgu_col"]

    def arrived(names, bufs):
        big.update(_compute_layout(dict(zip(names, bufs)), ql, kvl, heads, sw))

    a = _norm_fwd(x, small["norm_mix_g"], "norm_mix_fwd")
    z_lat, g_qk = _mm(a, big["w_lat_t"], tb=True, name="z_lat",
                      comm=_gather_stage(1, [shards["w_uq"], shards["w_ukv"]]))
    z_uv, (g_sgu, *g_qk) = _mm(a, big["w_uv_t"], tb=True, name="z_uv",
                               comm=_join(_gather_stage(1, [shards["w_o_sgu"]]), _gather_stage(2, g_qk)))
    z_g, (g_attn, g_sgu, *g_qk) = _mm(
        a, big["w_g_t"], tb=True, name="z_g",
        comm=_join(_gather_stage(1, [shards["w_o_attn"]]), _gather_stage(2, [g_sgu]), _gather_stage(3, g_qk)))
    arrived(["w_uq", "w_ukv"], g_qk)
    qn, kvn, kpe = _lat_fwd(z_lat, small["q_norm_g"], small["kv_norm_g"], cos, sin, ql, kvl)
    q_c, (g_attn, g_sgu) = _mm(qn, big["w_uq"], name="q_up_rope", rope=(cos, sin),
                               comm=_join(_gather_stage(2, [g_attn]), _gather_stage(3, [g_sgu])))
    kv, (g_attn, g_out) = _mm(kvn, big["w_ukv"], out_dtype=BF16, name="kv_up",
                              comm=_join(_gather_stage(3, [g_attn]), _gather_stage(1, [shards["w_out"]])))
    arrived(["w_o_sgu", "w_o_attn"], [g_sgu, g_attn])
    attn, attn_b, lse, (w_gate, w_up) = _attn_fwd(
        q_c, kv, kpe, comm=_gather_stage(1, [shards["w_gate_ffn"], shards["w_up_ffn"]]))
    s_out = _sgu_fwd(z_uv, small["sgu_norm_g"], ws, b_col)
    y_sgu, (g_out,) = _mm(s_out, big["w_o_sgu"], name="y_sgu", comm=_gather_stage(2, [g_out]))
    y_attn, (w_gate, g_out) = _mm(attn_b, big["w_o_attn"], name="y_attn",
                                  comm=_join(_gather_stage(2, [w_gate]), _gather_stage(3, [g_out])))
    arrived(["w_out"], [g_out])
    merged, (w_up, w_gate) = _merge_fwd(y_attn, y_sgu, z_g, small["b_gate"],
                                        comm=_join(_gather_stage(2, [w_up]), _gather_stage(3, [w_gate])))
    h1, (w_up,) = _mm(merged, big["w_out"], add=x, name="h1", comm=_gather_stage(3, [w_up]))
    f = _norm_fwd(h1, small["norm_ffn_g"], "norm_ffn_fwd")
    gate, w_down = _mm(f, w_gate, tb=True, slab="n", name="ffn_gate", comm=_gather_stage(1, [shards["w_down_ffn"]]))
    up, w_down = _mm(f, w_up, tb=True, slab="n", name="ffn_up", comm=_gather_stage(2, w_down))
    ffn = gate.shape[2]
    gate, up = gate.reshape(N_DEV * t, ffn), up.reshape(N_DEV * t, ffn)
    act, (w_down,) = _swiglu_fwd(gate, up, comm=_gather_stage(3, w_down))
    act = act.reshape(N_DEV, t, ffn)
    h2 = _mm(act, w_down, slab="k", add=h1, name="h2")
    loss_row, dh2, dh2_b, d_norm_final = _loss_head(h2, small["norm_final_g"], target)

    def pair_sums(names, slabs, bufs):
        return [_pair_sum(g, b, "pair_sum_" + k) for k, g, b in zip(names, slabs, bufs)]

    parts, updates = {}, {}

    def update(names, label, comm=None):
        res, got = _adamw_shards([parts[k] for k in names], [opt[k] for k in names], "adamw_" + label, comm=comm)
        updates.update(zip(names, res))
        return got

    down_slabs = [_mm(act, dh2_b, ta=True, slab="m", out_dtype=BF16, name="dw_down")]
    dgu, bufs = _mm(dh2_b, w_down, tb=True, slab="n", tm=MM_TILE[0] // 2, name="dact_swiglu_bwd",
                    comm=_to_sibling(down_slabs), swiglu=(gate.reshape(N_DEV, t, ffn), up.reshape(N_DEV, t, ffn)))
    dgu = dgu.reshape(2 * N_DEV, t, ffn)
    down_pair = pair_sums(["w_down_ffn"], down_slabs, bufs)
    dw_gu, got = _mm(dgu, f, ta=True, slab="m", out_dtype=BF16, name="dw_gate_up", comm=_to_chips(down_pair))
    parts["w_down_ffn"] = got[0]
    gu_names = ["w_gate_ffn", "w_up_ffn"]
    df, bufs = _mm(dgu, w_gate, slab="k", name="df_gate", comm=_to_sibling([dw_gu, dw_gu], first=[0, N_DEV]))
    gu_pairs = [_pair_sum(dw_gu, b, "pair_sum_" + k, first=s0) for k, b, s0 in zip(gu_names, bufs, [0, N_DEV])]
    half = _pick(gu_pairs[1].shape[1], gu_pairs[1].shape[1] // 2, 2 * SUBLANES)
    df, up_parts = _mm(dgu, w_up, slab="k", a_slab0=N_DEV, add=df, name="df_up",
                       comm=_to_chips(gu_pairs[1:], rows=[("r", 0, half)]))
    quarter = _pick(half, half // 2, 2 * SUBLANES)
    dh1, dh1_b, d_norm_ffn, gate_parts = _norm_bwd(h1, small["norm_ffn_g"], df, dh2, "norm_ffn_bwd",
                                                  comm=_to_chips(gu_pairs[:1], rows=[("r", 0, quarter)]))
    dw_out = _mm(merged, dh1_b, ta=True, out_dtype=BF16, name="dw_out")
    out_slabs = [_slabs_from_rows(dw_out)]
    dmerged, bufs = _mm(dh1_b, big["w_out"], tb=True, name="dmerged", comm=_to_sibling(out_slabs))
    out_pair = pair_sums(["w_out"], out_slabs, bufs)
    dy_attn, dy_sgu, dz_g, d_b_gate = _merge_bwd(dmerged, y_attn, y_sgu, z_g, small["b_gate"])
    dw_o_sgu = _mm(s_out, dy_sgu, ta=True, out_dtype=BF16, name="dw_o_sgu")
    ds_out = _mm(dy_sgu, big["w_o_sgu"], tb=True, name="ds_out")
    dz_uv, d_ws, d_b_col, d_sgu_norm = _sgu_bwd(z_uv, ds_out, small["sgu_norm_g"], ws, b_col)
    dw_o_attn = _mm(attn_b, dy_attn, ta=True, out_dtype=BF16, name="dw_o_attn")
    mix_names = ["w_o_sgu", "w_o_attn"]
    mix_slabs = [_slabs_from_cols(dw_o_sgu), _slabs_from_rows(dw_o_attn)]
    dattn, bufs = _mm(dy_attn, big["w_o_attn"], tb=True, name="dattn", comm=_to_sibling(mix_slabs))
    mix_pairs = pair_sums(mix_names, mix_slabs, bufs)
    rows = gu_pairs[1].shape[1]
    dq_p, dkv, dkpe_heads, got = _attn_bwd(
        q_c, kv, kpe, attn, dattn, lse, cos, sin,
        comm=_join(_to_chips(gu_pairs[:1], rows=[("r", quarter, rows - quarter)], into=gate_parts),
                   _to_chips(gu_pairs[1:], rows=[("r", half, rows - half)], into=up_parts)))
    parts.update(zip(gu_names, got))
    dw_uq = _mm(qn, dq_p, ta=True, out_dtype=BF16, name="dw_uq")
    dw_ukv = _mm(kvn, dkv, ta=True, out_dtype=BF16, name="dw_ukv")
    dqn = _mm(dq_p, big["w_uq"], tb=True, name="dqn")
    dkvn = _mm(dkv, big["w_ukv"], tb=True, name="dkvn")
    dz_lat, d_q_norm, d_kv_norm = _lat_bwd(z_lat, small["q_norm_g"], small["kv_norm_g"], dqn, dkvn, dkpe_heads,
                                           cos, sin, ql, kvl)
    dw_g, got = _mm(dz_g, a, ta=True, out_dtype=BF16, name="dw_g", comm=_to_chips(out_pair))
    parts["w_out"] = got[0]
    dw_uv, got = _mm(dz_uv, a, ta=True, out_dtype=BF16, name="dw_uv", comm=_to_chips(mix_pairs[1:]))
    parts["w_o_attn"] = got[0]
    dw_lat, got = _mm(dz_lat, a, ta=True, out_dtype=BF16, name="dw_lat", comm=_to_chips(mix_pairs[:1]))
    parts["w_o_sgu"] = got[0]
    lat = ql + kvl + QK_ROPE
    dw_uq_cols = dw_uq.reshape(ql, heads, HEAD_PAD)[:, :, :QK_NOPE + QK_ROPE].reshape(ql, heads * (QK_NOPE + QK_ROPE))
    in_names = ["w_uq", "w_ukv", "w_in"]
    in_slabs = [_slabs_from_cols(dw_uq_cols), _slabs_from_cols(dw_ukv),
                _slabs_from_rows(jnp.concatenate([dw_lat[:lat], dw_uv, dw_g], axis=0))]
    da = _mm(dz_lat, big["w_lat_t"], name="da_lat")
    da, bufs = _mm(dz_uv, big["w_uv_t"], add=da, name="da_uv", comm=_to_sibling(in_slabs))
    uq_pair, ukv_pair, in_pair = pair_sums(in_names, in_slabs, bufs)
    cols = in_pair.shape[2]
    first = ((cols * TAIL_SPLIT[0]) // TAIL_SPLIT[1]) // LANES * LANES or cols
    da, in_parts = _mm(dz_g, big["w_g_t"], add=da, name="da_g", comm=_to_chips([in_pair], rows=[("c", 0, first)]))
    grad_x, _, d_norm_mix, got = _norm_bwd(x, small["norm_mix_g"], da, dh1, "norm_mix_bwd",
                                          comm=_to_chips([uq_pair, ukv_pair]))
    parts["w_uq"], parts["w_ukv"] = got
    rest = _to_chips([in_pair], rows=[("c", first, cols - first)], into=in_parts) if first < cols else None
    got = update(["w_gate_ffn", "w_up_ffn", "w_down_ffn"], "ffn", comm=rest)
    parts["w_in"] = got[0] if rest is not None else in_parts[0]
    update(["w_out", "w_o_attn"], "mixer_out")
    for k in ("w_o_sgu", "w_uq", "w_ukv", "w_in"):
        update([k], k)

    gs = {"norm_mix_g": d_norm_mix, "b_gate": d_b_gate, "q_norm_g": d_q_norm, "kv_norm_g": d_kv_norm,
          "sgu_norm_g": d_sgu_norm, "w_sgu": d_ws, "b_sgu_col": d_b_col, "norm_ffn_g": d_norm_ffn,
          "norm_final_g": d_norm_final}
    return loss_row, grad_x, gs, updates


def _my_place():
    return lax.axis_index("x"), lax.axis_index("y"), lax.axis_index("c")


N_CHIPS = N_DEV // 2

_GATHER_SEMS = [[(3,), (3,), ()], [(4,), (4,)], [(1,), (1,)]]


def _halves(shape):
    r, c = shape
    if (c // 2) % LANES == 0:
        return ("c", 0, c // 2), ("c", c // 2, c // 2)
    assert (r // 2) % (2 * SUBLANES) == 0, shape
    return ("r", 0, r // 2), ("r", r // 2, r // 2)


def _gather_copies(stage, ins, outs, sems, strips=None):
    x, y, c = _my_place()
    me, x_nbr, y_nbr, diag = 4 * x + 2 * y + c, 4 * (1 - x) + 2 * y + c, 4 * x + 2 * (1 - y) + c, 4 * (1 - x) + 2 * (1 - y) + c
    sibling = (x, y, 1 - c)
    if strips is None:
        strips = [(w, None) for w in range(len(outs))]

    out = []
    for s, (w, cols) in enumerate(strips):
        def remote(k, src, dst, to):
            return pltpu.make_async_remote_copy(src_ref=src, dst_ref=dst, send_sem=sems[0].at[s, k],
                                                recv_sem=sems[1].at[s, k], device_id=to, device_id_type=MESH)

        if cols is None:
            whole, (first, second) = None, _halves(outs[w].shape[1:])
        else:
            c0, nc = cols
            whole, first, second = ("c", c0, nc), ("c", c0, nc // 2), ("c", c0 + nc // 2, nc // 2)
        if stage == 1:
            src = ins[w] if cols is None else ins[w].at[slice(None), pl.ds(*cols)]
            dst = _window(outs[w], me, whole)
            out.append(pltpu.make_async_copy(src, dst, sems[2].at[s]))
            out += [remote(k, src, dst, to) for k, to in enumerate([sibling, (1 - x, y, c), (x, 1 - y, c)])]
        elif stage == 2:
            out.append(remote(0, _window(ins[w], x_nbr, first), _window(outs[w], x_nbr, first), (x, 1 - y, c)))
            out.append(remote(1, _window(ins[w], y_nbr, second), _window(outs[w], y_nbr, second), (1 - x, y, c)))
            out.append(remote(2, _window(ins[w], x_nbr, whole), _window(outs[w], x_nbr, whole), sibling))
            out.append(remote(3, _window(ins[w], y_nbr, whole), _window(outs[w], y_nbr, whole), sibling))
        else:
            out.append(remote(0, _window(ins[w], diag, whole), _window(outs[w], diag, whole), sibling))
    return out


def _gather_stage(stage, arrays):
    n = len(arrays)

    def start(ins, outs, sems):
        for cp in _gather_copies(stage, ins, outs, sems):
            cp.start()

    def finish(ins, outs, sems):
        for cp in _gather_copies(stage, ins, outs, sems):
            cp.wait()

    shapes = [jax.ShapeDtypeStruct(((N_DEV,) + a.shape) if stage == 1 else a.shape, a.dtype) for a in arrays]
    return _Comm(arrays, shapes, [pltpu.SemaphoreType.DMA((n,) + s) for s in _GATHER_SEMS[stage - 1]], start, finish,
                 aliases=None if stage == 1 else {w: w for w in range(n)})


def _join(*comms):
    ins, shapes, sems, aliases, spans = [], [], [], {}, []
    for cm in comms:
        spans.append((len(ins), len(ins) + len(cm.ins), len(shapes), len(shapes) + len(cm.out_shapes),
                      len(sems), len(sems) + len(cm.sems)))
        aliases.update({len(ins) + i: len(shapes) + o for i, o in cm.aliases.items()})
        ins, shapes, sems = ins + cm.ins, shapes + cm.out_shapes, sems + cm.sems

    def each(half):
        def run(i_refs, o_refs, s_refs):
            for cm, (i0, i1, o0, o1, s0, s1) in zip(comms, spans):
                getattr(cm, half)(i_refs[i0:i1], o_refs[o0:o1], s_refs[s0:s1])
        return run

    return _Comm(ins, shapes, sems, each("start"), each("finish"), aliases)


def _all_gather(shards):
    n = len(shards)
    strips = []
    for w, sh in enumerate(shards):
        cols = sh.shape[1]
        nc = cols // GATHER_STRIPS if cols % (GATHER_STRIPS * 2 * LANES) == 0 else cols
        strips += [(w, (c0, nc)) for c0 in range(0, cols, nc)]
    ns = len(strips)
    n_sems = [len(s) for s in _GATHER_SEMS]

    def body(*refs):
        ins, outs, sems = refs[:n], refs[n:2 * n], refs[2 * n:]
        sem1, sem2, sem3 = (sems[sum(n_sems[:i]):sum(n_sems[:i + 1])] for i in range(3))
        c1 = _gather_copies(1, ins, outs, sem1, strips)
        c2 = _gather_copies(2, outs, outs, sem2, strips)
        c3 = _gather_copies(3, outs, outs, sem3, strips)
        for cp in c1:
            cp.start()
        for s in range(ns):
            for cp in c1[4 * s:4 * s + 4]:
                cp.wait()
            for cp in c2[4 * s:4 * s + 4]:
                cp.start()
        for s in range(ns):
            for cp in c2[4 * s:4 * s + 4]:
                cp.wait()
            c3[s].start()
        for cp in c3:
            cp.wait()

    any_spec = pl.BlockSpec(memory_space=pl.ANY)
    return pl.pallas_call(
        body, in_specs=[any_spec] * n, out_specs=[any_spec] * n,
        out_shape=[jax.ShapeDtypeStruct((N_DEV,) + s.shape, s.dtype) for s in shards],
        scratch_shapes=[pltpu.SemaphoreType.DMA((ns,) + s) for stage in _GATHER_SEMS for s in stage],
        compiler_params=pltpu.CompilerParams(has_side_effects=True), name="all_gather_weights")(*shards)


def _to_sibling(grads, first=None):
    n = len(grads)
    first = first or [0] * n

    def copies(ins, outs, sems):
        x, y, c = _my_place()
        send_sems, recv_sems = sems
        return [pltpu.make_async_remote_copy(
            src_ref=ins[w].at[first[w] + 2 * i + (1 - c)], dst_ref=outs[w].at[i], send_sem=send_sems.at[w, i],
            recv_sem=recv_sems.at[w, i], device_id=(x, y, 1 - c), device_id_type=MESH)
            for w in range(n) for i in range(N_CHIPS)]

    def start(ins, outs, sems):
        for cp in copies(ins, outs, sems):
            cp.start()

    def finish(ins, outs, sems):
        for cp in copies(ins, outs, sems):
            cp.wait()

    return _Comm(grads, [jax.ShapeDtypeStruct((N_CHIPS,) + g.shape[1:], g.dtype) for g in grads],
                 [pltpu.SemaphoreType.DMA((n, N_CHIPS)), pltpu.SemaphoreType.DMA((n, N_CHIPS))], start, finish)


def _window(ref, slab, win):
    if win is None:
        return ref.at[slab]
    if win[0] == "r":
        return ref.at[slab, pl.ds(win[1], win[2])]
    return ref.at[slab, slice(None), pl.ds(win[1], win[2])]


def _to_chips(parts, rows=None, into=None):
    n = len(parts)
    rows = rows or [None] * n

    def copies(ins, outs, sems):
        x, y, c = _my_place()
        send_sems, recv_sems, local_sems = sems
        mine = 2 * x + y
        chips = [(1 - x, y), (x, 1 - y), (1 - x, 1 - y)]
        remote = [pltpu.make_async_remote_copy(
            src_ref=_window(ins[w], 2 * cx + cy, rows[w]), dst_ref=_window(outs[w], mine, rows[w]),
            send_sem=send_sems.at[w, j], recv_sem=recv_sems.at[w, j], device_id=(cx, cy, c), device_id_type=MESH)
            for w in range(n) for j, (cx, cy) in enumerate(chips)]
        local = [pltpu.make_async_copy(_window(ins[w], mine, rows[w]), _window(outs[w], mine, rows[w]),
                                       local_sems.at[w]) for w in range(n)]
        return remote + local

    def start(ins, outs, sems):
        for cp in copies(ins, outs, sems):
            cp.start()

    def finish(ins, outs, sems):
        for cp in copies(ins, outs, sems):
            cp.wait()

    return _Comm(list(parts) + list(into or []), [jax.ShapeDtypeStruct(p.shape, p.dtype) for p in parts],
                 [pltpu.SemaphoreType.DMA((n, N_CHIPS - 1)), pltpu.SemaphoreType.DMA((n, N_CHIPS - 1)),
                  pltpu.SemaphoreType.DMA((n,))], start, finish,
                 aliases={n + w: w for w in range(n)} if into else None)


def _pair_sum(g, buf, name, first=0):
    _, r, c = g.shape
    tr, tc = _shard_tile(r, c, 4 * SHARD_TILE_ELEMS, 1024)
    core = (lax.axis_index("c") + first).astype(jnp.int32).reshape(1)

    def body(core_ref, g_ref, b_ref, o_ref):
        o_ref[...] = (g_ref[...].astype(F32) + b_ref[...].astype(F32)).astype(o_ref.dtype)

    blk = (1, tr, tc)
    return pl.pallas_call(
        body, grid_spec=pltpu.PrefetchScalarGridSpec(
            num_scalar_prefetch=1, grid=(N_CHIPS, r // tr, c // tc),
            in_specs=[pl.BlockSpec(blk, lambda i, j, l, core_ref: (2 * i + core_ref[0], j, l)),
                      pl.BlockSpec(blk, lambda i, j, l, core_ref: (i, j, l))],
            out_specs=pl.BlockSpec(blk, lambda i, j, l, core_ref: (i, j, l))),
        out_shape=jax.ShapeDtypeStruct(buf.shape, buf.dtype),
        compiler_params=_params(("parallel", "parallel", "parallel")), name=name)(core, g, buf)


def _all_reduce_pack(pack):
    r = pack.shape[0]

    def body(x_ref, out_ref, gath_ref, send_sems, recv_sems, local_sem):
        x, y, c = _my_place()
        me, sibling = (x, y, c), (x, y, 1 - c)
        chips = [(1 - x, y), (x, 1 - y), (1 - x, 1 - y)]

        def slab(place):
            return gath_ref.at[4 * place[0] + 2 * place[1] + place[2]]

        def copy(k, place, to, src=None):
            return pltpu.make_async_remote_copy(
                src_ref=slab(place) if src is None else src, dst_ref=slab(place),
                send_sem=send_sems.at[k], recv_sem=recv_sems.at[k], device_id=to, device_id_type=MESH)

        mine = pltpu.make_async_copy(x_ref, slab(me), local_sem)
        mine.start()
        first = [copy(0, me, sibling, src=x_ref)]
        first += [copy(1 + j, me, (*chip, c), src=x_ref) for j, chip in enumerate(chips)]
        for cp in first:
            cp.start()
        passed = [copy(4 + j, (*chip, c), sibling) for j, chip in enumerate(chips)]
        for j, chip in enumerate(chips):
            copy(1 + j, (*chip, c), me).wait_recv()
            passed[j].start()
        copy(0, sibling, me).wait_recv()
        for j, chip in enumerate(chips):
            copy(4 + j, (*chip, 1 - c), me).wait_recv()
        for cp in first + passed:
            cp.wait_send()
        mine.wait()
        acc = gath_ref[0]
        for i in range(1, N_DEV):
            acc = acc + gath_ref[i]
        out_ref[...] = acc

    vmem = pl.BlockSpec(memory_space=pltpu.VMEM)
    return pl.pallas_call(
        body, in_specs=[vmem], out_specs=vmem, out_shape=jax.ShapeDtypeStruct(pack.shape, F32),
        scratch_shapes=[pltpu.VMEM((N_DEV, r, LANES), F32), pltpu.SemaphoreType.DMA((7,)),
                        pltpu.SemaphoreType.DMA((7,)), pltpu.SemaphoreType.DMA],
        compiler_params=pltpu.CompilerParams(vmem_limit_bytes=VMEM_LIMIT), name="all_reduce_small")(pack)


def _adamw_math(w, g, m, v):
    m = ADAM_B1 * m + (1.0 - ADAM_B1) * g
    v = ADAM_B2 * v + (1.0 - ADAM_B2) * (g * g)
    m_hat = m / (1.0 - ADAM_B1 ** ADAM_STEP)
    v_hat = v / (1.0 - ADAM_B2 ** ADAM_STEP)
    delta = -ADAM_LR * (m_hat / (jnp.sqrt(v_hat) + ADAM_EPS) + ADAM_WD * w)
    return delta, m, v


def _adamw_shards(parts, opts, name, comm=None):
    r, c = opts[0][0].shape
    n_parts, k = parts[0].shape[0], len(parts)
    tr, tc = _shard_tile(r, c, SHARD_TILE_ELEMS // k)

    def body(*refs):
        ins, outs = refs[:4 * k], refs[4 * k:]
        for s in range(k):
            p_ref, w_ref, m_ref, v_ref = ins[4 * s:4 * s + 4]
            g_ref, d_ref, nm_ref, nv_ref = outs[4 * s:4 * s + 4]
            g = p_ref[0].astype(F32)
            for i in range(1, n_parts):
                g = g + p_ref[i].astype(F32)
            g_ref[...] = g
            d_ref[...], nm_ref[...], nv_ref[...] = _adamw_math(w_ref[...], g, m_ref[...], v_ref[...])

    spec = pl.BlockSpec((tr, tc), lambda i, j: (i, j))
    args = [a for p, o in zip(parts, opts) for a in (p,) + tuple(o)]
    outs, comm_outs = _call(
        body, grid=(r // tr, c // tc),
        in_specs=[pl.BlockSpec((n_parts, tr, tc), lambda i, j: (0, i, j)), spec, spec, spec] * k,
        out_specs=[spec] * (4 * k), out_shape=[jax.ShapeDtypeStruct((r, c), F32)] * (4 * k),
        sem=("parallel", "parallel"), name=name, args=args, comm=comm)
    return [outs[4 * s:4 * s + 4] for s in range(k)], comm_outs


def _adamw_pack(g, w, m, v):
    r, c = w.shape

    def body(g_ref, w_ref, m_ref, v_ref, d_ref, nm_ref, nv_ref):
        d_ref[...], nm_ref[...], nv_ref[...] = _adamw_math(w_ref[...], g_ref[...], m_ref[...], v_ref[...])

    return pl.pallas_call(
        body, in_specs=[_full((r, c))] * 4, out_specs=[_full((r, c))] * 3, grid=(1,),
        out_shape=[jax.ShapeDtypeStruct((r, c), F32)] * 3,
        compiler_params=_params(("arbitrary",)), name="adamw_small")(g, w, m, v)


def _cols_from_slabs(g):
    return jnp.transpose(g, (1, 0, 2)).reshape(g.shape[1], N_DEV * g.shape[2])


def _slabs_from_cols(w):
    r, c8 = w.shape
    return jnp.transpose(w.reshape(r, N_DEV, c8 // N_DEV), (1, 0, 2))


def _rows_from_slabs(g):
    return g.reshape(N_DEV * g.shape[1], g.shape[2])


def _slabs_from_rows(w):
    return w.reshape(N_DEV, w.shape[0] // N_DEV, w.shape[1])


def _compute_layout(gathered, ql, kvl, heads, sw):
    out = {}
    for k, g in gathered.items():
        if k == "w_in":
            lat = ql + kvl + QK_ROPE
            w_in_t = _rows_from_slabs(g)
            out["w_lat_t"] = jnp.pad(w_in_t[:lat], ((0, LANES - QK_ROPE), (0, 0)))
            out["w_uv_t"] = w_in_t[lat:lat + 2 * sw]
            out["w_g_t"] = w_in_t[lat + 2 * sw:]
        elif k == "w_uq":
            per_head = _cols_from_slabs(g).reshape(ql, heads, QK_NOPE + QK_ROPE)
            pad = HEAD_PAD - QK_NOPE - QK_ROPE
            out["w_uq"] = jnp.pad(per_head, ((0, 0), (0, 0), (0, pad))).reshape(ql, heads * HEAD_PAD)
        elif k in ("w_o_attn", "w_out", "w_down_ffn"):
            out[k.removesuffix("_ffn")] = _rows_from_slabs(g)
        else:
            out[k.removesuffix("_ffn")] = _cols_from_slabs(g)
    return out


_SMALL =["norm_mix_g", "b_gate", "q_norm_g", "kv_norm_g", "sgu_norm_g", "w_sgu", "b_sgu", "norm_ffn_g", "norm_final_g"]
_BIG = ["w_in", "w_uq", "w_ukv", "w_o_attn", "w_o_sgu", "w_out", "w_gate_ffn", "w_up_ffn", "w_down_ffn"]
_TRANSPOSED = ("w_in", "w_gate_ffn", "w_up_ffn")
_ORDER = ["norm_mix_g", "w_in", "b_gate", "q_norm_g", "w_uq", "kv_norm_g", "w_ukv", "w_o_attn", "sgu_norm_g", "w_sgu",
          "b_sgu", "w_o_sgu", "w_out", "norm_ffn_g", "w_gate_ffn", "w_up_ffn", "w_down_ffn", "norm_final_g"]


def _pack_rows(parts):
    rows, sizes = [], []
    for p in parts:
        flat = p.reshape(-1)
        n = flat.shape[0]
        padded = -(-n // (SUBLANES * LANES)) * (SUBLANES * LANES)
        rows.append(jnp.pad(flat, (0, padded - n)).reshape(padded // LANES, LANES))
        sizes.append((n, padded // LANES))
    return jnp.concatenate(rows, axis=0), sizes


def _unpack_rows(pack, sizes, shapes):
    out, r0 = [], 0
    for (n, nr), shp in zip(sizes, shapes):
        out.append(pack[r0:r0 + nr].reshape(-1)[:n].reshape(shp))
        r0 += nr
    return out


def kernel(x, positions, norm_mix_g, w_in, b_gate, q_norm_g, w_uq, kv_norm_g, w_ukv, w_o_attn, sgu_norm_g, w_sgu, b_sgu, w_o_sgu, w_out, norm_ffn_g, w_gate_ffn, w_up_ffn, w_down_ffn, norm_final_g, loss_target, m_norm_mix_g, m_w_in, m_b_gate, m_q_norm_g, m_w_uq, m_kv_norm_g, m_w_ukv, m_w_o_attn, m_sgu_norm_g, m_w_sgu, m_b_sgu, m_w_o_sgu, m_w_out, m_norm_ffn_g, m_w_gate_ffn, m_w_up_ffn, m_w_down_ffn, m_norm_final_g, v_norm_mix_g, v_w_in, v_b_gate, v_q_norm_g, v_w_uq, v_kv_norm_g, v_w_ukv, v_w_o_attn, v_sgu_norm_g, v_w_sgu, v_b_sgu, v_w_o_sgu, v_w_out, v_norm_ffn_g, v_w_gate_ffn, v_w_up_ffn, v_w_down_ffn, v_norm_final_g):
    wts = dict(norm_mix_g=norm_mix_g, w_in=w_in, b_gate=b_gate, q_norm_g=q_norm_g, w_uq=w_uq, kv_norm_g=kv_norm_g,
               w_ukv=w_ukv, w_o_attn=w_o_attn, sgu_norm_g=sgu_norm_g, w_sgu=w_sgu, b_sgu=b_sgu, w_o_sgu=w_o_sgu,
               w_out=w_out, norm_ffn_g=norm_ffn_g, w_gate_ffn=w_gate_ffn, w_up_ffn=w_up_ffn, w_down_ffn=w_down_ffn,
               norm_final_g=norm_final_g)
    mom = dict(norm_mix_g=m_norm_mix_g, w_in=m_w_in, b_gate=m_b_gate, q_norm_g=m_q_norm_g, w_uq=m_w_uq,
               kv_norm_g=m_kv_norm_g, w_ukv=m_w_ukv, w_o_attn=m_w_o_attn, sgu_norm_g=m_sgu_norm_g, w_sgu=m_w_sgu,
               b_sgu=m_b_sgu, w_o_sgu=m_w_o_sgu, w_out=m_w_out, norm_ffn_g=m_norm_ffn_g, w_gate_ffn=m_w_gate_ffn,
               w_up_ffn=m_w_up_ffn, w_down_ffn=m_w_down_ffn, norm_final_g=m_norm_final_g)
    var = dict(norm_mix_g=v_norm_mix_g, w_in=v_w_in, b_gate=v_b_gate, q_norm_g=v_q_norm_g, w_uq=v_w_uq,
               kv_norm_g=v_kv_norm_g, w_ukv=v_w_ukv, w_o_attn=v_w_o_attn, sgu_norm_g=v_sgu_norm_g, w_sgu=v_w_sgu,
               b_sgu=v_b_sgu, w_o_sgu=v_w_o_sgu, w_out=v_w_out, norm_ffn_g=v_norm_ffn_g, w_gate_ffn=v_w_gate_ffn,
               w_up_ffn=v_w_up_ffn, w_down_ffn=v_w_down_ffn, norm_final_g=v_norm_final_g)

    t, d = x.shape[1], x.shape[2]
    ql, kvl = q_norm_g.shape[1], kv_norm_g.shape[1]
    heads = (w_uq.shape[2] * N_DEV) // (QK_NOPE + QK_ROPE)
    sw = sgu_norm_g.shape[1]

    def shard(a, k):
        return a[0].T if k in _TRANSPOSED else a[0]

    def unshard(a, k):
        return (a.T if k in _TRANSPOSED else a).reshape(wts[k].shape)

    opt = {k: (shard(wts[k], k), shard(mom[k], k), shard(var[k], k)) for k in _BIG}
    shards = {k: opt[k][0].astype(BF16) for k in _BIG}
    small = {
        "norm_mix_g": norm_mix_g, "b_gate": b_gate, "q_norm_g": q_norm_g, "kv_norm_g": kv_norm_g,
        "sgu_norm_g": sgu_norm_g, "w_sgu": w_sgu[0], "b_sgu_col": b_sgu[0][:, :, None], "norm_ffn_g": norm_ffn_g,
        "norm_final_g": norm_final_g[None, :],
    }

    loss_row, grad_x, gs, updates = _local_step(x[0], positions.reshape(t, 1), loss_target[0], small, shards, opt)
    grads, deltas, new_m, new_v = {}, {}, {}, {}
    for k in _BIG:
        grads[k], deltas[k], new_m[k], new_v[k] = (unshard(a, k) for a in updates[k])

    small_grads = [gs["norm_mix_g"], gs["b_gate"], gs["q_norm_g"], gs["kv_norm_g"], gs["sgu_norm_g"], gs["w_sgu"],
                   gs["b_sgu_col"], gs["norm_ffn_g"], gs["norm_final_g"]]
    pack, sizes = _pack_rows([loss_row] + small_grads)
    total = _all_reduce_pack(pack)
    shapes = [(1, LANES)] + [wts[k].shape for k in _SMALL]
    unpacked = _unpack_rows(total, sizes, shapes)
    loss = unpacked[0][0, 0]
    for k, g in zip(_SMALL, unpacked[1:]):
        grads[k] = g
    g_pack = total[sizes[0][1]:]
    w_pack, _ = _pack_rows([wts[k] for k in _SMALL])
    m_pack, _ = _pack_rows([mom[k] for k in _SMALL])
    v_pack, _ = _pack_rows([var[k] for k in _SMALL])
    d_pack, nm_pack, nv_pack = _adamw_pack(g_pack, w_pack, m_pack, v_pack)
    small_shapes = [wts[k].shape for k in _SMALL]
    for store, pk in ((deltas, d_pack), (new_m, nm_pack), (new_v, nv_pack)):
        for k, a in zip(_SMALL, _unpack_rows(pk, sizes[1:], small_shapes)):
            store[k] = a

    return (loss, grad_x[None], *[grads[k] for k in _ORDER], *[deltas[k] for k in _ORDER],
            *[new_m[k] for k in _ORDER], *[new_v[k] for k in _ORDER])
```

```python
import functools
import math

import jax
import jax.numpy as jnp
from jax import lax
from jax.experimental import pallas as pl
from jax.experimental.pallas import tpu as pltpu

F32 = jnp.float32
BF16 = jnp.bfloat16

N_DEV = 8
QK_NOPE = 128
QK_ROPE = 64
V_HEAD = 128
HEAD_PAD = 256
ROPE_THETA = 10000.0
CHUNK = 128
SGU_GROUP = 128
RMS_EPS = 1e-6
LANES = 128
SUBLANES = 8

ADAM_LR = 0.001
ADAM_B1 = 0.9
ADAM_B2 = 0.999
ADAM_EPS = 1e-08
ADAM_WD = 0.01
ADAM_STEP = 10

VMEM_LIMIT = 48 * 1024 * 1024
MM_TILE = (2048, 512, 2048)
MM_TILE_TA = (512, 2048)
ATTN_TILE = 512
GATHER_STRIPS = 8
HEADS_PER_STEP = (4, 4)
ROW_KERNEL_BYTES = 24 * 1024 * 1024
SHARD_TILE_ELEMS = 256 * 1024
SLABS_PER_STEP = 2
TAIL_SPLIT = (3, 8)
NEG_BIG = -1e30
MESH = pl.DeviceIdType.MESH


def _pick(n, target, mult=LANES):
    best = None
    d = mult
    while d <= min(n, target):
        if n % d == 0:
            best = d
        d += mult
    return best or n


def _row_tile(t, width, n_blocks, mult=2 * SUBLANES):
    return _pick(t, max(mult, ROW_KERNEL_BYTES // (3 * n_blocks * width * 4)), mult)


def _shard_tile(r, c, elems=SHARD_TILE_ELEMS, max_rows=256):
    tr = _pick(r, max_rows, 2 * SUBLANES)
    return tr, _pick(c, max(LANES, elems // tr))


def _params(sem):
    return pltpu.CompilerParams(dimension_semantics=sem, vmem_limit_bytes=VMEM_LIMIT)


def _full(shape):
    nd = len(shape)
    return pl.BlockSpec(shape, lambda *_: (0,) * nd)


def _rows(tr, w, cb=0):
    return pl.BlockSpec((tr, w), lambda i: (i, cb))


class _Comm:
    def __init__(self, ins, out_shapes, sems, start, finish, aliases=None):
        self.ins, self.out_shapes, self.sems, self.start, self.finish = list(ins), list(out_shapes), list(sems), start, finish
        self.aliases = dict(aliases or {})


def _call(body, *, grid, in_specs, out_specs, out_shape, scratch_shapes=(), sem, name, args, comm=None):
    if comm is None:
        outs = pl.pallas_call(body, grid=grid, in_specs=list(in_specs), out_specs=list(out_specs),
                              out_shape=list(out_shape), scratch_shapes=list(scratch_shapes),
                              compiler_params=_params(sem), name=name)(*args)
        return list(outs), []
    n_in, n_out, n_sc = len(in_specs), len(out_shape), len(scratch_shapes)
    nci, nco = len(comm.ins), len(comm.out_shapes)

    def hosted(*refs):
        ins, refs = refs[:n_in], refs[n_in:]
        cins, refs = refs[:nci], refs[nci:]
        outs, refs = refs[:n_out], refs[n_out:]
        couts, refs = refs[:nco], refs[nco:]
        scratch, csems = refs[:n_sc], refs[n_sc:]
        ids = [pl.program_id(i) for i in range(len(grid))]
        first = functools.reduce(jnp.logical_and, [i == 0 for i in ids])
        last = functools.reduce(jnp.logical_and, [i == g - 1 for i, g in zip(ids, grid)])

        @pl.when(first)
        def _():
            comm.start(cins, couts, csems)

        body(*ins, *outs, *scratch)

        @pl.when(last)
        def _():
            comm.finish(cins, couts, csems)

    any_spec = pl.BlockSpec(memory_space=pl.ANY)
    res = pl.pallas_call(
        hosted, grid=grid, in_specs=list(in_specs) + [any_spec] * nci, out_specs=list(out_specs) + [any_spec] * nco,
        out_shape=list(out_shape) + comm.out_shapes, scratch_shapes=list(scratch_shapes) + comm.sems,
        input_output_aliases={n_in + i: n_out + o for i, o in comm.aliases.items()},
        compiler_params=pltpu.CompilerParams(dimension_semantics=("arbitrary",) * len(grid),
                                             vmem_limit_bytes=VMEM_LIMIT, has_side_effects=True),
        name=name)(*args, *comm.ins)
    return list(res[:n_out]), list(res[n_out:])


def _swiglu_grads(g, u, d):
    s = 1.0 / (1.0 + jnp.exp(-g))
    return (d * u * (s * (1.0 + g * (1.0 - s)))).astype(BF16), (d * (g * s)).astype(BF16)


def _mm(a, b, *, ta=False, tb=False, add=None, out_dtype=F32, tm=None, tn=None, tk=None, name, comm=None,
        slab=None, a_slab0=0, swiglu=None, rope=None):
    sq = None
    if ta:
        tm, tn = tm or MM_TILE_TA[0], tn or MM_TILE_TA[1]
    if slab is None:
        m, k = (a.shape[1], a.shape[0]) if ta else a.shape
        n = b.shape[0] if tb else b.shape[1]
        assert k == (b.shape[1] if tb else b.shape[0]), (a.shape, b.shape, ta, tb)
        tm, tn, tk = _pick(m, tm or MM_TILE[0]), _pick(n, tn or MM_TILE[1]), _pick(k, tk or MM_TILE[2])
        if rope is not None:
            tn = _pick(n, max(tn, HEAD_PAD), HEAD_PAD)
        grid = (m // tm, n // tn, k // tk)
        a_spec = pl.BlockSpec((tk, tm), lambda i, j, kk: (kk, i)) if ta else pl.BlockSpec((tm, tk), lambda i, j, kk: (i, kk))
        b_spec = pl.BlockSpec((tn, tk), lambda i, j, kk: (j, kk)) if tb else pl.BlockSpec((tk, tn), lambda i, j, kk: (kk, j))
        o_spec, o_shape = pl.BlockSpec((tm, tn), lambda i, j, kk: (i, j)), (m, n)
    elif slab == "n":
        m, k = (a.shape[1], a.shape[0]) if ta else a.shape
        s, c = b.shape[0], (b.shape[1] if tb else b.shape[2])
        assert k == (b.shape[2] if tb else b.shape[1]), (a.shape, b.shape, ta, tb)
        tm, tn, tk = _pick(m, tm or MM_TILE[0]), c, _pick(k, tk or MM_TILE[2])
        grid = (m // tm, s, k // tk)
        a_spec = pl.BlockSpec((tk, tm), lambda i, j, kk: (kk, i)) if ta else pl.BlockSpec((tm, tk), lambda i, j, kk: (i, kk))
        b_spec = (pl.BlockSpec((sq, c, tk), lambda i, j, kk: (j, 0, kk)) if tb
                  else pl.BlockSpec((sq, tk, c), lambda i, j, kk: (j, kk, 0)))
        o_spec, o_shape = pl.BlockSpec((sq, tm, c), lambda i, j, kk: (j, i, 0)), (s, m, c)
    elif slab == "m":
        assert ta and not tb
        s, k, c = a.shape
        n = b.shape[1]
        assert k == b.shape[0], (a.shape, b.shape)
        tm, tn, tk = c, _pick(n, tn or MM_TILE[1]), _pick(k, tk or MM_TILE[2])
        grid = (s, n // tn, k // tk)
        a_spec = pl.BlockSpec((sq, tk, c), lambda i, j, kk: (i, kk, 0))
        b_spec = pl.BlockSpec((tk, tn), lambda i, j, kk: (kk, j))
        o_spec, o_shape = pl.BlockSpec((sq, c, tn), lambda i, j, kk: (i, 0, j)), (s, c, n)
    else:
        assert slab == "k" and not ta
        s, c = b.shape[0], (b.shape[2] if tb else b.shape[1])
        m, n = a.shape[1], (b.shape[1] if tb else b.shape[2])
        assert a.shape[2] == c and a.shape[0] >= a_slab0 + s, (a.shape, b.shape, a_slab0)
        tm, tn, tk = _pick(m, tm or MM_TILE[0]), _pick(n, tn or MM_TILE[1]), c
        per_step = SLABS_PER_STEP if (s % SLABS_PER_STEP == 0 and a_slab0 % SLABS_PER_STEP == 0) else 1
        first = a_slab0 // per_step
        grid = (m // tm, n // tn, s // per_step)
        a_spec = pl.BlockSpec((per_step, tm, c), lambda i, j, kk: (kk + first, i, 0))
        b_spec = (pl.BlockSpec((per_step, tn, c), lambda i, j, kk: (kk, j, 0)) if tb
                  else pl.BlockSpec((per_step, c, tn), lambda i, j, kk: (kk, 0, j)))
        o_spec, o_shape = pl.BlockSpec((tm, tn), lambda i, j, kk: (i, j)), (m, n)
    nk = grid[2]
    dims = (((0 if ta else 1,), (1 if tb else 0,)), ((), ()))

    def product(a_ref, b_ref):
        if slab != "k":
            return lax.dot_general(a_ref[...].astype(BF16), b_ref[...].astype(BF16), dims, preferred_element_type=F32)
        r = None
        for u in range(a_ref.shape[0]):
            p = lax.dot_general(a_ref[u].astype(BF16), b_ref[u].astype(BF16), dims, preferred_element_type=F32)
            r = p if r is None else r + p
        return r

    if swiglu is not None:
        assert slab == "n" and add is None
        o_block = pl.BlockSpec((2, sq, tm, c), lambda i, j, kk: (0, j, i, 0))
        o_shape, out_dtype = (2,) + o_shape, BF16

    if rope is not None:
        assert slab is None and add is None and swiglu is None and tn % HEAD_PAD == 0
        out_dtype = BF16
    extras = tuple(swiglu or ()) + tuple(rope or ())

    def body(*refs):
        a_ref, b_ref = refs[:2]
        add_ref = refs[2] if add is not None else None
        x0_ref, x1_ref = refs[2:4] if extras else (None, None)
        o_ref = refs[2 + (add is not None) + len(extras)]
        acc_ref = refs[-1] if nk > 1 else None

        def finish(r):
            if swiglu is not None:
                o_ref[0], o_ref[1] = _swiglu_grads(x0_ref[...], x1_ref[...], r)
                return
            if rope is not None:
                cos, sin = x0_ref[...], x1_ref[...]
                for h in range(tn // HEAD_PAD):
                    lo = h * HEAD_PAD
                    o_ref[:, lo:lo + QK_NOPE] = r[:, lo:lo + QK_NOPE].astype(BF16)
                    o_ref[:, lo + QK_NOPE:lo + HEAD_PAD] = _rope(r[:, lo + QK_NOPE:lo + HEAD_PAD], cos, sin).astype(BF16)
                return
            if add_ref is not None:
                r = r + add_ref[...].astype(F32)
            o_ref[...] = r.astype(o_ref.dtype)

        if nk == 1:
            finish(product(a_ref, b_ref))
            return
        kk = pl.program_id(2)

        @pl.when(kk == 0)
        def _():
            acc_ref[...] = product(a_ref, b_ref)

        if nk > 2:
            @pl.when(jnp.logical_and(kk > 0, kk < nk - 1))
            def _():
                acc_ref[...] += product(a_ref, b_ref)

        @pl.when(kk == nk - 1)
        def _():
            finish(acc_ref[...] + product(a_ref, b_ref))

    in_specs = [a_spec, b_spec] + ([o_spec] if add is not None else []) + ([o_spec] * 2 if swiglu is not None else [])
    if rope is not None:
        in_specs += [pl.BlockSpec((tm, LANES), lambda i, j, kk: (i, 0))] * 2
    args = (a, b) + ((add,) if add is not None else ()) + extras
    if swiglu is not None:
        o_spec = o_block
    outs, comm_outs = _call(
        body, grid=grid, in_specs=in_specs, out_specs=[o_spec],
        out_shape=[jax.ShapeDtypeStruct(o_shape, out_dtype)],
        scratch_shapes=[pltpu.VMEM((tm, tn), F32)] if nk > 1 else [],
        sem=("parallel", "parallel", "arbitrary"), name=name, args=args, comm=comm)
    return outs[0] if comm is None else (outs[0], comm_outs)


def _rms_scale(x):
    return lax.rsqrt(jnp.mean(x * x, axis=-1, keepdims=True) + RMS_EPS)


def _rms_bwd(xhat, r, g, dy):
    t = dy * g
    dx = r * (t - xhat * jnp.mean(t * xhat, axis=-1, keepdims=True))
    return dx, dy * xhat


_GELU_C = math.sqrt(2.0 / math.pi)


def _gelu(x):
    return x * (0.5 * (1.0 + jnp.tanh(_GELU_C * (x + 0.044715 * (x * x * x)))))


def _gelu_and_grad(x):
    t = jnp.tanh(_GELU_C * (x + 0.044715 * (x * x * x)))
    cdf = 0.5 * (1.0 + t)
    return x * cdf, cdf + x * (0.5 * (1.0 - t * t) * (_GELU_C * (1.0 + 3.0 * 0.044715 * (x * x))))


def _sigmoid(x):
    return 1.0 / (1.0 + jnp.exp(-x))


def _swap_halves(x):
    lane = lax.broadcasted_iota(jnp.int32, x.shape, 1)
    first = (lane % QK_ROPE) < (QK_ROPE // 2)
    return jnp.where(first, pltpu.roll(x, LANES - QK_ROPE // 2, 1), pltpu.roll(x, QK_ROPE // 2, 1))


def _rope(x, cos, sin_signed):
    return x * cos + _swap_halves(x) * sin_signed


def _rope_bwd(d, cos, sin_signed):
    return d * cos + _swap_halves(d * sin_signed)


def _rope_tables(pos_col, inv_freq_row, sign_row):
    t = pos_col.shape[0]
    tr = _pick(t, 512, SUBLANES)

    def body(p_ref, f_ref, s_ref, cos_ref, sin_ref):
        ang = p_ref[...].astype(F32) * f_ref[...]
        cos_ref[...] = jnp.cos(ang)
        sin_ref[...] = jnp.sin(ang) * s_ref[...]

    return pl.pallas_call(
        body, grid=(t // tr,), in_specs=[_rows(tr, 1), _full((1, LANES)), _full((1, LANES))],
        out_specs=[_rows(tr, LANES), _rows(tr, LANES)],
        out_shape=[jax.ShapeDtypeStruct((t, LANES), F32)] * 2,
        compiler_params=_params(("parallel",)), name="rope_tables")(pos_col, inv_freq_row, sign_row)


def _norm_fwd(x, g, name):
    t, d = x.shape
    tr = _row_tile(t, d, 2)

    def body(x_ref, g_ref, y_ref):
        xv = x_ref[...]
        y_ref[...] = (xv * _rms_scale(xv) * g_ref[...]).astype(BF16)

    return pl.pallas_call(
        body, grid=(t // tr,), in_specs=[_rows(tr, d), _full((1, d))], out_specs=_rows(tr, d),
        out_shape=jax.ShapeDtypeStruct((t, d), BF16), compiler_params=_params(("parallel",)), name=name)(x, g)


def _lat_fwd(z_lat, qg, kvg, cos, sin, ql, kvl):
    t = z_lat.shape[0]
    tr = _row_tile(t, z_lat.shape[1], 2)

    def body(z_ref, qg_ref, kvg_ref, cos_ref, sin_ref, qn_ref, kvn_ref, kpe_ref):
        q = z_ref[:, 0:ql]
        qn_ref[...] = (q * _rms_scale(q) * qg_ref[...]).astype(BF16)
        kv = z_ref[:, ql:ql + kvl]
        kvn_ref[...] = (kv * _rms_scale(kv) * kvg_ref[...]).astype(BF16)
        kpe_ref[...] = _rope(z_ref[:, ql + kvl:ql + kvl + LANES], cos_ref[...], sin_ref[...]).astype(BF16)

    w = z_lat.shape[1]
    return pl.pallas_call(
        body, grid=(t // tr,),
        in_specs=[_rows(tr, w), _full((1, ql)), _full((1, kvl)), _rows(tr, LANES), _rows(tr, LANES)],
        out_specs=[_rows(tr, ql), _rows(tr, kvl), _rows(tr, LANES)],
        out_shape=[jax.ShapeDtypeStruct((t, ql), BF16), jax.ShapeDtypeStruct((t, kvl), BF16),
                   jax.ShapeDtypeStruct((t, LANES), BF16)],
        compiler_params=_params(("parallel",)), name="lat_fwd")(z_lat, qg, kvg, cos, sin)


def _tril_mask():
    r = lax.broadcasted_iota(jnp.int32, (CHUNK, CHUNK), 0)
    c = lax.broadcasted_iota(jnp.int32, (CHUNK, CHUNK), 1)
    return r >= c


def _sgu_fwd(z_uv, gs, ws, b_col):
    t = z_uv.shape[0]
    sw = z_uv.shape[1] // 2
    groups = sw // SGU_GROUP
    tr = _pick(t, 256, CHUNK)

    def body(u_ref, v_ref, gs_ref, ws_ref, b_ref, o_ref):
        v = _gelu(v_ref[...])
        vn = (v * _rms_scale(v) * gs_ref[...]).astype(BF16)
        tri = _tril_mask()
        for g in range(groups):
            wg = jnp.where(tri, ws_ref[g], 0.0).astype(BF16)
            cols = slice(g * SGU_GROUP, (g + 1) * SGU_GROUP)
            for c in range(tr // CHUNK):
                rows = slice(c * CHUNK, (c + 1) * CHUNK)
                mixed = jnp.dot(wg, vn[rows, cols], preferred_element_type=F32) + b_ref[g]
                o_ref[rows, cols] = (_gelu(u_ref[rows, cols]) * mixed).astype(BF16)

    return pl.pallas_call(
        body, grid=(t // tr,),
        in_specs=[_rows(tr, sw, 0), _rows(tr, sw, 1), _full((1, sw)), _full(ws.shape), _full(b_col.shape)],
        out_specs=_rows(tr, sw), out_shape=jax.ShapeDtypeStruct((t, sw), BF16),
        compiler_params=_params(("parallel",)), name="sgu_fwd")(z_uv, z_uv, gs, ws, b_col)


def _merge_fwd(y_attn, y_sgu, z_g, b_gate, comm=None):
    t, d = y_attn.shape
    tr = _row_tile(t, d, 5)

    def body(ya_ref, ys_ref, g0_ref, g1_ref, b0_ref, b1_ref, o_ref):
        g0 = _sigmoid(g0_ref[...] + b0_ref[...])
        g1 = _sigmoid(g1_ref[...] + b1_ref[...])
        o_ref[...] = (g0 * ya_ref[...] + g1 * ys_ref[...]).astype(BF16)

    bspec0 = pl.BlockSpec((1, d), lambda i: (0, 0))
    bspec1 = pl.BlockSpec((1, d), lambda i: (0, 1))
    outs, comm_outs = _call(
        body, grid=(t // tr,),
        in_specs=[_rows(tr, d), _rows(tr, d), _rows(tr, d, 0), _rows(tr, d, 1), bspec0, bspec1],
        out_specs=[_rows(tr, d)], out_shape=[jax.ShapeDtypeStruct((t, d), BF16)],
        sem=("parallel",), name="merge_fwd", args=(y_attn, y_sgu, z_g, z_g, b_gate, b_gate), comm=comm)
    return outs[0], comm_outs


def _swiglu_fwd(gate, up, comm=None):
    t, f = gate.shape
    tr = _row_tile(t, f, 3)

    def body(g_ref, u_ref, o_ref):
        g = g_ref[...]
        o_ref[...] = (g * _sigmoid(g) * u_ref[...]).astype(BF16)

    outs, comm_outs = _call(
        body, grid=(t // tr,), in_specs=[_rows(tr, f), _rows(tr, f)], out_specs=[_rows(tr, f)],
        out_shape=[jax.ShapeDtypeStruct((t, f), BF16)], sem=("parallel",), name="swiglu_fwd", args=(gate, up), comm=comm)
    return outs[0], comm_outs


def _loss_head(h2, g, target):
    t, d = h2.shape
    tr = _row_tile(t, d, 3)

    def body(h_ref, g_ref, t_ref, loss_ref, dh_ref, dhb_ref, dg_ref):
        @pl.when(pl.program_id(0) == 0)
        def _():
            loss_ref[...] = jnp.zeros_like(loss_ref)
            dg_ref[...] = jnp.zeros_like(dg_ref)

        h = h_ref[...]
        r = _rms_scale(h)
        hhat = h * r
        gv = g_ref[...]
        err = hhat * gv - t_ref[...]
        loss_ref[...] += jnp.full(loss_ref.shape, 0.5 * jnp.sum(jnp.mean(err * err, axis=-1)), F32)
        dx, dg_rows = _rms_bwd(hhat, r, gv, err * (1.0 / d))
        dh_ref[...] = dx
        dhb_ref[...] = dx.astype(BF16)
        dg_ref[...] += jnp.sum(dg_rows, axis=0, keepdims=True)

    return pl.pallas_call(
        body, grid=(t // tr,), in_specs=[_rows(tr, d), _full((1, d)), _rows(tr, d)],
        out_specs=[_full((1, LANES)), _rows(tr, d), _rows(tr, d), _full((1, d))],
        out_shape=[jax.ShapeDtypeStruct((1, LANES), F32), jax.ShapeDtypeStruct((t, d), F32),
                   jax.ShapeDtypeStruct((t, d), BF16), jax.ShapeDtypeStruct((1, d), F32)],
        compiler_params=_params(("arbitrary",)), name="loss_head")(h2, g, target)


def _norm_bwd(x, g, dy, resid, name, comm=None):
    t, d = x.shape
    tr = _row_tile(t, d, 5)

    def body(x_ref, g_ref, dy_ref, r_ref, dx_ref, dxb_ref, dg_ref):
        @pl.when(pl.program_id(0) == 0)
        def _():
            dg_ref[...] = jnp.zeros_like(dg_ref)

        xv = x_ref[...]
        r = _rms_scale(xv)
        dx, dg_rows = _rms_bwd(xv * r, r, g_ref[...], dy_ref[...])
        dx = r_ref[...] + dx
        dx_ref[...] = dx
        dxb_ref[...] = dx.astype(BF16)
        dg_ref[...] += jnp.sum(dg_rows, axis=0, keepdims=True)

    outs, comm_outs = _call(
        body, grid=(t // tr,), in_specs=[_rows(tr, d), _full((1, d)), _rows(tr, d), _rows(tr, d)],
        out_specs=[_rows(tr, d), _rows(tr, d), _full((1, d))],
        out_shape=[jax.ShapeDtypeStruct((t, d), F32), jax.ShapeDtypeStruct((t, d), BF16),
                   jax.ShapeDtypeStruct((1, d), F32)],
        sem=("arbitrary",), name=name, args=(x, g, dy, resid), comm=comm)
    return (outs[0], outs[1], outs[2]) if comm is None else (outs[0], outs[1], outs[2], comm_outs)


def _merge_bwd(dmerged, y_attn, y_sgu, z_g, b_gate):
    t, d = y_attn.shape
    tr = _row_tile(t, d, 7)

    def body(dm_ref, ya_ref, ys_ref, g0_ref, g1_ref, b0_ref, b1_ref, dya_ref, dys_ref, dz_ref, db_ref):
        @pl.when(pl.program_id(0) == 0)
        def _():
            db_ref[...] = jnp.zeros_like(db_ref)

        dm = dm_ref[...]
        g0 = _sigmoid(g0_ref[...] + b0_ref[...])
        g1 = _sigmoid(g1_ref[...] + b1_ref[...])
        dya_ref[...] = (dm * g0).astype(BF16)
        dys_ref[...] = (dm * g1).astype(BF16)
        dl0 = dm * ya_ref[...] * (g0 * (1.0 - g0))
        dl1 = dm * ys_ref[...] * (g1 * (1.0 - g1))
        dz_ref[:, 0:d] = dl0.astype(BF16)
        dz_ref[:, d:2 * d] = dl1.astype(BF16)
        db_ref[:, 0:d] += jnp.sum(dl0, axis=0, keepdims=True)
        db_ref[:, d:2 * d] += jnp.sum(dl1, axis=0, keepdims=True)

    bspec0 = pl.BlockSpec((1, d), lambda i: (0, 0))
    bspec1 = pl.BlockSpec((1, d), lambda i: (0, 1))
    return pl.pallas_call(
        body, grid=(t // tr,),
        in_specs=[_rows(tr, d), _rows(tr, d), _rows(tr, d), _rows(tr, d, 0), _rows(tr, d, 1), bspec0, bspec1],
        out_specs=[_rows(tr, d), _rows(tr, d), _rows(tr, 2 * d), _full((1, 2 * d))],
        out_shape=[jax.ShapeDtypeStruct((t, d), BF16), jax.ShapeDtypeStruct((t, d), BF16),
                   jax.ShapeDtypeStruct((t, 2 * d), BF16), jax.ShapeDtypeStruct((1, 2 * d), F32)],
        compiler_params=_params(("arbitrary",)), name="merge_bwd")(dmerged, y_attn, y_sgu, z_g, z_g, b_gate, b_gate)


def _sgu_bwd(z_uv, ds_out, gs, ws, b_col):
    t = z_uv.shape[0]
    sw = z_uv.shape[1] // 2
    groups = sw // SGU_GROUP
    tr = _pick(t, 256, CHUNK)

    def body(u_ref, v_ref, d_ref, gs_ref, ws_ref, b_ref, dz_ref, dws_ref, db_ref, dgs_ref, dvn_ref):
        @pl.when(pl.program_id(0) == 0)
        def _():
            dws_ref[...] = jnp.zeros_like(dws_ref)
            db_ref[...] = jnp.zeros_like(db_ref)
            dgs_ref[...] = jnp.zeros_like(dgs_ref)

        v, dgelu_v = _gelu_and_grad(v_ref[...])
        r = _rms_scale(v)
        vhat = v * r
        gsv = gs_ref[...]
        vn = (vhat * gsv).astype(BF16)
        tri = _tril_mask()
        for g in range(groups):
            wg = jnp.where(tri, ws_ref[g], 0.0).astype(BF16)
            cols = slice(g * SGU_GROUP, (g + 1) * SGU_GROUP)
            for c in range(tr // CHUNK):
                rows = slice(c * CHUNK, (c + 1) * CHUNK)
                vn_cg = vn[rows, cols]
                mixed = jnp.dot(wg, vn_cg, preferred_element_type=F32) + b_ref[g]
                u, dgelu_u = _gelu_and_grad(u_ref[rows, cols])
                dso = d_ref[rows, cols]
                dz_ref[rows, cols] = (dso * mixed * dgelu_u).astype(BF16)
                dmixed = dso * u
                db_ref[g] += jnp.sum(dmixed, axis=1, keepdims=True)
                dmixed_b = dmixed.astype(BF16)
                dws_ref[g] += jnp.where(
                    tri, lax.dot_general(dmixed_b, vn_cg, (((1,), (1,)), ((), ())), preferred_element_type=F32), 0.0)
                dvn_ref[rows, cols] = lax.dot_general(wg, dmixed_b, (((0,), (0,)), ((), ())), preferred_element_type=F32)
        dvn = dvn_ref[...]
        dv, dgs_rows = _rms_bwd(vhat, r, gsv, dvn)
        dz_ref[:, sw:2 * sw] = (dv * dgelu_v).astype(BF16)
        dgs_ref[...] += jnp.sum(dgs_rows, axis=0, keepdims=True)

    return pl.pallas_call(
        body, grid=(t // tr,),
        in_specs=[_rows(tr, sw, 0), _rows(tr, sw, 1), _rows(tr, sw), _full((1, sw)), _full(ws.shape), _full(b_col.shape)],
        out_specs=[_rows(tr, 2 * sw), _full(ws.shape), _full(b_col.shape), _full((1, sw))],
        out_shape=[jax.ShapeDtypeStruct((t, 2 * sw), BF16), jax.ShapeDtypeStruct(ws.shape, F32),
                   jax.ShapeDtypeStruct(b_col.shape, F32), jax.ShapeDtypeStruct((1, sw), F32)],
        scratch_shapes=[pltpu.VMEM((tr, sw), F32)],
        compiler_params=_params(("arbitrary",)), name="sgu_bwd")(z_uv, z_uv, ds_out, gs, ws, b_col)


def _lat_bwd(z_lat, qg, kvg, dqn, dkvn, dkpe_heads, cos, sin, ql, kvl):
    t, w = z_lat.shape
    heads = dkpe_heads.shape[0]
    tr = _row_tile(t, w + heads * LANES, 3)

    def body(z_ref, qg_ref, kvg_ref, dq_ref, dkv_ref, dk_ref, cos_ref, sin_ref, dz_ref, dqg_ref, dkvg_ref):
        @pl.when(pl.program_id(0) == 0)
        def _():
            dqg_ref[...] = jnp.zeros_like(dqg_ref)
            dkvg_ref[...] = jnp.zeros_like(dkvg_ref)

        q = z_ref[:, 0:ql]
        r = _rms_scale(q)
        dx, dg_rows = _rms_bwd(q * r, r, qg_ref[...], dq_ref[...])
        dz_ref[:, 0:ql] = dx.astype(BF16)
        dqg_ref[...] += jnp.sum(dg_rows, axis=0, keepdims=True)
        kv = z_ref[:, ql:ql + kvl]
        r = _rms_scale(kv)
        dx, dg_rows = _rms_bwd(kv * r, r, kvg_ref[...], dkv_ref[...])
        dz_ref[:, ql:ql + kvl] = dx.astype(BF16)
        dkvg_ref[...] += jnp.sum(dg_rows, axis=0, keepdims=True)
        dk = dk_ref[0]
        for h in range(1, heads):
            dk = dk + dk_ref[h]
        dz_ref[:, ql + kvl:ql + kvl + LANES] = _rope_bwd(dk, cos_ref[...], sin_ref[...]).astype(BF16)

    return pl.pallas_call(
        body, grid=(t // tr,),
        in_specs=[_rows(tr, w), _full((1, ql)), _full((1, kvl)), _rows(tr, ql), _rows(tr, kvl),
                  pl.BlockSpec((heads, tr, LANES), lambda i: (0, i, 0)), _rows(tr, LANES), _rows(tr, LANES)],
        out_specs=[_rows(tr, w), _full((1, ql)), _full((1, kvl))],
        out_shape=[jax.ShapeDtypeStruct((t, w), BF16), jax.ShapeDtypeStruct((1, ql), F32),
                   jax.ShapeDtypeStruct((1, kvl), F32)],
        compiler_params=_params(("arbitrary",)), name="lat_bwd")(z_lat, qg, kvg, dqn, dkvn, dkpe_heads, cos, sin)


_NT = (((1,), (1,)), ((), ()))


def _attn_scale():
    return (QK_NOPE + QK_ROPE) ** -0.5


def _heads_per_step(heads, wanted):
    return wanted if heads % wanted == 0 else 1


def _attn_fwd(q_c, kv, kpe, comm=None):
    t = q_c.shape[0]
    heads = q_c.shape[1] // HEAD_PAD
    tq = _pick(t, ATTN_TILE)
    nq = t // tq
    scale = _attn_scale()
    to_log2 = scale * math.log2(math.e)
    tn_dims = (((0,), (0,)), ((), ()))

    hps = _heads_per_step(heads, HEADS_PER_STEP[0])

    def body(q_ref, kv_ref, kpe_ref, o_ref, ob_ref, lse_ref, m_sc, l_sc, acc_sc):
        qi, ki = pl.program_id(1), pl.program_id(2)

        @pl.when(ki == 0)
        def _():
            m_sc[...] = jnp.full_like(m_sc, NEG_BIG)
            l_sc[...] = jnp.zeros_like(l_sc)
            acc_sc[...] = jnp.zeros_like(acc_sc)

        def step(diagonal):
            for u in range(hps):
                lo = u * HEAD_PAD
                kc = jnp.concatenate([kv_ref[:, lo:lo + QK_NOPE], kpe_ref[...]], axis=1)
                st = lax.dot_general(kc, q_ref[:, lo:lo + HEAD_PAD], _NT, preferred_element_type=F32)
                if diagonal:
                    krow = lax.broadcasted_iota(jnp.int32, st.shape, 0)
                    qcol = lax.broadcasted_iota(jnp.int32, st.shape, 1)
                    st = jnp.where(qcol >= krow, st, NEG_BIG)
                m_prev = m_sc[u]
                m_new = jnp.maximum(m_prev, jnp.max(st, axis=0, keepdims=True))
                alpha = jnp.exp2((m_prev - m_new) * to_log2)
                pt = jnp.exp2((st - m_new) * to_log2)
                l_sc[u] = alpha * l_sc[u] + jnp.sum(pt, axis=0, keepdims=True)
                acc_sc[u] = alpha * acc_sc[u] + lax.dot_general(
                    kv_ref[:, lo + QK_NOPE:lo + HEAD_PAD], pt.astype(BF16), tn_dims, preferred_element_type=F32)
                m_sc[u] = m_new

        @pl.when(ki < qi)
        def _():
            step(False)

        @pl.when(ki == qi)
        def _():
            step(True)
            for u in range(hps):
                o = (acc_sc[u] / l_sc[u]).T
                o_ref[:, u * V_HEAD:(u + 1) * V_HEAD] = o
                ob_ref[:, u * V_HEAD:(u + 1) * V_HEAD] = o.astype(BF16)
                lse_ref[u] = m_sc[u] * scale + jnp.log(l_sc[u])

    omap = lambda g, qi, ki: (qi, g)
    outs, comm_outs = _call(
        body, grid=(heads // hps, nq, nq),
        in_specs=[pl.BlockSpec((tq, hps * HEAD_PAD), omap),
                  pl.BlockSpec((tq, hps * HEAD_PAD), lambda g, qi, ki: (jnp.minimum(ki, qi), g)),
                  pl.BlockSpec((tq, LANES), lambda g, qi, ki: (jnp.minimum(ki, qi), 0))],
        out_specs=[pl.BlockSpec((tq, hps * V_HEAD), omap), pl.BlockSpec((tq, hps * V_HEAD), omap),
                   pl.BlockSpec((hps, 1, tq), lambda g, qi, ki: (g, 0, qi))],
        out_shape=[jax.ShapeDtypeStruct((t, heads * V_HEAD), F32), jax.ShapeDtypeStruct((t, heads * V_HEAD), BF16),
                   jax.ShapeDtypeStruct((heads, 1, t), F32)],
        scratch_shapes=[pltpu.VMEM((hps, 1, tq), F32), pltpu.VMEM((hps, 1, tq), F32),
                        pltpu.VMEM((hps, V_HEAD, tq), F32)],
        sem=("parallel", "parallel", "arbitrary"), name="attn_fwd", args=(q_c, kv, kpe), comm=comm)
    return outs[0], outs[1], outs[2], comm_outs


def _attn_bwd(q_c, kv, kpe, o, do, lse_row, cos, sin, comm=None):
    t = q_c.shape[0]
    heads = q_c.shape[1] // HEAD_PAD
    tk = _pick(t, ATTN_TILE)
    nk = t // tk
    scale = _attn_scale()
    tn_dims = (((0,), (0,)), ((), ()))

    hps = _heads_per_step(heads, HEADS_PER_STEP[1])

    def body(q_ref, kv_ref, kpe_ref, do_ref, lse_ref, o_ref, cos_ref, sin_ref, dq_ref, dkv_ref, dkpe_ref,
             dk_sc, dv_sc, delta_sc, dq_sc):
        ki, qi = pl.program_id(1), pl.program_id(2)

        @pl.when(jnp.logical_and(ki == 0, qi == 0))
        def _():
            dq_sc[...] = jnp.zeros_like(dq_sc)

        @pl.when(qi == 0)
        def _():
            dk_sc[...] = jnp.zeros_like(dk_sc)
            dv_sc[...] = jnp.zeros_like(dv_sc)

        @pl.when(ki == 0)
        def _():
            for u in range(hps):
                cols = slice(u * V_HEAD, (u + 1) * V_HEAD)
                delta_sc[qi * hps + u] = jnp.sum((do_ref[:, cols] * o_ref[:, cols]).T, axis=0, keepdims=True)

        def step(diagonal):
            for u in range(hps):
                lo = u * HEAD_PAD
                kc = jnp.concatenate([kv_ref[:, lo:lo + QK_NOPE], kpe_ref[...]], axis=1)
                q = q_ref[:, lo:lo + HEAD_PAD]
                st = lax.dot_general(kc, q, _NT, preferred_element_type=F32) * scale
                pt = jnp.exp(st - lse_ref[u])
                if diagonal:
                    krow = lax.broadcasted_iota(jnp.int32, st.shape, 0)
                    qcol = lax.broadcasted_iota(jnp.int32, st.shape, 1)
                    pt = jnp.where(qcol >= krow, pt, 0.0)
                do_b = do_ref[:, u * V_HEAD:(u + 1) * V_HEAD].astype(BF16)
                dv_sc[u] += jnp.dot(pt.astype(BF16), do_b, preferred_element_type=F32)
                dpt = lax.dot_general(kv_ref[:, lo + QK_NOPE:lo + HEAD_PAD], do_b, _NT, preferred_element_type=F32)
                dst = (pt * (dpt - delta_sc[qi * hps + u]) * scale).astype(BF16)
                dk_sc[u] += jnp.dot(dst, q, preferred_element_type=F32)
                rows = pl.ds(pl.multiple_of(qi * tk, tk), tk)
                dq_sc[rows, lo:lo + HEAD_PAD] += lax.dot_general(dst, kc, tn_dims, preferred_element_type=F32)

        @pl.when(qi > ki)
        def _():
            step(False)

        @pl.when(qi == ki)
        def _():
            step(True)

        @pl.when(qi == nk - 1)
        def _():
            for u in range(hps):
                lo = u * HEAD_PAD
                dkv_ref[:, lo:lo + QK_NOPE] = dk_sc[u, :, 0:QK_NOPE].astype(BF16)
                dkv_ref[:, lo + QK_NOPE:lo + HEAD_PAD] = dv_sc[u].astype(BF16)
                dkpe_ref[u] = dk_sc[u, :, QK_NOPE:QK_NOPE + LANES]

        @pl.when(jnp.logical_and(ki == nk - 1, qi == nk - 1))
        def _():
            cos, sin = cos_ref[...], sin_ref[...]
            for u in range(hps):
                lo = u * HEAD_PAD
                dq_ref[:, lo:lo + QK_NOPE] = dq_sc[:, lo:lo + QK_NOPE].astype(BF16)
                dq_ref[:, lo + QK_NOPE:lo + HEAD_PAD] = _rope_bwd(dq_sc[:, lo + QK_NOPE:lo + HEAD_PAD], cos, sin).astype(BF16)

    qclamp = lambda g, ki, qi: (jnp.maximum(qi, ki), g)
    outs, comm_outs = _call(
        body, grid=(heads // hps, nk, nk),
        in_specs=[pl.BlockSpec((tk, hps * HEAD_PAD), qclamp),
                  pl.BlockSpec((tk, hps * HEAD_PAD), lambda g, ki, qi: (ki, g)),
                  pl.BlockSpec((tk, LANES), lambda g, ki, qi: (ki, 0)),
                  pl.BlockSpec((tk, hps * V_HEAD), qclamp),
                  pl.BlockSpec((hps, 1, tk), lambda g, ki, qi: (g, 0, jnp.maximum(qi, ki))),
                  pl.BlockSpec((tk, hps * V_HEAD), lambda g, ki, qi: (jnp.where(ki == 0, qi, 0), g)),
                  _full((t, LANES)), _full((t, LANES))],
        out_specs=[pl.BlockSpec((t, hps * HEAD_PAD), lambda g, ki, qi: (0, g)),
                   pl.BlockSpec((tk, hps * HEAD_PAD), lambda g, ki, qi: (ki, g)),
                   pl.BlockSpec((hps, tk, LANES), lambda g, ki, qi: (g, ki, 0))],
        out_shape=[jax.ShapeDtypeStruct((t, heads * HEAD_PAD), BF16),
                   jax.ShapeDtypeStruct((t, heads * HEAD_PAD), BF16), jax.ShapeDtypeStruct((heads, t, LANES), F32)],
        scratch_shapes=[pltpu.VMEM((hps, tk, HEAD_PAD), F32), pltpu.VMEM((hps, tk, V_HEAD), F32),
                        pltpu.VMEM((nk * hps, 1, tk), F32), pltpu.VMEM((t, hps * HEAD_PAD), F32)],
        sem=("parallel", "arbitrary", "arbitrary"), name="attn_bwd",
        args=(q_c, kv, kpe, do, lse_row, o, cos, sin), comm=comm)
    return outs[0], outs[1], outs[2], comm_outs


def _local_step(x, pos_col, target, small, shards, opt):
    t = x.shape[0]
    ql, kvl = small["q_norm_g"].shape[1], small["kv_norm_g"].shape[1]
    sw = small["sgu_norm_g"].shape[1]
    heads = (shards["w_uq"].shape[1] * N_DEV) // (QK_NOPE + QK_ROPE)
    big = {}
    big.update(_compute_layout({"w_in": _all_gather([shards["w_in"]])[0]}, ql, kvl, heads, sw))
    half = QK_ROPE // 2
    lane = jnp.arange(LANES)
    inv_freq = ROPE_THETA ** (-jnp.arange(0, QK_ROPE, 2, dtype=F32) / QK_ROPE)
    inv_row = inv_freq[lane % half][None, :]
    sign_row = jnp.where((lane % QK_ROPE) < half, -1.0, 1.0).astype(F32)[None, :]
    cos, sin = _rope_tables(pos_col, inv_row, sign_row)
    ws = small["w_sgu"]
    b_col = small["b_sgu_col"]

    def arrived(names, bufs):
        big.update(_compute_layout(dict(zip(names, bufs)), ql, kvl, heads, sw))

    a = _norm_fwd(x, small["norm_mix_g"], "norm_mix_fwd")
    z_lat, g_qk = _mm(a, big["w_lat_t"], tb=True, name="z_lat",
                      comm=_gather_stage(1, [shards["w_uq"], shards["w_ukv"]]))
    z_uv, (g_sgu, *g_qk) = _mm(a, big["w_uv_t"], tb=True, name="z_uv",
                               comm=_join(_gather_stage(1, [shards["w_o_sgu"]]), _gather_stage(2, g_qk)))
    z_g, (g_attn, g_sgu, *g_qk) = _mm(
        a, big["w_g_t"], tb=True, name="z_g",
        comm=_join(_gather_stage(1, [shards["w_o_attn"]]), _gather_stage(2, [g_sgu]), _gather_stage(3, g_qk)))
    arrived(["w_uq", "w_ukv"], g_qk)
    qn, kvn, kpe = _lat_fwd(z_lat, small["q_norm_g"], small["kv_norm_g"], cos, sin, ql, kvl)
    q_c, (g_attn, g_sgu) = _mm(qn, big["w_uq"], name="q_up_rope", rope=(cos, sin),
                               comm=_join(_gather_stage(2, [g_attn]), _gather_stage(3, [g_sgu])))
    kv, (g_attn, g_out) = _mm(kvn, big["w_ukv"], out_dtype=BF16, name="kv_up",
                              comm=_join(_gather_stage(3, [g_attn]), _gather_stage(1, [shards["w_out"]])))
    arrived(["w_o_sgu", "w_o_attn"], [g_sgu, g_attn])
    attn, attn_b, lse, (w_gate, w_up) = _attn_fwd(
        q_c, kv, kpe, comm=_gather_stage(1, [shards["w_gate_ffn"], shards["w_up_ffn"]]))
    s_out = _sgu_fwd(z_uv, small["sgu_norm_g"], ws, b_col)
    y_sgu, (g_out,) = _mm(s_out, big["w_o_sgu"], name="y_sgu", comm=_gather_stage(2, [g_out]))
    y_attn, (w_gate, g_out) = _mm(attn_b, big["w_o_attn"], name="y_attn",
                                  comm=_join(_gather_stage(2, [w_gate]), _gather_stage(3, [g_out])))
    arrived(["w_out"], [g_out])
    merged, (w_up, w_gate) = _merge_fwd(y_attn, y_sgu, z_g, small["b_gate"],
                                        comm=_join(_gather_stage(2, [w_up]), _gather_stage(3, [w_gate])))
    h1, (w_up,) = _mm(merged, big["w_out"], add=x, name="h1", comm=_gather_stage(3, [w_up]))
    f = _norm_fwd(h1, small["norm_ffn_g"], "norm_ffn_fwd")
    gate, w_down = _mm(f, w_gate, tb=True, slab="n", name="ffn_gate", comm=_gather_stage(1, [shards["w_down_ffn"]]))
    up, w_down = _mm(f, w_up, tb=True, slab="n", name="ffn_up", comm=_gather_stage(2, w_down))
    ffn = gate.shape[2]
    gate, up = gate.reshape(N_DEV * t, ffn), up.reshape(N_DEV * t, ffn)
    act, (w_down,) = _swiglu_fwd(gate, up, comm=_gather_stage(3, w_down))
    act = act.reshape(N_DEV, t, ffn)
    h2 = _mm(act, w_down, slab="k", add=h1, name="h2")
    loss_row, dh2, dh2_b, d_norm_final = _loss_head(h2, small["norm_final_g"], target)

    def pair_sums(names, slabs, bufs):
        return [_pair_sum(g, b, "pair_sum_" + k) for k, g, b in zip(names, slabs, bufs)]

    parts, updates = {}, {}

    def update(names, label, comm=None):
        res, got = _adamw_shards([parts[k] for k in names], [opt[k] for k in names], "adamw_" + label, comm=comm)
        updates.update(zip(names, res))
        return got

    down_slabs = [_mm(act, dh2_b, ta=True, slab="m", out_dtype=BF16, name="dw_down")]
    dgu, bufs = _mm(dh2_b, w_down, tb=True, slab="n", tm=MM_TILE[0] // 2, name="dact_swiglu_bwd",
                    comm=_to_sibling(down_slabs), swiglu=(gate.reshape(N_DEV, t, ffn), up.reshape(N_DEV, t, ffn)))
    dgu = dgu.reshape(2 * N_DEV, t, ffn)
    down_pair = pair_sums(["w_down_ffn"], down_slabs, bufs)
    dw_gu, got = _mm(dgu, f, ta=True, slab="m", out_dtype=BF16, name="dw_gate_up", comm=_to_chips(down_pair))
    parts["w_down_ffn"] = got[0]
    gu_names = ["w_gate_ffn", "w_up_ffn"]
    df, bufs = _mm(dgu, w_gate, slab="k", name="df_gate", comm=_to_sibling([dw_gu, dw_gu], first=[0, N_DEV]))
    gu_pairs = [_pair_sum(dw_gu, b, "pair_sum_" + k, first=s0) for k, b, s0 in zip(gu_names, bufs, [0, N_DEV])]
    half = _pick(gu_pairs[1].shape[1], gu_pairs[1].shape[1] // 2, 2 * SUBLANES)
    df, up_parts = _mm(dgu, w_up, slab="k", a_slab0=N_DEV, add=df, name="df_up",
                       comm=_to_chips(gu_pairs[1:], rows=[("r", 0, half)]))
    quarter = _pick(half, half // 2, 2 * SUBLANES)
    dh1, dh1_b, d_norm_ffn, gate_parts = _norm_bwd(h1, small["norm_ffn_g"], df, dh2, "norm_ffn_bwd",
                                                  comm=_to_chips(gu_pairs[:1], rows=[("r", 0, quarter)]))
    dw_out = _mm(merged, dh1_b, ta=True, out_dtype=BF16, name="dw_out")
    out_slabs = [_slabs_from_rows(dw_out)]
    dmerged, bufs = _mm(dh1_b, big["w_out"], tb=True, name="dmerged", comm=_to_sibling(out_slabs))
    out_pair = pair_sums(["w_out"], out_slabs, bufs)
    dy_attn, dy_sgu, dz_g, d_b_gate = _merge_bwd(dmerged, y_attn, y_sgu, z_g, small["b_gate"])
    dw_o_sgu = _mm(s_out, dy_sgu, ta=True, out_dtype=BF16, name="dw_o_sgu")
    ds_out = _mm(dy_sgu, big["w_o_sgu"], tb=True, name="ds_out")
    dz_uv, d_ws, d_b_col, d_sgu_norm = _sgu_bwd(z_uv, ds_out, small["sgu_norm_g"], ws, b_col)
    dw_o_attn = _mm(attn_b, dy_attn, ta=True, out_dtype=BF16, name="dw_o_attn")
    mix_names = ["w_o_sgu", "w_o_attn"]
    mix_slabs = [_slabs_from_cols(dw_o_sgu), _slabs_from_rows(dw_o_attn)]
    dattn, bufs = _mm(dy_attn, big["w_o_attn"], tb=True, name="dattn", comm=_to_sibling(mix_slabs))
    mix_pairs = pair_sums(mix_names, mix_slabs, bufs)
    rows = gu_pairs[1].shape[1]
    dq_p, dkv, dkpe_heads, got = _attn_bwd(
        q_c, kv, kpe, attn, dattn, lse, cos, sin,
        comm=_join(_to_chips(gu_pairs[:1], rows=[("r", quarter, rows - quarter)], into=gate_parts),
                   _to_chips(gu_pairs[1:], rows=[("r", half, rows - half)], into=up_parts)))
    parts.update(zip(gu_names, got))
    dw_uq = _mm(qn, dq_p, ta=True, out_dtype=BF16, name="dw_uq")
    dw_ukv = _mm(kvn, dkv, ta=True, out_dtype=BF16, name="dw_ukv")
    dqn = _mm(dq_p, big["w_uq"], tb=True, name="dqn")
    dkvn = _mm(dkv, big["w_ukv"], tb=True, name="dkvn")
    dz_lat, d_q_norm, d_kv_norm = _lat_bwd(z_lat, small["q_norm_g"], small["kv_norm_g"], dqn, dkvn, dkpe_heads,
                                           cos, sin, ql, kvl)
    dw_g, got = _mm(dz_g, a, ta=True, out_dtype=BF16, name="dw_g", comm=_to_chips(out_pair))
    parts["w_out"] = got[0]
    dw_uv, got = _mm(dz_uv, a, ta=True, out_dtype=BF16, name="dw_uv", comm=_to_chips(mix_pairs[1:]))
    parts["w_o_attn"] = got[0]
    dw_lat, got = _mm(dz_lat, a, ta=True, out_dtype=BF16, name="dw_lat", comm=_to_chips(mix_pairs[:1]))
    parts["w_o_sgu"] = got[0]
    lat = ql + kvl + QK_ROPE
    dw_uq_cols = dw_uq.reshape(ql, heads, HEAD_PAD)[:, :, :QK_NOPE + QK_ROPE].reshape(ql, heads * (QK_NOPE + QK_ROPE))
    in_names = ["w_uq", "w_ukv", "w_in"]
    in_slabs = [_slabs_from_cols(dw_uq_cols), _slabs_from_cols(dw_ukv),
                _slabs_from_rows(jnp.concatenate([dw_lat[:lat], dw_uv, dw_g], axis=0))]
    da = _mm(dz_lat, big["w_lat_t"], name="da_lat")
    da, bufs = _mm(dz_uv, big["w_uv_t"], add=da, name="da_uv", comm=_to_sibling(in_slabs))
    uq_pair, ukv_pair, in_pair = pair_sums(in_names, in_slabs, bufs)
    cols = in_pair.shape[2]
    first = ((cols * TAIL_SPLIT[0]) // TAIL_SPLIT[1]) // LANES * LANES or cols
    da, in_parts = _mm(dz_g, big["w_g_t"], add=da, name="da_g", comm=_to_chips([in_pair], rows=[("c", 0, first)]))
    grad_x, _, d_norm_mix, got = _norm_bwd(x, small["norm_mix_g"], da, dh1, "norm_mix_bwd",
                                          comm=_to_chips([uq_pair, ukv_pair]))
    parts["w_uq"], parts["w_ukv"] = got
    rest = _to_chips([in_pair], rows=[("c", first, cols - first)], into=in_parts) if first < cols else None
    got = update(["w_gate_ffn", "w_up_ffn", "w_down_ffn"], "ffn", comm=rest)
    parts["w_in"] = got[0] if rest is not None else in_parts[0]
    update(["w_out", "w_o_attn"], "mixer_out")
    for k in ("w_o_sgu", "w_uq", "w_ukv", "w_in"):
        update([k], k)

    gs = {"norm_mix_g": d_norm_mix, "b_gate": d_b_gate, "q_norm_g": d_q_norm, "kv_norm_g": d_kv_norm,
          "sgu_norm_g": d_sgu_norm, "w_sgu": d_ws, "b_sgu_col": d_b_col, "norm_ffn_g": d_norm_ffn,
          "norm_final_g": d_norm_final}
    return loss_row, grad_x, gs, updates


def _my_place():
    return lax.axis_index("x"), lax.axis_index("y"), lax.axis_index("c")


N_CHIPS = N_DEV // 2

_GATHER_SEMS = [[(3,), (3,), ()], [(4,), (4,)], [(1,), (1,)]]


def _halves(shape):
    r, c = shape
    if (c // 2) % LANES == 0:
        return ("c", 0, c // 2), ("c", c // 2, c // 2)
    assert (r // 2) % (2 * SUBLANES) == 0, shape
    return ("r", 0, r // 2), ("r", r // 2, r // 2)


def _gather_copies(stage, ins, outs, sems, strips=None):
    x, y, c = _my_place()
    me, x_nbr, y_nbr, diag = 4 * x + 2 * y + c, 4 * (1 - x) + 2 * y + c, 4 * x + 2 * (1 - y) + c, 4 * (1 - x) + 2 * (1 - y) + c
    sibling = (x, y, 1 - c)
    if strips is None:
        strips = [(w, None) for w in range(len(outs))]

    out = []
    for s, (w, cols) in enumerate(strips):
        def remote(k, src, dst, to):
            return pltpu.make_async_remote_copy(src_ref=src, dst_ref=dst, send_sem=sems[0].at[s, k],
                                                recv_sem=sems[1].at[s, k], device_id=to, device_id_type=MESH)

        if cols is None:
            whole, (first, second) = None, _halves(outs[w].shape[1:])
        else:
            c0, nc = cols
            whole, first, second = ("c", c0, nc), ("c", c0, nc // 2), ("c", c0 + nc // 2, nc // 2)
        if stage == 1:
            src = ins[w] if cols is None else ins[w].at[slice(None), pl.ds(*cols)]
            dst = _window(outs[w], me, whole)
            out.append(pltpu.make_async_copy(src, dst, sems[2].at[s]))
            out += [remote(k, src, dst, to) for k, to in enumerate([sibling, (1 - x, y, c), (x, 1 - y, c)])]
        elif stage == 2:
            out.append(remote(0, _window(ins[w], x_nbr, first), _window(outs[w], x_nbr, first), (x, 1 - y, c)))
            out.append(remote(1, _window(ins[w], y_nbr, second), _window(outs[w], y_nbr, second), (1 - x, y, c)))
            out.append(remote(2, _window(ins[w], x_nbr, whole), _window(outs[w], x_nbr, whole), sibling))
            out.append(remote(3, _window(ins[w], y_nbr, whole), _window(outs[w], y_nbr, whole), sibling))
        else:
            out.append(remote(0, _window(ins[w], diag, whole), _window(outs[w], diag, whole), sibling))
    return out


def _gather_stage(stage, arrays):
    n = len(arrays)

    def start(ins, outs, sems):
        for cp in _gather_copies(stage, ins, outs, sems):
            cp.start()

    def finish(ins, outs, sems):
        for cp in _gather_copies(stage, ins, outs, sems):
            cp.wait()

    shapes = [jax.ShapeDtypeStruct(((N_DEV,) + a.shape) if stage == 1 else a.shape, a.dtype) for a in arrays]
    return _Comm(arrays, shapes, [pltpu.SemaphoreType.DMA((n,) + s) for s in _GATHER_SEMS[stage - 1]], start, finish,
                 aliases=None if stage == 1 else {w: w for w in range(n)})


def _join(*comms):
    ins, shapes, sems, aliases, spans = [], [], [], {}, []
    for cm in comms:
        spans.append((len(ins), len(ins) + len(cm.ins), len(shapes), len(shapes) + len(cm.out_shapes),
                      len(sems), len(sems) + len(cm.sems)))
        aliases.update({len(ins) + i: len(shapes) + o for i, o in cm.aliases.items()})
        ins, shapes, sems = ins + cm.ins, shapes + cm.out_shapes, sems + cm.sems

    def each(half):
        def run(i_refs, o_refs, s_refs):
            for cm, (i0, i1, o0, o1, s0, s1) in zip(comms, spans):
                getattr(cm, half)(i_refs[i0:i1], o_refs[o0:o1], s_refs[s0:s1])
        return run

    return _Comm(ins, shapes, sems, each("start"), each("finish"), aliases)


def _all_gather(shards):
    n = len(shards)
    strips = []
    for w, sh in enumerate(shards):
        cols = sh.shape[1]
        nc = cols // GATHER_STRIPS if cols % (GATHER_STRIPS * 2 * LANES) == 0 else cols
        strips += [(w, (c0, nc)) for c0 in range(0, cols, nc)]
    ns = len(strips)
    n_sems = [len(s) for s in _GATHER_SEMS]

    def body(*refs):
        ins, outs, sems = refs[:n], refs[n:2 * n], refs[2 * n:]
        sem1, sem2, sem3 = (sems[sum(n_sems[:i]):sum(n_sems[:i + 1])] for i in range(3))
        c1 = _gather_copies(1, ins, outs, sem1, strips)
        c2 = _gather_copies(2, outs, outs, sem2, strips)
        c3 = _gather_copies(3, outs, outs, sem3, strips)
        for cp in c1:
            cp.start()
        for s in range(ns):
            for cp in c1[4 * s:4 * s + 4]:
                cp.wait()
            for cp in c2[4 * s:4 * s + 4]:
                cp.start()
        for s in range(ns):
            for cp in c2[4 * s:4 * s + 4]:
                cp.wait()
            c3[s].start()
        for cp in c3:
            cp.wait()

    any_spec = pl.BlockSpec(memory_space=pl.ANY)
    return pl.pallas_call(
        body, in_specs=[any_spec] * n, out_specs=[any_spec] * n,
        out_shape=[jax.ShapeDtypeStruct((N_DEV,) + s.shape, s.dtype) for s in shards],
        scratch_shapes=[pltpu.SemaphoreType.DMA((ns,) + s) for stage in _GATHER_SEMS for s in stage],
        compiler_params=pltpu.CompilerParams(has_side_effects=True), name="all_gather_weights")(*shards)


def _to_sibling(grads, first=None):
    n = len(grads)
    first = first or [0] * n

    def copies(ins, outs, sems):
        x, y, c = _my_place()
        send_sems, recv_sems = sems
        return [pltpu.make_async_remote_copy(
            src_ref=ins[w].at[first[w] + 2 * i + (1 - c)], dst_ref=outs[w].at[i], send_sem=send_sems.at[w, i],
            recv_sem=recv_sems.at[w, i], device_id=(x, y, 1 - c), device_id_type=MESH)
            for w in range(n) for i in range(N_CHIPS)]

    def start(ins, outs, sems):
        for cp in copies(ins, outs, sems):
            cp.start()

    def finish(ins, outs, sems):
        for cp in copies(ins, outs, sems):
            cp.wait()

    return _Comm(grads, [jax.ShapeDtypeStruct((N_CHIPS,) + g.shape[1:], g.dtype) for g in grads],
                 [pltpu.SemaphoreType.DMA((n, N_CHIPS)), pltpu.SemaphoreType.DMA((n, N_CHIPS))], start, finish)


def _window(ref, slab, win):
    if win is None:
        return ref.at[slab]
    if win[0] == "r":
        return ref.at[slab, pl.ds(win[1], win[2])]
    return ref.at[slab, slice(None), pl.ds(win[1], win[2])]


def _to_chips(parts, rows=None, into=None):
    n = len(parts)
    rows = rows or [None] * n

    def copies(ins, outs, sems):
        x, y, c = _my_place()
        send_sems, recv_sems, local_sems = sems
        mine = 2 * x + y
        chips = [(1 - x, y), (x, 1 - y), (1 - x, 1 - y)]
        remote = [pltpu.make_async_remote_copy(
            src_ref=_window(ins[w], 2 * cx + cy, rows[w]), dst_ref=_window(outs[w], mine, rows[w]),
            send_sem=send_sems.at[w, j], recv_sem=recv_sems.at[w, j], device_id=(cx, cy, c), device_id_type=MESH)
            for w in range(n) for j, (cx, cy) in enumerate(chips)]
        local = [pltpu.make_async_copy(_window(ins[w], mine, rows[w]), _window(outs[w], mine, rows[w]),
                                       local_sems.at[w]) for w in range(n)]
        return remote + local

    def start(ins, outs, sems):
        for cp in copies(ins, outs, sems):
            cp.start()

    def finish(ins, outs, sems):
        for cp in copies(ins, outs, sems):
            cp.wait()

    return _Comm(list(parts) + list(into or []), [jax.ShapeDtypeStruct(p.shape, p.dtype) for p in parts],
                 [pltpu.SemaphoreType.DMA((n, N_CHIPS - 1)), pltpu.SemaphoreType.DMA((n, N_CHIPS - 1)),
                  pltpu.SemaphoreType.DMA((n,))], start, finish,
                 aliases={n + w: w for w in range(n)} if into else None)


def _pair_sum(g, buf, name, first=0):
    _, r, c = g.shape
    tr, tc = _shard_tile(r, c, 4 * SHARD_TILE_ELEMS, 1024)
    core = (lax.axis_index("c") + first).astype(jnp.int32).reshape(1)

    def body(core_ref, g_ref, b_ref, o_ref):
        o_ref[...] = (g_ref[...].astype(F32) + b_ref[...].astype(F32)).astype(o_ref.dtype)

    blk = (1, tr, tc)
    return pl.pallas_call(
        body, grid_spec=pltpu.PrefetchScalarGridSpec(
            num_scalar_prefetch=1, grid=(N_CHIPS, r // tr, c // tc),
            in_specs=[pl.BlockSpec(blk, lambda i, j, l, core_ref: (2 * i + core_ref[0], j, l)),
                      pl.BlockSpec(blk, lambda i, j, l, core_ref: (i, j, l))],
            out_specs=pl.BlockSpec(blk, lambda i, j, l, core_ref: (i, j, l))),
        out_shape=jax.ShapeDtypeStruct(buf.shape, buf.dtype),
        compiler_params=_params(("parallel", "parallel", "parallel")), name=name)(core, g, buf)


def _all_reduce_pack(pack):
    r = pack.shape[0]

    def body(x_ref, out_ref, gath_ref, send_sems, recv_sems, local_sem):
        x, y, c = _my_place()
        me, sibling = (x, y, c), (x, y, 1 - c)
        chips = [(1 - x, y), (x, 1 - y), (1 - x, 1 - y)]

        def slab(place):
            return gath_ref.at[4 * place[0] + 2 * place[1] + place[2]]

        def copy(k, place, to, src=None):
            return pltpu.make_async_remote_copy(
                src_ref=slab(place) if src is None else src, dst_ref=slab(place),
                send_sem=send_sems.at[k], recv_sem=recv_sems.at[k], device_id=to, device_id_type=MESH)

        mine = pltpu.make_async_copy(x_ref, slab(me), local_sem)
        mine.start()
        first = [copy(0, me, sibling, src=x_ref)]
        first += [copy(1 + j, me, (*chip, c), src=x_ref) for j, chip in enumerate(chips)]
        for cp in first:
            cp.start()
        passed = [copy(4 + j, (*chip, c), sibling) for j, chip in enumerate(chips)]
        for j, chip in enumerate(chips):
            copy(1 + j, (*chip, c), me).wait_recv()
            passed[j].start()
        copy(0, sibling, me).wait_recv()
        for j, chip in enumerate(chips):
            copy(4 + j, (*chip, 1 - c), me).wait_recv()
        for cp in first + passed:
            cp.wait_send()
        mine.wait()
        acc = gath_ref[0]
        for i in range(1, N_DEV):
            acc = acc + gath_ref[i]
        out_ref[...] = acc

    vmem = pl.BlockSpec(memory_space=pltpu.VMEM)
    return pl.pallas_call(
        body, in_specs=[vmem], out_specs=vmem, out_shape=jax.ShapeDtypeStruct(pack.shape, F32),
        scratch_shapes=[pltpu.VMEM((N_DEV, r, LANES), F32), pltpu.SemaphoreType.DMA((7,)),
                        pltpu.SemaphoreType.DMA((7,)), pltpu.SemaphoreType.DMA],
        compiler_params=pltpu.CompilerParams(vmem_limit_bytes=VMEM_LIMIT), name="all_reduce_small")(pack)


def _adamw_math(w, g, m, v):
    m = ADAM_B1 * m + (1.0 - ADAM_B1) * g
    v = ADAM_B2 * v + (1.0 - ADAM_B2) * (g * g)
    m_hat = m / (1.0 - ADAM_B1 ** ADAM_STEP)
    v_hat = v / (1.0 - ADAM_B2 ** ADAM_STEP)
    delta = -ADAM_LR * (m_hat / (jnp.sqrt(v_hat) + ADAM_EPS) + ADAM_WD * w)
    return delta, m, v


def _adamw_shards(parts, opts, name, comm=None):
    r, c = opts[0][0].shape
    n_parts, k = parts[0].shape[0], len(parts)
    tr, tc = _shard_tile(r, c, SHARD_TILE_ELEMS // k)

    def body(*refs):
        ins, outs = refs[:4 * k], refs[4 * k:]
        for s in range(k):
            p_ref, w_ref, m_ref, v_ref = ins[4 * s:4 * s + 4]
            g_ref, d_ref, nm_ref, nv_ref = outs[4 * s:4 * s + 4]
            g = p_ref[0].astype(F32)
            for i in range(1, n_parts):
                g = g + p_ref[i].astype(F32)
            g_ref[...] = g
            d_ref[...], nm_ref[...], nv_ref[...] = _adamw_math(w_ref[...], g, m_ref[...], v_ref[...])

    spec = pl.BlockSpec((tr, tc), lambda i, j: (i, j))
    args = [a for p, o in zip(parts, opts) for a in (p,) + tuple(o)]
    outs, comm_outs = _call(
        body, grid=(r // tr, c // tc),
        in_specs=[pl.BlockSpec((n_parts, tr, tc), lambda i, j: (0, i, j)), spec, spec, spec] * k,
        out_specs=[spec] * (4 * k), out_shape=[jax.ShapeDtypeStruct((r, c), F32)] * (4 * k),
        sem=("parallel", "parallel"), name=name, args=args, comm=comm)
    return [outs[4 * s:4 * s + 4] for s in range(k)], comm_outs


def _adamw_pack(g, w, m, v):
    r, c = w.shape

    def body(g_ref, w_ref, m_ref, v_ref, d_ref, nm_ref, nv_ref):
        d_ref[...], nm_ref[...], nv_ref[...] = _adamw_math(w_ref[...], g_ref[...], m_ref[...], v_ref[...])

    return pl.pallas_call(
        body, in_specs=[_full((r, c))] * 4, out_specs=[_full((r, c))] * 3, grid=(1,),
        out_shape=[jax.ShapeDtypeStruct((r, c), F32)] * 3,
        compiler_params=_params(("arbitrary",)), name="adamw_small")(g, w, m, v)


def _cols_from_slabs(g):
    return jnp.transpose(g, (1, 0, 2)).reshape(g.shape[1], N_DEV * g.shape[2])


def _slabs_from_cols(w):
    r, c8 = w.shape
    return jnp.transpose(w.reshape(r, N_DEV, c8 // N_DEV), (1, 0, 2))


def _rows_from_slabs(g):
    return g.reshape(N_DEV * g.shape[1], g.shape[2])


def _slabs_from_rows(w):
    return w.reshape(N_DEV, w.shape[0] // N_DEV, w.shape[1])


def _compute_layout(gathered, ql, kvl, heads, sw):
    out = {}
    for k, g in gathered.items():
        if k == "w_in":
            lat = ql + kvl + QK_ROPE
            w_in_t = _rows_from_slabs(g)
            out["w_lat_t"] = jnp.pad(w_in_t[:lat], ((0, LANES - QK_ROPE), (0, 0)))
            out["w_uv_t"] = w_in_t[lat:lat + 2 * sw]
            out["w_g_t"] = w_in_t[lat + 2 * sw:]
        elif k == "w_uq":
            per_head = _cols_from_slabs(g).reshape(ql, heads, QK_NOPE + QK_ROPE)
            pad = HEAD_PAD - QK_NOPE - QK_ROPE
            out["w_uq"] = jnp.pad(per_head, ((0, 0), (0, 0), (0, pad))).reshape(ql, heads * HEAD_PAD)
        elif k in ("w_o_attn", "w_out", "w_down_ffn"):
            out[k.removesuffix("_ffn")] = _rows_from_slabs(g)
        else:
            out[k.removesuffix("_ffn")] = _cols_from_slabs(g)
    return out


_SMALL =["norm_mix_g", "b_gate", "q_norm_g", "kv_norm_g", "sgu_norm_g", "w_sgu", "b_sgu", "norm_ffn_g", "norm_final_g"]
_BIG = ["w_in", "w_uq", "w_ukv", "w_o_attn", "w_o_sgu", "w_out", "w_gate_ffn", "w_up_ffn", "w_down_ffn"]
_TRANSPOSED = ("w_in", "w_gate_ffn", "w_up_ffn")
_ORDER = ["norm_mix_g", "w_in", "b_gate", "q_norm_g", "w_uq", "kv_norm_g", "w_ukv", "w_o_attn", "sgu_norm_g", "w_sgu",
          "b_sgu", "w_o_sgu", "w_out", "norm_ffn_g", "w_gate_ffn", "w_up_ffn", "w_down_ffn", "norm_final_g"]


def _pack_rows(parts):
    rows, sizes = [], []
    for p in parts:
        flat = p.reshape(-1)
        n = flat.shape[0]
        padded = -(-n // (SUBLANES * LANES)) * (SUBLANES * LANES)
        rows.append(jnp.pad(flat, (0, padded - n)).reshape(padded // LANES, LANES))
        sizes.append((n, padded // LANES))
    return jnp.concatenate(rows, axis=0), sizes


def _unpack_rows(pack, sizes, shapes):
    out, r0 = [], 0
    for (n, nr), shp in zip(sizes, shapes):
        out.append(pack[r0:r0 + nr].reshape(-1)[:n].reshape(shp))
        r0 += nr
    return out


def kernel(x, positions, norm_mix_g, w_in, b_gate, q_norm_g, w_uq, kv_norm_g, w_ukv, w_o_attn, sgu_norm_g, w_sgu, b_sgu, w_o_sgu, w_out, norm_ffn_g, w_gate_ffn, w_up_ffn, w_down_ffn, norm_final_g, loss_target, m_norm_mix_g, m_w_in, m_b_gate, m_q_norm_g, m_w_uq, m_kv_norm_g, m_w_ukv, m_w_o_attn, m_sgu_norm_g, m_w_sgu, m_b_sgu, m_w_o_sgu, m_w_out, m_norm_ffn_g, m_w_gate_ffn, m_w_up_ffn, m_w_down_ffn, m_norm_final_g, v_norm_mix_g, v_w_in, v_b_gate, v_q_norm_g, v_w_uq, v_kv_norm_g, v_w_ukv, v_w_o_attn, v_sgu_norm_g, v_w_sgu, v_b_sgu, v_w_o_sgu, v_w_out, v_norm_ffn_g, v_w_gate_ffn, v_w_up_ffn, v_w_down_ffn, v_norm_final_g):
    wts = dict(norm_mix_g=norm_mix_g, w_in=w_in, b_gate=b_gate, q_norm_g=q_norm_g, w_uq=w_uq, kv_norm_g=kv_norm_g,
               w_ukv=w_ukv, w_o_attn=w_o_attn, sgu_norm_g=sgu_norm_g, w_sgu=w_sgu, b_sgu=b_sgu, w_o_sgu=w_o_sgu,
               w_out=w_out, norm_ffn_g=norm_ffn_g, w_gate_ffn=w_gate_ffn, w_up_ffn=w_up_ffn, w_down_ffn=w_down_ffn,
               norm_final_g=norm_final_g)
    mom = dict(norm_mix_g=m_norm_mix_g, w_in=m_w_in, b_gate=m_b_gate, q_norm_g=m_q_norm_g, w_uq=m_w_uq,
               kv_norm_g=m_kv_norm_g, w_ukv=m_w_ukv, w_o_attn=m_w_o_attn, sgu_norm_g=m_sgu_norm_g, w_sgu=m_w_sgu,
               b_sgu=m_b_sgu, w_o_sgu=m_w_o_sgu, w_out=m_w_out, norm_ffn_g=m_norm_ffn_g, w_gate_ffn=m_w_gate_ffn,
               w_up_ffn=m_w_up_ffn, w_down_ffn=m_w_down_ffn, norm_final_g=m_norm_final_g)
    var = dict(norm_mix_g=v_norm_mix_g, w_in=v_w_in, b_gate=v_b_gate, q_norm_g=v_q_norm_g, w_uq=v_w_uq,
               kv_norm_g=v_kv_norm_g, w_ukv=v_w_ukv, w_o_attn=v_w_o_attn, sgu_norm_g=v_sgu_norm_g, w_sgu=v_w_sgu,
               b_sgu=v_b_sgu, w_o_sgu=v_w_o_sgu, w_out=v_w_out, norm_ffn_g=v_norm_ffn_g, w_gate_ffn=v_w_gate_ffn,
               w_up_ffn=v_w_up_ffn, w_down_ffn=v_w_down_ffn, norm_final_g=v_norm_final_g)

    t, d = x.shape[1], x.shape[2]
    ql, kvl = q_norm_g.shape[1], kv_norm_g.shape[1]
    heads = (w_uq.shape[2] * N_DEV) // (QK_NOPE + QK_ROPE)
    sw = sgu_norm_g.shape[1]

    def shard(a, k):
        return a[0].T if k in _TRANSPOSED else a[0]

    def unshard(a, k):
        return (a.T if k in _TRANSPOSED else a).reshape(wts[k].shape)

    opt = {k: (shard(wts[k], k), shard(mom[k], k), shard(var[k], k)) for k in _BIG}
    shards = {k: opt[k][0].astype(BF16) for k in _BIG}
    small = {
        "norm_mix_g": norm_mix_g, "b_gate": b_gate, "q_norm_g": q_norm_g, "kv_norm_g": kv_norm_g,
        "sgu_norm_g": sgu_norm_g, "w_sgu": w_sgu[0], "b_sgu_col": b_sgu[0][:, :, None], "norm_ffn_g": norm_ffn_g,
        "norm_final_g": norm_final_g[None, :],
    }

    loss_row, grad_x, gs, updates = _local_step(x[0], positions.reshape(t, 1), loss_target[0], small, shards, opt)
    grads, deltas, new_m, new_v = {}, {}, {}, {}
    for k in _BIG:
        grads[k], deltas[k], new_m[k], new_v[k] = (unshard(a, k) for a in updates[k])

    small_grads = [gs["norm_mix_g"], gs["b_gate"], gs["q_norm_g"], gs["kv_norm_g"], gs["sgu_norm_g"], gs["w_sgu"],
                   gs["b_sgu_col"], gs["norm_ffn_g"], gs["norm_final_g"]]
    pack, sizes = _pack_rows([loss_row] + small_grads)
    total = _all_reduce_pack(pack)
    shapes = [(1, LANES)] + [wts[k].shape for k in _SMALL]
    unpacked = _unpack_rows(total, sizes, shapes)
    loss = unpacked[0][0, 0]
    for k, g in zip(_SMALL, unpacked[1:]):
        grads[k] = g
    g_pack = total[sizes[0][1]:]
    w_pack, _ = _pack_rows([wts[k] for k in _SMALL])
    m_pack, _ = _pack_rows([mom[k] for k in _SMALL])
    v_pack, _ = _pack_rows([var[k] for k in _SMALL])
    d_pack, nm_pack, nv_pack = _adamw_pack(g_pack, w_pack, m_pack, v_pack)
    small_shapes = [wts[k].shape for k in _SMALL]
    for store, pk in ((deltas, d_pack), (new_m, nm_pack), (new_v, nv_pack)):
        for k, a in zip(_SMALL, _unpack_rows(pk, sizes[1:], small_shapes)):
            store[k] = a

    return (loss, grad_x[None], *[grads[k] for k in _ORDER], *[deltas[k] for k in _ORDER],
            *[new_m[k] for k in _ORDER], *[new_v[k] for k in _ORDER])
```

```python
import functools
import math

import jax
import jax.numpy as jnp
from jax import lax
from jax.experimental import pallas as pl
from jax.experimental.pallas import tpu as pltpu

F32 = jnp.float32
BF16 = jnp.bfloat16

N_DEV = 8
QK_NOPE = 128
QK_ROPE = 64
V_HEAD = 128
HEAD_PAD = 256
ROPE_THETA = 10000.0
CHUNK = 128
SGU_GROUP = 128
RMS_EPS = 1e-6
LANES = 128
SUBLANES = 8

ADAM_LR = 0.001
ADAM_B1 = 0.9
ADAM_B2 = 0.999
ADAM_EPS = 1e-08
ADAM_WD = 0.01
ADAM_STEP = 10

VMEM_LIMIT = 48 * 1024 * 1024
MM_TILE = (2048, 512, 2048)
MM_TILE_TA = (512, 2048)
ATTN_TILE = 512
GATHER_STRIPS = 8
HEADS_PER_STEP = (4, 4)
ROW_KERNEL_BYTES = 24 * 1024 * 1024
SHARD_TILE_ELEMS = 256 * 1024
SLABS_PER_STEP = 2
TAIL_SPLIT = (3, 8)
NEG_BIG = -1e30
MESH = pl.DeviceIdType.MESH


def _pick(n, target, mult=LANES):
    best = None
    d = mult
    while d <= min(n, target):
        if n % d == 0:
            best = d
        d += mult
    return best or n


def _row_tile(t, width, n_blocks, mult=2 * SUBLANES):
    return _pick(t, max(mult, ROW_KERNEL_BYTES // (3 * n_blocks * width * 4)), mult)


def _shard_tile(r, c, elems=SHARD_TILE_ELEMS, max_rows=256):
    tr = _pick(r, max_rows, 2 * SUBLANES)
    return tr, _pick(c, max(LANES, elems // tr))


def _params(sem):
    return pltpu.CompilerParams(dimension_semantics=sem, vmem_limit_bytes=VMEM_LIMIT)


def _full(shape):
    nd = len(shape)
    return pl.BlockSpec(shape, lambda *_: (0,) * nd)


def _rows(tr, w, cb=0):
    return pl.BlockSpec((tr, w), lambda i: (i, cb))


class _Comm:
    def __init__(self, ins, out_shapes, sems, start, finish, aliases=None):
        self.ins, self.out_shapes, self.sems, self.start, self.finish = list(ins), list(out_shapes), list(sems), start, finish
        self.aliases = dict(aliases or {})


def _call(body, *, grid, in_specs, out_specs, out_shape, scratch_shapes=(), sem, name, args, comm=None):
    if comm is None:
        outs = pl.pallas_call(body, grid=grid, in_specs=list(in_specs), out_specs=list(out_specs),
                              out_shape=list(out_shape), scratch_shapes=list(scratch_shapes),
                              compiler_params=_params(sem), name=name)(*args)
        return list(outs), []
    n_in, n_out, n_sc = len(in_specs), len(out_shape), len(scratch_shapes)
    nci, nco = len(comm.ins), len(comm.out_shapes)

    def hosted(*refs):
        ins, refs = refs[:n_in], refs[n_in:]
        cins, refs = refs[:nci], refs[nci:]
        outs, refs = refs[:n_out], refs[n_out:]
        couts, refs = refs[:nco], refs[nco:]
        scratch, csems = refs[:n_sc], refs[n_sc:]
        ids = [pl.program_id(i) for i in range(len(grid))]
        first = functools.reduce(jnp.logical_and, [i == 0 for i in ids])
        last = functools.reduce(jnp.logical_and, [i == g - 1 for i, g in zip(ids, grid)])

        @pl.when(first)
        def _():
            comm.start(cins, couts, csems)

        body(*ins, *outs, *scratch)

        @pl.when(last)
        def _():
            comm.finish(cins, couts, csems)

    any_spec = pl.BlockSpec(memory_space=pl.ANY)
    res = pl.pallas_call(
        hosted, grid=grid, in_specs=list(in_specs) + [any_spec] * nci, out_specs=list(out_specs) + [any_spec] * nco,
        out_shape=list(out_shape) + comm.out_shapes, scratch_shapes=list(scratch_shapes) + comm.sems,
        input_output_aliases={n_in + i: n_out + o for i, o in comm.aliases.items()},
        compiler_params=pltpu.CompilerParams(dimension_semantics=("arbitrary",) * len(grid),
                                             vmem_limit_bytes=VMEM_LIMIT, has_side_effects=True),
        name=name)(*args, *comm.ins)
    return list(res[:n_out]), list(res[n_out:])


def _swiglu_grads(g, u, d):
    s = 1.0 / (1.0 + jnp.exp(-g))
    return (d * u * (s * (1.0 + g * (1.0 - s)))).astype(BF16), (d * (g * s)).astype(BF16)


def _mm(a, b, *, ta=False, tb=False, add=None, out_dtype=F32, tm=None, tn=None, tk=None, name, comm=None,
        slab=None, a_slab0=0, swiglu=None, rope=None):
    sq = None
    if ta:
        tm, tn = tm or MM_TILE_TA[0], tn or MM_TILE_TA[1]
    if slab is None:
        m, k = (a.shape[1], a.shape[0]) if ta else a.shape
        n = b.shape[0] if tb else b.shape[1]
        assert k == (b.shape[1] if tb else b.shape[0]), (a.shape, b.shape, ta, tb)
        tm, tn, tk = _pick(m, tm or MM_TILE[0]), _pick(n, tn or MM_TILE[1]), _pick(k, tk or MM_TILE[2])
        if rope is not None:
            tn = _pick(n, max(tn, HEAD_PAD), HEAD_PAD)
        grid = (m // tm, n // tn, k // tk)
        a_spec = pl.BlockSpec((tk, tm), lambda i, j, kk: (kk, i)) if ta else pl.BlockSpec((tm, tk), lambda i, j, kk: (i, kk))
        b_spec = pl.BlockSpec((tn, tk), lambda i, j, kk: (j, kk)) if tb else pl.BlockSpec((tk, tn), lambda i, j, kk: (kk, j))
        o_spec, o_shape = pl.BlockSpec((tm, tn), lambda i, j, kk: (i, j)), (m, n)
    elif slab == "n":
        m, k = (a.shape[1], a.shape[0]) if ta else a.shape
        s, c = b.shape[0], (b.shape[1] if tb else b.shape[2])
        assert k == (b.shape[2] if tb else b.shape[1]), (a.shape, b.shape, ta, tb)
        tm, tn, tk = _pick(m, tm or MM_TILE[0]), c, _pick(k, tk or MM_TILE[2])
        grid = (m // tm, s, k // tk)
        a_spec = pl.BlockSpec((tk, tm), lambda i, j, kk: (kk, i)) if ta else pl.BlockSpec((tm, tk), lambda i, j, kk: (i, kk))
        b_spec = (pl.BlockSpec((sq, c, tk), lambda i, j, kk: (j, 0, kk)) if tb
                  else pl.BlockSpec((sq, tk, c), lambda i, j, kk: (j, kk, 0)))
        o_spec, o_shape = pl.BlockSpec((sq, tm, c), lambda i, j, kk: (j, i, 0)), (s, m, c)
    elif slab == "m":
        assert ta and not tb
        s, k, c = a.shape
        n = b.shape[1]
        assert k == b.shape[0], (a.shape, b.shape)
        tm, tn, tk = c, _pick(n, tn or MM_TILE[1]), _pick(k, tk or MM_TILE[2])
        grid = (s, n // tn, k // tk)
        a_spec = pl.BlockSpec((sq, tk, c), lambda i, j, kk: (i, kk, 0))
        b_spec = pl.BlockSpec((tk, tn), lambda i, j, kk: (kk, j))
        o_spec, o_shape = pl.BlockSpec((sq, c, tn), lambda i, j, kk: (i, 0, j)), (s, c, n)
    else:
        assert slab == "k" and not ta
        s, c = b.shape[0], (b.shape[2] if tb else b.shape[1])
        m, n = a.shape[1], (b.shape[1] if tb else b.shape[2])
        assert a.shape[2] == c and a.shape[0] >= a_slab0 + s, (a.shape, b.shape, a_slab0)
        tm, tn, tk = _pick(m, tm or MM_TILE[0]), _pick(n, tn or MM_TILE[1]), c
        per_step = SLABS_PER_STEP if (s % SLABS_PER_STEP == 0 and a_slab0 % SLABS_PER_STEP == 0) else 1
        first = a_slab0 // per_step
        grid = (m // tm, n // tn, s // per_step)
        a_spec = pl.BlockSpec((per_step, tm, c), lambda i, j, kk: (kk + first, i, 0))
        b_spec = (pl.BlockSpec((per_step, tn, c), lambda i, j, kk: (kk, j, 0)) if tb
                  else pl.BlockSpec((per_step, c, tn), lambda i, j, kk: (kk, 0, j)))
        o_spec, o_shape = pl.BlockSpec((tm, tn), lambda i, j, kk: (i, j)), (m, n)
    nk = grid[2]
    dims = (((0 if ta else 1,), (1 if tb else 0,)), ((), ()))

    def product(a_ref, b_ref):
        if slab != "k":
            return lax.dot_general(a_ref[...].astype(BF16), b_ref[...].astype(BF16), dims, preferred_element_type=F32)
        r = None
        for u in range(a_ref.shape[0]):
            p = lax.dot_general(a_ref[u].astype(BF16), b_ref[u].astype(BF16), dims, preferred_element_type=F32)
            r = p if r is None else r + p
        return r

    if swiglu is not None:
        assert slab == "n" and add is None
        o_block = pl.BlockSpec((2, sq, tm, c), lambda i, j, kk: (0, j, i, 0))
        o_shape, out_dtype = (2,) + o_shape, BF16

    if rope is not None:
        assert slab is None and add is None and swiglu is None and tn % HEAD_PAD == 0
        out_dtype = BF16
    extras = tuple(swiglu or ()) + tuple(rope or ())

    def body(*refs):
        a_ref, b_ref = refs[:2]
        add_ref = refs[2] if add is not None else None
        x0_ref, x1_ref = refs[2:4] if extras else (None, None)
        o_ref = refs[2 + (add is not None) + len(extras)]
        acc_ref = refs[-1] if nk > 1 else None

        def finish(r):
            if swiglu is not None:
                o_ref[0], o_ref[1] = _swiglu_grads(x0_ref[...], x1_ref[...], r)
                return
            if rope is not None:
                cos, sin = x0_ref[...], x1_ref[...]
                for h in range(tn // HEAD_PAD):
                    lo = h * HEAD_PAD
                    o_ref[:, lo:lo + QK_NOPE] = r[:, lo:lo + QK_NOPE].astype(BF16)
                    o_ref[:, lo + QK_NOPE:lo + HEAD_PAD] = _rope(r[:, lo + QK_NOPE:lo + HEAD_PAD], cos, sin).astype(BF16)
                return
            if add_ref is not None:
                r = r + add_ref[...].astype(F32)
            o_ref[...] = r.astype(o_ref.dtype)

        if nk == 1:
            finish(product(a_ref, b_ref))
            return
        kk = pl.program_id(2)

        @pl.when(kk == 0)
        def _():
            acc_ref[...] = product(a_ref, b_ref)

        if nk > 2:
            @pl.when(jnp.logical_and(kk > 0, kk < nk - 1))
            def _():
                acc_ref[...] += product(a_ref, b_ref)

        @pl.when(kk == nk - 1)
        def _():
            finish(acc_ref[...] + product(a_ref, b_ref))

    in_specs = [a_spec, b_spec] + ([o_spec] if add is not None else []) + ([o_spec] * 2 if swiglu is not None else [])
    if rope is not None:
        in_specs += [pl.BlockSpec((tm, LANES), lambda i, j, kk: (i, 0))] * 2
    args = (a, b) + ((add,) if add is not None else ()) + extras
    if swiglu is not None:
        o_spec = o_block
    outs, comm_outs = _call(
        body, grid=grid, in_specs=in_specs, out_specs=[o_spec],
        out_shape=[jax.ShapeDtypeStruct(o_shape, out_dtype)],
        scratch_shapes=[pltpu.VMEM((tm, tn), F32)] if nk > 1 else [],
        sem=("parallel", "parallel", "arbitrary"), name=name, args=args, comm=comm)
    return outs[0] if comm is None else (outs[0], comm_outs)


def _rms_scale(x):
    return lax.rsqrt(jnp.mean(x * x, axis=-1, keepdims=True) + RMS_EPS)


def _rms_bwd(xhat, r, g, dy):
    t = dy * g
    dx = r * (t - xhat * jnp.mean(t * xhat, axis=-1, keepdims=True))
    return dx, dy * xhat


_GELU_C = math.sqrt(2.0 / math.pi)


def _gelu(x):
    return x * (0.5 * (1.0 + jnp.tanh(_GELU_C * (x + 0.044715 * (x * x * x)))))


def _gelu_and_grad(x):
    t = jnp.tanh(_GELU_C * (x + 0.044715 * (x * x * x)))
    cdf = 0.5 * (1.0 + t)
    return x * cdf, cdf + x * (0.5 * (1.0 - t * t) * (_GELU_C * (1.0 + 3.0 * 0.044715 * (x * x))))


def _sigmoid(x):
    return 1.0 / (1.0 + jnp.exp(-x))


def _swap_halves(x):
    lane = lax.broadcasted_iota(jnp.int32, x.shape, 1)
    first = (lane % QK_ROPE) < (QK_ROPE // 2)
    return jnp.where(first, pltpu.roll(x, LANES - QK_ROPE // 2, 1), pltpu.roll(x, QK_ROPE // 2, 1))


def _rope(x, cos, sin_signed):
    return x * cos + _swap_halves(x) * sin_signed


def _rope_bwd(d, cos, sin_signed):
    return d * cos + _swap_halves(d * sin_signed)


def _rope_tables(pos_col, inv_freq_row, sign_row):
    t = pos_col.shape[0]
    tr = _pick(t, 512, SUBLANES)

    def body(p_ref, f_ref, s_ref, cos_ref, sin_ref):
        ang = p_ref[...].astype(F32) * f_ref[...]
        cos_ref[...] = jnp.cos(ang)
        sin_ref[...] = jnp.sin(ang) * s_ref[...]

    return pl.pallas_call(
        body, grid=(t // tr,), in_specs=[_rows(tr, 1), _full((1, LANES)), _full((1, LANES))],
        out_specs=[_rows(tr, LANES), _rows(tr, LANES)],
        out_shape=[jax.ShapeDtypeStruct((t, LANES), F32)] * 2,
        compiler_params=_params(("parallel",)), name="rope_tables")(pos_col, inv_freq_row, sign_row)


def _norm_fwd(x, g, name):
    t, d = x.shape
    tr = _row_tile(t, d, 2)

    def body(x_ref, g_ref, y_ref):
        xv = x_ref[...]
        y_ref[...] = (xv * _rms_scale(xv) * g_ref[...]).astype(BF16)

    return pl.pallas_call(
        body, grid=(t // tr,), in_specs=[_rows(tr, d), _full((1, d))], out_specs=_rows(tr, d),
        out_shape=jax.ShapeDtypeStruct((t, d), BF16), compiler_params=_params(("parallel",)), name=name)(x, g)


def _lat_fwd(z_lat, qg, kvg, cos, sin, ql, kvl):
    t = z_lat.shape[0]
    tr = _row_tile(t, z_lat.shape[1], 2)

    def body(z_ref, qg_ref, kvg_ref, cos_ref, sin_ref, qn_ref, kvn_ref, kpe_ref):
        q = z_ref[:, 0:ql]
        qn_ref[...] = (q * _rms_scale(q) * qg_ref[...]).astype(BF16)
        kv = z_ref[:, ql:ql + kvl]
        kvn_ref[...] = (kv * _rms_scale(kv) * kvg_ref[...]).astype(BF16)
        kpe_ref[...] = _rope(z_ref[:, ql + kvl:ql + kvl + LANES], cos_ref[...], sin_ref[...]).astype(BF16)

    w = z_lat.shape[1]
    return pl.pallas_call(
        body, grid=(t // tr,),
        in_specs=[_rows(tr, w), _full((1, ql)), _full((1, kvl)), _rows(tr, LANES), _rows(tr, LANES)],
        out_specs=[_rows(tr, ql), _rows(tr, kvl), _rows(tr, LANES)],
        out_shape=[jax.ShapeDtypeStruct((t, ql), BF16), jax.ShapeDtypeStruct((t, kvl), BF16),
                   jax.ShapeDtypeStruct((t, LANES), BF16)],
        compiler_params=_params(("parallel",)), name="lat_fwd")(z_lat, qg, kvg, cos, sin)


def _tril_mask():
    r = lax.broadcasted_iota(jnp.int32, (CHUNK, CHUNK), 0)
    c = lax.broadcasted_iota(jnp.int32, (CHUNK, CHUNK), 1)
    return r >= c


def _sgu_fwd(z_uv, gs, ws, b_col):
    t = z_uv.shape[0]
    sw = z_uv.shape[1] // 2
    groups = sw // SGU_GROUP
    tr = _pick(t, 256, CHUNK)

    def body(u_ref, v_ref, gs_ref, ws_ref, b_ref, o_ref):
        v = _gelu(v_ref[...])
        vn = (v * _rms_scale(v) * gs_ref[...]).astype(BF16)
        tri = _tril_mask()
        for g in range(groups):
            wg = jnp.where(tri, ws_ref[g], 0.0).astype(BF16)
            cols = slice(g * SGU_GROUP, (g + 1) * SGU_GROUP)
            for c in range(tr // CHUNK):
                rows = slice(c * CHUNK, (c + 1) * CHUNK)
                mixed = jnp.dot(wg, vn[rows, cols], preferred_element_type=F32) + b_ref[g]
                o_ref[rows, cols] = (_gelu(u_ref[rows, cols]) * mixed).astype(BF16)

    return pl.pallas_call(
        body, grid=(t // tr,),
        in_specs=[_rows(tr, sw, 0), _rows(tr, sw, 1), _full((1, sw)), _full(ws.shape), _full(b_col.shape)],
        out_specs=_rows(tr, sw), out_shape=jax.ShapeDtypeStruct((t, sw), BF16),
        compiler_params=_params(("parallel",)), name="sgu_fwd")(z_uv, z_uv, gs, ws, b_col)


def _merge_fwd(y_attn, y_sgu, z_g, b_gate, comm=None):
    t, d = y_attn.shape
    tr = _row_tile(t, d, 5)

    def body(ya_ref, ys_ref, g0_ref, g1_ref, b0_ref, b1_ref, o_ref):
        g0 = _sigmoid(g0_ref[...] + b0_ref[...])
        g1 = _sigmoid(g1_ref[...] + b1_ref[...])
        o_ref[...] = (g0 * ya_ref[...] + g1 * ys_ref[...]).astype(BF16)

    bspec0 = pl.BlockSpec((1, d), lambda i: (0, 0))
    bspec1 = pl.BlockSpec((1, d), lambda i: (0, 1))
    outs, comm_outs = _call(
        body, grid=(t // tr,),
        in_specs=[_rows(tr, d), _rows(tr, d), _rows(tr, d, 0), _rows(tr, d, 1), bspec0, bspec1],
        out_specs=[_rows(tr, d)], out_shape=[jax.ShapeDtypeStruct((t, d), BF16)],
        sem=("parallel",), name="merge_fwd", args=(y_attn, y_sgu, z_g, z_g, b_gate, b_gate), comm=comm)
    return outs[0], comm_outs


def _swiglu_fwd(gate, up, comm=None):
    t, f = gate.shape
    tr = _row_tile(t, f, 3)

    def body(g_ref, u_ref, o_ref):
        g = g_ref[...]
        o_ref[...] = (g * _sigmoid(g) * u_ref[...]).astype(BF16)

    outs, comm_outs = _call(
        body, grid=(t // tr,), in_specs=[_rows(tr, f), _rows(tr, f)], out_specs=[_rows(tr, f)],
        out_shape=[jax.ShapeDtypeStruct((t, f), BF16)], sem=("parallel",), name="swiglu_fwd", args=(gate, up), comm=comm)
    return outs[0], comm_outs


def _loss_head(h2, g, target):
    t, d = h2.shape
    tr = _row_tile(t, d, 3)

    def body(h_ref, g_ref, t_ref, loss_ref, dh_ref, dhb_ref, dg_ref):
        @pl.when(pl.program_id(0) == 0)
        def _():
            loss_ref[...] = jnp.zeros_like(loss_ref)
            dg_ref[...] = jnp.zeros_like(dg_ref)

        h = h_ref[...]
        r = _rms_scale(h)
        hhat = h * r
        gv = g_ref[...]
        err = hhat * gv - t_ref[...]
        loss_ref[...] += jnp.full(loss_ref.shape, 0.5 * jnp.sum(jnp.mean(err * err, axis=-1)), F32)
        dx, dg_rows = _rms_bwd(hhat, r, gv, err * (1.0 / d))
        dh_ref[...] = dx
        dhb_ref[...] = dx.astype(BF16)
        dg_ref[...] += jnp.sum(dg_rows, axis=0, keepdims=True)

    return pl.pallas_call(
        body, grid=(t // tr,), in_specs=[_rows(tr, d), _full((1, d)), _rows(tr, d)],
        out_specs=[_full((1, LANES)), _rows(tr, d), _rows(tr, d), _full((1, d))],
        out_shape=[jax.ShapeDtypeStruct((1, LANES), F32), jax.ShapeDtypeStruct((t, d), F32),
                   jax.ShapeDtypeStruct((t, d), BF16), jax.ShapeDtypeStruct((1, d), F32)],
        compiler_params=_params(("arbitrary",)), name="loss_head")(h2, g, target)


def _norm_bwd(x, g, dy, resid, name, comm=None):
    t, d = x.shape
    tr = _row_tile(t, d, 5)

    def body(x_ref, g_ref, dy_ref, r_ref, dx_ref, dxb_ref, dg_ref):
        @pl.when(pl.program_id(0) == 0)
        def _():
            dg_ref[...] = jnp.zeros_like(dg_ref)

        xv = x_ref[...]
        r = _rms_scale(xv)
        dx, dg_rows = _rms_bwd(xv * r, r, g_ref[...], dy_ref[...])
        dx = r_ref[...] + dx
        dx_ref[...] = dx
        dxb_ref[...] = dx.astype(BF16)
        dg_ref[...] += jnp.sum(dg_rows, axis=0, keepdims=True)

    outs, comm_outs = _call(
        body, grid=(t // tr,), in_specs=[_rows(tr, d), _full((1, d)), _rows(tr, d), _rows(tr, d)],
        out_specs=[_rows(tr, d), _rows(tr, d), _full((1, d))],
        out_shape=[jax.ShapeDtypeStruct((t, d), F32), jax.ShapeDtypeStruct((t, d), BF16),
                   jax.ShapeDtypeStruct((1, d), F32)],
        sem=("arbitrary",), name=name, args=(x, g, dy, resid), comm=comm)
    return (outs[0], outs[1], outs[2]) if comm is None else (outs[0], outs[1], outs[2], comm_outs)


def _merge_bwd(dmerged, y_attn, y_sgu, z_g, b_gate):
    t, d = y_attn.shape
    tr = _row_tile(t, d, 7)

    def body(dm_ref, ya_ref, ys_ref, g0_ref, g1_ref, b0_ref, b1_ref, dya_ref, dys_ref, dz_ref, db_ref):
        @pl.when(pl.program_id(0) == 0)
        def _():
            db_ref[...] = jnp.zeros_like(db_ref)

        dm = dm_ref[...]
        g0 = _sigmoid(g0_ref[...] + b0_ref[...])
        g1 = _sigmoid(g1_ref[...] + b1_ref[...])
        dya_ref[...] = (dm * g0).astype(BF16)
        dys_ref[...] = (dm * g1).astype(BF16)
        dl0 = dm * ya_ref[...] * (g0 * (1.0 - g0))
        dl1 = dm * ys_ref[...] * (g1 * (1.0 - g1))
        dz_ref[:, 0:d] = dl0.astype(BF16)
        dz_ref[:, d:2 * d] = dl1.astype(BF16)
        db_ref[:, 0:d] += jnp.sum(dl0, axis=0, keepdims=True)
        db_ref[:, d:2 * d] += jnp.sum(dl1, axis=0, keepdims=True)

    bspec0 = pl.BlockSpec((1, d), lambda i: (0, 0))
    bspec1 = pl.BlockSpec((1, d), lambda i: (0, 1))
    return pl.pallas_call(
        body, grid=(t // tr,),
        in_specs=[_rows(tr, d), _rows(tr, d), _rows(tr, d), _rows(tr, d, 0), _rows(tr, d, 1), bspec0, bspec1],
        out_specs=[_rows(tr, d), _rows(tr, d), _rows(tr, 2 * d), _full((1, 2 * d))],
        out_shape=[jax.ShapeDtypeStruct((t, d), BF16), jax.ShapeDtypeStruct((t, d), BF16),
                   jax.ShapeDtypeStruct((t, 2 * d), BF16), jax.ShapeDtypeStruct((1, 2 * d), F32)],
        compiler_params=_params(("arbitrary",)), name="merge_bwd")(dmerged, y_attn, y_sgu, z_g, z_g, b_gate, b_gate)


def _sgu_bwd(z_uv, ds_out, gs, ws, b_col):
    t = z_uv.shape[0]
    sw = z_uv.shape[1] // 2
    groups = sw // SGU_GROUP
    tr = _pick(t, 256, CHUNK)

    def body(u_ref, v_ref, d_ref, gs_ref, ws_ref, b_ref, dz_ref, dws_ref, db_ref, dgs_ref, dvn_ref):
        @pl.when(pl.program_id(0) == 0)
        def _():
            dws_ref[...] = jnp.zeros_like(dws_ref)
            db_ref[...] = jnp.zeros_like(db_ref)
            dgs_ref[...] = jnp.zeros_like(dgs_ref)

        v, dgelu_v = _gelu_and_grad(v_ref[...])
        r = _rms_scale(v)
        vhat = v * r
        gsv = gs_ref[...]
        vn = (vhat * gsv).astype(BF16)
        tri = _tril_mask()
        for g in range(groups):
            wg = jnp.where(tri, ws_ref[g], 0.0).astype(BF16)
            cols = slice(g * SGU_GROUP, (g + 1) * SGU_GROUP)
            for c in range(tr // CHUNK):
                rows = slice(c * CHUNK, (c + 1) * CHUNK)
                vn_cg = vn[rows, cols]
                mixed = jnp.dot(wg, vn_cg, preferred_element_type=F32) + b_ref[g]
                u, dgelu_u = _gelu_and_grad(u_ref[rows, cols])
                dso = d_ref[rows, cols]
                dz_ref[rows, cols] = (dso * mixed * dgelu_u).astype(BF16)
                dmixed = dso * u
                db_ref[g] += jnp.sum(dmixed, axis=1, keepdims=True)
                dmixed_b = dmixed.astype(BF16)
                dws_ref[g] += jnp.where(
                    tri, lax.dot_general(dmixed_b, vn_cg, (((1,), (1,)), ((), ())), preferred_element_type=F32), 0.0)
                dvn_ref[rows, cols] = lax.dot_general(wg, dmixed_b, (((0,), (0,)), ((), ())), preferred_element_type=F32)
        dvn = dvn_ref[...]
        dv, dgs_rows = _rms_bwd(vhat, r, gsv, dvn)
        dz_ref[:, sw:2 * sw] = (dv * dgelu_v).astype(BF16)
        dgs_ref[...] += jnp.sum(dgs_rows, axis=0, keepdims=True)

    return pl.pallas_call(
        body, grid=(t // tr,),
        in_specs=[_rows(tr, sw, 0), _rows(tr, sw, 1), _rows(tr, sw), _full((1, sw)), _full(ws.shape), _full(b_col.shape)],
        out_specs=[_rows(tr, 2 * sw), _full(ws.shape), _full(b_col.shape), _full((1, sw))],
        out_shape=[jax.ShapeDtypeStruct((t, 2 * sw), BF16), jax.ShapeDtypeStruct(ws.shape, F32),
                   jax.ShapeDtypeStruct(b_col.shape, F32), jax.ShapeDtypeStruct((1, sw), F32)],
        scratch_shapes=[pltpu.VMEM((tr, sw), F32)],
        compiler_params=_params(("arbitrary",)), name="sgu_bwd")(z_uv, z_uv, ds_out, gs, ws, b_col)


def _lat_bwd(z_lat, qg, kvg, dqn, dkvn, dkpe_heads, cos, sin, ql, kvl):
    t, w = z_lat.shape
    heads = dkpe_heads.shape[0]
    tr = _row_tile(t, w + heads * LANES, 3)

    def body(z_ref, qg_ref, kvg_ref, dq_ref, dkv_ref, dk_ref, cos_ref, sin_ref, dz_ref, dqg_ref, dkvg_ref):
        @pl.when(pl.program_id(0) == 0)
        def _():
            dqg_ref[...] = jnp.zeros_like(dqg_ref)
            dkvg_ref[...] = jnp.zeros_like(dkvg_ref)

        q = z_ref[:, 0:ql]
        r = _rms_scale(q)
        dx, dg_rows = _rms_bwd(q * r, r, qg_ref[...], dq_ref[...])
        dz_ref[:, 0:ql] = dx.astype(BF16)
        dqg_ref[...] += jnp.sum(dg_rows, axis=0, keepdims=True)
        kv = z_ref[:, ql:ql + kvl]
        r = _rms_scale(kv)
        dx, dg_rows = _rms_bwd(kv * r, r, kvg_ref[...], dkv_ref[...])
        dz_ref[:, ql:ql + kvl] = dx.astype(BF16)
        dkvg_ref[...] += jnp.sum(dg_rows, axis=0, keepdims=True)
        dk = dk_ref[0]
        for h in range(1, heads):
            dk = dk + dk_ref[h]
        dz_ref[:, ql + kvl:ql + kvl + LANES] = _rope_bwd(dk, cos_ref[...], sin_ref[...]).astype(BF16)

    return pl.pallas_call(
        body, grid=(t // tr,),
        in_specs=[_rows(tr, w), _full((1, ql)), _full((1, kvl)), _rows(tr, ql), _rows(tr, kvl),
                  pl.BlockSpec((heads, tr, LANES), lambda i: (0, i, 0)), _rows(tr, LANES), _rows(tr, LANES)],
        out_specs=[_rows(tr, w), _full((1, ql)), _full((1, kvl))],
        out_shape=[jax.ShapeDtypeStruct((t, w), BF16), jax.ShapeDtypeStruct((1, ql), F32),
                   jax.ShapeDtypeStruct((1, kvl), F32)],
        compiler_params=_params(("arbitrary",)), name="lat_bwd")(z_lat, qg, kvg, dqn, dkvn, dkpe_heads, cos, sin)


_NT = (((1,), (1,)), ((), ()))


def _attn_scale():
    return (QK_NOPE + QK_ROPE) ** -0.5


def _heads_per_step(heads, wanted):
    return wanted if heads % wanted == 0 else 1


def _attn_fwd(q_c, kv, kpe, comm=None):
    t = q_c.shape[0]
    heads = q_c.shape[1] // HEAD_PAD
    tq = _pick(t, ATTN_TILE)
    nq = t // tq
    scale = _attn_scale()
    to_log2 = scale * math.log2(math.e)
    tn_dims = (((0,), (0,)), ((), ()))

    hps = _heads_per_step(heads, HEADS_PER_STEP[0])
    half = tq // 2

    def body(q_ref, kv_ref, kpe_ref, o_ref, ob_ref, lse_ref, m_sc, l_sc, acc_sc):
        qi, ki = pl.program_id(1), pl.program_id(2)

        @pl.when(ki == 0)
        def _():
            m_sc[...] = jnp.full_like(m_sc, NEG_BIG)
            l_sc[...] = jnp.zeros_like(l_sc)
            acc_sc[...] = jnp.zeros_like(acc_sc)

        def block(u, q0, nq_, nk_, diagonal):
            lo, qs = u * HEAD_PAD, slice(q0, q0 + nq_)
            kc = jnp.concatenate([kv_ref[0:nk_, lo:lo + QK_NOPE], kpe_ref[0:nk_, :]], axis=1)
            st = lax.dot_general(kc, q_ref[qs, lo:lo + HEAD_PAD], _NT, preferred_element_type=F32)
            if diagonal:
                krow = lax.broadcasted_iota(jnp.int32, st.shape, 0)
                qcol = lax.broadcasted_iota(jnp.int32, st.shape, 1) + q0
                st = jnp.where(qcol >= krow, st, NEG_BIG)
            m_prev = m_sc[u, :, qs]
            m_new = jnp.maximum(m_prev, jnp.max(st, axis=0, keepdims=True))
            alpha = jnp.exp2((m_prev - m_new) * to_log2)
            pt = jnp.exp2((st - m_new) * to_log2)
            l_sc[u, :, qs] = alpha * l_sc[u, :, qs] + jnp.sum(pt, axis=0, keepdims=True)
            acc_sc[u, :, qs] = alpha * acc_sc[u, :, qs] + lax.dot_general(
                kv_ref[0:nk_, lo + QK_NOPE:lo + HEAD_PAD], pt.astype(BF16), tn_dims, preferred_element_type=F32)
            m_sc[u, :, qs] = m_new

        def step(diagonal):
            for u in range(hps):
                if diagonal and half % LANES == 0:
                    block(u, 0, half, half, True)
                    block(u, half, half, tq, True)
                else:
                    block(u, 0, tq, tq, diagonal)

        @pl.when(ki < qi)
        def _():
            step(False)

        @pl.when(ki == qi)
        def _():
            step(True)
            for u in range(hps):
                o = (acc_sc[u] / l_sc[u]).T
                o_ref[:, u * V_HEAD:(u + 1) * V_HEAD] = o
                ob_ref[:, u * V_HEAD:(u + 1) * V_HEAD] = o.astype(BF16)
                lse_ref[u] = m_sc[u] * scale + jnp.log(l_sc[u])

    omap = lambda g, qi, ki: (qi, g)
    outs, comm_outs = _call(
        body, grid=(heads // hps, nq, nq),
        in_specs=[pl.BlockSpec((tq, hps * HEAD_PAD), omap),
                  pl.BlockSpec((tq, hps * HEAD_PAD), lambda g, qi, ki: (jnp.minimum(ki, qi), g)),
                  pl.BlockSpec((tq, LANES), lambda g, qi, ki: (jnp.minimum(ki, qi), 0))],
        out_specs=[pl.BlockSpec((tq, hps * V_HEAD), omap), pl.BlockSpec((tq, hps * V_HEAD), omap),
                   pl.BlockSpec((hps, 1, tq), lambda g, qi, ki: (g, 0, qi))],
        out_shape=[jax.ShapeDtypeStruct((t, heads * V_HEAD), F32), jax.ShapeDtypeStruct((t, heads * V_HEAD), BF16),
                   jax.ShapeDtypeStruct((heads, 1, t), F32)],
        scratch_shapes=[pltpu.VMEM((hps, 1, tq), F32), pltpu.VMEM((hps, 1, tq), F32),
                        pltpu.VMEM((hps, V_HEAD, tq), F32)],
        sem=("parallel", "parallel", "arbitrary"), name="attn_fwd", args=(q_c, kv, kpe), comm=comm)
    return outs[0], outs[1], outs[2], comm_outs


def _attn_bwd(q_c, kv, kpe, o, do, lse_row, cos, sin, comm=None):
    t = q_c.shape[0]
    heads = q_c.shape[1] // HEAD_PAD
    tk = _pick(t, ATTN_TILE)
    nk = t // tk
    scale = _attn_scale()
    tn_dims = (((0,), (0,)), ((), ()))

    hps = _heads_per_step(heads, HEADS_PER_STEP[1])
    half = tk // 2

    def body(q_ref, kv_ref, kpe_ref, do_ref, lse_ref, o_ref, cos_ref, sin_ref, dq_ref, dkv_ref, dkpe_ref,
             dk_sc, dv_sc, delta_sc, dq_sc):
        ki, qi = pl.program_id(1), pl.program_id(2)

        @pl.when(jnp.logical_and(ki == 0, qi == 0))
        def _():
            dq_sc[...] = jnp.zeros_like(dq_sc)

        @pl.when(qi == 0)
        def _():
            dk_sc[...] = jnp.zeros_like(dk_sc)
            dv_sc[...] = jnp.zeros_like(dv_sc)

        @pl.when(ki == 0)
        def _():
            for u in range(hps):
                cols = slice(u * V_HEAD, (u + 1) * V_HEAD)
                delta_sc[qi * hps + u] = jnp.sum((do_ref[:, cols] * o_ref[:, cols]).T, axis=0, keepdims=True)

        def block(u, q0, nq_, nk_, diagonal):
            lo, qs, ks = u * HEAD_PAD, slice(q0, q0 + nq_), slice(0, nk_)
            kc = jnp.concatenate([kv_ref[ks, lo:lo + QK_NOPE], kpe_ref[ks, :]], axis=1)
            q = q_ref[qs, lo:lo + HEAD_PAD]
            st = lax.dot_general(kc, q, _NT, preferred_element_type=F32) * scale
            pt = jnp.exp(st - lse_ref[u, :, qs])
            if diagonal:
                krow = lax.broadcasted_iota(jnp.int32, st.shape, 0)
                qcol = lax.broadcasted_iota(jnp.int32, st.shape, 1) + q0
                pt = jnp.where(qcol >= krow, pt, 0.0)
            do_b = do_ref[qs, u * V_HEAD:(u + 1) * V_HEAD].astype(BF16)
            dv_sc[u, ks] += jnp.dot(pt.astype(BF16), do_b, preferred_element_type=F32)
            dpt = lax.dot_general(kv_ref[ks, lo + QK_NOPE:lo + HEAD_PAD], do_b, _NT, preferred_element_type=F32)
            dst = (pt * (dpt - delta_sc[qi * hps + u, :, qs]) * scale).astype(BF16)
            dk_sc[u, ks] += jnp.dot(dst, q, preferred_element_type=F32)
            rows = pl.ds(pl.multiple_of(qi * tk + q0, nq_), nq_)
            dq_sc[rows, lo:lo + HEAD_PAD] += lax.dot_general(dst, kc, tn_dims, preferred_element_type=F32)

        def step(diagonal):
            for u in range(hps):
                if diagonal and half % LANES == 0:
                    block(u, 0, half, half, True)
                    block(u, half, half, tk, True)
                else:
                    block(u, 0, tk, tk, diagonal)

        @pl.when(qi > ki)
        def _():
            step(False)

        @pl.when(qi == ki)
        def _():
            step(True)

        @pl.when(qi == nk - 1)
        def _():
            for u in range(hps):
                lo = u * HEAD_PAD
                dkv_ref[:, lo:lo + QK_NOPE] = dk_sc[u, :, 0:QK_NOPE].astype(BF16)
                dkv_ref[:, lo + QK_NOPE:lo + HEAD_PAD] = dv_sc[u].astype(BF16)
                dkpe_ref[u] = dk_sc[u, :, QK_NOPE:QK_NOPE + LANES]

        @pl.when(jnp.logical_and(ki == nk - 1, qi == nk - 1))
        def _():
            cos, sin = cos_ref[...], sin_ref[...]
            for u in range(hps):
                lo = u * HEAD_PAD
                dq_ref[:, lo:lo + QK_NOPE] = dq_sc[:, lo:lo + QK_NOPE].astype(BF16)
                dq_ref[:, lo + QK_NOPE:lo + HEAD_PAD] = _rope_bwd(dq_sc[:, lo + QK_NOPE:lo + HEAD_PAD], cos, sin).astype(BF16)

    qclamp = lambda g, ki, qi: (jnp.maximum(qi, ki), g)
    outs, comm_outs = _call(
        body, grid=(heads // hps, nk, nk),
        in_specs=[pl.BlockSpec((tk, hps * HEAD_PAD), qclamp),
                  pl.BlockSpec((tk, hps * HEAD_PAD), lambda g, ki, qi: (ki, g)),
                  pl.BlockSpec((tk, LANES), lambda g, ki, qi: (ki, 0)),
                  pl.BlockSpec((tk, hps * V_HEAD), qclamp),
                  pl.BlockSpec((hps, 1, tk), lambda g, ki, qi: (g, 0, jnp.maximum(qi, ki))),
                  pl.BlockSpec((tk, hps * V_HEAD), lambda g, ki, qi: (jnp.where(ki == 0, qi, 0), g)),
                  _full((t, LANES)), _full((t, LANES))],
        out_specs=[pl.BlockSpec((t, hps * HEAD_PAD), lambda g, ki, qi: (0, g)),
                   pl.BlockSpec((tk, hps * HEAD_PAD), lambda g, ki, qi: (ki, g)),
                   pl.BlockSpec((hps, tk, LANES), lambda g, ki, qi: (g, ki, 0))],
        out_shape=[jax.ShapeDtypeStruct((t, heads * HEAD_PAD), BF16),
                   jax.ShapeDtypeStruct((t, heads * HEAD_PAD), BF16), jax.ShapeDtypeStruct((heads, t, LANES), F32)],
        scratch_shapes=[pltpu.VMEM((hps, tk, HEAD_PAD), F32), pltpu.VMEM((hps, tk, V_HEAD), F32),
                        pltpu.VMEM((nk * hps, 1, tk), F32), pltpu.VMEM((t, hps * HEAD_PAD), F32)],
        sem=("parallel", "arbitrary", "arbitrary"), name="attn_bwd",
        args=(q_c, kv, kpe, do, lse_row, o, cos, sin), comm=comm)
    return outs[0], outs[1], outs[2], comm_outs


def _local_step(x, pos_col, target, small, shards, opt):
    t = x.shape[0]
    ql, kvl = small["q_norm_g"].shape[1], small["kv_norm_g"].shape[1]
    sw = small["sgu_norm_g"].shape[1]
    heads = (shards["w_uq"].shape[1] * N_DEV) // (QK_NOPE + QK_ROPE)
    big = {}
    big.update(_compute_layout({"w_in": _all_gather([shards["w_in"]])[0]}, ql, kvl, heads, sw))
    half = QK_ROPE // 2
    lane = jnp.arange(LANES)
    inv_freq = ROPE_THETA ** (-jnp.arange(0, QK_ROPE, 2, dtype=F32) / QK_ROPE)
    inv_row = inv_freq[lane % half][None, :]
    sign_row = jnp.where((lane % QK_ROPE) < half, -1.0, 1.0).astype(F32)[None, :]
    cos, sin = _rope_tables(pos_col, inv_row, sign_row)
    ws = small["w_sgu"]
    b_col = small["b_sgu_col"]

    def arrived(names, bufs):
        big.update(_compute_layout(dict(zip(names, bufs)), ql, kvl, heads, sw))

    a = _norm_fwd(x, small["norm_mix_g"], "norm_mix_fwd")
    z_lat, g_qk = _mm(a, big["w_lat_t"], tb=True, name="z_lat",
                      comm=_gather_stage(1, [shards["w_uq"], shards["w_ukv"]]))
    z_uv, (g_sgu, *g_qk) = _mm(a, big["w_uv_t"], tb=True, name="z_uv",
                               comm=_join(_gather_stage(1, [shards["w_o_sgu"]]), _gather_stage(2, g_qk)))
    z_g, (g_attn, g_sgu, *g_qk) = _mm(
        a, big["w_g_t"], tb=True, name="z_g",
        comm=_join(_gather_stage(1, [shards["w_o_attn"]]), _gather_stage(2, [g_sgu]), _gather_stage(3, g_qk)))
    arrived(["w_uq", "w_ukv"], g_qk)
    qn, kvn, kpe = _lat_fwd(z_lat, small["q_norm_g"], small["kv_norm_g"], cos, sin, ql, kvl)
    q_c, (g_attn, g_sgu) = _mm(qn, big["w_uq"], name="q_up_rope", rope=(cos, sin),
                               comm=_join(_gather_stage(2, [g_attn]), _gather_stage(3, [g_sgu])))
    kv, (g_attn, g_out) = _mm(kvn, big["w_ukv"], out_dtype=BF16, name="kv_up",
                              comm=_join(_gather_stage(3, [g_attn]), _gather_stage(1, [shards["w_out"]])))
    arrived(["w_o_sgu", "w_o_attn"], [g_sgu, g_attn])
    attn, attn_b, lse, (w_gate, w_up) = _attn_fwd(
        q_c, kv, kpe, comm=_gather_stage(1, [shards["w_gate_ffn"], shards["w_up_ffn"]]))
    s_out = _sgu_fwd(z_uv, small["sgu_norm_g"], ws, b_col)
    y_sgu, (g_out,) = _mm(s_out, big["w_o_sgu"], name="y_sgu", comm=_gather_stage(2, [g_out]))
    y_attn, (w_gate, g_out) = _mm(attn_b, big["w_o_attn"], name="y_attn",
                                  comm=_join(_gather_stage(2, [w_gate]), _gather_stage(3, [g_out])))
    arrived(["w_out"], [g_out])
    merged, (w_up, w_gate) = _merge_fwd(y_attn, y_sgu, z_g, small["b_gate"],
                                        comm=_join(_gather_stage(2, [w_up]), _gather_stage(3, [w_gate])))
    h1, (w_up,) = _mm(merged, big["w_out"], add=x, name="h1", comm=_gather_stage(3, [w_up]))
    f = _norm_fwd(h1, small["norm_ffn_g"], "norm_ffn_fwd")
    gate, w_down = _mm(f, w_gate, tb=True, slab="n", name="ffn_gate", comm=_gather_stage(1, [shards["w_down_ffn"]]))
    up, w_down = _mm(f, w_up, tb=True, slab="n", name="ffn_up", comm=_gather_stage(2, w_down))
    ffn = gate.shape[2]
    gate, up = gate.reshape(N_DEV * t, ffn), up.reshape(N_DEV * t, ffn)
    act, (w_down,) = _swiglu_fwd(gate, up, comm=_gather_stage(3, w_down))
    act = act.reshape(N_DEV, t, ffn)
    h2 = _mm(act, w_down, slab="k", add=h1, name="h2")
    loss_row, dh2, dh2_b, d_norm_final = _loss_head(h2, small["norm_final_g"], target)

    def pair_sums(names, slabs, bufs):
        return [_pair_sum(g, b, "pair_sum_" + k) for k, g, b in zip(names, slabs, bufs)]

    parts, updates = {}, {}

    def update(names, label, comm=None):
        res, got = _adamw_shards([parts[k] for k in names], [opt[k] for k in names], "adamw_" + label, comm=comm)
        updates.update(zip(names, res))
        return got

    down_slabs = [_mm(act, dh2_b, ta=True, slab="m", out_dtype=BF16, name="dw_down")]
    dgu, bufs = _mm(dh2_b, w_down, tb=True, slab="n", tm=MM_TILE[0] // 2, name="dact_swiglu_bwd",
                    comm=_to_sibling(down_slabs), swiglu=(gate.reshape(N_DEV, t, ffn), up.reshape(N_DEV, t, ffn)))
    dgu = dgu.reshape(2 * N_DEV, t, ffn)
    down_pair = pair_sums(["w_down_ffn"], down_slabs, bufs)
    dw_gu, got = _mm(dgu, f, ta=True, slab="m", out_dtype=BF16, name="dw_gate_up", comm=_to_chips(down_pair))
    parts["w_down_ffn"] = got[0]
    gu_names = ["w_gate_ffn", "w_up_ffn"]
    df, bufs = _mm(dgu, w_gate, slab="k", name="df_gate", comm=_to_sibling([dw_gu, dw_gu], first=[0, N_DEV]))
    gu_pairs = [_pair_sum(dw_gu, b, "pair_sum_" + k, first=s0) for k, b, s0 in zip(gu_names, bufs, [0, N_DEV])]
    half = _pick(gu_pairs[1].shape[1], gu_pairs[1].shape[1] // 2, 2 * SUBLANES)
    df, up_parts = _mm(dgu, w_up, slab="k", a_slab0=N_DEV, add=df, name="df_up",
                       comm=_to_chips(gu_pairs[1:], rows=[("r", 0, half)]))
    quarter = _pick(half, half // 2, 2 * SUBLANES)
    dh1, dh1_b, d_norm_ffn, gate_parts = _norm_bwd(h1, small["norm_ffn_g"], df, dh2, "norm_ffn_bwd",
                                                  comm=_to_chips(gu_pairs[:1], rows=[("r", 0, quarter)]))
    dw_out = _mm(merged, dh1_b, ta=True, out_dtype=BF16, name="dw_out")
    out_slabs = [_slabs_from_rows(dw_out)]
    dmerged, bufs = _mm(dh1_b, big["w_out"], tb=True, name="dmerged", comm=_to_sibling(out_slabs))
    out_pair = pair_sums(["w_out"], out_slabs, bufs)
    dy_attn, dy_sgu, dz_g, d_b_gate = _merge_bwd(dmerged, y_attn, y_sgu, z_g, small["b_gate"])
    dw_o_sgu = _mm(s_out, dy_sgu, ta=True, out_dtype=BF16, name="dw_o_sgu")
    ds_out = _mm(dy_sgu, big["w_o_sgu"], tb=True, name="ds_out")
    dz_uv, d_ws, d_b_col, d_sgu_norm = _sgu_bwd(z_uv, ds_out, small["sgu_norm_g"], ws, b_col)
    dw_o_attn = _mm(attn_b, dy_attn, ta=True, out_dtype=BF16, name="dw_o_attn")
    mix_names = ["w_o_sgu", "w_o_attn"]
    mix_slabs = [_slabs_from_cols(dw_o_sgu), _slabs_from_rows(dw_o_attn)]
    dattn, bufs = _mm(dy_attn, big["w_o_attn"], tb=True, name="dattn", comm=_to_sibling(mix_slabs))
    mix_pairs = pair_sums(mix_names, mix_slabs, bufs)
    rows = gu_pairs[1].shape[1]
    dq_p, dkv, dkpe_heads, got = _attn_bwd(
        q_c, kv, kpe, attn, dattn, lse, cos, sin,
        comm=_join(_to_chips(gu_pairs[:1], rows=[("r", quarter, rows - quarter)], into=gate_parts),
                   _to_chips(gu_pairs[1:], rows=[("r", half, rows - half)], into=up_parts)))
    parts.update(zip(gu_names, got))
    dw_uq = _mm(qn, dq_p, ta=True, out_dtype=BF16, name="dw_uq")
    dw_ukv = _mm(kvn, dkv, ta=True, out_dtype=BF16, name="dw_ukv")
    dqn = _mm(dq_p, big["w_uq"], tb=True, name="dqn")
    dkvn = _mm(dkv, big["w_ukv"], tb=True, name="dkvn")
    dz_lat, d_q_norm, d_kv_norm = _lat_bwd(z_lat, small["q_norm_g"], small["kv_norm_g"], dqn, dkvn, dkpe_heads,
                                           cos, sin, ql, kvl)
    dw_g, got = _mm(dz_g, a, ta=True, out_dtype=BF16, name="dw_g", comm=_to_chips(out_pair))
    parts["w_out"] = got[0]
    dw_uv, got = _mm(dz_uv, a, ta=True, out_dtype=BF16, name="dw_uv", comm=_to_chips(mix_pairs[1:]))
    parts["w_o_attn"] = got[0]
    dw_lat, got = _mm(dz_lat, a, ta=True, out_dtype=BF16, name="dw_lat", comm=_to_chips(mix_pairs[:1]))
    parts["w_o_sgu"] = got[0]
    lat = ql + kvl + QK_ROPE
    dw_uq_cols = dw_uq.reshape(ql, heads, HEAD_PAD)[:, :, :QK_NOPE + QK_ROPE].reshape(ql, heads * (QK_NOPE + QK_ROPE))
    in_names = ["w_uq", "w_ukv", "w_in"]
    in_slabs = [_slabs_from_cols(dw_uq_cols), _slabs_from_cols(dw_ukv),
                _slabs_from_rows(jnp.concatenate([dw_lat[:lat], dw_uv, dw_g], axis=0))]
    da = _mm(dz_lat, big["w_lat_t"], name="da_lat")
    da, bufs = _mm(dz_uv, big["w_uv_t"], add=da, name="da_uv", comm=_to_sibling(in_slabs))
    uq_pair, ukv_pair, in_pair = pair_sums(in_names, in_slabs, bufs)
    cols = in_pair.shape[2]
    first = ((cols * TAIL_SPLIT[0]) // TAIL_SPLIT[1]) // LANES * LANES or cols
    da, in_parts = _mm(dz_g, big["w_g_t"], add=da, name="da_g", comm=_to_chips([in_pair], rows=[("c", 0, first)]))
    grad_x, _, d_norm_mix, got = _norm_bwd(x, small["norm_mix_g"], da, dh1, "norm_mix_bwd",
                                          comm=_to_chips([uq_pair, ukv_pair]))
    parts["w_uq"], parts["w_ukv"] = got
    rest = _to_chips([in_pair], rows=[("c", first, cols - first)], into=in_parts) if first < cols else None
    got = update(["w_gate_ffn", "w_up_ffn", "w_down_ffn"], "ffn", comm=rest)
    parts["w_in"] = got[0] if rest is not None else in_parts[0]
    update(["w_out", "w_o_attn"], "mixer_out")
    for k in ("w_o_sgu", "w_uq", "w_ukv", "w_in"):
        update([k], k)

    gs = {"norm_mix_g": d_norm_mix, "b_gate": d_b_gate, "q_norm_g": d_q_norm, "kv_norm_g": d_kv_norm,
          "sgu_norm_g": d_sgu_norm, "w_sgu": d_ws, "b_sgu_col": d_b_col, "norm_ffn_g": d_norm_ffn,
          "norm_final_g": d_norm_final}
    return loss_row, grad_x, gs, updates


def _my_place():
    return lax.axis_index("x"), lax.axis_index("y"), lax.axis_index("c")


N_CHIPS = N_DEV // 2

_GATHER_SEMS = [[(3,), (3,), ()], [(4,), (4,)], [(1,), (1,)]]


def _halves(shape):
    r, c = shape
    if (c // 2) % LANES == 0:
        return ("c", 0, c // 2), ("c", c // 2, c // 2)
    assert (r // 2) % (2 * SUBLANES) == 0, shape
    return ("r", 0, r // 2), ("r", r // 2, r // 2)


def _gather_copies(stage, ins, outs, sems, strips=None):
    x, y, c = _my_place()
    me, x_nbr, y_nbr, diag = 4 * x + 2 * y + c, 4 * (1 - x) + 2 * y + c, 4 * x + 2 * (1 - y) + c, 4 * (1 - x) + 2 * (1 - y) + c
    sibling = (x, y, 1 - c)
    if strips is None:
        strips = [(w, None) for w in range(len(outs))]

    out = []
    for s, (w, cols) in enumerate(strips):
        def remote(k, src, dst, to):
            return pltpu.make_async_remote_copy(src_ref=src, dst_ref=dst, send_sem=sems[0].at[s, k],
                                                recv_sem=sems[1].at[s, k], device_id=to, device_id_type=MESH)

        if cols is None:
            whole, (first, second) = None, _halves(outs[w].shape[1:])
        else:
            c0, nc = cols
            whole, first, second = ("c", c0, nc), ("c", c0, nc // 2), ("c", c0 + nc // 2, nc // 2)
        if stage == 1:
            src = ins[w] if cols is None else ins[w].at[slice(None), pl.ds(*cols)]
            dst = _window(outs[w], me, whole)
            out.append(pltpu.make_async_copy(src, dst, sems[2].at[s]))
            out += [remote(k, src, dst, to) for k, to in enumerate([sibling, (1 - x, y, c), (x, 1 - y, c)])]
        elif stage == 2:
            out.append(remote(0, _window(ins[w], x_nbr, first), _window(outs[w], x_nbr, first), (x, 1 - y, c)))
            out.append(remote(1, _window(ins[w], y_nbr, second), _window(outs[w], y_nbr, second), (1 - x, y, c)))
            out.append(remote(2, _window(ins[w], x_nbr, whole), _window(outs[w], x_nbr, whole), sibling))
            out.append(remote(3, _window(ins[w], y_nbr, whole), _window(outs[w], y_nbr, whole), sibling))
        else:
            out.append(remote(0, _window(ins[w], diag, whole), _window(outs[w], diag, whole), sibling))
    return out


def _gather_stage(stage, arrays):
    n = len(arrays)

    def start(ins, outs, sems):
        for cp in _gather_copies(stage, ins, outs, sems):
            cp.start()

    def finish(ins, outs, sems):
        for cp in _gather_copies(stage, ins, outs, sems):
            cp.wait()

    shapes = [jax.ShapeDtypeStruct(((N_DEV,) + a.shape) if stage == 1 else a.shape, a.dtype) for a in arrays]
    return _Comm(arrays, shapes, [pltpu.SemaphoreType.DMA((n,) + s) for s in _GATHER_SEMS[stage - 1]], start, finish,
                 aliases=None if stage == 1 else {w: w for w in range(n)})


def _join(*comms):
    ins, shapes, sems, aliases, spans = [], [], [], {}, []
    for cm in comms:
        spans.append((len(ins), len(ins) + len(cm.ins), len(shapes), len(shapes) + len(cm.out_shapes),
                      len(sems), len(sems) + len(cm.sems)))
        aliases.update({len(ins) + i: len(shapes) + o for i, o in cm.aliases.items()})
        ins, shapes, sems = ins + cm.ins, shapes + cm.out_shapes, sems + cm.sems

    def each(half):
        def run(i_refs, o_refs, s_refs):
            for cm, (i0, i1, o0, o1, s0, s1) in zip(comms, spans):
                getattr(cm, half)(i_refs[i0:i1], o_refs[o0:o1], s_refs[s0:s1])
        return run

    return _Comm(ins, shapes, sems, each("start"), each("finish"), aliases)


def _all_gather(shards):
    n = len(shards)
    strips = []
    for w, sh in enumerate(shards):
        cols = sh.shape[1]
        nc = cols // GATHER_STRIPS if cols % (GATHER_STRIPS * 2 * LANES) == 0 else cols
        strips += [(w, (c0, nc)) for c0 in range(0, cols, nc)]
    ns = len(strips)
    n_sems = [len(s) for s in _GATHER_SEMS]

    def body(*refs):
        ins, outs, sems = refs[:n], refs[n:2 * n], refs[2 * n:]
        sem1, sem2, sem3 = (sems[sum(n_sems[:i]):sum(n_sems[:i + 1])] for i in range(3))
        c1 = _gather_copies(1, ins, outs, sem1, strips)
        c2 = _gather_copies(2, outs, outs, sem2, strips)
        c3 = _gather_copies(3, outs, outs, sem3, strips)
        for cp in c1:
            cp.start()
        for s in range(ns):
            for cp in c1[4 * s:4 * s + 4]:
                cp.wait()
            for cp in c2[4 * s:4 * s + 4]:
                cp.start()
        for s in range(ns):
            for cp in c2[4 * s:4 * s + 4]:
                cp.wait()
            c3[s].start()
        for cp in c3:
            cp.wait()

    any_spec = pl.BlockSpec(memory_space=pl.ANY)
    return pl.pallas_call(
        body, in_specs=[any_spec] * n, out_specs=[any_spec] * n,
        out_shape=[jax.ShapeDtypeStruct((N_DEV,) + s.shape, s.dtype) for s in shards],
        scratch_shapes=[pltpu.SemaphoreType.DMA((ns,) + s) for stage in _GATHER_SEMS for s in stage],
        compiler_params=pltpu.CompilerParams(has_side_effects=True), name="all_gather_weights")(*shards)


def _to_sibling(grads, first=None):
    n = len(grads)
    first = first or [0] * n

    def copies(ins, outs, sems):
        x, y, c = _my_place()
        send_sems, recv_sems = sems
        return [pltpu.make_async_remote_copy(
            src_ref=ins[w].at[first[w] + 2 * i + (1 - c)], dst_ref=outs[w].at[i], send_sem=send_sems.at[w, i],
            recv_sem=recv_sems.at[w, i], device_id=(x, y, 1 - c), device_id_type=MESH)
            for w in range(n) for i in range(N_CHIPS)]

    def start(ins, outs, sems):
        for cp in copies(ins, outs, sems):
            cp.start()

    def finish(ins, outs, sems):
        for cp in copies(ins, outs, sems):
            cp.wait()

    return _Comm(grads, [jax.ShapeDtypeStruct((N_CHIPS,) + g.shape[1:], g.dtype) for g in grads],
                 [pltpu.SemaphoreType.DMA((n, N_CHIPS)), pltpu.SemaphoreType.DMA((n, N_CHIPS))], start, finish)


def _window(ref, slab, win):
    if win is None:
        return ref.at[slab]
    if win[0] == "r":
        return ref.at[slab, pl.ds(win[1], win[2])]
    return ref.at[slab, slice(None), pl.ds(win[1], win[2])]


def _to_chips(parts, rows=None, into=None):
    n = len(parts)
    rows = rows or [None] * n

    def copies(ins, outs, sems):
        x, y, c = _my_place()
        send_sems, recv_sems, local_sems = sems
        mine = 2 * x + y
        chips = [(1 - x, y), (x, 1 - y), (1 - x, 1 - y)]
        remote = [pltpu.make_async_remote_copy(
            src_ref=_window(ins[w], 2 * cx + cy, rows[w]), dst_ref=_window(outs[w], mine, rows[w]),
            send_sem=send_sems.at[w, j], recv_sem=recv_sems.at[w, j], device_id=(cx, cy, c), device_id_type=MESH)
            for w in range(n) for j, (cx, cy) in enumerate(chips)]
        local = [pltpu.make_async_copy(_window(ins[w], mine, rows[w]), _window(outs[w], mine, rows[w]),
                                       local_sems.at[w]) for w in range(n)]
        return remote + local

    def start(ins, outs, sems):
        for cp in copies(ins, outs, sems):
            cp.start()

    def finish(ins, outs, sems):
        for cp in copies(ins, outs, sems):
            cp.wait()

    return _Comm(list(parts) + list(into or []), [jax.ShapeDtypeStruct(p.shape, p.dtype) for p in parts],
                 [pltpu.SemaphoreType.DMA((n, N_CHIPS - 1)), pltpu.SemaphoreType.DMA((n, N_CHIPS - 1)),
                  pltpu.SemaphoreType.DMA((n,))], start, finish,
                 aliases={n + w: w for w in range(n)} if into else None)


def _pair_sum(g, buf, name, first=0):
    _, r, c = g.shape
    tr, tc = _shard_tile(r, c, 4 * SHARD_TILE_ELEMS, 1024)
    core = (lax.axis_index("c") + first).astype(jnp.int32).reshape(1)

    def body(core_ref, g_ref, b_ref, o_ref):
        o_ref[...] = (g_ref[...].astype(F32) + b_ref[...].astype(F32)).astype(o_ref.dtype)

    blk = (1, tr, tc)
    return pl.pallas_call(
        body, grid_spec=pltpu.PrefetchScalarGridSpec(
            num_scalar_prefetch=1, grid=(N_CHIPS, r // tr, c // tc),
            in_specs=[pl.BlockSpec(blk, lambda i, j, l, core_ref: (2 * i + core_ref[0], j, l)),
                      pl.BlockSpec(blk, lambda i, j, l, core_ref: (i, j, l))],
            out_specs=pl.BlockSpec(blk, lambda i, j, l, core_ref: (i, j, l))),
        out_shape=jax.ShapeDtypeStruct(buf.shape, buf.dtype),
        compiler_params=_params(("parallel", "parallel", "parallel")), name=name)(core, g, buf)


def _all_reduce_pack(pack):
    r = pack.shape[0]

    def body(x_ref, out_ref, gath_ref, send_sems, recv_sems, local_sem):
        x, y, c = _my_place()
        me, sibling = (x, y, c), (x, y, 1 - c)
        chips = [(1 - x, y), (x, 1 - y), (1 - x, 1 - y)]

        def slab(place):
            return gath_ref.at[4 * place[0] + 2 * place[1] + place[2]]

        def copy(k, place, to, src=None):
            return pltpu.make_async_remote_copy(
                src_ref=slab(place) if src is None else src, dst_ref=slab(place),
                send_sem=send_sems.at[k], recv_sem=recv_sems.at[k], device_id=to, device_id_type=MESH)

        mine = pltpu.make_async_copy(x_ref, slab(me), local_sem)
        mine.start()
        first = [copy(0, me, sibling, src=x_ref)]
        first += [copy(1 + j, me, (*chip, c), src=x_ref) for j, chip in enumerate(chips)]
        for cp in first:
            cp.start()
        passed = [copy(4 + j, (*chip, c), sibling) for j, chip in enumerate(chips)]
        for j, chip in enumerate(chips):
            copy(1 + j, (*chip, c), me).wait_recv()
            passed[j].start()
        copy(0, sibling, me).wait_recv()
        for j, chip in enumerate(chips):
            copy(4 + j, (*chip, 1 - c), me).wait_recv()
        for cp in first + passed:
            cp.wait_send()
        mine.wait()
        acc = gath_ref[0]
        for i in range(1, N_DEV):
            acc = acc + gath_ref[i]
        out_ref[...] = acc

    vmem = pl.BlockSpec(memory_space=pltpu.VMEM)
    return pl.pallas_call(
        body, in_specs=[vmem], out_specs=vmem, out_shape=jax.ShapeDtypeStruct(pack.shape, F32),
        scratch_shapes=[pltpu.VMEM((N_DEV, r, LANES), F32), pltpu.SemaphoreType.DMA((7,)),
                        pltpu.SemaphoreType.DMA((7,)), pltpu.SemaphoreType.DMA],
        compiler_params=pltpu.CompilerParams(vmem_limit_bytes=VMEM_LIMIT), name="all_reduce_small")(pack)


def _adamw_math(w, g, m, v):
    m = ADAM_B1 * m + (1.0 - ADAM_B1) * g
    v = ADAM_B2 * v + (1.0 - ADAM_B2) * (g * g)
    m_hat = m / (1.0 - ADAM_B1 ** ADAM_STEP)
    v_hat = v / (1.0 - ADAM_B2 ** ADAM_STEP)
    delta = -ADAM_LR * (m_hat / (jnp.sqrt(v_hat) + ADAM_EPS) + ADAM_WD * w)
    return delta, m, v


def _adamw_shards(parts, opts, name, comm=None):
    r, c = opts[0][0].shape
    n_parts, k = parts[0].shape[0], len(parts)
    tr, tc = _shard_tile(r, c, SHARD_TILE_ELEMS // k)

    def body(*refs):
        ins, outs = refs[:4 * k], refs[4 * k:]
        for s in range(k):
            p_ref, w_ref, m_ref, v_ref = ins[4 * s:4 * s + 4]
            g_ref, d_ref, nm_ref, nv_ref = outs[4 * s:4 * s + 4]
            g = p_ref[0].astype(F32)
            for i in range(1, n_parts):
                g = g + p_ref[i].astype(F32)
            g_ref[...] = g
            d_ref[...], nm_ref[...], nv_ref[...] = _adamw_math(w_ref[...], g, m_ref[...], v_ref[...])

    spec = pl.BlockSpec((tr, tc), lambda i, j: (i, j))
    args = [a for p, o in zip(parts, opts) for a in (p,) + tuple(o)]
    outs, comm_outs = _call(
        body, grid=(r // tr, c // tc),
        in_specs=[pl.BlockSpec((n_parts, tr, tc), lambda i, j: (0, i, j)), spec, spec, spec] * k,
        out_specs=[spec] * (4 * k), out_shape=[jax.ShapeDtypeStruct((r, c), F32)] * (4 * k),
        sem=("parallel", "parallel"), name=name, args=args, comm=comm)
    return [outs[4 * s:4 * s + 4] for s in range(k)], comm_outs


def _adamw_pack(g, w, m, v):
    r, c = w.shape

    def body(g_ref, w_ref, m_ref, v_ref, d_ref, nm_ref, nv_ref):
        d_ref[...], nm_ref[...], nv_ref[...] = _adamw_math(w_ref[...], g_ref[...], m_ref[...], v_ref[...])

    return pl.pallas_call(
        body, in_specs=[_full((r, c))] * 4, out_specs=[_full((r, c))] * 3, grid=(1,),
        out_shape=[jax.ShapeDtypeStruct((r, c), F32)] * 3,
        compiler_params=_params(("arbitrary",)), name="adamw_small")(g, w, m, v)


def _cols_from_slabs(g):
    return jnp.transpose(g, (1, 0, 2)).reshape(g.shape[1], N_DEV * g.shape[2])


def _slabs_from_cols(w):
    r, c8 = w.shape
    return jnp.transpose(w.reshape(r, N_DEV, c8 // N_DEV), (1, 0, 2))


def _rows_from_slabs(g):
    return g.reshape(N_DEV * g.shape[1], g.shape[2])


def _slabs_from_rows(w):
    return w.reshape(N_DEV, w.shape[0] // N_DEV, w.shape[1])


def _compute_layout(gathered, ql, kvl, heads, sw):
    out = {}
    for k, g in gathered.items():
        if k == "w_in":
            lat = ql + kvl + QK_ROPE
            w_in_t = _rows_from_slabs(g)
            out["w_lat_t"] = jnp.pad(w_in_t[:lat], ((0, LANES - QK_ROPE), (0, 0)))
            out["w_uv_t"] = w_in_t[lat:lat + 2 * sw]
            out["w_g_t"] = w_in_t[lat + 2 * sw:]
        elif k == "w_uq":
            per_head = _cols_from_slabs(g).reshape(ql, heads, QK_NOPE + QK_ROPE)
            pad = HEAD_PAD - QK_NOPE - QK_ROPE
            out["w_uq"] = jnp.pad(per_head, ((0, 0), (0, 0), (0, pad))).reshape(ql, heads * HEAD_PAD)
        elif k in ("w_o_attn", "w_out", "w_down_ffn"):
            out[k.removesuffix("_ffn")] = _rows_from_slabs(g)
        else:
            out[k.removesuffix("_ffn")] = _cols_from_slabs(g)
    return out


_SMALL =["norm_mix_g", "b_gate", "q_norm_g", "kv_norm_g", "sgu_norm_g", "w_sgu", "b_sgu", "norm_ffn_g", "norm_final_g"]
_BIG = ["w_in", "w_uq", "w_ukv", "w_o_attn", "w_o_sgu", "w_out", "w_gate_ffn", "w_up_ffn", "w_down_ffn"]
_TRANSPOSED = ("w_in", "w_gate_ffn", "w_up_ffn")
_ORDER = ["norm_mix_g", "w_in", "b_gate", "q_norm_g", "w_uq", "kv_norm_g", "w_ukv", "w_o_attn", "sgu_norm_g", "w_sgu",
          "b_sgu", "w_o_sgu", "w_out", "norm_ffn_g", "w_gate_ffn", "w_up_ffn", "w_down_ffn", "norm_final_g"]


def _pack_rows(parts):
    rows, sizes = [], []
    for p in parts:
        flat = p.reshape(-1)
        n = flat.shape[0]
        padded = -(-n // (SUBLANES * LANES)) * (SUBLANES * LANES)
        rows.append(jnp.pad(flat, (0, padded - n)).reshape(padded // LANES, LANES))
        sizes.append((n, padded // LANES))
    return jnp.concatenate(rows, axis=0), sizes


def _unpack_rows(pack, sizes, shapes):
    out, r0 = [], 0
    for (n, nr), shp in zip(sizes, shapes):
        out.append(pack[r0:r0 + nr].reshape(-1)[:n].reshape(shp))
        r0 += nr
    return out


def kernel(x, positions, norm_mix_g, w_in, b_gate, q_norm_g, w_uq, kv_norm_g, w_ukv, w_o_attn, sgu_norm_g, w_sgu, b_sgu, w_o_sgu, w_out, norm_ffn_g, w_gate_ffn, w_up_ffn, w_down_ffn, norm_final_g, loss_target, m_norm_mix_g, m_w_in, m_b_gate, m_q_norm_g, m_w_uq, m_kv_norm_g, m_w_ukv, m_w_o_attn, m_sgu_norm_g, m_w_sgu, m_b_sgu, m_w_o_sgu, m_w_out, m_norm_ffn_g, m_w_gate_ffn, m_w_up_ffn, m_w_down_ffn, m_norm_final_g, v_norm_mix_g, v_w_in, v_b_gate, v_q_norm_g, v_w_uq, v_kv_norm_g, v_w_ukv, v_w_o_attn, v_sgu_norm_g, v_w_sgu, v_b_sgu, v_w_o_sgu, v_w_out, v_norm_ffn_g, v_w_gate_ffn, v_w_up_ffn, v_w_down_ffn, v_norm_final_g):
    wts = dict(norm_mix_g=norm_mix_g, w_in=w_in, b_gate=b_gate, q_norm_g=q_norm_g, w_uq=w_uq, kv_norm_g=kv_norm_g,
               w_ukv=w_ukv, w_o_attn=w_o_attn, sgu_norm_g=sgu_norm_g, w_sgu=w_sgu, b_sgu=b_sgu, w_o_sgu=w_o_sgu,
               w_out=w_out, norm_ffn_g=norm_ffn_g, w_gate_ffn=w_gate_ffn, w_up_ffn=w_up_ffn, w_down_ffn=w_down_ffn,
               norm_final_g=norm_final_g)
    mom = dict(norm_mix_g=m_norm_mix_g, w_in=m_w_in, b_gate=m_b_gate, q_norm_g=m_q_norm_g, w_uq=m_w_uq,
               kv_norm_g=m_kv_norm_g, w_ukv=m_w_ukv, w_o_attn=m_w_o_attn, sgu_norm_g=m_sgu_norm_g, w_sgu=m_w_sgu,
               b_sgu=m_b_sgu, w_o_sgu=m_w_o_sgu, w_out=m_w_out, norm_ffn_g=m_norm_ffn_g, w_gate_ffn=m_w_gate_ffn,
               w_up_ffn=m_w_up_ffn, w_down_ffn=m_w_down_ffn, norm_final_g=m_norm_final_g)
    var = dict(norm_mix_g=v_norm_mix_g, w_in=v_w_in, b_gate=v_b_gate, q_norm_g=v_q_norm_g, w_uq=v_w_uq,
               kv_norm_g=v_kv_norm_g, w_ukv=v_w_ukv, w_o_attn=v_w_o_attn, sgu_norm_g=v_sgu_norm_g, w_sgu=v_w_sgu,
               b_sgu=v_b_sgu, w_o_sgu=v_w_o_sgu, w_out=v_w_out, norm_ffn_g=v_norm_ffn_g, w_gate_ffn=v_w_gate_ffn,
               w_up_ffn=v_w_up_ffn, w_down_ffn=v_w_down_ffn, norm_final_g=v_norm_final_g)

    t, d = x.shape[1], x.shape[2]
    ql, kvl = q_norm_g.shape[1], kv_norm_g.shape[1]
    heads = (w_uq.shape[2] * N_DEV) // (QK_NOPE + QK_ROPE)
    sw = sgu_norm_g.shape[1]

    def shard(a, k):
        return a[0].T if k in _TRANSPOSED else a[0]

    def unshard(a, k):
        return (a.T if k in _TRANSPOSED else a).reshape(wts[k].shape)

    opt = {k: (shard(wts[k], k), shard(mom[k], k), shard(var[k], k)) for k in _BIG}
    shards = {k: opt[k][0].astype(BF16) for k in _BIG}
    small = {
        "norm_mix_g": norm_mix_g, "b_gate": b_gate, "q_norm_g": q_norm_g, "kv_norm_g": kv_norm_g,
        "sgu_norm_g": sgu_norm_g, "w_sgu": w_sgu[0], "b_sgu_col": b_sgu[0][:, :, None], "norm_ffn_g": norm_ffn_g,
        "norm_final_g": norm_final_g[None, :],
    }

    loss_row, grad_x, gs, updates = _local_step(x[0], positions.reshape(t, 1), loss_target[0], small, shards, opt)
    grads, deltas, new_m, new_v = {}, {}, {}, {}
    for k in _BIG:
        grads[k], deltas[k], new_m[k], new_v[k] = (unshard(a, k) for a in updates[k])

    small_grads = [gs["norm_mix_g"], gs["b_gate"], gs["q_norm_g"], gs["kv_norm_g"], gs["sgu_norm_g"], gs["w_sgu"],
                   gs["b_sgu_col"], gs["norm_ffn_g"], gs["norm_final_g"]]
    pack, sizes = _pack_rows([loss_row] + small_grads)
    total = _all_reduce_pack(pack)
    shapes = [(1, LANES)] + [wts[k].shape for k in _SMALL]
    unpacked = _unpack_rows(total, sizes, shapes)
    loss = unpacked[0][0, 0]
    for k, g in zip(_SMALL, unpacked[1:]):
        grads[k] = g
    g_pack = total[sizes[0][1]:]
    w_pack, _ = _pack_rows([wts[k] for k in _SMALL])
    m_pack, _ = _pack_rows([mom[k] for k in _SMALL])
    v_pack, _ = _pack_rows([var[k] for k in _SMALL])
    d_pack, nm_pack, nv_pack = _adamw_pack(g_pack, w_pack, m_pack, v_pack)
    small_shapes = [wts[k].shape for k in _SMALL]
    for store, pk in ((deltas, d_pack), (new_m, nm_pack), (new_v, nv_pack)):
        for k, a in zip(_SMALL, _unpack_rows(pk, sizes[1:], small_shapes)):
            store[k] = a

    return (loss, grad_x[None], *[grads[k] for k in _ORDER], *[deltas[k] for k in _ORDER],
            *[new_m[k] for k in _ORDER], *[new_v[k] for k in _ORDER])
```

```python
import functools
import math

import jax
import jax.numpy as jnp
from jax import lax
from jax.experimental import pallas as pl
from jax.experimental.pallas import tpu as pltpu

F32 = jnp.float32
BF16 = jnp.bfloat16

N_DEV = 8
QK_NOPE = 128
QK_ROPE = 64
V_HEAD = 128
HEAD_PAD = 256
ROPE_THETA = 10000.0
CHUNK = 128
SGU_GROUP = 128
RMS_EPS = 1e-6
LANES = 128
SUBLANES = 8

ADAM_LR = 0.001
ADAM_B1 = 0.9
ADAM_B2 = 0.999
ADAM_EPS = 1e-08
ADAM_WD = 0.01
ADAM_STEP = 10

VMEM_LIMIT = 48 * 1024 * 1024
MM_TILE = (2048, 512, 2048)
MM_TILE_TA = (512, 2048)
ATTN_TILE = 512
GATHER_STRIPS = 8
HEADS_PER_STEP = (4, 4)
ROW_KERNEL_BYTES = 40 * 1024 * 1024
SHARD_TILE_ELEMS = 256 * 1024
SLABS_PER_STEP = 2
TAIL_SPLIT = (3, 8)
NEG_BIG = -1e30
MESH = pl.DeviceIdType.MESH


def _pick(n, target, mult=LANES):
    best = None
    d = mult
    while d <= min(n, target):
        if n % d == 0:
            best = d
        d += mult
    return best or n


def _row_tile(t, width, n_blocks, mult=2 * SUBLANES):
    return _pick(t, max(mult, ROW_KERNEL_BYTES // (3 * n_blocks * width * 4)), mult)


def _shard_tile(r, c, elems=SHARD_TILE_ELEMS, max_rows=256):
    tr = _pick(r, max_rows, 2 * SUBLANES)
    return tr, _pick(c, max(LANES, elems // tr))


def _params(sem):
    return pltpu.CompilerParams(dimension_semantics=sem, vmem_limit_bytes=VMEM_LIMIT)


def _full(shape):
    nd = len(shape)
    return pl.BlockSpec(shape, lambda *_: (0,) * nd)


def _rows(tr, w, cb=0):
    return pl.BlockSpec((tr, w), lambda i: (i, cb))


class _Comm:
    def __init__(self, ins, out_shapes, sems, start, finish, aliases=None):
        self.ins, self.out_shapes, self.sems, self.start, self.finish = list(ins), list(out_shapes), list(sems), start, finish
        self.aliases = dict(aliases or {})


def _call(body, *, grid, in_specs, out_specs, out_shape, scratch_shapes=(), sem, name, args, comm=None):
    if comm is None:
        outs = pl.pallas_call(body, grid=grid, in_specs=list(in_specs), out_specs=list(out_specs),
                              out_shape=list(out_shape), scratch_shapes=list(scratch_shapes),
                              compiler_params=_params(sem), name=name)(*args)
        return list(outs), []
    n_in, n_out, n_sc = len(in_specs), len(out_shape), len(scratch_shapes)
    nci, nco = len(comm.ins), len(comm.out_shapes)

    def hosted(*refs):
        ins, refs = refs[:n_in], refs[n_in:]
        cins, refs = refs[:nci], refs[nci:]
        outs, refs = refs[:n_out], refs[n_out:]
        couts, refs = refs[:nco], refs[nco:]
        scratch, csems = refs[:n_sc], refs[n_sc:]
        ids = [pl.program_id(i) for i in range(len(grid))]
        first = functools.reduce(jnp.logical_and, [i == 0 for i in ids])
        last = functools.reduce(jnp.logical_and, [i == g - 1 for i, g in zip(ids, grid)])

        @pl.when(first)
        def _():
            comm.start(cins, couts, csems)

        body(*ins, *outs, *scratch)

        @pl.when(last)
        def _():
            comm.finish(cins, couts, csems)

    any_spec = pl.BlockSpec(memory_space=pl.ANY)
    res = pl.pallas_call(
        hosted, grid=grid, in_specs=list(in_specs) + [any_spec] * nci, out_specs=list(out_specs) + [any_spec] * nco,
        out_shape=list(out_shape) + comm.out_shapes, scratch_shapes=list(scratch_shapes) + comm.sems,
        input_output_aliases={n_in + i: n_out + o for i, o in comm.aliases.items()},
        compiler_params=pltpu.CompilerParams(dimension_semantics=("arbitrary",) * len(grid),
                                             vmem_limit_bytes=VMEM_LIMIT, has_side_effects=True),
        name=name)(*args, *comm.ins)
    return list(res[:n_out]), list(res[n_out:])


def _swiglu_grads(g, u, d):
    s = 1.0 / (1.0 + jnp.exp(-g))
    return (d * u * (s * (1.0 + g * (1.0 - s)))).astype(BF16), (d * (g * s)).astype(BF16)


def _mm(a, b, *, ta=False, tb=False, add=None, out_dtype=F32, tm=None, tn=None, tk=None, name, comm=None,
        slab=None, a_slab0=0, swiglu=None, rope=None):
    sq = None
    if ta:
        tm, tn = tm or MM_TILE_TA[0], tn or MM_TILE_TA[1]
    if slab is None:
        m, k = (a.shape[1], a.shape[0]) if ta else a.shape
        n = b.shape[0] if tb else b.shape[1]
        assert k == (b.shape[1] if tb else b.shape[0]), (a.shape, b.shape, ta, tb)
        tm, tn, tk = _pick(m, tm or MM_TILE[0]), _pick(n, tn or MM_TILE[1]), _pick(k, tk or MM_TILE[2])
        if rope is not None:
            tn = _pick(n, max(tn, HEAD_PAD), HEAD_PAD)
        grid = (m // tm, n // tn, k // tk)
        a_spec = pl.BlockSpec((tk, tm), lambda i, j, kk: (kk, i)) if ta else pl.BlockSpec((tm, tk), lambda i, j, kk: (i, kk))
        b_spec = pl.BlockSpec((tn, tk), lambda i, j, kk: (j, kk)) if tb else pl.BlockSpec((tk, tn), lambda i, j, kk: (kk, j))
        o_spec, o_shape = pl.BlockSpec((tm, tn), lambda i, j, kk: (i, j)), (m, n)
    elif slab == "n":
        m, k = (a.shape[1], a.shape[0]) if ta else a.shape
        s, c = b.shape[0], (b.shape[1] if tb else b.shape[2])
        assert k == (b.shape[2] if tb else b.shape[1]), (a.shape, b.shape, ta, tb)
        tm, tn, tk = _pick(m, tm or MM_TILE[0]), c, _pick(k, tk or MM_TILE[2])
        grid = (m // tm, s, k // tk)
        a_spec = pl.BlockSpec((tk, tm), lambda i, j, kk: (kk, i)) if ta else pl.BlockSpec((tm, tk), lambda i, j, kk: (i, kk))
        b_spec = (pl.BlockSpec((sq, c, tk), lambda i, j, kk: (j, 0, kk)) if tb
                  else pl.BlockSpec((sq, tk, c), lambda i, j, kk: (j, kk, 0)))
        o_spec, o_shape = pl.BlockSpec((sq, tm, c), lambda i, j, kk: (j, i, 0)), (s, m, c)
    elif slab == "m":
        assert ta and not tb
        s, k, c = a.shape
        n = b.shape[1]
        assert k == b.shape[0], (a.shape, b.shape)
        tm, tn, tk = c, _pick(n, tn or MM_TILE[1]), _pick(k, tk or MM_TILE[2])
        grid = (s, n // tn, k // tk)
        a_spec = pl.BlockSpec((sq, tk, c), lambda i, j, kk: (i, kk, 0))
        b_spec = pl.BlockSpec((tk, tn), lambda i, j, kk: (kk, j))
        o_spec, o_shape = pl.BlockSpec((sq, c, tn), lambda i, j, kk: (i, 0, j)), (s, c, n)
    else:
        assert slab == "k" and not ta
        s, c = b.shape[0], (b.shape[2] if tb else b.shape[1])
        m, n = a.shape[1], (b.shape[1] if tb else b.shape[2])
        assert a.shape[2] == c and a.shape[0] >= a_slab0 + s, (a.shape, b.shape, a_slab0)
        tm, tn, tk = _pick(m, tm or MM_TILE[0]), _pick(n, tn or MM_TILE[1]), c
        per_step = SLABS_PER_STEP if (s % SLABS_PER_STEP == 0 and a_slab0 % SLABS_PER_STEP == 0) else 1
        first = a_slab0 // per_step
        grid = (m // tm, n // tn, s // per_step)
        a_spec = pl.BlockSpec((per_step, tm, c), lambda i, j, kk: (kk + first, i, 0))
        b_spec = (pl.BlockSpec((per_step, tn, c), lambda i, j, kk: (kk, j, 0)) if tb
                  else pl.BlockSpec((per_step, c, tn), lambda i, j, kk: (kk, 0, j)))
        o_spec, o_shape = pl.BlockSpec((tm, tn), lambda i, j, kk: (i, j)), (m, n)
    nk = grid[2]
    dims = (((0 if ta else 1,), (1 if tb else 0,)), ((), ()))

    def product(a_ref, b_ref):
        if slab != "k":
            return lax.dot_general(a_ref[...].astype(BF16), b_ref[...].astype(BF16), dims, preferred_element_type=F32)
        r = None
        for u in range(a_ref.shape[0]):
            p = lax.dot_general(a_ref[u].astype(BF16), b_ref[u].astype(BF16), dims, preferred_element_type=F32)
            r = p if r is None else r + p
        return r

    if swiglu is not None:
        assert slab == "n" and add is None
        o_block = pl.BlockSpec((2, sq, tm, c), lambda i, j, kk: (0, j, i, 0))
        o_shape, out_dtype = (2,) + o_shape, BF16

    if rope is not None:
        assert slab is None and add is None and swiglu is None and tn % HEAD_PAD == 0
        out_dtype = BF16
    extras = tuple(swiglu or ()) + tuple(rope or ())

    def body(*refs):
        a_ref, b_ref = refs[:2]
        add_ref = refs[2] if add is not None else None
        x0_ref, x1_ref = refs[2:4] if extras else (None, None)
        o_ref = refs[2 + (add is not None) + len(extras)]
        acc_ref = refs[-1] if nk > 1 else None

        def finish(r):
            if swiglu is not None:
                o_ref[0], o_ref[1] = _swiglu_grads(x0_ref[...], x1_ref[...], r)
                return
            if rope is not None:
                cos, sin = x0_ref[...], x1_ref[...]
                for h in range(tn // HEAD_PAD):
                    lo = h * HEAD_PAD
                    o_ref[:, lo:lo + QK_NOPE] = r[:, lo:lo + QK_NOPE].astype(BF16)
                    o_ref[:, lo + QK_NOPE:lo + HEAD_PAD] = _rope(r[:, lo + QK_NOPE:lo + HEAD_PAD], cos, sin).astype(BF16)
                return
            if add_ref is not None:
                r = r + add_ref[...].astype(F32)
            o_ref[...] = r.astype(o_ref.dtype)

        if nk == 1:
            finish(product(a_ref, b_ref))
            return
        kk = pl.program_id(2)

        @pl.when(kk == 0)
        def _():
            acc_ref[...] = product(a_ref, b_ref)

        if nk > 2:
            @pl.when(jnp.logical_and(kk > 0, kk < nk - 1))
            def _():
                acc_ref[...] += product(a_ref, b_ref)

        @pl.when(kk == nk - 1)
        def _():
            finish(acc_ref[...] + product(a_ref, b_ref))

    in_specs = [a_spec, b_spec] + ([o_spec] if add is not None else []) + ([o_spec] * 2 if swiglu is not None else [])
    if rope is not None:
        in_specs += [pl.BlockSpec((tm, LANES), lambda i, j, kk: (i, 0))] * 2
    args = (a, b) + ((add,) if add is not None else ()) + extras
    if swiglu is not None:
        o_spec = o_block
    outs, comm_outs = _call(
        body, grid=grid, in_specs=in_specs, out_specs=[o_spec],
        out_shape=[jax.ShapeDtypeStruct(o_shape, out_dtype)],
        scratch_shapes=[pltpu.VMEM((tm, tn), F32)] if nk > 1 else [],
        sem=("parallel", "parallel", "arbitrary"), name=name, args=args, comm=comm)
    return outs[0] if comm is None else (outs[0], comm_outs)


def _rms_scale(x):
    return lax.rsqrt(jnp.mean(x * x, axis=-1, keepdims=True) + RMS_EPS)


def _rms_bwd(xhat, r, g, dy):
    t = dy * g
    dx = r * (t - xhat * jnp.mean(t * xhat, axis=-1, keepdims=True))
    return dx, dy * xhat


_GELU_C = math.sqrt(2.0 / math.pi)


def _gelu(x):
    return x * (0.5 * (1.0 + jnp.tanh(_GELU_C * (x + 0.044715 * (x * x * x)))))


def _gelu_and_grad(x):
    t = jnp.tanh(_GELU_C * (x + 0.044715 * (x * x * x)))
    cdf = 0.5 * (1.0 + t)
    return x * cdf, cdf + x * (0.5 * (1.0 - t * t) * (_GELU_C * (1.0 + 3.0 * 0.044715 * (x * x))))


def _sigmoid(x):
    return 1.0 / (1.0 + jnp.exp(-x))


def _swap_halves(x):
    lane = lax.broadcasted_iota(jnp.int32, x.shape, 1)
    first = (lane % QK_ROPE) < (QK_ROPE // 2)
    return jnp.where(first, pltpu.roll(x, LANES - QK_ROPE // 2, 1), pltpu.roll(x, QK_ROPE // 2, 1))


def _rope(x, cos, sin_signed):
    return x * cos + _swap_halves(x) * sin_signed


def _rope_bwd(d, cos, sin_signed):
    return d * cos + _swap_halves(d * sin_signed)


def _rope_tables(pos_col, inv_freq_row, sign_row):
    t = pos_col.shape[0]
    tr = _pick(t, 512, SUBLANES)

    def body(p_ref, f_ref, s_ref, cos_ref, sin_ref):
        ang = p_ref[...].astype(F32) * f_ref[...]
        cos_ref[...] = jnp.cos(ang)
        sin_ref[...] = jnp.sin(ang) * s_ref[...]

    return pl.pallas_call(
        body, grid=(t // tr,), in_specs=[_rows(tr, 1), _full((1, LANES)), _full((1, LANES))],
        out_specs=[_rows(tr, LANES), _rows(tr, LANES)],
        out_shape=[jax.ShapeDtypeStruct((t, LANES), F32)] * 2,
        compiler_params=_params(("parallel",)), name="rope_tables")(pos_col, inv_freq_row, sign_row)


def _norm_fwd(x, g, name):
    t, d = x.shape
    tr = _row_tile(t, d, 2)

    def body(x_ref, g_ref, y_ref):
        xv = x_ref[...]
        y_ref[...] = (xv * _rms_scale(xv) * g_ref[...]).astype(BF16)

    return pl.pallas_call(
        body, grid=(t // tr,), in_specs=[_rows(tr, d), _full((1, d))], out_specs=_rows(tr, d),
        out_shape=jax.ShapeDtypeStruct((t, d), BF16), compiler_params=_params(("parallel",)), name=name)(x, g)


def _lat_fwd(z_lat, qg, kvg, cos, sin, ql, kvl):
    t = z_lat.shape[0]
    tr = _row_tile(t, z_lat.shape[1], 2)

    def body(z_ref, qg_ref, kvg_ref, cos_ref, sin_ref, qn_ref, kvn_ref, kpe_ref):
        q = z_ref[:, 0:ql]
        qn_ref[...] = (q * _rms_scale(q) * qg_ref[...]).astype(BF16)
        kv = z_ref[:, ql:ql + kvl]
        kvn_ref[...] = (kv * _rms_scale(kv) * kvg_ref[...]).astype(BF16)
        kpe_ref[...] = _rope(z_ref[:, ql + kvl:ql + kvl + LANES], cos_ref[...], sin_ref[...]).astype(BF16)

    w = z_lat.shape[1]
    return pl.pallas_call(
        body, grid=(t // tr,),
        in_specs=[_rows(tr, w), _full((1, ql)), _full((1, kvl)), _rows(tr, LANES), _rows(tr, LANES)],
        out_specs=[_rows(tr, ql), _rows(tr, kvl), _rows(tr, LANES)],
        out_shape=[jax.ShapeDtypeStruct((t, ql), BF16), jax.ShapeDtypeStruct((t, kvl), BF16),
                   jax.ShapeDtypeStruct((t, LANES), BF16)],
        compiler_params=_params(("parallel",)), name="lat_fwd")(z_lat, qg, kvg, cos, sin)


def _tril_mask():
    r = lax.broadcasted_iota(jnp.int32, (CHUNK, CHUNK), 0)
    c = lax.broadcasted_iota(jnp.int32, (CHUNK, CHUNK), 1)
    return r >= c


def _sgu_fwd(z_uv, gs, ws, b_col):
    t = z_uv.shape[0]
    sw = z_uv.shape[1] // 2
    groups = sw // SGU_GROUP
    tr = _pick(t, 256, CHUNK)

    def body(u_ref, v_ref, gs_ref, ws_ref, b_ref, o_ref):
        v = _gelu(v_ref[...])
        vn = (v * _rms_scale(v) * gs_ref[...]).astype(BF16)
        tri = _tril_mask()
        for g in range(groups):
            wg = jnp.where(tri, ws_ref[g], 0.0).astype(BF16)
            cols = slice(g * SGU_GROUP, (g + 1) * SGU_GROUP)
            for c in range(tr // CHUNK):
                rows = slice(c * CHUNK, (c + 1) * CHUNK)
                mixed = jnp.dot(wg, vn[rows, cols], preferred_element_type=F32) + b_ref[g]
                o_ref[rows, cols] = (_gelu(u_ref[rows, cols]) * mixed).astype(BF16)

    return pl.pallas_call(
        body, grid=(t // tr,),
        in_specs=[_rows(tr, sw, 0), _rows(tr, sw, 1), _full((1, sw)), _full(ws.shape), _full(b_col.shape)],
        out_specs=_rows(tr, sw), out_shape=jax.ShapeDtypeStruct((t, sw), BF16),
        compiler_params=_params(("parallel",)), name="sgu_fwd")(z_uv, z_uv, gs, ws, b_col)


def _merge_fwd(y_attn, y_sgu, z_g, b_gate, comm=None):
    t, d = y_attn.shape
    tr = _row_tile(t, d, 5)

    def body(ya_ref, ys_ref, g0_ref, g1_ref, b0_ref, b1_ref, o_ref):
        g0 = _sigmoid(g0_ref[...] + b0_ref[...])
        g1 = _sigmoid(g1_ref[...] + b1_ref[...])
        o_ref[...] = (g0 * ya_ref[...] + g1 * ys_ref[...]).astype(BF16)

    bspec0 = pl.BlockSpec((1, d), lambda i: (0, 0))
    bspec1 = pl.BlockSpec((1, d), lambda i: (0, 1))
    outs, comm_outs = _call(
        body, grid=(t // tr,),
        in_specs=[_rows(tr, d), _rows(tr, d), _rows(tr, d, 0), _rows(tr, d, 1), bspec0, bspec1],
        out_specs=[_rows(tr, d)], out_shape=[jax.ShapeDtypeStruct((t, d), BF16)],
        sem=("parallel",), name="merge_fwd", args=(y_attn, y_sgu, z_g, z_g, b_gate, b_gate), comm=comm)
    return outs[0], comm_outs


def _swiglu_fwd(gate, up, comm=None):
    t, f = gate.shape
    tr = _row_tile(t, f, 3)

    def body(g_ref, u_ref, o_ref):
        g = g_ref[...]
        o_ref[...] = (g * _sigmoid(g) * u_ref[...]).astype(BF16)

    outs, comm_outs = _call(
        body, grid=(t // tr,), in_specs=[_rows(tr, f), _rows(tr, f)], out_specs=[_rows(tr, f)],
        out_shape=[jax.ShapeDtypeStruct((t, f), BF16)], sem=("parallel",), name="swiglu_fwd", args=(gate, up), comm=comm)
    return outs[0], comm_outs


def _loss_head(h2, g, target):
    t, d = h2.shape
    tr = _row_tile(t, d, 3)

    def body(h_ref, g_ref, t_ref, loss_ref, dh_ref, dhb_ref, dg_ref):
        @pl.when(pl.program_id(0) == 0)
        def _():
            loss_ref[...] = jnp.zeros_like(loss_ref)
            dg_ref[...] = jnp.zeros_like(dg_ref)

        h = h_ref[...]
        r = _rms_scale(h)
        hhat = h * r
        gv = g_ref[...]
        err = hhat * gv - t_ref[...]
        loss_ref[...] += jnp.full(loss_ref.shape, 0.5 * jnp.sum(jnp.mean(err * err, axis=-1)), F32)
        dx, dg_rows = _rms_bwd(hhat, r, gv, err * (1.0 / d))
        dh_ref[...] = dx
        dhb_ref[...] = dx.astype(BF16)
        dg_ref[...] += jnp.sum(dg_rows, axis=0, keepdims=True)

    return pl.pallas_call(
        body, grid=(t // tr,), in_specs=[_rows(tr, d), _full((1, d)), _rows(tr, d)],
        out_specs=[_full((1, LANES)), _rows(tr, d), _rows(tr, d), _full((1, d))],
        out_shape=[jax.ShapeDtypeStruct((1, LANES), F32), jax.ShapeDtypeStruct((t, d), F32),
                   jax.ShapeDtypeStruct((t, d), BF16), jax.ShapeDtypeStruct((1, d), F32)],
        compiler_params=_params(("arbitrary",)), name="loss_head")(h2, g, target)


def _norm_bwd(x, g, dy, resid, name, comm=None):
    t, d = x.shape
    tr = _row_tile(t, d, 5)

    def body(x_ref, g_ref, dy_ref, r_ref, dx_ref, dxb_ref, dg_ref):
        @pl.when(pl.program_id(0) == 0)
        def _():
            dg_ref[...] = jnp.zeros_like(dg_ref)

        xv = x_ref[...]
        r = _rms_scale(xv)
        dx, dg_rows = _rms_bwd(xv * r, r, g_ref[...], dy_ref[...])
        dx = r_ref[...] + dx
        dx_ref[...] = dx
        dxb_ref[...] = dx.astype(BF16)
        dg_ref[...] += jnp.sum(dg_rows, axis=0, keepdims=True)

    outs, comm_outs = _call(
        body, grid=(t // tr,), in_specs=[_rows(tr, d), _full((1, d)), _rows(tr, d), _rows(tr, d)],
        out_specs=[_rows(tr, d), _rows(tr, d), _full((1, d))],
        out_shape=[jax.ShapeDtypeStruct((t, d), F32), jax.ShapeDtypeStruct((t, d), BF16),
                   jax.ShapeDtypeStruct((1, d), F32)],
        sem=("arbitrary",), name=name, args=(x, g, dy, resid), comm=comm)
    return (outs[0], outs[1], outs[2]) if comm is None else (outs[0], outs[1], outs[2], comm_outs)


def _merge_bwd(dmerged, y_attn, y_sgu, z_g, b_gate):
    t, d = y_attn.shape
    tr = _row_tile(t, d, 7)

    def body(dm_ref, ya_ref, ys_ref, g0_ref, g1_ref, b0_ref, b1_ref, dya_ref, dys_ref, dz_ref, db_ref):
        @pl.when(pl.program_id(0) == 0)
        def _():
            db_ref[...] = jnp.zeros_like(db_ref)

        dm = dm_ref[...]
        g0 = _sigmoid(g0_ref[...] + b0_ref[...])
        g1 = _sigmoid(g1_ref[...] + b1_ref[...])
        dya_ref[...] = (dm * g0).astype(BF16)
        dys_ref[...] = (dm * g1).astype(BF16)
        dl0 = dm * ya_ref[...] * (g0 * (1.0 - g0))
        dl1 = dm * ys_ref[...] * (g1 * (1.0 - g1))
        dz_ref[:, 0:d] = dl0.astype(BF16)
        dz_ref[:, d:2 * d] = dl1.astype(BF16)
        db_ref[:, 0:d] += jnp.sum(dl0, axis=0, keepdims=True)
        db_ref[:, d:2 * d] += jnp.sum(dl1, axis=0, keepdims=True)

    bspec0 = pl.BlockSpec((1, d), lambda i: (0, 0))
    bspec1 = pl.BlockSpec((1, d), lambda i: (0, 1))
    return pl.pallas_call(
        body, grid=(t // tr,),
        in_specs=[_rows(tr, d), _rows(tr, d), _rows(tr, d), _rows(tr, d, 0), _rows(tr, d, 1), bspec0, bspec1],
        out_specs=[_rows(tr, d), _rows(tr, d), _rows(tr, 2 * d), _full((1, 2 * d))],
        out_shape=[jax.ShapeDtypeStruct((t, d), BF16), jax.ShapeDtypeStruct((t, d), BF16),
                   jax.ShapeDtypeStruct((t, 2 * d), BF16), jax.ShapeDtypeStruct((1, 2 * d), F32)],
        compiler_params=_params(("arbitrary",)), name="merge_bwd")(dmerged, y_attn, y_sgu, z_g, z_g, b_gate, b_gate)


def _sgu_bwd(z_uv, ds_out, gs, ws, b_col):
    t = z_uv.shape[0]
    sw = z_uv.shape[1] // 2
    groups = sw // SGU_GROUP
    tr = _pick(t, 256, CHUNK)

    def body(u_ref, v_ref, d_ref, gs_ref, ws_ref, b_ref, dz_ref, dws_ref, db_ref, dgs_ref, dvn_ref):
        @pl.when(pl.program_id(0) == 0)
        def _():
            dws_ref[...] = jnp.zeros_like(dws_ref)
            db_ref[...] = jnp.zeros_like(db_ref)
            dgs_ref[...] = jnp.zeros_like(dgs_ref)

        v, dgelu_v = _gelu_and_grad(v_ref[...])
        r = _rms_scale(v)
        vhat = v * r
        gsv = gs_ref[...]
        vn = (vhat * gsv).astype(BF16)
        tri = _tril_mask()
        for g in range(groups):
            wg = jnp.where(tri, ws_ref[g], 0.0).astype(BF16)
            cols = slice(g * SGU_GROUP, (g + 1) * SGU_GROUP)
            for c in range(tr // CHUNK):
                rows = slice(c * CHUNK, (c + 1) * CHUNK)
                vn_cg = vn[rows, cols]
                mixed = jnp.dot(wg, vn_cg, preferred_element_type=F32) + b_ref[g]
                u, dgelu_u = _gelu_and_grad(u_ref[rows, cols])
                dso = d_ref[rows, cols]
                dz_ref[rows, cols] = (dso * mixed * dgelu_u).astype(BF16)
                dmixed = dso * u
                db_ref[g] += jnp.sum(dmixed, axis=1, keepdims=True)
                dmixed_b = dmixed.astype(BF16)
                dws_ref[g] += jnp.where(
                    tri, lax.dot_general(dmixed_b, vn_cg, (((1,), (1,)), ((), ())), preferred_element_type=F32), 0.0)
                dvn_ref[rows, cols] = lax.dot_general(wg, dmixed_b, (((0,), (0,)), ((), ())), preferred_element_type=F32)
        dvn = dvn_ref[...]
        dv, dgs_rows = _rms_bwd(vhat, r, gsv, dvn)
        dz_ref[:, sw:2 * sw] = (dv * dgelu_v).astype(BF16)
        dgs_ref[...] += jnp.sum(dgs_rows, axis=0, keepdims=True)

    return pl.pallas_call(
        body, grid=(t // tr,),
        in_specs=[_rows(tr, sw, 0), _rows(tr, sw, 1), _rows(tr, sw), _full((1, sw)), _full(ws.shape), _full(b_col.shape)],
        out_specs=[_rows(tr, 2 * sw), _full(ws.shape), _full(b_col.shape), _full((1, sw))],
        out_shape=[jax.ShapeDtypeStruct((t, 2 * sw), BF16), jax.ShapeDtypeStruct(ws.shape, F32),
                   jax.ShapeDtypeStruct(b_col.shape, F32), jax.ShapeDtypeStruct((1, sw), F32)],
        scratch_shapes=[pltpu.VMEM((tr, sw), F32)],
        compiler_params=_params(("arbitrary",)), name="sgu_bwd")(z_uv, z_uv, ds_out, gs, ws, b_col)


def _lat_bwd(z_lat, qg, kvg, dqn, dkvn, dkpe_heads, cos, sin, ql, kvl):
    t, w = z_lat.shape
    heads = dkpe_heads.shape[0]
    tr = _row_tile(t, w + heads * LANES, 3)

    def body(z_ref, qg_ref, kvg_ref, dq_ref, dkv_ref, dk_ref, cos_ref, sin_ref, dz_ref, dqg_ref, dkvg_ref):
        @pl.when(pl.program_id(0) == 0)
        def _():
            dqg_ref[...] = jnp.zeros_like(dqg_ref)
            dkvg_ref[...] = jnp.zeros_like(dkvg_ref)

        q = z_ref[:, 0:ql]
        r = _rms_scale(q)
        dx, dg_rows = _rms_bwd(q * r, r, qg_ref[...], dq_ref[...])
        dz_ref[:, 0:ql] = dx.astype(BF16)
        dqg_ref[...] += jnp.sum(dg_rows, axis=0, keepdims=True)
        kv = z_ref[:, ql:ql + kvl]
        r = _rms_scale(kv)
        dx, dg_rows = _rms_bwd(kv * r, r, kvg_ref[...], dkv_ref[...])
        dz_ref[:, ql:ql + kvl] = dx.astype(BF16)
        dkvg_ref[...] += jnp.sum(dg_rows, axis=0, keepdims=True)
        dk = dk_ref[0]
        for h in range(1, heads):
            dk = dk + dk_ref[h]
        dz_ref[:, ql + kvl:ql + kvl + LANES] = _rope_bwd(dk, cos_ref[...], sin_ref[...]).astype(BF16)

    return pl.pallas_call(
        body, grid=(t // tr,),
        in_specs=[_rows(tr, w), _full((1, ql)), _full((1, kvl)), _rows(tr, ql), _rows(tr, kvl),
                  pl.BlockSpec((heads, tr, LANES), lambda i: (0, i, 0)), _rows(tr, LANES), _rows(tr, LANES)],
        out_specs=[_rows(tr, w), _full((1, ql)), _full((1, kvl))],
        out_shape=[jax.ShapeDtypeStruct((t, w), BF16), jax.ShapeDtypeStruct((1, ql), F32),
                   jax.ShapeDtypeStruct((1, kvl), F32)],
        compiler_params=_params(("arbitrary",)), name="lat_bwd")(z_lat, qg, kvg, dqn, dkvn, dkpe_heads, cos, sin)


_NT = (((1,), (1,)), ((), ()))


def _attn_scale():
    return (QK_NOPE + QK_ROPE) ** -0.5


def _heads_per_step(heads, wanted):
    return wanted if heads % wanted == 0 else 1


def _attn_fwd(q_c, kv, kpe, comm=None):
    t = q_c.shape[0]
    heads = q_c.shape[1] // HEAD_PAD
    tq = _pick(t, ATTN_TILE)
    nq = t // tq
    scale = _attn_scale()
    to_log2 = scale * math.log2(math.e)
    tn_dims = (((0,), (0,)), ((), ()))

    hps = _heads_per_step(heads, HEADS_PER_STEP[0])

    def body(q_ref, kv_ref, kpe_ref, o_ref, ob_ref, lse_ref, m_sc, l_sc, acc_sc):
        qi, ki = pl.program_id(1), pl.program_id(2)

        @pl.when(ki == 0)
        def _():
            m_sc[...] = jnp.full_like(m_sc, NEG_BIG)
            l_sc[...] = jnp.zeros_like(l_sc)
            acc_sc[...] = jnp.zeros_like(acc_sc)

        def step(diagonal):
            for u in range(hps):
                lo = u * HEAD_PAD
                kc = jnp.concatenate([kv_ref[:, lo:lo + QK_NOPE], kpe_ref[...]], axis=1)
                st = lax.dot_general(kc, q_ref[:, lo:lo + HEAD_PAD], _NT, preferred_element_type=F32)
                if diagonal:
                    krow = lax.broadcasted_iota(jnp.int32, st.shape, 0)
                    qcol = lax.broadcasted_iota(jnp.int32, st.shape, 1)
                    st = jnp.where(qcol >= krow, st, NEG_BIG)
                m_prev = m_sc[u]
                m_new = jnp.maximum(m_prev, jnp.max(st, axis=0, keepdims=True))
                alpha = jnp.exp2((m_prev - m_new) * to_log2)
                pt = jnp.exp2((st - m_new) * to_log2)
                l_sc[u] = alpha * l_sc[u] + jnp.sum(pt, axis=0, keepdims=True)
                acc_sc[u] = alpha * acc_sc[u] + lax.dot_general(
                    kv_ref[:, lo + QK_NOPE:lo + HEAD_PAD], pt.astype(BF16), tn_dims, preferred_element_type=F32)
                m_sc[u] = m_new

        @pl.when(ki < qi)
        def _():
            step(False)

        @pl.when(ki == qi)
        def _():
            step(True)
            for u in range(hps):
                o = (acc_sc[u] / l_sc[u]).T
                o_ref[:, u * V_HEAD:(u + 1) * V_HEAD] = o
                ob_ref[:, u * V_HEAD:(u + 1) * V_HEAD] = o.astype(BF16)
                lse_ref[u] = m_sc[u] * scale + jnp.log(l_sc[u])

    omap = lambda g, qi, ki: (qi, g)
    outs, comm_outs = _call(
        body, grid=(heads // hps, nq, nq),
        in_specs=[pl.BlockSpec((tq, hps * HEAD_PAD), omap),
                  pl.BlockSpec((tq, hps * HEAD_PAD), lambda g, qi, ki: (jnp.minimum(ki, qi), g)),
                  pl.BlockSpec((tq, LANES), lambda g, qi, ki: (jnp.minimum(ki, qi), 0))],
        out_specs=[pl.BlockSpec((tq, hps * V_HEAD), omap), pl.BlockSpec((tq, hps * V_HEAD), omap),
                   pl.BlockSpec((hps, 1, tq), lambda g, qi, ki: (g, 0, qi))],
        out_shape=[jax.ShapeDtypeStruct((t, heads * V_HEAD), F32), jax.ShapeDtypeStruct((t, heads * V_HEAD), BF16),
                   jax.ShapeDtypeStruct((heads, 1, t), F32)],
        scratch_shapes=[pltpu.VMEM((hps, 1, tq), F32), pltpu.VMEM((hps, 1, tq), F32),
                        pltpu.VMEM((hps, V_HEAD, tq), F32)],
        sem=("parallel", "parallel", "arbitrary"), name="attn_fwd", args=(q_c, kv, kpe), comm=comm)
    return outs[0], outs[1], outs[2], comm_outs


def _attn_bwd(q_c, kv, kpe, o, do, lse_row, cos, sin, comm=None):
    t = q_c.shape[0]
    heads = q_c.shape[1] // HEAD_PAD
    tk = _pick(t, ATTN_TILE)
    nk = t // tk
    scale = _attn_scale()
    tn_dims = (((0,), (0,)), ((), ()))

    hps = _heads_per_step(heads, HEADS_PER_STEP[1])

    def body(q_ref, kv_ref, kpe_ref, do_ref, lse_ref, o_ref, cos_ref, sin_ref, dq_ref, dkv_ref, dkpe_ref,
             dk_sc, dv_sc, delta_sc, dq_sc):
        ki, qi = pl.program_id(1), pl.program_id(2)

        @pl.when(jnp.logical_and(ki == 0, qi == 0))
        def _():
            dq_sc[...] = jnp.zeros_like(dq_sc)

        @pl.when(qi == 0)
        def _():
            dk_sc[...] = jnp.zeros_like(dk_sc)
            dv_sc[...] = jnp.zeros_like(dv_sc)

        @pl.when(ki == 0)
        def _():
            for u in range(hps):
                cols = slice(u * V_HEAD, (u + 1) * V_HEAD)
                delta_sc[qi * hps + u] = jnp.sum((do_ref[:, cols] * o_ref[:, cols]).T, axis=0, keepdims=True)

        def step(diagonal):
            for u in range(hps):
                lo = u * HEAD_PAD
                kc = jnp.concatenate([kv_ref[:, lo:lo + QK_NOPE], kpe_ref[...]], axis=1)
                q = q_ref[:, lo:lo + HEAD_PAD]
                st = lax.dot_general(kc, q, _NT, preferred_element_type=F32) * scale
                pt = jnp.exp(st - lse_ref[u])
                if diagonal:
                    krow = lax.broadcasted_iota(jnp.int32, st.shape, 0)
                    qcol = lax.broadcasted_iota(jnp.int32, st.shape, 1)
                    pt = jnp.where(qcol >= krow, pt, 0.0)
                do_b = do_ref[:, u * V_HEAD:(u + 1) * V_HEAD].astype(BF16)
                dv_sc[u] += jnp.dot(pt.astype(BF16), do_b, preferred_element_type=F32)
                dpt = lax.dot_general(kv_ref[:, lo + QK_NOPE:lo + HEAD_PAD], do_b, _NT, preferred_element_type=F32)
                dst = (pt * (dpt - delta_sc[qi * hps + u]) * scale).astype(BF16)
                dk_sc[u] += jnp.dot(dst, q, preferred_element_type=F32)
                rows = pl.ds(pl.multiple_of(qi * tk, tk), tk)
                dq_sc[rows, lo:lo + HEAD_PAD] += lax.dot_general(dst, kc, tn_dims, preferred_element_type=F32)

        @pl.when(qi > ki)
        def _():
            step(False)

        @pl.when(qi == ki)
        def _():
            step(True)

        @pl.when(qi == nk - 1)
        def _():
            for u in range(hps):
                lo = u * HEAD_PAD
                dkv_ref[:, lo:lo + QK_NOPE] = dk_sc[u, :, 0:QK_NOPE].astype(BF16)
                dkv_ref[:, lo + QK_NOPE:lo + HEAD_PAD] = dv_sc[u].astype(BF16)
                dkpe_ref[u] = dk_sc[u, :, QK_NOPE:QK_NOPE + LANES]

        @pl.when(jnp.logical_and(ki == nk - 1, qi == nk - 1))
        def _():
            cos, sin = cos_ref[...], sin_ref[...]
            for u in range(hps):
                lo = u * HEAD_PAD
                dq_ref[:, lo:lo + QK_NOPE] = dq_sc[:, lo:lo + QK_NOPE].astype(BF16)
                dq_ref[:, lo + QK_NOPE:lo + HEAD_PAD] = _rope_bwd(dq_sc[:, lo + QK_NOPE:lo + HEAD_PAD], cos, sin).astype(BF16)

    qclamp = lambda g, ki, qi: (jnp.maximum(qi, ki), g)
    outs, comm_outs = _call(
        body, grid=(heads // hps, nk, nk),
        in_specs=[pl.BlockSpec((tk, hps * HEAD_PAD), qclamp),
                  pl.BlockSpec((tk, hps * HEAD_PAD), lambda g, ki, qi: (ki, g)),
                  pl.BlockSpec((tk, LANES), lambda g, ki, qi: (ki, 0)),
                  pl.BlockSpec((tk, hps * V_HEAD), qclamp),
                  pl.BlockSpec((hps, 1, tk), lambda g, ki, qi: (g, 0, jnp.maximum(qi, ki))),
                  pl.BlockSpec((tk, hps * V_HEAD), lambda g, ki, qi: (jnp.where(ki == 0, qi, 0), g)),
                  _full((t, LANES)), _full((t, LANES))],
        out_specs=[pl.BlockSpec((t, hps * HEAD_PAD), lambda g, ki, qi: (0, g)),
                   pl.BlockSpec((tk, hps * HEAD_PAD), lambda g, ki, qi: (ki, g)),
                   pl.BlockSpec((hps, tk, LANES), lambda g, ki, qi: (g, ki, 0))],
        out_shape=[jax.ShapeDtypeStruct((t, heads * HEAD_PAD), BF16),
                   jax.ShapeDtypeStruct((t, heads * HEAD_PAD), BF16), jax.ShapeDtypeStruct((heads, t, LANES), F32)],
        scratch_shapes=[pltpu.VMEM((hps, tk, HEAD_PAD), F32), pltpu.VMEM((hps, tk, V_HEAD), F32),
                        pltpu.VMEM((nk * hps, 1, tk), F32), pltpu.VMEM((t, hps * HEAD_PAD), F32)],
        sem=("parallel", "arbitrary", "arbitrary"), name="attn_bwd",
        args=(q_c, kv, kpe, do, lse_row, o, cos, sin), comm=comm)
    return outs[0], outs[1], outs[2], comm_outs


def _local_step(x, pos_col, target, small, shards, opt):
    t = x.shape[0]
    ql, kvl = small["q_norm_g"].shape[1], small["kv_norm_g"].shape[1]
    sw = small["sgu_norm_g"].shape[1]
    heads = (shards["w_uq"].shape[1] * N_DEV) // (QK_NOPE + QK_ROPE)
    big = {}
    big.update(_compute_layout({"w_in": _all_gather([shards["w_in"]])[0]}, ql, kvl, heads, sw))
    half = QK_ROPE // 2
    lane = jnp.arange(LANES)
    inv_freq = ROPE_THETA ** (-jnp.arange(0, QK_ROPE, 2, dtype=F32) / QK_ROPE)
    inv_row = inv_freq[lane % half][None, :]
    sign_row = jnp.where((lane % QK_ROPE) < half, -1.0, 1.0).astype(F32)[None, :]
    cos, sin = _rope_tables(pos_col, inv_row, sign_row)
    ws = small["w_sgu"]
    b_col = small["b_sgu_col"]

    def arrived(names, bufs):
        big.update(_compute_layout(dict(zip(names, bufs)), ql, kvl, heads, sw))

    a = _norm_fwd(x, small["norm_mix_g"], "norm_mix_fwd")
    z_lat, g_qk = _mm(a, big["w_lat_t"], tb=True, name="z_lat",
                      comm=_gather_stage(1, [shards["w_uq"], shards["w_ukv"]]))
    z_uv, (g_sgu, *g_qk) = _mm(a, big["w_uv_t"], tb=True, name="z_uv",
                               comm=_join(_gather_stage(1, [shards["w_o_sgu"]]), _gather_stage(2, g_qk)))
    z_g, (g_attn, g_sgu, *g_qk) = _mm(
        a, big["w_g_t"], tb=True, name="z_g",
        comm=_join(_gather_stage(1, [shards["w_o_attn"]]), _gather_stage(2, [g_sgu]), _gather_stage(3, g_qk)))
    arrived(["w_uq", "w_ukv"], g_qk)
    qn, kvn, kpe = _lat_fwd(z_lat, small["q_norm_g"], small["kv_norm_g"], cos, sin, ql, kvl)
    q_c, (g_attn, g_sgu) = _mm(qn, big["w_uq"], name="q_up_rope", rope=(cos, sin),
                               comm=_join(_gather_stage(2, [g_attn]), _gather_stage(3, [g_sgu])))
    kv, (g_attn, g_out) = _mm(kvn, big["w_ukv"], out_dtype=BF16, name="kv_up",
                              comm=_join(_gather_stage(3, [g_attn]), _gather_stage(1, [shards["w_out"]])))
    arrived(["w_o_sgu", "w_o_attn"], [g_sgu, g_attn])
    attn, attn_b, lse, (w_gate, w_up) = _attn_fwd(
        q_c, kv, kpe, comm=_gather_stage(1, [shards["w_gate_ffn"], shards["w_up_ffn"]]))
    s_out = _sgu_fwd(z_uv, small["sgu_norm_g"], ws, b_col)
    y_sgu, (g_out,) = _mm(s_out, big["w_o_sgu"], name="y_sgu", comm=_gather_stage(2, [g_out]))
    y_attn, (w_gate, g_out) = _mm(attn_b, big["w_o_attn"], name="y_attn",
                                  comm=_join(_gather_stage(2, [w_gate]), _gather_stage(3, [g_out])))
    arrived(["w_out"], [g_out])
    merged, (w_up, w_gate) = _merge_fwd(y_attn, y_sgu, z_g, small["b_gate"],
                                        comm=_join(_gather_stage(2, [w_up]), _gather_stage(3, [w_gate])))
    h1, (w_up,) = _mm(merged, big["w_out"], add=x, name="h1", comm=_gather_stage(3, [w_up]))
    f = _norm_fwd(h1, small["norm_ffn_g"], "norm_ffn_fwd")
    gate, w_down = _mm(f, w_gate, tb=True, slab="n", name="ffn_gate", comm=_gather_stage(1, [shards["w_down_ffn"]]))
    up, w_down = _mm(f, w_up, tb=True, slab="n", name="ffn_up", comm=_gather_stage(2, w_down))
    ffn = gate.shape[2]
    gate, up = gate.reshape(N_DEV * t, ffn), up.reshape(N_DEV * t, ffn)
    act, (w_down,) = _swiglu_fwd(gate, up, comm=_gather_stage(3, w_down))
    act = act.reshape(N_DEV, t, ffn)
    h2 = _mm(act, w_down, slab="k", add=h1, name="h2")
    loss_row, dh2, dh2_b, d_norm_final = _loss_head(h2, small["norm_final_g"], target)

    def pair_sums(names, slabs, bufs):
        return [_pair_sum(g, b, "pair_sum_" + k) for k, g, b in zip(names, slabs, bufs)]

    parts, updates = {}, {}

    def update(names, label, comm=None):
        res, got = _adamw_shards([parts[k] for k in names], [opt[k] for k in names], "adamw_" + label, comm=comm)
        updates.update(zip(names, res))
        return got

    down_slabs = [_mm(act, dh2_b, ta=True, slab="m", out_dtype=BF16, name="dw_down")]
    dgu, bufs = _mm(dh2_b, w_down, tb=True, slab="n", tm=MM_TILE[0] // 2, name="dact_swiglu_bwd",
                    comm=_to_sibling(down_slabs), swiglu=(gate.reshape(N_DEV, t, ffn), up.reshape(N_DEV, t, ffn)))
    dgu = dgu.reshape(2 * N_DEV, t, ffn)
    down_pair = pair_sums(["w_down_ffn"], down_slabs, bufs)
    dw_gu, got = _mm(dgu, f, ta=True, slab="m", out_dtype=BF16, name="dw_gate_up", comm=_to_chips(down_pair))
    parts["w_down_ffn"] = got[0]
    gu_names = ["w_gate_ffn", "w_up_ffn"]
    df, bufs = _mm(dgu, w_gate, slab="k", name="df_gate", comm=_to_sibling([dw_gu, dw_gu], first=[0, N_DEV]))
    gu_pairs = [_pair_sum(dw_gu, b, "pair_sum_" + k, first=s0) for k, b, s0 in zip(gu_names, bufs, [0, N_DEV])]
    half = _pick(gu_pairs[1].shape[1], gu_pairs[1].shape[1] // 2, 2 * SUBLANES)
    df, up_parts = _mm(dgu, w_up, slab="k", a_slab0=N_DEV, add=df, name="df_up",
                       comm=_to_chips(gu_pairs[1:], rows=[("r", 0, half)]))
    quarter = _pick(half, half // 2, 2 * SUBLANES)
    dh1, dh1_b, d_norm_ffn, gate_parts = _norm_bwd(h1, small["norm_ffn_g"], df, dh2, "norm_ffn_bwd",
                                                  comm=_to_chips(gu_pairs[:1], rows=[("r", 0, quarter)]))
    dw_out = _mm(merged, dh1_b, ta=True, out_dtype=BF16, name="dw_out")
    out_slabs = [_slabs_from_rows(dw_out)]
    dmerged, bufs = _mm(dh1_b, big["w_out"], tb=True, name="dmerged", comm=_to_sibling(out_slabs))
    out_pair = pair_sums(["w_out"], out_slabs, bufs)
    dy_attn, dy_sgu, dz_g, d_b_gate = _merge_bwd(dmerged, y_attn, y_sgu, z_g, small["b_gate"])
    dw_o_sgu = _mm(s_out, dy_sgu, ta=True, out_dtype=BF16, name="dw_o_sgu")
    ds_out = _mm(dy_sgu, big["w_o_sgu"], tb=True, name="ds_out")
    dz_uv, d_ws, d_b_col, d_sgu_norm = _sgu_bwd(z_uv, ds_out, small["sgu_norm_g"], ws, b_col)
    dw_o_attn = _mm(attn_b, dy_attn, ta=True, out_dtype=BF16, name="dw_o_attn")
    mix_names = ["w_o_sgu", "w_o_attn"]
    mix_slabs = [_slabs_from_cols(dw_o_sgu), _slabs_from_rows(dw_o_attn)]
    dattn, bufs = _mm(dy_attn, big["w_o_attn"], tb=True, name="dattn", comm=_to_sibling(mix_slabs))
    mix_pairs = pair_sums(mix_names, mix_slabs, bufs)
    rows = gu_pairs[1].shape[1]
    dq_p, dkv, dkpe_heads, got = _attn_bwd(
        q_c, kv, kpe, attn, dattn, lse, cos, sin,
        comm=_join(_to_chips(gu_pairs[:1], rows=[("r", quarter, rows - quarter)], into=gate_parts),
                   _to_chips(gu_pairs[1:], rows=[("r", half, rows - half)], into=up_parts)))
    parts.update(zip(gu_names, got))
    dw_uq = _mm(qn, dq_p, ta=True, out_dtype=BF16, name="dw_uq")
    dw_ukv = _mm(kvn, dkv, ta=True, out_dtype=BF16, name="dw_ukv")
    dqn = _mm(dq_p, big["w_uq"], tb=True, name="dqn")
    dkvn = _mm(dkv, big["w_ukv"], tb=True, name="dkvn")
    dz_lat, d_q_norm, d_kv_norm = _lat_bwd(z_lat, small["q_norm_g"], small["kv_norm_g"], dqn, dkvn, dkpe_heads,
                                           cos, sin, ql, kvl)
    dw_g, got = _mm(dz_g, a, ta=True, out_dtype=BF16, name="dw_g", comm=_to_chips(out_pair))
    parts["w_out"] = got[0]
    dw_uv, got = _mm(dz_uv, a, ta=True, out_dtype=BF16, name="dw_uv", comm=_to_chips(mix_pairs[1:]))
    parts["w_o_attn"] = got[0]
    dw_lat, got = _mm(dz_lat, a, ta=True, out_dtype=BF16, name="dw_lat", comm=_to_chips(mix_pairs[:1]))
    parts["w_o_sgu"] = got[0]
    lat = ql + kvl + QK_ROPE
    dw_uq_cols = dw_uq.reshape(ql, heads, HEAD_PAD)[:, :, :QK_NOPE + QK_ROPE].reshape(ql, heads * (QK_NOPE + QK_ROPE))
    in_names = ["w_uq", "w_ukv", "w_in"]
    in_slabs = [_slabs_from_cols(dw_uq_cols), _slabs_from_cols(dw_ukv),
                _slabs_from_rows(jnp.concatenate([dw_lat[:lat], dw_uv, dw_g], axis=0))]
    da = _mm(dz_lat, big["w_lat_t"], name="da_lat")
    da, bufs = _mm(dz_uv, big["w_uv_t"], add=da, name="da_uv", comm=_to_sibling(in_slabs))
    uq_pair, ukv_pair, in_pair = pair_sums(in_names, in_slabs, bufs)
    cols = in_pair.shape[2]
    first = ((cols * TAIL_SPLIT[0]) // TAIL_SPLIT[1]) // LANES * LANES or cols
    da, in_parts = _mm(dz_g, big["w_g_t"], add=da, name="da_g", comm=_to_chips([in_pair], rows=[("c", 0, first)]))
    grad_x, _, d_norm_mix, got = _norm_bwd(x, small["norm_mix_g"], da, dh1, "norm_mix_bwd",
                                          comm=_to_chips([uq_pair, ukv_pair]))
    parts["w_uq"], parts["w_ukv"] = got
    rest = _to_chips([in_pair], rows=[("c", first, cols - first)], into=in_parts) if first < cols else None
    got = update(["w_gate_ffn", "w_up_ffn", "w_down_ffn"], "ffn", comm=rest)
    parts["w_in"] = got[0] if rest is not None else in_parts[0]
    update(["w_out", "w_o_attn"], "mixer_out")
    for k in ("w_o_sgu", "w_uq", "w_ukv", "w_in"):
        update([k], k)

    gs = {"norm_mix_g": d_norm_mix, "b_gate": d_b_gate, "q_norm_g": d_q_norm, "kv_norm_g": d_kv_norm,
          "sgu_norm_g": d_sgu_norm, "w_sgu": d_ws, "b_sgu_col": d_b_col, "norm_ffn_g": d_norm_ffn,
          "norm_final_g": d_norm_final}
    return loss_row, grad_x, gs, updates


def _my_place():
    return lax.axis_index("x"), lax.axis_index("y"), lax.axis_index("c")


N_CHIPS = N_DEV // 2

_GATHER_SEMS = [[(3,), (3,), ()], [(4,), (4,)], [(1,), (1,)]]


def _halves(shape):
    r, c = shape
    if (c // 2) % LANES == 0:
        return ("c", 0, c // 2), ("c", c // 2, c // 2)
    assert (r // 2) % (2 * SUBLANES) == 0, shape
    return ("r", 0, r // 2), ("r", r // 2, r // 2)


def _gather_copies(stage, ins, outs, sems, strips=None):
    x, y, c = _my_place()
    me, x_nbr, y_nbr, diag = 4 * x + 2 * y + c, 4 * (1 - x) + 2 * y + c, 4 * x + 2 * (1 - y) + c, 4 * (1 - x) + 2 * (1 - y) + c
    sibling = (x, y, 1 - c)
    if strips is None:
        strips = [(w, None) for w in range(len(outs))]

    out = []
    for s, (w, cols) in enumerate(strips):
        def remote(k, src, dst, to):
            return pltpu.make_async_remote_copy(src_ref=src, dst_ref=dst, send_sem=sems[0].at[s, k],
                                                recv_sem=sems[1].at[s, k], device_id=to, device_id_type=MESH)

        if cols is None:
            whole, (first, second) = None, _halves(outs[w].shape[1:])
        else:
            c0, nc = cols
            whole, first, second = ("c", c0, nc), ("c", c0, nc // 2), ("c", c0 + nc // 2, nc // 2)
        if stage == 1:
            src = ins[w] if cols is None else ins[w].at[slice(None), pl.ds(*cols)]
            dst = _window(outs[w], me, whole)
            out.append(pltpu.make_async_copy(src, dst, sems[2].at[s]))
            out += [remote(k, src, dst, to) for k, to in enumerate([sibling, (1 - x, y, c), (x, 1 - y, c)])]
        elif stage == 2:
            out.append(remote(0, _window(ins[w], x_nbr, first), _window(outs[w], x_nbr, first), (x, 1 - y, c)))
            out.append(remote(1, _window(ins[w], y_nbr, second), _window(outs[w], y_nbr, second), (1 - x, y, c)))
            out.append(remote(2, _window(ins[w], x_nbr, whole), _window(outs[w], x_nbr, whole), sibling))
            out.append(remote(3, _window(ins[w], y_nbr, whole), _window(outs[w], y_nbr, whole), sibling))
        else:
            out.append(remote(0, _window(ins[w], diag, whole), _window(outs[w], diag, whole), sibling))
    return out


def _gather_stage(stage, arrays):
    n = len(arrays)

    def start(ins, outs, sems):
        for cp in _gather_copies(stage, ins, outs, sems):
            cp.start()

    def finish(ins, outs, sems):
        for cp in _gather_copies(stage, ins, outs, sems):
            cp.wait()

    shapes = [jax.ShapeDtypeStruct(((N_DEV,) + a.shape) if stage == 1 else a.shape, a.dtype) for a in arrays]
    return _Comm(arrays, shapes, [pltpu.SemaphoreType.DMA((n,) + s) for s in _GATHER_SEMS[stage - 1]], start, finish,
                 aliases=None if stage == 1 else {w: w for w in range(n)})


def _join(*comms):
    ins, shapes, sems, aliases, spans = [], [], [], {}, []
    for cm in comms:
        spans.append((len(ins), len(ins) + len(cm.ins), len(shapes), len(shapes) + len(cm.out_shapes),
                      len(sems), len(sems) + len(cm.sems)))
        aliases.update({len(ins) + i: len(shapes) + o for i, o in cm.aliases.items()})
        ins, shapes, sems = ins + cm.ins, shapes + cm.out_shapes, sems + cm.sems

    def each(half):
        def run(i_refs, o_refs, s_refs):
            for cm, (i0, i1, o0, o1, s0, s1) in zip(comms, spans):
                getattr(cm, half)(i_refs[i0:i1], o_refs[o0:o1], s_refs[s0:s1])
        return run

    return _Comm(ins, shapes, sems, each("start"), each("finish"), aliases)


def _all_gather(shards):
    n = len(shards)
    strips = []
    for w, sh in enumerate(shards):
        cols = sh.shape[1]
        nc = cols // GATHER_STRIPS if cols % (GATHER_STRIPS * 2 * LANES) == 0 else cols
        strips += [(w, (c0, nc)) for c0 in range(0, cols, nc)]
    ns = len(strips)
    n_sems = [len(s) for s in _GATHER_SEMS]

    def body(*refs):
        ins, outs, sems = refs[:n], refs[n:2 * n], refs[2 * n:]
        sem1, sem2, sem3 = (sems[sum(n_sems[:i]):sum(n_sems[:i + 1])] for i in range(3))
        c1 = _gather_copies(1, ins, outs, sem1, strips)
        c2 = _gather_copies(2, outs, outs, sem2, strips)
        c3 = _gather_copies(3, outs, outs, sem3, strips)
        for cp in c1:
            cp.start()
        for s in range(ns):
            for cp in c1[4 * s:4 * s + 4]:
                cp.wait()
            for cp in c2[4 * s:4 * s + 4]:
                cp.start()
        for s in range(ns):
            for cp in c2[4 * s:4 * s + 4]:
                cp.wait()
            c3[s].start()
        for cp in c3:
            cp.wait()

    any_spec = pl.BlockSpec(memory_space=pl.ANY)
    return pl.pallas_call(
        body, in_specs=[any_spec] * n, out_specs=[any_spec] * n,
        out_shape=[jax.ShapeDtypeStruct((N_DEV,) + s.shape, s.dtype) for s in shards],
        scratch_shapes=[pltpu.SemaphoreType.DMA((ns,) + s) for stage in _GATHER_SEMS for s in stage],
        compiler_params=pltpu.CompilerParams(has_side_effects=True), name="all_gather_weights")(*shards)


def _to_sibling(grads, first=None):
    n = len(grads)
    first = first or [0] * n

    def copies(ins, outs, sems):
        x, y, c = _my_place()
        send_sems, recv_sems = sems
        return [pltpu.make_async_remote_copy(
            src_ref=ins[w].at[first[w] + 2 * i + (1 - c)], dst_ref=outs[w].at[i], send_sem=send_sems.at[w, i],
            recv_sem=recv_sems.at[w, i], device_id=(x, y, 1 - c), device_id_type=MESH)
            for w in range(n) for i in range(N_CHIPS)]

    def start(ins, outs, sems):
        for cp in copies(ins, outs, sems):
            cp.start()

    def finish(ins, outs, sems):
        for cp in copies(ins, outs, sems):
            cp.wait()

    return _Comm(grads, [jax.ShapeDtypeStruct((N_CHIPS,) + g.shape[1:], g.dtype) for g in grads],
                 [pltpu.SemaphoreType.DMA((n, N_CHIPS)), pltpu.SemaphoreType.DMA((n, N_CHIPS))], start, finish)


def _window(ref, slab, win):
    if win is None:
        return ref.at[slab]
    if win[0] == "r":
        return ref.at[slab, pl.ds(win[1], win[2])]
    return ref.at[slab, slice(None), pl.ds(win[1], win[2])]


def _to_chips(parts, rows=None, into=None):
    n = len(parts)
    rows = rows or [None] * n

    def copies(ins, outs, sems):
        x, y, c = _my_place()
        send_sems, recv_sems, local_sems = sems
        mine = 2 * x + y
        chips = [(1 - x, y), (x, 1 - y), (1 - x, 1 - y)]
        remote = [pltpu.make_async_remote_copy(
            src_ref=_window(ins[w], 2 * cx + cy, rows[w]), dst_ref=_window(outs[w], mine, rows[w]),
            send_sem=send_sems.at[w, j], recv_sem=recv_sems.at[w, j], device_id=(cx, cy, c), device_id_type=MESH)
            for w in range(n) for j, (cx, cy) in enumerate(chips)]
        local = [pltpu.make_async_copy(_window(ins[w], mine, rows[w]), _window(outs[w], mine, rows[w]),
                                       local_sems.at[w]) for w in range(n)]
        return remote + local

    def start(ins, outs, sems):
        for cp in copies(ins, outs, sems):
            cp.start()

    def finish(ins, outs, sems):
        for cp in copies(ins, outs, sems):
            cp.wait()

    return _Comm(list(parts) + list(into or []), [jax.ShapeDtypeStruct(p.shape, p.dtype) for p in parts],
                 [pltpu.SemaphoreType.DMA((n, N_CHIPS - 1)), pltpu.SemaphoreType.DMA((n, N_CHIPS - 1)),
                  pltpu.SemaphoreType.DMA((n,))], start, finish,
                 aliases={n + w: w for w in range(n)} if into else None)


def _pair_sum(g, buf, name, first=0):
    _, r, c = g.shape
    tr, tc = _shard_tile(r, c, 4 * SHARD_TILE_ELEMS, 1024)
    core = (lax.axis_index("c") + first).astype(jnp.int32).reshape(1)

    def body(core_ref, g_ref, b_ref, o_ref):
        o_ref[...] = (g_ref[...].astype(F32) + b_ref[...].astype(F32)).astype(o_ref.dtype)

    blk = (1, tr, tc)
    return pl.pallas_call(
        body, grid_spec=pltpu.PrefetchScalarGridSpec(
            num_scalar_prefetch=1, grid=(N_CHIPS, r // tr, c // tc),
            in_specs=[pl.BlockSpec(blk, lambda i, j, l, core_ref: (2 * i + core_ref[0], j, l)),
                      pl.BlockSpec(blk, lambda i, j, l, core_ref: (i, j, l))],
            out_specs=pl.BlockSpec(blk, lambda i, j, l, core_ref: (i, j, l))),
        out_shape=jax.ShapeDtypeStruct(buf.shape, buf.dtype),
        compiler_params=_params(("parallel", "parallel", "parallel")), name=name)(core, g, buf)


def _all_reduce_pack(pack):
    r = pack.shape[0]

    def body(x_ref, out_ref, gath_ref, send_sems, recv_sems, local_sem):
        x, y, c = _my_place()
        me, sibling = (x, y, c), (x, y, 1 - c)
        chips = [(1 - x, y), (x, 1 - y), (1 - x, 1 - y)]

        def slab(place):
            return gath_ref.at[4 * place[0] + 2 * place[1] + place[2]]

        def copy(k, place, to, src=None):
            return pltpu.make_async_remote_copy(
                src_ref=slab(place) if src is None else src, dst_ref=slab(place),
                send_sem=send_sems.at[k], recv_sem=recv_sems.at[k], device_id=to, device_id_type=MESH)

        mine = pltpu.make_async_copy(x_ref, slab(me), local_sem)
        mine.start()
        first = [copy(0, me, sibling, src=x_ref)]
        first += [copy(1 + j, me, (*chip, c), src=x_ref) for j, chip in enumerate(chips)]
        for cp in first:
            cp.start()
        passed = [copy(4 + j, (*chip, c), sibling) for j, chip in enumerate(chips)]
        for j, chip in enumerate(chips):
            copy(1 + j, (*chip, c), me).wait_recv()
            passed[j].start()
        copy(0, sibling, me).wait_recv()
        for j, chip in enumerate(chips):
            copy(4 + j, (*chip, 1 - c), me).wait_recv()
        for cp in first + passed:
            cp.wait_send()
        mine.wait()
        acc = gath_ref[0]
        for i in range(1, N_DEV):
            acc = acc + gath_ref[i]
        out_ref[...] = acc

    vmem = pl.BlockSpec(memory_space=pltpu.VMEM)
    return pl.pallas_call(
        body, in_specs=[vmem], out_specs=vmem, out_shape=jax.ShapeDtypeStruct(pack.shape, F32),
        scratch_shapes=[pltpu.VMEM((N_DEV, r, LANES), F32), pltpu.SemaphoreType.DMA((7,)),
                        pltpu.SemaphoreType.DMA((7,)), pltpu.SemaphoreType.DMA],
        compiler_params=pltpu.CompilerParams(vmem_limit_bytes=VMEM_LIMIT), name="all_reduce_small")(pack)


def _adamw_math(w, g, m, v):
    m = ADAM_B1 * m + (1.0 - ADAM_B1) * g
    v = ADAM_B2 * v + (1.0 - ADAM_B2) * (g * g)
    m_hat = m / (1.0 - ADAM_B1 ** ADAM_STEP)
    v_hat = v / (1.0 - ADAM_B2 ** ADAM_STEP)
    delta = -ADAM_LR * (m_hat / (jnp.sqrt(v_hat) + ADAM_EPS) + ADAM_WD * w)
    return delta, m, v


def _adamw_shards(parts, opts, name, comm=None):
    r, c = opts[0][0].shape
    n_parts, k = parts[0].shape[0], len(parts)
    tr, tc = _shard_tile(r, c, SHARD_TILE_ELEMS // k)

    def body(*refs):
        ins, outs = refs[:4 * k], refs[4 * k:]
        for s in range(k):
            p_ref, w_ref, m_ref, v_ref = ins[4 * s:4 * s + 4]
            g_ref, d_ref, nm_ref, nv_ref = outs[4 * s:4 * s + 4]
            g = p_ref[0].astype(F32)
            for i in range(1, n_parts):
                g = g + p_ref[i].astype(F32)
            g_ref[...] = g
            d_ref[...], nm_ref[...], nv_ref[...] = _adamw_math(w_ref[...], g, m_ref[...], v_ref[...])

    spec = pl.BlockSpec((tr, tc), lambda i, j: (i, j))
    args = [a for p, o in zip(parts, opts) for a in (p,) + tuple(o)]
    outs, comm_outs = _call(
        body, grid=(r // tr, c // tc),
        in_specs=[pl.BlockSpec((n_parts, tr, tc), lambda i, j: (0, i, j)), spec, spec, spec] * k,
        out_specs=[spec] * (4 * k), out_shape=[jax.ShapeDtypeStruct((r, c), F32)] * (4 * k),
        sem=("parallel", "parallel"), name=name, args=args, comm=comm)
    return [outs[4 * s:4 * s + 4] for s in range(k)], comm_outs


def _adamw_pack(g, w, m, v):
    r, c = w.shape

    def body(g_ref, w_ref, m_ref, v_ref, d_ref, nm_ref, nv_ref):
        d_ref[...], nm_ref[...], nv_ref[...] = _adamw_math(w_ref[...], g_ref[...], m_ref[...], v_ref[...])

    return pl.pallas_call(
        body, in_specs=[_full((r, c))] * 4, out_specs=[_full((r, c))] * 3, grid=(1,),
        out_shape=[jax.ShapeDtypeStruct((r, c), F32)] * 3,
        compiler_params=_params(("arbitrary",)), name="adamw_small")(g, w, m, v)


def _cols_from_slabs(g):
    return jnp.transpose(g, (1, 0, 2)).reshape(g.shape[1], N_DEV * g.shape[2])


def _slabs_from_cols(w):
    r, c8 = w.shape
    return jnp.transpose(w.reshape(r, N_DEV, c8 // N_DEV), (1, 0, 2))


def _rows_from_slabs(g):
    return g.reshape(N_DEV * g.shape[1], g.shape[2])


def _slabs_from_rows(w):
    return w.reshape(N_DEV, w.shape[0] // N_DEV, w.shape[1])


def _compute_layout(gathered, ql, kvl, heads, sw):
    out = {}
    for k, g in gathered.items():
        if k == "w_in":
            lat = ql + kvl + QK_ROPE
            w_in_t = _rows_from_slabs(g)
            out["w_lat_t"] = jnp.pad(w_in_t[:lat], ((0, LANES - QK_ROPE), (0, 0)))
            out["w_uv_t"] = w_in_t[lat:lat + 2 * sw]
            out["w_g_t"] = w_in_t[lat + 2 * sw:]
        elif k == "w_uq":
            per_head = _cols_from_slabs(g).reshape(ql, heads, QK_NOPE + QK_ROPE)
            pad = HEAD_PAD - QK_NOPE - QK_ROPE
            out["w_uq"] = jnp.pad(per_head, ((0, 0), (0, 0), (0, pad))).reshape(ql, heads * HEAD_PAD)
        elif k in ("w_o_attn", "w_out", "w_down_ffn"):
            out[k.removesuffix("_ffn")] = _rows_from_slabs(g)
        else:
            out[k.removesuffix("_ffn")] = _cols_from_slabs(g)
    return out


_SMALL =["norm_mix_g", "b_gate", "q_norm_g", "kv_norm_g", "sgu_norm_g", "w_sgu", "b_sgu", "norm_ffn_g", "norm_final_g"]
_BIG = ["w_in", "w_uq", "w_ukv", "w_o_attn", "w_o_sgu", "w_out", "w_gate_ffn", "w_up_ffn", "w_down_ffn"]
_TRANSPOSED = ("w_in", "w_gate_ffn", "w_up_ffn")
_ORDER = ["norm_mix_g", "w_in", "b_gate", "q_norm_g", "w_uq", "kv_norm_g", "w_ukv", "w_o_attn", "sgu_norm_g", "w_sgu",
          "b_sgu", "w_o_sgu", "w_out", "norm_ffn_g", "w_gate_ffn", "w_up_ffn", "w_down_ffn", "norm_final_g"]


def _pack_rows(parts):
    rows, sizes = [], []
    for p in parts:
        flat = p.reshape(-1)
        n = flat.shape[0]
        padded = -(-n // (SUBLANES * LANES)) * (SUBLANES * LANES)
        rows.append(jnp.pad(flat, (0, padded - n)).reshape(padded // LANES, LANES))
        sizes.append((n, padded // LANES))
    return jnp.concatenate(rows, axis=0), sizes


def _unpack_rows(pack, sizes, shapes):
    out, r0 = [], 0
    for (n, nr), shp in zip(sizes, shapes):
        out.append(pack[r0:r0 + nr].reshape(-1)[:n].reshape(shp))
        r0 += nr
    return out


def kernel(x, positions, norm_mix_g, w_in, b_gate, q_norm_g, w_uq, kv_norm_g, w_ukv, w_o_attn, sgu_norm_g, w_sgu, b_sgu, w_o_sgu, w_out, norm_ffn_g, w_gate_ffn, w_up_ffn, w_down_ffn, norm_final_g, loss_target, m_norm_mix_g, m_w_in, m_b_gate, m_q_norm_g, m_w_uq, m_kv_norm_g, m_w_ukv, m_w_o_attn, m_sgu_norm_g, m_w_sgu, m_b_sgu, m_w_o_sgu, m_w_out, m_norm_ffn_g, m_w_gate_ffn, m_w_up_ffn, m_w_down_ffn, m_norm_final_g, v_norm_mix_g, v_w_in, v_b_gate, v_q_norm_g, v_w_uq, v_kv_norm_g, v_w_ukv, v_w_o_attn, v_sgu_norm_g, v_w_sgu, v_b_sgu, v_w_o_sgu, v_w_out, v_norm_ffn_g, v_w_gate_ffn, v_w_up_ffn, v_w_down_ffn, v_norm_final_g):
    wts = dict(norm_mix_g=norm_mix_g, w_in=w_in, b_gate=b_gate, q_norm_g=q_norm_g, w_uq=w_uq, kv_norm_g=kv_norm_g,
               w_ukv=w_ukv, w_o_attn=w_o_attn, sgu_norm_g=sgu_norm_g, w_sgu=w_sgu, b_sgu=b_sgu, w_o_sgu=w_o_sgu,
               w_out=w_out, norm_ffn_g=norm_ffn_g, w_gate_ffn=w_gate_ffn, w_up_ffn=w_up_ffn, w_down_ffn=w_down_ffn,
               norm_final_g=norm_final_g)
    mom = dict(norm_mix_g=m_norm_mix_g, w_in=m_w_in, b_gate=m_b_gate, q_norm_g=m_q_norm_g, w_uq=m_w_uq,
               kv_norm_g=m_kv_norm_g, w_ukv=m_w_ukv, w_o_attn=m_w_o_attn, sgu_norm_g=m_sgu_norm_g, w_sgu=m_w_sgu,
               b_sgu=m_b_sgu, w_o_sgu=m_w_o_sgu, w_out=m_w_out, norm_ffn_g=m_norm_ffn_g, w_gate_ffn=m_w_gate_ffn,
               w_up_ffn=m_w_up_ffn, w_down_ffn=m_w_down_ffn, norm_final_g=m_norm_final_g)
    var = dict(norm_mix_g=v_norm_mix_g, w_in=v_w_in, b_gate=v_b_gate, q_norm_g=v_q_norm_g, w_uq=v_w_uq,
               kv_norm_g=v_kv_norm_g, w_ukv=v_w_ukv, w_o_attn=v_w_o_attn, sgu_norm_g=v_sgu_norm_g, w_sgu=v_w_sgu,
               b_sgu=v_b_sgu, w_o_sgu=v_w_o_sgu, w_out=v_w_out, norm_ffn_g=v_norm_ffn_g, w_gate_ffn=v_w_gate_ffn,
               w_up_ffn=v_w_up_ffn, w_down_ffn=v_w_down_ffn, norm_final_g=v_norm_final_g)

    t, d = x.shape[1], x.shape[2]
    ql, kvl = q_norm_g.shape[1], kv_norm_g.shape[1]
    heads = (w_uq.shape[2] * N_DEV) // (QK_NOPE + QK_ROPE)
    sw = sgu_norm_g.shape[1]

    def shard(a, k):
        return a[0].T if k in _TRANSPOSED else a[0]

    def unshard(a, k):
        return (a.T if k in _TRANSPOSED else a).reshape(wts[k].shape)

    opt = {k: (shard(wts[k], k), shard(mom[k], k), shard(var[k], k)) for k in _BIG}
    shards = {k: opt[k][0].astype(BF16) for k in _BIG}
    small = {
        "norm_mix_g": norm_mix_g, "b_gate": b_gate, "q_norm_g": q_norm_g, "kv_norm_g": kv_norm_g,
        "sgu_norm_g": sgu_norm_g, "w_sgu": w_sgu[0], "b_sgu_col": b_sgu[0][:, :, None], "norm_ffn_g": norm_ffn_g,
        "norm_final_g": norm_final_g[None, :],
    }

    loss_row, grad_x, gs, updates = _local_step(x[0], positions.reshape(t, 1), loss_target[0], small, shards, opt)
    grads, deltas, new_m, new_v = {}, {}, {}, {}
    for k in _BIG:
        grads[k], deltas[k], new_m[k], new_v[k] = (unshard(a, k) for a in updates[k])

    small_grads = [gs["norm_mix_g"], gs["b_gate"], gs["q_norm_g"], gs["kv_norm_g"], gs["sgu_norm_g"], gs["w_sgu"],
                   gs["b_sgu_col"], gs["norm_ffn_g"], gs["norm_final_g"]]
    pack, sizes = _pack_rows([loss_row] + small_grads)
    total = _all_reduce_pack(pack)
    shapes = [(1, LANES)] + [wts[k].shape for k in _SMALL]
    unpacked = _unpack_rows(total, sizes, shapes)
    loss = unpacked[0][0, 0]
    for k, g in zip(_SMALL, unpacked[1:]):
        grads[k] = g
    g_pack = total[sizes[0][1]:]
    w_pack, _ = _pack_rows([wts[k] for k in _SMALL])
    m_pack, _ = _pack_rows([mom[k] for k in _SMALL])
    v_pack, _ = _pack_rows([var[k] for k in _SMALL])
    d_pack, nm_pack, nv_pack = _adamw_pack(g_pack, w_pack, m_pack, v_pack)
    small_shapes = [wts[k].shape for k in _SMALL]
    for store, pk in ((deltas, d_pack), (new_m, nm_pack), (new_v, nv_pack)):
        for k, a in zip(_SMALL, _unpack_rows(pk, sizes[1:], small_shapes)):
            store[k] = a

    return (loss, grad_x[None], *[grads[k] for k in _ORDER], *[deltas[k] for k in _ORDER],
            *[new_m[k] for k in _ORDER], *[new_v[k] for k in _ORDER])
```
